```python
import math
import jax, jax.numpy as jnp
from jax import lax
import numpy as np

D_MODEL = 1024
BATCH = 8
SEQ = 2048
DEPTH = 1

MIX_WIDTH = D_MODEL
SSM_WIDTH = MIX_WIDTH // 2
SSM_GROUP = 16
SSM_GROUPS = SSM_WIDTH // SSM_GROUP
SSM_STATE = 64
ATTN_WIDTH = MIX_WIDTH - SSM_WIDTH
HEAD_DIM = 64
N_Q_HEADS = ATTN_WIDTH // HEAD_DIM
N_KV_HEADS = 2
Q_PER_KV = N_Q_HEADS // N_KV_HEADS
KV_WIDTH = N_KV_HEADS * HEAD_DIM
IN_WIDTH = SSM_WIDTH + ATTN_WIDTH + 2 * KV_WIDTH
WINDOW = 128
BLOCK = 128
ROPE_DIM = HEAD_DIM // 4
ROPE_THETA = 500000.0
D_FF = -(-8 * D_MODEL // (3 * 256)) * 256
NORM_EPS = 1e-6
DT_MIN = 1e-3
DT_MAX = 1e-1
MASK_VALUE = -1e30

kernel_name = "hymba_s5_swa_sink_hybrid"


def rms_norm(x, g):
    xf = x.astype(jnp.float32)
    y = xf * lax.rsqrt(jnp.mean(xf * xf, axis=-1, keepdims=True) + NORM_EPS)
    return (y * g.astype(jnp.float32)).astype(x.dtype)


def _s5_combine(e1, e2):
    a1r, a1i, b1r, b1i = e1
    a2r, a2i, b2r, b2i = e2
    ar = a2r * a1r - a2i * a1i
    ai = a2r * a1i + a2i * a1r
    br = a2r * b1r - a2i * b1i + b2r
    bi = a2r * b1i + a2i * b1r + b2i
    return (ar, ai, br, bi)


def s5_mixer(u, lam_re, lam_im, log_dt, b_re, b_im, c_re, c_im, d, w_glu, b_glu):
    bsz, L, _ = u.shape
    uf = u.astype(jnp.float32).reshape(bsz, L, SSM_GROUPS, SSM_GROUP)
    lr = jnp.minimum(lam_re.astype(jnp.float32), -1e-4)
    li = lam_im.astype(jnp.float32)
    dt = jnp.exp(log_dt.astype(jnp.float32))[:, None]
    mag = jnp.exp(lr * dt)
    ar = mag * jnp.cos(li * dt)
    ai = mag * jnp.sin(li * dt)
    den = lr * lr + li * li
    fr = ((ar - 1.0) * lr + ai * li) / den
    fi = (ai * lr - (ar - 1.0) * li) / den
    br = b_re.astype(jnp.float32)
    bi = b_im.astype(jnp.float32)
    bbar_re = fr[..., None] * br - fi[..., None] * bi
    bbar_im = fr[..., None] * bi + fi[..., None] * br
    bu_re = jnp.einsum('blgh,gph->blgp', uf, bbar_re)
    bu_im = jnp.einsum('blgh,gph->blgp', uf, bbar_im)
    a_re = jnp.broadcast_to(ar, (1, L, SSM_GROUPS, SSM_STATE))
    a_im = jnp.broadcast_to(ai, (1, L, SSM_GROUPS, SSM_STATE))
    _, _, xr, xi = lax.associative_scan(_s5_combine, (a_re, a_im, bu_re, bu_im), axis=1)
    y = (jnp.einsum('blgp,ghp->blgh', xr, c_re.astype(jnp.float32))
         - jnp.einsum('blgp,ghp->blgh', xi, c_im.astype(jnp.float32))
         + d.astype(jnp.float32) * uf)
    y = jax.nn.gelu(y.reshape(bsz, L, SSM_WIDTH))
    z = y @ w_glu.astype(jnp.float32) + b_glu.astype(jnp.float32)
    out = z[..., :SSM_WIDTH] * jax.nn.sigmoid(z[..., SSM_WIDTH:])
    return out.astype(u.dtype)


def partial_rotary(t, positions):
    half = ROPE_DIM // 2
    inv_freq = ROPE_THETA ** (-jnp.arange(half, dtype=jnp.float32) * 2.0 / ROPE_DIM)
    ang = positions.astype(jnp.float32)[..., None] * inv_freq
    cos = jnp.cos(ang)[:, :, None, :]
    sin = jnp.sin(ang)[:, :, None, :]
    tf = t.astype(jnp.float32)
    t1 = tf[..., :half]
    t2 = tf[..., half:ROPE_DIM]
    rot = jnp.concatenate([t1 * cos - t2 * sin, t2 * cos + t1 * sin, tf[..., ROPE_DIM:]], axis=-1)
    return rot.astype(t.dtype)


def sliding_window_sink_attention(q, k, v, sinks):
    bsz, L = q.shape[0], q.shape[1]
    nb = L // BLOCK
    qb = q.reshape(bsz, nb, BLOCK, N_KV_HEADS, Q_PER_KV, HEAD_DIM).transpose(1, 0, 2, 3, 4, 5)

    def windows(t):
        tb = t.reshape(bsz, nb, BLOCK, N_KV_HEADS, HEAD_DIM)
        prev = jnp.pad(tb, ((0, 0), (1, 0), (0, 0), (0, 0), (0, 0)))[:, :-1]
        return jnp.concatenate([prev, tb], axis=2).transpose(1, 0, 2, 3, 4)

    kw = windows(k)
    vw = windows(v)
    qi = jnp.arange(BLOCK)[:, None]
    kj = jnp.arange(2 * BLOCK)[None, :]
    rel = qi + BLOCK - kj
    band = (rel >= 0) & (rel < WINDOW)
    sink = sinks.astype(jnp.float32).reshape(N_KV_HEADS, Q_PER_KV)[None, :, :, None, None]
    scale = HEAD_DIM ** -0.5

    def one_block(args):
        q_blk, k_blk, v_blk, blk = args
        valid = band & (blk * BLOCK - BLOCK + kj >= 0)
        s = jnp.einsum('bqhgd,bkhd->bhgqk', q_blk, k_blk).astype(jnp.float32) * scale
        s = jnp.where(valid, s, MASK_VALUE)
        m = jnp.maximum(jnp.max(s, axis=-1, keepdims=True), sink)
        p = jnp.exp(s - m)
        p = p / (jnp.sum(p, axis=-1, keepdims=True) + jnp.exp(sink - m))
        return jnp.einsum('bhgqk,bkhd->bqhgd', p.astype(v_blk.dtype), v_blk)

    o = lax.map(one_block, (qb, kw, vw, jnp.arange(nb)))
    return o.transpose(1, 0, 2, 3, 4, 5).reshape(bsz, L, ATTN_WIDTH)


def _fwd_setup_inputs(seed: int = 0) -> dict:
    key = jax.random.key(seed)
    ks = jax.random.split(key, 24)
    f32 = jnp.float32

    def nrm(k, shape, std):
        return jax.random.normal(k, shape, f32) * std

    def gain(k, n):
        return 1.0 + 0.05 * jax.random.normal(k, (DEPTH, n), f32)

    n_idx = jnp.arange(SSM_STATE, dtype=f32)
    lam_re = -0.5 + 0.01 * jax.random.normal(ks[3], (DEPTH, SSM_GROUPS, SSM_STATE), f32)
    lam_im = math.pi * n_idx + 0.01 * jax.random.normal(ks[4], (DEPTH, SSM_GROUPS, SSM_STATE), f32)
    log_dt = jax.random.uniform(ks[5], (DEPTH, SSM_GROUPS), f32, math.log(DT_MIN), math.log(DT_MAX))
    return {
        "x": jax.random.normal(ks[0], (BATCH, SEQ, D_MODEL), f32),
        "positions": jnp.broadcast_to(jnp.arange(SEQ, dtype=jnp.int32), (BATCH, SEQ)),
        "g_pre_mix": gain(ks[1], D_MODEL),
        "w_in": nrm(ks[2], (DEPTH, D_MODEL, IN_WIDTH), D_MODEL ** -0.5),
        "ssm_lambda_re": lam_re,
        "ssm_lambda_im": lam_im,
        "ssm_log_dt": log_dt,
        "ssm_b_re": nrm(ks[6], (DEPTH, SSM_GROUPS, SSM_STATE, SSM_GROUP), (2 * SSM_GROUP) ** -0.5),
        "ssm_b_im": nrm(ks[7], (DEPTH, SSM_GROUPS, SSM_STATE, SSM_GROUP), (2 * SSM_GROUP) ** -0.5),
        "ssm_c_re": nrm(ks[8], (DEPTH, SSM_GROUPS, SSM_GROUP, SSM_STATE), (2 * SSM_STATE) ** -0.5),
        "ssm_c_im": nrm(ks[9], (DEPTH, SSM_GROUPS, SSM_GROUP, SSM_STATE), (2 * SSM_STATE) ** -0.5),
        "ssm_d": nrm(ks[10], (DEPTH, SSM_GROUPS, SSM_GROUP), 1.0),
        "w_glu": nrm(ks[11], (DEPTH, SSM_WIDTH, 2 * SSM_WIDTH), SSM_WIDTH ** -0.5),
        "b_glu": nrm(ks[12], (DEPTH, 2 * SSM_WIDTH), 0.01),
        "attn_sinks": nrm(ks[13], (DEPTH, N_Q_HEADS), 0.5),
        "g_ssm_out": gain(ks[14], SSM_WIDTH),
        "g_attn_out": gain(ks[15], ATTN_WIDTH),
        "w_out": nrm(ks[16], (DEPTH, MIX_WIDTH, D_MODEL), MIX_WIDTH ** -0.5),
        "g_post_mix": gain(ks[17], D_MODEL),
        "g_pre_ffn": gain(ks[18], D_MODEL),
        "w_gate_up": nrm(ks[19], (DEPTH, D_MODEL, 2 * D_FF), D_MODEL ** -0.5),
        "w_down": nrm(ks[20], (DEPTH, D_FF, D_MODEL), D_FF ** -0.5),
        "g_post_ffn": gain(ks[21], D_MODEL),
    }


def _fwd_reference(x, positions, g_pre_mix, w_in, ssm_lambda_re, ssm_lambda_im, ssm_log_dt,
              ssm_b_re, ssm_b_im, ssm_c_re, ssm_c_im, ssm_d, w_glu, b_glu, attn_sinks,
              g_ssm_out, g_attn_out, w_out, g_post_mix, g_pre_ffn, w_gate_up, w_down, g_post_ffn):
    bsz, L, _ = x.shape
    h = x
    for l in range(DEPTH):
        hn = rms_norm(h, g_pre_mix[l])
        proj = hn @ w_in[l]
        u = proj[..., :SSM_WIDTH]
        q = proj[..., SSM_WIDTH:SSM_WIDTH + ATTN_WIDTH]
        k = proj[..., SSM_WIDTH + ATTN_WIDTH:SSM_WIDTH + ATTN_WIDTH + KV_WIDTH]
        v = proj[..., SSM_WIDTH + ATTN_WIDTH + KV_WIDTH:]

        y_ssm = s5_mixer(u, ssm_lambda_re[l], ssm_lambda_im[l], ssm_log_dt[l],
                         ssm_b_re[l], ssm_b_im[l], ssm_c_re[l], ssm_c_im[l], ssm_d[l],
                         w_glu[l], b_glu[l])

        q = partial_rotary(q.reshape(bsz, L, N_Q_HEADS, HEAD_DIM), positions)
        k = partial_rotary(k.reshape(bsz, L, N_KV_HEADS, HEAD_DIM), positions)
        v = v.reshape(bsz, L, N_KV_HEADS, HEAD_DIM)
        y_attn = sliding_window_sink_attention(q, k, v, attn_sinks[l])

        merged = jnp.concatenate([rms_norm(y_ssm, g_ssm_out[l]), rms_norm(y_attn, g_attn_out[l])], axis=-1)
        h = h + rms_norm(merged @ w_out[l], g_post_mix[l])

        hn = rms_norm(h, g_pre_ffn[l])
        gu = hn @ w_gate_up[l]
        ff = (jax.nn.silu(gu[..., :D_FF]) * gu[..., D_FF:]) @ w_down[l]
        h = h + rms_norm(ff, g_post_ffn[l])
    return h


import jax as _jax
import jax.numpy as _jnp

TWIN_FORMAT = 'train_step'
FWD_PARAMS = ['x', 'positions', 'g_pre_mix', 'w_in', 'ssm_lambda_re', 'ssm_lambda_im', 'ssm_log_dt', 'ssm_b_re', 'ssm_b_im', 'ssm_c_re', 'ssm_c_im', 'ssm_d', 'w_glu', 'b_glu', 'attn_sinks', 'g_ssm_out', 'g_attn_out', 'w_out', 'g_post_mix', 'g_pre_ffn', 'w_gate_up', 'w_down', 'g_post_ffn']
TWIN_WEIGHTS = ['g_pre_mix', 'w_in', 'ssm_lambda_re', 'ssm_lambda_im', 'ssm_log_dt', 'ssm_b_re', 'ssm_b_im', 'ssm_c_re', 'ssm_c_im', 'ssm_d', 'w_glu', 'b_glu', 'attn_sinks', 'g_ssm_out', 'g_attn_out', 'w_out', 'g_post_mix', 'g_pre_ffn', 'w_gate_up', 'w_down', 'g_post_ffn']
TWIN_DIFF_INPUT = 'x'
TWIN_INPUTS = ['x', 'positions', 'g_pre_mix', 'w_in', 'ssm_lambda_re', 'ssm_lambda_im', 'ssm_log_dt', 'ssm_b_re', 'ssm_b_im', 'ssm_c_re', 'ssm_c_im', 'ssm_d', 'w_glu', 'b_glu', 'attn_sinks', 'g_ssm_out', 'g_attn_out', 'w_out', 'g_post_mix', 'g_pre_ffn', 'w_gate_up', 'w_down', 'g_post_ffn', 'loss_target', 'm_g_pre_mix', 'm_w_in', 'm_ssm_lambda_re', 'm_ssm_lambda_im', 'm_ssm_log_dt', 'm_ssm_b_re', 'm_ssm_b_im', 'm_ssm_c_re', 'm_ssm_c_im', 'm_ssm_d', 'm_w_glu', 'm_b_glu', 'm_attn_sinks', 'm_g_ssm_out', 'm_g_attn_out', 'm_w_out', 'm_g_post_mix', 'm_g_pre_ffn', 'm_w_gate_up', 'm_w_down', 'm_g_post_ffn', 'v_g_pre_mix', 'v_w_in', 'v_ssm_lambda_re', 'v_ssm_lambda_im', 'v_ssm_log_dt', 'v_ssm_b_re', 'v_ssm_b_im', 'v_ssm_c_re', 'v_ssm_c_im', 'v_ssm_d', 'v_w_glu', 'v_b_glu', 'v_attn_sinks', 'v_g_ssm_out', 'v_g_attn_out', 'v_w_out', 'v_g_post_mix', 'v_g_pre_ffn', 'v_w_gate_up', 'v_w_down', 'v_g_post_ffn']
TWIN_OUTPUTS = ['loss', 'grad_x', 'grad_g_pre_mix', 'grad_w_in', 'grad_ssm_lambda_re', 'grad_ssm_lambda_im', 'grad_ssm_log_dt', 'grad_ssm_b_re', 'grad_ssm_b_im', 'grad_ssm_c_re', 'grad_ssm_c_im', 'grad_ssm_d', 'grad_w_glu', 'grad_b_glu', 'grad_attn_sinks', 'grad_g_ssm_out', 'grad_g_attn_out', 'grad_w_out', 'grad_g_post_mix', 'grad_g_pre_ffn', 'grad_w_gate_up', 'grad_w_down', 'grad_g_post_ffn', 'delta_g_pre_mix', 'delta_w_in', 'delta_ssm_lambda_re', 'delta_ssm_lambda_im', 'delta_ssm_log_dt', 'delta_ssm_b_re', 'delta_ssm_b_im', 'delta_ssm_c_re', 'delta_ssm_c_im', 'delta_ssm_d', 'delta_w_glu', 'delta_b_glu', 'delta_attn_sinks', 'delta_g_ssm_out', 'delta_g_attn_out', 'delta_w_out', 'delta_g_post_mix', 'delta_g_pre_ffn', 'delta_w_gate_up', 'delta_w_down', 'delta_g_post_ffn', 'new_m_g_pre_mix', 'new_m_w_in', 'new_m_ssm_lambda_re', 'new_m_ssm_lambda_im', 'new_m_ssm_log_dt', 'new_m_ssm_b_re', 'new_m_ssm_b_im', 'new_m_ssm_c_re', 'new_m_ssm_c_im', 'new_m_ssm_d', 'new_m_w_glu', 'new_m_b_glu', 'new_m_attn_sinks', 'new_m_g_ssm_out', 'new_m_g_attn_out', 'new_m_w_out', 'new_m_g_post_mix', 'new_m_g_pre_ffn', 'new_m_w_gate_up', 'new_m_w_down', 'new_m_g_post_ffn', 'new_v_g_pre_mix', 'new_v_w_in', 'new_v_ssm_lambda_re', 'new_v_ssm_lambda_im', 'new_v_ssm_log_dt', 'new_v_ssm_b_re', 'new_v_ssm_b_im', 'new_v_ssm_c_re', 'new_v_ssm_c_im', 'new_v_ssm_d', 'new_v_w_glu', 'new_v_b_glu', 'new_v_attn_sinks', 'new_v_g_ssm_out', 'new_v_g_attn_out', 'new_v_w_out', 'new_v_g_post_mix', 'new_v_g_pre_ffn', 'new_v_w_gate_up', 'new_v_w_down', 'new_v_g_post_ffn']
TWIN_LEAF_KINDS = {'loss': 'loss', 'grad_x': 'grad_x', 'grad_g_pre_mix': 'grad_w', 'grad_w_in': 'grad_w', 'grad_ssm_lambda_re': 'grad_w', 'grad_ssm_lambda_im': 'grad_w', 'grad_ssm_log_dt': 'grad_w', 'grad_ssm_b_re': 'grad_w', 'grad_ssm_b_im': 'grad_w', 'grad_ssm_c_re': 'grad_w', 'grad_ssm_c_im': 'grad_w', 'grad_ssm_d': 'grad_w', 'grad_w_glu': 'grad_w', 'grad_b_glu': 'grad_w', 'grad_attn_sinks': 'grad_w', 'grad_g_ssm_out': 'grad_w', 'grad_g_attn_out': 'grad_w', 'grad_w_out': 'grad_w', 'grad_g_post_mix': 'grad_w', 'grad_g_pre_ffn': 'grad_w', 'grad_w_gate_up': 'grad_w', 'grad_w_down': 'grad_w', 'grad_g_post_ffn': 'grad_w', 'delta_g_pre_mix': 'delta_w', 'delta_w_in': 'delta_w', 'delta_ssm_lambda_re': 'delta_w', 'delta_ssm_lambda_im': 'delta_w', 'delta_ssm_log_dt': 'delta_w', 'delta_ssm_b_re': 'delta_w', 'delta_ssm_b_im': 'delta_w', 'delta_ssm_c_re': 'delta_w', 'delta_ssm_c_im': 'delta_w', 'delta_ssm_d': 'delta_w', 'delta_w_glu': 'delta_w', 'delta_b_glu': 'delta_w', 'delta_attn_sinks': 'delta_w', 'delta_g_ssm_out': 'delta_w', 'delta_g_attn_out': 'delta_w', 'delta_w_out': 'delta_w', 'delta_g_post_mix': 'delta_w', 'delta_g_pre_ffn': 'delta_w', 'delta_w_gate_up': 'delta_w', 'delta_w_down': 'delta_w', 'delta_g_post_ffn': 'delta_w', 'new_m_g_pre_mix': 'new_m', 'new_m_w_in': 'new_m', 'new_m_ssm_lambda_re': 'new_m', 'new_m_ssm_lambda_im': 'new_m', 'new_m_ssm_log_dt': 'new_m', 'new_m_ssm_b_re': 'new_m', 'new_m_ssm_b_im': 'new_m', 'new_m_ssm_c_re': 'new_m', 'new_m_ssm_c_im': 'new_m', 'new_m_ssm_d': 'new_m', 'new_m_w_glu': 'new_m', 'new_m_b_glu': 'new_m', 'new_m_attn_sinks': 'new_m', 'new_m_g_ssm_out': 'new_m', 'new_m_g_attn_out': 'new_m', 'new_m_w_out': 'new_m', 'new_m_g_post_mix': 'new_m', 'new_m_g_pre_ffn': 'new_m', 'new_m_w_gate_up': 'new_m', 'new_m_w_down': 'new_m', 'new_m_g_post_ffn': 'new_m', 'new_v_g_pre_mix': 'new_v', 'new_v_w_in': 'new_v', 'new_v_ssm_lambda_re': 'new_v', 'new_v_ssm_lambda_im': 'new_v', 'new_v_ssm_log_dt': 'new_v', 'new_v_ssm_b_re': 'new_v', 'new_v_ssm_b_im': 'new_v', 'new_v_ssm_c_re': 'new_v', 'new_v_ssm_c_im': 'new_v', 'new_v_ssm_d': 'new_v', 'new_v_w_glu': 'new_v', 'new_v_b_glu': 'new_v', 'new_v_attn_sinks': 'new_v', 'new_v_g_ssm_out': 'new_v', 'new_v_g_attn_out': 'new_v', 'new_v_w_out': 'new_v', 'new_v_g_post_mix': 'new_v', 'new_v_g_pre_ffn': 'new_v', 'new_v_w_gate_up': 'new_v', 'new_v_w_down': 'new_v', 'new_v_g_post_ffn': 'new_v'}


def _forward(args):
    return _fwd_reference(*[args[k] for k in FWD_PARAMS])


def _output_shape():
    out = _jax.eval_shape(lambda: _forward(_fwd_setup_inputs(0)))
    return out.shape, out.dtype

N_MICROBATCH = 1
ADAM_LR = 0.001
ADAM_B1 = 0.9
ADAM_B2 = 0.999
ADAM_EPS = 1e-08
ADAM_WD = 0.01
ADAM_STEP = 10
PER_EXAMPLE_BATCH_AXIS = {'x': 0, 'positions': 0, 'loss_target': 0}
SHARED_INPUTS = []
_WEIGHT_DTYPES = {'g_pre_mix': _jnp.float32, 'w_in': _jnp.float32, 'ssm_lambda_re': _jnp.float32, 'ssm_lambda_im': _jnp.float32, 'ssm_log_dt': _jnp.float32, 'ssm_b_re': _jnp.float32, 'ssm_b_im': _jnp.float32, 'ssm_c_re': _jnp.float32, 'ssm_c_im': _jnp.float32, 'ssm_d': _jnp.float32, 'w_glu': _jnp.float32, 'b_glu': _jnp.float32, 'attn_sinks': _jnp.float32, 'g_ssm_out': _jnp.float32, 'g_attn_out': _jnp.float32, 'w_out': _jnp.float32, 'g_post_mix': _jnp.float32, 'g_pre_ffn': _jnp.float32, 'w_gate_up': _jnp.float32, 'w_down': _jnp.float32, 'g_post_ffn': _jnp.float32}
MOMENT_SCALE = {'g_pre_mix': 5.875819e-01, 'w_in': 4.696927e-01, 'ssm_lambda_re': 1.713563e-02, 'ssm_lambda_im': 1.525931e-02, 'ssm_log_dt': 1.070147e+01, 'ssm_b_re': 1.173219e-02, 'ssm_b_im': 1.127819e-02, 'ssm_c_re': 2.299231e-02, 'ssm_c_im': 2.221124e-02, 'ssm_d': 1.223527e+00, 'w_glu': 8.797848e-01, 'b_glu': 2.614608e+00, 'attn_sinks': 9.571195e-02, 'g_ssm_out': 1.432623e+00, 'g_attn_out': 4.822425e-01, 'w_out': 8.474294e-01, 'g_post_mix': 1.608192e+01, 'g_pre_ffn': 7.170275e-01, 'w_gate_up': 2.927551e-01, 'w_down': 5.894058e-01, 'g_post_ffn': 1.604242e+01}


def _to_microbatches(a, axis):
    t = _jnp.moveaxis(a, axis, 0)
    t = t.reshape((N_MICROBATCH, t.shape[0] // N_MICROBATCH) + t.shape[1:])
    return _jnp.moveaxis(t, 1, axis + 1)


def setup_inputs(seed: int = 0) -> dict:
    inp = _fwd_setup_inputs(seed)
    key = _jax.random.fold_in(_jax.random.key(seed), 7919)
    shape, _ = _output_shape()
    out = dict(inp)
    out["loss_target"] = _jax.random.normal(_jax.random.fold_in(key, 0), shape, _jnp.float32)
    for i, name in enumerate(TWIN_WEIGHTS):
        w = inp[name].astype(_jnp.float32)
        if MOMENT_SCALE is None:
            s = _jnp.sqrt(_jnp.mean(_jnp.square(w)) + 1e-30)
        else:
            s = MOMENT_SCALE[name]
        km, kv = _jax.random.split(_jax.random.fold_in(key, i + 1))
        out[name] = w
        out["m_" + name] = s * _jax.random.normal(km, w.shape, _jnp.float32)
        out["v_" + name] = (s * s) * _jax.random.uniform(kv, w.shape, _jnp.float32, 0.5, 1.5)
    if N_MICROBATCH > 1:
        for name, axis in PER_EXAMPLE_BATCH_AXIS.items():
            out[name] = _to_microbatches(out[name], axis)
    return {'x': out['x'], 'positions': out['positions'], 'g_pre_mix': out['g_pre_mix'], 'w_in': out['w_in'], 'ssm_lambda_re': out['ssm_lambda_re'], 'ssm_lambda_im': out['ssm_lambda_im'], 'ssm_log_dt': out['ssm_log_dt'], 'ssm_b_re': out['ssm_b_re'], 'ssm_b_im': out['ssm_b_im'], 'ssm_c_re': out['ssm_c_re'], 'ssm_c_im': out['ssm_c_im'], 'ssm_d': out['ssm_d'], 'w_glu': out['w_glu'], 'b_glu': out['b_glu'], 'attn_sinks': out['attn_sinks'], 'g_ssm_out': out['g_ssm_out'], 'g_attn_out': out['g_attn_out'], 'w_out': out['w_out'], 'g_post_mix': out['g_post_mix'], 'g_pre_ffn': out['g_pre_ffn'], 'w_gate_up': out['w_gate_up'], 'w_down': out['w_down'], 'g_post_ffn': out['g_post_ffn'], 'loss_target': out['loss_target'], 'm_g_pre_mix': out['m_g_pre_mix'], 'm_w_in': out['m_w_in'], 'm_ssm_lambda_re': out['m_ssm_lambda_re'], 'm_ssm_lambda_im': out['m_ssm_lambda_im'], 'm_ssm_log_dt': out['m_ssm_log_dt'], 'm_ssm_b_re': out['m_ssm_b_re'], 'm_ssm_b_im': out['m_ssm_b_im'], 'm_ssm_c_re': out['m_ssm_c_re'], 'm_ssm_c_im': out['m_ssm_c_im'], 'm_ssm_d': out['m_ssm_d'], 'm_w_glu': out['m_w_glu'], 'm_b_glu': out['m_b_glu'], 'm_attn_sinks': out['m_attn_sinks'], 'm_g_ssm_out': out['m_g_ssm_out'], 'm_g_attn_out': out['m_g_attn_out'], 'm_w_out': out['m_w_out'], 'm_g_post_mix': out['m_g_post_mix'], 'm_g_pre_ffn': out['m_g_pre_ffn'], 'm_w_gate_up': out['m_w_gate_up'], 'm_w_down': out['m_w_down'], 'm_g_post_ffn': out['m_g_post_ffn'], 'v_g_pre_mix': out['v_g_pre_mix'], 'v_w_in': out['v_w_in'], 'v_ssm_lambda_re': out['v_ssm_lambda_re'], 'v_ssm_lambda_im': out['v_ssm_lambda_im'], 'v_ssm_log_dt': out['v_ssm_log_dt'], 'v_ssm_b_re': out['v_ssm_b_re'], 'v_ssm_b_im': out['v_ssm_b_im'], 'v_ssm_c_re': out['v_ssm_c_re'], 'v_ssm_c_im': out['v_ssm_c_im'], 'v_ssm_d': out['v_ssm_d'], 'v_w_glu': out['v_w_glu'], 'v_b_glu': out['v_b_glu'], 'v_attn_sinks': out['v_attn_sinks'], 'v_g_ssm_out': out['v_g_ssm_out'], 'v_g_attn_out': out['v_g_attn_out'], 'v_w_out': out['v_w_out'], 'v_g_post_mix': out['v_g_post_mix'], 'v_g_pre_ffn': out['v_g_pre_ffn'], 'v_w_gate_up': out['v_w_gate_up'], 'v_w_down': out['v_w_down'], 'v_g_post_ffn': out['v_g_post_ffn']}


def _loss(weights, diff, rest, loss_target):
    with _jax.named_scope("forward"):
        args = {**rest, TWIN_DIFF_INPUT: diff, **{k: w.astype(_WEIGHT_DTYPES[k]) for k, w in weights.items()}}
        y = _forward(args)
    with _jax.named_scope("loss_head"):
        err = _jnp.square(y.astype(_jnp.float32) - loss_target)
        return 0.5 * _jnp.sum(_jnp.mean(err, axis=-1)) if err.ndim else 0.5 * err


def _adamw(w, g, m, v):
    m = ADAM_B1 * m + (1.0 - ADAM_B1) * g
    v = ADAM_B2 * v + (1.0 - ADAM_B2) * _jnp.square(g)
    m_hat = m / (1.0 - ADAM_B1 ** ADAM_STEP)
    v_hat = v / (1.0 - ADAM_B2 ** ADAM_STEP)
    delta = -ADAM_LR * (m_hat / (_jnp.sqrt(v_hat) + ADAM_EPS) + ADAM_WD * w)
    return delta, m, v


def reference(x, positions, g_pre_mix, w_in, ssm_lambda_re, ssm_lambda_im, ssm_log_dt, ssm_b_re, ssm_b_im, ssm_c_re, ssm_c_im, ssm_d, w_glu, b_glu, attn_sinks, g_ssm_out, g_attn_out, w_out, g_post_mix, g_pre_ffn, w_gate_up, w_down, g_post_ffn, loss_target, m_g_pre_mix, m_w_in, m_ssm_lambda_re, m_ssm_lambda_im, m_ssm_log_dt, m_ssm_b_re, m_ssm_b_im, m_ssm_c_re, m_ssm_c_im, m_ssm_d, m_w_glu, m_b_glu, m_attn_sinks, m_g_ssm_out, m_g_attn_out, m_w_out, m_g_post_mix, m_g_pre_ffn, m_w_gate_up, m_w_down, m_g_post_ffn, v_g_pre_mix, v_w_in, v_ssm_lambda_re, v_ssm_lambda_im, v_ssm_log_dt, v_ssm_b_re, v_ssm_b_im, v_ssm_c_re, v_ssm_c_im, v_ssm_d, v_w_glu, v_b_glu, v_attn_sinks, v_g_ssm_out, v_g_attn_out, v_w_out, v_g_post_mix, v_g_pre_ffn, v_w_gate_up, v_w_down, v_g_post_ffn):
    given = dict(x=x, positions=positions, g_pre_mix=g_pre_mix, w_in=w_in, ssm_lambda_re=ssm_lambda_re, ssm_lambda_im=ssm_lambda_im, ssm_log_dt=ssm_log_dt, ssm_b_re=ssm_b_re, ssm_b_im=ssm_b_im, ssm_c_re=ssm_c_re, ssm_c_im=ssm_c_im, ssm_d=ssm_d, w_glu=w_glu, b_glu=b_glu, attn_sinks=attn_sinks, g_ssm_out=g_ssm_out, g_attn_out=g_attn_out, w_out=w_out, g_post_mix=g_post_mix, g_pre_ffn=g_pre_ffn, w_gate_up=w_gate_up, w_down=w_down, g_post_ffn=g_post_ffn, loss_target=loss_target, m_g_pre_mix=m_g_pre_mix, m_w_in=m_w_in, m_ssm_lambda_re=m_ssm_lambda_re, m_ssm_lambda_im=m_ssm_lambda_im, m_ssm_log_dt=m_ssm_log_dt, m_ssm_b_re=m_ssm_b_re, m_ssm_b_im=m_ssm_b_im, m_ssm_c_re=m_ssm_c_re, m_ssm_c_im=m_ssm_c_im, m_ssm_d=m_ssm_d, m_w_glu=m_w_glu, m_b_glu=m_b_glu, m_attn_sinks=m_attn_sinks, m_g_ssm_out=m_g_ssm_out, m_g_attn_out=m_g_attn_out, m_w_out=m_w_out, m_g_post_mix=m_g_post_mix, m_g_pre_ffn=m_g_pre_ffn, m_w_gate_up=m_w_gate_up, m_w_down=m_w_down, m_g_post_ffn=m_g_post_ffn, v_g_pre_mix=v_g_pre_mix, v_w_in=v_w_in, v_ssm_lambda_re=v_ssm_lambda_re, v_ssm_lambda_im=v_ssm_lambda_im, v_ssm_log_dt=v_ssm_log_dt, v_ssm_b_re=v_ssm_b_re, v_ssm_b_im=v_ssm_b_im, v_ssm_c_re=v_ssm_c_re, v_ssm_c_im=v_ssm_c_im, v_ssm_d=v_ssm_d, v_w_glu=v_w_glu, v_b_glu=v_b_glu, v_attn_sinks=v_attn_sinks, v_g_ssm_out=v_g_ssm_out, v_g_attn_out=v_g_attn_out, v_w_out=v_w_out, v_g_post_mix=v_g_post_mix, v_g_pre_ffn=v_g_pre_ffn, v_w_gate_up=v_w_gate_up, v_w_down=v_w_down, v_g_post_ffn=v_g_post_ffn)
    weights = {n: given[n] for n in TWIN_WEIGHTS}
    shared = {n: given[n] for n in SHARED_INPUTS}
    per_example = {n: given[n] for n in ['x', 'positions']}
    grad_fn = _jax.value_and_grad(_loss, argnums=(0, 1))

    def one_microbatch(ex, loss_target):
        ex = dict(ex)
        diff = ex.pop(TWIN_DIFF_INPUT)
        return grad_fn(weights, diff, {**shared, **ex}, loss_target)

    if N_MICROBATCH == 1:
        loss, (grad_w, grad_x) = one_microbatch(per_example, given["loss_target"])
    else:
        def body(carry, xs):
            loss_sum, grad_sum = carry
            l_k, (gw_k, gx_k) = one_microbatch(xs[0], xs[1])
            with _jax.named_scope("update"):
                return (loss_sum + l_k, _jax.tree.map(_jnp.add, grad_sum, gw_k)), gx_k

        init = (_jnp.zeros((), _jnp.float32), _jax.tree.map(_jnp.zeros_like, weights))
        (loss, grad_w), grad_x = _jax.lax.scan(body, init, (per_example, given["loss_target"]))
    with _jax.named_scope("update"):
        delta_w, new_m, new_v = {}, {}, {}
        for n in TWIN_WEIGHTS:
            delta_w[n], new_m[n], new_v[n] = _adamw(weights[n], grad_w[n], given["m_" + n], given["v_" + n])
    return (loss, grad_x, *[grad_w[n] for n in TWIN_WEIGHTS], *[delta_w[n] for n in TWIN_WEIGHTS],
            *[new_m[n] for n in TWIN_WEIGHTS], *[new_v[n] for n in TWIN_WEIGHTS])
```

```python
import functools
import math

import numpy as np
import jax
import jax.numpy as jnp
from jax import lax
from jax.experimental import pallas as pl
from jax.experimental.pallas import tpu as pltpu

D_MODEL = 1024
SSM_WIDTH = 512
SSM_GROUP = 16
SSM_GROUPS = 32
SSM_STATE = 64
N_STATE = SSM_GROUPS * SSM_STATE
ATTN_WIDTH = 512
HEAD_DIM = 64
N_Q_HEADS = 8
N_KV_HEADS = 2
Q_PER_KV = 4
KV_WIDTH = 128
IN_WIDTH = 1280
BLOCK = 128
ROPE_DIM = 16
ROPE_THETA = 500000.0
D_FF = 2816
NORM_EPS = 1e-6
MASK_VALUE = -1e30
ADAM_LR = 0.001
ADAM_B1 = 0.9
ADAM_B2 = 0.999
ADAM_EPS = 1e-08
ADAM_WD = 0.01
ADAM_STEP = 10

N_DEV = 8
SCAN_CHUNKS = 8
SCAN_COLS = 512
TOKEN_TILE = 256
VMEM_LIMIT = 56 * 1024 * 1024

_F32 = jnp.float32
_BF16 = jnp.bfloat16
_MXU = jnp.bfloat16

_NN = ((1,), (0,))
_NT = ((1,), (1,))
_TN = ((0,), (0,))


def _dot(a, b, dims):
    return lax.dot_general(a.astype(_MXU), b.astype(_MXU), (dims, ((), ())),
                           preferred_element_type=_F32)


def _dot_exact(a, b, dims):
    return lax.dot_general(a.astype(_F32), b.astype(_F32), (dims, ((), ())),
                           precision=lax.Precision.HIGHEST, preferred_element_type=_F32)


def _iota(shape, dim):
    return lax.broadcasted_iota(jnp.int32, shape, dim)


def _rms_fwd(x, g):
    r = lax.rsqrt(jnp.mean(x * x, axis=-1, keepdims=True) + NORM_EPS)
    return x * r * g, r


def _rms_bwd(dy, x, g, r):
    a = dy * g
    xn = x * r
    dx = r * (a - xn * jnp.mean(a * xn, axis=-1, keepdims=True))
    dg = jnp.sum(dy * xn, axis=0, keepdims=True)
    return dx, dg


def _call(body, grid, in_specs, out_specs, out_shape, name, scratch=()):
    params = pltpu.CompilerParams(dimension_semantics=("arbitrary",) * len(grid),
                                  vmem_limit_bytes=VMEM_LIMIT)
    return pl.pallas_call(body, grid=grid, in_specs=in_specs, out_specs=out_specs,
                          out_shape=out_shape, scratch_shapes=list(scratch),
                          compiler_params=params, name=name)


def _rows(tm, n):
    return pl.BlockSpec((tm, n), lambda i: (i, 0))


def _whole(shape):
    nd = len(shape)
    return pl.BlockSpec(shape, lambda i: (0,) * nd)


def _sds(shape, dtype):
    return jax.ShapeDtypeStruct(shape, dtype)


def _tile(L):
    return min(TOKEN_TILE, L)


def _accumulate(ref, val, first):
    @pl.when(first)
    def _():
        ref[...] = val

    @pl.when(jnp.logical_not(first))
    def _():
        ref[...] += val


def _rope_rows():
    half = ROPE_DIM // 2
    inv = (np.float32(ROPE_THETA) ** (-np.arange(half, dtype=np.float32) * np.float32(2.0) / np.float32(ROPE_DIM))).astype(np.float32)
    col = np.arange(KV_WIDTH) % HEAD_DIM
    freq = np.where(col < ROPE_DIM, inv[col % half], 0.0).astype(np.float32)
    sign = np.where(col < half, -1.0, np.where(col < ROPE_DIM, 1.0, 0.0)).astype(np.float32)
    return freq[None, :], sign[None, :]


def _rope_tables(pos_col):
    L = pos_col.shape[0]
    tm = _tile(L)
    freq, sign = _rope_rows()

    def body(pos_ref, freq_ref, sign_ref, cos_ref, sin_ref):
        ang = pos_ref[...].astype(_F32) * freq_ref[...]
        cos_ref[...] = jnp.cos(ang)
        sin_ref[...] = jnp.sin(ang) * sign_ref[...]

    return _call(body, (L // tm,),
                 [_rows(tm, 1), _whole((1, KV_WIDTH)), _whole((1, KV_WIDTH))],
                 [_rows(tm, KV_WIDTH), _rows(tm, KV_WIDTH)],
                 [_sds((L, KV_WIDTH), _F32)] * 2, "rope_tables")(pos_col, jnp.asarray(freq), jnp.asarray(sign))


def _widen(t, width):
    return t if width == KV_WIDTH else jnp.concatenate([t] * (width // KV_WIDTH), axis=1)


def _rope_partner(t):
    w = t.shape[1]
    in_head = _iota((1, w), 1) & (HEAD_DIM - 1)
    second = jnp.where(in_head < ROPE_DIM, pltpu.roll(t, ROPE_DIM // 2, 1), 0.0)
    return jnp.where(in_head < ROPE_DIM // 2, pltpu.roll(t, w - ROPE_DIM // 2, 1), second)


def _rope_apply(t, cos_t, sin_t):
    w = t.shape[1]
    return t * _widen(cos_t, w) + _rope_partner(t) * _widen(sin_t, w)


def _rope_transpose(dt, cos_t, sin_t):
    w = dt.shape[1]
    return dt * _widen(cos_t, w) + _rope_partner(dt * _widen(sin_t, w))


def _in_proj(x, g_pre_mix, w_in, cos_t, sin_t):
    L = x.shape[0]
    tm = _tile(L)

    def body(x_ref, g_ref, w_ref, cos_ref, sin_ref, hn_ref, u_ref, q_ref, k_ref, v_ref):
        hn, _ = _rms_fwd(x_ref[...], g_ref[...])
        hn = hn.astype(_BF16)
        hn_ref[...] = hn
        proj = _dot(hn, w_ref[...], _NN)
        u_ref[...] = proj[:, :SSM_WIDTH]
        q = proj[:, SSM_WIDTH:SSM_WIDTH + ATTN_WIDTH]
        k = proj[:, SSM_WIDTH + ATTN_WIDTH:SSM_WIDTH + ATTN_WIDTH + KV_WIDTH]
        cos_v, sin_v = cos_ref[...], sin_ref[...]
        q_ref[...] = _rope_apply(q, cos_v, sin_v).astype(_BF16)
        k_ref[...] = _rope_apply(k, cos_v, sin_v).astype(_BF16)
        v_ref[...] = proj[:, SSM_WIDTH + ATTN_WIDTH + KV_WIDTH:].astype(_BF16)

    return _call(body, (L // tm,),
                 [_rows(tm, D_MODEL), _whole((1, D_MODEL)), _whole((D_MODEL, IN_WIDTH)),
                  _rows(tm, KV_WIDTH), _rows(tm, KV_WIDTH)],
                 [_rows(tm, D_MODEL), _rows(tm, SSM_WIDTH), _rows(tm, ATTN_WIDTH),
                  _rows(tm, KV_WIDTH), _rows(tm, KV_WIDTH)],
                 [_sds((L, D_MODEL), _BF16), _sds((L, SSM_WIDTH), _F32), _sds((L, ATTN_WIDTH), _BF16),
                  _sds((L, KV_WIDTH), _BF16), _sds((L, KV_WIDTH), _BF16)],
                 "in_proj")(x, g_pre_mix, w_in, cos_t, sin_t)


def _s5_discretize(lam_re, lam_im, log_dt):
    lr = jnp.minimum(lam_re, -1e-4)
    li = lam_im
    dt = jnp.exp(log_dt)
    mag = jnp.exp(lr * dt)
    ar = mag * jnp.cos(li * dt)
    ai = mag * jnp.sin(li * dt)
    den = lr * lr + li * li
    fr = ((ar - 1.0) * lr + ai * li) / den
    fi = (ai * lr - (ar - 1.0) * li) / den
    return ar, ai, fr, fi


def _s5_bbar(lam_re, lam_im, log_dt, b_re, b_im):
    ar, ai, fr, fi = _s5_discretize(lam_re, lam_im, log_dt)
    return ar, ai, fr * b_re - fi * b_im, fr * b_im + fi * b_re


def _spread_masks():
    e16 = (_iota((SSM_GROUP, SSM_WIDTH), 1) & (SSM_GROUP - 1)) == _iota((SSM_GROUP, SSM_WIDTH), 0)
    e64 = (_iota((SSM_STATE, N_STATE), 1) & (SSM_STATE - 1)) == _iota((SSM_STATE, N_STATE), 0)
    mask_b = (_iota((N_STATE, SSM_WIDTH), 0) >> 6) == (_iota((N_STATE, SSM_WIDTH), 1) >> 4)
    mask_c = (_iota((SSM_WIDTH, N_STATE), 0) >> 4) == (_iota((SSM_WIDTH, N_STATE), 1) >> 6)
    return e16.astype(_F32), e64.astype(_F32), mask_b, mask_c


def _ssm_prep(lam_re_r, lam_im_r, ldt_r, lam_re_c, lam_im_c, ldt_c, b_re2, b_im2, c_re2, c_im2):
    def body(lrr, lir, ldr, lrc, lic, ldc, bre, bim, cre, cim, ar_ref, ai_ref, btr, bti, ctr, cti):
        ar, ai, _, _ = _s5_discretize(lrr[...], lir[...], ldr[...])
        ar_ref[...] = ar
        ai_ref[...] = ai
        _, _, bbr, bbi = _s5_bbar(lrc[...], lic[...], ldc[...], bre[...], bim[...])
        e16, e64, mask_b, mask_c = _spread_masks()
        btr[...] = jnp.where(mask_b, _dot(bbr, e16, _NN), 0.0).astype(_BF16)
        bti[...] = jnp.where(mask_b, _dot(bbi, e16, _NN), 0.0).astype(_BF16)
        ctr[...] = jnp.where(mask_c, _dot(cre[...], e64, _NN), 0.0).astype(_BF16)
        cti[...] = jnp.where(mask_c, _dot(cim[...], e64, _NN), 0.0).astype(_BF16)

    row, col = (1, N_STATE), (N_STATE, 1)
    ins = [lam_re_r, lam_im_r, ldt_r, lam_re_c, lam_im_c, ldt_c, b_re2, b_im2, c_re2, c_im2]
    return _call(body, (1,), [_whole(a.shape) for a in ins],
                 [_whole(row), _whole(row), _whole((N_STATE, SSM_WIDTH)), _whole((N_STATE, SSM_WIDTH)),
                  _whole((SSM_WIDTH, N_STATE)), _whole((SSM_WIDTH, N_STATE))],
                 [_sds(row, _F32), _sds(row, _F32), _sds((N_STATE, SSM_WIDTH), _BF16),
                  _sds((N_STATE, SSM_WIDTH), _BF16), _sds((SSM_WIDTH, N_STATE), _BF16),
                  _sds((SSM_WIDTH, N_STATE), _BF16)], "ssm_prep")(*ins)


def _ssm_bu(u, bt_re, bt_im):
    L = u.shape[0]
    tm = _tile(L)

    def body(u_ref, br_ref, bi_ref, or_ref, oi_ref):
        ub = u_ref[...].astype(_BF16)
        or_ref[...] = _dot(ub, br_ref[...], _NT)
        oi_ref[...] = _dot(ub, bi_ref[...], _NT)

    return _call(body, (L // tm,),
                 [_rows(tm, SSM_WIDTH), _whole((N_STATE, SSM_WIDTH)), _whole((N_STATE, SSM_WIDTH))],
                 [_rows(tm, N_STATE), _rows(tm, N_STATE)],
                 [_sds((L, N_STATE), _F32)] * 2, "ssm_bu")(u, bt_re, bt_im)


def _complex_power(ar, ai, n):
    def step(_, c):
        pr, pi = c
        return pr * ar - pi * ai, pr * ai + pi * ar
    return lax.fori_loop(0, n, step, (jnp.ones_like(ar), jnp.zeros_like(ai)))


def _chunk_carries(er, ei, pr, pi, reverse):
    rows = _iota(er.shape, 0)
    sr = jnp.zeros_like(pr)
    si = jnp.zeros_like(pi)
    out_r = jnp.zeros_like(er)
    out_i = jnp.zeros_like(ei)
    order = range(SCAN_CHUNKS - 1, 0, -1) if reverse else range(SCAN_CHUNKS - 1)
    for c in order:
        e_r = er[c:c + 1, :]
        e_i = ei[c:c + 1, :]
        sr, si = pr * sr - pi * si + e_r, pr * si + pi * sr + e_i
        nxt = c - 1 if reverse else c + 1
        out_r = jnp.where(rows == nxt, sr, out_r)
        out_i = jnp.where(rows == nxt, si, out_i)
    return out_r, out_i


def _scan_fwd(b_re, b_im, a_re, a_im):
    T = b_re.shape[0]
    W = SCAN_COLS
    blk = pl.BlockSpec((T, SCAN_CHUNKS, W), lambda j: (0, 0, j))
    vec = pl.BlockSpec((1, W), lambda j: (0, j))

    def body(br_ref, bi_ref, ar_ref, ai_ref, xr_ref, xi_ref):
        ar, ai = ar_ref[...], ai_ref[...]
        ar8 = jnp.broadcast_to(ar, (SCAN_CHUNKS, W))
        ai8 = jnp.broadcast_to(ai, (SCAN_CHUNKS, W))

        def local(t, c):
            cr, ci = c
            return ar8 * cr - ai8 * ci + br_ref[t], ar8 * ci + ai8 * cr + bi_ref[t]

        zero = jnp.zeros((SCAN_CHUNKS, W), _F32)
        er, ei = lax.fori_loop(0, T, local, (zero, zero))
        pr, pi = _complex_power(ar, ai, T)
        sr, si = _chunk_carries(er, ei, pr, pi, reverse=False)

        def final(t, c):
            nr, ni = local(t, c)
            xr_ref[t] = nr
            xi_ref[t] = ni
            return nr, ni

        lax.fori_loop(0, T, final, (sr, si))

    shape = _sds(b_re.shape, _F32)
    return _call(body, (N_STATE // W,), [blk, blk, vec, vec], [blk, blk], [shape, shape],
                 "scan_fwd")(b_re, b_im, a_re, a_im)


def _scan_bwd(dx_re, dx_im, x_re, x_im, a_re, a_im):
    T = dx_re.shape[0]
    W = SCAN_COLS
    blk = pl.BlockSpec((T, SCAN_CHUNKS, W), lambda j: (0, 0, j))
    vec = pl.BlockSpec((1, W), lambda j: (0, j))

    def body(dr_ref, di_ref, xr_ref, xi_ref, ar_ref, ai_ref, lr_ref, li_ref, dar_ref, dai_ref):
        ar, ai = ar_ref[...], ai_ref[...]
        ar8 = jnp.broadcast_to(ar, (SCAN_CHUNKS, W))
        ai8 = jnp.broadcast_to(ai, (SCAN_CHUNKS, W))

        def local(t, c):
            cr, ci = c
            return ar8 * cr + ai8 * ci + dr_ref[t], ar8 * ci - ai8 * cr + di_ref[t]

        zero = jnp.zeros((SCAN_CHUNKS, W), _F32)
        er, ei = lax.fori_loop(0, T, lambda k, c: local(T - 1 - k, c), (zero, zero))
        pr, pi = _complex_power(ar, -ai, T)
        sr, si = _chunk_carries(er, ei, pr, pi, reverse=True)

        def grad_a(acc, nr, ni, xpr, xpi):
            return acc[0] + nr * xpr + ni * xpi, acc[1] + ni * xpr - nr * xpi

        def final(k, c):
            t = T - 1 - k
            nr, ni = local(t, c[:2])
            lr_ref[t] = nr
            li_ref[t] = ni
            gr, gi = grad_a(c[2:], nr, ni, xr_ref[t - 1], xi_ref[t - 1])
            return nr, ni, gr, gi

        cr, ci, gr, gi = lax.fori_loop(0, T - 1, final, (sr, si, zero, zero))
        nr, ni = local(0, (cr, ci))
        lr_ref[0] = nr
        li_ref[0] = ni
        first = _iota((SCAN_CHUNKS, W), 0) == 0
        xpr = jnp.where(first, 0.0, pltpu.roll(xr_ref[T - 1], 1, 0))
        xpi = jnp.where(first, 0.0, pltpu.roll(xi_ref[T - 1], 1, 0))
        gr, gi = grad_a((gr, gi), nr, ni, xpr, xpi)
        dar_ref[...] = jnp.sum(gr, axis=0, keepdims=True)
        dai_ref[...] = jnp.sum(gi, axis=0, keepdims=True)

    shape = _sds(dx_re.shape, _F32)
    row = _sds((1, N_STATE), _F32)
    return _call(body, (N_STATE // W,), [blk, blk, blk, blk, vec, vec], [blk, blk, vec, vec],
                 [shape, shape, row, row], "scan_bwd")(dx_re, dx_im, x_re, x_im, a_re, a_im)


_GELU_K = math.sqrt(2.0 / math.pi)
_GELU_C = 0.044715


def _gelu(y):
    return 0.5 * y * (1.0 + jnp.tanh(_GELU_K * (y + _GELU_C * y * y * y)))


def _gelu_grad(y):
    t = jnp.tanh(_GELU_K * (y + _GELU_C * y * y * y))
    return 0.5 * (1.0 + t) + 0.5 * y * (1.0 - t * t) * _GELU_K * (1.0 + 3.0 * _GELU_C * y * y)


def _ssm_out(x_re, x_im, u, ct_re, ct_im, d_row, w_glu, b_glu, g_ssm):
    L = u.shape[0]
    tm = _tile(L)

    def body(xr_ref, xi_ref, u_ref, cr_ref, ci_ref, d_ref, w_ref, b_ref, g_ref, y_ref, z_ref, n_ref):
        y = (_dot(xr_ref[...], cr_ref[...], _NT) - _dot(xi_ref[...], ci_ref[...], _NT)
             + d_ref[...] * u_ref[...])
        y_ref[...] = y
        z = _dot(_gelu(y), w_ref[...], _NN) + b_ref[...]
        z_ref[...] = z
        out = z[:, :SSM_WIDTH] * jax.nn.sigmoid(z[:, SSM_WIDTH:])
        n, _ = _rms_fwd(out, g_ref[...])
        n_ref[...] = n.astype(_BF16)

    return _call(body, (L // tm,),
                 [_rows(tm, N_STATE), _rows(tm, N_STATE), _rows(tm, SSM_WIDTH),
                  _whole((SSM_WIDTH, N_STATE)), _whole((SSM_WIDTH, N_STATE)), _whole((1, SSM_WIDTH)),
                  _whole((SSM_WIDTH, 2 * SSM_WIDTH)), _whole((1, 2 * SSM_WIDTH)), _whole((1, SSM_WIDTH))],
                 [_rows(tm, SSM_WIDTH), _rows(tm, 2 * SSM_WIDTH), _rows(tm, SSM_WIDTH)],
                 [_sds((L, SSM_WIDTH), _F32), _sds((L, 2 * SSM_WIDTH), _F32), _sds((L, SSM_WIDTH), _BF16)],
                 "ssm_out")(x_re, x_im, u, ct_re, ct_im, d_row, w_glu, b_glu, g_ssm)


def _ssm_out_bwd(dn, y, z, u, ct_re, ct_im, d_row, w_glu, g_ssm):
    L = u.shape[0]
    tm = _tile(L)

    def body(dn_ref, y_ref, z_ref, u_ref, cr_ref, ci_ref, d_ref, w_ref, g_ref,
             gy_ref, dz_ref, dy_ref, dud_ref, dxr_ref, dxi_ref, dg_ref, db_ref, dd_ref):
        first = pl.program_id(0) == 0
        z = z_ref[...]
        z1, z2 = z[:, :SSM_WIDTH], z[:, SSM_WIDTH:]
        sig = jax.nn.sigmoid(z2)
        out = z1 * sig
        g = g_ref[...]
        _, r = _rms_fwd(out, g)
        dout, dg = _rms_bwd(dn_ref[...], out, g, r)
        _accumulate(dg_ref, dg, first)
        dz = jnp.concatenate([dout * sig, dout * z1 * sig * (1.0 - sig)], axis=1)
        _accumulate(db_ref, jnp.sum(dz, axis=0, keepdims=True), first)
        dzb = dz.astype(_BF16)
        dz_ref[...] = dzb
        y = y_ref[...]
        gy_ref[...] = _gelu(y).astype(_BF16)
        dy = _dot(dzb, w_ref[...], _NT) * _gelu_grad(y)
        u = u_ref[...]
        _accumulate(dd_ref, jnp.sum(dy * u, axis=0, keepdims=True), first)
        dud_ref[...] = d_ref[...] * dy
        dyb = dy.astype(_BF16)
        dy_ref[...] = dyb
        dxr_ref[...] = _dot(dyb, cr_ref[...], _NN)
        dxi_ref[...] = -_dot(dyb, ci_ref[...], _NN)

    row = _whole((1, SSM_WIDTH))
    return _call(body, (L // tm,),
                 [_rows(tm, SSM_WIDTH), _rows(tm, SSM_WIDTH), _rows(tm, 2 * SSM_WIDTH), _rows(tm, SSM_WIDTH),
                  _whole((SSM_WIDTH, N_STATE)), _whole((SSM_WIDTH, N_STATE)), row,
                  _whole((SSM_WIDTH, 2 * SSM_WIDTH)), row],
                 [_rows(tm, SSM_WIDTH), _rows(tm, 2 * SSM_WIDTH), _rows(tm, SSM_WIDTH), _rows(tm, SSM_WIDTH),
                  _rows(tm, N_STATE), _rows(tm, N_STATE), row, _whole((1, 2 * SSM_WIDTH)), row],
                 [_sds((L, SSM_WIDTH), _BF16), _sds((L, 2 * SSM_WIDTH), _BF16), _sds((L, SSM_WIDTH), _BF16),
                  _sds((L, SSM_WIDTH), _F32), _sds((L, N_STATE), _F32), _sds((L, N_STATE), _F32),
                  _sds((1, SSM_WIDTH), _F32), _sds((1, 2 * SSM_WIDTH), _F32), _sds((1, SSM_WIDTH), _F32)],
                 "ssm_out_bwd")(dn, y, z, u, ct_re, ct_im, d_row, w_glu, g_ssm)


def _ssm_du(lam_re, lam_im, bt_re, bt_im, dud):
    L = dud.shape[0]
    tm = _tile(L)

    def body(lr_ref, li_ref, br_ref, bi_ref, dud_ref, du_ref):
        du_ref[...] = (_dot(lr_ref[...], br_ref[...], _NN) + _dot(li_ref[...], bi_ref[...], _NN)
                       + dud_ref[...])

    return _call(body, (L // tm,),
                 [_rows(tm, N_STATE), _rows(tm, N_STATE), _whole((N_STATE, SSM_WIDTH)),
                  _whole((N_STATE, SSM_WIDTH)), _rows(tm, SSM_WIDTH)],
                 _rows(tm, SSM_WIDTH), _sds((L, SSM_WIDTH), _F32), "ssm_du")(lam_re, lam_im, bt_re, bt_im, dud)


def _ssm_param_bwd(da_re_c, da_im_c, dbt_re, dbt_im, dct_re, dct_im,
                   lam_re_c, lam_im_c, ldt_c, b_re2, b_im2):
    def body(dar, dai, dbr, dbi, dcr, dci, lrc, lic, ldc, bre, bim,
             glr, gli, gdt, gbr, gbi, gcr, gci):
        e16, e64, mask_b, mask_c = _spread_masks()
        dbbr = _dot_exact(jnp.where(mask_b, dbr[...], 0.0), e16, _NT)
        dbbi = _dot_exact(jnp.where(mask_b, dbi[...], 0.0), e16, _NT)
        gcr[...] = _dot_exact(jnp.where(mask_c, dcr[...], 0.0), e64, _NT)
        gci[...] = -_dot_exact(jnp.where(mask_c, dci[...], 0.0), e64, _NT)
        _, vjp = jax.vjp(_s5_bbar, lrc[...], lic[...], ldc[...], bre[...], bim[...])
        d_lr, d_li, d_dt, d_br, d_bi = vjp((dar[...], dai[...], dbbr, dbbi))
        glr[...] = d_lr
        gli[...] = d_li
        gbr[...] = d_br
        gbi[...] = d_bi
        groups = (_iota((SSM_GROUPS, N_STATE), 1) >> 6) == _iota((SSM_GROUPS, N_STATE), 0)
        gdt[...] = _dot_exact(groups.astype(_F32), jnp.broadcast_to(d_dt, (N_STATE, 128)), _NN)

    col = (N_STATE, 1)
    ins = [da_re_c, da_im_c, dbt_re, dbt_im, dct_re, dct_im, lam_re_c, lam_im_c, ldt_c, b_re2, b_im2]
    outs = [col, col, (SSM_GROUPS, 128), (N_STATE, SSM_GROUP), (N_STATE, SSM_GROUP),
            (SSM_WIDTH, SSM_STATE), (SSM_WIDTH, SSM_STATE)]
    return _call(body, (1,), [_whole(a.shape) for a in ins], [_whole(s) for s in outs],
                 [_sds(s, _F32) for s in outs], "ssm_param_bwd")(*ins)


def _head_spread(j):
    r = _iota((KV_WIDTH, 256), 0)
    c = _iota((KV_WIDTH, 256), 1)
    return (r == HEAD_DIM * j + (c & (HEAD_DIM - 1))).astype(_BF16)


def _attn_scores(q_masked, kt, blk, sink):
    s = _dot(q_masked, kt, _NT) * (HEAD_DIM ** -0.5)
    qi = _iota((BLOCK, 2 * BLOCK), 0)
    kj = _iota((BLOCK, 2 * BLOCK), 1)
    rel = qi + BLOCK - kj
    valid = (rel >= 0) & (rel < BLOCK) & (blk * BLOCK - BLOCK + kj >= 0)
    s = jnp.where(valid, s, MASK_VALUE)
    m = jnp.maximum(jnp.max(s, axis=-1, keepdims=True), sink)
    p = jnp.exp(s - m)
    e_sink = jnp.exp(sink - m)
    den = jnp.sum(p, axis=-1, keepdims=True) + e_sink
    return p / den, e_sink / den


def _attn_specs():
    prev = lambda i: (jnp.maximum(i - 1, 0), 0)
    cur = lambda i: (i, 0)
    kv = [pl.BlockSpec((BLOCK, KV_WIDTH), prev), pl.BlockSpec((BLOCK, KV_WIDTH), cur)]
    return [pl.BlockSpec((BLOCK, ATTN_WIDTH), cur)] + kv + kv


def _attn_fwd(q, k, v, sinks, g_attn):
    L = q.shape[0]

    def body(q_ref, kp_ref, kc_ref, vp_ref, vc_ref, sink_ref, g_ref, o_ref, n_ref):
        blk = pl.program_id(0)
        kwin = jnp.concatenate([kp_ref[...], kc_ref[...]], axis=0)
        vwin = jnp.concatenate([vp_ref[...], vc_ref[...]], axis=0)
        lane_head = _iota((1, 256), 1) >> 6
        halves = []
        for j in range(N_KV_HEADS):
            spread = _head_spread(j)
            kt = _dot(kwin, spread, _NN).astype(_BF16)
            vt = _dot(vwin, spread, _NN).astype(_BF16)
            qj = q_ref[:, 256 * j:256 * (j + 1)]
            oj = jnp.zeros((BLOCK, 256), _F32)
            for g in range(Q_PER_KV):
                hm = lane_head == g
                p, _ = _attn_scores(jnp.where(hm, qj, jnp.zeros_like(qj)), kt, blk, sink_ref[Q_PER_KV * j + g])
                oj = oj + jnp.where(hm, _dot(p, vt, _NN), 0.0)
            halves.append(oj)
        o = jnp.concatenate(halves, axis=1)
        o_ref[...] = o
        n, _ = _rms_fwd(o, g_ref[...])
        n_ref[...] = n.astype(_BF16)

    cur = lambda i: (i, 0)
    return _call(body, (L // BLOCK,),
                 _attn_specs() + [pl.BlockSpec(memory_space=pltpu.SMEM), _whole((1, ATTN_WIDTH))],
                 [pl.BlockSpec((BLOCK, ATTN_WIDTH), cur)] * 2,
                 [_sds((L, ATTN_WIDTH), _F32), _sds((L, ATTN_WIDTH), _BF16)],
                 "attn_fwd")(q, k, k, v, v, sinks, g_attn)


def _attn_bwd(q, k, v, o, dn, sinks, g_attn):
    L = q.shape[0]

    def body(q_ref, kp_ref, kc_ref, vp_ref, vc_ref, o_ref, dn_ref, sink_ref, g_ref,
             dq_ref, dk_ref, dv_ref, dsink_ref, dg_ref):
        blk = pl.program_id(0)
        first = blk == 0

        @pl.when(first)
        def _():
            dk_ref[...] = jnp.zeros_like(dk_ref)
            dv_ref[...] = jnp.zeros_like(dv_ref)
            dsink_ref[...] = jnp.zeros_like(dsink_ref)

        o = o_ref[...]
        g = g_ref[...]
        _, r = _rms_fwd(o, g)
        do, dg = _rms_bwd(dn_ref[...], o, g, r)
        _accumulate(dg_ref, dg, first)
        kwin = jnp.concatenate([kp_ref[...], kc_ref[...]], axis=0)
        vwin = jnp.concatenate([vp_ref[...], vc_ref[...]], axis=0)
        lane_head = _iota((1, 256), 1) >> 6
        lane = _iota((1, 128), 1)
        dsink = jnp.zeros((1, 128), _F32)
        dkwin = jnp.zeros((2 * BLOCK, KV_WIDTH), _F32)
        dvwin = jnp.zeros((2 * BLOCK, KV_WIDTH), _F32)
        dq_halves = []
        for j in range(N_KV_HEADS):
            spread = _head_spread(j)
            kt = _dot(kwin, spread, _NN).astype(_BF16)
            vt = _dot(vwin, spread, _NN).astype(_BF16)
            qj = q_ref[:, 256 * j:256 * (j + 1)]
            doj = do[:, 256 * j:256 * (j + 1)]
            dqj = jnp.zeros((BLOCK, 256), _F32)
            dkt = jnp.zeros((2 * BLOCK, 256), _F32)
            dvt = jnp.zeros((2 * BLOCK, 256), _F32)
            for gi in range(Q_PER_KV):
                hm = lane_head == gi
                qm = jnp.where(hm, qj, jnp.zeros_like(qj))
                p, p_sink = _attn_scores(qm, kt, blk, sink_ref[Q_PER_KV * j + gi])
                dom = jnp.where(hm, doj, 0.0)
                dp = _dot(dom, vt, _NT)
                delta = jnp.sum(p * dp, axis=-1, keepdims=True)
                ds = p * (dp - delta) * (HEAD_DIM ** -0.5)
                dsink = dsink - jnp.where(lane == Q_PER_KV * j + gi, jnp.sum(p_sink * delta, axis=0, keepdims=True), 0.0)
                dvt = dvt + _dot(p, dom, _TN)
                dkt = dkt + _dot(ds, qm, _TN)
                dqj = dqj + jnp.where(hm, _dot(ds, kt, _NN), 0.0)
            dq_halves.append(dqj)
            spread_f = spread.astype(_F32)
            dkwin = dkwin + _dot_exact(dkt, spread_f, _NT)
            dvwin = dvwin + _dot_exact(dvt, spread_f, _NT)
        dq_ref[...] = jnp.concatenate(dq_halves, axis=1)
        dsink_ref[...] += dsink
        prev = pl.ds(pl.multiple_of(jnp.maximum(blk - 1, 0) * BLOCK, BLOCK), BLOCK)
        cur = pl.ds(pl.multiple_of(blk * BLOCK, BLOCK), BLOCK)
        dk_ref[prev, :] += dkwin[:BLOCK]
        dk_ref[cur, :] += dkwin[BLOCK:]
        dv_ref[prev, :] += dvwin[:BLOCK]
        dv_ref[cur, :] += dvwin[BLOCK:]

    cur = lambda i: (i, 0)
    blk_q = pl.BlockSpec((BLOCK, ATTN_WIDTH), cur)
    return _call(body, (L // BLOCK,),
                 _attn_specs() + [blk_q, blk_q, pl.BlockSpec(memory_space=pltpu.SMEM), _whole((1, ATTN_WIDTH))],
                 [blk_q, _whole((L, KV_WIDTH)), _whole((L, KV_WIDTH)), _whole((1, 128)), _whole((1, ATTN_WIDTH))],
                 [_sds((L, ATTN_WIDTH), _F32), _sds((L, KV_WIDTH), _F32), _sds((L, KV_WIDTH), _F32),
                  _sds((1, 128), _F32), _sds((1, ATTN_WIDTH), _F32)],
                 "attn_bwd")(q, k, k, v, v, o, dn, sinks, g_attn)


def _out_proj(n_ssm, n_attn, x, w_out, g_post_mix, g_pre_ffn):
    L = x.shape[0]
    tm = _tile(L)

    def body(ns_ref, na_ref, x_ref, w_ref, g1_ref, g2_ref, merged_ref, mo_ref, h1_ref, hn2_ref):
        merged = jnp.concatenate([ns_ref[...], na_ref[...]], axis=1)
        merged_ref[...] = merged
        mo = _dot(merged, w_ref[...], _NN)
        mo_ref[...] = mo
        n, _ = _rms_fwd(mo, g1_ref[...])
        h1 = x_ref[...] + n
        h1_ref[...] = h1
        hn2, _ = _rms_fwd(h1, g2_ref[...])
        hn2_ref[...] = hn2.astype(_BF16)

    row = _whole((1, D_MODEL))
    return _call(body, (L // tm,),
                 [_rows(tm, SSM_WIDTH), _rows(tm, ATTN_WIDTH), _rows(tm, D_MODEL), _whole((D_MODEL, D_MODEL)), row, row],
                 [_rows(tm, D_MODEL)] * 4,
                 [_sds((L, D_MODEL), _BF16), _sds((L, D_MODEL), _F32), _sds((L, D_MODEL), _F32), _sds((L, D_MODEL), _BF16)],
                 "out_proj")(n_ssm, n_attn, x, w_out, g_post_mix, g_pre_ffn)


def _ffn(hn2, h1, target, w_gate_up, w_down, g_pre_ffn, g_post_ffn):
    L = h1.shape[0]
    tm = _tile(L)
    half = D_FF // 2

    def body(hn2_ref, h1_ref, tgt_ref, wgu_hbm, wd_hbm, g2_ref, g3_ref,
             act_ref, dgu_ref, dff_ref, dh1_ref, loss_ref, dg3_ref, dg2_ref,
             wgu, wd, gu, sem):
        first = pl.program_id(0) == 0

        @pl.when(first)
        def _():
            c1 = pltpu.make_async_copy(wgu_hbm, wgu, sem.at[0])
            c2 = pltpu.make_async_copy(wd_hbm, wd, sem.at[1])
            c1.start()
            c2.start()
            c1.wait()
            c2.wait()

        hn2 = hn2_ref[...]
        ff = jnp.zeros((tm, D_MODEL), _F32)
        for c in range(2):
            gate = _dot(hn2, wgu[:, half * c:half * (c + 1)], _NN)
            up = _dot(hn2, wgu[:, D_FF + half * c:D_FF + half * (c + 1)], _NN)
            gu[:, half * c:half * (c + 1)] = gate
            gu[:, D_FF + half * c:D_FF + half * (c + 1)] = up
            act = (gate * jax.nn.sigmoid(gate) * up).astype(_BF16)
            act_ref[:, half * c:half * (c + 1)] = act
            ff = ff + _dot(act, wd[half * c:half * (c + 1), :], _NN)
        g3 = g3_ref[...]
        n, r = _rms_fwd(ff, g3)
        h1 = h1_ref[...]
        err = h1 + n - tgt_ref[...]
        loss = 0.5 * jnp.sum(jnp.mean(err * err, axis=-1, keepdims=True), axis=0, keepdims=True)
        _accumulate(loss_ref, jnp.broadcast_to(loss, (1, 128)), first)
        dh2 = err * (1.0 / D_MODEL)
        dff, dg3 = _rms_bwd(dh2, ff, g3, r)
        _accumulate(dg3_ref, dg3, first)
        dffb = dff.astype(_BF16)
        dff_ref[...] = dffb
        dhn2 = jnp.zeros((tm, D_MODEL), _F32)
        for c in range(2):
            dact = _dot(dffb, wd[half * c:half * (c + 1), :], _NT)
            gate = gu[:, half * c:half * (c + 1)]
            up = gu[:, D_FF + half * c:D_FF + half * (c + 1)]
            sig = jax.nn.sigmoid(gate)
            silu = gate * sig
            dgate = (dact * up * (sig + silu * (1.0 - sig))).astype(_BF16)
            dup = (dact * silu).astype(_BF16)
            dgu_ref[:, half * c:half * (c + 1)] = dgate
            dgu_ref[:, D_FF + half * c:D_FF + half * (c + 1)] = dup
            dhn2 = dhn2 + _dot(dgate, wgu[:, half * c:half * (c + 1)], _NT)
            dhn2 = dhn2 + _dot(dup, wgu[:, D_FF + half * c:D_FF + half * (c + 1)], _NT)
        g2 = g2_ref[...]
        _, r2 = _rms_fwd(h1, g2)
        dh1, dg2 = _rms_bwd(dhn2, h1, g2, r2)
        _accumulate(dg2_ref, dg2, first)
        dh1_ref[...] = dh2 + dh1

    row = _whole((1, D_MODEL))
    anyspace = pl.BlockSpec(memory_space=pl.ANY)
    return _call(body, (L // tm,),
                 [_rows(tm, D_MODEL), _rows(tm, D_MODEL), _rows(tm, D_MODEL), anyspace, anyspace, row, row],
                 [_rows(tm, D_FF), _rows(tm, 2 * D_FF), _rows(tm, D_MODEL), _rows(tm, D_MODEL),
                  _whole((1, 128)), row, row],
                 [_sds((L, D_FF), _BF16), _sds((L, 2 * D_FF), _BF16), _sds((L, D_MODEL), _BF16),
                  _sds((L, D_MODEL), _F32), _sds((1, 128), _F32), _sds((1, D_MODEL), _F32), _sds((1, D_MODEL), _F32)],
                 "ffn",
                 scratch=[pltpu.VMEM((D_MODEL, 2 * D_FF), _BF16), pltpu.VMEM((D_FF, D_MODEL), _BF16),
                          pltpu.VMEM((tm, 2 * D_FF), _F32), pltpu.SemaphoreType.DMA((2,))],
                 )(hn2, h1, target, w_gate_up, w_down, g_pre_ffn, g_post_ffn)


def _out_proj_bwd(dh1, mo, w_out, g_post_mix):
    L = dh1.shape[0]
    tm = _tile(L)

    def body(dh1_ref, mo_ref, w_ref, g_ref, dmo_ref, dns_ref, dna_ref, dg_ref):
        first = pl.program_id(0) == 0
        mo = mo_ref[...]
        g = g_ref[...]
        _, r = _rms_fwd(mo, g)
        dmo, dg = _rms_bwd(dh1_ref[...], mo, g, r)
        _accumulate(dg_ref, dg, first)
        dmob = dmo.astype(_BF16)
        dmo_ref[...] = dmob
        dmerged = _dot(dmob, w_ref[...], _NT)
        dns_ref[...] = dmerged[:, :SSM_WIDTH]
        dna_ref[...] = dmerged[:, SSM_WIDTH:]

    row = _whole((1, D_MODEL))
    return _call(body, (L // tm,),
                 [_rows(tm, D_MODEL), _rows(tm, D_MODEL), _whole((D_MODEL, D_MODEL)), row],
                 [_rows(tm, D_MODEL), _rows(tm, SSM_WIDTH), _rows(tm, ATTN_WIDTH), row],
                 [_sds((L, D_MODEL), _BF16), _sds((L, SSM_WIDTH), _F32), _sds((L, ATTN_WIDTH), _F32),
                  _sds((1, D_MODEL), _F32)],
                 "out_proj_bwd")(dh1, mo, w_out, g_post_mix)


def _in_proj_bwd(du, dq, dk, dv, cos_t, sin_t, x, dh1, g_pre_mix, w_in):
    L = x.shape[0]
    tm = _tile(L)

    def body(du_ref, dq_ref, dk_ref, dv_ref, cos_ref, sin_ref, x_ref, dh1_ref, g_ref, w_ref,
             dproj_ref, dx_ref, dg_ref):
        first = pl.program_id(0) == 0
        cos_v, sin_v = cos_ref[...], sin_ref[...]
        dproj = jnp.concatenate([du_ref[...], _rope_transpose(dq_ref[...], cos_v, sin_v),
                                 _rope_transpose(dk_ref[...], cos_v, sin_v), dv_ref[...]], axis=1).astype(_BF16)
        dproj_ref[...] = dproj
        dhn = _dot(dproj, w_ref[...], _NT)
        x = x_ref[...]
        g = g_ref[...]
        _, r = _rms_fwd(x, g)
        dx, dg = _rms_bwd(dhn, x, g, r)
        _accumulate(dg_ref, dg, first)
        dx_ref[...] = dh1_ref[...] + dx

    row = _whole((1, D_MODEL))
    return _call(body, (L // tm,),
                 [_rows(tm, SSM_WIDTH), _rows(tm, ATTN_WIDTH), _rows(tm, KV_WIDTH), _rows(tm, KV_WIDTH),
                  _rows(tm, KV_WIDTH), _rows(tm, KV_WIDTH), _rows(tm, D_MODEL), _rows(tm, D_MODEL), row,
                  _whole((D_MODEL, IN_WIDTH))],
                 [_rows(tm, IN_WIDTH), _rows(tm, D_MODEL), row],
                 [_sds((L, IN_WIDTH), _BF16), _sds((L, D_MODEL), _F32), _sds((1, D_MODEL), _F32)],
                 "in_proj_bwd")(du, dq, dk, dv, cos_t, sin_t, x, dh1, g_pre_mix, w_in)


def _matmul_tn(a, b, out_dtype, name, scale=1.0):
    K, M = a.shape
    N = b.shape[1]
    tm = next(t for t in (512, 256, 128) if M % t == 0)
    tn = next(t for t in (512, 256, 128) if N % t == 0)

    def body(a_ref, b_ref, o_ref):
        acc = _dot(a_ref[...], b_ref[...], _TN)
        o_ref[...] = (acc if scale == 1.0 else acc * scale).astype(out_dtype)

    params = pltpu.CompilerParams(dimension_semantics=("arbitrary", "arbitrary"), vmem_limit_bytes=VMEM_LIMIT)
    return pl.pallas_call(body, grid=(M // tm, N // tn),
                          in_specs=[pl.BlockSpec((K, tm), lambda i, j: (0, i)),
                                    pl.BlockSpec((K, tn), lambda i, j: (0, j))],
                          out_specs=pl.BlockSpec((tm, tn), lambda i, j: (i, j)),
                          out_shape=_sds((M, N), out_dtype), compiler_params=params, name=name)(a, b)


def _to_chunked(a):
    L, n = a.shape
    return a.reshape(SCAN_CHUNKS, L // SCAN_CHUNKS, n).transpose(1, 0, 2).reshape(L, n)


def _from_chunked(a):
    L, n = a.shape
    return a.reshape(L // SCAN_CHUNKS, SCAN_CHUNKS, n).transpose(1, 0, 2).reshape(L, n)


def _local_step(x, pos, target, p):
    L = x.shape[0]
    T = L // SCAN_CHUNKS
    cos_t, sin_t = _rope_tables(pos.reshape(L, 1))
    hn, u, q, k, v = _in_proj(x, p["g_pre_mix"], p["w_in"], cos_t, sin_t)

    lam_re_r = p["ssm_lambda_re"].reshape(1, N_STATE)
    lam_im_r = p["ssm_lambda_im"].reshape(1, N_STATE)
    ldt_r = jnp.broadcast_to(p["ssm_log_dt"].reshape(SSM_GROUPS, 1), (SSM_GROUPS, SSM_STATE)).reshape(1, N_STATE)
    lam_re_c, lam_im_c, ldt_c = (a.reshape(N_STATE, 1) for a in (lam_re_r, lam_im_r, ldt_r))
    b_re2 = p["ssm_b_re"].reshape(N_STATE, SSM_GROUP)
    b_im2 = p["ssm_b_im"].reshape(N_STATE, SSM_GROUP)
    c_re2 = p["ssm_c_re"].reshape(SSM_WIDTH, SSM_STATE)
    c_im2 = p["ssm_c_im"].reshape(SSM_WIDTH, SSM_STATE)
    d_row = p["ssm_d"].reshape(1, SSM_WIDTH)
    a_re, a_im, bt_re, bt_im, ct_re, ct_im = _ssm_prep(
        lam_re_r, lam_im_r, ldt_r, lam_re_c, lam_im_c, ldt_c, b_re2, b_im2, c_re2, c_im2)

    u_c = _to_chunked(u)
    bu_re, bu_im = _ssm_bu(u_c, bt_re, bt_im)
    x_re, x_im = _scan_fwd(bu_re.reshape(T, SCAN_CHUNKS, N_STATE), bu_im.reshape(T, SCAN_CHUNKS, N_STATE), a_re, a_im)
    y, z, n_ssm_c = _ssm_out(x_re.reshape(L, N_STATE), x_im.reshape(L, N_STATE), u_c, ct_re, ct_im, d_row,
                             p["w_glu"], p["b_glu"], p["g_ssm_out"])
    n_ssm = _from_chunked(n_ssm_c)

    sinks = p["attn_sinks"].reshape(N_Q_HEADS)
    o, n_attn = _attn_fwd(q, k, v, sinks, p["g_attn_out"])
    merged, mo, h1, hn2 = _out_proj(n_ssm, n_attn, x, p["w_out"], p["g_post_mix"], p["g_pre_ffn"])
    act, dgu, dff, dh1, loss, dg_post_ffn, dg_pre_ffn = _ffn(
        hn2, h1, target, p["w_gate_up"], p["w_down"], p["g_pre_ffn"], p["g_post_ffn"])
    grads = {"g_post_ffn": dg_post_ffn, "g_pre_ffn": dg_pre_ffn}
    grads["w_down"] = _matmul_tn(act, dff, _BF16, "grad_w_down")
    grads["w_gate_up"] = _matmul_tn(hn2, dgu, _BF16, "grad_w_gate_up")

    dmo, dn_ssm, dn_attn, grads["g_post_mix"] = _out_proj_bwd(dh1, mo, p["w_out"], p["g_post_mix"])
    grads["w_out"] = _matmul_tn(merged, dmo, _BF16, "grad_w_out")

    dq, dk, dv, dsink, grads["g_attn_out"] = _attn_bwd(q, k, v, o, dn_attn, sinks, p["g_attn_out"])
    grads["attn_sinks"] = dsink[:, :N_Q_HEADS]

    gy, dz, dy, dud, dx_re, dx_im, grads["g_ssm_out"], grads["b_glu"], dd = _ssm_out_bwd(
        _to_chunked(dn_ssm), y, z, u_c, ct_re, ct_im, d_row, p["w_glu"], p["g_ssm_out"])
    grads["ssm_d"] = dd.reshape(1, SSM_GROUPS, SSM_GROUP)
    grads["w_glu"] = _matmul_tn(gy, dz, _BF16, "grad_w_glu")
    lam_re, lam_im, da_re, da_im = _scan_bwd(dx_re.reshape(T, SCAN_CHUNKS, N_STATE), dx_im.reshape(T, SCAN_CHUNKS, N_STATE),
                                             x_re, x_im, a_re, a_im)
    lam_re = lam_re.reshape(L, N_STATE)
    lam_im = lam_im.reshape(L, N_STATE)
    dct_re = _matmul_tn(dy, x_re.reshape(L, N_STATE), _F32, "grad_ct_re")
    dct_im = _matmul_tn(dy, x_im.reshape(L, N_STATE), _F32, "grad_ct_im")
    dbt_re = _matmul_tn(lam_re, u_c, _F32, "grad_bt_re")
    dbt_im = _matmul_tn(lam_im, u_c, _F32, "grad_bt_im")
    g_lr, g_li, g_dt, g_br, g_bi, g_cr, g_ci = _ssm_param_bwd(
        da_re.reshape(N_STATE, 1), da_im.reshape(N_STATE, 1), dbt_re, dbt_im, dct_re, dct_im,
        lam_re_c, lam_im_c, ldt_c, b_re2, b_im2)
    grads["ssm_lambda_re"] = g_lr.reshape(1, SSM_GROUPS, SSM_STATE)
    grads["ssm_lambda_im"] = g_li.reshape(1, SSM_GROUPS, SSM_STATE)
    grads["ssm_log_dt"] = g_dt[:, 0].reshape(1, SSM_GROUPS)
    grads["ssm_b_re"] = g_br.reshape(1, SSM_GROUPS, SSM_STATE, SSM_GROUP)
    grads["ssm_b_im"] = g_bi.reshape(1, SSM_GROUPS, SSM_STATE, SSM_GROUP)
    grads["ssm_c_re"] = g_cr.reshape(1, SSM_GROUPS, SSM_GROUP, SSM_STATE)
    grads["ssm_c_im"] = g_ci.reshape(1, SSM_GROUPS, SSM_GROUP, SSM_STATE)

    du = _from_chunked(_ssm_du(lam_re, lam_im, bt_re, bt_im, dud))
    dproj, grad_x, grads["g_pre_mix"] = _in_proj_bwd(du, dq, dk, dv, cos_t, sin_t, x, dh1, p["g_pre_mix"], p["w_in"])
    grads["w_in"] = _matmul_tn(hn, dproj, _BF16, "grad_w_in")
    return loss[0, 0], grad_x, grads


_MESH = pl.DeviceIdType.MESH
_PEERS = N_DEV - 1


def _mesh_pos():
    return lax.axis_index("x"), lax.axis_index("y"), lax.axis_index("c")


def _dev_index(px, py, pc):
    return 4 * px + 2 * py + pc


def _all_gather(shards, out_dtype, name):
    n = len(shards)

    def body(*refs):
        ins, outs, stages = refs[:n], refs[n:2 * n], refs[2 * n:3 * n]
        send_sems, recv_sems, local_sems = refs[3 * n:]
        x, y, c = _mesh_pos()
        me, sibling = (x, y, c), (x, y, 1 - c)
        chips = [(1 - x, y), (x, 1 - y), (1 - x, 1 - y)]

        def copy(w, k, block, to, src=None):
            slot = outs[w].at[_dev_index(*block)]
            return pltpu.make_async_remote_copy(
                src_ref=slot if src is None else src, dst_ref=slot,
                send_sem=send_sems.at[_PEERS * w + k], recv_sem=recv_sems.at[_PEERS * w + k],
                device_id=to, device_id_type=_MESH)

        for w in range(n):
            stages[w][...] = ins[w][...].astype(out_dtype)
        mine, first, passed = [], [], []
        for w in range(n):
            cp = pltpu.make_async_copy(stages[w], outs[w].at[_dev_index(*me)], local_sems.at[w])
            cp.start()
            mine.append(cp)
            sends = [copy(w, 0, me, sibling, src=stages[w])]
            sends += [copy(w, 1 + j, me, (*chip, c), src=stages[w]) for j, chip in enumerate(chips)]
            for cp in sends:
                cp.start()
            first += sends
        for w in range(n):
            for j, chip in enumerate(chips):
                copy(w, 1 + j, (*chip, c), me).wait_recv()
                cp = copy(w, 4 + j, (*chip, c), sibling)
                cp.start()
                passed.append(cp)
        for w in range(n):
            copy(w, 0, sibling, me).wait_recv()
            for j, chip in enumerate(chips):
                copy(w, 4 + j, (*chip, 1 - c), me).wait_recv()
        for cp in first + passed:
            cp.wait_send()
        for cp in mine:
            cp.wait()

    return pl.pallas_call(
        body, name=name,
        out_shape=[_sds((N_DEV,) + s.shape, out_dtype) for s in shards],
        in_specs=[pl.BlockSpec(memory_space=pltpu.VMEM)] * n,
        out_specs=[pl.BlockSpec(memory_space=pl.ANY)] * n,
        scratch_shapes=[pltpu.VMEM(s.shape, out_dtype) for s in shards]
        + [pltpu.SemaphoreType.DMA((_PEERS * n,)), pltpu.SemaphoreType.DMA((_PEERS * n,)),
           pltpu.SemaphoreType.DMA((n,))],
        compiler_params=pltpu.CompilerParams(vmem_limit_bytes=VMEM_LIMIT),
    )(*shards)


def _exchange(blocks, name):
    n = len(blocks)

    def body(*refs):
        ins, outs = refs[:n], refs[n:2 * n]
        send_sems, recv_sems, local_sems = refs[2 * n:]
        x, y, c = _mesh_pos()
        me = _dev_index(x, y, c)
        local, sends, recvs = [], [], []
        for w in range(n):
            cp = pltpu.make_async_copy(ins[w].at[me], outs[w].at[me], local_sems.at[w])
            cp.start()
            local.append(cp)
            for r in range(1, N_DEV):
                peer = (x ^ ((r >> 2) & 1), y ^ ((r >> 1) & 1), c ^ (r & 1))
                peer_idx = _dev_index(*peer)
                k = _PEERS * w + r - 1
                send = pltpu.make_async_remote_copy(
                    src_ref=ins[w].at[peer_idx], dst_ref=outs[w].at[me],
                    send_sem=send_sems.at[k], recv_sem=recv_sems.at[k], device_id=peer, device_id_type=_MESH)
                send.start()
                sends.append(send)
                recvs.append(pltpu.make_async_remote_copy(
                    src_ref=ins[w].at[me], dst_ref=outs[w].at[peer_idx],
                    send_sem=send_sems.at[k], recv_sem=recv_sems.at[k], device_id=peer, device_id_type=_MESH))
        for cp in recvs:
            cp.wait_recv()
        for cp in sends:
            cp.wait_send()
        for cp in local:
            cp.wait()

    anyspace = pl.BlockSpec(memory_space=pl.ANY)
    return pl.pallas_call(
        body, name=name,
        out_shape=[_sds(b.shape, b.dtype) for b in blocks],
        in_specs=[anyspace] * n, out_specs=[anyspace] * n,
        scratch_shapes=[pltpu.SemaphoreType.DMA((_PEERS * n,)), pltpu.SemaphoreType.DMA((_PEERS * n,)),
                        pltpu.SemaphoreType.DMA((n,))],
    )(*blocks)


def _row_tile(rows):
    return next(t for t in range(min(rows, 256), 0, -16) if rows % t == 0)


def _adamw(parts, w, m, v, name):
    rows, cols = w.shape
    tr = _row_tile(rows)

    def body(p_ref, w_ref, m_ref, v_ref, g_ref, d_ref, nm_ref, nv_ref):
        g = p_ref[0].astype(_F32)
        for s in range(1, N_DEV):
            g = g + p_ref[s].astype(_F32)
        new_m = ADAM_B1 * m_ref[...] + (1.0 - ADAM_B1) * g
        new_v = ADAM_B2 * v_ref[...] + (1.0 - ADAM_B2) * (g * g)
        m_hat = new_m / (1.0 - ADAM_B1 ** ADAM_STEP)
        v_hat = new_v / (1.0 - ADAM_B2 ** ADAM_STEP)
        g_ref[...] = g
        d_ref[...] = -ADAM_LR * (m_hat / (jnp.sqrt(v_hat) + ADAM_EPS) + ADAM_WD * w_ref[...])
        nm_ref[...] = new_m
        nv_ref[...] = new_v

    blk = _rows(tr, cols)
    return _call(body, (rows // tr,),
                 [pl.BlockSpec((N_DEV, tr, cols), lambda i: (0, i, 0)), blk, blk, blk],
                 [blk] * 4, [_sds((rows, cols), _F32)] * 4, name)(parts, w, m, v)


_SMALL = ("g_pre_mix", "ssm_lambda_re", "ssm_lambda_im", "ssm_log_dt", "ssm_b_re", "ssm_b_im",
          "ssm_c_re", "ssm_c_im", "ssm_d", "b_glu", "attn_sinks", "g_ssm_out", "g_attn_out",
          "g_post_mix", "g_pre_ffn", "g_post_ffn")
_BIG = ("w_in", "w_glu", "w_out", "w_gate_up", "w_down")
_WEIGHTS = ("g_pre_mix", "w_in", "ssm_lambda_re", "ssm_lambda_im", "ssm_log_dt", "ssm_b_re", "ssm_b_im",
            "ssm_c_re", "ssm_c_im", "ssm_d", "w_glu", "b_glu", "attn_sinks", "g_ssm_out", "g_attn_out",
            "w_out", "g_post_mix", "g_pre_ffn", "w_gate_up", "w_down", "g_post_ffn")
_LANES = 128


def _pack(arrays, extra_rows):
    rows = []
    for a in arrays:
        flat = a.reshape(-1).astype(_F32)
        pad = (-flat.shape[0]) % _LANES
        rows.append(jnp.pad(flat, (0, pad)).reshape(-1, _LANES))
    packed = jnp.concatenate(rows + [jnp.zeros((extra_rows, _LANES), _F32)], axis=0)
    return jnp.pad(packed, ((0, (-packed.shape[0]) % 16), (0, 0)))


def _unpack(packed, like):
    out, row = [], 0
    for a in like:
        size = int(np.prod(a.shape))
        nrows = -(-size // _LANES)
        out.append(packed[row:row + nrows].reshape(-1)[:size].reshape(a.shape))
        row += nrows
    return out, row


def _to_blocks(a):
    r, c = a.shape
    return a.reshape(r, N_DEV, c // N_DEV).transpose(1, 0, 2)


def _from_blocks(a):
    n, r, c = a.shape
    return a.transpose(1, 0, 2).reshape(r, n * c)


def kernel(x, positions, g_pre_mix, w_in, ssm_lambda_re, ssm_lambda_im, ssm_log_dt, ssm_b_re, ssm_b_im, ssm_c_re, ssm_c_im, ssm_d, w_glu, b_glu, attn_sinks, g_ssm_out, g_attn_out, w_out, g_post_mix, g_pre_ffn, w_gate_up, w_down, g_post_ffn, loss_target, m_g_pre_mix, m_w_in, m_ssm_lambda_re, m_ssm_lambda_im, m_ssm_log_dt, m_ssm_b_re, m_ssm_b_im, m_ssm_c_re, m_ssm_c_im, m_ssm_d, m_w_glu, m_b_glu, m_attn_sinks, m_g_ssm_out, m_g_attn_out, m_w_out, m_g_post_mix, m_g_pre_ffn, m_w_gate_up, m_w_down, m_g_post_ffn, v_g_pre_mix, v_w_in, v_ssm_lambda_re, v_ssm_lambda_im, v_ssm_log_dt, v_ssm_b_re, v_ssm_b_im, v_ssm_c_re, v_ssm_c_im, v_ssm_d, v_w_glu, v_b_glu, v_attn_sinks, v_g_ssm_out, v_g_attn_out, v_w_out, v_g_post_mix, v_g_pre_ffn, v_w_gate_up, v_w_down, v_g_post_ffn):
    w = dict(g_pre_mix=g_pre_mix, w_in=w_in, ssm_lambda_re=ssm_lambda_re, ssm_lambda_im=ssm_lambda_im,
             ssm_log_dt=ssm_log_dt, ssm_b_re=ssm_b_re, ssm_b_im=ssm_b_im, ssm_c_re=ssm_c_re, ssm_c_im=ssm_c_im,
             ssm_d=ssm_d, w_glu=w_glu, b_glu=b_glu, attn_sinks=attn_sinks, g_ssm_out=g_ssm_out,
             g_attn_out=g_attn_out, w_out=w_out, g_post_mix=g_post_mix, g_pre_ffn=g_pre_ffn,
             w_gate_up=w_gate_up, w_down=w_down, g_post_ffn=g_post_ffn)
    m = dict(g_pre_mix=m_g_pre_mix, w_in=m_w_in, ssm_lambda_re=m_ssm_lambda_re, ssm_lambda_im=m_ssm_lambda_im,
             ssm_log_dt=m_ssm_log_dt, ssm_b_re=m_ssm_b_re, ssm_b_im=m_ssm_b_im, ssm_c_re=m_ssm_c_re,
             ssm_c_im=m_ssm_c_im, ssm_d=m_ssm_d, w_glu=m_w_glu, b_glu=m_b_glu, attn_sinks=m_attn_sinks,
             g_ssm_out=m_g_ssm_out, g_attn_out=m_g_attn_out, w_out=m_w_out, g_post_mix=m_g_post_mix,
             g_pre_ffn=m_g_pre_ffn, w_gate_up=m_w_gate_up, w_down=m_w_down, g_post_ffn=m_g_post_ffn)
    v = dict(g_pre_mix=v_g_pre_mix, w_in=v_w_in, ssm_lambda_re=v_ssm_lambda_re, ssm_lambda_im=v_ssm_lambda_im,
             ssm_log_dt=v_ssm_log_dt, ssm_b_re=v_ssm_b_re, ssm_b_im=v_ssm_b_im, ssm_c_re=v_ssm_c_re,
             ssm_c_im=v_ssm_c_im, ssm_d=v_ssm_d, w_glu=v_w_glu, b_glu=v_b_glu, attn_sinks=v_attn_sinks,
             g_ssm_out=v_g_ssm_out, g_attn_out=v_g_attn_out, w_out=v_w_out, g_post_mix=v_g_post_mix,
             g_pre_ffn=v_g_pre_ffn, w_gate_up=v_w_gate_up, w_down=v_w_down, g_post_ffn=v_g_post_ffn)

    g_in, g_glu, g_out, g_gu, g_down = _all_gather([w[n][0] for n in _BIG], _BF16, "gather_weights")
    p = {n: w[n] for n in _SMALL}
    p["w_in"] = _from_blocks(g_in)
    p["w_glu"] = _from_blocks(g_glu)
    p["w_out"] = g_out.reshape(D_MODEL, D_MODEL)
    p["w_gate_up"] = _from_blocks(g_gu)
    p["w_down"] = g_down.reshape(D_FF, D_MODEL)

    loss, grad_x, grads = _local_step(x[0], positions[0], loss_target[0], p)

    blocks = [_to_blocks(grads["w_in"]), _to_blocks(grads["w_glu"]),
              grads["w_out"].reshape(N_DEV, D_MODEL // N_DEV, D_MODEL), _to_blocks(grads["w_gate_up"]),
              grads["w_down"].reshape(N_DEV, D_FF // N_DEV, D_MODEL)]
    parts = _exchange(blocks, "exchange_grads")
    result = {}
    for name, part in zip(_BIG, parts):
        result[name] = [a[None] for a in _adamw(part, w[name][0], m[name][0], v[name][0], "adamw_" + name)]

    small_grads = _pack([grads[n] for n in _SMALL] + [jnp.pad(loss.reshape(1), (0, _LANES - 1))], 0)
    gathered, = _all_gather([small_grads], _F32, "gather_small")
    packed = _adamw(gathered, _pack([w[n] for n in _SMALL], 1), _pack([m[n] for n in _SMALL], 1),
                    _pack([v[n] for n in _SMALL], 1), "adamw_small")
    loss_row = None
    for name in _SMALL:
        result[name] = []
    for arr in packed:
        vals, loss_row = _unpack(arr, [w[n] for n in _SMALL])
        for name, val in zip(_SMALL, vals):
            result[name].append(val)
    total_loss = packed[0][loss_row, 0]

    out = [total_loss, grad_x[None]]
    for kind in range(4):
        out += [result[n][kind] for n in _WEIGHTS]
    return tuple(out)
```

```python
import functools
import math

import numpy as np
import jax
import jax.numpy as jnp
from jax import lax
from jax.experimental import pallas as pl
from jax.experimental.pallas import tpu as pltpu

D_MODEL = 1024
SSM_WIDTH = 512
SSM_GROUP = 16
SSM_GROUPS = 32
SSM_STATE = 64
N_STATE = SSM_GROUPS * SSM_STATE
ATTN_WIDTH = 512
HEAD_DIM = 64
N_Q_HEADS = 8
N_KV_HEADS = 2
Q_PER_KV = 4
KV_WIDTH = 128
IN_WIDTH = 1280
BLOCK = 128
ROPE_DIM = 16
ROPE_THETA = 500000.0
D_FF = 2816
NORM_EPS = 1e-6
MASK_VALUE = -1e30
ADAM_LR = 0.001
ADAM_B1 = 0.9
ADAM_B2 = 0.999
ADAM_EPS = 1e-08
ADAM_WD = 0.01
ADAM_STEP = 10

N_DEV = 8
SCAN_CHUNKS = 8
SCAN_COLS = 512
TOKEN_TILE = 256
VMEM_LIMIT = 56 * 1024 * 1024

_F32 = jnp.float32
_BF16 = jnp.bfloat16
_MXU = jnp.bfloat16

_NN = ((1,), (0,))
_NT = ((1,), (1,))
_TN = ((0,), (0,))


def _dot(a, b, dims):
    return lax.dot_general(a.astype(_MXU), b.astype(_MXU), (dims, ((), ())),
                           preferred_element_type=_F32)


def _dot_exact(a, b, dims):
    return lax.dot_general(a.astype(_F32), b.astype(_F32), (dims, ((), ())),
                           precision=lax.Precision.HIGHEST, preferred_element_type=_F32)


def _iota(shape, dim):
    return lax.broadcasted_iota(jnp.int32, shape, dim)


def _rms_fwd(x, g):
    r = lax.rsqrt(jnp.mean(x * x, axis=-1, keepdims=True) + NORM_EPS)
    return x * r * g, r


def _rms_bwd(dy, x, g, r):
    a = dy * g
    xn = x * r
    dx = r * (a - xn * jnp.mean(a * xn, axis=-1, keepdims=True))
    dg = jnp.sum(dy * xn, axis=0, keepdims=True)
    return dx, dg


def _call(body, grid, in_specs, out_specs, out_shape, name, scratch=(), tokens=()):
    params = pltpu.CompilerParams(dimension_semantics=("arbitrary",) * len(grid),
                                  vmem_limit_bytes=VMEM_LIMIT)
    n_in, n_tok = len(in_specs), len(tokens)

    def run(*refs):
        return body(*refs[:n_in], *refs[n_in + n_tok:])

    call = pl.pallas_call(run, grid=grid,
                          in_specs=list(in_specs) + [pl.BlockSpec(memory_space=pl.ANY)] * n_tok,
                          out_specs=out_specs, out_shape=out_shape, scratch_shapes=list(scratch),
                          compiler_params=params, name=name)
    return lambda *args: call(*args, *tokens)


def _rows(tm, n):
    return pl.BlockSpec((tm, n), lambda i: (i, 0))


def _whole(shape):
    nd = len(shape)
    return pl.BlockSpec(shape, lambda i: (0,) * nd)


def _sds(shape, dtype):
    return jax.ShapeDtypeStruct(shape, dtype)


def _tile(L):
    return min(TOKEN_TILE, L)


def _accumulate(ref, val, first):
    @pl.when(first)
    def _():
        ref[...] = val

    @pl.when(jnp.logical_not(first))
    def _():
        ref[...] += val


def _rope_rows():
    half = ROPE_DIM // 2
    inv = (np.float32(ROPE_THETA) ** (-np.arange(half, dtype=np.float32) * np.float32(2.0) / np.float32(ROPE_DIM))).astype(np.float32)
    col = np.arange(KV_WIDTH) % HEAD_DIM
    freq = np.where(col < ROPE_DIM, inv[col % half], 0.0).astype(np.float32)
    sign = np.where(col < half, -1.0, np.where(col < ROPE_DIM, 1.0, 0.0)).astype(np.float32)
    return freq[None, :], sign[None, :]


def _rope_tables(pos_col):
    L = pos_col.shape[0]
    tm = _tile(L)
    freq, sign = _rope_rows()

    def body(pos_ref, freq_ref, sign_ref, cos_ref, sin_ref):
        ang = pos_ref[...].astype(_F32) * freq_ref[...]
        cos_ref[...] = jnp.cos(ang)
        sin_ref[...] = jnp.sin(ang) * sign_ref[...]

    return _call(body, (L // tm,),
                 [_rows(tm, 1), _whole((1, KV_WIDTH)), _whole((1, KV_WIDTH))],
                 [_rows(tm, KV_WIDTH), _rows(tm, KV_WIDTH)],
                 [_sds((L, KV_WIDTH), _F32)] * 2, "rope_tables")(pos_col, jnp.asarray(freq), jnp.asarray(sign))


def _widen(t, width):
    return t if width == KV_WIDTH else jnp.concatenate([t] * (width // KV_WIDTH), axis=1)


def _rope_partner(t):
    w = t.shape[1]
    in_head = _iota((1, w), 1) & (HEAD_DIM - 1)
    second = jnp.where(in_head < ROPE_DIM, pltpu.roll(t, ROPE_DIM // 2, 1), 0.0)
    return jnp.where(in_head < ROPE_DIM // 2, pltpu.roll(t, w - ROPE_DIM // 2, 1), second)


def _rope_apply(t, cos_t, sin_t):
    w = t.shape[1]
    return t * _widen(cos_t, w) + _rope_partner(t) * _widen(sin_t, w)


def _rope_transpose(dt, cos_t, sin_t):
    w = dt.shape[1]
    return dt * _widen(cos_t, w) + _rope_partner(dt * _widen(sin_t, w))


def _in_proj(x, g_pre_mix, w_in, cos_t, sin_t):
    L = x.shape[0]
    tm = _tile(L)

    def body(x_ref, g_ref, w_ref, cos_ref, sin_ref, hn_ref, u_ref, q_ref, k_ref, v_ref):
        hn, _ = _rms_fwd(x_ref[...], g_ref[...])
        hn = hn.astype(_BF16)
        hn_ref[...] = hn
        proj = _dot(hn, w_ref[...], _NN)
        u_ref[...] = proj[:, :SSM_WIDTH]
        q = proj[:, SSM_WIDTH:SSM_WIDTH + ATTN_WIDTH]
        k = proj[:, SSM_WIDTH + ATTN_WIDTH:SSM_WIDTH + ATTN_WIDTH + KV_WIDTH]
        cos_v, sin_v = cos_ref[...], sin_ref[...]
        q_ref[...] = _rope_apply(q, cos_v, sin_v).astype(_BF16)
        k_ref[...] = _rope_apply(k, cos_v, sin_v).astype(_BF16)
        v_ref[...] = proj[:, SSM_WIDTH + ATTN_WIDTH + KV_WIDTH:].astype(_BF16)

    return _call(body, (L // tm,),
                 [_rows(tm, D_MODEL), _whole((1, D_MODEL)), _whole((D_MODEL, IN_WIDTH)),
                  _rows(tm, KV_WIDTH), _rows(tm, KV_WIDTH)],
                 [_rows(tm, D_MODEL), _rows(tm, SSM_WIDTH), _rows(tm, ATTN_WIDTH),
                  _rows(tm, KV_WIDTH), _rows(tm, KV_WIDTH)],
                 [_sds((L, D_MODEL), _BF16), _sds((L, SSM_WIDTH), _F32), _sds((L, ATTN_WIDTH), _BF16),
                  _sds((L, KV_WIDTH), _BF16), _sds((L, KV_WIDTH), _BF16)],
                 "in_proj")(x, g_pre_mix, w_in, cos_t, sin_t)


def _s5_discretize(lam_re, lam_im, log_dt):
    lr = jnp.minimum(lam_re, -1e-4)
    li = lam_im
    dt = jnp.exp(log_dt)
    mag = jnp.exp(lr * dt)
    ar = mag * jnp.cos(li * dt)
    ai = mag * jnp.sin(li * dt)
    den = lr * lr + li * li
    fr = ((ar - 1.0) * lr + ai * li) / den
    fi = (ai * lr - (ar - 1.0) * li) / den
    return ar, ai, fr, fi


def _s5_bbar(lam_re, lam_im, log_dt, b_re, b_im):
    ar, ai, fr, fi = _s5_discretize(lam_re, lam_im, log_dt)
    return ar, ai, fr * b_re - fi * b_im, fr * b_im + fi * b_re


def _spread_masks():
    e16 = (_iota((SSM_GROUP, SSM_WIDTH), 1) & (SSM_GROUP - 1)) == _iota((SSM_GROUP, SSM_WIDTH), 0)
    e64 = (_iota((SSM_STATE, N_STATE), 1) & (SSM_STATE - 1)) == _iota((SSM_STATE, N_STATE), 0)
    mask_b = (_iota((N_STATE, SSM_WIDTH), 0) >> 6) == (_iota((N_STATE, SSM_WIDTH), 1) >> 4)
    mask_c = (_iota((SSM_WIDTH, N_STATE), 0) >> 4) == (_iota((SSM_WIDTH, N_STATE), 1) >> 6)
    return e16.astype(_F32), e64.astype(_F32), mask_b, mask_c


def _ssm_prep(lam_re_r, lam_im_r, ldt_r, lam_re_c, lam_im_c, ldt_c, b_re2, b_im2, c_re2, c_im2):
    def body(lrr, lir, ldr, lrc, lic, ldc, bre, bim, cre, cim, ar_ref, ai_ref, btr, bti, ctr, cti):
        ar, ai, _, _ = _s5_discretize(lrr[...], lir[...], ldr[...])
        ar_ref[...] = ar
        ai_ref[...] = ai
        _, _, bbr, bbi = _s5_bbar(lrc[...], lic[...], ldc[...], bre[...], bim[...])
        e16, e64, mask_b, mask_c = _spread_masks()
        btr[...] = jnp.where(mask_b, _dot(bbr, e16, _NN), 0.0).astype(_BF16)
        bti[...] = jnp.where(mask_b, _dot(bbi, e16, _NN), 0.0).astype(_BF16)
        ctr[...] = jnp.where(mask_c, _dot(cre[...], e64, _NN), 0.0).astype(_BF16)
        cti[...] = jnp.where(mask_c, _dot(cim[...], e64, _NN), 0.0).astype(_BF16)

    row, col = (1, N_STATE), (N_STATE, 1)
    ins = [lam_re_r, lam_im_r, ldt_r, lam_re_c, lam_im_c, ldt_c, b_re2, b_im2, c_re2, c_im2]
    return _call(body, (1,), [_whole(a.shape) for a in ins],
                 [_whole(row), _whole(row), _whole((N_STATE, SSM_WIDTH)), _whole((N_STATE, SSM_WIDTH)),
                  _whole((SSM_WIDTH, N_STATE)), _whole((SSM_WIDTH, N_STATE))],
                 [_sds(row, _F32), _sds(row, _F32), _sds((N_STATE, SSM_WIDTH), _BF16),
                  _sds((N_STATE, SSM_WIDTH), _BF16), _sds((SSM_WIDTH, N_STATE), _BF16),
                  _sds((SSM_WIDTH, N_STATE), _BF16)], "ssm_prep")(*ins)


def _ssm_bu(u, bt_re, bt_im):
    L = u.shape[0]
    tm = _tile(L)

    def body(u_ref, br_ref, bi_ref, or_ref, oi_ref):
        ub = u_ref[...].astype(_BF16)
        or_ref[...] = _dot(ub, br_ref[...], _NT)
        oi_ref[...] = _dot(ub, bi_ref[...], _NT)

    return _call(body, (L // tm,),
                 [_rows(tm, SSM_WIDTH), _whole((N_STATE, SSM_WIDTH)), _whole((N_STATE, SSM_WIDTH))],
                 [_rows(tm, N_STATE), _rows(tm, N_STATE)],
                 [_sds((L, N_STATE), _F32)] * 2, "ssm_bu")(u, bt_re, bt_im)


def _complex_power(ar, ai, n):
    def step(_, c):
        pr, pi = c
        return pr * ar - pi * ai, pr * ai + pi * ar
    return lax.fori_loop(0, n, step, (jnp.ones_like(ar), jnp.zeros_like(ai)))


def _chunk_carries(er, ei, pr, pi, reverse):
    rows = _iota(er.shape, 0)
    sr = jnp.zeros_like(pr)
    si = jnp.zeros_like(pi)
    out_r = jnp.zeros_like(er)
    out_i = jnp.zeros_like(ei)
    order = range(SCAN_CHUNKS - 1, 0, -1) if reverse else range(SCAN_CHUNKS - 1)
    for c in order:
        e_r = er[c:c + 1, :]
        e_i = ei[c:c + 1, :]
        sr, si = pr * sr - pi * si + e_r, pr * si + pi * sr + e_i
        nxt = c - 1 if reverse else c + 1
        out_r = jnp.where(rows == nxt, sr, out_r)
        out_i = jnp.where(rows == nxt, si, out_i)
    return out_r, out_i


def _scan_fwd(b_re, b_im, a_re, a_im):
    T = b_re.shape[0]
    W = SCAN_COLS
    blk = pl.BlockSpec((T, SCAN_CHUNKS, W), lambda j: (0, 0, j))
    vec = pl.BlockSpec((1, W), lambda j: (0, j))

    def body(br_ref, bi_ref, ar_ref, ai_ref, xr_ref, xi_ref):
        ar, ai = ar_ref[...], ai_ref[...]
        ar8 = jnp.broadcast_to(ar, (SCAN_CHUNKS, W))
        ai8 = jnp.broadcast_to(ai, (SCAN_CHUNKS, W))

        def local(t, c):
            cr, ci = c
            return ar8 * cr - ai8 * ci + br_ref[t], ar8 * ci + ai8 * cr + bi_ref[t]

        zero = jnp.zeros((SCAN_CHUNKS, W), _F32)
        er, ei = lax.fori_loop(0, T, local, (zero, zero))
        pr, pi = _complex_power(ar, ai, T)
        sr, si = _chunk_carries(er, ei, pr, pi, reverse=False)

        def final(t, c):
            nr, ni = local(t, c)
            xr_ref[t] = nr
            xi_ref[t] = ni
            return nr, ni

        lax.fori_loop(0, T, final, (sr, si))

    shape = _sds(b_re.shape, _F32)
    return _call(body, (N_STATE // W,), [blk, blk, vec, vec], [blk, blk], [shape, shape],
                 "scan_fwd")(b_re, b_im, a_re, a_im)


def _scan_bwd(dx_re, dx_im, x_re, x_im, a_re, a_im, tokens=()):
    T = dx_re.shape[0]
    W = SCAN_COLS
    blk = pl.BlockSpec((T, SCAN_CHUNKS, W), lambda j: (0, 0, j))
    vec = pl.BlockSpec((1, W), lambda j: (0, j))

    def body(dr_ref, di_ref, xr_ref, xi_ref, ar_ref, ai_ref, lr_ref, li_ref, dar_ref, dai_ref):
        ar, ai = ar_ref[...], ai_ref[...]
        ar8 = jnp.broadcast_to(ar, (SCAN_CHUNKS, W))
        ai8 = jnp.broadcast_to(ai, (SCAN_CHUNKS, W))

        def local(t, c):
            cr, ci = c
            return ar8 * cr + ai8 * ci + dr_ref[t], ar8 * ci - ai8 * cr + di_ref[t]

        zero = jnp.zeros((SCAN_CHUNKS, W), _F32)
        er, ei = lax.fori_loop(0, T, lambda k, c: local(T - 1 - k, c), (zero, zero))
        pr, pi = _complex_power(ar, -ai, T)
        sr, si = _chunk_carries(er, ei, pr, pi, reverse=True)

        def grad_a(acc, nr, ni, xpr, xpi):
            return acc[0] + nr * xpr + ni * xpi, acc[1] + ni * xpr - nr * xpi

        def final(k, c):
            t = T - 1 - k
            nr, ni = local(t, c[:2])
            lr_ref[t] = nr
            li_ref[t] = ni
            gr, gi = grad_a(c[2:], nr, ni, xr_ref[t - 1], xi_ref[t - 1])
            return nr, ni, gr, gi

        cr, ci, gr, gi = lax.fori_loop(0, T - 1, final, (sr, si, zero, zero))
        nr, ni = local(0, (cr, ci))
        lr_ref[0] = nr
        li_ref[0] = ni
        first = _iota((SCAN_CHUNKS, W), 0) == 0
        xpr = jnp.where(first, 0.0, pltpu.roll(xr_ref[T - 1], 1, 0))
        xpi = jnp.where(first, 0.0, pltpu.roll(xi_ref[T - 1], 1, 0))
        gr, gi = grad_a((gr, gi), nr, ni, xpr, xpi)
        dar_ref[...] = jnp.sum(gr, axis=0, keepdims=True)
        dai_ref[...] = jnp.sum(gi, axis=0, keepdims=True)

    shape = _sds(dx_re.shape, _F32)
    row = _sds((1, N_STATE), _F32)
    return _call(body, (N_STATE // W,), [blk, blk, blk, blk, vec, vec], [blk, blk, vec, vec],
                 [shape, shape, row, row], "scan_bwd", tokens=tokens)(dx_re, dx_im, x_re, x_im, a_re, a_im)


_GELU_K = math.sqrt(2.0 / math.pi)
_GELU_C = 0.044715


def _gelu(y):
    return 0.5 * y * (1.0 + jnp.tanh(_GELU_K * (y + _GELU_C * y * y * y)))


def _gelu_grad(y):
    t = jnp.tanh(_GELU_K * (y + _GELU_C * y * y * y))
    return 0.5 * (1.0 + t) + 0.5 * y * (1.0 - t * t) * _GELU_K * (1.0 + 3.0 * _GELU_C * y * y)


def _ssm_out(x_re, x_im, u, ct_re, ct_im, d_row, w_glu, b_glu, g_ssm):
    L = u.shape[0]
    tm = _tile(L)

    def body(xr_ref, xi_ref, u_ref, cr_ref, ci_ref, d_ref, w_ref, b_ref, g_ref, y_ref, z_ref, n_ref):
        y = (_dot(xr_ref[...], cr_ref[...], _NT) - _dot(xi_ref[...], ci_ref[...], _NT)
             + d_ref[...] * u_ref[...])
        y_ref[...] = y
        z = _dot(_gelu(y), w_ref[...], _NN) + b_ref[...]
        z_ref[...] = z
        out = z[:, :SSM_WIDTH] * jax.nn.sigmoid(z[:, SSM_WIDTH:])
        n, _ = _rms_fwd(out, g_ref[...])
        n_ref[...] = n.astype(_BF16)

    return _call(body, (L // tm,),
                 [_rows(tm, N_STATE), _rows(tm, N_STATE), _rows(tm, SSM_WIDTH),
                  _whole((SSM_WIDTH, N_STATE)), _whole((SSM_WIDTH, N_STATE)), _whole((1, SSM_WIDTH)),
                  _whole((SSM_WIDTH, 2 * SSM_WIDTH)), _whole((1, 2 * SSM_WIDTH)), _whole((1, SSM_WIDTH))],
                 [_rows(tm, SSM_WIDTH), _rows(tm, 2 * SSM_WIDTH), _rows(tm, SSM_WIDTH)],
                 [_sds((L, SSM_WIDTH), _F32), _sds((L, 2 * SSM_WIDTH), _F32), _sds((L, SSM_WIDTH), _BF16)],
                 "ssm_out")(x_re, x_im, u, ct_re, ct_im, d_row, w_glu, b_glu, g_ssm)


def _ssm_out_bwd(dn, y, z, u, ct_re, ct_im, d_row, w_glu, g_ssm):
    L = u.shape[0]
    tm = _tile(L)

    def body(dn_ref, y_ref, z_ref, u_ref, cr_ref, ci_ref, d_ref, w_ref, g_ref,
             gy_ref, dz_ref, dy_ref, dud_ref, dxr_ref, dxi_ref, dg_ref, db_ref, dd_ref):
        first = pl.program_id(0) == 0
        z = z_ref[...]
        z1, z2 = z[:, :SSM_WIDTH], z[:, SSM_WIDTH:]
        sig = jax.nn.sigmoid(z2)
        out = z1 * sig
        g = g_ref[...]
        _, r = _rms_fwd(out, g)
        dout, dg = _rms_bwd(dn_ref[...], out, g, r)
        _accumulate(dg_ref, dg, first)
        dz = jnp.concatenate([dout * sig, dout * z1 * sig * (1.0 - sig)], axis=1)
        _accumulate(db_ref, jnp.sum(dz, axis=0, keepdims=True), first)
        dzb = dz.astype(_BF16)
        dz_ref[...] = dzb
        y = y_ref[...]
        gy_ref[...] = _gelu(y).astype(_BF16)
        dy = _dot(dzb, w_ref[...], _NT) * _gelu_grad(y)
        u = u_ref[...]
        _accumulate(dd_ref, jnp.sum(dy * u, axis=0, keepdims=True), first)
        dud_ref[...] = d_ref[...] * dy
        dyb = dy.astype(_BF16)
        dy_ref[...] = dyb
        dxr_ref[...] = _dot(dyb, cr_ref[...], _NN)
        dxi_ref[...] = -_dot(dyb, ci_ref[...], _NN)

    row = _whole((1, SSM_WIDTH))
    return _call(body, (L // tm,),
                 [_rows(tm, SSM_WIDTH), _rows(tm, SSM_WIDTH), _rows(tm, 2 * SSM_WIDTH), _rows(tm, SSM_WIDTH),
                  _whole((SSM_WIDTH, N_STATE)), _whole((SSM_WIDTH, N_STATE)), row,
                  _whole((SSM_WIDTH, 2 * SSM_WIDTH)), row],
                 [_rows(tm, SSM_WIDTH), _rows(tm, 2 * SSM_WIDTH), _rows(tm, SSM_WIDTH), _rows(tm, SSM_WIDTH),
                  _rows(tm, N_STATE), _rows(tm, N_STATE), row, _whole((1, 2 * SSM_WIDTH)), row],
                 [_sds((L, SSM_WIDTH), _BF16), _sds((L, 2 * SSM_WIDTH), _BF16), _sds((L, SSM_WIDTH), _BF16),
                  _sds((L, SSM_WIDTH), _F32), _sds((L, N_STATE), _F32), _sds((L, N_STATE), _F32),
                  _sds((1, SSM_WIDTH), _F32), _sds((1, 2 * SSM_WIDTH), _F32), _sds((1, SSM_WIDTH), _F32)],
                 "ssm_out_bwd")(dn, y, z, u, ct_re, ct_im, d_row, w_glu, g_ssm)


def _ssm_du(lam_re, lam_im, bt_re, bt_im, dud):
    L = dud.shape[0]
    tm = _tile(L)

    def body(lr_ref, li_ref, br_ref, bi_ref, dud_ref, du_ref):
        du_ref[...] = (_dot(lr_ref[...], br_ref[...], _NN) + _dot(li_ref[...], bi_ref[...], _NN)
                       + dud_ref[...])

    return _call(body, (L // tm,),
                 [_rows(tm, N_STATE), _rows(tm, N_STATE), _whole((N_STATE, SSM_WIDTH)),
                  _whole((N_STATE, SSM_WIDTH)), _rows(tm, SSM_WIDTH)],
                 _rows(tm, SSM_WIDTH), _sds((L, SSM_WIDTH), _F32), "ssm_du")(lam_re, lam_im, bt_re, bt_im, dud)


def _ssm_param_bwd(da_re_c, da_im_c, dbt_re, dbt_im, dct_re, dct_im,
                   lam_re_c, lam_im_c, ldt_c, b_re2, b_im2):
    def body(dar, dai, dbr, dbi, dcr, dci, lrc, lic, ldc, bre, bim,
             glr, gli, gdt, gbr, gbi, gcr, gci):
        e16, e64, mask_b, mask_c = _spread_masks()
        dbbr = _dot_exact(jnp.where(mask_b, dbr[...], 0.0), e16, _NT)
        dbbi = _dot_exact(jnp.where(mask_b, dbi[...], 0.0), e16, _NT)
        gcr[...] = _dot_exact(jnp.where(mask_c, dcr[...], 0.0), e64, _NT)
        gci[...] = -_dot_exact(jnp.where(mask_c, dci[...], 0.0), e64, _NT)
        _, vjp = jax.vjp(_s5_bbar, lrc[...], lic[...], ldc[...], bre[...], bim[...])
        d_lr, d_li, d_dt, d_br, d_bi = vjp((dar[...], dai[...], dbbr, dbbi))
        glr[...] = d_lr
        gli[...] = d_li
        gbr[...] = d_br
        gbi[...] = d_bi
        groups = (_iota((SSM_GROUPS, N_STATE), 1) >> 6) == _iota((SSM_GROUPS, N_STATE), 0)
        gdt[...] = _dot_exact(groups.astype(_F32), jnp.broadcast_to(d_dt, (N_STATE, 128)), _NN)

    col = (N_STATE, 1)
    ins = [da_re_c, da_im_c, dbt_re, dbt_im, dct_re, dct_im, lam_re_c, lam_im_c, ldt_c, b_re2, b_im2]
    outs = [col, col, (SSM_GROUPS, 128), (N_STATE, SSM_GROUP), (N_STATE, SSM_GROUP),
            (SSM_WIDTH, SSM_STATE), (SSM_WIDTH, SSM_STATE)]
    return _call(body, (1,), [_whole(a.shape) for a in ins], [_whole(s) for s in outs],
                 [_sds(s, _F32) for s in outs], "ssm_param_bwd")(*ins)


def _head_spread(j):
    r = _iota((KV_WIDTH, 256), 0)
    c = _iota((KV_WIDTH, 256), 1)
    return (r == HEAD_DIM * j + (c & (HEAD_DIM - 1))).astype(_BF16)


def _attn_scores(q_masked, kt, blk, sink):
    s = _dot(q_masked, kt, _NT) * (HEAD_DIM ** -0.5)
    qi = _iota((BLOCK, 2 * BLOCK), 0)
    kj = _iota((BLOCK, 2 * BLOCK), 1)
    rel = qi + BLOCK - kj
    valid = (rel >= 0) & (rel < BLOCK) & (blk * BLOCK - BLOCK + kj >= 0)
    s = jnp.where(valid, s, MASK_VALUE)
    m = jnp.maximum(jnp.max(s, axis=-1, keepdims=True), sink)
    p = jnp.exp(s - m)
    e_sink = jnp.exp(sink - m)
    den = jnp.sum(p, axis=-1, keepdims=True) + e_sink
    return p / den, e_sink / den


def _attn_specs():
    prev = lambda i: (jnp.maximum(i - 1, 0), 0)
    cur = lambda i: (i, 0)
    kv = [pl.BlockSpec((BLOCK, KV_WIDTH), prev), pl.BlockSpec((BLOCK, KV_WIDTH), cur)]
    return [pl.BlockSpec((BLOCK, ATTN_WIDTH), cur)] + kv + kv


def _attn_fwd(q, k, v, sinks, g_attn):
    L = q.shape[0]

    def body(q_ref, kp_ref, kc_ref, vp_ref, vc_ref, sink_ref, g_ref, o_ref, n_ref):
        blk = pl.program_id(0)
        kwin = jnp.concatenate([kp_ref[...], kc_ref[...]], axis=0)
        vwin = jnp.concatenate([vp_ref[...], vc_ref[...]], axis=0)
        lane_head = _iota((1, 256), 1) >> 6
        halves = []
        for j in range(N_KV_HEADS):
            spread = _head_spread(j)
            kt = _dot(kwin, spread, _NN).astype(_BF16)
            vt = _dot(vwin, spread, _NN).astype(_BF16)
            qj = q_ref[:, 256 * j:256 * (j + 1)]
            oj = jnp.zeros((BLOCK, 256), _F32)
            for g in range(Q_PER_KV):
                hm = lane_head == g
                p, _ = _attn_scores(jnp.where(hm, qj, jnp.zeros_like(qj)), kt, blk, sink_ref[Q_PER_KV * j + g])
                oj = oj + jnp.where(hm, _dot(p, vt, _NN), 0.0)
            halves.append(oj)
        o = jnp.concatenate(halves, axis=1)
        o_ref[...] = o
        n, _ = _rms_fwd(o, g_ref[...])
        n_ref[...] = n.astype(_BF16)

    cur = lambda i: (i, 0)
    return _call(body, (L // BLOCK,),
                 _attn_specs() + [pl.BlockSpec(memory_space=pltpu.SMEM), _whole((1, ATTN_WIDTH))],
                 [pl.BlockSpec((BLOCK, ATTN_WIDTH), cur)] * 2,
                 [_sds((L, ATTN_WIDTH), _F32), _sds((L, ATTN_WIDTH), _BF16)],
                 "attn_fwd")(q, k, k, v, v, sinks, g_attn)


def _attn_bwd(q, k, v, o, dn, sinks, g_attn):
    L = q.shape[0]

    def body(q_ref, kp_ref, kc_ref, vp_ref, vc_ref, o_ref, dn_ref, sink_ref, g_ref,
             dq_ref, dk_ref, dv_ref, dsink_ref, dg_ref):
        blk = pl.program_id(0)
        first = blk == 0

        @pl.when(first)
        def _():
            dk_ref[...] = jnp.zeros_like(dk_ref)
            dv_ref[...] = jnp.zeros_like(dv_ref)
            dsink_ref[...] = jnp.zeros_like(dsink_ref)

        o = o_ref[...]
        g = g_ref[...]
        _, r = _rms_fwd(o, g)
        do, dg = _rms_bwd(dn_ref[...], o, g, r)
        _accumulate(dg_ref, dg, first)
        kwin = jnp.concatenate([kp_ref[...], kc_ref[...]], axis=0)
        vwin = jnp.concatenate([vp_ref[...], vc_ref[...]], axis=0)
        lane_head = _iota((1, 256), 1) >> 6
        lane = _iota((1, 128), 1)
        dsink = jnp.zeros((1, 128), _F32)
        dkwin = jnp.zeros((2 * BLOCK, KV_WIDTH), _F32)
        dvwin = jnp.zeros((2 * BLOCK, KV_WIDTH), _F32)
        dq_halves = []
        for j in range(N_KV_HEADS):
            spread = _head_spread(j)
            kt = _dot(kwin, spread, _NN).astype(_BF16)
            vt = _dot(vwin, spread, _NN).astype(_BF16)
            qj = q_ref[:, 256 * j:256 * (j + 1)]
            doj = do[:, 256 * j:256 * (j + 1)]
            dqj = jnp.zeros((BLOCK, 256), _F32)
            dkt = jnp.zeros((2 * BLOCK, 256), _F32)
            dvt = jnp.zeros((2 * BLOCK, 256), _F32)
            for gi in range(Q_PER_KV):
                hm = lane_head == gi
                qm = jnp.where(hm, qj, jnp.zeros_like(qj))
                p, p_sink = _attn_scores(qm, kt, blk, sink_ref[Q_PER_KV * j + gi])
                dom = jnp.where(hm, doj, 0.0)
                dp = _dot(dom, vt, _NT)
                delta = jnp.sum(p * dp, axis=-1, keepdims=True)
                ds = p * (dp - delta) * (HEAD_DIM ** -0.5)
                dsink = dsink - jnp.where(lane == Q_PER_KV * j + gi, jnp.sum(p_sink * delta, axis=0, keepdims=True), 0.0)
                dvt = dvt + _dot(p, dom, _TN)
                dkt = dkt + _dot(ds, qm, _TN)
                dqj = dqj + jnp.where(hm, _dot(ds, kt, _NN), 0.0)
            dq_halves.append(dqj)
            spread_f = spread.astype(_F32)
            dkwin = dkwin + _dot_exact(dkt, spread_f, _NT)
            dvwin = dvwin + _dot_exact(dvt, spread_f, _NT)
        dq_ref[...] = jnp.concatenate(dq_halves, axis=1)
        dsink_ref[...] += dsink
        prev = pl.ds(pl.multiple_of(jnp.maximum(blk - 1, 0) * BLOCK, BLOCK), BLOCK)
        cur = pl.ds(pl.multiple_of(blk * BLOCK, BLOCK), BLOCK)
        dk_ref[prev, :] += dkwin[:BLOCK]
        dk_ref[cur, :] += dkwin[BLOCK:]
        dv_ref[prev, :] += dvwin[:BLOCK]
        dv_ref[cur, :] += dvwin[BLOCK:]

    cur = lambda i: (i, 0)
    blk_q = pl.BlockSpec((BLOCK, ATTN_WIDTH), cur)
    return _call(body, (L // BLOCK,),
                 _attn_specs() + [blk_q, blk_q, pl.BlockSpec(memory_space=pltpu.SMEM), _whole((1, ATTN_WIDTH))],
                 [blk_q, _whole((L, KV_WIDTH)), _whole((L, KV_WIDTH)), _whole((1, 128)), _whole((1, ATTN_WIDTH))],
                 [_sds((L, ATTN_WIDTH), _F32), _sds((L, KV_WIDTH), _F32), _sds((L, KV_WIDTH), _F32),
                  _sds((1, 128), _F32), _sds((1, ATTN_WIDTH), _F32)],
                 "attn_bwd")(q, k, k, v, v, o, dn, sinks, g_attn)


def _out_proj(n_ssm, n_attn, x, w_out, g_post_mix, g_pre_ffn):
    L = x.shape[0]
    tm = _tile(L)

    def body(ns_ref, na_ref, x_ref, w_ref, g1_ref, g2_ref, merged_ref, mo_ref, h1_ref, hn2_ref):
        merged = jnp.concatenate([ns_ref[...], na_ref[...]], axis=1)
        merged_ref[...] = merged
        mo = _dot(merged, w_ref[...], _NN)
        mo_ref[...] = mo
        n, _ = _rms_fwd(mo, g1_ref[...])
        h1 = x_ref[...] + n
        h1_ref[...] = h1
        hn2, _ = _rms_fwd(h1, g2_ref[...])
        hn2_ref[...] = hn2.astype(_BF16)

    row = _whole((1, D_MODEL))
    return _call(body, (L // tm,),
                 [_rows(tm, SSM_WIDTH), _rows(tm, ATTN_WIDTH), _rows(tm, D_MODEL), _whole((D_MODEL, D_MODEL)), row, row],
                 [_rows(tm, D_MODEL)] * 4,
                 [_sds((L, D_MODEL), _BF16), _sds((L, D_MODEL), _F32), _sds((L, D_MODEL), _F32), _sds((L, D_MODEL), _BF16)],
                 "out_proj")(n_ssm, n_attn, x, w_out, g_post_mix, g_pre_ffn)


def _ffn(hn2, h1, target, w_gate_up, w_down, g_pre_ffn, g_post_ffn):
    L = h1.shape[0]
    tm = _tile(L)
    half = D_FF // 2

    def body(hn2_ref, h1_ref, tgt_ref, wgu_hbm, wd_hbm, g2_ref, g3_ref,
             act_ref, dgu_ref, dff_ref, dh1_ref, loss_ref, dg3_ref, dg2_ref,
             wgu, wd, gu, sem):
        first = pl.program_id(0) == 0

        @pl.when(first)
        def _():
            c1 = pltpu.make_async_copy(wgu_hbm, wgu, sem.at[0])
            c2 = pltpu.make_async_copy(wd_hbm, wd, sem.at[1])
            c1.start()
            c2.start()
            c1.wait()
            c2.wait()

        hn2 = hn2_ref[...]
        ff = jnp.zeros((tm, D_MODEL), _F32)
        for c in range(2):
            gate = _dot(hn2, wgu[:, half * c:half * (c + 1)], _NN)
            up = _dot(hn2, wgu[:, D_FF + half * c:D_FF + half * (c + 1)], _NN)
            gu[:, half * c:half * (c + 1)] = gate
            gu[:, D_FF + half * c:D_FF + half * (c + 1)] = up
            act = (gate * jax.nn.sigmoid(gate) * up).astype(_BF16)
            act_ref[:, half * c:half * (c + 1)] = act
            ff = ff + _dot(act, wd[half * c:half * (c + 1), :], _NN)
        g3 = g3_ref[...]
        n, r = _rms_fwd(ff, g3)
        h1 = h1_ref[...]
        err = h1 + n - tgt_ref[...]
        loss = 0.5 * jnp.sum(jnp.mean(err * err, axis=-1, keepdims=True), axis=0, keepdims=True)
        _accumulate(loss_ref, jnp.broadcast_to(loss, (1, 128)), first)
        dh2 = err * (1.0 / D_MODEL)
        dff, dg3 = _rms_bwd(dh2, ff, g3, r)
        _accumulate(dg3_ref, dg3, first)
        dffb = dff.astype(_BF16)
        dff_ref[...] = dffb
        dhn2 = jnp.zeros((tm, D_MODEL), _F32)
        for c in range(2):
            dact = _dot(dffb, wd[half * c:half * (c + 1), :], _NT)
            gate = gu[:, half * c:half * (c + 1)]
            up = gu[:, D_FF + half * c:D_FF + half * (c + 1)]
            sig = jax.nn.sigmoid(gate)
            silu = gate * sig
            dgate = (dact * up * (sig + silu * (1.0 - sig))).astype(_BF16)
            dup = (dact * silu).astype(_BF16)
            dgu_ref[:, half * c:half * (c + 1)] = dgate
            dgu_ref[:, D_FF + half * c:D_FF + half * (c + 1)] = dup
            dhn2 = dhn2 + _dot(dgate, wgu[:, half * c:half * (c + 1)], _NT)
            dhn2 = dhn2 + _dot(dup, wgu[:, D_FF + half * c:D_FF + half * (c + 1)], _NT)
        g2 = g2_ref[...]
        _, r2 = _rms_fwd(h1, g2)
        dh1, dg2 = _rms_bwd(dhn2, h1, g2, r2)
        _accumulate(dg2_ref, dg2, first)
        dh1_ref[...] = dh2 + dh1

    row = _whole((1, D_MODEL))
    anyspace = pl.BlockSpec(memory_space=pl.ANY)
    return _call(body, (L // tm,),
                 [_rows(tm, D_MODEL), _rows(tm, D_MODEL), _rows(tm, D_MODEL), anyspace, anyspace, row, row],
                 [_rows(tm, D_FF), _rows(tm, 2 * D_FF), _rows(tm, D_MODEL), _rows(tm, D_MODEL),
                  _whole((1, 128)), row, row],
                 [_sds((L, D_FF), _BF16), _sds((L, 2 * D_FF), _BF16), _sds((L, D_MODEL), _BF16),
                  _sds((L, D_MODEL), _F32), _sds((1, 128), _F32), _sds((1, D_MODEL), _F32), _sds((1, D_MODEL), _F32)],
                 "ffn",
                 scratch=[pltpu.VMEM((D_MODEL, 2 * D_FF), _BF16), pltpu.VMEM((D_FF, D_MODEL), _BF16),
                          pltpu.VMEM((tm, 2 * D_FF), _F32), pltpu.SemaphoreType.DMA((2,))],
                 )(hn2, h1, target, w_gate_up, w_down, g_pre_ffn, g_post_ffn)


def _out_proj_bwd(dh1, mo, w_out, g_post_mix, tokens=()):
    L = dh1.shape[0]
    tm = _tile(L)

    def body(dh1_ref, mo_ref, w_ref, g_ref, dmo_ref, dns_ref, dna_ref, dg_ref):
        first = pl.program_id(0) == 0
        mo = mo_ref[...]
        g = g_ref[...]
        _, r = _rms_fwd(mo, g)
        dmo, dg = _rms_bwd(dh1_ref[...], mo, g, r)
        _accumulate(dg_ref, dg, first)
        dmob = dmo.astype(_BF16)
        dmo_ref[...] = dmob
        dmerged = _dot(dmob, w_ref[...], _NT)
        dns_ref[...] = dmerged[:, :SSM_WIDTH]
        dna_ref[...] = dmerged[:, SSM_WIDTH:]

    row = _whole((1, D_MODEL))
    return _call(body, (L // tm,),
                 [_rows(tm, D_MODEL), _rows(tm, D_MODEL), _whole((D_MODEL, D_MODEL)), row],
                 [_rows(tm, D_MODEL), _rows(tm, SSM_WIDTH), _rows(tm, ATTN_WIDTH), row],
                 [_sds((L, D_MODEL), _BF16), _sds((L, SSM_WIDTH), _F32), _sds((L, ATTN_WIDTH), _F32),
                  _sds((1, D_MODEL), _F32)],
                 "out_proj_bwd", tokens=tokens)(dh1, mo, w_out, g_post_mix)


def _in_proj_bwd(du, dq, dk, dv, cos_t, sin_t, x, dh1, g_pre_mix, w_in):
    L = x.shape[0]
    tm = _tile(L)

    def body(du_ref, dq_ref, dk_ref, dv_ref, cos_ref, sin_ref, x_ref, dh1_ref, g_ref, w_ref,
             dproj_ref, dx_ref, dg_ref):
        first = pl.program_id(0) == 0
        cos_v, sin_v = cos_ref[...], sin_ref[...]
        dproj = jnp.concatenate([du_ref[...], _rope_transpose(dq_ref[...], cos_v, sin_v),
                                 _rope_transpose(dk_ref[...], cos_v, sin_v), dv_ref[...]], axis=1).astype(_BF16)
        dproj_ref[...] = dproj
        dhn = _dot(dproj, w_ref[...], _NT)
        x = x_ref[...]
        g = g_ref[...]
        _, r = _rms_fwd(x, g)
        dx, dg = _rms_bwd(dhn, x, g, r)
        _accumulate(dg_ref, dg, first)
        dx_ref[...] = dh1_ref[...] + dx

    row = _whole((1, D_MODEL))
    return _call(body, (L // tm,),
                 [_rows(tm, SSM_WIDTH), _rows(tm, ATTN_WIDTH), _rows(tm, KV_WIDTH), _rows(tm, KV_WIDTH),
                  _rows(tm, KV_WIDTH), _rows(tm, KV_WIDTH), _rows(tm, D_MODEL), _rows(tm, D_MODEL), row,
                  _whole((D_MODEL, IN_WIDTH))],
                 [_rows(tm, IN_WIDTH), _rows(tm, D_MODEL), row],
                 [_sds((L, IN_WIDTH), _BF16), _sds((L, D_MODEL), _F32), _sds((1, D_MODEL), _F32)],
                 "in_proj_bwd")(du, dq, dk, dv, cos_t, sin_t, x, dh1, g_pre_mix, w_in)


def _matmul_tn(a, b, out_dtype, name, scale=1.0):
    K, M = a.shape
    N = b.shape[1]
    tm = next(t for t in (512, 256, 128) if M % t == 0)
    tn = next(t for t in (512, 256, 128) if N % t == 0)

    def body(a_ref, b_ref, o_ref):
        acc = _dot(a_ref[...], b_ref[...], _TN)
        o_ref[...] = (acc if scale == 1.0 else acc * scale).astype(out_dtype)

    params = pltpu.CompilerParams(dimension_semantics=("arbitrary", "arbitrary"), vmem_limit_bytes=VMEM_LIMIT)
    return pl.pallas_call(body, grid=(M // tm, N // tn),
                          in_specs=[pl.BlockSpec((K, tm), lambda i, j: (0, i)),
                                    pl.BlockSpec((K, tn), lambda i, j: (0, j))],
                          out_specs=pl.BlockSpec((tm, tn), lambda i, j: (i, j)),
                          out_shape=_sds((M, N), out_dtype), compiler_params=params, name=name)(a, b)


def _to_chunked(a):
    L, n = a.shape
    return a.reshape(SCAN_CHUNKS, L // SCAN_CHUNKS, n).transpose(1, 0, 2).reshape(L, n)


def _from_chunked(a):
    L, n = a.shape
    return a.reshape(L // SCAN_CHUNKS, SCAN_CHUNKS, n).transpose(1, 0, 2).reshape(L, n)


def _local_step(x, pos, target, p, fetch, publish):
    L = x.shape[0]
    T = L // SCAN_CHUNKS
    cos_t, sin_t = _rope_tables(pos.reshape(L, 1))
    w_in, = fetch(("w_in",), None)
    hn, u, q, k, v = _in_proj(x, p["g_pre_mix"], w_in, cos_t, sin_t)

    lam_re_r = p["ssm_lambda_re"].reshape(1, N_STATE)
    lam_im_r = p["ssm_lambda_im"].reshape(1, N_STATE)
    ldt_r = jnp.broadcast_to(p["ssm_log_dt"].reshape(SSM_GROUPS, 1), (SSM_GROUPS, SSM_STATE)).reshape(1, N_STATE)
    lam_re_c, lam_im_c, ldt_c = (a.reshape(N_STATE, 1) for a in (lam_re_r, lam_im_r, ldt_r))
    b_re2 = p["ssm_b_re"].reshape(N_STATE, SSM_GROUP)
    b_im2 = p["ssm_b_im"].reshape(N_STATE, SSM_GROUP)
    c_re2 = p["ssm_c_re"].reshape(SSM_WIDTH, SSM_STATE)
    c_im2 = p["ssm_c_im"].reshape(SSM_WIDTH, SSM_STATE)
    d_row = p["ssm_d"].reshape(1, SSM_WIDTH)
    a_re, a_im, bt_re, bt_im, ct_re, ct_im = _ssm_prep(
        lam_re_r, lam_im_r, ldt_r, lam_re_c, lam_im_c, ldt_c, b_re2, b_im2, c_re2, c_im2)

    u_c = _to_chunked(u)
    bu_re, bu_im = _ssm_bu(u_c, bt_re, bt_im)
    x_re, x_im = _scan_fwd(bu_re.reshape(T, SCAN_CHUNKS, N_STATE), bu_im.reshape(T, SCAN_CHUNKS, N_STATE), a_re, a_im)
    w_glu, = fetch(("w_glu",), x_re)
    y, z, n_ssm_c = _ssm_out(x_re.reshape(L, N_STATE), x_im.reshape(L, N_STATE), u_c, ct_re, ct_im, d_row,
                             w_glu, p["b_glu"], p["g_ssm_out"])
    n_ssm = _from_chunked(n_ssm_c)

    sinks = p["attn_sinks"].reshape(N_Q_HEADS)
    o, n_attn = _attn_fwd(q, k, v, sinks, p["g_attn_out"])
    w_out, = fetch(("w_out",), n_attn)
    merged, mo, h1, hn2 = _out_proj(n_ssm, n_attn, x, w_out, p["g_post_mix"], p["g_pre_ffn"])
    w_gate_up, w_down = fetch(("w_gate_up", "w_down"), hn2)
    act, dgu, dff, dh1, loss, dg_post_ffn, dg_pre_ffn = _ffn(
        hn2, h1, target, w_gate_up, w_down, p["g_pre_ffn"], p["g_post_ffn"])
    grads = {"g_post_ffn": dg_post_ffn, "g_pre_ffn": dg_pre_ffn}
    tokens = publish({"w_down": _matmul_tn(act, dff, _BF16, "grad_w_down"),
                      "w_gate_up": _matmul_tn(hn2, dgu, _BF16, "grad_w_gate_up")})

    dmo, dn_ssm, dn_attn, grads["g_post_mix"] = _out_proj_bwd(dh1, mo, w_out, p["g_post_mix"], tokens)
    grad_w_out = _matmul_tn(merged, dmo, _BF16, "grad_w_out")

    dq, dk, dv, dsink, grads["g_attn_out"] = _attn_bwd(q, k, v, o, dn_attn, sinks, p["g_attn_out"])
    grads["attn_sinks"] = dsink[:, :N_Q_HEADS]

    gy, dz, dy, dud, dx_re, dx_im, grads["g_ssm_out"], grads["b_glu"], dd = _ssm_out_bwd(
        _to_chunked(dn_ssm), y, z, u_c, ct_re, ct_im, d_row, w_glu, p["g_ssm_out"])
    grads["ssm_d"] = dd.reshape(1, SSM_GROUPS, SSM_GROUP)
    tokens = publish({"w_out": grad_w_out, "w_glu": _matmul_tn(gy, dz, _BF16, "grad_w_glu")})
    lam_re, lam_im, da_re, da_im = _scan_bwd(dx_re.reshape(T, SCAN_CHUNKS, N_STATE), dx_im.reshape(T, SCAN_CHUNKS, N_STATE),
                                             x_re, x_im, a_re, a_im, tokens)
    lam_re = lam_re.reshape(L, N_STATE)
    lam_im = lam_im.reshape(L, N_STATE)
    dct_re = _matmul_tn(dy, x_re.reshape(L, N_STATE), _F32, "grad_ct_re")
    dct_im = _matmul_tn(dy, x_im.reshape(L, N_STATE), _F32, "grad_ct_im")
    dbt_re = _matmul_tn(lam_re, u_c, _F32, "grad_bt_re")
    dbt_im = _matmul_tn(lam_im, u_c, _F32, "grad_bt_im")
    g_lr, g_li, g_dt, g_br, g_bi, g_cr, g_ci = _ssm_param_bwd(
        da_re.reshape(N_STATE, 1), da_im.reshape(N_STATE, 1), dbt_re, dbt_im, dct_re, dct_im,
        lam_re_c, lam_im_c, ldt_c, b_re2, b_im2)
    grads["ssm_lambda_re"] = g_lr.reshape(1, SSM_GROUPS, SSM_STATE)
    grads["ssm_lambda_im"] = g_li.reshape(1, SSM_GROUPS, SSM_STATE)
    grads["ssm_log_dt"] = g_dt[:, 0].reshape(1, SSM_GROUPS)
    grads["ssm_b_re"] = g_br.reshape(1, SSM_GROUPS, SSM_STATE, SSM_GROUP)
    grads["ssm_b_im"] = g_bi.reshape(1, SSM_GROUPS, SSM_STATE, SSM_GROUP)
    grads["ssm_c_re"] = g_cr.reshape(1, SSM_GROUPS, SSM_GROUP, SSM_STATE)
    grads["ssm_c_im"] = g_ci.reshape(1, SSM_GROUPS, SSM_GROUP, SSM_STATE)

    du = _from_chunked(_ssm_du(lam_re, lam_im, bt_re, bt_im, dud))
    dproj, grad_x, grads["g_pre_mix"] = _in_proj_bwd(du, dq, dk, dv, cos_t, sin_t, x, dh1, p["g_pre_mix"], w_in)
    publish({"w_in": _matmul_tn(hn, dproj, _BF16, "grad_w_in")})
    return loss[0, 0], grad_x, grads


_MESH = pl.DeviceIdType.MESH
_PEERS = N_DEV - 1


def _mesh_pos():
    return lax.axis_index("x"), lax.axis_index("y"), lax.axis_index("c")


def _dev_index(px, py, pc):
    return 4 * px + 2 * py + pc


def _all_gather(shards, out_dtype, name):
    n = len(shards)

    def body(*refs):
        ins, outs, stages = refs[:n], refs[n:2 * n], refs[2 * n:3 * n]
        send_sems, recv_sems, local_sems = refs[3 * n:]
        x, y, c = _mesh_pos()
        me, sibling = (x, y, c), (x, y, 1 - c)
        chips = [(1 - x, y), (x, 1 - y), (1 - x, 1 - y)]

        def copy(w, k, block, to, src=None):
            slot = outs[w].at[_dev_index(*block)]
            return pltpu.make_async_remote_copy(
                src_ref=slot if src is None else src, dst_ref=slot,
                send_sem=send_sems.at[_PEERS * w + k], recv_sem=recv_sems.at[_PEERS * w + k],
                device_id=to, device_id_type=_MESH)

        for w in range(n):
            stages[w][...] = ins[w][...].astype(out_dtype)
        mine, first, passed = [], [], []
        for w in range(n):
            cp = pltpu.make_async_copy(stages[w], outs[w].at[_dev_index(*me)], local_sems.at[w])
            cp.start()
            mine.append(cp)
            sends = [copy(w, 0, me, sibling, src=stages[w])]
            sends += [copy(w, 1 + j, me, (*chip, c), src=stages[w]) for j, chip in enumerate(chips)]
            for cp in sends:
                cp.start()
            first += sends
        for w in range(n):
            for j, chip in enumerate(chips):
                copy(w, 1 + j, (*chip, c), me).wait_recv()
                cp = copy(w, 4 + j, (*chip, c), sibling)
                cp.start()
                passed.append(cp)
        for w in range(n):
            copy(w, 0, sibling, me).wait_recv()
            for j, chip in enumerate(chips):
                copy(w, 4 + j, (*chip, 1 - c), me).wait_recv()
        for cp in first + passed:
            cp.wait_send()
        for cp in mine:
            cp.wait()

    return pl.pallas_call(
        body, name=name,
        out_shape=[_sds((N_DEV,) + s.shape, out_dtype) for s in shards],
        in_specs=[pl.BlockSpec(memory_space=pltpu.VMEM)] * n,
        out_specs=[pl.BlockSpec(memory_space=pl.ANY)] * n,
        scratch_shapes=[pltpu.VMEM(s.shape, out_dtype) for s in shards]
        + [pltpu.SemaphoreType.DMA((_PEERS * n,)), pltpu.SemaphoreType.DMA((_PEERS * n,)),
           pltpu.SemaphoreType.DMA((n,))],
        compiler_params=pltpu.CompilerParams(vmem_limit_bytes=VMEM_LIMIT),
    )(*shards)


_HBM_SPEC = pl.BlockSpec(memory_space=pltpu.HBM)
_SEM_SPEC = pl.BlockSpec(memory_space=pltpu.SEMAPHORE)
_DATAFLOW = pltpu.SideEffectType.DATAFLOW_SIDE_EFFECTING


def _peer(x, y, c, r):
    return (x ^ ((r >> 2) & 1), y ^ ((r >> 1) & 1), c ^ (r & 1))


def _hbm(a):
    return pltpu.with_memory_space_constraint(a, pltpu.HBM)


def _send_start(sources, blocked, name):
    n = len(sources)
    lands = [lax.empty((N_DEV,) + (s.shape[1:] if blocked else s.shape), s.dtype) for s in sources]

    def body(*refs):
        srcs, zones = refs[:n], refs[n:2 * n]
        send_sems, recv_sems = refs[2 * n:3 * n], refs[3 * n:4 * n]
        token, local_sems = refs[6 * n], refs[6 * n + 1]
        x, y, c = _mesh_pos()
        me = _dev_index(x, y, c)
        local = []
        for w in range(n):
            cp = pltpu.make_async_copy(srcs[w].at[me] if blocked else srcs[w], zones[w].at[me], local_sems.at[w])
            cp.start()
            local.append(cp)
            for r in range(1, N_DEV):
                peer = _peer(x, y, c, r)
                pltpu.make_async_remote_copy(
                    src_ref=srcs[w].at[_dev_index(*peer)] if blocked else srcs[w], dst_ref=zones[w].at[me],
                    send_sem=send_sems[w].at[r - 1], recv_sem=recv_sems[w].at[r - 1],
                    device_id=peer, device_id_type=_MESH).start()
        for cp in local:
            cp.wait()
        token[...] = jnp.zeros_like(token)

    sems = [pltpu.SemaphoreType.DMA((_PEERS,))] * (2 * n)
    out = pl.pallas_call(
        body, name=name,
        out_shape=sems + [pltpu.HBM(a.shape, a.dtype) for a in list(sources) + lands] + [_sds((8, 128), _F32)],
        in_specs=[_HBM_SPEC] * (2 * n),
        out_specs=[_SEM_SPEC] * (2 * n) + [_HBM_SPEC] * (2 * n) + [pl.BlockSpec(memory_space=pltpu.VMEM)],
        input_output_aliases={i: 2 * n + i for i in range(2 * n)},
        scratch_shapes=[pltpu.SemaphoreType.DMA((n,))],
        compiler_params=pltpu.CompilerParams(has_side_effects=_DATAFLOW),
    )(*[_hbm(a) for a in sources], *[_hbm(a) for a in lands])
    return out[:n], out[n:2 * n], out[2 * n:3 * n], out[3 * n:4 * n], out[4 * n]


def _send_wait(send_sems, recv_sems, sources, lands, after, blocked, name):
    n = len(sources)

    def body(*refs):
        srcs, zones = refs[:n], refs[n:2 * n]
        sends, recvs = refs[2 * n:3 * n], refs[3 * n:4 * n]
        x, y, c = _mesh_pos()
        for w in range(n):
            for r in range(1, N_DEV):
                peer = _peer(x, y, c, r)
                idx = _dev_index(*peer)
                cp = pltpu.make_async_remote_copy(
                    src_ref=srcs[w].at[idx] if blocked else srcs[w], dst_ref=zones[w].at[idx],
                    send_sem=sends[w].at[r - 1], recv_sem=recvs[w].at[r - 1],
                    device_id=peer, device_id_type=_MESH)
                cp.wait_send()
                cp.wait_recv()

    out = pl.pallas_call(
        body, name=name,
        out_shape=[pltpu.HBM(a.shape, a.dtype) for a in list(sources) + list(lands)],
        in_specs=[_HBM_SPEC] * (2 * n) + [_SEM_SPEC] * (2 * n) + [pl.BlockSpec(memory_space=pl.ANY)],
        out_specs=[_HBM_SPEC] * (2 * n),
        input_output_aliases={i: i for i in range(2 * n)},
        compiler_params=pltpu.CompilerParams(has_side_effects=_DATAFLOW),
    )(*sources, *lands, *send_sems, *recv_sems, after)
    return out[n:]


def _row_tile(rows):
    return next(t for t in range(min(rows, 256), 0, -16) if rows % t == 0)


def _adamw(parts, w, m, v, name):
    rows, cols = w.shape
    tr = _row_tile(rows)

    def body(p_ref, w_ref, m_ref, v_ref, g_ref, d_ref, nm_ref, nv_ref):
        g = p_ref[0].astype(_F32)
        for s in range(1, N_DEV):
            g = g + p_ref[s].astype(_F32)
        new_m = ADAM_B1 * m_ref[...] + (1.0 - ADAM_B1) * g
        new_v = ADAM_B2 * v_ref[...] + (1.0 - ADAM_B2) * (g * g)
        m_hat = new_m / (1.0 - ADAM_B1 ** ADAM_STEP)
        v_hat = new_v / (1.0 - ADAM_B2 ** ADAM_STEP)
        g_ref[...] = g
        d_ref[...] = -ADAM_LR * (m_hat / (jnp.sqrt(v_hat) + ADAM_EPS) + ADAM_WD * w_ref[...])
        nm_ref[...] = new_m
        nv_ref[...] = new_v

    blk = _rows(tr, cols)
    return _call(body, (rows // tr,),
                 [pl.BlockSpec((N_DEV, tr, cols), lambda i: (0, i, 0)), blk, blk, blk],
                 [blk] * 4, [_sds((rows, cols), _F32)] * 4, name)(parts, w, m, v)


_SMALL = ("g_pre_mix", "ssm_lambda_re", "ssm_lambda_im", "ssm_log_dt", "ssm_b_re", "ssm_b_im",
          "ssm_c_re", "ssm_c_im", "ssm_d", "b_glu", "attn_sinks", "g_ssm_out", "g_attn_out",
          "g_post_mix", "g_pre_ffn", "g_post_ffn")
_BIG = ("w_in", "w_glu", "w_out", "w_gate_up", "w_down")
_WEIGHTS = ("g_pre_mix", "w_in", "ssm_lambda_re", "ssm_lambda_im", "ssm_log_dt", "ssm_b_re", "ssm_b_im",
            "ssm_c_re", "ssm_c_im", "ssm_d", "w_glu", "b_glu", "attn_sinks", "g_ssm_out", "g_attn_out",
            "w_out", "g_post_mix", "g_pre_ffn", "w_gate_up", "w_down", "g_post_ffn")
_LANES = 128


def _pack(arrays, extra_rows):
    rows = []
    for a in arrays:
        flat = a.reshape(-1).astype(_F32)
        pad = (-flat.shape[0]) % _LANES
        rows.append(jnp.pad(flat, (0, pad)).reshape(-1, _LANES))
    packed = jnp.concatenate(rows + [jnp.zeros((extra_rows, _LANES), _F32)], axis=0)
    return jnp.pad(packed, ((0, (-packed.shape[0]) % 16), (0, 0)))


def _unpack(packed, like):
    out, row = [], 0
    for a in like:
        size = int(np.prod(a.shape))
        nrows = -(-size // _LANES)
        out.append(packed[row:row + nrows].reshape(-1)[:size].reshape(a.shape))
        row += nrows
    return out, row


def _to_blocks(a):
    r, c = a.shape
    return a.reshape(r, N_DEV, c // N_DEV).transpose(1, 0, 2)


def _from_blocks(a):
    n, r, c = a.shape
    return a.transpose(1, 0, 2).reshape(r, n * c)


def kernel(x, positions, g_pre_mix, w_in, ssm_lambda_re, ssm_lambda_im, ssm_log_dt, ssm_b_re, ssm_b_im, ssm_c_re, ssm_c_im, ssm_d, w_glu, b_glu, attn_sinks, g_ssm_out, g_attn_out, w_out, g_post_mix, g_pre_ffn, w_gate_up, w_down, g_post_ffn, loss_target, m_g_pre_mix, m_w_in, m_ssm_lambda_re, m_ssm_lambda_im, m_ssm_log_dt, m_ssm_b_re, m_ssm_b_im, m_ssm_c_re, m_ssm_c_im, m_ssm_d, m_w_glu, m_b_glu, m_attn_sinks, m_g_ssm_out, m_g_attn_out, m_w_out, m_g_post_mix, m_g_pre_ffn, m_w_gate_up, m_w_down, m_g_post_ffn, v_g_pre_mix, v_w_in, v_ssm_lambda_re, v_ssm_lambda_im, v_ssm_log_dt, v_ssm_b_re, v_ssm_b_im, v_ssm_c_re, v_ssm_c_im, v_ssm_d, v_w_glu, v_b_glu, v_attn_sinks, v_g_ssm_out, v_g_attn_out, v_w_out, v_g_post_mix, v_g_pre_ffn, v_w_gate_up, v_w_down, v_g_post_ffn):
    w = dict(g_pre_mix=g_pre_mix, w_in=w_in, ssm_lambda_re=ssm_lambda_re, ssm_lambda_im=ssm_lambda_im,
             ssm_log_dt=ssm_log_dt, ssm_b_re=ssm_b_re, ssm_b_im=ssm_b_im, ssm_c_re=ssm_c_re, ssm_c_im=ssm_c_im,
             ssm_d=ssm_d, w_glu=w_glu, b_glu=b_glu, attn_sinks=attn_sinks, g_ssm_out=g_ssm_out,
             g_attn_out=g_attn_out, w_out=w_out, g_post_mix=g_post_mix, g_pre_ffn=g_pre_ffn,
             w_gate_up=w_gate_up, w_down=w_down, g_post_ffn=g_post_ffn)
    m = dict(g_pre_mix=m_g_pre_mix, w_in=m_w_in, ssm_lambda_re=m_ssm_lambda_re, ssm_lambda_im=m_ssm_lambda_im,
             ssm_log_dt=m_ssm_log_dt, ssm_b_re=m_ssm_b_re, ssm_b_im=m_ssm_b_im, ssm_c_re=m_ssm_c_re,
             ssm_c_im=m_ssm_c_im, ssm_d=m_ssm_d, w_glu=m_w_glu, b_glu=m_b_glu, attn_sinks=m_attn_sinks,
             g_ssm_out=m_g_ssm_out, g_attn_out=m_g_attn_out, w_out=m_w_out, g_post_mix=m_g_post_mix,
             g_pre_ffn=m_g_pre_ffn, w_gate_up=m_w_gate_up, w_down=m_w_down, g_post_ffn=m_g_post_ffn)
    v = dict(g_pre_mix=v_g_pre_mix, w_in=v_w_in, ssm_lambda_re=v_ssm_lambda_re, ssm_lambda_im=v_ssm_lambda_im,
             ssm_log_dt=v_ssm_log_dt, ssm_b_re=v_ssm_b_re, ssm_b_im=v_ssm_b_im, ssm_c_re=v_ssm_c_re,
             ssm_c_im=v_ssm_c_im, ssm_d=v_ssm_d, w_glu=v_w_glu, b_glu=v_b_glu, attn_sinks=v_attn_sinks,
             g_ssm_out=v_g_ssm_out, g_attn_out=v_g_attn_out, w_out=v_w_out, g_post_mix=v_g_post_mix,
             g_pre_ffn=v_g_pre_ffn, w_gate_up=v_w_gate_up, w_down=v_w_down, g_post_ffn=v_g_post_ffn)

    row_sharded = ("w_out", "w_down")
    gather = _send_start([w[n][0].astype(_BF16) for n in _BIG], False, "gather_start")
    gather_sends, gather_recvs, gather_srcs, gather_lands, gather_token = gather

    def fetch(names, after):
        idx = [_BIG.index(n) for n in names]
        pick = lambda seq: [seq[i] for i in idx]
        lands = _send_wait(pick(gather_sends), pick(gather_recvs), pick(gather_srcs), pick(gather_lands),
                           gather_token if after is None else after, False, "gather_wait_" + names[0])
        return [a.reshape(-1, a.shape[2]) if n in row_sharded else _from_blocks(a) for n, a in zip(names, lands)]

    sent = []

    def publish(named):
        names = list(named)
        blocks = [named[n].reshape(N_DEV, -1, named[n].shape[1]) if n in row_sharded else _to_blocks(named[n])
                  for n in names]
        started = _send_start(blocks, True, "grads_start_" + names[0])
        sent.append((names, started))
        return [started[4]]

    p = {n: w[n] for n in _SMALL}
    loss, grad_x, grads = _local_step(x[0], positions[0], loss_target[0], p, fetch, publish)

    result = {}
    for names, (sends, recvs, srcs, lands, _) in sent:
        parts = _send_wait(sends, recvs, srcs, lands, grad_x, True, "grads_wait_" + names[0])
        for name, part in zip(names, parts):
            result[name] = [a[None] for a in _adamw(part, w[name][0], m[name][0], v[name][0], "adamw_" + name)]

    small_grads = _pack([grads[n] for n in _SMALL] + [jnp.pad(loss.reshape(1), (0, _LANES - 1))], 0)
    gathered, = _all_gather([small_grads], _F32, "gather_small")
    packed = _adamw(gathered, _pack([w[n] for n in _SMALL], 1), _pack([m[n] for n in _SMALL], 1),
                    _pack([v[n] for n in _SMALL], 1), "adamw_small")
    loss_row = None
    for name in _SMALL:
        result[name] = []
    for arr in packed:
        vals, loss_row = _unpack(arr, [w[n] for n in _SMALL])
        for name, val in zip(_SMALL, vals):
            result[name].append(val)
    total_loss = packed[0][loss_row, 0]

    out = [total_loss, grad_x[None]]
    for kind in range(4):
        out += [result[n][kind] for n in _WEIGHTS]
    return tuple(out)
```

```python
import functools
import math

import numpy as np
import jax
import jax.numpy as jnp
from jax import lax
from jax.experimental import pallas as pl
from jax.experimental.pallas import tpu as pltpu

D_MODEL = 1024
SSM_WIDTH = 512
SSM_GROUP = 16
SSM_GROUPS = 32
SSM_STATE = 64
N_STATE = SSM_GROUPS * SSM_STATE
ATTN_WIDTH = 512
HEAD_DIM = 64
N_Q_HEADS = 8
N_KV_HEADS = 2
Q_PER_KV = 4
KV_WIDTH = 128
IN_WIDTH = 1280
BLOCK = 128
ROPE_DIM = 16
ROPE_THETA = 500000.0
D_FF = 2816
NORM_EPS = 1e-6
MASK_VALUE = -1e30
ADAM_LR = 0.001
ADAM_B1 = 0.9
ADAM_B2 = 0.999
ADAM_EPS = 1e-08
ADAM_WD = 0.01
ADAM_STEP = 10

N_DEV = 8
SCAN_CHUNKS = 8
SCAN_COLS = 512
TOKEN_TILE = 256
VMEM_LIMIT = 56 * 1024 * 1024

_F32 = jnp.float32
_BF16 = jnp.bfloat16
_MXU = jnp.bfloat16

_NN = ((1,), (0,))
_NT = ((1,), (1,))
_TN = ((0,), (0,))


def _dot(a, b, dims):
    return lax.dot_general(a.astype(_MXU), b.astype(_MXU), (dims, ((), ())),
                           preferred_element_type=_F32)


def _dot_exact(a, b, dims):
    return lax.dot_general(a.astype(_F32), b.astype(_F32), (dims, ((), ())),
                           precision=lax.Precision.HIGHEST, preferred_element_type=_F32)


def _iota(shape, dim):
    return lax.broadcasted_iota(jnp.int32, shape, dim)


def _rms_fwd(x, g):
    r = lax.rsqrt(jnp.mean(x * x, axis=-1, keepdims=True) + NORM_EPS)
    return x * r * g, r


def _rms_bwd(dy, x, g, r):
    a = dy * g
    xn = x * r
    dx = r * (a - xn * jnp.mean(a * xn, axis=-1, keepdims=True))
    dg = jnp.sum(dy * xn, axis=0, keepdims=True)
    return dx, dg


def _call(body, grid, in_specs, out_specs, out_shape, name, scratch=(), tokens=()):
    params = pltpu.CompilerParams(dimension_semantics=("arbitrary",) * len(grid),
                                  vmem_limit_bytes=VMEM_LIMIT)
    n_in, n_tok = len(in_specs), len(tokens)

    def run(*refs):
        return body(*refs[:n_in], *refs[n_in + n_tok:])

    call = pl.pallas_call(run, grid=grid,
                          in_specs=list(in_specs) + [pl.BlockSpec(memory_space=pl.ANY)] * n_tok,
                          out_specs=out_specs, out_shape=out_shape, scratch_shapes=list(scratch),
                          compiler_params=params, name=name)
    return lambda *args: call(*args, *tokens)


def _rows(tm, n):
    return pl.BlockSpec((tm, n), lambda i: (i, 0))


def _whole(shape):
    nd = len(shape)
    return pl.BlockSpec(shape, lambda i: (0,) * nd)


def _sds(shape, dtype):
    return jax.ShapeDtypeStruct(shape, dtype)


def _tile(L):
    return min(TOKEN_TILE, L)


def _accumulate(ref, val, first):
    @pl.when(first)
    def _():
        ref[...] = val

    @pl.when(jnp.logical_not(first))
    def _():
        ref[...] += val


def _rope_rows():
    half = ROPE_DIM // 2
    inv = (np.float32(ROPE_THETA) ** (-np.arange(half, dtype=np.float32) * np.float32(2.0) / np.float32(ROPE_DIM))).astype(np.float32)
    col = np.arange(KV_WIDTH) % HEAD_DIM
    freq = np.where(col < ROPE_DIM, inv[col % half], 0.0).astype(np.float32)
    sign = np.where(col < half, -1.0, np.where(col < ROPE_DIM, 1.0, 0.0)).astype(np.float32)
    return freq[None, :], sign[None, :]


def _rope_tables(pos_col):
    L = pos_col.shape[0]
    tm = _tile(L)
    freq, sign = _rope_rows()

    def body(pos_ref, freq_ref, sign_ref, cos_ref, sin_ref):
        ang = pos_ref[...].astype(_F32) * freq_ref[...]
        cos_ref[...] = jnp.cos(ang)
        sin_ref[...] = jnp.sin(ang) * sign_ref[...]

    return _call(body, (L // tm,),
                 [_rows(tm, 1), _whole((1, KV_WIDTH)), _whole((1, KV_WIDTH))],
                 [_rows(tm, KV_WIDTH), _rows(tm, KV_WIDTH)],
                 [_sds((L, KV_WIDTH), _F32)] * 2, "rope_tables")(pos_col, jnp.asarray(freq), jnp.asarray(sign))


def _widen(t, width):
    return t if width == KV_WIDTH else jnp.concatenate([t] * (width // KV_WIDTH), axis=1)


def _rope_partner(t):
    w = t.shape[1]
    in_head = _iota((1, w), 1) & (HEAD_DIM - 1)
    second = jnp.where(in_head < ROPE_DIM, pltpu.roll(t, ROPE_DIM // 2, 1), 0.0)
    return jnp.where(in_head < ROPE_DIM // 2, pltpu.roll(t, w - ROPE_DIM // 2, 1), second)


def _rope_apply(t, cos_t, sin_t):
    w = t.shape[1]
    return t * _widen(cos_t, w) + _rope_partner(t) * _widen(sin_t, w)


def _rope_transpose(dt, cos_t, sin_t):
    w = dt.shape[1]
    return dt * _widen(cos_t, w) + _rope_partner(dt * _widen(sin_t, w))


def _in_proj(x, g_pre_mix, w_in, cos_t, sin_t):
    L = x.shape[0]
    tm = _tile(L)

    def body(x_ref, g_ref, w_ref, cos_ref, sin_ref, hn_ref, u_ref, q_ref, k_ref, v_ref):
        hn, _ = _rms_fwd(x_ref[...], g_ref[...])
        hn = hn.astype(_BF16)
        hn_ref[...] = hn
        proj = _dot(hn, w_ref[...], _NN)
        u_ref[...] = proj[:, :SSM_WIDTH]
        q = proj[:, SSM_WIDTH:SSM_WIDTH + ATTN_WIDTH]
        k = proj[:, SSM_WIDTH + ATTN_WIDTH:SSM_WIDTH + ATTN_WIDTH + KV_WIDTH]
        cos_v, sin_v = cos_ref[...], sin_ref[...]
        q_ref[...] = _rope_apply(q, cos_v, sin_v).astype(_BF16)
        k_ref[...] = _rope_apply(k, cos_v, sin_v).astype(_BF16)
        v_ref[...] = proj[:, SSM_WIDTH + ATTN_WIDTH + KV_WIDTH:].astype(_BF16)

    return _call(body, (L // tm,),
                 [_rows(tm, D_MODEL), _whole((1, D_MODEL)), _whole((D_MODEL, IN_WIDTH)),
                  _rows(tm, KV_WIDTH), _rows(tm, KV_WIDTH)],
                 [_rows(tm, D_MODEL), _rows(tm, SSM_WIDTH), _rows(tm, ATTN_WIDTH),
                  _rows(tm, KV_WIDTH), _rows(tm, KV_WIDTH)],
                 [_sds((L, D_MODEL), _BF16), _sds((L, SSM_WIDTH), _F32), _sds((L, ATTN_WIDTH), _BF16),
                  _sds((L, KV_WIDTH), _BF16), _sds((L, KV_WIDTH), _BF16)],
                 "in_proj")(x, g_pre_mix, w_in, cos_t, sin_t)


def _s5_discretize(lam_re, lam_im, log_dt):
    lr = jnp.minimum(lam_re, -1e-4)
    li = lam_im
    dt = jnp.exp(log_dt)
    mag = jnp.exp(lr * dt)
    ar = mag * jnp.cos(li * dt)
    ai = mag * jnp.sin(li * dt)
    den = lr * lr + li * li
    fr = ((ar - 1.0) * lr + ai * li) / den
    fi = (ai * lr - (ar - 1.0) * li) / den
    return ar, ai, fr, fi


def _s5_bbar(lam_re, lam_im, log_dt, b_re, b_im):
    ar, ai, fr, fi = _s5_discretize(lam_re, lam_im, log_dt)
    return ar, ai, fr * b_re - fi * b_im, fr * b_im + fi * b_re


def _spread_masks():
    e16 = (_iota((SSM_GROUP, SSM_WIDTH), 1) & (SSM_GROUP - 1)) == _iota((SSM_GROUP, SSM_WIDTH), 0)
    e64 = (_iota((SSM_STATE, N_STATE), 1) & (SSM_STATE - 1)) == _iota((SSM_STATE, N_STATE), 0)
    mask_b = (_iota((N_STATE, SSM_WIDTH), 0) >> 6) == (_iota((N_STATE, SSM_WIDTH), 1) >> 4)
    mask_c = (_iota((SSM_WIDTH, N_STATE), 0) >> 4) == (_iota((SSM_WIDTH, N_STATE), 1) >> 6)
    return e16.astype(_F32), e64.astype(_F32), mask_b, mask_c


SUPER = 4
SB_STATE = N_STATE // SUPER
SB_WIDTH = SSM_WIDTH // SUPER


def _sb_state(k):
    return slice(SB_STATE * k, SB_STATE * (k + 1))


def _sb_width(k):
    return slice(SB_WIDTH * k, SB_WIDTH * (k + 1))


def _ssm_prep(lam_re_r, lam_im_r, ldt_r, lam_re_c, lam_im_c, ldt_c, b_re2, b_im2, c_re2, c_im2):
    def body(lrr, lir, ldr, lrc, lic, ldc, bre, bim, cre, cim, ar_ref, ai_ref, btr, bti, ctr, cti):
        ar, ai, _, _ = _s5_discretize(lrr[...], lir[...], ldr[...])
        ar_ref[...] = ar
        ai_ref[...] = ai
        _, _, bbr, bbi = _s5_bbar(lrc[...], lic[...], ldc[...], bre[...], bim[...])
        e16, e64, mask_b, mask_c = _spread_masks()

        def fold_b(bb):
            full = jnp.where(mask_b, _dot(bb, e16, _NN), 0.0)
            return sum(full[:, _sb_width(k)] for k in range(SUPER)).astype(_BF16)

        def fold_c(cc):
            full = jnp.where(mask_c, _dot(cc, e64, _NN), 0.0)
            return sum(full[_sb_width(k), :] for k in range(SUPER)).astype(_BF16)

        btr[...] = fold_b(bbr)
        bti[...] = fold_b(bbi)
        ctr[...] = fold_c(cre[...])
        cti[...] = fold_c(cim[...])

    row = (1, N_STATE)
    ins = [lam_re_r, lam_im_r, ldt_r, lam_re_c, lam_im_c, ldt_c, b_re2, b_im2, c_re2, c_im2]
    return _call(body, (1,), [_whole(a.shape) for a in ins],
                 [_whole(row), _whole(row), _whole((N_STATE, SB_WIDTH)), _whole((N_STATE, SB_WIDTH)),
                  _whole((SB_WIDTH, N_STATE)), _whole((SB_WIDTH, N_STATE))],
                 [_sds(row, _F32), _sds(row, _F32), _sds((N_STATE, SB_WIDTH), _BF16),
                  _sds((N_STATE, SB_WIDTH), _BF16), _sds((SB_WIDTH, N_STATE), _BF16),
                  _sds((SB_WIDTH, N_STATE), _BF16)], "ssm_prep")(*ins)


def _ssm_bu(u, bt_re, bt_im):
    L = u.shape[0]
    tm = _tile(L)

    def body(u_ref, br_ref, bi_ref, or_ref, oi_ref):
        for k in range(SUPER):
            ub = u_ref[:, _sb_width(k)].astype(_BF16)
            or_ref[:, _sb_state(k)] = _dot(ub, br_ref[_sb_state(k), :], _NT)
            oi_ref[:, _sb_state(k)] = _dot(ub, bi_ref[_sb_state(k), :], _NT)

    return _call(body, (L // tm,),
                 [_rows(tm, SSM_WIDTH), _whole((N_STATE, SB_WIDTH)), _whole((N_STATE, SB_WIDTH))],
                 [_rows(tm, N_STATE), _rows(tm, N_STATE)],
                 [_sds((L, N_STATE), _F32)] * 2, "ssm_bu")(u, bt_re, bt_im)


def _complex_power(ar, ai, n):
    def step(_, c):
        pr, pi = c
        return pr * ar - pi * ai, pr * ai + pi * ar
    return lax.fori_loop(0, n, step, (jnp.ones_like(ar), jnp.zeros_like(ai)))


def _chunk_carries(er, ei, pr, pi, reverse):
    rows = _iota(er.shape, 0)
    sr = jnp.zeros_like(pr)
    si = jnp.zeros_like(pi)
    out_r = jnp.zeros_like(er)
    out_i = jnp.zeros_like(ei)
    order = range(SCAN_CHUNKS - 1, 0, -1) if reverse else range(SCAN_CHUNKS - 1)
    for c in order:
        e_r = er[c:c + 1, :]
        e_i = ei[c:c + 1, :]
        sr, si = pr * sr - pi * si + e_r, pr * si + pi * sr + e_i
        nxt = c - 1 if reverse else c + 1
        out_r = jnp.where(rows == nxt, sr, out_r)
        out_i = jnp.where(rows == nxt, si, out_i)
    return out_r, out_i


def _scan_fwd(b_re, b_im, a_re, a_im):
    T = b_re.shape[0]
    W = SCAN_COLS
    blk = pl.BlockSpec((T, SCAN_CHUNKS, W), lambda j: (0, 0, j))
    vec = pl.BlockSpec((1, W), lambda j: (0, j))

    def body(br_ref, bi_ref, ar_ref, ai_ref, xr_ref, xi_ref):
        ar, ai = ar_ref[...], ai_ref[...]
        ar8 = jnp.broadcast_to(ar, (SCAN_CHUNKS, W))
        ai8 = jnp.broadcast_to(ai, (SCAN_CHUNKS, W))

        def local(t, c):
            cr, ci = c
            return ar8 * cr - ai8 * ci + br_ref[t], ar8 * ci + ai8 * cr + bi_ref[t]

        zero = jnp.zeros((SCAN_CHUNKS, W), _F32)
        er, ei = lax.fori_loop(0, T, local, (zero, zero))
        pr, pi = _complex_power(ar, ai, T)
        sr, si = _chunk_carries(er, ei, pr, pi, reverse=False)

        def final(t, c):
            nr, ni = local(t, c)
            xr_ref[t] = nr
            xi_ref[t] = ni
            return nr, ni

        lax.fori_loop(0, T, final, (sr, si))

    shape = _sds(b_re.shape, _F32)
    return _call(body, (N_STATE // W,), [blk, blk, vec, vec], [blk, blk], [shape, shape],
                 "scan_fwd")(b_re, b_im, a_re, a_im)


def _scan_bwd(dx_re, dx_im, x_re, x_im, a_re, a_im, tokens=()):
    T = dx_re.shape[0]
    W = SCAN_COLS
    blk = pl.BlockSpec((T, SCAN_CHUNKS, W), lambda j: (0, 0, j))
    vec = pl.BlockSpec((1, W), lambda j: (0, j))

    def body(dr_ref, di_ref, xr_ref, xi_ref, ar_ref, ai_ref, lr_ref, li_ref, dar_ref, dai_ref):
        ar, ai = ar_ref[...], ai_ref[...]
        ar8 = jnp.broadcast_to(ar, (SCAN_CHUNKS, W))
        ai8 = jnp.broadcast_to(ai, (SCAN_CHUNKS, W))

        def local(t, c):
            cr, ci = c
            return ar8 * cr + ai8 * ci + dr_ref[t], ar8 * ci - ai8 * cr + di_ref[t]

        zero = jnp.zeros((SCAN_CHUNKS, W), _F32)
        er, ei = lax.fori_loop(0, T, lambda k, c: local(T - 1 - k, c), (zero, zero))
        pr, pi = _complex_power(ar, -ai, T)
        sr, si = _chunk_carries(er, ei, pr, pi, reverse=True)

        def grad_a(acc, nr, ni, xpr, xpi):
            return acc[0] + nr * xpr + ni * xpi, acc[1] + ni * xpr - nr * xpi

        def final(k, c):
            t = T - 1 - k
            nr, ni = local(t, c[:2])
            lr_ref[t] = nr
            li_ref[t] = ni
            gr, gi = grad_a(c[2:], nr, ni, xr_ref[t - 1], xi_ref[t - 1])
            return nr, ni, gr, gi

        cr, ci, gr, gi = lax.fori_loop(0, T - 1, final, (sr, si, zero, zero))
        nr, ni = local(0, (cr, ci))
        lr_ref[0] = nr
        li_ref[0] = ni
        first = _iota((SCAN_CHUNKS, W), 0) == 0
        xpr = jnp.where(first, 0.0, pltpu.roll(xr_ref[T - 1], 1, 0))
        xpi = jnp.where(first, 0.0, pltpu.roll(xi_ref[T - 1], 1, 0))
        gr, gi = grad_a((gr, gi), nr, ni, xpr, xpi)
        dar_ref[...] = jnp.sum(gr, axis=0, keepdims=True)
        dai_ref[...] = jnp.sum(gi, axis=0, keepdims=True)

    shape = _sds(dx_re.shape, _F32)
    row = _sds((1, N_STATE), _F32)
    return _call(body, (N_STATE // W,), [blk, blk, blk, blk, vec, vec], [blk, blk, vec, vec],
                 [shape, shape, row, row], "scan_bwd", tokens=tokens)(dx_re, dx_im, x_re, x_im, a_re, a_im)


_GELU_K = math.sqrt(2.0 / math.pi)
_GELU_C = 0.044715


def _gelu(y):
    return 0.5 * y * (1.0 + jnp.tanh(_GELU_K * (y + _GELU_C * y * y * y)))


def _gelu_grad(y):
    t = jnp.tanh(_GELU_K * (y + _GELU_C * y * y * y))
    return 0.5 * (1.0 + t) + 0.5 * y * (1.0 - t * t) * _GELU_K * (1.0 + 3.0 * _GELU_C * y * y)


def _ssm_out(x_re, x_im, u, ct_re, ct_im, d_row, w_glu, b_glu, g_ssm):
    L = u.shape[0]
    tm = _tile(L)

    def body(xr_ref, xi_ref, u_ref, cr_ref, ci_ref, d_ref, w_ref, b_ref, g_ref, y_ref, z_ref, n_ref):
        cx = [_dot(xr_ref[:, _sb_state(k)], cr_ref[:, _sb_state(k)], _NT)
              - _dot(xi_ref[:, _sb_state(k)], ci_ref[:, _sb_state(k)], _NT) for k in range(SUPER)]
        y = jnp.concatenate(cx, axis=1) + d_ref[...] * u_ref[...]
        y_ref[...] = y
        z = _dot(_gelu(y), w_ref[...], _NN) + b_ref[...]
        z_ref[...] = z
        out = z[:, :SSM_WIDTH] * jax.nn.sigmoid(z[:, SSM_WIDTH:])
        n, _ = _rms_fwd(out, g_ref[...])
        n_ref[...] = n.astype(_BF16)

    return _call(body, (L // tm,),
                 [_rows(tm, N_STATE), _rows(tm, N_STATE), _rows(tm, SSM_WIDTH),
                  _whole((SB_WIDTH, N_STATE)), _whole((SB_WIDTH, N_STATE)), _whole((1, SSM_WIDTH)),
                  _whole((SSM_WIDTH, 2 * SSM_WIDTH)), _whole((1, 2 * SSM_WIDTH)), _whole((1, SSM_WIDTH))],
                 [_rows(tm, SSM_WIDTH), _rows(tm, 2 * SSM_WIDTH), _rows(tm, SSM_WIDTH)],
                 [_sds((L, SSM_WIDTH), _F32), _sds((L, 2 * SSM_WIDTH), _F32), _sds((L, SSM_WIDTH), _BF16)],
                 "ssm_out")(x_re, x_im, u, ct_re, ct_im, d_row, w_glu, b_glu, g_ssm)


def _ssm_out_bwd(dn, y, z, u, ct_re, ct_im, d_row, w_glu, g_ssm):
    L = u.shape[0]
    tm = _tile(L)

    def body(dn_ref, y_ref, z_ref, u_ref, cr_ref, ci_ref, d_ref, w_ref, g_ref,
             gy_ref, dz_ref, dy_ref, dud_ref, dxr_ref, dxi_ref, dg_ref, db_ref, dd_ref):
        first = pl.program_id(0) == 0
        z = z_ref[...]
        z1, z2 = z[:, :SSM_WIDTH], z[:, SSM_WIDTH:]
        sig = jax.nn.sigmoid(z2)
        out = z1 * sig
        g = g_ref[...]
        _, r = _rms_fwd(out, g)
        dout, dg = _rms_bwd(dn_ref[...], out, g, r)
        _accumulate(dg_ref, dg, first)
        dz = jnp.concatenate([dout * sig, dout * z1 * sig * (1.0 - sig)], axis=1)
        _accumulate(db_ref, jnp.sum(dz, axis=0, keepdims=True), first)
        dzb = dz.astype(_BF16)
        dz_ref[...] = dzb
        y = y_ref[...]
        gy_ref[...] = _gelu(y).astype(_BF16)
        dy = _dot(dzb, w_ref[...], _NT) * _gelu_grad(y)
        u = u_ref[...]
        _accumulate(dd_ref, jnp.sum(dy * u, axis=0, keepdims=True), first)
        dud_ref[...] = d_ref[...] * dy
        dyb = dy.astype(_BF16)
        dy_ref[...] = dyb
        for k in range(SUPER):
            dxr_ref[:, _sb_state(k)] = _dot(dyb[:, _sb_width(k)], cr_ref[:, _sb_state(k)], _NN)
            dxi_ref[:, _sb_state(k)] = -_dot(dyb[:, _sb_width(k)], ci_ref[:, _sb_state(k)], _NN)

    row = _whole((1, SSM_WIDTH))
    return _call(body, (L // tm,),
                 [_rows(tm, SSM_WIDTH), _rows(tm, SSM_WIDTH), _rows(tm, 2 * SSM_WIDTH), _rows(tm, SSM_WIDTH),
                  _whole((SB_WIDTH, N_STATE)), _whole((SB_WIDTH, N_STATE)), row,
                  _whole((SSM_WIDTH, 2 * SSM_WIDTH)), row],
                 [_rows(tm, SSM_WIDTH), _rows(tm, 2 * SSM_WIDTH), _rows(tm, SSM_WIDTH), _rows(tm, SSM_WIDTH),
                  _rows(tm, N_STATE), _rows(tm, N_STATE), row, _whole((1, 2 * SSM_WIDTH)), row],
                 [_sds((L, SSM_WIDTH), _BF16), _sds((L, 2 * SSM_WIDTH), _BF16), _sds((L, SSM_WIDTH), _BF16),
                  _sds((L, SSM_WIDTH), _F32), _sds((L, N_STATE), _F32), _sds((L, N_STATE), _F32),
                  _sds((1, SSM_WIDTH), _F32), _sds((1, 2 * SSM_WIDTH), _F32), _sds((1, SSM_WIDTH), _F32)],
                 "ssm_out_bwd")(dn, y, z, u, ct_re, ct_im, d_row, w_glu, g_ssm)


def _ssm_du(lam_re, lam_im, bt_re, bt_im, dud):
    L = dud.shape[0]
    tm = _tile(L)

    def body(lr_ref, li_ref, br_ref, bi_ref, dud_ref, du_ref):
        for k in range(SUPER):
            du_ref[:, _sb_width(k)] = (_dot(lr_ref[:, _sb_state(k)], br_ref[_sb_state(k), :], _NN)
                                       + _dot(li_ref[:, _sb_state(k)], bi_ref[_sb_state(k), :], _NN)
                                       + dud_ref[:, _sb_width(k)])

    return _call(body, (L // tm,),
                 [_rows(tm, N_STATE), _rows(tm, N_STATE), _whole((N_STATE, SB_WIDTH)),
                  _whole((N_STATE, SB_WIDTH)), _rows(tm, SSM_WIDTH)],
                 _rows(tm, SSM_WIDTH), _sds((L, SSM_WIDTH), _F32), "ssm_du")(lam_re, lam_im, bt_re, bt_im, dud)


def _ssm_weight_grads(dy, x_re, x_im, lam_re, lam_im, u):
    L = u.shape[0]

    def body(dy_ref, xr_ref, xi_ref, lr_ref, li_ref, u_ref, dcr_ref, dci_ref, dbr_ref, dbi_ref):
        dyb = dy_ref[...]
        ub = u_ref[...].astype(_BF16)
        dcr_ref[...] = _dot(dyb, xr_ref[...], _TN)
        dci_ref[...] = _dot(dyb, xi_ref[...], _TN)
        dbr_ref[...] = _dot(lr_ref[...], ub, _TN)
        dbi_ref[...] = _dot(li_ref[...], ub, _TN)

    width = pl.BlockSpec((L, SB_WIDTH), lambda k: (0, k))
    state = pl.BlockSpec((L, SB_STATE), lambda k: (0, k))
    out_c = pl.BlockSpec((SB_WIDTH, SB_STATE), lambda k: (0, k))
    out_b = pl.BlockSpec((SB_STATE, SB_WIDTH), lambda k: (k, 0))
    return _call(body, (SUPER,), [width, state, state, state, state, width], [out_c, out_c, out_b, out_b],
                 [_sds((SB_WIDTH, N_STATE), _F32)] * 2 + [_sds((N_STATE, SB_WIDTH), _F32)] * 2,
                 "ssm_weight_grads")(dy, x_re, x_im, lam_re, lam_im, u)


def _ssm_param_bwd(da_re_c, da_im_c, dbt_re, dbt_im, dct_re, dct_im,
                   lam_re_c, lam_im_c, ldt_c, b_re2, b_im2):
    def body(dar, dai, dbr, dbi, dcr, dci, lrc, lic, ldc, bre, bim,
             glr, gli, gdt, gbr, gbi, gcr, gci):
        own_b = ((_iota((N_STATE, SB_WIDTH), 0) >> 6) & 7) == (_iota((N_STATE, SB_WIDTH), 1) >> 4)
        own_c = (_iota((SB_WIDTH, SB_STATE), 0) >> 4) == (_iota((SB_WIDTH, SB_STATE), 1) >> 6)

        def fold_b(ref):
            t = jnp.where(own_b, ref[...], 0.0)
            for shift in (64, 32, 16):
                t = t + pltpu.roll(t, shift, 1)
            return t[:, :SSM_GROUP]

        def fold_c(ref, k):
            t = jnp.where(own_c, ref[:, _sb_state(k)], 0.0)
            t = sum(t[:, 128 * i:128 * (i + 1)] for i in range(SB_STATE // 128))
            return (t + pltpu.roll(t, SSM_STATE, 1))[:, :SSM_STATE]

        dbbr = fold_b(dbr)
        dbbi = fold_b(dbi)
        for k in range(SUPER):
            gcr[_sb_width(k), :] = fold_c(dcr, k)
            gci[_sb_width(k), :] = -fold_c(dci, k)
        _, vjp = jax.vjp(_s5_bbar, lrc[...], lic[...], ldc[...], bre[...], bim[...])
        d_lr, d_li, d_dt, d_br, d_bi = vjp((dar[...], dai[...], dbbr, dbbi))
        glr[...] = d_lr
        gli[...] = d_li
        gbr[...] = d_br
        gbi[...] = d_bi
        groups = (_iota((SSM_GROUPS, N_STATE), 1) >> 6) == _iota((SSM_GROUPS, N_STATE), 0)
        gdt[...] = _dot_exact(groups.astype(_F32), jnp.broadcast_to(d_dt, (N_STATE, 128)), _NN)

    col = (N_STATE, 1)
    ins = [da_re_c, da_im_c, dbt_re, dbt_im, dct_re, dct_im, lam_re_c, lam_im_c, ldt_c, b_re2, b_im2]
    outs = [col, col, (SSM_GROUPS, 128), (N_STATE, SSM_GROUP), (N_STATE, SSM_GROUP),
            (SSM_WIDTH, SSM_STATE), (SSM_WIDTH, SSM_STATE)]
    return _call(body, (1,), [_whole(a.shape) for a in ins], [_whole(s) for s in outs],
                 [_sds(s, _F32) for s in outs], "ssm_param_bwd")(*ins)


def _head_spread(j):
    r = _iota((KV_WIDTH, 256), 0)
    c = _iota((KV_WIDTH, 256), 1)
    return (r == HEAD_DIM * j + (c & (HEAD_DIM - 1))).astype(_BF16)


STACK = Q_PER_KV * BLOCK


def _stack_heads(t):
    lane_head = _iota((1, 256), 1) >> 6
    return jnp.concatenate([jnp.where(lane_head == g, t, jnp.zeros_like(t)) for g in range(Q_PER_KV)], axis=0)


def _unstack_heads(t):
    lane_head = _iota((1, 256), 1) >> 6
    return sum(jnp.where(lane_head == g, t[BLOCK * g:BLOCK * (g + 1)], 0.0) for g in range(Q_PER_KV))


def _stacked_sinks(sink_ref, j):
    block = _iota((STACK, 1), 0) >> 7
    col = jnp.full((STACK, 1), sink_ref[Q_PER_KV * j], _F32)
    for g in range(1, Q_PER_KV):
        col = jnp.where(block == g, sink_ref[Q_PER_KV * j + g], col)
    return col


def _fold_heads(t, j):
    t = t[:, :KV_WIDTH] + t[:, KV_WIDTH:]
    t = t + pltpu.roll(t, HEAD_DIM, 1)
    return jnp.where((_iota((1, KV_WIDTH), 1) >> 6) == j, t, 0.0)


def _attn_scores(q_stacked, kt, blk, sink):
    s = _dot(q_stacked, kt, _NT) * (HEAD_DIM ** -0.5)
    qi = _iota((STACK, 2 * BLOCK), 0) & (BLOCK - 1)
    kj = _iota((STACK, 2 * BLOCK), 1)
    rel = qi + BLOCK - kj
    valid = (rel >= 0) & (rel < BLOCK) & (blk * BLOCK - BLOCK + kj >= 0)
    s = jnp.where(valid, s, MASK_VALUE)
    m = jnp.maximum(jnp.max(s, axis=-1, keepdims=True), sink)
    p = jnp.exp(s - m)
    e_sink = jnp.exp(sink - m)
    den = jnp.sum(p, axis=-1, keepdims=True) + e_sink
    return p / den, e_sink / den


def _attn_specs():
    prev = lambda i: (jnp.maximum(i - 1, 0), 0)
    cur = lambda i: (i, 0)
    kv = [pl.BlockSpec((BLOCK, KV_WIDTH), prev), pl.BlockSpec((BLOCK, KV_WIDTH), cur)]
    return [pl.BlockSpec((BLOCK, ATTN_WIDTH), cur)] + kv + kv


def _attn_fwd(q, k, v, sinks, g_attn):
    L = q.shape[0]

    def body(q_ref, kp_ref, kc_ref, vp_ref, vc_ref, sink_ref, g_ref, o_ref, n_ref):
        blk = pl.program_id(0)
        kwin = jnp.concatenate([kp_ref[...], kc_ref[...]], axis=0)
        vwin = jnp.concatenate([vp_ref[...], vc_ref[...]], axis=0)
        halves = []
        for j in range(N_KV_HEADS):
            spread = _head_spread(j)
            kt = _dot(kwin, spread, _NN).astype(_BF16)
            vt = _dot(vwin, spread, _NN).astype(_BF16)
            qs = _stack_heads(q_ref[:, 256 * j:256 * (j + 1)])
            p, _ = _attn_scores(qs, kt, blk, _stacked_sinks(sink_ref, j))
            halves.append(_unstack_heads(_dot(p, vt, _NN)))
        o = jnp.concatenate(halves, axis=1)
        o_ref[...] = o
        n, _ = _rms_fwd(o, g_ref[...])
        n_ref[...] = n.astype(_BF16)

    cur = lambda i: (i, 0)
    return _call(body, (L // BLOCK,),
                 _attn_specs() + [pl.BlockSpec(memory_space=pltpu.SMEM), _whole((1, ATTN_WIDTH))],
                 [pl.BlockSpec((BLOCK, ATTN_WIDTH), cur)] * 2,
                 [_sds((L, ATTN_WIDTH), _F32), _sds((L, ATTN_WIDTH), _BF16)],
                 "attn_fwd")(q, k, k, v, v, sinks, g_attn)


def _attn_bwd(q, k, v, o, dn, sinks, g_attn):
    L = q.shape[0]

    def body(q_ref, kp_ref, kc_ref, vp_ref, vc_ref, o_ref, dn_ref, sink_ref, g_ref,
             dq_ref, dk_ref, dv_ref, dsink_ref, dg_ref):
        blk = pl.program_id(0)
        first = blk == 0

        @pl.when(first)
        def _():
            dk_ref[...] = jnp.zeros_like(dk_ref)
            dv_ref[...] = jnp.zeros_like(dv_ref)
            dsink_ref[...] = jnp.zeros_like(dsink_ref)

        o = o_ref[...]
        g = g_ref[...]
        _, r = _rms_fwd(o, g)
        do, dg = _rms_bwd(dn_ref[...], o, g, r)
        _accumulate(dg_ref, dg, first)
        kwin = jnp.concatenate([kp_ref[...], kc_ref[...]], axis=0)
        vwin = jnp.concatenate([vp_ref[...], vc_ref[...]], axis=0)
        lane = _iota((1, 128), 1)
        dsink = jnp.zeros((1, 128), _F32)
        dkwin = jnp.zeros((2 * BLOCK, KV_WIDTH), _F32)
        dvwin = jnp.zeros((2 * BLOCK, KV_WIDTH), _F32)
        dq_halves = []
        for j in range(N_KV_HEADS):
            spread = _head_spread(j)
            kt = _dot(kwin, spread, _NN).astype(_BF16)
            vt = _dot(vwin, spread, _NN).astype(_BF16)
            qs = _stack_heads(q_ref[:, 256 * j:256 * (j + 1)])
            dos = _stack_heads(do[:, 256 * j:256 * (j + 1)]).astype(_BF16)
            p, p_sink = _attn_scores(qs, kt, blk, _stacked_sinks(sink_ref, j))
            dp = _dot(dos, vt, _NT)
            delta = jnp.sum(p * dp, axis=-1, keepdims=True)
            ds = (p * (dp - delta) * (HEAD_DIM ** -0.5)).astype(_BF16)
            sink_term = p_sink * delta
            for g in range(Q_PER_KV):
                head_sum = jnp.sum(sink_term[BLOCK * g:BLOCK * (g + 1)], axis=0, keepdims=True)
                dsink = dsink - jnp.where(lane == Q_PER_KV * j + g, head_sum, 0.0)
            dvwin = dvwin + _fold_heads(_dot(p, dos, _TN), j)
            dkwin = dkwin + _fold_heads(_dot(ds, qs, _TN), j)
            dq_halves.append(_unstack_heads(_dot(ds, kt, _NN)))
        dq_ref[...] = jnp.concatenate(dq_halves, axis=1)
        dsink_ref[...] += dsink
        prev = pl.ds(pl.multiple_of(jnp.maximum(blk - 1, 0) * BLOCK, BLOCK), BLOCK)
        cur = pl.ds(pl.multiple_of(blk * BLOCK, BLOCK), BLOCK)
        dk_ref[prev, :] += dkwin[:BLOCK]
        dk_ref[cur, :] += dkwin[BLOCK:]
        dv_ref[prev, :] += dvwin[:BLOCK]
        dv_ref[cur, :] += dvwin[BLOCK:]

    cur = lambda i: (i, 0)
    blk_q = pl.BlockSpec((BLOCK, ATTN_WIDTH), cur)
    return _call(body, (L // BLOCK,),
                 _attn_specs() + [blk_q, blk_q, pl.BlockSpec(memory_space=pltpu.SMEM), _whole((1, ATTN_WIDTH))],
                 [blk_q, _whole((L, KV_WIDTH)), _whole((L, KV_WIDTH)), _whole((1, 128)), _whole((1, ATTN_WIDTH))],
                 [_sds((L, ATTN_WIDTH), _F32), _sds((L, KV_WIDTH), _F32), _sds((L, KV_WIDTH), _F32),
                  _sds((1, 128), _F32), _sds((1, ATTN_WIDTH), _F32)],
                 "attn_bwd")(q, k, k, v, v, o, dn, sinks, g_attn)


def _out_proj(n_ssm, n_attn, x, w_out, g_post_mix, g_pre_ffn):
    L = x.shape[0]
    tm = _tile(L)

    def body(ns_ref, na_ref, x_ref, w_ref, g1_ref, g2_ref, merged_ref, mo_ref, h1_ref, hn2_ref):
        merged = jnp.concatenate([ns_ref[...], na_ref[...]], axis=1)
        merged_ref[...] = merged
        mo = _dot(merged, w_ref[...], _NN)
        mo_ref[...] = mo
        n, _ = _rms_fwd(mo, g1_ref[...])
        h1 = x_ref[...] + n
        h1_ref[...] = h1
        hn2, _ = _rms_fwd(h1, g2_ref[...])
        hn2_ref[...] = hn2.astype(_BF16)

    row = _whole((1, D_MODEL))
    return _call(body, (L // tm,),
                 [_rows(tm, SSM_WIDTH), _rows(tm, ATTN_WIDTH), _rows(tm, D_MODEL), _whole((D_MODEL, D_MODEL)), row, row],
                 [_rows(tm, D_MODEL)] * 4,
                 [_sds((L, D_MODEL), _BF16), _sds((L, D_MODEL), _F32), _sds((L, D_MODEL), _F32), _sds((L, D_MODEL), _BF16)],
                 "out_proj")(n_ssm, n_attn, x, w_out, g_post_mix, g_pre_ffn)


def _ffn(hn2, h1, target, w_gate_up, w_down, g_pre_ffn, g_post_ffn):
    L = h1.shape[0]
    tm = _tile(L)
    half = D_FF // 2

    def body(hn2_ref, h1_ref, tgt_ref, wgu_hbm, wd_hbm, g2_ref, g3_ref,
             act_ref, dgu_ref, dff_ref, dh1_ref, loss_ref, dg3_ref, dg2_ref,
             wgu, wd, gu, sem):
        first = pl.program_id(0) == 0

        @pl.when(first)
        def _():
            c1 = pltpu.make_async_copy(wgu_hbm, wgu, sem.at[0])
            c2 = pltpu.make_async_copy(wd_hbm, wd, sem.at[1])
            c1.start()
            c2.start()
            c1.wait()
            c2.wait()

        hn2 = hn2_ref[...]
        ff = jnp.zeros((tm, D_MODEL), _F32)
        for c in range(2):
            gate = _dot(hn2, wgu[:, half * c:half * (c + 1)], _NN)
            up = _dot(hn2, wgu[:, D_FF + half * c:D_FF + half * (c + 1)], _NN)
            gu[:, half * c:half * (c + 1)] = gate
            gu[:, D_FF + half * c:D_FF + half * (c + 1)] = up
            act = (gate * jax.nn.sigmoid(gate) * up).astype(_BF16)
            act_ref[:, half * c:half * (c + 1)] = act
            ff = ff + _dot(act, wd[half * c:half * (c + 1), :], _NN)
        g3 = g3_ref[...]
        n, r = _rms_fwd(ff, g3)
        h1 = h1_ref[...]
        err = h1 + n - tgt_ref[...]
        loss = 0.5 * jnp.sum(jnp.mean(err * err, axis=-1, keepdims=True), axis=0, keepdims=True)
        _accumulate(loss_ref, jnp.broadcast_to(loss, (1, 128)), first)
        dh2 = err * (1.0 / D_MODEL)
        dff, dg3 = _rms_bwd(dh2, ff, g3, r)
        _accumulate(dg3_ref, dg3, first)
        dffb = dff.astype(_BF16)
        dff_ref[...] = dffb
        dhn2 = jnp.zeros((tm, D_MODEL), _F32)
        for c in range(2):
            dact = _dot(dffb, wd[half * c:half * (c + 1), :], _NT)
            gate = gu[:, half * c:half * (c + 1)]
            up = gu[:, D_FF + half * c:D_FF + half * (c + 1)]
            sig = jax.nn.sigmoid(gate)
            silu = gate * sig
            dgate = (dact * up * (sig + silu * (1.0 - sig))).astype(_BF16)
            dup = (dact * silu).astype(_BF16)
            dgu_ref[:, half * c:half * (c + 1)] = dgate
            dgu_ref[:, D_FF + half * c:D_FF + half * (c + 1)] = dup
            dhn2 = dhn2 + _dot(dgate, wgu[:, half * c:half * (c + 1)], _NT)
            dhn2 = dhn2 + _dot(dup, wgu[:, D_FF + half * c:D_FF + half * (c + 1)], _NT)
        g2 = g2_ref[...]
        _, r2 = _rms_fwd(h1, g2)
        dh1, dg2 = _rms_bwd(dhn2, h1, g2, r2)
        _accumulate(dg2_ref, dg2, first)
        dh1_ref[...] = dh2 + dh1

    row = _whole((1, D_MODEL))
    anyspace = pl.BlockSpec(memory_space=pl.ANY)
    return _call(body, (L // tm,),
                 [_rows(tm, D_MODEL), _rows(tm, D_MODEL), _rows(tm, D_MODEL), anyspace, anyspace, row, row],
                 [_rows(tm, D_FF), _rows(tm, 2 * D_FF), _rows(tm, D_MODEL), _rows(tm, D_MODEL),
                  _whole((1, 128)), row, row],
                 [_sds((L, D_FF), _BF16), _sds((L, 2 * D_FF), _BF16), _sds((L, D_MODEL), _BF16),
                  _sds((L, D_MODEL), _F32), _sds((1, 128), _F32), _sds((1, D_MODEL), _F32), _sds((1, D_MODEL), _F32)],
                 "ffn",
                 scratch=[pltpu.VMEM((D_MODEL, 2 * D_FF), _BF16), pltpu.VMEM((D_FF, D_MODEL), _BF16),
                          pltpu.VMEM((tm, 2 * D_FF), _F32), pltpu.SemaphoreType.DMA((2,))],
                 )(hn2, h1, target, w_gate_up, w_down, g_pre_ffn, g_post_ffn)


def _out_proj_bwd(dh1, mo, w_out, g_post_mix, tokens=()):
    L = dh1.shape[0]
    tm = _tile(L)

    def body(dh1_ref, mo_ref, w_ref, g_ref, dmo_ref, dns_ref, dna_ref, dg_ref):
        first = pl.program_id(0) == 0
        mo = mo_ref[...]
        g = g_ref[...]
        _, r = _rms_fwd(mo, g)
        dmo, dg = _rms_bwd(dh1_ref[...], mo, g, r)
        _accumulate(dg_ref, dg, first)
        dmob = dmo.astype(_BF16)
        dmo_ref[...] = dmob
        dmerged = _dot(dmob, w_ref[...], _NT)
        dns_ref[...] = dmerged[:, :SSM_WIDTH]
        dna_ref[...] = dmerged[:, SSM_WIDTH:]

    row = _whole((1, D_MODEL))
    return _call(body, (L // tm,),
                 [_rows(tm, D_MODEL), _rows(tm, D_MODEL), _whole((D_MODEL, D_MODEL)), row],
                 [_rows(tm, D_MODEL), _rows(tm, SSM_WIDTH), _rows(tm, ATTN_WIDTH), row],
                 [_sds((L, D_MODEL), _BF16), _sds((L, SSM_WIDTH), _F32), _sds((L, ATTN_WIDTH), _F32),
                  _sds((1, D_MODEL), _F32)],
                 "out_proj_bwd", tokens=tokens)(dh1, mo, w_out, g_post_mix)


def _in_proj_bwd(du, dq, dk, dv, cos_t, sin_t, x, dh1, g_pre_mix, w_in):
    L = x.shape[0]
    tm = _tile(L)

    def body(du_ref, dq_ref, dk_ref, dv_ref, cos_ref, sin_ref, x_ref, dh1_ref, g_ref, w_ref,
             dproj_ref, dx_ref, dg_ref):
        first = pl.program_id(0) == 0
        cos_v, sin_v = cos_ref[...], sin_ref[...]
        dproj = jnp.concatenate([du_ref[...], _rope_transpose(dq_ref[...], cos_v, sin_v),
                                 _rope_transpose(dk_ref[...], cos_v, sin_v), dv_ref[...]], axis=1).astype(_BF16)
        dproj_ref[...] = dproj
        dhn = _dot(dproj, w_ref[...], _NT)
        x = x_ref[...]
        g = g_ref[...]
        _, r = _rms_fwd(x, g)
        dx, dg = _rms_bwd(dhn, x, g, r)
        _accumulate(dg_ref, dg, first)
        dx_ref[...] = dh1_ref[...] + dx

    row = _whole((1, D_MODEL))
    return _call(body, (L // tm,),
                 [_rows(tm, SSM_WIDTH), _rows(tm, ATTN_WIDTH), _rows(tm, KV_WIDTH), _rows(tm, KV_WIDTH),
                  _rows(tm, KV_WIDTH), _rows(tm, KV_WIDTH), _rows(tm, D_MODEL), _rows(tm, D_MODEL), row,
                  _whole((D_MODEL, IN_WIDTH))],
                 [_rows(tm, IN_WIDTH), _rows(tm, D_MODEL), row],
                 [_sds((L, IN_WIDTH), _BF16), _sds((L, D_MODEL), _F32), _sds((1, D_MODEL), _F32)],
                 "in_proj_bwd")(du, dq, dk, dv, cos_t, sin_t, x, dh1, g_pre_mix, w_in)


def _matmul_tn(a, b, out_dtype, name, scale=1.0):
    K, M = a.shape
    N = b.shape[1]
    tm = next(t for t in (512, 256, 128) if M % t == 0)
    tn = next(t for t in (512, 256, 128) if N % t == 0)

    def body(a_ref, b_ref, o_ref):
        acc = _dot(a_ref[...], b_ref[...], _TN)
        o_ref[...] = (acc if scale == 1.0 else acc * scale).astype(out_dtype)

    params = pltpu.CompilerParams(dimension_semantics=("arbitrary", "arbitrary"), vmem_limit_bytes=VMEM_LIMIT)
    return pl.pallas_call(body, grid=(M // tm, N // tn),
                          in_specs=[pl.BlockSpec((K, tm), lambda i, j: (0, i)),
                                    pl.BlockSpec((K, tn), lambda i, j: (0, j))],
                          out_specs=pl.BlockSpec((tm, tn), lambda i, j: (i, j)),
                          out_shape=_sds((M, N), out_dtype), compiler_params=params, name=name)(a, b)


def _to_chunked(a):
    L, n = a.shape
    return a.reshape(SCAN_CHUNKS, L // SCAN_CHUNKS, n).transpose(1, 0, 2).reshape(L, n)


def _from_chunked(a):
    L, n = a.shape
    return a.reshape(L // SCAN_CHUNKS, SCAN_CHUNKS, n).transpose(1, 0, 2).reshape(L, n)


def _local_step(x, pos, target, p, fetch, publish):
    L = x.shape[0]
    T = L // SCAN_CHUNKS
    cos_t, sin_t = _rope_tables(pos.reshape(L, 1))
    w_in, = fetch(("w_in",), None)
    hn, u, q, k, v = _in_proj(x, p["g_pre_mix"], w_in, cos_t, sin_t)

    lam_re_r = p["ssm_lambda_re"].reshape(1, N_STATE)
    lam_im_r = p["ssm_lambda_im"].reshape(1, N_STATE)
    ldt_r = jnp.broadcast_to(p["ssm_log_dt"].reshape(SSM_GROUPS, 1), (SSM_GROUPS, SSM_STATE)).reshape(1, N_STATE)
    lam_re_c, lam_im_c, ldt_c = (a.reshape(N_STATE, 1) for a in (lam_re_r, lam_im_r, ldt_r))
    b_re2 = p["ssm_b_re"].reshape(N_STATE, SSM_GROUP)
    b_im2 = p["ssm_b_im"].reshape(N_STATE, SSM_GROUP)
    c_re2 = p["ssm_c_re"].reshape(SSM_WIDTH, SSM_STATE)
    c_im2 = p["ssm_c_im"].reshape(SSM_WIDTH, SSM_STATE)
    d_row = p["ssm_d"].reshape(1, SSM_WIDTH)
    a_re, a_im, bt_re, bt_im, ct_re, ct_im = _ssm_prep(
        lam_re_r, lam_im_r, ldt_r, lam_re_c, lam_im_c, ldt_c, b_re2, b_im2, c_re2, c_im2)

    u_c = _to_chunked(u)
    bu_re, bu_im = _ssm_bu(u_c, bt_re, bt_im)
    x_re, x_im = _scan_fwd(bu_re.reshape(T, SCAN_CHUNKS, N_STATE), bu_im.reshape(T, SCAN_CHUNKS, N_STATE), a_re, a_im)
    w_glu, = fetch(("w_glu",), x_re)
    y, z, n_ssm_c = _ssm_out(x_re.reshape(L, N_STATE), x_im.reshape(L, N_STATE), u_c, ct_re, ct_im, d_row,
                             w_glu, p["b_glu"], p["g_ssm_out"])
    n_ssm = _from_chunked(n_ssm_c)

    sinks = p["attn_sinks"].reshape(N_Q_HEADS)
    o, n_attn = _attn_fwd(q, k, v, sinks, p["g_attn_out"])
    w_out, = fetch(("w_out",), n_attn)
    merged, mo, h1, hn2 = _out_proj(n_ssm, n_attn, x, w_out, p["g_post_mix"], p["g_pre_ffn"])
    w_gate_up, w_down = fetch(("w_gate_up", "w_down"), hn2)
    act, dgu, dff, dh1, loss, dg_post_ffn, dg_pre_ffn = _ffn(
        hn2, h1, target, w_gate_up, w_down, p["g_pre_ffn"], p["g_post_ffn"])
    grads = {"g_post_ffn": dg_post_ffn, "g_pre_ffn": dg_pre_ffn}
    tokens = publish({"w_down": _matmul_tn(act, dff, _BF16, "grad_w_down"),
                      "w_gate_up": _matmul_tn(hn2, dgu, _BF16, "grad_w_gate_up")})

    dmo, dn_ssm, dn_attn, grads["g_post_mix"] = _out_proj_bwd(dh1, mo, w_out, p["g_post_mix"], tokens)
    grad_w_out = _matmul_tn(merged, dmo, _BF16, "grad_w_out")

    dq, dk, dv, dsink, grads["g_attn_out"] = _attn_bwd(q, k, v, o, dn_attn, sinks, p["g_attn_out"])
    grads["attn_sinks"] = dsink[:, :N_Q_HEADS]

    gy, dz, dy, dud, dx_re, dx_im, grads["g_ssm_out"], grads["b_glu"], dd = _ssm_out_bwd(
        _to_chunked(dn_ssm), y, z, u_c, ct_re, ct_im, d_row, w_glu, p["g_ssm_out"])
    grads["ssm_d"] = dd.reshape(1, SSM_GROUPS, SSM_GROUP)
    tokens = publish({"w_out": grad_w_out, "w_glu": _matmul_tn(gy, dz, _BF16, "grad_w_glu")})
    lam_re, lam_im, da_re, da_im = _scan_bwd(dx_re.reshape(T, SCAN_CHUNKS, N_STATE), dx_im.reshape(T, SCAN_CHUNKS, N_STATE),
                                             x_re, x_im, a_re, a_im, tokens)
    lam_re = lam_re.reshape(L, N_STATE)
    lam_im = lam_im.reshape(L, N_STATE)
    dct_re, dct_im, dbt_re, dbt_im = _ssm_weight_grads(
        dy, x_re.reshape(L, N_STATE), x_im.reshape(L, N_STATE), lam_re, lam_im, u_c)
    g_lr, g_li, g_dt, g_br, g_bi, g_cr, g_ci = _ssm_param_bwd(
        da_re.reshape(N_STATE, 1), da_im.reshape(N_STATE, 1), dbt_re, dbt_im, dct_re, dct_im,
        lam_re_c, lam_im_c, ldt_c, b_re2, b_im2)
    grads["ssm_lambda_re"] = g_lr.reshape(1, SSM_GROUPS, SSM_STATE)
    grads["ssm_lambda_im"] = g_li.reshape(1, SSM_GROUPS, SSM_STATE)
    grads["ssm_log_dt"] = g_dt[:, 0].reshape(1, SSM_GROUPS)
    grads["ssm_b_re"] = g_br.reshape(1, SSM_GROUPS, SSM_STATE, SSM_GROUP)
    grads["ssm_b_im"] = g_bi.reshape(1, SSM_GROUPS, SSM_STATE, SSM_GROUP)
    grads["ssm_c_re"] = g_cr.reshape(1, SSM_GROUPS, SSM_GROUP, SSM_STATE)
    grads["ssm_c_im"] = g_ci.reshape(1, SSM_GROUPS, SSM_GROUP, SSM_STATE)

    du = _from_chunked(_ssm_du(lam_re, lam_im, bt_re, bt_im, dud))
    dproj, grad_x, grads["g_pre_mix"] = _in_proj_bwd(du, dq, dk, dv, cos_t, sin_t, x, dh1, p["g_pre_mix"], w_in)
    publish({"w_in": _matmul_tn(hn, dproj, _BF16, "grad_w_in")})
    return loss[0, 0], grad_x, grads


_MESH = pl.DeviceIdType.MESH
_PEERS = N_DEV - 1


def _mesh_pos():
    return lax.axis_index("x"), lax.axis_index("y"), lax.axis_index("c")


def _dev_index(px, py, pc):
    return 4 * px + 2 * py + pc


def _all_gather(shards, out_dtype, name):
    n = len(shards)

    def body(*refs):
        ins, outs, stages = refs[:n], refs[n:2 * n], refs[2 * n:3 * n]
        send_sems, recv_sems, local_sems = refs[3 * n:]
        x, y, c = _mesh_pos()
        me, sibling = (x, y, c), (x, y, 1 - c)
        chips = [(1 - x, y), (x, 1 - y), (1 - x, 1 - y)]

        def copy(w, k, block, to, src=None):
            slot = outs[w].at[_dev_index(*block)]
            return pltpu.make_async_remote_copy(
                src_ref=slot if src is None else src, dst_ref=slot,
                send_sem=send_sems.at[_PEERS * w + k], recv_sem=recv_sems.at[_PEERS * w + k],
                device_id=to, device_id_type=_MESH)

        for w in range(n):
            stages[w][...] = ins[w][...].astype(out_dtype)
        mine, first, passed = [], [], []
        for w in range(n):
            cp = pltpu.make_async_copy(stages[w], outs[w].at[_dev_index(*me)], local_sems.at[w])
            cp.start()
            mine.append(cp)
            sends = [copy(w, 0, me, sibling, src=stages[w])]
            sends += [copy(w, 1 + j, me, (*chip, c), src=stages[w]) for j, chip in enumerate(chips)]
            for cp in sends:
                cp.start()
            first += sends
        for w in range(n):
            for j, chip in enumerate(chips):
                copy(w, 1 + j, (*chip, c), me).wait_recv()
                cp = copy(w, 4 + j, (*chip, c), sibling)
                cp.start()
                passed.append(cp)
        for w in range(n):
            copy(w, 0, sibling, me).wait_recv()
            for j, chip in enumerate(chips):
                copy(w, 4 + j, (*chip, 1 - c), me).wait_recv()
        for cp in first + passed:
            cp.wait_send()
        for cp in mine:
            cp.wait()

    return pl.pallas_call(
        body, name=name,
        out_shape=[_sds((N_DEV,) + s.shape, out_dtype) for s in shards],
        in_specs=[pl.BlockSpec(memory_space=pltpu.VMEM)] * n,
        out_specs=[pl.BlockSpec(memory_space=pl.ANY)] * n,
        scratch_shapes=[pltpu.VMEM(s.shape, out_dtype) for s in shards]
        + [pltpu.SemaphoreType.DMA((_PEERS * n,)), pltpu.SemaphoreType.DMA((_PEERS * n,)),
           pltpu.SemaphoreType.DMA((n,))],
        compiler_params=pltpu.CompilerParams(vmem_limit_bytes=VMEM_LIMIT),
    )(*shards)


_HBM_SPEC = pl.BlockSpec(memory_space=pltpu.HBM)
_SEM_SPEC = pl.BlockSpec(memory_space=pltpu.SEMAPHORE)
_DATAFLOW = pltpu.SideEffectType.DATAFLOW_SIDE_EFFECTING


def _peer(x, y, c, r):
    return (x ^ ((r >> 2) & 1), y ^ ((r >> 1) & 1), c ^ (r & 1))


def _hbm(a):
    return pltpu.with_memory_space_constraint(a, pltpu.HBM)


def _send_start(sources, blocked, name):
    n = len(sources)
    lands = [lax.empty((N_DEV,) + (s.shape[1:] if blocked else s.shape), s.dtype) for s in sources]

    def body(*refs):
        srcs, zones = refs[:n], refs[n:2 * n]
        send_sems, recv_sems = refs[2 * n:3 * n], refs[3 * n:4 * n]
        token, local_sems = refs[6 * n], refs[6 * n + 1]
        x, y, c = _mesh_pos()
        me = _dev_index(x, y, c)
        local = []
        for w in range(n):
            cp = pltpu.make_async_copy(srcs[w].at[me] if blocked else srcs[w], zones[w].at[me], local_sems.at[w])
            cp.start()
            local.append(cp)
            for r in range(1, N_DEV):
                peer = _peer(x, y, c, r)
                pltpu.make_async_remote_copy(
                    src_ref=srcs[w].at[_dev_index(*peer)] if blocked else srcs[w], dst_ref=zones[w].at[me],
                    send_sem=send_sems[w].at[r - 1], recv_sem=recv_sems[w].at[r - 1],
                    device_id=peer, device_id_type=_MESH).start()
        for cp in local:
            cp.wait()
        token[...] = jnp.zeros_like(token)

    sems = [pltpu.SemaphoreType.DMA((_PEERS,))] * (2 * n)
    out = pl.pallas_call(
        body, name=name,
        out_shape=sems + [pltpu.HBM(a.shape, a.dtype) for a in list(sources) + lands] + [_sds((8, 128), _F32)],
        in_specs=[_HBM_SPEC] * (2 * n),
        out_specs=[_SEM_SPEC] * (2 * n) + [_HBM_SPEC] * (2 * n) + [pl.BlockSpec(memory_space=pltpu.VMEM)],
        input_output_aliases={i: 2 * n + i for i in range(2 * n)},
        scratch_shapes=[pltpu.SemaphoreType.DMA((n,))],
        compiler_params=pltpu.CompilerParams(has_side_effects=_DATAFLOW),
    )(*[_hbm(a) for a in sources], *[_hbm(a) for a in lands])
    return out[:n], out[n:2 * n], out[2 * n:3 * n], out[3 * n:4 * n], out[4 * n]


def _send_wait(send_sems, recv_sems, sources, lands, after, blocked, name):
    n = len(sources)

    def body(*refs):
        srcs, zones = refs[:n], refs[n:2 * n]
        sends, recvs = refs[2 * n:3 * n], refs[3 * n:4 * n]
        x, y, c = _mesh_pos()
        for w in range(n):
            for r in range(1, N_DEV):
                peer = _peer(x, y, c, r)
                idx = _dev_index(*peer)
                cp = pltpu.make_async_remote_copy(
                    src_ref=srcs[w].at[idx] if blocked else srcs[w], dst_ref=zones[w].at[idx],
                    send_sem=sends[w].at[r - 1], recv_sem=recvs[w].at[r - 1],
                    device_id=peer, device_id_type=_MESH)
                cp.wait_send()
                cp.wait_recv()

    out = pl.pallas_call(
        body, name=name,
        out_shape=[pltpu.HBM(a.shape, a.dtype) for a in list(sources) + list(lands)],
        in_specs=[_HBM_SPEC] * (2 * n) + [_SEM_SPEC] * (2 * n) + [pl.BlockSpec(memory_space=pl.ANY)],
        out_specs=[_HBM_SPEC] * (2 * n),
        input_output_aliases={i: i for i in range(2 * n)},
        compiler_params=pltpu.CompilerParams(has_side_effects=_DATAFLOW),
    )(*sources, *lands, *send_sems, *recv_sems, after)
    return out[n:]


def _row_tile(rows):
    return next(t for t in range(min(rows, 256), 0, -16) if rows % t == 0)


def _adamw(parts, w, m, v, name):
    rows, cols = w.shape
    tr = _row_tile(rows)

    def body(p_ref, w_ref, m_ref, v_ref, g_ref, d_ref, nm_ref, nv_ref):
        g = p_ref[0].astype(_F32)
        for s in range(1, N_DEV):
            g = g + p_ref[s].astype(_F32)
        new_m = ADAM_B1 * m_ref[...] + (1.0 - ADAM_B1) * g
        new_v = ADAM_B2 * v_ref[...] + (1.0 - ADAM_B2) * (g * g)
        m_hat = new_m / (1.0 - ADAM_B1 ** ADAM_STEP)
        v_hat = new_v / (1.0 - ADAM_B2 ** ADAM_STEP)
        g_ref[...] = g
        d_ref[...] = -ADAM_LR * (m_hat / (jnp.sqrt(v_hat) + ADAM_EPS) + ADAM_WD * w_ref[...])
        nm_ref[...] = new_m
        nv_ref[...] = new_v

    blk = _rows(tr, cols)
    return _call(body, (rows // tr,),
                 [pl.BlockSpec((N_DEV, tr, cols), lambda i: (0, i, 0)), blk, blk, blk],
                 [blk] * 4, [_sds((rows, cols), _F32)] * 4, name)(parts, w, m, v)


_SMALL = ("g_pre_mix", "ssm_lambda_re", "ssm_lambda_im", "ssm_log_dt", "ssm_b_re", "ssm_b_im",
          "ssm_c_re", "ssm_c_im", "ssm_d", "b_glu", "attn_sinks", "g_ssm_out", "g_attn_out",
          "g_post_mix", "g_pre_ffn", "g_post_ffn")
_BIG = ("w_in", "w_glu", "w_out", "w_gate_up", "w_down")
_WEIGHTS = ("g_pre_mix", "w_in", "ssm_lambda_re", "ssm_lambda_im", "ssm_log_dt", "ssm_b_re", "ssm_b_im",
            "ssm_c_re", "ssm_c_im", "ssm_d", "w_glu", "b_glu", "attn_sinks", "g_ssm_out", "g_attn_out",
            "w_out", "g_post_mix", "g_pre_ffn", "w_gate_up", "w_down", "g_post_ffn")
_LANES = 128


def _pack(arrays, extra_rows):
    rows = []
    for a in arrays:
        flat = a.reshape(-1).astype(_F32)
        pad = (-flat.shape[0]) % _LANES
        rows.append(jnp.pad(flat, (0, pad)).reshape(-1, _LANES))
    packed = jnp.concatenate(rows + [jnp.zeros((extra_rows, _LANES), _F32)], axis=0)
    return jnp.pad(packed, ((0, (-packed.shape[0]) % 16), (0, 0)))


def _unpack(packed, like):
    out, row = [], 0
    for a in like:
        size = int(np.prod(a.shape))
        nrows = -(-size // _LANES)
        out.append(packed[row:row + nrows].reshape(-1)[:size].reshape(a.shape))
        row += nrows
    return out, row


def _to_blocks(a):
    r, c = a.shape
    return a.reshape(r, N_DEV, c // N_DEV).transpose(1, 0, 2)


def _from_blocks(a):
    n, r, c = a.shape
    return a.transpose(1, 0, 2).reshape(r, n * c)


def kernel(x, positions, g_pre_mix, w_in, ssm_lambda_re, ssm_lambda_im, ssm_log_dt, ssm_b_re, ssm_b_im, ssm_c_re, ssm_c_im, ssm_d, w_glu, b_glu, attn_sinks, g_ssm_out, g_attn_out, w_out, g_post_mix, g_pre_ffn, w_gate_up, w_down, g_post_ffn, loss_target, m_g_pre_mix, m_w_in, m_ssm_lambda_re, m_ssm_lambda_im, m_ssm_log_dt, m_ssm_b_re, m_ssm_b_im, m_ssm_c_re, m_ssm_c_im, m_ssm_d, m_w_glu, m_b_glu, m_attn_sinks, m_g_ssm_out, m_g_attn_out, m_w_out, m_g_post_mix, m_g_pre_ffn, m_w_gate_up, m_w_down, m_g_post_ffn, v_g_pre_mix, v_w_in, v_ssm_lambda_re, v_ssm_lambda_im, v_ssm_log_dt, v_ssm_b_re, v_ssm_b_im, v_ssm_c_re, v_ssm_c_im, v_ssm_d, v_w_glu, v_b_glu, v_attn_sinks, v_g_ssm_out, v_g_attn_out, v_w_out, v_g_post_mix, v_g_pre_ffn, v_w_gate_up, v_w_down, v_g_post_ffn):
    w = dict(g_pre_mix=g_pre_mix, w_in=w_in, ssm_lambda_re=ssm_lambda_re, ssm_lambda_im=ssm_lambda_im,
             ssm_log_dt=ssm_log_dt, ssm_b_re=ssm_b_re, ssm_b_im=ssm_b_im, ssm_c_re=ssm_c_re, ssm_c_im=ssm_c_im,
             ssm_d=ssm_d, w_glu=w_glu, b_glu=b_glu, attn_sinks=attn_sinks, g_ssm_out=g_ssm_out,
             g_attn_out=g_attn_out, w_out=w_out, g_post_mix=g_post_mix, g_pre_ffn=g_pre_ffn,
             w_gate_up=w_gate_up, w_down=w_down, g_post_ffn=g_post_ffn)
    m = dict(g_pre_mix=m_g_pre_mix, w_in=m_w_in, ssm_lambda_re=m_ssm_lambda_re, ssm_lambda_im=m_ssm_lambda_im,
             ssm_log_dt=m_ssm_log_dt, ssm_b_re=m_ssm_b_re, ssm_b_im=m_ssm_b_im, ssm_c_re=m_ssm_c_re,
             ssm_c_im=m_ssm_c_im, ssm_d=m_ssm_d, w_glu=m_w_glu, b_glu=m_b_glu, attn_sinks=m_attn_sinks,
             g_ssm_out=m_g_ssm_out, g_attn_out=m_g_attn_out, w_out=m_w_out, g_post_mix=m_g_post_mix,
             g_pre_ffn=m_g_pre_ffn, w_gate_up=m_w_gate_up, w_down=m_w_down, g_post_ffn=m_g_post_ffn)
    v = dict(g_pre_mix=v_g_pre_mix, w_in=v_w_in, ssm_lambda_re=v_ssm_lambda_re, ssm_lambda_im=v_ssm_lambda_im,
             ssm_log_dt=v_ssm_log_dt, ssm_b_re=v_ssm_b_re, ssm_b_im=v_ssm_b_im, ssm_c_re=v_ssm_c_re,
             ssm_c_im=v_ssm_c_im, ssm_d=v_ssm_d, w_glu=v_w_glu, b_glu=v_b_glu, attn_sinks=v_attn_sinks,
             g_ssm_out=v_g_ssm_out, g_attn_out=v_g_attn_out, w_out=v_w_out, g_post_mix=v_g_post_mix,
             g_pre_ffn=v_g_pre_ffn, w_gate_up=v_w_gate_up, w_down=v_w_down, g_post_ffn=v_g_post_ffn)

    row_sharded = ("w_out", "w_down")
    gather = _send_start([w[n][0].astype(_BF16) for n in _BIG], False, "gather_start")
    gather_sends, gather_recvs, gather_srcs, gather_lands, gather_token = gather

    def fetch(names, after):
        idx = [_BIG.index(n) for n in names]
        pick = lambda seq: [seq[i] for i in idx]
        lands = _send_wait(pick(gather_sends), pick(gather_recvs), pick(gather_srcs), pick(gather_lands),
                           gather_token if after is None else after, False, "gather_wait_" + names[0])
        return [a.reshape(-1, a.shape[2]) if n in row_sharded else _from_blocks(a) for n, a in zip(names, lands)]

    sent = []

    def publish(named):
        names = list(named)
        blocks = [named[n].reshape(N_DEV, -1, named[n].shape[1]) if n in row_sharded else _to_blocks(named[n])
                  for n in names]
        started = _send_start(blocks, True, "grads_start_" + names[0])
        sent.append((names, started))
        return [started[4]]

    p = {n: w[n] for n in _SMALL}
    loss, grad_x, grads = _local_step(x[0], positions[0], loss_target[0], p, fetch, publish)

    result = {}
    for names, (sends, recvs, srcs, lands, _) in sent:
        parts = _send_wait(sends, recvs, srcs, lands, grad_x, True, "grads_wait_" + names[0])
        for name, part in zip(names, parts):
            result[name] = [a[None] for a in _adamw(part, w[name][0], m[name][0], v[name][0], "adamw_" + name)]

    small_grads = _pack([grads[n] for n in _SMALL] + [jnp.pad(loss.reshape(1), (0, _LANES - 1))], 0)
    gathered, = _all_gather([small_grads], _F32, "gather_small")
    packed = _adamw(gathered, _pack([w[n] for n in _SMALL], 1), _pack([m[n] for n in _SMALL], 1),
                    _pack([v[n] for n in _SMALL], 1), "adamw_small")
    loss_row = None
    for name in _SMALL:
        result[name] = []
    for arr in packed:
        vals, loss_row = _unpack(arr, [w[n] for n in _SMALL])
        for name, val in zip(_SMALL, vals):
            result[name].append(val)
    total_loss = packed[0][loss_row, 0]

    out = [total_loss, grad_x[None]]
    for kind in range(4):
        out += [result[n][kind] for n in _WEIGHTS]
    return tuple(out)
```

```python
import functools
import math

import numpy as np
import jax
import jax.numpy as jnp
from jax import lax
from jax.experimental import pallas as pl
from jax.experimental.pallas import tpu as pltpu

D_MODEL = 1024
SSM_WIDTH = 512
SSM_GROUP = 16
SSM_GROUPS = 32
SSM_STATE = 64
N_STATE = SSM_GROUPS * SSM_STATE
ATTN_WIDTH = 512
HEAD_DIM = 64
N_Q_HEADS = 8
N_KV_HEADS = 2
Q_PER_KV = 4
KV_WIDTH = 128
IN_WIDTH = 1280
BLOCK = 128
ROPE_DIM = 16
ROPE_THETA = 500000.0
D_FF = 2816
NORM_EPS = 1e-6
MASK_VALUE = -1e30
ADAM_LR = 0.001
ADAM_B1 = 0.9
ADAM_B2 = 0.999
ADAM_EPS = 1e-08
ADAM_WD = 0.01
ADAM_STEP = 10

N_DEV = 8
SCAN_CHUNKS = 8
SCAN_COLS = 512
TOKEN_TILE = 256
VMEM_LIMIT = 56 * 1024 * 1024

_F32 = jnp.float32
_BF16 = jnp.bfloat16
_MXU = jnp.bfloat16

_NN = ((1,), (0,))
_NT = ((1,), (1,))
_TN = ((0,), (0,))


def _dot(a, b, dims):
    return lax.dot_general(a.astype(_MXU), b.astype(_MXU), (dims, ((), ())),
                           preferred_element_type=_F32)


def _dot_exact(a, b, dims):
    return lax.dot_general(a.astype(_F32), b.astype(_F32), (dims, ((), ())),
                           precision=lax.Precision.HIGHEST, preferred_element_type=_F32)


def _iota(shape, dim):
    return lax.broadcasted_iota(jnp.int32, shape, dim)


def _rms_fwd(x, g):
    r = lax.rsqrt(jnp.mean(x * x, axis=-1, keepdims=True) + NORM_EPS)
    return x * r * g, r


def _rms_bwd(dy, x, g, r):
    a = dy * g
    xn = x * r
    dx = r * (a - xn * jnp.mean(a * xn, axis=-1, keepdims=True))
    dg = jnp.sum(dy * xn, axis=0, keepdims=True)
    return dx, dg


def _call(body, grid, in_specs, out_specs, out_shape, name, scratch=(), tokens=()):
    params = pltpu.CompilerParams(dimension_semantics=("arbitrary",) * len(grid),
                                  vmem_limit_bytes=VMEM_LIMIT)
    n_in, n_tok = len(in_specs), len(tokens)

    def run(*refs):
        return body(*refs[:n_in], *refs[n_in + n_tok:])

    call = pl.pallas_call(run, grid=grid,
                          in_specs=list(in_specs) + [pl.BlockSpec(memory_space=pl.ANY)] * n_tok,
                          out_specs=out_specs, out_shape=out_shape, scratch_shapes=list(scratch),
                          compiler_params=params, name=name)
    return lambda *args: call(*args, *tokens)


def _rows(tm, n):
    return pl.BlockSpec((tm, n), lambda i: (i, 0))


def _whole(shape):
    nd = len(shape)
    return pl.BlockSpec(shape, lambda i: (0,) * nd)


def _sds(shape, dtype):
    return jax.ShapeDtypeStruct(shape, dtype)


def _tile(L):
    return min(TOKEN_TILE, L)


def _accumulate(ref, val, first):
    @pl.when(first)
    def _():
        ref[...] = val

    @pl.when(jnp.logical_not(first))
    def _():
        ref[...] += val


def _rope_rows():
    half = ROPE_DIM // 2
    inv = (np.float32(ROPE_THETA) ** (-np.arange(half, dtype=np.float32) * np.float32(2.0) / np.float32(ROPE_DIM))).astype(np.float32)
    col = np.arange(KV_WIDTH) % HEAD_DIM
    freq = np.where(col < ROPE_DIM, inv[col % half], 0.0).astype(np.float32)
    sign = np.where(col < half, -1.0, np.where(col < ROPE_DIM, 1.0, 0.0)).astype(np.float32)
    return freq[None, :], sign[None, :]


def _rope_tables(pos_col):
    L = pos_col.shape[0]
    tm = _tile(L)
    freq, sign = _rope_rows()

    def body(pos_ref, freq_ref, sign_ref, cos_ref, sin_ref):
        ang = pos_ref[...].astype(_F32) * freq_ref[...]
        cos_ref[...] = jnp.cos(ang)
        sin_ref[...] = jnp.sin(ang) * sign_ref[...]

    return _call(body, (L // tm,),
                 [_rows(tm, 1), _whole((1, KV_WIDTH)), _whole((1, KV_WIDTH))],
                 [_rows(tm, KV_WIDTH), _rows(tm, KV_WIDTH)],
                 [_sds((L, KV_WIDTH), _F32)] * 2, "rope_tables")(pos_col, jnp.asarray(freq), jnp.asarray(sign))


def _widen(t, width):
    return t if width == KV_WIDTH else jnp.concatenate([t] * (width // KV_WIDTH), axis=1)


def _rope_partner(t):
    w = t.shape[1]
    in_head = _iota((1, w), 1) & (HEAD_DIM - 1)
    second = jnp.where(in_head < ROPE_DIM, pltpu.roll(t, ROPE_DIM // 2, 1), 0.0)
    return jnp.where(in_head < ROPE_DIM // 2, pltpu.roll(t, w - ROPE_DIM // 2, 1), second)


def _rope_apply(t, cos_t, sin_t):
    w = t.shape[1]
    return t * _widen(cos_t, w) + _rope_partner(t) * _widen(sin_t, w)


def _rope_transpose(dt, cos_t, sin_t):
    w = dt.shape[1]
    return dt * _widen(cos_t, w) + _rope_partner(dt * _widen(sin_t, w))


def _in_proj(x, g_pre_mix, w_in, cos_t, sin_t):
    L = x.shape[0]
    tm = _tile(L)

    def body(x_ref, g_ref, w_ref, cos_ref, sin_ref, hn_ref, u_ref, q_ref, k_ref, v_ref):
        hn, _ = _rms_fwd(x_ref[...], g_ref[...])
        hn = hn.astype(_BF16)
        hn_ref[...] = hn
        proj = _dot(hn, w_ref[...], _NT)
        u_ref[...] = proj[:, :SSM_WIDTH]
        q = proj[:, SSM_WIDTH:SSM_WIDTH + ATTN_WIDTH]
        k = proj[:, SSM_WIDTH + ATTN_WIDTH:SSM_WIDTH + ATTN_WIDTH + KV_WIDTH]
        cos_v, sin_v = cos_ref[...], sin_ref[...]
        q_ref[...] = _rope_apply(q, cos_v, sin_v).astype(_BF16)
        k_ref[...] = _rope_apply(k, cos_v, sin_v).astype(_BF16)
        v_ref[...] = proj[:, SSM_WIDTH + ATTN_WIDTH + KV_WIDTH:].astype(_BF16)

    return _call(body, (L // tm,),
                 [_rows(tm, D_MODEL), _whole((1, D_MODEL)), _whole((IN_WIDTH, D_MODEL)),
                  _rows(tm, KV_WIDTH), _rows(tm, KV_WIDTH)],
                 [_rows(tm, D_MODEL), _rows(tm, SSM_WIDTH), _rows(tm, ATTN_WIDTH),
                  _rows(tm, KV_WIDTH), _rows(tm, KV_WIDTH)],
                 [_sds((L, D_MODEL), _BF16), _sds((L, SSM_WIDTH), _F32), _sds((L, ATTN_WIDTH), _BF16),
                  _sds((L, KV_WIDTH), _BF16), _sds((L, KV_WIDTH), _BF16)],
                 "in_proj")(x, g_pre_mix, w_in, cos_t, sin_t)


def _s5_discretize(lam_re, lam_im, log_dt):
    lr = jnp.minimum(lam_re, -1e-4)
    li = lam_im
    dt = jnp.exp(log_dt)
    mag = jnp.exp(lr * dt)
    ar = mag * jnp.cos(li * dt)
    ai = mag * jnp.sin(li * dt)
    den = lr * lr + li * li
    fr = ((ar - 1.0) * lr + ai * li) / den
    fi = (ai * lr - (ar - 1.0) * li) / den
    return ar, ai, fr, fi


def _s5_bbar(lam_re, lam_im, log_dt, b_re, b_im):
    ar, ai, fr, fi = _s5_discretize(lam_re, lam_im, log_dt)
    return ar, ai, fr * b_re - fi * b_im, fr * b_im + fi * b_re


def _spread_masks():
    e16 = (_iota((SSM_GROUP, SSM_WIDTH), 1) & (SSM_GROUP - 1)) == _iota((SSM_GROUP, SSM_WIDTH), 0)
    e64 = (_iota((SSM_STATE, N_STATE), 1) & (SSM_STATE - 1)) == _iota((SSM_STATE, N_STATE), 0)
    mask_b = (_iota((N_STATE, SSM_WIDTH), 0) >> 6) == (_iota((N_STATE, SSM_WIDTH), 1) >> 4)
    mask_c = (_iota((SSM_WIDTH, N_STATE), 0) >> 4) == (_iota((SSM_WIDTH, N_STATE), 1) >> 6)
    return e16.astype(_F32), e64.astype(_F32), mask_b, mask_c


SUPER = 4
SB_STATE = N_STATE // SUPER
SB_WIDTH = SSM_WIDTH // SUPER


def _sb_state(k):
    return slice(SB_STATE * k, SB_STATE * (k + 1))


def _sb_width(k):
    return slice(SB_WIDTH * k, SB_WIDTH * (k + 1))


def _ssm_prep(lam_re_r, lam_im_r, ldt_r, lam_re_c, lam_im_c, ldt_c, b_re2, b_im2, c_re2, c_im2):
    def body(lrr, lir, ldr, lrc, lic, ldc, bre, bim, cre, cim, ar_ref, ai_ref, btr, bti, ctr, cti):
        ar, ai, _, _ = _s5_discretize(lrr[...], lir[...], ldr[...])
        ar_ref[...] = ar
        ai_ref[...] = ai
        _, _, bbr, bbi = _s5_bbar(lrc[...], lic[...], ldc[...], bre[...], bim[...])
        e16, e64, mask_b, mask_c = _spread_masks()

        def fold_b(bb):
            full = jnp.where(mask_b, _dot(bb, e16, _NN), 0.0)
            return sum(full[:, _sb_width(k)] for k in range(SUPER)).astype(_BF16)

        def fold_c(cc):
            full = jnp.where(mask_c, _dot(cc, e64, _NN), 0.0)
            return sum(full[_sb_width(k), :] for k in range(SUPER)).astype(_BF16)

        btr[...] = fold_b(bbr)
        bti[...] = fold_b(bbi)
        ctr[...] = fold_c(cre[...])
        cti[...] = fold_c(cim[...])

    row = (1, N_STATE)
    ins = [lam_re_r, lam_im_r, ldt_r, lam_re_c, lam_im_c, ldt_c, b_re2, b_im2, c_re2, c_im2]
    return _call(body, (1,), [_whole(a.shape) for a in ins],
                 [_whole(row), _whole(row), _whole((N_STATE, SB_WIDTH)), _whole((N_STATE, SB_WIDTH)),
                  _whole((SB_WIDTH, N_STATE)), _whole((SB_WIDTH, N_STATE))],
                 [_sds(row, _F32), _sds(row, _F32), _sds((N_STATE, SB_WIDTH), _BF16),
                  _sds((N_STATE, SB_WIDTH), _BF16), _sds((SB_WIDTH, N_STATE), _BF16),
                  _sds((SB_WIDTH, N_STATE), _BF16)], "ssm_prep")(*ins)


def _ssm_bu(u, bt_re, bt_im):
    L = u.shape[0]
    tm = _tile(L)

    def body(u_ref, br_ref, bi_ref, or_ref, oi_ref):
        for k in range(SUPER):
            ub = u_ref[:, _sb_width(k)].astype(_BF16)
            or_ref[:, _sb_state(k)] = _dot(ub, br_ref[_sb_state(k), :], _NT)
            oi_ref[:, _sb_state(k)] = _dot(ub, bi_ref[_sb_state(k), :], _NT)

    return _call(body, (L // tm,),
                 [_rows(tm, SSM_WIDTH), _whole((N_STATE, SB_WIDTH)), _whole((N_STATE, SB_WIDTH))],
                 [_rows(tm, N_STATE), _rows(tm, N_STATE)],
                 [_sds((L, N_STATE), _F32)] * 2, "ssm_bu")(u, bt_re, bt_im)


def _complex_power(ar, ai, n):
    def step(_, c):
        pr, pi = c
        return pr * ar - pi * ai, pr * ai + pi * ar
    return lax.fori_loop(0, n, step, (jnp.ones_like(ar), jnp.zeros_like(ai)))


def _chunk_carries(er, ei, pr, pi, reverse):
    rows = _iota(er.shape, 0)
    sr = jnp.zeros_like(pr)
    si = jnp.zeros_like(pi)
    out_r = jnp.zeros_like(er)
    out_i = jnp.zeros_like(ei)
    order = range(SCAN_CHUNKS - 1, 0, -1) if reverse else range(SCAN_CHUNKS - 1)
    for c in order:
        e_r = er[c:c + 1, :]
        e_i = ei[c:c + 1, :]
        sr, si = pr * sr - pi * si + e_r, pr * si + pi * sr + e_i
        nxt = c - 1 if reverse else c + 1
        out_r = jnp.where(rows == nxt, sr, out_r)
        out_i = jnp.where(rows == nxt, si, out_i)
    return out_r, out_i


def _scan_fwd(b_re, b_im, a_re, a_im):
    T = b_re.shape[0]
    W = SCAN_COLS
    blk = pl.BlockSpec((T, SCAN_CHUNKS, W), lambda j: (0, 0, j))
    vec = pl.BlockSpec((1, W), lambda j: (0, j))

    def body(br_ref, bi_ref, ar_ref, ai_ref, xr_ref, xi_ref):
        ar, ai = ar_ref[...], ai_ref[...]
        ar8 = jnp.broadcast_to(ar, (SCAN_CHUNKS, W))
        ai8 = jnp.broadcast_to(ai, (SCAN_CHUNKS, W))

        def local(t, c):
            cr, ci = c
            return ar8 * cr - ai8 * ci + br_ref[t], ar8 * ci + ai8 * cr + bi_ref[t]

        zero = jnp.zeros((SCAN_CHUNKS, W), _F32)
        er, ei = lax.fori_loop(0, T, local, (zero, zero))
        pr, pi = _complex_power(ar, ai, T)
        sr, si = _chunk_carries(er, ei, pr, pi, reverse=False)

        def final(t, c):
            nr, ni = local(t, c)
            xr_ref[t] = nr
            xi_ref[t] = ni
            return nr, ni

        lax.fori_loop(0, T, final, (sr, si))

    shape = _sds(b_re.shape, _F32)
    return _call(body, (N_STATE // W,), [blk, blk, vec, vec], [blk, blk], [shape, shape],
                 "scan_fwd")(b_re, b_im, a_re, a_im)


def _scan_bwd(dx_re, dx_im, x_re, x_im, a_re, a_im, tokens=()):
    T = dx_re.shape[0]
    W = SCAN_COLS
    blk = pl.BlockSpec((T, SCAN_CHUNKS, W), lambda j: (0, 0, j))
    vec = pl.BlockSpec((1, W), lambda j: (0, j))

    def body(dr_ref, di_ref, xr_ref, xi_ref, ar_ref, ai_ref, lr_ref, li_ref, dar_ref, dai_ref):
        ar, ai = ar_ref[...], ai_ref[...]
        ar8 = jnp.broadcast_to(ar, (SCAN_CHUNKS, W))
        ai8 = jnp.broadcast_to(ai, (SCAN_CHUNKS, W))

        def local(t, c):
            cr, ci = c
            return ar8 * cr + ai8 * ci + dr_ref[t], ar8 * ci - ai8 * cr + di_ref[t]

        zero = jnp.zeros((SCAN_CHUNKS, W), _F32)
        er, ei = lax.fori_loop(0, T, lambda k, c: local(T - 1 - k, c), (zero, zero))
        pr, pi = _complex_power(ar, -ai, T)
        sr, si = _chunk_carries(er, ei, pr, pi, reverse=True)

        def grad_a(acc, nr, ni, xpr, xpi):
            return acc[0] + nr * xpr + ni * xpi, acc[1] + ni * xpr - nr * xpi

        def final(k, c):
            t = T - 1 - k
            nr, ni = local(t, c[:2])
            lr_ref[t] = nr
            li_ref[t] = ni
            gr, gi = grad_a(c[2:], nr, ni, xr_ref[t - 1], xi_ref[t - 1])
            return nr, ni, gr, gi

        cr, ci, gr, gi = lax.fori_loop(0, T - 1, final, (sr, si, zero, zero))
        nr, ni = local(0, (cr, ci))
        lr_ref[0] = nr
        li_ref[0] = ni
        first = _iota((SCAN_CHUNKS, W), 0) == 0
        xpr = jnp.where(first, 0.0, pltpu.roll(xr_ref[T - 1], 1, 0))
        xpi = jnp.where(first, 0.0, pltpu.roll(xi_ref[T - 1], 1, 0))
        gr, gi = grad_a((gr, gi), nr, ni, xpr, xpi)
        dar_ref[...] = jnp.sum(gr, axis=0, keepdims=True)
        dai_ref[...] = jnp.sum(gi, axis=0, keepdims=True)

    shape = _sds(dx_re.shape, _F32)
    row = _sds((1, N_STATE), _F32)
    return _call(body, (N_STATE // W,), [blk, blk, blk, blk, vec, vec], [blk, blk, vec, vec],
                 [shape, shape, row, row], "scan_bwd", tokens=tokens)(dx_re, dx_im, x_re, x_im, a_re, a_im)


_GELU_K = math.sqrt(2.0 / math.pi)
_GELU_C = 0.044715


def _gelu(y):
    return 0.5 * y * (1.0 + jnp.tanh(_GELU_K * (y + _GELU_C * y * y * y)))


def _gelu_grad(y):
    t = jnp.tanh(_GELU_K * (y + _GELU_C * y * y * y))
    return 0.5 * (1.0 + t) + 0.5 * y * (1.0 - t * t) * _GELU_K * (1.0 + 3.0 * _GELU_C * y * y)


def _ssm_out(x_re, x_im, u, ct_re, ct_im, d_row, w_glu, b_glu, g_ssm):
    L = u.shape[0]
    tm = _tile(L)

    def body(xr_ref, xi_ref, u_ref, cr_ref, ci_ref, d_ref, w_ref, b_ref, g_ref, y_ref, z_ref, n_ref):
        cx = [_dot(xr_ref[:, _sb_state(k)], cr_ref[:, _sb_state(k)], _NT)
              - _dot(xi_ref[:, _sb_state(k)], ci_ref[:, _sb_state(k)], _NT) for k in range(SUPER)]
        y = jnp.concatenate(cx, axis=1) + d_ref[...] * u_ref[...]
        y_ref[...] = y
        z = _dot(_gelu(y), w_ref[...], _NT) + b_ref[...]
        z_ref[...] = z
        out = z[:, :SSM_WIDTH] * jax.nn.sigmoid(z[:, SSM_WIDTH:])
        n, _ = _rms_fwd(out, g_ref[...])
        n_ref[...] = n.astype(_BF16)

    return _call(body, (L // tm,),
                 [_rows(tm, N_STATE), _rows(tm, N_STATE), _rows(tm, SSM_WIDTH),
                  _whole((SB_WIDTH, N_STATE)), _whole((SB_WIDTH, N_STATE)), _whole((1, SSM_WIDTH)),
                  _whole((2 * SSM_WIDTH, SSM_WIDTH)), _whole((1, 2 * SSM_WIDTH)), _whole((1, SSM_WIDTH))],
                 [_rows(tm, SSM_WIDTH), _rows(tm, 2 * SSM_WIDTH), _rows(tm, SSM_WIDTH)],
                 [_sds((L, SSM_WIDTH), _F32), _sds((L, 2 * SSM_WIDTH), _F32), _sds((L, SSM_WIDTH), _BF16)],
                 "ssm_out")(x_re, x_im, u, ct_re, ct_im, d_row, w_glu, b_glu, g_ssm)


def _ssm_out_bwd(dn, y, z, u, ct_re, ct_im, d_row, w_glu, g_ssm):
    L = u.shape[0]
    tm = _tile(L)

    def body(dn_ref, y_ref, z_ref, u_ref, cr_ref, ci_ref, d_ref, w_ref, g_ref,
             gy_ref, dz_ref, dy_ref, dud_ref, dxr_ref, dxi_ref, dg_ref, db_ref, dd_ref):
        first = pl.program_id(0) == 0
        z = z_ref[...]
        z1, z2 = z[:, :SSM_WIDTH], z[:, SSM_WIDTH:]
        sig = jax.nn.sigmoid(z2)
        out = z1 * sig
        g = g_ref[...]
        _, r = _rms_fwd(out, g)
        dout, dg = _rms_bwd(dn_ref[...], out, g, r)
        _accumulate(dg_ref, dg, first)
        dz = jnp.concatenate([dout * sig, dout * z1 * sig * (1.0 - sig)], axis=1)
        _accumulate(db_ref, jnp.sum(dz, axis=0, keepdims=True), first)
        dzb = dz.astype(_BF16)
        dz_ref[...] = dzb
        y = y_ref[...]
        gy_ref[...] = _gelu(y).astype(_BF16)
        dy = _dot(dzb, w_ref[...], _NN) * _gelu_grad(y)
        u = u_ref[...]
        _accumulate(dd_ref, jnp.sum(dy * u, axis=0, keepdims=True), first)
        dud_ref[...] = d_ref[...] * dy
        dyb = dy.astype(_BF16)
        dy_ref[...] = dyb
        for k in range(SUPER):
            dxr_ref[:, _sb_state(k)] = _dot(dyb[:, _sb_width(k)], cr_ref[:, _sb_state(k)], _NN)
            dxi_ref[:, _sb_state(k)] = -_dot(dyb[:, _sb_width(k)], ci_ref[:, _sb_state(k)], _NN)

    row = _whole((1, SSM_WIDTH))
    return _call(body, (L // tm,),
                 [_rows(tm, SSM_WIDTH), _rows(tm, SSM_WIDTH), _rows(tm, 2 * SSM_WIDTH), _rows(tm, SSM_WIDTH),
                  _whole((SB_WIDTH, N_STATE)), _whole((SB_WIDTH, N_STATE)), row,
                  _whole((2 * SSM_WIDTH, SSM_WIDTH)), row],
                 [_rows(tm, SSM_WIDTH), _rows(tm, 2 * SSM_WIDTH), _rows(tm, SSM_WIDTH), _rows(tm, SSM_WIDTH),
                  _rows(tm, N_STATE), _rows(tm, N_STATE), row, _whole((1, 2 * SSM_WIDTH)), row],
                 [_sds((L, SSM_WIDTH), _BF16), _sds((L, 2 * SSM_WIDTH), _BF16), _sds((L, SSM_WIDTH), _BF16),
                  _sds((L, SSM_WIDTH), _F32), _sds((L, N_STATE), _F32), _sds((L, N_STATE), _F32),
                  _sds((1, SSM_WIDTH), _F32), _sds((1, 2 * SSM_WIDTH), _F32), _sds((1, SSM_WIDTH), _F32)],
                 "ssm_out_bwd")(dn, y, z, u, ct_re, ct_im, d_row, w_glu, g_ssm)


def _ssm_du(lam_re, lam_im, bt_re, bt_im, dud):
    L = dud.shape[0]
    tm = _tile(L)

    def body(lr_ref, li_ref, br_ref, bi_ref, dud_ref, du_ref):
        for k in range(SUPER):
            du_ref[:, _sb_width(k)] = (_dot(lr_ref[:, _sb_state(k)], br_ref[_sb_state(k), :], _NN)
                                       + _dot(li_ref[:, _sb_state(k)], bi_ref[_sb_state(k), :], _NN)
                                       + dud_ref[:, _sb_width(k)])

    return _call(body, (L // tm,),
                 [_rows(tm, N_STATE), _rows(tm, N_STATE), _whole((N_STATE, SB_WIDTH)),
                  _whole((N_STATE, SB_WIDTH)), _rows(tm, SSM_WIDTH)],
                 _rows(tm, SSM_WIDTH), _sds((L, SSM_WIDTH), _F32), "ssm_du")(lam_re, lam_im, bt_re, bt_im, dud)


def _ssm_weight_grads(dy, x_re, x_im, lam_re, lam_im, u):
    L = u.shape[0]

    def body(dy_ref, xr_ref, xi_ref, lr_ref, li_ref, u_ref, dcr_ref, dci_ref, dbr_ref, dbi_ref):
        dyb = dy_ref[...]
        ub = u_ref[...].astype(_BF16)
        dcr_ref[...] = _dot(dyb, xr_ref[...], _TN)
        dci_ref[...] = _dot(dyb, xi_ref[...], _TN)
        dbr_ref[...] = _dot(lr_ref[...], ub, _TN)
        dbi_ref[...] = _dot(li_ref[...], ub, _TN)

    width = pl.BlockSpec((L, SB_WIDTH), lambda k: (0, k))
    state = pl.BlockSpec((L, SB_STATE), lambda k: (0, k))
    out_c = pl.BlockSpec((SB_WIDTH, SB_STATE), lambda k: (0, k))
    out_b = pl.BlockSpec((SB_STATE, SB_WIDTH), lambda k: (k, 0))
    return _call(body, (SUPER,), [width, state, state, state, state, width], [out_c, out_c, out_b, out_b],
                 [_sds((SB_WIDTH, N_STATE), _F32)] * 2 + [_sds((N_STATE, SB_WIDTH), _F32)] * 2,
                 "ssm_weight_grads")(dy, x_re, x_im, lam_re, lam_im, u)


def _ssm_param_bwd(da_re_c, da_im_c, dbt_re, dbt_im, dct_re, dct_im,
                   lam_re_c, lam_im_c, ldt_c, b_re2, b_im2):
    def body(dar, dai, dbr, dbi, dcr, dci, lrc, lic, ldc, bre, bim,
             glr, gli, gdt, gbr, gbi, gcr, gci):
        own_b = ((_iota((N_STATE, SB_WIDTH), 0) >> 6) & 7) == (_iota((N_STATE, SB_WIDTH), 1) >> 4)
        own_c = (_iota((SB_WIDTH, SB_STATE), 0) >> 4) == (_iota((SB_WIDTH, SB_STATE), 1) >> 6)

        def fold_b(ref):
            t = jnp.where(own_b, ref[...], 0.0)
            for shift in (64, 32, 16):
                t = t + pltpu.roll(t, shift, 1)
            return t[:, :SSM_GROUP]

        def fold_c(ref, k):
            t = jnp.where(own_c, ref[:, _sb_state(k)], 0.0)
            t = sum(t[:, 128 * i:128 * (i + 1)] for i in range(SB_STATE // 128))
            return (t + pltpu.roll(t, SSM_STATE, 1))[:, :SSM_STATE]

        dbbr = fold_b(dbr)
        dbbi = fold_b(dbi)
        for k in range(SUPER):
            gcr[_sb_width(k), :] = fold_c(dcr, k)
            gci[_sb_width(k), :] = -fold_c(dci, k)
        _, vjp = jax.vjp(_s5_bbar, lrc[...], lic[...], ldc[...], bre[...], bim[...])
        d_lr, d_li, d_dt, d_br, d_bi = vjp((dar[...], dai[...], dbbr, dbbi))
        glr[...] = d_lr
        gli[...] = d_li
        gbr[...] = d_br
        gbi[...] = d_bi
        groups = (_iota((SSM_GROUPS, N_STATE), 1) >> 6) == _iota((SSM_GROUPS, N_STATE), 0)
        gdt[...] = _dot_exact(groups.astype(_F32), jnp.broadcast_to(d_dt, (N_STATE, 128)), _NN)

    col = (N_STATE, 1)
    ins = [da_re_c, da_im_c, dbt_re, dbt_im, dct_re, dct_im, lam_re_c, lam_im_c, ldt_c, b_re2, b_im2]
    outs = [col, col, (SSM_GROUPS, 128), (N_STATE, SSM_GROUP), (N_STATE, SSM_GROUP),
            (SSM_WIDTH, SSM_STATE), (SSM_WIDTH, SSM_STATE)]
    return _call(body, (1,), [_whole(a.shape) for a in ins], [_whole(s) for s in outs],
                 [_sds(s, _F32) for s in outs], "ssm_param_bwd")(*ins)


def _head_spread(j):
    r = _iota((KV_WIDTH, 256), 0)
    c = _iota((KV_WIDTH, 256), 1)
    return (r == HEAD_DIM * j + (c & (HEAD_DIM - 1))).astype(_BF16)


STACK = Q_PER_KV * BLOCK


def _stack_heads(t):
    lane_head = _iota((1, 256), 1) >> 6
    return jnp.concatenate([jnp.where(lane_head == g, t, jnp.zeros_like(t)) for g in range(Q_PER_KV)], axis=0)


def _unstack_heads(t):
    lane_head = _iota((1, 256), 1) >> 6
    return sum(jnp.where(lane_head == g, t[BLOCK * g:BLOCK * (g + 1)], 0.0) for g in range(Q_PER_KV))


def _stacked_sinks(sink_ref, j):
    block = _iota((STACK, 1), 0) >> 7
    col = jnp.full((STACK, 1), sink_ref[Q_PER_KV * j], _F32)
    for g in range(1, Q_PER_KV):
        col = jnp.where(block == g, sink_ref[Q_PER_KV * j + g], col)
    return col


def _fold_heads(t, j):
    t = t[:, :KV_WIDTH] + t[:, KV_WIDTH:]
    t = t + pltpu.roll(t, HEAD_DIM, 1)
    return jnp.where((_iota((1, KV_WIDTH), 1) >> 6) == j, t, 0.0)


def _attn_scores(q_stacked, kt, blk, sink):
    s = _dot(q_stacked, kt, _NT) * (HEAD_DIM ** -0.5)
    qi = _iota((STACK, 2 * BLOCK), 0) & (BLOCK - 1)
    kj = _iota((STACK, 2 * BLOCK), 1)
    rel = qi + BLOCK - kj
    valid = (rel >= 0) & (rel < BLOCK) & (blk * BLOCK - BLOCK + kj >= 0)
    s = jnp.where(valid, s, MASK_VALUE)
    m = jnp.maximum(jnp.max(s, axis=-1, keepdims=True), sink)
    p = jnp.exp(s - m)
    e_sink = jnp.exp(sink - m)
    den = jnp.sum(p, axis=-1, keepdims=True) + e_sink
    return p / den, e_sink / den


def _attn_specs():
    prev = lambda i: (jnp.maximum(i - 1, 0), 0)
    cur = lambda i: (i, 0)
    kv = [pl.BlockSpec((BLOCK, KV_WIDTH), prev), pl.BlockSpec((BLOCK, KV_WIDTH), cur)]
    return [pl.BlockSpec((BLOCK, ATTN_WIDTH), cur)] + kv + kv


def _attn_fwd(q, k, v, sinks, g_attn):
    L = q.shape[0]

    def body(q_ref, kp_ref, kc_ref, vp_ref, vc_ref, sink_ref, g_ref, o_ref, n_ref):
        blk = pl.program_id(0)
        kwin = jnp.concatenate([kp_ref[...], kc_ref[...]], axis=0)
        vwin = jnp.concatenate([vp_ref[...], vc_ref[...]], axis=0)
        halves = []
        for j in range(N_KV_HEADS):
            spread = _head_spread(j)
            kt = _dot(kwin, spread, _NN).astype(_BF16)
            vt = _dot(vwin, spread, _NN).astype(_BF16)
            qs = _stack_heads(q_ref[:, 256 * j:256 * (j + 1)])
            p, _ = _attn_scores(qs, kt, blk, _stacked_sinks(sink_ref, j))
            halves.append(_unstack_heads(_dot(p, vt, _NN)))
        o = jnp.concatenate(halves, axis=1)
        o_ref[...] = o
        n, _ = _rms_fwd(o, g_ref[...])
        n_ref[...] = n.astype(_BF16)

    cur = lambda i: (i, 0)
    return _call(body, (L // BLOCK,),
                 _attn_specs() + [pl.BlockSpec(memory_space=pltpu.SMEM), _whole((1, ATTN_WIDTH))],
                 [pl.BlockSpec((BLOCK, ATTN_WIDTH), cur)] * 2,
                 [_sds((L, ATTN_WIDTH), _F32), _sds((L, ATTN_WIDTH), _BF16)],
                 "attn_fwd")(q, k, k, v, v, sinks, g_attn)


def _attn_bwd(q, k, v, o, dn, sinks, g_attn):
    L = q.shape[0]

    def body(q_ref, kp_ref, kc_ref, vp_ref, vc_ref, o_ref, dn_ref, sink_ref, g_ref,
             dq_ref, dk_ref, dv_ref, dsink_ref, dg_ref):
        blk = pl.program_id(0)
        first = blk == 0

        @pl.when(first)
        def _():
            dk_ref[...] = jnp.zeros_like(dk_ref)
            dv_ref[...] = jnp.zeros_like(dv_ref)
            dsink_ref[...] = jnp.zeros_like(dsink_ref)

        o = o_ref[...]
        g = g_ref[...]
        _, r = _rms_fwd(o, g)
        do, dg = _rms_bwd(dn_ref[...], o, g, r)
        _accumulate(dg_ref, dg, first)
        kwin = jnp.concatenate([kp_ref[...], kc_ref[...]], axis=0)
        vwin = jnp.concatenate([vp_ref[...], vc_ref[...]], axis=0)
        lane = _iota((1, 128), 1)
        dsink = jnp.zeros((1, 128), _F32)
        dkwin = jnp.zeros((2 * BLOCK, KV_WIDTH), _F32)
        dvwin = jnp.zeros((2 * BLOCK, KV_WIDTH), _F32)
        dq_halves = []
        for j in range(N_KV_HEADS):
            spread = _head_spread(j)
            kt = _dot(kwin, spread, _NN).astype(_BF16)
            vt = _dot(vwin, spread, _NN).astype(_BF16)
            qs = _stack_heads(q_ref[:, 256 * j:256 * (j + 1)])
            dos = _stack_heads(do[:, 256 * j:256 * (j + 1)]).astype(_BF16)
            p, p_sink = _attn_scores(qs, kt, blk, _stacked_sinks(sink_ref, j))
            dp = _dot(dos, vt, _NT)
            delta = jnp.sum(p * dp, axis=-1, keepdims=True)
            ds = (p * (dp - delta) * (HEAD_DIM ** -0.5)).astype(_BF16)
            sink_term = p_sink * delta
            for g in range(Q_PER_KV):
                head_sum = jnp.sum(sink_term[BLOCK * g:BLOCK * (g + 1)], axis=0, keepdims=True)
                dsink = dsink - jnp.where(lane == Q_PER_KV * j + g, head_sum, 0.0)
            dvwin = dvwin + _fold_heads(_dot(p, dos, _TN), j)
            dkwin = dkwin + _fold_heads(_dot(ds, qs, _TN), j)
            dq_halves.append(_unstack_heads(_dot(ds, kt, _NN)))
        dq_ref[...] = jnp.concatenate(dq_halves, axis=1)
        dsink_ref[...] += dsink
        prev = pl.ds(pl.multiple_of(jnp.maximum(blk - 1, 0) * BLOCK, BLOCK), BLOCK)
        cur = pl.ds(pl.multiple_of(blk * BLOCK, BLOCK), BLOCK)
        dk_ref[prev, :] += dkwin[:BLOCK]
        dk_ref[cur, :] += dkwin[BLOCK:]
        dv_ref[prev, :] += dvwin[:BLOCK]
        dv_ref[cur, :] += dvwin[BLOCK:]

    cur = lambda i: (i, 0)
    blk_q = pl.BlockSpec((BLOCK, ATTN_WIDTH), cur)
    return _call(body, (L // BLOCK,),
                 _attn_specs() + [blk_q, blk_q, pl.BlockSpec(memory_space=pltpu.SMEM), _whole((1, ATTN_WIDTH))],
                 [blk_q, _whole((L, KV_WIDTH)), _whole((L, KV_WIDTH)), _whole((1, 128)), _whole((1, ATTN_WIDTH))],
                 [_sds((L, ATTN_WIDTH), _F32), _sds((L, KV_WIDTH), _F32), _sds((L, KV_WIDTH), _F32),
                  _sds((1, 128), _F32), _sds((1, ATTN_WIDTH), _F32)],
                 "attn_bwd")(q, k, k, v, v, o, dn, sinks, g_attn)


def _out_proj(n_ssm, n_attn, x, w_out, g_post_mix, g_pre_ffn):
    L = x.shape[0]
    tm = _tile(L)

    def body(ns_ref, na_ref, x_ref, w_ref, g1_ref, g2_ref, merged_ref, mo_ref, h1_ref, hn2_ref):
        merged = jnp.concatenate([ns_ref[...], na_ref[...]], axis=1)
        merged_ref[...] = merged
        mo = _dot(merged, w_ref[...], _NN)
        mo_ref[...] = mo
        n, _ = _rms_fwd(mo, g1_ref[...])
        h1 = x_ref[...] + n
        h1_ref[...] = h1
        hn2, _ = _rms_fwd(h1, g2_ref[...])
        hn2_ref[...] = hn2.astype(_BF16)

    row = _whole((1, D_MODEL))
    return _call(body, (L // tm,),
                 [_rows(tm, SSM_WIDTH), _rows(tm, ATTN_WIDTH), _rows(tm, D_MODEL), _whole((D_MODEL, D_MODEL)), row, row],
                 [_rows(tm, D_MODEL)] * 4,
                 [_sds((L, D_MODEL), _BF16), _sds((L, D_MODEL), _F32), _sds((L, D_MODEL), _F32), _sds((L, D_MODEL), _BF16)],
                 "out_proj")(n_ssm, n_attn, x, w_out, g_post_mix, g_pre_ffn)


def _ffn(hn2, h1, target, w_gate_up, w_down, g_pre_ffn, g_post_ffn):
    L = h1.shape[0]
    tm = _tile(L)
    half = D_FF // 2

    def body(hn2_ref, h1_ref, tgt_ref, wgu_hbm, wd_hbm, g2_ref, g3_ref,
             act_ref, dgu_ref, dff_ref, dh1_ref, loss_ref, dg3_ref, dg2_ref,
             wgu, wd, gu, sem):
        first = pl.program_id(0) == 0

        @pl.when(first)
        def _():
            c1 = pltpu.make_async_copy(wgu_hbm, wgu, sem.at[0])
            c2 = pltpu.make_async_copy(wd_hbm, wd, sem.at[1])
            c1.start()
            c2.start()
            c1.wait()
            c2.wait()

        hn2 = hn2_ref[...]
        ff = jnp.zeros((tm, D_MODEL), _F32)
        for c in range(2):
            gate = _dot(hn2, wgu[half * c:half * (c + 1), :], _NT)
            up = _dot(hn2, wgu[D_FF + half * c:D_FF + half * (c + 1), :], _NT)
            gu[:, half * c:half * (c + 1)] = gate
            gu[:, D_FF + half * c:D_FF + half * (c + 1)] = up
            act = (gate * jax.nn.sigmoid(gate) * up).astype(_BF16)
            act_ref[:, half * c:half * (c + 1)] = act
            ff = ff + _dot(act, wd[half * c:half * (c + 1), :], _NN)
        g3 = g3_ref[...]
        n, r = _rms_fwd(ff, g3)
        h1 = h1_ref[...]
        err = h1 + n - tgt_ref[...]
        loss = 0.5 * jnp.sum(jnp.mean(err * err, axis=-1, keepdims=True), axis=0, keepdims=True)
        _accumulate(loss_ref, jnp.broadcast_to(loss, (1, 128)), first)
        dh2 = err * (1.0 / D_MODEL)
        dff, dg3 = _rms_bwd(dh2, ff, g3, r)
        _accumulate(dg3_ref, dg3, first)
        dffb = dff.astype(_BF16)
        dff_ref[...] = dffb
        dhn2 = jnp.zeros((tm, D_MODEL), _F32)
        for c in range(2):
            dact = _dot(dffb, wd[half * c:half * (c + 1), :], _NT)
            gate = gu[:, half * c:half * (c + 1)]
            up = gu[:, D_FF + half * c:D_FF + half * (c + 1)]
            sig = jax.nn.sigmoid(gate)
            silu = gate * sig
            dgate = (dact * up * (sig + silu * (1.0 - sig))).astype(_BF16)
            dup = (dact * silu).astype(_BF16)
            dgu_ref[:, half * c:half * (c + 1)] = dgate
            dgu_ref[:, D_FF + half * c:D_FF + half * (c + 1)] = dup
            dhn2 = dhn2 + _dot(dgate, wgu[half * c:half * (c + 1), :], _NN)
            dhn2 = dhn2 + _dot(dup, wgu[D_FF + half * c:D_FF + half * (c + 1), :], _NN)
        g2 = g2_ref[...]
        _, r2 = _rms_fwd(h1, g2)
        dh1, dg2 = _rms_bwd(dhn2, h1, g2, r2)
        _accumulate(dg2_ref, dg2, first)
        dh1_ref[...] = dh2 + dh1

    row = _whole((1, D_MODEL))
    anyspace = pl.BlockSpec(memory_space=pl.ANY)
    return _call(body, (L // tm,),
                 [_rows(tm, D_MODEL), _rows(tm, D_MODEL), _rows(tm, D_MODEL), anyspace, anyspace, row, row],
                 [_rows(tm, D_FF), _rows(tm, 2 * D_FF), _rows(tm, D_MODEL), _rows(tm, D_MODEL),
                  _whole((1, 128)), row, row],
                 [_sds((L, D_FF), _BF16), _sds((L, 2 * D_FF), _BF16), _sds((L, D_MODEL), _BF16),
                  _sds((L, D_MODEL), _F32), _sds((1, 128), _F32), _sds((1, D_MODEL), _F32), _sds((1, D_MODEL), _F32)],
                 "ffn",
                 scratch=[pltpu.VMEM((2 * D_FF, D_MODEL), _BF16), pltpu.VMEM((D_FF, D_MODEL), _BF16),
                          pltpu.VMEM((tm, 2 * D_FF), _F32), pltpu.SemaphoreType.DMA((2,))],
                 )(hn2, h1, target, w_gate_up, w_down, g_pre_ffn, g_post_ffn)


def _out_proj_bwd(dh1, mo, w_out, g_post_mix, tokens=()):
    L = dh1.shape[0]
    tm = _tile(L)

    def body(dh1_ref, mo_ref, w_ref, g_ref, dmo_ref, dns_ref, dna_ref, dg_ref):
        first = pl.program_id(0) == 0
        mo = mo_ref[...]
        g = g_ref[...]
        _, r = _rms_fwd(mo, g)
        dmo, dg = _rms_bwd(dh1_ref[...], mo, g, r)
        _accumulate(dg_ref, dg, first)
        dmob = dmo.astype(_BF16)
        dmo_ref[...] = dmob
        dmerged = _dot(dmob, w_ref[...], _NT)
        dns_ref[...] = dmerged[:, :SSM_WIDTH]
        dna_ref[...] = dmerged[:, SSM_WIDTH:]

    row = _whole((1, D_MODEL))
    return _call(body, (L // tm,),
                 [_rows(tm, D_MODEL), _rows(tm, D_MODEL), _whole((D_MODEL, D_MODEL)), row],
                 [_rows(tm, D_MODEL), _rows(tm, SSM_WIDTH), _rows(tm, ATTN_WIDTH), row],
                 [_sds((L, D_MODEL), _BF16), _sds((L, SSM_WIDTH), _F32), _sds((L, ATTN_WIDTH), _F32),
                  _sds((1, D_MODEL), _F32)],
                 "out_proj_bwd", tokens=tokens)(dh1, mo, w_out, g_post_mix)


def _in_proj_bwd(du, dq, dk, dv, cos_t, sin_t, x, dh1, g_pre_mix, w_in):
    L = x.shape[0]
    tm = _tile(L)

    def body(du_ref, dq_ref, dk_ref, dv_ref, cos_ref, sin_ref, x_ref, dh1_ref, g_ref, w_ref,
             dproj_ref, dx_ref, dg_ref):
        first = pl.program_id(0) == 0
        cos_v, sin_v = cos_ref[...], sin_ref[...]
        dproj = jnp.concatenate([du_ref[...], _rope_transpose(dq_ref[...], cos_v, sin_v),
                                 _rope_transpose(dk_ref[...], cos_v, sin_v), dv_ref[...]], axis=1).astype(_BF16)
        dproj_ref[...] = dproj
        dhn = _dot(dproj, w_ref[...], _NN)
        x = x_ref[...]
        g = g_ref[...]
        _, r = _rms_fwd(x, g)
        dx, dg = _rms_bwd(dhn, x, g, r)
        _accumulate(dg_ref, dg, first)
        dx_ref[...] = dh1_ref[...] + dx

    row = _whole((1, D_MODEL))
    return _call(body, (L // tm,),
                 [_rows(tm, SSM_WIDTH), _rows(tm, ATTN_WIDTH), _rows(tm, KV_WIDTH), _rows(tm, KV_WIDTH),
                  _rows(tm, KV_WIDTH), _rows(tm, KV_WIDTH), _rows(tm, D_MODEL), _rows(tm, D_MODEL), row,
                  _whole((IN_WIDTH, D_MODEL))],
                 [_rows(tm, IN_WIDTH), _rows(tm, D_MODEL), row],
                 [_sds((L, IN_WIDTH), _BF16), _sds((L, D_MODEL), _F32), _sds((1, D_MODEL), _F32)],
                 "in_proj_bwd")(du, dq, dk, dv, cos_t, sin_t, x, dh1, g_pre_mix, w_in)


def _matmul_tn(a, b, out_dtype, name, scale=1.0):
    K, M = a.shape
    N = b.shape[1]
    tm = next(t for t in (512, 256, 128) if M % t == 0)
    tn = next(t for t in (512, 256, 128) if N % t == 0)

    def body(a_ref, b_ref, o_ref):
        acc = _dot(a_ref[...], b_ref[...], _TN)
        o_ref[...] = (acc if scale == 1.0 else acc * scale).astype(out_dtype)

    params = pltpu.CompilerParams(dimension_semantics=("arbitrary", "arbitrary"), vmem_limit_bytes=VMEM_LIMIT)
    return pl.pallas_call(body, grid=(M // tm, N // tn),
                          in_specs=[pl.BlockSpec((K, tm), lambda i, j: (0, i)),
                                    pl.BlockSpec((K, tn), lambda i, j: (0, j))],
                          out_specs=pl.BlockSpec((tm, tn), lambda i, j: (i, j)),
                          out_shape=_sds((M, N), out_dtype), compiler_params=params, name=name)(a, b)


def _to_chunked(a):
    L, n = a.shape
    return a.reshape(SCAN_CHUNKS, L // SCAN_CHUNKS, n).transpose(1, 0, 2).reshape(L, n)


def _from_chunked(a):
    L, n = a.shape
    return a.reshape(L // SCAN_CHUNKS, SCAN_CHUNKS, n).transpose(1, 0, 2).reshape(L, n)


def _local_step(x, pos, target, p, fetch, publish):
    L = x.shape[0]
    T = L // SCAN_CHUNKS
    cos_t, sin_t = _rope_tables(pos.reshape(L, 1))
    w_in, = fetch(("w_in",), None)
    hn, u, q, k, v = _in_proj(x, p["g_pre_mix"], w_in, cos_t, sin_t)

    lam_re_r = p["ssm_lambda_re"].reshape(1, N_STATE)
    lam_im_r = p["ssm_lambda_im"].reshape(1, N_STATE)
    ldt_r = jnp.broadcast_to(p["ssm_log_dt"].reshape(SSM_GROUPS, 1), (SSM_GROUPS, SSM_STATE)).reshape(1, N_STATE)
    lam_re_c, lam_im_c, ldt_c = (a.reshape(N_STATE, 1) for a in (lam_re_r, lam_im_r, ldt_r))
    b_re2 = p["ssm_b_re"].reshape(N_STATE, SSM_GROUP)
    b_im2 = p["ssm_b_im"].reshape(N_STATE, SSM_GROUP)
    c_re2 = p["ssm_c_re"].reshape(SSM_WIDTH, SSM_STATE)
    c_im2 = p["ssm_c_im"].reshape(SSM_WIDTH, SSM_STATE)
    d_row = p["ssm_d"].reshape(1, SSM_WIDTH)
    a_re, a_im, bt_re, bt_im, ct_re, ct_im = _ssm_prep(
        lam_re_r, lam_im_r, ldt_r, lam_re_c, lam_im_c, ldt_c, b_re2, b_im2, c_re2, c_im2)

    u_c = _to_chunked(u)
    bu_re, bu_im = _ssm_bu(u_c, bt_re, bt_im)
    x_re, x_im = _scan_fwd(bu_re.reshape(T, SCAN_CHUNKS, N_STATE), bu_im.reshape(T, SCAN_CHUNKS, N_STATE), a_re, a_im)
    w_glu, = fetch(("w_glu",), x_re)
    y, z, n_ssm_c = _ssm_out(x_re.reshape(L, N_STATE), x_im.reshape(L, N_STATE), u_c, ct_re, ct_im, d_row,
                             w_glu, p["b_glu"], p["g_ssm_out"])
    n_ssm = _from_chunked(n_ssm_c)

    sinks = p["attn_sinks"].reshape(N_Q_HEADS)
    o, n_attn = _attn_fwd(q, k, v, sinks, p["g_attn_out"])
    w_out, = fetch(("w_out",), n_attn)
    merged, mo, h1, hn2 = _out_proj(n_ssm, n_attn, x, w_out, p["g_post_mix"], p["g_pre_ffn"])
    w_gate_up, w_down = fetch(("w_gate_up", "w_down"), hn2)
    act, dgu, dff, dh1, loss, dg_post_ffn, dg_pre_ffn = _ffn(
        hn2, h1, target, w_gate_up, w_down, p["g_pre_ffn"], p["g_post_ffn"])
    grads = {"g_post_ffn": dg_post_ffn, "g_pre_ffn": dg_pre_ffn}
    tokens = publish({"w_down": _matmul_tn(act, dff, _BF16, "grad_w_down"),
                      "w_gate_up": _matmul_tn(dgu, hn2, _BF16, "grad_w_gate_up")})

    dmo, dn_ssm, dn_attn, grads["g_post_mix"] = _out_proj_bwd(dh1, mo, w_out, p["g_post_mix"], tokens)
    grad_w_out = _matmul_tn(merged, dmo, _BF16, "grad_w_out")

    dq, dk, dv, dsink, grads["g_attn_out"] = _attn_bwd(q, k, v, o, dn_attn, sinks, p["g_attn_out"])
    grads["attn_sinks"] = dsink[:, :N_Q_HEADS]

    gy, dz, dy, dud, dx_re, dx_im, grads["g_ssm_out"], grads["b_glu"], dd = _ssm_out_bwd(
        _to_chunked(dn_ssm), y, z, u_c, ct_re, ct_im, d_row, w_glu, p["g_ssm_out"])
    grads["ssm_d"] = dd.reshape(1, SSM_GROUPS, SSM_GROUP)
    tokens = publish({"w_out": grad_w_out, "w_glu": _matmul_tn(dz, gy, _BF16, "grad_w_glu")})
    lam_re, lam_im, da_re, da_im = _scan_bwd(dx_re.reshape(T, SCAN_CHUNKS, N_STATE), dx_im.reshape(T, SCAN_CHUNKS, N_STATE),
                                             x_re, x_im, a_re, a_im, tokens)
    lam_re = lam_re.reshape(L, N_STATE)
    lam_im = lam_im.reshape(L, N_STATE)
    dct_re, dct_im, dbt_re, dbt_im = _ssm_weight_grads(
        dy, x_re.reshape(L, N_STATE), x_im.reshape(L, N_STATE), lam_re, lam_im, u_c)
    g_lr, g_li, g_dt, g_br, g_bi, g_cr, g_ci = _ssm_param_bwd(
        da_re.reshape(N_STATE, 1), da_im.reshape(N_STATE, 1), dbt_re, dbt_im, dct_re, dct_im,
        lam_re_c, lam_im_c, ldt_c, b_re2, b_im2)
    grads["ssm_lambda_re"] = g_lr.reshape(1, SSM_GROUPS, SSM_STATE)
    grads["ssm_lambda_im"] = g_li.reshape(1, SSM_GROUPS, SSM_STATE)
    grads["ssm_log_dt"] = g_dt[:, 0].reshape(1, SSM_GROUPS)
    grads["ssm_b_re"] = g_br.reshape(1, SSM_GROUPS, SSM_STATE, SSM_GROUP)
    grads["ssm_b_im"] = g_bi.reshape(1, SSM_GROUPS, SSM_STATE, SSM_GROUP)
    grads["ssm_c_re"] = g_cr.reshape(1, SSM_GROUPS, SSM_GROUP, SSM_STATE)
    grads["ssm_c_im"] = g_ci.reshape(1, SSM_GROUPS, SSM_GROUP, SSM_STATE)

    du = _from_chunked(_ssm_du(lam_re, lam_im, bt_re, bt_im, dud))
    dproj, grad_x, grads["g_pre_mix"] = _in_proj_bwd(du, dq, dk, dv, cos_t, sin_t, x, dh1, p["g_pre_mix"], w_in)
    publish({"w_in": _matmul_tn(dproj, hn, _BF16, "grad_w_in")})
    return loss[0, 0], grad_x, grads


_MESH = pl.DeviceIdType.MESH
_PEERS = N_DEV - 1


def _mesh_pos():
    return lax.axis_index("x"), lax.axis_index("y"), lax.axis_index("c")


def _dev_index(px, py, pc):
    return 4 * px + 2 * py + pc


def _all_gather(shards, out_dtype, name):
    n = len(shards)

    def body(*refs):
        ins, outs, stages = refs[:n], refs[n:2 * n], refs[2 * n:3 * n]
        send_sems, recv_sems, local_sems = refs[3 * n:]
        x, y, c = _mesh_pos()
        me, sibling = (x, y, c), (x, y, 1 - c)
        chips = [(1 - x, y), (x, 1 - y), (1 - x, 1 - y)]

        def copy(w, k, block, to, src=None):
            slot = outs[w].at[_dev_index(*block)]
            return pltpu.make_async_remote_copy(
                src_ref=slot if src is None else src, dst_ref=slot,
                send_sem=send_sems.at[_PEERS * w + k], recv_sem=recv_sems.at[_PEERS * w + k],
                device_id=to, device_id_type=_MESH)

        for w in range(n):
            stages[w][...] = ins[w][...].astype(out_dtype)
        mine, first, passed = [], [], []
        for w in range(n):
            cp = pltpu.make_async_copy(stages[w], outs[w].at[_dev_index(*me)], local_sems.at[w])
            cp.start()
            mine.append(cp)
            sends = [copy(w, 0, me, sibling, src=stages[w])]
            sends += [copy(w, 1 + j, me, (*chip, c), src=stages[w]) for j, chip in enumerate(chips)]
            for cp in sends:
                cp.start()
            first += sends
        for w in range(n):
            for j, chip in enumerate(chips):
                copy(w, 1 + j, (*chip, c), me).wait_recv()
                cp = copy(w, 4 + j, (*chip, c), sibling)
                cp.start()
                passed.append(cp)
        for w in range(n):
            copy(w, 0, sibling, me).wait_recv()
            for j, chip in enumerate(chips):
                copy(w, 4 + j, (*chip, 1 - c), me).wait_recv()
        for cp in first + passed:
            cp.wait_send()
        for cp in mine:
            cp.wait()

    return pl.pallas_call(
        body, name=name,
        out_shape=[_sds((N_DEV,) + s.shape, out_dtype) for s in shards],
        in_specs=[pl.BlockSpec(memory_space=pltpu.VMEM)] * n,
        out_specs=[pl.BlockSpec(memory_space=pl.ANY)] * n,
        scratch_shapes=[pltpu.VMEM(s.shape, out_dtype) for s in shards]
        + [pltpu.SemaphoreType.DMA((_PEERS * n,)), pltpu.SemaphoreType.DMA((_PEERS * n,)),
           pltpu.SemaphoreType.DMA((n,))],
        compiler_params=pltpu.CompilerParams(vmem_limit_bytes=VMEM_LIMIT),
    )(*shards)


_HBM_SPEC = pl.BlockSpec(memory_space=pltpu.HBM)
_SEM_SPEC = pl.BlockSpec(memory_space=pltpu.SEMAPHORE)
_DATAFLOW = pltpu.SideEffectType.DATAFLOW_SIDE_EFFECTING


def _peer(x, y, c, r):
    return (x ^ ((r >> 2) & 1), y ^ ((r >> 1) & 1), c ^ (r & 1))


def _hbm(a):
    return pltpu.with_memory_space_constraint(a, pltpu.HBM)


def _send_start(sources, blocked, name):
    n = len(sources)
    lands = [lax.empty((N_DEV,) + (s.shape[1:] if blocked else s.shape), s.dtype) for s in sources]

    def body(*refs):
        srcs, zones = refs[:n], refs[n:2 * n]
        send_sems, recv_sems = refs[2 * n:3 * n], refs[3 * n:4 * n]
        token, local_sems = refs[6 * n], refs[6 * n + 1]
        x, y, c = _mesh_pos()
        me = _dev_index(x, y, c)
        local = []
        for w in range(n):
            cp = pltpu.make_async_copy(srcs[w].at[me] if blocked else srcs[w], zones[w].at[me], local_sems.at[w])
            cp.start()
            local.append(cp)
            for r in range(1, N_DEV):
                peer = _peer(x, y, c, r)
                pltpu.make_async_remote_copy(
                    src_ref=srcs[w].at[_dev_index(*peer)] if blocked else srcs[w], dst_ref=zones[w].at[me],
                    send_sem=send_sems[w].at[r - 1], recv_sem=recv_sems[w].at[r - 1],
                    device_id=peer, device_id_type=_MESH).start()
        for cp in local:
            cp.wait()
        token[...] = jnp.zeros_like(token)

    sems = [pltpu.SemaphoreType.DMA((_PEERS,))] * (2 * n)
    out = pl.pallas_call(
        body, name=name,
        out_shape=sems + [pltpu.HBM(a.shape, a.dtype) for a in list(sources) + lands] + [_sds((8, 128), _F32)],
        in_specs=[_HBM_SPEC] * (2 * n),
        out_specs=[_SEM_SPEC] * (2 * n) + [_HBM_SPEC] * (2 * n) + [pl.BlockSpec(memory_space=pltpu.VMEM)],
        input_output_aliases={i: 2 * n + i for i in range(2 * n)},
        scratch_shapes=[pltpu.SemaphoreType.DMA((n,))],
        compiler_params=pltpu.CompilerParams(has_side_effects=_DATAFLOW),
    )(*[_hbm(a) for a in sources], *[_hbm(a) for a in lands])
    return out[:n], out[n:2 * n], out[2 * n:3 * n], out[3 * n:4 * n], out[4 * n]


def _send_wait(send_sems, recv_sems, sources, lands, after, blocked, name):
    n = len(sources)

    def body(*refs):
        srcs, zones = refs[:n], refs[n:2 * n]
        sends, recvs = refs[2 * n:3 * n], refs[3 * n:4 * n]
        x, y, c = _mesh_pos()
        for w in range(n):
            for r in range(1, N_DEV):
                peer = _peer(x, y, c, r)
                idx = _dev_index(*peer)
                cp = pltpu.make_async_remote_copy(
                    src_ref=srcs[w].at[idx] if blocked else srcs[w], dst_ref=zones[w].at[idx],
                    send_sem=sends[w].at[r - 1], recv_sem=recvs[w].at[r - 1],
                    device_id=peer, device_id_type=_MESH)
                cp.wait_send()
                cp.wait_recv()

    out = pl.pallas_call(
        body, name=name,
        out_shape=[pltpu.HBM(a.shape, a.dtype) for a in list(sources) + list(lands)],
        in_specs=[_HBM_SPEC] * (2 * n) + [_SEM_SPEC] * (2 * n) + [pl.BlockSpec(memory_space=pl.ANY)],
        out_specs=[_HBM_SPEC] * (2 * n),
        input_output_aliases={i: i for i in range(2 * n)},
        compiler_params=pltpu.CompilerParams(has_side_effects=_DATAFLOW),
    )(*sources, *lands, *send_sems, *recv_sems, after)
    return out[n:]


def _row_tile(rows):
    return next(t for t in range(min(rows, 256), 0, -16) if rows % t == 0)


def _sum_parts(parts, name):
    _, rows, cols = parts.shape
    tr = _row_tile(rows)

    def body(p_ref, g_ref):
        g = p_ref[0].astype(_F32)
        for s in range(1, N_DEV):
            g = g + p_ref[s].astype(_F32)
        g_ref[...] = g

    return _call(body, (rows // tr,), [pl.BlockSpec((N_DEV, tr, cols), lambda i: (0, i, 0))],
                 _rows(tr, cols), _sds((rows, cols), _F32), name)(parts)


def _adamw(parts, w, m, v, name):
    rows, cols = w.shape
    tr = _row_tile(rows)
    n_parts = parts.shape[0]

    def body(p_ref, w_ref, m_ref, v_ref, g_ref, d_ref, nm_ref, nv_ref):
        g = p_ref[0].astype(_F32)
        for s in range(1, n_parts):
            g = g + p_ref[s].astype(_F32)
        new_m = ADAM_B1 * m_ref[...] + (1.0 - ADAM_B1) * g
        new_v = ADAM_B2 * v_ref[...] + (1.0 - ADAM_B2) * (g * g)
        m_hat = new_m / (1.0 - ADAM_B1 ** ADAM_STEP)
        v_hat = new_v / (1.0 - ADAM_B2 ** ADAM_STEP)
        g_ref[...] = g
        d_ref[...] = -ADAM_LR * (m_hat / (jnp.sqrt(v_hat) + ADAM_EPS) + ADAM_WD * w_ref[...])
        nm_ref[...] = new_m
        nv_ref[...] = new_v

    blk = _rows(tr, cols)
    return _call(body, (rows // tr,),
                 [pl.BlockSpec((n_parts, tr, cols), lambda i: (0, i, 0)), blk, blk, blk],
                 [blk] * 4, [_sds((rows, cols), _F32)] * 4, name)(parts, w, m, v)


_SMALL = ("g_pre_mix", "ssm_lambda_re", "ssm_lambda_im", "ssm_log_dt", "ssm_b_re", "ssm_b_im",
          "ssm_c_re", "ssm_c_im", "ssm_d", "b_glu", "attn_sinks", "g_ssm_out", "g_attn_out",
          "g_post_mix", "g_pre_ffn", "g_post_ffn")
_BIG = ("w_in", "w_glu", "w_out", "w_gate_up", "w_down")
_WEIGHTS = ("g_pre_mix", "w_in", "ssm_lambda_re", "ssm_lambda_im", "ssm_log_dt", "ssm_b_re", "ssm_b_im",
            "ssm_c_re", "ssm_c_im", "ssm_d", "w_glu", "b_glu", "attn_sinks", "g_ssm_out", "g_attn_out",
            "w_out", "g_post_mix", "g_pre_ffn", "w_gate_up", "w_down", "g_post_ffn")
_LANES = 128


def _pack(arrays, extra_rows):
    rows = []
    for a in arrays:
        flat = a.reshape(-1).astype(_F32)
        pad = (-flat.shape[0]) % _LANES
        rows.append(jnp.pad(flat, (0, pad)).reshape(-1, _LANES))
    packed = jnp.concatenate(rows + [jnp.zeros((extra_rows, _LANES), _F32)], axis=0)
    return jnp.pad(packed, ((0, (-packed.shape[0]) % 16), (0, 0)))


def _unpack(packed, like):
    out, row = [], 0
    for a in like:
        size = int(np.prod(a.shape))
        nrows = -(-size // _LANES)
        out.append(packed[row:row + nrows].reshape(-1)[:size].reshape(a.shape))
        row += nrows
    return out, row


def _to_blocks(a):
    r, c = a.shape
    return a.reshape(r, N_DEV, c // N_DEV).transpose(1, 0, 2)


def _from_blocks(a):
    n, r, c = a.shape
    return a.transpose(1, 0, 2).reshape(r, n * c)


def kernel(x, positions, g_pre_mix, w_in, ssm_lambda_re, ssm_lambda_im, ssm_log_dt, ssm_b_re, ssm_b_im, ssm_c_re, ssm_c_im, ssm_d, w_glu, b_glu, attn_sinks, g_ssm_out, g_attn_out, w_out, g_post_mix, g_pre_ffn, w_gate_up, w_down, g_post_ffn, loss_target, m_g_pre_mix, m_w_in, m_ssm_lambda_re, m_ssm_lambda_im, m_ssm_log_dt, m_ssm_b_re, m_ssm_b_im, m_ssm_c_re, m_ssm_c_im, m_ssm_d, m_w_glu, m_b_glu, m_attn_sinks, m_g_ssm_out, m_g_attn_out, m_w_out, m_g_post_mix, m_g_pre_ffn, m_w_gate_up, m_w_down, m_g_post_ffn, v_g_pre_mix, v_w_in, v_ssm_lambda_re, v_ssm_lambda_im, v_ssm_log_dt, v_ssm_b_re, v_ssm_b_im, v_ssm_c_re, v_ssm_c_im, v_ssm_d, v_w_glu, v_b_glu, v_attn_sinks, v_g_ssm_out, v_g_attn_out, v_w_out, v_g_post_mix, v_g_pre_ffn, v_w_gate_up, v_w_down, v_g_post_ffn):
    w = dict(g_pre_mix=g_pre_mix, w_in=w_in, ssm_lambda_re=ssm_lambda_re, ssm_lambda_im=ssm_lambda_im,
             ssm_log_dt=ssm_log_dt, ssm_b_re=ssm_b_re, ssm_b_im=ssm_b_im, ssm_c_re=ssm_c_re, ssm_c_im=ssm_c_im,
             ssm_d=ssm_d, w_glu=w_glu, b_glu=b_glu, attn_sinks=attn_sinks, g_ssm_out=g_ssm_out,
             g_attn_out=g_attn_out, w_out=w_out, g_post_mix=g_post_mix, g_pre_ffn=g_pre_ffn,
             w_gate_up=w_gate_up, w_down=w_down, g_post_ffn=g_post_ffn)
    m = dict(g_pre_mix=m_g_pre_mix, w_in=m_w_in, ssm_lambda_re=m_ssm_lambda_re, ssm_lambda_im=m_ssm_lambda_im,
             ssm_log_dt=m_ssm_log_dt, ssm_b_re=m_ssm_b_re, ssm_b_im=m_ssm_b_im, ssm_c_re=m_ssm_c_re,
             ssm_c_im=m_ssm_c_im, ssm_d=m_ssm_d, w_glu=m_w_glu, b_glu=m_b_glu, attn_sinks=m_attn_sinks,
             g_ssm_out=m_g_ssm_out, g_attn_out=m_g_attn_out, w_out=m_w_out, g_post_mix=m_g_post_mix,
             g_pre_ffn=m_g_pre_ffn, w_gate_up=m_w_gate_up, w_down=m_w_down, g_post_ffn=m_g_post_ffn)
    v = dict(g_pre_mix=v_g_pre_mix, w_in=v_w_in, ssm_lambda_re=v_ssm_lambda_re, ssm_lambda_im=v_ssm_lambda_im,
             ssm_log_dt=v_ssm_log_dt, ssm_b_re=v_ssm_b_re, ssm_b_im=v_ssm_b_im, ssm_c_re=v_ssm_c_re,
             ssm_c_im=v_ssm_c_im, ssm_d=v_ssm_d, w_glu=v_w_glu, b_glu=v_b_glu, attn_sinks=v_attn_sinks,
             g_ssm_out=v_g_ssm_out, g_attn_out=v_g_attn_out, w_out=v_w_out, g_post_mix=v_g_post_mix,
             g_pre_ffn=v_g_pre_ffn, w_gate_up=v_w_gate_up, w_down=v_w_down, g_post_ffn=v_g_post_ffn)

    transposed = ("w_in", "w_glu", "w_gate_up")
    shards = [(w[n][0].T if n in transposed else w[n][0]).astype(_BF16) for n in _BIG]
    gather = _send_start(shards, False, "gather_start")
    gather_sends, gather_recvs, gather_srcs, gather_lands, gather_token = gather

    def fetch(names, after):
        idx = [_BIG.index(n) for n in names]
        pick = lambda seq: [seq[i] for i in idx]
        lands = _send_wait(pick(gather_sends), pick(gather_recvs), pick(gather_srcs), pick(gather_lands),
                           gather_token if after is None else after, False, "gather_wait_" + names[0])
        return [a.reshape(-1, a.shape[2]) for a in lands]

    sent = []

    def publish(named):
        names = list(named)
        blocks = [named[n].reshape(N_DEV, -1, named[n].shape[1]) for n in names]
        started = _send_start(blocks, True, "grads_start_" + names[0])
        sent.append((names, started))
        return [started[4]]

    p = {n: w[n] for n in _SMALL}
    loss, grad_x, grads = _local_step(x[0], positions[0], loss_target[0], p, fetch, publish)

    result = {}
    for names, (sends, recvs, srcs, lands, _) in sent:
        parts = _send_wait(sends, recvs, srcs, lands, grad_x, True, "grads_wait_" + names[0])
        for name, part in zip(names, parts):
            if name in transposed:
                part = _sum_parts(part, "sum_" + name).T[None]
            result[name] = [a[None] for a in _adamw(part, w[name][0], m[name][0], v[name][0], "adamw_" + name)]

    small_grads = _pack([grads[n] for n in _SMALL] + [jnp.pad(loss.reshape(1), (0, _LANES - 1))], 0)
    gathered, = _all_gather([small_grads], _F32, "gather_small")
    packed = _adamw(gathered, _pack([w[n] for n in _SMALL], 1), _pack([m[n] for n in _SMALL], 1),
                    _pack([v[n] for n in _SMALL], 1), "adamw_small")
    loss_row = None
    for name in _SMALL:
        result[name] = []
    for arr in packed:
        vals, loss_row = _unpack(arr, [w[n] for n in _SMALL])
        for name, val in zip(_SMALL, vals):
            result[name].append(val)
    total_loss = packed[0][loss_row, 0]

    out = [total_loss, grad_x[None]]
    for kind in range(4):
        out += [result[n][kind] for n in _WEIGHTS]
    return tuple(out)
```

```python
import functools
import math

import numpy as np
import jax
import jax.numpy as jnp
from jax import lax
from jax.experimental import pallas as pl
from jax.experimental.pallas import tpu as pltpu
from jax.experimental.pallas import tpu_sc as plsc

D_MODEL = 1024
SSM_WIDTH = 512
SSM_GROUP = 16
SSM_GROUPS = 32
SSM_STATE = 64
N_STATE = SSM_GROUPS * SSM_STATE
ATTN_WIDTH = 512
HEAD_DIM = 64
N_Q_HEADS = 8
N_KV_HEADS = 2
Q_PER_KV = 4
KV_WIDTH = 128
IN_WIDTH = 1280
BLOCK = 128
ROPE_DIM = 16
ROPE_THETA = 500000.0
D_FF = 2816
NORM_EPS = 1e-6
MASK_VALUE = -1e30
ADAM_LR = 0.001
ADAM_B1 = 0.9
ADAM_B2 = 0.999
ADAM_EPS = 1e-08
ADAM_WD = 0.01
ADAM_STEP = 10

N_DEV = 8
SCAN_CHUNKS = 8
SCAN_COLS = 512
TOKEN_TILE = 256
VMEM_LIMIT = 56 * 1024 * 1024

_F32 = jnp.float32
_BF16 = jnp.bfloat16
_MXU = jnp.bfloat16

_NN = ((1,), (0,))
_NT = ((1,), (1,))
_TN = ((0,), (0,))


def _dot(a, b, dims):
    return lax.dot_general(a.astype(_MXU), b.astype(_MXU), (dims, ((), ())),
                           preferred_element_type=_F32)


def _dot_exact(a, b, dims):
    return lax.dot_general(a.astype(_F32), b.astype(_F32), (dims, ((), ())),
                           precision=lax.Precision.HIGHEST, preferred_element_type=_F32)


def _iota(shape, dim):
    return lax.broadcasted_iota(jnp.int32, shape, dim)


def _rms_fwd(x, g):
    r = lax.rsqrt(jnp.mean(x * x, axis=-1, keepdims=True) + NORM_EPS)
    return x * r * g, r


def _rms_bwd(dy, x, g, r):
    a = dy * g
    xn = x * r
    dx = r * (a - xn * jnp.mean(a * xn, axis=-1, keepdims=True))
    dg = jnp.sum(dy * xn, axis=0, keepdims=True)
    return dx, dg


def _call(body, grid, in_specs, out_specs, out_shape, name, scratch=(), tokens=()):
    params = pltpu.CompilerParams(dimension_semantics=("arbitrary",) * len(grid),
                                  vmem_limit_bytes=VMEM_LIMIT)
    n_in, n_tok = len(in_specs), len(tokens)

    def run(*refs):
        return body(*refs[:n_in], *refs[n_in + n_tok:])

    call = pl.pallas_call(run, grid=grid,
                          in_specs=list(in_specs) + [pl.BlockSpec(memory_space=pl.ANY)] * n_tok,
                          out_specs=out_specs, out_shape=out_shape, scratch_shapes=list(scratch),
                          compiler_params=params, name=name)
    return lambda *args: call(*args, *tokens)


def _rows(tm, n):
    return pl.BlockSpec((tm, n), lambda i: (i, 0))


def _whole(shape):
    nd = len(shape)
    return pl.BlockSpec(shape, lambda i: (0,) * nd)


def _sds(shape, dtype):
    return jax.ShapeDtypeStruct(shape, dtype)


def _tile(L):
    return min(TOKEN_TILE, L)


def _accumulate(ref, val, first):
    @pl.when(first)
    def _():
        ref[...] = val

    @pl.when(jnp.logical_not(first))
    def _():
        ref[...] += val


def _rope_rows():
    half = ROPE_DIM // 2
    inv = (np.float32(ROPE_THETA) ** (-np.arange(half, dtype=np.float32) * np.float32(2.0) / np.float32(ROPE_DIM))).astype(np.float32)
    col = np.arange(KV_WIDTH) % HEAD_DIM
    freq = np.where(col < ROPE_DIM, inv[col % half], 0.0).astype(np.float32)
    sign = np.where(col < half, -1.0, np.where(col < ROPE_DIM, 1.0, 0.0)).astype(np.float32)
    return freq[None, :], sign[None, :]


def _rope_tables(pos_col):
    L = pos_col.shape[0]
    tm = _tile(L)
    freq, sign = _rope_rows()

    def body(pos_ref, freq_ref, sign_ref, cos_ref, sin_ref):
        ang = pos_ref[...].astype(_F32) * freq_ref[...]
        cos_ref[...] = jnp.cos(ang)
        sin_ref[...] = jnp.sin(ang) * sign_ref[...]

    return _call(body, (L // tm,),
                 [_rows(tm, 1), _whole((1, KV_WIDTH)), _whole((1, KV_WIDTH))],
                 [_rows(tm, KV_WIDTH), _rows(tm, KV_WIDTH)],
                 [_sds((L, KV_WIDTH), _F32)] * 2, "rope_tables")(pos_col, jnp.asarray(freq), jnp.asarray(sign))


def _widen(t, width):
    return t if width == KV_WIDTH else jnp.concatenate([t] * (width // KV_WIDTH), axis=1)


def _rope_partner(t):
    w = t.shape[1]
    in_head = _iota((1, w), 1) & (HEAD_DIM - 1)
    second = jnp.where(in_head < ROPE_DIM, pltpu.roll(t, ROPE_DIM // 2, 1), 0.0)
    return jnp.where(in_head < ROPE_DIM // 2, pltpu.roll(t, w - ROPE_DIM // 2, 1), second)


def _rope_apply(t, cos_t, sin_t):
    w = t.shape[1]
    return t * _widen(cos_t, w) + _rope_partner(t) * _widen(sin_t, w)


def _rope_transpose(dt, cos_t, sin_t):
    w = dt.shape[1]
    return dt * _widen(cos_t, w) + _rope_partner(dt * _widen(sin_t, w))


def _in_proj(x, g_pre_mix, w_in, cos_t, sin_t):
    L = x.shape[0]
    tm = _tile(L)

    def body(x_ref, g_ref, w_ref, cos_ref, sin_ref, hn_ref, u_ref, q_ref, k_ref, v_ref):
        hn, _ = _rms_fwd(x_ref[...], g_ref[...])
        hn = hn.astype(_BF16)
        hn_ref[...] = hn
        proj = _dot(hn, w_ref[...], _NT)
        u_ref[...] = proj[:, :SSM_WIDTH]
        q = proj[:, SSM_WIDTH:SSM_WIDTH + ATTN_WIDTH]
        k = proj[:, SSM_WIDTH + ATTN_WIDTH:SSM_WIDTH + ATTN_WIDTH + KV_WIDTH]
        cos_v, sin_v = cos_ref[...], sin_ref[...]
        q_ref[...] = _rope_apply(q, cos_v, sin_v).astype(_BF16)
        k_ref[...] = _rope_apply(k, cos_v, sin_v).astype(_BF16)
        v_ref[...] = proj[:, SSM_WIDTH + ATTN_WIDTH + KV_WIDTH:].astype(_BF16)

    return _call(body, (L // tm,),
                 [_rows(tm, D_MODEL), _whole((1, D_MODEL)), _whole((IN_WIDTH, D_MODEL)),
                  _rows(tm, KV_WIDTH), _rows(tm, KV_WIDTH)],
                 [_rows(tm, D_MODEL), _rows(tm, SSM_WIDTH), _rows(tm, ATTN_WIDTH),
                  _rows(tm, KV_WIDTH), _rows(tm, KV_WIDTH)],
                 [_sds((L, D_MODEL), _BF16), _sds((L, SSM_WIDTH), _F32), _sds((L, ATTN_WIDTH), _BF16),
                  _sds((L, KV_WIDTH), _BF16), _sds((L, KV_WIDTH), _BF16)],
                 "in_proj")(x, g_pre_mix, w_in, cos_t, sin_t)


def _s5_discretize(lam_re, lam_im, log_dt):
    lr = jnp.minimum(lam_re, -1e-4)
    li = lam_im
    dt = jnp.exp(log_dt)
    mag = jnp.exp(lr * dt)
    ar = mag * jnp.cos(li * dt)
    ai = mag * jnp.sin(li * dt)
    den = lr * lr + li * li
    fr = ((ar - 1.0) * lr + ai * li) / den
    fi = (ai * lr - (ar - 1.0) * li) / den
    return ar, ai, fr, fi


def _s5_bbar(lam_re, lam_im, log_dt, b_re, b_im):
    ar, ai, fr, fi = _s5_discretize(lam_re, lam_im, log_dt)
    return ar, ai, fr * b_re - fi * b_im, fr * b_im + fi * b_re


def _spread_masks():
    e16 = (_iota((SSM_GROUP, SSM_WIDTH), 1) & (SSM_GROUP - 1)) == _iota((SSM_GROUP, SSM_WIDTH), 0)
    e64 = (_iota((SSM_STATE, N_STATE), 1) & (SSM_STATE - 1)) == _iota((SSM_STATE, N_STATE), 0)
    mask_b = (_iota((N_STATE, SSM_WIDTH), 0) >> 6) == (_iota((N_STATE, SSM_WIDTH), 1) >> 4)
    mask_c = (_iota((SSM_WIDTH, N_STATE), 0) >> 4) == (_iota((SSM_WIDTH, N_STATE), 1) >> 6)
    return e16.astype(_F32), e64.astype(_F32), mask_b, mask_c


SUPER = 4
SB_STATE = N_STATE // SUPER
SB_WIDTH = SSM_WIDTH // SUPER


def _sb_state(k):
    return slice(SB_STATE * k, SB_STATE * (k + 1))


def _sb_width(k):
    return slice(SB_WIDTH * k, SB_WIDTH * (k + 1))


def _ssm_prep(lam_re_r, lam_im_r, ldt_r, lam_re_c, lam_im_c, ldt_c, b_re2, b_im2, c_re2, c_im2):
    def body(lrr, lir, ldr, lrc, lic, ldc, bre, bim, cre, cim, ar_ref, ai_ref, btr, bti, ctr, cti):
        ar, ai, _, _ = _s5_discretize(lrr[...], lir[...], ldr[...])
        ar_ref[...] = ar
        ai_ref[...] = ai
        _, _, bbr, bbi = _s5_bbar(lrc[...], lic[...], ldc[...], bre[...], bim[...])
        e16, e64, mask_b, mask_c = _spread_masks()

        def fold_b(bb):
            full = jnp.where(mask_b, _dot(bb, e16, _NN), 0.0)
            return sum(full[:, _sb_width(k)] for k in range(SUPER)).astype(_BF16)

        def fold_c(cc):
            full = jnp.where(mask_c, _dot(cc, e64, _NN), 0.0)
            return sum(full[_sb_width(k), :] for k in range(SUPER)).astype(_BF16)

        btr[...] = fold_b(bbr)
        bti[...] = fold_b(bbi)
        ctr[...] = fold_c(cre[...])
        cti[...] = fold_c(cim[...])

    row = (1, N_STATE)
    ins = [lam_re_r, lam_im_r, ldt_r, lam_re_c, lam_im_c, ldt_c, b_re2, b_im2, c_re2, c_im2]
    return _call(body, (1,), [_whole(a.shape) for a in ins],
                 [_whole(row), _whole(row), _whole((N_STATE, SB_WIDTH)), _whole((N_STATE, SB_WIDTH)),
                  _whole((SB_WIDTH, N_STATE)), _whole((SB_WIDTH, N_STATE))],
                 [_sds(row, _F32), _sds(row, _F32), _sds((N_STATE, SB_WIDTH), _BF16),
                  _sds((N_STATE, SB_WIDTH), _BF16), _sds((SB_WIDTH, N_STATE), _BF16),
                  _sds((SB_WIDTH, N_STATE), _BF16)], "ssm_prep")(*ins)


def _ssm_bu(u, bt_re, bt_im):
    L = u.shape[0]
    tm = _tile(L)

    def body(u_ref, br_ref, bi_ref, or_ref, oi_ref):
        for k in range(SUPER):
            ub = u_ref[:, _sb_width(k)].astype(_BF16)
            or_ref[:, _sb_state(k)] = _dot(ub, br_ref[_sb_state(k), :], _NT)
            oi_ref[:, _sb_state(k)] = _dot(ub, bi_ref[_sb_state(k), :], _NT)

    return _call(body, (L // tm,),
                 [_rows(tm, SSM_WIDTH), _whole((N_STATE, SB_WIDTH)), _whole((N_STATE, SB_WIDTH))],
                 [_rows(tm, N_STATE), _rows(tm, N_STATE)],
                 [_sds((L, N_STATE), _F32)] * 2, "ssm_bu")(u, bt_re, bt_im)


def _complex_power(ar, ai, n):
    def step(_, c):
        pr, pi = c
        return pr * ar - pi * ai, pr * ai + pi * ar
    return lax.fori_loop(0, n, step, (jnp.ones_like(ar), jnp.zeros_like(ai)))


def _chunk_carries(er, ei, pr, pi, reverse):
    rows = _iota(er.shape, 0)
    sr = jnp.zeros_like(pr)
    si = jnp.zeros_like(pi)
    out_r = jnp.zeros_like(er)
    out_i = jnp.zeros_like(ei)
    order = range(SCAN_CHUNKS - 1, 0, -1) if reverse else range(SCAN_CHUNKS - 1)
    for c in order:
        e_r = er[c:c + 1, :]
        e_i = ei[c:c + 1, :]
        sr, si = pr * sr - pi * si + e_r, pr * si + pi * sr + e_i
        nxt = c - 1 if reverse else c + 1
        out_r = jnp.where(rows == nxt, sr, out_r)
        out_i = jnp.where(rows == nxt, si, out_i)
    return out_r, out_i


def _scan_fwd(b_re, b_im, a_re, a_im):
    T = b_re.shape[0]
    W = SCAN_COLS
    blk = pl.BlockSpec((T, SCAN_CHUNKS, W), lambda j: (0, 0, j))
    vec = pl.BlockSpec((1, W), lambda j: (0, j))

    def body(br_ref, bi_ref, ar_ref, ai_ref, xr_ref, xi_ref):
        ar, ai = ar_ref[...], ai_ref[...]
        ar8 = jnp.broadcast_to(ar, (SCAN_CHUNKS, W))
        ai8 = jnp.broadcast_to(ai, (SCAN_CHUNKS, W))

        def local(t, c):
            cr, ci = c
            return ar8 * cr - ai8 * ci + br_ref[t], ar8 * ci + ai8 * cr + bi_ref[t]

        zero = jnp.zeros((SCAN_CHUNKS, W), _F32)
        er, ei = lax.fori_loop(0, T, local, (zero, zero))
        pr, pi = _complex_power(ar, ai, T)
        sr, si = _chunk_carries(er, ei, pr, pi, reverse=False)

        def final(t, c):
            nr, ni = local(t, c)
            xr_ref[t] = nr
            xi_ref[t] = ni
            return nr, ni

        lax.fori_loop(0, T, final, (sr, si))

    shape = _sds(b_re.shape, _F32)
    return _call(body, (N_STATE // W,), [blk, blk, vec, vec], [blk, blk], [shape, shape],
                 "scan_fwd")(b_re, b_im, a_re, a_im)


def _scan_bwd(dx_re, dx_im, x_re, x_im, a_re, a_im, tokens=()):
    T = dx_re.shape[0]
    W = SCAN_COLS
    blk = pl.BlockSpec((T, SCAN_CHUNKS, W), lambda j: (0, 0, j))
    vec = pl.BlockSpec((1, W), lambda j: (0, j))

    def body(dr_ref, di_ref, xr_ref, xi_ref, ar_ref, ai_ref, lr_ref, li_ref, dar_ref, dai_ref):
        ar, ai = ar_ref[...], ai_ref[...]
        ar8 = jnp.broadcast_to(ar, (SCAN_CHUNKS, W))
        ai8 = jnp.broadcast_to(ai, (SCAN_CHUNKS, W))

        def local(t, c):
            cr, ci = c
            return ar8 * cr + ai8 * ci + dr_ref[t], ar8 * ci - ai8 * cr + di_ref[t]

        zero = jnp.zeros((SCAN_CHUNKS, W), _F32)
        er, ei = lax.fori_loop(0, T, lambda k, c: local(T - 1 - k, c), (zero, zero))
        pr, pi = _complex_power(ar, -ai, T)
        sr, si = _chunk_carries(er, ei, pr, pi, reverse=True)

        def grad_a(acc, nr, ni, xpr, xpi):
            return acc[0] + nr * xpr + ni * xpi, acc[1] + ni * xpr - nr * xpi

        def final(k, c):
            t = T - 1 - k
            nr, ni = local(t, c[:2])
            lr_ref[t] = nr
            li_ref[t] = ni
            gr, gi = grad_a(c[2:], nr, ni, xr_ref[t - 1], xi_ref[t - 1])
            return nr, ni, gr, gi

        cr, ci, gr, gi = lax.fori_loop(0, T - 1, final, (sr, si, zero, zero))
        nr, ni = local(0, (cr, ci))
        lr_ref[0] = nr
        li_ref[0] = ni
        first = _iota((SCAN_CHUNKS, W), 0) == 0
        xpr = jnp.where(first, 0.0, pltpu.roll(xr_ref[T - 1], 1, 0))
        xpi = jnp.where(first, 0.0, pltpu.roll(xi_ref[T - 1], 1, 0))
        gr, gi = grad_a((gr, gi), nr, ni, xpr, xpi)
        dar_ref[...] = jnp.sum(gr, axis=0, keepdims=True)
        dai_ref[...] = jnp.sum(gi, axis=0, keepdims=True)

    shape = _sds(dx_re.shape, _F32)
    row = _sds((1, N_STATE), _F32)
    return _call(body, (N_STATE // W,), [blk, blk, blk, blk, vec, vec], [blk, blk, vec, vec],
                 [shape, shape, row, row], "scan_bwd", tokens=tokens)(dx_re, dx_im, x_re, x_im, a_re, a_im)


_GELU_K = math.sqrt(2.0 / math.pi)
_GELU_C = 0.044715


def _gelu(y):
    return 0.5 * y * (1.0 + jnp.tanh(_GELU_K * (y + _GELU_C * y * y * y)))


def _gelu_grad(y):
    t = jnp.tanh(_GELU_K * (y + _GELU_C * y * y * y))
    return 0.5 * (1.0 + t) + 0.5 * y * (1.0 - t * t) * _GELU_K * (1.0 + 3.0 * _GELU_C * y * y)


def _ssm_out(x_re, x_im, u, ct_re, ct_im, d_row, w_glu, b_glu, g_ssm):
    L = u.shape[0]
    tm = _tile(L)

    def body(xr_ref, xi_ref, u_ref, cr_ref, ci_ref, d_ref, w_ref, b_ref, g_ref, y_ref, z_ref, n_ref):
        cx = [_dot(xr_ref[:, _sb_state(k)], cr_ref[:, _sb_state(k)], _NT)
              - _dot(xi_ref[:, _sb_state(k)], ci_ref[:, _sb_state(k)], _NT) for k in range(SUPER)]
        y = jnp.concatenate(cx, axis=1) + d_ref[...] * u_ref[...]
        y_ref[...] = y
        z = _dot(_gelu(y), w_ref[...], _NT) + b_ref[...]
        z_ref[...] = z
        out = z[:, :SSM_WIDTH] * jax.nn.sigmoid(z[:, SSM_WIDTH:])
        n, _ = _rms_fwd(out, g_ref[...])
        n_ref[...] = n.astype(_BF16)

    return _call(body, (L // tm,),
                 [_rows(tm, N_STATE), _rows(tm, N_STATE), _rows(tm, SSM_WIDTH),
                  _whole((SB_WIDTH, N_STATE)), _whole((SB_WIDTH, N_STATE)), _whole((1, SSM_WIDTH)),
                  _whole((2 * SSM_WIDTH, SSM_WIDTH)), _whole((1, 2 * SSM_WIDTH)), _whole((1, SSM_WIDTH))],
                 [_rows(tm, SSM_WIDTH), _rows(tm, 2 * SSM_WIDTH), _rows(tm, SSM_WIDTH)],
                 [_sds((L, SSM_WIDTH), _F32), _sds((L, 2 * SSM_WIDTH), _F32), _sds((L, SSM_WIDTH), _BF16)],
                 "ssm_out")(x_re, x_im, u, ct_re, ct_im, d_row, w_glu, b_glu, g_ssm)


def _ssm_out_bwd(dn, y, z, u, ct_re, ct_im, d_row, w_glu, g_ssm):
    L = u.shape[0]
    tm = _tile(L)

    def body(dn_ref, y_ref, z_ref, u_ref, cr_ref, ci_ref, d_ref, w_ref, g_ref,
             gy_ref, dz_ref, dy_ref, dud_ref, dxr_ref, dxi_ref, dg_ref, db_ref, dd_ref):
        first = pl.program_id(0) == 0
        z = z_ref[...]
        z1, z2 = z[:, :SSM_WIDTH], z[:, SSM_WIDTH:]
        sig = jax.nn.sigmoid(z2)
        out = z1 * sig
        g = g_ref[...]
        _, r = _rms_fwd(out, g)
        dout, dg = _rms_bwd(dn_ref[...], out, g, r)
        _accumulate(dg_ref, dg, first)
        dz = jnp.concatenate([dout * sig, dout * z1 * sig * (1.0 - sig)], axis=1)
        _accumulate(db_ref, jnp.sum(dz, axis=0, keepdims=True), first)
        dzb = dz.astype(_BF16)
        dz_ref[...] = dzb
        y = y_ref[...]
        gy_ref[...] = _gelu(y).astype(_BF16)
        dy = _dot(dzb, w_ref[...], _NN) * _gelu_grad(y)
        u = u_ref[...]
        _accumulate(dd_ref, jnp.sum(dy * u, axis=0, keepdims=True), first)
        dud_ref[...] = d_ref[...] * dy
        dyb = dy.astype(_BF16)
        dy_ref[...] = dyb
        for k in range(SUPER):
            dxr_ref[:, _sb_state(k)] = _dot(dyb[:, _sb_width(k)], cr_ref[:, _sb_state(k)], _NN)
            dxi_ref[:, _sb_state(k)] = -_dot(dyb[:, _sb_width(k)], ci_ref[:, _sb_state(k)], _NN)

    row = _whole((1, SSM_WIDTH))
    return _call(body, (L // tm,),
                 [_rows(tm, SSM_WIDTH), _rows(tm, SSM_WIDTH), _rows(tm, 2 * SSM_WIDTH), _rows(tm, SSM_WIDTH),
                  _whole((SB_WIDTH, N_STATE)), _whole((SB_WIDTH, N_STATE)), row,
                  _whole((2 * SSM_WIDTH, SSM_WIDTH)), row],
                 [_rows(tm, SSM_WIDTH), _rows(tm, 2 * SSM_WIDTH), _rows(tm, SSM_WIDTH), _rows(tm, SSM_WIDTH),
                  _rows(tm, N_STATE), _rows(tm, N_STATE), row, _whole((1, 2 * SSM_WIDTH)), row],
                 [_sds((L, SSM_WIDTH), _BF16), _sds((L, 2 * SSM_WIDTH), _BF16), _sds((L, SSM_WIDTH), _BF16),
                  _sds((L, SSM_WIDTH), _F32), _sds((L, N_STATE), _F32), _sds((L, N_STATE), _F32),
                  _sds((1, SSM_WIDTH), _F32), _sds((1, 2 * SSM_WIDTH), _F32), _sds((1, SSM_WIDTH), _F32)],
                 "ssm_out_bwd")(dn, y, z, u, ct_re, ct_im, d_row, w_glu, g_ssm)


def _ssm_du(lam_re, lam_im, bt_re, bt_im, dud):
    L = dud.shape[0]
    tm = _tile(L)

    def body(lr_ref, li_ref, br_ref, bi_ref, dud_ref, du_ref):
        for k in range(SUPER):
            du_ref[:, _sb_width(k)] = (_dot(lr_ref[:, _sb_state(k)], br_ref[_sb_state(k), :], _NN)
                                       + _dot(li_ref[:, _sb_state(k)], bi_ref[_sb_state(k), :], _NN)
                                       + dud_ref[:, _sb_width(k)])

    return _call(body, (L // tm,),
                 [_rows(tm, N_STATE), _rows(tm, N_STATE), _whole((N_STATE, SB_WIDTH)),
                  _whole((N_STATE, SB_WIDTH)), _rows(tm, SSM_WIDTH)],
                 _rows(tm, SSM_WIDTH), _sds((L, SSM_WIDTH), _F32), "ssm_du")(lam_re, lam_im, bt_re, bt_im, dud)


def _ssm_weight_grads(dy, x_re, x_im, lam_re, lam_im, u):
    L = u.shape[0]

    def body(dy_ref, xr_ref, xi_ref, lr_ref, li_ref, u_ref, dcr_ref, dci_ref, dbr_ref, dbi_ref):
        dyb = dy_ref[...]
        ub = u_ref[...].astype(_BF16)
        dcr_ref[...] = _dot(dyb, xr_ref[...], _TN)
        dci_ref[...] = _dot(dyb, xi_ref[...], _TN)
        dbr_ref[...] = _dot(lr_ref[...], ub, _TN)
        dbi_ref[...] = _dot(li_ref[...], ub, _TN)

    width = pl.BlockSpec((L, SB_WIDTH), lambda k: (0, k))
    state = pl.BlockSpec((L, SB_STATE), lambda k: (0, k))
    out_c = pl.BlockSpec((SB_WIDTH, SB_STATE), lambda k: (0, k))
    out_b = pl.BlockSpec((SB_STATE, SB_WIDTH), lambda k: (k, 0))
    return _call(body, (SUPER,), [width, state, state, state, state, width], [out_c, out_c, out_b, out_b],
                 [_sds((SB_WIDTH, N_STATE), _F32)] * 2 + [_sds((N_STATE, SB_WIDTH), _F32)] * 2,
                 "ssm_weight_grads")(dy, x_re, x_im, lam_re, lam_im, u)


def _ssm_param_bwd(da_re_c, da_im_c, dbt_re, dbt_im, dct_re, dct_im,
                   lam_re_c, lam_im_c, ldt_c, b_re2, b_im2):
    def body(dar, dai, dbr, dbi, dcr, dci, lrc, lic, ldc, bre, bim,
             glr, gli, gdt, gbr, gbi, gcr, gci):
        own_b = ((_iota((N_STATE, SB_WIDTH), 0) >> 6) & 7) == (_iota((N_STATE, SB_WIDTH), 1) >> 4)
        own_c = (_iota((SB_WIDTH, SB_STATE), 0) >> 4) == (_iota((SB_WIDTH, SB_STATE), 1) >> 6)

        def fold_b(ref):
            t = jnp.where(own_b, ref[...], 0.0)
            for shift in (64, 32, 16):
                t = t + pltpu.roll(t, shift, 1)
            return t[:, :SSM_GROUP]

        def fold_c(ref, k):
            t = jnp.where(own_c, ref[:, _sb_state(k)], 0.0)
            t = sum(t[:, 128 * i:128 * (i + 1)] for i in range(SB_STATE // 128))
            return (t + pltpu.roll(t, SSM_STATE, 1))[:, :SSM_STATE]

        dbbr = fold_b(dbr)
        dbbi = fold_b(dbi)
        for k in range(SUPER):
            gcr[_sb_width(k), :] = fold_c(dcr, k)
            gci[_sb_width(k), :] = -fold_c(dci, k)
        _, vjp = jax.vjp(_s5_bbar, lrc[...], lic[...], ldc[...], bre[...], bim[...])
        d_lr, d_li, d_dt, d_br, d_bi = vjp((dar[...], dai[...], dbbr, dbbi))
        glr[...] = d_lr
        gli[...] = d_li
        gbr[...] = d_br
        gbi[...] = d_bi
        groups = (_iota((SSM_GROUPS, N_STATE), 1) >> 6) == _iota((SSM_GROUPS, N_STATE), 0)
        gdt[...] = _dot_exact(groups.astype(_F32), jnp.broadcast_to(d_dt, (N_STATE, 128)), _NN)

    col = (N_STATE, 1)
    ins = [da_re_c, da_im_c, dbt_re, dbt_im, dct_re, dct_im, lam_re_c, lam_im_c, ldt_c, b_re2, b_im2]
    outs = [col, col, (SSM_GROUPS, 128), (N_STATE, SSM_GROUP), (N_STATE, SSM_GROUP),
            (SSM_WIDTH, SSM_STATE), (SSM_WIDTH, SSM_STATE)]
    return _call(body, (1,), [_whole(a.shape) for a in ins], [_whole(s) for s in outs],
                 [_sds(s, _F32) for s in outs], "ssm_param_bwd")(*ins)


def _head_spread(j):
    r = _iota((KV_WIDTH, 256), 0)
    c = _iota((KV_WIDTH, 256), 1)
    return (r == HEAD_DIM * j + (c & (HEAD_DIM - 1))).astype(_BF16)


STACK = Q_PER_KV * BLOCK


def _stack_heads(t):
    lane_head = _iota((1, 256), 1) >> 6
    return jnp.concatenate([jnp.where(lane_head == g, t, jnp.zeros_like(t)) for g in range(Q_PER_KV)], axis=0)


def _unstack_heads(t):
    lane_head = _iota((1, 256), 1) >> 6
    return sum(jnp.where(lane_head == g, t[BLOCK * g:BLOCK * (g + 1)], 0.0) for g in range(Q_PER_KV))


def _stacked_sinks(sink_ref, j):
    block = _iota((STACK, 1), 0) >> 7
    col = jnp.full((STACK, 1), sink_ref[Q_PER_KV * j], _F32)
    for g in range(1, Q_PER_KV):
        col = jnp.where(block == g, sink_ref[Q_PER_KV * j + g], col)
    return col


def _fold_heads(t, j):
    t = t[:, :KV_WIDTH] + t[:, KV_WIDTH:]
    t = t + pltpu.roll(t, HEAD_DIM, 1)
    return jnp.where((_iota((1, KV_WIDTH), 1) >> 6) == j, t, 0.0)


def _attn_scores(q_stacked, kt, blk, sink):
    s = _dot(q_stacked, kt, _NT) * (HEAD_DIM ** -0.5)
    qi = _iota((STACK, 2 * BLOCK), 0) & (BLOCK - 1)
    kj = _iota((STACK, 2 * BLOCK), 1)
    rel = qi + BLOCK - kj
    valid = (rel >= 0) & (rel < BLOCK) & (blk * BLOCK - BLOCK + kj >= 0)
    s = jnp.where(valid, s, MASK_VALUE)
    m = jnp.maximum(jnp.max(s, axis=-1, keepdims=True), sink)
    p = jnp.exp(s - m)
    e_sink = jnp.exp(sink - m)
    den = jnp.sum(p, axis=-1, keepdims=True) + e_sink
    return p / den, e_sink / den


def _attn_specs():
    prev = lambda i: (jnp.maximum(i - 1, 0), 0)
    cur = lambda i: (i, 0)
    kv = [pl.BlockSpec((BLOCK, KV_WIDTH), prev), pl.BlockSpec((BLOCK, KV_WIDTH), cur)]
    return [pl.BlockSpec((BLOCK, ATTN_WIDTH), cur)] + kv + kv


def _attn_fwd(q, k, v, sinks, g_attn):
    L = q.shape[0]

    def body(q_ref, kp_ref, kc_ref, vp_ref, vc_ref, sink_ref, g_ref, o_ref, n_ref):
        blk = pl.program_id(0)
        kwin = jnp.concatenate([kp_ref[...], kc_ref[...]], axis=0)
        vwin = jnp.concatenate([vp_ref[...], vc_ref[...]], axis=0)
        halves = []
        for j in range(N_KV_HEADS):
            spread = _head_spread(j)
            kt = _dot(kwin, spread, _NN).astype(_BF16)
            vt = _dot(vwin, spread, _NN).astype(_BF16)
            qs = _stack_heads(q_ref[:, 256 * j:256 * (j + 1)])
            p, _ = _attn_scores(qs, kt, blk, _stacked_sinks(sink_ref, j))
            halves.append(_unstack_heads(_dot(p, vt, _NN)))
        o = jnp.concatenate(halves, axis=1)
        o_ref[...] = o
        n, _ = _rms_fwd(o, g_ref[...])
        n_ref[...] = n.astype(_BF16)

    cur = lambda i: (i, 0)
    return _call(body, (L // BLOCK,),
                 _attn_specs() + [pl.BlockSpec(memory_space=pltpu.SMEM), _whole((1, ATTN_WIDTH))],
                 [pl.BlockSpec((BLOCK, ATTN_WIDTH), cur)] * 2,
                 [_sds((L, ATTN_WIDTH), _F32), _sds((L, ATTN_WIDTH), _BF16)],
                 "attn_fwd")(q, k, k, v, v, sinks, g_attn)


def _attn_bwd(q, k, v, o, dn, sinks, g_attn):
    L = q.shape[0]

    def body(q_ref, kp_ref, kc_ref, vp_ref, vc_ref, o_ref, dn_ref, sink_ref, g_ref,
             dq_ref, dk_ref, dv_ref, dsink_ref, dg_ref):
        blk = pl.program_id(0)
        first = blk == 0

        @pl.when(first)
        def _():
            dk_ref[...] = jnp.zeros_like(dk_ref)
            dv_ref[...] = jnp.zeros_like(dv_ref)
            dsink_ref[...] = jnp.zeros_like(dsink_ref)

        o = o_ref[...]
        g = g_ref[...]
        _, r = _rms_fwd(o, g)
        do, dg = _rms_bwd(dn_ref[...], o, g, r)
        _accumulate(dg_ref, dg, first)
        kwin = jnp.concatenate([kp_ref[...], kc_ref[...]], axis=0)
        vwin = jnp.concatenate([vp_ref[...], vc_ref[...]], axis=0)
        lane = _iota((1, 128), 1)
        dsink = jnp.zeros((1, 128), _F32)
        dkwin = jnp.zeros((2 * BLOCK, KV_WIDTH), _F32)
        dvwin = jnp.zeros((2 * BLOCK, KV_WIDTH), _F32)
        dq_halves = []
        for j in range(N_KV_HEADS):
            spread = _head_spread(j)
            kt = _dot(kwin, spread, _NN).astype(_BF16)
            vt = _dot(vwin, spread, _NN).astype(_BF16)
            qs = _stack_heads(q_ref[:, 256 * j:256 * (j + 1)])
            dos = _stack_heads(do[:, 256 * j:256 * (j + 1)]).astype(_BF16)
            p, p_sink = _attn_scores(qs, kt, blk, _stacked_sinks(sink_ref, j))
            dp = _dot(dos, vt, _NT)
            delta = jnp.sum(p * dp, axis=-1, keepdims=True)
            ds = (p * (dp - delta) * (HEAD_DIM ** -0.5)).astype(_BF16)
            sink_term = p_sink * delta
            for g in range(Q_PER_KV):
                head_sum = jnp.sum(sink_term[BLOCK * g:BLOCK * (g + 1)], axis=0, keepdims=True)
                dsink = dsink - jnp.where(lane == Q_PER_KV * j + g, head_sum, 0.0)
            dvwin = dvwin + _fold_heads(_dot(p, dos, _TN), j)
            dkwin = dkwin + _fold_heads(_dot(ds, qs, _TN), j)
            dq_halves.append(_unstack_heads(_dot(ds, kt, _NN)))
        dq_ref[...] = jnp.concatenate(dq_halves, axis=1)
        dsink_ref[...] += dsink
        prev = pl.ds(pl.multiple_of(jnp.maximum(blk - 1, 0) * BLOCK, BLOCK), BLOCK)
        cur = pl.ds(pl.multiple_of(blk * BLOCK, BLOCK), BLOCK)
        dk_ref[prev, :] += dkwin[:BLOCK]
        dk_ref[cur, :] += dkwin[BLOCK:]
        dv_ref[prev, :] += dvwin[:BLOCK]
        dv_ref[cur, :] += dvwin[BLOCK:]

    cur = lambda i: (i, 0)
    blk_q = pl.BlockSpec((BLOCK, ATTN_WIDTH), cur)
    return _call(body, (L // BLOCK,),
                 _attn_specs() + [blk_q, blk_q, pl.BlockSpec(memory_space=pltpu.SMEM), _whole((1, ATTN_WIDTH))],
                 [blk_q, _whole((L, KV_WIDTH)), _whole((L, KV_WIDTH)), _whole((1, 128)), _whole((1, ATTN_WIDTH))],
                 [_sds((L, ATTN_WIDTH), _F32), _sds((L, KV_WIDTH), _F32), _sds((L, KV_WIDTH), _F32),
                  _sds((1, 128), _F32), _sds((1, ATTN_WIDTH), _F32)],
                 "attn_bwd")(q, k, k, v, v, o, dn, sinks, g_attn)


def _out_proj(n_ssm, n_attn, x, w_out, g_post_mix, g_pre_ffn):
    L = x.shape[0]
    tm = _tile(L)

    def body(ns_ref, na_ref, x_ref, w_ref, g1_ref, g2_ref, merged_ref, mo_ref, h1_ref, hn2_ref):
        merged = jnp.concatenate([ns_ref[...], na_ref[...]], axis=1)
        merged_ref[...] = merged
        mo = _dot(merged, w_ref[...], _NN)
        mo_ref[...] = mo
        n, _ = _rms_fwd(mo, g1_ref[...])
        h1 = x_ref[...] + n
        h1_ref[...] = h1
        hn2, _ = _rms_fwd(h1, g2_ref[...])
        hn2_ref[...] = hn2.astype(_BF16)

    row = _whole((1, D_MODEL))
    return _call(body, (L // tm,),
                 [_rows(tm, SSM_WIDTH), _rows(tm, ATTN_WIDTH), _rows(tm, D_MODEL), _whole((D_MODEL, D_MODEL)), row, row],
                 [_rows(tm, D_MODEL)] * 4,
                 [_sds((L, D_MODEL), _BF16), _sds((L, D_MODEL), _F32), _sds((L, D_MODEL), _F32), _sds((L, D_MODEL), _BF16)],
                 "out_proj")(n_ssm, n_attn, x, w_out, g_post_mix, g_pre_ffn)


def _ffn(hn2, h1, target, w_gate_up, w_down, g_pre_ffn, g_post_ffn):
    L = h1.shape[0]
    tm = _tile(L)
    half = D_FF // 2

    def body(hn2_ref, h1_ref, tgt_ref, wgu_hbm, wd_hbm, g2_ref, g3_ref,
             act_ref, dgu_ref, dff_ref, dh1_ref, loss_ref, dg3_ref, dg2_ref,
             wgu, wd, gu, sem):
        first = pl.program_id(0) == 0

        @pl.when(first)
        def _():
            c1 = pltpu.make_async_copy(wgu_hbm, wgu, sem.at[0])
            c2 = pltpu.make_async_copy(wd_hbm, wd, sem.at[1])
            c1.start()
            c2.start()
            c1.wait()
            c2.wait()

        hn2 = hn2_ref[...]
        ff = jnp.zeros((tm, D_MODEL), _F32)
        for c in range(2):
            gate = _dot(hn2, wgu[half * c:half * (c + 1), :], _NT)
            up = _dot(hn2, wgu[D_FF + half * c:D_FF + half * (c + 1), :], _NT)
            gu[:, half * c:half * (c + 1)] = gate
            gu[:, D_FF + half * c:D_FF + half * (c + 1)] = up
            act = (gate * jax.nn.sigmoid(gate) * up).astype(_BF16)
            act_ref[:, half * c:half * (c + 1)] = act
            ff = ff + _dot(act, wd[half * c:half * (c + 1), :], _NN)
        g3 = g3_ref[...]
        n, r = _rms_fwd(ff, g3)
        h1 = h1_ref[...]
        err = h1 + n - tgt_ref[...]
        loss = 0.5 * jnp.sum(jnp.mean(err * err, axis=-1, keepdims=True), axis=0, keepdims=True)
        _accumulate(loss_ref, jnp.broadcast_to(loss, (1, 128)), first)
        dh2 = err * (1.0 / D_MODEL)
        dff, dg3 = _rms_bwd(dh2, ff, g3, r)
        _accumulate(dg3_ref, dg3, first)
        dffb = dff.astype(_BF16)
        dff_ref[...] = dffb
        dhn2 = jnp.zeros((tm, D_MODEL), _F32)
        for c in range(2):
            dact = _dot(dffb, wd[half * c:half * (c + 1), :], _NT)
            gate = gu[:, half * c:half * (c + 1)]
            up = gu[:, D_FF + half * c:D_FF + half * (c + 1)]
            sig = jax.nn.sigmoid(gate)
            silu = gate * sig
            dgate = (dact * up * (sig + silu * (1.0 - sig))).astype(_BF16)
            dup = (dact * silu).astype(_BF16)
            dgu_ref[:, half * c:half * (c + 1)] = dgate
            dgu_ref[:, D_FF + half * c:D_FF + half * (c + 1)] = dup
            dhn2 = dhn2 + _dot(dgate, wgu[half * c:half * (c + 1), :], _NN)
            dhn2 = dhn2 + _dot(dup, wgu[D_FF + half * c:D_FF + half * (c + 1), :], _NN)
        g2 = g2_ref[...]
        _, r2 = _rms_fwd(h1, g2)
        dh1, dg2 = _rms_bwd(dhn2, h1, g2, r2)
        _accumulate(dg2_ref, dg2, first)
        dh1_ref[...] = dh2 + dh1

    row = _whole((1, D_MODEL))
    anyspace = pl.BlockSpec(memory_space=pl.ANY)
    return _call(body, (L // tm,),
                 [_rows(tm, D_MODEL), _rows(tm, D_MODEL), _rows(tm, D_MODEL), anyspace, anyspace, row, row],
                 [_rows(tm, D_FF), _rows(tm, 2 * D_FF), _rows(tm, D_MODEL), _rows(tm, D_MODEL),
                  _whole((1, 128)), row, row],
                 [_sds((L, D_FF), _BF16), _sds((L, 2 * D_FF), _BF16), _sds((L, D_MODEL), _BF16),
                  _sds((L, D_MODEL), _F32), _sds((1, 128), _F32), _sds((1, D_MODEL), _F32), _sds((1, D_MODEL), _F32)],
                 "ffn",
                 scratch=[pltpu.VMEM((2 * D_FF, D_MODEL), _BF16), pltpu.VMEM((D_FF, D_MODEL), _BF16),
                          pltpu.VMEM((tm, 2 * D_FF), _F32), pltpu.SemaphoreType.DMA((2,))],
                 )(hn2, h1, target, w_gate_up, w_down, g_pre_ffn, g_post_ffn)


def _out_proj_bwd(dh1, mo, w_out, g_post_mix, tokens=()):
    L = dh1.shape[0]
    tm = _tile(L)

    def body(dh1_ref, mo_ref, w_ref, g_ref, dmo_ref, dns_ref, dna_ref, dg_ref):
        first = pl.program_id(0) == 0
        mo = mo_ref[...]
        g = g_ref[...]
        _, r = _rms_fwd(mo, g)
        dmo, dg = _rms_bwd(dh1_ref[...], mo, g, r)
        _accumulate(dg_ref, dg, first)
        dmob = dmo.astype(_BF16)
        dmo_ref[...] = dmob
        dmerged = _dot(dmob, w_ref[...], _NT)
        dns_ref[...] = dmerged[:, :SSM_WIDTH]
        dna_ref[...] = dmerged[:, SSM_WIDTH:]

    row = _whole((1, D_MODEL))
    return _call(body, (L // tm,),
                 [_rows(tm, D_MODEL), _rows(tm, D_MODEL), _whole((D_MODEL, D_MODEL)), row],
                 [_rows(tm, D_MODEL), _rows(tm, SSM_WIDTH), _rows(tm, ATTN_WIDTH), row],
                 [_sds((L, D_MODEL), _BF16), _sds((L, SSM_WIDTH), _F32), _sds((L, ATTN_WIDTH), _F32),
                  _sds((1, D_MODEL), _F32)],
                 "out_proj_bwd", tokens=tokens)(dh1, mo, w_out, g_post_mix)


def _in_proj_bwd(du, dq, dk, dv, cos_t, sin_t, x, dh1, g_pre_mix, w_in):
    L = x.shape[0]
    tm = _tile(L)

    def body(du_ref, dq_ref, dk_ref, dv_ref, cos_ref, sin_ref, x_ref, dh1_ref, g_ref, w_ref,
             dproj_ref, dx_ref, dg_ref):
        first = pl.program_id(0) == 0
        cos_v, sin_v = cos_ref[...], sin_ref[...]
        dproj = jnp.concatenate([du_ref[...], _rope_transpose(dq_ref[...], cos_v, sin_v),
                                 _rope_transpose(dk_ref[...], cos_v, sin_v), dv_ref[...]], axis=1).astype(_BF16)
        dproj_ref[...] = dproj
        dhn = _dot(dproj, w_ref[...], _NN)
        x = x_ref[...]
        g = g_ref[...]
        _, r = _rms_fwd(x, g)
        dx, dg = _rms_bwd(dhn, x, g, r)
        _accumulate(dg_ref, dg, first)
        dx_ref[...] = dh1_ref[...] + dx

    row = _whole((1, D_MODEL))
    return _call(body, (L // tm,),
                 [_rows(tm, SSM_WIDTH), _rows(tm, ATTN_WIDTH), _rows(tm, KV_WIDTH), _rows(tm, KV_WIDTH),
                  _rows(tm, KV_WIDTH), _rows(tm, KV_WIDTH), _rows(tm, D_MODEL), _rows(tm, D_MODEL), row,
                  _whole((IN_WIDTH, D_MODEL))],
                 [_rows(tm, IN_WIDTH), _rows(tm, D_MODEL), row],
                 [_sds((L, IN_WIDTH), _BF16), _sds((L, D_MODEL), _F32), _sds((1, D_MODEL), _F32)],
                 "in_proj_bwd")(du, dq, dk, dv, cos_t, sin_t, x, dh1, g_pre_mix, w_in)


def _matmul_tn(a, b, out_dtype, name, scale=1.0):
    K, M = a.shape
    N = b.shape[1]
    tm = next(t for t in (512, 256, 128) if M % t == 0)
    tn = next(t for t in (512, 256, 128) if N % t == 0)

    def body(a_ref, b_ref, o_ref):
        acc = _dot(a_ref[...], b_ref[...], _TN)
        o_ref[...] = (acc if scale == 1.0 else acc * scale).astype(out_dtype)

    params = pltpu.CompilerParams(dimension_semantics=("arbitrary", "arbitrary"), vmem_limit_bytes=VMEM_LIMIT)
    return pl.pallas_call(body, grid=(M // tm, N // tn),
                          in_specs=[pl.BlockSpec((K, tm), lambda i, j: (0, i)),
                                    pl.BlockSpec((K, tn), lambda i, j: (0, j))],
                          out_specs=pl.BlockSpec((tm, tn), lambda i, j: (i, j)),
                          out_shape=_sds((M, N), out_dtype), compiler_params=params, name=name)(a, b)


def _to_chunked(a):
    L, n = a.shape
    return a.reshape(SCAN_CHUNKS, L // SCAN_CHUNKS, n).transpose(1, 0, 2).reshape(L, n)


def _from_chunked(a):
    L, n = a.shape
    return a.reshape(L // SCAN_CHUNKS, SCAN_CHUNKS, n).transpose(1, 0, 2).reshape(L, n)


def _local_step(x, pos, target, p, fetch, publish):
    L = x.shape[0]
    T = L // SCAN_CHUNKS
    cos_t, sin_t = _rope_tables(pos.reshape(L, 1))
    w_in, = fetch(("w_in",), None)
    hn, u, q, k, v = _in_proj(x, p["g_pre_mix"], w_in, cos_t, sin_t)

    lam_re_r = p["ssm_lambda_re"].reshape(1, N_STATE)
    lam_im_r = p["ssm_lambda_im"].reshape(1, N_STATE)
    ldt_r = jnp.broadcast_to(p["ssm_log_dt"].reshape(SSM_GROUPS, 1), (SSM_GROUPS, SSM_STATE)).reshape(1, N_STATE)
    lam_re_c, lam_im_c, ldt_c = (a.reshape(N_STATE, 1) for a in (lam_re_r, lam_im_r, ldt_r))
    b_re2 = p["ssm_b_re"].reshape(N_STATE, SSM_GROUP)
    b_im2 = p["ssm_b_im"].reshape(N_STATE, SSM_GROUP)
    c_re2 = p["ssm_c_re"].reshape(SSM_WIDTH, SSM_STATE)
    c_im2 = p["ssm_c_im"].reshape(SSM_WIDTH, SSM_STATE)
    d_row = p["ssm_d"].reshape(1, SSM_WIDTH)
    a_re, a_im, bt_re, bt_im, ct_re, ct_im = _ssm_prep(
        lam_re_r, lam_im_r, ldt_r, lam_re_c, lam_im_c, ldt_c, b_re2, b_im2, c_re2, c_im2)

    u_c = _to_chunked(u)
    bu_re, bu_im = _ssm_bu(u_c, bt_re, bt_im)
    x_re, x_im = _scan_fwd(bu_re.reshape(T, SCAN_CHUNKS, N_STATE), bu_im.reshape(T, SCAN_CHUNKS, N_STATE), a_re, a_im)
    w_glu, = fetch(("w_glu",), x_re)
    y, z, n_ssm_c = _ssm_out(x_re.reshape(L, N_STATE), x_im.reshape(L, N_STATE), u_c, ct_re, ct_im, d_row,
                             w_glu, p["b_glu"], p["g_ssm_out"])
    n_ssm = _from_chunked(n_ssm_c)

    sinks = p["attn_sinks"].reshape(N_Q_HEADS)
    o, n_attn = _attn_fwd(q, k, v, sinks, p["g_attn_out"])
    w_out, = fetch(("w_out",), n_attn)
    merged, mo, h1, hn2 = _out_proj(n_ssm, n_attn, x, w_out, p["g_post_mix"], p["g_pre_ffn"])
    w_gate_up, w_down = fetch(("w_gate_up", "w_down"), hn2)
    act, dgu, dff, dh1, loss, dg_post_ffn, dg_pre_ffn = _ffn(
        hn2, h1, target, w_gate_up, w_down, p["g_pre_ffn"], p["g_post_ffn"])
    grads = {"g_post_ffn": dg_post_ffn, "g_pre_ffn": dg_pre_ffn}
    tokens = publish({"w_down": _matmul_tn(act, dff, _BF16, "grad_w_down"),
                      "w_gate_up": _matmul_tn(dgu, hn2, _BF16, "grad_w_gate_up")})

    dmo, dn_ssm, dn_attn, grads["g_post_mix"] = _out_proj_bwd(dh1, mo, w_out, p["g_post_mix"], tokens)
    grad_w_out = _matmul_tn(merged, dmo, _BF16, "grad_w_out")

    dq, dk, dv, dsink, grads["g_attn_out"] = _attn_bwd(q, k, v, o, dn_attn, sinks, p["g_attn_out"])
    grads["attn_sinks"] = dsink[:, :N_Q_HEADS]

    gy, dz, dy, dud, dx_re, dx_im, grads["g_ssm_out"], grads["b_glu"], dd = _ssm_out_bwd(
        _to_chunked(dn_ssm), y, z, u_c, ct_re, ct_im, d_row, w_glu, p["g_ssm_out"])
    grads["ssm_d"] = dd.reshape(1, SSM_GROUPS, SSM_GROUP)
    tokens = publish({"w_out": grad_w_out, "w_glu": _matmul_tn(dz, gy, _BF16, "grad_w_glu")})
    lam_re, lam_im, da_re, da_im = _scan_bwd(dx_re.reshape(T, SCAN_CHUNKS, N_STATE), dx_im.reshape(T, SCAN_CHUNKS, N_STATE),
                                             x_re, x_im, a_re, a_im, tokens)
    lam_re = lam_re.reshape(L, N_STATE)
    lam_im = lam_im.reshape(L, N_STATE)
    dct_re, dct_im, dbt_re, dbt_im = _ssm_weight_grads(
        dy, x_re.reshape(L, N_STATE), x_im.reshape(L, N_STATE), lam_re, lam_im, u_c)
    g_lr, g_li, g_dt, g_br, g_bi, g_cr, g_ci = _ssm_param_bwd(
        da_re.reshape(N_STATE, 1), da_im.reshape(N_STATE, 1), dbt_re, dbt_im, dct_re, dct_im,
        lam_re_c, lam_im_c, ldt_c, b_re2, b_im2)
    grads["ssm_lambda_re"] = g_lr.reshape(1, SSM_GROUPS, SSM_STATE)
    grads["ssm_lambda_im"] = g_li.reshape(1, SSM_GROUPS, SSM_STATE)
    grads["ssm_log_dt"] = g_dt[:, 0].reshape(1, SSM_GROUPS)
    grads["ssm_b_re"] = g_br.reshape(1, SSM_GROUPS, SSM_STATE, SSM_GROUP)
    grads["ssm_b_im"] = g_bi.reshape(1, SSM_GROUPS, SSM_STATE, SSM_GROUP)
    grads["ssm_c_re"] = g_cr.reshape(1, SSM_GROUPS, SSM_GROUP, SSM_STATE)
    grads["ssm_c_im"] = g_ci.reshape(1, SSM_GROUPS, SSM_GROUP, SSM_STATE)

    du = _from_chunked(_ssm_du(lam_re, lam_im, bt_re, bt_im, dud))
    dproj, grad_x, grads["g_pre_mix"] = _in_proj_bwd(du, dq, dk, dv, cos_t, sin_t, x, dh1, p["g_pre_mix"], w_in)
    publish({"w_in": _matmul_tn(dproj, hn, _BF16, "grad_w_in")})
    return loss[0, 0], grad_x, grads


_MESH = pl.DeviceIdType.MESH
_PEERS = N_DEV - 1


def _mesh_pos():
    return lax.axis_index("x"), lax.axis_index("y"), lax.axis_index("c")


def _dev_index(px, py, pc):
    return 4 * px + 2 * py + pc


def _all_gather(shards, out_dtype, name):
    n = len(shards)

    def body(*refs):
        ins, outs, stages = refs[:n], refs[n:2 * n], refs[2 * n:3 * n]
        send_sems, recv_sems, local_sems = refs[3 * n:]
        x, y, c = _mesh_pos()
        me, sibling = (x, y, c), (x, y, 1 - c)
        chips = [(1 - x, y), (x, 1 - y), (1 - x, 1 - y)]

        def copy(w, k, block, to, src=None):
            slot = outs[w].at[_dev_index(*block)]
            return pltpu.make_async_remote_copy(
                src_ref=slot if src is None else src, dst_ref=slot,
                send_sem=send_sems.at[_PEERS * w + k], recv_sem=recv_sems.at[_PEERS * w + k],
                device_id=to, device_id_type=_MESH)

        for w in range(n):
            stages[w][...] = ins[w][...].astype(out_dtype)
        mine, first, passed = [], [], []
        for w in range(n):
            cp = pltpu.make_async_copy(stages[w], outs[w].at[_dev_index(*me)], local_sems.at[w])
            cp.start()
            mine.append(cp)
            sends = [copy(w, 0, me, sibling, src=stages[w])]
            sends += [copy(w, 1 + j, me, (*chip, c), src=stages[w]) for j, chip in enumerate(chips)]
            for cp in sends:
                cp.start()
            first += sends
        for w in range(n):
            for j, chip in enumerate(chips):
                copy(w, 1 + j, (*chip, c), me).wait_recv()
                cp = copy(w, 4 + j, (*chip, c), sibling)
                cp.start()
                passed.append(cp)
        for w in range(n):
            copy(w, 0, sibling, me).wait_recv()
            for j, chip in enumerate(chips):
                copy(w, 4 + j, (*chip, 1 - c), me).wait_recv()
        for cp in first + passed:
            cp.wait_send()
        for cp in mine:
            cp.wait()

    return pl.pallas_call(
        body, name=name,
        out_shape=[_sds((N_DEV,) + s.shape, out_dtype) for s in shards],
        in_specs=[pl.BlockSpec(memory_space=pltpu.VMEM)] * n,
        out_specs=[pl.BlockSpec(memory_space=pl.ANY)] * n,
        scratch_shapes=[pltpu.VMEM(s.shape, out_dtype) for s in shards]
        + [pltpu.SemaphoreType.DMA((_PEERS * n,)), pltpu.SemaphoreType.DMA((_PEERS * n,)),
           pltpu.SemaphoreType.DMA((n,))],
        compiler_params=pltpu.CompilerParams(vmem_limit_bytes=VMEM_LIMIT),
    )(*shards)


_HBM_SPEC = pl.BlockSpec(memory_space=pltpu.HBM)
_SEM_SPEC = pl.BlockSpec(memory_space=pltpu.SEMAPHORE)
_DATAFLOW = pltpu.SideEffectType.DATAFLOW_SIDE_EFFECTING


def _peer(x, y, c, r):
    return (x ^ ((r >> 2) & 1), y ^ ((r >> 1) & 1), c ^ (r & 1))


def _hbm(a):
    return pltpu.with_memory_space_constraint(a, pltpu.HBM)


def _send_start(sources, blocked, name):
    n = len(sources)
    lands = [lax.empty((N_DEV,) + (s.shape[1:] if blocked else s.shape), s.dtype) for s in sources]

    def body(*refs):
        srcs, zones = refs[:n], refs[n:2 * n]
        send_sems, recv_sems = refs[2 * n:3 * n], refs[3 * n:4 * n]
        token, local_sems = refs[6 * n], refs[6 * n + 1]
        x, y, c = _mesh_pos()
        me = _dev_index(x, y, c)
        local = []
        for w in range(n):
            cp = pltpu.make_async_copy(srcs[w].at[me] if blocked else srcs[w], zones[w].at[me], local_sems.at[w])
            cp.start()
            local.append(cp)
            for r in range(1, N_DEV):
                peer = _peer(x, y, c, r)
                pltpu.make_async_remote_copy(
                    src_ref=srcs[w].at[_dev_index(*peer)] if blocked else srcs[w], dst_ref=zones[w].at[me],
                    send_sem=send_sems[w].at[r - 1], recv_sem=recv_sems[w].at[r - 1],
                    device_id=peer, device_id_type=_MESH).start()
        for cp in local:
            cp.wait()
        token[...] = jnp.zeros_like(token)

    sems = [pltpu.SemaphoreType.DMA((_PEERS,))] * (2 * n)
    out = pl.pallas_call(
        body, name=name,
        out_shape=sems + [pltpu.HBM(a.shape, a.dtype) for a in list(sources) + lands] + [_sds((8, 128), _F32)],
        in_specs=[_HBM_SPEC] * (2 * n),
        out_specs=[_SEM_SPEC] * (2 * n) + [_HBM_SPEC] * (2 * n) + [pl.BlockSpec(memory_space=pltpu.VMEM)],
        input_output_aliases={i: 2 * n + i for i in range(2 * n)},
        scratch_shapes=[pltpu.SemaphoreType.DMA((n,))],
        compiler_params=pltpu.CompilerParams(has_side_effects=_DATAFLOW),
    )(*[_hbm(a) for a in sources], *[_hbm(a) for a in lands])
    return out[:n], out[n:2 * n], out[2 * n:3 * n], out[3 * n:4 * n], out[4 * n]


def _send_wait(send_sems, recv_sems, sources, lands, after, blocked, name):
    n = len(sources)

    def body(*refs):
        srcs, zones = refs[:n], refs[n:2 * n]
        sends, recvs = refs[2 * n:3 * n], refs[3 * n:4 * n]
        x, y, c = _mesh_pos()
        for w in range(n):
            for r in range(1, N_DEV):
                peer = _peer(x, y, c, r)
                idx = _dev_index(*peer)
                cp = pltpu.make_async_remote_copy(
                    src_ref=srcs[w].at[idx] if blocked else srcs[w], dst_ref=zones[w].at[idx],
                    send_sem=sends[w].at[r - 1], recv_sem=recvs[w].at[r - 1],
                    device_id=peer, device_id_type=_MESH)
                cp.wait_send()
                cp.wait_recv()

    out = pl.pallas_call(
        body, name=name,
        out_shape=[pltpu.HBM(a.shape, a.dtype) for a in list(sources) + list(lands)],
        in_specs=[_HBM_SPEC] * (2 * n) + [_SEM_SPEC] * (2 * n) + [pl.BlockSpec(memory_space=pl.ANY)],
        out_specs=[_HBM_SPEC] * (2 * n),
        input_output_aliases={i: i for i in range(2 * n)},
        compiler_params=pltpu.CompilerParams(has_side_effects=_DATAFLOW),
    )(*sources, *lands, *send_sems, *recv_sems, after)
    return out[n:]


def _sequencer_exchange(sources, blocked, name, collective_id):
    n = len(sources)

    def body(*refs):
        srcs, zones = refs[:n], refs[n:2 * n]
        send_sems, recv_sems, local_sems = refs[2 * n:]
        x, y, c = _mesh_pos()
        me = _dev_index(x, y, c)
        barrier = pltpu.get_barrier_semaphore()
        for r in range(1, N_DEV):
            pl.semaphore_signal(barrier, inc=1, device_id=_peer(x, y, c, r), device_id_type=_MESH)
        pl.semaphore_wait(barrier, _PEERS)
        local, sends, recvs = [], [], []
        for w in range(n):
            cp = pltpu.make_async_copy(srcs[w].at[me] if blocked else srcs[w], zones[w].at[me], local_sems.at[w])
            cp.start()
            local.append(cp)
            for r in range(1, N_DEV):
                peer = _peer(x, y, c, r)
                idx = _dev_index(*peer)
                k = _PEERS * w + r - 1
                src = srcs[w].at[idx] if blocked else srcs[w]
                send = pltpu.make_async_remote_copy(
                    src_ref=src, dst_ref=zones[w].at[me], send_sem=send_sems.at[k], recv_sem=recv_sems.at[k],
                    device_id=peer, device_id_type=_MESH)
                send.start()
                sends.append(send)
                recvs.append(pltpu.make_async_remote_copy(
                    src_ref=src, dst_ref=zones[w].at[idx], send_sem=send_sems.at[k], recv_sem=recv_sems.at[k],
                    device_id=peer, device_id_type=_MESH))
        for cp in recvs:
            cp.wait_recv()
        for cp in sends:
            cp.wait_send()
        for cp in local:
            cp.wait()

    return pl.kernel(
        body, name=name,
        out_type=[_sds((N_DEV,) + (s.shape[1:] if blocked else s.shape), s.dtype) for s in sources],
        mesh=plsc.ScalarSubcoreMesh(axis_name="sequencer", num_cores=1),
        scratch_types=[pltpu.SemaphoreType.DMA((_PEERS * n,)), pltpu.SemaphoreType.DMA((_PEERS * n,)),
                       pltpu.SemaphoreType.DMA((n,))],
        compiler_params=pltpu.CompilerParams(collective_id=collective_id),
    )(*sources)


def _row_tile(rows):
    return next(t for t in range(min(rows, 256), 0, -16) if rows % t == 0)


def _sum_parts(parts, name):
    _, rows, cols = parts.shape
    tr = _row_tile(rows)

    def body(p_ref, g_ref):
        g = p_ref[0].astype(_F32)
        for s in range(1, N_DEV):
            g = g + p_ref[s].astype(_F32)
        g_ref[...] = g

    return _call(body, (rows // tr,), [pl.BlockSpec((N_DEV, tr, cols), lambda i: (0, i, 0))],
                 _rows(tr, cols), _sds((rows, cols), _F32), name)(parts)


def _adamw(parts, w, m, v, name):
    rows, cols = w.shape
    tr = _row_tile(rows)
    n_parts = parts.shape[0]

    def body(p_ref, w_ref, m_ref, v_ref, g_ref, d_ref, nm_ref, nv_ref):
        g = p_ref[0].astype(_F32)
        for s in range(1, n_parts):
            g = g + p_ref[s].astype(_F32)
        new_m = ADAM_B1 * m_ref[...] + (1.0 - ADAM_B1) * g
        new_v = ADAM_B2 * v_ref[...] + (1.0 - ADAM_B2) * (g * g)
        m_hat = new_m / (1.0 - ADAM_B1 ** ADAM_STEP)
        v_hat = new_v / (1.0 - ADAM_B2 ** ADAM_STEP)
        g_ref[...] = g
        d_ref[...] = -ADAM_LR * (m_hat / (jnp.sqrt(v_hat) + ADAM_EPS) + ADAM_WD * w_ref[...])
        nm_ref[...] = new_m
        nv_ref[...] = new_v

    blk = _rows(tr, cols)
    return _call(body, (rows // tr,),
                 [pl.BlockSpec((n_parts, tr, cols), lambda i: (0, i, 0)), blk, blk, blk],
                 [blk] * 4, [_sds((rows, cols), _F32)] * 4, name)(parts, w, m, v)


_SMALL = ("g_pre_mix", "ssm_lambda_re", "ssm_lambda_im", "ssm_log_dt", "ssm_b_re", "ssm_b_im",
          "ssm_c_re", "ssm_c_im", "ssm_d", "b_glu", "attn_sinks", "g_ssm_out", "g_attn_out",
          "g_post_mix", "g_pre_ffn", "g_post_ffn")
_BIG = ("w_in", "w_glu", "w_out", "w_gate_up", "w_down")
_WEIGHTS = ("g_pre_mix", "w_in", "ssm_lambda_re", "ssm_lambda_im", "ssm_log_dt", "ssm_b_re", "ssm_b_im",
            "ssm_c_re", "ssm_c_im", "ssm_d", "w_glu", "b_glu", "attn_sinks", "g_ssm_out", "g_attn_out",
            "w_out", "g_post_mix", "g_pre_ffn", "w_gate_up", "w_down", "g_post_ffn")
_LANES = 128


def _pack(arrays, extra_rows):
    rows = []
    for a in arrays:
        flat = a.reshape(-1).astype(_F32)
        pad = (-flat.shape[0]) % _LANES
        rows.append(jnp.pad(flat, (0, pad)).reshape(-1, _LANES))
    packed = jnp.concatenate(rows + [jnp.zeros((extra_rows, _LANES), _F32)], axis=0)
    return jnp.pad(packed, ((0, (-packed.shape[0]) % 16), (0, 0)))


def _unpack(packed, like):
    out, row = [], 0
    for a in like:
        size = int(np.prod(a.shape))
        nrows = -(-size // _LANES)
        out.append(packed[row:row + nrows].reshape(-1)[:size].reshape(a.shape))
        row += nrows
    return out, row


def _to_blocks(a):
    r, c = a.shape
    return a.reshape(r, N_DEV, c // N_DEV).transpose(1, 0, 2)


def _from_blocks(a):
    n, r, c = a.shape
    return a.transpose(1, 0, 2).reshape(r, n * c)


def kernel(x, positions, g_pre_mix, w_in, ssm_lambda_re, ssm_lambda_im, ssm_log_dt, ssm_b_re, ssm_b_im, ssm_c_re, ssm_c_im, ssm_d, w_glu, b_glu, attn_sinks, g_ssm_out, g_attn_out, w_out, g_post_mix, g_pre_ffn, w_gate_up, w_down, g_post_ffn, loss_target, m_g_pre_mix, m_w_in, m_ssm_lambda_re, m_ssm_lambda_im, m_ssm_log_dt, m_ssm_b_re, m_ssm_b_im, m_ssm_c_re, m_ssm_c_im, m_ssm_d, m_w_glu, m_b_glu, m_attn_sinks, m_g_ssm_out, m_g_attn_out, m_w_out, m_g_post_mix, m_g_pre_ffn, m_w_gate_up, m_w_down, m_g_post_ffn, v_g_pre_mix, v_w_in, v_ssm_lambda_re, v_ssm_lambda_im, v_ssm_log_dt, v_ssm_b_re, v_ssm_b_im, v_ssm_c_re, v_ssm_c_im, v_ssm_d, v_w_glu, v_b_glu, v_attn_sinks, v_g_ssm_out, v_g_attn_out, v_w_out, v_g_post_mix, v_g_pre_ffn, v_w_gate_up, v_w_down, v_g_post_ffn):
    w = dict(g_pre_mix=g_pre_mix, w_in=w_in, ssm_lambda_re=ssm_lambda_re, ssm_lambda_im=ssm_lambda_im,
             ssm_log_dt=ssm_log_dt, ssm_b_re=ssm_b_re, ssm_b_im=ssm_b_im, ssm_c_re=ssm_c_re, ssm_c_im=ssm_c_im,
             ssm_d=ssm_d, w_glu=w_glu, b_glu=b_glu, attn_sinks=attn_sinks, g_ssm_out=g_ssm_out,
             g_attn_out=g_attn_out, w_out=w_out, g_post_mix=g_post_mix, g_pre_ffn=g_pre_ffn,
             w_gate_up=w_gate_up, w_down=w_down, g_post_ffn=g_post_ffn)
    m = dict(g_pre_mix=m_g_pre_mix, w_in=m_w_in, ssm_lambda_re=m_ssm_lambda_re, ssm_lambda_im=m_ssm_lambda_im,
             ssm_log_dt=m_ssm_log_dt, ssm_b_re=m_ssm_b_re, ssm_b_im=m_ssm_b_im, ssm_c_re=m_ssm_c_re,
             ssm_c_im=m_ssm_c_im, ssm_d=m_ssm_d, w_glu=m_w_glu, b_glu=m_b_glu, attn_sinks=m_attn_sinks,
             g_ssm_out=m_g_ssm_out, g_attn_out=m_g_attn_out, w_out=m_w_out, g_post_mix=m_g_post_mix,
             g_pre_ffn=m_g_pre_ffn, w_gate_up=m_w_gate_up, w_down=m_w_down, g_post_ffn=m_g_post_ffn)
    v = dict(g_pre_mix=v_g_pre_mix, w_in=v_w_in, ssm_lambda_re=v_ssm_lambda_re, ssm_lambda_im=v_ssm_lambda_im,
             ssm_log_dt=v_ssm_log_dt, ssm_b_re=v_ssm_b_re, ssm_b_im=v_ssm_b_im, ssm_c_re=v_ssm_c_re,
             ssm_c_im=v_ssm_c_im, ssm_d=v_ssm_d, w_glu=v_w_glu, b_glu=v_b_glu, attn_sinks=v_attn_sinks,
             g_ssm_out=v_g_ssm_out, g_attn_out=v_g_attn_out, w_out=v_w_out, g_post_mix=v_g_post_mix,
             g_pre_ffn=v_g_pre_ffn, w_gate_up=v_w_gate_up, w_down=v_w_down, g_post_ffn=v_g_post_ffn)

    transposed = ("w_in", "w_glu", "w_gate_up")
    shard = {n: (w[n][0].T if n in transposed else w[n][0]).astype(_BF16) for n in _BIG}
    gathered = {}
    for cid, names in enumerate((("w_in",), ("w_glu", "w_out"), ("w_gate_up", "w_down")), start=1):
        lands = _sequencer_exchange([shard[n] for n in names], False, "gather_" + names[0], cid)
        gathered.update({n: a.reshape(-1, a.shape[2]) for n, a in zip(names, lands)})

    def fetch(names, after):
        del after
        return [gathered[n] for n in names]

    sent = []

    def publish(named):
        names = list(named)
        blocks = [named[n].reshape(N_DEV, -1, named[n].shape[1]) for n in names]
        sent.append((names, _sequencer_exchange(blocks, True, "grads_" + names[0], 4 + len(sent))))
        return []

    p = {n: w[n] for n in _SMALL}
    loss, grad_x, grads = _local_step(x[0], positions[0], loss_target[0], p, fetch, publish)

    result = {}
    for names, parts in sent:
        for name, part in zip(names, parts):
            if name in transposed:
                part = _sum_parts(part, "sum_" + name).T[None]
            result[name] = [a[None] for a in _adamw(part, w[name][0], m[name][0], v[name][0], "adamw_" + name)]

    small_grads = _pack([grads[n] for n in _SMALL] + [jnp.pad(loss.reshape(1), (0, _LANES - 1))], 0)
    small_parts, = _sequencer_exchange([small_grads], False, "gather_small", 7)
    packed = _adamw(small_parts, _pack([w[n] for n in _SMALL], 1), _pack([m[n] for n in _SMALL], 1),
                    _pack([v[n] for n in _SMALL], 1), "adamw_small")
    loss_row = None
    for name in _SMALL:
        result[name] = []
    for arr in packed:
        vals, loss_row = _unpack(arr, [w[n] for n in _SMALL])
        for name, val in zip(_SMALL, vals):
            result[name].append(val)
    total_loss = packed[0][loss_row, 0]

    out = [total_loss, grad_x[None]]
    for kind in range(4):
        out += [result[n][kind] for n in _WEIGHTS]
    return tuple(out)
```

```python
import functools
import math

import numpy as np
import jax
import jax.numpy as jnp
from jax import lax
from jax.experimental import pallas as pl
from jax.experimental.pallas import tpu as pltpu
from jax.experimental.pallas import tpu_sc as plsc

D_MODEL = 1024
SSM_WIDTH = 512
SSM_GROUP = 16
SSM_GROUPS = 32
SSM_STATE = 64
N_STATE = SSM_GROUPS * SSM_STATE
ATTN_WIDTH = 512
HEAD_DIM = 64
N_Q_HEADS = 8
N_KV_HEADS = 2
Q_PER_KV = 4
KV_WIDTH = 128
IN_WIDTH = 1280
BLOCK = 128
ROPE_DIM = 16
ROPE_THETA = 500000.0
D_FF = 2816
NORM_EPS = 1e-6
MASK_VALUE = -1e30
ADAM_LR = 0.001
ADAM_B1 = 0.9
ADAM_B2 = 0.999
ADAM_EPS = 1e-08
ADAM_WD = 0.01
ADAM_STEP = 10

N_DEV = 8
SCAN_CHUNKS = 8
SCAN_COLS = 512
TOKEN_TILE = 256
VMEM_LIMIT = 56 * 1024 * 1024

_F32 = jnp.float32
_BF16 = jnp.bfloat16
_MXU = jnp.bfloat16

_NN = ((1,), (0,))
_NT = ((1,), (1,))
_TN = ((0,), (0,))


def _dot(a, b, dims):
    return lax.dot_general(a.astype(_MXU), b.astype(_MXU), (dims, ((), ())),
                           preferred_element_type=_F32)


def _dot_exact(a, b, dims):
    return lax.dot_general(a.astype(_F32), b.astype(_F32), (dims, ((), ())),
                           precision=lax.Precision.HIGHEST, preferred_element_type=_F32)


def _iota(shape, dim):
    return lax.broadcasted_iota(jnp.int32, shape, dim)


def _rms_fwd(x, g):
    r = lax.rsqrt(jnp.mean(x * x, axis=-1, keepdims=True) + NORM_EPS)
    return x * r * g, r


def _rms_bwd(dy, x, g, r):
    a = dy * g
    xn = x * r
    dx = r * (a - xn * jnp.mean(a * xn, axis=-1, keepdims=True))
    dg = jnp.sum(dy * xn, axis=0, keepdims=True)
    return dx, dg


def _call(body, grid, in_specs, out_specs, out_shape, name, scratch=(), tokens=()):
    params = pltpu.CompilerParams(dimension_semantics=("arbitrary",) * len(grid),
                                  vmem_limit_bytes=VMEM_LIMIT)
    n_in, n_tok = len(in_specs), len(tokens)

    def run(*refs):
        return body(*refs[:n_in], *refs[n_in + n_tok:])

    call = pl.pallas_call(run, grid=grid,
                          in_specs=list(in_specs) + [pl.BlockSpec(memory_space=pl.ANY)] * n_tok,
                          out_specs=out_specs, out_shape=out_shape, scratch_shapes=list(scratch),
                          compiler_params=params, name=name)
    return lambda *args: call(*args, *tokens)


def _rows(tm, n):
    return pl.BlockSpec((tm, n), lambda i: (i, 0))


def _whole(shape):
    nd = len(shape)
    return pl.BlockSpec(shape, lambda i: (0,) * nd)


def _sds(shape, dtype):
    return jax.ShapeDtypeStruct(shape, dtype)


def _tile(L):
    return min(TOKEN_TILE, L)


def _accumulate(ref, val, first):
    @pl.when(first)
    def _():
        ref[...] = val

    @pl.when(jnp.logical_not(first))
    def _():
        ref[...] += val


def _rope_rows():
    half = ROPE_DIM // 2
    inv = (np.float32(ROPE_THETA) ** (-np.arange(half, dtype=np.float32) * np.float32(2.0) / np.float32(ROPE_DIM))).astype(np.float32)
    col = np.arange(KV_WIDTH) % HEAD_DIM
    freq = np.where(col < ROPE_DIM, inv[col % half], 0.0).astype(np.float32)
    sign = np.where(col < half, -1.0, np.where(col < ROPE_DIM, 1.0, 0.0)).astype(np.float32)
    return freq[None, :], sign[None, :]


def _rope_tables(pos_col):
    L = pos_col.shape[0]
    tm = _tile(L)
    freq, sign = _rope_rows()

    def body(pos_ref, freq_ref, sign_ref, cos_ref, sin_ref):
        ang = pos_ref[...].astype(_F32) * freq_ref[...]
        cos_ref[...] = jnp.cos(ang)
        sin_ref[...] = jnp.sin(ang) * sign_ref[...]

    return _call(body, (L // tm,),
                 [_rows(tm, 1), _whole((1, KV_WIDTH)), _whole((1, KV_WIDTH))],
                 [_rows(tm, KV_WIDTH), _rows(tm, KV_WIDTH)],
                 [_sds((L, KV_WIDTH), _F32)] * 2, "rope_tables")(pos_col, jnp.asarray(freq), jnp.asarray(sign))


def _widen(t, width):
    return t if width == KV_WIDTH else jnp.concatenate([t] * (width // KV_WIDTH), axis=1)


def _rope_partner(t):
    w = t.shape[1]
    in_head = _iota((1, w), 1) & (HEAD_DIM - 1)
    second = jnp.where(in_head < ROPE_DIM, pltpu.roll(t, ROPE_DIM // 2, 1), 0.0)
    return jnp.where(in_head < ROPE_DIM // 2, pltpu.roll(t, w - ROPE_DIM // 2, 1), second)


def _rope_apply(t, cos_t, sin_t):
    w = t.shape[1]
    return t * _widen(cos_t, w) + _rope_partner(t) * _widen(sin_t, w)


def _rope_transpose(dt, cos_t, sin_t):
    w = dt.shape[1]
    return dt * _widen(cos_t, w) + _rope_partner(dt * _widen(sin_t, w))


def _in_proj(x, g_pre_mix, w_in, cos_t, sin_t):
    L = x.shape[0]
    tm = _tile(L)

    def body(x_ref, g_ref, w_ref, cos_ref, sin_ref, hn_ref, u_ref, q_ref, k_ref, v_ref):
        hn, _ = _rms_fwd(x_ref[...], g_ref[...])
        hn = hn.astype(_BF16)
        hn_ref[...] = hn
        proj = _dot(hn, w_ref[...], _NT)
        u_ref[...] = proj[:, :SSM_WIDTH]
        q = proj[:, SSM_WIDTH:SSM_WIDTH + ATTN_WIDTH]
        k = proj[:, SSM_WIDTH + ATTN_WIDTH:SSM_WIDTH + ATTN_WIDTH + KV_WIDTH]
        cos_v, sin_v = cos_ref[...], sin_ref[...]
        q_ref[...] = _rope_apply(q, cos_v, sin_v).astype(_BF16)
        k_ref[...] = _rope_apply(k, cos_v, sin_v).astype(_BF16)
        v_ref[...] = proj[:, SSM_WIDTH + ATTN_WIDTH + KV_WIDTH:].astype(_BF16)

    return _call(body, (L // tm,),
                 [_rows(tm, D_MODEL), _whole((1, D_MODEL)), _whole((IN_WIDTH, D_MODEL)),
                  _rows(tm, KV_WIDTH), _rows(tm, KV_WIDTH)],
                 [_rows(tm, D_MODEL), _rows(tm, SSM_WIDTH), _rows(tm, ATTN_WIDTH),
                  _rows(tm, KV_WIDTH), _rows(tm, KV_WIDTH)],
                 [_sds((L, D_MODEL), _BF16), _sds((L, SSM_WIDTH), _F32), _sds((L, ATTN_WIDTH), _BF16),
                  _sds((L, KV_WIDTH), _BF16), _sds((L, KV_WIDTH), _BF16)],
                 "in_proj")(x, g_pre_mix, w_in, cos_t, sin_t)


def _s5_discretize(lam_re, lam_im, log_dt):
    lr = jnp.minimum(lam_re, -1e-4)
    li = lam_im
    dt = jnp.exp(log_dt)
    mag = jnp.exp(lr * dt)
    ar = mag * jnp.cos(li * dt)
    ai = mag * jnp.sin(li * dt)
    den = lr * lr + li * li
    fr = ((ar - 1.0) * lr + ai * li) / den
    fi = (ai * lr - (ar - 1.0) * li) / den
    return ar, ai, fr, fi


def _s5_bbar(lam_re, lam_im, log_dt, b_re, b_im):
    ar, ai, fr, fi = _s5_discretize(lam_re, lam_im, log_dt)
    return ar, ai, fr * b_re - fi * b_im, fr * b_im + fi * b_re


def _spread_masks():
    e16 = (_iota((SSM_GROUP, SSM_WIDTH), 1) & (SSM_GROUP - 1)) == _iota((SSM_GROUP, SSM_WIDTH), 0)
    e64 = (_iota((SSM_STATE, N_STATE), 1) & (SSM_STATE - 1)) == _iota((SSM_STATE, N_STATE), 0)
    mask_b = (_iota((N_STATE, SSM_WIDTH), 0) >> 6) == (_iota((N_STATE, SSM_WIDTH), 1) >> 4)
    mask_c = (_iota((SSM_WIDTH, N_STATE), 0) >> 4) == (_iota((SSM_WIDTH, N_STATE), 1) >> 6)
    return e16.astype(_F32), e64.astype(_F32), mask_b, mask_c


SUPER = 4
SB_STATE = N_STATE // SUPER
SB_WIDTH = SSM_WIDTH // SUPER


def _sb_state(k):
    return slice(SB_STATE * k, SB_STATE * (k + 1))


def _sb_width(k):
    return slice(SB_WIDTH * k, SB_WIDTH * (k + 1))


def _ssm_prep(lam_re_r, lam_im_r, ldt_r, lam_re_c, lam_im_c, ldt_c, b_re2, b_im2, c_re2, c_im2):
    def body(lrr, lir, ldr, lrc, lic, ldc, bre, bim, cre, cim, ar_ref, ai_ref, btr, bti, ctr, cti):
        ar, ai, _, _ = _s5_discretize(lrr[...], lir[...], ldr[...])
        ar_ref[...] = ar
        ai_ref[...] = ai
        _, _, bbr, bbi = _s5_bbar(lrc[...], lic[...], ldc[...], bre[...], bim[...])
        e16, e64, mask_b, mask_c = _spread_masks()

        def fold_b(bb):
            full = jnp.where(mask_b, _dot(bb, e16, _NN), 0.0)
            return sum(full[:, _sb_width(k)] for k in range(SUPER)).astype(_BF16)

        def fold_c(cc):
            full = jnp.where(mask_c, _dot(cc, e64, _NN), 0.0)
            return sum(full[_sb_width(k), :] for k in range(SUPER)).astype(_BF16)

        btr[...] = fold_b(bbr)
        bti[...] = fold_b(bbi)
        ctr[...] = fold_c(cre[...])
        cti[...] = fold_c(cim[...])

    row = (1, N_STATE)
    ins = [lam_re_r, lam_im_r, ldt_r, lam_re_c, lam_im_c, ldt_c, b_re2, b_im2, c_re2, c_im2]
    return _call(body, (1,), [_whole(a.shape) for a in ins],
                 [_whole(row), _whole(row), _whole((N_STATE, SB_WIDTH)), _whole((N_STATE, SB_WIDTH)),
                  _whole((SB_WIDTH, N_STATE)), _whole((SB_WIDTH, N_STATE))],
                 [_sds(row, _F32), _sds(row, _F32), _sds((N_STATE, SB_WIDTH), _BF16),
                  _sds((N_STATE, SB_WIDTH), _BF16), _sds((SB_WIDTH, N_STATE), _BF16),
                  _sds((SB_WIDTH, N_STATE), _BF16)], "ssm_prep")(*ins)


def _ssm_bu(u, bt_re, bt_im):
    L = u.shape[0]
    tm = _tile(L)

    def body(u_ref, br_ref, bi_ref, or_ref, oi_ref):
        for k in range(SUPER):
            ub = u_ref[:, _sb_width(k)].astype(_BF16)
            or_ref[:, _sb_state(k)] = _dot(ub, br_ref[_sb_state(k), :], _NT)
            oi_ref[:, _sb_state(k)] = _dot(ub, bi_ref[_sb_state(k), :], _NT)

    return _call(body, (L // tm,),
                 [_rows(tm, SSM_WIDTH), _whole((N_STATE, SB_WIDTH)), _whole((N_STATE, SB_WIDTH))],
                 [_rows(tm, N_STATE), _rows(tm, N_STATE)],
                 [_sds((L, N_STATE), _F32)] * 2, "ssm_bu")(u, bt_re, bt_im)


def _complex_power(ar, ai, n):
    def step(_, c):
        pr, pi = c
        return pr * ar - pi * ai, pr * ai + pi * ar
    return lax.fori_loop(0, n, step, (jnp.ones_like(ar), jnp.zeros_like(ai)))


def _chunk_carries(er, ei, pr, pi, reverse):
    rows = _iota(er.shape, 0)
    sr = jnp.zeros_like(pr)
    si = jnp.zeros_like(pi)
    out_r = jnp.zeros_like(er)
    out_i = jnp.zeros_like(ei)
    order = range(SCAN_CHUNKS - 1, 0, -1) if reverse else range(SCAN_CHUNKS - 1)
    for c in order:
        e_r = er[c:c + 1, :]
        e_i = ei[c:c + 1, :]
        sr, si = pr * sr - pi * si + e_r, pr * si + pi * sr + e_i
        nxt = c - 1 if reverse else c + 1
        out_r = jnp.where(rows == nxt, sr, out_r)
        out_i = jnp.where(rows == nxt, si, out_i)
    return out_r, out_i


def _scan_fwd(b_re, b_im, a_re, a_im):
    T = b_re.shape[0]
    W = SCAN_COLS
    blk = pl.BlockSpec((T, SCAN_CHUNKS, W), lambda j: (0, 0, j))
    vec = pl.BlockSpec((1, W), lambda j: (0, j))

    def body(br_ref, bi_ref, ar_ref, ai_ref, xr_ref, xi_ref):
        ar, ai = ar_ref[...], ai_ref[...]
        ar8 = jnp.broadcast_to(ar, (SCAN_CHUNKS, W))
        ai8 = jnp.broadcast_to(ai, (SCAN_CHUNKS, W))

        def local(t, c):
            cr, ci = c
            return ar8 * cr - ai8 * ci + br_ref[t], ar8 * ci + ai8 * cr + bi_ref[t]

        zero = jnp.zeros((SCAN_CHUNKS, W), _F32)
        er, ei = lax.fori_loop(0, T, local, (zero, zero))
        pr, pi = _complex_power(ar, ai, T)
        sr, si = _chunk_carries(er, ei, pr, pi, reverse=False)

        def final(t, c):
            nr, ni = local(t, c)
            xr_ref[t] = nr
            xi_ref[t] = ni
            return nr, ni

        lax.fori_loop(0, T, final, (sr, si))

    shape = _sds(b_re.shape, _F32)
    return _call(body, (N_STATE // W,), [blk, blk, vec, vec], [blk, blk], [shape, shape],
                 "scan_fwd")(b_re, b_im, a_re, a_im)


def _scan_bwd(dx_re, dx_im, x_re, x_im, a_re, a_im, tokens=()):
    T = dx_re.shape[0]
    W = SCAN_COLS
    blk = pl.BlockSpec((T, SCAN_CHUNKS, W), lambda j: (0, 0, j))
    vec = pl.BlockSpec((1, W), lambda j: (0, j))

    def body(dr_ref, di_ref, xr_ref, xi_ref, ar_ref, ai_ref, lr_ref, li_ref, dar_ref, dai_ref):
        ar, ai = ar_ref[...], ai_ref[...]
        ar8 = jnp.broadcast_to(ar, (SCAN_CHUNKS, W))
        ai8 = jnp.broadcast_to(ai, (SCAN_CHUNKS, W))

        def local(t, c):
            cr, ci = c
            return ar8 * cr + ai8 * ci + dr_ref[t], ar8 * ci - ai8 * cr + di_ref[t]

        zero = jnp.zeros((SCAN_CHUNKS, W), _F32)
        er, ei = lax.fori_loop(0, T, lambda k, c: local(T - 1 - k, c), (zero, zero))
        pr, pi = _complex_power(ar, -ai, T)
        sr, si = _chunk_carries(er, ei, pr, pi, reverse=True)

        def grad_a(acc, nr, ni, xpr, xpi):
            return acc[0] + nr * xpr + ni * xpi, acc[1] + ni * xpr - nr * xpi

        def final(k, c):
            t = T - 1 - k
            nr, ni = local(t, c[:2])
            lr_ref[t] = nr
            li_ref[t] = ni
            gr, gi = grad_a(c[2:], nr, ni, xr_ref[t - 1], xi_ref[t - 1])
            return nr, ni, gr, gi

        cr, ci, gr, gi = lax.fori_loop(0, T - 1, final, (sr, si, zero, zero))
        nr, ni = local(0, (cr, ci))
        lr_ref[0] = nr
        li_ref[0] = ni
        first = _iota((SCAN_CHUNKS, W), 0) == 0
        xpr = jnp.where(first, 0.0, pltpu.roll(xr_ref[T - 1], 1, 0))
        xpi = jnp.where(first, 0.0, pltpu.roll(xi_ref[T - 1], 1, 0))
        gr, gi = grad_a((gr, gi), nr, ni, xpr, xpi)
        dar_ref[...] = jnp.sum(gr, axis=0, keepdims=True)
        dai_ref[...] = jnp.sum(gi, axis=0, keepdims=True)

    shape = _sds(dx_re.shape, _F32)
    row = _sds((1, N_STATE), _F32)
    return _call(body, (N_STATE // W,), [blk, blk, blk, blk, vec, vec], [blk, blk, vec, vec],
                 [shape, shape, row, row], "scan_bwd", tokens=tokens)(dx_re, dx_im, x_re, x_im, a_re, a_im)


_GELU_K = math.sqrt(2.0 / math.pi)
_GELU_C = 0.044715


def _gelu(y):
    return 0.5 * y * (1.0 + jnp.tanh(_GELU_K * (y + _GELU_C * y * y * y)))


def _gelu_grad(y):
    t = jnp.tanh(_GELU_K * (y + _GELU_C * y * y * y))
    return 0.5 * (1.0 + t) + 0.5 * y * (1.0 - t * t) * _GELU_K * (1.0 + 3.0 * _GELU_C * y * y)


def _ssm_out(x_re, x_im, u, ct_re, ct_im, d_row, w_glu, b_glu, g_ssm):
    L = u.shape[0]
    tm = _tile(L)

    def body(xr_ref, xi_ref, u_ref, cr_ref, ci_ref, d_ref, w_ref, b_ref, g_ref, y_ref, z_ref, n_ref):
        cx = [_dot(xr_ref[:, _sb_state(k)], cr_ref[:, _sb_state(k)], _NT)
              - _dot(xi_ref[:, _sb_state(k)], ci_ref[:, _sb_state(k)], _NT) for k in range(SUPER)]
        y = jnp.concatenate(cx, axis=1) + d_ref[...] * u_ref[...]
        y_ref[...] = y
        z = _dot(_gelu(y), w_ref[...], _NT) + b_ref[...]
        z_ref[...] = z
        out = z[:, :SSM_WIDTH] * jax.nn.sigmoid(z[:, SSM_WIDTH:])
        n, _ = _rms_fwd(out, g_ref[...])
        n_ref[...] = n.astype(_BF16)

    return _call(body, (L // tm,),
                 [_rows(tm, N_STATE), _rows(tm, N_STATE), _rows(tm, SSM_WIDTH),
                  _whole((SB_WIDTH, N_STATE)), _whole((SB_WIDTH, N_STATE)), _whole((1, SSM_WIDTH)),
                  _whole((2 * SSM_WIDTH, SSM_WIDTH)), _whole((1, 2 * SSM_WIDTH)), _whole((1, SSM_WIDTH))],
                 [_rows(tm, SSM_WIDTH), _rows(tm, 2 * SSM_WIDTH), _rows(tm, SSM_WIDTH)],
                 [_sds((L, SSM_WIDTH), _F32), _sds((L, 2 * SSM_WIDTH), _F32), _sds((L, SSM_WIDTH), _BF16)],
                 "ssm_out")(x_re, x_im, u, ct_re, ct_im, d_row, w_glu, b_glu, g_ssm)


def _ssm_out_bwd(dn, y, z, u, ct_re, ct_im, d_row, w_glu, g_ssm):
    L = u.shape[0]
    tm = _tile(L)

    def body(dn_ref, y_ref, z_ref, u_ref, cr_ref, ci_ref, d_ref, w_ref, g_ref,
             gy_ref, dz_ref, dy_ref, dud_ref, dxr_ref, dxi_ref, dg_ref, db_ref, dd_ref):
        first = pl.program_id(0) == 0
        z = z_ref[...]
        z1, z2 = z[:, :SSM_WIDTH], z[:, SSM_WIDTH:]
        sig = jax.nn.sigmoid(z2)
        out = z1 * sig
        g = g_ref[...]
        _, r = _rms_fwd(out, g)
        dout, dg = _rms_bwd(dn_ref[...], out, g, r)
        _accumulate(dg_ref, dg, first)
        dz = jnp.concatenate([dout * sig, dout * z1 * sig * (1.0 - sig)], axis=1)
        _accumulate(db_ref, jnp.sum(dz, axis=0, keepdims=True), first)
        dzb = dz.astype(_BF16)
        dz_ref[...] = dzb
        y = y_ref[...]
        gy_ref[...] = _gelu(y).astype(_BF16)
        dy = _dot(dzb, w_ref[...], _NN) * _gelu_grad(y)
        u = u_ref[...]
        _accumulate(dd_ref, jnp.sum(dy * u, axis=0, keepdims=True), first)
        dud_ref[...] = d_ref[...] * dy
        dyb = dy.astype(_BF16)
        dy_ref[...] = dyb
        for k in range(SUPER):
            dxr_ref[:, _sb_state(k)] = _dot(dyb[:, _sb_width(k)], cr_ref[:, _sb_state(k)], _NN)
            dxi_ref[:, _sb_state(k)] = -_dot(dyb[:, _sb_width(k)], ci_ref[:, _sb_state(k)], _NN)

    row = _whole((1, SSM_WIDTH))
    return _call(body, (L // tm,),
                 [_rows(tm, SSM_WIDTH), _rows(tm, SSM_WIDTH), _rows(tm, 2 * SSM_WIDTH), _rows(tm, SSM_WIDTH),
                  _whole((SB_WIDTH, N_STATE)), _whole((SB_WIDTH, N_STATE)), row,
                  _whole((2 * SSM_WIDTH, SSM_WIDTH)), row],
                 [_rows(tm, SSM_WIDTH), _rows(tm, 2 * SSM_WIDTH), _rows(tm, SSM_WIDTH), _rows(tm, SSM_WIDTH),
                  _rows(tm, N_STATE), _rows(tm, N_STATE), row, _whole((1, 2 * SSM_WIDTH)), row],
                 [_sds((L, SSM_WIDTH), _BF16), _sds((L, 2 * SSM_WIDTH), _BF16), _sds((L, SSM_WIDTH), _BF16),
                  _sds((L, SSM_WIDTH), _F32), _sds((L, N_STATE), _F32), _sds((L, N_STATE), _F32),
                  _sds((1, SSM_WIDTH), _F32), _sds((1, 2 * SSM_WIDTH), _F32), _sds((1, SSM_WIDTH), _F32)],
                 "ssm_out_bwd")(dn, y, z, u, ct_re, ct_im, d_row, w_glu, g_ssm)


def _ssm_du(lam_re, lam_im, bt_re, bt_im, dud):
    L = dud.shape[0]
    tm = _tile(L)

    def body(lr_ref, li_ref, br_ref, bi_ref, dud_ref, du_ref):
        for k in range(SUPER):
            du_ref[:, _sb_width(k)] = (_dot(lr_ref[:, _sb_state(k)], br_ref[_sb_state(k), :], _NN)
                                       + _dot(li_ref[:, _sb_state(k)], bi_ref[_sb_state(k), :], _NN)
                                       + dud_ref[:, _sb_width(k)])

    return _call(body, (L // tm,),
                 [_rows(tm, N_STATE), _rows(tm, N_STATE), _whole((N_STATE, SB_WIDTH)),
                  _whole((N_STATE, SB_WIDTH)), _rows(tm, SSM_WIDTH)],
                 _rows(tm, SSM_WIDTH), _sds((L, SSM_WIDTH), _F32), "ssm_du")(lam_re, lam_im, bt_re, bt_im, dud)


def _ssm_weight_grads(dy, x_re, x_im, lam_re, lam_im, u):
    L = u.shape[0]

    def body(dy_ref, xr_ref, xi_ref, lr_ref, li_ref, u_ref, dcr_ref, dci_ref, dbr_ref, dbi_ref):
        dyb = dy_ref[...]
        ub = u_ref[...].astype(_BF16)
        dcr_ref[...] = _dot(dyb, xr_ref[...], _TN)
        dci_ref[...] = _dot(dyb, xi_ref[...], _TN)
        dbr_ref[...] = _dot(lr_ref[...], ub, _TN)
        dbi_ref[...] = _dot(li_ref[...], ub, _TN)

    width = pl.BlockSpec((L, SB_WIDTH), lambda k: (0, k))
    state = pl.BlockSpec((L, SB_STATE), lambda k: (0, k))
    out_c = pl.BlockSpec((SB_WIDTH, SB_STATE), lambda k: (0, k))
    out_b = pl.BlockSpec((SB_STATE, SB_WIDTH), lambda k: (k, 0))
    return _call(body, (SUPER,), [width, state, state, state, state, width], [out_c, out_c, out_b, out_b],
                 [_sds((SB_WIDTH, N_STATE), _F32)] * 2 + [_sds((N_STATE, SB_WIDTH), _F32)] * 2,
                 "ssm_weight_grads")(dy, x_re, x_im, lam_re, lam_im, u)


def _ssm_param_bwd(da_re_c, da_im_c, dbt_re, dbt_im, dct_re, dct_im,
                   lam_re_c, lam_im_c, ldt_c, b_re2, b_im2):
    def body(dar, dai, dbr, dbi, dcr, dci, lrc, lic, ldc, bre, bim,
             glr, gli, gdt, gbr, gbi, gcr, gci):
        own_b = ((_iota((N_STATE, SB_WIDTH), 0) >> 6) & 7) == (_iota((N_STATE, SB_WIDTH), 1) >> 4)
        own_c = (_iota((SB_WIDTH, SB_STATE), 0) >> 4) == (_iota((SB_WIDTH, SB_STATE), 1) >> 6)

        def fold_b(ref):
            t = jnp.where(own_b, ref[...], 0.0)
            for shift in (64, 32, 16):
                t = t + pltpu.roll(t, shift, 1)
            return t[:, :SSM_GROUP]

        def fold_c(ref, k):
            t = jnp.where(own_c, ref[:, _sb_state(k)], 0.0)
            t = sum(t[:, 128 * i:128 * (i + 1)] for i in range(SB_STATE // 128))
            return (t + pltpu.roll(t, SSM_STATE, 1))[:, :SSM_STATE]

        dbbr = fold_b(dbr)
        dbbi = fold_b(dbi)
        for k in range(SUPER):
            gcr[_sb_width(k), :] = fold_c(dcr, k)
            gci[_sb_width(k), :] = -fold_c(dci, k)
        _, vjp = jax.vjp(_s5_bbar, lrc[...], lic[...], ldc[...], bre[...], bim[...])
        d_lr, d_li, d_dt, d_br, d_bi = vjp((dar[...], dai[...], dbbr, dbbi))
        glr[...] = d_lr
        gli[...] = d_li
        gbr[...] = d_br
        gbi[...] = d_bi
        groups = (_iota((SSM_GROUPS, N_STATE), 1) >> 6) == _iota((SSM_GROUPS, N_STATE), 0)
        gdt[...] = _dot_exact(groups.astype(_F32), jnp.broadcast_to(d_dt, (N_STATE, 128)), _NN)

    col = (N_STATE, 1)
    ins = [da_re_c, da_im_c, dbt_re, dbt_im, dct_re, dct_im, lam_re_c, lam_im_c, ldt_c, b_re2, b_im2]
    outs = [col, col, (SSM_GROUPS, 128), (N_STATE, SSM_GROUP), (N_STATE, SSM_GROUP),
            (SSM_WIDTH, SSM_STATE), (SSM_WIDTH, SSM_STATE)]
    return _call(body, (1,), [_whole(a.shape) for a in ins], [_whole(s) for s in outs],
                 [_sds(s, _F32) for s in outs], "ssm_param_bwd")(*ins)


def _head_spread(j):
    r = _iota((KV_WIDTH, 256), 0)
    c = _iota((KV_WIDTH, 256), 1)
    return (r == HEAD_DIM * j + (c & (HEAD_DIM - 1))).astype(_BF16)


STACK = Q_PER_KV * BLOCK


def _stack_heads(t):
    lane_head = _iota((1, 256), 1) >> 6
    return jnp.concatenate([jnp.where(lane_head == g, t, jnp.zeros_like(t)) for g in range(Q_PER_KV)], axis=0)


def _unstack_heads(t):
    lane_head = _iota((1, 256), 1) >> 6
    return sum(jnp.where(lane_head == g, t[BLOCK * g:BLOCK * (g + 1)], 0.0) for g in range(Q_PER_KV))


def _stacked_sinks(sink_ref, j):
    block = _iota((STACK, 1), 0) >> 7
    col = jnp.full((STACK, 1), sink_ref[Q_PER_KV * j], _F32)
    for g in range(1, Q_PER_KV):
        col = jnp.where(block == g, sink_ref[Q_PER_KV * j + g], col)
    return col


def _fold_heads(t, j):
    t = t[:, :KV_WIDTH] + t[:, KV_WIDTH:]
    t = t + pltpu.roll(t, HEAD_DIM, 1)
    return jnp.where((_iota((1, KV_WIDTH), 1) >> 6) == j, t, 0.0)


def _attn_scores(q_stacked, kt, blk, sink):
    s = _dot(q_stacked, kt, _NT) * (HEAD_DIM ** -0.5)
    qi = _iota((STACK, 2 * BLOCK), 0) & (BLOCK - 1)
    kj = _iota((STACK, 2 * BLOCK), 1)
    rel = qi + BLOCK - kj
    valid = (rel >= 0) & (rel < BLOCK) & (blk * BLOCK - BLOCK + kj >= 0)
    s = jnp.where(valid, s, MASK_VALUE)
    m = jnp.maximum(jnp.max(s, axis=-1, keepdims=True), sink)
    p = jnp.exp(s - m)
    e_sink = jnp.exp(sink - m)
    den = jnp.sum(p, axis=-1, keepdims=True) + e_sink
    return p / den, e_sink / den


def _attn_specs():
    prev = lambda i: (jnp.maximum(i - 1, 0), 0)
    cur = lambda i: (i, 0)
    kv = [pl.BlockSpec((BLOCK, KV_WIDTH), prev), pl.BlockSpec((BLOCK, KV_WIDTH), cur)]
    return [pl.BlockSpec((BLOCK, ATTN_WIDTH), cur)] + kv + kv


def _attn_fwd(q, k, v, sinks, g_attn):
    L = q.shape[0]

    def body(q_ref, kp_ref, kc_ref, vp_ref, vc_ref, sink_ref, g_ref, o_ref, n_ref):
        blk = pl.program_id(0)
        kwin = jnp.concatenate([kp_ref[...], kc_ref[...]], axis=0)
        vwin = jnp.concatenate([vp_ref[...], vc_ref[...]], axis=0)
        halves = []
        for j in range(N_KV_HEADS):
            spread = _head_spread(j)
            kt = _dot(kwin, spread, _NN).astype(_BF16)
            vt = _dot(vwin, spread, _NN).astype(_BF16)
            qs = _stack_heads(q_ref[:, 256 * j:256 * (j + 1)])
            p, _ = _attn_scores(qs, kt, blk, _stacked_sinks(sink_ref, j))
            halves.append(_unstack_heads(_dot(p, vt, _NN)))
        o = jnp.concatenate(halves, axis=1)
        o_ref[...] = o
        n, _ = _rms_fwd(o, g_ref[...])
        n_ref[...] = n.astype(_BF16)

    cur = lambda i: (i, 0)
    return _call(body, (L // BLOCK,),
                 _attn_specs() + [pl.BlockSpec(memory_space=pltpu.SMEM), _whole((1, ATTN_WIDTH))],
                 [pl.BlockSpec((BLOCK, ATTN_WIDTH), cur)] * 2,
                 [_sds((L, ATTN_WIDTH), _F32), _sds((L, ATTN_WIDTH), _BF16)],
                 "attn_fwd")(q, k, k, v, v, sinks, g_attn)


def _attn_bwd(q, k, v, o, dn, sinks, g_attn):
    L = q.shape[0]

    def body(q_ref, kp_ref, kc_ref, vp_ref, vc_ref, o_ref, dn_ref, sink_ref, g_ref,
             dq_ref, dk_ref, dv_ref, dsink_ref, dg_ref):
        blk = pl.program_id(0)
        first = blk == 0

        @pl.when(first)
        def _():
            dk_ref[...] = jnp.zeros_like(dk_ref)
            dv_ref[...] = jnp.zeros_like(dv_ref)
            dsink_ref[...] = jnp.zeros_like(dsink_ref)

        o = o_ref[...]
        g = g_ref[...]
        _, r = _rms_fwd(o, g)
        do, dg = _rms_bwd(dn_ref[...], o, g, r)
        _accumulate(dg_ref, dg, first)
        kwin = jnp.concatenate([kp_ref[...], kc_ref[...]], axis=0)
        vwin = jnp.concatenate([vp_ref[...], vc_ref[...]], axis=0)
        lane = _iota((1, 128), 1)
        dsink = jnp.zeros((1, 128), _F32)
        dkwin = jnp.zeros((2 * BLOCK, KV_WIDTH), _F32)
        dvwin = jnp.zeros((2 * BLOCK, KV_WIDTH), _F32)
        dq_halves = []
        for j in range(N_KV_HEADS):
            spread = _head_spread(j)
            kt = _dot(kwin, spread, _NN).astype(_BF16)
            vt = _dot(vwin, spread, _NN).astype(_BF16)
            qs = _stack_heads(q_ref[:, 256 * j:256 * (j + 1)])
            dos = _stack_heads(do[:, 256 * j:256 * (j + 1)]).astype(_BF16)
            p, p_sink = _attn_scores(qs, kt, blk, _stacked_sinks(sink_ref, j))
            dp = _dot(dos, vt, _NT)
            delta = jnp.sum(p * dp, axis=-1, keepdims=True)
            ds = (p * (dp - delta) * (HEAD_DIM ** -0.5)).astype(_BF16)
            sink_term = p_sink * delta
            for g in range(Q_PER_KV):
                head_sum = jnp.sum(sink_term[BLOCK * g:BLOCK * (g + 1)], axis=0, keepdims=True)
                dsink = dsink - jnp.where(lane == Q_PER_KV * j + g, head_sum, 0.0)
            dvwin = dvwin + _fold_heads(_dot(p, dos, _TN), j)
            dkwin = dkwin + _fold_heads(_dot(ds, qs, _TN), j)
            dq_halves.append(_unstack_heads(_dot(ds, kt, _NN)))
        dq_ref[...] = jnp.concatenate(dq_halves, axis=1)
        dsink_ref[...] += dsink
        prev = pl.ds(pl.multiple_of(jnp.maximum(blk - 1, 0) * BLOCK, BLOCK), BLOCK)
        cur = pl.ds(pl.multiple_of(blk * BLOCK, BLOCK), BLOCK)
        dk_ref[prev, :] += dkwin[:BLOCK]
        dk_ref[cur, :] += dkwin[BLOCK:]
        dv_ref[prev, :] += dvwin[:BLOCK]
        dv_ref[cur, :] += dvwin[BLOCK:]

    cur = lambda i: (i, 0)
    blk_q = pl.BlockSpec((BLOCK, ATTN_WIDTH), cur)
    return _call(body, (L // BLOCK,),
                 _attn_specs() + [blk_q, blk_q, pl.BlockSpec(memory_space=pltpu.SMEM), _whole((1, ATTN_WIDTH))],
                 [blk_q, _whole((L, KV_WIDTH)), _whole((L, KV_WIDTH)), _whole((1, 128)), _whole((1, ATTN_WIDTH))],
                 [_sds((L, ATTN_WIDTH), _F32), _sds((L, KV_WIDTH), _F32), _sds((L, KV_WIDTH), _F32),
                  _sds((1, 128), _F32), _sds((1, ATTN_WIDTH), _F32)],
                 "attn_bwd")(q, k, k, v, v, o, dn, sinks, g_attn)


def _out_proj(n_ssm, n_attn, x, w_out, g_post_mix, g_pre_ffn):
    L = x.shape[0]
    tm = _tile(L)

    def body(ns_ref, na_ref, x_ref, w_ref, g1_ref, g2_ref, merged_ref, mo_ref, h1_ref, hn2_ref):
        merged = jnp.concatenate([ns_ref[...], na_ref[...]], axis=1)
        merged_ref[...] = merged
        mo = _dot(merged, w_ref[...], _NN)
        mo_ref[...] = mo
        n, _ = _rms_fwd(mo, g1_ref[...])
        h1 = x_ref[...] + n
        h1_ref[...] = h1
        hn2, _ = _rms_fwd(h1, g2_ref[...])
        hn2_ref[...] = hn2.astype(_BF16)

    row = _whole((1, D_MODEL))
    return _call(body, (L // tm,),
                 [_rows(tm, SSM_WIDTH), _rows(tm, ATTN_WIDTH), _rows(tm, D_MODEL), _whole((D_MODEL, D_MODEL)), row, row],
                 [_rows(tm, D_MODEL)] * 4,
                 [_sds((L, D_MODEL), _BF16), _sds((L, D_MODEL), _F32), _sds((L, D_MODEL), _F32), _sds((L, D_MODEL), _BF16)],
                 "out_proj")(n_ssm, n_attn, x, w_out, g_post_mix, g_pre_ffn)


def _ffn(hn2, h1, target, w_gate_up, w_down, g_pre_ffn, g_post_ffn):
    L = h1.shape[0]
    tm = _tile(L)
    half = D_FF // 2

    def body(hn2_ref, h1_ref, tgt_ref, wgu_hbm, wd_hbm, g2_ref, g3_ref,
             act_ref, dgu_ref, dff_ref, dh1_ref, loss_ref, dg3_ref, dg2_ref,
             wgu, wd, gu, sem):
        first = pl.program_id(0) == 0

        @pl.when(first)
        def _():
            c1 = pltpu.make_async_copy(wgu_hbm, wgu, sem.at[0])
            c2 = pltpu.make_async_copy(wd_hbm, wd, sem.at[1])
            c1.start()
            c2.start()
            c1.wait()
            c2.wait()

        hn2 = hn2_ref[...]
        ff = jnp.zeros((tm, D_MODEL), _F32)
        for c in range(2):
            gate = _dot(hn2, wgu[half * c:half * (c + 1), :], _NT)
            up = _dot(hn2, wgu[D_FF + half * c:D_FF + half * (c + 1), :], _NT)
            gu[:, half * c:half * (c + 1)] = gate
            gu[:, D_FF + half * c:D_FF + half * (c + 1)] = up
            act = (gate * jax.nn.sigmoid(gate) * up).astype(_BF16)
            act_ref[:, half * c:half * (c + 1)] = act
            ff = ff + _dot(act, wd[half * c:half * (c + 1), :], _NN)
        g3 = g3_ref[...]
        n, r = _rms_fwd(ff, g3)
        h1 = h1_ref[...]
        err = h1 + n - tgt_ref[...]
        loss = 0.5 * jnp.sum(jnp.mean(err * err, axis=-1, keepdims=True), axis=0, keepdims=True)
        _accumulate(loss_ref, jnp.broadcast_to(loss, (1, 128)), first)
        dh2 = err * (1.0 / D_MODEL)
        dff, dg3 = _rms_bwd(dh2, ff, g3, r)
        _accumulate(dg3_ref, dg3, first)
        dffb = dff.astype(_BF16)
        dff_ref[...] = dffb
        dhn2 = jnp.zeros((tm, D_MODEL), _F32)
        for c in range(2):
            dact = _dot(dffb, wd[half * c:half * (c + 1), :], _NT)
            gate = gu[:, half * c:half * (c + 1)]
            up = gu[:, D_FF + half * c:D_FF + half * (c + 1)]
            sig = jax.nn.sigmoid(gate)
            silu = gate * sig
            dgate = (dact * up * (sig + silu * (1.0 - sig))).astype(_BF16)
            dup = (dact * silu).astype(_BF16)
            dgu_ref[:, half * c:half * (c + 1)] = dgate
            dgu_ref[:, D_FF + half * c:D_FF + half * (c + 1)] = dup
            dhn2 = dhn2 + _dot(dgate, wgu[half * c:half * (c + 1), :], _NN)
            dhn2 = dhn2 + _dot(dup, wgu[D_FF + half * c:D_FF + half * (c + 1), :], _NN)
        g2 = g2_ref[...]
        _, r2 = _rms_fwd(h1, g2)
        dh1, dg2 = _rms_bwd(dhn2, h1, g2, r2)
        _accumulate(dg2_ref, dg2, first)
        dh1_ref[...] = dh2 + dh1

    row = _whole((1, D_MODEL))
    anyspace = pl.BlockSpec(memory_space=pl.ANY)
    return _call(body, (L // tm,),
                 [_rows(tm, D_MODEL), _rows(tm, D_MODEL), _rows(tm, D_MODEL), anyspace, anyspace, row, row],
                 [_rows(tm, D_FF), _rows(tm, 2 * D_FF), _rows(tm, D_MODEL), _rows(tm, D_MODEL),
                  _whole((1, 128)), row, row],
                 [_sds((L, D_FF), _BF16), _sds((L, 2 * D_FF), _BF16), _sds((L, D_MODEL), _BF16),
                  _sds((L, D_MODEL), _F32), _sds((1, 128), _F32), _sds((1, D_MODEL), _F32), _sds((1, D_MODEL), _F32)],
                 "ffn",
                 scratch=[pltpu.VMEM((2 * D_FF, D_MODEL), _BF16), pltpu.VMEM((D_FF, D_MODEL), _BF16),
                          pltpu.VMEM((tm, 2 * D_FF), _F32), pltpu.SemaphoreType.DMA((2,))],
                 )(hn2, h1, target, w_gate_up, w_down, g_pre_ffn, g_post_ffn)


def _out_proj_bwd(dh1, mo, w_out, g_post_mix, tokens=()):
    L = dh1.shape[0]
    tm = _tile(L)

    def body(dh1_ref, mo_ref, w_ref, g_ref, dmo_ref, dns_ref, dna_ref, dg_ref):
        first = pl.program_id(0) == 0
        mo = mo_ref[...]
        g = g_ref[...]
        _, r = _rms_fwd(mo, g)
        dmo, dg = _rms_bwd(dh1_ref[...], mo, g, r)
        _accumulate(dg_ref, dg, first)
        dmob = dmo.astype(_BF16)
        dmo_ref[...] = dmob
        dmerged = _dot(dmob, w_ref[...], _NT)
        dns_ref[...] = dmerged[:, :SSM_WIDTH]
        dna_ref[...] = dmerged[:, SSM_WIDTH:]

    row = _whole((1, D_MODEL))
    return _call(body, (L // tm,),
                 [_rows(tm, D_MODEL), _rows(tm, D_MODEL), _whole((D_MODEL, D_MODEL)), row],
                 [_rows(tm, D_MODEL), _rows(tm, SSM_WIDTH), _rows(tm, ATTN_WIDTH), row],
                 [_sds((L, D_MODEL), _BF16), _sds((L, SSM_WIDTH), _F32), _sds((L, ATTN_WIDTH), _F32),
                  _sds((1, D_MODEL), _F32)],
                 "out_proj_bwd", tokens=tokens)(dh1, mo, w_out, g_post_mix)


def _in_proj_bwd(du, dq, dk, dv, cos_t, sin_t, x, dh1, g_pre_mix, w_in):
    L = x.shape[0]
    tm = _tile(L)

    def body(du_ref, dq_ref, dk_ref, dv_ref, cos_ref, sin_ref, x_ref, dh1_ref, g_ref, w_ref,
             dproj_ref, dx_ref, dg_ref):
        first = pl.program_id(0) == 0
        cos_v, sin_v = cos_ref[...], sin_ref[...]
        dproj = jnp.concatenate([du_ref[...], _rope_transpose(dq_ref[...], cos_v, sin_v),
                                 _rope_transpose(dk_ref[...], cos_v, sin_v), dv_ref[...]], axis=1).astype(_BF16)
        dproj_ref[...] = dproj
        dhn = _dot(dproj, w_ref[...], _NN)
        x = x_ref[...]
        g = g_ref[...]
        _, r = _rms_fwd(x, g)
        dx, dg = _rms_bwd(dhn, x, g, r)
        _accumulate(dg_ref, dg, first)
        dx_ref[...] = dh1_ref[...] + dx

    row = _whole((1, D_MODEL))
    return _call(body, (L // tm,),
                 [_rows(tm, SSM_WIDTH), _rows(tm, ATTN_WIDTH), _rows(tm, KV_WIDTH), _rows(tm, KV_WIDTH),
                  _rows(tm, KV_WIDTH), _rows(tm, KV_WIDTH), _rows(tm, D_MODEL), _rows(tm, D_MODEL), row,
                  _whole((IN_WIDTH, D_MODEL))],
                 [_rows(tm, IN_WIDTH), _rows(tm, D_MODEL), row],
                 [_sds((L, IN_WIDTH), _BF16), _sds((L, D_MODEL), _F32), _sds((1, D_MODEL), _F32)],
                 "in_proj_bwd")(du, dq, dk, dv, cos_t, sin_t, x, dh1, g_pre_mix, w_in)


def _matmul_tn(a, b, out_dtype, name, scale=1.0):
    K, M = a.shape
    N = b.shape[1]
    tm = next(t for t in (512, 256, 128) if M % t == 0)
    tn = next(t for t in (512, 256, 128) if N % t == 0)

    def body(a_ref, b_ref, o_ref):
        acc = _dot(a_ref[...], b_ref[...], _TN)
        o_ref[...] = (acc if scale == 1.0 else acc * scale).astype(out_dtype)

    params = pltpu.CompilerParams(dimension_semantics=("arbitrary", "arbitrary"), vmem_limit_bytes=VMEM_LIMIT)
    return pl.pallas_call(body, grid=(M // tm, N // tn),
                          in_specs=[pl.BlockSpec((K, tm), lambda i, j: (0, i)),
                                    pl.BlockSpec((K, tn), lambda i, j: (0, j))],
                          out_specs=pl.BlockSpec((tm, tn), lambda i, j: (i, j)),
                          out_shape=_sds((M, N), out_dtype), compiler_params=params, name=name)(a, b)


def _to_chunked(a):
    L, n = a.shape
    return a.reshape(SCAN_CHUNKS, L // SCAN_CHUNKS, n).transpose(1, 0, 2).reshape(L, n)


def _from_chunked(a):
    L, n = a.shape
    return a.reshape(L // SCAN_CHUNKS, SCAN_CHUNKS, n).transpose(1, 0, 2).reshape(L, n)


def _local_step(x, pos, target, p, fetch, publish):
    L = x.shape[0]
    T = L // SCAN_CHUNKS
    cos_t, sin_t = _rope_tables(pos.reshape(L, 1))
    w_in, = fetch(("w_in",), None)
    hn, u, q, k, v = _in_proj(x, p["g_pre_mix"], w_in, cos_t, sin_t)

    lam_re_r = p["ssm_lambda_re"].reshape(1, N_STATE)
    lam_im_r = p["ssm_lambda_im"].reshape(1, N_STATE)
    ldt_r = jnp.broadcast_to(p["ssm_log_dt"].reshape(SSM_GROUPS, 1), (SSM_GROUPS, SSM_STATE)).reshape(1, N_STATE)
    lam_re_c, lam_im_c, ldt_c = (a.reshape(N_STATE, 1) for a in (lam_re_r, lam_im_r, ldt_r))
    b_re2 = p["ssm_b_re"].reshape(N_STATE, SSM_GROUP)
    b_im2 = p["ssm_b_im"].reshape(N_STATE, SSM_GROUP)
    c_re2 = p["ssm_c_re"].reshape(SSM_WIDTH, SSM_STATE)
    c_im2 = p["ssm_c_im"].reshape(SSM_WIDTH, SSM_STATE)
    d_row = p["ssm_d"].reshape(1, SSM_WIDTH)
    a_re, a_im, bt_re, bt_im, ct_re, ct_im = _ssm_prep(
        lam_re_r, lam_im_r, ldt_r, lam_re_c, lam_im_c, ldt_c, b_re2, b_im2, c_re2, c_im2)

    u_c = _to_chunked(u)
    bu_re, bu_im = _ssm_bu(u_c, bt_re, bt_im)
    x_re, x_im = _scan_fwd(bu_re.reshape(T, SCAN_CHUNKS, N_STATE), bu_im.reshape(T, SCAN_CHUNKS, N_STATE), a_re, a_im)
    w_glu, = fetch(("w_glu",), x_re)
    y, z, n_ssm_c = _ssm_out(x_re.reshape(L, N_STATE), x_im.reshape(L, N_STATE), u_c, ct_re, ct_im, d_row,
                             w_glu, p["b_glu"], p["g_ssm_out"])
    n_ssm = _from_chunked(n_ssm_c)

    sinks = p["attn_sinks"].reshape(N_Q_HEADS)
    o, n_attn = _attn_fwd(q, k, v, sinks, p["g_attn_out"])
    w_out, = fetch(("w_out",), n_attn)
    merged, mo, h1, hn2 = _out_proj(n_ssm, n_attn, x, w_out, p["g_post_mix"], p["g_pre_ffn"])
    w_gate_up, w_down = fetch(("w_gate_up", "w_down"), hn2)
    act, dgu, dff, dh1, loss, dg_post_ffn, dg_pre_ffn = _ffn(
        hn2, h1, target, w_gate_up, w_down, p["g_pre_ffn"], p["g_post_ffn"])
    grads = {"g_post_ffn": dg_post_ffn, "g_pre_ffn": dg_pre_ffn}
    tokens = publish({"w_down": _matmul_tn(act, dff, _BF16, "grad_w_down"),
                      "w_gate_up": _matmul_tn(dgu, hn2, _BF16, "grad_w_gate_up")})

    dmo, dn_ssm, dn_attn, grads["g_post_mix"] = _out_proj_bwd(dh1, mo, w_out, p["g_post_mix"], tokens)
    grad_w_out = _matmul_tn(merged, dmo, _BF16, "grad_w_out")

    dq, dk, dv, dsink, grads["g_attn_out"] = _attn_bwd(q, k, v, o, dn_attn, sinks, p["g_attn_out"])
    grads["attn_sinks"] = dsink[:, :N_Q_HEADS]

    gy, dz, dy, dud, dx_re, dx_im, grads["g_ssm_out"], grads["b_glu"], dd = _ssm_out_bwd(
        _to_chunked(dn_ssm), y, z, u_c, ct_re, ct_im, d_row, w_glu, p["g_ssm_out"])
    grads["ssm_d"] = dd.reshape(1, SSM_GROUPS, SSM_GROUP)
    tokens = publish({"w_out": grad_w_out, "w_glu": _matmul_tn(dz, gy, _BF16, "grad_w_glu")})
    lam_re, lam_im, da_re, da_im = _scan_bwd(dx_re.reshape(T, SCAN_CHUNKS, N_STATE), dx_im.reshape(T, SCAN_CHUNKS, N_STATE),
                                             x_re, x_im, a_re, a_im, tokens)
    lam_re = lam_re.reshape(L, N_STATE)
    lam_im = lam_im.reshape(L, N_STATE)
    dct_re, dct_im, dbt_re, dbt_im = _ssm_weight_grads(
        dy, x_re.reshape(L, N_STATE), x_im.reshape(L, N_STATE), lam_re, lam_im, u_c)
    g_lr, g_li, g_dt, g_br, g_bi, g_cr, g_ci = _ssm_param_bwd(
        da_re.reshape(N_STATE, 1), da_im.reshape(N_STATE, 1), dbt_re, dbt_im, dct_re, dct_im,
        lam_re_c, lam_im_c, ldt_c, b_re2, b_im2)
    grads["ssm_lambda_re"] = g_lr.reshape(1, SSM_GROUPS, SSM_STATE)
    grads["ssm_lambda_im"] = g_li.reshape(1, SSM_GROUPS, SSM_STATE)
    grads["ssm_log_dt"] = g_dt[:, 0].reshape(1, SSM_GROUPS)
    grads["ssm_b_re"] = g_br.reshape(1, SSM_GROUPS, SSM_STATE, SSM_GROUP)
    grads["ssm_b_im"] = g_bi.reshape(1, SSM_GROUPS, SSM_STATE, SSM_GROUP)
    grads["ssm_c_re"] = g_cr.reshape(1, SSM_GROUPS, SSM_GROUP, SSM_STATE)
    grads["ssm_c_im"] = g_ci.reshape(1, SSM_GROUPS, SSM_GROUP, SSM_STATE)

    du = _from_chunked(_ssm_du(lam_re, lam_im, bt_re, bt_im, dud))
    dproj, grad_x, grads["g_pre_mix"] = _in_proj_bwd(du, dq, dk, dv, cos_t, sin_t, x, dh1, p["g_pre_mix"], w_in)
    publish({"w_in": _matmul_tn(dproj, hn, _BF16, "grad_w_in")})
    return loss[0, 0], grad_x, grads


_MESH = pl.DeviceIdType.MESH
_PEERS = N_DEV - 1


def _mesh_pos():
    return lax.axis_index("x"), lax.axis_index("y"), lax.axis_index("c")


def _dev_index(px, py, pc):
    return 4 * px + 2 * py + pc


def _all_gather(shards, out_dtype, name):
    n = len(shards)

    def body(*refs):
        ins, outs, stages = refs[:n], refs[n:2 * n], refs[2 * n:3 * n]
        send_sems, recv_sems, local_sems = refs[3 * n:]
        x, y, c = _mesh_pos()
        me, sibling = (x, y, c), (x, y, 1 - c)
        chips = [(1 - x, y), (x, 1 - y), (1 - x, 1 - y)]

        def copy(w, k, block, to, src=None):
            slot = outs[w].at[_dev_index(*block)]
            return pltpu.make_async_remote_copy(
                src_ref=slot if src is None else src, dst_ref=slot,
                send_sem=send_sems.at[_PEERS * w + k], recv_sem=recv_sems.at[_PEERS * w + k],
                device_id=to, device_id_type=_MESH)

        for w in range(n):
            stages[w][...] = ins[w][...].astype(out_dtype)
        mine, first, passed = [], [], []
        for w in range(n):
            cp = pltpu.make_async_copy(stages[w], outs[w].at[_dev_index(*me)], local_sems.at[w])
            cp.start()
            mine.append(cp)
            sends = [copy(w, 0, me, sibling, src=stages[w])]
            sends += [copy(w, 1 + j, me, (*chip, c), src=stages[w]) for j, chip in enumerate(chips)]
            for cp in sends:
                cp.start()
            first += sends
        for w in range(n):
            for j, chip in enumerate(chips):
                copy(w, 1 + j, (*chip, c), me).wait_recv()
                cp = copy(w, 4 + j, (*chip, c), sibling)
                cp.start()
                passed.append(cp)
        for w in range(n):
            copy(w, 0, sibling, me).wait_recv()
            for j, chip in enumerate(chips):
                copy(w, 4 + j, (*chip, 1 - c), me).wait_recv()
        for cp in first + passed:
            cp.wait_send()
        for cp in mine:
            cp.wait()

    return pl.pallas_call(
        body, name=name,
        out_shape=[_sds((N_DEV,) + s.shape, out_dtype) for s in shards],
        in_specs=[pl.BlockSpec(memory_space=pltpu.VMEM)] * n,
        out_specs=[pl.BlockSpec(memory_space=pl.ANY)] * n,
        scratch_shapes=[pltpu.VMEM(s.shape, out_dtype) for s in shards]
        + [pltpu.SemaphoreType.DMA((_PEERS * n,)), pltpu.SemaphoreType.DMA((_PEERS * n,)),
           pltpu.SemaphoreType.DMA((n,))],
        compiler_params=pltpu.CompilerParams(vmem_limit_bytes=VMEM_LIMIT),
    )(*shards)


_HBM_SPEC = pl.BlockSpec(memory_space=pltpu.HBM)
_SEM_SPEC = pl.BlockSpec(memory_space=pltpu.SEMAPHORE)
_DATAFLOW = pltpu.SideEffectType.DATAFLOW_SIDE_EFFECTING


def _peer(x, y, c, r):
    return (x ^ ((r >> 2) & 1), y ^ ((r >> 1) & 1), c ^ (r & 1))


def _hbm(a):
    return pltpu.with_memory_space_constraint(a, pltpu.HBM)


def _send_start(sources, blocked, name):
    n = len(sources)
    lands = [lax.empty((N_DEV,) + (s.shape[1:] if blocked else s.shape), s.dtype) for s in sources]

    def body(*refs):
        srcs, zones = refs[:n], refs[n:2 * n]
        send_sems, recv_sems = refs[2 * n:3 * n], refs[3 * n:4 * n]
        token, local_sems = refs[6 * n], refs[6 * n + 1]
        x, y, c = _mesh_pos()
        me = _dev_index(x, y, c)
        local = []
        for w in range(n):
            cp = pltpu.make_async_copy(srcs[w].at[me] if blocked else srcs[w], zones[w].at[me], local_sems.at[w])
            cp.start()
            local.append(cp)
            for r in range(1, N_DEV):
                peer = _peer(x, y, c, r)
                pltpu.make_async_remote_copy(
                    src_ref=srcs[w].at[_dev_index(*peer)] if blocked else srcs[w], dst_ref=zones[w].at[me],
                    send_sem=send_sems[w].at[r - 1], recv_sem=recv_sems[w].at[r - 1],
                    device_id=peer, device_id_type=_MESH).start()
        for cp in local:
            cp.wait()
        token[...] = jnp.zeros_like(token)

    sems = [pltpu.SemaphoreType.DMA((_PEERS,))] * (2 * n)
    out = pl.pallas_call(
        body, name=name,
        out_shape=sems + [pltpu.HBM(a.shape, a.dtype) for a in list(sources) + lands] + [_sds((8, 128), _F32)],
        in_specs=[_HBM_SPEC] * (2 * n),
        out_specs=[_SEM_SPEC] * (2 * n) + [_HBM_SPEC] * (2 * n) + [pl.BlockSpec(memory_space=pltpu.VMEM)],
        input_output_aliases={i: 2 * n + i for i in range(2 * n)},
        scratch_shapes=[pltpu.SemaphoreType.DMA((n,))],
        compiler_params=pltpu.CompilerParams(has_side_effects=_DATAFLOW),
    )(*[_hbm(a) for a in sources], *[_hbm(a) for a in lands])
    return out[:n], out[n:2 * n], out[2 * n:3 * n], out[3 * n:4 * n], out[4 * n]


def _send_wait(send_sems, recv_sems, sources, lands, after, blocked, name):
    n = len(sources)

    def body(*refs):
        srcs, zones = refs[:n], refs[n:2 * n]
        sends, recvs = refs[2 * n:3 * n], refs[3 * n:4 * n]
        x, y, c = _mesh_pos()
        for w in range(n):
            for r in range(1, N_DEV):
                peer = _peer(x, y, c, r)
                idx = _dev_index(*peer)
                cp = pltpu.make_async_remote_copy(
                    src_ref=srcs[w].at[idx] if blocked else srcs[w], dst_ref=zones[w].at[idx],
                    send_sem=sends[w].at[r - 1], recv_sem=recvs[w].at[r - 1],
                    device_id=peer, device_id_type=_MESH)
                cp.wait_send()
                cp.wait_recv()

    out = pl.pallas_call(
        body, name=name,
        out_shape=[pltpu.HBM(a.shape, a.dtype) for a in list(sources) + list(lands)],
        in_specs=[_HBM_SPEC] * (2 * n) + [_SEM_SPEC] * (2 * n) + [pl.BlockSpec(memory_space=pl.ANY)],
        out_specs=[_HBM_SPEC] * (2 * n),
        input_output_aliases={i: i for i in range(2 * n)},
        compiler_params=pltpu.CompilerParams(has_side_effects=_DATAFLOW),
    )(*sources, *lands, *send_sems, *recv_sems, after)
    return out[n:]


def _sequencer_exchange(sources, blocked, name, collective_id):
    n = len(sources)

    def body(*refs):
        srcs, zones = refs[:n], refs[n:2 * n]
        send_sems, recv_sems, local_sems = refs[2 * n:]
        x, y, c = _mesh_pos()
        me = _dev_index(x, y, c)
        barrier = pltpu.get_barrier_semaphore()
        for r in range(1, N_DEV):
            pl.semaphore_signal(barrier, inc=1, device_id=_peer(x, y, c, r), device_id_type=_MESH)
        pl.semaphore_wait(barrier, _PEERS)
        local, sends, recvs = [], [], []
        for w in range(n):
            cp = pltpu.make_async_copy(srcs[w].at[me] if blocked else srcs[w], zones[w].at[me], local_sems.at[w])
            cp.start()
            local.append(cp)
            for r in range(1, N_DEV):
                peer = _peer(x, y, c, r)
                idx = _dev_index(*peer)
                k = _PEERS * w + r - 1
                src = srcs[w].at[idx] if blocked else srcs[w]
                send = pltpu.make_async_remote_copy(
                    src_ref=src, dst_ref=zones[w].at[me], send_sem=send_sems.at[k], recv_sem=recv_sems.at[k],
                    device_id=peer, device_id_type=_MESH)
                send.start()
                sends.append(send)
                recvs.append(pltpu.make_async_remote_copy(
                    src_ref=src, dst_ref=zones[w].at[idx], send_sem=send_sems.at[k], recv_sem=recv_sems.at[k],
                    device_id=peer, device_id_type=_MESH))
        for cp in recvs:
            cp.wait_recv()
        for cp in sends:
            cp.wait_send()
        for cp in local:
            cp.wait()

    return pl.kernel(
        body, name=name,
        out_type=[_sds((N_DEV,) + (s.shape[1:] if blocked else s.shape), s.dtype) for s in sources],
        mesh=plsc.ScalarSubcoreMesh(axis_name="sequencer", num_cores=1),
        scratch_types=[pltpu.SemaphoreType.DMA((_PEERS * n,)), pltpu.SemaphoreType.DMA((_PEERS * n,)),
                       pltpu.SemaphoreType.DMA((n,))],
        compiler_params=pltpu.CompilerParams(collective_id=collective_id),
    )(*sources)


def _sequencer_gather(shards, name, collective_id):
    n = len(shards)
    fan = 4

    def body(*refs):
        srcs, zones = refs[:n], refs[n:2 * n]
        send_sems, recv_sems, local_sems = refs[2 * n:]
        x, y, c = _mesh_pos()
        me, sibling = (x, y, c), (x, y, 1 - c)
        chips = [(1 - x, y), (x, 1 - y), (1 - x, 1 - y)]
        barrier = pltpu.get_barrier_semaphore()
        for peer in [sibling] + [(*chip, c) for chip in chips]:
            pl.semaphore_signal(barrier, inc=1, device_id=peer, device_id_type=_MESH)
        pl.semaphore_wait(barrier, fan)

        def copy(w, k, block, to, src=None):
            slot = zones[w].at[_dev_index(*block)]
            return pltpu.make_async_remote_copy(
                src_ref=slot if src is None else src, dst_ref=slot,
                send_sem=send_sems.at[_PEERS * w + k], recv_sem=recv_sems.at[_PEERS * w + k],
                device_id=to, device_id_type=_MESH)

        mine, first, passed = [], [], []
        for w in range(n):
            cp = pltpu.make_async_copy(srcs[w], zones[w].at[_dev_index(*me)], local_sems.at[w])
            cp.start()
            mine.append(cp)
            sends = [copy(w, 0, me, sibling, src=srcs[w])]
            sends += [copy(w, 1 + j, me, (*chip, c), src=srcs[w]) for j, chip in enumerate(chips)]
            for cp in sends:
                cp.start()
            first += sends
        for w in range(n):
            for j, chip in enumerate(chips):
                copy(w, 1 + j, (*chip, c), me).wait_recv()
                cp = copy(w, fan + j, (*chip, c), sibling)
                cp.start()
                passed.append(cp)
        for w in range(n):
            copy(w, 0, sibling, me).wait_recv()
            for j, chip in enumerate(chips):
                copy(w, fan + j, (*chip, 1 - c), me).wait_recv()
        for cp in first + passed:
            cp.wait_send()
        for cp in mine:
            cp.wait()

    return pl.kernel(
        body, name=name, out_type=[_sds((N_DEV,) + s.shape, s.dtype) for s in shards],
        mesh=plsc.ScalarSubcoreMesh(axis_name="sequencer", num_cores=1),
        scratch_types=[pltpu.SemaphoreType.DMA((_PEERS * n,)), pltpu.SemaphoreType.DMA((_PEERS * n,)),
                       pltpu.SemaphoreType.DMA((n,))],
        compiler_params=pltpu.CompilerParams(collective_id=collective_id),
    )(*shards)


def _row_tile(rows):
    return next(t for t in range(min(rows, 256), 0, -16) if rows % t == 0)


def _sum_parts(parts, name):
    _, rows, cols = parts.shape
    tr = _row_tile(rows)

    def body(p_ref, g_ref):
        g = p_ref[0].astype(_F32)
        for s in range(1, N_DEV):
            g = g + p_ref[s].astype(_F32)
        g_ref[...] = g

    return _call(body, (rows // tr,), [pl.BlockSpec((N_DEV, tr, cols), lambda i: (0, i, 0))],
                 _rows(tr, cols), _sds((rows, cols), _F32), name)(parts)


def _adamw(parts, w, m, v, name):
    rows, cols = w.shape
    tr = _row_tile(rows)
    n_parts = parts.shape[0]

    def body(p_ref, w_ref, m_ref, v_ref, g_ref, d_ref, nm_ref, nv_ref):
        g = p_ref[0].astype(_F32)
        for s in range(1, n_parts):
            g = g + p_ref[s].astype(_F32)
        new_m = ADAM_B1 * m_ref[...] + (1.0 - ADAM_B1) * g
        new_v = ADAM_B2 * v_ref[...] + (1.0 - ADAM_B2) * (g * g)
        m_hat = new_m / (1.0 - ADAM_B1 ** ADAM_STEP)
        v_hat = new_v / (1.0 - ADAM_B2 ** ADAM_STEP)
        g_ref[...] = g
        d_ref[...] = -ADAM_LR * (m_hat / (jnp.sqrt(v_hat) + ADAM_EPS) + ADAM_WD * w_ref[...])
        nm_ref[...] = new_m
        nv_ref[...] = new_v

    blk = _rows(tr, cols)
    return _call(body, (rows // tr,),
                 [pl.BlockSpec((n_parts, tr, cols), lambda i: (0, i, 0)), blk, blk, blk],
                 [blk] * 4, [_sds((rows, cols), _F32)] * 4, name)(parts, w, m, v)


_SMALL = ("g_pre_mix", "ssm_lambda_re", "ssm_lambda_im", "ssm_log_dt", "ssm_b_re", "ssm_b_im",
          "ssm_c_re", "ssm_c_im", "ssm_d", "b_glu", "attn_sinks", "g_ssm_out", "g_attn_out",
          "g_post_mix", "g_pre_ffn", "g_post_ffn")
_BIG = ("w_in", "w_glu", "w_out", "w_gate_up", "w_down")
_WEIGHTS = ("g_pre_mix", "w_in", "ssm_lambda_re", "ssm_lambda_im", "ssm_log_dt", "ssm_b_re", "ssm_b_im",
            "ssm_c_re", "ssm_c_im", "ssm_d", "w_glu", "b_glu", "attn_sinks", "g_ssm_out", "g_attn_out",
            "w_out", "g_post_mix", "g_pre_ffn", "w_gate_up", "w_down", "g_post_ffn")
_LANES = 128


def _pack(arrays, extra_rows):
    rows = []
    for a in arrays:
        flat = a.reshape(-1).astype(_F32)
        pad = (-flat.shape[0]) % _LANES
        rows.append(jnp.pad(flat, (0, pad)).reshape(-1, _LANES))
    packed = jnp.concatenate(rows + [jnp.zeros((extra_rows, _LANES), _F32)], axis=0)
    return jnp.pad(packed, ((0, (-packed.shape[0]) % 16), (0, 0)))


def _unpack(packed, like):
    out, row = [], 0
    for a in like:
        size = int(np.prod(a.shape))
        nrows = -(-size // _LANES)
        out.append(packed[row:row + nrows].reshape(-1)[:size].reshape(a.shape))
        row += nrows
    return out, row


def _to_blocks(a):
    r, c = a.shape
    return a.reshape(r, N_DEV, c // N_DEV).transpose(1, 0, 2)


def _from_blocks(a):
    n, r, c = a.shape
    return a.transpose(1, 0, 2).reshape(r, n * c)


def kernel(x, positions, g_pre_mix, w_in, ssm_lambda_re, ssm_lambda_im, ssm_log_dt, ssm_b_re, ssm_b_im, ssm_c_re, ssm_c_im, ssm_d, w_glu, b_glu, attn_sinks, g_ssm_out, g_attn_out, w_out, g_post_mix, g_pre_ffn, w_gate_up, w_down, g_post_ffn, loss_target, m_g_pre_mix, m_w_in, m_ssm_lambda_re, m_ssm_lambda_im, m_ssm_log_dt, m_ssm_b_re, m_ssm_b_im, m_ssm_c_re, m_ssm_c_im, m_ssm_d, m_w_glu, m_b_glu, m_attn_sinks, m_g_ssm_out, m_g_attn_out, m_w_out, m_g_post_mix, m_g_pre_ffn, m_w_gate_up, m_w_down, m_g_post_ffn, v_g_pre_mix, v_w_in, v_ssm_lambda_re, v_ssm_lambda_im, v_ssm_log_dt, v_ssm_b_re, v_ssm_b_im, v_ssm_c_re, v_ssm_c_im, v_ssm_d, v_w_glu, v_b_glu, v_attn_sinks, v_g_ssm_out, v_g_attn_out, v_w_out, v_g_post_mix, v_g_pre_ffn, v_w_gate_up, v_w_down, v_g_post_ffn):
    w = dict(g_pre_mix=g_pre_mix, w_in=w_in, ssm_lambda_re=ssm_lambda_re, ssm_lambda_im=ssm_lambda_im,
             ssm_log_dt=ssm_log_dt, ssm_b_re=ssm_b_re, ssm_b_im=ssm_b_im, ssm_c_re=ssm_c_re, ssm_c_im=ssm_c_im,
             ssm_d=ssm_d, w_glu=w_glu, b_glu=b_glu, attn_sinks=attn_sinks, g_ssm_out=g_ssm_out,
             g_attn_out=g_attn_out, w_out=w_out, g_post_mix=g_post_mix, g_pre_ffn=g_pre_ffn,
             w_gate_up=w_gate_up, w_down=w_down, g_post_ffn=g_post_ffn)
    m = dict(g_pre_mix=m_g_pre_mix, w_in=m_w_in, ssm_lambda_re=m_ssm_lambda_re, ssm_lambda_im=m_ssm_lambda_im,
             ssm_log_dt=m_ssm_log_dt, ssm_b_re=m_ssm_b_re, ssm_b_im=m_ssm_b_im, ssm_c_re=m_ssm_c_re,
             ssm_c_im=m_ssm_c_im, ssm_d=m_ssm_d, w_glu=m_w_glu, b_glu=m_b_glu, attn_sinks=m_attn_sinks,
             g_ssm_out=m_g_ssm_out, g_attn_out=m_g_attn_out, w_out=m_w_out, g_post_mix=m_g_post_mix,
             g_pre_ffn=m_g_pre_ffn, w_gate_up=m_w_gate_up, w_down=m_w_down, g_post_ffn=m_g_post_ffn)
    v = dict(g_pre_mix=v_g_pre_mix, w_in=v_w_in, ssm_lambda_re=v_ssm_lambda_re, ssm_lambda_im=v_ssm_lambda_im,
             ssm_log_dt=v_ssm_log_dt, ssm_b_re=v_ssm_b_re, ssm_b_im=v_ssm_b_im, ssm_c_re=v_ssm_c_re,
             ssm_c_im=v_ssm_c_im, ssm_d=v_ssm_d, w_glu=v_w_glu, b_glu=v_b_glu, attn_sinks=v_attn_sinks,
             g_ssm_out=v_g_ssm_out, g_attn_out=v_g_attn_out, w_out=v_w_out, g_post_mix=v_g_post_mix,
             g_pre_ffn=v_g_pre_ffn, w_gate_up=v_w_gate_up, w_down=v_w_down, g_post_ffn=v_g_post_ffn)

    transposed = ("w_in", "w_glu", "w_gate_up")
    shard = {n: (w[n][0].T if n in transposed else w[n][0]).astype(_BF16) for n in _BIG}
    gathered = {}
    early, late = ("w_in", "w_glu", "w_out"), ("w_gate_up", "w_down")
    for names, lands in ((early, _sequencer_exchange([shard[n] for n in early], False, "gather_early", 1)),
                         (late, _sequencer_gather([shard[n] for n in late], "gather_ffn", 2))):
        gathered.update({n: a.reshape(-1, a.shape[2]) for n, a in zip(names, lands)})

    def fetch(names, after):
        del after
        return [gathered[n] for n in names]

    sent = []

    def publish(named):
        names = list(named)
        blocks = [named[n].reshape(N_DEV, -1, named[n].shape[1]) for n in names]
        sent.append((names, _sequencer_exchange(blocks, True, "grads_" + names[0], 4 + len(sent))))
        return []

    p = {n: w[n] for n in _SMALL}
    loss, grad_x, grads = _local_step(x[0], positions[0], loss_target[0], p, fetch, publish)

    result = {}
    for names, parts in sent:
        for name, part in zip(names, parts):
            if name in transposed:
                part = _sum_parts(part, "sum_" + name).T[None]
            result[name] = [a[None] for a in _adamw(part, w[name][0], m[name][0], v[name][0], "adamw_" + name)]

    small_grads = _pack([grads[n] for n in _SMALL] + [jnp.pad(loss.reshape(1), (0, _LANES - 1))], 0)
    small_parts, = _sequencer_exchange([small_grads], False, "gather_small", 7)
    packed = _adamw(small_parts, _pack([w[n] for n in _SMALL], 1), _pack([m[n] for n in _SMALL], 1),
                    _pack([v[n] for n in _SMALL], 1), "adamw_small")
    loss_row = None
    for name in _SMALL:
        result[name] = []
    for arr in packed:
        vals, loss_row = _unpack(arr, [w[n] for n in _SMALL])
        for name, val in zip(_SMALL, vals):
            result[name].append(val)
    total_loss = packed[0][loss_row, 0]

    out = [total_loss, grad_x[None]]
    for kind in range(4):
        out += [result[n][kind] for n in _WEIGHTS]
    return tuple(out)
```

```python
import functools
import math

import numpy as np
import jax
import jax.numpy as jnp
from jax import lax
from jax.experimental import pallas as pl
from jax.experimental.pallas import tpu as pltpu
from jax.experimental.pallas import tpu_sc as plsc

D_MODEL = 1024
SSM_WIDTH = 512
SSM_GROUP = 16
SSM_GROUPS = 32
SSM_STATE = 64
N_STATE = SSM_GROUPS * SSM_STATE
ATTN_WIDTH = 512
HEAD_DIM = 64
N_Q_HEADS = 8
N_KV_HEADS = 2
Q_PER_KV = 4
KV_WIDTH = 128
IN_WIDTH = 1280
BLOCK = 128
ROPE_DIM = 16
ROPE_THETA = 500000.0
D_FF = 2816
NORM_EPS = 1e-6
MASK_VALUE = -1e30
ADAM_LR = 0.001
ADAM_B1 = 0.9
ADAM_B2 = 0.999
ADAM_EPS = 1e-08
ADAM_WD = 0.01
ADAM_STEP = 10

N_DEV = 8
SCAN_CHUNKS = 8
SCAN_COLS = 512
TOKEN_TILE = 256
VMEM_LIMIT = 56 * 1024 * 1024

_F32 = jnp.float32
_BF16 = jnp.bfloat16
_MXU = jnp.bfloat16

_NN = ((1,), (0,))
_NT = ((1,), (1,))
_TN = ((0,), (0,))


def _dot(a, b, dims):
    return lax.dot_general(a.astype(_MXU), b.astype(_MXU), (dims, ((), ())),
                           preferred_element_type=_F32)


def _dot_exact(a, b, dims):
    return lax.dot_general(a.astype(_F32), b.astype(_F32), (dims, ((), ())),
                           precision=lax.Precision.HIGHEST, preferred_element_type=_F32)


def _iota(shape, dim):
    return lax.broadcasted_iota(jnp.int32, shape, dim)


def _rms_fwd(x, g):
    r = lax.rsqrt(jnp.mean(x * x, axis=-1, keepdims=True) + NORM_EPS)
    return x * r * g, r


def _rms_bwd(dy, x, g, r):
    a = dy * g
    xn = x * r
    dx = r * (a - xn * jnp.mean(a * xn, axis=-1, keepdims=True))
    dg = jnp.sum(dy * xn, axis=0, keepdims=True)
    return dx, dg


def _call(body, grid, in_specs, out_specs, out_shape, name, scratch=(), tokens=()):
    params = pltpu.CompilerParams(dimension_semantics=("arbitrary",) * len(grid),
                                  vmem_limit_bytes=VMEM_LIMIT)
    n_in, n_tok = len(in_specs), len(tokens)

    def run(*refs):
        return body(*refs[:n_in], *refs[n_in + n_tok:])

    call = pl.pallas_call(run, grid=grid,
                          in_specs=list(in_specs) + [pl.BlockSpec(memory_space=pl.ANY)] * n_tok,
                          out_specs=out_specs, out_shape=out_shape, scratch_shapes=list(scratch),
                          compiler_params=params, name=name)
    return lambda *args: call(*args, *tokens)


def _rows(tm, n):
    return pl.BlockSpec((tm, n), lambda i: (i, 0))


def _whole(shape):
    nd = len(shape)
    return pl.BlockSpec(shape, lambda i: (0,) * nd)


def _sds(shape, dtype):
    return jax.ShapeDtypeStruct(shape, dtype)


def _tile(L):
    return min(TOKEN_TILE, L)


def _accumulate(ref, val, first):
    @pl.when(first)
    def _():
        ref[...] = val

    @pl.when(jnp.logical_not(first))
    def _():
        ref[...] += val


def _rope_rows():
    half = ROPE_DIM // 2
    inv = (np.float32(ROPE_THETA) ** (-np.arange(half, dtype=np.float32) * np.float32(2.0) / np.float32(ROPE_DIM))).astype(np.float32)
    col = np.arange(KV_WIDTH) % HEAD_DIM
    freq = np.where(col < ROPE_DIM, inv[col % half], 0.0).astype(np.float32)
    sign = np.where(col < half, -1.0, np.where(col < ROPE_DIM, 1.0, 0.0)).astype(np.float32)
    return freq[None, :], sign[None, :]


def _rope_tables(pos_col):
    L = pos_col.shape[0]
    tm = _tile(L)
    freq, sign = _rope_rows()

    def body(pos_ref, freq_ref, sign_ref, cos_ref, sin_ref):
        ang = pos_ref[...].astype(_F32) * freq_ref[...]
        cos_ref[...] = jnp.cos(ang)
        sin_ref[...] = jnp.sin(ang) * sign_ref[...]

    return _call(body, (L // tm,),
                 [_rows(tm, 1), _whole((1, KV_WIDTH)), _whole((1, KV_WIDTH))],
                 [_rows(tm, KV_WIDTH), _rows(tm, KV_WIDTH)],
                 [_sds((L, KV_WIDTH), _F32)] * 2, "rope_tables")(pos_col, jnp.asarray(freq), jnp.asarray(sign))


def _widen(t, width):
    return t if width == KV_WIDTH else jnp.concatenate([t] * (width // KV_WIDTH), axis=1)


def _rope_partner(t):
    w = t.shape[1]
    in_head = _iota((1, w), 1) & (HEAD_DIM - 1)
    second = jnp.where(in_head < ROPE_DIM, pltpu.roll(t, ROPE_DIM // 2, 1), 0.0)
    return jnp.where(in_head < ROPE_DIM // 2, pltpu.roll(t, w - ROPE_DIM // 2, 1), second)


def _rope_apply(t, cos_t, sin_t):
    w = t.shape[1]
    return t * _widen(cos_t, w) + _rope_partner(t) * _widen(sin_t, w)


def _rope_transpose(dt, cos_t, sin_t):
    w = dt.shape[1]
    return dt * _widen(cos_t, w) + _rope_partner(dt * _widen(sin_t, w))


def _in_proj(x, g_pre_mix, w_in, cos_t, sin_t):
    L = x.shape[0]
    tm = _tile(L)

    def body(x_ref, g_ref, w_ref, cos_ref, sin_ref, hn_ref, u_ref, q_ref, k_ref, v_ref):
        hn, _ = _rms_fwd(x_ref[...], g_ref[...])
        hn = hn.astype(_BF16)
        hn_ref[...] = hn
        proj = _dot(hn, w_ref[...], _NT)
        u_ref[...] = proj[:, :SSM_WIDTH]
        q = proj[:, SSM_WIDTH:SSM_WIDTH + ATTN_WIDTH]
        k = proj[:, SSM_WIDTH + ATTN_WIDTH:SSM_WIDTH + ATTN_WIDTH + KV_WIDTH]
        cos_v, sin_v = cos_ref[...], sin_ref[...]
        q_ref[...] = _rope_apply(q, cos_v, sin_v).astype(_BF16)
        k_ref[...] = _rope_apply(k, cos_v, sin_v).astype(_BF16)
        v_ref[...] = proj[:, SSM_WIDTH + ATTN_WIDTH + KV_WIDTH:].astype(_BF16)

    return _call(body, (L // tm,),
                 [_rows(tm, D_MODEL), _whole((1, D_MODEL)), _whole((IN_WIDTH, D_MODEL)),
                  _rows(tm, KV_WIDTH), _rows(tm, KV_WIDTH)],
                 [_rows(tm, D_MODEL), _rows(tm, SSM_WIDTH), _rows(tm, ATTN_WIDTH),
                  _rows(tm, KV_WIDTH), _rows(tm, KV_WIDTH)],
                 [_sds((L, D_MODEL), _BF16), _sds((L, SSM_WIDTH), _F32), _sds((L, ATTN_WIDTH), _BF16),
                  _sds((L, KV_WIDTH), _BF16), _sds((L, KV_WIDTH), _BF16)],
                 "in_proj")(x, g_pre_mix, w_in, cos_t, sin_t)


def _s5_discretize(lam_re, lam_im, log_dt):
    lr = jnp.minimum(lam_re, -1e-4)
    li = lam_im
    dt = jnp.exp(log_dt)
    mag = jnp.exp(lr * dt)
    ar = mag * jnp.cos(li * dt)
    ai = mag * jnp.sin(li * dt)
    den = lr * lr + li * li
    fr = ((ar - 1.0) * lr + ai * li) / den
    fi = (ai * lr - (ar - 1.0) * li) / den
    return ar, ai, fr, fi


def _s5_bbar(lam_re, lam_im, log_dt, b_re, b_im):
    ar, ai, fr, fi = _s5_discretize(lam_re, lam_im, log_dt)
    return ar, ai, fr * b_re - fi * b_im, fr * b_im + fi * b_re


def _spread_masks():
    e16 = (_iota((SSM_GROUP, SSM_WIDTH), 1) & (SSM_GROUP - 1)) == _iota((SSM_GROUP, SSM_WIDTH), 0)
    e64 = (_iota((SSM_STATE, N_STATE), 1) & (SSM_STATE - 1)) == _iota((SSM_STATE, N_STATE), 0)
    mask_b = (_iota((N_STATE, SSM_WIDTH), 0) >> 6) == (_iota((N_STATE, SSM_WIDTH), 1) >> 4)
    mask_c = (_iota((SSM_WIDTH, N_STATE), 0) >> 4) == (_iota((SSM_WIDTH, N_STATE), 1) >> 6)
    return e16.astype(_F32), e64.astype(_F32), mask_b, mask_c


SUPER = 4
SB_STATE = N_STATE // SUPER
SB_WIDTH = SSM_WIDTH // SUPER


def _sb_state(k):
    return slice(SB_STATE * k, SB_STATE * (k + 1))


def _sb_width(k):
    return slice(SB_WIDTH * k, SB_WIDTH * (k + 1))


def _ssm_prep(lam_re_r, lam_im_r, ldt_r, lam_re_c, lam_im_c, ldt_c, b_re2, b_im2, c_re2, c_im2):
    def body(lrr, lir, ldr, lrc, lic, ldc, bre, bim, cre, cim, ar_ref, ai_ref, btr, bti, ctr, cti):
        ar, ai, _, _ = _s5_discretize(lrr[...], lir[...], ldr[...])
        ar_ref[...] = ar
        ai_ref[...] = ai
        _, _, bbr, bbi = _s5_bbar(lrc[...], lic[...], ldc[...], bre[...], bim[...])
        e16, e64, mask_b, mask_c = _spread_masks()

        def fold_b(bb):
            full = jnp.where(mask_b, _dot(bb, e16, _NN), 0.0)
            return sum(full[:, _sb_width(k)] for k in range(SUPER)).astype(_BF16)

        def fold_c(cc):
            full = jnp.where(mask_c, _dot(cc, e64, _NN), 0.0)
            return sum(full[_sb_width(k), :] for k in range(SUPER)).astype(_BF16)

        btr[...] = fold_b(bbr)
        bti[...] = fold_b(bbi)
        ctr[...] = fold_c(cre[...])
        cti[...] = fold_c(cim[...])

    row = (1, N_STATE)
    ins = [lam_re_r, lam_im_r, ldt_r, lam_re_c, lam_im_c, ldt_c, b_re2, b_im2, c_re2, c_im2]
    return _call(body, (1,), [_whole(a.shape) for a in ins],
                 [_whole(row), _whole(row), _whole((N_STATE, SB_WIDTH)), _whole((N_STATE, SB_WIDTH)),
                  _whole((SB_WIDTH, N_STATE)), _whole((SB_WIDTH, N_STATE))],
                 [_sds(row, _F32), _sds(row, _F32), _sds((N_STATE, SB_WIDTH), _BF16),
                  _sds((N_STATE, SB_WIDTH), _BF16), _sds((SB_WIDTH, N_STATE), _BF16),
                  _sds((SB_WIDTH, N_STATE), _BF16)], "ssm_prep")(*ins)


def _ssm_bu(u, bt_re, bt_im):
    L = u.shape[0]
    tm = _tile(L)

    def body(u_ref, br_ref, bi_ref, or_ref, oi_ref):
        for k in range(SUPER):
            ub = u_ref[:, _sb_width(k)].astype(_BF16)
            or_ref[:, _sb_state(k)] = _dot(ub, br_ref[_sb_state(k), :], _NT)
            oi_ref[:, _sb_state(k)] = _dot(ub, bi_ref[_sb_state(k), :], _NT)

    return _call(body, (L // tm,),
                 [_rows(tm, SSM_WIDTH), _whole((N_STATE, SB_WIDTH)), _whole((N_STATE, SB_WIDTH))],
                 [_rows(tm, N_STATE), _rows(tm, N_STATE)],
                 [_sds((L, N_STATE), _F32)] * 2, "ssm_bu")(u, bt_re, bt_im)


def _complex_power(ar, ai, n):
    def step(_, c):
        pr, pi = c
        return pr * ar - pi * ai, pr * ai + pi * ar
    return lax.fori_loop(0, n, step, (jnp.ones_like(ar), jnp.zeros_like(ai)))


def _chunk_carries(er, ei, pr, pi, reverse):
    rows = _iota(er.shape, 0)
    sr = jnp.zeros_like(pr)
    si = jnp.zeros_like(pi)
    out_r = jnp.zeros_like(er)
    out_i = jnp.zeros_like(ei)
    order = range(SCAN_CHUNKS - 1, 0, -1) if reverse else range(SCAN_CHUNKS - 1)
    for c in order:
        e_r = er[c:c + 1, :]
        e_i = ei[c:c + 1, :]
        sr, si = pr * sr - pi * si + e_r, pr * si + pi * sr + e_i
        nxt = c - 1 if reverse else c + 1
        out_r = jnp.where(rows == nxt, sr, out_r)
        out_i = jnp.where(rows == nxt, si, out_i)
    return out_r, out_i


def _scan_fwd(b_re, b_im, a_re, a_im):
    T = b_re.shape[0]
    W = SCAN_COLS
    blk = pl.BlockSpec((T, SCAN_CHUNKS, W), lambda j: (0, 0, j))
    vec = pl.BlockSpec((1, W), lambda j: (0, j))

    def body(br_ref, bi_ref, ar_ref, ai_ref, xr_ref, xi_ref):
        ar, ai = ar_ref[...], ai_ref[...]
        ar8 = jnp.broadcast_to(ar, (SCAN_CHUNKS, W))
        ai8 = jnp.broadcast_to(ai, (SCAN_CHUNKS, W))

        def local(t, c):
            cr, ci = c
            return ar8 * cr - ai8 * ci + br_ref[t], ar8 * ci + ai8 * cr + bi_ref[t]

        zero = jnp.zeros((SCAN_CHUNKS, W), _F32)
        er, ei = lax.fori_loop(0, T, local, (zero, zero))
        pr, pi = _complex_power(ar, ai, T)
        sr, si = _chunk_carries(er, ei, pr, pi, reverse=False)

        def final(t, c):
            nr, ni = local(t, c)
            xr_ref[t] = nr
            xi_ref[t] = ni
            return nr, ni

        lax.fori_loop(0, T, final, (sr, si))

    shape = _sds(b_re.shape, _F32)
    return _call(body, (N_STATE // W,), [blk, blk, vec, vec], [blk, blk], [shape, shape],
                 "scan_fwd")(b_re, b_im, a_re, a_im)


def _scan_bwd(dx_re, dx_im, x_re, x_im, a_re, a_im, tokens=()):
    T = dx_re.shape[0]
    W = SCAN_COLS
    blk = pl.BlockSpec((T, SCAN_CHUNKS, W), lambda j: (0, 0, j))
    vec = pl.BlockSpec((1, W), lambda j: (0, j))

    def body(dr_ref, di_ref, xr_ref, xi_ref, ar_ref, ai_ref, lr_ref, li_ref, dar_ref, dai_ref):
        ar, ai = ar_ref[...], ai_ref[...]
        ar8 = jnp.broadcast_to(ar, (SCAN_CHUNKS, W))
        ai8 = jnp.broadcast_to(ai, (SCAN_CHUNKS, W))

        def local(t, c):
            cr, ci = c
            return ar8 * cr + ai8 * ci + dr_ref[t], ar8 * ci - ai8 * cr + di_ref[t]

        zero = jnp.zeros((SCAN_CHUNKS, W), _F32)
        er, ei = lax.fori_loop(0, T, lambda k, c: local(T - 1 - k, c), (zero, zero))
        pr, pi = _complex_power(ar, -ai, T)
        sr, si = _chunk_carries(er, ei, pr, pi, reverse=True)

        def grad_a(acc, nr, ni, xpr, xpi):
            return acc[0] + nr * xpr + ni * xpi, acc[1] + ni * xpr - nr * xpi

        def final(k, c):
            t = T - 1 - k
            nr, ni = local(t, c[:2])
            lr_ref[t] = nr
            li_ref[t] = ni
            gr, gi = grad_a(c[2:], nr, ni, xr_ref[t - 1], xi_ref[t - 1])
            return nr, ni, gr, gi

        cr, ci, gr, gi = lax.fori_loop(0, T - 1, final, (sr, si, zero, zero))
        nr, ni = local(0, (cr, ci))
        lr_ref[0] = nr
        li_ref[0] = ni
        first = _iota((SCAN_CHUNKS, W), 0) == 0
        xpr = jnp.where(first, 0.0, pltpu.roll(xr_ref[T - 1], 1, 0))
        xpi = jnp.where(first, 0.0, pltpu.roll(xi_ref[T - 1], 1, 0))
        gr, gi = grad_a((gr, gi), nr, ni, xpr, xpi)
        dar_ref[...] = jnp.sum(gr, axis=0, keepdims=True)
        dai_ref[...] = jnp.sum(gi, axis=0, keepdims=True)

    shape = _sds(dx_re.shape, _F32)
    row = _sds((1, N_STATE), _F32)
    return _call(body, (N_STATE // W,), [blk, blk, blk, blk, vec, vec], [blk, blk, vec, vec],
                 [shape, shape, row, row], "scan_bwd", tokens=tokens)(dx_re, dx_im, x_re, x_im, a_re, a_im)


_GELU_K = math.sqrt(2.0 / math.pi)
_GELU_C = 0.044715


def _gelu(y):
    return 0.5 * y * (1.0 + jnp.tanh(_GELU_K * (y + _GELU_C * y * y * y)))


def _gelu_grad(y):
    t = jnp.tanh(_GELU_K * (y + _GELU_C * y * y * y))
    return 0.5 * (1.0 + t) + 0.5 * y * (1.0 - t * t) * _GELU_K * (1.0 + 3.0 * _GELU_C * y * y)


def _ssm_out(x_re, x_im, u, ct_re, ct_im, d_row, w_glu, b_glu, g_ssm):
    L = u.shape[0]
    tm = _tile(L)

    def body(xr_ref, xi_ref, u_ref, cr_ref, ci_ref, d_ref, w_ref, b_ref, g_ref, y_ref, z_ref, n_ref):
        cx = [_dot(xr_ref[:, _sb_state(k)], cr_ref[:, _sb_state(k)], _NT)
              - _dot(xi_ref[:, _sb_state(k)], ci_ref[:, _sb_state(k)], _NT) for k in range(SUPER)]
        y = jnp.concatenate(cx, axis=1) + d_ref[...] * u_ref[...]
        y_ref[...] = y
        z = _dot(_gelu(y), w_ref[...], _NT) + b_ref[...]
        z_ref[...] = z
        out = z[:, :SSM_WIDTH] * jax.nn.sigmoid(z[:, SSM_WIDTH:])
        n, _ = _rms_fwd(out, g_ref[...])
        n_ref[...] = n.astype(_BF16)

    return _call(body, (L // tm,),
                 [_rows(tm, N_STATE), _rows(tm, N_STATE), _rows(tm, SSM_WIDTH),
                  _whole((SB_WIDTH, N_STATE)), _whole((SB_WIDTH, N_STATE)), _whole((1, SSM_WIDTH)),
                  _whole((2 * SSM_WIDTH, SSM_WIDTH)), _whole((1, 2 * SSM_WIDTH)), _whole((1, SSM_WIDTH))],
                 [_rows(tm, SSM_WIDTH), _rows(tm, 2 * SSM_WIDTH), _rows(tm, SSM_WIDTH)],
                 [_sds((L, SSM_WIDTH), _F32), _sds((L, 2 * SSM_WIDTH), _F32), _sds((L, SSM_WIDTH), _BF16)],
                 "ssm_out")(x_re, x_im, u, ct_re, ct_im, d_row, w_glu, b_glu, g_ssm)


def _ssm_out_bwd(dn, y, z, u, ct_re, ct_im, d_row, w_glu, g_ssm):
    L = u.shape[0]
    tm = _tile(L)

    def body(dn_ref, y_ref, z_ref, u_ref, cr_ref, ci_ref, d_ref, w_ref, g_ref,
             gy_ref, dz_ref, dy_ref, dud_ref, dxr_ref, dxi_ref, dg_ref, db_ref, dd_ref):
        first = pl.program_id(0) == 0
        z = z_ref[...]
        z1, z2 = z[:, :SSM_WIDTH], z[:, SSM_WIDTH:]
        sig = jax.nn.sigmoid(z2)
        out = z1 * sig
        g = g_ref[...]
        _, r = _rms_fwd(out, g)
        dout, dg = _rms_bwd(dn_ref[...], out, g, r)
        _accumulate(dg_ref, dg, first)
        dz = jnp.concatenate([dout * sig, dout * z1 * sig * (1.0 - sig)], axis=1)
        _accumulate(db_ref, jnp.sum(dz, axis=0, keepdims=True), first)
        dzb = dz.astype(_BF16)
        dz_ref[...] = dzb
        y = y_ref[...]
        gy_ref[...] = _gelu(y).astype(_BF16)
        dy = _dot(dzb, w_ref[...], _NN) * _gelu_grad(y)
        u = u_ref[...]
        _accumulate(dd_ref, jnp.sum(dy * u, axis=0, keepdims=True), first)
        dud_ref[...] = d_ref[...] * dy
        dyb = dy.astype(_BF16)
        dy_ref[...] = dyb
        for k in range(SUPER):
            dxr_ref[:, _sb_state(k)] = _dot(dyb[:, _sb_width(k)], cr_ref[:, _sb_state(k)], _NN)
            dxi_ref[:, _sb_state(k)] = -_dot(dyb[:, _sb_width(k)], ci_ref[:, _sb_state(k)], _NN)

    row = _whole((1, SSM_WIDTH))
    return _call(body, (L // tm,),
                 [_rows(tm, SSM_WIDTH), _rows(tm, SSM_WIDTH), _rows(tm, 2 * SSM_WIDTH), _rows(tm, SSM_WIDTH),
                  _whole((SB_WIDTH, N_STATE)), _whole((SB_WIDTH, N_STATE)), row,
                  _whole((2 * SSM_WIDTH, SSM_WIDTH)), row],
                 [_rows(tm, SSM_WIDTH), _rows(tm, 2 * SSM_WIDTH), _rows(tm, SSM_WIDTH), _rows(tm, SSM_WIDTH),
                  _rows(tm, N_STATE), _rows(tm, N_STATE), row, _whole((1, 2 * SSM_WIDTH)), row],
                 [_sds((L, SSM_WIDTH), _BF16), _sds((L, 2 * SSM_WIDTH), _BF16), _sds((L, SSM_WIDTH), _BF16),
                  _sds((L, SSM_WIDTH), _F32), _sds((L, N_STATE), _F32), _sds((L, N_STATE), _F32),
                  _sds((1, SSM_WIDTH), _F32), _sds((1, 2 * SSM_WIDTH), _F32), _sds((1, SSM_WIDTH), _F32)],
                 "ssm_out_bwd")(dn, y, z, u, ct_re, ct_im, d_row, w_glu, g_ssm)


def _ssm_du(lam_re, lam_im, bt_re, bt_im, dud):
    L = dud.shape[0]
    tm = _tile(L)

    def body(lr_ref, li_ref, br_ref, bi_ref, dud_ref, du_ref):
        for k in range(SUPER):
            du_ref[:, _sb_width(k)] = (_dot(lr_ref[:, _sb_state(k)], br_ref[_sb_state(k), :], _NN)
                                       + _dot(li_ref[:, _sb_state(k)], bi_ref[_sb_state(k), :], _NN)
                                       + dud_ref[:, _sb_width(k)])

    return _call(body, (L // tm,),
                 [_rows(tm, N_STATE), _rows(tm, N_STATE), _whole((N_STATE, SB_WIDTH)),
                  _whole((N_STATE, SB_WIDTH)), _rows(tm, SSM_WIDTH)],
                 _rows(tm, SSM_WIDTH), _sds((L, SSM_WIDTH), _F32), "ssm_du")(lam_re, lam_im, bt_re, bt_im, dud)


def _ssm_weight_grads(dy, x_re, x_im, lam_re, lam_im, u):
    L = u.shape[0]

    def body(dy_ref, xr_ref, xi_ref, lr_ref, li_ref, u_ref, dcr_ref, dci_ref, dbr_ref, dbi_ref):
        dyb = dy_ref[...]
        ub = u_ref[...].astype(_BF16)
        dcr_ref[...] = _dot(dyb, xr_ref[...], _TN)
        dci_ref[...] = _dot(dyb, xi_ref[...], _TN)
        dbr_ref[...] = _dot(lr_ref[...], ub, _TN)
        dbi_ref[...] = _dot(li_ref[...], ub, _TN)

    width = pl.BlockSpec((L, SB_WIDTH), lambda k: (0, k))
    state = pl.BlockSpec((L, SB_STATE), lambda k: (0, k))
    out_c = pl.BlockSpec((SB_WIDTH, SB_STATE), lambda k: (0, k))
    out_b = pl.BlockSpec((SB_STATE, SB_WIDTH), lambda k: (k, 0))
    return _call(body, (SUPER,), [width, state, state, state, state, width], [out_c, out_c, out_b, out_b],
                 [_sds((SB_WIDTH, N_STATE), _F32)] * 2 + [_sds((N_STATE, SB_WIDTH), _F32)] * 2,
                 "ssm_weight_grads")(dy, x_re, x_im, lam_re, lam_im, u)


def _ssm_param_bwd(da_re_c, da_im_c, dbt_re, dbt_im, dct_re, dct_im,
                   lam_re_c, lam_im_c, ldt_c, b_re2, b_im2):
    def body(dar, dai, dbr, dbi, dcr, dci, lrc, lic, ldc, bre, bim,
             glr, gli, gdt, gbr, gbi, gcr, gci):
        own_b = ((_iota((N_STATE, SB_WIDTH), 0) >> 6) & 7) == (_iota((N_STATE, SB_WIDTH), 1) >> 4)
        own_c = (_iota((SB_WIDTH, SB_STATE), 0) >> 4) == (_iota((SB_WIDTH, SB_STATE), 1) >> 6)

        def fold_b(ref):
            t = jnp.where(own_b, ref[...], 0.0)
            for shift in (64, 32, 16):
                t = t + pltpu.roll(t, shift, 1)
            return t[:, :SSM_GROUP]

        def fold_c(ref, k):
            t = jnp.where(own_c, ref[:, _sb_state(k)], 0.0)
            t = sum(t[:, 128 * i:128 * (i + 1)] for i in range(SB_STATE // 128))
            return (t + pltpu.roll(t, SSM_STATE, 1))[:, :SSM_STATE]

        dbbr = fold_b(dbr)
        dbbi = fold_b(dbi)
        for k in range(SUPER):
            gcr[_sb_width(k), :] = fold_c(dcr, k)
            gci[_sb_width(k), :] = -fold_c(dci, k)
        _, vjp = jax.vjp(_s5_bbar, lrc[...], lic[...], ldc[...], bre[...], bim[...])
        d_lr, d_li, d_dt, d_br, d_bi = vjp((dar[...], dai[...], dbbr, dbbi))
        gbr[...] = d_br
        gbi[...] = d_bi
        groups = ((_iota((SSM_GROUPS, N_STATE), 1) >> 6) == _iota((SSM_GROUPS, N_STATE), 0)).astype(_F32)
        in_group = ((_iota((N_STATE, SSM_STATE), 0) & (SSM_STATE - 1)) == _iota((N_STATE, SSM_STATE), 1)).astype(_F32)
        glr[...] = _dot_exact(groups, d_lr * in_group, _NN)
        gli[...] = _dot_exact(groups, d_li * in_group, _NN)
        groups_t = ((_iota((N_STATE, SSM_GROUPS), 0) >> 6) == _iota((N_STATE, SSM_GROUPS), 1)).astype(_F32)
        gdt[...] = _dot_exact(jnp.broadcast_to(d_dt, (N_STATE, 128)), groups_t, _TN)[0:1]

    ins = [da_re_c, da_im_c, dbt_re, dbt_im, dct_re, dct_im, lam_re_c, lam_im_c, ldt_c, b_re2, b_im2]
    outs = [(SSM_GROUPS, SSM_STATE), (SSM_GROUPS, SSM_STATE), (1, SSM_GROUPS), (N_STATE, SSM_GROUP),
            (N_STATE, SSM_GROUP), (SSM_WIDTH, SSM_STATE), (SSM_WIDTH, SSM_STATE)]
    return _call(body, (1,), [_whole(a.shape) for a in ins], [_whole(s) for s in outs],
                 [_sds(s, _F32) for s in outs], "ssm_param_bwd")(*ins)


def _head_spread(j):
    r = _iota((KV_WIDTH, 256), 0)
    c = _iota((KV_WIDTH, 256), 1)
    return (r == HEAD_DIM * j + (c & (HEAD_DIM - 1))).astype(_BF16)


STACK = Q_PER_KV * BLOCK


def _stack_heads(t):
    lane_head = _iota((1, 256), 1) >> 6
    return jnp.concatenate([jnp.where(lane_head == g, t, jnp.zeros_like(t)) for g in range(Q_PER_KV)], axis=0)


def _unstack_heads(t):
    lane_head = _iota((1, 256), 1) >> 6
    return sum(jnp.where(lane_head == g, t[BLOCK * g:BLOCK * (g + 1)], 0.0) for g in range(Q_PER_KV))


def _stacked_sinks(sink_ref, j):
    block = _iota((STACK, 1), 0) >> 7
    col = jnp.full((STACK, 1), sink_ref[Q_PER_KV * j], _F32)
    for g in range(1, Q_PER_KV):
        col = jnp.where(block == g, sink_ref[Q_PER_KV * j + g], col)
    return col


def _fold_heads(t, j):
    t = t[:, :KV_WIDTH] + t[:, KV_WIDTH:]
    t = t + pltpu.roll(t, HEAD_DIM, 1)
    return jnp.where((_iota((1, KV_WIDTH), 1) >> 6) == j, t, 0.0)


def _attn_scores(q_stacked, kt, blk, sink):
    s = _dot(q_stacked, kt, _NT) * (HEAD_DIM ** -0.5)
    qi = _iota((STACK, 2 * BLOCK), 0) & (BLOCK - 1)
    kj = _iota((STACK, 2 * BLOCK), 1)
    rel = qi + BLOCK - kj
    valid = (rel >= 0) & (rel < BLOCK) & (blk * BLOCK - BLOCK + kj >= 0)
    s = jnp.where(valid, s, MASK_VALUE)
    m = jnp.maximum(jnp.max(s, axis=-1, keepdims=True), sink)
    p = jnp.exp(s - m)
    e_sink = jnp.exp(sink - m)
    den = jnp.sum(p, axis=-1, keepdims=True) + e_sink
    return p / den, e_sink / den


def _attn_specs():
    prev = lambda i: (jnp.maximum(i - 1, 0), 0)
    cur = lambda i: (i, 0)
    kv = [pl.BlockSpec((BLOCK, KV_WIDTH), prev), pl.BlockSpec((BLOCK, KV_WIDTH), cur)]
    return [pl.BlockSpec((BLOCK, ATTN_WIDTH), cur)] + kv + kv


def _attn_fwd(q, k, v, sinks, g_attn):
    L = q.shape[0]

    def body(q_ref, kp_ref, kc_ref, vp_ref, vc_ref, sink_ref, g_ref, o_ref, n_ref):
        blk = pl.program_id(0)
        kwin = jnp.concatenate([kp_ref[...], kc_ref[...]], axis=0)
        vwin = jnp.concatenate([vp_ref[...], vc_ref[...]], axis=0)
        halves = []
        for j in range(N_KV_HEADS):
            spread = _head_spread(j)
            kt = _dot(kwin, spread, _NN).astype(_BF16)
            vt = _dot(vwin, spread, _NN).astype(_BF16)
            qs = _stack_heads(q_ref[:, 256 * j:256 * (j + 1)])
            p, _ = _attn_scores(qs, kt, blk, _stacked_sinks(sink_ref, j))
            halves.append(_unstack_heads(_dot(p, vt, _NN)))
        o = jnp.concatenate(halves, axis=1)
        o_ref[...] = o
        n, _ = _rms_fwd(o, g_ref[...])
        n_ref[...] = n.astype(_BF16)

    cur = lambda i: (i, 0)
    return _call(body, (L // BLOCK,),
                 _attn_specs() + [pl.BlockSpec(memory_space=pltpu.SMEM), _whole((1, ATTN_WIDTH))],
                 [pl.BlockSpec((BLOCK, ATTN_WIDTH), cur)] * 2,
                 [_sds((L, ATTN_WIDTH), _F32), _sds((L, ATTN_WIDTH), _BF16)],
                 "attn_fwd")(q, k, k, v, v, sinks, g_attn)


def _attn_bwd(q, k, v, o, dn, sinks, g_attn):
    L = q.shape[0]

    def body(q_ref, kp_ref, kc_ref, vp_ref, vc_ref, o_ref, dn_ref, sink_ref, g_ref,
             dq_ref, dk_ref, dv_ref, dsink_ref, dg_ref):
        blk = pl.program_id(0)
        first = blk == 0

        @pl.when(first)
        def _():
            dk_ref[...] = jnp.zeros_like(dk_ref)
            dv_ref[...] = jnp.zeros_like(dv_ref)
            dsink_ref[...] = jnp.zeros_like(dsink_ref)

        o = o_ref[...]
        g = g_ref[...]
        _, r = _rms_fwd(o, g)
        do, dg = _rms_bwd(dn_ref[...], o, g, r)
        _accumulate(dg_ref, dg, first)
        kwin = jnp.concatenate([kp_ref[...], kc_ref[...]], axis=0)
        vwin = jnp.concatenate([vp_ref[...], vc_ref[...]], axis=0)
        lane = _iota((1, 128), 1)
        dsink = jnp.zeros((1, 128), _F32)
        dkwin = jnp.zeros((2 * BLOCK, KV_WIDTH), _F32)
        dvwin = jnp.zeros((2 * BLOCK, KV_WIDTH), _F32)
        dq_halves = []
        for j in range(N_KV_HEADS):
            spread = _head_spread(j)
            kt = _dot(kwin, spread, _NN).astype(_BF16)
            vt = _dot(vwin, spread, _NN).astype(_BF16)
            qs = _stack_heads(q_ref[:, 256 * j:256 * (j + 1)])
            dos = _stack_heads(do[:, 256 * j:256 * (j + 1)]).astype(_BF16)
            p, p_sink = _attn_scores(qs, kt, blk, _stacked_sinks(sink_ref, j))
            dp = _dot(dos, vt, _NT)
            delta = jnp.sum(p * dp, axis=-1, keepdims=True)
            ds = (p * (dp - delta) * (HEAD_DIM ** -0.5)).astype(_BF16)
            sink_term = p_sink * delta
            for g in range(Q_PER_KV):
                head_sum = jnp.sum(sink_term[BLOCK * g:BLOCK * (g + 1)], axis=0, keepdims=True)
                dsink = dsink - jnp.where(lane == Q_PER_KV * j + g, head_sum, 0.0)
            dvwin = dvwin + _fold_heads(_dot(p, dos, _TN), j)
            dkwin = dkwin + _fold_heads(_dot(ds, qs, _TN), j)
            dq_halves.append(_unstack_heads(_dot(ds, kt, _NN)))
        dq_ref[...] = jnp.concatenate(dq_halves, axis=1)
        dsink_ref[...] += dsink
        prev = pl.ds(pl.multiple_of(jnp.maximum(blk - 1, 0) * BLOCK, BLOCK), BLOCK)
        cur = pl.ds(pl.multiple_of(blk * BLOCK, BLOCK), BLOCK)
        dk_ref[prev, :] += dkwin[:BLOCK]
        dk_ref[cur, :] += dkwin[BLOCK:]
        dv_ref[prev, :] += dvwin[:BLOCK]
        dv_ref[cur, :] += dvwin[BLOCK:]

    cur = lambda i: (i, 0)
    blk_q = pl.BlockSpec((BLOCK, ATTN_WIDTH), cur)
    return _call(body, (L // BLOCK,),
                 _attn_specs() + [blk_q, blk_q, pl.BlockSpec(memory_space=pltpu.SMEM), _whole((1, ATTN_WIDTH))],
                 [blk_q, _whole((L, KV_WIDTH)), _whole((L, KV_WIDTH)), _whole((1, 128)), _whole((1, ATTN_WIDTH))],
                 [_sds((L, ATTN_WIDTH), _F32), _sds((L, KV_WIDTH), _F32), _sds((L, KV_WIDTH), _F32),
                  _sds((1, 128), _F32), _sds((1, ATTN_WIDTH), _F32)],
                 "attn_bwd")(q, k, k, v, v, o, dn, sinks, g_attn)


def _out_proj(n_ssm, n_attn, x, w_out, g_post_mix, g_pre_ffn):
    L = x.shape[0]
    tm = _tile(L)

    def body(ns_ref, na_ref, x_ref, w_ref, g1_ref, g2_ref, merged_ref, mo_ref, h1_ref, hn2_ref):
        merged = jnp.concatenate([ns_ref[...], na_ref[...]], axis=1)
        merged_ref[...] = merged
        mo = _dot(merged, w_ref[...], _NN)
        mo_ref[...] = mo
        n, _ = _rms_fwd(mo, g1_ref[...])
        h1 = x_ref[...] + n
        h1_ref[...] = h1
        hn2, _ = _rms_fwd(h1, g2_ref[...])
        hn2_ref[...] = hn2.astype(_BF16)

    row = _whole((1, D_MODEL))
    return _call(body, (L // tm,),
                 [_rows(tm, SSM_WIDTH), _rows(tm, ATTN_WIDTH), _rows(tm, D_MODEL), _whole((D_MODEL, D_MODEL)), row, row],
                 [_rows(tm, D_MODEL)] * 4,
                 [_sds((L, D_MODEL), _BF16), _sds((L, D_MODEL), _F32), _sds((L, D_MODEL), _F32), _sds((L, D_MODEL), _BF16)],
                 "out_proj")(n_ssm, n_attn, x, w_out, g_post_mix, g_pre_ffn)


def _ffn(hn2, h1, target, w_gate_up, w_down, g_pre_ffn, g_post_ffn):
    L = h1.shape[0]
    tm = _tile(L)
    half = D_FF // 2

    def body(hn2_ref, h1_ref, tgt_ref, wgu_hbm, wd_hbm, g2_ref, g3_ref,
             act_ref, dgu_ref, dff_ref, dh1_ref, loss_ref, dg3_ref, dg2_ref,
             wgu, wd, gu, sem):
        first = pl.program_id(0) == 0

        @pl.when(first)
        def _():
            c1 = pltpu.make_async_copy(wgu_hbm, wgu, sem.at[0])
            c2 = pltpu.make_async_copy(wd_hbm, wd, sem.at[1])
            c1.start()
            c2.start()
            c1.wait()
            c2.wait()

        hn2 = hn2_ref[...]
        ff = jnp.zeros((tm, D_MODEL), _F32)
        for c in range(2):
            gate = _dot(hn2, wgu[half * c:half * (c + 1), :], _NT)
            up = _dot(hn2, wgu[D_FF + half * c:D_FF + half * (c + 1), :], _NT)
            gu[:, half * c:half * (c + 1)] = gate
            gu[:, D_FF + half * c:D_FF + half * (c + 1)] = up
            act = (gate * jax.nn.sigmoid(gate) * up).astype(_BF16)
            act_ref[:, half * c:half * (c + 1)] = act
            ff = ff + _dot(act, wd[half * c:half * (c + 1), :], _NN)
        g3 = g3_ref[...]
        n, r = _rms_fwd(ff, g3)
        h1 = h1_ref[...]
        err = h1 + n - tgt_ref[...]
        loss = 0.5 * jnp.sum(jnp.mean(err * err, axis=-1, keepdims=True), axis=0, keepdims=True)
        _accumulate(loss_ref, jnp.broadcast_to(loss, (1, 128)), first)
        dh2 = err * (1.0 / D_MODEL)
        dff, dg3 = _rms_bwd(dh2, ff, g3, r)
        _accumulate(dg3_ref, dg3, first)
        dffb = dff.astype(_BF16)
        dff_ref[...] = dffb
        dhn2 = jnp.zeros((tm, D_MODEL), _F32)
        for c in range(2):
            dact = _dot(dffb, wd[half * c:half * (c + 1), :], _NT)
            gate = gu[:, half * c:half * (c + 1)]
            up = gu[:, D_FF + half * c:D_FF + half * (c + 1)]
            sig = jax.nn.sigmoid(gate)
            silu = gate * sig
            dgate = (dact * up * (sig + silu * (1.0 - sig))).astype(_BF16)
            dup = (dact * silu).astype(_BF16)
            dgu_ref[:, half * c:half * (c + 1)] = dgate
            dgu_ref[:, D_FF + half * c:D_FF + half * (c + 1)] = dup
            dhn2 = dhn2 + _dot(dgate, wgu[half * c:half * (c + 1), :], _NN)
            dhn2 = dhn2 + _dot(dup, wgu[D_FF + half * c:D_FF + half * (c + 1), :], _NN)
        g2 = g2_ref[...]
        _, r2 = _rms_fwd(h1, g2)
        dh1, dg2 = _rms_bwd(dhn2, h1, g2, r2)
        _accumulate(dg2_ref, dg2, first)
        dh1_ref[...] = dh2 + dh1

    row = _whole((1, D_MODEL))
    anyspace = pl.BlockSpec(memory_space=pl.ANY)
    return _call(body, (L // tm,),
                 [_rows(tm, D_MODEL), _rows(tm, D_MODEL), _rows(tm, D_MODEL), anyspace, anyspace, row, row],
                 [_rows(tm, D_FF), _rows(tm, 2 * D_FF), _rows(tm, D_MODEL), _rows(tm, D_MODEL),
                  _whole((1, 128)), row, row],
                 [_sds((L, D_FF), _BF16), _sds((L, 2 * D_FF), _BF16), _sds((L, D_MODEL), _BF16),
                  _sds((L, D_MODEL), _F32), _sds((1, 128), _F32), _sds((1, D_MODEL), _F32), _sds((1, D_MODEL), _F32)],
                 "ffn",
                 scratch=[pltpu.VMEM((2 * D_FF, D_MODEL), _BF16), pltpu.VMEM((D_FF, D_MODEL), _BF16),
                          pltpu.VMEM((tm, 2 * D_FF), _F32), pltpu.SemaphoreType.DMA((2,))],
                 )(hn2, h1, target, w_gate_up, w_down, g_pre_ffn, g_post_ffn)


def _out_proj_bwd(dh1, mo, w_out, g_post_mix, tokens=()):
    L = dh1.shape[0]
    tm = _tile(L)

    def body(dh1_ref, mo_ref, w_ref, g_ref, dmo_ref, dns_ref, dna_ref, dg_ref):
        first = pl.program_id(0) == 0
        mo = mo_ref[...]
        g = g_ref[...]
        _, r = _rms_fwd(mo, g)
        dmo, dg = _rms_bwd(dh1_ref[...], mo, g, r)
        _accumulate(dg_ref, dg, first)
        dmob = dmo.astype(_BF16)
        dmo_ref[...] = dmob
        dmerged = _dot(dmob, w_ref[...], _NT)
        dns_ref[...] = dmerged[:, :SSM_WIDTH]
        dna_ref[...] = dmerged[:, SSM_WIDTH:]

    row = _whole((1, D_MODEL))
    return _call(body, (L // tm,),
                 [_rows(tm, D_MODEL), _rows(tm, D_MODEL), _whole((D_MODEL, D_MODEL)), row],
                 [_rows(tm, D_MODEL), _rows(tm, SSM_WIDTH), _rows(tm, ATTN_WIDTH), row],
                 [_sds((L, D_MODEL), _BF16), _sds((L, SSM_WIDTH), _F32), _sds((L, ATTN_WIDTH), _F32),
                  _sds((1, D_MODEL), _F32)],
                 "out_proj_bwd", tokens=tokens)(dh1, mo, w_out, g_post_mix)


def _in_proj_bwd(du, dq, dk, dv, cos_t, sin_t, x, dh1, g_pre_mix, w_in):
    L = x.shape[0]
    tm = _tile(L)

    def body(du_ref, dq_ref, dk_ref, dv_ref, cos_ref, sin_ref, x_ref, dh1_ref, g_ref, w_ref,
             dproj_ref, dx_ref, dg_ref):
        first = pl.program_id(0) == 0
        cos_v, sin_v = cos_ref[...], sin_ref[...]
        dproj = jnp.concatenate([du_ref[...], _rope_transpose(dq_ref[...], cos_v, sin_v),
                                 _rope_transpose(dk_ref[...], cos_v, sin_v), dv_ref[...]], axis=1).astype(_BF16)
        dproj_ref[...] = dproj
        dhn = _dot(dproj, w_ref[...], _NN)
        x = x_ref[...]
        g = g_ref[...]
        _, r = _rms_fwd(x, g)
        dx, dg = _rms_bwd(dhn, x, g, r)
        _accumulate(dg_ref, dg, first)
        dx_ref[...] = dh1_ref[...] + dx

    row = _whole((1, D_MODEL))
    return _call(body, (L // tm,),
                 [_rows(tm, SSM_WIDTH), _rows(tm, ATTN_WIDTH), _rows(tm, KV_WIDTH), _rows(tm, KV_WIDTH),
                  _rows(tm, KV_WIDTH), _rows(tm, KV_WIDTH), _rows(tm, D_MODEL), _rows(tm, D_MODEL), row,
                  _whole((IN_WIDTH, D_MODEL))],
                 [_rows(tm, IN_WIDTH), _rows(tm, D_MODEL), row],
                 [_sds((L, IN_WIDTH), _BF16), _sds((L, D_MODEL), _F32), _sds((1, D_MODEL), _F32)],
                 "in_proj_bwd")(du, dq, dk, dv, cos_t, sin_t, x, dh1, g_pre_mix, w_in)


def _matmul_tn(a, b, out_dtype, name, scale=1.0):
    K, M = a.shape
    N = b.shape[1]
    tm = next(t for t in (512, 256, 128) if M % t == 0)
    tn = next(t for t in (512, 256, 128) if N % t == 0)

    def body(a_ref, b_ref, o_ref):
        acc = _dot(a_ref[...], b_ref[...], _TN)
        o_ref[...] = (acc if scale == 1.0 else acc * scale).astype(out_dtype)

    params = pltpu.CompilerParams(dimension_semantics=("arbitrary", "arbitrary"), vmem_limit_bytes=VMEM_LIMIT)
    return pl.pallas_call(body, grid=(M // tm, N // tn),
                          in_specs=[pl.BlockSpec((K, tm), lambda i, j: (0, i)),
                                    pl.BlockSpec((K, tn), lambda i, j: (0, j))],
                          out_specs=pl.BlockSpec((tm, tn), lambda i, j: (i, j)),
                          out_shape=_sds((M, N), out_dtype), compiler_params=params, name=name)(a, b)


def _to_chunked(a):
    L, n = a.shape
    return a.reshape(SCAN_CHUNKS, L // SCAN_CHUNKS, n).transpose(1, 0, 2).reshape(L, n)


def _from_chunked(a):
    L, n = a.shape
    return a.reshape(L // SCAN_CHUNKS, SCAN_CHUNKS, n).transpose(1, 0, 2).reshape(L, n)


def _local_step(x, pos, target, p, fetch, publish):
    L = x.shape[0]
    T = L // SCAN_CHUNKS
    cos_t, sin_t = _rope_tables(pos.reshape(L, 1))
    w_in, = fetch(("w_in",), None)
    hn, u, q, k, v = _in_proj(x, p["g_pre_mix"], w_in, cos_t, sin_t)

    lam_re_r = p["ssm_lambda_re"].reshape(1, N_STATE)
    lam_im_r = p["ssm_lambda_im"].reshape(1, N_STATE)
    ldt_r = jnp.broadcast_to(p["ssm_log_dt"].reshape(SSM_GROUPS, 1), (SSM_GROUPS, SSM_STATE)).reshape(1, N_STATE)
    lam_re_c, lam_im_c, ldt_c = (a.reshape(N_STATE, 1) for a in (lam_re_r, lam_im_r, ldt_r))
    b_re2 = p["ssm_b_re"].reshape(N_STATE, SSM_GROUP)
    b_im2 = p["ssm_b_im"].reshape(N_STATE, SSM_GROUP)
    c_re2 = p["ssm_c_re"].reshape(SSM_WIDTH, SSM_STATE)
    c_im2 = p["ssm_c_im"].reshape(SSM_WIDTH, SSM_STATE)
    d_row = p["ssm_d"].reshape(1, SSM_WIDTH)
    a_re, a_im, bt_re, bt_im, ct_re, ct_im = _ssm_prep(
        lam_re_r, lam_im_r, ldt_r, lam_re_c, lam_im_c, ldt_c, b_re2, b_im2, c_re2, c_im2)

    u_c = _to_chunked(u)
    bu_re, bu_im = _ssm_bu(u_c, bt_re, bt_im)
    x_re, x_im = _scan_fwd(bu_re.reshape(T, SCAN_CHUNKS, N_STATE), bu_im.reshape(T, SCAN_CHUNKS, N_STATE), a_re, a_im)
    w_glu, = fetch(("w_glu",), x_re)
    y, z, n_ssm_c = _ssm_out(x_re.reshape(L, N_STATE), x_im.reshape(L, N_STATE), u_c, ct_re, ct_im, d_row,
                             w_glu, p["b_glu"], p["g_ssm_out"])
    n_ssm = _from_chunked(n_ssm_c)

    sinks = p["attn_sinks"].reshape(N_Q_HEADS)
    o, n_attn = _attn_fwd(q, k, v, sinks, p["g_attn_out"])
    w_out, = fetch(("w_out",), n_attn)
    merged, mo, h1, hn2 = _out_proj(n_ssm, n_attn, x, w_out, p["g_post_mix"], p["g_pre_ffn"])
    w_gate_up, w_down = fetch(("w_gate_up", "w_down"), hn2)
    act, dgu, dff, dh1, loss, dg_post_ffn, dg_pre_ffn = _ffn(
        hn2, h1, target, w_gate_up, w_down, p["g_pre_ffn"], p["g_post_ffn"])
    grads = {"g_post_ffn": dg_post_ffn, "g_pre_ffn": dg_pre_ffn}
    tokens = publish({"w_down": _matmul_tn(act, dff, _BF16, "grad_w_down"),
                      "w_gate_up": _matmul_tn(dgu, hn2, _BF16, "grad_w_gate_up")})

    dmo, dn_ssm, dn_attn, grads["g_post_mix"] = _out_proj_bwd(dh1, mo, w_out, p["g_post_mix"], tokens)
    grad_w_out = _matmul_tn(merged, dmo, _BF16, "grad_w_out")

    dq, dk, dv, dsink, grads["g_attn_out"] = _attn_bwd(q, k, v, o, dn_attn, sinks, p["g_attn_out"])
    grads["attn_sinks"] = dsink

    gy, dz, dy, dud, dx_re, dx_im, grads["g_ssm_out"], grads["b_glu"], dd = _ssm_out_bwd(
        _to_chunked(dn_ssm), y, z, u_c, ct_re, ct_im, d_row, w_glu, p["g_ssm_out"])
    grads["ssm_d"] = dd.reshape(SSM_GROUPS, SSM_GROUP)
    tokens = publish({"w_out": grad_w_out, "w_glu": _matmul_tn(dz, gy, _BF16, "grad_w_glu")})
    lam_re, lam_im, da_re, da_im = _scan_bwd(dx_re.reshape(T, SCAN_CHUNKS, N_STATE), dx_im.reshape(T, SCAN_CHUNKS, N_STATE),
                                             x_re, x_im, a_re, a_im, tokens)
    lam_re = lam_re.reshape(L, N_STATE)
    lam_im = lam_im.reshape(L, N_STATE)
    dct_re, dct_im, dbt_re, dbt_im = _ssm_weight_grads(
        dy, x_re.reshape(L, N_STATE), x_im.reshape(L, N_STATE), lam_re, lam_im, u_c)
    g_lr, g_li, g_dt, g_br, g_bi, g_cr, g_ci = _ssm_param_bwd(
        da_re.reshape(N_STATE, 1), da_im.reshape(N_STATE, 1), dbt_re, dbt_im, dct_re, dct_im,
        lam_re_c, lam_im_c, ldt_c, b_re2, b_im2)
    grads.update(ssm_lambda_re=g_lr, ssm_lambda_im=g_li, ssm_log_dt=g_dt, ssm_b_re=g_br, ssm_b_im=g_bi,
                 ssm_c_re=g_cr, ssm_c_im=g_ci, loss=loss)
    publish(grads)

    du = _from_chunked(_ssm_du(lam_re, lam_im, bt_re, bt_im, dud))
    dproj, grad_x, g_pre_mix = _in_proj_bwd(du, dq, dk, dv, cos_t, sin_t, x, dh1, p["g_pre_mix"], w_in)
    publish({"g_pre_mix": g_pre_mix, "w_in": _matmul_tn(dproj, hn, _BF16, "grad_w_in")})
    return grad_x


_MESH = pl.DeviceIdType.MESH
_PEERS = N_DEV - 1


def _mesh_pos():
    return lax.axis_index("x"), lax.axis_index("y"), lax.axis_index("c")


def _dev_index(px, py, pc):
    return 4 * px + 2 * py + pc


def _all_gather(shards, out_dtype, name):
    n = len(shards)

    def body(*refs):
        ins, outs, stages = refs[:n], refs[n:2 * n], refs[2 * n:3 * n]
        send_sems, recv_sems, local_sems = refs[3 * n:]
        x, y, c = _mesh_pos()
        me, sibling = (x, y, c), (x, y, 1 - c)
        chips = [(1 - x, y), (x, 1 - y), (1 - x, 1 - y)]

        def copy(w, k, block, to, src=None):
            slot = outs[w].at[_dev_index(*block)]
            return pltpu.make_async_remote_copy(
                src_ref=slot if src is None else src, dst_ref=slot,
                send_sem=send_sems.at[_PEERS * w + k], recv_sem=recv_sems.at[_PEERS * w + k],
                device_id=to, device_id_type=_MESH)

        for w in range(n):
            stages[w][...] = ins[w][...].astype(out_dtype)
        mine, first, passed = [], [], []
        for w in range(n):
            cp = pltpu.make_async_copy(stages[w], outs[w].at[_dev_index(*me)], local_sems.at[w])
            cp.start()
            mine.append(cp)
            sends = [copy(w, 0, me, sibling, src=stages[w])]
            sends += [copy(w, 1 + j, me, (*chip, c), src=stages[w]) for j, chip in enumerate(chips)]
            for cp in sends:
                cp.start()
            first += sends
        for w in range(n):
            for j, chip in enumerate(chips):
                copy(w, 1 + j, (*chip, c), me).wait_recv()
                cp = copy(w, 4 + j, (*chip, c), sibling)
                cp.start()
                passed.append(cp)
        for w in range(n):
            copy(w, 0, sibling, me).wait_recv()
            for j, chip in enumerate(chips):
                copy(w, 4 + j, (*chip, 1 - c), me).wait_recv()
        for cp in first + passed:
            cp.wait_send()
        for cp in mine:
            cp.wait()

    return pl.pallas_call(
        body, name=name,
        out_shape=[_sds((N_DEV,) + s.shape, out_dtype) for s in shards],
        in_specs=[pl.BlockSpec(memory_space=pltpu.VMEM)] * n,
        out_specs=[pl.BlockSpec(memory_space=pl.ANY)] * n,
        scratch_shapes=[pltpu.VMEM(s.shape, out_dtype) for s in shards]
        + [pltpu.SemaphoreType.DMA((_PEERS * n,)), pltpu.SemaphoreType.DMA((_PEERS * n,)),
           pltpu.SemaphoreType.DMA((n,))],
        compiler_params=pltpu.CompilerParams(vmem_limit_bytes=VMEM_LIMIT),
    )(*shards)


_HBM_SPEC = pl.BlockSpec(memory_space=pltpu.HBM)
_SEM_SPEC = pl.BlockSpec(memory_space=pltpu.SEMAPHORE)
_DATAFLOW = pltpu.SideEffectType.DATAFLOW_SIDE_EFFECTING


def _peer(x, y, c, r):
    return (x ^ ((r >> 2) & 1), y ^ ((r >> 1) & 1), c ^ (r & 1))


def _hbm(a):
    return pltpu.with_memory_space_constraint(a, pltpu.HBM)


def _send_start(sources, blocked, name):
    n = len(sources)
    lands = [lax.empty((N_DEV,) + (s.shape[1:] if blocked else s.shape), s.dtype) for s in sources]

    def body(*refs):
        srcs, zones = refs[:n], refs[n:2 * n]
        send_sems, recv_sems = refs[2 * n:3 * n], refs[3 * n:4 * n]
        token, local_sems = refs[6 * n], refs[6 * n + 1]
        x, y, c = _mesh_pos()
        me = _dev_index(x, y, c)
        local = []
        for w in range(n):
            cp = pltpu.make_async_copy(srcs[w].at[me] if blocked else srcs[w], zones[w].at[me], local_sems.at[w])
            cp.start()
            local.append(cp)
            for r in range(1, N_DEV):
                peer = _peer(x, y, c, r)
                pltpu.make_async_remote_copy(
                    src_ref=srcs[w].at[_dev_index(*peer)] if blocked else srcs[w], dst_ref=zones[w].at[me],
                    send_sem=send_sems[w].at[r - 1], recv_sem=recv_sems[w].at[r - 1],
                    device_id=peer, device_id_type=_MESH).start()
        for cp in local:
            cp.wait()
        token[...] = jnp.zeros_like(token)

    sems = [pltpu.SemaphoreType.DMA((_PEERS,))] * (2 * n)
    out = pl.pallas_call(
        body, name=name,
        out_shape=sems + [pltpu.HBM(a.shape, a.dtype) for a in list(sources) + lands] + [_sds((8, 128), _F32)],
        in_specs=[_HBM_SPEC] * (2 * n),
        out_specs=[_SEM_SPEC] * (2 * n) + [_HBM_SPEC] * (2 * n) + [pl.BlockSpec(memory_space=pltpu.VMEM)],
        input_output_aliases={i: 2 * n + i for i in range(2 * n)},
        scratch_shapes=[pltpu.SemaphoreType.DMA((n,))],
        compiler_params=pltpu.CompilerParams(has_side_effects=_DATAFLOW),
    )(*[_hbm(a) for a in sources], *[_hbm(a) for a in lands])
    return out[:n], out[n:2 * n], out[2 * n:3 * n], out[3 * n:4 * n], out[4 * n]


def _send_wait(send_sems, recv_sems, sources, lands, after, blocked, name):
    n = len(sources)

    def body(*refs):
        srcs, zones = refs[:n], refs[n:2 * n]
        sends, recvs = refs[2 * n:3 * n], refs[3 * n:4 * n]
        x, y, c = _mesh_pos()
        for w in range(n):
            for r in range(1, N_DEV):
                peer = _peer(x, y, c, r)
                idx = _dev_index(*peer)
                cp = pltpu.make_async_remote_copy(
                    src_ref=srcs[w].at[idx] if blocked else srcs[w], dst_ref=zones[w].at[idx],
                    send_sem=sends[w].at[r - 1], recv_sem=recvs[w].at[r - 1],
                    device_id=peer, device_id_type=_MESH)
                cp.wait_send()
                cp.wait_recv()

    out = pl.pallas_call(
        body, name=name,
        out_shape=[pltpu.HBM(a.shape, a.dtype) for a in list(sources) + list(lands)],
        in_specs=[_HBM_SPEC] * (2 * n) + [_SEM_SPEC] * (2 * n) + [pl.BlockSpec(memory_space=pl.ANY)],
        out_specs=[_HBM_SPEC] * (2 * n),
        input_output_aliases={i: i for i in range(2 * n)},
        compiler_params=pltpu.CompilerParams(has_side_effects=_DATAFLOW),
    )(*sources, *lands, *send_sems, *recv_sems, after)
    return out[n:]


def _sequencer_exchange(sources, blocked, name, collective_id):
    n = len(sources)
    flags = blocked

    def body(*refs):
        srcs, zones = refs[:n], refs[n:2 * n]
        send_sems, recv_sems, local_sems = refs[2 * n:]
        x, y, c = _mesh_pos()
        me = _dev_index(x, y, c)
        barrier = pltpu.get_barrier_semaphore()
        for r in range(1, N_DEV):
            pl.semaphore_signal(barrier, inc=1, device_id=_peer(x, y, c, r), device_id_type=_MESH)
        pl.semaphore_wait(barrier, _PEERS)
        local, sends, recvs = [], [], []
        for w in range(n):
            cp = pltpu.make_async_copy(srcs[w].at[me] if flags[w] else srcs[w], zones[w].at[me], local_sems.at[w])
            cp.start()
            local.append(cp)
            for r in range(1, N_DEV):
                peer = _peer(x, y, c, r)
                idx = _dev_index(*peer)
                k = _PEERS * w + r - 1
                src = srcs[w].at[idx] if flags[w] else srcs[w]
                send = pltpu.make_async_remote_copy(
                    src_ref=src, dst_ref=zones[w].at[me], send_sem=send_sems.at[k], recv_sem=recv_sems.at[k],
                    device_id=peer, device_id_type=_MESH)
                send.start()
                sends.append(send)
                recvs.append(pltpu.make_async_remote_copy(
                    src_ref=src, dst_ref=zones[w].at[idx], send_sem=send_sems.at[k], recv_sem=recv_sems.at[k],
                    device_id=peer, device_id_type=_MESH))
        for cp in recvs:
            cp.wait_recv()
        for cp in sends:
            cp.wait_send()
        for cp in local:
            cp.wait()

    return pl.kernel(
        body, name=name,
        out_type=[_sds((N_DEV,) + (s.shape[1:] if f else s.shape), s.dtype) for s, f in zip(sources, flags)],
        mesh=plsc.ScalarSubcoreMesh(axis_name="sequencer", num_cores=1),
        scratch_types=[pltpu.SemaphoreType.DMA((_PEERS * n,)), pltpu.SemaphoreType.DMA((_PEERS * n,)),
                       pltpu.SemaphoreType.DMA((n,))],
        compiler_params=pltpu.CompilerParams(collective_id=collective_id),
    )(*sources)


def _sequencer_gather(shards, name, collective_id):
    n = len(shards)
    fan = 4

    def body(*refs):
        srcs, zones = refs[:n], refs[n:2 * n]
        send_sems, recv_sems, local_sems = refs[2 * n:]
        x, y, c = _mesh_pos()
        me, sibling = (x, y, c), (x, y, 1 - c)
        chips = [(1 - x, y), (x, 1 - y), (1 - x, 1 - y)]
        barrier = pltpu.get_barrier_semaphore()
        for peer in [sibling] + [(*chip, c) for chip in chips]:
            pl.semaphore_signal(barrier, inc=1, device_id=peer, device_id_type=_MESH)
        pl.semaphore_wait(barrier, fan)

        def copy(w, k, block, to, src=None):
            slot = zones[w].at[_dev_index(*block)]
            return pltpu.make_async_remote_copy(
                src_ref=slot if src is None else src, dst_ref=slot,
                send_sem=send_sems.at[_PEERS * w + k], recv_sem=recv_sems.at[_PEERS * w + k],
                device_id=to, device_id_type=_MESH)

        mine, first, passed = [], [], []
        for w in range(n):
            cp = pltpu.make_async_copy(srcs[w], zones[w].at[_dev_index(*me)], local_sems.at[w])
            cp.start()
            mine.append(cp)
            sends = [copy(w, 0, me, sibling, src=srcs[w])]
            sends += [copy(w, 1 + j, me, (*chip, c), src=srcs[w]) for j, chip in enumerate(chips)]
            for cp in sends:
                cp.start()
            first += sends
        for w in range(n):
            for j, chip in enumerate(chips):
                copy(w, 1 + j, (*chip, c), me).wait_recv()
                cp = copy(w, fan + j, (*chip, c), sibling)
                cp.start()
                passed.append(cp)
        for w in range(n):
            copy(w, 0, sibling, me).wait_recv()
            for j, chip in enumerate(chips):
                copy(w, fan + j, (*chip, 1 - c), me).wait_recv()
        for cp in first + passed:
            cp.wait_send()
        for cp in mine:
            cp.wait()

    return pl.kernel(
        body, name=name, out_type=[_sds((N_DEV,) + s.shape, s.dtype) for s in shards],
        mesh=plsc.ScalarSubcoreMesh(axis_name="sequencer", num_cores=1),
        scratch_types=[pltpu.SemaphoreType.DMA((_PEERS * n,)), pltpu.SemaphoreType.DMA((_PEERS * n,)),
                       pltpu.SemaphoreType.DMA((n,))],
        compiler_params=pltpu.CompilerParams(collective_id=collective_id),
    )(*shards)


def _row_tile(rows):
    return next(t for t in range(min(rows, 256), 0, -16) if rows % t == 0)


def _sum_parts(parts, name):
    _, rows, cols = parts.shape
    tr = _row_tile(rows)

    def body(p_ref, g_ref):
        g = p_ref[0].astype(_F32)
        for s in range(1, N_DEV):
            g = g + p_ref[s].astype(_F32)
        g_ref[...] = g

    return _call(body, (rows // tr,), [pl.BlockSpec((N_DEV, tr, cols), lambda i: (0, i, 0))],
                 _rows(tr, cols), _sds((rows, cols), _F32), name)(parts)


def _adam_update(g, w, m, v):
    new_m = ADAM_B1 * m + (1.0 - ADAM_B1) * g
    new_v = ADAM_B2 * v + (1.0 - ADAM_B2) * (g * g)
    m_hat = new_m / (1.0 - ADAM_B1 ** ADAM_STEP)
    v_hat = new_v / (1.0 - ADAM_B2 ** ADAM_STEP)
    return -ADAM_LR * (m_hat / (jnp.sqrt(v_hat) + ADAM_EPS) + ADAM_WD * w), new_m, new_v


def _adamw_small(parts, items, sums, name):
    n_p, n_i = len(parts), len(items)

    def body(*refs):
        p_refs, state, outs = refs[:n_p], refs[n_p:n_p + 3 * n_i], refs[n_p + 3 * n_i:]

        def total(part, rows, cols):
            g = p_refs[part][0, rows, cols]
            for s in range(1, N_DEV):
                g = g + p_refs[part][s, rows, cols]
            return g

        for i, (part, rows, cols, _, _, _) in enumerate(items):
            g = total(part, rows, cols)
            w_ref, m_ref, v_ref = state[3 * i:3 * i + 3]
            delta, new_m, new_v = _adam_update(g, w_ref[...], m_ref[...], v_ref[...])
            outs[4 * i][...] = g
            outs[4 * i + 1][...] = delta
            outs[4 * i + 2][...] = new_m
            outs[4 * i + 3][...] = new_v
        for j, (part, rows, cols) in enumerate(sums):
            outs[4 * n_i + j][...] = total(part, rows, cols)

    ins = list(parts) + [a for item in items for a in item[3:]]
    out_shapes = [item[3].shape for item in items for _ in range(4)]
    out_shapes += [(rows.stop - rows.start, cols.stop - cols.start) for _, rows, cols in sums]
    out = _call(body, (1,), [_whole(a.shape) for a in ins], [_whole(s) for s in out_shapes],
                [_sds(s, _F32) for s in out_shapes], name)(*ins)
    return [out[4 * i:4 * i + 4] for i in range(n_i)], out[4 * n_i:]


def _adamw(parts, w, m, v, name):
    rows, cols = w.shape
    tr = _row_tile(rows)
    n_parts = parts.shape[0]

    def body(p_ref, w_ref, m_ref, v_ref, g_ref, d_ref, nm_ref, nv_ref):
        g = p_ref[0].astype(_F32)
        for s in range(1, n_parts):
            g = g + p_ref[s].astype(_F32)
        new_m = ADAM_B1 * m_ref[...] + (1.0 - ADAM_B1) * g
        new_v = ADAM_B2 * v_ref[...] + (1.0 - ADAM_B2) * (g * g)
        m_hat = new_m / (1.0 - ADAM_B1 ** ADAM_STEP)
        v_hat = new_v / (1.0 - ADAM_B2 ** ADAM_STEP)
        g_ref[...] = g
        d_ref[...] = -ADAM_LR * (m_hat / (jnp.sqrt(v_hat) + ADAM_EPS) + ADAM_WD * w_ref[...])
        nm_ref[...] = new_m
        nv_ref[...] = new_v

    blk = _rows(tr, cols)
    return _call(body, (rows // tr,),
                 [pl.BlockSpec((n_parts, tr, cols), lambda i: (0, i, 0)), blk, blk, blk],
                 [blk] * 4, [_sds((rows, cols), _F32)] * 4, name)(parts, w, m, v)


_SMALL = ("g_pre_mix", "ssm_lambda_re", "ssm_lambda_im", "ssm_log_dt", "ssm_b_re", "ssm_b_im",
          "ssm_c_re", "ssm_c_im", "ssm_d", "b_glu", "attn_sinks", "g_ssm_out", "g_attn_out",
          "g_post_mix", "g_pre_ffn", "g_post_ffn")
_BIG = ("w_in", "w_glu", "w_out", "w_gate_up", "w_down")
_WEIGHTS = ("g_pre_mix", "w_in", "ssm_lambda_re", "ssm_lambda_im", "ssm_log_dt", "ssm_b_re", "ssm_b_im",
            "ssm_c_re", "ssm_c_im", "ssm_d", "w_glu", "b_glu", "attn_sinks", "g_ssm_out", "g_attn_out",
            "w_out", "g_post_mix", "g_pre_ffn", "w_gate_up", "w_down", "g_post_ffn")
_LANES = 128


_SHAPE_2D = {
    "g_pre_mix": (1, D_MODEL), "ssm_lambda_re": (SSM_GROUPS, SSM_STATE), "ssm_lambda_im": (SSM_GROUPS, SSM_STATE),
    "ssm_log_dt": (1, SSM_GROUPS), "ssm_b_re": (N_STATE, SSM_GROUP), "ssm_b_im": (N_STATE, SSM_GROUP),
    "ssm_c_re": (SSM_WIDTH, SSM_STATE), "ssm_c_im": (SSM_WIDTH, SSM_STATE), "ssm_d": (SSM_GROUPS, SSM_GROUP),
    "b_glu": (1, 2 * SSM_WIDTH), "attn_sinks": (1, N_Q_HEADS), "g_ssm_out": (1, SSM_WIDTH),
    "g_attn_out": (1, ATTN_WIDTH), "g_post_mix": (1, D_MODEL), "g_pre_ffn": (1, D_MODEL), "g_post_ffn": (1, D_MODEL)}
_ROW_WIDTH = {"g_pre_mix": D_MODEL, "b_glu": 2 * SSM_WIDTH, "attn_sinks": _LANES, "g_ssm_out": SSM_WIDTH,
              "g_attn_out": ATTN_WIDTH, "g_post_mix": D_MODEL, "g_pre_ffn": D_MODEL, "g_post_ffn": D_MODEL,
              "loss": _LANES}
_DENSE = ("ssm_b_re", "ssm_b_im", "ssm_c_re", "ssm_c_im")


def _row_slots(names):
    slots, row, col = {}, 0, 0
    for n in names:
        width = _ROW_WIDTH[n]
        if col + width > D_MODEL:
            row, col = row + 1, 0
        slots[n] = (row, col, width)
        col += width
    return slots


def _stack_rows(named, slots):
    n_rows = -(-(max(r for r, _, _ in slots.values()) + 1) // 8) * 8
    lines = []
    for r in range(n_rows):
        pieces = [named[n] for n, (row, _, _) in slots.items() if row == r]
        used = sum(p.shape[1] for p in pieces)
        if used < D_MODEL:
            pieces.append(jnp.zeros((1, D_MODEL - used), _F32))
        lines.append(jnp.concatenate(pieces, axis=1) if len(pieces) > 1 else pieces[0])
    return jnp.concatenate(lines, axis=0)


def kernel(x, positions, g_pre_mix, w_in, ssm_lambda_re, ssm_lambda_im, ssm_log_dt, ssm_b_re, ssm_b_im, ssm_c_re, ssm_c_im, ssm_d, w_glu, b_glu, attn_sinks, g_ssm_out, g_attn_out, w_out, g_post_mix, g_pre_ffn, w_gate_up, w_down, g_post_ffn, loss_target, m_g_pre_mix, m_w_in, m_ssm_lambda_re, m_ssm_lambda_im, m_ssm_log_dt, m_ssm_b_re, m_ssm_b_im, m_ssm_c_re, m_ssm_c_im, m_ssm_d, m_w_glu, m_b_glu, m_attn_sinks, m_g_ssm_out, m_g_attn_out, m_w_out, m_g_post_mix, m_g_pre_ffn, m_w_gate_up, m_w_down, m_g_post_ffn, v_g_pre_mix, v_w_in, v_ssm_lambda_re, v_ssm_lambda_im, v_ssm_log_dt, v_ssm_b_re, v_ssm_b_im, v_ssm_c_re, v_ssm_c_im, v_ssm_d, v_w_glu, v_b_glu, v_attn_sinks, v_g_ssm_out, v_g_attn_out, v_w_out, v_g_post_mix, v_g_pre_ffn, v_w_gate_up, v_w_down, v_g_post_ffn):
    w = dict(g_pre_mix=g_pre_mix, w_in=w_in, ssm_lambda_re=ssm_lambda_re, ssm_lambda_im=ssm_lambda_im,
             ssm_log_dt=ssm_log_dt, ssm_b_re=ssm_b_re, ssm_b_im=ssm_b_im, ssm_c_re=ssm_c_re, ssm_c_im=ssm_c_im,
             ssm_d=ssm_d, w_glu=w_glu, b_glu=b_glu, attn_sinks=attn_sinks, g_ssm_out=g_ssm_out,
             g_attn_out=g_attn_out, w_out=w_out, g_post_mix=g_post_mix, g_pre_ffn=g_pre_ffn,
             w_gate_up=w_gate_up, w_down=w_down, g_post_ffn=g_post_ffn)
    m = dict(g_pre_mix=m_g_pre_mix, w_in=m_w_in, ssm_lambda_re=m_ssm_lambda_re, ssm_lambda_im=m_ssm_lambda_im,
             ssm_log_dt=m_ssm_log_dt, ssm_b_re=m_ssm_b_re, ssm_b_im=m_ssm_b_im, ssm_c_re=m_ssm_c_re,
             ssm_c_im=m_ssm_c_im, ssm_d=m_ssm_d, w_glu=m_w_glu, b_glu=m_b_glu, attn_sinks=m_attn_sinks,
             g_ssm_out=m_g_ssm_out, g_attn_out=m_g_attn_out, w_out=m_w_out, g_post_mix=m_g_post_mix,
             g_pre_ffn=m_g_pre_ffn, w_gate_up=m_w_gate_up, w_down=m_w_down, g_post_ffn=m_g_post_ffn)
    v = dict(g_pre_mix=v_g_pre_mix, w_in=v_w_in, ssm_lambda_re=v_ssm_lambda_re, ssm_lambda_im=v_ssm_lambda_im,
             ssm_log_dt=v_ssm_log_dt, ssm_b_re=v_ssm_b_re, ssm_b_im=v_ssm_b_im, ssm_c_re=v_ssm_c_re,
             ssm_c_im=v_ssm_c_im, ssm_d=v_ssm_d, w_glu=v_w_glu, b_glu=v_b_glu, attn_sinks=v_attn_sinks,
             g_ssm_out=v_g_ssm_out, g_attn_out=v_g_attn_out, w_out=v_w_out, g_post_mix=v_g_post_mix,
             g_pre_ffn=v_g_pre_ffn, w_gate_up=v_w_gate_up, w_down=v_w_down, g_post_ffn=v_g_post_ffn)

    transposed = ("w_in", "w_glu", "w_gate_up")
    shard = {n: (w[n][0].T if n in transposed else w[n][0]).astype(_BF16) for n in _BIG}
    gathered = {}
    for names, lands in (
            (("w_in",), _sequencer_exchange([shard["w_in"]], [False], "gather_w_in", 1)),
            (("w_glu", "w_out"), _sequencer_exchange([shard["w_glu"], shard["w_out"]], [False] * 2, "gather_mix", 2)),
            (("w_gate_up", "w_down"), _sequencer_gather([shard["w_gate_up"], shard["w_down"]], "gather_ffn", 3))):
        gathered.update({n: a.reshape(-1, a.shape[2]) for n, a in zip(names, lands)})

    def fetch(names, after):
        del after
        return [gathered[n] for n in names]

    sent = []

    def publish(named):
        big = [n for n in named if n in _BIG]
        rows = [n for n in named if n in _ROW_WIDTH]
        dense = [n for n in named if n in _DENSE]
        plain = [n for n in named if n not in big + rows + dense]
        sources = [named[n].reshape(N_DEV, -1, named[n].shape[1]) for n in big]
        slots = _row_slots(rows)
        if rows:
            sources.append(_stack_rows(named, slots))
        sources += [named[n].reshape(-1, _LANES) for n in dense] + [named[n] for n in plain]
        flags = [True] * len(big) + [False] * (len(sources) - len(big))
        cid = 4 + len(sent)
        sent.append((big, slots, dense, plain, _sequencer_exchange(sources, flags, "grads_%d" % cid, cid)))
        return []

    p = {n: w[n] for n in _SMALL}
    grad_x = _local_step(x[0], positions[0], loss_target[0], p, fetch, publish)

    state = {n: [a.reshape(_SHAPE_2D[n]) for a in (w[n], m[n], v[n])] for n in _SMALL}
    result = {}
    total_loss = None
    for big, slots, dense, plain, lands in sent:
        lands = list(lands)
        for name in big:
            part = lands.pop(0)
            if name in transposed:
                part = _sum_parts(part, "sum_" + name).T[None]
            result[name] = [a[None] for a in _adamw(part, w[name][0], m[name][0], v[name][0], "adamw_" + name)]
        parts, items, sums, names = [], [], [], []
        if slots:
            parts.append(lands.pop(0))
            for name, (row, col, _) in slots.items():
                if name == "loss":
                    sums.append((0, slice(row, row + 1), slice(col, col + _LANES)))
                else:
                    items.append((0, slice(row, row + 1), slice(col, col + _SHAPE_2D[name][1]), *state[name]))
                    names.append(name)
        for name in dense:
            part = lands.pop(0).reshape((N_DEV,) + _SHAPE_2D[name])
            result[name] = _adamw(part, *state[name], "adamw_" + name)
        for name in plain:
            rows_n, cols_n = _SHAPE_2D[name]
            items.append((len(parts), slice(0, rows_n), slice(0, cols_n), *state[name]))
            parts.append(lands.pop(0))
            names.append(name)
        if items:
            updated, summed = _adamw_small(parts, items, sums, "adamw_small_" + names[0])
            result.update(dict(zip(names, updated)))
            if summed:
                total_loss = summed[0][0, 0]

    out = [total_loss, grad_x[None]]
    for kind in range(4):
        out += [result[n][kind].reshape(w[n].shape) for n in _WEIGHTS]
    return tuple(out)
```

```python
import functools
import math

import numpy as np
import jax
import jax.numpy as jnp
from jax import lax
from jax.experimental import pallas as pl
from jax.experimental.pallas import tpu as pltpu
from jax.experimental.pallas import tpu_sc as plsc

D_MODEL = 1024
SSM_WIDTH = 512
SSM_GROUP = 16
SSM_GROUPS = 32
SSM_STATE = 64
N_STATE = SSM_GROUPS * SSM_STATE
ATTN_WIDTH = 512
HEAD_DIM = 64
N_Q_HEADS = 8
N_KV_HEADS = 2
Q_PER_KV = 4
KV_WIDTH = 128
IN_WIDTH = 1280
BLOCK = 128
ROPE_DIM = 16
ROPE_THETA = 500000.0
D_FF = 2816
NORM_EPS = 1e-6
MASK_VALUE = -1e30
ADAM_LR = 0.001
ADAM_B1 = 0.9
ADAM_B2 = 0.999
ADAM_EPS = 1e-08
ADAM_WD = 0.01
ADAM_STEP = 10

N_DEV = 8
SCAN_CHUNKS = 8
SCAN_COLS = 512
TOKEN_TILE = 256
VMEM_LIMIT = 56 * 1024 * 1024

_F32 = jnp.float32
_BF16 = jnp.bfloat16
_MXU = jnp.bfloat16

_NN = ((1,), (0,))
_NT = ((1,), (1,))
_TN = ((0,), (0,))


def _dot(a, b, dims):
    return lax.dot_general(a.astype(_MXU), b.astype(_MXU), (dims, ((), ())),
                           preferred_element_type=_F32)


def _dot_exact(a, b, dims):
    return lax.dot_general(a.astype(_F32), b.astype(_F32), (dims, ((), ())),
                           precision=lax.Precision.HIGHEST, preferred_element_type=_F32)


def _iota(shape, dim):
    return lax.broadcasted_iota(jnp.int32, shape, dim)


def _rms_fwd(x, g):
    r = lax.rsqrt(jnp.mean(x * x, axis=-1, keepdims=True) + NORM_EPS)
    return x * r * g, r


def _rms_bwd(dy, x, g, r):
    a = dy * g
    xn = x * r
    dx = r * (a - xn * jnp.mean(a * xn, axis=-1, keepdims=True))
    dg = jnp.sum(dy * xn, axis=0, keepdims=True)
    return dx, dg


def _call(body, grid, in_specs, out_specs, out_shape, name, scratch=(), tokens=()):
    params = pltpu.CompilerParams(dimension_semantics=("arbitrary",) * len(grid),
                                  vmem_limit_bytes=VMEM_LIMIT)
    n_in, n_tok = len(in_specs), len(tokens)

    def run(*refs):
        return body(*refs[:n_in], *refs[n_in + n_tok:])

    call = pl.pallas_call(run, grid=grid,
                          in_specs=list(in_specs) + [pl.BlockSpec(memory_space=pl.ANY)] * n_tok,
                          out_specs=out_specs, out_shape=out_shape, scratch_shapes=list(scratch),
                          compiler_params=params, name=name)
    return lambda *args: call(*args, *tokens)


def _rows(tm, n):
    return pl.BlockSpec((tm, n), lambda i: (i, 0))


def _whole(shape):
    nd = len(shape)
    return pl.BlockSpec(shape, lambda i: (0,) * nd)


def _sds(shape, dtype):
    return jax.ShapeDtypeStruct(shape, dtype)


def _tile(L):
    return min(TOKEN_TILE, L)


def _accumulate(ref, val, first):
    @pl.when(first)
    def _():
        ref[...] = val

    @pl.when(jnp.logical_not(first))
    def _():
        ref[...] += val


def _rope_rows():
    half = ROPE_DIM // 2
    inv = (np.float32(ROPE_THETA) ** (-np.arange(half, dtype=np.float32) * np.float32(2.0) / np.float32(ROPE_DIM))).astype(np.float32)
    col = np.arange(KV_WIDTH) % HEAD_DIM
    freq = np.where(col < ROPE_DIM, inv[col % half], 0.0).astype(np.float32)
    sign = np.where(col < half, -1.0, np.where(col < ROPE_DIM, 1.0, 0.0)).astype(np.float32)
    return freq[None, :], sign[None, :]


def _rope_tables(pos_col):
    L = pos_col.shape[0]
    tm = _tile(L)
    freq, sign = _rope_rows()

    def body(pos_ref, freq_ref, sign_ref, cos_ref, sin_ref):
        ang = pos_ref[...].astype(_F32) * freq_ref[...]
        cos_ref[...] = jnp.cos(ang)
        sin_ref[...] = jnp.sin(ang) * sign_ref[...]

    return _call(body, (L // tm,),
                 [_rows(tm, 1), _whole((1, KV_WIDTH)), _whole((1, KV_WIDTH))],
                 [_rows(tm, KV_WIDTH), _rows(tm, KV_WIDTH)],
                 [_sds((L, KV_WIDTH), _F32)] * 2, "rope_tables")(pos_col, jnp.asarray(freq), jnp.asarray(sign))


def _widen(t, width):
    return t if width == KV_WIDTH else jnp.concatenate([t] * (width // KV_WIDTH), axis=1)


def _rope_partner(t):
    w = t.shape[1]
    in_head = _iota((1, w), 1) & (HEAD_DIM - 1)
    second = jnp.where(in_head < ROPE_DIM, pltpu.roll(t, ROPE_DIM // 2, 1), 0.0)
    return jnp.where(in_head < ROPE_DIM // 2, pltpu.roll(t, w - ROPE_DIM // 2, 1), second)


def _rope_apply(t, cos_t, sin_t):
    w = t.shape[1]
    return t * _widen(cos_t, w) + _rope_partner(t) * _widen(sin_t, w)


def _rope_transpose(dt, cos_t, sin_t):
    w = dt.shape[1]
    return dt * _widen(cos_t, w) + _rope_partner(dt * _widen(sin_t, w))


def _in_proj(x, g_pre_mix, w_in, cos_t, sin_t):
    L = x.shape[0]
    tm = _tile(L)

    def body(x_ref, g_ref, w_ref, cos_ref, sin_ref, hn_ref, u_ref, q_ref, k_ref, v_ref):
        hn, _ = _rms_fwd(x_ref[...], g_ref[...])
        hn = hn.astype(_BF16)
        hn_ref[...] = hn
        proj = _dot(hn, w_ref[...], _NT)
        u_ref[...] = proj[:, :SSM_WIDTH]
        q = proj[:, SSM_WIDTH:SSM_WIDTH + ATTN_WIDTH]
        k = proj[:, SSM_WIDTH + ATTN_WIDTH:SSM_WIDTH + ATTN_WIDTH + KV_WIDTH]
        cos_v, sin_v = cos_ref[...], sin_ref[...]
        q_ref[...] = _rope_apply(q, cos_v, sin_v).astype(_BF16)
        k_ref[...] = _rope_apply(k, cos_v, sin_v).astype(_BF16)
        v_ref[...] = proj[:, SSM_WIDTH + ATTN_WIDTH + KV_WIDTH:].astype(_BF16)

    return _call(body, (L // tm,),
                 [_rows(tm, D_MODEL), _whole((1, D_MODEL)), _whole((IN_WIDTH, D_MODEL)),
                  _rows(tm, KV_WIDTH), _rows(tm, KV_WIDTH)],
                 [_rows(tm, D_MODEL), _rows(tm, SSM_WIDTH), _rows(tm, ATTN_WIDTH),
                  _rows(tm, KV_WIDTH), _rows(tm, KV_WIDTH)],
                 [_sds((L, D_MODEL), _BF16), _sds((L, SSM_WIDTH), _F32), _sds((L, ATTN_WIDTH), _BF16),
                  _sds((L, KV_WIDTH), _BF16), _sds((L, KV_WIDTH), _BF16)],
                 "in_proj")(x, g_pre_mix, w_in, cos_t, sin_t)


def _s5_discretize(lam_re, lam_im, log_dt):
    lr = jnp.minimum(lam_re, -1e-4)
    li = lam_im
    dt = jnp.exp(log_dt)
    mag = jnp.exp(lr * dt)
    ar = mag * jnp.cos(li * dt)
    ai = mag * jnp.sin(li * dt)
    den = lr * lr + li * li
    fr = ((ar - 1.0) * lr + ai * li) / den
    fi = (ai * lr - (ar - 1.0) * li) / den
    return ar, ai, fr, fi


def _s5_bbar(lam_re, lam_im, log_dt, b_re, b_im):
    ar, ai, fr, fi = _s5_discretize(lam_re, lam_im, log_dt)
    return ar, ai, fr * b_re - fi * b_im, fr * b_im + fi * b_re


def _spread_masks():
    e16 = (_iota((SSM_GROUP, SSM_WIDTH), 1) & (SSM_GROUP - 1)) == _iota((SSM_GROUP, SSM_WIDTH), 0)
    e64 = (_iota((SSM_STATE, N_STATE), 1) & (SSM_STATE - 1)) == _iota((SSM_STATE, N_STATE), 0)
    mask_b = (_iota((N_STATE, SSM_WIDTH), 0) >> 6) == (_iota((N_STATE, SSM_WIDTH), 1) >> 4)
    mask_c = (_iota((SSM_WIDTH, N_STATE), 0) >> 4) == (_iota((SSM_WIDTH, N_STATE), 1) >> 6)
    return e16.astype(_F32), e64.astype(_F32), mask_b, mask_c


SUPER = 4
SB_STATE = N_STATE // SUPER
SB_WIDTH = SSM_WIDTH // SUPER


def _sb_state(k):
    return slice(SB_STATE * k, SB_STATE * (k + 1))


def _sb_width(k):
    return slice(SB_WIDTH * k, SB_WIDTH * (k + 1))


def _ssm_prep(lam_re_r, lam_im_r, ldt_r, lam_re_c, lam_im_c, ldt_c, b_re2, b_im2, c_re2, c_im2):
    def body(lrr, lir, ldr, lrc, lic, ldc, bre, bim, cre, cim, ar_ref, ai_ref, btr, bti, ctr, cti):
        ar, ai, _, _ = _s5_discretize(lrr[...], lir[...], ldr[...])
        ar_ref[...] = ar
        ai_ref[...] = ai
        _, _, bbr, bbi = _s5_bbar(lrc[...], lic[...], ldc[...], bre[...], bim[...])
        e16, e64, mask_b, mask_c = _spread_masks()

        def fold_b(bb):
            full = jnp.where(mask_b, _dot(bb, e16, _NN), 0.0)
            return sum(full[:, _sb_width(k)] for k in range(SUPER)).astype(_BF16)

        def fold_c(cc):
            full = jnp.where(mask_c, _dot(cc, e64, _NN), 0.0)
            return sum(full[_sb_width(k), :] for k in range(SUPER)).astype(_BF16)

        btr[...] = fold_b(bbr)
        bti[...] = fold_b(bbi)
        ctr[...] = fold_c(cre[...])
        cti[...] = fold_c(cim[...])

    row = (1, N_STATE)
    ins = [lam_re_r, lam_im_r, ldt_r, lam_re_c, lam_im_c, ldt_c, b_re2, b_im2, c_re2, c_im2]
    return _call(body, (1,), [_whole(a.shape) for a in ins],
                 [_whole(row), _whole(row), _whole((N_STATE, SB_WIDTH)), _whole((N_STATE, SB_WIDTH)),
                  _whole((SB_WIDTH, N_STATE)), _whole((SB_WIDTH, N_STATE))],
                 [_sds(row, _F32), _sds(row, _F32), _sds((N_STATE, SB_WIDTH), _BF16),
                  _sds((N_STATE, SB_WIDTH), _BF16), _sds((SB_WIDTH, N_STATE), _BF16),
                  _sds((SB_WIDTH, N_STATE), _BF16)], "ssm_prep")(*ins)


def _ssm_bu(u, bt_re, bt_im):
    L = u.shape[0]
    tm = _tile(L)

    def body(u_ref, br_ref, bi_ref, or_ref, oi_ref):
        for k in range(SUPER):
            ub = u_ref[:, _sb_width(k)].astype(_BF16)
            or_ref[:, _sb_state(k)] = _dot(ub, br_ref[_sb_state(k), :], _NT)
            oi_ref[:, _sb_state(k)] = _dot(ub, bi_ref[_sb_state(k), :], _NT)

    return _call(body, (L // tm,),
                 [_rows(tm, SSM_WIDTH), _whole((N_STATE, SB_WIDTH)), _whole((N_STATE, SB_WIDTH))],
                 [_rows(tm, N_STATE), _rows(tm, N_STATE)],
                 [_sds((L, N_STATE), _F32)] * 2, "ssm_bu")(u, bt_re, bt_im)


def _complex_power(ar, ai, n):
    def step(_, c):
        pr, pi = c
        return pr * ar - pi * ai, pr * ai + pi * ar
    return lax.fori_loop(0, n, step, (jnp.ones_like(ar), jnp.zeros_like(ai)))


def _chunk_carries(er, ei, pr, pi, reverse):
    rows = _iota(er.shape, 0)
    sr = jnp.zeros_like(pr)
    si = jnp.zeros_like(pi)
    out_r = jnp.zeros_like(er)
    out_i = jnp.zeros_like(ei)
    order = range(SCAN_CHUNKS - 1, 0, -1) if reverse else range(SCAN_CHUNKS - 1)
    for c in order:
        e_r = er[c:c + 1, :]
        e_i = ei[c:c + 1, :]
        sr, si = pr * sr - pi * si + e_r, pr * si + pi * sr + e_i
        nxt = c - 1 if reverse else c + 1
        out_r = jnp.where(rows == nxt, sr, out_r)
        out_i = jnp.where(rows == nxt, si, out_i)
    return out_r, out_i


def _scan_fwd(b_re, b_im, a_re, a_im):
    T = b_re.shape[0]
    W = SCAN_COLS
    blk = pl.BlockSpec((T, SCAN_CHUNKS, W), lambda j: (0, 0, j))
    vec = pl.BlockSpec((1, W), lambda j: (0, j))

    def body(br_ref, bi_ref, ar_ref, ai_ref, xr_ref, xi_ref):
        ar, ai = ar_ref[...], ai_ref[...]
        ar8 = jnp.broadcast_to(ar, (SCAN_CHUNKS, W))
        ai8 = jnp.broadcast_to(ai, (SCAN_CHUNKS, W))

        def local(t, c):
            cr, ci = c
            return ar8 * cr - ai8 * ci + br_ref[t], ar8 * ci + ai8 * cr + bi_ref[t]

        zero = jnp.zeros((SCAN_CHUNKS, W), _F32)
        er, ei = lax.fori_loop(0, T, local, (zero, zero))
        pr, pi = _complex_power(ar, ai, T)
        sr, si = _chunk_carries(er, ei, pr, pi, reverse=False)

        def final(t, c):
            nr, ni = local(t, c)
            xr_ref[t] = nr
            xi_ref[t] = ni
            return nr, ni

        lax.fori_loop(0, T, final, (sr, si))

    shape = _sds(b_re.shape, _F32)
    return _call(body, (N_STATE // W,), [blk, blk, vec, vec], [blk, blk], [shape, shape],
                 "scan_fwd")(b_re, b_im, a_re, a_im)


def _scan_bwd(dx_re, dx_im, x_re, x_im, a_re, a_im, tokens=()):
    T = dx_re.shape[0]
    W = SCAN_COLS
    blk = pl.BlockSpec((T, SCAN_CHUNKS, W), lambda j: (0, 0, j))
    vec = pl.BlockSpec((1, W), lambda j: (0, j))

    def body(dr_ref, di_ref, xr_ref, xi_ref, ar_ref, ai_ref, lr_ref, li_ref, dar_ref, dai_ref):
        ar, ai = ar_ref[...], ai_ref[...]
        ar8 = jnp.broadcast_to(ar, (SCAN_CHUNKS, W))
        ai8 = jnp.broadcast_to(ai, (SCAN_CHUNKS, W))

        def local(t, c):
            cr, ci = c
            return ar8 * cr + ai8 * ci + dr_ref[t], ar8 * ci - ai8 * cr + di_ref[t]

        zero = jnp.zeros((SCAN_CHUNKS, W), _F32)
        er, ei = lax.fori_loop(0, T, lambda k, c: local(T - 1 - k, c), (zero, zero))
        pr, pi = _complex_power(ar, -ai, T)
        sr, si = _chunk_carries(er, ei, pr, pi, reverse=True)

        def grad_a(acc, nr, ni, xpr, xpi):
            return acc[0] + nr * xpr + ni * xpi, acc[1] + ni * xpr - nr * xpi

        def final(k, c):
            t = T - 1 - k
            nr, ni = local(t, c[:2])
            lr_ref[t] = nr
            li_ref[t] = ni
            gr, gi = grad_a(c[2:], nr, ni, xr_ref[t - 1], xi_ref[t - 1])
            return nr, ni, gr, gi

        cr, ci, gr, gi = lax.fori_loop(0, T - 1, final, (sr, si, zero, zero))
        nr, ni = local(0, (cr, ci))
        lr_ref[0] = nr
        li_ref[0] = ni
        first = _iota((SCAN_CHUNKS, W), 0) == 0
        xpr = jnp.where(first, 0.0, pltpu.roll(xr_ref[T - 1], 1, 0))
        xpi = jnp.where(first, 0.0, pltpu.roll(xi_ref[T - 1], 1, 0))
        gr, gi = grad_a((gr, gi), nr, ni, xpr, xpi)
        dar_ref[...] = jnp.sum(gr, axis=0, keepdims=True)
        dai_ref[...] = jnp.sum(gi, axis=0, keepdims=True)

    shape = _sds(dx_re.shape, _F32)
    row = _sds((1, N_STATE), _F32)
    return _call(body, (N_STATE // W,), [blk, blk, blk, blk, vec, vec], [blk, blk, vec, vec],
                 [shape, shape, row, row], "scan_bwd", tokens=tokens)(dx_re, dx_im, x_re, x_im, a_re, a_im)


_GELU_K = math.sqrt(2.0 / math.pi)
_GELU_C = 0.044715


def _gelu(y):
    return 0.5 * y * (1.0 + jnp.tanh(_GELU_K * (y + _GELU_C * y * y * y)))


def _gelu_grad(y):
    t = jnp.tanh(_GELU_K * (y + _GELU_C * y * y * y))
    return 0.5 * (1.0 + t) + 0.5 * y * (1.0 - t * t) * _GELU_K * (1.0 + 3.0 * _GELU_C * y * y)


def _ssm_out(x_re, x_im, u, ct_re, ct_im, d_row, w_glu, b_glu, g_ssm):
    L = u.shape[0]
    tm = _tile(L)

    def body(xr_ref, xi_ref, u_ref, cr_ref, ci_ref, d_ref, w_ref, b_ref, g_ref, y_ref, z_ref, n_ref):
        cx = [_dot(xr_ref[:, _sb_state(k)], cr_ref[:, _sb_state(k)], _NT)
              - _dot(xi_ref[:, _sb_state(k)], ci_ref[:, _sb_state(k)], _NT) for k in range(SUPER)]
        y = jnp.concatenate(cx, axis=1) + d_ref[...] * u_ref[...]
        y_ref[...] = y
        z = _dot(_gelu(y), w_ref[...], _NT) + b_ref[...]
        z_ref[...] = z
        out = z[:, :SSM_WIDTH] * jax.nn.sigmoid(z[:, SSM_WIDTH:])
        n, _ = _rms_fwd(out, g_ref[...])
        n_ref[...] = n.astype(_BF16)

    return _call(body, (L // tm,),
                 [_rows(tm, N_STATE), _rows(tm, N_STATE), _rows(tm, SSM_WIDTH),
                  _whole((SB_WIDTH, N_STATE)), _whole((SB_WIDTH, N_STATE)), _whole((1, SSM_WIDTH)),
                  _whole((2 * SSM_WIDTH, SSM_WIDTH)), _whole((1, 2 * SSM_WIDTH)), _whole((1, SSM_WIDTH))],
                 [_rows(tm, SSM_WIDTH), _rows(tm, 2 * SSM_WIDTH), _rows(tm, SSM_WIDTH)],
                 [_sds((L, SSM_WIDTH), _F32), _sds((L, 2 * SSM_WIDTH), _F32), _sds((L, SSM_WIDTH), _BF16)],
                 "ssm_out")(x_re, x_im, u, ct_re, ct_im, d_row, w_glu, b_glu, g_ssm)


def _ssm_out_bwd(dn, y, z, u, ct_re, ct_im, d_row, w_glu, g_ssm):
    L = u.shape[0]
    tm = _tile(L)

    def body(dn_ref, y_ref, z_ref, u_ref, cr_ref, ci_ref, d_ref, w_ref, g_ref,
             gy_ref, dz_ref, dy_ref, dud_ref, dxr_ref, dxi_ref, dg_ref, db_ref, dd_ref):
        first = pl.program_id(0) == 0
        z = z_ref[...]
        z1, z2 = z[:, :SSM_WIDTH], z[:, SSM_WIDTH:]
        sig = jax.nn.sigmoid(z2)
        out = z1 * sig
        g = g_ref[...]
        _, r = _rms_fwd(out, g)
        dout, dg = _rms_bwd(dn_ref[...], out, g, r)
        _accumulate(dg_ref, dg, first)
        dz = jnp.concatenate([dout * sig, dout * z1 * sig * (1.0 - sig)], axis=1)
        _accumulate(db_ref, jnp.sum(dz, axis=0, keepdims=True), first)
        dzb = dz.astype(_BF16)
        dz_ref[...] = dzb
        y = y_ref[...]
        gy_ref[...] = _gelu(y).astype(_BF16)
        dy = _dot(dzb, w_ref[...], _NN) * _gelu_grad(y)
        u = u_ref[...]
        _accumulate(dd_ref, jnp.sum(dy * u, axis=0, keepdims=True), first)
        dud_ref[...] = d_ref[...] * dy
        dyb = dy.astype(_BF16)
        dy_ref[...] = dyb
        for k in range(SUPER):
            dxr_ref[:, _sb_state(k)] = _dot(dyb[:, _sb_width(k)], cr_ref[:, _sb_state(k)], _NN)
            dxi_ref[:, _sb_state(k)] = -_dot(dyb[:, _sb_width(k)], ci_ref[:, _sb_state(k)], _NN)

    row = _whole((1, SSM_WIDTH))
    return _call(body, (L // tm,),
                 [_rows(tm, SSM_WIDTH), _rows(tm, SSM_WIDTH), _rows(tm, 2 * SSM_WIDTH), _rows(tm, SSM_WIDTH),
                  _whole((SB_WIDTH, N_STATE)), _whole((SB_WIDTH, N_STATE)), row,
                  _whole((2 * SSM_WIDTH, SSM_WIDTH)), row],
                 [_rows(tm, SSM_WIDTH), _rows(tm, 2 * SSM_WIDTH), _rows(tm, SSM_WIDTH), _rows(tm, SSM_WIDTH),
                  _rows(tm, N_STATE), _rows(tm, N_STATE), row, _whole((1, 2 * SSM_WIDTH)), row],
                 [_sds((L, SSM_WIDTH), _BF16), _sds((L, 2 * SSM_WIDTH), _BF16), _sds((L, SSM_WIDTH), _BF16),
                  _sds((L, SSM_WIDTH), _F32), _sds((L, N_STATE), _F32), _sds((L, N_STATE), _F32),
                  _sds((1, SSM_WIDTH), _F32), _sds((1, 2 * SSM_WIDTH), _F32), _sds((1, SSM_WIDTH), _F32)],
                 "ssm_out_bwd")(dn, y, z, u, ct_re, ct_im, d_row, w_glu, g_ssm)


def _ssm_du(lam_re, lam_im, bt_re, bt_im, dud):
    L = dud.shape[0]
    tm = _tile(L)

    def body(lr_ref, li_ref, br_ref, bi_ref, dud_ref, du_ref):
        for k in range(SUPER):
            du_ref[:, _sb_width(k)] = (_dot(lr_ref[:, _sb_state(k)], br_ref[_sb_state(k), :], _NN)
                                       + _dot(li_ref[:, _sb_state(k)], bi_ref[_sb_state(k), :], _NN)
                                       + dud_ref[:, _sb_width(k)])

    return _call(body, (L // tm,),
                 [_rows(tm, N_STATE), _rows(tm, N_STATE), _whole((N_STATE, SB_WIDTH)),
                  _whole((N_STATE, SB_WIDTH)), _rows(tm, SSM_WIDTH)],
                 _rows(tm, SSM_WIDTH), _sds((L, SSM_WIDTH), _F32), "ssm_du")(lam_re, lam_im, bt_re, bt_im, dud)


def _ssm_weight_grads(dy, x_re, x_im, lam_re, lam_im, u):
    L = u.shape[0]

    def body(dy_ref, xr_ref, xi_ref, lr_ref, li_ref, u_ref, dcr_ref, dci_ref, dbr_ref, dbi_ref):
        dyb = dy_ref[...]
        ub = u_ref[...].astype(_BF16)
        dcr_ref[...] = _dot(dyb, xr_ref[...], _TN)
        dci_ref[...] = _dot(dyb, xi_ref[...], _TN)
        dbr_ref[...] = _dot(lr_ref[...], ub, _TN)
        dbi_ref[...] = _dot(li_ref[...], ub, _TN)

    width = pl.BlockSpec((L, SB_WIDTH), lambda k: (0, k))
    state = pl.BlockSpec((L, SB_STATE), lambda k: (0, k))
    out_c = pl.BlockSpec((SB_WIDTH, SB_STATE), lambda k: (0, k))
    out_b = pl.BlockSpec((SB_STATE, SB_WIDTH), lambda k: (k, 0))
    return _call(body, (SUPER,), [width, state, state, state, state, width], [out_c, out_c, out_b, out_b],
                 [_sds((SB_WIDTH, N_STATE), _F32)] * 2 + [_sds((N_STATE, SB_WIDTH), _F32)] * 2,
                 "ssm_weight_grads")(dy, x_re, x_im, lam_re, lam_im, u)


def _ssm_param_bwd(da_re_c, da_im_c, dbt_re, dbt_im, dct_re, dct_im,
                   lam_re_c, lam_im_c, ldt_c, b_re2, b_im2):
    def body(dar, dai, dbr, dbi, dcr, dci, lrc, lic, ldc, bre, bim,
             glr, gli, gdt, gbr, gbi, gcr, gci):
        own_b = ((_iota((N_STATE, SB_WIDTH), 0) >> 6) & 7) == (_iota((N_STATE, SB_WIDTH), 1) >> 4)
        own_c = (_iota((SB_WIDTH, SB_STATE), 0) >> 4) == (_iota((SB_WIDTH, SB_STATE), 1) >> 6)

        def fold_b(ref):
            t = jnp.where(own_b, ref[...], 0.0)
            for shift in (64, 32, 16):
                t = t + pltpu.roll(t, shift, 1)
            return t[:, :SSM_GROUP]

        def fold_c(ref, k):
            t = jnp.where(own_c, ref[:, _sb_state(k)], 0.0)
            t = sum(t[:, 128 * i:128 * (i + 1)] for i in range(SB_STATE // 128))
            return (t + pltpu.roll(t, SSM_STATE, 1))[:, :SSM_STATE]

        dbbr = fold_b(dbr)
        dbbi = fold_b(dbi)
        for k in range(SUPER):
            gcr[_sb_width(k), :] = fold_c(dcr, k)
            gci[_sb_width(k), :] = -fold_c(dci, k)
        _, vjp = jax.vjp(_s5_bbar, lrc[...], lic[...], ldc[...], bre[...], bim[...])
        d_lr, d_li, d_dt, d_br, d_bi = vjp((dar[...], dai[...], dbbr, dbbi))
        gbr[...] = d_br
        gbi[...] = d_bi
        groups = ((_iota((SSM_GROUPS, N_STATE), 1) >> 6) == _iota((SSM_GROUPS, N_STATE), 0)).astype(_F32)
        in_group = ((_iota((N_STATE, SSM_STATE), 0) & (SSM_STATE - 1)) == _iota((N_STATE, SSM_STATE), 1)).astype(_F32)
        glr[...] = _dot_exact(groups, d_lr * in_group, _NN)
        gli[...] = _dot_exact(groups, d_li * in_group, _NN)
        groups_t = ((_iota((N_STATE, SSM_GROUPS), 0) >> 6) == _iota((N_STATE, SSM_GROUPS), 1)).astype(_F32)
        gdt[...] = _dot_exact(jnp.broadcast_to(d_dt, (N_STATE, 128)), groups_t, _TN)[0:1]

    ins = [da_re_c, da_im_c, dbt_re, dbt_im, dct_re, dct_im, lam_re_c, lam_im_c, ldt_c, b_re2, b_im2]
    outs = [(SSM_GROUPS, SSM_STATE), (SSM_GROUPS, SSM_STATE), (1, SSM_GROUPS), (N_STATE, SSM_GROUP),
            (N_STATE, SSM_GROUP), (SSM_WIDTH, SSM_STATE), (SSM_WIDTH, SSM_STATE)]
    return _call(body, (1,), [_whole(a.shape) for a in ins], [_whole(s) for s in outs],
                 [_sds(s, _F32) for s in outs], "ssm_param_bwd")(*ins)


def _head_spread(j):
    r = _iota((KV_WIDTH, 256), 0)
    c = _iota((KV_WIDTH, 256), 1)
    return (r == HEAD_DIM * j + (c & (HEAD_DIM - 1))).astype(_BF16)


STACK = Q_PER_KV * BLOCK


def _stack_heads(t):
    lane_head = _iota((1, 256), 1) >> 6
    return jnp.concatenate([jnp.where(lane_head == g, t, jnp.zeros_like(t)) for g in range(Q_PER_KV)], axis=0)


def _unstack_heads(t):
    lane_head = _iota((1, 256), 1) >> 6
    return sum(jnp.where(lane_head == g, t[BLOCK * g:BLOCK * (g + 1)], 0.0) for g in range(Q_PER_KV))


def _stacked_sinks(sink_ref, j):
    block = _iota((STACK, 1), 0) >> 7
    col = jnp.full((STACK, 1), sink_ref[Q_PER_KV * j], _F32)
    for g in range(1, Q_PER_KV):
        col = jnp.where(block == g, sink_ref[Q_PER_KV * j + g], col)
    return col


def _fold_heads(t, j):
    t = t[:, :KV_WIDTH] + t[:, KV_WIDTH:]
    t = t + pltpu.roll(t, HEAD_DIM, 1)
    return jnp.where((_iota((1, KV_WIDTH), 1) >> 6) == j, t, 0.0)


def _attn_scores(q_stacked, kt, blk, sink):
    s = _dot(q_stacked, kt, _NT) * (HEAD_DIM ** -0.5)
    qi = _iota((STACK, 2 * BLOCK), 0) & (BLOCK - 1)
    kj = _iota((STACK, 2 * BLOCK), 1)
    rel = qi + BLOCK - kj
    valid = (rel >= 0) & (rel < BLOCK) & (blk * BLOCK - BLOCK + kj >= 0)
    s = jnp.where(valid, s, MASK_VALUE)
    m = jnp.maximum(jnp.max(s, axis=-1, keepdims=True), sink)
    p = jnp.exp(s - m)
    e_sink = jnp.exp(sink - m)
    den = jnp.sum(p, axis=-1, keepdims=True) + e_sink
    return p / den, e_sink / den


def _attn_specs():
    prev = lambda i: (jnp.maximum(i - 1, 0), 0)
    cur = lambda i: (i, 0)
    kv = [pl.BlockSpec((BLOCK, KV_WIDTH), prev), pl.BlockSpec((BLOCK, KV_WIDTH), cur)]
    return [pl.BlockSpec((BLOCK, ATTN_WIDTH), cur)] + kv + kv


def _attn_fwd(q, k, v, sinks, g_attn):
    L = q.shape[0]

    def body(q_ref, kp_ref, kc_ref, vp_ref, vc_ref, sink_ref, g_ref, o_ref, n_ref):
        blk = pl.program_id(0)
        kwin = jnp.concatenate([kp_ref[...], kc_ref[...]], axis=0)
        vwin = jnp.concatenate([vp_ref[...], vc_ref[...]], axis=0)
        halves = []
        for j in range(N_KV_HEADS):
            spread = _head_spread(j)
            kt = _dot(kwin, spread, _NN).astype(_BF16)
            vt = _dot(vwin, spread, _NN).astype(_BF16)
            qs = _stack_heads(q_ref[:, 256 * j:256 * (j + 1)])
            p, _ = _attn_scores(qs, kt, blk, _stacked_sinks(sink_ref, j))
            halves.append(_unstack_heads(_dot(p, vt, _NN)))
        o = jnp.concatenate(halves, axis=1)
        o_ref[...] = o
        n, _ = _rms_fwd(o, g_ref[...])
        n_ref[...] = n.astype(_BF16)

    cur = lambda i: (i, 0)
    return _call(body, (L // BLOCK,),
                 _attn_specs() + [pl.BlockSpec(memory_space=pltpu.SMEM), _whole((1, ATTN_WIDTH))],
                 [pl.BlockSpec((BLOCK, ATTN_WIDTH), cur)] * 2,
                 [_sds((L, ATTN_WIDTH), _F32), _sds((L, ATTN_WIDTH), _BF16)],
                 "attn_fwd")(q, k, k, v, v, sinks, g_attn)


def _attn_bwd(q, k, v, o, dn, sinks, g_attn):
    L = q.shape[0]

    def body(q_ref, kp_ref, kc_ref, vp_ref, vc_ref, o_ref, dn_ref, sink_ref, g_ref,
             dq_ref, dk_ref, dv_ref, dsink_ref, dg_ref):
        blk = pl.program_id(0)
        first = blk == 0

        @pl.when(first)
        def _():
            dk_ref[...] = jnp.zeros_like(dk_ref)
            dv_ref[...] = jnp.zeros_like(dv_ref)
            dsink_ref[...] = jnp.zeros_like(dsink_ref)

        o = o_ref[...]
        g = g_ref[...]
        _, r = _rms_fwd(o, g)
        do, dg = _rms_bwd(dn_ref[...], o, g, r)
        _accumulate(dg_ref, dg, first)
        kwin = jnp.concatenate([kp_ref[...], kc_ref[...]], axis=0)
        vwin = jnp.concatenate([vp_ref[...], vc_ref[...]], axis=0)
        lane = _iota((1, 128), 1)
        dsink = jnp.zeros((1, 128), _F32)
        dkwin = jnp.zeros((2 * BLOCK, KV_WIDTH), _F32)
        dvwin = jnp.zeros((2 * BLOCK, KV_WIDTH), _F32)
        dq_halves = []
        for j in range(N_KV_HEADS):
            spread = _head_spread(j)
            kt = _dot(kwin, spread, _NN).astype(_BF16)
            vt = _dot(vwin, spread, _NN).astype(_BF16)
            qs = _stack_heads(q_ref[:, 256 * j:256 * (j + 1)])
            dos = _stack_heads(do[:, 256 * j:256 * (j + 1)]).astype(_BF16)
            p, p_sink = _attn_scores(qs, kt, blk, _stacked_sinks(sink_ref, j))
            dp = _dot(dos, vt, _NT)
            delta = jnp.sum(p * dp, axis=-1, keepdims=True)
            ds = (p * (dp - delta) * (HEAD_DIM ** -0.5)).astype(_BF16)
            sink_term = p_sink * delta
            for g in range(Q_PER_KV):
                head_sum = jnp.sum(sink_term[BLOCK * g:BLOCK * (g + 1)], axis=0, keepdims=True)
                dsink = dsink - jnp.where(lane == Q_PER_KV * j + g, head_sum, 0.0)
            dvwin = dvwin + _fold_heads(_dot(p, dos, _TN), j)
            dkwin = dkwin + _fold_heads(_dot(ds, qs, _TN), j)
            dq_halves.append(_unstack_heads(_dot(ds, kt, _NN)))
        dq_ref[...] = jnp.concatenate(dq_halves, axis=1)
        dsink_ref[...] += dsink
        prev = pl.ds(pl.multiple_of(jnp.maximum(blk - 1, 0) * BLOCK, BLOCK), BLOCK)
        cur = pl.ds(pl.multiple_of(blk * BLOCK, BLOCK), BLOCK)
        dk_ref[prev, :] += dkwin[:BLOCK]
        dk_ref[cur, :] += dkwin[BLOCK:]
        dv_ref[prev, :] += dvwin[:BLOCK]
        dv_ref[cur, :] += dvwin[BLOCK:]

    cur = lambda i: (i, 0)
    blk_q = pl.BlockSpec((BLOCK, ATTN_WIDTH), cur)
    return _call(body, (L // BLOCK,),
                 _attn_specs() + [blk_q, blk_q, pl.BlockSpec(memory_space=pltpu.SMEM), _whole((1, ATTN_WIDTH))],
                 [blk_q, _whole((L, KV_WIDTH)), _whole((L, KV_WIDTH)), _whole((1, 128)), _whole((1, ATTN_WIDTH))],
                 [_sds((L, ATTN_WIDTH), _F32), _sds((L, KV_WIDTH), _F32), _sds((L, KV_WIDTH), _F32),
                  _sds((1, 128), _F32), _sds((1, ATTN_WIDTH), _F32)],
                 "attn_bwd")(q, k, k, v, v, o, dn, sinks, g_attn)


def _out_proj(n_ssm, n_attn, x, w_out, g_post_mix, g_pre_ffn):
    L = x.shape[0]
    tm = _tile(L)

    def body(ns_ref, na_ref, x_ref, w_ref, g1_ref, g2_ref, merged_ref, mo_ref, h1_ref, hn2_ref):
        merged = jnp.concatenate([ns_ref[...], na_ref[...]], axis=1)
        merged_ref[...] = merged
        mo = _dot(merged, w_ref[...], _NN)
        mo_ref[...] = mo
        n, _ = _rms_fwd(mo, g1_ref[...])
        h1 = x_ref[...] + n
        h1_ref[...] = h1
        hn2, _ = _rms_fwd(h1, g2_ref[...])
        hn2_ref[...] = hn2.astype(_BF16)

    row = _whole((1, D_MODEL))
    return _call(body, (L // tm,),
                 [_rows(tm, SSM_WIDTH), _rows(tm, ATTN_WIDTH), _rows(tm, D_MODEL), _whole((D_MODEL, D_MODEL)), row, row],
                 [_rows(tm, D_MODEL)] * 4,
                 [_sds((L, D_MODEL), _BF16), _sds((L, D_MODEL), _F32), _sds((L, D_MODEL), _F32), _sds((L, D_MODEL), _BF16)],
                 "out_proj")(n_ssm, n_attn, x, w_out, g_post_mix, g_pre_ffn)


def _ffn(hn2, h1, target, w_gate_up, w_down, g_pre_ffn, g_post_ffn):
    L = h1.shape[0]
    tm = _tile(L)
    half = D_FF // 2

    def body(hn2_ref, h1_ref, tgt_ref, wgu_hbm, wd_hbm, g2_ref, g3_ref,
             act_ref, dgu_ref, dff_ref, dh1_ref, loss_ref, dg3_ref, dg2_ref,
             wgu, wd, gu, sem):
        first = pl.program_id(0) == 0

        @pl.when(first)
        def _():
            c1 = pltpu.make_async_copy(wgu_hbm, wgu, sem.at[0])
            c2 = pltpu.make_async_copy(wd_hbm, wd, sem.at[1])
            c1.start()
            c2.start()
            c1.wait()
            c2.wait()

        hn2 = hn2_ref[...]
        ff = jnp.zeros((tm, D_MODEL), _F32)
        for c in range(2):
            gate = _dot(hn2, wgu[half * c:half * (c + 1), :], _NT)
            up = _dot(hn2, wgu[D_FF + half * c:D_FF + half * (c + 1), :], _NT)
            gu[:, half * c:half * (c + 1)] = gate
            gu[:, D_FF + half * c:D_FF + half * (c + 1)] = up
            act = (gate * jax.nn.sigmoid(gate) * up).astype(_BF16)
            act_ref[:, half * c:half * (c + 1)] = act
            ff = ff + _dot(act, wd[half * c:half * (c + 1), :], _NN)
        g3 = g3_ref[...]
        n, r = _rms_fwd(ff, g3)
        h1 = h1_ref[...]
        err = h1 + n - tgt_ref[...]
        loss = 0.5 * jnp.sum(jnp.mean(err * err, axis=-1, keepdims=True), axis=0, keepdims=True)
        _accumulate(loss_ref, jnp.broadcast_to(loss, (1, 128)), first)
        dh2 = err * (1.0 / D_MODEL)
        dff, dg3 = _rms_bwd(dh2, ff, g3, r)
        _accumulate(dg3_ref, dg3, first)
        dffb = dff.astype(_BF16)
        dff_ref[...] = dffb
        dhn2 = jnp.zeros((tm, D_MODEL), _F32)
        for c in range(2):
            dact = _dot(dffb, wd[half * c:half * (c + 1), :], _NT)
            gate = gu[:, half * c:half * (c + 1)]
            up = gu[:, D_FF + half * c:D_FF + half * (c + 1)]
            sig = jax.nn.sigmoid(gate)
            silu = gate * sig
            dgate = (dact * up * (sig + silu * (1.0 - sig))).astype(_BF16)
            dup = (dact * silu).astype(_BF16)
            dgu_ref[:, half * c:half * (c + 1)] = dgate
            dgu_ref[:, D_FF + half * c:D_FF + half * (c + 1)] = dup
            dhn2 = dhn2 + _dot(dgate, wgu[half * c:half * (c + 1), :], _NN)
            dhn2 = dhn2 + _dot(dup, wgu[D_FF + half * c:D_FF + half * (c + 1), :], _NN)
        g2 = g2_ref[...]
        _, r2 = _rms_fwd(h1, g2)
        dh1, dg2 = _rms_bwd(dhn2, h1, g2, r2)
        _accumulate(dg2_ref, dg2, first)
        dh1_ref[...] = dh2 + dh1

    row = _whole((1, D_MODEL))
    anyspace = pl.BlockSpec(memory_space=pl.ANY)
    return _call(body, (L // tm,),
                 [_rows(tm, D_MODEL), _rows(tm, D_MODEL), _rows(tm, D_MODEL), anyspace, anyspace, row, row],
                 [_rows(tm, D_FF), _rows(tm, 2 * D_FF), _rows(tm, D_MODEL), _rows(tm, D_MODEL),
                  _whole((1, 128)), row, row],
                 [_sds((L, D_FF), _BF16), _sds((L, 2 * D_FF), _BF16), _sds((L, D_MODEL), _BF16),
                  _sds((L, D_MODEL), _F32), _sds((1, 128), _F32), _sds((1, D_MODEL), _F32), _sds((1, D_MODEL), _F32)],
                 "ffn",
                 scratch=[pltpu.VMEM((2 * D_FF, D_MODEL), _BF16), pltpu.VMEM((D_FF, D_MODEL), _BF16),
                          pltpu.VMEM((tm, 2 * D_FF), _F32), pltpu.SemaphoreType.DMA((2,))],
                 )(hn2, h1, target, w_gate_up, w_down, g_pre_ffn, g_post_ffn)


def _out_proj_bwd(dh1, mo, w_out, g_post_mix, tokens=()):
    L = dh1.shape[0]
    tm = _tile(L)

    def body(dh1_ref, mo_ref, w_ref, g_ref, dmo_ref, dns_ref, dna_ref, dg_ref):
        first = pl.program_id(0) == 0
        mo = mo_ref[...]
        g = g_ref[...]
        _, r = _rms_fwd(mo, g)
        dmo, dg = _rms_bwd(dh1_ref[...], mo, g, r)
        _accumulate(dg_ref, dg, first)
        dmob = dmo.astype(_BF16)
        dmo_ref[...] = dmob
        dmerged = _dot(dmob, w_ref[...], _NT)
        dns_ref[...] = dmerged[:, :SSM_WIDTH]
        dna_ref[...] = dmerged[:, SSM_WIDTH:]

    row = _whole((1, D_MODEL))
    return _call(body, (L // tm,),
                 [_rows(tm, D_MODEL), _rows(tm, D_MODEL), _whole((D_MODEL, D_MODEL)), row],
                 [_rows(tm, D_MODEL), _rows(tm, SSM_WIDTH), _rows(tm, ATTN_WIDTH), row],
                 [_sds((L, D_MODEL), _BF16), _sds((L, SSM_WIDTH), _F32), _sds((L, ATTN_WIDTH), _F32),
                  _sds((1, D_MODEL), _F32)],
                 "out_proj_bwd", tokens=tokens)(dh1, mo, w_out, g_post_mix)


def _in_proj_bwd(du, dq, dk, dv, cos_t, sin_t, x, dh1, g_pre_mix, w_in, tokens=()):
    L = x.shape[0]
    tm = _tile(L)

    def body(du_ref, dq_ref, dk_ref, dv_ref, cos_ref, sin_ref, x_ref, dh1_ref, g_ref, w_ref,
             dproj_ref, dx_ref, dg_ref):
        first = pl.program_id(0) == 0
        cos_v, sin_v = cos_ref[...], sin_ref[...]
        dproj = jnp.concatenate([du_ref[...], _rope_transpose(dq_ref[...], cos_v, sin_v),
                                 _rope_transpose(dk_ref[...], cos_v, sin_v), dv_ref[...]], axis=1).astype(_BF16)
        dproj_ref[...] = dproj
        dhn = _dot(dproj, w_ref[...], _NN)
        x = x_ref[...]
        g = g_ref[...]
        _, r = _rms_fwd(x, g)
        dx, dg = _rms_bwd(dhn, x, g, r)
        _accumulate(dg_ref, dg, first)
        dx_ref[...] = dh1_ref[...] + dx

    row = _whole((1, D_MODEL))
    return _call(body, (L // tm,),
                 [_rows(tm, SSM_WIDTH), _rows(tm, ATTN_WIDTH), _rows(tm, KV_WIDTH), _rows(tm, KV_WIDTH),
                  _rows(tm, KV_WIDTH), _rows(tm, KV_WIDTH), _rows(tm, D_MODEL), _rows(tm, D_MODEL), row,
                  _whole((IN_WIDTH, D_MODEL))],
                 [_rows(tm, IN_WIDTH), _rows(tm, D_MODEL), row],
                 [_sds((L, IN_WIDTH), _BF16), _sds((L, D_MODEL), _F32), _sds((1, D_MODEL), _F32)],
                 "in_proj_bwd", tokens=tokens)(du, dq, dk, dv, cos_t, sin_t, x, dh1, g_pre_mix, w_in)


def _matmul_tn(a, b, out_dtype, name, scale=1.0):
    K, M = a.shape
    N = b.shape[1]
    tm = next(t for t in (512, 256, 128) if M % t == 0)
    tn = next(t for t in (512, 256, 128) if N % t == 0)

    def body(a_ref, b_ref, o_ref):
        acc = _dot(a_ref[...], b_ref[...], _TN)
        o_ref[...] = (acc if scale == 1.0 else acc * scale).astype(out_dtype)

    params = pltpu.CompilerParams(dimension_semantics=("arbitrary", "arbitrary"), vmem_limit_bytes=VMEM_LIMIT)
    return pl.pallas_call(body, grid=(M // tm, N // tn),
                          in_specs=[pl.BlockSpec((K, tm), lambda i, j: (0, i)),
                                    pl.BlockSpec((K, tn), lambda i, j: (0, j))],
                          out_specs=pl.BlockSpec((tm, tn), lambda i, j: (i, j)),
                          out_shape=_sds((M, N), out_dtype), compiler_params=params, name=name)(a, b)


def _to_chunked(a):
    L, n = a.shape
    return a.reshape(SCAN_CHUNKS, L // SCAN_CHUNKS, n).transpose(1, 0, 2).reshape(L, n)


def _from_chunked(a):
    L, n = a.shape
    return a.reshape(L // SCAN_CHUNKS, SCAN_CHUNKS, n).transpose(1, 0, 2).reshape(L, n)


def _local_step(x, pos, target, p, fetch, publish):
    L = x.shape[0]
    T = L // SCAN_CHUNKS
    cos_t, sin_t = _rope_tables(pos.reshape(L, 1))
    w_in, = fetch(("w_in",), None)
    hn, u, q, k, v = _in_proj(x, p["g_pre_mix"], w_in, cos_t, sin_t)

    lam_re_r = p["ssm_lambda_re"].reshape(1, N_STATE)
    lam_im_r = p["ssm_lambda_im"].reshape(1, N_STATE)
    ldt_r = jnp.broadcast_to(p["ssm_log_dt"].reshape(SSM_GROUPS, 1), (SSM_GROUPS, SSM_STATE)).reshape(1, N_STATE)
    lam_re_c, lam_im_c, ldt_c = (a.reshape(N_STATE, 1) for a in (lam_re_r, lam_im_r, ldt_r))
    b_re2 = p["ssm_b_re"].reshape(N_STATE, SSM_GROUP)
    b_im2 = p["ssm_b_im"].reshape(N_STATE, SSM_GROUP)
    c_re2 = p["ssm_c_re"].reshape(SSM_WIDTH, SSM_STATE)
    c_im2 = p["ssm_c_im"].reshape(SSM_WIDTH, SSM_STATE)
    d_row = p["ssm_d"].reshape(1, SSM_WIDTH)
    a_re, a_im, bt_re, bt_im, ct_re, ct_im = _ssm_prep(
        lam_re_r, lam_im_r, ldt_r, lam_re_c, lam_im_c, ldt_c, b_re2, b_im2, c_re2, c_im2)

    u_c = _to_chunked(u)
    bu_re, bu_im = _ssm_bu(u_c, bt_re, bt_im)
    x_re, x_im = _scan_fwd(bu_re.reshape(T, SCAN_CHUNKS, N_STATE), bu_im.reshape(T, SCAN_CHUNKS, N_STATE), a_re, a_im)
    w_glu, = fetch(("w_glu",), x_re)
    y, z, n_ssm_c = _ssm_out(x_re.reshape(L, N_STATE), x_im.reshape(L, N_STATE), u_c, ct_re, ct_im, d_row,
                             w_glu, p["b_glu"], p["g_ssm_out"])
    n_ssm = _from_chunked(n_ssm_c)

    sinks = p["attn_sinks"].reshape(N_Q_HEADS)
    o, n_attn = _attn_fwd(q, k, v, sinks, p["g_attn_out"])
    w_out, = fetch(("w_out",), n_attn)
    merged, mo, h1, hn2 = _out_proj(n_ssm, n_attn, x, w_out, p["g_post_mix"], p["g_pre_ffn"])
    w_gate_up, w_down = fetch(("w_gate_up", "w_down"), hn2)
    act, dgu, dff, dh1, loss, dg_post_ffn, dg_pre_ffn = _ffn(
        hn2, h1, target, w_gate_up, w_down, p["g_pre_ffn"], p["g_post_ffn"])
    grads = {"g_post_ffn": dg_post_ffn, "g_pre_ffn": dg_pre_ffn}
    tokens = publish({"w_down": _matmul_tn(act, dff, _BF16, "grad_w_down"),
                      "w_gate_up": _matmul_tn(dgu, hn2, _BF16, "grad_w_gate_up")})

    dmo, dn_ssm, dn_attn, grads["g_post_mix"] = _out_proj_bwd(dh1, mo, w_out, p["g_post_mix"], tokens)
    grad_w_out = _matmul_tn(merged, dmo, _BF16, "grad_w_out")

    dq, dk, dv, dsink, grads["g_attn_out"] = _attn_bwd(q, k, v, o, dn_attn, sinks, p["g_attn_out"])
    grads["attn_sinks"] = dsink

    gy, dz, dy, dud, dx_re, dx_im, grads["g_ssm_out"], grads["b_glu"], dd = _ssm_out_bwd(
        _to_chunked(dn_ssm), y, z, u_c, ct_re, ct_im, d_row, w_glu, p["g_ssm_out"])
    grads["ssm_d"] = dd.reshape(SSM_GROUPS, SSM_GROUP)
    tokens = publish({"w_out": grad_w_out, "w_glu": _matmul_tn(dz, gy, _BF16, "grad_w_glu")})
    lam_re, lam_im, da_re, da_im = _scan_bwd(dx_re.reshape(T, SCAN_CHUNKS, N_STATE), dx_im.reshape(T, SCAN_CHUNKS, N_STATE),
                                             x_re, x_im, a_re, a_im, tokens)
    lam_re = lam_re.reshape(L, N_STATE)
    lam_im = lam_im.reshape(L, N_STATE)
    dct_re, dct_im, dbt_re, dbt_im = _ssm_weight_grads(
        dy, x_re.reshape(L, N_STATE), x_im.reshape(L, N_STATE), lam_re, lam_im, u_c)
    g_lr, g_li, g_dt, g_br, g_bi, g_cr, g_ci = _ssm_param_bwd(
        da_re.reshape(N_STATE, 1), da_im.reshape(N_STATE, 1), dbt_re, dbt_im, dct_re, dct_im,
        lam_re_c, lam_im_c, ldt_c, b_re2, b_im2)
    grads.update(ssm_lambda_re=g_lr, ssm_lambda_im=g_li, ssm_log_dt=g_dt, ssm_b_re=g_br, ssm_b_im=g_bi,
                 ssm_c_re=g_cr, ssm_c_im=g_ci, loss=loss)
    publish(grads)

    du = _from_chunked(_ssm_du(lam_re, lam_im, bt_re, bt_im, dud))
    dproj, grad_x, g_pre_mix = _in_proj_bwd(du, dq, dk, dv, cos_t, sin_t, x, dh1, p["g_pre_mix"], w_in, [g_dt])
    publish({"g_pre_mix": g_pre_mix, "w_in": _matmul_tn(dproj, hn, _BF16, "grad_w_in")})
    return grad_x


_MESH = pl.DeviceIdType.MESH
_PEERS = N_DEV - 1


def _mesh_pos():
    return lax.axis_index("x"), lax.axis_index("y"), lax.axis_index("c")


def _dev_index(px, py, pc):
    return 4 * px + 2 * py + pc


def _all_gather(shards, out_dtype, name):
    n = len(shards)

    def body(*refs):
        ins, outs, stages = refs[:n], refs[n:2 * n], refs[2 * n:3 * n]
        send_sems, recv_sems, local_sems = refs[3 * n:]
        x, y, c = _mesh_pos()
        me, sibling = (x, y, c), (x, y, 1 - c)
        chips = [(1 - x, y), (x, 1 - y), (1 - x, 1 - y)]

        def copy(w, k, block, to, src=None):
            slot = outs[w].at[_dev_index(*block)]
            return pltpu.make_async_remote_copy(
                src_ref=slot if src is None else src, dst_ref=slot,
                send_sem=send_sems.at[_PEERS * w + k], recv_sem=recv_sems.at[_PEERS * w + k],
                device_id=to, device_id_type=_MESH)

        for w in range(n):
            stages[w][...] = ins[w][...].astype(out_dtype)
        mine, first, passed = [], [], []
        for w in range(n):
            cp = pltpu.make_async_copy(stages[w], outs[w].at[_dev_index(*me)], local_sems.at[w])
            cp.start()
            mine.append(cp)
            sends = [copy(w, 0, me, sibling, src=stages[w])]
            sends += [copy(w, 1 + j, me, (*chip, c), src=stages[w]) for j, chip in enumerate(chips)]
            for cp in sends:
                cp.start()
            first += sends
        for w in range(n):
            for j, chip in enumerate(chips):
                copy(w, 1 + j, (*chip, c), me).wait_recv()
                cp = copy(w, 4 + j, (*chip, c), sibling)
                cp.start()
                passed.append(cp)
        for w in range(n):
            copy(w, 0, sibling, me).wait_recv()
            for j, chip in enumerate(chips):
                copy(w, 4 + j, (*chip, 1 - c), me).wait_recv()
        for cp in first + passed:
            cp.wait_send()
        for cp in mine:
            cp.wait()

    return pl.pallas_call(
        body, name=name,
        out_shape=[_sds((N_DEV,) + s.shape, out_dtype) for s in shards],
        in_specs=[pl.BlockSpec(memory_space=pltpu.VMEM)] * n,
        out_specs=[pl.BlockSpec(memory_space=pl.ANY)] * n,
        scratch_shapes=[pltpu.VMEM(s.shape, out_dtype) for s in shards]
        + [pltpu.SemaphoreType.DMA((_PEERS * n,)), pltpu.SemaphoreType.DMA((_PEERS * n,)),
           pltpu.SemaphoreType.DMA((n,))],
        compiler_params=pltpu.CompilerParams(vmem_limit_bytes=VMEM_LIMIT),
    )(*shards)


_HBM_SPEC = pl.BlockSpec(memory_space=pltpu.HBM)
_SEM_SPEC = pl.BlockSpec(memory_space=pltpu.SEMAPHORE)
_DATAFLOW = pltpu.SideEffectType.DATAFLOW_SIDE_EFFECTING


def _peer(x, y, c, r):
    return (x ^ ((r >> 2) & 1), y ^ ((r >> 1) & 1), c ^ (r & 1))


def _hbm(a):
    return pltpu.with_memory_space_constraint(a, pltpu.HBM)


def _send_start(sources, blocked, name):
    n = len(sources)
    lands = [lax.empty((N_DEV,) + (s.shape[1:] if blocked else s.shape), s.dtype) for s in sources]

    def body(*refs):
        srcs, zones = refs[:n], refs[n:2 * n]
        send_sems, recv_sems = refs[2 * n:3 * n], refs[3 * n:4 * n]
        token, local_sems = refs[6 * n], refs[6 * n + 1]
        x, y, c = _mesh_pos()
        me = _dev_index(x, y, c)
        local = []
        for w in range(n):
            cp = pltpu.make_async_copy(srcs[w].at[me] if blocked else srcs[w], zones[w].at[me], local_sems.at[w])
            cp.start()
            local.append(cp)
            for r in range(1, N_DEV):
                peer = _peer(x, y, c, r)
                pltpu.make_async_remote_copy(
                    src_ref=srcs[w].at[_dev_index(*peer)] if blocked else srcs[w], dst_ref=zones[w].at[me],
                    send_sem=send_sems[w].at[r - 1], recv_sem=recv_sems[w].at[r - 1],
                    device_id=peer, device_id_type=_MESH).start()
        for cp in local:
            cp.wait()
        token[...] = jnp.zeros_like(token)

    sems = [pltpu.SemaphoreType.DMA((_PEERS,))] * (2 * n)
    out = pl.pallas_call(
        body, name=name,
        out_shape=sems + [pltpu.HBM(a.shape, a.dtype) for a in list(sources) + lands] + [_sds((8, 128), _F32)],
        in_specs=[_HBM_SPEC] * (2 * n),
        out_specs=[_SEM_SPEC] * (2 * n) + [_HBM_SPEC] * (2 * n) + [pl.BlockSpec(memory_space=pltpu.VMEM)],
        input_output_aliases={i: 2 * n + i for i in range(2 * n)},
        scratch_shapes=[pltpu.SemaphoreType.DMA((n,))],
        compiler_params=pltpu.CompilerParams(has_side_effects=_DATAFLOW),
    )(*[_hbm(a) for a in sources], *[_hbm(a) for a in lands])
    return out[:n], out[n:2 * n], out[2 * n:3 * n], out[3 * n:4 * n], out[4 * n]


def _send_wait(send_sems, recv_sems, sources, lands, after, blocked, name):
    n = len(sources)

    def body(*refs):
        srcs, zones = refs[:n], refs[n:2 * n]
        sends, recvs = refs[2 * n:3 * n], refs[3 * n:4 * n]
        x, y, c = _mesh_pos()
        for w in range(n):
            for r in range(1, N_DEV):
                peer = _peer(x, y, c, r)
                idx = _dev_index(*peer)
                cp = pltpu.make_async_remote_copy(
                    src_ref=srcs[w].at[idx] if blocked else srcs[w], dst_ref=zones[w].at[idx],
                    send_sem=sends[w].at[r - 1], recv_sem=recvs[w].at[r - 1],
                    device_id=peer, device_id_type=_MESH)
                cp.wait_send()
                cp.wait_recv()

    out = pl.pallas_call(
        body, name=name,
        out_shape=[pltpu.HBM(a.shape, a.dtype) for a in list(sources) + list(lands)],
        in_specs=[_HBM_SPEC] * (2 * n) + [_SEM_SPEC] * (2 * n) + [pl.BlockSpec(memory_space=pl.ANY)],
        out_specs=[_HBM_SPEC] * (2 * n),
        input_output_aliases={i: i for i in range(2 * n)},
        compiler_params=pltpu.CompilerParams(has_side_effects=_DATAFLOW),
    )(*sources, *lands, *send_sems, *recv_sems, after)
    return out[n:]


def _sequencer_exchange(sources, blocked, name, collective_id):
    n = len(sources)
    flags = blocked

    def body(*refs):
        srcs, zones = refs[:n], refs[n:2 * n]
        send_sems, recv_sems, local_sems = refs[2 * n:]
        x, y, c = _mesh_pos()
        me = _dev_index(x, y, c)
        barrier = pltpu.get_barrier_semaphore()
        for r in range(1, N_DEV):
            pl.semaphore_signal(barrier, inc=1, device_id=_peer(x, y, c, r), device_id_type=_MESH)
        pl.semaphore_wait(barrier, _PEERS)
        local, sends, recvs = [], [], []
        for w in range(n):
            cp = pltpu.make_async_copy(srcs[w].at[me] if flags[w] else srcs[w], zones[w].at[me], local_sems.at[w])
            cp.start()
            local.append(cp)
            for r in range(1, N_DEV):
                peer = _peer(x, y, c, r)
                idx = _dev_index(*peer)
                k = _PEERS * w + r - 1
                src = srcs[w].at[idx] if flags[w] else srcs[w]
                send = pltpu.make_async_remote_copy(
                    src_ref=src, dst_ref=zones[w].at[me], send_sem=send_sems.at[k], recv_sem=recv_sems.at[k],
                    device_id=peer, device_id_type=_MESH)
                send.start()
                sends.append(send)
                recvs.append(pltpu.make_async_remote_copy(
                    src_ref=src, dst_ref=zones[w].at[idx], send_sem=send_sems.at[k], recv_sem=recv_sems.at[k],
                    device_id=peer, device_id_type=_MESH))
        for cp in recvs:
            cp.wait_recv()
        for cp in sends:
            cp.wait_send()
        for cp in local:
            cp.wait()

    return pl.kernel(
        body, name=name,
        out_type=[_sds((N_DEV,) + (s.shape[1:] if f else s.shape), s.dtype) for s, f in zip(sources, flags)],
        mesh=plsc.ScalarSubcoreMesh(axis_name="sequencer", num_cores=1),
        scratch_types=[pltpu.SemaphoreType.DMA((_PEERS * n,)), pltpu.SemaphoreType.DMA((_PEERS * n,)),
                       pltpu.SemaphoreType.DMA((n,))],
        compiler_params=pltpu.CompilerParams(collective_id=collective_id),
    )(*sources)


def _sequencer_gather(shards, name, collective_id):
    n = len(shards)
    fan = 4

    def body(*refs):
        srcs, zones = refs[:n], refs[n:2 * n]
        send_sems, recv_sems, local_sems = refs[2 * n:]
        x, y, c = _mesh_pos()
        me, sibling = (x, y, c), (x, y, 1 - c)
        chips = [(1 - x, y), (x, 1 - y), (1 - x, 1 - y)]
        barrier = pltpu.get_barrier_semaphore()
        for peer in [sibling] + [(*chip, c) for chip in chips]:
            pl.semaphore_signal(barrier, inc=1, device_id=peer, device_id_type=_MESH)
        pl.semaphore_wait(barrier, fan)

        def copy(w, k, block, to, src=None):
            slot = zones[w].at[_dev_index(*block)]
            return pltpu.make_async_remote_copy(
                src_ref=slot if src is None else src, dst_ref=slot,
                send_sem=send_sems.at[_PEERS * w + k], recv_sem=recv_sems.at[_PEERS * w + k],
                device_id=to, device_id_type=_MESH)

        mine, first, passed = [], [], []
        for w in range(n):
            cp = pltpu.make_async_copy(srcs[w], zones[w].at[_dev_index(*me)], local_sems.at[w])
            cp.start()
            mine.append(cp)
            sends = [copy(w, 0, me, sibling, src=srcs[w])]
            sends += [copy(w, 1 + j, me, (*chip, c), src=srcs[w]) for j, chip in enumerate(chips)]
            for cp in sends:
                cp.start()
            first += sends
        for w in range(n):
            for j, chip in enumerate(chips):
                copy(w, 1 + j, (*chip, c), me).wait_recv()
                cp = copy(w, fan + j, (*chip, c), sibling)
                cp.start()
                passed.append(cp)
        for w in range(n):
            copy(w, 0, sibling, me).wait_recv()
            for j, chip in enumerate(chips):
                copy(w, fan + j, (*chip, 1 - c), me).wait_recv()
        for cp in first + passed:
            cp.wait_send()
        for cp in mine:
            cp.wait()

    return pl.kernel(
        body, name=name, out_type=[_sds((N_DEV,) + s.shape, s.dtype) for s in shards],
        mesh=plsc.ScalarSubcoreMesh(axis_name="sequencer", num_cores=1),
        scratch_types=[pltpu.SemaphoreType.DMA((_PEERS * n,)), pltpu.SemaphoreType.DMA((_PEERS * n,)),
                       pltpu.SemaphoreType.DMA((n,))],
        compiler_params=pltpu.CompilerParams(collective_id=collective_id),
    )(*shards)


def _row_tile(rows):
    return next(t for t in range(min(rows, 256), 0, -16) if rows % t == 0)


def _sum_parts(parts, name):
    _, rows, cols = parts.shape
    tr = _row_tile(rows)

    def body(p_ref, g_ref):
        g = p_ref[0].astype(_F32)
        for s in range(1, N_DEV):
            g = g + p_ref[s].astype(_F32)
        g_ref[...] = g

    return _call(body, (rows // tr,), [pl.BlockSpec((N_DEV, tr, cols), lambda i: (0, i, 0))],
                 _rows(tr, cols), _sds((rows, cols), _F32), name)(parts)


def _adam_update(g, w, m, v):
    new_m = ADAM_B1 * m + (1.0 - ADAM_B1) * g
    new_v = ADAM_B2 * v + (1.0 - ADAM_B2) * (g * g)
    m_hat = new_m / (1.0 - ADAM_B1 ** ADAM_STEP)
    v_hat = new_v / (1.0 - ADAM_B2 ** ADAM_STEP)
    return -ADAM_LR * (m_hat / (jnp.sqrt(v_hat) + ADAM_EPS) + ADAM_WD * w), new_m, new_v


def _adamw_small(parts, items, sums, name):
    n_p, n_i = len(parts), len(items)

    def body(*refs):
        p_refs, state, outs = refs[:n_p], refs[n_p:n_p + 3 * n_i], refs[n_p + 3 * n_i:]

        def total(part, rows, cols):
            g = p_refs[part][0, rows, cols]
            for s in range(1, N_DEV):
                g = g + p_refs[part][s, rows, cols]
            return g

        for i, (part, rows, cols, _, _, _) in enumerate(items):
            g = total(part, rows, cols)
            w_ref, m_ref, v_ref = state[3 * i:3 * i + 3]
            delta, new_m, new_v = _adam_update(g, w_ref[...], m_ref[...], v_ref[...])
            outs[4 * i][...] = g
            outs[4 * i + 1][...] = delta
            outs[4 * i + 2][...] = new_m
            outs[4 * i + 3][...] = new_v
        for j, (part, rows, cols) in enumerate(sums):
            outs[4 * n_i + j][...] = total(part, rows, cols)

    ins = list(parts) + [a for item in items for a in item[3:]]
    out_shapes = [item[3].shape for item in items for _ in range(4)]
    out_shapes += [(rows.stop - rows.start, cols.stop - cols.start) for _, rows, cols in sums]
    out = _call(body, (1,), [_whole(a.shape) for a in ins], [_whole(s) for s in out_shapes],
                [_sds(s, _F32) for s in out_shapes], name)(*ins)
    return [out[4 * i:4 * i + 4] for i in range(n_i)], out[4 * n_i:]


def _adamw(parts, w, m, v, name):
    rows, cols = w.shape
    tr = _row_tile(rows)
    n_parts = parts.shape[0]

    def body(p_ref, w_ref, m_ref, v_ref, g_ref, d_ref, nm_ref, nv_ref):
        g = p_ref[0].astype(_F32)
        for s in range(1, n_parts):
            g = g + p_ref[s].astype(_F32)
        new_m = ADAM_B1 * m_ref[...] + (1.0 - ADAM_B1) * g
        new_v = ADAM_B2 * v_ref[...] + (1.0 - ADAM_B2) * (g * g)
        m_hat = new_m / (1.0 - ADAM_B1 ** ADAM_STEP)
        v_hat = new_v / (1.0 - ADAM_B2 ** ADAM_STEP)
        g_ref[...] = g
        d_ref[...] = -ADAM_LR * (m_hat / (jnp.sqrt(v_hat) + ADAM_EPS) + ADAM_WD * w_ref[...])
        nm_ref[...] = new_m
        nv_ref[...] = new_v

    blk = _rows(tr, cols)
    return _call(body, (rows // tr,),
                 [pl.BlockSpec((n_parts, tr, cols), lambda i: (0, i, 0)), blk, blk, blk],
                 [blk] * 4, [_sds((rows, cols), _F32)] * 4, name)(parts, w, m, v)


_SMALL = ("g_pre_mix", "ssm_lambda_re", "ssm_lambda_im", "ssm_log_dt", "ssm_b_re", "ssm_b_im",
          "ssm_c_re", "ssm_c_im", "ssm_d", "b_glu", "attn_sinks", "g_ssm_out", "g_attn_out",
          "g_post_mix", "g_pre_ffn", "g_post_ffn")
_BIG = ("w_in", "w_glu", "w_out", "w_gate_up", "w_down")
_WEIGHTS = ("g_pre_mix", "w_in", "ssm_lambda_re", "ssm_lambda_im", "ssm_log_dt", "ssm_b_re", "ssm_b_im",
            "ssm_c_re", "ssm_c_im", "ssm_d", "w_glu", "b_glu", "attn_sinks", "g_ssm_out", "g_attn_out",
            "w_out", "g_post_mix", "g_pre_ffn", "w_gate_up", "w_down", "g_post_ffn")
_LANES = 128


_SHAPE_2D = {
    "g_pre_mix": (1, D_MODEL), "ssm_lambda_re": (SSM_GROUPS, SSM_STATE), "ssm_lambda_im": (SSM_GROUPS, SSM_STATE),
    "ssm_log_dt": (1, SSM_GROUPS), "ssm_b_re": (N_STATE, SSM_GROUP), "ssm_b_im": (N_STATE, SSM_GROUP),
    "ssm_c_re": (SSM_WIDTH, SSM_STATE), "ssm_c_im": (SSM_WIDTH, SSM_STATE), "ssm_d": (SSM_GROUPS, SSM_GROUP),
    "b_glu": (1, 2 * SSM_WIDTH), "attn_sinks": (1, N_Q_HEADS), "g_ssm_out": (1, SSM_WIDTH),
    "g_attn_out": (1, ATTN_WIDTH), "g_post_mix": (1, D_MODEL), "g_pre_ffn": (1, D_MODEL), "g_post_ffn": (1, D_MODEL)}
_ROW_WIDTH = {"g_pre_mix": D_MODEL, "b_glu": 2 * SSM_WIDTH, "attn_sinks": _LANES, "g_ssm_out": SSM_WIDTH,
              "g_attn_out": ATTN_WIDTH, "g_post_mix": D_MODEL, "g_pre_ffn": D_MODEL, "g_post_ffn": D_MODEL,
              "loss": _LANES}
_DENSE = ("ssm_b_re", "ssm_b_im", "ssm_c_re", "ssm_c_im")


def _row_slots(names):
    slots, row, col = {}, 0, 0
    for n in names:
        width = _ROW_WIDTH[n]
        if col + width > D_MODEL:
            row, col = row + 1, 0
        slots[n] = (row, col, width)
        col += width
    return slots


def _stack_rows(named, slots):
    n_rows = -(-(max(r for r, _, _ in slots.values()) + 1) // 8) * 8
    lines = []
    for r in range(n_rows):
        pieces = [named[n] for n, (row, _, _) in slots.items() if row == r]
        used = sum(p.shape[1] for p in pieces)
        if used < D_MODEL:
            pieces.append(jnp.zeros((1, D_MODEL - used), _F32))
        lines.append(jnp.concatenate(pieces, axis=1) if len(pieces) > 1 else pieces[0])
    return jnp.concatenate(lines, axis=0)


def kernel(x, positions, g_pre_mix, w_in, ssm_lambda_re, ssm_lambda_im, ssm_log_dt, ssm_b_re, ssm_b_im, ssm_c_re, ssm_c_im, ssm_d, w_glu, b_glu, attn_sinks, g_ssm_out, g_attn_out, w_out, g_post_mix, g_pre_ffn, w_gate_up, w_down, g_post_ffn, loss_target, m_g_pre_mix, m_w_in, m_ssm_lambda_re, m_ssm_lambda_im, m_ssm_log_dt, m_ssm_b_re, m_ssm_b_im, m_ssm_c_re, m_ssm_c_im, m_ssm_d, m_w_glu, m_b_glu, m_attn_sinks, m_g_ssm_out, m_g_attn_out, m_w_out, m_g_post_mix, m_g_pre_ffn, m_w_gate_up, m_w_down, m_g_post_ffn, v_g_pre_mix, v_w_in, v_ssm_lambda_re, v_ssm_lambda_im, v_ssm_log_dt, v_ssm_b_re, v_ssm_b_im, v_ssm_c_re, v_ssm_c_im, v_ssm_d, v_w_glu, v_b_glu, v_attn_sinks, v_g_ssm_out, v_g_attn_out, v_w_out, v_g_post_mix, v_g_pre_ffn, v_w_gate_up, v_w_down, v_g_post_ffn):
    w = dict(g_pre_mix=g_pre_mix, w_in=w_in, ssm_lambda_re=ssm_lambda_re, ssm_lambda_im=ssm_lambda_im,
             ssm_log_dt=ssm_log_dt, ssm_b_re=ssm_b_re, ssm_b_im=ssm_b_im, ssm_c_re=ssm_c_re, ssm_c_im=ssm_c_im,
             ssm_d=ssm_d, w_glu=w_glu, b_glu=b_glu, attn_sinks=attn_sinks, g_ssm_out=g_ssm_out,
             g_attn_out=g_attn_out, w_out=w_out, g_post_mix=g_post_mix, g_pre_ffn=g_pre_ffn,
             w_gate_up=w_gate_up, w_down=w_down, g_post_ffn=g_post_ffn)
    m = dict(g_pre_mix=m_g_pre_mix, w_in=m_w_in, ssm_lambda_re=m_ssm_lambda_re, ssm_lambda_im=m_ssm_lambda_im,
             ssm_log_dt=m_ssm_log_dt, ssm_b_re=m_ssm_b_re, ssm_b_im=m_ssm_b_im, ssm_c_re=m_ssm_c_re,
             ssm_c_im=m_ssm_c_im, ssm_d=m_ssm_d, w_glu=m_w_glu, b_glu=m_b_glu, attn_sinks=m_attn_sinks,
             g_ssm_out=m_g_ssm_out, g_attn_out=m_g_attn_out, w_out=m_w_out, g_post_mix=m_g_post_mix,
             g_pre_ffn=m_g_pre_ffn, w_gate_up=m_w_gate_up, w_down=m_w_down, g_post_ffn=m_g_post_ffn)
    v = dict(g_pre_mix=v_g_pre_mix, w_in=v_w_in, ssm_lambda_re=v_ssm_lambda_re, ssm_lambda_im=v_ssm_lambda_im,
             ssm_log_dt=v_ssm_log_dt, ssm_b_re=v_ssm_b_re, ssm_b_im=v_ssm_b_im, ssm_c_re=v_ssm_c_re,
             ssm_c_im=v_ssm_c_im, ssm_d=v_ssm_d, w_glu=v_w_glu, b_glu=v_b_glu, attn_sinks=v_attn_sinks,
             g_ssm_out=v_g_ssm_out, g_attn_out=v_g_attn_out, w_out=v_w_out, g_post_mix=v_g_post_mix,
             g_pre_ffn=v_g_pre_ffn, w_gate_up=v_w_gate_up, w_down=v_w_down, g_post_ffn=v_g_post_ffn)

    transposed = ("w_in", "w_glu", "w_gate_up")
    native_transposed = ("w_in", "w_gate_up")
    shard = {n: (w[n][0].T if n in transposed else w[n][0]).astype(_BF16) for n in _BIG}
    gathered = {}
    for names, lands in (
            (("w_in",), _sequencer_exchange([shard["w_in"]], [False], "gather_w_in", 1)),
            (("w_glu", "w_out"), _sequencer_exchange([shard["w_glu"], shard["w_out"]], [False] * 2, "gather_mix", 2)),
            (("w_gate_up", "w_down"), _sequencer_gather([shard["w_gate_up"], shard["w_down"]], "gather_ffn", 3))):
        gathered.update({n: a.reshape(-1, a.shape[2]) for n, a in zip(names, lands)})

    def fetch(names, after):
        del after
        return [gathered[n] for n in names]

    sent = []

    def publish(named):
        big = [n for n in named if n in _BIG]
        rows = [n for n in named if n in _ROW_WIDTH]
        dense = [n for n in named if n in _DENSE]
        plain = [n for n in named if n not in big + rows + dense]
        sources = [named[n].reshape(N_DEV, -1, named[n].shape[1]) for n in big]
        slots = _row_slots(rows)
        if rows:
            sources.append(_stack_rows(named, slots))
        sources += [named[n].reshape(-1, _LANES) for n in dense] + [named[n] for n in plain]
        flags = [True] * len(big) + [False] * (len(sources) - len(big))
        cid = 4 + len(sent)
        sent.append((big, slots, dense, plain, _sequencer_exchange(sources, flags, "grads_%d" % cid, cid)))
        return []

    p = {n: w[n] for n in _SMALL}
    grad_x = _local_step(x[0], positions[0], loss_target[0], p, fetch, publish)

    state = {n: [a.reshape(_SHAPE_2D[n]) for a in (w[n], m[n], v[n])] for n in _SMALL}
    result = {}
    total_loss = None
    for big, slots, dense, plain, lands in sent:
        lands = list(lands)
        for name in big:
            part = lands.pop(0)
            if name in native_transposed:
                updated = _adamw(part, w[name][0].T, m[name][0].T, v[name][0].T, "adamw_" + name)
                result[name] = [a.T[None] for a in updated]
                continue
            if name in transposed:
                part = _sum_parts(part, "sum_" + name).T[None]
            result[name] = [a[None] for a in _adamw(part, w[name][0], m[name][0], v[name][0], "adamw_" + name)]
        parts, items, sums, names = [], [], [], []
        if slots:
            parts.append(lands.pop(0))
            for name, (row, col, _) in slots.items():
                if name == "loss":
                    sums.append((0, slice(row, row + 1), slice(col, col + _LANES)))
                else:
                    items.append((0, slice(row, row + 1), slice(col, col + _SHAPE_2D[name][1]), *state[name]))
                    names.append(name)
        for name in dense:
            part = lands.pop(0).reshape((N_DEV,) + _SHAPE_2D[name])
            result[name] = _adamw(part, *state[name], "adamw_" + name)
        for name in plain:
            rows_n, cols_n = _SHAPE_2D[name]
            items.append((len(parts), slice(0, rows_n), slice(0, cols_n), *state[name]))
            parts.append(lands.pop(0))
            names.append(name)
        if items:
            updated, summed = _adamw_small(parts, items, sums, "adamw_small_" + names[0])
            result.update(dict(zip(names, updated)))
            if summed:
                total_loss = summed[0][0, 0]

    out = [total_loss, grad_x[None]]
    for kind in range(4):
        out += [result[n][kind].reshape(w[n].shape) for n in _WEIGHTS]
    return tuple(out)
```

```python
import functools
import math

import numpy as np
import jax
import jax.numpy as jnp
from jax import lax
from jax.experimental import pallas as pl
from jax.experimental.pallas import tpu as pltpu
from jax.experimental.pallas import tpu_sc as plsc

D_MODEL = 1024
SSM_WIDTH = 512
SSM_GROUP = 16
SSM_GROUPS = 32
SSM_STATE = 64
N_STATE = SSM_GROUPS * SSM_STATE
ATTN_WIDTH = 512
HEAD_DIM = 64
N_Q_HEADS = 8
N_KV_HEADS = 2
Q_PER_KV = 4
KV_WIDTH = 128
IN_WIDTH = 1280
BLOCK = 128
ROPE_DIM = 16
ROPE_THETA = 500000.0
D_FF = 2816
NORM_EPS = 1e-6
MASK_VALUE = -1e30
ADAM_LR = 0.001
ADAM_B1 = 0.9
ADAM_B2 = 0.999
ADAM_EPS = 1e-08
ADAM_WD = 0.01
ADAM_STEP = 10

N_DEV = 8
SCAN_CHUNKS = 8
SCAN_COLS = 512
TOKEN_TILE = 256
VMEM_LIMIT = 56 * 1024 * 1024

_F32 = jnp.float32
_BF16 = jnp.bfloat16
_MXU = jnp.bfloat16

_NN = ((1,), (0,))
_NT = ((1,), (1,))
_TN = ((0,), (0,))


def _dot(a, b, dims):
    return lax.dot_general(a.astype(_MXU), b.astype(_MXU), (dims, ((), ())),
                           preferred_element_type=_F32)


def _dot_exact(a, b, dims):
    return lax.dot_general(a.astype(_F32), b.astype(_F32), (dims, ((), ())),
                           precision=lax.Precision.HIGHEST, preferred_element_type=_F32)


def _iota(shape, dim):
    return lax.broadcasted_iota(jnp.int32, shape, dim)


def _rms_fwd(x, g):
    r = lax.rsqrt(jnp.mean(x * x, axis=-1, keepdims=True) + NORM_EPS)
    return x * r * g, r


def _rms_bwd(dy, x, g, r):
    a = dy * g
    xn = x * r
    dx = r * (a - xn * jnp.mean(a * xn, axis=-1, keepdims=True))
    dg = jnp.sum(dy * xn, axis=0, keepdims=True)
    return dx, dg


def _call(body, grid, in_specs, out_specs, out_shape, name, scratch=(), tokens=()):
    params = pltpu.CompilerParams(dimension_semantics=("arbitrary",) * len(grid),
                                  vmem_limit_bytes=VMEM_LIMIT)
    n_in, n_tok = len(in_specs), len(tokens)

    def run(*refs):
        return body(*refs[:n_in], *refs[n_in + n_tok:])

    call = pl.pallas_call(run, grid=grid,
                          in_specs=list(in_specs) + [pl.BlockSpec(memory_space=pl.ANY)] * n_tok,
                          out_specs=out_specs, out_shape=out_shape, scratch_shapes=list(scratch),
                          compiler_params=params, name=name)
    return lambda *args: call(*args, *tokens)


def _rows(tm, n):
    return pl.BlockSpec((tm, n), lambda i: (i, 0))


def _whole(shape):
    nd = len(shape)
    return pl.BlockSpec(shape, lambda i: (0,) * nd)


def _sds(shape, dtype):
    return jax.ShapeDtypeStruct(shape, dtype)


def _tile(L):
    return min(TOKEN_TILE, L)


def _accumulate(ref, val, first):
    @pl.when(first)
    def _():
        ref[...] = val

    @pl.when(jnp.logical_not(first))
    def _():
        ref[...] += val


def _rope_rows():
    half = ROPE_DIM // 2
    inv = (np.float32(ROPE_THETA) ** (-np.arange(half, dtype=np.float32) * np.float32(2.0) / np.float32(ROPE_DIM))).astype(np.float32)
    col = np.arange(KV_WIDTH) % HEAD_DIM
    freq = np.where(col < ROPE_DIM, inv[col % half], 0.0).astype(np.float32)
    sign = np.where(col < half, -1.0, np.where(col < ROPE_DIM, 1.0, 0.0)).astype(np.float32)
    return freq[None, :], sign[None, :]


def _rope_tables(pos_col):
    L = pos_col.shape[0]
    tm = _tile(L)
    freq, sign = _rope_rows()

    def body(pos_ref, freq_ref, sign_ref, cos_ref, sin_ref):
        ang = pos_ref[...].astype(_F32) * freq_ref[...]
        cos_ref[...] = jnp.cos(ang)
        sin_ref[...] = jnp.sin(ang) * sign_ref[...]

    return _call(body, (L // tm,),
                 [_rows(tm, 1), _whole((1, KV_WIDTH)), _whole((1, KV_WIDTH))],
                 [_rows(tm, KV_WIDTH), _rows(tm, KV_WIDTH)],
                 [_sds((L, KV_WIDTH), _F32)] * 2, "rope_tables")(pos_col, jnp.asarray(freq), jnp.asarray(sign))


def _widen(t, width):
    return t if width == KV_WIDTH else jnp.concatenate([t] * (width // KV_WIDTH), axis=1)


def _rope_partner(t):
    w = t.shape[1]
    in_head = _iota((1, w), 1) & (HEAD_DIM - 1)
    second = jnp.where(in_head < ROPE_DIM, pltpu.roll(t, ROPE_DIM // 2, 1), 0.0)
    return jnp.where(in_head < ROPE_DIM // 2, pltpu.roll(t, w - ROPE_DIM // 2, 1), second)


def _rope_apply(t, cos_t, sin_t):
    w = t.shape[1]
    return t * _widen(cos_t, w) + _rope_partner(t) * _widen(sin_t, w)


def _rope_transpose(dt, cos_t, sin_t):
    w = dt.shape[1]
    return dt * _widen(cos_t, w) + _rope_partner(dt * _widen(sin_t, w))


def _in_proj(x, g_pre_mix, w_in, cos_t, sin_t):
    L = x.shape[0]
    tm = _tile(L)

    def body(x_ref, g_ref, w_ref, cos_ref, sin_ref, hn_ref, u_ref, q_ref, k_ref, v_ref):
        hn, _ = _rms_fwd(x_ref[...], g_ref[...])
        hn = hn.astype(_BF16)
        hn_ref[...] = hn
        proj = _dot(hn, w_ref[...], _NT)
        u_ref[...] = proj[:, :SSM_WIDTH]
        q = proj[:, SSM_WIDTH:SSM_WIDTH + ATTN_WIDTH]
        k = proj[:, SSM_WIDTH + ATTN_WIDTH:SSM_WIDTH + ATTN_WIDTH + KV_WIDTH]
        cos_v, sin_v = cos_ref[...], sin_ref[...]
        q_ref[...] = _rope_apply(q, cos_v, sin_v).astype(_BF16)
        k_ref[...] = _rope_apply(k, cos_v, sin_v).astype(_BF16)
        v_ref[...] = proj[:, SSM_WIDTH + ATTN_WIDTH + KV_WIDTH:].astype(_BF16)

    return _call(body, (L // tm,),
                 [_rows(tm, D_MODEL), _whole((1, D_MODEL)), _whole((IN_WIDTH, D_MODEL)),
                  _rows(tm, KV_WIDTH), _rows(tm, KV_WIDTH)],
                 [_rows(tm, D_MODEL), _rows(tm, SSM_WIDTH), _rows(tm, ATTN_WIDTH),
                  _rows(tm, KV_WIDTH), _rows(tm, KV_WIDTH)],
                 [_sds((L, D_MODEL), _BF16), _sds((L, SSM_WIDTH), _F32), _sds((L, ATTN_WIDTH), _BF16),
                  _sds((L, KV_WIDTH), _BF16), _sds((L, KV_WIDTH), _BF16)],
                 "in_proj")(x, g_pre_mix, w_in, cos_t, sin_t)


def _s5_discretize(lam_re, lam_im, log_dt):
    lr = jnp.minimum(lam_re, -1e-4)
    li = lam_im
    dt = jnp.exp(log_dt)
    mag = jnp.exp(lr * dt)
    ar = mag * jnp.cos(li * dt)
    ai = mag * jnp.sin(li * dt)
    den = lr * lr + li * li
    fr = ((ar - 1.0) * lr + ai * li) / den
    fi = (ai * lr - (ar - 1.0) * li) / den
    return ar, ai, fr, fi


def _s5_bbar(lam_re, lam_im, log_dt, b_re, b_im):
    ar, ai, fr, fi = _s5_discretize(lam_re, lam_im, log_dt)
    return ar, ai, fr * b_re - fi * b_im, fr * b_im + fi * b_re


def _spread_masks():
    e16 = (_iota((SSM_GROUP, SSM_WIDTH), 1) & (SSM_GROUP - 1)) == _iota((SSM_GROUP, SSM_WIDTH), 0)
    e64 = (_iota((SSM_STATE, N_STATE), 1) & (SSM_STATE - 1)) == _iota((SSM_STATE, N_STATE), 0)
    mask_b = (_iota((N_STATE, SSM_WIDTH), 0) >> 6) == (_iota((N_STATE, SSM_WIDTH), 1) >> 4)
    mask_c = (_iota((SSM_WIDTH, N_STATE), 0) >> 4) == (_iota((SSM_WIDTH, N_STATE), 1) >> 6)
    return e16.astype(_F32), e64.astype(_F32), mask_b, mask_c


SUPER = 4
SB_STATE = N_STATE // SUPER
SB_WIDTH = SSM_WIDTH // SUPER


def _sb_state(k):
    return slice(SB_STATE * k, SB_STATE * (k + 1))


def _sb_width(k):
    return slice(SB_WIDTH * k, SB_WIDTH * (k + 1))


def _ssm_prep(lam_re_r, lam_im_r, ldt_r, lam_re_c, lam_im_c, ldt_c, b_re2, b_im2, c_re2, c_im2):
    def body(lrr, lir, ldr, lrc, lic, ldc, bre, bim, cre, cim, ar_ref, ai_ref, btr, bti, ctr, cti):
        ar, ai, _, _ = _s5_discretize(lrr[...], lir[...], ldr[...])
        ar_ref[...] = ar
        ai_ref[...] = ai
        _, _, bbr, bbi = _s5_bbar(lrc[...], lic[...], ldc[...], bre[...], bim[...])
        e16, e64, mask_b, mask_c = _spread_masks()

        def fold_b(bb):
            full = jnp.where(mask_b, _dot(bb, e16, _NN), 0.0)
            return sum(full[:, _sb_width(k)] for k in range(SUPER)).astype(_BF16)

        def fold_c(cc):
            full = jnp.where(mask_c, _dot(cc, e64, _NN), 0.0)
            return sum(full[_sb_width(k), :] for k in range(SUPER)).astype(_BF16)

        btr[...] = fold_b(bbr)
        bti[...] = fold_b(bbi)
        ctr[...] = fold_c(cre[...])
        cti[...] = fold_c(cim[...])

    row = (1, N_STATE)
    ins = [lam_re_r, lam_im_r, ldt_r, lam_re_c, lam_im_c, ldt_c, b_re2, b_im2, c_re2, c_im2]
    return _call(body, (1,), [_whole(a.shape) for a in ins],
                 [_whole(row), _whole(row), _whole((N_STATE, SB_WIDTH)), _whole((N_STATE, SB_WIDTH)),
                  _whole((SB_WIDTH, N_STATE)), _whole((SB_WIDTH, N_STATE))],
                 [_sds(row, _F32), _sds(row, _F32), _sds((N_STATE, SB_WIDTH), _BF16),
                  _sds((N_STATE, SB_WIDTH), _BF16), _sds((SB_WIDTH, N_STATE), _BF16),
                  _sds((SB_WIDTH, N_STATE), _BF16)], "ssm_prep")(*ins)


def _ssm_bu(u, bt_re, bt_im):
    L = u.shape[0]
    tm = _tile(L)

    def body(u_ref, br_ref, bi_ref, or_ref, oi_ref):
        for k in range(SUPER):
            ub = u_ref[:, _sb_width(k)].astype(_BF16)
            or_ref[:, _sb_state(k)] = _dot(ub, br_ref[_sb_state(k), :], _NT)
            oi_ref[:, _sb_state(k)] = _dot(ub, bi_ref[_sb_state(k), :], _NT)

    return _call(body, (L // tm,),
                 [_rows(tm, SSM_WIDTH), _whole((N_STATE, SB_WIDTH)), _whole((N_STATE, SB_WIDTH))],
                 [_rows(tm, N_STATE), _rows(tm, N_STATE)],
                 [_sds((L, N_STATE), _F32)] * 2, "ssm_bu")(u, bt_re, bt_im)


def _complex_power(ar, ai, n):
    def step(_, c):
        pr, pi = c
        return pr * ar - pi * ai, pr * ai + pi * ar
    return lax.fori_loop(0, n, step, (jnp.ones_like(ar), jnp.zeros_like(ai)))


def _chunk_carries(er, ei, pr, pi, reverse):
    rows = _iota(er.shape, 0)
    sr = jnp.zeros_like(pr)
    si = jnp.zeros_like(pi)
    out_r = jnp.zeros_like(er)
    out_i = jnp.zeros_like(ei)
    order = range(SCAN_CHUNKS - 1, 0, -1) if reverse else range(SCAN_CHUNKS - 1)
    for c in order:
        e_r = er[c:c + 1, :]
        e_i = ei[c:c + 1, :]
        sr, si = pr * sr - pi * si + e_r, pr * si + pi * sr + e_i
        nxt = c - 1 if reverse else c + 1
        out_r = jnp.where(rows == nxt, sr, out_r)
        out_i = jnp.where(rows == nxt, si, out_i)
    return out_r, out_i


def _scan_fwd(b_re, b_im, a_re, a_im):
    T = b_re.shape[0]
    W = SCAN_COLS
    blk = pl.BlockSpec((T, SCAN_CHUNKS, W), lambda j: (0, 0, j))
    vec = pl.BlockSpec((1, W), lambda j: (0, j))

    def body(br_ref, bi_ref, ar_ref, ai_ref, xr_ref, xi_ref):
        ar, ai = ar_ref[...], ai_ref[...]
        ar8 = jnp.broadcast_to(ar, (SCAN_CHUNKS, W))
        ai8 = jnp.broadcast_to(ai, (SCAN_CHUNKS, W))

        def local(t, c):
            cr, ci = c
            return ar8 * cr - ai8 * ci + br_ref[t], ar8 * ci + ai8 * cr + bi_ref[t]

        zero = jnp.zeros((SCAN_CHUNKS, W), _F32)
        er, ei = lax.fori_loop(0, T, local, (zero, zero))
        pr, pi = _complex_power(ar, ai, T)
        sr, si = _chunk_carries(er, ei, pr, pi, reverse=False)

        def final(t, c):
            nr, ni = local(t, c)
            xr_ref[t] = nr
            xi_ref[t] = ni
            return nr, ni

        lax.fori_loop(0, T, final, (sr, si))

    shape = _sds(b_re.shape, _F32)
    return _call(body, (N_STATE // W,), [blk, blk, vec, vec], [blk, blk], [shape, shape],
                 "scan_fwd")(b_re, b_im, a_re, a_im)


def _scan_bwd(dx_re, dx_im, x_re, x_im, a_re, a_im, tokens=()):
    T = dx_re.shape[0]
    W = SCAN_COLS
    blk = pl.BlockSpec((T, SCAN_CHUNKS, W), lambda j: (0, 0, j))
    vec = pl.BlockSpec((1, W), lambda j: (0, j))

    def body(dr_ref, di_ref, xr_ref, xi_ref, ar_ref, ai_ref, lr_ref, li_ref, dar_ref, dai_ref):
        ar, ai = ar_ref[...], ai_ref[...]
        ar8 = jnp.broadcast_to(ar, (SCAN_CHUNKS, W))
        ai8 = jnp.broadcast_to(ai, (SCAN_CHUNKS, W))

        def local(t, c):
            cr, ci = c
            return ar8 * cr + ai8 * ci + dr_ref[t], ar8 * ci - ai8 * cr + di_ref[t]

        zero = jnp.zeros((SCAN_CHUNKS, W), _F32)
        er, ei = lax.fori_loop(0, T, lambda k, c: local(T - 1 - k, c), (zero, zero))
        pr, pi = _complex_power(ar, -ai, T)
        sr, si = _chunk_carries(er, ei, pr, pi, reverse=True)

        def grad_a(acc, nr, ni, xpr, xpi):
            return acc[0] + nr * xpr + ni * xpi, acc[1] + ni * xpr - nr * xpi

        def final(k, c):
            t = T - 1 - k
            nr, ni = local(t, c[:2])
            lr_ref[t] = nr
            li_ref[t] = ni
            gr, gi = grad_a(c[2:], nr, ni, xr_ref[t - 1], xi_ref[t - 1])
            return nr, ni, gr, gi

        cr, ci, gr, gi = lax.fori_loop(0, T - 1, final, (sr, si, zero, zero))
        nr, ni = local(0, (cr, ci))
        lr_ref[0] = nr
        li_ref[0] = ni
        first = _iota((SCAN_CHUNKS, W), 0) == 0
        xpr = jnp.where(first, 0.0, pltpu.roll(xr_ref[T - 1], 1, 0))
        xpi = jnp.where(first, 0.0, pltpu.roll(xi_ref[T - 1], 1, 0))
        gr, gi = grad_a((gr, gi), nr, ni, xpr, xpi)
        dar_ref[...] = jnp.sum(gr, axis=0, keepdims=True)
        dai_ref[...] = jnp.sum(gi, axis=0, keepdims=True)

    shape = _sds(dx_re.shape, _F32)
    row = _sds((1, N_STATE), _F32)
    return _call(body, (N_STATE // W,), [blk, blk, blk, blk, vec, vec], [blk, blk, vec, vec],
                 [shape, shape, row, row], "scan_bwd", tokens=tokens)(dx_re, dx_im, x_re, x_im, a_re, a_im)


_GELU_K = math.sqrt(2.0 / math.pi)
_GELU_C = 0.044715


def _gelu(y):
    return 0.5 * y * (1.0 + jnp.tanh(_GELU_K * (y + _GELU_C * y * y * y)))


def _gelu_grad(y):
    t = jnp.tanh(_GELU_K * (y + _GELU_C * y * y * y))
    return 0.5 * (1.0 + t) + 0.5 * y * (1.0 - t * t) * _GELU_K * (1.0 + 3.0 * _GELU_C * y * y)


def _ssm_out(x_re, x_im, u, ct_re, ct_im, d_row, w_glu, b_glu, g_ssm):
    L = u.shape[0]
    tm = _tile(L)

    def body(xr_ref, xi_ref, u_ref, cr_ref, ci_ref, d_ref, w_ref, b_ref, g_ref, y_ref, z_ref, n_ref):
        cx = [_dot(xr_ref[:, _sb_state(k)], cr_ref[:, _sb_state(k)], _NT)
              - _dot(xi_ref[:, _sb_state(k)], ci_ref[:, _sb_state(k)], _NT) for k in range(SUPER)]
        y = jnp.concatenate(cx, axis=1) + d_ref[...] * u_ref[...]
        y_ref[...] = y
        z = _dot(_gelu(y), w_ref[...], _NT) + b_ref[...]
        z_ref[...] = z
        out = z[:, :SSM_WIDTH] * jax.nn.sigmoid(z[:, SSM_WIDTH:])
        n, _ = _rms_fwd(out, g_ref[...])
        n_ref[...] = n.astype(_BF16)

    return _call(body, (L // tm,),
                 [_rows(tm, N_STATE), _rows(tm, N_STATE), _rows(tm, SSM_WIDTH),
                  _whole((SB_WIDTH, N_STATE)), _whole((SB_WIDTH, N_STATE)), _whole((1, SSM_WIDTH)),
                  _whole((2 * SSM_WIDTH, SSM_WIDTH)), _whole((1, 2 * SSM_WIDTH)), _whole((1, SSM_WIDTH))],
                 [_rows(tm, SSM_WIDTH), _rows(tm, 2 * SSM_WIDTH), _rows(tm, SSM_WIDTH)],
                 [_sds((L, SSM_WIDTH), _F32), _sds((L, 2 * SSM_WIDTH), _F32), _sds((L, SSM_WIDTH), _BF16)],
                 "ssm_out")(x_re, x_im, u, ct_re, ct_im, d_row, w_glu, b_glu, g_ssm)


def _ssm_out_bwd(dn, y, z, u, ct_re, ct_im, d_row, w_glu, g_ssm):
    L = u.shape[0]
    tm = _tile(L)

    def body(dn_ref, y_ref, z_ref, u_ref, cr_ref, ci_ref, d_ref, w_ref, g_ref,
             gy_ref, dz_ref, dy_ref, dud_ref, dxr_ref, dxi_ref, dg_ref, db_ref, dd_ref):
        first = pl.program_id(0) == 0
        z = z_ref[...]
        z1, z2 = z[:, :SSM_WIDTH], z[:, SSM_WIDTH:]
        sig = jax.nn.sigmoid(z2)
        out = z1 * sig
        g = g_ref[...]
        _, r = _rms_fwd(out, g)
        dout, dg = _rms_bwd(dn_ref[...], out, g, r)
        _accumulate(dg_ref, dg, first)
        dz = jnp.concatenate([dout * sig, dout * z1 * sig * (1.0 - sig)], axis=1)
        _accumulate(db_ref, jnp.sum(dz, axis=0, keepdims=True), first)
        dzb = dz.astype(_BF16)
        dz_ref[...] = dzb
        y = y_ref[...]
        gy_ref[...] = _gelu(y).astype(_BF16)
        dy = _dot(dzb, w_ref[...], _NN) * _gelu_grad(y)
        u = u_ref[...]
        _accumulate(dd_ref, jnp.sum(dy * u, axis=0, keepdims=True), first)
        dud_ref[...] = d_ref[...] * dy
        dyb = dy.astype(_BF16)
        dy_ref[...] = dyb
        for k in range(SUPER):
            dxr_ref[:, _sb_state(k)] = _dot(dyb[:, _sb_width(k)], cr_ref[:, _sb_state(k)], _NN)
            dxi_ref[:, _sb_state(k)] = -_dot(dyb[:, _sb_width(k)], ci_ref[:, _sb_state(k)], _NN)

    row = _whole((1, SSM_WIDTH))
    return _call(body, (L // tm,),
                 [_rows(tm, SSM_WIDTH), _rows(tm, SSM_WIDTH), _rows(tm, 2 * SSM_WIDTH), _rows(tm, SSM_WIDTH),
                  _whole((SB_WIDTH, N_STATE)), _whole((SB_WIDTH, N_STATE)), row,
                  _whole((2 * SSM_WIDTH, SSM_WIDTH)), row],
                 [_rows(tm, SSM_WIDTH), _rows(tm, 2 * SSM_WIDTH), _rows(tm, SSM_WIDTH), _rows(tm, SSM_WIDTH),
                  _rows(tm, N_STATE), _rows(tm, N_STATE), row, _whole((1, 2 * SSM_WIDTH)), row],
                 [_sds((L, SSM_WIDTH), _BF16), _sds((L, 2 * SSM_WIDTH), _BF16), _sds((L, SSM_WIDTH), _BF16),
                  _sds((L, SSM_WIDTH), _F32), _sds((L, N_STATE), _F32), _sds((L, N_STATE), _F32),
                  _sds((1, SSM_WIDTH), _F32), _sds((1, 2 * SSM_WIDTH), _F32), _sds((1, SSM_WIDTH), _F32)],
                 "ssm_out_bwd")(dn, y, z, u, ct_re, ct_im, d_row, w_glu, g_ssm)


def _ssm_du(lam_re, lam_im, bt_re, bt_im, dud):
    L = dud.shape[0]
    tm = _tile(L)

    def body(lr_ref, li_ref, br_ref, bi_ref, dud_ref, du_ref):
        for k in range(SUPER):
            du_ref[:, _sb_width(k)] = (_dot(lr_ref[:, _sb_state(k)], br_ref[_sb_state(k), :], _NN)
                                       + _dot(li_ref[:, _sb_state(k)], bi_ref[_sb_state(k), :], _NN)
                                       + dud_ref[:, _sb_width(k)])

    return _call(body, (L // tm,),
                 [_rows(tm, N_STATE), _rows(tm, N_STATE), _whole((N_STATE, SB_WIDTH)),
                  _whole((N_STATE, SB_WIDTH)), _rows(tm, SSM_WIDTH)],
                 _rows(tm, SSM_WIDTH), _sds((L, SSM_WIDTH), _F32), "ssm_du")(lam_re, lam_im, bt_re, bt_im, dud)


def _ssm_weight_grads(dy, x_re, x_im, lam_re, lam_im, u):
    L = u.shape[0]

    def body(dy_ref, xr_ref, xi_ref, lr_ref, li_ref, u_ref, dcr_ref, dci_ref, dbr_ref, dbi_ref):
        dyb = dy_ref[...]
        ub = u_ref[...].astype(_BF16)
        dcr_ref[...] = _dot(dyb, xr_ref[...], _TN)
        dci_ref[...] = _dot(dyb, xi_ref[...], _TN)
        dbr_ref[...] = _dot(lr_ref[...], ub, _TN)
        dbi_ref[...] = _dot(li_ref[...], ub, _TN)

    width = pl.BlockSpec((L, SB_WIDTH), lambda k: (0, k))
    state = pl.BlockSpec((L, SB_STATE), lambda k: (0, k))
    out_c = pl.BlockSpec((SB_WIDTH, SB_STATE), lambda k: (0, k))
    out_b = pl.BlockSpec((SB_STATE, SB_WIDTH), lambda k: (k, 0))
    return _call(body, (SUPER,), [width, state, state, state, state, width], [out_c, out_c, out_b, out_b],
                 [_sds((SB_WIDTH, N_STATE), _F32)] * 2 + [_sds((N_STATE, SB_WIDTH), _F32)] * 2,
                 "ssm_weight_grads")(dy, x_re, x_im, lam_re, lam_im, u)


def _ssm_param_bwd(da_re_c, da_im_c, dbt_re, dbt_im, dct_re, dct_im,
                   lam_re_c, lam_im_c, ldt_c, b_re2, b_im2):
    def body(dar, dai, dbr, dbi, dcr, dci, lrc, lic, ldc, bre, bim,
             glr, gli, gdt, gbr, gbi, gcr, gci):
        own_b = ((_iota((N_STATE, SB_WIDTH), 0) >> 6) & 7) == (_iota((N_STATE, SB_WIDTH), 1) >> 4)
        own_c = (_iota((SB_WIDTH, SB_STATE), 0) >> 4) == (_iota((SB_WIDTH, SB_STATE), 1) >> 6)

        def fold_b(ref):
            t = jnp.where(own_b, ref[...], 0.0)
            for shift in (64, 32, 16):
                t = t + pltpu.roll(t, shift, 1)
            return t[:, :SSM_GROUP]

        def fold_c(ref, k):
            t = jnp.where(own_c, ref[:, _sb_state(k)], 0.0)
            t = sum(t[:, 128 * i:128 * (i + 1)] for i in range(SB_STATE // 128))
            return (t + pltpu.roll(t, SSM_STATE, 1))[:, :SSM_STATE]

        dbbr = fold_b(dbr)
        dbbi = fold_b(dbi)
        for k in range(SUPER):
            gcr[_sb_width(k), :] = fold_c(dcr, k)
            gci[_sb_width(k), :] = -fold_c(dci, k)
        _, vjp = jax.vjp(_s5_bbar, lrc[...], lic[...], ldc[...], bre[...], bim[...])
        d_lr, d_li, d_dt, d_br, d_bi = vjp((dar[...], dai[...], dbbr, dbbi))
        gbr[...] = d_br
        gbi[...] = d_bi
        groups = ((_iota((SSM_GROUPS, N_STATE), 1) >> 6) == _iota((SSM_GROUPS, N_STATE), 0)).astype(_F32)
        in_group = ((_iota((N_STATE, SSM_STATE), 0) & (SSM_STATE - 1)) == _iota((N_STATE, SSM_STATE), 1)).astype(_F32)
        glr[...] = _dot_exact(groups, d_lr * in_group, _NN)
        gli[...] = _dot_exact(groups, d_li * in_group, _NN)
        groups_t = ((_iota((N_STATE, SSM_GROUPS), 0) >> 6) == _iota((N_STATE, SSM_GROUPS), 1)).astype(_F32)
        gdt[...] = _dot_exact(jnp.broadcast_to(d_dt, (N_STATE, 128)), groups_t, _TN)[0:1]

    ins = [da_re_c, da_im_c, dbt_re, dbt_im, dct_re, dct_im, lam_re_c, lam_im_c, ldt_c, b_re2, b_im2]
    outs = [(SSM_GROUPS, SSM_STATE), (SSM_GROUPS, SSM_STATE), (1, SSM_GROUPS), (N_STATE, SSM_GROUP),
            (N_STATE, SSM_GROUP), (SSM_WIDTH, SSM_STATE), (SSM_WIDTH, SSM_STATE)]
    return _call(body, (1,), [_whole(a.shape) for a in ins], [_whole(s) for s in outs],
                 [_sds(s, _F32) for s in outs], "ssm_param_bwd")(*ins)


def _head_spread(j):
    r = _iota((KV_WIDTH, 256), 0)
    c = _iota((KV_WIDTH, 256), 1)
    return (r == HEAD_DIM * j + (c & (HEAD_DIM - 1))).astype(_BF16)


STACK = Q_PER_KV * BLOCK


def _stack_heads(t):
    lane_head = _iota((1, 256), 1) >> 6
    return jnp.concatenate([jnp.where(lane_head == g, t, jnp.zeros_like(t)) for g in range(Q_PER_KV)], axis=0)


def _unstack_heads(t):
    lane_head = _iota((1, 256), 1) >> 6
    return sum(jnp.where(lane_head == g, t[BLOCK * g:BLOCK * (g + 1)], 0.0) for g in range(Q_PER_KV))


def _stacked_sinks(sink_ref, j):
    block = _iota((STACK, 1), 0) >> 7
    col = jnp.full((STACK, 1), sink_ref[Q_PER_KV * j], _F32)
    for g in range(1, Q_PER_KV):
        col = jnp.where(block == g, sink_ref[Q_PER_KV * j + g], col)
    return col


def _fold_heads(t, j):
    t = t[:, :KV_WIDTH] + t[:, KV_WIDTH:]
    t = t + pltpu.roll(t, HEAD_DIM, 1)
    return jnp.where((_iota((1, KV_WIDTH), 1) >> 6) == j, t, 0.0)


def _attn_scores(q_stacked, kt, blk, sink):
    s = _dot(q_stacked, kt, _NT) * (HEAD_DIM ** -0.5)
    qi = _iota((STACK, 2 * BLOCK), 0) & (BLOCK - 1)
    kj = _iota((STACK, 2 * BLOCK), 1)
    rel = qi + BLOCK - kj
    valid = (rel >= 0) & (rel < BLOCK) & (blk * BLOCK - BLOCK + kj >= 0)
    s = jnp.where(valid, s, MASK_VALUE)
    m = jnp.maximum(jnp.max(s, axis=-1, keepdims=True), sink)
    p = jnp.exp(s - m)
    e_sink = jnp.exp(sink - m)
    den = jnp.sum(p, axis=-1, keepdims=True) + e_sink
    return p / den, e_sink / den


def _attn_specs():
    prev = lambda i: (jnp.maximum(i - 1, 0), 0)
    cur = lambda i: (i, 0)
    kv = [pl.BlockSpec((BLOCK, KV_WIDTH), prev), pl.BlockSpec((BLOCK, KV_WIDTH), cur)]
    return [pl.BlockSpec((BLOCK, ATTN_WIDTH), cur)] + kv + kv


def _attn_fwd(q, k, v, sinks, g_attn):
    L = q.shape[0]

    def body(q_ref, kp_ref, kc_ref, vp_ref, vc_ref, sink_ref, g_ref, o_ref, n_ref):
        blk = pl.program_id(0)
        kwin = jnp.concatenate([kp_ref[...], kc_ref[...]], axis=0)
        vwin = jnp.concatenate([vp_ref[...], vc_ref[...]], axis=0)
        halves = []
        for j in range(N_KV_HEADS):
            spread = _head_spread(j)
            kt = _dot(kwin, spread, _NN).astype(_BF16)
            vt = _dot(vwin, spread, _NN).astype(_BF16)
            qs = _stack_heads(q_ref[:, 256 * j:256 * (j + 1)])
            p, _ = _attn_scores(qs, kt, blk, _stacked_sinks(sink_ref, j))
            halves.append(_unstack_heads(_dot(p, vt, _NN)))
        o = jnp.concatenate(halves, axis=1)
        o_ref[...] = o
        n, _ = _rms_fwd(o, g_ref[...])
        n_ref[...] = n.astype(_BF16)

    cur = lambda i: (i, 0)
    return _call(body, (L // BLOCK,),
                 _attn_specs() + [pl.BlockSpec(memory_space=pltpu.SMEM), _whole((1, ATTN_WIDTH))],
                 [pl.BlockSpec((BLOCK, ATTN_WIDTH), cur)] * 2,
                 [_sds((L, ATTN_WIDTH), _F32), _sds((L, ATTN_WIDTH), _BF16)],
                 "attn_fwd")(q, k, k, v, v, sinks, g_attn)


def _attn_bwd(q, k, v, o, dn, sinks, g_attn):
    L = q.shape[0]

    def body(q_ref, kp_ref, kc_ref, vp_ref, vc_ref, o_ref, dn_ref, sink_ref, g_ref,
             dq_ref, dk_ref, dv_ref, dsink_ref, dg_ref):
        blk = pl.program_id(0)
        first = blk == 0

        @pl.when(first)
        def _():
            dk_ref[...] = jnp.zeros_like(dk_ref)
            dv_ref[...] = jnp.zeros_like(dv_ref)
            dsink_ref[...] = jnp.zeros_like(dsink_ref)

        o = o_ref[...]
        g = g_ref[...]
        _, r = _rms_fwd(o, g)
        do, dg = _rms_bwd(dn_ref[...], o, g, r)
        _accumulate(dg_ref, dg, first)
        kwin = jnp.concatenate([kp_ref[...], kc_ref[...]], axis=0)
        vwin = jnp.concatenate([vp_ref[...], vc_ref[...]], axis=0)
        lane = _iota((1, 128), 1)
        dsink = jnp.zeros((1, 128), _F32)
        dkwin = jnp.zeros((2 * BLOCK, KV_WIDTH), _F32)
        dvwin = jnp.zeros((2 * BLOCK, KV_WIDTH), _F32)
        dq_halves = []
        for j in range(N_KV_HEADS):
            spread = _head_spread(j)
            kt = _dot(kwin, spread, _NN).astype(_BF16)
            vt = _dot(vwin, spread, _NN).astype(_BF16)
            qs = _stack_heads(q_ref[:, 256 * j:256 * (j + 1)])
            dos = _stack_heads(do[:, 256 * j:256 * (j + 1)]).astype(_BF16)
            p, p_sink = _attn_scores(qs, kt, blk, _stacked_sinks(sink_ref, j))
            dp = _dot(dos, vt, _NT)
            delta = jnp.sum(p * dp, axis=-1, keepdims=True)
            ds = (p * (dp - delta) * (HEAD_DIM ** -0.5)).astype(_BF16)
            sink_term = p_sink * delta
            for g in range(Q_PER_KV):
                head_sum = jnp.sum(sink_term[BLOCK * g:BLOCK * (g + 1)], axis=0, keepdims=True)
                dsink = dsink - jnp.where(lane == Q_PER_KV * j + g, head_sum, 0.0)
            dvwin = dvwin + _fold_heads(_dot(p, dos, _TN), j)
            dkwin = dkwin + _fold_heads(_dot(ds, qs, _TN), j)
            dq_halves.append(_unstack_heads(_dot(ds, kt, _NN)))
        dq_ref[...] = jnp.concatenate(dq_halves, axis=1)
        dsink_ref[...] += dsink
        prev = pl.ds(pl.multiple_of(jnp.maximum(blk - 1, 0) * BLOCK, BLOCK), BLOCK)
        cur = pl.ds(pl.multiple_of(blk * BLOCK, BLOCK), BLOCK)
        dk_ref[prev, :] += dkwin[:BLOCK]
        dk_ref[cur, :] += dkwin[BLOCK:]
        dv_ref[prev, :] += dvwin[:BLOCK]
        dv_ref[cur, :] += dvwin[BLOCK:]

    cur = lambda i: (i, 0)
    blk_q = pl.BlockSpec((BLOCK, ATTN_WIDTH), cur)
    return _call(body, (L // BLOCK,),
                 _attn_specs() + [blk_q, blk_q, pl.BlockSpec(memory_space=pltpu.SMEM), _whole((1, ATTN_WIDTH))],
                 [blk_q, _whole((L, KV_WIDTH)), _whole((L, KV_WIDTH)), _whole((1, 128)), _whole((1, ATTN_WIDTH))],
                 [_sds((L, ATTN_WIDTH), _F32), _sds((L, KV_WIDTH), _F32), _sds((L, KV_WIDTH), _F32),
                  _sds((1, 128), _F32), _sds((1, ATTN_WIDTH), _F32)],
                 "attn_bwd")(q, k, k, v, v, o, dn, sinks, g_attn)


def _out_proj(n_ssm, n_attn, x, w_out, g_post_mix, g_pre_ffn):
    L = x.shape[0]
    tm = _tile(L)

    def body(ns_ref, na_ref, x_ref, w_ref, g1_ref, g2_ref, merged_ref, mo_ref, h1_ref, hn2_ref):
        merged = jnp.concatenate([ns_ref[...], na_ref[...]], axis=1)
        merged_ref[...] = merged
        mo = _dot(merged, w_ref[...], _NN)
        mo_ref[...] = mo
        n, _ = _rms_fwd(mo, g1_ref[...])
        h1 = x_ref[...] + n
        h1_ref[...] = h1
        hn2, _ = _rms_fwd(h1, g2_ref[...])
        hn2_ref[...] = hn2.astype(_BF16)

    row = _whole((1, D_MODEL))
    return _call(body, (L // tm,),
                 [_rows(tm, SSM_WIDTH), _rows(tm, ATTN_WIDTH), _rows(tm, D_MODEL), _whole((D_MODEL, D_MODEL)), row, row],
                 [_rows(tm, D_MODEL)] * 4,
                 [_sds((L, D_MODEL), _BF16), _sds((L, D_MODEL), _F32), _sds((L, D_MODEL), _F32), _sds((L, D_MODEL), _BF16)],
                 "out_proj")(n_ssm, n_attn, x, w_out, g_post_mix, g_pre_ffn)


def _ffn(hn2, h1, target, w_gate_up, w_down, g_pre_ffn, g_post_ffn):
    L = h1.shape[0]
    tm = _tile(L)
    half = D_FF // 2

    def body(hn2_ref, h1_ref, tgt_ref, wgu_hbm, wd_hbm, g2_ref, g3_ref,
             act_ref, dgu_ref, dff_ref, dh1_ref, loss_ref, dg3_ref, dg2_ref,
             wgu, wd, gu, sem):
        first = pl.program_id(0) == 0

        @pl.when(first)
        def _():
            c1 = pltpu.make_async_copy(wgu_hbm, wgu, sem.at[0])
            c2 = pltpu.make_async_copy(wd_hbm, wd, sem.at[1])
            c1.start()
            c2.start()
            c1.wait()
            c2.wait()

        hn2 = hn2_ref[...]
        ff = jnp.zeros((tm, D_MODEL), _F32)
        for c in range(2):
            gate = _dot(hn2, wgu[half * c:half * (c + 1), :], _NT)
            up = _dot(hn2, wgu[D_FF + half * c:D_FF + half * (c + 1), :], _NT)
            gu[:, half * c:half * (c + 1)] = gate
            gu[:, D_FF + half * c:D_FF + half * (c + 1)] = up
            act = (gate * jax.nn.sigmoid(gate) * up).astype(_BF16)
            act_ref[:, half * c:half * (c + 1)] = act
            ff = ff + _dot(act, wd[half * c:half * (c + 1), :], _NN)
        g3 = g3_ref[...]
        n, r = _rms_fwd(ff, g3)
        h1 = h1_ref[...]
        err = h1 + n - tgt_ref[...]
        loss = 0.5 * jnp.sum(jnp.mean(err * err, axis=-1, keepdims=True), axis=0, keepdims=True)
        _accumulate(loss_ref, jnp.broadcast_to(loss, (1, 128)), first)
        dh2 = err * (1.0 / D_MODEL)
        dff, dg3 = _rms_bwd(dh2, ff, g3, r)
        _accumulate(dg3_ref, dg3, first)
        dffb = dff.astype(_BF16)
        dff_ref[...] = dffb
        dhn2 = jnp.zeros((tm, D_MODEL), _F32)
        for c in range(2):
            dact = _dot(dffb, wd[half * c:half * (c + 1), :], _NT)
            gate = gu[:, half * c:half * (c + 1)]
            up = gu[:, D_FF + half * c:D_FF + half * (c + 1)]
            sig = jax.nn.sigmoid(gate)
            silu = gate * sig
            dgate = (dact * up * (sig + silu * (1.0 - sig))).astype(_BF16)
            dup = (dact * silu).astype(_BF16)
            dgu_ref[:, half * c:half * (c + 1)] = dgate
            dgu_ref[:, D_FF + half * c:D_FF + half * (c + 1)] = dup
            dhn2 = dhn2 + _dot(dgate, wgu[half * c:half * (c + 1), :], _NN)
            dhn2 = dhn2 + _dot(dup, wgu[D_FF + half * c:D_FF + half * (c + 1), :], _NN)
        g2 = g2_ref[...]
        _, r2 = _rms_fwd(h1, g2)
        dh1, dg2 = _rms_bwd(dhn2, h1, g2, r2)
        _accumulate(dg2_ref, dg2, first)
        dh1_ref[...] = dh2 + dh1

    row = _whole((1, D_MODEL))
    anyspace = pl.BlockSpec(memory_space=pl.ANY)
    return _call(body, (L // tm,),
                 [_rows(tm, D_MODEL), _rows(tm, D_MODEL), _rows(tm, D_MODEL), anyspace, anyspace, row, row],
                 [_rows(tm, D_FF), _rows(tm, 2 * D_FF), _rows(tm, D_MODEL), _rows(tm, D_MODEL),
                  _whole((1, 128)), row, row],
                 [_sds((L, D_FF), _BF16), _sds((L, 2 * D_FF), _BF16), _sds((L, D_MODEL), _BF16),
                  _sds((L, D_MODEL), _F32), _sds((1, 128), _F32), _sds((1, D_MODEL), _F32), _sds((1, D_MODEL), _F32)],
                 "ffn",
                 scratch=[pltpu.VMEM((2 * D_FF, D_MODEL), _BF16), pltpu.VMEM((D_FF, D_MODEL), _BF16),
                          pltpu.VMEM((tm, 2 * D_FF), _F32), pltpu.SemaphoreType.DMA((2,))],
                 )(hn2, h1, target, w_gate_up, w_down, g_pre_ffn, g_post_ffn)


def _out_proj_bwd(dh1, mo, w_out, g_post_mix, tokens=()):
    L = dh1.shape[0]
    tm = _tile(L)

    def body(dh1_ref, mo_ref, w_ref, g_ref, dmo_ref, dns_ref, dna_ref, dg_ref):
        first = pl.program_id(0) == 0
        mo = mo_ref[...]
        g = g_ref[...]
        _, r = _rms_fwd(mo, g)
        dmo, dg = _rms_bwd(dh1_ref[...], mo, g, r)
        _accumulate(dg_ref, dg, first)
        dmob = dmo.astype(_BF16)
        dmo_ref[...] = dmob
        dmerged = _dot(dmob, w_ref[...], _NT)
        dns_ref[...] = dmerged[:, :SSM_WIDTH]
        dna_ref[...] = dmerged[:, SSM_WIDTH:]

    row = _whole((1, D_MODEL))
    return _call(body, (L // tm,),
                 [_rows(tm, D_MODEL), _rows(tm, D_MODEL), _whole((D_MODEL, D_MODEL)), row],
                 [_rows(tm, D_MODEL), _rows(tm, SSM_WIDTH), _rows(tm, ATTN_WIDTH), row],
                 [_sds((L, D_MODEL), _BF16), _sds((L, SSM_WIDTH), _F32), _sds((L, ATTN_WIDTH), _F32),
                  _sds((1, D_MODEL), _F32)],
                 "out_proj_bwd", tokens=tokens)(dh1, mo, w_out, g_post_mix)


def _in_proj_bwd(du, dq, dk, dv, cos_t, sin_t, x, dh1, g_pre_mix, w_in, tokens=()):
    L = x.shape[0]
    tm = _tile(L)

    def body(du_ref, dq_ref, dk_ref, dv_ref, cos_ref, sin_ref, x_ref, dh1_ref, g_ref, w_ref,
             dproj_ref, dx_ref, dg_ref):
        first = pl.program_id(0) == 0
        cos_v, sin_v = cos_ref[...], sin_ref[...]
        dproj = jnp.concatenate([du_ref[...], _rope_transpose(dq_ref[...], cos_v, sin_v),
                                 _rope_transpose(dk_ref[...], cos_v, sin_v), dv_ref[...]], axis=1).astype(_BF16)
        dproj_ref[...] = dproj
        dhn = _dot(dproj, w_ref[...], _NN)
        x = x_ref[...]
        g = g_ref[...]
        _, r = _rms_fwd(x, g)
        dx, dg = _rms_bwd(dhn, x, g, r)
        _accumulate(dg_ref, dg, first)
        dx_ref[...] = dh1_ref[...] + dx

    row = _whole((1, D_MODEL))
    return _call(body, (L // tm,),
                 [_rows(tm, SSM_WIDTH), _rows(tm, ATTN_WIDTH), _rows(tm, KV_WIDTH), _rows(tm, KV_WIDTH),
                  _rows(tm, KV_WIDTH), _rows(tm, KV_WIDTH), _rows(tm, D_MODEL), _rows(tm, D_MODEL), row,
                  _whole((IN_WIDTH, D_MODEL))],
                 [_rows(tm, IN_WIDTH), _rows(tm, D_MODEL), row],
                 [_sds((L, IN_WIDTH), _BF16), _sds((L, D_MODEL), _F32), _sds((1, D_MODEL), _F32)],
                 "in_proj_bwd", tokens=tokens)(du, dq, dk, dv, cos_t, sin_t, x, dh1, g_pre_mix, w_in)


def _matmul_tn(a, b, out_dtype, name, scale=1.0):
    K, M = a.shape
    N = b.shape[1]
    tm = next(t for t in (512, 256, 128) if M % t == 0)
    tn = next(t for t in (512, 256, 128) if N % t == 0)

    def body(a_ref, b_ref, o_ref):
        acc = _dot(a_ref[...], b_ref[...], _TN)
        o_ref[...] = (acc if scale == 1.0 else acc * scale).astype(out_dtype)

    params = pltpu.CompilerParams(dimension_semantics=("arbitrary", "arbitrary"), vmem_limit_bytes=VMEM_LIMIT)
    return pl.pallas_call(body, grid=(M // tm, N // tn),
                          in_specs=[pl.BlockSpec((K, tm), lambda i, j: (0, i)),
                                    pl.BlockSpec((K, tn), lambda i, j: (0, j))],
                          out_specs=pl.BlockSpec((tm, tn), lambda i, j: (i, j)),
                          out_shape=_sds((M, N), out_dtype), compiler_params=params, name=name)(a, b)


def _to_chunked(a):
    L, n = a.shape
    return a.reshape(SCAN_CHUNKS, L // SCAN_CHUNKS, n).transpose(1, 0, 2).reshape(L, n)


def _from_chunked(a):
    L, n = a.shape
    return a.reshape(L // SCAN_CHUNKS, SCAN_CHUNKS, n).transpose(1, 0, 2).reshape(L, n)


def _local_step(x, pos, target, p, fetch, publish):
    L = x.shape[0]
    T = L // SCAN_CHUNKS
    cos_t, sin_t = _rope_tables(pos.reshape(L, 1))
    w_in, = fetch(("w_in",), None)
    hn, u, q, k, v = _in_proj(x, p["g_pre_mix"], w_in, cos_t, sin_t)

    lam_re_r = p["ssm_lambda_re"].reshape(1, N_STATE)
    lam_im_r = p["ssm_lambda_im"].reshape(1, N_STATE)
    ldt_r = jnp.broadcast_to(p["ssm_log_dt"].reshape(SSM_GROUPS, 1), (SSM_GROUPS, SSM_STATE)).reshape(1, N_STATE)
    lam_re_c, lam_im_c, ldt_c = (a.reshape(N_STATE, 1) for a in (lam_re_r, lam_im_r, ldt_r))
    b_re2 = p["ssm_b_re"].reshape(N_STATE, SSM_GROUP)
    b_im2 = p["ssm_b_im"].reshape(N_STATE, SSM_GROUP)
    c_re2 = p["ssm_c_re"].reshape(SSM_WIDTH, SSM_STATE)
    c_im2 = p["ssm_c_im"].reshape(SSM_WIDTH, SSM_STATE)
    d_row = p["ssm_d"].reshape(1, SSM_WIDTH)
    a_re, a_im, bt_re, bt_im, ct_re, ct_im = _ssm_prep(
        lam_re_r, lam_im_r, ldt_r, lam_re_c, lam_im_c, ldt_c, b_re2, b_im2, c_re2, c_im2)

    u_c = _to_chunked(u)
    bu_re, bu_im = _ssm_bu(u_c, bt_re, bt_im)
    x_re, x_im = _scan_fwd(bu_re.reshape(T, SCAN_CHUNKS, N_STATE), bu_im.reshape(T, SCAN_CHUNKS, N_STATE), a_re, a_im)
    w_glu, = fetch(("w_glu",), x_re)
    y, z, n_ssm_c = _ssm_out(x_re.reshape(L, N_STATE), x_im.reshape(L, N_STATE), u_c, ct_re, ct_im, d_row,
                             w_glu, p["b_glu"], p["g_ssm_out"])
    n_ssm = _from_chunked(n_ssm_c)

    sinks = p["attn_sinks"].reshape(N_Q_HEADS)
    o, n_attn = _attn_fwd(q, k, v, sinks, p["g_attn_out"])
    w_out, = fetch(("w_out",), n_attn)
    merged, mo, h1, hn2 = _out_proj(n_ssm, n_attn, x, w_out, p["g_post_mix"], p["g_pre_ffn"])
    w_gate_up, w_down = fetch(("w_gate_up", "w_down"), hn2)
    act, dgu, dff, dh1, loss, dg_post_ffn, dg_pre_ffn = _ffn(
        hn2, h1, target, w_gate_up, w_down, p["g_pre_ffn"], p["g_post_ffn"])
    grads = {"g_post_ffn": dg_post_ffn, "g_pre_ffn": dg_pre_ffn}
    tokens = publish({"w_down": _matmul_tn(act, dff, _BF16, "grad_w_down"),
                      "w_gate_up": _matmul_tn(dgu, hn2, _BF16, "grad_w_gate_up")})

    dmo, dn_ssm, dn_attn, grads["g_post_mix"] = _out_proj_bwd(dh1, mo, w_out, p["g_post_mix"], tokens)
    grad_w_out = _matmul_tn(merged, dmo, _BF16, "grad_w_out")

    dq, dk, dv, dsink, grads["g_attn_out"] = _attn_bwd(q, k, v, o, dn_attn, sinks, p["g_attn_out"])
    grads["attn_sinks"] = dsink

    gy, dz, dy, dud, dx_re, dx_im, grads["g_ssm_out"], grads["b_glu"], dd = _ssm_out_bwd(
        _to_chunked(dn_ssm), y, z, u_c, ct_re, ct_im, d_row, w_glu, p["g_ssm_out"])
    grads["ssm_d"] = dd.reshape(SSM_GROUPS, SSM_GROUP)
    tokens = publish({"w_out": grad_w_out, "w_glu": _matmul_tn(dz, gy, _BF16, "grad_w_glu")})
    lam_re, lam_im, da_re, da_im = _scan_bwd(dx_re.reshape(T, SCAN_CHUNKS, N_STATE), dx_im.reshape(T, SCAN_CHUNKS, N_STATE),
                                             x_re, x_im, a_re, a_im, tokens)
    lam_re = lam_re.reshape(L, N_STATE)
    lam_im = lam_im.reshape(L, N_STATE)
    dct_re, dct_im, dbt_re, dbt_im = _ssm_weight_grads(
        dy, x_re.reshape(L, N_STATE), x_im.reshape(L, N_STATE), lam_re, lam_im, u_c)
    g_lr, g_li, g_dt, g_br, g_bi, g_cr, g_ci = _ssm_param_bwd(
        da_re.reshape(N_STATE, 1), da_im.reshape(N_STATE, 1), dbt_re, dbt_im, dct_re, dct_im,
        lam_re_c, lam_im_c, ldt_c, b_re2, b_im2)
    grads.update(ssm_lambda_re=g_lr, ssm_lambda_im=g_li, ssm_log_dt=g_dt, ssm_b_re=g_br, ssm_b_im=g_bi,
                 ssm_c_re=g_cr, ssm_c_im=g_ci, loss=loss)
    publish(grads)

    du = _from_chunked(_ssm_du(lam_re, lam_im, bt_re, bt_im, dud))
    dproj, grad_x, g_pre_mix = _in_proj_bwd(du, dq, dk, dv, cos_t, sin_t, x, dh1, p["g_pre_mix"], w_in, [g_dt])
    publish({"g_pre_mix": g_pre_mix, "w_in": _matmul_tn(dproj, hn, _BF16, "grad_w_in")})
    return grad_x


_MESH = pl.DeviceIdType.MESH
_PEERS = N_DEV - 1


def _mesh_pos():
    return lax.axis_index("x"), lax.axis_index("y"), lax.axis_index("c")


def _dev_index(px, py, pc):
    return 4 * px + 2 * py + pc


def _all_gather(shards, out_dtype, name):
    n = len(shards)

    def body(*refs):
        ins, outs, stages = refs[:n], refs[n:2 * n], refs[2 * n:3 * n]
        send_sems, recv_sems, local_sems = refs[3 * n:]
        x, y, c = _mesh_pos()
        me, sibling = (x, y, c), (x, y, 1 - c)
        chips = [(1 - x, y), (x, 1 - y), (1 - x, 1 - y)]

        def copy(w, k, block, to, src=None):
            slot = outs[w].at[_dev_index(*block)]
            return pltpu.make_async_remote_copy(
                src_ref=slot if src is None else src, dst_ref=slot,
                send_sem=send_sems.at[_PEERS * w + k], recv_sem=recv_sems.at[_PEERS * w + k],
                device_id=to, device_id_type=_MESH)

        for w in range(n):
            stages[w][...] = ins[w][...].astype(out_dtype)
        mine, first, passed = [], [], []
        for w in range(n):
            cp = pltpu.make_async_copy(stages[w], outs[w].at[_dev_index(*me)], local_sems.at[w])
            cp.start()
            mine.append(cp)
            sends = [copy(w, 0, me, sibling, src=stages[w])]
            sends += [copy(w, 1 + j, me, (*chip, c), src=stages[w]) for j, chip in enumerate(chips)]
            for cp in sends:
                cp.start()
            first += sends
        for w in range(n):
            for j, chip in enumerate(chips):
                copy(w, 1 + j, (*chip, c), me).wait_recv()
                cp = copy(w, 4 + j, (*chip, c), sibling)
                cp.start()
                passed.append(cp)
        for w in range(n):
            copy(w, 0, sibling, me).wait_recv()
            for j, chip in enumerate(chips):
                copy(w, 4 + j, (*chip, 1 - c), me).wait_recv()
        for cp in first + passed:
            cp.wait_send()
        for cp in mine:
            cp.wait()

    return pl.pallas_call(
        body, name=name,
        out_shape=[_sds((N_DEV,) + s.shape, out_dtype) for s in shards],
        in_specs=[pl.BlockSpec(memory_space=pltpu.VMEM)] * n,
        out_specs=[pl.BlockSpec(memory_space=pl.ANY)] * n,
        scratch_shapes=[pltpu.VMEM(s.shape, out_dtype) for s in shards]
        + [pltpu.SemaphoreType.DMA((_PEERS * n,)), pltpu.SemaphoreType.DMA((_PEERS * n,)),
           pltpu.SemaphoreType.DMA((n,))],
        compiler_params=pltpu.CompilerParams(vmem_limit_bytes=VMEM_LIMIT),
    )(*shards)


_HBM_SPEC = pl.BlockSpec(memory_space=pltpu.HBM)
_SEM_SPEC = pl.BlockSpec(memory_space=pltpu.SEMAPHORE)
_DATAFLOW = pltpu.SideEffectType.DATAFLOW_SIDE_EFFECTING


def _peer(x, y, c, r):
    return (x ^ ((r >> 2) & 1), y ^ ((r >> 1) & 1), c ^ (r & 1))


def _hbm(a):
    return pltpu.with_memory_space_constraint(a, pltpu.HBM)


def _send_start(sources, blocked, name):
    n = len(sources)
    lands = [lax.empty((N_DEV,) + (s.shape[1:] if blocked else s.shape), s.dtype) for s in sources]

    def body(*refs):
        srcs, zones = refs[:n], refs[n:2 * n]
        send_sems, recv_sems = refs[2 * n:3 * n], refs[3 * n:4 * n]
        token, local_sems = refs[6 * n], refs[6 * n + 1]
        x, y, c = _mesh_pos()
        me = _dev_index(x, y, c)
        local = []
        for w in range(n):
            cp = pltpu.make_async_copy(srcs[w].at[me] if blocked else srcs[w], zones[w].at[me], local_sems.at[w])
            cp.start()
            local.append(cp)
            for r in range(1, N_DEV):
                peer = _peer(x, y, c, r)
                pltpu.make_async_remote_copy(
                    src_ref=srcs[w].at[_dev_index(*peer)] if blocked else srcs[w], dst_ref=zones[w].at[me],
                    send_sem=send_sems[w].at[r - 1], recv_sem=recv_sems[w].at[r - 1],
                    device_id=peer, device_id_type=_MESH).start()
        for cp in local:
            cp.wait()
        token[...] = jnp.zeros_like(token)

    sems = [pltpu.SemaphoreType.DMA((_PEERS,))] * (2 * n)
    out = pl.pallas_call(
        body, name=name,
        out_shape=sems + [pltpu.HBM(a.shape, a.dtype) for a in list(sources) + lands] + [_sds((8, 128), _F32)],
        in_specs=[_HBM_SPEC] * (2 * n),
        out_specs=[_SEM_SPEC] * (2 * n) + [_HBM_SPEC] * (2 * n) + [pl.BlockSpec(memory_space=pltpu.VMEM)],
        input_output_aliases={i: 2 * n + i for i in range(2 * n)},
        scratch_shapes=[pltpu.SemaphoreType.DMA((n,))],
        compiler_params=pltpu.CompilerParams(has_side_effects=_DATAFLOW),
    )(*[_hbm(a) for a in sources], *[_hbm(a) for a in lands])
    return out[:n], out[n:2 * n], out[2 * n:3 * n], out[3 * n:4 * n], out[4 * n]


def _send_wait(send_sems, recv_sems, sources, lands, after, blocked, name):
    n = len(sources)

    def body(*refs):
        srcs, zones = refs[:n], refs[n:2 * n]
        sends, recvs = refs[2 * n:3 * n], refs[3 * n:4 * n]
        x, y, c = _mesh_pos()
        for w in range(n):
            for r in range(1, N_DEV):
                peer = _peer(x, y, c, r)
                idx = _dev_index(*peer)
                cp = pltpu.make_async_remote_copy(
                    src_ref=srcs[w].at[idx] if blocked else srcs[w], dst_ref=zones[w].at[idx],
                    send_sem=sends[w].at[r - 1], recv_sem=recvs[w].at[r - 1],
                    device_id=peer, device_id_type=_MESH)
                cp.wait_send()
                cp.wait_recv()

    out = pl.pallas_call(
        body, name=name,
        out_shape=[pltpu.HBM(a.shape, a.dtype) for a in list(sources) + list(lands)],
        in_specs=[_HBM_SPEC] * (2 * n) + [_SEM_SPEC] * (2 * n) + [pl.BlockSpec(memory_space=pl.ANY)],
        out_specs=[_HBM_SPEC] * (2 * n),
        input_output_aliases={i: i for i in range(2 * n)},
        compiler_params=pltpu.CompilerParams(has_side_effects=_DATAFLOW),
    )(*sources, *lands, *send_sems, *recv_sems, after)
    return out[n:]


def _sequencer_exchange(sources, blocked, name, collective_id):
    n = len(sources)
    flags = blocked

    def body(*refs):
        srcs, zones = refs[:n], refs[n:2 * n]
        send_sems, recv_sems, local_sems = refs[2 * n:]
        x, y, c = _mesh_pos()
        me = _dev_index(x, y, c)
        barrier = pltpu.get_barrier_semaphore()
        for r in range(1, N_DEV):
            pl.semaphore_signal(barrier, inc=1, device_id=_peer(x, y, c, r), device_id_type=_MESH)
        pl.semaphore_wait(barrier, _PEERS)
        local, sends, recvs = [], [], []
        for w in range(n):
            cp = pltpu.make_async_copy(srcs[w].at[me] if flags[w] else srcs[w], zones[w].at[me], local_sems.at[w])
            cp.start()
            local.append(cp)
            for r in range(1, N_DEV):
                peer = _peer(x, y, c, r)
                idx = _dev_index(*peer)
                k = _PEERS * w + r - 1
                src = srcs[w].at[idx] if flags[w] else srcs[w]
                send = pltpu.make_async_remote_copy(
                    src_ref=src, dst_ref=zones[w].at[me], send_sem=send_sems.at[k], recv_sem=recv_sems.at[k],
                    device_id=peer, device_id_type=_MESH)
                send.start()
                sends.append(send)
                recvs.append(pltpu.make_async_remote_copy(
                    src_ref=src, dst_ref=zones[w].at[idx], send_sem=send_sems.at[k], recv_sem=recv_sems.at[k],
                    device_id=peer, device_id_type=_MESH))
        for cp in recvs:
            cp.wait_recv()
        for cp in sends:
            cp.wait_send()
        for cp in local:
            cp.wait()

    return pl.kernel(
        body, name=name,
        out_type=[_sds((N_DEV,) + (s.shape[1:] if f else s.shape), s.dtype) for s, f in zip(sources, flags)],
        mesh=plsc.ScalarSubcoreMesh(axis_name="sequencer", num_cores=1),
        scratch_types=[pltpu.SemaphoreType.DMA((_PEERS * n,)), pltpu.SemaphoreType.DMA((_PEERS * n,)),
                       pltpu.SemaphoreType.DMA((n,))],
        compiler_params=pltpu.CompilerParams(collective_id=collective_id),
    )(*sources)


def _sequencer_gather(shards, name, collective_id):
    n = len(shards)
    fan = 4

    def body(*refs):
        srcs, zones = refs[:n], refs[n:2 * n]
        send_sems, recv_sems, local_sems = refs[2 * n:]
        x, y, c = _mesh_pos()
        me, sibling = (x, y, c), (x, y, 1 - c)
        chips = [(1 - x, y), (x, 1 - y), (1 - x, 1 - y)]
        barrier = pltpu.get_barrier_semaphore()
        for peer in [sibling] + [(*chip, c) for chip in chips]:
            pl.semaphore_signal(barrier, inc=1, device_id=peer, device_id_type=_MESH)
        pl.semaphore_wait(barrier, fan)

        def copy(w, k, block, to, src=None):
            slot = zones[w].at[_dev_index(*block)]
            return pltpu.make_async_remote_copy(
                src_ref=slot if src is None else src, dst_ref=slot,
                send_sem=send_sems.at[_PEERS * w + k], recv_sem=recv_sems.at[_PEERS * w + k],
                device_id=to, device_id_type=_MESH)

        mine, first, passed = [], [], []
        for w in range(n):
            cp = pltpu.make_async_copy(srcs[w], zones[w].at[_dev_index(*me)], local_sems.at[w])
            cp.start()
            mine.append(cp)
            sends = [copy(w, 0, me, sibling, src=srcs[w])]
            sends += [copy(w, 1 + j, me, (*chip, c), src=srcs[w]) for j, chip in enumerate(chips)]
            for cp in sends:
                cp.start()
            first += sends
        for w in range(n):
            for j, chip in enumerate(chips):
                copy(w, 1 + j, (*chip, c), me).wait_recv()
                cp = copy(w, fan + j, (*chip, c), sibling)
                cp.start()
                passed.append(cp)
        for w in range(n):
            copy(w, 0, sibling, me).wait_recv()
            for j, chip in enumerate(chips):
                copy(w, fan + j, (*chip, 1 - c), me).wait_recv()
        for cp in first + passed:
            cp.wait_send()
        for cp in mine:
            cp.wait()

    return pl.kernel(
        body, name=name, out_type=[_sds((N_DEV,) + s.shape, s.dtype) for s in shards],
        mesh=plsc.ScalarSubcoreMesh(axis_name="sequencer", num_cores=1),
        scratch_types=[pltpu.SemaphoreType.DMA((_PEERS * n,)), pltpu.SemaphoreType.DMA((_PEERS * n,)),
                       pltpu.SemaphoreType.DMA((n,))],
        compiler_params=pltpu.CompilerParams(collective_id=collective_id),
    )(*shards)


def _row_tile(rows):
    return next(t for t in range(min(rows, 256), 0, -16) if rows % t == 0)


def _sum_parts(parts, name):
    _, rows, cols = parts.shape
    tr = _row_tile(rows)

    def body(p_ref, g_ref):
        g = p_ref[0].astype(_F32)
        for s in range(1, N_DEV):
            g = g + p_ref[s].astype(_F32)
        g_ref[...] = g

    return _call(body, (rows // tr,), [pl.BlockSpec((N_DEV, tr, cols), lambda i: (0, i, 0))],
                 _rows(tr, cols), _sds((rows, cols), _F32), name)(parts)


def _adam_update(g, w, m, v):
    new_m = ADAM_B1 * m + (1.0 - ADAM_B1) * g
    new_v = ADAM_B2 * v + (1.0 - ADAM_B2) * (g * g)
    m_hat = new_m / (1.0 - ADAM_B1 ** ADAM_STEP)
    v_hat = new_v / (1.0 - ADAM_B2 ** ADAM_STEP)
    return -ADAM_LR * (m_hat / (jnp.sqrt(v_hat) + ADAM_EPS) + ADAM_WD * w), new_m, new_v


def _adamw_small(parts, items, sums, name):
    n_p, n_i = len(parts), len(items)

    def body(*refs):
        p_refs, state, outs = refs[:n_p], refs[n_p:n_p + 3 * n_i], refs[n_p + 3 * n_i:]

        def total(part, rows, cols):
            g = p_refs[part][0, rows, cols]
            for s in range(1, N_DEV):
                g = g + p_refs[part][s, rows, cols]
            return g

        for i, (part, rows, cols, _, _, _) in enumerate(items):
            g = total(part, rows, cols)
            w_ref, m_ref, v_ref = state[3 * i:3 * i + 3]
            delta, new_m, new_v = _adam_update(g, w_ref[...], m_ref[...], v_ref[...])
            outs[4 * i][...] = g
            outs[4 * i + 1][...] = delta
            outs[4 * i + 2][...] = new_m
            outs[4 * i + 3][...] = new_v
        for j, (part, rows, cols) in enumerate(sums):
            outs[4 * n_i + j][...] = total(part, rows, cols)

    ins = list(parts) + [a for item in items for a in item[3:]]
    out_shapes = [item[3].shape for item in items for _ in range(4)]
    out_shapes += [(rows.stop - rows.start, cols.stop - cols.start) for _, rows, cols in sums]
    out = _call(body, (1,), [_whole(a.shape) for a in ins], [_whole(s) for s in out_shapes],
                [_sds(s, _F32) for s in out_shapes], name)(*ins)
    return [out[4 * i:4 * i + 4] for i in range(n_i)], out[4 * n_i:]


def _adamw(parts, w, m, v, name):
    rows, cols = w.shape
    tr = _row_tile(rows)
    n_parts = parts.shape[0]

    def body(p_ref, w_ref, m_ref, v_ref, g_ref, d_ref, nm_ref, nv_ref):
        g = p_ref[0].astype(_F32)
        for s in range(1, n_parts):
            g = g + p_ref[s].astype(_F32)
        new_m = ADAM_B1 * m_ref[...] + (1.0 - ADAM_B1) * g
        new_v = ADAM_B2 * v_ref[...] + (1.0 - ADAM_B2) * (g * g)
        m_hat = new_m / (1.0 - ADAM_B1 ** ADAM_STEP)
        v_hat = new_v / (1.0 - ADAM_B2 ** ADAM_STEP)
        g_ref[...] = g
        d_ref[...] = -ADAM_LR * (m_hat / (jnp.sqrt(v_hat) + ADAM_EPS) + ADAM_WD * w_ref[...])
        nm_ref[...] = new_m
        nv_ref[...] = new_v

    blk = _rows(tr, cols)
    return _call(body, (rows // tr,),
                 [pl.BlockSpec((n_parts, tr, cols), lambda i: (0, i, 0)), blk, blk, blk],
                 [blk] * 4, [_sds((rows, cols), _F32)] * 4, name)(parts, w, m, v)


_SMALL = ("g_pre_mix", "ssm_lambda_re", "ssm_lambda_im", "ssm_log_dt", "ssm_b_re", "ssm_b_im",
          "ssm_c_re", "ssm_c_im", "ssm_d", "b_glu", "attn_sinks", "g_ssm_out", "g_attn_out",
          "g_post_mix", "g_pre_ffn", "g_post_ffn")
_BIG = ("w_in", "w_glu", "w_out", "w_gate_up", "w_down")
_WEIGHTS = ("g_pre_mix", "w_in", "ssm_lambda_re", "ssm_lambda_im", "ssm_log_dt", "ssm_b_re", "ssm_b_im",
            "ssm_c_re", "ssm_c_im", "ssm_d", "w_glu", "b_glu", "attn_sinks", "g_ssm_out", "g_attn_out",
            "w_out", "g_post_mix", "g_pre_ffn", "w_gate_up", "w_down", "g_post_ffn")
_LANES = 128


_SHAPE_2D = {
    "g_pre_mix": (1, D_MODEL), "ssm_lambda_re": (SSM_GROUPS, SSM_STATE), "ssm_lambda_im": (SSM_GROUPS, SSM_STATE),
    "ssm_log_dt": (1, SSM_GROUPS), "ssm_b_re": (N_STATE, SSM_GROUP), "ssm_b_im": (N_STATE, SSM_GROUP),
    "ssm_c_re": (SSM_WIDTH, SSM_STATE), "ssm_c_im": (SSM_WIDTH, SSM_STATE), "ssm_d": (SSM_GROUPS, SSM_GROUP),
    "b_glu": (1, 2 * SSM_WIDTH), "attn_sinks": (1, N_Q_HEADS), "g_ssm_out": (1, SSM_WIDTH),
    "g_attn_out": (1, ATTN_WIDTH), "g_post_mix": (1, D_MODEL), "g_pre_ffn": (1, D_MODEL), "g_post_ffn": (1, D_MODEL)}
_ROW_WIDTH = {"g_pre_mix": D_MODEL, "b_glu": 2 * SSM_WIDTH, "attn_sinks": _LANES, "g_ssm_out": SSM_WIDTH,
              "g_attn_out": ATTN_WIDTH, "g_post_mix": D_MODEL, "g_pre_ffn": D_MODEL, "g_post_ffn": D_MODEL,
              "loss": _LANES}
_DENSE = ("ssm_b_re", "ssm_b_im", "ssm_c_re", "ssm_c_im")


def _row_slots(names):
    slots, row, col = {}, 0, 0
    for n in names:
        width = _ROW_WIDTH[n]
        if col + width > D_MODEL:
            row, col = row + 1, 0
        slots[n] = (row, col, width)
        col += width
    return slots


def _stack_rows(named, slots):
    n_rows = -(-(max(r for r, _, _ in slots.values()) + 1) // 8) * 8
    lines = []
    for r in range(n_rows):
        pieces = [named[n] for n, (row, _, _) in slots.items() if row == r]
        used = sum(p.shape[1] for p in pieces)
        if used < D_MODEL:
            pieces.append(jnp.zeros((1, D_MODEL - used), _F32))
        lines.append(jnp.concatenate(pieces, axis=1) if len(pieces) > 1 else pieces[0])
    return jnp.concatenate(lines, axis=0)


def kernel(x, positions, g_pre_mix, w_in, ssm_lambda_re, ssm_lambda_im, ssm_log_dt, ssm_b_re, ssm_b_im, ssm_c_re, ssm_c_im, ssm_d, w_glu, b_glu, attn_sinks, g_ssm_out, g_attn_out, w_out, g_post_mix, g_pre_ffn, w_gate_up, w_down, g_post_ffn, loss_target, m_g_pre_mix, m_w_in, m_ssm_lambda_re, m_ssm_lambda_im, m_ssm_log_dt, m_ssm_b_re, m_ssm_b_im, m_ssm_c_re, m_ssm_c_im, m_ssm_d, m_w_glu, m_b_glu, m_attn_sinks, m_g_ssm_out, m_g_attn_out, m_w_out, m_g_post_mix, m_g_pre_ffn, m_w_gate_up, m_w_down, m_g_post_ffn, v_g_pre_mix, v_w_in, v_ssm_lambda_re, v_ssm_lambda_im, v_ssm_log_dt, v_ssm_b_re, v_ssm_b_im, v_ssm_c_re, v_ssm_c_im, v_ssm_d, v_w_glu, v_b_glu, v_attn_sinks, v_g_ssm_out, v_g_attn_out, v_w_out, v_g_post_mix, v_g_pre_ffn, v_w_gate_up, v_w_down, v_g_post_ffn):
    w = dict(g_pre_mix=g_pre_mix, w_in=w_in, ssm_lambda_re=ssm_lambda_re, ssm_lambda_im=ssm_lambda_im,
             ssm_log_dt=ssm_log_dt, ssm_b_re=ssm_b_re, ssm_b_im=ssm_b_im, ssm_c_re=ssm_c_re, ssm_c_im=ssm_c_im,
             ssm_d=ssm_d, w_glu=w_glu, b_glu=b_glu, attn_sinks=attn_sinks, g_ssm_out=g_ssm_out,
             g_attn_out=g_attn_out, w_out=w_out, g_post_mix=g_post_mix, g_pre_ffn=g_pre_ffn,
             w_gate_up=w_gate_up, w_down=w_down, g_post_ffn=g_post_ffn)
    m = dict(g_pre_mix=m_g_pre_mix, w_in=m_w_in, ssm_lambda_re=m_ssm_lambda_re, ssm_lambda_im=m_ssm_lambda_im,
             ssm_log_dt=m_ssm_log_dt, ssm_b_re=m_ssm_b_re, ssm_b_im=m_ssm_b_im, ssm_c_re=m_ssm_c_re,
             ssm_c_im=m_ssm_c_im, ssm_d=m_ssm_d, w_glu=m_w_glu, b_glu=m_b_glu, attn_sinks=m_attn_sinks,
             g_ssm_out=m_g_ssm_out, g_attn_out=m_g_attn_out, w_out=m_w_out, g_post_mix=m_g_post_mix,
             g_pre_ffn=m_g_pre_ffn, w_gate_up=m_w_gate_up, w_down=m_w_down, g_post_ffn=m_g_post_ffn)
    v = dict(g_pre_mix=v_g_pre_mix, w_in=v_w_in, ssm_lambda_re=v_ssm_lambda_re, ssm_lambda_im=v_ssm_lambda_im,
             ssm_log_dt=v_ssm_log_dt, ssm_b_re=v_ssm_b_re, ssm_b_im=v_ssm_b_im, ssm_c_re=v_ssm_c_re,
             ssm_c_im=v_ssm_c_im, ssm_d=v_ssm_d, w_glu=v_w_glu, b_glu=v_b_glu, attn_sinks=v_attn_sinks,
             g_ssm_out=v_g_ssm_out, g_attn_out=v_g_attn_out, w_out=v_w_out, g_post_mix=v_g_post_mix,
             g_pre_ffn=v_g_pre_ffn, w_gate_up=v_w_gate_up, w_down=v_w_down, g_post_ffn=v_g_post_ffn)

    transposed = ("w_in", "w_glu", "w_gate_up")
    native_transposed = ("w_in", "w_gate_up")
    shard = {n: (w[n][0].T if n in transposed else w[n][0]).astype(_BF16) for n in _BIG}
    gathered = {}
    for names, lands in (
            (("w_in",), _sequencer_exchange([shard["w_in"]], [False], "gather_w_in", 1)),
            (("w_glu", "w_out"), _sequencer_exchange([shard["w_glu"], shard["w_out"]], [False] * 2, "gather_mix", 2)),
            (("w_gate_up", "w_down"), _sequencer_gather([shard["w_gate_up"], shard["w_down"]], "gather_ffn", 3))):
        gathered.update({n: a.reshape(-1, a.shape[2]) for n, a in zip(names, lands)})

    def fetch(names, after):
        del after
        return [gathered[n] for n in names]

    sent = []

    def publish(named):
        big = [n for n in named if n in _BIG]
        rows = [n for n in named if n in _ROW_WIDTH]
        dense = [n for n in named if n in _DENSE]
        plain = [n for n in named if n not in big + rows + dense]
        sources = [named[n].reshape(N_DEV, -1, named[n].shape[1]) for n in big]
        slots = _row_slots(rows)
        if rows:
            sources.append(_stack_rows(named, slots))
        sources += [named[n].reshape(-1, _LANES) for n in dense] + [named[n] for n in plain]
        flags = [True] * len(big) + [False] * (len(sources) - len(big))
        cid = 4 + len(sent)
        sent.append((big, slots, dense, plain, _sequencer_exchange(sources, flags, "grads_%d" % cid, cid)))
        return [named[n] for n in big]

    p = {n: w[n] for n in _SMALL}
    grad_x = _local_step(x[0], positions[0], loss_target[0], p, fetch, publish)

    state = {n: [a.reshape(_SHAPE_2D[n]) for a in (w[n], m[n], v[n])] for n in _SMALL}
    result = {}
    total_loss = None
    for big, slots, dense, plain, lands in sent:
        lands = list(lands)
        for name in big:
            part = lands.pop(0)
            if name in native_transposed:
                updated = _adamw(part, w[name][0].T, m[name][0].T, v[name][0].T, "adamw_" + name)
                result[name] = [a.T[None] for a in updated]
                continue
            if name in transposed:
                part = _sum_parts(part, "sum_" + name).T[None]
            result[name] = [a[None] for a in _adamw(part, w[name][0], m[name][0], v[name][0], "adamw_" + name)]
        parts, items, sums, names = [], [], [], []
        if slots:
            parts.append(lands.pop(0))
            for name, (row, col, _) in slots.items():
                if name == "loss":
                    sums.append((0, slice(row, row + 1), slice(col, col + _LANES)))
                else:
                    items.append((0, slice(row, row + 1), slice(col, col + _SHAPE_2D[name][1]), *state[name]))
                    names.append(name)
        for name in dense:
            part = lands.pop(0).reshape((N_DEV,) + _SHAPE_2D[name])
            result[name] = _adamw(part, *state[name], "adamw_" + name)
        for name in plain:
            rows_n, cols_n = _SHAPE_2D[name]
            items.append((len(parts), slice(0, rows_n), slice(0, cols_n), *state[name]))
            parts.append(lands.pop(0))
            names.append(name)
        if items:
            updated, summed = _adamw_small(parts, items, sums, "adamw_small_" + names[0])
            result.update(dict(zip(names, updated)))
            if summed:
                total_loss = summed[0][0, 0]

    out = [total_loss, grad_x[None]]
    for kind in range(4):
        out += [result[n][kind].reshape(w[n].shape) for n in _WEIGHTS]
    return tuple(out)
```

```python
import functools
import math

import numpy as np
import jax
import jax.numpy as jnp
from jax import lax
from jax.experimental import pallas as pl
from jax.experimental.pallas import tpu as pltpu
from jax.experimental.pallas import tpu_sc as plsc

D_MODEL = 1024
SSM_WIDTH = 512
SSM_GROUP = 16
SSM_GROUPS = 32
SSM_STATE = 64
N_STATE = SSM_GROUPS * SSM_STATE
ATTN_WIDTH = 512
HEAD_DIM = 64
N_Q_HEADS = 8
N_KV_HEADS = 2
Q_PER_KV = 4
KV_WIDTH = 128
IN_WIDTH = 1280
BLOCK = 128
ROPE_DIM = 16
ROPE_THETA = 500000.0
D_FF = 2816
NORM_EPS = 1e-6
MASK_VALUE = -1e30
ADAM_LR = 0.001
ADAM_B1 = 0.9
ADAM_B2 = 0.999
ADAM_EPS = 1e-08
ADAM_WD = 0.01
ADAM_STEP = 10

N_DEV = 8
SCAN_CHUNKS = 8
SCAN_COLS = 512
TOKEN_TILE = 256
VMEM_LIMIT = 56 * 1024 * 1024

_F32 = jnp.float32
_BF16 = jnp.bfloat16
_MXU = jnp.bfloat16

_NN = ((1,), (0,))
_NT = ((1,), (1,))
_TN = ((0,), (0,))


def _dot(a, b, dims):
    return lax.dot_general(a.astype(_MXU), b.astype(_MXU), (dims, ((), ())),
                           preferred_element_type=_F32)


def _dot_exact(a, b, dims):
    return lax.dot_general(a.astype(_F32), b.astype(_F32), (dims, ((), ())),
                           precision=lax.Precision.HIGHEST, preferred_element_type=_F32)


def _iota(shape, dim):
    return lax.broadcasted_iota(jnp.int32, shape, dim)


def _rms_fwd(x, g):
    r = lax.rsqrt(jnp.mean(x * x, axis=-1, keepdims=True) + NORM_EPS)
    return x * r * g, r


def _rms_bwd(dy, x, g, r):
    a = dy * g
    xn = x * r
    dx = r * (a - xn * jnp.mean(a * xn, axis=-1, keepdims=True))
    dg = jnp.sum(dy * xn, axis=0, keepdims=True)
    return dx, dg


def _call(body, grid, in_specs, out_specs, out_shape, name, scratch=(), tokens=()):
    params = pltpu.CompilerParams(dimension_semantics=("arbitrary",) * len(grid),
                                  vmem_limit_bytes=VMEM_LIMIT)
    n_in, n_tok = len(in_specs), len(tokens)

    def run(*refs):
        return body(*refs[:n_in], *refs[n_in + n_tok:])

    call = pl.pallas_call(run, grid=grid,
                          in_specs=list(in_specs) + [pl.BlockSpec(memory_space=pl.ANY)] * n_tok,
                          out_specs=out_specs, out_shape=out_shape, scratch_shapes=list(scratch),
                          compiler_params=params, name=name)
    return lambda *args: call(*args, *tokens)


def _rows(tm, n):
    return pl.BlockSpec((tm, n), lambda i: (i, 0))


def _whole(shape):
    nd = len(shape)
    return pl.BlockSpec(shape, lambda i: (0,) * nd)


def _sds(shape, dtype):
    return jax.ShapeDtypeStruct(shape, dtype)


def _tile(L):
    return min(TOKEN_TILE, L)


def _accumulate(ref, val, first):
    @pl.when(first)
    def _():
        ref[...] = val

    @pl.when(jnp.logical_not(first))
    def _():
        ref[...] += val


def _rope_rows():
    half = ROPE_DIM // 2
    inv = (np.float32(ROPE_THETA) ** (-np.arange(half, dtype=np.float32) * np.float32(2.0) / np.float32(ROPE_DIM))).astype(np.float32)
    col = np.arange(KV_WIDTH) % HEAD_DIM
    freq = np.where(col < ROPE_DIM, inv[col % half], 0.0).astype(np.float32)
    sign = np.where(col < half, -1.0, np.where(col < ROPE_DIM, 1.0, 0.0)).astype(np.float32)
    return freq[None, :], sign[None, :]


def _rope_tables(pos_col):
    L = pos_col.shape[0]
    tm = _tile(L)
    freq, sign = _rope_rows()

    def body(pos_ref, freq_ref, sign_ref, cos_ref, sin_ref):
        ang = pos_ref[...].astype(_F32) * freq_ref[...]
        cos_ref[...] = jnp.cos(ang)
        sin_ref[...] = jnp.sin(ang) * sign_ref[...]

    return _call(body, (L // tm,),
                 [_rows(tm, 1), _whole((1, KV_WIDTH)), _whole((1, KV_WIDTH))],
                 [_rows(tm, KV_WIDTH), _rows(tm, KV_WIDTH)],
                 [_sds((L, KV_WIDTH), _F32)] * 2, "rope_tables")(pos_col, jnp.asarray(freq), jnp.asarray(sign))


def _widen(t, width):
    return t if width == KV_WIDTH else jnp.concatenate([t] * (width // KV_WIDTH), axis=1)


def _rope_partner(t):
    w = t.shape[1]
    in_head = _iota((1, w), 1) & (HEAD_DIM - 1)
    second = jnp.where(in_head < ROPE_DIM, pltpu.roll(t, ROPE_DIM // 2, 1), 0.0)
    return jnp.where(in_head < ROPE_DIM // 2, pltpu.roll(t, w - ROPE_DIM // 2, 1), second)


def _rope_apply(t, cos_t, sin_t):
    w = t.shape[1]
    return t * _widen(cos_t, w) + _rope_partner(t) * _widen(sin_t, w)


def _rope_transpose(dt, cos_t, sin_t):
    w = dt.shape[1]
    return dt * _widen(cos_t, w) + _rope_partner(dt * _widen(sin_t, w))


def _in_proj(x, g_pre_mix, w_in, cos_t, sin_t):
    L = x.shape[0]
    tm = _tile(L)

    def body(x_ref, g_ref, w_ref, cos_ref, sin_ref, hn_ref, u_ref, q_ref, k_ref, v_ref):
        hn, _ = _rms_fwd(x_ref[...], g_ref[...])
        hn = hn.astype(_BF16)
        hn_ref[...] = hn
        proj = _dot(hn, w_ref[...], _NT)
        u_ref[...] = proj[:, :SSM_WIDTH]
        q = proj[:, SSM_WIDTH:SSM_WIDTH + ATTN_WIDTH]
        k = proj[:, SSM_WIDTH + ATTN_WIDTH:SSM_WIDTH + ATTN_WIDTH + KV_WIDTH]
        cos_v, sin_v = cos_ref[...], sin_ref[...]
        q_ref[...] = _rope_apply(q, cos_v, sin_v).astype(_BF16)
        k_ref[...] = _rope_apply(k, cos_v, sin_v).astype(_BF16)
        v_ref[...] = proj[:, SSM_WIDTH + ATTN_WIDTH + KV_WIDTH:].astype(_BF16)

    return _call(body, (L // tm,),
                 [_rows(tm, D_MODEL), _whole((1, D_MODEL)), _whole((IN_WIDTH, D_MODEL)),
                  _rows(tm, KV_WIDTH), _rows(tm, KV_WIDTH)],
                 [_rows(tm, D_MODEL), _rows(tm, SSM_WIDTH), _rows(tm, ATTN_WIDTH),
                  _rows(tm, KV_WIDTH), _rows(tm, KV_WIDTH)],
                 [_sds((L, D_MODEL), _BF16), _sds((L, SSM_WIDTH), _F32), _sds((L, ATTN_WIDTH), _BF16),
                  _sds((L, KV_WIDTH), _BF16), _sds((L, KV_WIDTH), _BF16)],
                 "in_proj")(x, g_pre_mix, w_in, cos_t, sin_t)


def _s5_discretize(lam_re, lam_im, log_dt):
    lr = jnp.minimum(lam_re, -1e-4)
    li = lam_im
    dt = jnp.exp(log_dt)
    mag = jnp.exp(lr * dt)
    ar = mag * jnp.cos(li * dt)
    ai = mag * jnp.sin(li * dt)
    den = lr * lr + li * li
    fr = ((ar - 1.0) * lr + ai * li) / den
    fi = (ai * lr - (ar - 1.0) * li) / den
    return ar, ai, fr, fi


def _s5_bbar(lam_re, lam_im, log_dt, b_re, b_im):
    ar, ai, fr, fi = _s5_discretize(lam_re, lam_im, log_dt)
    return ar, ai, fr * b_re - fi * b_im, fr * b_im + fi * b_re


def _spread_masks():
    e16 = (_iota((SSM_GROUP, SSM_WIDTH), 1) & (SSM_GROUP - 1)) == _iota((SSM_GROUP, SSM_WIDTH), 0)
    e64 = (_iota((SSM_STATE, N_STATE), 1) & (SSM_STATE - 1)) == _iota((SSM_STATE, N_STATE), 0)
    mask_b = (_iota((N_STATE, SSM_WIDTH), 0) >> 6) == (_iota((N_STATE, SSM_WIDTH), 1) >> 4)
    mask_c = (_iota((SSM_WIDTH, N_STATE), 0) >> 4) == (_iota((SSM_WIDTH, N_STATE), 1) >> 6)
    return e16.astype(_F32), e64.astype(_F32), mask_b, mask_c


SUPER = 4
SB_STATE = N_STATE // SUPER
SB_WIDTH = SSM_WIDTH // SUPER


def _sb_state(k):
    return slice(SB_STATE * k, SB_STATE * (k + 1))


def _sb_width(k):
    return slice(SB_WIDTH * k, SB_WIDTH * (k + 1))


def _ssm_prep(lam_re_r, lam_im_r, ldt_r, lam_re_c, lam_im_c, ldt_c, b_re2, b_im2, c_re2, c_im2):
    def body(lrr, lir, ldr, lrc, lic, ldc, bre, bim, cre, cim, ar_ref, ai_ref, btr, bti, ctr, cti):
        ar, ai, _, _ = _s5_discretize(lrr[...], lir[...], ldr[...])
        ar_ref[...] = ar
        ai_ref[...] = ai
        _, _, bbr, bbi = _s5_bbar(lrc[...], lic[...], ldc[...], bre[...], bim[...])
        e16, e64, mask_b, mask_c = _spread_masks()

        def fold_b(bb):
            full = jnp.where(mask_b, _dot(bb, e16, _NN), 0.0)
            return sum(full[:, _sb_width(k)] for k in range(SUPER)).astype(_BF16)

        def fold_c(cc):
            full = jnp.where(mask_c, _dot(cc, e64, _NN), 0.0)
            return sum(full[_sb_width(k), :] for k in range(SUPER)).astype(_BF16)

        btr[...] = fold_b(bbr)
        bti[...] = fold_b(bbi)
        ctr[...] = fold_c(cre[...])
        cti[...] = fold_c(cim[...])

    row = (1, N_STATE)
    ins = [lam_re_r, lam_im_r, ldt_r, lam_re_c, lam_im_c, ldt_c, b_re2, b_im2, c_re2, c_im2]
    return _call(body, (1,), [_whole(a.shape) for a in ins],
                 [_whole(row), _whole(row), _whole((N_STATE, SB_WIDTH)), _whole((N_STATE, SB_WIDTH)),
                  _whole((SB_WIDTH, N_STATE)), _whole((SB_WIDTH, N_STATE))],
                 [_sds(row, _F32), _sds(row, _F32), _sds((N_STATE, SB_WIDTH), _BF16),
                  _sds((N_STATE, SB_WIDTH), _BF16), _sds((SB_WIDTH, N_STATE), _BF16),
                  _sds((SB_WIDTH, N_STATE), _BF16)], "ssm_prep")(*ins)


def _ssm_bu(u, bt_re, bt_im):
    L = u.shape[0]
    tm = _tile(L)

    def body(u_ref, br_ref, bi_ref, or_ref, oi_ref):
        for k in range(SUPER):
            ub = u_ref[:, _sb_width(k)].astype(_BF16)
            or_ref[:, _sb_state(k)] = _dot(ub, br_ref[_sb_state(k), :], _NT)
            oi_ref[:, _sb_state(k)] = _dot(ub, bi_ref[_sb_state(k), :], _NT)

    return _call(body, (L // tm,),
                 [_rows(tm, SSM_WIDTH), _whole((N_STATE, SB_WIDTH)), _whole((N_STATE, SB_WIDTH))],
                 [_rows(tm, N_STATE), _rows(tm, N_STATE)],
                 [_sds((L, N_STATE), _F32)] * 2, "ssm_bu")(u, bt_re, bt_im)


def _complex_power(ar, ai, n):
    def step(_, c):
        pr, pi = c
        return pr * ar - pi * ai, pr * ai + pi * ar
    return lax.fori_loop(0, n, step, (jnp.ones_like(ar), jnp.zeros_like(ai)))


def _chunk_carries(er, ei, pr, pi, reverse):
    rows = _iota(er.shape, 0)
    sr = jnp.zeros_like(pr)
    si = jnp.zeros_like(pi)
    out_r = jnp.zeros_like(er)
    out_i = jnp.zeros_like(ei)
    order = range(SCAN_CHUNKS - 1, 0, -1) if reverse else range(SCAN_CHUNKS - 1)
    for c in order:
        e_r = er[c:c + 1, :]
        e_i = ei[c:c + 1, :]
        sr, si = pr * sr - pi * si + e_r, pr * si + pi * sr + e_i
        nxt = c - 1 if reverse else c + 1
        out_r = jnp.where(rows == nxt, sr, out_r)
        out_i = jnp.where(rows == nxt, si, out_i)
    return out_r, out_i


def _scan_fwd(b_re, b_im, a_re, a_im):
    T = b_re.shape[0]
    W = SCAN_COLS
    blk = pl.BlockSpec((T, SCAN_CHUNKS, W), lambda j: (0, 0, j))
    vec = pl.BlockSpec((1, W), lambda j: (0, j))

    def body(br_ref, bi_ref, ar_ref, ai_ref, xr_ref, xi_ref):
        ar, ai = ar_ref[...], ai_ref[...]
        ar8 = jnp.broadcast_to(ar, (SCAN_CHUNKS, W))
        ai8 = jnp.broadcast_to(ai, (SCAN_CHUNKS, W))

        def local(t, c):
            cr, ci = c
            return ar8 * cr - ai8 * ci + br_ref[t], ar8 * ci + ai8 * cr + bi_ref[t]

        zero = jnp.zeros((SCAN_CHUNKS, W), _F32)
        er, ei = lax.fori_loop(0, T, local, (zero, zero))
        pr, pi = _complex_power(ar, ai, T)
        sr, si = _chunk_carries(er, ei, pr, pi, reverse=False)

        def final(t, c):
            nr, ni = local(t, c)
            xr_ref[t] = nr
            xi_ref[t] = ni
            return nr, ni

        lax.fori_loop(0, T, final, (sr, si))

    shape = _sds(b_re.shape, _F32)
    return _call(body, (N_STATE // W,), [blk, blk, vec, vec], [blk, blk], [shape, shape],
                 "scan_fwd")(b_re, b_im, a_re, a_im)


def _scan_bwd(dx_re, dx_im, x_re, x_im, a_re, a_im, tokens=()):
    T = dx_re.shape[0]
    W = SCAN_COLS
    blk = pl.BlockSpec((T, SCAN_CHUNKS, W), lambda j: (0, 0, j))
    vec = pl.BlockSpec((1, W), lambda j: (0, j))

    def body(dr_ref, di_ref, xr_ref, xi_ref, ar_ref, ai_ref, lr_ref, li_ref, dar_ref, dai_ref):
        ar, ai = ar_ref[...], ai_ref[...]
        ar8 = jnp.broadcast_to(ar, (SCAN_CHUNKS, W))
        ai8 = jnp.broadcast_to(ai, (SCAN_CHUNKS, W))

        def local(t, c):
            cr, ci = c
            return ar8 * cr + ai8 * ci + dr_ref[t], ar8 * ci - ai8 * cr + di_ref[t]

        zero = jnp.zeros((SCAN_CHUNKS, W), _F32)
        er, ei = lax.fori_loop(0, T, lambda k, c: local(T - 1 - k, c), (zero, zero))
        pr, pi = _complex_power(ar, -ai, T)
        sr, si = _chunk_carries(er, ei, pr, pi, reverse=True)

        def grad_a(acc, nr, ni, xpr, xpi):
            return acc[0] + nr * xpr + ni * xpi, acc[1] + ni * xpr - nr * xpi

        def final(k, c):
            t = T - 1 - k
            nr, ni = local(t, c[:2])
            lr_ref[t] = nr
            li_ref[t] = ni
            gr, gi = grad_a(c[2:], nr, ni, xr_ref[t - 1], xi_ref[t - 1])
            return nr, ni, gr, gi

        cr, ci, gr, gi = lax.fori_loop(0, T - 1, final, (sr, si, zero, zero))
        nr, ni = local(0, (cr, ci))
        lr_ref[0] = nr
        li_ref[0] = ni
        first = _iota((SCAN_CHUNKS, W), 0) == 0
        xpr = jnp.where(first, 0.0, pltpu.roll(xr_ref[T - 1], 1, 0))
        xpi = jnp.where(first, 0.0, pltpu.roll(xi_ref[T - 1], 1, 0))
        gr, gi = grad_a((gr, gi), nr, ni, xpr, xpi)
        dar_ref[...] = jnp.sum(gr, axis=0, keepdims=True)
        dai_ref[...] = jnp.sum(gi, axis=0, keepdims=True)

    shape = _sds(dx_re.shape, _F32)
    row = _sds((1, N_STATE), _F32)
    return _call(body, (N_STATE // W,), [blk, blk, blk, blk, vec, vec], [blk, blk, vec, vec],
                 [shape, shape, row, row], "scan_bwd", tokens=tokens)(dx_re, dx_im, x_re, x_im, a_re, a_im)


_GELU_K = math.sqrt(2.0 / math.pi)
_GELU_C = 0.044715


def _gelu(y):
    return 0.5 * y * (1.0 + jnp.tanh(_GELU_K * (y + _GELU_C * y * y * y)))


def _gelu_grad(y):
    t = jnp.tanh(_GELU_K * (y + _GELU_C * y * y * y))
    return 0.5 * (1.0 + t) + 0.5 * y * (1.0 - t * t) * _GELU_K * (1.0 + 3.0 * _GELU_C * y * y)


def _ssm_out(x_re, x_im, u, ct_re, ct_im, d_row, w_glu, b_glu, g_ssm):
    L = u.shape[0]
    tm = _tile(L)

    def body(xr_ref, xi_ref, u_ref, cr_ref, ci_ref, d_ref, w_ref, b_ref, g_ref, y_ref, z_ref, n_ref):
        cx = [_dot(xr_ref[:, _sb_state(k)], cr_ref[:, _sb_state(k)], _NT)
              - _dot(xi_ref[:, _sb_state(k)], ci_ref[:, _sb_state(k)], _NT) for k in range(SUPER)]
        y = jnp.concatenate(cx, axis=1) + d_ref[...] * u_ref[...]
        y_ref[...] = y
        z = _dot(_gelu(y), w_ref[...], _NT) + b_ref[...]
        z_ref[...] = z
        out = z[:, :SSM_WIDTH] * jax.nn.sigmoid(z[:, SSM_WIDTH:])
        n, _ = _rms_fwd(out, g_ref[...])
        n_ref[...] = n.astype(_BF16)

    return _call(body, (L // tm,),
                 [_rows(tm, N_STATE), _rows(tm, N_STATE), _rows(tm, SSM_WIDTH),
                  _whole((SB_WIDTH, N_STATE)), _whole((SB_WIDTH, N_STATE)), _whole((1, SSM_WIDTH)),
                  _whole((2 * SSM_WIDTH, SSM_WIDTH)), _whole((1, 2 * SSM_WIDTH)), _whole((1, SSM_WIDTH))],
                 [_rows(tm, SSM_WIDTH), _rows(tm, 2 * SSM_WIDTH), _rows(tm, SSM_WIDTH)],
                 [_sds((L, SSM_WIDTH), _F32), _sds((L, 2 * SSM_WIDTH), _F32), _sds((L, SSM_WIDTH), _BF16)],
                 "ssm_out")(x_re, x_im, u, ct_re, ct_im, d_row, w_glu, b_glu, g_ssm)


def _ssm_out_bwd(dn, y, z, u, ct_re, ct_im, d_row, w_glu, g_ssm):
    L = u.shape[0]
    tm = _tile(L)

    def body(dn_ref, y_ref, z_ref, u_ref, cr_ref, ci_ref, d_ref, w_ref, g_ref,
             gy_ref, dz_ref, dy_ref, dud_ref, dxr_ref, dxi_ref, dg_ref, db_ref, dd_ref):
        first = pl.program_id(0) == 0
        z = z_ref[...]
        z1, z2 = z[:, :SSM_WIDTH], z[:, SSM_WIDTH:]
        sig = jax.nn.sigmoid(z2)
        out = z1 * sig
        g = g_ref[...]
        _, r = _rms_fwd(out, g)
        dout, dg = _rms_bwd(dn_ref[...], out, g, r)
        _accumulate(dg_ref, dg, first)
        dz = jnp.concatenate([dout * sig, dout * z1 * sig * (1.0 - sig)], axis=1)
        _accumulate(db_ref, jnp.sum(dz, axis=0, keepdims=True), first)
        dzb = dz.astype(_BF16)
        dz_ref[...] = dzb
        y = y_ref[...]
        gy_ref[...] = _gelu(y).astype(_BF16)
        dy = _dot(dzb, w_ref[...], _NN) * _gelu_grad(y)
        u = u_ref[...]
        _accumulate(dd_ref, jnp.sum(dy * u, axis=0, keepdims=True), first)
        dud_ref[...] = d_ref[...] * dy
        dyb = dy.astype(_BF16)
        dy_ref[...] = dyb
        for k in range(SUPER):
            dxr_ref[:, _sb_state(k)] = _dot(dyb[:, _sb_width(k)], cr_ref[:, _sb_state(k)], _NN)
            dxi_ref[:, _sb_state(k)] = -_dot(dyb[:, _sb_width(k)], ci_ref[:, _sb_state(k)], _NN)

    row = _whole((1, SSM_WIDTH))
    return _call(body, (L // tm,),
                 [_rows(tm, SSM_WIDTH), _rows(tm, SSM_WIDTH), _rows(tm, 2 * SSM_WIDTH), _rows(tm, SSM_WIDTH),
                  _whole((SB_WIDTH, N_STATE)), _whole((SB_WIDTH, N_STATE)), row,
                  _whole((2 * SSM_WIDTH, SSM_WIDTH)), row],
                 [_rows(tm, SSM_WIDTH), _rows(tm, 2 * SSM_WIDTH), _rows(tm, SSM_WIDTH), _rows(tm, SSM_WIDTH),
                  _rows(tm, N_STATE), _rows(tm, N_STATE), row, _whole((1, 2 * SSM_WIDTH)), row],
                 [_sds((L, SSM_WIDTH), _BF16), _sds((L, 2 * SSM_WIDTH), _BF16), _sds((L, SSM_WIDTH), _BF16),
                  _sds((L, SSM_WIDTH), _F32), _sds((L, N_STATE), _F32), _sds((L, N_STATE), _F32),
                  _sds((1, SSM_WIDTH), _F32), _sds((1, 2 * SSM_WIDTH), _F32), _sds((1, SSM_WIDTH), _F32)],
                 "ssm_out_bwd")(dn, y, z, u, ct_re, ct_im, d_row, w_glu, g_ssm)


def _ssm_du(lam_re, lam_im, bt_re, bt_im, dud):
    L = dud.shape[0]
    tm = _tile(L)

    def body(lr_ref, li_ref, br_ref, bi_ref, dud_ref, du_ref):
        for k in range(SUPER):
            du_ref[:, _sb_width(k)] = (_dot(lr_ref[:, _sb_state(k)], br_ref[_sb_state(k), :], _NN)
                                       + _dot(li_ref[:, _sb_state(k)], bi_ref[_sb_state(k), :], _NN)
                                       + dud_ref[:, _sb_width(k)])

    return _call(body, (L // tm,),
                 [_rows(tm, N_STATE), _rows(tm, N_STATE), _whole((N_STATE, SB_WIDTH)),
                  _whole((N_STATE, SB_WIDTH)), _rows(tm, SSM_WIDTH)],
                 _rows(tm, SSM_WIDTH), _sds((L, SSM_WIDTH), _F32), "ssm_du")(lam_re, lam_im, bt_re, bt_im, dud)


def _ssm_weight_grads(dy, x_re, x_im, lam_re, lam_im, u):
    L = u.shape[0]

    def body(dy_ref, xr_ref, xi_ref, lr_ref, li_ref, u_ref, dcr_ref, dci_ref, dbr_ref, dbi_ref):
        dyb = dy_ref[...]
        ub = u_ref[...].astype(_BF16)
        dcr_ref[...] = _dot(dyb, xr_ref[...], _TN)
        dci_ref[...] = _dot(dyb, xi_ref[...], _TN)
        dbr_ref[...] = _dot(lr_ref[...], ub, _TN)
        dbi_ref[...] = _dot(li_ref[...], ub, _TN)

    width = pl.BlockSpec((L, SB_WIDTH), lambda k: (0, k))
    state = pl.BlockSpec((L, SB_STATE), lambda k: (0, k))
    out_c = pl.BlockSpec((SB_WIDTH, SB_STATE), lambda k: (0, k))
    out_b = pl.BlockSpec((SB_STATE, SB_WIDTH), lambda k: (k, 0))
    return _call(body, (SUPER,), [width, state, state, state, state, width], [out_c, out_c, out_b, out_b],
                 [_sds((SB_WIDTH, N_STATE), _F32)] * 2 + [_sds((N_STATE, SB_WIDTH), _F32)] * 2,
                 "ssm_weight_grads")(dy, x_re, x_im, lam_re, lam_im, u)


def _ssm_param_bwd(da_re_c, da_im_c, dbt_re, dbt_im, dct_re, dct_im,
                   lam_re_c, lam_im_c, ldt_c, b_re2, b_im2):
    def body(dar, dai, dbr, dbi, dcr, dci, lrc, lic, ldc, bre, bim,
             glr, gli, gdt, gbr, gbi, gcr, gci):
        own_b = ((_iota((N_STATE, SB_WIDTH), 0) >> 6) & 7) == (_iota((N_STATE, SB_WIDTH), 1) >> 4)
        own_c = (_iota((SB_WIDTH, SB_STATE), 0) >> 4) == (_iota((SB_WIDTH, SB_STATE), 1) >> 6)

        def fold_b(ref):
            t = jnp.where(own_b, ref[...], 0.0)
            for shift in (64, 32, 16):
                t = t + pltpu.roll(t, shift, 1)
            return t[:, :SSM_GROUP]

        def fold_c(ref, k):
            t = jnp.where(own_c, ref[:, _sb_state(k)], 0.0)
            t = sum(t[:, 128 * i:128 * (i + 1)] for i in range(SB_STATE // 128))
            return (t + pltpu.roll(t, SSM_STATE, 1))[:, :SSM_STATE]

        dbbr = fold_b(dbr)
        dbbi = fold_b(dbi)
        for k in range(SUPER):
            gcr[_sb_width(k), :] = fold_c(dcr, k)
            gci[_sb_width(k), :] = -fold_c(dci, k)
        _, vjp = jax.vjp(_s5_bbar, lrc[...], lic[...], ldc[...], bre[...], bim[...])
        d_lr, d_li, d_dt, d_br, d_bi = vjp((dar[...], dai[...], dbbr, dbbi))
        gbr[...] = d_br
        gbi[...] = d_bi
        groups = ((_iota((SSM_GROUPS, N_STATE), 1) >> 6) == _iota((SSM_GROUPS, N_STATE), 0)).astype(_F32)
        in_group = ((_iota((N_STATE, SSM_STATE), 0) & (SSM_STATE - 1)) == _iota((N_STATE, SSM_STATE), 1)).astype(_F32)
        glr[...] = _dot_exact(groups, d_lr * in_group, _NN)
        gli[...] = _dot_exact(groups, d_li * in_group, _NN)
        groups_t = ((_iota((N_STATE, SSM_GROUPS), 0) >> 6) == _iota((N_STATE, SSM_GROUPS), 1)).astype(_F32)
        gdt[...] = _dot_exact(jnp.broadcast_to(d_dt, (N_STATE, 128)), groups_t, _TN)[0:1]

    ins = [da_re_c, da_im_c, dbt_re, dbt_im, dct_re, dct_im, lam_re_c, lam_im_c, ldt_c, b_re2, b_im2]
    outs = [(SSM_GROUPS, SSM_STATE), (SSM_GROUPS, SSM_STATE), (1, SSM_GROUPS), (N_STATE, SSM_GROUP),
            (N_STATE, SSM_GROUP), (SSM_WIDTH, SSM_STATE), (SSM_WIDTH, SSM_STATE)]
    return _call(body, (1,), [_whole(a.shape) for a in ins], [_whole(s) for s in outs],
                 [_sds(s, _F32) for s in outs], "ssm_param_bwd")(*ins)


def _head_spread(j):
    r = _iota((KV_WIDTH, 256), 0)
    c = _iota((KV_WIDTH, 256), 1)
    return (r == HEAD_DIM * j + (c & (HEAD_DIM - 1))).astype(_BF16)


STACK = Q_PER_KV * BLOCK


def _stack_heads(t):
    lane_head = _iota((1, 256), 1) >> 6
    return jnp.concatenate([jnp.where(lane_head == g, t, jnp.zeros_like(t)) for g in range(Q_PER_KV)], axis=0)


def _unstack_heads(t):
    lane_head = _iota((1, 256), 1) >> 6
    return sum(jnp.where(lane_head == g, t[BLOCK * g:BLOCK * (g + 1)], 0.0) for g in range(Q_PER_KV))


def _stacked_sinks(sink_ref, j):
    block = _iota((STACK, 1), 0) >> 7
    col = jnp.full((STACK, 1), sink_ref[Q_PER_KV * j], _F32)
    for g in range(1, Q_PER_KV):
        col = jnp.where(block == g, sink_ref[Q_PER_KV * j + g], col)
    return col


def _fold_heads(t, j):
    t = t[:, :KV_WIDTH] + t[:, KV_WIDTH:]
    t = t + pltpu.roll(t, HEAD_DIM, 1)
    return jnp.where((_iota((1, KV_WIDTH), 1) >> 6) == j, t, 0.0)


def _attn_scores(q_stacked, kt, blk, sink):
    s = _dot(q_stacked, kt, _NT) * (HEAD_DIM ** -0.5)
    qi = _iota((STACK, 2 * BLOCK), 0) & (BLOCK - 1)
    kj = _iota((STACK, 2 * BLOCK), 1)
    rel = qi + BLOCK - kj
    valid = (rel >= 0) & (rel < BLOCK) & (blk * BLOCK - BLOCK + kj >= 0)
    s = jnp.where(valid, s, MASK_VALUE)
    m = jnp.maximum(jnp.max(s, axis=-1, keepdims=True), sink)
    p = jnp.exp(s - m)
    e_sink = jnp.exp(sink - m)
    den = jnp.sum(p, axis=-1, keepdims=True) + e_sink
    return p / den, e_sink / den


def _attn_specs():
    prev = lambda i: (jnp.maximum(i - 1, 0), 0)
    cur = lambda i: (i, 0)
    kv = [pl.BlockSpec((BLOCK, KV_WIDTH), prev), pl.BlockSpec((BLOCK, KV_WIDTH), cur)]
    return [pl.BlockSpec((BLOCK, ATTN_WIDTH), cur)] + kv + kv


def _attn_fwd(q, k, v, sinks, g_attn):
    L = q.shape[0]

    def body(q_ref, kp_ref, kc_ref, vp_ref, vc_ref, sink_ref, g_ref, o_ref, n_ref):
        blk = pl.program_id(0)
        kwin = jnp.concatenate([kp_ref[...], kc_ref[...]], axis=0)
        vwin = jnp.concatenate([vp_ref[...], vc_ref[...]], axis=0)
        halves = []
        for j in range(N_KV_HEADS):
            spread = _head_spread(j)
            kt = _dot(kwin, spread, _NN).astype(_BF16)
            vt = _dot(vwin, spread, _NN).astype(_BF16)
            qs = _stack_heads(q_ref[:, 256 * j:256 * (j + 1)])
            p, _ = _attn_scores(qs, kt, blk, _stacked_sinks(sink_ref, j))
            halves.append(_unstack_heads(_dot(p, vt, _NN)))
        o = jnp.concatenate(halves, axis=1)
        o_ref[...] = o
        n, _ = _rms_fwd(o, g_ref[...])
        n_ref[...] = n.astype(_BF16)

    cur = lambda i: (i, 0)
    return _call(body, (L // BLOCK,),
                 _attn_specs() + [pl.BlockSpec(memory_space=pltpu.SMEM), _whole((1, ATTN_WIDTH))],
                 [pl.BlockSpec((BLOCK, ATTN_WIDTH), cur)] * 2,
                 [_sds((L, ATTN_WIDTH), _F32), _sds((L, ATTN_WIDTH), _BF16)],
                 "attn_fwd")(q, k, k, v, v, sinks, g_attn)


def _attn_bwd(q, k, v, o, dn, sinks, g_attn):
    L = q.shape[0]

    def body(q_ref, kp_ref, kc_ref, vp_ref, vc_ref, o_ref, dn_ref, sink_ref, g_ref,
             dq_ref, dk_ref, dv_ref, dsink_ref, dg_ref):
        blk = pl.program_id(0)
        first = blk == 0

        @pl.when(first)
        def _():
            dk_ref[...] = jnp.zeros_like(dk_ref)
            dv_ref[...] = jnp.zeros_like(dv_ref)
            dsink_ref[...] = jnp.zeros_like(dsink_ref)

        o = o_ref[...]
        g = g_ref[...]
        _, r = _rms_fwd(o, g)
        do, dg = _rms_bwd(dn_ref[...], o, g, r)
        _accumulate(dg_ref, dg, first)
        kwin = jnp.concatenate([kp_ref[...], kc_ref[...]], axis=0)
        vwin = jnp.concatenate([vp_ref[...], vc_ref[...]], axis=0)
        lane = _iota((1, 128), 1)
        dsink = jnp.zeros((1, 128), _F32)
        dkwin = jnp.zeros((2 * BLOCK, KV_WIDTH), _F32)
        dvwin = jnp.zeros((2 * BLOCK, KV_WIDTH), _F32)
        dq_halves = []
        for j in range(N_KV_HEADS):
            spread = _head_spread(j)
            kt = _dot(kwin, spread, _NN).astype(_BF16)
            vt = _dot(vwin, spread, _NN).astype(_BF16)
            qs = _stack_heads(q_ref[:, 256 * j:256 * (j + 1)])
            dos = _stack_heads(do[:, 256 * j:256 * (j + 1)]).astype(_BF16)
            p, p_sink = _attn_scores(qs, kt, blk, _stacked_sinks(sink_ref, j))
            dp = _dot(dos, vt, _NT)
            delta = jnp.sum(p * dp, axis=-1, keepdims=True)
            ds = (p * (dp - delta) * (HEAD_DIM ** -0.5)).astype(_BF16)
            sink_term = p_sink * delta
            for g in range(Q_PER_KV):
                head_sum = jnp.sum(sink_term[BLOCK * g:BLOCK * (g + 1)], axis=0, keepdims=True)
                dsink = dsink - jnp.where(lane == Q_PER_KV * j + g, head_sum, 0.0)
            dvwin = dvwin + _fold_heads(_dot(p, dos, _TN), j)
            dkwin = dkwin + _fold_heads(_dot(ds, qs, _TN), j)
            dq_halves.append(_unstack_heads(_dot(ds, kt, _NN)))
        dq_ref[...] = jnp.concatenate(dq_halves, axis=1)
        dsink_ref[...] += dsink
        prev = pl.ds(pl.multiple_of(jnp.maximum(blk - 1, 0) * BLOCK, BLOCK), BLOCK)
        cur = pl.ds(pl.multiple_of(blk * BLOCK, BLOCK), BLOCK)
        dk_ref[prev, :] += dkwin[:BLOCK]
        dk_ref[cur, :] += dkwin[BLOCK:]
        dv_ref[prev, :] += dvwin[:BLOCK]
        dv_ref[cur, :] += dvwin[BLOCK:]

    cur = lambda i: (i, 0)
    blk_q = pl.BlockSpec((BLOCK, ATTN_WIDTH), cur)
    return _call(body, (L // BLOCK,),
                 _attn_specs() + [blk_q, blk_q, pl.BlockSpec(memory_space=pltpu.SMEM), _whole((1, ATTN_WIDTH))],
                 [blk_q, _whole((L, KV_WIDTH)), _whole((L, KV_WIDTH)), _whole((1, 128)), _whole((1, ATTN_WIDTH))],
                 [_sds((L, ATTN_WIDTH), _F32), _sds((L, KV_WIDTH), _F32), _sds((L, KV_WIDTH), _F32),
                  _sds((1, 128), _F32), _sds((1, ATTN_WIDTH), _F32)],
                 "attn_bwd")(q, k, k, v, v, o, dn, sinks, g_attn)


def _out_proj(n_ssm, n_attn, x, w_out, g_post_mix, g_pre_ffn):
    L = x.shape[0]
    tm = _tile(L)

    def body(ns_ref, na_ref, x_ref, w_ref, g1_ref, g2_ref, merged_ref, mo_ref, h1_ref, hn2_ref):
        merged = jnp.concatenate([ns_ref[...], na_ref[...]], axis=1)
        merged_ref[...] = merged
        mo = _dot(merged, w_ref[...], _NN)
        mo_ref[...] = mo
        n, _ = _rms_fwd(mo, g1_ref[...])
        h1 = x_ref[...] + n
        h1_ref[...] = h1
        hn2, _ = _rms_fwd(h1, g2_ref[...])
        hn2_ref[...] = hn2.astype(_BF16)

    row = _whole((1, D_MODEL))
    return _call(body, (L // tm,),
                 [_rows(tm, SSM_WIDTH), _rows(tm, ATTN_WIDTH), _rows(tm, D_MODEL), _whole((D_MODEL, D_MODEL)), row, row],
                 [_rows(tm, D_MODEL)] * 4,
                 [_sds((L, D_MODEL), _BF16), _sds((L, D_MODEL), _F32), _sds((L, D_MODEL), _F32), _sds((L, D_MODEL), _BF16)],
                 "out_proj")(n_ssm, n_attn, x, w_out, g_post_mix, g_pre_ffn)


def _ffn(hn2, h1, target, w_gate_up, w_down, g_pre_ffn, g_post_ffn):
    L = h1.shape[0]
    tm = _tile(L)
    half = D_FF // 2

    def body(hn2_ref, h1_ref, tgt_ref, wgu_hbm, wd_hbm, g2_ref, g3_ref,
             act_ref, dgu_ref, dff_ref, dh1_ref, loss_ref, dg3_ref, dg2_ref,
             wgu, wd, gu, sem):
        first = pl.program_id(0) == 0

        @pl.when(first)
        def _():
            c1 = pltpu.make_async_copy(wgu_hbm, wgu, sem.at[0])
            c2 = pltpu.make_async_copy(wd_hbm, wd, sem.at[1])
            c1.start()
            c2.start()
            c1.wait()
            c2.wait()

        hn2 = hn2_ref[...]
        ff = jnp.zeros((tm, D_MODEL), _F32)
        for c in range(2):
            gate = _dot(hn2, wgu[half * c:half * (c + 1), :], _NT)
            up = _dot(hn2, wgu[D_FF + half * c:D_FF + half * (c + 1), :], _NT)
            gu[:, half * c:half * (c + 1)] = gate
            gu[:, D_FF + half * c:D_FF + half * (c + 1)] = up
            act = (gate * jax.nn.sigmoid(gate) * up).astype(_BF16)
            act_ref[:, half * c:half * (c + 1)] = act
            ff = ff + _dot(act, wd[half * c:half * (c + 1), :], _NN)
        g3 = g3_ref[...]
        n, r = _rms_fwd(ff, g3)
        h1 = h1_ref[...]
        err = h1 + n - tgt_ref[...]
        loss = 0.5 * jnp.sum(jnp.mean(err * err, axis=-1, keepdims=True), axis=0, keepdims=True)
        _accumulate(loss_ref, jnp.broadcast_to(loss, (1, 128)), first)
        dh2 = err * (1.0 / D_MODEL)
        dff, dg3 = _rms_bwd(dh2, ff, g3, r)
        _accumulate(dg3_ref, dg3, first)
        dffb = dff.astype(_BF16)
        dff_ref[...] = dffb
        dhn2 = jnp.zeros((tm, D_MODEL), _F32)
        for c in range(2):
            dact = _dot(dffb, wd[half * c:half * (c + 1), :], _NT)
            gate = gu[:, half * c:half * (c + 1)]
            up = gu[:, D_FF + half * c:D_FF + half * (c + 1)]
            sig = jax.nn.sigmoid(gate)
            silu = gate * sig
            dgate = (dact * up * (sig + silu * (1.0 - sig))).astype(_BF16)
            dup = (dact * silu).astype(_BF16)
            dgu_ref[:, half * c:half * (c + 1)] = dgate
            dgu_ref[:, D_FF + half * c:D_FF + half * (c + 1)] = dup
            dhn2 = dhn2 + _dot(dgate, wgu[half * c:half * (c + 1), :], _NN)
            dhn2 = dhn2 + _dot(dup, wgu[D_FF + half * c:D_FF + half * (c + 1), :], _NN)
        g2 = g2_ref[...]
        _, r2 = _rms_fwd(h1, g2)
        dh1, dg2 = _rms_bwd(dhn2, h1, g2, r2)
        _accumulate(dg2_ref, dg2, first)
        dh1_ref[...] = dh2 + dh1

    row = _whole((1, D_MODEL))
    anyspace = pl.BlockSpec(memory_space=pl.ANY)
    return _call(body, (L // tm,),
                 [_rows(tm, D_MODEL), _rows(tm, D_MODEL), _rows(tm, D_MODEL), anyspace, anyspace, row, row],
                 [_rows(tm, D_FF), _rows(tm, 2 * D_FF), _rows(tm, D_MODEL), _rows(tm, D_MODEL),
                  _whole((1, 128)), row, row],
                 [_sds((L, D_FF), _BF16), _sds((L, 2 * D_FF), _BF16), _sds((L, D_MODEL), _BF16),
                  _sds((L, D_MODEL), _F32), _sds((1, 128), _F32), _sds((1, D_MODEL), _F32), _sds((1, D_MODEL), _F32)],
                 "ffn",
                 scratch=[pltpu.VMEM((2 * D_FF, D_MODEL), _BF16), pltpu.VMEM((D_FF, D_MODEL), _BF16),
                          pltpu.VMEM((tm, 2 * D_FF), _F32), pltpu.SemaphoreType.DMA((2,))],
                 )(hn2, h1, target, w_gate_up, w_down, g_pre_ffn, g_post_ffn)


def _out_proj_bwd(dh1, mo, w_out, g_post_mix, tokens=()):
    L = dh1.shape[0]
    tm = _tile(L)

    def body(dh1_ref, mo_ref, w_ref, g_ref, dmo_ref, dns_ref, dna_ref, dg_ref):
        first = pl.program_id(0) == 0
        mo = mo_ref[...]
        g = g_ref[...]
        _, r = _rms_fwd(mo, g)
        dmo, dg = _rms_bwd(dh1_ref[...], mo, g, r)
        _accumulate(dg_ref, dg, first)
        dmob = dmo.astype(_BF16)
        dmo_ref[...] = dmob
        dmerged = _dot(dmob, w_ref[...], _NT)
        dns_ref[...] = dmerged[:, :SSM_WIDTH]
        dna_ref[...] = dmerged[:, SSM_WIDTH:]

    row = _whole((1, D_MODEL))
    return _call(body, (L // tm,),
                 [_rows(tm, D_MODEL), _rows(tm, D_MODEL), _whole((D_MODEL, D_MODEL)), row],
                 [_rows(tm, D_MODEL), _rows(tm, SSM_WIDTH), _rows(tm, ATTN_WIDTH), row],
                 [_sds((L, D_MODEL), _BF16), _sds((L, SSM_WIDTH), _F32), _sds((L, ATTN_WIDTH), _F32),
                  _sds((1, D_MODEL), _F32)],
                 "out_proj_bwd", tokens=tokens)(dh1, mo, w_out, g_post_mix)


def _in_proj_bwd(du, dq, dk, dv, cos_t, sin_t, x, dh1, g_pre_mix, w_in, tokens=()):
    L = x.shape[0]
    tm = _tile(L)

    def body(du_ref, dq_ref, dk_ref, dv_ref, cos_ref, sin_ref, x_ref, dh1_ref, g_ref, w_ref,
             dproj_ref, dx_ref, dg_ref):
        first = pl.program_id(0) == 0
        cos_v, sin_v = cos_ref[...], sin_ref[...]
        dproj = jnp.concatenate([du_ref[...], _rope_transpose(dq_ref[...], cos_v, sin_v),
                                 _rope_transpose(dk_ref[...], cos_v, sin_v), dv_ref[...]], axis=1).astype(_BF16)
        dproj_ref[...] = dproj
        dhn = _dot(dproj, w_ref[...], _NN)
        x = x_ref[...]
        g = g_ref[...]
        _, r = _rms_fwd(x, g)
        dx, dg = _rms_bwd(dhn, x, g, r)
        _accumulate(dg_ref, dg, first)
        dx_ref[...] = dh1_ref[...] + dx

    row = _whole((1, D_MODEL))
    return _call(body, (L // tm,),
                 [_rows(tm, SSM_WIDTH), _rows(tm, ATTN_WIDTH), _rows(tm, KV_WIDTH), _rows(tm, KV_WIDTH),
                  _rows(tm, KV_WIDTH), _rows(tm, KV_WIDTH), _rows(tm, D_MODEL), _rows(tm, D_MODEL), row,
                  _whole((IN_WIDTH, D_MODEL))],
                 [_rows(tm, IN_WIDTH), _rows(tm, D_MODEL), row],
                 [_sds((L, IN_WIDTH), _BF16), _sds((L, D_MODEL), _F32), _sds((1, D_MODEL), _F32)],
                 "in_proj_bwd", tokens=tokens)(du, dq, dk, dv, cos_t, sin_t, x, dh1, g_pre_mix, w_in)


def _matmul_tn(a, b, out_dtype, name, scale=1.0):
    K, M = a.shape
    N = b.shape[1]
    tm = next(t for t in (512, 256, 128) if M % t == 0)
    tn = next(t for t in (512, 256, 128) if N % t == 0)

    def body(a_ref, b_ref, o_ref):
        acc = _dot(a_ref[...], b_ref[...], _TN)
        o_ref[...] = (acc if scale == 1.0 else acc * scale).astype(out_dtype)

    params = pltpu.CompilerParams(dimension_semantics=("arbitrary", "arbitrary"), vmem_limit_bytes=VMEM_LIMIT)
    return pl.pallas_call(body, grid=(M // tm, N // tn),
                          in_specs=[pl.BlockSpec((K, tm), lambda i, j: (0, i)),
                                    pl.BlockSpec((K, tn), lambda i, j: (0, j))],
                          out_specs=pl.BlockSpec((tm, tn), lambda i, j: (i, j)),
                          out_shape=_sds((M, N), out_dtype), compiler_params=params, name=name)(a, b)


def _to_chunked(a):
    L, n = a.shape
    return a.reshape(SCAN_CHUNKS, L // SCAN_CHUNKS, n).transpose(1, 0, 2).reshape(L, n)


def _from_chunked(a):
    L, n = a.shape
    return a.reshape(L // SCAN_CHUNKS, SCAN_CHUNKS, n).transpose(1, 0, 2).reshape(L, n)


def _local_step(x, pos, target, p, fetch, publish):
    L = x.shape[0]
    T = L // SCAN_CHUNKS
    cos_t, sin_t = _rope_tables(pos.reshape(L, 1))
    w_in, = fetch(("w_in",), None)
    hn, u, q, k, v = _in_proj(x, p["g_pre_mix"], w_in, cos_t, sin_t)

    lam_re_r = p["ssm_lambda_re"].reshape(1, N_STATE)
    lam_im_r = p["ssm_lambda_im"].reshape(1, N_STATE)
    ldt_r = jnp.broadcast_to(p["ssm_log_dt"].reshape(SSM_GROUPS, 1), (SSM_GROUPS, SSM_STATE)).reshape(1, N_STATE)
    lam_re_c, lam_im_c, ldt_c = (a.reshape(N_STATE, 1) for a in (lam_re_r, lam_im_r, ldt_r))
    b_re2 = p["ssm_b_re"].reshape(N_STATE, SSM_GROUP)
    b_im2 = p["ssm_b_im"].reshape(N_STATE, SSM_GROUP)
    c_re2 = p["ssm_c_re"].reshape(SSM_WIDTH, SSM_STATE)
    c_im2 = p["ssm_c_im"].reshape(SSM_WIDTH, SSM_STATE)
    d_row = p["ssm_d"].reshape(1, SSM_WIDTH)
    a_re, a_im, bt_re, bt_im, ct_re, ct_im = _ssm_prep(
        lam_re_r, lam_im_r, ldt_r, lam_re_c, lam_im_c, ldt_c, b_re2, b_im2, c_re2, c_im2)

    u_c = _to_chunked(u)
    bu_re, bu_im = _ssm_bu(u_c, bt_re, bt_im)
    x_re, x_im = _scan_fwd(bu_re.reshape(T, SCAN_CHUNKS, N_STATE), bu_im.reshape(T, SCAN_CHUNKS, N_STATE), a_re, a_im)
    w_glu, = fetch(("w_glu",), x_re)
    y, z, n_ssm_c = _ssm_out(x_re.reshape(L, N_STATE), x_im.reshape(L, N_STATE), u_c, ct_re, ct_im, d_row,
                             w_glu, p["b_glu"], p["g_ssm_out"])
    n_ssm = _from_chunked(n_ssm_c)

    sinks = p["attn_sinks"].reshape(N_Q_HEADS)
    o, n_attn = _attn_fwd(q, k, v, sinks, p["g_attn_out"])
    w_out, = fetch(("w_out",), n_attn)
    merged, mo, h1, hn2 = _out_proj(n_ssm, n_attn, x, w_out, p["g_post_mix"], p["g_pre_ffn"])
    w_gate_up, w_down = fetch(("w_gate_up", "w_down"), hn2)
    act, dgu, dff, dh1, loss, dg_post_ffn, dg_pre_ffn = _ffn(
        hn2, h1, target, w_gate_up, w_down, p["g_pre_ffn"], p["g_post_ffn"])
    grads = {"g_post_ffn": dg_post_ffn, "g_pre_ffn": dg_pre_ffn}
    tokens = publish({"w_down": _matmul_tn(act, dff, _BF16, "grad_w_down"),
                      "w_gate_up": _matmul_tn(dgu, hn2, _BF16, "grad_w_gate_up")})

    dmo, dn_ssm, dn_attn, grads["g_post_mix"] = _out_proj_bwd(dh1, mo, w_out, p["g_post_mix"], tokens)
    grad_w_out = _matmul_tn(merged, dmo, _BF16, "grad_w_out")

    dq, dk, dv, dsink, grads["g_attn_out"] = _attn_bwd(q, k, v, o, dn_attn, sinks, p["g_attn_out"])
    grads["attn_sinks"] = dsink

    gy, dz, dy, dud, dx_re, dx_im, grads["g_ssm_out"], grads["b_glu"], dd = _ssm_out_bwd(
        _to_chunked(dn_ssm), y, z, u_c, ct_re, ct_im, d_row, w_glu, p["g_ssm_out"])
    grads["ssm_d"] = dd.reshape(SSM_GROUPS, SSM_GROUP)
    tokens = publish({"w_out": grad_w_out, "w_glu": _matmul_tn(dz, gy, _BF16, "grad_w_glu")})
    lam_re, lam_im, da_re, da_im = _scan_bwd(dx_re.reshape(T, SCAN_CHUNKS, N_STATE), dx_im.reshape(T, SCAN_CHUNKS, N_STATE),
                                             x_re, x_im, a_re, a_im, tokens)
    lam_re = lam_re.reshape(L, N_STATE)
    lam_im = lam_im.reshape(L, N_STATE)
    dct_re, dct_im, dbt_re, dbt_im = _ssm_weight_grads(
        dy, x_re.reshape(L, N_STATE), x_im.reshape(L, N_STATE), lam_re, lam_im, u_c)
    g_lr, g_li, g_dt, g_br, g_bi, g_cr, g_ci = _ssm_param_bwd(
        da_re.reshape(N_STATE, 1), da_im.reshape(N_STATE, 1), dbt_re, dbt_im, dct_re, dct_im,
        lam_re_c, lam_im_c, ldt_c, b_re2, b_im2)
    grads.update(ssm_lambda_re=g_lr, ssm_lambda_im=g_li, ssm_log_dt=g_dt, ssm_b_re=g_br, ssm_b_im=g_bi,
                 ssm_c_re=g_cr, ssm_c_im=g_ci, loss=loss)
    publish(grads)

    du = _from_chunked(_ssm_du(lam_re, lam_im, bt_re, bt_im, dud))
    dproj, grad_x, g_pre_mix = _in_proj_bwd(du, dq, dk, dv, cos_t, sin_t, x, dh1, p["g_pre_mix"], w_in, [g_dt])
    publish({"g_pre_mix": g_pre_mix, "w_in": _matmul_tn(dproj, hn, _BF16, "grad_w_in")})
    return grad_x


_MESH = pl.DeviceIdType.MESH
_PEERS = N_DEV - 1


def _mesh_pos():
    return lax.axis_index("x"), lax.axis_index("y"), lax.axis_index("c")


def _dev_index(px, py, pc):
    return 4 * px + 2 * py + pc


def _all_gather(shards, out_dtype, name):
    n = len(shards)

    def body(*refs):
        ins, outs, stages = refs[:n], refs[n:2 * n], refs[2 * n:3 * n]
        send_sems, recv_sems, local_sems = refs[3 * n:]
        x, y, c = _mesh_pos()
        me, sibling = (x, y, c), (x, y, 1 - c)
        chips = [(1 - x, y), (x, 1 - y), (1 - x, 1 - y)]

        def copy(w, k, block, to, src=None):
            slot = outs[w].at[_dev_index(*block)]
            return pltpu.make_async_remote_copy(
                src_ref=slot if src is None else src, dst_ref=slot,
                send_sem=send_sems.at[_PEERS * w + k], recv_sem=recv_sems.at[_PEERS * w + k],
                device_id=to, device_id_type=_MESH)

        for w in range(n):
            stages[w][...] = ins[w][...].astype(out_dtype)
        mine, first, passed = [], [], []
        for w in range(n):
            cp = pltpu.make_async_copy(stages[w], outs[w].at[_dev_index(*me)], local_sems.at[w])
            cp.start()
            mine.append(cp)
            sends = [copy(w, 0, me, sibling, src=stages[w])]
            sends += [copy(w, 1 + j, me, (*chip, c), src=stages[w]) for j, chip in enumerate(chips)]
            for cp in sends:
                cp.start()
            first += sends
        for w in range(n):
            for j, chip in enumerate(chips):
                copy(w, 1 + j, (*chip, c), me).wait_recv()
                cp = copy(w, 4 + j, (*chip, c), sibling)
                cp.start()
                passed.append(cp)
        for w in range(n):
            copy(w, 0, sibling, me).wait_recv()
            for j, chip in enumerate(chips):
                copy(w, 4 + j, (*chip, 1 - c), me).wait_recv()
        for cp in first + passed:
            cp.wait_send()
        for cp in mine:
            cp.wait()

    return pl.pallas_call(
        body, name=name,
        out_shape=[_sds((N_DEV,) + s.shape, out_dtype) for s in shards],
        in_specs=[pl.BlockSpec(memory_space=pltpu.VMEM)] * n,
        out_specs=[pl.BlockSpec(memory_space=pl.ANY)] * n,
        scratch_shapes=[pltpu.VMEM(s.shape, out_dtype) for s in shards]
        + [pltpu.SemaphoreType.DMA((_PEERS * n,)), pltpu.SemaphoreType.DMA((_PEERS * n,)),
           pltpu.SemaphoreType.DMA((n,))],
        compiler_params=pltpu.CompilerParams(vmem_limit_bytes=VMEM_LIMIT),
    )(*shards)


_HBM_SPEC = pl.BlockSpec(memory_space=pltpu.HBM)
_SEM_SPEC = pl.BlockSpec(memory_space=pltpu.SEMAPHORE)
_DATAFLOW = pltpu.SideEffectType.DATAFLOW_SIDE_EFFECTING


def _peer(x, y, c, r):
    return (x ^ ((r >> 2) & 1), y ^ ((r >> 1) & 1), c ^ (r & 1))


def _hbm(a):
    return pltpu.with_memory_space_constraint(a, pltpu.HBM)


def _send_start(sources, blocked, name):
    n = len(sources)
    lands = [lax.empty((N_DEV,) + (s.shape[1:] if blocked else s.shape), s.dtype) for s in sources]

    def body(*refs):
        srcs, zones = refs[:n], refs[n:2 * n]
        send_sems, recv_sems = refs[2 * n:3 * n], refs[3 * n:4 * n]
        token, local_sems = refs[6 * n], refs[6 * n + 1]
        x, y, c = _mesh_pos()
        me = _dev_index(x, y, c)
        local = []
        for w in range(n):
            cp = pltpu.make_async_copy(srcs[w].at[me] if blocked else srcs[w], zones[w].at[me], local_sems.at[w])
            cp.start()
            local.append(cp)
            for r in range(1, N_DEV):
                peer = _peer(x, y, c, r)
                pltpu.make_async_remote_copy(
                    src_ref=srcs[w].at[_dev_index(*peer)] if blocked else srcs[w], dst_ref=zones[w].at[me],
                    send_sem=send_sems[w].at[r - 1], recv_sem=recv_sems[w].at[r - 1],
                    device_id=peer, device_id_type=_MESH).start()
        for cp in local:
            cp.wait()
        token[...] = jnp.zeros_like(token)

    sems = [pltpu.SemaphoreType.DMA((_PEERS,))] * (2 * n)
    out = pl.pallas_call(
        body, name=name,
        out_shape=sems + [pltpu.HBM(a.shape, a.dtype) for a in list(sources) + lands] + [_sds((8, 128), _F32)],
        in_specs=[_HBM_SPEC] * (2 * n),
        out_specs=[_SEM_SPEC] * (2 * n) + [_HBM_SPEC] * (2 * n) + [pl.BlockSpec(memory_space=pltpu.VMEM)],
        input_output_aliases={i: 2 * n + i for i in range(2 * n)},
        scratch_shapes=[pltpu.SemaphoreType.DMA((n,))],
        compiler_params=pltpu.CompilerParams(has_side_effects=_DATAFLOW),
    )(*[_hbm(a) for a in sources], *[_hbm(a) for a in lands])
    return out[:n], out[n:2 * n], out[2 * n:3 * n], out[3 * n:4 * n], out[4 * n]


def _send_wait(send_sems, recv_sems, sources, lands, after, blocked, name):
    n = len(sources)

    def body(*refs):
        srcs, zones = refs[:n], refs[n:2 * n]
        sends, recvs = refs[2 * n:3 * n], refs[3 * n:4 * n]
        x, y, c = _mesh_pos()
        for w in range(n):
            for r in range(1, N_DEV):
                peer = _peer(x, y, c, r)
                idx = _dev_index(*peer)
                cp = pltpu.make_async_remote_copy(
                    src_ref=srcs[w].at[idx] if blocked else srcs[w], dst_ref=zones[w].at[idx],
                    send_sem=sends[w].at[r - 1], recv_sem=recvs[w].at[r - 1],
                    device_id=peer, device_id_type=_MESH)
                cp.wait_send()
                cp.wait_recv()

    out = pl.pallas_call(
        body, name=name,
        out_shape=[pltpu.HBM(a.shape, a.dtype) for a in list(sources) + list(lands)],
        in_specs=[_HBM_SPEC] * (2 * n) + [_SEM_SPEC] * (2 * n) + [pl.BlockSpec(memory_space=pl.ANY)],
        out_specs=[_HBM_SPEC] * (2 * n),
        input_output_aliases={i: i for i in range(2 * n)},
        compiler_params=pltpu.CompilerParams(has_side_effects=_DATAFLOW),
    )(*sources, *lands, *send_sems, *recv_sems, after)
    return out[n:]


def _sequencer_exchange(sources, blocked, name, collective_id):
    n = len(sources)
    flags = blocked

    def body(*refs):
        srcs, zones = refs[:n], refs[n:2 * n]
        send_sems, recv_sems, local_sems = refs[2 * n:]
        x, y, c = _mesh_pos()
        me = _dev_index(x, y, c)
        barrier = pltpu.get_barrier_semaphore()
        for r in range(1, N_DEV):
            pl.semaphore_signal(barrier, inc=1, device_id=_peer(x, y, c, r), device_id_type=_MESH)
        pl.semaphore_wait(barrier, _PEERS)
        local, sends, recvs = [], [], []
        for w in range(n):
            cp = pltpu.make_async_copy(srcs[w].at[me] if flags[w] else srcs[w], zones[w].at[me], local_sems.at[w])
            cp.start()
            local.append(cp)
            for r in range(1, N_DEV):
                peer = _peer(x, y, c, r)
                idx = _dev_index(*peer)
                k = _PEERS * w + r - 1
                src = srcs[w].at[idx] if flags[w] else srcs[w]
                send = pltpu.make_async_remote_copy(
                    src_ref=src, dst_ref=zones[w].at[me], send_sem=send_sems.at[k], recv_sem=recv_sems.at[k],
                    device_id=peer, device_id_type=_MESH)
                send.start()
                sends.append(send)
                recvs.append(pltpu.make_async_remote_copy(
                    src_ref=src, dst_ref=zones[w].at[idx], send_sem=send_sems.at[k], recv_sem=recv_sems.at[k],
                    device_id=peer, device_id_type=_MESH))
        for cp in recvs:
            cp.wait_recv()
        for cp in sends:
            cp.wait_send()
        for cp in local:
            cp.wait()

    return pl.kernel(
        body, name=name,
        out_type=[_sds((N_DEV,) + (s.shape[1:] if f else s.shape), s.dtype) for s, f in zip(sources, flags)],
        mesh=plsc.ScalarSubcoreMesh(axis_name="sequencer", num_cores=1),
        scratch_types=[pltpu.SemaphoreType.DMA((_PEERS * n,)), pltpu.SemaphoreType.DMA((_PEERS * n,)),
                       pltpu.SemaphoreType.DMA((n,))],
        compiler_params=pltpu.CompilerParams(collective_id=collective_id),
    )(*sources)


def _sequencer_gather(shards, name, collective_id):
    n = len(shards)
    fan = 4

    def body(*refs):
        srcs, zones = refs[:n], refs[n:2 * n]
        send_sems, recv_sems, local_sems = refs[2 * n:]
        x, y, c = _mesh_pos()
        me, sibling = (x, y, c), (x, y, 1 - c)
        chips = [(1 - x, y), (x, 1 - y), (1 - x, 1 - y)]
        barrier = pltpu.get_barrier_semaphore()
        for peer in [sibling] + [(*chip, c) for chip in chips]:
            pl.semaphore_signal(barrier, inc=1, device_id=peer, device_id_type=_MESH)
        pl.semaphore_wait(barrier, fan)

        def copy(w, k, block, to, src=None):
            slot = zones[w].at[_dev_index(*block)]
            return pltpu.make_async_remote_copy(
                src_ref=slot if src is None else src, dst_ref=slot,
                send_sem=send_sems.at[_PEERS * w + k], recv_sem=recv_sems.at[_PEERS * w + k],
                device_id=to, device_id_type=_MESH)

        mine, first, passed = [], [], []
        for w in range(n):
            cp = pltpu.make_async_copy(srcs[w], zones[w].at[_dev_index(*me)], local_sems.at[w])
            cp.start()
            mine.append(cp)
            sends = [copy(w, 0, me, sibling, src=srcs[w])]
            sends += [copy(w, 1 + j, me, (*chip, c), src=srcs[w]) for j, chip in enumerate(chips)]
            for cp in sends:
                cp.start()
            first += sends
        for w in range(n):
            for j, chip in enumerate(chips):
                copy(w, 1 + j, (*chip, c), me).wait_recv()
                cp = copy(w, fan + j, (*chip, c), sibling)
                cp.start()
                passed.append(cp)
        for w in range(n):
            copy(w, 0, sibling, me).wait_recv()
            for j, chip in enumerate(chips):
                copy(w, fan + j, (*chip, 1 - c), me).wait_recv()
        for cp in first + passed:
            cp.wait_send()
        for cp in mine:
            cp.wait()

    return pl.kernel(
        body, name=name, out_type=[_sds((N_DEV,) + s.shape, s.dtype) for s in shards],
        mesh=plsc.ScalarSubcoreMesh(axis_name="sequencer", num_cores=1),
        scratch_types=[pltpu.SemaphoreType.DMA((_PEERS * n,)), pltpu.SemaphoreType.DMA((_PEERS * n,)),
                       pltpu.SemaphoreType.DMA((n,))],
        compiler_params=pltpu.CompilerParams(collective_id=collective_id),
    )(*shards)


def _row_tile(rows):
    return next(t for t in range(min(rows, 256), 0, -16) if rows % t == 0)


def _sum_parts(parts, name, tokens=()):
    _, rows, cols = parts.shape
    tr = _row_tile(rows)

    def body(p_ref, g_ref):
        g = p_ref[0].astype(_F32)
        for s in range(1, N_DEV):
            g = g + p_ref[s].astype(_F32)
        g_ref[...] = g

    return _call(body, (rows // tr,), [pl.BlockSpec((N_DEV, tr, cols), lambda i: (0, i, 0))],
                 _rows(tr, cols), _sds((rows, cols), _F32), name, tokens=tokens)(parts)


def _adam_update(g, w, m, v):
    new_m = ADAM_B1 * m + (1.0 - ADAM_B1) * g
    new_v = ADAM_B2 * v + (1.0 - ADAM_B2) * (g * g)
    m_hat = new_m / (1.0 - ADAM_B1 ** ADAM_STEP)
    v_hat = new_v / (1.0 - ADAM_B2 ** ADAM_STEP)
    return -ADAM_LR * (m_hat / (jnp.sqrt(v_hat) + ADAM_EPS) + ADAM_WD * w), new_m, new_v


def _adamw_small(parts, items, sums, name, tokens=()):
    n_p, n_i = len(parts), len(items)

    def body(*refs):
        p_refs, state, outs = refs[:n_p], refs[n_p:n_p + 3 * n_i], refs[n_p + 3 * n_i:]

        def total(part, rows, cols):
            g = p_refs[part][0, rows, cols]
            for s in range(1, N_DEV):
                g = g + p_refs[part][s, rows, cols]
            return g

        for i, (part, rows, cols, _, _, _) in enumerate(items):
            g = total(part, rows, cols)
            w_ref, m_ref, v_ref = state[3 * i:3 * i + 3]
            delta, new_m, new_v = _adam_update(g, w_ref[...], m_ref[...], v_ref[...])
            outs[4 * i][...] = g
            outs[4 * i + 1][...] = delta
            outs[4 * i + 2][...] = new_m
            outs[4 * i + 3][...] = new_v
        for j, (part, rows, cols) in enumerate(sums):
            outs[4 * n_i + j][...] = total(part, rows, cols)

    ins = list(parts) + [a for item in items for a in item[3:]]
    out_shapes = [item[3].shape for item in items for _ in range(4)]
    out_shapes += [(rows.stop - rows.start, cols.stop - cols.start) for _, rows, cols in sums]
    out = _call(body, (1,), [_whole(a.shape) for a in ins], [_whole(s) for s in out_shapes],
                [_sds(s, _F32) for s in out_shapes], name, tokens=tokens)(*ins)
    return [out[4 * i:4 * i + 4] for i in range(n_i)], out[4 * n_i:]


def _adamw(parts, w, m, v, name, tokens=()):
    rows, cols = w.shape
    tr = _row_tile(rows)
    n_parts = parts.shape[0]

    def body(p_ref, w_ref, m_ref, v_ref, g_ref, d_ref, nm_ref, nv_ref):
        g = p_ref[0].astype(_F32)
        for s in range(1, n_parts):
            g = g + p_ref[s].astype(_F32)
        new_m = ADAM_B1 * m_ref[...] + (1.0 - ADAM_B1) * g
        new_v = ADAM_B2 * v_ref[...] + (1.0 - ADAM_B2) * (g * g)
        m_hat = new_m / (1.0 - ADAM_B1 ** ADAM_STEP)
        v_hat = new_v / (1.0 - ADAM_B2 ** ADAM_STEP)
        g_ref[...] = g
        d_ref[...] = -ADAM_LR * (m_hat / (jnp.sqrt(v_hat) + ADAM_EPS) + ADAM_WD * w_ref[...])
        nm_ref[...] = new_m
        nv_ref[...] = new_v

    blk = _rows(tr, cols)
    return _call(body, (rows // tr,),
                 [pl.BlockSpec((n_parts, tr, cols), lambda i: (0, i, 0)), blk, blk, blk],
                 [blk] * 4, [_sds((rows, cols), _F32)] * 4, name, tokens=tokens)(parts, w, m, v)


_SMALL = ("g_pre_mix", "ssm_lambda_re", "ssm_lambda_im", "ssm_log_dt", "ssm_b_re", "ssm_b_im",
          "ssm_c_re", "ssm_c_im", "ssm_d", "b_glu", "attn_sinks", "g_ssm_out", "g_attn_out",
          "g_post_mix", "g_pre_ffn", "g_post_ffn")
_BIG = ("w_in", "w_glu", "w_out", "w_gate_up", "w_down")
_WEIGHTS = ("g_pre_mix", "w_in", "ssm_lambda_re", "ssm_lambda_im", "ssm_log_dt", "ssm_b_re", "ssm_b_im",
            "ssm_c_re", "ssm_c_im", "ssm_d", "w_glu", "b_glu", "attn_sinks", "g_ssm_out", "g_attn_out",
            "w_out", "g_post_mix", "g_pre_ffn", "w_gate_up", "w_down", "g_post_ffn")
_LANES = 128


_SHAPE_2D = {
    "g_pre_mix": (1, D_MODEL), "ssm_lambda_re": (SSM_GROUPS, SSM_STATE), "ssm_lambda_im": (SSM_GROUPS, SSM_STATE),
    "ssm_log_dt": (1, SSM_GROUPS), "ssm_b_re": (N_STATE, SSM_GROUP), "ssm_b_im": (N_STATE, SSM_GROUP),
    "ssm_c_re": (SSM_WIDTH, SSM_STATE), "ssm_c_im": (SSM_WIDTH, SSM_STATE), "ssm_d": (SSM_GROUPS, SSM_GROUP),
    "b_glu": (1, 2 * SSM_WIDTH), "attn_sinks": (1, N_Q_HEADS), "g_ssm_out": (1, SSM_WIDTH),
    "g_attn_out": (1, ATTN_WIDTH), "g_post_mix": (1, D_MODEL), "g_pre_ffn": (1, D_MODEL), "g_post_ffn": (1, D_MODEL)}
_ROW_WIDTH = {"g_pre_mix": D_MODEL, "b_glu": 2 * SSM_WIDTH, "attn_sinks": _LANES, "g_ssm_out": SSM_WIDTH,
              "g_attn_out": ATTN_WIDTH, "g_post_mix": D_MODEL, "g_pre_ffn": D_MODEL, "g_post_ffn": D_MODEL,
              "loss": _LANES}
_DENSE = ("ssm_b_re", "ssm_b_im", "ssm_c_re", "ssm_c_im")


def _row_slots(names):
    slots, row, col = {}, 0, 0
    for n in names:
        width = _ROW_WIDTH[n]
        if col + width > D_MODEL:
            row, col = row + 1, 0
        slots[n] = (row, col, width)
        col += width
    return slots


def _stack_rows(named, slots):
    n_rows = -(-(max(r for r, _, _ in slots.values()) + 1) // 8) * 8
    lines = []
    for r in range(n_rows):
        pieces = [named[n] for n, (row, _, _) in slots.items() if row == r]
        used = sum(p.shape[1] for p in pieces)
        if used < D_MODEL:
            pieces.append(jnp.zeros((1, D_MODEL - used), _F32))
        lines.append(jnp.concatenate(pieces, axis=1) if len(pieces) > 1 else pieces[0])
    return jnp.concatenate(lines, axis=0)


def kernel(x, positions, g_pre_mix, w_in, ssm_lambda_re, ssm_lambda_im, ssm_log_dt, ssm_b_re, ssm_b_im, ssm_c_re, ssm_c_im, ssm_d, w_glu, b_glu, attn_sinks, g_ssm_out, g_attn_out, w_out, g_post_mix, g_pre_ffn, w_gate_up, w_down, g_post_ffn, loss_target, m_g_pre_mix, m_w_in, m_ssm_lambda_re, m_ssm_lambda_im, m_ssm_log_dt, m_ssm_b_re, m_ssm_b_im, m_ssm_c_re, m_ssm_c_im, m_ssm_d, m_w_glu, m_b_glu, m_attn_sinks, m_g_ssm_out, m_g_attn_out, m_w_out, m_g_post_mix, m_g_pre_ffn, m_w_gate_up, m_w_down, m_g_post_ffn, v_g_pre_mix, v_w_in, v_ssm_lambda_re, v_ssm_lambda_im, v_ssm_log_dt, v_ssm_b_re, v_ssm_b_im, v_ssm_c_re, v_ssm_c_im, v_ssm_d, v_w_glu, v_b_glu, v_attn_sinks, v_g_ssm_out, v_g_attn_out, v_w_out, v_g_post_mix, v_g_pre_ffn, v_w_gate_up, v_w_down, v_g_post_ffn):
    w = dict(g_pre_mix=g_pre_mix, w_in=w_in, ssm_lambda_re=ssm_lambda_re, ssm_lambda_im=ssm_lambda_im,
             ssm_log_dt=ssm_log_dt, ssm_b_re=ssm_b_re, ssm_b_im=ssm_b_im, ssm_c_re=ssm_c_re, ssm_c_im=ssm_c_im,
             ssm_d=ssm_d, w_glu=w_glu, b_glu=b_glu, attn_sinks=attn_sinks, g_ssm_out=g_ssm_out,
             g_attn_out=g_attn_out, w_out=w_out, g_post_mix=g_post_mix, g_pre_ffn=g_pre_ffn,
             w_gate_up=w_gate_up, w_down=w_down, g_post_ffn=g_post_ffn)
    m = dict(g_pre_mix=m_g_pre_mix, w_in=m_w_in, ssm_lambda_re=m_ssm_lambda_re, ssm_lambda_im=m_ssm_lambda_im,
             ssm_log_dt=m_ssm_log_dt, ssm_b_re=m_ssm_b_re, ssm_b_im=m_ssm_b_im, ssm_c_re=m_ssm_c_re,
             ssm_c_im=m_ssm_c_im, ssm_d=m_ssm_d, w_glu=m_w_glu, b_glu=m_b_glu, attn_sinks=m_attn_sinks,
             g_ssm_out=m_g_ssm_out, g_attn_out=m_g_attn_out, w_out=m_w_out, g_post_mix=m_g_post_mix,
             g_pre_ffn=m_g_pre_ffn, w_gate_up=m_w_gate_up, w_down=m_w_down, g_post_ffn=m_g_post_ffn)
    v = dict(g_pre_mix=v_g_pre_mix, w_in=v_w_in, ssm_lambda_re=v_ssm_lambda_re, ssm_lambda_im=v_ssm_lambda_im,
             ssm_log_dt=v_ssm_log_dt, ssm_b_re=v_ssm_b_re, ssm_b_im=v_ssm_b_im, ssm_c_re=v_ssm_c_re,
             ssm_c_im=v_ssm_c_im, ssm_d=v_ssm_d, w_glu=v_w_glu, b_glu=v_b_glu, attn_sinks=v_attn_sinks,
             g_ssm_out=v_g_ssm_out, g_attn_out=v_g_attn_out, w_out=v_w_out, g_post_mix=v_g_post_mix,
             g_pre_ffn=v_g_pre_ffn, w_gate_up=v_w_gate_up, w_down=v_w_down, g_post_ffn=v_g_post_ffn)

    transposed = ("w_in", "w_glu", "w_gate_up")
    native_transposed = ("w_in", "w_gate_up")
    shard = {n: (w[n][0].T if n in transposed else w[n][0]).astype(_BF16) for n in _BIG}
    gathered = {}
    for names, lands in (
            (("w_in",), _sequencer_exchange([shard["w_in"]], [False], "gather_w_in", 1)),
            (("w_glu", "w_out"), _sequencer_exchange([shard["w_glu"], shard["w_out"]], [False] * 2, "gather_mix", 2)),
            (("w_gate_up", "w_down"), _sequencer_gather([shard["w_gate_up"], shard["w_down"]], "gather_ffn", 3))):
        gathered.update({n: a.reshape(-1, a.shape[2]) for n, a in zip(names, lands)})

    def fetch(names, after):
        del after
        return [gathered[n] for n in names]

    sent = []

    def publish(named):
        big = [n for n in named if n in _BIG]
        rows = [n for n in named if n in _ROW_WIDTH]
        dense = [n for n in named if n in _DENSE]
        plain = [n for n in named if n not in big + rows + dense]
        sources = [named[n].reshape(N_DEV, -1, named[n].shape[1]) for n in big]
        slots = _row_slots(rows)
        if rows:
            sources.append(_stack_rows(named, slots))
        sources += [named[n].reshape(-1, _LANES) for n in dense] + [named[n] for n in plain]
        flags = [True] * len(big) + [False] * (len(sources) - len(big))
        cid = 4 + len(sent)
        sent.append((big, slots, dense, plain, _sequencer_exchange(sources, flags, "grads_%d" % cid, cid)))
        return [named[n] for n in big]

    p = {n: w[n] for n in _SMALL}
    grad_x = _local_step(x[0], positions[0], loss_target[0], p, fetch, publish)

    state = {n: [a.reshape(_SHAPE_2D[n]) for a in (w[n], m[n], v[n])] for n in _SMALL}
    result = {}
    total_loss = None
    chain = []
    for big, slots, dense, plain, lands in sent:
        lands = list(lands)
        after = list(chain)
        for name in big:
            part = lands.pop(0)
            if name in native_transposed:
                updated = _adamw(part, w[name][0].T, m[name][0].T, v[name][0].T, "adamw_" + name, after)
                result[name] = [a.T[None] for a in updated]
                chain = [updated[3]]
                continue
            if name in transposed:
                part = _sum_parts(part, "sum_" + name, after).T[None]
            updated = _adamw(part, w[name][0], m[name][0], v[name][0], "adamw_" + name, after)
            result[name] = [a[None] for a in updated]
            chain = [updated[3]]
        parts, items, sums, names = [], [], [], []
        if slots:
            parts.append(lands.pop(0))
            for name, (row, col, _) in slots.items():
                if name == "loss":
                    sums.append((0, slice(row, row + 1), slice(col, col + _LANES)))
                else:
                    items.append((0, slice(row, row + 1), slice(col, col + _SHAPE_2D[name][1]), *state[name]))
                    names.append(name)
        for name in dense:
            part = lands.pop(0).reshape((N_DEV,) + _SHAPE_2D[name])
            result[name] = _adamw(part, *state[name], "adamw_" + name, after)
            chain = [result[name][3]]
        for name in plain:
            rows_n, cols_n = _SHAPE_2D[name]
            items.append((len(parts), slice(0, rows_n), slice(0, cols_n), *state[name]))
            parts.append(lands.pop(0))
            names.append(name)
        if items:
            updated, summed = _adamw_small(parts, items, sums, "adamw_small_" + names[0], after)
            chain = [updated[0][3]]
            result.update(dict(zip(names, updated)))
            if summed:
                total_loss = summed[0][0, 0]

    out = [total_loss, grad_x[None]]
    for kind in range(4):
        out += [result[n][kind].reshape(w[n].shape) for n in _WEIGHTS]
    return tuple(out)
```

```python
import functools
import math

import numpy as np
import jax
import jax.numpy as jnp
from jax import lax
from jax.experimental import pallas as pl
from jax.experimental.pallas import tpu as pltpu
from jax.experimental.pallas import tpu_sc as plsc

D_MODEL = 1024
SSM_WIDTH = 512
SSM_GROUP = 16
SSM_GROUPS = 32
SSM_STATE = 64
N_STATE = SSM_GROUPS * SSM_STATE
ATTN_WIDTH = 512
HEAD_DIM = 64
N_Q_HEADS = 8
N_KV_HEADS = 2
Q_PER_KV = 4
KV_WIDTH = 128
IN_WIDTH = 1280
BLOCK = 128
ROPE_DIM = 16
ROPE_THETA = 500000.0
D_FF = 2816
NORM_EPS = 1e-6
MASK_VALUE = -1e30
ADAM_LR = 0.001
ADAM_B1 = 0.9
ADAM_B2 = 0.999
ADAM_EPS = 1e-08
ADAM_WD = 0.01
ADAM_STEP = 10

N_DEV = 8
SCAN_CHUNKS = 8
SCAN_COLS = 512
TOKEN_TILE = 256
VMEM_LIMIT = 56 * 1024 * 1024

_F32 = jnp.float32
_BF16 = jnp.bfloat16
_MXU = jnp.bfloat16

_NN = ((1,), (0,))
_NT = ((1,), (1,))
_TN = ((0,), (0,))


def _dot(a, b, dims):
    return lax.dot_general(a.astype(_MXU), b.astype(_MXU), (dims, ((), ())),
                           preferred_element_type=_F32)


def _dot_exact(a, b, dims):
    return lax.dot_general(a.astype(_F32), b.astype(_F32), (dims, ((), ())),
                           precision=lax.Precision.HIGHEST, preferred_element_type=_F32)


def _iota(shape, dim):
    return lax.broadcasted_iota(jnp.int32, shape, dim)


def _rms_fwd(x, g):
    r = lax.rsqrt(jnp.mean(x * x, axis=-1, keepdims=True) + NORM_EPS)
    return x * r * g, r


def _rms_bwd(dy, x, g, r):
    a = dy * g
    xn = x * r
    dx = r * (a - xn * jnp.mean(a * xn, axis=-1, keepdims=True))
    dg = jnp.sum(dy * xn, axis=0, keepdims=True)
    return dx, dg


def _call(body, grid, in_specs, out_specs, out_shape, name, scratch=(), tokens=()):
    params = pltpu.CompilerParams(dimension_semantics=("arbitrary",) * len(grid),
                                  vmem_limit_bytes=VMEM_LIMIT)
    n_in, n_tok = len(in_specs), len(tokens)

    def run(*refs):
        return body(*refs[:n_in], *refs[n_in + n_tok:])

    call = pl.pallas_call(run, grid=grid,
                          in_specs=list(in_specs) + [pl.BlockSpec(memory_space=pl.ANY)] * n_tok,
                          out_specs=out_specs, out_shape=out_shape, scratch_shapes=list(scratch),
                          compiler_params=params, name=name)
    return lambda *args: call(*args, *tokens)


def _rows(tm, n):
    return pl.BlockSpec((tm, n), lambda i: (i, 0))


def _whole(shape):
    nd = len(shape)
    return pl.BlockSpec(shape, lambda i: (0,) * nd)


def _sds(shape, dtype):
    return jax.ShapeDtypeStruct(shape, dtype)


def _tile(L):
    return min(TOKEN_TILE, L)


def _accumulate(ref, val, first):
    @pl.when(first)
    def _():
        ref[...] = val

    @pl.when(jnp.logical_not(first))
    def _():
        ref[...] += val


def _rope_rows():
    half = ROPE_DIM // 2
    inv = (np.float32(ROPE_THETA) ** (-np.arange(half, dtype=np.float32) * np.float32(2.0) / np.float32(ROPE_DIM))).astype(np.float32)
    col = np.arange(KV_WIDTH) % HEAD_DIM
    freq = np.where(col < ROPE_DIM, inv[col % half], 0.0).astype(np.float32)
    sign = np.where(col < half, -1.0, np.where(col < ROPE_DIM, 1.0, 0.0)).astype(np.float32)
    return freq[None, :], sign[None, :]


def _rope_tables(pos_col):
    L = pos_col.shape[0]
    tm = _tile(L)
    freq, sign = _rope_rows()

    def body(pos_ref, freq_ref, sign_ref, cos_ref, sin_ref):
        ang = pos_ref[...].astype(_F32) * freq_ref[...]
        cos_ref[...] = jnp.cos(ang)
        sin_ref[...] = jnp.sin(ang) * sign_ref[...]

    return _call(body, (L // tm,),
                 [_rows(tm, 1), _whole((1, KV_WIDTH)), _whole((1, KV_WIDTH))],
                 [_rows(tm, KV_WIDTH), _rows(tm, KV_WIDTH)],
                 [_sds((L, KV_WIDTH), _F32)] * 2, "rope_tables")(pos_col, jnp.asarray(freq), jnp.asarray(sign))


def _widen(t, width):
    return t if width == KV_WIDTH else jnp.concatenate([t] * (width // KV_WIDTH), axis=1)


def _rope_partner(t):
    w = t.shape[1]
    in_head = _iota((1, w), 1) & (HEAD_DIM - 1)
    second = jnp.where(in_head < ROPE_DIM, pltpu.roll(t, ROPE_DIM // 2, 1), 0.0)
    return jnp.where(in_head < ROPE_DIM // 2, pltpu.roll(t, w - ROPE_DIM // 2, 1), second)


def _rope_apply(t, cos_t, sin_t):
    w = t.shape[1]
    return t * _widen(cos_t, w) + _rope_partner(t) * _widen(sin_t, w)


def _rope_transpose(dt, cos_t, sin_t):
    w = dt.shape[1]
    return dt * _widen(cos_t, w) + _rope_partner(dt * _widen(sin_t, w))


def _in_proj(x, g_pre_mix, w_in, cos_t, sin_t):
    L = x.shape[0]
    tm = _tile(L)

    def body(x_ref, g_ref, w_ref, cos_ref, sin_ref, hn_ref, u_ref, q_ref, k_ref, v_ref):
        hn, _ = _rms_fwd(x_ref[...], g_ref[...])
        hn = hn.astype(_BF16)
        hn_ref[...] = hn
        proj = _dot(hn, w_ref[...], _NT)
        u_ref[...] = proj[:, :SSM_WIDTH]
        q = proj[:, SSM_WIDTH:SSM_WIDTH + ATTN_WIDTH]
        k = proj[:, SSM_WIDTH + ATTN_WIDTH:SSM_WIDTH + ATTN_WIDTH + KV_WIDTH]
        cos_v, sin_v = cos_ref[...], sin_ref[...]
        q_ref[...] = _rope_apply(q, cos_v, sin_v).astype(_BF16)
        k_ref[...] = _rope_apply(k, cos_v, sin_v).astype(_BF16)
        v_ref[...] = proj[:, SSM_WIDTH + ATTN_WIDTH + KV_WIDTH:].astype(_BF16)

    return _call(body, (L // tm,),
                 [_rows(tm, D_MODEL), _whole((1, D_MODEL)), _whole((IN_WIDTH, D_MODEL)),
                  _rows(tm, KV_WIDTH), _rows(tm, KV_WIDTH)],
                 [_rows(tm, D_MODEL), _rows(tm, SSM_WIDTH), _rows(tm, ATTN_WIDTH),
                  _rows(tm, KV_WIDTH), _rows(tm, KV_WIDTH)],
                 [_sds((L, D_MODEL), _BF16), _sds((L, SSM_WIDTH), _F32), _sds((L, ATTN_WIDTH), _BF16),
                  _sds((L, KV_WIDTH), _BF16), _sds((L, KV_WIDTH), _BF16)],
                 "in_proj")(x, g_pre_mix, w_in, cos_t, sin_t)


def _s5_discretize(lam_re, lam_im, log_dt):
    lr = jnp.minimum(lam_re, -1e-4)
    li = lam_im
    dt = jnp.exp(log_dt)
    mag = jnp.exp(lr * dt)
    ar = mag * jnp.cos(li * dt)
    ai = mag * jnp.sin(li * dt)
    den = lr * lr + li * li
    fr = ((ar - 1.0) * lr + ai * li) / den
    fi = (ai * lr - (ar - 1.0) * li) / den
    return ar, ai, fr, fi


def _s5_bbar(lam_re, lam_im, log_dt, b_re, b_im):
    ar, ai, fr, fi = _s5_discretize(lam_re, lam_im, log_dt)
    return ar, ai, fr * b_re - fi * b_im, fr * b_im + fi * b_re


def _spread_masks():
    e16 = (_iota((SSM_GROUP, SSM_WIDTH), 1) & (SSM_GROUP - 1)) == _iota((SSM_GROUP, SSM_WIDTH), 0)
    e64 = (_iota((SSM_STATE, N_STATE), 1) & (SSM_STATE - 1)) == _iota((SSM_STATE, N_STATE), 0)
    mask_b = (_iota((N_STATE, SSM_WIDTH), 0) >> 6) == (_iota((N_STATE, SSM_WIDTH), 1) >> 4)
    mask_c = (_iota((SSM_WIDTH, N_STATE), 0) >> 4) == (_iota((SSM_WIDTH, N_STATE), 1) >> 6)
    return e16.astype(_F32), e64.astype(_F32), mask_b, mask_c


SUPER = 4
SB_STATE = N_STATE // SUPER
SB_WIDTH = SSM_WIDTH // SUPER


def _sb_state(k):
    return slice(SB_STATE * k, SB_STATE * (k + 1))


def _sb_width(k):
    return slice(SB_WIDTH * k, SB_WIDTH * (k + 1))


def _dt_column(log_dt_row):
    eye = _iota((SSM_GROUPS, SSM_GROUPS), 0) == _iota((SSM_GROUPS, SSM_GROUPS), 1)
    return jnp.sum(jnp.where(eye, log_dt_row, 0.0), axis=1, keepdims=True)


def _group_masks():
    e64 = ((_iota((SSM_STATE, N_STATE), 1) & (SSM_STATE - 1)) == _iota((SSM_STATE, N_STATE), 0)).astype(_F32)
    own = _iota((SSM_GROUPS, N_STATE), 0) == (_iota((SSM_GROUPS, N_STATE), 1) >> 6)
    return e64, own


def _rows_of_group():
    return ((_iota((SSM_WIDTH, SSM_GROUPS), 0) >> 4) == _iota((SSM_WIDTH, SSM_GROUPS), 1)).astype(_F32)


def _ssm_prep(lam_re, lam_im, log_dt, b_re, b_im, c_re, c_im):
    def body(lr_ref, li_ref, ld_ref, bre, bim, cre, cim, ar_ref, ai_ref, btr, bti, ctr, cti):
        ar, ai, fr, fi = _s5_discretize(lr_ref[...], li_ref[...], _dt_column(ld_ref[...]))
        e64, own = _group_masks()
        mask_c = (_iota((SSM_WIDTH, N_STATE), 0) >> 4) == (_iota((SSM_WIDTH, N_STATE), 1) >> 6)

        def to_row(t):
            return jnp.sum(jnp.where(own, _dot_exact(t, e64, _NN), 0.0), axis=0, keepdims=True)

        def fold(m):
            full = jnp.where(mask_c, _dot(m, e64, _NN), 0.0)
            return sum(full[_sb_width(k), :] for k in range(SUPER)).astype(_BF16)

        ar_ref[...] = to_row(ar)
        ai_ref[...] = to_row(ai)
        spread = _rows_of_group()
        fr_t = _dot_exact(spread, fr, _NN)
        fi_t = _dot_exact(spread, fi, _NN)
        btr[...] = fold(fr_t * bre[...] - fi_t * bim[...])
        bti[...] = fold(fr_t * bim[...] + fi_t * bre[...])
        ctr[...] = fold(cre[...])
        cti[...] = fold(cim[...])

    row = (1, N_STATE)
    ins = [lam_re, lam_im, log_dt, b_re, b_im, c_re, c_im]
    return _call(body, (1,), [_whole(a.shape) for a in ins],
                 [_whole(row), _whole(row)] + [_whole((SB_WIDTH, N_STATE))] * 4,
                 [_sds(row, _F32), _sds(row, _F32)] + [_sds((SB_WIDTH, N_STATE), _BF16)] * 4,
                 "ssm_prep")(*ins)


def _ssm_bu(u, bt_re, bt_im):
    L = u.shape[0]
    tm = _tile(L)

    def body(u_ref, br_ref, bi_ref, or_ref, oi_ref):
        for k in range(SUPER):
            ub = u_ref[:, _sb_width(k)].astype(_BF16)
            or_ref[:, _sb_state(k)] = _dot(ub, br_ref[:, _sb_state(k)], _NN)
            oi_ref[:, _sb_state(k)] = _dot(ub, bi_ref[:, _sb_state(k)], _NN)

    return _call(body, (L // tm,),
                 [_rows(tm, SSM_WIDTH), _whole((SB_WIDTH, N_STATE)), _whole((SB_WIDTH, N_STATE))],
                 [_rows(tm, N_STATE), _rows(tm, N_STATE)],
                 [_sds((L, N_STATE), _F32)] * 2, "ssm_bu")(u, bt_re, bt_im)


def _complex_power(ar, ai, n):
    def step(_, c):
        pr, pi = c
        return pr * ar - pi * ai, pr * ai + pi * ar
    return lax.fori_loop(0, n, step, (jnp.ones_like(ar), jnp.zeros_like(ai)))


def _chunk_carries(er, ei, pr, pi, reverse):
    rows = _iota(er.shape, 0)
    sr = jnp.zeros_like(pr)
    si = jnp.zeros_like(pi)
    out_r = jnp.zeros_like(er)
    out_i = jnp.zeros_like(ei)
    order = range(SCAN_CHUNKS - 1, 0, -1) if reverse else range(SCAN_CHUNKS - 1)
    for c in order:
        e_r = er[c:c + 1, :]
        e_i = ei[c:c + 1, :]
        sr, si = pr * sr - pi * si + e_r, pr * si + pi * sr + e_i
        nxt = c - 1 if reverse else c + 1
        out_r = jnp.where(rows == nxt, sr, out_r)
        out_i = jnp.where(rows == nxt, si, out_i)
    return out_r, out_i


def _scan_fwd(b_re, b_im, a_re, a_im):
    T = b_re.shape[0]
    W = SCAN_COLS
    blk = pl.BlockSpec((T, SCAN_CHUNKS, W), lambda j: (0, 0, j))
    vec = pl.BlockSpec((1, W), lambda j: (0, j))

    def body(br_ref, bi_ref, ar_ref, ai_ref, xr_ref, xi_ref):
        ar, ai = ar_ref[...], ai_ref[...]
        ar8 = jnp.broadcast_to(ar, (SCAN_CHUNKS, W))
        ai8 = jnp.broadcast_to(ai, (SCAN_CHUNKS, W))

        def local(t, c):
            cr, ci = c
            return ar8 * cr - ai8 * ci + br_ref[t], ar8 * ci + ai8 * cr + bi_ref[t]

        zero = jnp.zeros((SCAN_CHUNKS, W), _F32)
        er, ei = lax.fori_loop(0, T, local, (zero, zero))
        pr, pi = _complex_power(ar, ai, T)
        sr, si = _chunk_carries(er, ei, pr, pi, reverse=False)

        def final(t, c):
            nr, ni = local(t, c)
            xr_ref[t] = nr
            xi_ref[t] = ni
            return nr, ni

        lax.fori_loop(0, T, final, (sr, si))

    shape = _sds(b_re.shape, _F32)
    return _call(body, (N_STATE // W,), [blk, blk, vec, vec], [blk, blk], [shape, shape],
                 "scan_fwd")(b_re, b_im, a_re, a_im)


def _scan_bwd(dx_re, dx_im, x_re, x_im, a_re, a_im, tokens=()):
    T = dx_re.shape[0]
    W = SCAN_COLS
    blk = pl.BlockSpec((T, SCAN_CHUNKS, W), lambda j: (0, 0, j))
    vec = pl.BlockSpec((1, W), lambda j: (0, j))

    def body(dr_ref, di_ref, xr_ref, xi_ref, ar_ref, ai_ref, lr_ref, li_ref, dar_ref, dai_ref):
        ar, ai = ar_ref[...], ai_ref[...]
        ar8 = jnp.broadcast_to(ar, (SCAN_CHUNKS, W))
        ai8 = jnp.broadcast_to(ai, (SCAN_CHUNKS, W))

        def local(t, c):
            cr, ci = c
            return ar8 * cr + ai8 * ci + dr_ref[t], ar8 * ci - ai8 * cr + di_ref[t]

        zero = jnp.zeros((SCAN_CHUNKS, W), _F32)
        er, ei = lax.fori_loop(0, T, lambda k, c: local(T - 1 - k, c), (zero, zero))
        pr, pi = _complex_power(ar, -ai, T)
        sr, si = _chunk_carries(er, ei, pr, pi, reverse=True)

        def grad_a(acc, nr, ni, xpr, xpi):
            return acc[0] + nr * xpr + ni * xpi, acc[1] + ni * xpr - nr * xpi

        def final(k, c):
            t = T - 1 - k
            nr, ni = local(t, c[:2])
            lr_ref[t] = nr
            li_ref[t] = ni
            gr, gi = grad_a(c[2:], nr, ni, xr_ref[t - 1], xi_ref[t - 1])
            return nr, ni, gr, gi

        cr, ci, gr, gi = lax.fori_loop(0, T - 1, final, (sr, si, zero, zero))
        nr, ni = local(0, (cr, ci))
        lr_ref[0] = nr
        li_ref[0] = ni
        first = _iota((SCAN_CHUNKS, W), 0) == 0
        xpr = jnp.where(first, 0.0, pltpu.roll(xr_ref[T - 1], 1, 0))
        xpi = jnp.where(first, 0.0, pltpu.roll(xi_ref[T - 1], 1, 0))
        gr, gi = grad_a((gr, gi), nr, ni, xpr, xpi)
        dar_ref[...] = jnp.sum(gr, axis=0, keepdims=True)
        dai_ref[...] = jnp.sum(gi, axis=0, keepdims=True)

    shape = _sds(dx_re.shape, _F32)
    row = _sds((1, N_STATE), _F32)
    return _call(body, (N_STATE // W,), [blk, blk, blk, blk, vec, vec], [blk, blk, vec, vec],
                 [shape, shape, row, row], "scan_bwd", tokens=tokens)(dx_re, dx_im, x_re, x_im, a_re, a_im)


_GELU_K = math.sqrt(2.0 / math.pi)
_GELU_C = 0.044715


def _gelu(y):
    return 0.5 * y * (1.0 + jnp.tanh(_GELU_K * (y + _GELU_C * y * y * y)))


def _gelu_grad(y):
    t = jnp.tanh(_GELU_K * (y + _GELU_C * y * y * y))
    return 0.5 * (1.0 + t) + 0.5 * y * (1.0 - t * t) * _GELU_K * (1.0 + 3.0 * _GELU_C * y * y)


def _ssm_out(x_re, x_im, u, ct_re, ct_im, d_row, w_glu, b_glu, g_ssm):
    L = u.shape[0]
    tm = _tile(L)

    def body(xr_ref, xi_ref, u_ref, cr_ref, ci_ref, d_ref, w_ref, b_ref, g_ref, y_ref, z_ref, n_ref):
        cx = [_dot(xr_ref[:, _sb_state(k)], cr_ref[:, _sb_state(k)], _NT)
              - _dot(xi_ref[:, _sb_state(k)], ci_ref[:, _sb_state(k)], _NT) for k in range(SUPER)]
        y = jnp.concatenate(cx, axis=1) + d_ref[...] * u_ref[...]
        y_ref[...] = y
        z = _dot(_gelu(y), w_ref[...], _NT) + b_ref[...]
        z_ref[...] = z
        out = z[:, :SSM_WIDTH] * jax.nn.sigmoid(z[:, SSM_WIDTH:])
        n, _ = _rms_fwd(out, g_ref[...])
        n_ref[...] = n.astype(_BF16)

    return _call(body, (L // tm,),
                 [_rows(tm, N_STATE), _rows(tm, N_STATE), _rows(tm, SSM_WIDTH),
                  _whole((SB_WIDTH, N_STATE)), _whole((SB_WIDTH, N_STATE)), _whole((1, SSM_WIDTH)),
                  _whole((2 * SSM_WIDTH, SSM_WIDTH)), _whole((1, 2 * SSM_WIDTH)), _whole((1, SSM_WIDTH))],
                 [_rows(tm, SSM_WIDTH), _rows(tm, 2 * SSM_WIDTH), _rows(tm, SSM_WIDTH)],
                 [_sds((L, SSM_WIDTH), _F32), _sds((L, 2 * SSM_WIDTH), _F32), _sds((L, SSM_WIDTH), _BF16)],
                 "ssm_out")(x_re, x_im, u, ct_re, ct_im, d_row, w_glu, b_glu, g_ssm)


def _ssm_out_bwd(dn, y, z, u, ct_re, ct_im, d_row, w_glu, g_ssm):
    L = u.shape[0]
    tm = _tile(L)

    def body(dn_ref, y_ref, z_ref, u_ref, cr_ref, ci_ref, d_ref, w_ref, g_ref,
             gy_ref, dz_ref, dy_ref, dud_ref, dxr_ref, dxi_ref, dg_ref, db_ref, dd_ref):
        first = pl.program_id(0) == 0
        z = z_ref[...]
        z1, z2 = z[:, :SSM_WIDTH], z[:, SSM_WIDTH:]
        sig = jax.nn.sigmoid(z2)
        out = z1 * sig
        g = g_ref[...]
        _, r = _rms_fwd(out, g)
        dout, dg = _rms_bwd(dn_ref[...], out, g, r)
        _accumulate(dg_ref, dg, first)
        dz = jnp.concatenate([dout * sig, dout * z1 * sig * (1.0 - sig)], axis=1)
        _accumulate(db_ref, jnp.sum(dz, axis=0, keepdims=True), first)
        dzb = dz.astype(_BF16)
        dz_ref[...] = dzb
        y = y_ref[...]
        gy_ref[...] = _gelu(y).astype(_BF16)
        dy = _dot(dzb, w_ref[...], _NN) * _gelu_grad(y)
        u = u_ref[...]
        _accumulate(dd_ref, jnp.sum(dy * u, axis=0, keepdims=True), first)
        dud_ref[...] = d_ref[...] * dy
        dyb = dy.astype(_BF16)
        dy_ref[...] = dyb
        for k in range(SUPER):
            dxr_ref[:, _sb_state(k)] = _dot(dyb[:, _sb_width(k)], cr_ref[:, _sb_state(k)], _NN)
            dxi_ref[:, _sb_state(k)] = -_dot(dyb[:, _sb_width(k)], ci_ref[:, _sb_state(k)], _NN)

    row = _whole((1, SSM_WIDTH))
    return _call(body, (L // tm,),
                 [_rows(tm, SSM_WIDTH), _rows(tm, SSM_WIDTH), _rows(tm, 2 * SSM_WIDTH), _rows(tm, SSM_WIDTH),
                  _whole((SB_WIDTH, N_STATE)), _whole((SB_WIDTH, N_STATE)), row,
                  _whole((2 * SSM_WIDTH, SSM_WIDTH)), row],
                 [_rows(tm, SSM_WIDTH), _rows(tm, 2 * SSM_WIDTH), _rows(tm, SSM_WIDTH), _rows(tm, SSM_WIDTH),
                  _rows(tm, N_STATE), _rows(tm, N_STATE), row, _whole((1, 2 * SSM_WIDTH)), row],
                 [_sds((L, SSM_WIDTH), _BF16), _sds((L, 2 * SSM_WIDTH), _BF16), _sds((L, SSM_WIDTH), _BF16),
                  _sds((L, SSM_WIDTH), _F32), _sds((L, N_STATE), _F32), _sds((L, N_STATE), _F32),
                  _sds((1, SSM_WIDTH), _F32), _sds((1, 2 * SSM_WIDTH), _F32), _sds((1, SSM_WIDTH), _F32)],
                 "ssm_out_bwd")(dn, y, z, u, ct_re, ct_im, d_row, w_glu, g_ssm)


def _ssm_du(lam_re, lam_im, bt_re, bt_im, dud):
    L = dud.shape[0]
    tm = _tile(L)

    def body(lr_ref, li_ref, br_ref, bi_ref, dud_ref, du_ref):
        for k in range(SUPER):
            du_ref[:, _sb_width(k)] = (_dot(lr_ref[:, _sb_state(k)], br_ref[:, _sb_state(k)], _NT)
                                       + _dot(li_ref[:, _sb_state(k)], bi_ref[:, _sb_state(k)], _NT)
                                       + dud_ref[:, _sb_width(k)])

    return _call(body, (L // tm,),
                 [_rows(tm, N_STATE), _rows(tm, N_STATE), _whole((SB_WIDTH, N_STATE)),
                  _whole((SB_WIDTH, N_STATE)), _rows(tm, SSM_WIDTH)],
                 _rows(tm, SSM_WIDTH), _sds((L, SSM_WIDTH), _F32), "ssm_du")(lam_re, lam_im, bt_re, bt_im, dud)


def _ssm_weight_grads(dy, x_re, x_im, lam_re, lam_im, u):
    L = u.shape[0]

    def body(dy_ref, xr_ref, xi_ref, lr_ref, li_ref, u_ref, dcr_ref, dci_ref, dbr_ref, dbi_ref):
        dyb = dy_ref[...]
        ub = u_ref[...].astype(_BF16)
        dcr_ref[...] = _dot(dyb, xr_ref[...], _TN)
        dci_ref[...] = _dot(dyb, xi_ref[...], _TN)
        dbr_ref[...] = _dot(ub, lr_ref[...], _TN)
        dbi_ref[...] = _dot(ub, li_ref[...], _TN)

    width = pl.BlockSpec((L, SB_WIDTH), lambda k: (0, k))
    state = pl.BlockSpec((L, SB_STATE), lambda k: (0, k))
    out = pl.BlockSpec((SB_WIDTH, SB_STATE), lambda k: (0, k))
    return _call(body, (SUPER,), [width, state, state, state, state, width], [out] * 4,
                 [_sds((SB_WIDTH, N_STATE), _F32)] * 4,
                 "ssm_weight_grads")(dy, x_re, x_im, lam_re, lam_im, u)


def _ssm_param_bwd(da_re, da_im, dbt_re, dbt_im, dct_re, dct_im, lam_re, lam_im, log_dt, b_re, b_im):
    def body(dar, dai, dbr, dbi, dcr, dci, lr_ref, li_ref, ld_ref, bre_ref, bim_ref,
             glr, gli, gdt, gbr, gbi, gcr, gci):
        own_c = (_iota((SB_WIDTH, SB_STATE), 0) >> 4) == (_iota((SB_WIDTH, SB_STATE), 1) >> 6)

        def unfold(ref):
            blocks = []
            for k in range(SUPER):
                t = jnp.where(own_c, ref[:, _sb_state(k)], 0.0)
                t = sum(t[:, 128 * i:128 * (i + 1)] for i in range(SB_STATE // 128))
                blocks.append((t + pltpu.roll(t, SSM_STATE, 1))[:, :SSM_STATE])
            return jnp.concatenate(blocks, axis=0)

        gcr[...] = unfold(dcr)
        gci[...] = -unfold(dci)
        dbb_re, dbb_im = unfold(dbr), unfold(dbi)
        b_re, b_im = bre_ref[...], bim_ref[...]
        dt_col = _dt_column(ld_ref[...])
        (_, _, fr, fi), vjp = jax.vjp(_s5_discretize, lr_ref[...], li_ref[...], dt_col)
        spread = _rows_of_group()
        fr_t = _dot_exact(spread, fr, _NN)
        fi_t = _dot_exact(spread, fi, _NN)
        gbr[...] = fr_t * dbb_re + fi_t * dbb_im
        gbi[...] = fr_t * dbb_im - fi_t * dbb_re
        d_fr = _dot_exact(spread, dbb_re * b_re + dbb_im * b_im, _TN)
        d_fi = _dot_exact(spread, dbb_im * b_re - dbb_re * b_im, _TN)
        e64, own = _group_masks()

        def from_row(ref):
            return _dot_exact(jnp.where(own, ref[...], 0.0), e64, _NT)

        d_lr, d_li, d_dt = vjp((from_row(dar), from_row(dai), d_fr, d_fi))
        glr[...] = d_lr
        gli[...] = d_li
        eye = (_iota((SSM_GROUPS, SSM_GROUPS), 0) == _iota((SSM_GROUPS, SSM_GROUPS), 1)).astype(_F32)
        gdt[...] = _dot_exact(jnp.broadcast_to(d_dt, (SSM_GROUPS, 128)), eye, _TN)[0:1]

    ins = [da_re, da_im, dbt_re, dbt_im, dct_re, dct_im, lam_re, lam_im, log_dt, b_re, b_im]
    outs = [(SSM_GROUPS, SSM_STATE), (SSM_GROUPS, SSM_STATE), (1, SSM_GROUPS)] + [(SSM_WIDTH, SSM_STATE)] * 4
    return _call(body, (1,), [_whole(a.shape) for a in ins], [_whole(s) for s in outs],
                 [_sds(s, _F32) for s in outs], "ssm_param_bwd")(*ins)


def _head_spread(j):
    r = _iota((KV_WIDTH, 256), 0)
    c = _iota((KV_WIDTH, 256), 1)
    return (r == HEAD_DIM * j + (c & (HEAD_DIM - 1))).astype(_BF16)


STACK = Q_PER_KV * BLOCK


def _stack_heads(t):
    lane_head = _iota((1, 256), 1) >> 6
    return jnp.concatenate([jnp.where(lane_head == g, t, jnp.zeros_like(t)) for g in range(Q_PER_KV)], axis=0)


def _unstack_heads(t):
    lane_head = _iota((1, 256), 1) >> 6
    return sum(jnp.where(lane_head == g, t[BLOCK * g:BLOCK * (g + 1)], 0.0) for g in range(Q_PER_KV))


def _stacked_sinks(sink_ref, j):
    block = _iota((STACK, 1), 0) >> 7
    col = jnp.full((STACK, 1), sink_ref[Q_PER_KV * j], _F32)
    for g in range(1, Q_PER_KV):
        col = jnp.where(block == g, sink_ref[Q_PER_KV * j + g], col)
    return col


def _fold_heads(t, j):
    t = t[:, :KV_WIDTH] + t[:, KV_WIDTH:]
    t = t + pltpu.roll(t, HEAD_DIM, 1)
    return jnp.where((_iota((1, KV_WIDTH), 1) >> 6) == j, t, 0.0)


def _attn_scores(q_stacked, kt, blk, sink):
    s = _dot(q_stacked, kt, _NT) * (HEAD_DIM ** -0.5)
    qi = _iota((STACK, 2 * BLOCK), 0) & (BLOCK - 1)
    kj = _iota((STACK, 2 * BLOCK), 1)
    rel = qi + BLOCK - kj
    valid = (rel >= 0) & (rel < BLOCK) & (blk * BLOCK - BLOCK + kj >= 0)
    s = jnp.where(valid, s, MASK_VALUE)
    m = jnp.maximum(jnp.max(s, axis=-1, keepdims=True), sink)
    p = jnp.exp(s - m)
    e_sink = jnp.exp(sink - m)
    den = jnp.sum(p, axis=-1, keepdims=True) + e_sink
    return p / den, e_sink / den


def _attn_specs():
    prev = lambda i: (jnp.maximum(i - 1, 0), 0)
    cur = lambda i: (i, 0)
    kv = [pl.BlockSpec((BLOCK, KV_WIDTH), prev), pl.BlockSpec((BLOCK, KV_WIDTH), cur)]
    return [pl.BlockSpec((BLOCK, ATTN_WIDTH), cur)] + kv + kv


def _attn_fwd(q, k, v, sinks, g_attn):
    L = q.shape[0]

    def body(q_ref, kp_ref, kc_ref, vp_ref, vc_ref, sink_ref, g_ref, o_ref, n_ref):
        blk = pl.program_id(0)
        kwin = jnp.concatenate([kp_ref[...], kc_ref[...]], axis=0)
        vwin = jnp.concatenate([vp_ref[...], vc_ref[...]], axis=0)
        halves = []
        for j in range(N_KV_HEADS):
            spread = _head_spread(j)
            kt = _dot(kwin, spread, _NN).astype(_BF16)
            vt = _dot(vwin, spread, _NN).astype(_BF16)
            qs = _stack_heads(q_ref[:, 256 * j:256 * (j + 1)])
            p, _ = _attn_scores(qs, kt, blk, _stacked_sinks(sink_ref, j))
            halves.append(_unstack_heads(_dot(p, vt, _NN)))
        o = jnp.concatenate(halves, axis=1)
        o_ref[...] = o
        n, _ = _rms_fwd(o, g_ref[...])
        n_ref[...] = n.astype(_BF16)

    cur = lambda i: (i, 0)
    return _call(body, (L // BLOCK,),
                 _attn_specs() + [pl.BlockSpec(memory_space=pltpu.SMEM), _whole((1, ATTN_WIDTH))],
                 [pl.BlockSpec((BLOCK, ATTN_WIDTH), cur)] * 2,
                 [_sds((L, ATTN_WIDTH), _F32), _sds((L, ATTN_WIDTH), _BF16)],
                 "attn_fwd")(q, k, k, v, v, sinks, g_attn)


def _attn_bwd(q, k, v, o, dn, sinks, g_attn):
    L = q.shape[0]

    def body(q_ref, kp_ref, kc_ref, vp_ref, vc_ref, o_ref, dn_ref, sink_ref, g_ref,
             dq_ref, dk_ref, dv_ref, dsink_ref, dg_ref):
        blk = pl.program_id(0)
        first = blk == 0

        @pl.when(first)
        def _():
            dk_ref[...] = jnp.zeros_like(dk_ref)
            dv_ref[...] = jnp.zeros_like(dv_ref)
            dsink_ref[...] = jnp.zeros_like(dsink_ref)

        o = o_ref[...]
        g = g_ref[...]
        _, r = _rms_fwd(o, g)
        do, dg = _rms_bwd(dn_ref[...], o, g, r)
        _accumulate(dg_ref, dg, first)
        kwin = jnp.concatenate([kp_ref[...], kc_ref[...]], axis=0)
        vwin = jnp.concatenate([vp_ref[...], vc_ref[...]], axis=0)
        lane = _iota((1, 128), 1)
        dsink = jnp.zeros((1, 128), _F32)
        dkwin = jnp.zeros((2 * BLOCK, KV_WIDTH), _F32)
        dvwin = jnp.zeros((2 * BLOCK, KV_WIDTH), _F32)
        dq_halves = []
        for j in range(N_KV_HEADS):
            spread = _head_spread(j)
            kt = _dot(kwin, spread, _NN).astype(_BF16)
            vt = _dot(vwin, spread, _NN).astype(_BF16)
            qs = _stack_heads(q_ref[:, 256 * j:256 * (j + 1)])
            dos = _stack_heads(do[:, 256 * j:256 * (j + 1)]).astype(_BF16)
            p, p_sink = _attn_scores(qs, kt, blk, _stacked_sinks(sink_ref, j))
            dp = _dot(dos, vt, _NT)
            delta = jnp.sum(p * dp, axis=-1, keepdims=True)
            ds = (p * (dp - delta) * (HEAD_DIM ** -0.5)).astype(_BF16)
            sink_term = p_sink * delta
            for g in range(Q_PER_KV):
                head_sum = jnp.sum(sink_term[BLOCK * g:BLOCK * (g + 1)], axis=0, keepdims=True)
                dsink = dsink - jnp.where(lane == Q_PER_KV * j + g, head_sum, 0.0)
            dvwin = dvwin + _fold_heads(_dot(p, dos, _TN), j)
            dkwin = dkwin + _fold_heads(_dot(ds, qs, _TN), j)
            dq_halves.append(_unstack_heads(_dot(ds, kt, _NN)))
        dq_ref[...] = jnp.concatenate(dq_halves, axis=1)
        dsink_ref[...] += dsink
        prev = pl.ds(pl.multiple_of(jnp.maximum(blk - 1, 0) * BLOCK, BLOCK), BLOCK)
        cur = pl.ds(pl.multiple_of(blk * BLOCK, BLOCK), BLOCK)
        dk_ref[prev, :] += dkwin[:BLOCK]
        dk_ref[cur, :] += dkwin[BLOCK:]
        dv_ref[prev, :] += dvwin[:BLOCK]
        dv_ref[cur, :] += dvwin[BLOCK:]

    cur = lambda i: (i, 0)
    blk_q = pl.BlockSpec((BLOCK, ATTN_WIDTH), cur)
    return _call(body, (L // BLOCK,),
                 _attn_specs() + [blk_q, blk_q, pl.BlockSpec(memory_space=pltpu.SMEM), _whole((1, ATTN_WIDTH))],
                 [blk_q, _whole((L, KV_WIDTH)), _whole((L, KV_WIDTH)), _whole((1, 128)), _whole((1, ATTN_WIDTH))],
                 [_sds((L, ATTN_WIDTH), _F32), _sds((L, KV_WIDTH), _F32), _sds((L, KV_WIDTH), _F32),
                  _sds((1, 128), _F32), _sds((1, ATTN_WIDTH), _F32)],
                 "attn_bwd")(q, k, k, v, v, o, dn, sinks, g_attn)


def _out_proj(n_ssm, n_attn, x, w_out, g_post_mix, g_pre_ffn):
    L = x.shape[0]
    tm = _tile(L)

    def body(ns_ref, na_ref, x_ref, w_ref, g1_ref, g2_ref, merged_ref, mo_ref, h1_ref, hn2_ref):
        merged = jnp.concatenate([ns_ref[...], na_ref[...]], axis=1)
        merged_ref[...] = merged
        mo = _dot(merged, w_ref[...], _NN)
        mo_ref[...] = mo
        n, _ = _rms_fwd(mo, g1_ref[...])
        h1 = x_ref[...] + n
        h1_ref[...] = h1
        hn2, _ = _rms_fwd(h1, g2_ref[...])
        hn2_ref[...] = hn2.astype(_BF16)

    row = _whole((1, D_MODEL))
    return _call(body, (L // tm,),
                 [_rows(tm, SSM_WIDTH), _rows(tm, ATTN_WIDTH), _rows(tm, D_MODEL), _whole((D_MODEL, D_MODEL)), row, row],
                 [_rows(tm, D_MODEL)] * 4,
                 [_sds((L, D_MODEL), _BF16), _sds((L, D_MODEL), _F32), _sds((L, D_MODEL), _F32), _sds((L, D_MODEL), _BF16)],
                 "out_proj")(n_ssm, n_attn, x, w_out, g_post_mix, g_pre_ffn)


def _ffn(hn2, h1, target, w_gate_up, w_down, g_pre_ffn, g_post_ffn):
    L = h1.shape[0]
    tm = _tile(L)
    half = D_FF // 2

    def body(hn2_ref, h1_ref, tgt_ref, wgu_hbm, wd_hbm, g2_ref, g3_ref,
             act_ref, dgu_ref, dff_ref, dh1_ref, loss_ref, dg3_ref, dg2_ref,
             wgu, wd, gu, sem):
        first = pl.program_id(0) == 0

        @pl.when(first)
        def _():
            c1 = pltpu.make_async_copy(wgu_hbm, wgu, sem.at[0])
            c2 = pltpu.make_async_copy(wd_hbm, wd, sem.at[1])
            c1.start()
            c2.start()
            c1.wait()
            c2.wait()

        hn2 = hn2_ref[...]
        ff = jnp.zeros((tm, D_MODEL), _F32)
        for c in range(2):
            gate = _dot(hn2, wgu[half * c:half * (c + 1), :], _NT)
            up = _dot(hn2, wgu[D_FF + half * c:D_FF + half * (c + 1), :], _NT)
            gu[:, half * c:half * (c + 1)] = gate
            gu[:, D_FF + half * c:D_FF + half * (c + 1)] = up
            act = (gate * jax.nn.sigmoid(gate) * up).astype(_BF16)
            act_ref[:, half * c:half * (c + 1)] = act
            ff = ff + _dot(act, wd[half * c:half * (c + 1), :], _NN)
        g3 = g3_ref[...]
        n, r = _rms_fwd(ff, g3)
        h1 = h1_ref[...]
        err = h1 + n - tgt_ref[...]
        loss = 0.5 * jnp.sum(jnp.mean(err * err, axis=-1, keepdims=True), axis=0, keepdims=True)
        _accumulate(loss_ref, jnp.broadcast_to(loss, (1, 128)), first)
        dh2 = err * (1.0 / D_MODEL)
        dff, dg3 = _rms_bwd(dh2, ff, g3, r)
        _accumulate(dg3_ref, dg3, first)
        dffb = dff.astype(_BF16)
        dff_ref[...] = dffb
        dhn2 = jnp.zeros((tm, D_MODEL), _F32)
        for c in range(2):
            dact = _dot(dffb, wd[half * c:half * (c + 1), :], _NT)
            gate = gu[:, half * c:half * (c + 1)]
            up = gu[:, D_FF + half * c:D_FF + half * (c + 1)]
            sig = jax.nn.sigmoid(gate)
            silu = gate * sig
            dgate = (dact * up * (sig + silu * (1.0 - sig))).astype(_BF16)
            dup = (dact * silu).astype(_BF16)
            dgu_ref[:, half * c:half * (c + 1)] = dgate
            dgu_ref[:, D_FF + half * c:D_FF + half * (c + 1)] = dup
            dhn2 = dhn2 + _dot(dgate, wgu[half * c:half * (c + 1), :], _NN)
            dhn2 = dhn2 + _dot(dup, wgu[D_FF + half * c:D_FF + half * (c + 1), :], _NN)
        g2 = g2_ref[...]
        _, r2 = _rms_fwd(h1, g2)
        dh1, dg2 = _rms_bwd(dhn2, h1, g2, r2)
        _accumulate(dg2_ref, dg2, first)
        dh1_ref[...] = dh2 + dh1

    row = _whole((1, D_MODEL))
    anyspace = pl.BlockSpec(memory_space=pl.ANY)
    return _call(body, (L // tm,),
                 [_rows(tm, D_MODEL), _rows(tm, D_MODEL), _rows(tm, D_MODEL), anyspace, anyspace, row, row],
                 [_rows(tm, D_FF), _rows(tm, 2 * D_FF), _rows(tm, D_MODEL), _rows(tm, D_MODEL),
                  _whole((1, 128)), row, row],
                 [_sds((L, D_FF), _BF16), _sds((L, 2 * D_FF), _BF16), _sds((L, D_MODEL), _BF16),
                  _sds((L, D_MODEL), _F32), _sds((1, 128), _F32), _sds((1, D_MODEL), _F32), _sds((1, D_MODEL), _F32)],
                 "ffn",
                 scratch=[pltpu.VMEM((2 * D_FF, D_MODEL), _BF16), pltpu.VMEM((D_FF, D_MODEL), _BF16),
                          pltpu.VMEM((tm, 2 * D_FF), _F32), pltpu.SemaphoreType.DMA((2,))],
                 )(hn2, h1, target, w_gate_up, w_down, g_pre_ffn, g_post_ffn)


def _out_proj_bwd(dh1, mo, w_out, g_post_mix, tokens=()):
    L = dh1.shape[0]
    tm = _tile(L)

    def body(dh1_ref, mo_ref, w_ref, g_ref, dmo_ref, dns_ref, dna_ref, dg_ref):
        first = pl.program_id(0) == 0
        mo = mo_ref[...]
        g = g_ref[...]
        _, r = _rms_fwd(mo, g)
        dmo, dg = _rms_bwd(dh1_ref[...], mo, g, r)
        _accumulate(dg_ref, dg, first)
        dmob = dmo.astype(_BF16)
        dmo_ref[...] = dmob
        dmerged = _dot(dmob, w_ref[...], _NT)
        dns_ref[...] = dmerged[:, :SSM_WIDTH]
        dna_ref[...] = dmerged[:, SSM_WIDTH:]

    row = _whole((1, D_MODEL))
    return _call(body, (L // tm,),
                 [_rows(tm, D_MODEL), _rows(tm, D_MODEL), _whole((D_MODEL, D_MODEL)), row],
                 [_rows(tm, D_MODEL), _rows(tm, SSM_WIDTH), _rows(tm, ATTN_WIDTH), row],
                 [_sds((L, D_MODEL), _BF16), _sds((L, SSM_WIDTH), _F32), _sds((L, ATTN_WIDTH), _F32),
                  _sds((1, D_MODEL), _F32)],
                 "out_proj_bwd", tokens=tokens)(dh1, mo, w_out, g_post_mix)


def _in_proj_bwd(du, dq, dk, dv, cos_t, sin_t, x, dh1, g_pre_mix, w_in, tokens=()):
    L = x.shape[0]
    tm = _tile(L)

    def body(du_ref, dq_ref, dk_ref, dv_ref, cos_ref, sin_ref, x_ref, dh1_ref, g_ref, w_ref,
             dproj_ref, dx_ref, dg_ref):
        first = pl.program_id(0) == 0
        cos_v, sin_v = cos_ref[...], sin_ref[...]
        dproj = jnp.concatenate([du_ref[...], _rope_transpose(dq_ref[...], cos_v, sin_v),
                                 _rope_transpose(dk_ref[...], cos_v, sin_v), dv_ref[...]], axis=1).astype(_BF16)
        dproj_ref[...] = dproj
        dhn = _dot(dproj, w_ref[...], _NN)
        x = x_ref[...]
        g = g_ref[...]
        _, r = _rms_fwd(x, g)
        dx, dg = _rms_bwd(dhn, x, g, r)
        _accumulate(dg_ref, dg, first)
        dx_ref[...] = dh1_ref[...] + dx

    row = _whole((1, D_MODEL))
    return _call(body, (L // tm,),
                 [_rows(tm, SSM_WIDTH), _rows(tm, ATTN_WIDTH), _rows(tm, KV_WIDTH), _rows(tm, KV_WIDTH),
                  _rows(tm, KV_WIDTH), _rows(tm, KV_WIDTH), _rows(tm, D_MODEL), _rows(tm, D_MODEL), row,
                  _whole((IN_WIDTH, D_MODEL))],
                 [_rows(tm, IN_WIDTH), _rows(tm, D_MODEL), row],
                 [_sds((L, IN_WIDTH), _BF16), _sds((L, D_MODEL), _F32), _sds((1, D_MODEL), _F32)],
                 "in_proj_bwd", tokens=tokens)(du, dq, dk, dv, cos_t, sin_t, x, dh1, g_pre_mix, w_in)


def _matmul_tn(a, b, out_dtype, name, scale=1.0):
    K, M = a.shape
    N = b.shape[1]
    tm = next(t for t in (512, 256, 128) if M % t == 0)
    tn = next(t for t in (512, 256, 128) if N % t == 0)

    def body(a_ref, b_ref, o_ref):
        acc = _dot(a_ref[...], b_ref[...], _TN)
        o_ref[...] = (acc if scale == 1.0 else acc * scale).astype(out_dtype)

    params = pltpu.CompilerParams(dimension_semantics=("arbitrary", "arbitrary"), vmem_limit_bytes=VMEM_LIMIT)
    return pl.pallas_call(body, grid=(M // tm, N // tn),
                          in_specs=[pl.BlockSpec((K, tm), lambda i, j: (0, i)),
                                    pl.BlockSpec((K, tn), lambda i, j: (0, j))],
                          out_specs=pl.BlockSpec((tm, tn), lambda i, j: (i, j)),
                          out_shape=_sds((M, N), out_dtype), compiler_params=params, name=name)(a, b)


def _to_chunked(a):
    L, n = a.shape
    return a.reshape(SCAN_CHUNKS, L // SCAN_CHUNKS, n).transpose(1, 0, 2).reshape(L, n)


def _from_chunked(a):
    L, n = a.shape
    return a.reshape(L // SCAN_CHUNKS, SCAN_CHUNKS, n).transpose(1, 0, 2).reshape(L, n)


def _local_step(x, pos, target, p, fetch, publish):
    L = x.shape[0]
    T = L // SCAN_CHUNKS
    cos_t, sin_t = _rope_tables(pos.reshape(L, 1))
    w_in, = fetch(("w_in",), None)
    hn, u, q, k, v = _in_proj(x, p["g_pre_mix"], w_in, cos_t, sin_t)

    ssm = {n: _to_2d(n, p[n]) for n in ("ssm_lambda_re", "ssm_lambda_im", "ssm_log_dt", "ssm_b_re", "ssm_b_im",
                                        "ssm_c_re", "ssm_c_im")}
    d_row = p["ssm_d"].reshape(1, SSM_WIDTH)
    a_re, a_im, bt_re, bt_im, ct_re, ct_im = _ssm_prep(
        ssm["ssm_lambda_re"], ssm["ssm_lambda_im"], ssm["ssm_log_dt"], ssm["ssm_b_re"], ssm["ssm_b_im"],
        ssm["ssm_c_re"], ssm["ssm_c_im"])

    u_c = _to_chunked(u)
    bu_re, bu_im = _ssm_bu(u_c, bt_re, bt_im)
    x_re, x_im = _scan_fwd(bu_re.reshape(T, SCAN_CHUNKS, N_STATE), bu_im.reshape(T, SCAN_CHUNKS, N_STATE), a_re, a_im)
    w_glu, = fetch(("w_glu",), x_re)
    y, z, n_ssm_c = _ssm_out(x_re.reshape(L, N_STATE), x_im.reshape(L, N_STATE), u_c, ct_re, ct_im, d_row,
                             w_glu, p["b_glu"], p["g_ssm_out"])
    n_ssm = _from_chunked(n_ssm_c)

    sinks = p["attn_sinks"].reshape(N_Q_HEADS)
    o, n_attn = _attn_fwd(q, k, v, sinks, p["g_attn_out"])
    w_out, = fetch(("w_out",), n_attn)
    merged, mo, h1, hn2 = _out_proj(n_ssm, n_attn, x, w_out, p["g_post_mix"], p["g_pre_ffn"])
    w_gate_up, w_down = fetch(("w_gate_up", "w_down"), hn2)
    act, dgu, dff, dh1, loss, dg_post_ffn, dg_pre_ffn = _ffn(
        hn2, h1, target, w_gate_up, w_down, p["g_pre_ffn"], p["g_post_ffn"])
    grads = {"g_post_ffn": dg_post_ffn, "g_pre_ffn": dg_pre_ffn}
    tokens = publish({"w_down": _matmul_tn(act, dff, _BF16, "grad_w_down"),
                      "w_gate_up": _matmul_tn(dgu, hn2, _BF16, "grad_w_gate_up")})

    dmo, dn_ssm, dn_attn, grads["g_post_mix"] = _out_proj_bwd(dh1, mo, w_out, p["g_post_mix"], tokens)
    grad_w_out = _matmul_tn(merged, dmo, _BF16, "grad_w_out")

    dq, dk, dv, dsink, grads["g_attn_out"] = _attn_bwd(q, k, v, o, dn_attn, sinks, p["g_attn_out"])
    grads["attn_sinks"] = dsink

    gy, dz, dy, dud, dx_re, dx_im, grads["g_ssm_out"], grads["b_glu"], dd = _ssm_out_bwd(
        _to_chunked(dn_ssm), y, z, u_c, ct_re, ct_im, d_row, w_glu, p["g_ssm_out"])
    grads["ssm_d"] = dd.reshape(SSM_GROUPS, SSM_GROUP)
    tokens = publish({"w_out": grad_w_out, "w_glu": _matmul_tn(dz, gy, _BF16, "grad_w_glu")})
    lam_re, lam_im, da_re, da_im = _scan_bwd(dx_re.reshape(T, SCAN_CHUNKS, N_STATE), dx_im.reshape(T, SCAN_CHUNKS, N_STATE),
                                             x_re, x_im, a_re, a_im, tokens)
    lam_re = lam_re.reshape(L, N_STATE)
    lam_im = lam_im.reshape(L, N_STATE)
    dct_re, dct_im, dbt_re, dbt_im = _ssm_weight_grads(
        dy, x_re.reshape(L, N_STATE), x_im.reshape(L, N_STATE), lam_re, lam_im, u_c)
    g_lr, g_li, g_dt, g_br, g_bi, g_cr, g_ci = _ssm_param_bwd(
        da_re, da_im, dbt_re, dbt_im, dct_re, dct_im,
        ssm["ssm_lambda_re"], ssm["ssm_lambda_im"], ssm["ssm_log_dt"], ssm["ssm_b_re"], ssm["ssm_b_im"])
    grads.update(ssm_lambda_re=g_lr, ssm_lambda_im=g_li, ssm_log_dt=g_dt, ssm_b_re=g_br, ssm_b_im=g_bi,
                 ssm_c_re=g_cr, ssm_c_im=g_ci, loss=loss)
    publish(grads)

    du = _from_chunked(_ssm_du(lam_re, lam_im, bt_re, bt_im, dud))
    dproj, grad_x, g_pre_mix = _in_proj_bwd(du, dq, dk, dv, cos_t, sin_t, x, dh1, p["g_pre_mix"], w_in, [g_dt])
    publish({"g_pre_mix": g_pre_mix, "w_in": _matmul_tn(dproj, hn, _BF16, "grad_w_in")})
    return grad_x


_MESH = pl.DeviceIdType.MESH
_PEERS = N_DEV - 1


def _mesh_pos():
    return lax.axis_index("x"), lax.axis_index("y"), lax.axis_index("c")


def _dev_index(px, py, pc):
    return 4 * px + 2 * py + pc


def _all_gather(shards, out_dtype, name):
    n = len(shards)

    def body(*refs):
        ins, outs, stages = refs[:n], refs[n:2 * n], refs[2 * n:3 * n]
        send_sems, recv_sems, local_sems = refs[3 * n:]
        x, y, c = _mesh_pos()
        me, sibling = (x, y, c), (x, y, 1 - c)
        chips = [(1 - x, y), (x, 1 - y), (1 - x, 1 - y)]

        def copy(w, k, block, to, src=None):
            slot = outs[w].at[_dev_index(*block)]
            return pltpu.make_async_remote_copy(
                src_ref=slot if src is None else src, dst_ref=slot,
                send_sem=send_sems.at[_PEERS * w + k], recv_sem=recv_sems.at[_PEERS * w + k],
                device_id=to, device_id_type=_MESH)

        for w in range(n):
            stages[w][...] = ins[w][...].astype(out_dtype)
        mine, first, passed = [], [], []
        for w in range(n):
            cp = pltpu.make_async_copy(stages[w], outs[w].at[_dev_index(*me)], local_sems.at[w])
            cp.start()
            mine.append(cp)
            sends = [copy(w, 0, me, sibling, src=stages[w])]
            sends += [copy(w, 1 + j, me, (*chip, c), src=stages[w]) for j, chip in enumerate(chips)]
            for cp in sends:
                cp.start()
            first += sends
        for w in range(n):
            for j, chip in enumerate(chips):
                copy(w, 1 + j, (*chip, c), me).wait_recv()
                cp = copy(w, 4 + j, (*chip, c), sibling)
                cp.start()
                passed.append(cp)
        for w in range(n):
            copy(w, 0, sibling, me).wait_recv()
            for j, chip in enumerate(chips):
                copy(w, 4 + j, (*chip, 1 - c), me).wait_recv()
        for cp in first + passed:
            cp.wait_send()
        for cp in mine:
            cp.wait()

    return pl.pallas_call(
        body, name=name,
        out_shape=[_sds((N_DEV,) + s.shape, out_dtype) for s in shards],
        in_specs=[pl.BlockSpec(memory_space=pltpu.VMEM)] * n,
        out_specs=[pl.BlockSpec(memory_space=pl.ANY)] * n,
        scratch_shapes=[pltpu.VMEM(s.shape, out_dtype) for s in shards]
        + [pltpu.SemaphoreType.DMA((_PEERS * n,)), pltpu.SemaphoreType.DMA((_PEERS * n,)),
           pltpu.SemaphoreType.DMA((n,))],
        compiler_params=pltpu.CompilerParams(vmem_limit_bytes=VMEM_LIMIT),
    )(*shards)


_HBM_SPEC = pl.BlockSpec(memory_space=pltpu.HBM)
_SEM_SPEC = pl.BlockSpec(memory_space=pltpu.SEMAPHORE)
_DATAFLOW = pltpu.SideEffectType.DATAFLOW_SIDE_EFFECTING


def _peer(x, y, c, r):
    return (x ^ ((r >> 2) & 1), y ^ ((r >> 1) & 1), c ^ (r & 1))


def _hbm(a):
    return pltpu.with_memory_space_constraint(a, pltpu.HBM)


def _send_start(sources, blocked, name):
    n = len(sources)
    lands = [lax.empty((N_DEV,) + (s.shape[1:] if blocked else s.shape), s.dtype) for s in sources]

    def body(*refs):
        srcs, zones = refs[:n], refs[n:2 * n]
        send_sems, recv_sems = refs[2 * n:3 * n], refs[3 * n:4 * n]
        token, local_sems = refs[6 * n], refs[6 * n + 1]
        x, y, c = _mesh_pos()
        me = _dev_index(x, y, c)
        local = []
        for w in range(n):
            cp = pltpu.make_async_copy(srcs[w].at[me] if blocked else srcs[w], zones[w].at[me], local_sems.at[w])
            cp.start()
            local.append(cp)
            for r in range(1, N_DEV):
                peer = _peer(x, y, c, r)
                pltpu.make_async_remote_copy(
                    src_ref=srcs[w].at[_dev_index(*peer)] if blocked else srcs[w], dst_ref=zones[w].at[me],
                    send_sem=send_sems[w].at[r - 1], recv_sem=recv_sems[w].at[r - 1],
                    device_id=peer, device_id_type=_MESH).start()
        for cp in local:
            cp.wait()
        token[...] = jnp.zeros_like(token)

    sems = [pltpu.SemaphoreType.DMA((_PEERS,))] * (2 * n)
    out = pl.pallas_call(
        body, name=name,
        out_shape=sems + [pltpu.HBM(a.shape, a.dtype) for a in list(sources) + lands] + [_sds((8, 128), _F32)],
        in_specs=[_HBM_SPEC] * (2 * n),
        out_specs=[_SEM_SPEC] * (2 * n) + [_HBM_SPEC] * (2 * n) + [pl.BlockSpec(memory_space=pltpu.VMEM)],
        input_output_aliases={i: 2 * n + i for i in range(2 * n)},
        scratch_shapes=[pltpu.SemaphoreType.DMA((n,))],
        compiler_params=pltpu.CompilerParams(has_side_effects=_DATAFLOW),
    )(*[_hbm(a) for a in sources], *[_hbm(a) for a in lands])
    return out[:n], out[n:2 * n], out[2 * n:3 * n], out[3 * n:4 * n], out[4 * n]


def _send_wait(send_sems, recv_sems, sources, lands, after, blocked, name):
    n = len(sources)

    def body(*refs):
        srcs, zones = refs[:n], refs[n:2 * n]
        sends, recvs = refs[2 * n:3 * n], refs[3 * n:4 * n]
        x, y, c = _mesh_pos()
        for w in range(n):
            for r in range(1, N_DEV):
                peer = _peer(x, y, c, r)
                idx = _dev_index(*peer)
                cp = pltpu.make_async_remote_copy(
                    src_ref=srcs[w].at[idx] if blocked else srcs[w], dst_ref=zones[w].at[idx],
                    send_sem=sends[w].at[r - 1], recv_sem=recvs[w].at[r - 1],
                    device_id=peer, device_id_type=_MESH)
                cp.wait_send()
                cp.wait_recv()

    out = pl.pallas_call(
        body, name=name,
        out_shape=[pltpu.HBM(a.shape, a.dtype) for a in list(sources) + list(lands)],
        in_specs=[_HBM_SPEC] * (2 * n) + [_SEM_SPEC] * (2 * n) + [pl.BlockSpec(memory_space=pl.ANY)],
        out_specs=[_HBM_SPEC] * (2 * n),
        input_output_aliases={i: i for i in range(2 * n)},
        compiler_params=pltpu.CompilerParams(has_side_effects=_DATAFLOW),
    )(*sources, *lands, *send_sems, *recv_sems, after)
    return out[n:]


def _sequencer_exchange(sources, blocked, name, collective_id):
    n = len(sources)
    flags = blocked

    def body(*refs):
        srcs, zones = refs[:n], refs[n:2 * n]
        send_sems, recv_sems, local_sems = refs[2 * n:]
        x, y, c = _mesh_pos()
        me = _dev_index(x, y, c)
        barrier = pltpu.get_barrier_semaphore()
        for r in range(1, N_DEV):
            pl.semaphore_signal(barrier, inc=1, device_id=_peer(x, y, c, r), device_id_type=_MESH)
        pl.semaphore_wait(barrier, _PEERS)
        local, sends, recvs = [], [], []
        for w in range(n):
            cp = pltpu.make_async_copy(srcs[w].at[me] if flags[w] else srcs[w], zones[w].at[me], local_sems.at[w])
            cp.start()
            local.append(cp)
            for r in range(1, N_DEV):
                peer = _peer(x, y, c, r)
                idx = _dev_index(*peer)
                k = _PEERS * w + r - 1
                src = srcs[w].at[idx] if flags[w] else srcs[w]
                send = pltpu.make_async_remote_copy(
                    src_ref=src, dst_ref=zones[w].at[me], send_sem=send_sems.at[k], recv_sem=recv_sems.at[k],
                    device_id=peer, device_id_type=_MESH)
                send.start()
                sends.append(send)
                recvs.append(pltpu.make_async_remote_copy(
                    src_ref=src, dst_ref=zones[w].at[idx], send_sem=send_sems.at[k], recv_sem=recv_sems.at[k],
                    device_id=peer, device_id_type=_MESH))
        for cp in recvs:
            cp.wait_recv()
        for cp in sends:
            cp.wait_send()
        for cp in local:
            cp.wait()

    return pl.kernel(
        body, name=name,
        out_type=[_sds((N_DEV,) + (s.shape[1:] if f else s.shape), s.dtype) for s, f in zip(sources, flags)],
        mesh=plsc.ScalarSubcoreMesh(axis_name="sequencer", num_cores=1),
        scratch_types=[pltpu.SemaphoreType.DMA((_PEERS * n,)), pltpu.SemaphoreType.DMA((_PEERS * n,)),
                       pltpu.SemaphoreType.DMA((n,))],
        compiler_params=pltpu.CompilerParams(collective_id=collective_id),
    )(*sources)


def _sequencer_gather(shards, name, collective_id):
    n = len(shards)
    fan = 4

    def body(*refs):
        srcs, zones = refs[:n], refs[n:2 * n]
        send_sems, recv_sems, local_sems = refs[2 * n:]
        x, y, c = _mesh_pos()
        me, sibling = (x, y, c), (x, y, 1 - c)
        chips = [(1 - x, y), (x, 1 - y), (1 - x, 1 - y)]
        barrier = pltpu.get_barrier_semaphore()
        for peer in [sibling] + [(*chip, c) for chip in chips]:
            pl.semaphore_signal(barrier, inc=1, device_id=peer, device_id_type=_MESH)
        pl.semaphore_wait(barrier, fan)

        def copy(w, k, block, to, src=None):
            slot = zones[w].at[_dev_index(*block)]
            return pltpu.make_async_remote_copy(
                src_ref=slot if src is None else src, dst_ref=slot,
                send_sem=send_sems.at[_PEERS * w + k], recv_sem=recv_sems.at[_PEERS * w + k],
                device_id=to, device_id_type=_MESH)

        mine, first, passed = [], [], []
        for w in range(n):
            cp = pltpu.make_async_copy(srcs[w], zones[w].at[_dev_index(*me)], local_sems.at[w])
            cp.start()
            mine.append(cp)
            sends = [copy(w, 0, me, sibling, src=srcs[w])]
            sends += [copy(w, 1 + j, me, (*chip, c), src=srcs[w]) for j, chip in enumerate(chips)]
            for cp in sends:
                cp.start()
            first += sends
        for w in range(n):
            for j, chip in enumerate(chips):
                copy(w, 1 + j, (*chip, c), me).wait_recv()
                cp = copy(w, fan + j, (*chip, c), sibling)
                cp.start()
                passed.append(cp)
        for w in range(n):
            copy(w, 0, sibling, me).wait_recv()
            for j, chip in enumerate(chips):
                copy(w, fan + j, (*chip, 1 - c), me).wait_recv()
        for cp in first + passed:
            cp.wait_send()
        for cp in mine:
            cp.wait()

    return pl.kernel(
        body, name=name, out_type=[_sds((N_DEV,) + s.shape, s.dtype) for s in shards],
        mesh=plsc.ScalarSubcoreMesh(axis_name="sequencer", num_cores=1),
        scratch_types=[pltpu.SemaphoreType.DMA((_PEERS * n,)), pltpu.SemaphoreType.DMA((_PEERS * n,)),
                       pltpu.SemaphoreType.DMA((n,))],
        compiler_params=pltpu.CompilerParams(collective_id=collective_id),
    )(*shards)


def _row_tile(rows):
    return next(t for t in range(min(rows, 256), 0, -16) if rows % t == 0)


def _sum_parts(parts, name, tokens=()):
    _, rows, cols = parts.shape
    tr = _row_tile(rows)

    def body(p_ref, g_ref):
        g = p_ref[0].astype(_F32)
        for s in range(1, N_DEV):
            g = g + p_ref[s].astype(_F32)
        g_ref[...] = g

    return _call(body, (rows // tr,), [pl.BlockSpec((N_DEV, tr, cols), lambda i: (0, i, 0))],
                 _rows(tr, cols), _sds((rows, cols), _F32), name, tokens=tokens)(parts)


def _adam_update(g, w, m, v):
    new_m = ADAM_B1 * m + (1.0 - ADAM_B1) * g
    new_v = ADAM_B2 * v + (1.0 - ADAM_B2) * (g * g)
    m_hat = new_m / (1.0 - ADAM_B1 ** ADAM_STEP)
    v_hat = new_v / (1.0 - ADAM_B2 ** ADAM_STEP)
    return -ADAM_LR * (m_hat / (jnp.sqrt(v_hat) + ADAM_EPS) + ADAM_WD * w), new_m, new_v


def _adamw_small(parts, items, sums, name, tokens=()):
    n_p, n_i = len(parts), len(items)

    def body(*refs):
        p_refs, state, outs = refs[:n_p], refs[n_p:n_p + 3 * n_i], refs[n_p + 3 * n_i:]

        def total(part, rows, cols):
            g = p_refs[part][0, rows, cols]
            for s in range(1, N_DEV):
                g = g + p_refs[part][s, rows, cols]
            return g

        for i, (part, rows, cols, _, _, _) in enumerate(items):
            g = total(part, rows, cols)
            w_ref, m_ref, v_ref = state[3 * i:3 * i + 3]
            delta, new_m, new_v = _adam_update(g, w_ref[...], m_ref[...], v_ref[...])
            outs[4 * i][...] = g
            outs[4 * i + 1][...] = delta
            outs[4 * i + 2][...] = new_m
            outs[4 * i + 3][...] = new_v
        for j, (part, rows, cols) in enumerate(sums):
            outs[4 * n_i + j][...] = total(part, rows, cols)

    ins = list(parts) + [a for item in items for a in item[3:]]
    out_shapes = [item[3].shape for item in items for _ in range(4)]
    out_shapes += [(rows.stop - rows.start, cols.stop - cols.start) for _, rows, cols in sums]
    out = _call(body, (1,), [_whole(a.shape) for a in ins], [_whole(s) for s in out_shapes],
                [_sds(s, _F32) for s in out_shapes], name, tokens=tokens)(*ins)
    return [out[4 * i:4 * i + 4] for i in range(n_i)], out[4 * n_i:]


def _adamw(parts, w, m, v, name, tokens=()):
    rows, cols = w.shape
    tr = _row_tile(rows)
    n_parts = parts.shape[0]

    def body(p_ref, w_ref, m_ref, v_ref, g_ref, d_ref, nm_ref, nv_ref):
        g = p_ref[0].astype(_F32)
        for s in range(1, n_parts):
            g = g + p_ref[s].astype(_F32)
        new_m = ADAM_B1 * m_ref[...] + (1.0 - ADAM_B1) * g
        new_v = ADAM_B2 * v_ref[...] + (1.0 - ADAM_B2) * (g * g)
        m_hat = new_m / (1.0 - ADAM_B1 ** ADAM_STEP)
        v_hat = new_v / (1.0 - ADAM_B2 ** ADAM_STEP)
        g_ref[...] = g
        d_ref[...] = -ADAM_LR * (m_hat / (jnp.sqrt(v_hat) + ADAM_EPS) + ADAM_WD * w_ref[...])
        nm_ref[...] = new_m
        nv_ref[...] = new_v

    blk = _rows(tr, cols)
    return _call(body, (rows // tr,),
                 [pl.BlockSpec((n_parts, tr, cols), lambda i: (0, i, 0)), blk, blk, blk],
                 [blk] * 4, [_sds((rows, cols), _F32)] * 4, name, tokens=tokens)(parts, w, m, v)


_SMALL = ("g_pre_mix", "ssm_lambda_re", "ssm_lambda_im", "ssm_log_dt", "ssm_b_re", "ssm_b_im",
          "ssm_c_re", "ssm_c_im", "ssm_d", "b_glu", "attn_sinks", "g_ssm_out", "g_attn_out",
          "g_post_mix", "g_pre_ffn", "g_post_ffn")
_BIG = ("w_in", "w_glu", "w_out", "w_gate_up", "w_down")
_WEIGHTS = ("g_pre_mix", "w_in", "ssm_lambda_re", "ssm_lambda_im", "ssm_log_dt", "ssm_b_re", "ssm_b_im",
            "ssm_c_re", "ssm_c_im", "ssm_d", "w_glu", "b_glu", "attn_sinks", "g_ssm_out", "g_attn_out",
            "w_out", "g_post_mix", "g_pre_ffn", "w_gate_up", "w_down", "g_post_ffn")
_LANES = 128


_SHAPE_2D = {
    "g_pre_mix": (1, D_MODEL), "ssm_lambda_re": (SSM_GROUPS, SSM_STATE), "ssm_lambda_im": (SSM_GROUPS, SSM_STATE),
    "ssm_log_dt": (1, SSM_GROUPS), "ssm_b_re": (SSM_WIDTH, SSM_STATE), "ssm_b_im": (SSM_WIDTH, SSM_STATE),
    "ssm_c_re": (SSM_WIDTH, SSM_STATE), "ssm_c_im": (SSM_WIDTH, SSM_STATE), "ssm_d": (SSM_GROUPS, SSM_GROUP),
    "b_glu": (1, 2 * SSM_WIDTH), "attn_sinks": (1, N_Q_HEADS), "g_ssm_out": (1, SSM_WIDTH),
    "g_attn_out": (1, ATTN_WIDTH), "g_post_mix": (1, D_MODEL), "g_pre_ffn": (1, D_MODEL), "g_post_ffn": (1, D_MODEL)}
_ROW_WIDTH = {"g_pre_mix": D_MODEL, "b_glu": 2 * SSM_WIDTH, "attn_sinks": _LANES, "g_ssm_out": SSM_WIDTH,
              "g_attn_out": ATTN_WIDTH, "g_post_mix": D_MODEL, "g_pre_ffn": D_MODEL, "g_post_ffn": D_MODEL,
              "loss": _LANES}
_DENSE = ()
_PER_GROUP_TRANSPOSED = ("ssm_b_re", "ssm_b_im")


def _to_2d(name, a):
    if name in _PER_GROUP_TRANSPOSED:
        a = a.reshape(SSM_GROUPS, SSM_STATE, SSM_GROUP).transpose(0, 2, 1)
    return a.reshape(_SHAPE_2D[name])


def _from_2d(name, a, shape):
    if name in _PER_GROUP_TRANSPOSED:
        a = a.reshape(SSM_GROUPS, SSM_GROUP, SSM_STATE).transpose(0, 2, 1)
    return a.reshape(shape)


def _row_slots(names):
    slots, row, col = {}, 0, 0
    for n in names:
        width = _ROW_WIDTH[n]
        if col + width > D_MODEL:
            row, col = row + 1, 0
        slots[n] = (row, col, width)
        col += width
    return slots


def _stack_rows(named, slots):
    n_rows = -(-(max(r for r, _, _ in slots.values()) + 1) // 8) * 8
    lines = []
    for r in range(n_rows):
        pieces = [named[n] for n, (row, _, _) in slots.items() if row == r]
        used = sum(p.shape[1] for p in pieces)
        if used < D_MODEL:
            pieces.append(jnp.zeros((1, D_MODEL - used), _F32))
        lines.append(jnp.concatenate(pieces, axis=1) if len(pieces) > 1 else pieces[0])
    return jnp.concatenate(lines, axis=0)


def kernel(x, positions, g_pre_mix, w_in, ssm_lambda_re, ssm_lambda_im, ssm_log_dt, ssm_b_re, ssm_b_im, ssm_c_re, ssm_c_im, ssm_d, w_glu, b_glu, attn_sinks, g_ssm_out, g_attn_out, w_out, g_post_mix, g_pre_ffn, w_gate_up, w_down, g_post_ffn, loss_target, m_g_pre_mix, m_w_in, m_ssm_lambda_re, m_ssm_lambda_im, m_ssm_log_dt, m_ssm_b_re, m_ssm_b_im, m_ssm_c_re, m_ssm_c_im, m_ssm_d, m_w_glu, m_b_glu, m_attn_sinks, m_g_ssm_out, m_g_attn_out, m_w_out, m_g_post_mix, m_g_pre_ffn, m_w_gate_up, m_w_down, m_g_post_ffn, v_g_pre_mix, v_w_in, v_ssm_lambda_re, v_ssm_lambda_im, v_ssm_log_dt, v_ssm_b_re, v_ssm_b_im, v_ssm_c_re, v_ssm_c_im, v_ssm_d, v_w_glu, v_b_glu, v_attn_sinks, v_g_ssm_out, v_g_attn_out, v_w_out, v_g_post_mix, v_g_pre_ffn, v_w_gate_up, v_w_down, v_g_post_ffn):
    w = dict(g_pre_mix=g_pre_mix, w_in=w_in, ssm_lambda_re=ssm_lambda_re, ssm_lambda_im=ssm_lambda_im,
             ssm_log_dt=ssm_log_dt, ssm_b_re=ssm_b_re, ssm_b_im=ssm_b_im, ssm_c_re=ssm_c_re, ssm_c_im=ssm_c_im,
             ssm_d=ssm_d, w_glu=w_glu, b_glu=b_glu, attn_sinks=attn_sinks, g_ssm_out=g_ssm_out,
             g_attn_out=g_attn_out, w_out=w_out, g_post_mix=g_post_mix, g_pre_ffn=g_pre_ffn,
             w_gate_up=w_gate_up, w_down=w_down, g_post_ffn=g_post_ffn)
    m = dict(g_pre_mix=m_g_pre_mix, w_in=m_w_in, ssm_lambda_re=m_ssm_lambda_re, ssm_lambda_im=m_ssm_lambda_im,
             ssm_log_dt=m_ssm_log_dt, ssm_b_re=m_ssm_b_re, ssm_b_im=m_ssm_b_im, ssm_c_re=m_ssm_c_re,
             ssm_c_im=m_ssm_c_im, ssm_d=m_ssm_d, w_glu=m_w_glu, b_glu=m_b_glu, attn_sinks=m_attn_sinks,
             g_ssm_out=m_g_ssm_out, g_attn_out=m_g_attn_out, w_out=m_w_out, g_post_mix=m_g_post_mix,
             g_pre_ffn=m_g_pre_ffn, w_gate_up=m_w_gate_up, w_down=m_w_down, g_post_ffn=m_g_post_ffn)
    v = dict(g_pre_mix=v_g_pre_mix, w_in=v_w_in, ssm_lambda_re=v_ssm_lambda_re, ssm_lambda_im=v_ssm_lambda_im,
             ssm_log_dt=v_ssm_log_dt, ssm_b_re=v_ssm_b_re, ssm_b_im=v_ssm_b_im, ssm_c_re=v_ssm_c_re,
             ssm_c_im=v_ssm_c_im, ssm_d=v_ssm_d, w_glu=v_w_glu, b_glu=v_b_glu, attn_sinks=v_attn_sinks,
             g_ssm_out=v_g_ssm_out, g_attn_out=v_g_attn_out, w_out=v_w_out, g_post_mix=v_g_post_mix,
             g_pre_ffn=v_g_pre_ffn, w_gate_up=v_w_gate_up, w_down=v_w_down, g_post_ffn=v_g_post_ffn)

    transposed = ("w_in", "w_glu", "w_gate_up")
    native_transposed = ("w_in", "w_gate_up")
    shard = {n: (w[n][0].T if n in transposed else w[n][0]).astype(_BF16) for n in _BIG}
    gathered = {}
    for names, lands in (
            (("w_in",), _sequencer_exchange([shard["w_in"]], [False], "gather_w_in", 1)),
            (("w_glu", "w_out"), _sequencer_exchange([shard["w_glu"], shard["w_out"]], [False] * 2, "gather_mix", 2)),
            (("w_gate_up", "w_down"), _sequencer_gather([shard["w_gate_up"], shard["w_down"]], "gather_ffn", 3))):
        gathered.update({n: a.reshape(-1, a.shape[2]) for n, a in zip(names, lands)})

    def fetch(names, after):
        del after
        return [gathered[n] for n in names]

    sent = []

    def publish(named):
        big = [n for n in named if n in _BIG]
        rows = [n for n in named if n in _ROW_WIDTH]
        dense = [n for n in named if n in _DENSE]
        plain = [n for n in named if n not in big + rows + dense]
        sources = [named[n].reshape(N_DEV, -1, named[n].shape[1]) for n in big]
        slots = _row_slots(rows)
        if rows:
            sources.append(_stack_rows(named, slots))
        sources += [named[n].reshape(-1, _LANES) for n in dense] + [named[n] for n in plain]
        flags = [True] * len(big) + [False] * (len(sources) - len(big))
        cid = 4 + len(sent)
        sent.append((big, slots, dense, plain, _sequencer_exchange(sources, flags, "grads_%d" % cid, cid)))
        return [named[n] for n in big]

    p = {n: w[n] for n in _SMALL}
    grad_x = _local_step(x[0], positions[0], loss_target[0], p, fetch, publish)

    state = {n: [_to_2d(n, a) for a in (w[n], m[n], v[n])] for n in _SMALL}
    result = {}
    total_loss = None
    chain = []
    for big, slots, dense, plain, lands in sent:
        lands = list(lands)
        after = list(chain)
        for name in big:
            part = lands.pop(0)
            if name in native_transposed:
                updated = _adamw(part, w[name][0].T, m[name][0].T, v[name][0].T, "adamw_" + name, after)
                result[name] = [a.T[None] for a in updated]
                chain = [updated[3]]
                continue
            if name in transposed:
                part = _sum_parts(part, "sum_" + name, after).T[None]
            updated = _adamw(part, w[name][0], m[name][0], v[name][0], "adamw_" + name, after)
            result[name] = [a[None] for a in updated]
            chain = [updated[3]]
        parts, items, sums, names = [], [], [], []
        if slots:
            parts.append(lands.pop(0))
            for name, (row, col, _) in slots.items():
                if name == "loss":
                    sums.append((0, slice(row, row + 1), slice(col, col + _LANES)))
                else:
                    items.append((0, slice(row, row + 1), slice(col, col + _SHAPE_2D[name][1]), *state[name]))
                    names.append(name)
        for name in dense:
            part = lands.pop(0).reshape((N_DEV,) + _SHAPE_2D[name])
            result[name] = _adamw(part, *state[name], "adamw_" + name, after)
            chain = [result[name][3]]
        for name in plain:
            rows_n, cols_n = _SHAPE_2D[name]
            items.append((len(parts), slice(0, rows_n), slice(0, cols_n), *state[name]))
            parts.append(lands.pop(0))
            names.append(name)
        if items:
            updated, summed = _adamw_small(parts, items, sums, "adamw_small_" + names[0], after)
            chain = [updated[0][3]]
            result.update(dict(zip(names, updated)))
            if summed:
                total_loss = summed[0][0, 0]

    out = [total_loss, grad_x[None]]
    for kind in range(4):
        out += [_from_2d(n, result[n][kind], w[n].shape) for n in _WEIGHTS]
    return tuple(out)
```

```python
import functools
import math

import numpy as np
import jax
import jax.numpy as jnp
from jax import lax
from jax.experimental import pallas as pl
from jax.experimental.pallas import tpu as pltpu
from jax.experimental.pallas import tpu_sc as plsc

D_MODEL = 1024
SSM_WIDTH = 512
SSM_GROUP = 16
SSM_GROUPS = 32
SSM_STATE = 64
N_STATE = SSM_GROUPS * SSM_STATE
ATTN_WIDTH = 512
HEAD_DIM = 64
N_Q_HEADS = 8
N_KV_HEADS = 2
Q_PER_KV = 4
KV_WIDTH = 128
IN_WIDTH = 1280
BLOCK = 128
ROPE_DIM = 16
ROPE_THETA = 500000.0
D_FF = 2816
NORM_EPS = 1e-6
MASK_VALUE = -1e30
ADAM_LR = 0.001
ADAM_B1 = 0.9
ADAM_B2 = 0.999
ADAM_EPS = 1e-08
ADAM_WD = 0.01
ADAM_STEP = 10

N_DEV = 8
SCAN_CHUNKS = 8
SCAN_COLS = 512
TOKEN_TILE = 256
VMEM_LIMIT = 56 * 1024 * 1024

_F32 = jnp.float32
_BF16 = jnp.bfloat16
_MXU = jnp.bfloat16

_NN = ((1,), (0,))
_NT = ((1,), (1,))
_TN = ((0,), (0,))


def _dot(a, b, dims):
    return lax.dot_general(a.astype(_MXU), b.astype(_MXU), (dims, ((), ())),
                           preferred_element_type=_F32)


def _dot_exact(a, b, dims):
    return lax.dot_general(a.astype(_F32), b.astype(_F32), (dims, ((), ())),
                           precision=lax.Precision.HIGHEST, preferred_element_type=_F32)


def _iota(shape, dim):
    return lax.broadcasted_iota(jnp.int32, shape, dim)


def _rms_fwd(x, g):
    r = lax.rsqrt(jnp.mean(x * x, axis=-1, keepdims=True) + NORM_EPS)
    return x * r * g, r


def _rms_bwd(dy, x, g, r):
    a = dy * g
    xn = x * r
    dx = r * (a - xn * jnp.mean(a * xn, axis=-1, keepdims=True))
    dg = jnp.sum(dy * xn, axis=0, keepdims=True)
    return dx, dg


def _call(body, grid, in_specs, out_specs, out_shape, name, scratch=(), tokens=()):
    params = pltpu.CompilerParams(dimension_semantics=("arbitrary",) * len(grid),
                                  vmem_limit_bytes=VMEM_LIMIT)
    n_in, n_tok = len(in_specs), len(tokens)

    def run(*refs):
        return body(*refs[:n_in], *refs[n_in + n_tok:])

    call = pl.pallas_call(run, grid=grid,
                          in_specs=list(in_specs) + [pl.BlockSpec(memory_space=pl.ANY)] * n_tok,
                          out_specs=out_specs, out_shape=out_shape, scratch_shapes=list(scratch),
                          compiler_params=params, name=name)
    return lambda *args: call(*args, *tokens)


def _rows(tm, n):
    return pl.BlockSpec((tm, n), lambda i: (i, 0))


def _whole(shape):
    nd = len(shape)
    return pl.BlockSpec(shape, lambda i: (0,) * nd)


def _sds(shape, dtype):
    return jax.ShapeDtypeStruct(shape, dtype)


def _tile(L):
    return min(TOKEN_TILE, L)


def _accumulate(ref, val, first):
    @pl.when(first)
    def _():
        ref[...] = val

    @pl.when(jnp.logical_not(first))
    def _():
        ref[...] += val


def _rope_rows():
    half = ROPE_DIM // 2
    inv = (np.float32(ROPE_THETA) ** (-np.arange(half, dtype=np.float32) * np.float32(2.0) / np.float32(ROPE_DIM))).astype(np.float32)
    col = np.arange(KV_WIDTH) % HEAD_DIM
    freq = np.where(col < ROPE_DIM, inv[col % half], 0.0).astype(np.float32)
    sign = np.where(col < half, -1.0, np.where(col < ROPE_DIM, 1.0, 0.0)).astype(np.float32)
    return freq[None, :], sign[None, :]


def _rope_tables(pos_col):
    L = pos_col.shape[0]
    tm = _tile(L)
    freq, sign = _rope_rows()

    def body(pos_ref, freq_ref, sign_ref, cos_ref, sin_ref):
        ang = pos_ref[...].astype(_F32) * freq_ref[...]
        cos_ref[...] = jnp.cos(ang)
        sin_ref[...] = jnp.sin(ang) * sign_ref[...]

    return _call(body, (L // tm,),
                 [_rows(tm, 1), _whole((1, KV_WIDTH)), _whole((1, KV_WIDTH))],
                 [_rows(tm, KV_WIDTH), _rows(tm, KV_WIDTH)],
                 [_sds((L, KV_WIDTH), _F32)] * 2, "rope_tables")(pos_col, jnp.asarray(freq), jnp.asarray(sign))


def _widen(t, width):
    return t if width == KV_WIDTH else jnp.concatenate([t] * (width // KV_WIDTH), axis=1)


def _rope_partner(t):
    w = t.shape[1]
    in_head = _iota((1, w), 1) & (HEAD_DIM - 1)
    second = jnp.where(in_head < ROPE_DIM, pltpu.roll(t, ROPE_DIM // 2, 1), 0.0)
    return jnp.where(in_head < ROPE_DIM // 2, pltpu.roll(t, w - ROPE_DIM // 2, 1), second)


def _rope_apply(t, cos_t, sin_t):
    w = t.shape[1]
    return t * _widen(cos_t, w) + _rope_partner(t) * _widen(sin_t, w)


def _rope_transpose(dt, cos_t, sin_t):
    w = dt.shape[1]
    return dt * _widen(cos_t, w) + _rope_partner(dt * _widen(sin_t, w))


def _in_proj(x, g_pre_mix, w_in, cos_t, sin_t):
    L = x.shape[0]
    tm = _tile(L)

    def body(x_ref, g_ref, w_ref, cos_ref, sin_ref, hn_ref, u_ref, q_ref, k_ref, v_ref):
        hn, _ = _rms_fwd(x_ref[...], g_ref[...])
        hn = hn.astype(_BF16)
        hn_ref[...] = hn
        proj = _dot(hn, w_ref[...], _NT)
        u_ref[...] = proj[:, :SSM_WIDTH]
        q = proj[:, SSM_WIDTH:SSM_WIDTH + ATTN_WIDTH]
        k = proj[:, SSM_WIDTH + ATTN_WIDTH:SSM_WIDTH + ATTN_WIDTH + KV_WIDTH]
        cos_v, sin_v = cos_ref[...], sin_ref[...]
        q_ref[...] = _rope_apply(q, cos_v, sin_v).astype(_BF16)
        k_ref[...] = _rope_apply(k, cos_v, sin_v).astype(_BF16)
        v_ref[...] = proj[:, SSM_WIDTH + ATTN_WIDTH + KV_WIDTH:].astype(_BF16)

    return _call(body, (L // tm,),
                 [_rows(tm, D_MODEL), _whole((1, D_MODEL)), _whole((IN_WIDTH, D_MODEL)),
                  _rows(tm, KV_WIDTH), _rows(tm, KV_WIDTH)],
                 [_rows(tm, D_MODEL), _rows(tm, SSM_WIDTH), _rows(tm, ATTN_WIDTH),
                  _rows(tm, KV_WIDTH), _rows(tm, KV_WIDTH)],
                 [_sds((L, D_MODEL), _BF16), _sds((L, SSM_WIDTH), _F32), _sds((L, ATTN_WIDTH), _BF16),
                  _sds((L, KV_WIDTH), _BF16), _sds((L, KV_WIDTH), _BF16)],
                 "in_proj")(x, g_pre_mix, w_in, cos_t, sin_t)


def _s5_discretize(lam_re, lam_im, log_dt):
    lr = jnp.minimum(lam_re, -1e-4)
    li = lam_im
    dt = jnp.exp(log_dt)
    mag = jnp.exp(lr * dt)
    ar = mag * jnp.cos(li * dt)
    ai = mag * jnp.sin(li * dt)
    den = lr * lr + li * li
    fr = ((ar - 1.0) * lr + ai * li) / den
    fi = (ai * lr - (ar - 1.0) * li) / den
    return ar, ai, fr, fi


def _s5_bbar(lam_re, lam_im, log_dt, b_re, b_im):
    ar, ai, fr, fi = _s5_discretize(lam_re, lam_im, log_dt)
    return ar, ai, fr * b_re - fi * b_im, fr * b_im + fi * b_re


def _spread_masks():
    e16 = (_iota((SSM_GROUP, SSM_WIDTH), 1) & (SSM_GROUP - 1)) == _iota((SSM_GROUP, SSM_WIDTH), 0)
    e64 = (_iota((SSM_STATE, N_STATE), 1) & (SSM_STATE - 1)) == _iota((SSM_STATE, N_STATE), 0)
    mask_b = (_iota((N_STATE, SSM_WIDTH), 0) >> 6) == (_iota((N_STATE, SSM_WIDTH), 1) >> 4)
    mask_c = (_iota((SSM_WIDTH, N_STATE), 0) >> 4) == (_iota((SSM_WIDTH, N_STATE), 1) >> 6)
    return e16.astype(_F32), e64.astype(_F32), mask_b, mask_c


SUPER = 4
SB_STATE = N_STATE // SUPER
SB_WIDTH = SSM_WIDTH // SUPER


def _sb_state(k):
    return slice(SB_STATE * k, SB_STATE * (k + 1))


def _sb_width(k):
    return slice(SB_WIDTH * k, SB_WIDTH * (k + 1))


def _dt_column(log_dt_row):
    eye = _iota((SSM_GROUPS, SSM_GROUPS), 0) == _iota((SSM_GROUPS, SSM_GROUPS), 1)
    return jnp.sum(jnp.where(eye, log_dt_row, 0.0), axis=1, keepdims=True)


def _group_masks():
    e64 = ((_iota((SSM_STATE, N_STATE), 1) & (SSM_STATE - 1)) == _iota((SSM_STATE, N_STATE), 0)).astype(_F32)
    own = _iota((SSM_GROUPS, N_STATE), 0) == (_iota((SSM_GROUPS, N_STATE), 1) >> 6)
    return e64, own


def _rows_of_group():
    return ((_iota((SSM_WIDTH, SSM_GROUPS), 0) >> 4) == _iota((SSM_WIDTH, SSM_GROUPS), 1)).astype(_F32)


def _ssm_prep(lam_re, lam_im, log_dt, b_re, b_im, c_re, c_im):
    def body(lr_ref, li_ref, ld_ref, bre, bim, cre, cim, ar_ref, ai_ref, btr, bti, ctr, cti):
        ar, ai, fr, fi = _s5_discretize(lr_ref[...], li_ref[...], _dt_column(ld_ref[...]))
        e64, own = _group_masks()
        mask_c = (_iota((SSM_WIDTH, N_STATE), 0) >> 4) == (_iota((SSM_WIDTH, N_STATE), 1) >> 6)

        def to_row(t):
            return jnp.sum(jnp.where(own, _dot_exact(t, e64, _NN), 0.0), axis=0, keepdims=True)

        def fold(m):
            full = jnp.where(mask_c, _dot(m, e64, _NN), 0.0)
            return sum(full[_sb_width(k), :] for k in range(SUPER)).astype(_BF16)

        ar_ref[...] = to_row(ar)
        ai_ref[...] = to_row(ai)
        spread = _rows_of_group()
        fr_t = _dot_exact(spread, fr, _NN)
        fi_t = _dot_exact(spread, fi, _NN)
        btr[...] = fold(fr_t * bre[...] - fi_t * bim[...])
        bti[...] = fold(fr_t * bim[...] + fi_t * bre[...])
        ctr[...] = fold(cre[...])
        cti[...] = fold(cim[...])

    row = (1, N_STATE)
    ins = [lam_re, lam_im, log_dt, b_re, b_im, c_re, c_im]
    return _call(body, (1,), [_whole(a.shape) for a in ins],
                 [_whole(row), _whole(row)] + [_whole((SB_WIDTH, N_STATE))] * 4,
                 [_sds(row, _F32), _sds(row, _F32)] + [_sds((SB_WIDTH, N_STATE), _BF16)] * 4,
                 "ssm_prep")(*ins)


def _ssm_bu(u, bt_re, bt_im):
    L = u.shape[0]
    tm = _tile(L)

    def body(u_ref, br_ref, bi_ref, or_ref, oi_ref):
        for k in range(SUPER):
            ub = u_ref[:, _sb_width(k)].astype(_BF16)
            or_ref[:, _sb_state(k)] = _dot(ub, br_ref[:, _sb_state(k)], _NN)
            oi_ref[:, _sb_state(k)] = _dot(ub, bi_ref[:, _sb_state(k)], _NN)

    return _call(body, (L // tm,),
                 [_rows(tm, SSM_WIDTH), _whole((SB_WIDTH, N_STATE)), _whole((SB_WIDTH, N_STATE))],
                 [_rows(tm, N_STATE), _rows(tm, N_STATE)],
                 [_sds((L, N_STATE), _F32)] * 2, "ssm_bu")(u, bt_re, bt_im)


def _complex_power(ar, ai, n):
    def step(_, c):
        pr, pi = c
        return pr * ar - pi * ai, pr * ai + pi * ar
    return lax.fori_loop(0, n, step, (jnp.ones_like(ar), jnp.zeros_like(ai)))


def _chunk_carries(er, ei, pr, pi, reverse):
    rows = _iota(er.shape, 0)
    sr = jnp.zeros_like(pr)
    si = jnp.zeros_like(pi)
    out_r = jnp.zeros_like(er)
    out_i = jnp.zeros_like(ei)
    order = range(SCAN_CHUNKS - 1, 0, -1) if reverse else range(SCAN_CHUNKS - 1)
    for c in order:
        e_r = er[c:c + 1, :]
        e_i = ei[c:c + 1, :]
        sr, si = pr * sr - pi * si + e_r, pr * si + pi * sr + e_i
        nxt = c - 1 if reverse else c + 1
        out_r = jnp.where(rows == nxt, sr, out_r)
        out_i = jnp.where(rows == nxt, si, out_i)
    return out_r, out_i


def _scan_fwd(b_re, b_im, a_re, a_im):
    T = b_re.shape[0]
    W = SCAN_COLS
    blk = pl.BlockSpec((T, SCAN_CHUNKS, W), lambda j: (0, 0, j))
    vec = pl.BlockSpec((1, W), lambda j: (0, j))

    def body(br_ref, bi_ref, ar_ref, ai_ref, xr_ref, xi_ref):
        ar, ai = ar_ref[...], ai_ref[...]
        ar8 = jnp.broadcast_to(ar, (SCAN_CHUNKS, W))
        ai8 = jnp.broadcast_to(ai, (SCAN_CHUNKS, W))

        def local(t, c):
            cr, ci = c
            return ar8 * cr - ai8 * ci + br_ref[t], ar8 * ci + ai8 * cr + bi_ref[t]

        zero = jnp.zeros((SCAN_CHUNKS, W), _F32)
        er, ei = lax.fori_loop(0, T, local, (zero, zero))
        pr, pi = _complex_power(ar, ai, T)
        sr, si = _chunk_carries(er, ei, pr, pi, reverse=False)

        def final(t, c):
            nr, ni = local(t, c)
            xr_ref[t] = nr
            xi_ref[t] = ni
            return nr, ni

        lax.fori_loop(0, T, final, (sr, si))

    shape = _sds(b_re.shape, _F32)
    return _call(body, (N_STATE // W,), [blk, blk, vec, vec], [blk, blk], [shape, shape],
                 "scan_fwd")(b_re, b_im, a_re, a_im)


def _scan_bwd(dx_re, dx_im, x_re, x_im, a_re, a_im, tokens=()):
    T = dx_re.shape[0]
    W = SCAN_COLS
    blk = pl.BlockSpec((T, SCAN_CHUNKS, W), lambda j: (0, 0, j))
    vec = pl.BlockSpec((1, W), lambda j: (0, j))

    def body(dr_ref, di_ref, xr_ref, xi_ref, ar_ref, ai_ref, lr_ref, li_ref, dar_ref, dai_ref):
        ar, ai = ar_ref[...], ai_ref[...]
        ar8 = jnp.broadcast_to(ar, (SCAN_CHUNKS, W))
        ai8 = jnp.broadcast_to(ai, (SCAN_CHUNKS, W))

        def local(t, c):
            cr, ci = c
            return ar8 * cr + ai8 * ci + dr_ref[t], ar8 * ci - ai8 * cr + di_ref[t]

        zero = jnp.zeros((SCAN_CHUNKS, W), _F32)
        er, ei = lax.fori_loop(0, T, lambda k, c: local(T - 1 - k, c), (zero, zero))
        pr, pi = _complex_power(ar, -ai, T)
        sr, si = _chunk_carries(er, ei, pr, pi, reverse=True)

        def grad_a(acc, nr, ni, xpr, xpi):
            return acc[0] + nr * xpr + ni * xpi, acc[1] + ni * xpr - nr * xpi

        def final(k, c):
            t = T - 1 - k
            nr, ni = local(t, c[:2])
            lr_ref[t] = nr
            li_ref[t] = ni
            gr, gi = grad_a(c[2:], nr, ni, xr_ref[t - 1], xi_ref[t - 1])
            return nr, ni, gr, gi

        cr, ci, gr, gi = lax.fori_loop(0, T - 1, final, (sr, si, zero, zero))
        nr, ni = local(0, (cr, ci))
        lr_ref[0] = nr
        li_ref[0] = ni
        first = _iota((SCAN_CHUNKS, W), 0) == 0
        xpr = jnp.where(first, 0.0, pltpu.roll(xr_ref[T - 1], 1, 0))
        xpi = jnp.where(first, 0.0, pltpu.roll(xi_ref[T - 1], 1, 0))
        gr, gi = grad_a((gr, gi), nr, ni, xpr, xpi)
        dar_ref[...] = jnp.sum(gr, axis=0, keepdims=True)
        dai_ref[...] = jnp.sum(gi, axis=0, keepdims=True)

    shape = _sds(dx_re.shape, _F32)
    row = _sds((1, N_STATE), _F32)
    return _call(body, (N_STATE // W,), [blk, blk, blk, blk, vec, vec], [blk, blk, vec, vec],
                 [shape, shape, row, row], "scan_bwd", tokens=tokens)(dx_re, dx_im, x_re, x_im, a_re, a_im)


_GELU_K = math.sqrt(2.0 / math.pi)
_GELU_C = 0.044715


def _gelu(y):
    return 0.5 * y * (1.0 + jnp.tanh(_GELU_K * (y + _GELU_C * y * y * y)))


def _gelu_grad(y):
    t = jnp.tanh(_GELU_K * (y + _GELU_C * y * y * y))
    return 0.5 * (1.0 + t) + 0.5 * y * (1.0 - t * t) * _GELU_K * (1.0 + 3.0 * _GELU_C * y * y)


def _ssm_out(x_re, x_im, u, ct_re, ct_im, d_row, w_glu, b_glu, g_ssm):
    L = u.shape[0]
    tm = _tile(L)

    def body(xr_ref, xi_ref, u_ref, cr_ref, ci_ref, d_ref, w_ref, b_ref, g_ref, y_ref, z_ref, n_ref):
        cx = [_dot(xr_ref[:, _sb_state(k)], cr_ref[:, _sb_state(k)], _NT)
              - _dot(xi_ref[:, _sb_state(k)], ci_ref[:, _sb_state(k)], _NT) for k in range(SUPER)]
        y = jnp.concatenate(cx, axis=1) + d_ref[...] * u_ref[...]
        y_ref[...] = y
        z = _dot(_gelu(y), w_ref[...], _NT) + b_ref[...]
        z_ref[...] = z
        out = z[:, :SSM_WIDTH] * jax.nn.sigmoid(z[:, SSM_WIDTH:])
        n, _ = _rms_fwd(out, g_ref[...])
        n_ref[...] = n.astype(_BF16)

    return _call(body, (L // tm,),
                 [_rows(tm, N_STATE), _rows(tm, N_STATE), _rows(tm, SSM_WIDTH),
                  _whole((SB_WIDTH, N_STATE)), _whole((SB_WIDTH, N_STATE)), _whole((1, SSM_WIDTH)),
                  _whole((2 * SSM_WIDTH, SSM_WIDTH)), _whole((1, 2 * SSM_WIDTH)), _whole((1, SSM_WIDTH))],
                 [_rows(tm, SSM_WIDTH), _rows(tm, 2 * SSM_WIDTH), _rows(tm, SSM_WIDTH)],
                 [_sds((L, SSM_WIDTH), _F32), _sds((L, 2 * SSM_WIDTH), _F32), _sds((L, SSM_WIDTH), _BF16)],
                 "ssm_out")(x_re, x_im, u, ct_re, ct_im, d_row, w_glu, b_glu, g_ssm)


def _ssm_out_bwd(dn, y, z, u, ct_re, ct_im, d_row, w_glu, g_ssm):
    L = u.shape[0]
    tm = _tile(L)

    def body(dn_ref, y_ref, z_ref, u_ref, cr_ref, ci_ref, d_ref, w_ref, g_ref,
             gy_ref, dz_ref, dy_ref, dud_ref, dxr_ref, dxi_ref, dg_ref, db_ref, dd_ref):
        first = pl.program_id(0) == 0
        z = z_ref[...]
        z1, z2 = z[:, :SSM_WIDTH], z[:, SSM_WIDTH:]
        sig = jax.nn.sigmoid(z2)
        out = z1 * sig
        g = g_ref[...]
        _, r = _rms_fwd(out, g)
        dout, dg = _rms_bwd(dn_ref[...], out, g, r)
        _accumulate(dg_ref, dg, first)
        dz = jnp.concatenate([dout * sig, dout * z1 * sig * (1.0 - sig)], axis=1)
        _accumulate(db_ref, jnp.sum(dz, axis=0, keepdims=True), first)
        dzb = dz.astype(_BF16)
        dz_ref[...] = dzb
        y = y_ref[...]
        gy_ref[...] = _gelu(y).astype(_BF16)
        dy = _dot(dzb, w_ref[...], _NN) * _gelu_grad(y)
        u = u_ref[...]
        _accumulate(dd_ref, jnp.sum(dy * u, axis=0, keepdims=True), first)
        dud_ref[...] = d_ref[...] * dy
        dyb = dy.astype(_BF16)
        dy_ref[...] = dyb
        for k in range(SUPER):
            dxr_ref[:, _sb_state(k)] = _dot(dyb[:, _sb_width(k)], cr_ref[:, _sb_state(k)], _NN)
            dxi_ref[:, _sb_state(k)] = -_dot(dyb[:, _sb_width(k)], ci_ref[:, _sb_state(k)], _NN)

    row = _whole((1, SSM_WIDTH))
    return _call(body, (L // tm,),
                 [_rows(tm, SSM_WIDTH), _rows(tm, SSM_WIDTH), _rows(tm, 2 * SSM_WIDTH), _rows(tm, SSM_WIDTH),
                  _whole((SB_WIDTH, N_STATE)), _whole((SB_WIDTH, N_STATE)), row,
                  _whole((2 * SSM_WIDTH, SSM_WIDTH)), row],
                 [_rows(tm, SSM_WIDTH), _rows(tm, 2 * SSM_WIDTH), _rows(tm, SSM_WIDTH), _rows(tm, SSM_WIDTH),
                  _rows(tm, N_STATE), _rows(tm, N_STATE), row, _whole((1, 2 * SSM_WIDTH)), row],
                 [_sds((L, SSM_WIDTH), _BF16), _sds((L, 2 * SSM_WIDTH), _BF16), _sds((L, SSM_WIDTH), _BF16),
                  _sds((L, SSM_WIDTH), _F32), _sds((L, N_STATE), _F32), _sds((L, N_STATE), _F32),
                  _sds((1, SSM_WIDTH), _F32), _sds((1, 2 * SSM_WIDTH), _F32), _sds((1, SSM_WIDTH), _F32)],
                 "ssm_out_bwd")(dn, y, z, u, ct_re, ct_im, d_row, w_glu, g_ssm)


def _ssm_du(lam_re, lam_im, bt_re, bt_im, dud):
    L = dud.shape[0]
    tm = _tile(L)

    def body(lr_ref, li_ref, br_ref, bi_ref, dud_ref, du_ref):
        for k in range(SUPER):
            du_ref[:, _sb_width(k)] = (_dot(lr_ref[:, _sb_state(k)], br_ref[:, _sb_state(k)], _NT)
                                       + _dot(li_ref[:, _sb_state(k)], bi_ref[:, _sb_state(k)], _NT)
                                       + dud_ref[:, _sb_width(k)])

    return _call(body, (L // tm,),
                 [_rows(tm, N_STATE), _rows(tm, N_STATE), _whole((SB_WIDTH, N_STATE)),
                  _whole((SB_WIDTH, N_STATE)), _rows(tm, SSM_WIDTH)],
                 _rows(tm, SSM_WIDTH), _sds((L, SSM_WIDTH), _F32), "ssm_du")(lam_re, lam_im, bt_re, bt_im, dud)


def _ssm_weight_grads(dy, x_re, x_im, lam_re, lam_im, u):
    L = u.shape[0]

    def body(dy_ref, xr_ref, xi_ref, lr_ref, li_ref, u_ref, dcr_ref, dci_ref, dbr_ref, dbi_ref):
        dyb = dy_ref[...]
        ub = u_ref[...].astype(_BF16)
        dcr_ref[...] = _dot(dyb, xr_ref[...], _TN)
        dci_ref[...] = _dot(dyb, xi_ref[...], _TN)
        dbr_ref[...] = _dot(ub, lr_ref[...], _TN)
        dbi_ref[...] = _dot(ub, li_ref[...], _TN)

    width = pl.BlockSpec((L, SB_WIDTH), lambda k: (0, k))
    state = pl.BlockSpec((L, SB_STATE), lambda k: (0, k))
    out = pl.BlockSpec((SB_WIDTH, SB_STATE), lambda k: (0, k))
    return _call(body, (SUPER,), [width, state, state, state, state, width], [out] * 4,
                 [_sds((SB_WIDTH, N_STATE), _F32)] * 4,
                 "ssm_weight_grads")(dy, x_re, x_im, lam_re, lam_im, u)


def _ssm_param_bwd(da_re, da_im, dbt_re, dbt_im, dct_re, dct_im, lam_re, lam_im, log_dt, b_re, b_im):
    def body(dar, dai, dbr, dbi, dcr, dci, lr_ref, li_ref, ld_ref, bre_ref, bim_ref,
             glr, gli, gdt, gbr, gbi, gcr, gci):
        own_c = (_iota((SB_WIDTH, SB_STATE), 0) >> 4) == (_iota((SB_WIDTH, SB_STATE), 1) >> 6)

        def unfold(ref):
            blocks = []
            for k in range(SUPER):
                t = jnp.where(own_c, ref[:, _sb_state(k)], 0.0)
                t = sum(t[:, 128 * i:128 * (i + 1)] for i in range(SB_STATE // 128))
                blocks.append((t + pltpu.roll(t, SSM_STATE, 1))[:, :SSM_STATE])
            return jnp.concatenate(blocks, axis=0)

        gcr[...] = unfold(dcr)
        gci[...] = -unfold(dci)
        dbb_re, dbb_im = unfold(dbr), unfold(dbi)
        b_re, b_im = bre_ref[...], bim_ref[...]
        dt_col = _dt_column(ld_ref[...])
        (_, _, fr, fi), vjp = jax.vjp(_s5_discretize, lr_ref[...], li_ref[...], dt_col)
        spread = _rows_of_group()
        fr_t = _dot_exact(spread, fr, _NN)
        fi_t = _dot_exact(spread, fi, _NN)
        gbr[...] = fr_t * dbb_re + fi_t * dbb_im
        gbi[...] = fr_t * dbb_im - fi_t * dbb_re
        d_fr = _dot_exact(spread, dbb_re * b_re + dbb_im * b_im, _TN)
        d_fi = _dot_exact(spread, dbb_im * b_re - dbb_re * b_im, _TN)
        e64, own = _group_masks()

        def from_row(ref):
            return _dot_exact(jnp.where(own, ref[...], 0.0), e64, _NT)

        d_lr, d_li, d_dt = vjp((from_row(dar), from_row(dai), d_fr, d_fi))
        glr[...] = d_lr
        gli[...] = d_li
        eye = (_iota((SSM_GROUPS, SSM_GROUPS), 0) == _iota((SSM_GROUPS, SSM_GROUPS), 1)).astype(_F32)
        gdt[...] = _dot_exact(jnp.broadcast_to(d_dt, (SSM_GROUPS, 128)), eye, _TN)[0:1]

    ins = [da_re, da_im, dbt_re, dbt_im, dct_re, dct_im, lam_re, lam_im, log_dt, b_re, b_im]
    outs = [(SSM_GROUPS, SSM_STATE), (SSM_GROUPS, SSM_STATE), (1, SSM_GROUPS)] + [(SSM_WIDTH, SSM_STATE)] * 4
    return _call(body, (1,), [_whole(a.shape) for a in ins], [_whole(s) for s in outs],
                 [_sds(s, _F32) for s in outs], "ssm_param_bwd")(*ins)


def _head_spread(j):
    r = _iota((KV_WIDTH, 256), 0)
    c = _iota((KV_WIDTH, 256), 1)
    return (r == HEAD_DIM * j + (c & (HEAD_DIM - 1))).astype(_BF16)


STACK = Q_PER_KV * BLOCK


def _stack_heads(t):
    lane_head = _iota((1, 256), 1) >> 6
    return jnp.concatenate([jnp.where(lane_head == g, t, jnp.zeros_like(t)) for g in range(Q_PER_KV)], axis=0)


def _unstack_heads(t):
    lane_head = _iota((1, 256), 1) >> 6
    return sum(jnp.where(lane_head == g, t[BLOCK * g:BLOCK * (g + 1)], 0.0) for g in range(Q_PER_KV))


def _stacked_sinks(sink_ref, j):
    block = _iota((STACK, 1), 0) >> 7
    col = jnp.full((STACK, 1), sink_ref[Q_PER_KV * j], _F32)
    for g in range(1, Q_PER_KV):
        col = jnp.where(block == g, sink_ref[Q_PER_KV * j + g], col)
    return col


def _fold_heads(t, j):
    t = t[:, :KV_WIDTH] + t[:, KV_WIDTH:]
    t = t + pltpu.roll(t, HEAD_DIM, 1)
    return jnp.where((_iota((1, KV_WIDTH), 1) >> 6) == j, t, 0.0)


def _attn_scores(q_stacked, kt, blk, sink):
    s = _dot(q_stacked, kt, _NT) * (HEAD_DIM ** -0.5)
    qi = _iota((STACK, 2 * BLOCK), 0) & (BLOCK - 1)
    kj = _iota((STACK, 2 * BLOCK), 1)
    rel = qi + BLOCK - kj
    valid = (rel >= 0) & (rel < BLOCK) & (blk * BLOCK - BLOCK + kj >= 0)
    s = jnp.where(valid, s, MASK_VALUE)
    m = jnp.maximum(jnp.max(s, axis=-1, keepdims=True), sink)
    p = jnp.exp(s - m)
    e_sink = jnp.exp(sink - m)
    den = jnp.sum(p, axis=-1, keepdims=True) + e_sink
    return p / den, e_sink / den


def _attn_specs():
    prev = lambda i: (jnp.maximum(i - 1, 0), 0)
    cur = lambda i: (i, 0)
    kv = [pl.BlockSpec((BLOCK, KV_WIDTH), prev), pl.BlockSpec((BLOCK, KV_WIDTH), cur)]
    return [pl.BlockSpec((BLOCK, ATTN_WIDTH), cur)] + kv + kv


def _attn_fwd(q, k, v, sinks, g_attn):
    L = q.shape[0]

    def body(q_ref, kp_ref, kc_ref, vp_ref, vc_ref, sink_ref, g_ref, o_ref, n_ref):
        blk = pl.program_id(0)
        kwin = jnp.concatenate([kp_ref[...], kc_ref[...]], axis=0)
        vwin = jnp.concatenate([vp_ref[...], vc_ref[...]], axis=0)
        halves = []
        for j in range(N_KV_HEADS):
            spread = _head_spread(j)
            kt = _dot(kwin, spread, _NN).astype(_BF16)
            vt = _dot(vwin, spread, _NN).astype(_BF16)
            qs = _stack_heads(q_ref[:, 256 * j:256 * (j + 1)])
            p, _ = _attn_scores(qs, kt, blk, _stacked_sinks(sink_ref, j))
            halves.append(_unstack_heads(_dot(p, vt, _NN)))
        o = jnp.concatenate(halves, axis=1)
        o_ref[...] = o
        n, _ = _rms_fwd(o, g_ref[...])
        n_ref[...] = n.astype(_BF16)

    cur = lambda i: (i, 0)
    return _call(body, (L // BLOCK,),
                 _attn_specs() + [pl.BlockSpec(memory_space=pltpu.SMEM), _whole((1, ATTN_WIDTH))],
                 [pl.BlockSpec((BLOCK, ATTN_WIDTH), cur)] * 2,
                 [_sds((L, ATTN_WIDTH), _F32), _sds((L, ATTN_WIDTH), _BF16)],
                 "attn_fwd")(q, k, k, v, v, sinks, g_attn)


def _attn_bwd(q, k, v, o, dn, sinks, g_attn):
    L = q.shape[0]

    def body(q_ref, kp_ref, kc_ref, vp_ref, vc_ref, o_ref, dn_ref, sink_ref, g_ref,
             dq_ref, dk_ref, dv_ref, dsink_ref, dg_ref):
        blk = pl.program_id(0)
        first = blk == 0

        @pl.when(first)
        def _():
            dk_ref[...] = jnp.zeros_like(dk_ref)
            dv_ref[...] = jnp.zeros_like(dv_ref)
            dsink_ref[...] = jnp.zeros_like(dsink_ref)

        o = o_ref[...]
        g = g_ref[...]
        _, r = _rms_fwd(o, g)
        do, dg = _rms_bwd(dn_ref[...], o, g, r)
        _accumulate(dg_ref, dg, first)
        kwin = jnp.concatenate([kp_ref[...], kc_ref[...]], axis=0)
        vwin = jnp.concatenate([vp_ref[...], vc_ref[...]], axis=0)
        lane = _iota((1, 128), 1)
        dsink = jnp.zeros((1, 128), _F32)
        dkwin = jnp.zeros((2 * BLOCK, KV_WIDTH), _F32)
        dvwin = jnp.zeros((2 * BLOCK, KV_WIDTH), _F32)
        dq_halves = []
        for j in range(N_KV_HEADS):
            spread = _head_spread(j)
            kt = _dot(kwin, spread, _NN).astype(_BF16)
            vt = _dot(vwin, spread, _NN).astype(_BF16)
            qs = _stack_heads(q_ref[:, 256 * j:256 * (j + 1)])
            dos = _stack_heads(do[:, 256 * j:256 * (j + 1)]).astype(_BF16)
            p, p_sink = _attn_scores(qs, kt, blk, _stacked_sinks(sink_ref, j))
            dp = _dot(dos, vt, _NT)
            delta = jnp.sum(p * dp, axis=-1, keepdims=True)
            ds = (p * (dp - delta) * (HEAD_DIM ** -0.5)).astype(_BF16)
            sink_term = p_sink * delta
            for g in range(Q_PER_KV):
                head_sum = jnp.sum(sink_term[BLOCK * g:BLOCK * (g + 1)], axis=0, keepdims=True)
                dsink = dsink - jnp.where(lane == Q_PER_KV * j + g, head_sum, 0.0)
            dvwin = dvwin + _fold_heads(_dot(p, dos, _TN), j)
            dkwin = dkwin + _fold_heads(_dot(ds, qs, _TN), j)
            dq_halves.append(_unstack_heads(_dot(ds, kt, _NN)))
        dq_ref[...] = jnp.concatenate(dq_halves, axis=1)
        dsink_ref[...] += dsink
        prev = pl.ds(pl.multiple_of(jnp.maximum(blk - 1, 0) * BLOCK, BLOCK), BLOCK)
        cur = pl.ds(pl.multiple_of(blk * BLOCK, BLOCK), BLOCK)
        dk_ref[prev, :] += dkwin[:BLOCK]
        dk_ref[cur, :] += dkwin[BLOCK:]
        dv_ref[prev, :] += dvwin[:BLOCK]
        dv_ref[cur, :] += dvwin[BLOCK:]

    cur = lambda i: (i, 0)
    blk_q = pl.BlockSpec((BLOCK, ATTN_WIDTH), cur)
    return _call(body, (L // BLOCK,),
                 _attn_specs() + [blk_q, blk_q, pl.BlockSpec(memory_space=pltpu.SMEM), _whole((1, ATTN_WIDTH))],
                 [blk_q, _whole((L, KV_WIDTH)), _whole((L, KV_WIDTH)), _whole((1, 128)), _whole((1, ATTN_WIDTH))],
                 [_sds((L, ATTN_WIDTH), _F32), _sds((L, KV_WIDTH), _F32), _sds((L, KV_WIDTH), _F32),
                  _sds((1, 128), _F32), _sds((1, ATTN_WIDTH), _F32)],
                 "attn_bwd")(q, k, k, v, v, o, dn, sinks, g_attn)


def _out_proj(n_ssm, n_attn, x, w_out, g_post_mix, g_pre_ffn):
    L = x.shape[0]
    tm = _tile(L)

    def body(ns_ref, na_ref, x_ref, w_ref, g1_ref, g2_ref, merged_ref, mo_ref, h1_ref, hn2_ref):
        merged = jnp.concatenate([ns_ref[...], na_ref[...]], axis=1)
        merged_ref[...] = merged
        mo = _dot(merged, w_ref[...], _NN)
        mo_ref[...] = mo
        n, _ = _rms_fwd(mo, g1_ref[...])
        h1 = x_ref[...] + n
        h1_ref[...] = h1
        hn2, _ = _rms_fwd(h1, g2_ref[...])
        hn2_ref[...] = hn2.astype(_BF16)

    row = _whole((1, D_MODEL))
    return _call(body, (L // tm,),
                 [_rows(tm, SSM_WIDTH), _rows(tm, ATTN_WIDTH), _rows(tm, D_MODEL), _whole((D_MODEL, D_MODEL)), row, row],
                 [_rows(tm, D_MODEL)] * 4,
                 [_sds((L, D_MODEL), _BF16), _sds((L, D_MODEL), _F32), _sds((L, D_MODEL), _F32), _sds((L, D_MODEL), _BF16)],
                 "out_proj")(n_ssm, n_attn, x, w_out, g_post_mix, g_pre_ffn)


def _ffn(hn2, h1, target, w_gate_up, w_down, g_pre_ffn, g_post_ffn):
    L = h1.shape[0]
    tm = _tile(L)
    half = D_FF // 2

    def body(hn2_ref, h1_ref, tgt_ref, wgu_hbm, wd_hbm, g2_ref, g3_ref,
             act_ref, dgu_ref, dff_ref, dh1_ref, loss_ref, dg3_ref, dg2_ref,
             wgu, wd, gu, sem):
        first = pl.program_id(0) == 0

        @pl.when(first)
        def _():
            c1 = pltpu.make_async_copy(wgu_hbm, wgu, sem.at[0])
            c2 = pltpu.make_async_copy(wd_hbm, wd, sem.at[1])
            c1.start()
            c2.start()
            c1.wait()
            c2.wait()

        hn2 = hn2_ref[...]
        ff = jnp.zeros((tm, D_MODEL), _F32)
        for c in range(2):
            gate = _dot(hn2, wgu[half * c:half * (c + 1), :], _NT)
            up = _dot(hn2, wgu[D_FF + half * c:D_FF + half * (c + 1), :], _NT)
            gu[:, half * c:half * (c + 1)] = gate
            gu[:, D_FF + half * c:D_FF + half * (c + 1)] = up
            act = (gate * jax.nn.sigmoid(gate) * up).astype(_BF16)
            act_ref[:, half * c:half * (c + 1)] = act
            ff = ff + _dot(act, wd[half * c:half * (c + 1), :], _NN)
        g3 = g3_ref[...]
        n, r = _rms_fwd(ff, g3)
        h1 = h1_ref[...]
        err = h1 + n - tgt_ref[...]
        loss = 0.5 * jnp.sum(jnp.mean(err * err, axis=-1, keepdims=True), axis=0, keepdims=True)
        _accumulate(loss_ref, jnp.broadcast_to(loss, (1, 128)), first)
        dh2 = err * (1.0 / D_MODEL)
        dff, dg3 = _rms_bwd(dh2, ff, g3, r)
        _accumulate(dg3_ref, dg3, first)
        dffb = dff.astype(_BF16)
        dff_ref[...] = dffb
        dhn2 = jnp.zeros((tm, D_MODEL), _F32)
        for c in range(2):
            dact = _dot(dffb, wd[half * c:half * (c + 1), :], _NT)
            gate = gu[:, half * c:half * (c + 1)]
            up = gu[:, D_FF + half * c:D_FF + half * (c + 1)]
            sig = jax.nn.sigmoid(gate)
            silu = gate * sig
            dgate = (dact * up * (sig + silu * (1.0 - sig))).astype(_BF16)
            dup = (dact * silu).astype(_BF16)
            dgu_ref[:, half * c:half * (c + 1)] = dgate
            dgu_ref[:, D_FF + half * c:D_FF + half * (c + 1)] = dup
            dhn2 = dhn2 + _dot(dgate, wgu[half * c:half * (c + 1), :], _NN)
            dhn2 = dhn2 + _dot(dup, wgu[D_FF + half * c:D_FF + half * (c + 1), :], _NN)
        g2 = g2_ref[...]
        _, r2 = _rms_fwd(h1, g2)
        dh1, dg2 = _rms_bwd(dhn2, h1, g2, r2)
        _accumulate(dg2_ref, dg2, first)
        dh1_ref[...] = dh2 + dh1

    row = _whole((1, D_MODEL))
    anyspace = pl.BlockSpec(memory_space=pl.ANY)
    return _call(body, (L // tm,),
                 [_rows(tm, D_MODEL), _rows(tm, D_MODEL), _rows(tm, D_MODEL), anyspace, anyspace, row, row],
                 [_rows(tm, D_FF), _rows(tm, 2 * D_FF), _rows(tm, D_MODEL), _rows(tm, D_MODEL),
                  _whole((1, 128)), row, row],
                 [_sds((L, D_FF), _BF16), _sds((L, 2 * D_FF), _BF16), _sds((L, D_MODEL), _BF16),
                  _sds((L, D_MODEL), _F32), _sds((1, 128), _F32), _sds((1, D_MODEL), _F32), _sds((1, D_MODEL), _F32)],
                 "ffn",
                 scratch=[pltpu.VMEM((2 * D_FF, D_MODEL), _BF16), pltpu.VMEM((D_FF, D_MODEL), _BF16),
                          pltpu.VMEM((tm, 2 * D_FF), _F32), pltpu.SemaphoreType.DMA((2,))],
                 )(hn2, h1, target, w_gate_up, w_down, g_pre_ffn, g_post_ffn)


def _out_proj_bwd(dh1, mo, w_out, g_post_mix, tokens=()):
    L = dh1.shape[0]
    tm = _tile(L)

    def body(dh1_ref, mo_ref, w_ref, g_ref, dmo_ref, dns_ref, dna_ref, dg_ref):
        first = pl.program_id(0) == 0
        mo = mo_ref[...]
        g = g_ref[...]
        _, r = _rms_fwd(mo, g)
        dmo, dg = _rms_bwd(dh1_ref[...], mo, g, r)
        _accumulate(dg_ref, dg, first)
        dmob = dmo.astype(_BF16)
        dmo_ref[...] = dmob
        dmerged = _dot(dmob, w_ref[...], _NT)
        dns_ref[...] = dmerged[:, :SSM_WIDTH]
        dna_ref[...] = dmerged[:, SSM_WIDTH:]

    row = _whole((1, D_MODEL))
    return _call(body, (L // tm,),
                 [_rows(tm, D_MODEL), _rows(tm, D_MODEL), _whole((D_MODEL, D_MODEL)), row],
                 [_rows(tm, D_MODEL), _rows(tm, SSM_WIDTH), _rows(tm, ATTN_WIDTH), row],
                 [_sds((L, D_MODEL), _BF16), _sds((L, SSM_WIDTH), _F32), _sds((L, ATTN_WIDTH), _F32),
                  _sds((1, D_MODEL), _F32)],
                 "out_proj_bwd", tokens=tokens)(dh1, mo, w_out, g_post_mix)


def _in_proj_bwd(du, dq, dk, dv, cos_t, sin_t, x, dh1, g_pre_mix, w_in, tokens=()):
    L = x.shape[0]
    tm = _tile(L)

    def body(du_ref, dq_ref, dk_ref, dv_ref, cos_ref, sin_ref, x_ref, dh1_ref, g_ref, w_ref,
             dproj_ref, dx_ref, dg_ref):
        first = pl.program_id(0) == 0
        cos_v, sin_v = cos_ref[...], sin_ref[...]
        dproj = jnp.concatenate([du_ref[...], _rope_transpose(dq_ref[...], cos_v, sin_v),
                                 _rope_transpose(dk_ref[...], cos_v, sin_v), dv_ref[...]], axis=1).astype(_BF16)
        dproj_ref[...] = dproj
        dhn = _dot(dproj, w_ref[...], _NN)
        x = x_ref[...]
        g = g_ref[...]
        _, r = _rms_fwd(x, g)
        dx, dg = _rms_bwd(dhn, x, g, r)
        _accumulate(dg_ref, dg, first)
        dx_ref[...] = dh1_ref[...] + dx

    row = _whole((1, D_MODEL))
    return _call(body, (L // tm,),
                 [_rows(tm, SSM_WIDTH), _rows(tm, ATTN_WIDTH), _rows(tm, KV_WIDTH), _rows(tm, KV_WIDTH),
                  _rows(tm, KV_WIDTH), _rows(tm, KV_WIDTH), _rows(tm, D_MODEL), _rows(tm, D_MODEL), row,
                  _whole((IN_WIDTH, D_MODEL))],
                 [_rows(tm, IN_WIDTH), _rows(tm, D_MODEL), row],
                 [_sds((L, IN_WIDTH), _BF16), _sds((L, D_MODEL), _F32), _sds((1, D_MODEL), _F32)],
                 "in_proj_bwd", tokens=tokens)(du, dq, dk, dv, cos_t, sin_t, x, dh1, g_pre_mix, w_in)


def _matmul_tn(a, b, out_dtype, name, scale=1.0):
    K, M = a.shape
    N = b.shape[1]
    tm = next(t for t in (512, 256, 128) if M % t == 0)
    tn = next(t for t in (512, 256, 128) if N % t == 0)

    def body(a_ref, b_ref, o_ref):
        acc = _dot(a_ref[...], b_ref[...], _TN)
        o_ref[...] = (acc if scale == 1.0 else acc * scale).astype(out_dtype)

    params = pltpu.CompilerParams(dimension_semantics=("arbitrary", "arbitrary"), vmem_limit_bytes=VMEM_LIMIT)
    return pl.pallas_call(body, grid=(M // tm, N // tn),
                          in_specs=[pl.BlockSpec((K, tm), lambda i, j: (0, i)),
                                    pl.BlockSpec((K, tn), lambda i, j: (0, j))],
                          out_specs=pl.BlockSpec((tm, tn), lambda i, j: (i, j)),
                          out_shape=_sds((M, N), out_dtype), compiler_params=params, name=name)(a, b)


def _to_chunked(a):
    L, n = a.shape
    return a.reshape(SCAN_CHUNKS, L // SCAN_CHUNKS, n).transpose(1, 0, 2).reshape(L, n)


def _from_chunked(a):
    L, n = a.shape
    return a.reshape(L // SCAN_CHUNKS, SCAN_CHUNKS, n).transpose(1, 0, 2).reshape(L, n)


def _local_step(x, pos, target, p, fetch, publish):
    L = x.shape[0]
    T = L // SCAN_CHUNKS
    cos_t, sin_t = _rope_tables(pos.reshape(L, 1))
    w_in, = fetch(("w_in",), None)
    hn, u, q, k, v = _in_proj(x, p["g_pre_mix"], w_in, cos_t, sin_t)

    ssm = {n: _to_2d(n, p[n]) for n in ("ssm_lambda_re", "ssm_lambda_im", "ssm_log_dt", "ssm_b_re", "ssm_b_im",
                                        "ssm_c_re", "ssm_c_im")}
    d_row = p["ssm_d"].reshape(1, SSM_WIDTH)
    a_re, a_im, bt_re, bt_im, ct_re, ct_im = _ssm_prep(
        ssm["ssm_lambda_re"], ssm["ssm_lambda_im"], ssm["ssm_log_dt"], ssm["ssm_b_re"], ssm["ssm_b_im"],
        ssm["ssm_c_re"], ssm["ssm_c_im"])

    u_c = _to_chunked(u)
    bu_re, bu_im = _ssm_bu(u_c, bt_re, bt_im)
    x_re, x_im = _scan_fwd(bu_re.reshape(T, SCAN_CHUNKS, N_STATE), bu_im.reshape(T, SCAN_CHUNKS, N_STATE), a_re, a_im)
    w_glu, = fetch(("w_glu",), x_re)
    y, z, n_ssm_c = _ssm_out(x_re.reshape(L, N_STATE), x_im.reshape(L, N_STATE), u_c, ct_re, ct_im, d_row,
                             w_glu, p["b_glu"], p["g_ssm_out"])
    n_ssm = _from_chunked(n_ssm_c)

    sinks = p["attn_sinks"].reshape(N_Q_HEADS)
    o, n_attn = _attn_fwd(q, k, v, sinks, p["g_attn_out"])
    w_out, = fetch(("w_out",), n_attn)
    merged, mo, h1, hn2 = _out_proj(n_ssm, n_attn, x, w_out, p["g_post_mix"], p["g_pre_ffn"])
    w_gate_up, w_down = fetch(("w_gate_up", "w_down"), hn2)
    act, dgu, dff, dh1, loss, dg_post_ffn, dg_pre_ffn = _ffn(
        hn2, h1, target, w_gate_up, w_down, p["g_pre_ffn"], p["g_post_ffn"])
    grads = {"g_post_ffn": dg_post_ffn, "g_pre_ffn": dg_pre_ffn}
    tokens = publish({"w_down": _matmul_tn(act, dff, _BF16, "grad_w_down"),
                      "w_gate_up": _matmul_tn(dgu, hn2, _BF16, "grad_w_gate_up")})

    dmo, dn_ssm, dn_attn, grads["g_post_mix"] = _out_proj_bwd(dh1, mo, w_out, p["g_post_mix"], tokens)
    grad_w_out = _matmul_tn(merged, dmo, _BF16, "grad_w_out")

    dq, dk, dv, dsink, grads["g_attn_out"] = _attn_bwd(q, k, v, o, dn_attn, sinks, p["g_attn_out"])
    grads["attn_sinks"] = dsink

    gy, dz, dy, dud, dx_re, dx_im, grads["g_ssm_out"], grads["b_glu"], dd = _ssm_out_bwd(
        _to_chunked(dn_ssm), y, z, u_c, ct_re, ct_im, d_row, w_glu, p["g_ssm_out"])
    grads["ssm_d"] = dd.reshape(SSM_GROUPS, SSM_GROUP)
    grads.update(w_out=grad_w_out, w_glu=_matmul_tn(dz, gy, _BF16, "grad_w_glu"))
    lam_re, lam_im, da_re, da_im = _scan_bwd(dx_re.reshape(T, SCAN_CHUNKS, N_STATE), dx_im.reshape(T, SCAN_CHUNKS, N_STATE),
                                             x_re, x_im, a_re, a_im, [grads["w_out"], grads["w_glu"]])
    lam_re = lam_re.reshape(L, N_STATE)
    lam_im = lam_im.reshape(L, N_STATE)
    dct_re, dct_im, dbt_re, dbt_im = _ssm_weight_grads(
        dy, x_re.reshape(L, N_STATE), x_im.reshape(L, N_STATE), lam_re, lam_im, u_c)
    g_lr, g_li, g_dt, g_br, g_bi, g_cr, g_ci = _ssm_param_bwd(
        da_re, da_im, dbt_re, dbt_im, dct_re, dct_im,
        ssm["ssm_lambda_re"], ssm["ssm_lambda_im"], ssm["ssm_log_dt"], ssm["ssm_b_re"], ssm["ssm_b_im"])
    grads.update(ssm_lambda_re=g_lr, ssm_lambda_im=g_li, ssm_log_dt=g_dt, ssm_b_re=g_br, ssm_b_im=g_bi,
                 ssm_c_re=g_cr, ssm_c_im=g_ci, loss=loss)
    publish(grads)

    du = _from_chunked(_ssm_du(lam_re, lam_im, bt_re, bt_im, dud))
    dproj, grad_x, g_pre_mix = _in_proj_bwd(du, dq, dk, dv, cos_t, sin_t, x, dh1, p["g_pre_mix"], w_in, [g_dt])
    publish({"g_pre_mix": g_pre_mix, "w_in": _matmul_tn(dproj, hn, _BF16, "grad_w_in")})
    return grad_x


_MESH = pl.DeviceIdType.MESH
_PEERS = N_DEV - 1


def _mesh_pos():
    return lax.axis_index("x"), lax.axis_index("y"), lax.axis_index("c")


def _dev_index(px, py, pc):
    return 4 * px + 2 * py + pc


def _all_gather(shards, out_dtype, name):
    n = len(shards)

    def body(*refs):
        ins, outs, stages = refs[:n], refs[n:2 * n], refs[2 * n:3 * n]
        send_sems, recv_sems, local_sems = refs[3 * n:]
        x, y, c = _mesh_pos()
        me, sibling = (x, y, c), (x, y, 1 - c)
        chips = [(1 - x, y), (x, 1 - y), (1 - x, 1 - y)]

        def copy(w, k, block, to, src=None):
            slot = outs[w].at[_dev_index(*block)]
            return pltpu.make_async_remote_copy(
                src_ref=slot if src is None else src, dst_ref=slot,
                send_sem=send_sems.at[_PEERS * w + k], recv_sem=recv_sems.at[_PEERS * w + k],
                device_id=to, device_id_type=_MESH)

        for w in range(n):
            stages[w][...] = ins[w][...].astype(out_dtype)
        mine, first, passed = [], [], []
        for w in range(n):
            cp = pltpu.make_async_copy(stages[w], outs[w].at[_dev_index(*me)], local_sems.at[w])
            cp.start()
            mine.append(cp)
            sends = [copy(w, 0, me, sibling, src=stages[w])]
            sends += [copy(w, 1 + j, me, (*chip, c), src=stages[w]) for j, chip in enumerate(chips)]
            for cp in sends:
                cp.start()
            first += sends
        for w in range(n):
            for j, chip in enumerate(chips):
                copy(w, 1 + j, (*chip, c), me).wait_recv()
                cp = copy(w, 4 + j, (*chip, c), sibling)
                cp.start()
                passed.append(cp)
        for w in range(n):
            copy(w, 0, sibling, me).wait_recv()
            for j, chip in enumerate(chips):
                copy(w, 4 + j, (*chip, 1 - c), me).wait_recv()
        for cp in first + passed:
            cp.wait_send()
        for cp in mine:
            cp.wait()

    return pl.pallas_call(
        body, name=name,
        out_shape=[_sds((N_DEV,) + s.shape, out_dtype) for s in shards],
        in_specs=[pl.BlockSpec(memory_space=pltpu.VMEM)] * n,
        out_specs=[pl.BlockSpec(memory_space=pl.ANY)] * n,
        scratch_shapes=[pltpu.VMEM(s.shape, out_dtype) for s in shards]
        + [pltpu.SemaphoreType.DMA((_PEERS * n,)), pltpu.SemaphoreType.DMA((_PEERS * n,)),
           pltpu.SemaphoreType.DMA((n,))],
        compiler_params=pltpu.CompilerParams(vmem_limit_bytes=VMEM_LIMIT),
    )(*shards)


_HBM_SPEC = pl.BlockSpec(memory_space=pltpu.HBM)
_SEM_SPEC = pl.BlockSpec(memory_space=pltpu.SEMAPHORE)
_DATAFLOW = pltpu.SideEffectType.DATAFLOW_SIDE_EFFECTING


def _peer(x, y, c, r):
    return (x ^ ((r >> 2) & 1), y ^ ((r >> 1) & 1), c ^ (r & 1))


def _hbm(a):
    return pltpu.with_memory_space_constraint(a, pltpu.HBM)


def _send_start(sources, blocked, name):
    n = len(sources)
    lands = [lax.empty((N_DEV,) + (s.shape[1:] if blocked else s.shape), s.dtype) for s in sources]

    def body(*refs):
        srcs, zones = refs[:n], refs[n:2 * n]
        send_sems, recv_sems = refs[2 * n:3 * n], refs[3 * n:4 * n]
        token, local_sems = refs[6 * n], refs[6 * n + 1]
        x, y, c = _mesh_pos()
        me = _dev_index(x, y, c)
        local = []
        for w in range(n):
            cp = pltpu.make_async_copy(srcs[w].at[me] if blocked else srcs[w], zones[w].at[me], local_sems.at[w])
            cp.start()
            local.append(cp)
            for r in range(1, N_DEV):
                peer = _peer(x, y, c, r)
                pltpu.make_async_remote_copy(
                    src_ref=srcs[w].at[_dev_index(*peer)] if blocked else srcs[w], dst_ref=zones[w].at[me],
                    send_sem=send_sems[w].at[r - 1], recv_sem=recv_sems[w].at[r - 1],
                    device_id=peer, device_id_type=_MESH).start()
        for cp in local:
            cp.wait()
        token[...] = jnp.zeros_like(token)

    sems = [pltpu.SemaphoreType.DMA((_PEERS,))] * (2 * n)
    out = pl.pallas_call(
        body, name=name,
        out_shape=sems + [pltpu.HBM(a.shape, a.dtype) for a in list(sources) + lands] + [_sds((8, 128), _F32)],
        in_specs=[_HBM_SPEC] * (2 * n),
        out_specs=[_SEM_SPEC] * (2 * n) + [_HBM_SPEC] * (2 * n) + [pl.BlockSpec(memory_space=pltpu.VMEM)],
        input_output_aliases={i: 2 * n + i for i in range(2 * n)},
        scratch_shapes=[pltpu.SemaphoreType.DMA((n,))],
        compiler_params=pltpu.CompilerParams(has_side_effects=_DATAFLOW),
    )(*[_hbm(a) for a in sources], *[_hbm(a) for a in lands])
    return out[:n], out[n:2 * n], out[2 * n:3 * n], out[3 * n:4 * n], out[4 * n]


def _send_wait(send_sems, recv_sems, sources, lands, after, blocked, name):
    n = len(sources)

    def body(*refs):
        srcs, zones = refs[:n], refs[n:2 * n]
        sends, recvs = refs[2 * n:3 * n], refs[3 * n:4 * n]
        x, y, c = _mesh_pos()
        for w in range(n):
            for r in range(1, N_DEV):
                peer = _peer(x, y, c, r)
                idx = _dev_index(*peer)
                cp = pltpu.make_async_remote_copy(
                    src_ref=srcs[w].at[idx] if blocked else srcs[w], dst_ref=zones[w].at[idx],
                    send_sem=sends[w].at[r - 1], recv_sem=recvs[w].at[r - 1],
                    device_id=peer, device_id_type=_MESH)
                cp.wait_send()
                cp.wait_recv()

    out = pl.pallas_call(
        body, name=name,
        out_shape=[pltpu.HBM(a.shape, a.dtype) for a in list(sources) + list(lands)],
        in_specs=[_HBM_SPEC] * (2 * n) + [_SEM_SPEC] * (2 * n) + [pl.BlockSpec(memory_space=pl.ANY)],
        out_specs=[_HBM_SPEC] * (2 * n),
        input_output_aliases={i: i for i in range(2 * n)},
        compiler_params=pltpu.CompilerParams(has_side_effects=_DATAFLOW),
    )(*sources, *lands, *send_sems, *recv_sems, after)
    return out[n:]


def _sequencer_exchange(sources, blocked, name, collective_id):
    n = len(sources)
    flags = blocked

    def body(*refs):
        srcs, zones = refs[:n], refs[n:2 * n]
        send_sems, recv_sems, local_sems = refs[2 * n:]
        x, y, c = _mesh_pos()
        me = _dev_index(x, y, c)
        barrier = pltpu.get_barrier_semaphore()
        for r in range(1, N_DEV):
            pl.semaphore_signal(barrier, inc=1, device_id=_peer(x, y, c, r), device_id_type=_MESH)
        pl.semaphore_wait(barrier, _PEERS)
        local, sends, recvs = [], [], []
        for w in range(n):
            cp = pltpu.make_async_copy(srcs[w].at[me] if flags[w] else srcs[w], zones[w].at[me], local_sems.at[w])
            cp.start()
            local.append(cp)
            for r in range(1, N_DEV):
                peer = _peer(x, y, c, r)
                idx = _dev_index(*peer)
                k = _PEERS * w + r - 1
                src = srcs[w].at[idx] if flags[w] else srcs[w]
                send = pltpu.make_async_remote_copy(
                    src_ref=src, dst_ref=zones[w].at[me], send_sem=send_sems.at[k], recv_sem=recv_sems.at[k],
                    device_id=peer, device_id_type=_MESH)
                send.start()
                sends.append(send)
                recvs.append(pltpu.make_async_remote_copy(
                    src_ref=src, dst_ref=zones[w].at[idx], send_sem=send_sems.at[k], recv_sem=recv_sems.at[k],
                    device_id=peer, device_id_type=_MESH))
        for cp in recvs:
            cp.wait_recv()
        for cp in sends:
            cp.wait_send()
        for cp in local:
            cp.wait()

    return pl.kernel(
        body, name=name,
        out_type=[_sds((N_DEV,) + (s.shape[1:] if f else s.shape), s.dtype) for s, f in zip(sources, flags)],
        mesh=plsc.ScalarSubcoreMesh(axis_name="sequencer", num_cores=1),
        scratch_types=[pltpu.SemaphoreType.DMA((_PEERS * n,)), pltpu.SemaphoreType.DMA((_PEERS * n,)),
                       pltpu.SemaphoreType.DMA((n,))],
        compiler_params=pltpu.CompilerParams(collective_id=collective_id),
    )(*sources)


def _sequencer_gather(shards, name, collective_id):
    n = len(shards)
    fan = 4

    def body(*refs):
        srcs, zones = refs[:n], refs[n:2 * n]
        send_sems, recv_sems, local_sems = refs[2 * n:]
        x, y, c = _mesh_pos()
        me, sibling = (x, y, c), (x, y, 1 - c)
        chips = [(1 - x, y), (x, 1 - y), (1 - x, 1 - y)]
        barrier = pltpu.get_barrier_semaphore()
        for peer in [sibling] + [(*chip, c) for chip in chips]:
            pl.semaphore_signal(barrier, inc=1, device_id=peer, device_id_type=_MESH)
        pl.semaphore_wait(barrier, fan)

        def copy(w, k, block, to, src=None):
            slot = zones[w].at[_dev_index(*block)]
            return pltpu.make_async_remote_copy(
                src_ref=slot if src is None else src, dst_ref=slot,
                send_sem=send_sems.at[_PEERS * w + k], recv_sem=recv_sems.at[_PEERS * w + k],
                device_id=to, device_id_type=_MESH)

        mine, first, passed = [], [], []
        for w in range(n):
            cp = pltpu.make_async_copy(srcs[w], zones[w].at[_dev_index(*me)], local_sems.at[w])
            cp.start()
            mine.append(cp)
            sends = [copy(w, 0, me, sibling, src=srcs[w])]
            sends += [copy(w, 1 + j, me, (*chip, c), src=srcs[w]) for j, chip in enumerate(chips)]
            for cp in sends:
                cp.start()
            first += sends
        for w in range(n):
            for j, chip in enumerate(chips):
                copy(w, 1 + j, (*chip, c), me).wait_recv()
                cp = copy(w, fan + j, (*chip, c), sibling)
                cp.start()
                passed.append(cp)
        for w in range(n):
            copy(w, 0, sibling, me).wait_recv()
            for j, chip in enumerate(chips):
                copy(w, fan + j, (*chip, 1 - c), me).wait_recv()
        for cp in first + passed:
            cp.wait_send()
        for cp in mine:
            cp.wait()

    return pl.kernel(
        body, name=name, out_type=[_sds((N_DEV,) + s.shape, s.dtype) for s in shards],
        mesh=plsc.ScalarSubcoreMesh(axis_name="sequencer", num_cores=1),
        scratch_types=[pltpu.SemaphoreType.DMA((_PEERS * n,)), pltpu.SemaphoreType.DMA((_PEERS * n,)),
                       pltpu.SemaphoreType.DMA((n,))],
        compiler_params=pltpu.CompilerParams(collective_id=collective_id),
    )(*shards)


def _row_tile(rows):
    return next(t for t in range(min(rows, 256), 0, -16) if rows % t == 0)


def _sum_parts(parts, name, tokens=()):
    _, rows, cols = parts.shape
    tr = _row_tile(rows)

    def body(p_ref, g_ref):
        g = p_ref[0].astype(_F32)
        for s in range(1, N_DEV):
            g = g + p_ref[s].astype(_F32)
        g_ref[...] = g

    return _call(body, (rows // tr,), [pl.BlockSpec((N_DEV, tr, cols), lambda i: (0, i, 0))],
                 _rows(tr, cols), _sds((rows, cols), _F32), name, tokens=tokens)(parts)


def _adam_update(g, w, m, v):
    new_m = ADAM_B1 * m + (1.0 - ADAM_B1) * g
    new_v = ADAM_B2 * v + (1.0 - ADAM_B2) * (g * g)
    m_hat = new_m / (1.0 - ADAM_B1 ** ADAM_STEP)
    v_hat = new_v / (1.0 - ADAM_B2 ** ADAM_STEP)
    return -ADAM_LR * (m_hat / (jnp.sqrt(v_hat) + ADAM_EPS) + ADAM_WD * w), new_m, new_v


def _adamw_small(parts, items, sums, name, tokens=()):
    n_p, n_i = len(parts), len(items)

    def body(*refs):
        p_refs, state, outs = refs[:n_p], refs[n_p:n_p + 3 * n_i], refs[n_p + 3 * n_i:]

        def total(part, rows, cols):
            g = p_refs[part][0, rows, cols]
            for s in range(1, N_DEV):
                g = g + p_refs[part][s, rows, cols]
            return g

        for i, (part, rows, cols, _, _, _) in enumerate(items):
            g = total(part, rows, cols)
            w_ref, m_ref, v_ref = state[3 * i:3 * i + 3]
            delta, new_m, new_v = _adam_update(g, w_ref[...], m_ref[...], v_ref[...])
            outs[4 * i][...] = g
            outs[4 * i + 1][...] = delta
            outs[4 * i + 2][...] = new_m
            outs[4 * i + 3][...] = new_v
        for j, (part, rows, cols) in enumerate(sums):
            outs[4 * n_i + j][...] = total(part, rows, cols)

    ins = list(parts) + [a for item in items for a in item[3:]]
    out_shapes = [item[3].shape for item in items for _ in range(4)]
    out_shapes += [(rows.stop - rows.start, cols.stop - cols.start) for _, rows, cols in sums]
    out = _call(body, (1,), [_whole(a.shape) for a in ins], [_whole(s) for s in out_shapes],
                [_sds(s, _F32) for s in out_shapes], name, tokens=tokens)(*ins)
    return [out[4 * i:4 * i + 4] for i in range(n_i)], out[4 * n_i:]


def _adamw(parts, w, m, v, name, tokens=()):
    rows, cols = w.shape
    tr = _row_tile(rows)
    n_parts = parts.shape[0]

    def body(p_ref, w_ref, m_ref, v_ref, g_ref, d_ref, nm_ref, nv_ref):
        g = p_ref[0].astype(_F32)
        for s in range(1, n_parts):
            g = g + p_ref[s].astype(_F32)
        new_m = ADAM_B1 * m_ref[...] + (1.0 - ADAM_B1) * g
        new_v = ADAM_B2 * v_ref[...] + (1.0 - ADAM_B2) * (g * g)
        m_hat = new_m / (1.0 - ADAM_B1 ** ADAM_STEP)
        v_hat = new_v / (1.0 - ADAM_B2 ** ADAM_STEP)
        g_ref[...] = g
        d_ref[...] = -ADAM_LR * (m_hat / (jnp.sqrt(v_hat) + ADAM_EPS) + ADAM_WD * w_ref[...])
        nm_ref[...] = new_m
        nv_ref[...] = new_v

    blk = _rows(tr, cols)
    return _call(body, (rows // tr,),
                 [pl.BlockSpec((n_parts, tr, cols), lambda i: (0, i, 0)), blk, blk, blk],
                 [blk] * 4, [_sds((rows, cols), _F32)] * 4, name, tokens=tokens)(parts, w, m, v)


_SMALL = ("g_pre_mix", "ssm_lambda_re", "ssm_lambda_im", "ssm_log_dt", "ssm_b_re", "ssm_b_im",
          "ssm_c_re", "ssm_c_im", "ssm_d", "b_glu", "attn_sinks", "g_ssm_out", "g_attn_out",
          "g_post_mix", "g_pre_ffn", "g_post_ffn")
_BIG = ("w_in", "w_glu", "w_out", "w_gate_up", "w_down")
_WEIGHTS = ("g_pre_mix", "w_in", "ssm_lambda_re", "ssm_lambda_im", "ssm_log_dt", "ssm_b_re", "ssm_b_im",
            "ssm_c_re", "ssm_c_im", "ssm_d", "w_glu", "b_glu", "attn_sinks", "g_ssm_out", "g_attn_out",
            "w_out", "g_post_mix", "g_pre_ffn", "w_gate_up", "w_down", "g_post_ffn")
_LANES = 128


_SHAPE_2D = {
    "g_pre_mix": (1, D_MODEL), "ssm_lambda_re": (SSM_GROUPS, SSM_STATE), "ssm_lambda_im": (SSM_GROUPS, SSM_STATE),
    "ssm_log_dt": (1, SSM_GROUPS), "ssm_b_re": (SSM_WIDTH, SSM_STATE), "ssm_b_im": (SSM_WIDTH, SSM_STATE),
    "ssm_c_re": (SSM_WIDTH, SSM_STATE), "ssm_c_im": (SSM_WIDTH, SSM_STATE), "ssm_d": (SSM_GROUPS, SSM_GROUP),
    "b_glu": (1, 2 * SSM_WIDTH), "attn_sinks": (1, N_Q_HEADS), "g_ssm_out": (1, SSM_WIDTH),
    "g_attn_out": (1, ATTN_WIDTH), "g_post_mix": (1, D_MODEL), "g_pre_ffn": (1, D_MODEL), "g_post_ffn": (1, D_MODEL)}
_ROW_WIDTH = {"g_pre_mix": D_MODEL, "b_glu": 2 * SSM_WIDTH, "attn_sinks": _LANES, "g_ssm_out": SSM_WIDTH,
              "g_attn_out": ATTN_WIDTH, "g_post_mix": D_MODEL, "g_pre_ffn": D_MODEL, "g_post_ffn": D_MODEL,
              "loss": _LANES}
_DENSE = ()
_PER_GROUP_TRANSPOSED = ("ssm_b_re", "ssm_b_im")


def _to_2d(name, a):
    if name in _PER_GROUP_TRANSPOSED:
        a = a.reshape(SSM_GROUPS, SSM_STATE, SSM_GROUP).transpose(0, 2, 1)
    return a.reshape(_SHAPE_2D[name])


def _from_2d(name, a, shape):
    if name in _PER_GROUP_TRANSPOSED:
        a = a.reshape(SSM_GROUPS, SSM_GROUP, SSM_STATE).transpose(0, 2, 1)
    return a.reshape(shape)


def _row_slots(names):
    slots, row, col = {}, 0, 0
    for n in names:
        width = _ROW_WIDTH[n]
        if col + width > D_MODEL:
            row, col = row + 1, 0
        slots[n] = (row, col, width)
        col += width
    return slots


def _stack_rows(named, slots):
    n_rows = -(-(max(r for r, _, _ in slots.values()) + 1) // 8) * 8
    lines = []
    for r in range(n_rows):
        pieces = [named[n] for n, (row, _, _) in slots.items() if row == r]
        used = sum(p.shape[1] for p in pieces)
        if used < D_MODEL:
            pieces.append(jnp.zeros((1, D_MODEL - used), _F32))
        lines.append(jnp.concatenate(pieces, axis=1) if len(pieces) > 1 else pieces[0])
    return jnp.concatenate(lines, axis=0)


def kernel(x, positions, g_pre_mix, w_in, ssm_lambda_re, ssm_lambda_im, ssm_log_dt, ssm_b_re, ssm_b_im, ssm_c_re, ssm_c_im, ssm_d, w_glu, b_glu, attn_sinks, g_ssm_out, g_attn_out, w_out, g_post_mix, g_pre_ffn, w_gate_up, w_down, g_post_ffn, loss_target, m_g_pre_mix, m_w_in, m_ssm_lambda_re, m_ssm_lambda_im, m_ssm_log_dt, m_ssm_b_re, m_ssm_b_im, m_ssm_c_re, m_ssm_c_im, m_ssm_d, m_w_glu, m_b_glu, m_attn_sinks, m_g_ssm_out, m_g_attn_out, m_w_out, m_g_post_mix, m_g_pre_ffn, m_w_gate_up, m_w_down, m_g_post_ffn, v_g_pre_mix, v_w_in, v_ssm_lambda_re, v_ssm_lambda_im, v_ssm_log_dt, v_ssm_b_re, v_ssm_b_im, v_ssm_c_re, v_ssm_c_im, v_ssm_d, v_w_glu, v_b_glu, v_attn_sinks, v_g_ssm_out, v_g_attn_out, v_w_out, v_g_post_mix, v_g_pre_ffn, v_w_gate_up, v_w_down, v_g_post_ffn):
    w = dict(g_pre_mix=g_pre_mix, w_in=w_in, ssm_lambda_re=ssm_lambda_re, ssm_lambda_im=ssm_lambda_im,
             ssm_log_dt=ssm_log_dt, ssm_b_re=ssm_b_re, ssm_b_im=ssm_b_im, ssm_c_re=ssm_c_re, ssm_c_im=ssm_c_im,
             ssm_d=ssm_d, w_glu=w_glu, b_glu=b_glu, attn_sinks=attn_sinks, g_ssm_out=g_ssm_out,
             g_attn_out=g_attn_out, w_out=w_out, g_post_mix=g_post_mix, g_pre_ffn=g_pre_ffn,
             w_gate_up=w_gate_up, w_down=w_down, g_post_ffn=g_post_ffn)
    m = dict(g_pre_mix=m_g_pre_mix, w_in=m_w_in, ssm_lambda_re=m_ssm_lambda_re, ssm_lambda_im=m_ssm_lambda_im,
             ssm_log_dt=m_ssm_log_dt, ssm_b_re=m_ssm_b_re, ssm_b_im=m_ssm_b_im, ssm_c_re=m_ssm_c_re,
             ssm_c_im=m_ssm_c_im, ssm_d=m_ssm_d, w_glu=m_w_glu, b_glu=m_b_glu, attn_sinks=m_attn_sinks,
             g_ssm_out=m_g_ssm_out, g_attn_out=m_g_attn_out, w_out=m_w_out, g_post_mix=m_g_post_mix,
             g_pre_ffn=m_g_pre_ffn, w_gate_up=m_w_gate_up, w_down=m_w_down, g_post_ffn=m_g_post_ffn)
    v = dict(g_pre_mix=v_g_pre_mix, w_in=v_w_in, ssm_lambda_re=v_ssm_lambda_re, ssm_lambda_im=v_ssm_lambda_im,
             ssm_log_dt=v_ssm_log_dt, ssm_b_re=v_ssm_b_re, ssm_b_im=v_ssm_b_im, ssm_c_re=v_ssm_c_re,
             ssm_c_im=v_ssm_c_im, ssm_d=v_ssm_d, w_glu=v_w_glu, b_glu=v_b_glu, attn_sinks=v_attn_sinks,
             g_ssm_out=v_g_ssm_out, g_attn_out=v_g_attn_out, w_out=v_w_out, g_post_mix=v_g_post_mix,
             g_pre_ffn=v_g_pre_ffn, w_gate_up=v_w_gate_up, w_down=v_w_down, g_post_ffn=v_g_post_ffn)

    transposed = ("w_in", "w_glu", "w_gate_up")
    native_transposed = ("w_in", "w_gate_up")
    shard = {n: (w[n][0].T if n in transposed else w[n][0]).astype(_BF16) for n in _BIG}
    gathered = {}
    for names, lands in (
            (("w_in",), _sequencer_exchange([shard["w_in"]], [False], "gather_w_in", 1)),
            (("w_glu", "w_out"), _sequencer_exchange([shard["w_glu"], shard["w_out"]], [False] * 2, "gather_mix", 2)),
            (("w_gate_up", "w_down"), _sequencer_gather([shard["w_gate_up"], shard["w_down"]], "gather_ffn", 3))):
        gathered.update({n: a.reshape(-1, a.shape[2]) for n, a in zip(names, lands)})

    def fetch(names, after):
        del after
        return [gathered[n] for n in names]

    sent = []

    def publish(named):
        big = [n for n in named if n in _BIG]
        rows = [n for n in named if n in _ROW_WIDTH]
        dense = [n for n in named if n in _DENSE]
        plain = [n for n in named if n not in big + rows + dense]
        sources = [named[n].reshape(N_DEV, -1, named[n].shape[1]) for n in big]
        slots = _row_slots(rows)
        if rows:
            sources.append(_stack_rows(named, slots))
        sources += [named[n].reshape(-1, _LANES) for n in dense] + [named[n] for n in plain]
        flags = [True] * len(big) + [False] * (len(sources) - len(big))
        cid = 4 + len(sent)
        sent.append((big, slots, dense, plain, _sequencer_exchange(sources, flags, "grads_%d" % cid, cid)))
        return [named[n] for n in big]

    p = {n: w[n] for n in _SMALL}
    grad_x = _local_step(x[0], positions[0], loss_target[0], p, fetch, publish)

    state = {n: [_to_2d(n, a) for a in (w[n], m[n], v[n])] for n in _SMALL}
    result = {}
    total_loss = None
    chain = []
    for big, slots, dense, plain, lands in sent:
        lands = list(lands)
        after = list(chain)
        for name in big:
            part = lands.pop(0)
            if name in native_transposed:
                updated = _adamw(part, w[name][0].T, m[name][0].T, v[name][0].T, "adamw_" + name, after)
                result[name] = [a.T[None] for a in updated]
                chain = [updated[3]]
                continue
            if name in transposed:
                part = _sum_parts(part, "sum_" + name, after).T[None]
            updated = _adamw(part, w[name][0], m[name][0], v[name][0], "adamw_" + name, after)
            result[name] = [a[None] for a in updated]
            chain = [updated[3]]
        parts, items, sums, names = [], [], [], []
        if slots:
            parts.append(lands.pop(0))
            for name, (row, col, _) in slots.items():
                if name == "loss":
                    sums.append((0, slice(row, row + 1), slice(col, col + _LANES)))
                else:
                    items.append((0, slice(row, row + 1), slice(col, col + _SHAPE_2D[name][1]), *state[name]))
                    names.append(name)
        for name in dense:
            part = lands.pop(0).reshape((N_DEV,) + _SHAPE_2D[name])
            result[name] = _adamw(part, *state[name], "adamw_" + name, after)
            chain = [result[name][3]]
        for name in plain:
            rows_n, cols_n = _SHAPE_2D[name]
            items.append((len(parts), slice(0, rows_n), slice(0, cols_n), *state[name]))
            parts.append(lands.pop(0))
            names.append(name)
        if items:
            updated, summed = _adamw_small(parts, items, sums, "adamw_small_" + names[0], after)
            chain = [updated[0][3]]
            result.update(dict(zip(names, updated)))
            if summed:
                total_loss = summed[0][0, 0]

    out = [total_loss, grad_x[None]]
    for kind in range(4):
        out += [_from_2d(n, result[n][kind], w[n].shape) for n in _WEIGHTS]
    return tuple(out)
```

```python
import functools
import math

import numpy as np
import jax
import jax.numpy as jnp
from jax import lax
from jax.experimental import pallas as pl
from jax.experimental.pallas import tpu as pltpu
from jax.experimental.pallas import tpu_sc as plsc

D_MODEL = 1024
SSM_WIDTH = 512
SSM_GROUP = 16
SSM_GROUPS = 32
SSM_STATE = 64
N_STATE = SSM_GROUPS * SSM_STATE
ATTN_WIDTH = 512
HEAD_DIM = 64
N_Q_HEADS = 8
N_KV_HEADS = 2
Q_PER_KV = 4
KV_WIDTH = 128
IN_WIDTH = 1280
BLOCK = 128
ROPE_DIM = 16
ROPE_THETA = 500000.0
D_FF = 2816
NORM_EPS = 1e-6
MASK_VALUE = -1e30
ADAM_LR = 0.001
ADAM_B1 = 0.9
ADAM_B2 = 0.999
ADAM_EPS = 1e-08
ADAM_WD = 0.01
ADAM_STEP = 10

N_DEV = 8
SCAN_CHUNKS = 8
SCAN_COLS = 512
TOKEN_TILE = 256
VMEM_LIMIT = 56 * 1024 * 1024

_F32 = jnp.float32
_BF16 = jnp.bfloat16
_MXU = jnp.bfloat16

_NN = ((1,), (0,))
_NT = ((1,), (1,))
_TN = ((0,), (0,))


def _dot(a, b, dims):
    return lax.dot_general(a.astype(_MXU), b.astype(_MXU), (dims, ((), ())),
                           preferred_element_type=_F32)


def _dot_exact(a, b, dims):
    return lax.dot_general(a.astype(_F32), b.astype(_F32), (dims, ((), ())),
                           precision=lax.Precision.HIGHEST, preferred_element_type=_F32)


def _iota(shape, dim):
    return lax.broadcasted_iota(jnp.int32, shape, dim)


def _rms_fwd(x, g):
    r = lax.rsqrt(jnp.mean(x * x, axis=-1, keepdims=True) + NORM_EPS)
    return x * r * g, r


def _rms_bwd(dy, x, g, r):
    a = dy * g
    xn = x * r
    dx = r * (a - xn * jnp.mean(a * xn, axis=-1, keepdims=True))
    dg = jnp.sum(dy * xn, axis=0, keepdims=True)
    return dx, dg


def _call(body, grid, in_specs, out_specs, out_shape, name, scratch=(), tokens=()):
    params = pltpu.CompilerParams(dimension_semantics=("arbitrary",) * len(grid),
                                  vmem_limit_bytes=VMEM_LIMIT)
    n_in, n_tok = len(in_specs), len(tokens)

    def run(*refs):
        return body(*refs[:n_in], *refs[n_in + n_tok:])

    call = pl.pallas_call(run, grid=grid,
                          in_specs=list(in_specs) + [pl.BlockSpec(memory_space=pl.ANY)] * n_tok,
                          out_specs=out_specs, out_shape=out_shape, scratch_shapes=list(scratch),
                          compiler_params=params, name=name)
    return lambda *args: call(*args, *tokens)


def _rows(tm, n):
    return pl.BlockSpec((tm, n), lambda i: (i, 0))


def _whole(shape):
    nd = len(shape)
    return pl.BlockSpec(shape, lambda i: (0,) * nd)


def _sds(shape, dtype):
    return jax.ShapeDtypeStruct(shape, dtype)


def _tile(L):
    return min(TOKEN_TILE, L)


def _accumulate(ref, val, first):
    @pl.when(first)
    def _():
        ref[...] = val

    @pl.when(jnp.logical_not(first))
    def _():
        ref[...] += val


def _rope_rows():
    half = ROPE_DIM // 2
    inv = (np.float32(ROPE_THETA) ** (-np.arange(half, dtype=np.float32) * np.float32(2.0) / np.float32(ROPE_DIM))).astype(np.float32)
    col = np.arange(KV_WIDTH) % HEAD_DIM
    freq = np.where(col < ROPE_DIM, inv[col % half], 0.0).astype(np.float32)
    sign = np.where(col < half, -1.0, np.where(col < ROPE_DIM, 1.0, 0.0)).astype(np.float32)
    return freq[None, :], sign[None, :]


def _rope_tables(pos_col):
    L = pos_col.shape[0]
    tm = _tile(L)
    freq, sign = _rope_rows()

    def body(pos_ref, freq_ref, sign_ref, cos_ref, sin_ref):
        ang = pos_ref[...].astype(_F32) * freq_ref[...]
        cos_ref[...] = jnp.cos(ang)
        sin_ref[...] = jnp.sin(ang) * sign_ref[...]

    return _call(body, (L // tm,),
                 [_rows(tm, 1), _whole((1, KV_WIDTH)), _whole((1, KV_WIDTH))],
                 [_rows(tm, KV_WIDTH), _rows(tm, KV_WIDTH)],
                 [_sds((L, KV_WIDTH), _F32)] * 2, "rope_tables")(pos_col, jnp.asarray(freq), jnp.asarray(sign))


def _widen(t, width):
    return t if width == KV_WIDTH else jnp.concatenate([t] * (width // KV_WIDTH), axis=1)


def _rope_partner(t):
    w = t.shape[1]
    in_head = _iota((1, w), 1) & (HEAD_DIM - 1)
    second = jnp.where(in_head < ROPE_DIM, pltpu.roll(t, ROPE_DIM // 2, 1), 0.0)
    return jnp.where(in_head < ROPE_DIM // 2, pltpu.roll(t, w - ROPE_DIM // 2, 1), second)


def _rope_apply(t, cos_t, sin_t):
    w = t.shape[1]
    return t * _widen(cos_t, w) + _rope_partner(t) * _widen(sin_t, w)


def _rope_transpose(dt, cos_t, sin_t):
    w = dt.shape[1]
    return dt * _widen(cos_t, w) + _rope_partner(dt * _widen(sin_t, w))


def _in_proj(x, g_pre_mix, w_in, cos_t, sin_t):
    L = x.shape[0]
    tm = _tile(L)

    def body(x_ref, g_ref, w_ref, cos_ref, sin_ref, hn_ref, u_ref, q_ref, k_ref, v_ref):
        hn, _ = _rms_fwd(x_ref[...], g_ref[...])
        hn = hn.astype(_BF16)
        hn_ref[...] = hn
        proj = _dot(hn, w_ref[...], _NT)
        u_ref[...] = proj[:, :SSM_WIDTH]
        q = proj[:, SSM_WIDTH:SSM_WIDTH + ATTN_WIDTH]
        k = proj[:, SSM_WIDTH + ATTN_WIDTH:SSM_WIDTH + ATTN_WIDTH + KV_WIDTH]
        cos_v, sin_v = cos_ref[...], sin_ref[...]
        q_ref[...] = _rope_apply(q, cos_v, sin_v).astype(_BF16)
        k_ref[...] = _rope_apply(k, cos_v, sin_v).astype(_BF16)
        v_ref[...] = proj[:, SSM_WIDTH + ATTN_WIDTH + KV_WIDTH:].astype(_BF16)

    return _call(body, (L // tm,),
                 [_rows(tm, D_MODEL), _whole((1, D_MODEL)), _whole((IN_WIDTH, D_MODEL)),
                  _rows(tm, KV_WIDTH), _rows(tm, KV_WIDTH)],
                 [_rows(tm, D_MODEL), _rows(tm, SSM_WIDTH), _rows(tm, ATTN_WIDTH),
                  _rows(tm, KV_WIDTH), _rows(tm, KV_WIDTH)],
                 [_sds((L, D_MODEL), _BF16), _sds((L, SSM_WIDTH), _F32), _sds((L, ATTN_WIDTH), _BF16),
                  _sds((L, KV_WIDTH), _BF16), _sds((L, KV_WIDTH), _BF16)],
                 "in_proj")(x, g_pre_mix, w_in, cos_t, sin_t)


def _s5_discretize(lam_re, lam_im, log_dt):
    lr = jnp.minimum(lam_re, -1e-4)
    li = lam_im
    dt = jnp.exp(log_dt)
    mag = jnp.exp(lr * dt)
    ar = mag * jnp.cos(li * dt)
    ai = mag * jnp.sin(li * dt)
    den = lr * lr + li * li
    fr = ((ar - 1.0) * lr + ai * li) / den
    fi = (ai * lr - (ar - 1.0) * li) / den
    return ar, ai, fr, fi


def _s5_bbar(lam_re, lam_im, log_dt, b_re, b_im):
    ar, ai, fr, fi = _s5_discretize(lam_re, lam_im, log_dt)
    return ar, ai, fr * b_re - fi * b_im, fr * b_im + fi * b_re


def _spread_masks():
    e16 = (_iota((SSM_GROUP, SSM_WIDTH), 1) & (SSM_GROUP - 1)) == _iota((SSM_GROUP, SSM_WIDTH), 0)
    e64 = (_iota((SSM_STATE, N_STATE), 1) & (SSM_STATE - 1)) == _iota((SSM_STATE, N_STATE), 0)
    mask_b = (_iota((N_STATE, SSM_WIDTH), 0) >> 6) == (_iota((N_STATE, SSM_WIDTH), 1) >> 4)
    mask_c = (_iota((SSM_WIDTH, N_STATE), 0) >> 4) == (_iota((SSM_WIDTH, N_STATE), 1) >> 6)
    return e16.astype(_F32), e64.astype(_F32), mask_b, mask_c


SUPER = 4
SB_STATE = N_STATE // SUPER
SB_WIDTH = SSM_WIDTH // SUPER


def _sb_state(k):
    return slice(SB_STATE * k, SB_STATE * (k + 1))


def _sb_width(k):
    return slice(SB_WIDTH * k, SB_WIDTH * (k + 1))


def _dt_column(log_dt_row):
    eye = _iota((SSM_GROUPS, SSM_GROUPS), 0) == _iota((SSM_GROUPS, SSM_GROUPS), 1)
    return jnp.sum(jnp.where(eye, log_dt_row, 0.0), axis=1, keepdims=True)


def _group_masks():
    e64 = ((_iota((SSM_STATE, N_STATE), 1) & (SSM_STATE - 1)) == _iota((SSM_STATE, N_STATE), 0)).astype(_F32)
    own = _iota((SSM_GROUPS, N_STATE), 0) == (_iota((SSM_GROUPS, N_STATE), 1) >> 6)
    return e64, own


def _rows_of_group():
    return ((_iota((SSM_WIDTH, SSM_GROUPS), 0) >> 4) == _iota((SSM_WIDTH, SSM_GROUPS), 1)).astype(_F32)


def _ssm_prep(lam_re, lam_im, log_dt, b_re, b_im, c_re, c_im):
    def body(lr_ref, li_ref, ld_ref, bre, bim, cre, cim, ar_ref, ai_ref, btr, bti, ctr, cti):
        ar, ai, fr, fi = _s5_discretize(lr_ref[...], li_ref[...], _dt_column(ld_ref[...]))
        e64, own = _group_masks()
        mask_c = (_iota((SSM_WIDTH, N_STATE), 0) >> 4) == (_iota((SSM_WIDTH, N_STATE), 1) >> 6)

        def to_row(t):
            return jnp.sum(jnp.where(own, _dot_exact(t, e64, _NN), 0.0), axis=0, keepdims=True)

        def fold(m):
            full = jnp.where(mask_c, _dot(m, e64, _NN), 0.0)
            return sum(full[_sb_width(k), :] for k in range(SUPER)).astype(_BF16)

        ar_ref[...] = to_row(ar)
        ai_ref[...] = to_row(ai)
        spread = _rows_of_group()
        fr_t = _dot_exact(spread, fr, _NN)
        fi_t = _dot_exact(spread, fi, _NN)
        btr[...] = fold(fr_t * bre[...] - fi_t * bim[...])
        bti[...] = fold(fr_t * bim[...] + fi_t * bre[...])
        ctr[...] = fold(cre[...])
        cti[...] = fold(cim[...])

    row = (1, N_STATE)
    ins = [lam_re, lam_im, log_dt, b_re, b_im, c_re, c_im]
    return _call(body, (1,), [_whole(a.shape) for a in ins],
                 [_whole(row), _whole(row)] + [_whole((SB_WIDTH, N_STATE))] * 4,
                 [_sds(row, _F32), _sds(row, _F32)] + [_sds((SB_WIDTH, N_STATE), _BF16)] * 4,
                 "ssm_prep")(*ins)


def _ssm_bu(u, bt_re, bt_im):
    L = u.shape[0]
    tm = _tile(L)

    def body(u_ref, br_ref, bi_ref, or_ref, oi_ref):
        for k in range(SUPER):
            ub = u_ref[:, _sb_width(k)].astype(_BF16)
            or_ref[:, _sb_state(k)] = _dot(ub, br_ref[:, _sb_state(k)], _NN)
            oi_ref[:, _sb_state(k)] = _dot(ub, bi_ref[:, _sb_state(k)], _NN)

    return _call(body, (L // tm,),
                 [_rows(tm, SSM_WIDTH), _whole((SB_WIDTH, N_STATE)), _whole((SB_WIDTH, N_STATE))],
                 [_rows(tm, N_STATE), _rows(tm, N_STATE)],
                 [_sds((L, N_STATE), _F32)] * 2, "ssm_bu")(u, bt_re, bt_im)


def _complex_power(ar, ai, n):
    def step(_, c):
        pr, pi = c
        return pr * ar - pi * ai, pr * ai + pi * ar
    return lax.fori_loop(0, n, step, (jnp.ones_like(ar), jnp.zeros_like(ai)))


def _chunk_carries(er, ei, pr, pi, reverse):
    rows = _iota(er.shape, 0)
    sr = jnp.zeros_like(pr)
    si = jnp.zeros_like(pi)
    out_r = jnp.zeros_like(er)
    out_i = jnp.zeros_like(ei)
    order = range(SCAN_CHUNKS - 1, 0, -1) if reverse else range(SCAN_CHUNKS - 1)
    for c in order:
        e_r = er[c:c + 1, :]
        e_i = ei[c:c + 1, :]
        sr, si = pr * sr - pi * si + e_r, pr * si + pi * sr + e_i
        nxt = c - 1 if reverse else c + 1
        out_r = jnp.where(rows == nxt, sr, out_r)
        out_i = jnp.where(rows == nxt, si, out_i)
    return out_r, out_i


def _scan_fwd(b_re, b_im, a_re, a_im):
    T = b_re.shape[0]
    W = SCAN_COLS
    blk = pl.BlockSpec((T, SCAN_CHUNKS, W), lambda j: (0, 0, j))
    vec = pl.BlockSpec((1, W), lambda j: (0, j))

    def body(br_ref, bi_ref, ar_ref, ai_ref, xr_ref, xi_ref):
        ar, ai = ar_ref[...], ai_ref[...]
        ar8 = jnp.broadcast_to(ar, (SCAN_CHUNKS, W))
        ai8 = jnp.broadcast_to(ai, (SCAN_CHUNKS, W))

        def local(t, c):
            cr, ci = c
            return ar8 * cr - ai8 * ci + br_ref[t], ar8 * ci + ai8 * cr + bi_ref[t]

        zero = jnp.zeros((SCAN_CHUNKS, W), _F32)
        er, ei = lax.fori_loop(0, T, local, (zero, zero))
        pr, pi = _complex_power(ar, ai, T)
        sr, si = _chunk_carries(er, ei, pr, pi, reverse=False)

        def final(t, c):
            nr, ni = local(t, c)
            xr_ref[t] = nr
            xi_ref[t] = ni
            return nr, ni

        lax.fori_loop(0, T, final, (sr, si))

    shape = _sds(b_re.shape, _F32)
    return _call(body, (N_STATE // W,), [blk, blk, vec, vec], [blk, blk], [shape, shape],
                 "scan_fwd")(b_re, b_im, a_re, a_im)


def _scan_bwd(dx_re, dx_im, x_re, x_im, a_re, a_im, tokens=()):
    T = dx_re.shape[0]
    W = SCAN_COLS
    blk = pl.BlockSpec((T, SCAN_CHUNKS, W), lambda j: (0, 0, j))
    vec = pl.BlockSpec((1, W), lambda j: (0, j))

    def body(dr_ref, di_ref, xr_ref, xi_ref, ar_ref, ai_ref, lr_ref, li_ref, dar_ref, dai_ref):
        ar, ai = ar_ref[...], ai_ref[...]
        ar8 = jnp.broadcast_to(ar, (SCAN_CHUNKS, W))
        ai8 = jnp.broadcast_to(ai, (SCAN_CHUNKS, W))

        def local(t, c):
            cr, ci = c
            return ar8 * cr + ai8 * ci + dr_ref[t], ar8 * ci - ai8 * cr + di_ref[t]

        zero = jnp.zeros((SCAN_CHUNKS, W), _F32)
        er, ei = lax.fori_loop(0, T, lambda k, c: local(T - 1 - k, c), (zero, zero))
        pr, pi = _complex_power(ar, -ai, T)
        sr, si = _chunk_carries(er, ei, pr, pi, reverse=True)

        def grad_a(acc, nr, ni, xpr, xpi):
            return acc[0] + nr * xpr + ni * xpi, acc[1] + ni * xpr - nr * xpi

        def final(k, c):
            t = T - 1 - k
            nr, ni = local(t, c[:2])
            lr_ref[t] = nr
            li_ref[t] = ni
            gr, gi = grad_a(c[2:], nr, ni, xr_ref[t - 1], xi_ref[t - 1])
            return nr, ni, gr, gi

        cr, ci, gr, gi = lax.fori_loop(0, T - 1, final, (sr, si, zero, zero))
        nr, ni = local(0, (cr, ci))
        lr_ref[0] = nr
        li_ref[0] = ni
        first = _iota((SCAN_CHUNKS, W), 0) == 0
        xpr = jnp.where(first, 0.0, pltpu.roll(xr_ref[T - 1], 1, 0))
        xpi = jnp.where(first, 0.0, pltpu.roll(xi_ref[T - 1], 1, 0))
        gr, gi = grad_a((gr, gi), nr, ni, xpr, xpi)
        dar_ref[...] = jnp.sum(gr, axis=0, keepdims=True)
        dai_ref[...] = jnp.sum(gi, axis=0, keepdims=True)

    shape = _sds(dx_re.shape, _F32)
    row = _sds((1, N_STATE), _F32)
    return _call(body, (N_STATE // W,), [blk, blk, blk, blk, vec, vec], [blk, blk, vec, vec],
                 [shape, shape, row, row], "scan_bwd", tokens=tokens)(dx_re, dx_im, x_re, x_im, a_re, a_im)


_GELU_K = math.sqrt(2.0 / math.pi)
_GELU_C = 0.044715


def _gelu(y):
    return 0.5 * y * (1.0 + jnp.tanh(_GELU_K * (y + _GELU_C * y * y * y)))


def _gelu_grad(y):
    t = jnp.tanh(_GELU_K * (y + _GELU_C * y * y * y))
    return 0.5 * (1.0 + t) + 0.5 * y * (1.0 - t * t) * _GELU_K * (1.0 + 3.0 * _GELU_C * y * y)


def _ssm_out(x_re, x_im, u, ct_re, ct_im, d_row, w_glu, b_glu, g_ssm):
    L = u.shape[0]
    tm = _tile(L)

    def body(xr_ref, xi_ref, u_ref, cr_ref, ci_ref, d_ref, w_ref, b_ref, g_ref, y_ref, z_ref, n_ref):
        cx = [_dot(xr_ref[:, _sb_state(k)], cr_ref[:, _sb_state(k)], _NT)
              - _dot(xi_ref[:, _sb_state(k)], ci_ref[:, _sb_state(k)], _NT) for k in range(SUPER)]
        y = jnp.concatenate(cx, axis=1) + d_ref[...] * u_ref[...]
        y_ref[...] = y
        z = _dot(_gelu(y), w_ref[...], _NT) + b_ref[...]
        z_ref[...] = z
        out = z[:, :SSM_WIDTH] * jax.nn.sigmoid(z[:, SSM_WIDTH:])
        n, _ = _rms_fwd(out, g_ref[...])
        n_ref[...] = n.astype(_BF16)

    return _call(body, (L // tm,),
                 [_rows(tm, N_STATE), _rows(tm, N_STATE), _rows(tm, SSM_WIDTH),
                  _whole((SB_WIDTH, N_STATE)), _whole((SB_WIDTH, N_STATE)), _whole((1, SSM_WIDTH)),
                  _whole((2 * SSM_WIDTH, SSM_WIDTH)), _whole((1, 2 * SSM_WIDTH)), _whole((1, SSM_WIDTH))],
                 [_rows(tm, SSM_WIDTH), _rows(tm, 2 * SSM_WIDTH), _rows(tm, SSM_WIDTH)],
                 [_sds((L, SSM_WIDTH), _F32), _sds((L, 2 * SSM_WIDTH), _F32), _sds((L, SSM_WIDTH), _BF16)],
                 "ssm_out")(x_re, x_im, u, ct_re, ct_im, d_row, w_glu, b_glu, g_ssm)


def _ssm_out_bwd(dn, y, z, u, ct_re, ct_im, d_row, w_glu, g_ssm):
    L = u.shape[0]
    tm = _tile(L)

    def body(dn_ref, y_ref, z_ref, u_ref, cr_ref, ci_ref, d_ref, w_ref, g_ref,
             gy_ref, dz_ref, dy_ref, dud_ref, dxr_ref, dxi_ref, dg_ref, db_ref, dd_ref):
        first = pl.program_id(0) == 0
        z = z_ref[...]
        z1, z2 = z[:, :SSM_WIDTH], z[:, SSM_WIDTH:]
        sig = jax.nn.sigmoid(z2)
        out = z1 * sig
        g = g_ref[...]
        _, r = _rms_fwd(out, g)
        dout, dg = _rms_bwd(dn_ref[...], out, g, r)
        _accumulate(dg_ref, dg, first)
        dz = jnp.concatenate([dout * sig, dout * z1 * sig * (1.0 - sig)], axis=1)
        _accumulate(db_ref, jnp.sum(dz, axis=0, keepdims=True), first)
        dzb = dz.astype(_BF16)
        dz_ref[...] = dzb
        y = y_ref[...]
        gy_ref[...] = _gelu(y).astype(_BF16)
        dy = _dot(dzb, w_ref[...], _NN) * _gelu_grad(y)
        u = u_ref[...]
        _accumulate(dd_ref, jnp.sum(dy * u, axis=0, keepdims=True), first)
        dud_ref[...] = d_ref[...] * dy
        dyb = dy.astype(_BF16)
        dy_ref[...] = dyb
        for k in range(SUPER):
            dxr_ref[:, _sb_state(k)] = _dot(dyb[:, _sb_width(k)], cr_ref[:, _sb_state(k)], _NN)
            dxi_ref[:, _sb_state(k)] = -_dot(dyb[:, _sb_width(k)], ci_ref[:, _sb_state(k)], _NN)

    row = _whole((1, SSM_WIDTH))
    return _call(body, (L // tm,),
                 [_rows(tm, SSM_WIDTH), _rows(tm, SSM_WIDTH), _rows(tm, 2 * SSM_WIDTH), _rows(tm, SSM_WIDTH),
                  _whole((SB_WIDTH, N_STATE)), _whole((SB_WIDTH, N_STATE)), row,
                  _whole((2 * SSM_WIDTH, SSM_WIDTH)), row],
                 [_rows(tm, SSM_WIDTH), _rows(tm, 2 * SSM_WIDTH), _rows(tm, SSM_WIDTH), _rows(tm, SSM_WIDTH),
                  _rows(tm, N_STATE), _rows(tm, N_STATE), row, _whole((1, 2 * SSM_WIDTH)), row],
                 [_sds((L, SSM_WIDTH), _BF16), _sds((L, 2 * SSM_WIDTH), _BF16), _sds((L, SSM_WIDTH), _BF16),
                  _sds((L, SSM_WIDTH), _F32), _sds((L, N_STATE), _F32), _sds((L, N_STATE), _F32),
                  _sds((1, SSM_WIDTH), _F32), _sds((1, 2 * SSM_WIDTH), _F32), _sds((1, SSM_WIDTH), _F32)],
                 "ssm_out_bwd")(dn, y, z, u, ct_re, ct_im, d_row, w_glu, g_ssm)


def _ssm_du(lam_re, lam_im, bt_re, bt_im, dud):
    L = dud.shape[0]
    tm = _tile(L)

    def body(lr_ref, li_ref, br_ref, bi_ref, dud_ref, du_ref):
        for k in range(SUPER):
            du_ref[:, _sb_width(k)] = (_dot(lr_ref[:, _sb_state(k)], br_ref[:, _sb_state(k)], _NT)
                                       + _dot(li_ref[:, _sb_state(k)], bi_ref[:, _sb_state(k)], _NT)
                                       + dud_ref[:, _sb_width(k)])

    return _call(body, (L // tm,),
                 [_rows(tm, N_STATE), _rows(tm, N_STATE), _whole((SB_WIDTH, N_STATE)),
                  _whole((SB_WIDTH, N_STATE)), _rows(tm, SSM_WIDTH)],
                 _rows(tm, SSM_WIDTH), _sds((L, SSM_WIDTH), _F32), "ssm_du")(lam_re, lam_im, bt_re, bt_im, dud)


def _ssm_weight_grads(dy, x_re, x_im, lam_re, lam_im, u):
    L = u.shape[0]

    def body(dy_ref, xr_ref, xi_ref, lr_ref, li_ref, u_ref, dcr_ref, dci_ref, dbr_ref, dbi_ref):
        dyb = dy_ref[...]
        ub = u_ref[...].astype(_BF16)
        dcr_ref[...] = _dot(dyb, xr_ref[...], _TN)
        dci_ref[...] = _dot(dyb, xi_ref[...], _TN)
        dbr_ref[...] = _dot(ub, lr_ref[...], _TN)
        dbi_ref[...] = _dot(ub, li_ref[...], _TN)

    width = pl.BlockSpec((L, SB_WIDTH), lambda k: (0, k))
    state = pl.BlockSpec((L, SB_STATE), lambda k: (0, k))
    out = pl.BlockSpec((SB_WIDTH, SB_STATE), lambda k: (0, k))
    return _call(body, (SUPER,), [width, state, state, state, state, width], [out] * 4,
                 [_sds((SB_WIDTH, N_STATE), _F32)] * 4,
                 "ssm_weight_grads")(dy, x_re, x_im, lam_re, lam_im, u)


_SSM_PACK = {"ssm_b_re": (0, SSM_WIDTH, SSM_STATE), "ssm_b_im": (512, SSM_WIDTH, SSM_STATE),
             "ssm_c_re": (1024, SSM_WIDTH, SSM_STATE), "ssm_c_im": (1536, SSM_WIDTH, SSM_STATE),
             "ssm_lambda_re": (2048, SSM_GROUPS, SSM_STATE), "ssm_lambda_im": (2080, SSM_GROUPS, SSM_STATE),
             "ssm_d": (2112, SSM_GROUPS, SSM_GROUP), "ssm_log_dt": (2144, 1, SSM_GROUPS)}
_SSM_PACK_ROWS = 2152


def _ssm_param_bwd(da_re, da_im, dbt_re, dbt_im, dct_re, dct_im, lam_re, lam_im, log_dt, b_re, b_im, g_d):
    def body(dar, dai, dbr, dbi, dcr, dci, lr_ref, li_ref, ld_ref, bre_ref, bim_ref, gd_ref, pack_ref):
        def part(name):
            first, rows, lanes = _SSM_PACK[name]
            return pack_ref.at[first:first + rows, 0:lanes]

        gbr, gbi, gcr, gci = part("ssm_b_re"), part("ssm_b_im"), part("ssm_c_re"), part("ssm_c_im")
        glr, gli, gdt = part("ssm_lambda_re"), part("ssm_lambda_im"), part("ssm_log_dt")
        tail = _SSM_PACK["ssm_d"][0]
        pack_ref[tail:, :] = jnp.zeros((_SSM_PACK_ROWS - tail, SSM_STATE), _F32)
        part("ssm_d")[...] = gd_ref[...]
        own_c = (_iota((SB_WIDTH, SB_STATE), 0) >> 4) == (_iota((SB_WIDTH, SB_STATE), 1) >> 6)

        def unfold(ref):
            blocks = []
            for k in range(SUPER):
                t = jnp.where(own_c, ref[:, _sb_state(k)], 0.0)
                t = sum(t[:, 128 * i:128 * (i + 1)] for i in range(SB_STATE // 128))
                blocks.append((t + pltpu.roll(t, SSM_STATE, 1))[:, :SSM_STATE])
            return jnp.concatenate(blocks, axis=0)

        gcr[...] = unfold(dcr)
        gci[...] = -unfold(dci)
        dbb_re, dbb_im = unfold(dbr), unfold(dbi)
        b_re, b_im = bre_ref[...], bim_ref[...]
        dt_col = _dt_column(ld_ref[...])
        (_, _, fr, fi), vjp = jax.vjp(_s5_discretize, lr_ref[...], li_ref[...], dt_col)
        spread = _rows_of_group()
        fr_t = _dot_exact(spread, fr, _NN)
        fi_t = _dot_exact(spread, fi, _NN)
        gbr[...] = fr_t * dbb_re + fi_t * dbb_im
        gbi[...] = fr_t * dbb_im - fi_t * dbb_re
        d_fr = _dot_exact(spread, dbb_re * b_re + dbb_im * b_im, _TN)
        d_fi = _dot_exact(spread, dbb_im * b_re - dbb_re * b_im, _TN)
        e64, own = _group_masks()

        def from_row(ref):
            return _dot_exact(jnp.where(own, ref[...], 0.0), e64, _NT)

        d_lr, d_li, d_dt = vjp((from_row(dar), from_row(dai), d_fr, d_fi))
        glr[...] = d_lr
        gli[...] = d_li
        eye = (_iota((SSM_GROUPS, SSM_GROUPS), 0) == _iota((SSM_GROUPS, SSM_GROUPS), 1)).astype(_F32)
        gdt[...] = _dot_exact(jnp.broadcast_to(d_dt, (SSM_GROUPS, 128)), eye, _TN)[0:1]

    ins = [da_re, da_im, dbt_re, dbt_im, dct_re, dct_im, lam_re, lam_im, log_dt, b_re, b_im, g_d]
    out = (_SSM_PACK_ROWS, SSM_STATE)
    return _call(body, (1,), [_whole(a.shape) for a in ins], _whole(out), _sds(out, _F32), "ssm_param_bwd")(*ins)


def _head_spread(j):
    r = _iota((KV_WIDTH, 256), 0)
    c = _iota((KV_WIDTH, 256), 1)
    return (r == HEAD_DIM * j + (c & (HEAD_DIM - 1))).astype(_BF16)


STACK = Q_PER_KV * BLOCK


def _stack_heads(t):
    lane_head = _iota((1, 256), 1) >> 6
    return jnp.concatenate([jnp.where(lane_head == g, t, jnp.zeros_like(t)) for g in range(Q_PER_KV)], axis=0)


def _unstack_heads(t):
    lane_head = _iota((1, 256), 1) >> 6
    return sum(jnp.where(lane_head == g, t[BLOCK * g:BLOCK * (g + 1)], 0.0) for g in range(Q_PER_KV))


def _stacked_sinks(sink_ref, j):
    block = _iota((STACK, 1), 0) >> 7
    col = jnp.full((STACK, 1), sink_ref[Q_PER_KV * j], _F32)
    for g in range(1, Q_PER_KV):
        col = jnp.where(block == g, sink_ref[Q_PER_KV * j + g], col)
    return col


def _fold_heads(t, j):
    t = t[:, :KV_WIDTH] + t[:, KV_WIDTH:]
    t = t + pltpu.roll(t, HEAD_DIM, 1)
    return jnp.where((_iota((1, KV_WIDTH), 1) >> 6) == j, t, 0.0)


def _attn_scores(q_stacked, kt, blk, sink):
    s = _dot(q_stacked, kt, _NT) * (HEAD_DIM ** -0.5)
    qi = _iota((STACK, 2 * BLOCK), 0) & (BLOCK - 1)
    kj = _iota((STACK, 2 * BLOCK), 1)
    rel = qi + BLOCK - kj
    valid = (rel >= 0) & (rel < BLOCK) & (blk * BLOCK - BLOCK + kj >= 0)
    s = jnp.where(valid, s, MASK_VALUE)
    m = jnp.maximum(jnp.max(s, axis=-1, keepdims=True), sink)
    p = jnp.exp(s - m)
    e_sink = jnp.exp(sink - m)
    den = jnp.sum(p, axis=-1, keepdims=True) + e_sink
    return p / den, e_sink / den


def _attn_specs():
    prev = lambda i: (jnp.maximum(i - 1, 0), 0)
    cur = lambda i: (i, 0)
    kv = [pl.BlockSpec((BLOCK, KV_WIDTH), prev), pl.BlockSpec((BLOCK, KV_WIDTH), cur)]
    return [pl.BlockSpec((BLOCK, ATTN_WIDTH), cur)] + kv + kv


def _attn_fwd(q, k, v, sinks, g_attn):
    L = q.shape[0]

    def body(q_ref, kp_ref, kc_ref, vp_ref, vc_ref, sink_ref, g_ref, o_ref, n_ref):
        blk = pl.program_id(0)
        kwin = jnp.concatenate([kp_ref[...], kc_ref[...]], axis=0)
        vwin = jnp.concatenate([vp_ref[...], vc_ref[...]], axis=0)
        halves = []
        for j in range(N_KV_HEADS):
            spread = _head_spread(j)
            kt = _dot(kwin, spread, _NN).astype(_BF16)
            vt = _dot(vwin, spread, _NN).astype(_BF16)
            qs = _stack_heads(q_ref[:, 256 * j:256 * (j + 1)])
            p, _ = _attn_scores(qs, kt, blk, _stacked_sinks(sink_ref, j))
            halves.append(_unstack_heads(_dot(p, vt, _NN)))
        o = jnp.concatenate(halves, axis=1)
        o_ref[...] = o
        n, _ = _rms_fwd(o, g_ref[...])
        n_ref[...] = n.astype(_BF16)

    cur = lambda i: (i, 0)
    return _call(body, (L // BLOCK,),
                 _attn_specs() + [pl.BlockSpec(memory_space=pltpu.SMEM), _whole((1, ATTN_WIDTH))],
                 [pl.BlockSpec((BLOCK, ATTN_WIDTH), cur)] * 2,
                 [_sds((L, ATTN_WIDTH), _F32), _sds((L, ATTN_WIDTH), _BF16)],
                 "attn_fwd")(q, k, k, v, v, sinks, g_attn)


def _attn_bwd(q, k, v, o, dn, sinks, g_attn):
    L = q.shape[0]

    def body(q_ref, kp_ref, kc_ref, vp_ref, vc_ref, o_ref, dn_ref, sink_ref, g_ref,
             dq_ref, dk_ref, dv_ref, dsink_ref, dg_ref):
        blk = pl.program_id(0)
        first = blk == 0

        @pl.when(first)
        def _():
            dk_ref[...] = jnp.zeros_like(dk_ref)
            dv_ref[...] = jnp.zeros_like(dv_ref)
            dsink_ref[...] = jnp.zeros_like(dsink_ref)

        o = o_ref[...]
        g = g_ref[...]
        _, r = _rms_fwd(o, g)
        do, dg = _rms_bwd(dn_ref[...], o, g, r)
        _accumulate(dg_ref, dg, first)
        kwin = jnp.concatenate([kp_ref[...], kc_ref[...]], axis=0)
        vwin = jnp.concatenate([vp_ref[...], vc_ref[...]], axis=0)
        lane = _iota((1, 128), 1)
        dsink = jnp.zeros((1, 128), _F32)
        dkwin = jnp.zeros((2 * BLOCK, KV_WIDTH), _F32)
        dvwin = jnp.zeros((2 * BLOCK, KV_WIDTH), _F32)
        dq_halves = []
        for j in range(N_KV_HEADS):
            spread = _head_spread(j)
            kt = _dot(kwin, spread, _NN).astype(_BF16)
            vt = _dot(vwin, spread, _NN).astype(_BF16)
            qs = _stack_heads(q_ref[:, 256 * j:256 * (j + 1)])
            dos = _stack_heads(do[:, 256 * j:256 * (j + 1)]).astype(_BF16)
            p, p_sink = _attn_scores(qs, kt, blk, _stacked_sinks(sink_ref, j))
            dp = _dot(dos, vt, _NT)
            delta = jnp.sum(p * dp, axis=-1, keepdims=True)
            ds = (p * (dp - delta) * (HEAD_DIM ** -0.5)).astype(_BF16)
            sink_term = p_sink * delta
            for g in range(Q_PER_KV):
                head_sum = jnp.sum(sink_term[BLOCK * g:BLOCK * (g + 1)], axis=0, keepdims=True)
                dsink = dsink - jnp.where(lane == Q_PER_KV * j + g, head_sum, 0.0)
            dvwin = dvwin + _fold_heads(_dot(p, dos, _TN), j)
            dkwin = dkwin + _fold_heads(_dot(ds, qs, _TN), j)
            dq_halves.append(_unstack_heads(_dot(ds, kt, _NN)))
        dq_ref[...] = jnp.concatenate(dq_halves, axis=1)
        dsink_ref[...] += dsink
        prev = pl.ds(pl.multiple_of(jnp.maximum(blk - 1, 0) * BLOCK, BLOCK), BLOCK)
        cur = pl.ds(pl.multiple_of(blk * BLOCK, BLOCK), BLOCK)
        dk_ref[prev, :] += dkwin[:BLOCK]
        dk_ref[cur, :] += dkwin[BLOCK:]
        dv_ref[prev, :] += dvwin[:BLOCK]
        dv_ref[cur, :] += dvwin[BLOCK:]

    cur = lambda i: (i, 0)
    blk_q = pl.BlockSpec((BLOCK, ATTN_WIDTH), cur)
    return _call(body, (L // BLOCK,),
                 _attn_specs() + [blk_q, blk_q, pl.BlockSpec(memory_space=pltpu.SMEM), _whole((1, ATTN_WIDTH))],
                 [blk_q, _whole((L, KV_WIDTH)), _whole((L, KV_WIDTH)), _whole((1, 128)), _whole((1, ATTN_WIDTH))],
                 [_sds((L, ATTN_WIDTH), _F32), _sds((L, KV_WIDTH), _F32), _sds((L, KV_WIDTH), _F32),
                  _sds((1, 128), _F32), _sds((1, ATTN_WIDTH), _F32)],
                 "attn_bwd")(q, k, k, v, v, o, dn, sinks, g_attn)


def _out_proj(n_ssm, n_attn, x, w_out, g_post_mix, g_pre_ffn):
    L = x.shape[0]
    tm = _tile(L)

    def body(ns_ref, na_ref, x_ref, w_ref, g1_ref, g2_ref, merged_ref, mo_ref, h1_ref, hn2_ref):
        merged = jnp.concatenate([ns_ref[...], na_ref[...]], axis=1)
        merged_ref[...] = merged
        mo = _dot(merged, w_ref[...], _NN)
        mo_ref[...] = mo
        n, _ = _rms_fwd(mo, g1_ref[...])
        h1 = x_ref[...] + n
        h1_ref[...] = h1
        hn2, _ = _rms_fwd(h1, g2_ref[...])
        hn2_ref[...] = hn2.astype(_BF16)

    row = _whole((1, D_MODEL))
    return _call(body, (L // tm,),
                 [_rows(tm, SSM_WIDTH), _rows(tm, ATTN_WIDTH), _rows(tm, D_MODEL), _whole((D_MODEL, D_MODEL)), row, row],
                 [_rows(tm, D_MODEL)] * 4,
                 [_sds((L, D_MODEL), _BF16), _sds((L, D_MODEL), _F32), _sds((L, D_MODEL), _F32), _sds((L, D_MODEL), _BF16)],
                 "out_proj")(n_ssm, n_attn, x, w_out, g_post_mix, g_pre_ffn)


def _ffn(hn2, h1, target, w_gate_up, w_down, g_pre_ffn, g_post_ffn):
    L = h1.shape[0]
    tm = _tile(L)
    half = D_FF // 2

    def body(hn2_ref, h1_ref, tgt_ref, wgu_hbm, wd_hbm, g2_ref, g3_ref,
             act_ref, dgu_ref, dff_ref, dh1_ref, loss_ref, dg3_ref, dg2_ref,
             wgu, wd, gu, sem):
        first = pl.program_id(0) == 0

        @pl.when(first)
        def _():
            c1 = pltpu.make_async_copy(wgu_hbm, wgu, sem.at[0])
            c2 = pltpu.make_async_copy(wd_hbm, wd, sem.at[1])
            c1.start()
            c2.start()
            c1.wait()
            c2.wait()

        hn2 = hn2_ref[...]
        ff = jnp.zeros((tm, D_MODEL), _F32)
        for c in range(2):
            gate = _dot(hn2, wgu[half * c:half * (c + 1), :], _NT)
            up = _dot(hn2, wgu[D_FF + half * c:D_FF + half * (c + 1), :], _NT)
            gu[:, half * c:half * (c + 1)] = gate
            gu[:, D_FF + half * c:D_FF + half * (c + 1)] = up
            act = (gate * jax.nn.sigmoid(gate) * up).astype(_BF16)
            act_ref[:, half * c:half * (c + 1)] = act
            ff = ff + _dot(act, wd[half * c:half * (c + 1), :], _NN)
        g3 = g3_ref[...]
        n, r = _rms_fwd(ff, g3)
        h1 = h1_ref[...]
        err = h1 + n - tgt_ref[...]
        loss = 0.5 * jnp.sum(jnp.mean(err * err, axis=-1, keepdims=True), axis=0, keepdims=True)
        _accumulate(loss_ref, jnp.broadcast_to(loss, (1, 128)), first)
        dh2 = err * (1.0 / D_MODEL)
        dff, dg3 = _rms_bwd(dh2, ff, g3, r)
        _accumulate(dg3_ref, dg3, first)
        dffb = dff.astype(_BF16)
        dff_ref[...] = dffb
        dhn2 = jnp.zeros((tm, D_MODEL), _F32)
        for c in range(2):
            dact = _dot(dffb, wd[half * c:half * (c + 1), :], _NT)
            gate = gu[:, half * c:half * (c + 1)]
            up = gu[:, D_FF + half * c:D_FF + half * (c + 1)]
            sig = jax.nn.sigmoid(gate)
            silu = gate * sig
            dgate = (dact * up * (sig + silu * (1.0 - sig))).astype(_BF16)
            dup = (dact * silu).astype(_BF16)
            dgu_ref[:, half * c:half * (c + 1)] = dgate
            dgu_ref[:, D_FF + half * c:D_FF + half * (c + 1)] = dup
            dhn2 = dhn2 + _dot(dgate, wgu[half * c:half * (c + 1), :], _NN)
            dhn2 = dhn2 + _dot(dup, wgu[D_FF + half * c:D_FF + half * (c + 1), :], _NN)
        g2 = g2_ref[...]
        _, r2 = _rms_fwd(h1, g2)
        dh1, dg2 = _rms_bwd(dhn2, h1, g2, r2)
        _accumulate(dg2_ref, dg2, first)
        dh1_ref[...] = dh2 + dh1

    row = _whole((1, D_MODEL))
    anyspace = pl.BlockSpec(memory_space=pl.ANY)
    return _call(body, (L // tm,),
                 [_rows(tm, D_MODEL), _rows(tm, D_MODEL), _rows(tm, D_MODEL), anyspace, anyspace, row, row],
                 [_rows(tm, D_FF), _rows(tm, 2 * D_FF), _rows(tm, D_MODEL), _rows(tm, D_MODEL),
                  _whole((1, 128)), row, row],
                 [_sds((L, D_FF), _BF16), _sds((L, 2 * D_FF), _BF16), _sds((L, D_MODEL), _BF16),
                  _sds((L, D_MODEL), _F32), _sds((1, 128), _F32), _sds((1, D_MODEL), _F32), _sds((1, D_MODEL), _F32)],
                 "ffn",
                 scratch=[pltpu.VMEM((2 * D_FF, D_MODEL), _BF16), pltpu.VMEM((D_FF, D_MODEL), _BF16),
                          pltpu.VMEM((tm, 2 * D_FF), _F32), pltpu.SemaphoreType.DMA((2,))],
                 )(hn2, h1, target, w_gate_up, w_down, g_pre_ffn, g_post_ffn)


def _out_proj_bwd(dh1, mo, w_out, g_post_mix, tokens=()):
    L = dh1.shape[0]
    tm = _tile(L)

    def body(dh1_ref, mo_ref, w_ref, g_ref, dmo_ref, dns_ref, dna_ref, dg_ref):
        first = pl.program_id(0) == 0
        mo = mo_ref[...]
        g = g_ref[...]
        _, r = _rms_fwd(mo, g)
        dmo, dg = _rms_bwd(dh1_ref[...], mo, g, r)
        _accumulate(dg_ref, dg, first)
        dmob = dmo.astype(_BF16)
        dmo_ref[...] = dmob
        dmerged = _dot(dmob, w_ref[...], _NT)
        dns_ref[...] = dmerged[:, :SSM_WIDTH]
        dna_ref[...] = dmerged[:, SSM_WIDTH:]

    row = _whole((1, D_MODEL))
    return _call(body, (L // tm,),
                 [_rows(tm, D_MODEL), _rows(tm, D_MODEL), _whole((D_MODEL, D_MODEL)), row],
                 [_rows(tm, D_MODEL), _rows(tm, SSM_WIDTH), _rows(tm, ATTN_WIDTH), row],
                 [_sds((L, D_MODEL), _BF16), _sds((L, SSM_WIDTH), _F32), _sds((L, ATTN_WIDTH), _F32),
                  _sds((1, D_MODEL), _F32)],
                 "out_proj_bwd", tokens=tokens)(dh1, mo, w_out, g_post_mix)


def _in_proj_bwd(du, dq, dk, dv, cos_t, sin_t, x, dh1, g_pre_mix, w_in, tokens=()):
    L = x.shape[0]
    tm = _tile(L)

    def body(du_ref, dq_ref, dk_ref, dv_ref, cos_ref, sin_ref, x_ref, dh1_ref, g_ref, w_ref,
             dproj_ref, dx_ref, dg_ref):
        first = pl.program_id(0) == 0
        cos_v, sin_v = cos_ref[...], sin_ref[...]
        dproj = jnp.concatenate([du_ref[...], _rope_transpose(dq_ref[...], cos_v, sin_v),
                                 _rope_transpose(dk_ref[...], cos_v, sin_v), dv_ref[...]], axis=1).astype(_BF16)
        dproj_ref[...] = dproj
        dhn = _dot(dproj, w_ref[...], _NN)
        x = x_ref[...]
        g = g_ref[...]
        _, r = _rms_fwd(x, g)
        dx, dg = _rms_bwd(dhn, x, g, r)
        _accumulate(dg_ref, dg, first)
        dx_ref[...] = dh1_ref[...] + dx

    row = _whole((1, D_MODEL))
    return _call(body, (L // tm,),
                 [_rows(tm, SSM_WIDTH), _rows(tm, ATTN_WIDTH), _rows(tm, KV_WIDTH), _rows(tm, KV_WIDTH),
                  _rows(tm, KV_WIDTH), _rows(tm, KV_WIDTH), _rows(tm, D_MODEL), _rows(tm, D_MODEL), row,
                  _whole((IN_WIDTH, D_MODEL))],
                 [_rows(tm, IN_WIDTH), _rows(tm, D_MODEL), row],
                 [_sds((L, IN_WIDTH), _BF16), _sds((L, D_MODEL), _F32), _sds((1, D_MODEL), _F32)],
                 "in_proj_bwd", tokens=tokens)(du, dq, dk, dv, cos_t, sin_t, x, dh1, g_pre_mix, w_in)


def _matmul_tn(a, b, out_dtype, name, scale=1.0):
    K, M = a.shape
    N = b.shape[1]
    tm = next(t for t in (512, 256, 128) if M % t == 0)
    tn = next(t for t in (512, 256, 128) if N % t == 0)

    def body(a_ref, b_ref, o_ref):
        acc = _dot(a_ref[...], b_ref[...], _TN)
        o_ref[...] = (acc if scale == 1.0 else acc * scale).astype(out_dtype)

    params = pltpu.CompilerParams(dimension_semantics=("arbitrary", "arbitrary"), vmem_limit_bytes=VMEM_LIMIT)
    return pl.pallas_call(body, grid=(M // tm, N // tn),
                          in_specs=[pl.BlockSpec((K, tm), lambda i, j: (0, i)),
                                    pl.BlockSpec((K, tn), lambda i, j: (0, j))],
                          out_specs=pl.BlockSpec((tm, tn), lambda i, j: (i, j)),
                          out_shape=_sds((M, N), out_dtype), compiler_params=params, name=name)(a, b)


def _to_chunked(a):
    L, n = a.shape
    return a.reshape(SCAN_CHUNKS, L // SCAN_CHUNKS, n).transpose(1, 0, 2).reshape(L, n)


def _from_chunked(a):
    L, n = a.shape
    return a.reshape(L // SCAN_CHUNKS, SCAN_CHUNKS, n).transpose(1, 0, 2).reshape(L, n)


def _local_step(x, pos, target, p, fetch, publish):
    L = x.shape[0]
    T = L // SCAN_CHUNKS
    cos_t, sin_t = _rope_tables(pos.reshape(L, 1))
    w_in, = fetch(("w_in",), None)
    hn, u, q, k, v = _in_proj(x, p["g_pre_mix"], w_in, cos_t, sin_t)

    ssm = {n: _to_2d(n, p[n]) for n in ("ssm_lambda_re", "ssm_lambda_im", "ssm_log_dt", "ssm_b_re", "ssm_b_im",
                                        "ssm_c_re", "ssm_c_im")}
    d_row = p["ssm_d"].reshape(1, SSM_WIDTH)
    a_re, a_im, bt_re, bt_im, ct_re, ct_im = _ssm_prep(
        ssm["ssm_lambda_re"], ssm["ssm_lambda_im"], ssm["ssm_log_dt"], ssm["ssm_b_re"], ssm["ssm_b_im"],
        ssm["ssm_c_re"], ssm["ssm_c_im"])

    u_c = _to_chunked(u)
    bu_re, bu_im = _ssm_bu(u_c, bt_re, bt_im)
    x_re, x_im = _scan_fwd(bu_re.reshape(T, SCAN_CHUNKS, N_STATE), bu_im.reshape(T, SCAN_CHUNKS, N_STATE), a_re, a_im)
    w_glu, = fetch(("w_glu",), x_re)
    y, z, n_ssm_c = _ssm_out(x_re.reshape(L, N_STATE), x_im.reshape(L, N_STATE), u_c, ct_re, ct_im, d_row,
                             w_glu, p["b_glu"], p["g_ssm_out"])
    n_ssm = _from_chunked(n_ssm_c)

    sinks = p["attn_sinks"].reshape(N_Q_HEADS)
    o, n_attn = _attn_fwd(q, k, v, sinks, p["g_attn_out"])
    w_out, = fetch(("w_out",), n_attn)
    merged, mo, h1, hn2 = _out_proj(n_ssm, n_attn, x, w_out, p["g_post_mix"], p["g_pre_ffn"])
    w_gate_up, w_down = fetch(("w_gate_up", "w_down"), hn2)
    act, dgu, dff, dh1, loss, dg_post_ffn, dg_pre_ffn = _ffn(
        hn2, h1, target, w_gate_up, w_down, p["g_pre_ffn"], p["g_post_ffn"])
    grads = {"g_post_ffn": dg_post_ffn, "g_pre_ffn": dg_pre_ffn}
    tokens = publish({"w_down": _matmul_tn(act, dff, _BF16, "grad_w_down"),
                      "w_gate_up": _matmul_tn(dgu, hn2, _BF16, "grad_w_gate_up")})

    dmo, dn_ssm, dn_attn, grads["g_post_mix"] = _out_proj_bwd(dh1, mo, w_out, p["g_post_mix"], tokens)
    grad_w_out = _matmul_tn(merged, dmo, _BF16, "grad_w_out")

    dq, dk, dv, dsink, grads["g_attn_out"] = _attn_bwd(q, k, v, o, dn_attn, sinks, p["g_attn_out"])
    grads["attn_sinks"] = dsink

    gy, dz, dy, dud, dx_re, dx_im, grads["g_ssm_out"], grads["b_glu"], dd = _ssm_out_bwd(
        _to_chunked(dn_ssm), y, z, u_c, ct_re, ct_im, d_row, w_glu, p["g_ssm_out"])
    grads.update(w_out=grad_w_out, w_glu=_matmul_tn(dz, gy, _BF16, "grad_w_glu"))
    lam_re, lam_im, da_re, da_im = _scan_bwd(dx_re.reshape(T, SCAN_CHUNKS, N_STATE), dx_im.reshape(T, SCAN_CHUNKS, N_STATE),
                                             x_re, x_im, a_re, a_im, [grads["w_out"], grads["w_glu"]])
    lam_re = lam_re.reshape(L, N_STATE)
    lam_im = lam_im.reshape(L, N_STATE)
    dct_re, dct_im, dbt_re, dbt_im = _ssm_weight_grads(
        dy, x_re.reshape(L, N_STATE), x_im.reshape(L, N_STATE), lam_re, lam_im, u_c)
    ssm_pack = _ssm_param_bwd(
        da_re, da_im, dbt_re, dbt_im, dct_re, dct_im,
        ssm["ssm_lambda_re"], ssm["ssm_lambda_im"], ssm["ssm_log_dt"], ssm["ssm_b_re"], ssm["ssm_b_im"],
        dd.reshape(SSM_GROUPS, SSM_GROUP))
    grads.update(ssm_pack=ssm_pack, loss=loss)
    publish(grads)

    du = _from_chunked(_ssm_du(lam_re, lam_im, bt_re, bt_im, dud))
    dproj, grad_x, g_pre_mix = _in_proj_bwd(du, dq, dk, dv, cos_t, sin_t, x, dh1, p["g_pre_mix"], w_in, [ssm_pack])
    publish({"g_pre_mix": g_pre_mix, "w_in": _matmul_tn(dproj, hn, _BF16, "grad_w_in")})
    return grad_x


_MESH = pl.DeviceIdType.MESH
_PEERS = N_DEV - 1


def _mesh_pos():
    return lax.axis_index("x"), lax.axis_index("y"), lax.axis_index("c")


def _dev_index(px, py, pc):
    return 4 * px + 2 * py + pc


def _all_gather(shards, out_dtype, name):
    n = len(shards)

    def body(*refs):
        ins, outs, stages = refs[:n], refs[n:2 * n], refs[2 * n:3 * n]
        send_sems, recv_sems, local_sems = refs[3 * n:]
        x, y, c = _mesh_pos()
        me, sibling = (x, y, c), (x, y, 1 - c)
        chips = [(1 - x, y), (x, 1 - y), (1 - x, 1 - y)]

        def copy(w, k, block, to, src=None):
            slot = outs[w].at[_dev_index(*block)]
            return pltpu.make_async_remote_copy(
                src_ref=slot if src is None else src, dst_ref=slot,
                send_sem=send_sems.at[_PEERS * w + k], recv_sem=recv_sems.at[_PEERS * w + k],
                device_id=to, device_id_type=_MESH)

        for w in range(n):
            stages[w][...] = ins[w][...].astype(out_dtype)
        mine, first, passed = [], [], []
        for w in range(n):
            cp = pltpu.make_async_copy(stages[w], outs[w].at[_dev_index(*me)], local_sems.at[w])
            cp.start()
            mine.append(cp)
            sends = [copy(w, 0, me, sibling, src=stages[w])]
            sends += [copy(w, 1 + j, me, (*chip, c), src=stages[w]) for j, chip in enumerate(chips)]
            for cp in sends:
                cp.start()
            first += sends
        for w in range(n):
            for j, chip in enumerate(chips):
                copy(w, 1 + j, (*chip, c), me).wait_recv()
                cp = copy(w, 4 + j, (*chip, c), sibling)
                cp.start()
                passed.append(cp)
        for w in range(n):
            copy(w, 0, sibling, me).wait_recv()
            for j, chip in enumerate(chips):
                copy(w, 4 + j, (*chip, 1 - c), me).wait_recv()
        for cp in first + passed:
            cp.wait_send()
        for cp in mine:
            cp.wait()

    return pl.pallas_call(
        body, name=name,
        out_shape=[_sds((N_DEV,) + s.shape, out_dtype) for s in shards],
        in_specs=[pl.BlockSpec(memory_space=pltpu.VMEM)] * n,
        out_specs=[pl.BlockSpec(memory_space=pl.ANY)] * n,
        scratch_shapes=[pltpu.VMEM(s.shape, out_dtype) for s in shards]
        + [pltpu.SemaphoreType.DMA((_PEERS * n,)), pltpu.SemaphoreType.DMA((_PEERS * n,)),
           pltpu.SemaphoreType.DMA((n,))],
        compiler_params=pltpu.CompilerParams(vmem_limit_bytes=VMEM_LIMIT),
    )(*shards)


_HBM_SPEC = pl.BlockSpec(memory_space=pltpu.HBM)
_SEM_SPEC = pl.BlockSpec(memory_space=pltpu.SEMAPHORE)
_DATAFLOW = pltpu.SideEffectType.DATAFLOW_SIDE_EFFECTING


def _peer(x, y, c, r):
    return (x ^ ((r >> 2) & 1), y ^ ((r >> 1) & 1), c ^ (r & 1))


def _hbm(a):
    return pltpu.with_memory_space_constraint(a, pltpu.HBM)


def _send_start(sources, blocked, name):
    n = len(sources)
    lands = [lax.empty((N_DEV,) + (s.shape[1:] if blocked else s.shape), s.dtype) for s in sources]

    def body(*refs):
        srcs, zones = refs[:n], refs[n:2 * n]
        send_sems, recv_sems = refs[2 * n:3 * n], refs[3 * n:4 * n]
        token, local_sems = refs[6 * n], refs[6 * n + 1]
        x, y, c = _mesh_pos()
        me = _dev_index(x, y, c)
        local = []
        for w in range(n):
            cp = pltpu.make_async_copy(srcs[w].at[me] if blocked else srcs[w], zones[w].at[me], local_sems.at[w])
            cp.start()
            local.append(cp)
            for r in range(1, N_DEV):
                peer = _peer(x, y, c, r)
                pltpu.make_async_remote_copy(
                    src_ref=srcs[w].at[_dev_index(*peer)] if blocked else srcs[w], dst_ref=zones[w].at[me],
                    send_sem=send_sems[w].at[r - 1], recv_sem=recv_sems[w].at[r - 1],
                    device_id=peer, device_id_type=_MESH).start()
        for cp in local:
            cp.wait()
        token[...] = jnp.zeros_like(token)

    sems = [pltpu.SemaphoreType.DMA((_PEERS,))] * (2 * n)
    out = pl.pallas_call(
        body, name=name,
        out_shape=sems + [pltpu.HBM(a.shape, a.dtype) for a in list(sources) + lands] + [_sds((8, 128), _F32)],
        in_specs=[_HBM_SPEC] * (2 * n),
        out_specs=[_SEM_SPEC] * (2 * n) + [_HBM_SPEC] * (2 * n) + [pl.BlockSpec(memory_space=pltpu.VMEM)],
        input_output_aliases={i: 2 * n + i for i in range(2 * n)},
        scratch_shapes=[pltpu.SemaphoreType.DMA((n,))],
        compiler_params=pltpu.CompilerParams(has_side_effects=_DATAFLOW),
    )(*[_hbm(a) for a in sources], *[_hbm(a) for a in lands])
    return out[:n], out[n:2 * n], out[2 * n:3 * n], out[3 * n:4 * n], out[4 * n]


def _send_wait(send_sems, recv_sems, sources, lands, after, blocked, name):
    n = len(sources)

    def body(*refs):
        srcs, zones = refs[:n], refs[n:2 * n]
        sends, recvs = refs[2 * n:3 * n], refs[3 * n:4 * n]
        x, y, c = _mesh_pos()
        for w in range(n):
            for r in range(1, N_DEV):
                peer = _peer(x, y, c, r)
                idx = _dev_index(*peer)
                cp = pltpu.make_async_remote_copy(
                    src_ref=srcs[w].at[idx] if blocked else srcs[w], dst_ref=zones[w].at[idx],
                    send_sem=sends[w].at[r - 1], recv_sem=recvs[w].at[r - 1],
                    device_id=peer, device_id_type=_MESH)
                cp.wait_send()
                cp.wait_recv()

    out = pl.pallas_call(
        body, name=name,
        out_shape=[pltpu.HBM(a.shape, a.dtype) for a in list(sources) + list(lands)],
        in_specs=[_HBM_SPEC] * (2 * n) + [_SEM_SPEC] * (2 * n) + [pl.BlockSpec(memory_space=pl.ANY)],
        out_specs=[_HBM_SPEC] * (2 * n),
        input_output_aliases={i: i for i in range(2 * n)},
        compiler_params=pltpu.CompilerParams(has_side_effects=_DATAFLOW),
    )(*sources, *lands, *send_sems, *recv_sems, after)
    return out[n:]


def _sequencer_exchange(sources, blocked, name, collective_id):
    n = len(sources)
    flags = blocked

    def body(*refs):
        srcs, zones = refs[:n], refs[n:2 * n]
        send_sems, recv_sems, local_sems = refs[2 * n:]
        x, y, c = _mesh_pos()
        me = _dev_index(x, y, c)
        barrier = pltpu.get_barrier_semaphore()
        for r in range(1, N_DEV):
            pl.semaphore_signal(barrier, inc=1, device_id=_peer(x, y, c, r), device_id_type=_MESH)
        pl.semaphore_wait(barrier, _PEERS)
        local, sends, recvs = [], [], []
        for w in range(n):
            cp = pltpu.make_async_copy(srcs[w].at[me] if flags[w] else srcs[w], zones[w].at[me], local_sems.at[w])
            cp.start()
            local.append(cp)
            for r in range(1, N_DEV):
                peer = _peer(x, y, c, r)
                idx = _dev_index(*peer)
                k = _PEERS * w + r - 1
                src = srcs[w].at[idx] if flags[w] else srcs[w]
                send = pltpu.make_async_remote_copy(
                    src_ref=src, dst_ref=zones[w].at[me], send_sem=send_sems.at[k], recv_sem=recv_sems.at[k],
                    device_id=peer, device_id_type=_MESH)
                send.start()
                sends.append(send)
                recvs.append(pltpu.make_async_remote_copy(
                    src_ref=src, dst_ref=zones[w].at[idx], send_sem=send_sems.at[k], recv_sem=recv_sems.at[k],
                    device_id=peer, device_id_type=_MESH))
        for cp in recvs:
            cp.wait_recv()
        for cp in sends:
            cp.wait_send()
        for cp in local:
            cp.wait()

    return pl.kernel(
        body, name=name,
        out_type=[_sds((N_DEV,) + (s.shape[1:] if f else s.shape), s.dtype) for s, f in zip(sources, flags)],
        mesh=plsc.ScalarSubcoreMesh(axis_name="sequencer", num_cores=1),
        scratch_types=[pltpu.SemaphoreType.DMA((_PEERS * n,)), pltpu.SemaphoreType.DMA((_PEERS * n,)),
                       pltpu.SemaphoreType.DMA((n,))],
        compiler_params=pltpu.CompilerParams(collective_id=collective_id),
    )(*sources)


def _sequencer_gather(shards, name, collective_id):
    n = len(shards)
    fan = 4

    def body(*refs):
        srcs, zones = refs[:n], refs[n:2 * n]
        send_sems, recv_sems, local_sems = refs[2 * n:]
        x, y, c = _mesh_pos()
        me, sibling = (x, y, c), (x, y, 1 - c)
        chips = [(1 - x, y), (x, 1 - y), (1 - x, 1 - y)]
        barrier = pltpu.get_barrier_semaphore()
        for peer in [sibling] + [(*chip, c) for chip in chips]:
            pl.semaphore_signal(barrier, inc=1, device_id=peer, device_id_type=_MESH)
        pl.semaphore_wait(barrier, fan)

        def copy(w, k, block, to, src=None):
            slot = zones[w].at[_dev_index(*block)]
            return pltpu.make_async_remote_copy(
                src_ref=slot if src is None else src, dst_ref=slot,
                send_sem=send_sems.at[_PEERS * w + k], recv_sem=recv_sems.at[_PEERS * w + k],
                device_id=to, device_id_type=_MESH)

        mine, first, passed = [], [], []
        for w in range(n):
            cp = pltpu.make_async_copy(srcs[w], zones[w].at[_dev_index(*me)], local_sems.at[w])
            cp.start()
            mine.append(cp)
            sends = [copy(w, 0, me, sibling, src=srcs[w])]
            sends += [copy(w, 1 + j, me, (*chip, c), src=srcs[w]) for j, chip in enumerate(chips)]
            for cp in sends:
                cp.start()
            first += sends
        for w in range(n):
            for j, chip in enumerate(chips):
                copy(w, 1 + j, (*chip, c), me).wait_recv()
                cp = copy(w, fan + j, (*chip, c), sibling)
                cp.start()
                passed.append(cp)
        for w in range(n):
            copy(w, 0, sibling, me).wait_recv()
            for j, chip in enumerate(chips):
                copy(w, fan + j, (*chip, 1 - c), me).wait_recv()
        for cp in first + passed:
            cp.wait_send()
        for cp in mine:
            cp.wait()

    return pl.kernel(
        body, name=name, out_type=[_sds((N_DEV,) + s.shape, s.dtype) for s in shards],
        mesh=plsc.ScalarSubcoreMesh(axis_name="sequencer", num_cores=1),
        scratch_types=[pltpu.SemaphoreType.DMA((_PEERS * n,)), pltpu.SemaphoreType.DMA((_PEERS * n,)),
                       pltpu.SemaphoreType.DMA((n,))],
        compiler_params=pltpu.CompilerParams(collective_id=collective_id),
    )(*shards)


def _row_tile(rows):
    return next(t for t in range(min(rows, 256), 0, -16) if rows % t == 0)


def _sum_parts(parts, name, tokens=()):
    _, rows, cols = parts.shape
    tr = _row_tile(rows)

    def body(p_ref, g_ref):
        g = p_ref[0].astype(_F32)
        for s in range(1, N_DEV):
            g = g + p_ref[s].astype(_F32)
        g_ref[...] = g

    return _call(body, (rows // tr,), [pl.BlockSpec((N_DEV, tr, cols), lambda i: (0, i, 0))],
                 _rows(tr, cols), _sds((rows, cols), _F32), name, tokens=tokens)(parts)


def _adam_update(g, w, m, v):
    new_m = ADAM_B1 * m + (1.0 - ADAM_B1) * g
    new_v = ADAM_B2 * v + (1.0 - ADAM_B2) * (g * g)
    m_hat = new_m / (1.0 - ADAM_B1 ** ADAM_STEP)
    v_hat = new_v / (1.0 - ADAM_B2 ** ADAM_STEP)
    return -ADAM_LR * (m_hat / (jnp.sqrt(v_hat) + ADAM_EPS) + ADAM_WD * w), new_m, new_v


def _adamw_small(parts, items, sums, name, tokens=()):
    n_p, n_i = len(parts), len(items)

    def body(*refs):
        p_refs, state, outs = refs[:n_p], refs[n_p:n_p + 3 * n_i], refs[n_p + 3 * n_i:]

        def total(part, rows, cols):
            g = p_refs[part][0, rows, cols]
            for s in range(1, N_DEV):
                g = g + p_refs[part][s, rows, cols]
            return g

        for i, (part, rows, cols, _, _, _) in enumerate(items):
            g = total(part, rows, cols)
            w_ref, m_ref, v_ref = state[3 * i:3 * i + 3]
            delta, new_m, new_v = _adam_update(g, w_ref[...], m_ref[...], v_ref[...])
            outs[4 * i][...] = g
            outs[4 * i + 1][...] = delta
            outs[4 * i + 2][...] = new_m
            outs[4 * i + 3][...] = new_v
        for j, (part, rows, cols) in enumerate(sums):
            outs[4 * n_i + j][...] = total(part, rows, cols)

    ins = list(parts) + [a for item in items for a in item[3:]]
    out_shapes = [item[3].shape for item in items for _ in range(4)]
    out_shapes += [(rows.stop - rows.start, cols.stop - cols.start) for _, rows, cols in sums]
    out = _call(body, (1,), [_whole(a.shape) for a in ins], [_whole(s) for s in out_shapes],
                [_sds(s, _F32) for s in out_shapes], name, tokens=tokens)(*ins)
    return [out[4 * i:4 * i + 4] for i in range(n_i)], out[4 * n_i:]


def _adamw(parts, w, m, v, name, tokens=()):
    rows, cols = w.shape
    tr = _row_tile(rows)
    n_parts = parts.shape[0]

    def body(p_ref, w_ref, m_ref, v_ref, g_ref, d_ref, nm_ref, nv_ref):
        g = p_ref[0].astype(_F32)
        for s in range(1, n_parts):
            g = g + p_ref[s].astype(_F32)
        new_m = ADAM_B1 * m_ref[...] + (1.0 - ADAM_B1) * g
        new_v = ADAM_B2 * v_ref[...] + (1.0 - ADAM_B2) * (g * g)
        m_hat = new_m / (1.0 - ADAM_B1 ** ADAM_STEP)
        v_hat = new_v / (1.0 - ADAM_B2 ** ADAM_STEP)
        g_ref[...] = g
        d_ref[...] = -ADAM_LR * (m_hat / (jnp.sqrt(v_hat) + ADAM_EPS) + ADAM_WD * w_ref[...])
        nm_ref[...] = new_m
        nv_ref[...] = new_v

    blk = _rows(tr, cols)
    return _call(body, (rows // tr,),
                 [pl.BlockSpec((n_parts, tr, cols), lambda i: (0, i, 0)), blk, blk, blk],
                 [blk] * 4, [_sds((rows, cols), _F32)] * 4, name, tokens=tokens)(parts, w, m, v)


_SMALL = ("g_pre_mix", "ssm_lambda_re", "ssm_lambda_im", "ssm_log_dt", "ssm_b_re", "ssm_b_im",
          "ssm_c_re", "ssm_c_im", "ssm_d", "b_glu", "attn_sinks", "g_ssm_out", "g_attn_out",
          "g_post_mix", "g_pre_ffn", "g_post_ffn")
_BIG = ("w_in", "w_glu", "w_out", "w_gate_up", "w_down")
_WEIGHTS = ("g_pre_mix", "w_in", "ssm_lambda_re", "ssm_lambda_im", "ssm_log_dt", "ssm_b_re", "ssm_b_im",
            "ssm_c_re", "ssm_c_im", "ssm_d", "w_glu", "b_glu", "attn_sinks", "g_ssm_out", "g_attn_out",
            "w_out", "g_post_mix", "g_pre_ffn", "w_gate_up", "w_down", "g_post_ffn")
_LANES = 128


_SHAPE_2D = {
    "g_pre_mix": (1, D_MODEL), "ssm_lambda_re": (SSM_GROUPS, SSM_STATE), "ssm_lambda_im": (SSM_GROUPS, SSM_STATE),
    "ssm_log_dt": (1, SSM_GROUPS), "ssm_b_re": (SSM_WIDTH, SSM_STATE), "ssm_b_im": (SSM_WIDTH, SSM_STATE),
    "ssm_c_re": (SSM_WIDTH, SSM_STATE), "ssm_c_im": (SSM_WIDTH, SSM_STATE), "ssm_d": (SSM_GROUPS, SSM_GROUP),
    "b_glu": (1, 2 * SSM_WIDTH), "attn_sinks": (1, N_Q_HEADS), "g_ssm_out": (1, SSM_WIDTH),
    "g_attn_out": (1, ATTN_WIDTH), "g_post_mix": (1, D_MODEL), "g_pre_ffn": (1, D_MODEL), "g_post_ffn": (1, D_MODEL)}
_ROW_WIDTH = {"g_pre_mix": D_MODEL, "b_glu": 2 * SSM_WIDTH, "attn_sinks": _LANES, "g_ssm_out": SSM_WIDTH,
              "g_attn_out": ATTN_WIDTH, "g_post_mix": D_MODEL, "g_pre_ffn": D_MODEL, "g_post_ffn": D_MODEL,
              "loss": _LANES}
_DENSE = ()
_PER_GROUP_TRANSPOSED = ("ssm_b_re", "ssm_b_im")


def _to_2d(name, a):
    if name in _PER_GROUP_TRANSPOSED:
        a = a.reshape(SSM_GROUPS, SSM_STATE, SSM_GROUP).transpose(0, 2, 1)
    return a.reshape(_SHAPE_2D[name])


def _from_2d(name, a, shape):
    if name in _PER_GROUP_TRANSPOSED:
        a = a.reshape(SSM_GROUPS, SSM_GROUP, SSM_STATE).transpose(0, 2, 1)
    return a.reshape(shape)


def _row_slots(names):
    slots, row, col = {}, 0, 0
    for n in names:
        width = _ROW_WIDTH[n]
        if col + width > D_MODEL:
            row, col = row + 1, 0
        slots[n] = (row, col, width)
        col += width
    return slots


def _stack_rows(named, slots):
    n_rows = -(-(max(r for r, _, _ in slots.values()) + 1) // 8) * 8
    lines = []
    for r in range(n_rows):
        pieces = [named[n] for n, (row, _, _) in slots.items() if row == r]
        used = sum(p.shape[1] for p in pieces)
        if used < D_MODEL:
            pieces.append(jnp.zeros((1, D_MODEL - used), _F32))
        lines.append(jnp.concatenate(pieces, axis=1) if len(pieces) > 1 else pieces[0])
    return jnp.concatenate(lines, axis=0)


def kernel(x, positions, g_pre_mix, w_in, ssm_lambda_re, ssm_lambda_im, ssm_log_dt, ssm_b_re, ssm_b_im, ssm_c_re, ssm_c_im, ssm_d, w_glu, b_glu, attn_sinks, g_ssm_out, g_attn_out, w_out, g_post_mix, g_pre_ffn, w_gate_up, w_down, g_post_ffn, loss_target, m_g_pre_mix, m_w_in, m_ssm_lambda_re, m_ssm_lambda_im, m_ssm_log_dt, m_ssm_b_re, m_ssm_b_im, m_ssm_c_re, m_ssm_c_im, m_ssm_d, m_w_glu, m_b_glu, m_attn_sinks, m_g_ssm_out, m_g_attn_out, m_w_out, m_g_post_mix, m_g_pre_ffn, m_w_gate_up, m_w_down, m_g_post_ffn, v_g_pre_mix, v_w_in, v_ssm_lambda_re, v_ssm_lambda_im, v_ssm_log_dt, v_ssm_b_re, v_ssm_b_im, v_ssm_c_re, v_ssm_c_im, v_ssm_d, v_w_glu, v_b_glu, v_attn_sinks, v_g_ssm_out, v_g_attn_out, v_w_out, v_g_post_mix, v_g_pre_ffn, v_w_gate_up, v_w_down, v_g_post_ffn):
    w = dict(g_pre_mix=g_pre_mix, w_in=w_in, ssm_lambda_re=ssm_lambda_re, ssm_lambda_im=ssm_lambda_im,
             ssm_log_dt=ssm_log_dt, ssm_b_re=ssm_b_re, ssm_b_im=ssm_b_im, ssm_c_re=ssm_c_re, ssm_c_im=ssm_c_im,
             ssm_d=ssm_d, w_glu=w_glu, b_glu=b_glu, attn_sinks=attn_sinks, g_ssm_out=g_ssm_out,
             g_attn_out=g_attn_out, w_out=w_out, g_post_mix=g_post_mix, g_pre_ffn=g_pre_ffn,
             w_gate_up=w_gate_up, w_down=w_down, g_post_ffn=g_post_ffn)
    m = dict(g_pre_mix=m_g_pre_mix, w_in=m_w_in, ssm_lambda_re=m_ssm_lambda_re, ssm_lambda_im=m_ssm_lambda_im,
             ssm_log_dt=m_ssm_log_dt, ssm_b_re=m_ssm_b_re, ssm_b_im=m_ssm_b_im, ssm_c_re=m_ssm_c_re,
             ssm_c_im=m_ssm_c_im, ssm_d=m_ssm_d, w_glu=m_w_glu, b_glu=m_b_glu, attn_sinks=m_attn_sinks,
             g_ssm_out=m_g_ssm_out, g_attn_out=m_g_attn_out, w_out=m_w_out, g_post_mix=m_g_post_mix,
             g_pre_ffn=m_g_pre_ffn, w_gate_up=m_w_gate_up, w_down=m_w_down, g_post_ffn=m_g_post_ffn)
    v = dict(g_pre_mix=v_g_pre_mix, w_in=v_w_in, ssm_lambda_re=v_ssm_lambda_re, ssm_lambda_im=v_ssm_lambda_im,
             ssm_log_dt=v_ssm_log_dt, ssm_b_re=v_ssm_b_re, ssm_b_im=v_ssm_b_im, ssm_c_re=v_ssm_c_re,
             ssm_c_im=v_ssm_c_im, ssm_d=v_ssm_d, w_glu=v_w_glu, b_glu=v_b_glu, attn_sinks=v_attn_sinks,
             g_ssm_out=v_g_ssm_out, g_attn_out=v_g_attn_out, w_out=v_w_out, g_post_mix=v_g_post_mix,
             g_pre_ffn=v_g_pre_ffn, w_gate_up=v_w_gate_up, w_down=v_w_down, g_post_ffn=v_g_post_ffn)

    transposed = ("w_in", "w_glu", "w_gate_up")
    native_transposed = ("w_in", "w_gate_up")
    shard = {n: (w[n][0].T if n in transposed else w[n][0]).astype(_BF16) for n in _BIG}
    gathered = {}
    for names, lands in (
            (("w_in",), _sequencer_exchange([shard["w_in"]], [False], "gather_w_in", 1)),
            (("w_glu", "w_out"), _sequencer_exchange([shard["w_glu"], shard["w_out"]], [False] * 2, "gather_mix", 2)),
            (("w_gate_up", "w_down"), _sequencer_gather([shard["w_gate_up"], shard["w_down"]], "gather_ffn", 3))):
        gathered.update({n: a.reshape(-1, a.shape[2]) for n, a in zip(names, lands)})

    def fetch(names, after):
        del after
        return [gathered[n] for n in names]

    sent = []

    def publish(named):
        big = [n for n in named if n in _BIG]
        rows = [n for n in named if n in _ROW_WIDTH]
        dense = [n for n in named if n in _DENSE]
        plain = [n for n in named if n not in big + rows + dense]
        sources = [named[n].reshape(N_DEV, -1, named[n].shape[1]) for n in big]
        slots = _row_slots(rows)
        if rows:
            sources.append(_stack_rows(named, slots))
        sources += [named[n].reshape(-1, _LANES) for n in dense] + [named[n] for n in plain]
        flags = [True] * len(big) + [False] * (len(sources) - len(big))
        cid = 4 + len(sent)
        sent.append((big, slots, dense, plain, _sequencer_exchange(sources, flags, "grads_%d" % cid, cid)))
        return [named[n] for n in big]

    p = {n: w[n] for n in _SMALL}
    grad_x = _local_step(x[0], positions[0], loss_target[0], p, fetch, publish)

    state = {n: [_to_2d(n, a) for a in (w[n], m[n], v[n])] for n in _SMALL}
    result = {}
    total_loss = None
    chain = []
    for big, slots, dense, plain, lands in sent:
        lands = list(lands)
        after = list(chain)
        for name in big:
            part = lands.pop(0)
            if name in native_transposed:
                updated = _adamw(part, w[name][0].T, m[name][0].T, v[name][0].T, "adamw_" + name, after)
                result[name] = [a.T[None] for a in updated]
                chain = [updated[3]]
                continue
            if name in transposed:
                part = _sum_parts(part, "sum_" + name, after).T[None]
            updated = _adamw(part, w[name][0], m[name][0], v[name][0], "adamw_" + name, after)
            result[name] = [a[None] for a in updated]
            chain = [updated[3]]
        parts, items, sums, names = [], [], [], []
        if slots:
            parts.append(lands.pop(0))
            for name, (row, col, _) in slots.items():
                if name == "loss":
                    sums.append((0, slice(row, row + 1), slice(col, col + _LANES)))
                else:
                    items.append((0, slice(row, row + 1), slice(col, col + _SHAPE_2D[name][1]), *state[name]))
                    names.append(name)
        for name in dense:
            part = lands.pop(0).reshape((N_DEV,) + _SHAPE_2D[name])
            result[name] = _adamw(part, *state[name], "adamw_" + name, after)
            chain = [result[name][3]]
        for name in plain:
            packed = _SSM_PACK if name == "ssm_pack" else {name: (0,) + _SHAPE_2D[name]}
            for member, (first, rows_n, cols_n) in packed.items():
                items.append((len(parts), slice(first, first + rows_n), slice(0, cols_n), *state[member]))
                names.append(member)
            parts.append(lands.pop(0))
        if items:
            updated, summed = _adamw_small(parts, items, sums, "adamw_small_" + names[0], after)
            chain = [updated[0][3]]
            result.update(dict(zip(names, updated)))
            if summed:
                total_loss = summed[0][0, 0]

    out = [total_loss, grad_x[None]]
    for kind in range(4):
        out += [_from_2d(n, result[n][kind], w[n].shape) for n in _WEIGHTS]
    return tuple(out)
```

```python
import functools
import math

import numpy as np
import jax
import jax.numpy as jnp
from jax import lax
from jax.experimental import pallas as pl
from jax.experimental.pallas import tpu as pltpu
from jax.experimental.pallas import tpu_sc as plsc

D_MODEL = 1024
SSM_WIDTH = 512
SSM_GROUP = 16
SSM_GROUPS = 32
SSM_STATE = 64
N_STATE = SSM_GROUPS * SSM_STATE
ATTN_WIDTH = 512
HEAD_DIM = 64
N_Q_HEADS = 8
N_KV_HEADS = 2
Q_PER_KV = 4
KV_WIDTH = 128
IN_WIDTH = 1280
BLOCK = 128
ROPE_DIM = 16
ROPE_THETA = 500000.0
D_FF = 2816
NORM_EPS = 1e-6
MASK_VALUE = -1e30
ADAM_LR = 0.001
ADAM_B1 = 0.9
ADAM_B2 = 0.999
ADAM_EPS = 1e-08
ADAM_WD = 0.01
ADAM_STEP = 10

N_DEV = 8
SCAN_CHUNKS = 8
SCAN_COLS = 512
TOKEN_TILE = 256
VMEM_LIMIT = 56 * 1024 * 1024

_F32 = jnp.float32
_BF16 = jnp.bfloat16
_MXU = jnp.bfloat16

_NN = ((1,), (0,))
_NT = ((1,), (1,))
_TN = ((0,), (0,))


def _dot(a, b, dims):
    return lax.dot_general(a.astype(_MXU), b.astype(_MXU), (dims, ((), ())),
                           preferred_element_type=_F32)


def _dot_exact(a, b, dims):
    return lax.dot_general(a.astype(_F32), b.astype(_F32), (dims, ((), ())),
                           precision=lax.Precision.HIGHEST, preferred_element_type=_F32)


def _iota(shape, dim):
    return lax.broadcasted_iota(jnp.int32, shape, dim)


def _rms_fwd(x, g):
    r = lax.rsqrt(jnp.mean(x * x, axis=-1, keepdims=True) + NORM_EPS)
    return x * r * g, r


def _rms_bwd(dy, x, g, r):
    a = dy * g
    xn = x * r
    dx = r * (a - xn * jnp.mean(a * xn, axis=-1, keepdims=True))
    dg = jnp.sum(dy * xn, axis=0, keepdims=True)
    return dx, dg


def _call(body, grid, in_specs, out_specs, out_shape, name, scratch=(), tokens=()):
    params = pltpu.CompilerParams(dimension_semantics=("arbitrary",) * len(grid),
                                  vmem_limit_bytes=VMEM_LIMIT)
    n_in, n_tok = len(in_specs), len(tokens)

    def run(*refs):
        return body(*refs[:n_in], *refs[n_in + n_tok:])

    call = pl.pallas_call(run, grid=grid,
                          in_specs=list(in_specs) + [pl.BlockSpec(memory_space=pl.ANY)] * n_tok,
                          out_specs=out_specs, out_shape=out_shape, scratch_shapes=list(scratch),
                          compiler_params=params, name=name)
    return lambda *args: call(*args, *tokens)


def _rows(tm, n):
    return pl.BlockSpec((tm, n), lambda i: (i, 0))


def _whole(shape):
    nd = len(shape)
    return pl.BlockSpec(shape, lambda i: (0,) * nd)


def _sds(shape, dtype):
    return jax.ShapeDtypeStruct(shape, dtype)


def _tile(L):
    return min(TOKEN_TILE, L)


def _accumulate(ref, val, first):
    @pl.when(first)
    def _():
        ref[...] = val

    @pl.when(jnp.logical_not(first))
    def _():
        ref[...] += val


def _rope_rows():
    half = ROPE_DIM // 2
    inv = (np.float32(ROPE_THETA) ** (-np.arange(half, dtype=np.float32) * np.float32(2.0) / np.float32(ROPE_DIM))).astype(np.float32)
    col = np.arange(KV_WIDTH) % HEAD_DIM
    freq = np.where(col < ROPE_DIM, inv[col % half], 0.0).astype(np.float32)
    sign = np.where(col < half, -1.0, np.where(col < ROPE_DIM, 1.0, 0.0)).astype(np.float32)
    return freq[None, :], sign[None, :]


def _rope_tables(pos_col):
    L = pos_col.shape[0]
    tm = _tile(L)
    freq, sign = _rope_rows()

    def body(pos_ref, freq_ref, sign_ref, cos_ref, sin_ref):
        ang = pos_ref[...].astype(_F32) * freq_ref[...]
        cos_ref[...] = jnp.cos(ang)
        sin_ref[...] = jnp.sin(ang) * sign_ref[...]

    return _call(body, (L // tm,),
                 [_rows(tm, 1), _whole((1, KV_WIDTH)), _whole((1, KV_WIDTH))],
                 [_rows(tm, KV_WIDTH), _rows(tm, KV_WIDTH)],
                 [_sds((L, KV_WIDTH), _F32)] * 2, "rope_tables")(pos_col, jnp.asarray(freq), jnp.asarray(sign))


def _widen(t, width):
    return t if width == KV_WIDTH else jnp.concatenate([t] * (width // KV_WIDTH), axis=1)


def _rope_partner(t):
    w = t.shape[1]
    in_head = _iota((1, w), 1) & (HEAD_DIM - 1)
    second = jnp.where(in_head < ROPE_DIM, pltpu.roll(t, ROPE_DIM // 2, 1), 0.0)
    return jnp.where(in_head < ROPE_DIM // 2, pltpu.roll(t, w - ROPE_DIM // 2, 1), second)


def _rope_apply(t, cos_t, sin_t):
    w = t.shape[1]
    return t * _widen(cos_t, w) + _rope_partner(t) * _widen(sin_t, w)


def _rope_transpose(dt, cos_t, sin_t):
    w = dt.shape[1]
    return dt * _widen(cos_t, w) + _rope_partner(dt * _widen(sin_t, w))


def _in_proj(x, g_pre_mix, w_in, cos_t, sin_t):
    L = x.shape[0]
    tm = _tile(L)

    def body(x_ref, g_ref, w_ref, cos_ref, sin_ref, hn_ref, u_ref, q_ref, k_ref, v_ref):
        hn, _ = _rms_fwd(x_ref[...], g_ref[...])
        hn = hn.astype(_BF16)
        hn_ref[...] = hn
        proj = _dot(hn, w_ref[...], _NT)
        u_ref[...] = proj[:, :SSM_WIDTH]
        q = proj[:, SSM_WIDTH:SSM_WIDTH + ATTN_WIDTH]
        k = proj[:, SSM_WIDTH + ATTN_WIDTH:SSM_WIDTH + ATTN_WIDTH + KV_WIDTH]
        cos_v, sin_v = cos_ref[...], sin_ref[...]
        q_ref[...] = _rope_apply(q, cos_v, sin_v).astype(_BF16)
        k_ref[...] = _rope_apply(k, cos_v, sin_v).astype(_BF16)
        v_ref[...] = proj[:, SSM_WIDTH + ATTN_WIDTH + KV_WIDTH:].astype(_BF16)

    return _call(body, (L // tm,),
                 [_rows(tm, D_MODEL), _whole((1, D_MODEL)), _whole((IN_WIDTH, D_MODEL)),
                  _rows(tm, KV_WIDTH), _rows(tm, KV_WIDTH)],
                 [_rows(tm, D_MODEL), _rows(tm, SSM_WIDTH), _rows(tm, ATTN_WIDTH),
                  _rows(tm, KV_WIDTH), _rows(tm, KV_WIDTH)],
                 [_sds((L, D_MODEL), _BF16), _sds((L, SSM_WIDTH), _F32), _sds((L, ATTN_WIDTH), _BF16),
                  _sds((L, KV_WIDTH), _BF16), _sds((L, KV_WIDTH), _BF16)],
                 "in_proj")(x, g_pre_mix, w_in, cos_t, sin_t)


def _s5_discretize(lam_re, lam_im, log_dt):
    lr = jnp.minimum(lam_re, -1e-4)
    li = lam_im
    dt = jnp.exp(log_dt)
    mag = jnp.exp(lr * dt)
    ar = mag * jnp.cos(li * dt)
    ai = mag * jnp.sin(li * dt)
    den = lr * lr + li * li
    fr = ((ar - 1.0) * lr + ai * li) / den
    fi = (ai * lr - (ar - 1.0) * li) / den
    return ar, ai, fr, fi


def _s5_bbar(lam_re, lam_im, log_dt, b_re, b_im):
    ar, ai, fr, fi = _s5_discretize(lam_re, lam_im, log_dt)
    return ar, ai, fr * b_re - fi * b_im, fr * b_im + fi * b_re


def _spread_masks():
    e16 = (_iota((SSM_GROUP, SSM_WIDTH), 1) & (SSM_GROUP - 1)) == _iota((SSM_GROUP, SSM_WIDTH), 0)
    e64 = (_iota((SSM_STATE, N_STATE), 1) & (SSM_STATE - 1)) == _iota((SSM_STATE, N_STATE), 0)
    mask_b = (_iota((N_STATE, SSM_WIDTH), 0) >> 6) == (_iota((N_STATE, SSM_WIDTH), 1) >> 4)
    mask_c = (_iota((SSM_WIDTH, N_STATE), 0) >> 4) == (_iota((SSM_WIDTH, N_STATE), 1) >> 6)
    return e16.astype(_F32), e64.astype(_F32), mask_b, mask_c


SUPER = 4
SB_STATE = N_STATE // SUPER
SB_WIDTH = SSM_WIDTH // SUPER


def _sb_state(k):
    return slice(SB_STATE * k, SB_STATE * (k + 1))


def _sb_width(k):
    return slice(SB_WIDTH * k, SB_WIDTH * (k + 1))


def _dt_column(log_dt_row):
    eye = _iota((SSM_GROUPS, SSM_GROUPS), 0) == _iota((SSM_GROUPS, SSM_GROUPS), 1)
    return jnp.sum(jnp.where(eye, log_dt_row, 0.0), axis=1, keepdims=True)


def _group_masks():
    e64 = ((_iota((SSM_STATE, N_STATE), 1) & (SSM_STATE - 1)) == _iota((SSM_STATE, N_STATE), 0)).astype(_F32)
    own = _iota((SSM_GROUPS, N_STATE), 0) == (_iota((SSM_GROUPS, N_STATE), 1) >> 6)
    return e64, own


def _rows_of_group():
    return ((_iota((SSM_WIDTH, SSM_GROUPS), 0) >> 4) == _iota((SSM_WIDTH, SSM_GROUPS), 1)).astype(_F32)


def _ssm_prep(lam_re, lam_im, log_dt, b_re, b_im, c_re, c_im):
    def body(lr_ref, li_ref, ld_ref, bre, bim, cre, cim, ar_ref, ai_ref, btr, bti, ctr, cti):
        ar, ai, fr, fi = _s5_discretize(lr_ref[...], li_ref[...], _dt_column(ld_ref[...]))
        e64, own = _group_masks()
        mask_c = (_iota((SSM_WIDTH, N_STATE), 0) >> 4) == (_iota((SSM_WIDTH, N_STATE), 1) >> 6)

        def to_row(t):
            return jnp.sum(jnp.where(own, _dot_exact(t, e64, _NN), 0.0), axis=0, keepdims=True)

        def fold(m):
            full = jnp.where(mask_c, _dot(m, e64, _NN), 0.0)
            return sum(full[_sb_width(k), :] for k in range(SUPER)).astype(_BF16)

        ar_ref[...] = to_row(ar)
        ai_ref[...] = to_row(ai)
        spread = _rows_of_group()
        fr_t = _dot_exact(spread, fr, _NN)
        fi_t = _dot_exact(spread, fi, _NN)
        btr[...] = fold(fr_t * bre[...] - fi_t * bim[...])
        bti[...] = fold(fr_t * bim[...] + fi_t * bre[...])
        ctr[...] = fold(cre[...])
        cti[...] = fold(cim[...])

    row = (1, N_STATE)
    ins = [lam_re, lam_im, log_dt, b_re, b_im, c_re, c_im]
    return _call(body, (1,), [_whole(a.shape) for a in ins],
                 [_whole(row), _whole(row)] + [_whole((SB_WIDTH, N_STATE))] * 4,
                 [_sds(row, _F32), _sds(row, _F32)] + [_sds((SB_WIDTH, N_STATE), _BF16)] * 4,
                 "ssm_prep")(*ins)


def _ssm_bu(u, bt_re, bt_im):
    L = u.shape[0]
    tm = _tile(L)

    def body(u_ref, br_ref, bi_ref, or_ref, oi_ref):
        for k in range(SUPER):
            ub = u_ref[:, _sb_width(k)].astype(_BF16)
            or_ref[:, _sb_state(k)] = _dot(ub, br_ref[:, _sb_state(k)], _NN)
            oi_ref[:, _sb_state(k)] = _dot(ub, bi_ref[:, _sb_state(k)], _NN)

    return _call(body, (L // tm,),
                 [_rows(tm, SSM_WIDTH), _whole((SB_WIDTH, N_STATE)), _whole((SB_WIDTH, N_STATE))],
                 [_rows(tm, N_STATE), _rows(tm, N_STATE)],
                 [_sds((L, N_STATE), _F32)] * 2, "ssm_bu")(u, bt_re, bt_im)


def _complex_power(ar, ai, n):
    def step(_, c):
        pr, pi = c
        return pr * ar - pi * ai, pr * ai + pi * ar
    return lax.fori_loop(0, n, step, (jnp.ones_like(ar), jnp.zeros_like(ai)))


def _chunk_carries(er, ei, pr, pi, reverse):
    rows = _iota(er.shape, 0)
    sr = jnp.zeros_like(pr)
    si = jnp.zeros_like(pi)
    out_r = jnp.zeros_like(er)
    out_i = jnp.zeros_like(ei)
    order = range(SCAN_CHUNKS - 1, 0, -1) if reverse else range(SCAN_CHUNKS - 1)
    for c in order:
        e_r = er[c:c + 1, :]
        e_i = ei[c:c + 1, :]
        sr, si = pr * sr - pi * si + e_r, pr * si + pi * sr + e_i
        nxt = c - 1 if reverse else c + 1
        out_r = jnp.where(rows == nxt, sr, out_r)
        out_i = jnp.where(rows == nxt, si, out_i)
    return out_r, out_i


def _scan_fwd(b_re, b_im, a_re, a_im):
    T = b_re.shape[0]
    W = SCAN_COLS
    blk = pl.BlockSpec((T, SCAN_CHUNKS, W), lambda j: (0, 0, j))
    vec = pl.BlockSpec((1, W), lambda j: (0, j))

    def body(br_ref, bi_ref, ar_ref, ai_ref, xr_ref, xi_ref):
        ar, ai = ar_ref[...], ai_ref[...]
        ar8 = jnp.broadcast_to(ar, (SCAN_CHUNKS, W))
        ai8 = jnp.broadcast_to(ai, (SCAN_CHUNKS, W))

        def local(t, c):
            cr, ci = c
            return ar8 * cr - ai8 * ci + br_ref[t], ar8 * ci + ai8 * cr + bi_ref[t]

        zero = jnp.zeros((SCAN_CHUNKS, W), _F32)
        er, ei = lax.fori_loop(0, T, local, (zero, zero))
        pr, pi = _complex_power(ar, ai, T)
        sr, si = _chunk_carries(er, ei, pr, pi, reverse=False)

        def final(t, c):
            nr, ni = local(t, c)
            xr_ref[t] = nr
            xi_ref[t] = ni
            return nr, ni

        lax.fori_loop(0, T, final, (sr, si))

    shape = _sds(b_re.shape, _F32)
    return _call(body, (N_STATE // W,), [blk, blk, vec, vec], [blk, blk], [shape, shape],
                 "scan_fwd")(b_re, b_im, a_re, a_im)


def _scan_bwd(dx_re, dx_im, x_re, x_im, a_re, a_im, tokens=()):
    T = dx_re.shape[0]
    W = SCAN_COLS
    blk = pl.BlockSpec((T, SCAN_CHUNKS, W), lambda j: (0, 0, j))
    vec = pl.BlockSpec((1, W), lambda j: (0, j))

    def body(dr_ref, di_ref, xr_ref, xi_ref, ar_ref, ai_ref, lr_ref, li_ref, dar_ref, dai_ref):
        ar, ai = ar_ref[...], ai_ref[...]
        ar8 = jnp.broadcast_to(ar, (SCAN_CHUNKS, W))
        ai8 = jnp.broadcast_to(ai, (SCAN_CHUNKS, W))

        def local(t, c):
            cr, ci = c
            return ar8 * cr + ai8 * ci + dr_ref[t], ar8 * ci - ai8 * cr + di_ref[t]

        zero = jnp.zeros((SCAN_CHUNKS, W), _F32)
        er, ei = lax.fori_loop(0, T, lambda k, c: local(T - 1 - k, c), (zero, zero))
        pr, pi = _complex_power(ar, -ai, T)
        sr, si = _chunk_carries(er, ei, pr, pi, reverse=True)

        def grad_a(acc, nr, ni, xpr, xpi):
            return acc[0] + nr * xpr + ni * xpi, acc[1] + ni * xpr - nr * xpi

        def final(k, c):
            t = T - 1 - k
            nr, ni = local(t, c[:2])
            lr_ref[t] = nr
            li_ref[t] = ni
            gr, gi = grad_a(c[2:], nr, ni, xr_ref[t - 1], xi_ref[t - 1])
            return nr, ni, gr, gi

        cr, ci, gr, gi = lax.fori_loop(0, T - 1, final, (sr, si, zero, zero))
        nr, ni = local(0, (cr, ci))
        lr_ref[0] = nr
        li_ref[0] = ni
        first = _iota((SCAN_CHUNKS, W), 0) == 0
        xpr = jnp.where(first, 0.0, pltpu.roll(xr_ref[T - 1], 1, 0))
        xpi = jnp.where(first, 0.0, pltpu.roll(xi_ref[T - 1], 1, 0))
        gr, gi = grad_a((gr, gi), nr, ni, xpr, xpi)
        dar_ref[...] = jnp.sum(gr, axis=0, keepdims=True)
        dai_ref[...] = jnp.sum(gi, axis=0, keepdims=True)

    shape = _sds(dx_re.shape, _F32)
    row = _sds((1, N_STATE), _F32)
    return _call(body, (N_STATE // W,), [blk, blk, blk, blk, vec, vec], [blk, blk, vec, vec],
                 [shape, shape, row, row], "scan_bwd", tokens=tokens)(dx_re, dx_im, x_re, x_im, a_re, a_im)


_GELU_K = math.sqrt(2.0 / math.pi)
_GELU_C = 0.044715


def _gelu(y):
    return 0.5 * y * (1.0 + jnp.tanh(_GELU_K * (y + _GELU_C * y * y * y)))


def _gelu_grad(y):
    t = jnp.tanh(_GELU_K * (y + _GELU_C * y * y * y))
    return 0.5 * (1.0 + t) + 0.5 * y * (1.0 - t * t) * _GELU_K * (1.0 + 3.0 * _GELU_C * y * y)


def _ssm_out(x_re, x_im, u, ct_re, ct_im, d_row, w_glu, b_glu, g_ssm):
    L = u.shape[0]
    tm = _tile(L)

    def body(xr_ref, xi_ref, u_ref, cr_ref, ci_ref, d_ref, w_ref, b_ref, g_ref, y_ref, z_ref, n_ref):
        cx = [_dot(xr_ref[:, _sb_state(k)], cr_ref[:, _sb_state(k)], _NT)
              - _dot(xi_ref[:, _sb_state(k)], ci_ref[:, _sb_state(k)], _NT) for k in range(SUPER)]
        y = jnp.concatenate(cx, axis=1) + d_ref[...] * u_ref[...]
        y_ref[...] = y
        z = _dot(_gelu(y), w_ref[...], _NT) + b_ref[...]
        z_ref[...] = z
        out = z[:, :SSM_WIDTH] * jax.nn.sigmoid(z[:, SSM_WIDTH:])
        n, _ = _rms_fwd(out, g_ref[...])
        n_ref[...] = n.astype(_BF16)

    return _call(body, (L // tm,),
                 [_rows(tm, N_STATE), _rows(tm, N_STATE), _rows(tm, SSM_WIDTH),
                  _whole((SB_WIDTH, N_STATE)), _whole((SB_WIDTH, N_STATE)), _whole((1, SSM_WIDTH)),
                  _whole((2 * SSM_WIDTH, SSM_WIDTH)), _whole((1, 2 * SSM_WIDTH)), _whole((1, SSM_WIDTH))],
                 [_rows(tm, SSM_WIDTH), _rows(tm, 2 * SSM_WIDTH), _rows(tm, SSM_WIDTH)],
                 [_sds((L, SSM_WIDTH), _F32), _sds((L, 2 * SSM_WIDTH), _F32), _sds((L, SSM_WIDTH), _BF16)],
                 "ssm_out")(x_re, x_im, u, ct_re, ct_im, d_row, w_glu, b_glu, g_ssm)


def _ssm_out_bwd(dn, y, z, u, ct_re, ct_im, d_row, w_glu, g_ssm):
    L = u.shape[0]
    tm = _tile(L)

    def body(dn_ref, y_ref, z_ref, u_ref, cr_ref, ci_ref, d_ref, w_ref, g_ref,
             gy_ref, dz_ref, dy_ref, dud_ref, dxr_ref, dxi_ref, dg_ref, db_ref, dd_ref):
        first = pl.program_id(0) == 0
        z = z_ref[...]
        z1, z2 = z[:, :SSM_WIDTH], z[:, SSM_WIDTH:]
        sig = jax.nn.sigmoid(z2)
        out = z1 * sig
        g = g_ref[...]
        _, r = _rms_fwd(out, g)
        dout, dg = _rms_bwd(dn_ref[...], out, g, r)
        _accumulate(dg_ref, dg, first)
        dz = jnp.concatenate([dout * sig, dout * z1 * sig * (1.0 - sig)], axis=1)
        _accumulate(db_ref, jnp.sum(dz, axis=0, keepdims=True), first)
        dzb = dz.astype(_BF16)
        dz_ref[...] = dzb
        y = y_ref[...]
        gy_ref[...] = _gelu(y).astype(_BF16)
        dy = _dot(dzb, w_ref[...], _NN) * _gelu_grad(y)
        u = u_ref[...]
        _accumulate(dd_ref, jnp.sum(dy * u, axis=0, keepdims=True), first)
        dud_ref[...] = d_ref[...] * dy
        dyb = dy.astype(_BF16)
        dy_ref[...] = dyb
        for k in range(SUPER):
            dxr_ref[:, _sb_state(k)] = _dot(dyb[:, _sb_width(k)], cr_ref[:, _sb_state(k)], _NN)
            dxi_ref[:, _sb_state(k)] = -_dot(dyb[:, _sb_width(k)], ci_ref[:, _sb_state(k)], _NN)

    row = _whole((1, SSM_WIDTH))
    return _call(body, (L // tm,),
                 [_rows(tm, SSM_WIDTH), _rows(tm, SSM_WIDTH), _rows(tm, 2 * SSM_WIDTH), _rows(tm, SSM_WIDTH),
                  _whole((SB_WIDTH, N_STATE)), _whole((SB_WIDTH, N_STATE)), row,
                  _whole((2 * SSM_WIDTH, SSM_WIDTH)), row],
                 [_rows(tm, SSM_WIDTH), _rows(tm, 2 * SSM_WIDTH), _rows(tm, SSM_WIDTH), _rows(tm, SSM_WIDTH),
                  _rows(tm, N_STATE), _rows(tm, N_STATE), row, _whole((1, 2 * SSM_WIDTH)), row],
                 [_sds((L, SSM_WIDTH), _BF16), _sds((L, 2 * SSM_WIDTH), _BF16), _sds((L, SSM_WIDTH), _BF16),
                  _sds((L, SSM_WIDTH), _F32), _sds((L, N_STATE), _F32), _sds((L, N_STATE), _F32),
                  _sds((1, SSM_WIDTH), _F32), _sds((1, 2 * SSM_WIDTH), _F32), _sds((1, SSM_WIDTH), _F32)],
                 "ssm_out_bwd")(dn, y, z, u, ct_re, ct_im, d_row, w_glu, g_ssm)


def _ssm_du(lam_re, lam_im, bt_re, bt_im, dud):
    L = dud.shape[0]
    tm = _tile(L)

    def body(lr_ref, li_ref, br_ref, bi_ref, dud_ref, du_ref):
        for k in range(SUPER):
            du_ref[:, _sb_width(k)] = (_dot(lr_ref[:, _sb_state(k)], br_ref[:, _sb_state(k)], _NT)
                                       + _dot(li_ref[:, _sb_state(k)], bi_ref[:, _sb_state(k)], _NT)
                                       + dud_ref[:, _sb_width(k)])

    return _call(body, (L // tm,),
                 [_rows(tm, N_STATE), _rows(tm, N_STATE), _whole((SB_WIDTH, N_STATE)),
                  _whole((SB_WIDTH, N_STATE)), _rows(tm, SSM_WIDTH)],
                 _rows(tm, SSM_WIDTH), _sds((L, SSM_WIDTH), _F32), "ssm_du")(lam_re, lam_im, bt_re, bt_im, dud)


def _ssm_weight_grads(dy, x_re, x_im, lam_re, lam_im, u):
    L = u.shape[0]

    def body(dy_ref, xr_ref, xi_ref, lr_ref, li_ref, u_ref, dcr_ref, dci_ref, dbr_ref, dbi_ref):
        dyb = dy_ref[...]
        ub = u_ref[...].astype(_BF16)
        dcr_ref[...] = _dot(dyb, xr_ref[...], _TN)
        dci_ref[...] = _dot(dyb, xi_ref[...], _TN)
        dbr_ref[...] = _dot(ub, lr_ref[...], _TN)
        dbi_ref[...] = _dot(ub, li_ref[...], _TN)

    width = pl.BlockSpec((L, SB_WIDTH), lambda k: (0, k))
    state = pl.BlockSpec((L, SB_STATE), lambda k: (0, k))
    out = pl.BlockSpec((SB_WIDTH, SB_STATE), lambda k: (0, k))
    return _call(body, (SUPER,), [width, state, state, state, state, width], [out] * 4,
                 [_sds((SB_WIDTH, N_STATE), _F32)] * 4,
                 "ssm_weight_grads")(dy, x_re, x_im, lam_re, lam_im, u)


_SSM_PACK = {"ssm_b_re": (0, SSM_WIDTH, 0, SSM_STATE), "ssm_c_re": (0, SSM_WIDTH, 64, SSM_STATE),
             "ssm_b_im": (512, SSM_WIDTH, 0, SSM_STATE), "ssm_c_im": (512, SSM_WIDTH, 64, SSM_STATE),
             "ssm_lambda_re": (1024, SSM_GROUPS, 0, SSM_STATE), "ssm_lambda_im": (1024, SSM_GROUPS, 64, SSM_STATE),
             "ssm_d": (1056, SSM_GROUPS, 0, SSM_GROUP), "ssm_log_dt": (1088, 1, 0, SSM_GROUPS)}
_SSM_PACK_ROWS = 1096


def _ssm_param_bwd(da_re, da_im, dbt_re, dbt_im, dct_re, dct_im, lam_re, lam_im, log_dt, b_re, b_im, g_d):
    def body(dar, dai, dbr, dbi, dcr, dci, lr_ref, li_ref, ld_ref, bre_ref, bim_ref, gd_ref, pack_ref):
        lane_in = _iota((SSM_STATE, _LANES), 0)
        lane_out = _iota((SSM_STATE, _LANES), 1)
        low = (lane_out == lane_in).astype(_F32)
        high = (lane_out == lane_in + SSM_STATE).astype(_F32)

        def side_by_side(a, b):
            return _dot_exact(a, low, _NN) + _dot_exact(b, high, _NN)

        tail = _SSM_PACK["ssm_d"][0]
        pack_ref[tail:, :] = jnp.zeros((_SSM_PACK_ROWS - tail, _LANES), _F32)
        pack_ref[tail:tail + SSM_GROUPS, 0:SSM_GROUP] = gd_ref[...]
        own_c = (_iota((SB_WIDTH, SB_STATE), 0) >> 4) == (_iota((SB_WIDTH, SB_STATE), 1) >> 6)

        def unfold(ref):
            blocks = []
            for k in range(SUPER):
                t = jnp.where(own_c, ref[:, _sb_state(k)], 0.0)
                t = sum(t[:, 128 * i:128 * (i + 1)] for i in range(SB_STATE // 128))
                blocks.append((t + pltpu.roll(t, SSM_STATE, 1))[:, :SSM_STATE])
            return jnp.concatenate(blocks, axis=0)

        dbb_re, dbb_im = unfold(dbr), unfold(dbi)
        b_re, b_im = bre_ref[...], bim_ref[...]
        dt_col = _dt_column(ld_ref[...])
        (_, _, fr, fi), vjp = jax.vjp(_s5_discretize, lr_ref[...], li_ref[...], dt_col)
        spread = _rows_of_group()
        fr_t = _dot_exact(spread, fr, _NN)
        fi_t = _dot_exact(spread, fi, _NN)
        pack_ref[0:SSM_WIDTH, :] = side_by_side(fr_t * dbb_re + fi_t * dbb_im, unfold(dcr))
        pack_ref[SSM_WIDTH:2 * SSM_WIDTH, :] = side_by_side(fr_t * dbb_im - fi_t * dbb_re, -unfold(dci))
        d_fr = _dot_exact(spread, dbb_re * b_re + dbb_im * b_im, _TN)
        d_fi = _dot_exact(spread, dbb_im * b_re - dbb_re * b_im, _TN)
        e64, own = _group_masks()

        def from_row(ref):
            return _dot_exact(jnp.where(own, ref[...], 0.0), e64, _NT)

        d_lr, d_li, d_dt = vjp((from_row(dar), from_row(dai), d_fr, d_fi))
        lam_rows = _SSM_PACK["ssm_lambda_re"][0]
        pack_ref[lam_rows:lam_rows + SSM_GROUPS, :] = side_by_side(d_lr, d_li)
        eye = (_iota((SSM_GROUPS, SSM_GROUPS), 0) == _iota((SSM_GROUPS, SSM_GROUPS), 1)).astype(_F32)
        dt_row = _SSM_PACK["ssm_log_dt"][0]
        pack_ref[dt_row:dt_row + 1, 0:SSM_GROUPS] = _dot_exact(
            jnp.broadcast_to(d_dt, (SSM_GROUPS, 128)), eye, _TN)[0:1]

    ins = [da_re, da_im, dbt_re, dbt_im, dct_re, dct_im, lam_re, lam_im, log_dt, b_re, b_im, g_d]
    out = (_SSM_PACK_ROWS, _LANES)
    return _call(body, (1,), [_whole(a.shape) for a in ins], _whole(out), _sds(out, _F32), "ssm_param_bwd")(*ins)


def _head_spread(j):
    r = _iota((KV_WIDTH, 256), 0)
    c = _iota((KV_WIDTH, 256), 1)
    return (r == HEAD_DIM * j + (c & (HEAD_DIM - 1))).astype(_BF16)


STACK = Q_PER_KV * BLOCK


def _stack_heads(t):
    lane_head = _iota((1, 256), 1) >> 6
    return jnp.concatenate([jnp.where(lane_head == g, t, jnp.zeros_like(t)) for g in range(Q_PER_KV)], axis=0)


def _unstack_heads(t):
    lane_head = _iota((1, 256), 1) >> 6
    return sum(jnp.where(lane_head == g, t[BLOCK * g:BLOCK * (g + 1)], 0.0) for g in range(Q_PER_KV))


def _stacked_sinks(sink_ref, j):
    block = _iota((STACK, 1), 0) >> 7
    col = jnp.full((STACK, 1), sink_ref[Q_PER_KV * j], _F32)
    for g in range(1, Q_PER_KV):
        col = jnp.where(block == g, sink_ref[Q_PER_KV * j + g], col)
    return col


def _fold_heads(t, j):
    t = t[:, :KV_WIDTH] + t[:, KV_WIDTH:]
    t = t + pltpu.roll(t, HEAD_DIM, 1)
    return jnp.where((_iota((1, KV_WIDTH), 1) >> 6) == j, t, 0.0)


def _attn_scores(q_stacked, kt, blk, sink):
    s = _dot(q_stacked, kt, _NT) * (HEAD_DIM ** -0.5)
    qi = _iota((STACK, 2 * BLOCK), 0) & (BLOCK - 1)
    kj = _iota((STACK, 2 * BLOCK), 1)
    rel = qi + BLOCK - kj
    valid = (rel >= 0) & (rel < BLOCK) & (blk * BLOCK - BLOCK + kj >= 0)
    s = jnp.where(valid, s, MASK_VALUE)
    m = jnp.maximum(jnp.max(s, axis=-1, keepdims=True), sink)
    p = jnp.exp(s - m)
    e_sink = jnp.exp(sink - m)
    den = jnp.sum(p, axis=-1, keepdims=True) + e_sink
    return p / den, e_sink / den


def _attn_specs():
    prev = lambda i: (jnp.maximum(i - 1, 0), 0)
    cur = lambda i: (i, 0)
    kv = [pl.BlockSpec((BLOCK, KV_WIDTH), prev), pl.BlockSpec((BLOCK, KV_WIDTH), cur)]
    return [pl.BlockSpec((BLOCK, ATTN_WIDTH), cur)] + kv + kv


def _attn_fwd(q, k, v, sinks, g_attn):
    L = q.shape[0]

    def body(q_ref, kp_ref, kc_ref, vp_ref, vc_ref, sink_ref, g_ref, o_ref, n_ref):
        blk = pl.program_id(0)
        kwin = jnp.concatenate([kp_ref[...], kc_ref[...]], axis=0)
        vwin = jnp.concatenate([vp_ref[...], vc_ref[...]], axis=0)
        halves = []
        for j in range(N_KV_HEADS):
            spread = _head_spread(j)
            kt = _dot(kwin, spread, _NN).astype(_BF16)
            vt = _dot(vwin, spread, _NN).astype(_BF16)
            qs = _stack_heads(q_ref[:, 256 * j:256 * (j + 1)])
            p, _ = _attn_scores(qs, kt, blk, _stacked_sinks(sink_ref, j))
            halves.append(_unstack_heads(_dot(p, vt, _NN)))
        o = jnp.concatenate(halves, axis=1)
        o_ref[...] = o
        n, _ = _rms_fwd(o, g_ref[...])
        n_ref[...] = n.astype(_BF16)

    cur = lambda i: (i, 0)
    return _call(body, (L // BLOCK,),
                 _attn_specs() + [pl.BlockSpec(memory_space=pltpu.SMEM), _whole((1, ATTN_WIDTH))],
                 [pl.BlockSpec((BLOCK, ATTN_WIDTH), cur)] * 2,
                 [_sds((L, ATTN_WIDTH), _F32), _sds((L, ATTN_WIDTH), _BF16)],
                 "attn_fwd")(q, k, k, v, v, sinks, g_attn)


def _attn_bwd(q, k, v, o, dn, sinks, g_attn):
    L = q.shape[0]

    def body(q_ref, kp_ref, kc_ref, vp_ref, vc_ref, o_ref, dn_ref, sink_ref, g_ref,
             dq_ref, dk_ref, dv_ref, dsink_ref, dg_ref):
        blk = pl.program_id(0)
        first = blk == 0

        @pl.when(first)
        def _():
            dk_ref[...] = jnp.zeros_like(dk_ref)
            dv_ref[...] = jnp.zeros_like(dv_ref)
            dsink_ref[...] = jnp.zeros_like(dsink_ref)

        o = o_ref[...]
        g = g_ref[...]
        _, r = _rms_fwd(o, g)
        do, dg = _rms_bwd(dn_ref[...], o, g, r)
        _accumulate(dg_ref, dg, first)
        kwin = jnp.concatenate([kp_ref[...], kc_ref[...]], axis=0)
        vwin = jnp.concatenate([vp_ref[...], vc_ref[...]], axis=0)
        lane = _iota((1, 128), 1)
        dsink = jnp.zeros((1, 128), _F32)
        dkwin = jnp.zeros((2 * BLOCK, KV_WIDTH), _F32)
        dvwin = jnp.zeros((2 * BLOCK, KV_WIDTH), _F32)
        dq_halves = []
        for j in range(N_KV_HEADS):
            spread = _head_spread(j)
            kt = _dot(kwin, spread, _NN).astype(_BF16)
            vt = _dot(vwin, spread, _NN).astype(_BF16)
            qs = _stack_heads(q_ref[:, 256 * j:256 * (j + 1)])
            dos = _stack_heads(do[:, 256 * j:256 * (j + 1)]).astype(_BF16)
            p, p_sink = _attn_scores(qs, kt, blk, _stacked_sinks(sink_ref, j))
            dp = _dot(dos, vt, _NT)
            delta = jnp.sum(p * dp, axis=-1, keepdims=True)
            ds = (p * (dp - delta) * (HEAD_DIM ** -0.5)).astype(_BF16)
            sink_term = p_sink * delta
            for g in range(Q_PER_KV):
                head_sum = jnp.sum(sink_term[BLOCK * g:BLOCK * (g + 1)], axis=0, keepdims=True)
                dsink = dsink - jnp.where(lane == Q_PER_KV * j + g, head_sum, 0.0)
            dvwin = dvwin + _fold_heads(_dot(p, dos, _TN), j)
            dkwin = dkwin + _fold_heads(_dot(ds, qs, _TN), j)
            dq_halves.append(_unstack_heads(_dot(ds, kt, _NN)))
        dq_ref[...] = jnp.concatenate(dq_halves, axis=1)
        dsink_ref[...] += dsink
        prev = pl.ds(pl.multiple_of(jnp.maximum(blk - 1, 0) * BLOCK, BLOCK), BLOCK)
        cur = pl.ds(pl.multiple_of(blk * BLOCK, BLOCK), BLOCK)
        dk_ref[prev, :] += dkwin[:BLOCK]
        dk_ref[cur, :] += dkwin[BLOCK:]
        dv_ref[prev, :] += dvwin[:BLOCK]
        dv_ref[cur, :] += dvwin[BLOCK:]

    cur = lambda i: (i, 0)
    blk_q = pl.BlockSpec((BLOCK, ATTN_WIDTH), cur)
    return _call(body, (L // BLOCK,),
                 _attn_specs() + [blk_q, blk_q, pl.BlockSpec(memory_space=pltpu.SMEM), _whole((1, ATTN_WIDTH))],
                 [blk_q, _whole((L, KV_WIDTH)), _whole((L, KV_WIDTH)), _whole((1, 128)), _whole((1, ATTN_WIDTH))],
                 [_sds((L, ATTN_WIDTH), _F32), _sds((L, KV_WIDTH), _F32), _sds((L, KV_WIDTH), _F32),
                  _sds((1, 128), _F32), _sds((1, ATTN_WIDTH), _F32)],
                 "attn_bwd")(q, k, k, v, v, o, dn, sinks, g_attn)


def _out_proj(n_ssm, n_attn, x, w_out, g_post_mix, g_pre_ffn):
    L = x.shape[0]
    tm = _tile(L)

    def body(ns_ref, na_ref, x_ref, w_ref, g1_ref, g2_ref, merged_ref, mo_ref, h1_ref, hn2_ref):
        merged = jnp.concatenate([ns_ref[...], na_ref[...]], axis=1)
        merged_ref[...] = merged
        mo = _dot(merged, w_ref[...], _NN)
        mo_ref[...] = mo
        n, _ = _rms_fwd(mo, g1_ref[...])
        h1 = x_ref[...] + n
        h1_ref[...] = h1
        hn2, _ = _rms_fwd(h1, g2_ref[...])
        hn2_ref[...] = hn2.astype(_BF16)

    row = _whole((1, D_MODEL))
    return _call(body, (L // tm,),
                 [_rows(tm, SSM_WIDTH), _rows(tm, ATTN_WIDTH), _rows(tm, D_MODEL), _whole((D_MODEL, D_MODEL)), row, row],
                 [_rows(tm, D_MODEL)] * 4,
                 [_sds((L, D_MODEL), _BF16), _sds((L, D_MODEL), _F32), _sds((L, D_MODEL), _F32), _sds((L, D_MODEL), _BF16)],
                 "out_proj")(n_ssm, n_attn, x, w_out, g_post_mix, g_pre_ffn)


def _ffn(hn2, h1, target, w_gate_up, w_down, g_pre_ffn, g_post_ffn):
    L = h1.shape[0]
    tm = _tile(L)
    half = D_FF // 2

    def body(hn2_ref, h1_ref, tgt_ref, wgu_hbm, wd_hbm, g2_ref, g3_ref,
             act_ref, dgu_ref, dff_ref, dh1_ref, loss_ref, dg3_ref, dg2_ref,
             wgu, wd, gu, sem):
        first = pl.program_id(0) == 0

        @pl.when(first)
        def _():
            c1 = pltpu.make_async_copy(wgu_hbm, wgu, sem.at[0])
            c2 = pltpu.make_async_copy(wd_hbm, wd, sem.at[1])
            c1.start()
            c2.start()
            c1.wait()
            c2.wait()

        hn2 = hn2_ref[...]
        ff = jnp.zeros((tm, D_MODEL), _F32)
        for c in range(2):
            gate = _dot(hn2, wgu[half * c:half * (c + 1), :], _NT)
            up = _dot(hn2, wgu[D_FF + half * c:D_FF + half * (c + 1), :], _NT)
            gu[:, half * c:half * (c + 1)] = gate
            gu[:, D_FF + half * c:D_FF + half * (c + 1)] = up
            act = (gate * jax.nn.sigmoid(gate) * up).astype(_BF16)
            act_ref[:, half * c:half * (c + 1)] = act
            ff = ff + _dot(act, wd[half * c:half * (c + 1), :], _NN)
        g3 = g3_ref[...]
        n, r = _rms_fwd(ff, g3)
        h1 = h1_ref[...]
        err = h1 + n - tgt_ref[...]
        loss = 0.5 * jnp.sum(jnp.mean(err * err, axis=-1, keepdims=True), axis=0, keepdims=True)
        _accumulate(loss_ref, jnp.broadcast_to(loss, (1, 128)), first)
        dh2 = err * (1.0 / D_MODEL)
        dff, dg3 = _rms_bwd(dh2, ff, g3, r)
        _accumulate(dg3_ref, dg3, first)
        dffb = dff.astype(_BF16)
        dff_ref[...] = dffb
        dhn2 = jnp.zeros((tm, D_MODEL), _F32)
        for c in range(2):
            dact = _dot(dffb, wd[half * c:half * (c + 1), :], _NT)
            gate = gu[:, half * c:half * (c + 1)]
            up = gu[:, D_FF + half * c:D_FF + half * (c + 1)]
            sig = jax.nn.sigmoid(gate)
            silu = gate * sig
            dgate = (dact * up * (sig + silu * (1.0 - sig))).astype(_BF16)
            dup = (dact * silu).astype(_BF16)
            dgu_ref[:, half * c:half * (c + 1)] = dgate
            dgu_ref[:, D_FF + half * c:D_FF + half * (c + 1)] = dup
            dhn2 = dhn2 + _dot(dgate, wgu[half * c:half * (c + 1), :], _NN)
            dhn2 = dhn2 + _dot(dup, wgu[D_FF + half * c:D_FF + half * (c + 1), :], _NN)
        g2 = g2_ref[...]
        _, r2 = _rms_fwd(h1, g2)
        dh1, dg2 = _rms_bwd(dhn2, h1, g2, r2)
        _accumulate(dg2_ref, dg2, first)
        dh1_ref[...] = dh2 + dh1

    row = _whole((1, D_MODEL))
    anyspace = pl.BlockSpec(memory_space=pl.ANY)
    return _call(body, (L // tm,),
                 [_rows(tm, D_MODEL), _rows(tm, D_MODEL), _rows(tm, D_MODEL), anyspace, anyspace, row, row],
                 [_rows(tm, D_FF), _rows(tm, 2 * D_FF), _rows(tm, D_MODEL), _rows(tm, D_MODEL),
                  _whole((1, 128)), row, row],
                 [_sds((L, D_FF), _BF16), _sds((L, 2 * D_FF), _BF16), _sds((L, D_MODEL), _BF16),
                  _sds((L, D_MODEL), _F32), _sds((1, 128), _F32), _sds((1, D_MODEL), _F32), _sds((1, D_MODEL), _F32)],
                 "ffn",
                 scratch=[pltpu.VMEM((2 * D_FF, D_MODEL), _BF16), pltpu.VMEM((D_FF, D_MODEL), _BF16),
                          pltpu.VMEM((tm, 2 * D_FF), _F32), pltpu.SemaphoreType.DMA((2,))],
                 )(hn2, h1, target, w_gate_up, w_down, g_pre_ffn, g_post_ffn)


def _out_proj_bwd(dh1, mo, w_out, g_post_mix, tokens=()):
    L = dh1.shape[0]
    tm = _tile(L)

    def body(dh1_ref, mo_ref, w_ref, g_ref, dmo_ref, dns_ref, dna_ref, dg_ref):
        first = pl.program_id(0) == 0
        mo = mo_ref[...]
        g = g_ref[...]
        _, r = _rms_fwd(mo, g)
        dmo, dg = _rms_bwd(dh1_ref[...], mo, g, r)
        _accumulate(dg_ref, dg, first)
        dmob = dmo.astype(_BF16)
        dmo_ref[...] = dmob
        dmerged = _dot(dmob, w_ref[...], _NT)
        dns_ref[...] = dmerged[:, :SSM_WIDTH]
        dna_ref[...] = dmerged[:, SSM_WIDTH:]

    row = _whole((1, D_MODEL))
    return _call(body, (L // tm,),
                 [_rows(tm, D_MODEL), _rows(tm, D_MODEL), _whole((D_MODEL, D_MODEL)), row],
                 [_rows(tm, D_MODEL), _rows(tm, SSM_WIDTH), _rows(tm, ATTN_WIDTH), row],
                 [_sds((L, D_MODEL), _BF16), _sds((L, SSM_WIDTH), _F32), _sds((L, ATTN_WIDTH), _F32),
                  _sds((1, D_MODEL), _F32)],
                 "out_proj_bwd", tokens=tokens)(dh1, mo, w_out, g_post_mix)


def _in_proj_bwd(du, dq, dk, dv, cos_t, sin_t, x, dh1, g_pre_mix, w_in, tokens=()):
    L = x.shape[0]
    tm = _tile(L)

    def body(du_ref, dq_ref, dk_ref, dv_ref, cos_ref, sin_ref, x_ref, dh1_ref, g_ref, w_ref,
             dproj_ref, dx_ref, dg_ref):
        first = pl.program_id(0) == 0
        cos_v, sin_v = cos_ref[...], sin_ref[...]
        dproj = jnp.concatenate([du_ref[...], _rope_transpose(dq_ref[...], cos_v, sin_v),
                                 _rope_transpose(dk_ref[...], cos_v, sin_v), dv_ref[...]], axis=1).astype(_BF16)
        dproj_ref[...] = dproj
        dhn = _dot(dproj, w_ref[...], _NN)
        x = x_ref[...]
        g = g_ref[...]
        _, r = _rms_fwd(x, g)
        dx, dg = _rms_bwd(dhn, x, g, r)
        _accumulate(dg_ref, dg, first)
        dx_ref[...] = dh1_ref[...] + dx

    row = _whole((1, D_MODEL))
    return _call(body, (L // tm,),
                 [_rows(tm, SSM_WIDTH), _rows(tm, ATTN_WIDTH), _rows(tm, KV_WIDTH), _rows(tm, KV_WIDTH),
                  _rows(tm, KV_WIDTH), _rows(tm, KV_WIDTH), _rows(tm, D_MODEL), _rows(tm, D_MODEL), row,
                  _whole((IN_WIDTH, D_MODEL))],
                 [_rows(tm, IN_WIDTH), _rows(tm, D_MODEL), row],
                 [_sds((L, IN_WIDTH), _BF16), _sds((L, D_MODEL), _F32), _sds((1, D_MODEL), _F32)],
                 "in_proj_bwd", tokens=tokens)(du, dq, dk, dv, cos_t, sin_t, x, dh1, g_pre_mix, w_in)


def _matmul_tn(a, b, out_dtype, name, scale=1.0):
    K, M = a.shape
    N = b.shape[1]
    tm = next(t for t in (512, 256, 128) if M % t == 0)
    tn = next(t for t in (512, 256, 128) if N % t == 0)

    def body(a_ref, b_ref, o_ref):
        acc = _dot(a_ref[...], b_ref[...], _TN)
        o_ref[...] = (acc if scale == 1.0 else acc * scale).astype(out_dtype)

    params = pltpu.CompilerParams(dimension_semantics=("arbitrary", "arbitrary"), vmem_limit_bytes=VMEM_LIMIT)
    return pl.pallas_call(body, grid=(M // tm, N // tn),
                          in_specs=[pl.BlockSpec((K, tm), lambda i, j: (0, i)),
                                    pl.BlockSpec((K, tn), lambda i, j: (0, j))],
                          out_specs=pl.BlockSpec((tm, tn), lambda i, j: (i, j)),
                          out_shape=_sds((M, N), out_dtype), compiler_params=params, name=name)(a, b)


def _to_chunked(a):
    L, n = a.shape
    return a.reshape(SCAN_CHUNKS, L // SCAN_CHUNKS, n).transpose(1, 0, 2).reshape(L, n)


def _from_chunked(a):
    L, n = a.shape
    return a.reshape(L // SCAN_CHUNKS, SCAN_CHUNKS, n).transpose(1, 0, 2).reshape(L, n)


def _local_step(x, pos, target, p, fetch, publish):
    L = x.shape[0]
    T = L // SCAN_CHUNKS
    cos_t, sin_t = _rope_tables(pos.reshape(L, 1))
    w_in, = fetch(("w_in",), None)
    hn, u, q, k, v = _in_proj(x, p["g_pre_mix"], w_in, cos_t, sin_t)

    ssm = {n: _to_2d(n, p[n]) for n in ("ssm_lambda_re", "ssm_lambda_im", "ssm_log_dt", "ssm_b_re", "ssm_b_im",
                                        "ssm_c_re", "ssm_c_im")}
    d_row = p["ssm_d"].reshape(1, SSM_WIDTH)
    a_re, a_im, bt_re, bt_im, ct_re, ct_im = _ssm_prep(
        ssm["ssm_lambda_re"], ssm["ssm_lambda_im"], ssm["ssm_log_dt"], ssm["ssm_b_re"], ssm["ssm_b_im"],
        ssm["ssm_c_re"], ssm["ssm_c_im"])

    u_c = _to_chunked(u)
    bu_re, bu_im = _ssm_bu(u_c, bt_re, bt_im)
    x_re, x_im = _scan_fwd(bu_re.reshape(T, SCAN_CHUNKS, N_STATE), bu_im.reshape(T, SCAN_CHUNKS, N_STATE), a_re, a_im)
    w_glu, = fetch(("w_glu",), x_re)
    y, z, n_ssm_c = _ssm_out(x_re.reshape(L, N_STATE), x_im.reshape(L, N_STATE), u_c, ct_re, ct_im, d_row,
                             w_glu, p["b_glu"], p["g_ssm_out"])
    n_ssm = _from_chunked(n_ssm_c)

    sinks = p["attn_sinks"].reshape(N_Q_HEADS)
    o, n_attn = _attn_fwd(q, k, v, sinks, p["g_attn_out"])
    w_out, = fetch(("w_out",), n_attn)
    merged, mo, h1, hn2 = _out_proj(n_ssm, n_attn, x, w_out, p["g_post_mix"], p["g_pre_ffn"])
    w_gate_up, w_down = fetch(("w_gate_up", "w_down"), hn2)
    act, dgu, dff, dh1, loss, dg_post_ffn, dg_pre_ffn = _ffn(
        hn2, h1, target, w_gate_up, w_down, p["g_pre_ffn"], p["g_post_ffn"])
    grads = {"g_post_ffn": dg_post_ffn, "g_pre_ffn": dg_pre_ffn}
    tokens = publish({"w_down": _matmul_tn(act, dff, _BF16, "grad_w_down"),
                      "w_gate_up": _matmul_tn(dgu, hn2, _BF16, "grad_w_gate_up")})

    dmo, dn_ssm, dn_attn, grads["g_post_mix"] = _out_proj_bwd(dh1, mo, w_out, p["g_post_mix"], tokens)
    grad_w_out = _matmul_tn(merged, dmo, _BF16, "grad_w_out")

    dq, dk, dv, dsink, grads["g_attn_out"] = _attn_bwd(q, k, v, o, dn_attn, sinks, p["g_attn_out"])
    grads["attn_sinks"] = dsink

    gy, dz, dy, dud, dx_re, dx_im, grads["g_ssm_out"], grads["b_glu"], dd = _ssm_out_bwd(
        _to_chunked(dn_ssm), y, z, u_c, ct_re, ct_im, d_row, w_glu, p["g_ssm_out"])
    grads.update(w_out=grad_w_out, w_glu=_matmul_tn(dz, gy, _BF16, "grad_w_glu"))
    lam_re, lam_im, da_re, da_im = _scan_bwd(dx_re.reshape(T, SCAN_CHUNKS, N_STATE), dx_im.reshape(T, SCAN_CHUNKS, N_STATE),
                                             x_re, x_im, a_re, a_im, [grads["w_out"], grads["w_glu"]])
    lam_re = lam_re.reshape(L, N_STATE)
    lam_im = lam_im.reshape(L, N_STATE)
    dct_re, dct_im, dbt_re, dbt_im = _ssm_weight_grads(
        dy, x_re.reshape(L, N_STATE), x_im.reshape(L, N_STATE), lam_re, lam_im, u_c)
    ssm_pack = _ssm_param_bwd(
        da_re, da_im, dbt_re, dbt_im, dct_re, dct_im,
        ssm["ssm_lambda_re"], ssm["ssm_lambda_im"], ssm["ssm_log_dt"], ssm["ssm_b_re"], ssm["ssm_b_im"],
        dd.reshape(SSM_GROUPS, SSM_GROUP))
    grads.update(ssm_pack=ssm_pack, loss=loss)
    publish(grads)

    du = _from_chunked(_ssm_du(lam_re, lam_im, bt_re, bt_im, dud))
    dproj, grad_x, g_pre_mix = _in_proj_bwd(du, dq, dk, dv, cos_t, sin_t, x, dh1, p["g_pre_mix"], w_in, [ssm_pack])
    publish({"g_pre_mix": g_pre_mix, "w_in": _matmul_tn(dproj, hn, _BF16, "grad_w_in")})
    return grad_x


_MESH = pl.DeviceIdType.MESH
_PEERS = N_DEV - 1


def _mesh_pos():
    return lax.axis_index("x"), lax.axis_index("y"), lax.axis_index("c")


def _dev_index(px, py, pc):
    return 4 * px + 2 * py + pc


def _all_gather(shards, out_dtype, name):
    n = len(shards)

    def body(*refs):
        ins, outs, stages = refs[:n], refs[n:2 * n], refs[2 * n:3 * n]
        send_sems, recv_sems, local_sems = refs[3 * n:]
        x, y, c = _mesh_pos()
        me, sibling = (x, y, c), (x, y, 1 - c)
        chips = [(1 - x, y), (x, 1 - y), (1 - x, 1 - y)]

        def copy(w, k, block, to, src=None):
            slot = outs[w].at[_dev_index(*block)]
            return pltpu.make_async_remote_copy(
                src_ref=slot if src is None else src, dst_ref=slot,
                send_sem=send_sems.at[_PEERS * w + k], recv_sem=recv_sems.at[_PEERS * w + k],
                device_id=to, device_id_type=_MESH)

        for w in range(n):
            stages[w][...] = ins[w][...].astype(out_dtype)
        mine, first, passed = [], [], []
        for w in range(n):
            cp = pltpu.make_async_copy(stages[w], outs[w].at[_dev_index(*me)], local_sems.at[w])
            cp.start()
            mine.append(cp)
            sends = [copy(w, 0, me, sibling, src=stages[w])]
            sends += [copy(w, 1 + j, me, (*chip, c), src=stages[w]) for j, chip in enumerate(chips)]
            for cp in sends:
                cp.start()
            first += sends
        for w in range(n):
            for j, chip in enumerate(chips):
                copy(w, 1 + j, (*chip, c), me).wait_recv()
                cp = copy(w, 4 + j, (*chip, c), sibling)
                cp.start()
                passed.append(cp)
        for w in range(n):
            copy(w, 0, sibling, me).wait_recv()
            for j, chip in enumerate(chips):
                copy(w, 4 + j, (*chip, 1 - c), me).wait_recv()
        for cp in first + passed:
            cp.wait_send()
        for cp in mine:
            cp.wait()

    return pl.pallas_call(
        body, name=name,
        out_shape=[_sds((N_DEV,) + s.shape, out_dtype) for s in shards],
        in_specs=[pl.BlockSpec(memory_space=pltpu.VMEM)] * n,
        out_specs=[pl.BlockSpec(memory_space=pl.ANY)] * n,
        scratch_shapes=[pltpu.VMEM(s.shape, out_dtype) for s in shards]
        + [pltpu.SemaphoreType.DMA((_PEERS * n,)), pltpu.SemaphoreType.DMA((_PEERS * n,)),
           pltpu.SemaphoreType.DMA((n,))],
        compiler_params=pltpu.CompilerParams(vmem_limit_bytes=VMEM_LIMIT),
    )(*shards)


_HBM_SPEC = pl.BlockSpec(memory_space=pltpu.HBM)
_SEM_SPEC = pl.BlockSpec(memory_space=pltpu.SEMAPHORE)
_DATAFLOW = pltpu.SideEffectType.DATAFLOW_SIDE_EFFECTING


def _peer(x, y, c, r):
    return (x ^ ((r >> 2) & 1), y ^ ((r >> 1) & 1), c ^ (r & 1))


def _hbm(a):
    return pltpu.with_memory_space_constraint(a, pltpu.HBM)


def _send_start(sources, blocked, name):
    n = len(sources)
    lands = [lax.empty((N_DEV,) + (s.shape[1:] if blocked else s.shape), s.dtype) for s in sources]

    def body(*refs):
        srcs, zones = refs[:n], refs[n:2 * n]
        send_sems, recv_sems = refs[2 * n:3 * n], refs[3 * n:4 * n]
        token, local_sems = refs[6 * n], refs[6 * n + 1]
        x, y, c = _mesh_pos()
        me = _dev_index(x, y, c)
        local = []
        for w in range(n):
            cp = pltpu.make_async_copy(srcs[w].at[me] if blocked else srcs[w], zones[w].at[me], local_sems.at[w])
            cp.start()
            local.append(cp)
            for r in range(1, N_DEV):
                peer = _peer(x, y, c, r)
                pltpu.make_async_remote_copy(
                    src_ref=srcs[w].at[_dev_index(*peer)] if blocked else srcs[w], dst_ref=zones[w].at[me],
                    send_sem=send_sems[w].at[r - 1], recv_sem=recv_sems[w].at[r - 1],
                    device_id=peer, device_id_type=_MESH).start()
        for cp in local:
            cp.wait()
        token[...] = jnp.zeros_like(token)

    sems = [pltpu.SemaphoreType.DMA((_PEERS,))] * (2 * n)
    out = pl.pallas_call(
        body, name=name,
        out_shape=sems + [pltpu.HBM(a.shape, a.dtype) for a in list(sources) + lands] + [_sds((8, 128), _F32)],
        in_specs=[_HBM_SPEC] * (2 * n),
        out_specs=[_SEM_SPEC] * (2 * n) + [_HBM_SPEC] * (2 * n) + [pl.BlockSpec(memory_space=pltpu.VMEM)],
        input_output_aliases={i: 2 * n + i for i in range(2 * n)},
        scratch_shapes=[pltpu.SemaphoreType.DMA((n,))],
        compiler_params=pltpu.CompilerParams(has_side_effects=_DATAFLOW),
    )(*[_hbm(a) for a in sources], *[_hbm(a) for a in lands])
    return out[:n], out[n:2 * n], out[2 * n:3 * n], out[3 * n:4 * n], out[4 * n]


def _send_wait(send_sems, recv_sems, sources, lands, after, blocked, name):
    n = len(sources)

    def body(*refs):
        srcs, zones = refs[:n], refs[n:2 * n]
        sends, recvs = refs[2 * n:3 * n], refs[3 * n:4 * n]
        x, y, c = _mesh_pos()
        for w in range(n):
            for r in range(1, N_DEV):
                peer = _peer(x, y, c, r)
                idx = _dev_index(*peer)
                cp = pltpu.make_async_remote_copy(
                    src_ref=srcs[w].at[idx] if blocked else srcs[w], dst_ref=zones[w].at[idx],
                    send_sem=sends[w].at[r - 1], recv_sem=recvs[w].at[r - 1],
                    device_id=peer, device_id_type=_MESH)
                cp.wait_send()
                cp.wait_recv()

    out = pl.pallas_call(
        body, name=name,
        out_shape=[pltpu.HBM(a.shape, a.dtype) for a in list(sources) + list(lands)],
        in_specs=[_HBM_SPEC] * (2 * n) + [_SEM_SPEC] * (2 * n) + [pl.BlockSpec(memory_space=pl.ANY)],
        out_specs=[_HBM_SPEC] * (2 * n),
        input_output_aliases={i: i for i in range(2 * n)},
        compiler_params=pltpu.CompilerParams(has_side_effects=_DATAFLOW),
    )(*sources, *lands, *send_sems, *recv_sems, after)
    return out[n:]


def _sequencer_exchange(sources, blocked, name, collective_id):
    n = len(sources)
    flags = blocked

    def body(*refs):
        srcs, zones = refs[:n], refs[n:2 * n]
        send_sems, recv_sems, local_sems = refs[2 * n:]
        x, y, c = _mesh_pos()
        me = _dev_index(x, y, c)
        barrier = pltpu.get_barrier_semaphore()
        for r in range(1, N_DEV):
            pl.semaphore_signal(barrier, inc=1, device_id=_peer(x, y, c, r), device_id_type=_MESH)
        pl.semaphore_wait(barrier, _PEERS)
        local, sends, recvs = [], [], []
        for w in range(n):
            cp = pltpu.make_async_copy(srcs[w].at[me] if flags[w] else srcs[w], zones[w].at[me], local_sems.at[w])
            cp.start()
            local.append(cp)
            for r in range(1, N_DEV):
                peer = _peer(x, y, c, r)
                idx = _dev_index(*peer)
                k = _PEERS * w + r - 1
                src = srcs[w].at[idx] if flags[w] else srcs[w]
                send = pltpu.make_async_remote_copy(
                    src_ref=src, dst_ref=zones[w].at[me], send_sem=send_sems.at[k], recv_sem=recv_sems.at[k],
                    device_id=peer, device_id_type=_MESH)
                send.start()
                sends.append(send)
                recvs.append(pltpu.make_async_remote_copy(
                    src_ref=src, dst_ref=zones[w].at[idx], send_sem=send_sems.at[k], recv_sem=recv_sems.at[k],
                    device_id=peer, device_id_type=_MESH))
        for cp in recvs:
            cp.wait_recv()
        for cp in sends:
            cp.wait_send()
        for cp in local:
            cp.wait()

    return pl.kernel(
        body, name=name,
        out_type=[_sds((N_DEV,) + (s.shape[1:] if f else s.shape), s.dtype) for s, f in zip(sources, flags)],
        mesh=plsc.ScalarSubcoreMesh(axis_name="sequencer", num_cores=1),
        scratch_types=[pltpu.SemaphoreType.DMA((_PEERS * n,)), pltpu.SemaphoreType.DMA((_PEERS * n,)),
                       pltpu.SemaphoreType.DMA((n,))],
        compiler_params=pltpu.CompilerParams(collective_id=collective_id),
    )(*sources)


def _sequencer_gather(shards, name, collective_id):
    n = len(shards)
    fan = 4

    def body(*refs):
        srcs, zones = refs[:n], refs[n:2 * n]
        send_sems, recv_sems, local_sems = refs[2 * n:]
        x, y, c = _mesh_pos()
        me, sibling = (x, y, c), (x, y, 1 - c)
        chips = [(1 - x, y), (x, 1 - y), (1 - x, 1 - y)]
        barrier = pltpu.get_barrier_semaphore()
        for peer in [sibling] + [(*chip, c) for chip in chips]:
            pl.semaphore_signal(barrier, inc=1, device_id=peer, device_id_type=_MESH)
        pl.semaphore_wait(barrier, fan)

        def copy(w, k, block, to, src=None):
            slot = zones[w].at[_dev_index(*block)]
            return pltpu.make_async_remote_copy(
                src_ref=slot if src is None else src, dst_ref=slot,
                send_sem=send_sems.at[_PEERS * w + k], recv_sem=recv_sems.at[_PEERS * w + k],
                device_id=to, device_id_type=_MESH)

        mine, first, passed = [], [], []
        for w in range(n):
            cp = pltpu.make_async_copy(srcs[w], zones[w].at[_dev_index(*me)], local_sems.at[w])
            cp.start()
            mine.append(cp)
            sends = [copy(w, 0, me, sibling, src=srcs[w])]
            sends += [copy(w, 1 + j, me, (*chip, c), src=srcs[w]) for j, chip in enumerate(chips)]
            for cp in sends:
                cp.start()
            first += sends
        for w in range(n):
            for j, chip in enumerate(chips):
                copy(w, 1 + j, (*chip, c), me).wait_recv()
                cp = copy(w, fan + j, (*chip, c), sibling)
                cp.start()
                passed.append(cp)
        for w in range(n):
            copy(w, 0, sibling, me).wait_recv()
            for j, chip in enumerate(chips):
                copy(w, fan + j, (*chip, 1 - c), me).wait_recv()
        for cp in first + passed:
            cp.wait_send()
        for cp in mine:
            cp.wait()

    return pl.kernel(
        body, name=name, out_type=[_sds((N_DEV,) + s.shape, s.dtype) for s in shards],
        mesh=plsc.ScalarSubcoreMesh(axis_name="sequencer", num_cores=1),
        scratch_types=[pltpu.SemaphoreType.DMA((_PEERS * n,)), pltpu.SemaphoreType.DMA((_PEERS * n,)),
                       pltpu.SemaphoreType.DMA((n,))],
        compiler_params=pltpu.CompilerParams(collective_id=collective_id),
    )(*shards)


def _row_tile(rows):
    return next(t for t in range(min(rows, 256), 0, -16) if rows % t == 0)


def _sum_parts(parts, name, tokens=()):
    _, rows, cols = parts.shape
    tr = _row_tile(rows)

    def body(p_ref, g_ref):
        g = p_ref[0].astype(_F32)
        for s in range(1, N_DEV):
            g = g + p_ref[s].astype(_F32)
        g_ref[...] = g

    return _call(body, (rows // tr,), [pl.BlockSpec((N_DEV, tr, cols), lambda i: (0, i, 0))],
                 _rows(tr, cols), _sds((rows, cols), _F32), name, tokens=tokens)(parts)


def _adam_update(g, w, m, v):
    new_m = ADAM_B1 * m + (1.0 - ADAM_B1) * g
    new_v = ADAM_B2 * v + (1.0 - ADAM_B2) * (g * g)
    m_hat = new_m / (1.0 - ADAM_B1 ** ADAM_STEP)
    v_hat = new_v / (1.0 - ADAM_B2 ** ADAM_STEP)
    return -ADAM_LR * (m_hat / (jnp.sqrt(v_hat) + ADAM_EPS) + ADAM_WD * w), new_m, new_v


def _adamw_small(parts, items, sums, name, tokens=()):
    n_p, n_i = len(parts), len(items)

    def body(*refs):
        p_refs, state, outs = refs[:n_p], refs[n_p:n_p + 3 * n_i], refs[n_p + 3 * n_i:]

        def total(part, rows, cols):
            shift = cols.start % _LANES
            window = slice(cols.start - shift, cols.start - shift + _LANES) if shift else cols
            g = p_refs[part][0, rows, window]
            for s in range(1, N_DEV):
                g = g + p_refs[part][s, rows, window]
            return pltpu.roll(g, _LANES - shift, 1)[:, :cols.stop - cols.start] if shift else g

        for i, (part, rows, cols, _, _, _) in enumerate(items):
            g = total(part, rows, cols)
            w_ref, m_ref, v_ref = state[3 * i:3 * i + 3]
            delta, new_m, new_v = _adam_update(g, w_ref[...], m_ref[...], v_ref[...])
            outs[4 * i][...] = g
            outs[4 * i + 1][...] = delta
            outs[4 * i + 2][...] = new_m
            outs[4 * i + 3][...] = new_v
        for j, (part, rows, cols) in enumerate(sums):
            outs[4 * n_i + j][...] = total(part, rows, cols)

    ins = list(parts) + [a for item in items for a in item[3:]]
    out_shapes = [item[3].shape for item in items for _ in range(4)]
    out_shapes += [(rows.stop - rows.start, cols.stop - cols.start) for _, rows, cols in sums]
    out = _call(body, (1,), [_whole(a.shape) for a in ins], [_whole(s) for s in out_shapes],
                [_sds(s, _F32) for s in out_shapes], name, tokens=tokens)(*ins)
    return [out[4 * i:4 * i + 4] for i in range(n_i)], out[4 * n_i:]


def _adamw(parts, w, m, v, name, tokens=()):
    rows, cols = w.shape
    tr = _row_tile(rows)
    n_parts = parts.shape[0]

    def body(p_ref, w_ref, m_ref, v_ref, g_ref, d_ref, nm_ref, nv_ref):
        g = p_ref[0].astype(_F32)
        for s in range(1, n_parts):
            g = g + p_ref[s].astype(_F32)
        new_m = ADAM_B1 * m_ref[...] + (1.0 - ADAM_B1) * g
        new_v = ADAM_B2 * v_ref[...] + (1.0 - ADAM_B2) * (g * g)
        m_hat = new_m / (1.0 - ADAM_B1 ** ADAM_STEP)
        v_hat = new_v / (1.0 - ADAM_B2 ** ADAM_STEP)
        g_ref[...] = g
        d_ref[...] = -ADAM_LR * (m_hat / (jnp.sqrt(v_hat) + ADAM_EPS) + ADAM_WD * w_ref[...])
        nm_ref[...] = new_m
        nv_ref[...] = new_v

    blk = _rows(tr, cols)
    return _call(body, (rows // tr,),
                 [pl.BlockSpec((n_parts, tr, cols), lambda i: (0, i, 0)), blk, blk, blk],
                 [blk] * 4, [_sds((rows, cols), _F32)] * 4, name, tokens=tokens)(parts, w, m, v)


_SMALL = ("g_pre_mix", "ssm_lambda_re", "ssm_lambda_im", "ssm_log_dt", "ssm_b_re", "ssm_b_im",
          "ssm_c_re", "ssm_c_im", "ssm_d", "b_glu", "attn_sinks", "g_ssm_out", "g_attn_out",
          "g_post_mix", "g_pre_ffn", "g_post_ffn")
_BIG = ("w_in", "w_glu", "w_out", "w_gate_up", "w_down")
_WEIGHTS = ("g_pre_mix", "w_in", "ssm_lambda_re", "ssm_lambda_im", "ssm_log_dt", "ssm_b_re", "ssm_b_im",
            "ssm_c_re", "ssm_c_im", "ssm_d", "w_glu", "b_glu", "attn_sinks", "g_ssm_out", "g_attn_out",
            "w_out", "g_post_mix", "g_pre_ffn", "w_gate_up", "w_down", "g_post_ffn")
_LANES = 128


_SHAPE_2D = {
    "g_pre_mix": (1, D_MODEL), "ssm_lambda_re": (SSM_GROUPS, SSM_STATE), "ssm_lambda_im": (SSM_GROUPS, SSM_STATE),
    "ssm_log_dt": (1, SSM_GROUPS), "ssm_b_re": (SSM_WIDTH, SSM_STATE), "ssm_b_im": (SSM_WIDTH, SSM_STATE),
    "ssm_c_re": (SSM_WIDTH, SSM_STATE), "ssm_c_im": (SSM_WIDTH, SSM_STATE), "ssm_d": (SSM_GROUPS, SSM_GROUP),
    "b_glu": (1, 2 * SSM_WIDTH), "attn_sinks": (1, N_Q_HEADS), "g_ssm_out": (1, SSM_WIDTH),
    "g_attn_out": (1, ATTN_WIDTH), "g_post_mix": (1, D_MODEL), "g_pre_ffn": (1, D_MODEL), "g_post_ffn": (1, D_MODEL)}
_ROW_WIDTH = {"g_pre_mix": D_MODEL, "b_glu": 2 * SSM_WIDTH, "attn_sinks": _LANES, "g_ssm_out": SSM_WIDTH,
              "g_attn_out": ATTN_WIDTH, "g_post_mix": D_MODEL, "g_pre_ffn": D_MODEL, "g_post_ffn": D_MODEL,
              "loss": _LANES}
_DENSE = ()
_PER_GROUP_TRANSPOSED = ("ssm_b_re", "ssm_b_im")


def _to_2d(name, a):
    if name in _PER_GROUP_TRANSPOSED:
        a = a.reshape(SSM_GROUPS, SSM_STATE, SSM_GROUP).transpose(0, 2, 1)
    return a.reshape(_SHAPE_2D[name])


def _from_2d(name, a, shape):
    if name in _PER_GROUP_TRANSPOSED:
        a = a.reshape(SSM_GROUPS, SSM_GROUP, SSM_STATE).transpose(0, 2, 1)
    return a.reshape(shape)


def _row_slots(names):
    slots, row, col = {}, 0, 0
    for n in names:
        width = _ROW_WIDTH[n]
        if col + width > D_MODEL:
            row, col = row + 1, 0
        slots[n] = (row, col, width)
        col += width
    return slots


def _stack_rows(named, slots):
    n_rows = -(-(max(r for r, _, _ in slots.values()) + 1) // 8) * 8
    lines = []
    for r in range(n_rows):
        pieces = [named[n] for n, (row, _, _) in slots.items() if row == r]
        used = sum(p.shape[1] for p in pieces)
        if used < D_MODEL:
            pieces.append(jnp.zeros((1, D_MODEL - used), _F32))
        lines.append(jnp.concatenate(pieces, axis=1) if len(pieces) > 1 else pieces[0])
    return jnp.concatenate(lines, axis=0)


def kernel(x, positions, g_pre_mix, w_in, ssm_lambda_re, ssm_lambda_im, ssm_log_dt, ssm_b_re, ssm_b_im, ssm_c_re, ssm_c_im, ssm_d, w_glu, b_glu, attn_sinks, g_ssm_out, g_attn_out, w_out, g_post_mix, g_pre_ffn, w_gate_up, w_down, g_post_ffn, loss_target, m_g_pre_mix, m_w_in, m_ssm_lambda_re, m_ssm_lambda_im, m_ssm_log_dt, m_ssm_b_re, m_ssm_b_im, m_ssm_c_re, m_ssm_c_im, m_ssm_d, m_w_glu, m_b_glu, m_attn_sinks, m_g_ssm_out, m_g_attn_out, m_w_out, m_g_post_mix, m_g_pre_ffn, m_w_gate_up, m_w_down, m_g_post_ffn, v_g_pre_mix, v_w_in, v_ssm_lambda_re, v_ssm_lambda_im, v_ssm_log_dt, v_ssm_b_re, v_ssm_b_im, v_ssm_c_re, v_ssm_c_im, v_ssm_d, v_w_glu, v_b_glu, v_attn_sinks, v_g_ssm_out, v_g_attn_out, v_w_out, v_g_post_mix, v_g_pre_ffn, v_w_gate_up, v_w_down, v_g_post_ffn):
    w = dict(g_pre_mix=g_pre_mix, w_in=w_in, ssm_lambda_re=ssm_lambda_re, ssm_lambda_im=ssm_lambda_im,
             ssm_log_dt=ssm_log_dt, ssm_b_re=ssm_b_re, ssm_b_im=ssm_b_im, ssm_c_re=ssm_c_re, ssm_c_im=ssm_c_im,
             ssm_d=ssm_d, w_glu=w_glu, b_glu=b_glu, attn_sinks=attn_sinks, g_ssm_out=g_ssm_out,
             g_attn_out=g_attn_out, w_out=w_out, g_post_mix=g_post_mix, g_pre_ffn=g_pre_ffn,
             w_gate_up=w_gate_up, w_down=w_down, g_post_ffn=g_post_ffn)
    m = dict(g_pre_mix=m_g_pre_mix, w_in=m_w_in, ssm_lambda_re=m_ssm_lambda_re, ssm_lambda_im=m_ssm_lambda_im,
             ssm_log_dt=m_ssm_log_dt, ssm_b_re=m_ssm_b_re, ssm_b_im=m_ssm_b_im, ssm_c_re=m_ssm_c_re,
             ssm_c_im=m_ssm_c_im, ssm_d=m_ssm_d, w_glu=m_w_glu, b_glu=m_b_glu, attn_sinks=m_attn_sinks,
             g_ssm_out=m_g_ssm_out, g_attn_out=m_g_attn_out, w_out=m_w_out, g_post_mix=m_g_post_mix,
             g_pre_ffn=m_g_pre_ffn, w_gate_up=m_w_gate_up, w_down=m_w_down, g_post_ffn=m_g_post_ffn)
    v = dict(g_pre_mix=v_g_pre_mix, w_in=v_w_in, ssm_lambda_re=v_ssm_lambda_re, ssm_lambda_im=v_ssm_lambda_im,
             ssm_log_dt=v_ssm_log_dt, ssm_b_re=v_ssm_b_re, ssm_b_im=v_ssm_b_im, ssm_c_re=v_ssm_c_re,
             ssm_c_im=v_ssm_c_im, ssm_d=v_ssm_d, w_glu=v_w_glu, b_glu=v_b_glu, attn_sinks=v_attn_sinks,
             g_ssm_out=v_g_ssm_out, g_attn_out=v_g_attn_out, w_out=v_w_out, g_post_mix=v_g_post_mix,
             g_pre_ffn=v_g_pre_ffn, w_gate_up=v_w_gate_up, w_down=v_w_down, g_post_ffn=v_g_post_ffn)

    transposed = ("w_in", "w_glu", "w_gate_up")
    native_transposed = ("w_in", "w_gate_up")
    shard = {n: (w[n][0].T if n in transposed else w[n][0]).astype(_BF16) for n in _BIG}
    gathered = {}
    for names, lands in (
            (("w_in",), _sequencer_exchange([shard["w_in"]], [False], "gather_w_in", 1)),
            (("w_glu", "w_out"), _sequencer_exchange([shard["w_glu"], shard["w_out"]], [False] * 2, "gather_mix", 2)),
            (("w_gate_up", "w_down"), _sequencer_gather([shard["w_gate_up"], shard["w_down"]], "gather_ffn", 3))):
        gathered.update({n: a.reshape(-1, a.shape[2]) for n, a in zip(names, lands)})

    def fetch(names, after):
        del after
        return [gathered[n] for n in names]

    sent = []

    def publish(named):
        big = [n for n in named if n in _BIG]
        rows = [n for n in named if n in _ROW_WIDTH]
        dense = [n for n in named if n in _DENSE]
        plain = [n for n in named if n not in big + rows + dense]
        sources = [named[n].reshape(N_DEV, -1, named[n].shape[1]) for n in big]
        slots = _row_slots(rows)
        if rows:
            sources.append(_stack_rows(named, slots))
        sources += [named[n].reshape(-1, _LANES) for n in dense] + [named[n] for n in plain]
        flags = [True] * len(big) + [False] * (len(sources) - len(big))
        cid = 4 + len(sent)
        sent.append((big, slots, dense, plain, _sequencer_exchange(sources, flags, "grads_%d" % cid, cid)))
        return [named[n] for n in big]

    p = {n: w[n] for n in _SMALL}
    grad_x = _local_step(x[0], positions[0], loss_target[0], p, fetch, publish)

    state = {n: [_to_2d(n, a) for a in (w[n], m[n], v[n])] for n in _SMALL}
    result = {}
    total_loss = None
    chain = []
    for big, slots, dense, plain, lands in sent:
        lands = list(lands)
        after = list(chain)
        for name in big:
            part = lands.pop(0)
            if name in native_transposed:
                updated = _adamw(part, w[name][0].T, m[name][0].T, v[name][0].T, "adamw_" + name, after)
                result[name] = [a.T[None] for a in updated]
                chain = [updated[3]]
                continue
            if name in transposed:
                part = _sum_parts(part, "sum_" + name, after).T[None]
            updated = _adamw(part, w[name][0], m[name][0], v[name][0], "adamw_" + name, after)
            result[name] = [a[None] for a in updated]
            chain = [updated[3]]
        parts, items, sums, names = [], [], [], []
        if slots:
            parts.append(lands.pop(0))
            for name, (row, col, _) in slots.items():
                if name == "loss":
                    sums.append((0, slice(row, row + 1), slice(col, col + _LANES)))
                else:
                    items.append((0, slice(row, row + 1), slice(col, col + _SHAPE_2D[name][1]), *state[name]))
                    names.append(name)
        for name in dense:
            part = lands.pop(0).reshape((N_DEV,) + _SHAPE_2D[name])
            result[name] = _adamw(part, *state[name], "adamw_" + name, after)
            chain = [result[name][3]]
        for name in plain:
            packed = _SSM_PACK if name == "ssm_pack" else {name: (0, _SHAPE_2D[name][0], 0, _SHAPE_2D[name][1])}
            for member, (first, rows_n, lane, cols_n) in packed.items():
                items.append((len(parts), slice(first, first + rows_n), slice(lane, lane + cols_n), *state[member]))
                names.append(member)
            parts.append(lands.pop(0))
        if items:
            updated, summed = _adamw_small(parts, items, sums, "adamw_small_" + names[0], after)
            chain = [updated[0][3]]
            result.update(dict(zip(names, updated)))
            if summed:
                total_loss = summed[0][0, 0]

    out = [total_loss, grad_x[None]]
    for kind in range(4):
        out += [_from_2d(n, result[n][kind], w[n].shape) for n in _WEIGHTS]
    return tuple(out)
```

```python
import functools
import math

import numpy as np
import jax
import jax.numpy as jnp
from jax import lax
from jax.experimental import pallas as pl
from jax.experimental.pallas import tpu as pltpu
from jax.experimental.pallas import tpu_sc as plsc

D_MODEL = 1024
SSM_WIDTH = 512
SSM_GROUP = 16
SSM_GROUPS = 32
SSM_STATE = 64
N_STATE = SSM_GROUPS * SSM_STATE
ATTN_WIDTH = 512
HEAD_DIM = 64
N_Q_HEADS = 8
N_KV_HEADS = 2
Q_PER_KV = 4
KV_WIDTH = 128
IN_WIDTH = 1280
BLOCK = 128
ROPE_DIM = 16
ROPE_THETA = 500000.0
D_FF = 2816
NORM_EPS = 1e-6
MASK_VALUE = -1e30
ADAM_LR = 0.001
ADAM_B1 = 0.9
ADAM_B2 = 0.999
ADAM_EPS = 1e-08
ADAM_WD = 0.01
ADAM_STEP = 10

N_DEV = 8
SCAN_CHUNKS = 8
SCAN_COLS = 512
TOKEN_TILE = 256
VMEM_LIMIT = 56 * 1024 * 1024

_F32 = jnp.float32
_BF16 = jnp.bfloat16
_MXU = jnp.bfloat16

_NN = ((1,), (0,))
_NT = ((1,), (1,))
_TN = ((0,), (0,))


def _dot(a, b, dims):
    return lax.dot_general(a.astype(_MXU), b.astype(_MXU), (dims, ((), ())),
                           preferred_element_type=_F32)


def _dot_exact(a, b, dims):
    return lax.dot_general(a.astype(_F32), b.astype(_F32), (dims, ((), ())),
                           precision=lax.Precision.HIGHEST, preferred_element_type=_F32)


def _iota(shape, dim):
    return lax.broadcasted_iota(jnp.int32, shape, dim)


def _rms_fwd(x, g):
    r = lax.rsqrt(jnp.mean(x * x, axis=-1, keepdims=True) + NORM_EPS)
    return x * r * g, r


def _rms_bwd(dy, x, g, r):
    a = dy * g
    xn = x * r
    dx = r * (a - xn * jnp.mean(a * xn, axis=-1, keepdims=True))
    dg = jnp.sum(dy * xn, axis=0, keepdims=True)
    return dx, dg


def _call(body, grid, in_specs, out_specs, out_shape, name, scratch=(), tokens=()):
    params = pltpu.CompilerParams(dimension_semantics=("arbitrary",) * len(grid),
                                  vmem_limit_bytes=VMEM_LIMIT)
    n_in, n_tok = len(in_specs), len(tokens)

    def run(*refs):
        return body(*refs[:n_in], *refs[n_in + n_tok:])

    call = pl.pallas_call(run, grid=grid,
                          in_specs=list(in_specs) + [pl.BlockSpec(memory_space=pl.ANY)] * n_tok,
                          out_specs=out_specs, out_shape=out_shape, scratch_shapes=list(scratch),
                          compiler_params=params, name=name)
    return lambda *args: call(*args, *tokens)


def _rows(tm, n):
    return pl.BlockSpec((tm, n), lambda i: (i, 0))


def _whole(shape):
    nd = len(shape)
    return pl.BlockSpec(shape, lambda i: (0,) * nd)


def _sds(shape, dtype):
    return jax.ShapeDtypeStruct(shape, dtype)


def _tile(L):
    return min(TOKEN_TILE, L)


def _accumulate(ref, val, first):
    @pl.when(first)
    def _():
        ref[...] = val

    @pl.when(jnp.logical_not(first))
    def _():
        ref[...] += val


def _rope_rows():
    half = ROPE_DIM // 2
    inv = (np.float32(ROPE_THETA) ** (-np.arange(half, dtype=np.float32) * np.float32(2.0) / np.float32(ROPE_DIM))).astype(np.float32)
    col = np.arange(KV_WIDTH) % HEAD_DIM
    freq = np.where(col < ROPE_DIM, inv[col % half], 0.0).astype(np.float32)
    sign = np.where(col < half, -1.0, np.where(col < ROPE_DIM, 1.0, 0.0)).astype(np.float32)
    return freq[None, :], sign[None, :]


def _rope_tables(pos_col):
    L = pos_col.shape[0]
    tm = _tile(L)
    freq, sign = _rope_rows()

    def body(pos_ref, freq_ref, sign_ref, cos_ref, sin_ref):
        ang = pos_ref[...].astype(_F32) * freq_ref[...]
        cos_ref[...] = jnp.cos(ang)
        sin_ref[...] = jnp.sin(ang) * sign_ref[...]

    return _call(body, (L // tm,),
                 [_rows(tm, 1), _whole((1, KV_WIDTH)), _whole((1, KV_WIDTH))],
                 [_rows(tm, KV_WIDTH), _rows(tm, KV_WIDTH)],
                 [_sds((L, KV_WIDTH), _F32)] * 2, "rope_tables")(pos_col, jnp.asarray(freq), jnp.asarray(sign))


def _widen(t, width):
    return t if width == KV_WIDTH else jnp.concatenate([t] * (width // KV_WIDTH), axis=1)


def _rope_partner(t):
    w = t.shape[1]
    in_head = _iota((1, w), 1) & (HEAD_DIM - 1)
    second = jnp.where(in_head < ROPE_DIM, pltpu.roll(t, ROPE_DIM // 2, 1), 0.0)
    return jnp.where(in_head < ROPE_DIM // 2, pltpu.roll(t, w - ROPE_DIM // 2, 1), second)


def _rope_apply(t, cos_t, sin_t):
    w = t.shape[1]
    return t * _widen(cos_t, w) + _rope_partner(t) * _widen(sin_t, w)


def _rope_transpose(dt, cos_t, sin_t):
    w = dt.shape[1]
    return dt * _widen(cos_t, w) + _rope_partner(dt * _widen(sin_t, w))


def _in_proj(x, g_pre_mix, w_in, cos_t, sin_t):
    L = x.shape[0]
    tm = _tile(L)

    def body(x_ref, g_ref, w_ref, cos_ref, sin_ref, hn_ref, u_ref, q_ref, k_ref, v_ref):
        hn, _ = _rms_fwd(x_ref[...], g_ref[...])
        hn = hn.astype(_BF16)
        hn_ref[...] = hn
        proj = _dot(hn, w_ref[...], _NT)
        u_ref[...] = proj[:, :SSM_WIDTH]
        q = proj[:, SSM_WIDTH:SSM_WIDTH + ATTN_WIDTH]
        k = proj[:, SSM_WIDTH + ATTN_WIDTH:SSM_WIDTH + ATTN_WIDTH + KV_WIDTH]
        cos_v, sin_v = cos_ref[...], sin_ref[...]
        q_ref[...] = _rope_apply(q, cos_v, sin_v).astype(_BF16)
        k_ref[...] = _rope_apply(k, cos_v, sin_v).astype(_BF16)
        v_ref[...] = proj[:, SSM_WIDTH + ATTN_WIDTH + KV_WIDTH:].astype(_BF16)

    return _call(body, (L // tm,),
                 [_rows(tm, D_MODEL), _whole((1, D_MODEL)), _whole((IN_WIDTH, D_MODEL)),
                  _rows(tm, KV_WIDTH), _rows(tm, KV_WIDTH)],
                 [_rows(tm, D_MODEL), _rows(tm, SSM_WIDTH), _rows(tm, ATTN_WIDTH),
                  _rows(tm, KV_WIDTH), _rows(tm, KV_WIDTH)],
                 [_sds((L, D_MODEL), _BF16), _sds((L, SSM_WIDTH), _F32), _sds((L, ATTN_WIDTH), _BF16),
                  _sds((L, KV_WIDTH), _BF16), _sds((L, KV_WIDTH), _BF16)],
                 "in_proj")(x, g_pre_mix, w_in, cos_t, sin_t)


def _s5_discretize(lam_re, lam_im, log_dt):
    lr = jnp.minimum(lam_re, -1e-4)
    li = lam_im
    dt = jnp.exp(log_dt)
    mag = jnp.exp(lr * dt)
    ar = mag * jnp.cos(li * dt)
    ai = mag * jnp.sin(li * dt)
    den = lr * lr + li * li
    fr = ((ar - 1.0) * lr + ai * li) / den
    fi = (ai * lr - (ar - 1.0) * li) / den
    return ar, ai, fr, fi


def _s5_bbar(lam_re, lam_im, log_dt, b_re, b_im):
    ar, ai, fr, fi = _s5_discretize(lam_re, lam_im, log_dt)
    return ar, ai, fr * b_re - fi * b_im, fr * b_im + fi * b_re


def _spread_masks():
    e16 = (_iota((SSM_GROUP, SSM_WIDTH), 1) & (SSM_GROUP - 1)) == _iota((SSM_GROUP, SSM_WIDTH), 0)
    e64 = (_iota((SSM_STATE, N_STATE), 1) & (SSM_STATE - 1)) == _iota((SSM_STATE, N_STATE), 0)
    mask_b = (_iota((N_STATE, SSM_WIDTH), 0) >> 6) == (_iota((N_STATE, SSM_WIDTH), 1) >> 4)
    mask_c = (_iota((SSM_WIDTH, N_STATE), 0) >> 4) == (_iota((SSM_WIDTH, N_STATE), 1) >> 6)
    return e16.astype(_F32), e64.astype(_F32), mask_b, mask_c


SUPER = 4
SB_STATE = N_STATE // SUPER
SB_WIDTH = SSM_WIDTH // SUPER


def _sb_state(k):
    return slice(SB_STATE * k, SB_STATE * (k + 1))


def _sb_width(k):
    return slice(SB_WIDTH * k, SB_WIDTH * (k + 1))


def _dt_column(log_dt_row):
    eye = _iota((SSM_GROUPS, SSM_GROUPS), 0) == _iota((SSM_GROUPS, SSM_GROUPS), 1)
    return jnp.sum(jnp.where(eye, log_dt_row, 0.0), axis=1, keepdims=True)


def _group_masks():
    e64 = ((_iota((SSM_STATE, N_STATE), 1) & (SSM_STATE - 1)) == _iota((SSM_STATE, N_STATE), 0)).astype(_F32)
    own = _iota((SSM_GROUPS, N_STATE), 0) == (_iota((SSM_GROUPS, N_STATE), 1) >> 6)
    return e64, own


def _rows_of_group():
    return ((_iota((SSM_WIDTH, SSM_GROUPS), 0) >> 4) == _iota((SSM_WIDTH, SSM_GROUPS), 1)).astype(_F32)


def _ssm_prep(lam_re, lam_im, log_dt, b_re, b_im, c_re, c_im):
    def body(lr_ref, li_ref, ld_ref, bre, bim, cre, cim, ar_ref, ai_ref, btr, bti, ctr, cti):
        ar, ai, fr, fi = _s5_discretize(lr_ref[...], li_ref[...], _dt_column(ld_ref[...]))
        e64, own = _group_masks()
        mask_c = (_iota((SSM_WIDTH, N_STATE), 0) >> 4) == (_iota((SSM_WIDTH, N_STATE), 1) >> 6)

        def to_row(t):
            return jnp.sum(jnp.where(own, _dot_exact(t, e64, _NN), 0.0), axis=0, keepdims=True)

        def fold(m):
            full = jnp.where(mask_c, _dot(m, e64, _NN), 0.0)
            return sum(full[_sb_width(k), :] for k in range(SUPER)).astype(_BF16)

        ar_ref[...] = to_row(ar)
        ai_ref[...] = to_row(ai)
        spread = _rows_of_group()
        fr_t = _dot_exact(spread, fr, _NN)
        fi_t = _dot_exact(spread, fi, _NN)
        btr[...] = fold(fr_t * bre[...] - fi_t * bim[...])
        bti[...] = fold(fr_t * bim[...] + fi_t * bre[...])
        ctr[...] = fold(cre[...])
        cti[...] = fold(cim[...])

    row = (1, N_STATE)
    ins = [lam_re, lam_im, log_dt, b_re, b_im, c_re, c_im]
    return _call(body, (1,), [_whole(a.shape) for a in ins],
                 [_whole(row), _whole(row)] + [_whole((SB_WIDTH, N_STATE))] * 4,
                 [_sds(row, _F32), _sds(row, _F32)] + [_sds((SB_WIDTH, N_STATE), _BF16)] * 4,
                 "ssm_prep")(*ins)


def _ssm_bu(u, bt_re, bt_im):
    L = u.shape[0]
    tm = _tile(L)

    def body(u_ref, br_ref, bi_ref, or_ref, oi_ref):
        for k in range(SUPER):
            ub = u_ref[:, _sb_width(k)].astype(_BF16)
            or_ref[:, _sb_state(k)] = _dot(ub, br_ref[:, _sb_state(k)], _NN)
            oi_ref[:, _sb_state(k)] = _dot(ub, bi_ref[:, _sb_state(k)], _NN)

    return _call(body, (L // tm,),
                 [_rows(tm, SSM_WIDTH), _whole((SB_WIDTH, N_STATE)), _whole((SB_WIDTH, N_STATE))],
                 [_rows(tm, N_STATE), _rows(tm, N_STATE)],
                 [_sds((L, N_STATE), _F32)] * 2, "ssm_bu")(u, bt_re, bt_im)


def _complex_power(ar, ai, n):
    def step(_, c):
        pr, pi = c
        return pr * ar - pi * ai, pr * ai + pi * ar
    return lax.fori_loop(0, n, step, (jnp.ones_like(ar), jnp.zeros_like(ai)))


def _chunk_carries(er, ei, pr, pi, reverse):
    rows = _iota(er.shape, 0)
    sr = jnp.zeros_like(pr)
    si = jnp.zeros_like(pi)
    out_r = jnp.zeros_like(er)
    out_i = jnp.zeros_like(ei)
    order = range(SCAN_CHUNKS - 1, 0, -1) if reverse else range(SCAN_CHUNKS - 1)
    for c in order:
        e_r = er[c:c + 1, :]
        e_i = ei[c:c + 1, :]
        sr, si = pr * sr - pi * si + e_r, pr * si + pi * sr + e_i
        nxt = c - 1 if reverse else c + 1
        out_r = jnp.where(rows == nxt, sr, out_r)
        out_i = jnp.where(rows == nxt, si, out_i)
    return out_r, out_i


def _scan_fwd(b_re, b_im, a_re, a_im):
    T = b_re.shape[0]
    W = SCAN_COLS
    blk = pl.BlockSpec((T, SCAN_CHUNKS, W), lambda j: (0, 0, j))
    vec = pl.BlockSpec((1, W), lambda j: (0, j))

    def body(br_ref, bi_ref, ar_ref, ai_ref, xr_ref, xi_ref):
        ar, ai = ar_ref[...], ai_ref[...]
        ar8 = jnp.broadcast_to(ar, (SCAN_CHUNKS, W))
        ai8 = jnp.broadcast_to(ai, (SCAN_CHUNKS, W))

        def local(t, c):
            cr, ci = c
            return ar8 * cr - ai8 * ci + br_ref[t], ar8 * ci + ai8 * cr + bi_ref[t]

        zero = jnp.zeros((SCAN_CHUNKS, W), _F32)
        er, ei = lax.fori_loop(0, T, local, (zero, zero))
        pr, pi = _complex_power(ar, ai, T)
        sr, si = _chunk_carries(er, ei, pr, pi, reverse=False)

        def final(t, c):
            nr, ni = local(t, c)
            xr_ref[t] = nr
            xi_ref[t] = ni
            return nr, ni

        lax.fori_loop(0, T, final, (sr, si))

    shape = _sds(b_re.shape, _F32)
    return _call(body, (N_STATE // W,), [blk, blk, vec, vec], [blk, blk], [shape, shape],
                 "scan_fwd")(b_re, b_im, a_re, a_im)


def _scan_bwd(dx_re, dx_im, x_re, x_im, a_re, a_im, tokens=()):
    T = dx_re.shape[0]
    W = SCAN_COLS
    blk = pl.BlockSpec((T, SCAN_CHUNKS, W), lambda j: (0, 0, j))
    vec = pl.BlockSpec((1, W), lambda j: (0, j))

    def body(dr_ref, di_ref, xr_ref, xi_ref, ar_ref, ai_ref, lr_ref, li_ref, dar_ref, dai_ref):
        ar, ai = ar_ref[...], ai_ref[...]
        ar8 = jnp.broadcast_to(ar, (SCAN_CHUNKS, W))
        ai8 = jnp.broadcast_to(ai, (SCAN_CHUNKS, W))

        def local(t, c):
            cr, ci = c
            return ar8 * cr + ai8 * ci + dr_ref[t], ar8 * ci - ai8 * cr + di_ref[t]

        zero = jnp.zeros((SCAN_CHUNKS, W), _F32)
        er, ei = lax.fori_loop(0, T, lambda k, c: local(T - 1 - k, c), (zero, zero))
        pr, pi = _complex_power(ar, -ai, T)
        sr, si = _chunk_carries(er, ei, pr, pi, reverse=True)

        def grad_a(acc, nr, ni, xpr, xpi):
            return acc[0] + nr * xpr + ni * xpi, acc[1] + ni * xpr - nr * xpi

        def final(k, c):
            t = T - 1 - k
            nr, ni = local(t, c[:2])
            lr_ref[t] = nr
            li_ref[t] = ni
            gr, gi = grad_a(c[2:], nr, ni, xr_ref[t - 1], xi_ref[t - 1])
            return nr, ni, gr, gi

        cr, ci, gr, gi = lax.fori_loop(0, T - 1, final, (sr, si, zero, zero))
        nr, ni = local(0, (cr, ci))
        lr_ref[0] = nr
        li_ref[0] = ni
        first = _iota((SCAN_CHUNKS, W), 0) == 0
        xpr = jnp.where(first, 0.0, pltpu.roll(xr_ref[T - 1], 1, 0))
        xpi = jnp.where(first, 0.0, pltpu.roll(xi_ref[T - 1], 1, 0))
        gr, gi = grad_a((gr, gi), nr, ni, xpr, xpi)
        dar_ref[...] = jnp.sum(gr, axis=0, keepdims=True)
        dai_ref[...] = jnp.sum(gi, axis=0, keepdims=True)

    shape = _sds(dx_re.shape, _F32)
    row = _sds((1, N_STATE), _F32)
    return _call(body, (N_STATE // W,), [blk, blk, blk, blk, vec, vec], [blk, blk, vec, vec],
                 [shape, shape, row, row], "scan_bwd", tokens=tokens)(dx_re, dx_im, x_re, x_im, a_re, a_im)


_GELU_K = math.sqrt(2.0 / math.pi)
_GELU_C = 0.044715


def _gelu(y):
    return 0.5 * y * (1.0 + jnp.tanh(_GELU_K * (y + _GELU_C * y * y * y)))


def _gelu_grad(y):
    t = jnp.tanh(_GELU_K * (y + _GELU_C * y * y * y))
    return 0.5 * (1.0 + t) + 0.5 * y * (1.0 - t * t) * _GELU_K * (1.0 + 3.0 * _GELU_C * y * y)


def _ssm_out(x_re, x_im, u, ct_re, ct_im, d_row, w_glu, b_glu, g_ssm):
    L = u.shape[0]
    tm = _tile(L)

    def body(xr_ref, xi_ref, u_ref, cr_ref, ci_ref, d_ref, w_ref, b_ref, g_ref, y_ref, z_ref, n_ref):
        cx = [_dot(xr_ref[:, _sb_state(k)], cr_ref[:, _sb_state(k)], _NT)
              - _dot(xi_ref[:, _sb_state(k)], ci_ref[:, _sb_state(k)], _NT) for k in range(SUPER)]
        y = jnp.concatenate(cx, axis=1) + d_ref[...] * u_ref[...]
        y_ref[...] = y
        z = _dot(_gelu(y), w_ref[...], _NT) + b_ref[...]
        z_ref[...] = z
        out = z[:, :SSM_WIDTH] * jax.nn.sigmoid(z[:, SSM_WIDTH:])
        n, _ = _rms_fwd(out, g_ref[...])
        n_ref[...] = n.astype(_BF16)

    return _call(body, (L // tm,),
                 [_rows(tm, N_STATE), _rows(tm, N_STATE), _rows(tm, SSM_WIDTH),
                  _whole((SB_WIDTH, N_STATE)), _whole((SB_WIDTH, N_STATE)), _whole((1, SSM_WIDTH)),
                  _whole((2 * SSM_WIDTH, SSM_WIDTH)), _whole((1, 2 * SSM_WIDTH)), _whole((1, SSM_WIDTH))],
                 [_rows(tm, SSM_WIDTH), _rows(tm, 2 * SSM_WIDTH), _rows(tm, SSM_WIDTH)],
                 [_sds((L, SSM_WIDTH), _F32), _sds((L, 2 * SSM_WIDTH), _F32), _sds((L, SSM_WIDTH), _BF16)],
                 "ssm_out")(x_re, x_im, u, ct_re, ct_im, d_row, w_glu, b_glu, g_ssm)


def _ssm_out_bwd(dn, y, z, u, ct_re, ct_im, d_row, w_glu, g_ssm):
    L = u.shape[0]
    tm = _tile(L)

    def body(dn_ref, y_ref, z_ref, u_ref, cr_ref, ci_ref, d_ref, w_ref, g_ref,
             gy_ref, dz_ref, dy_ref, dud_ref, dxr_ref, dxi_ref, dg_ref, db_ref, dd_ref):
        first = pl.program_id(0) == 0
        z = z_ref[...]
        z1, z2 = z[:, :SSM_WIDTH], z[:, SSM_WIDTH:]
        sig = jax.nn.sigmoid(z2)
        out = z1 * sig
        g = g_ref[...]
        _, r = _rms_fwd(out, g)
        dout, dg = _rms_bwd(dn_ref[...], out, g, r)
        _accumulate(dg_ref, dg, first)
        dz = jnp.concatenate([dout * sig, dout * z1 * sig * (1.0 - sig)], axis=1)
        _accumulate(db_ref, jnp.sum(dz, axis=0, keepdims=True), first)
        dzb = dz.astype(_BF16)
        dz_ref[...] = dzb
        y = y_ref[...]
        gy_ref[...] = _gelu(y).astype(_BF16)
        dy = _dot(dzb, w_ref[...], _NN) * _gelu_grad(y)
        u = u_ref[...]
        _accumulate(dd_ref, jnp.sum(dy * u, axis=0, keepdims=True), first)
        dud_ref[...] = d_ref[...] * dy
        dyb = dy.astype(_BF16)
        dy_ref[...] = dyb
        for k in range(SUPER):
            dxr_ref[:, _sb_state(k)] = _dot(dyb[:, _sb_width(k)], cr_ref[:, _sb_state(k)], _NN)
            dxi_ref[:, _sb_state(k)] = -_dot(dyb[:, _sb_width(k)], ci_ref[:, _sb_state(k)], _NN)

    row = _whole((1, SSM_WIDTH))
    return _call(body, (L // tm,),
                 [_rows(tm, SSM_WIDTH), _rows(tm, SSM_WIDTH), _rows(tm, 2 * SSM_WIDTH), _rows(tm, SSM_WIDTH),
                  _whole((SB_WIDTH, N_STATE)), _whole((SB_WIDTH, N_STATE)), row,
                  _whole((2 * SSM_WIDTH, SSM_WIDTH)), row],
                 [_rows(tm, SSM_WIDTH), _rows(tm, 2 * SSM_WIDTH), _rows(tm, SSM_WIDTH), _rows(tm, SSM_WIDTH),
                  _rows(tm, N_STATE), _rows(tm, N_STATE), row, _whole((1, 2 * SSM_WIDTH)), row],
                 [_sds((L, SSM_WIDTH), _BF16), _sds((L, 2 * SSM_WIDTH), _BF16), _sds((L, SSM_WIDTH), _BF16),
                  _sds((L, SSM_WIDTH), _F32), _sds((L, N_STATE), _F32), _sds((L, N_STATE), _F32),
                  _sds((1, SSM_WIDTH), _F32), _sds((1, 2 * SSM_WIDTH), _F32), _sds((1, SSM_WIDTH), _F32)],
                 "ssm_out_bwd")(dn, y, z, u, ct_re, ct_im, d_row, w_glu, g_ssm)


def _ssm_du(lam_re, lam_im, bt_re, bt_im, dud):
    L = dud.shape[0]
    tm = _tile(L)

    def body(lr_ref, li_ref, br_ref, bi_ref, dud_ref, du_ref):
        for k in range(SUPER):
            du_ref[:, _sb_width(k)] = (_dot(lr_ref[:, _sb_state(k)], br_ref[:, _sb_state(k)], _NT)
                                       + _dot(li_ref[:, _sb_state(k)], bi_ref[:, _sb_state(k)], _NT)
                                       + dud_ref[:, _sb_width(k)])

    return _call(body, (L // tm,),
                 [_rows(tm, N_STATE), _rows(tm, N_STATE), _whole((SB_WIDTH, N_STATE)),
                  _whole((SB_WIDTH, N_STATE)), _rows(tm, SSM_WIDTH)],
                 _rows(tm, SSM_WIDTH), _sds((L, SSM_WIDTH), _F32), "ssm_du")(lam_re, lam_im, bt_re, bt_im, dud)


def _ssm_weight_grads(dy, x_re, x_im, lam_re, lam_im, u):
    L = u.shape[0]

    def body(dy_ref, xr_ref, xi_ref, lr_ref, li_ref, u_ref, dcr_ref, dci_ref, dbr_ref, dbi_ref):
        dyb = dy_ref[...]
        ub = u_ref[...].astype(_BF16)
        dcr_ref[...] = _dot(dyb, xr_ref[...], _TN)
        dci_ref[...] = _dot(dyb, xi_ref[...], _TN)
        dbr_ref[...] = _dot(ub, lr_ref[...], _TN)
        dbi_ref[...] = _dot(ub, li_ref[...], _TN)

    width = pl.BlockSpec((L, SB_WIDTH), lambda k: (0, k))
    state = pl.BlockSpec((L, SB_STATE), lambda k: (0, k))
    out = pl.BlockSpec((SB_WIDTH, SB_STATE), lambda k: (0, k))
    return _call(body, (SUPER,), [width, state, state, state, state, width], [out] * 4,
                 [_sds((SB_WIDTH, N_STATE), _F32)] * 4,
                 "ssm_weight_grads")(dy, x_re, x_im, lam_re, lam_im, u)


_SSM_PACK = {"ssm_b_re": (0, SSM_WIDTH, 0, SSM_STATE), "ssm_c_re": (0, SSM_WIDTH, 64, SSM_STATE),
             "ssm_b_im": (512, SSM_WIDTH, 0, SSM_STATE), "ssm_c_im": (512, SSM_WIDTH, 64, SSM_STATE),
             "ssm_lambda_re": (1024, SSM_GROUPS, 0, SSM_STATE), "ssm_lambda_im": (1024, SSM_GROUPS, 64, SSM_STATE),
             "ssm_d": (1056, SSM_GROUPS, 0, SSM_GROUP), "ssm_log_dt": (1088, 1, 0, SSM_GROUPS)}
_PACK_TILE = 16
_SSM_PACK_ROWS = 1088 + _PACK_TILE


def _ssm_param_bwd(da_re, da_im, dbt_re, dbt_im, dct_re, dct_im, lam_re, lam_im, log_dt, b_re, b_im, g_d):
    def body(dar, dai, dbr, dbi, dcr, dci, lr_ref, li_ref, ld_ref, bre_ref, bim_ref, gd_ref, pack_ref):
        lane_in = _iota((SSM_STATE, _LANES), 0)
        lane_out = _iota((SSM_STATE, _LANES), 1)
        low = (lane_out == lane_in).astype(_F32)
        high = (lane_out == lane_in + SSM_STATE).astype(_F32)

        def side_by_side(a, b):
            return _dot_exact(a, low, _NN) + _dot_exact(b, high, _NN)

        tail = _SSM_PACK["ssm_d"][0]
        pack_ref[tail:, :] = jnp.zeros((_SSM_PACK_ROWS - tail, _LANES), _BF16)
        pack_ref[tail:tail + SSM_GROUPS, 0:SSM_GROUP] = gd_ref[...].astype(_BF16)
        own_c = (_iota((SB_WIDTH, SB_STATE), 0) >> 4) == (_iota((SB_WIDTH, SB_STATE), 1) >> 6)

        def unfold(ref):
            blocks = []
            for k in range(SUPER):
                t = jnp.where(own_c, ref[:, _sb_state(k)], 0.0)
                t = sum(t[:, 128 * i:128 * (i + 1)] for i in range(SB_STATE // 128))
                blocks.append((t + pltpu.roll(t, SSM_STATE, 1))[:, :SSM_STATE])
            return jnp.concatenate(blocks, axis=0)

        dbb_re, dbb_im = unfold(dbr), unfold(dbi)
        b_re, b_im = bre_ref[...], bim_ref[...]
        dt_col = _dt_column(ld_ref[...])
        (_, _, fr, fi), vjp = jax.vjp(_s5_discretize, lr_ref[...], li_ref[...], dt_col)
        spread = _rows_of_group()
        fr_t = _dot_exact(spread, fr, _NN)
        fi_t = _dot_exact(spread, fi, _NN)
        pack_ref[0:SSM_WIDTH, :] = side_by_side(fr_t * dbb_re + fi_t * dbb_im, unfold(dcr)).astype(_BF16)
        pack_ref[SSM_WIDTH:2 * SSM_WIDTH, :] = side_by_side(fr_t * dbb_im - fi_t * dbb_re, -unfold(dci)).astype(_BF16)
        d_fr = _dot_exact(spread, dbb_re * b_re + dbb_im * b_im, _TN)
        d_fi = _dot_exact(spread, dbb_im * b_re - dbb_re * b_im, _TN)
        e64, own = _group_masks()

        def from_row(ref):
            return _dot_exact(jnp.where(own, ref[...], 0.0), e64, _NT)

        d_lr, d_li, d_dt = vjp((from_row(dar), from_row(dai), d_fr, d_fi))
        lam_rows = _SSM_PACK["ssm_lambda_re"][0]
        pack_ref[lam_rows:lam_rows + SSM_GROUPS, :] = side_by_side(d_lr, d_li).astype(_BF16)
        eye = (_iota((SSM_GROUPS, SSM_GROUPS), 0) == _iota((SSM_GROUPS, SSM_GROUPS), 1)).astype(_F32)
        dt_row = _SSM_PACK["ssm_log_dt"][0]
        pack_ref[dt_row:dt_row + _PACK_TILE, 0:SSM_GROUPS] = _dot_exact(
            jnp.broadcast_to(d_dt, (SSM_GROUPS, 128)), eye, _TN)[0:_PACK_TILE].astype(_BF16)

    ins = [da_re, da_im, dbt_re, dbt_im, dct_re, dct_im, lam_re, lam_im, log_dt, b_re, b_im, g_d]
    out = (_SSM_PACK_ROWS, _LANES)
    return _call(body, (1,), [_whole(a.shape) for a in ins], _whole(out), _sds(out, _BF16), "ssm_param_bwd")(*ins)


def _head_spread(j):
    r = _iota((KV_WIDTH, 256), 0)
    c = _iota((KV_WIDTH, 256), 1)
    return (r == HEAD_DIM * j + (c & (HEAD_DIM - 1))).astype(_BF16)


STACK = Q_PER_KV * BLOCK


def _stack_heads(t):
    lane_head = _iota((1, 256), 1) >> 6
    return jnp.concatenate([jnp.where(lane_head == g, t, jnp.zeros_like(t)) for g in range(Q_PER_KV)], axis=0)


def _unstack_heads(t):
    lane_head = _iota((1, 256), 1) >> 6
    return sum(jnp.where(lane_head == g, t[BLOCK * g:BLOCK * (g + 1)], 0.0) for g in range(Q_PER_KV))


def _stacked_sinks(sink_ref, j):
    block = _iota((STACK, 1), 0) >> 7
    col = jnp.full((STACK, 1), sink_ref[Q_PER_KV * j], _F32)
    for g in range(1, Q_PER_KV):
        col = jnp.where(block == g, sink_ref[Q_PER_KV * j + g], col)
    return col


def _fold_heads(t, j):
    t = t[:, :KV_WIDTH] + t[:, KV_WIDTH:]
    t = t + pltpu.roll(t, HEAD_DIM, 1)
    return jnp.where((_iota((1, KV_WIDTH), 1) >> 6) == j, t, 0.0)


def _attn_scores(q_stacked, kt, blk, sink):
    s = _dot(q_stacked, kt, _NT) * (HEAD_DIM ** -0.5)
    qi = _iota((STACK, 2 * BLOCK), 0) & (BLOCK - 1)
    kj = _iota((STACK, 2 * BLOCK), 1)
    rel = qi + BLOCK - kj
    valid = (rel >= 0) & (rel < BLOCK) & (blk * BLOCK - BLOCK + kj >= 0)
    s = jnp.where(valid, s, MASK_VALUE)
    m = jnp.maximum(jnp.max(s, axis=-1, keepdims=True), sink)
    p = jnp.exp(s - m)
    e_sink = jnp.exp(sink - m)
    den = jnp.sum(p, axis=-1, keepdims=True) + e_sink
    return p / den, e_sink / den


def _attn_specs():
    prev = lambda i: (jnp.maximum(i - 1, 0), 0)
    cur = lambda i: (i, 0)
    kv = [pl.BlockSpec((BLOCK, KV_WIDTH), prev), pl.BlockSpec((BLOCK, KV_WIDTH), cur)]
    return [pl.BlockSpec((BLOCK, ATTN_WIDTH), cur)] + kv + kv


def _attn_fwd(q, k, v, sinks, g_attn):
    L = q.shape[0]

    def body(q_ref, kp_ref, kc_ref, vp_ref, vc_ref, sink_ref, g_ref, o_ref, n_ref):
        blk = pl.program_id(0)
        kwin = jnp.concatenate([kp_ref[...], kc_ref[...]], axis=0)
        vwin = jnp.concatenate([vp_ref[...], vc_ref[...]], axis=0)
        halves = []
        for j in range(N_KV_HEADS):
            spread = _head_spread(j)
            kt = _dot(kwin, spread, _NN).astype(_BF16)
            vt = _dot(vwin, spread, _NN).astype(_BF16)
            qs = _stack_heads(q_ref[:, 256 * j:256 * (j + 1)])
            p, _ = _attn_scores(qs, kt, blk, _stacked_sinks(sink_ref, j))
            halves.append(_unstack_heads(_dot(p, vt, _NN)))
        o = jnp.concatenate(halves, axis=1)
        o_ref[...] = o
        n, _ = _rms_fwd(o, g_ref[...])
        n_ref[...] = n.astype(_BF16)

    cur = lambda i: (i, 0)
    return _call(body, (L // BLOCK,),
                 _attn_specs() + [pl.BlockSpec(memory_space=pltpu.SMEM), _whole((1, ATTN_WIDTH))],
                 [pl.BlockSpec((BLOCK, ATTN_WIDTH), cur)] * 2,
                 [_sds((L, ATTN_WIDTH), _F32), _sds((L, ATTN_WIDTH), _BF16)],
                 "attn_fwd")(q, k, k, v, v, sinks, g_attn)


def _attn_bwd(q, k, v, o, dn, sinks, g_attn):
    L = q.shape[0]

    def body(q_ref, kp_ref, kc_ref, vp_ref, vc_ref, o_ref, dn_ref, sink_ref, g_ref,
             dq_ref, dk_ref, dv_ref, dsink_ref, dg_ref):
        blk = pl.program_id(0)
        first = blk == 0

        @pl.when(first)
        def _():
            dk_ref[...] = jnp.zeros_like(dk_ref)
            dv_ref[...] = jnp.zeros_like(dv_ref)
            dsink_ref[...] = jnp.zeros_like(dsink_ref)

        o = o_ref[...]
        g = g_ref[...]
        _, r = _rms_fwd(o, g)
        do, dg = _rms_bwd(dn_ref[...], o, g, r)
        _accumulate(dg_ref, dg, first)
        kwin = jnp.concatenate([kp_ref[...], kc_ref[...]], axis=0)
        vwin = jnp.concatenate([vp_ref[...], vc_ref[...]], axis=0)
        lane = _iota((1, 128), 1)
        dsink = jnp.zeros((1, 128), _F32)
        dkwin = jnp.zeros((2 * BLOCK, KV_WIDTH), _F32)
        dvwin = jnp.zeros((2 * BLOCK, KV_WIDTH), _F32)
        dq_halves = []
        for j in range(N_KV_HEADS):
            spread = _head_spread(j)
            kt = _dot(kwin, spread, _NN).astype(_BF16)
            vt = _dot(vwin, spread, _NN).astype(_BF16)
            qs = _stack_heads(q_ref[:, 256 * j:256 * (j + 1)])
            dos = _stack_heads(do[:, 256 * j:256 * (j + 1)]).astype(_BF16)
            p, p_sink = _attn_scores(qs, kt, blk, _stacked_sinks(sink_ref, j))
            dp = _dot(dos, vt, _NT)
            delta = jnp.sum(p * dp, axis=-1, keepdims=True)
            ds = (p * (dp - delta) * (HEAD_DIM ** -0.5)).astype(_BF16)
            sink_term = p_sink * delta
            for g in range(Q_PER_KV):
                head_sum = jnp.sum(sink_term[BLOCK * g:BLOCK * (g + 1)], axis=0, keepdims=True)
                dsink = dsink - jnp.where(lane == Q_PER_KV * j + g, head_sum, 0.0)
            dvwin = dvwin + _fold_heads(_dot(p, dos, _TN), j)
            dkwin = dkwin + _fold_heads(_dot(ds, qs, _TN), j)
            dq_halves.append(_unstack_heads(_dot(ds, kt, _NN)))
        dq_ref[...] = jnp.concatenate(dq_halves, axis=1)
        dsink_ref[...] += dsink
        prev = pl.ds(pl.multiple_of(jnp.maximum(blk - 1, 0) * BLOCK, BLOCK), BLOCK)
        cur = pl.ds(pl.multiple_of(blk * BLOCK, BLOCK), BLOCK)
        dk_ref[prev, :] += dkwin[:BLOCK]
        dk_ref[cur, :] += dkwin[BLOCK:]
        dv_ref[prev, :] += dvwin[:BLOCK]
        dv_ref[cur, :] += dvwin[BLOCK:]

    cur = lambda i: (i, 0)
    blk_q = pl.BlockSpec((BLOCK, ATTN_WIDTH), cur)
    return _call(body, (L // BLOCK,),
                 _attn_specs() + [blk_q, blk_q, pl.BlockSpec(memory_space=pltpu.SMEM), _whole((1, ATTN_WIDTH))],
                 [blk_q, _whole((L, KV_WIDTH)), _whole((L, KV_WIDTH)), _whole((1, 128)), _whole((1, ATTN_WIDTH))],
                 [_sds((L, ATTN_WIDTH), _F32), _sds((L, KV_WIDTH), _F32), _sds((L, KV_WIDTH), _F32),
                  _sds((1, 128), _F32), _sds((1, ATTN_WIDTH), _F32)],
                 "attn_bwd")(q, k, k, v, v, o, dn, sinks, g_attn)


def _out_proj(n_ssm, n_attn, x, w_out, g_post_mix, g_pre_ffn):
    L = x.shape[0]
    tm = _tile(L)

    def body(ns_ref, na_ref, x_ref, w_ref, g1_ref, g2_ref, merged_ref, mo_ref, h1_ref, hn2_ref):
        merged = jnp.concatenate([ns_ref[...], na_ref[...]], axis=1)
        merged_ref[...] = merged
        mo = _dot(merged, w_ref[...], _NN)
        mo_ref[...] = mo
        n, _ = _rms_fwd(mo, g1_ref[...])
        h1 = x_ref[...] + n
        h1_ref[...] = h1
        hn2, _ = _rms_fwd(h1, g2_ref[...])
        hn2_ref[...] = hn2.astype(_BF16)

    row = _whole((1, D_MODEL))
    return _call(body, (L // tm,),
                 [_rows(tm, SSM_WIDTH), _rows(tm, ATTN_WIDTH), _rows(tm, D_MODEL), _whole((D_MODEL, D_MODEL)), row, row],
                 [_rows(tm, D_MODEL)] * 4,
                 [_sds((L, D_MODEL), _BF16), _sds((L, D_MODEL), _F32), _sds((L, D_MODEL), _F32), _sds((L, D_MODEL), _BF16)],
                 "out_proj")(n_ssm, n_attn, x, w_out, g_post_mix, g_pre_ffn)


def _ffn(hn2, h1, target, w_gate_up, w_down, g_pre_ffn, g_post_ffn):
    L = h1.shape[0]
    tm = _tile(L)
    half = D_FF // 2

    def body(hn2_ref, h1_ref, tgt_ref, wgu_hbm, wd_hbm, g2_ref, g3_ref,
             act_ref, dgu_ref, dff_ref, dh1_ref, loss_ref, dg3_ref, dg2_ref,
             wgu, wd, gu, sem):
        first = pl.program_id(0) == 0

        @pl.when(first)
        def _():
            c1 = pltpu.make_async_copy(wgu_hbm, wgu, sem.at[0])
            c2 = pltpu.make_async_copy(wd_hbm, wd, sem.at[1])
            c1.start()
            c2.start()
            c1.wait()
            c2.wait()

        hn2 = hn2_ref[...]
        ff = jnp.zeros((tm, D_MODEL), _F32)
        for c in range(2):
            gate = _dot(hn2, wgu[half * c:half * (c + 1), :], _NT)
            up = _dot(hn2, wgu[D_FF + half * c:D_FF + half * (c + 1), :], _NT)
            gu[:, half * c:half * (c + 1)] = gate
            gu[:, D_FF + half * c:D_FF + half * (c + 1)] = up
            act = gate * jax.nn.sigmoid(gate) * up
            act_ref[half * c:half * (c + 1), :] = act.T.astype(_BF16)
            ff = ff + _dot(act, wd[half * c:half * (c + 1), :], _NN)
        g3 = g3_ref[...]
        n, r = _rms_fwd(ff, g3)
        h1 = h1_ref[...]
        err = h1 + n - tgt_ref[...]
        loss = 0.5 * jnp.sum(jnp.mean(err * err, axis=-1, keepdims=True), axis=0, keepdims=True)
        _accumulate(loss_ref, jnp.broadcast_to(loss, (1, 128)), first)
        dh2 = err * (1.0 / D_MODEL)
        dff, dg3 = _rms_bwd(dh2, ff, g3, r)
        _accumulate(dg3_ref, dg3, first)
        dffb = dff.astype(_BF16)
        dff_ref[...] = dffb
        dhn2 = jnp.zeros((tm, D_MODEL), _F32)
        for c in range(2):
            dact = _dot(dffb, wd[half * c:half * (c + 1), :], _NT)
            gate = gu[:, half * c:half * (c + 1)]
            up = gu[:, D_FF + half * c:D_FF + half * (c + 1)]
            sig = jax.nn.sigmoid(gate)
            silu = gate * sig
            dgate = dact * up * (sig + silu * (1.0 - sig))
            dup = dact * silu
            dgu_ref[half * c:half * (c + 1), :] = dgate.T.astype(_BF16)
            dgu_ref[D_FF + half * c:D_FF + half * (c + 1), :] = dup.T.astype(_BF16)
            dhn2 = dhn2 + _dot(dgate, wgu[half * c:half * (c + 1), :], _NN)
            dhn2 = dhn2 + _dot(dup, wgu[D_FF + half * c:D_FF + half * (c + 1), :], _NN)
        g2 = g2_ref[...]
        _, r2 = _rms_fwd(h1, g2)
        dh1, dg2 = _rms_bwd(dhn2, h1, g2, r2)
        _accumulate(dg2_ref, dg2, first)
        dh1_ref[...] = dh2 + dh1

    row = _whole((1, D_MODEL))
    anyspace = pl.BlockSpec(memory_space=pl.ANY)
    return _call(body, (L // tm,),
                 [_rows(tm, D_MODEL), _rows(tm, D_MODEL), _rows(tm, D_MODEL), anyspace, anyspace, row, row],
                 [pl.BlockSpec((D_FF, tm), lambda i: (0, i)), pl.BlockSpec((2 * D_FF, tm), lambda i: (0, i)),
                  _rows(tm, D_MODEL), _rows(tm, D_MODEL), _whole((1, 128)), row, row],
                 [_sds((D_FF, L), _BF16), _sds((2 * D_FF, L), _BF16), _sds((L, D_MODEL), _BF16),
                  _sds((L, D_MODEL), _F32), _sds((1, 128), _F32), _sds((1, D_MODEL), _F32), _sds((1, D_MODEL), _F32)],
                 "ffn",
                 scratch=[pltpu.VMEM((2 * D_FF, D_MODEL), _BF16), pltpu.VMEM((D_FF, D_MODEL), _BF16),
                          pltpu.VMEM((tm, 2 * D_FF), _F32), pltpu.SemaphoreType.DMA((2,))],
                 )(hn2, h1, target, w_gate_up, w_down, g_pre_ffn, g_post_ffn)


def _out_proj_bwd(dh1, mo, w_out, g_post_mix, tokens=()):
    L = dh1.shape[0]
    tm = _tile(L)

    def body(dh1_ref, mo_ref, w_ref, g_ref, dmo_ref, dns_ref, dna_ref, dg_ref):
        first = pl.program_id(0) == 0
        mo = mo_ref[...]
        g = g_ref[...]
        _, r = _rms_fwd(mo, g)
        dmo, dg = _rms_bwd(dh1_ref[...], mo, g, r)
        _accumulate(dg_ref, dg, first)
        dmob = dmo.astype(_BF16)
        dmo_ref[...] = dmob
        dmerged = _dot(dmob, w_ref[...], _NT)
        dns_ref[...] = dmerged[:, :SSM_WIDTH]
        dna_ref[...] = dmerged[:, SSM_WIDTH:]

    row = _whole((1, D_MODEL))
    return _call(body, (L // tm,),
                 [_rows(tm, D_MODEL), _rows(tm, D_MODEL), _whole((D_MODEL, D_MODEL)), row],
                 [_rows(tm, D_MODEL), _rows(tm, SSM_WIDTH), _rows(tm, ATTN_WIDTH), row],
                 [_sds((L, D_MODEL), _BF16), _sds((L, SSM_WIDTH), _F32), _sds((L, ATTN_WIDTH), _F32),
                  _sds((1, D_MODEL), _F32)],
                 "out_proj_bwd", tokens=tokens)(dh1, mo, w_out, g_post_mix)


def _in_proj_bwd(du, dq, dk, dv, cos_t, sin_t, x, dh1, g_pre_mix, w_in, tokens=()):
    L = x.shape[0]
    tm = _tile(L)

    def body(du_ref, dq_ref, dk_ref, dv_ref, cos_ref, sin_ref, x_ref, dh1_ref, g_ref, w_ref,
             dproj_ref, dx_ref, dg_ref):
        first = pl.program_id(0) == 0
        cos_v, sin_v = cos_ref[...], sin_ref[...]
        dproj = jnp.concatenate([du_ref[...], _rope_transpose(dq_ref[...], cos_v, sin_v),
                                 _rope_transpose(dk_ref[...], cos_v, sin_v), dv_ref[...]], axis=1).astype(_BF16)
        dproj_ref[...] = dproj
        dhn = _dot(dproj, w_ref[...], _NN)
        x = x_ref[...]
        g = g_ref[...]
        _, r = _rms_fwd(x, g)
        dx, dg = _rms_bwd(dhn, x, g, r)
        _accumulate(dg_ref, dg, first)
        dx_ref[...] = dh1_ref[...] + dx

    row = _whole((1, D_MODEL))
    return _call(body, (L // tm,),
                 [_rows(tm, SSM_WIDTH), _rows(tm, ATTN_WIDTH), _rows(tm, KV_WIDTH), _rows(tm, KV_WIDTH),
                  _rows(tm, KV_WIDTH), _rows(tm, KV_WIDTH), _rows(tm, D_MODEL), _rows(tm, D_MODEL), row,
                  _whole((IN_WIDTH, D_MODEL))],
                 [_rows(tm, IN_WIDTH), _rows(tm, D_MODEL), row],
                 [_sds((L, IN_WIDTH), _BF16), _sds((L, D_MODEL), _F32), _sds((1, D_MODEL), _F32)],
                 "in_proj_bwd", tokens=tokens)(du, dq, dk, dv, cos_t, sin_t, x, dh1, g_pre_mix, w_in)


def _matmul_nn(a, b, out_dtype, name):
    M, K = a.shape
    N = b.shape[1]
    tm = next(t for t in (512, 256, 128) if M % t == 0)
    tn = next(t for t in (512, 256, 128) if N % t == 0)

    def body(a_ref, b_ref, o_ref):
        o_ref[...] = _dot(a_ref[...], b_ref[...], _NN).astype(out_dtype)

    params = pltpu.CompilerParams(dimension_semantics=("arbitrary", "arbitrary"), vmem_limit_bytes=VMEM_LIMIT)
    return pl.pallas_call(body, grid=(M // tm, N // tn),
                          in_specs=[pl.BlockSpec((tm, K), lambda i, j: (i, 0)),
                                    pl.BlockSpec((K, tn), lambda i, j: (0, j))],
                          out_specs=pl.BlockSpec((tm, tn), lambda i, j: (i, j)),
                          out_shape=_sds((M, N), out_dtype), compiler_params=params, name=name)(a, b)


def _matmul_tn(a, b, out_dtype, name, scale=1.0):
    K, M = a.shape
    N = b.shape[1]
    tm = next(t for t in (512, 256, 128) if M % t == 0)
    tn = next(t for t in (512, 256, 128) if N % t == 0)

    def body(a_ref, b_ref, o_ref):
        acc = _dot(a_ref[...], b_ref[...], _TN)
        o_ref[...] = (acc if scale == 1.0 else acc * scale).astype(out_dtype)

    params = pltpu.CompilerParams(dimension_semantics=("arbitrary", "arbitrary"), vmem_limit_bytes=VMEM_LIMIT)
    return pl.pallas_call(body, grid=(M // tm, N // tn),
                          in_specs=[pl.BlockSpec((K, tm), lambda i, j: (0, i)),
                                    pl.BlockSpec((K, tn), lambda i, j: (0, j))],
                          out_specs=pl.BlockSpec((tm, tn), lambda i, j: (i, j)),
                          out_shape=_sds((M, N), out_dtype), compiler_params=params, name=name)(a, b)


def _to_chunked(a):
    L, n = a.shape
    return a.reshape(SCAN_CHUNKS, L // SCAN_CHUNKS, n).transpose(1, 0, 2).reshape(L, n)


def _from_chunked(a):
    L, n = a.shape
    return a.reshape(L // SCAN_CHUNKS, SCAN_CHUNKS, n).transpose(1, 0, 2).reshape(L, n)


def _local_step(x, pos, target, p, fetch, publish):
    L = x.shape[0]
    T = L // SCAN_CHUNKS
    cos_t, sin_t = _rope_tables(pos.reshape(L, 1))
    w_in, = fetch(("w_in",), None)
    hn, u, q, k, v = _in_proj(x, p["g_pre_mix"], w_in, cos_t, sin_t)

    ssm = {n: _to_2d(n, p[n]) for n in ("ssm_lambda_re", "ssm_lambda_im", "ssm_log_dt", "ssm_b_re", "ssm_b_im",
                                        "ssm_c_re", "ssm_c_im")}
    d_row = p["ssm_d"].reshape(1, SSM_WIDTH)
    a_re, a_im, bt_re, bt_im, ct_re, ct_im = _ssm_prep(
        ssm["ssm_lambda_re"], ssm["ssm_lambda_im"], ssm["ssm_log_dt"], ssm["ssm_b_re"], ssm["ssm_b_im"],
        ssm["ssm_c_re"], ssm["ssm_c_im"])

    u_c = _to_chunked(u)
    bu_re, bu_im = _ssm_bu(u_c, bt_re, bt_im)
    x_re, x_im = _scan_fwd(bu_re.reshape(T, SCAN_CHUNKS, N_STATE), bu_im.reshape(T, SCAN_CHUNKS, N_STATE), a_re, a_im)
    w_glu, = fetch(("w_glu",), x_re)
    y, z, n_ssm_c = _ssm_out(x_re.reshape(L, N_STATE), x_im.reshape(L, N_STATE), u_c, ct_re, ct_im, d_row,
                             w_glu, p["b_glu"], p["g_ssm_out"])
    n_ssm = _from_chunked(n_ssm_c)

    sinks = p["attn_sinks"].reshape(N_Q_HEADS)
    o, n_attn = _attn_fwd(q, k, v, sinks, p["g_attn_out"])
    w_out, = fetch(("w_out",), n_attn)
    merged, mo, h1, hn2 = _out_proj(n_ssm, n_attn, x, w_out, p["g_post_mix"], p["g_pre_ffn"])
    w_gate_up, w_down = fetch(("w_gate_up", "w_down"), hn2)
    act_t, dgu_t, dff, dh1, loss, dg_post_ffn, dg_pre_ffn = _ffn(
        hn2, h1, target, w_gate_up, w_down, p["g_pre_ffn"], p["g_post_ffn"])
    grads = {"g_post_ffn": dg_post_ffn, "g_pre_ffn": dg_pre_ffn}
    tokens = publish({"w_down": _matmul_nn(act_t, dff, _BF16, "grad_w_down"),
                      "w_gate_up": _matmul_nn(dgu_t, hn2, _BF16, "grad_w_gate_up")})

    dmo, dn_ssm, dn_attn, grads["g_post_mix"] = _out_proj_bwd(dh1, mo, w_out, p["g_post_mix"], tokens)
    grad_w_out = _matmul_tn(merged, dmo, _BF16, "grad_w_out")

    dq, dk, dv, dsink, grads["g_attn_out"] = _attn_bwd(q, k, v, o, dn_attn, sinks, p["g_attn_out"])
    grads["attn_sinks"] = dsink

    gy, dz, dy, dud, dx_re, dx_im, grads["g_ssm_out"], grads["b_glu"], dd = _ssm_out_bwd(
        _to_chunked(dn_ssm), y, z, u_c, ct_re, ct_im, d_row, w_glu, p["g_ssm_out"])
    grads.update(w_out=grad_w_out, w_glu=_matmul_tn(dz, gy, _BF16, "grad_w_glu"))
    lam_re, lam_im, da_re, da_im = _scan_bwd(dx_re.reshape(T, SCAN_CHUNKS, N_STATE), dx_im.reshape(T, SCAN_CHUNKS, N_STATE),
                                             x_re, x_im, a_re, a_im, [grads["w_out"], grads["w_glu"]])
    lam_re = lam_re.reshape(L, N_STATE)
    lam_im = lam_im.reshape(L, N_STATE)
    dct_re, dct_im, dbt_re, dbt_im = _ssm_weight_grads(
        dy, x_re.reshape(L, N_STATE), x_im.reshape(L, N_STATE), lam_re, lam_im, u_c)
    ssm_pack = _ssm_param_bwd(
        da_re, da_im, dbt_re, dbt_im, dct_re, dct_im,
        ssm["ssm_lambda_re"], ssm["ssm_lambda_im"], ssm["ssm_log_dt"], ssm["ssm_b_re"], ssm["ssm_b_im"],
        dd.reshape(SSM_GROUPS, SSM_GROUP))
    grads.update(ssm_pack=ssm_pack, loss=loss)
    publish(grads)

    du = _from_chunked(_ssm_du(lam_re, lam_im, bt_re, bt_im, dud))
    dproj, grad_x, g_pre_mix = _in_proj_bwd(du, dq, dk, dv, cos_t, sin_t, x, dh1, p["g_pre_mix"], w_in, [ssm_pack])
    publish({"g_pre_mix": g_pre_mix, "w_in": _matmul_tn(dproj, hn, _BF16, "grad_w_in")})
    return grad_x


_MESH = pl.DeviceIdType.MESH
_PEERS = N_DEV - 1


def _mesh_pos():
    return lax.axis_index("x"), lax.axis_index("y"), lax.axis_index("c")


def _dev_index(px, py, pc):
    return 4 * px + 2 * py + pc


def _all_gather(shards, out_dtype, name):
    n = len(shards)

    def body(*refs):
        ins, outs, stages = refs[:n], refs[n:2 * n], refs[2 * n:3 * n]
        send_sems, recv_sems, local_sems = refs[3 * n:]
        x, y, c = _mesh_pos()
        me, sibling = (x, y, c), (x, y, 1 - c)
        chips = [(1 - x, y), (x, 1 - y), (1 - x, 1 - y)]

        def copy(w, k, block, to, src=None):
            slot = outs[w].at[_dev_index(*block)]
            return pltpu.make_async_remote_copy(
                src_ref=slot if src is None else src, dst_ref=slot,
                send_sem=send_sems.at[_PEERS * w + k], recv_sem=recv_sems.at[_PEERS * w + k],
                device_id=to, device_id_type=_MESH)

        for w in range(n):
            stages[w][...] = ins[w][...].astype(out_dtype)
        mine, first, passed = [], [], []
        for w in range(n):
            cp = pltpu.make_async_copy(stages[w], outs[w].at[_dev_index(*me)], local_sems.at[w])
            cp.start()
            mine.append(cp)
            sends = [copy(w, 0, me, sibling, src=stages[w])]
            sends += [copy(w, 1 + j, me, (*chip, c), src=stages[w]) for j, chip in enumerate(chips)]
            for cp in sends:
                cp.start()
            first += sends
        for w in range(n):
            for j, chip in enumerate(chips):
                copy(w, 1 + j, (*chip, c), me).wait_recv()
                cp = copy(w, 4 + j, (*chip, c), sibling)
                cp.start()
                passed.append(cp)
        for w in range(n):
            copy(w, 0, sibling, me).wait_recv()
            for j, chip in enumerate(chips):
                copy(w, 4 + j, (*chip, 1 - c), me).wait_recv()
        for cp in first + passed:
            cp.wait_send()
        for cp in mine:
            cp.wait()

    return pl.pallas_call(
        body, name=name,
        out_shape=[_sds((N_DEV,) + s.shape, out_dtype) for s in shards],
        in_specs=[pl.BlockSpec(memory_space=pltpu.VMEM)] * n,
        out_specs=[pl.BlockSpec(memory_space=pl.ANY)] * n,
        scratch_shapes=[pltpu.VMEM(s.shape, out_dtype) for s in shards]
        + [pltpu.SemaphoreType.DMA((_PEERS * n,)), pltpu.SemaphoreType.DMA((_PEERS * n,)),
           pltpu.SemaphoreType.DMA((n,))],
        compiler_params=pltpu.CompilerParams(vmem_limit_bytes=VMEM_LIMIT),
    )(*shards)


_HBM_SPEC = pl.BlockSpec(memory_space=pltpu.HBM)
_SEM_SPEC = pl.BlockSpec(memory_space=pltpu.SEMAPHORE)
_DATAFLOW = pltpu.SideEffectType.DATAFLOW_SIDE_EFFECTING


def _peer(x, y, c, r):
    return (x ^ ((r >> 2) & 1), y ^ ((r >> 1) & 1), c ^ (r & 1))


def _hbm(a):
    return pltpu.with_memory_space_constraint(a, pltpu.HBM)


def _send_start(sources, blocked, name):
    n = len(sources)
    lands = [lax.empty((N_DEV,) + (s.shape[1:] if blocked else s.shape), s.dtype) for s in sources]

    def body(*refs):
        srcs, zones = refs[:n], refs[n:2 * n]
        send_sems, recv_sems = refs[2 * n:3 * n], refs[3 * n:4 * n]
        token, local_sems = refs[6 * n], refs[6 * n + 1]
        x, y, c = _mesh_pos()
        me = _dev_index(x, y, c)
        local = []
        for w in range(n):
            cp = pltpu.make_async_copy(srcs[w].at[me] if blocked else srcs[w], zones[w].at[me], local_sems.at[w])
            cp.start()
            local.append(cp)
            for r in range(1, N_DEV):
                peer = _peer(x, y, c, r)
                pltpu.make_async_remote_copy(
                    src_ref=srcs[w].at[_dev_index(*peer)] if blocked else srcs[w], dst_ref=zones[w].at[me],
                    send_sem=send_sems[w].at[r - 1], recv_sem=recv_sems[w].at[r - 1],
                    device_id=peer, device_id_type=_MESH).start()
        for cp in local:
            cp.wait()
        token[...] = jnp.zeros_like(token)

    sems = [pltpu.SemaphoreType.DMA((_PEERS,))] * (2 * n)
    out = pl.pallas_call(
        body, name=name,
        out_shape=sems + [pltpu.HBM(a.shape, a.dtype) for a in list(sources) + lands] + [_sds((8, 128), _F32)],
        in_specs=[_HBM_SPEC] * (2 * n),
        out_specs=[_SEM_SPEC] * (2 * n) + [_HBM_SPEC] * (2 * n) + [pl.BlockSpec(memory_space=pltpu.VMEM)],
        input_output_aliases={i: 2 * n + i for i in range(2 * n)},
        scratch_shapes=[pltpu.SemaphoreType.DMA((n,))],
        compiler_params=pltpu.CompilerParams(has_side_effects=_DATAFLOW),
    )(*[_hbm(a) for a in sources], *[_hbm(a) for a in lands])
    return out[:n], out[n:2 * n], out[2 * n:3 * n], out[3 * n:4 * n], out[4 * n]


def _send_wait(send_sems, recv_sems, sources, lands, after, blocked, name):
    n = len(sources)

    def body(*refs):
        srcs, zones = refs[:n], refs[n:2 * n]
        sends, recvs = refs[2 * n:3 * n], refs[3 * n:4 * n]
        x, y, c = _mesh_pos()
        for w in range(n):
            for r in range(1, N_DEV):
                peer = _peer(x, y, c, r)
                idx = _dev_index(*peer)
                cp = pltpu.make_async_remote_copy(
                    src_ref=srcs[w].at[idx] if blocked else srcs[w], dst_ref=zones[w].at[idx],
                    send_sem=sends[w].at[r - 1], recv_sem=recvs[w].at[r - 1],
                    device_id=peer, device_id_type=_MESH)
                cp.wait_send()
                cp.wait_recv()

    out = pl.pallas_call(
        body, name=name,
        out_shape=[pltpu.HBM(a.shape, a.dtype) for a in list(sources) + list(lands)],
        in_specs=[_HBM_SPEC] * (2 * n) + [_SEM_SPEC] * (2 * n) + [pl.BlockSpec(memory_space=pl.ANY)],
        out_specs=[_HBM_SPEC] * (2 * n),
        input_output_aliases={i: i for i in range(2 * n)},
        compiler_params=pltpu.CompilerParams(has_side_effects=_DATAFLOW),
    )(*sources, *lands, *send_sems, *recv_sems, after)
    return out[n:]


def _sequencer_exchange(sources, blocked, name, collective_id):
    n = len(sources)
    flags = blocked

    def body(*refs):
        srcs, zones = refs[:n], refs[n:2 * n]
        send_sems, recv_sems, local_sems = refs[2 * n:]
        x, y, c = _mesh_pos()
        me = _dev_index(x, y, c)
        barrier = pltpu.get_barrier_semaphore()
        for r in range(1, N_DEV):
            pl.semaphore_signal(barrier, inc=1, device_id=_peer(x, y, c, r), device_id_type=_MESH)
        pl.semaphore_wait(barrier, _PEERS)
        local, sends, recvs = [], [], []
        for w in range(n):
            cp = pltpu.make_async_copy(srcs[w].at[me] if flags[w] else srcs[w], zones[w].at[me], local_sems.at[w])
            cp.start()
            local.append(cp)
            for r in range(1, N_DEV):
                peer = _peer(x, y, c, r)
                idx = _dev_index(*peer)
                k = _PEERS * w + r - 1
                src = srcs[w].at[idx] if flags[w] else srcs[w]
                send = pltpu.make_async_remote_copy(
                    src_ref=src, dst_ref=zones[w].at[me], send_sem=send_sems.at[k], recv_sem=recv_sems.at[k],
                    device_id=peer, device_id_type=_MESH)
                send.start()
                sends.append(send)
                recvs.append(pltpu.make_async_remote_copy(
                    src_ref=src, dst_ref=zones[w].at[idx], send_sem=send_sems.at[k], recv_sem=recv_sems.at[k],
                    device_id=peer, device_id_type=_MESH))
        for cp in recvs:
            cp.wait_recv()
        for cp in sends:
            cp.wait_send()
        for cp in local:
            cp.wait()

    return pl.kernel(
        body, name=name,
        out_type=[_sds((N_DEV,) + (s.shape[1:] if f else s.shape), s.dtype) for s, f in zip(sources, flags)],
        mesh=plsc.ScalarSubcoreMesh(axis_name="sequencer", num_cores=1),
        scratch_types=[pltpu.SemaphoreType.DMA((_PEERS * n,)), pltpu.SemaphoreType.DMA((_PEERS * n,)),
                       pltpu.SemaphoreType.DMA((n,))],
        compiler_params=pltpu.CompilerParams(collective_id=collective_id),
    )(*sources)


def _sequencer_gather(shards, name, collective_id):
    n = len(shards)
    fan = 4

    def body(*refs):
        srcs, zones = refs[:n], refs[n:2 * n]
        send_sems, recv_sems, local_sems = refs[2 * n:]
        x, y, c = _mesh_pos()
        me, sibling = (x, y, c), (x, y, 1 - c)
        chips = [(1 - x, y), (x, 1 - y), (1 - x, 1 - y)]
        barrier = pltpu.get_barrier_semaphore()
        for peer in [sibling] + [(*chip, c) for chip in chips]:
            pl.semaphore_signal(barrier, inc=1, device_id=peer, device_id_type=_MESH)
        pl.semaphore_wait(barrier, fan)

        def copy(w, k, block, to, src=None):
            slot = zones[w].at[_dev_index(*block)]
            return pltpu.make_async_remote_copy(
                src_ref=slot if src is None else src, dst_ref=slot,
                send_sem=send_sems.at[_PEERS * w + k], recv_sem=recv_sems.at[_PEERS * w + k],
                device_id=to, device_id_type=_MESH)

        mine, first, passed = [], [], []
        for w in range(n):
            cp = pltpu.make_async_copy(srcs[w], zones[w].at[_dev_index(*me)], local_sems.at[w])
            cp.start()
            mine.append(cp)
            sends = [copy(w, 0, me, sibling, src=srcs[w])]
            sends += [copy(w, 1 + j, me, (*chip, c), src=srcs[w]) for j, chip in enumerate(chips)]
            for cp in sends:
                cp.start()
            first += sends
        for w in range(n):
            for j, chip in enumerate(chips):
                copy(w, 1 + j, (*chip, c), me).wait_recv()
                cp = copy(w, fan + j, (*chip, c), sibling)
                cp.start()
                passed.append(cp)
        for w in range(n):
            copy(w, 0, sibling, me).wait_recv()
            for j, chip in enumerate(chips):
                copy(w, fan + j, (*chip, 1 - c), me).wait_recv()
        for cp in first + passed:
            cp.wait_send()
        for cp in mine:
            cp.wait()

    return pl.kernel(
        body, name=name, out_type=[_sds((N_DEV,) + s.shape, s.dtype) for s in shards],
        mesh=plsc.ScalarSubcoreMesh(axis_name="sequencer", num_cores=1),
        scratch_types=[pltpu.SemaphoreType.DMA((_PEERS * n,)), pltpu.SemaphoreType.DMA((_PEERS * n,)),
                       pltpu.SemaphoreType.DMA((n,))],
        compiler_params=pltpu.CompilerParams(collective_id=collective_id),
    )(*shards)


def _row_tile(rows):
    return next(t for t in range(min(rows, 256), 0, -16) if rows % t == 0)


def _sum_parts(parts, name, tokens=()):
    _, rows, cols = parts.shape
    tr = _row_tile(rows)

    def body(p_ref, g_ref):
        g = p_ref[0].astype(_F32)
        for s in range(1, N_DEV):
            g = g + p_ref[s].astype(_F32)
        g_ref[...] = g

    return _call(body, (rows // tr,), [pl.BlockSpec((N_DEV, tr, cols), lambda i: (0, i, 0))],
                 _rows(tr, cols), _sds((rows, cols), _F32), name, tokens=tokens)(parts)


def _adam_update(g, w, m, v):
    new_m = ADAM_B1 * m + (1.0 - ADAM_B1) * g
    new_v = ADAM_B2 * v + (1.0 - ADAM_B2) * (g * g)
    m_hat = new_m / (1.0 - ADAM_B1 ** ADAM_STEP)
    v_hat = new_v / (1.0 - ADAM_B2 ** ADAM_STEP)
    return -ADAM_LR * (m_hat / (jnp.sqrt(v_hat) + ADAM_EPS) + ADAM_WD * w), new_m, new_v


def _adamw_small(parts, items, sums, name, tokens=()):
    n_p, n_i = len(parts), len(items)

    def body(*refs):
        p_refs, state, outs = refs[:n_p], refs[n_p:n_p + 3 * n_i], refs[n_p + 3 * n_i:]

        def total(part, rows, cols):
            shift = cols.start % _LANES
            window = slice(cols.start - shift, cols.start - shift + _LANES) if shift else cols
            n_rows = rows.stop - rows.start
            narrow = p_refs[part].dtype.itemsize < 4 and n_rows % _PACK_TILE
            tile = slice(rows.start, rows.start + _PACK_TILE) if narrow else rows
            g = p_refs[part][0, tile, window].astype(_F32)
            for s in range(1, N_DEV):
                g = g + p_refs[part][s, tile, window].astype(_F32)
            g = g[:n_rows] if narrow else g
            return pltpu.roll(g, _LANES - shift, 1)[:, :cols.stop - cols.start] if shift else g

        for i, (part, rows, cols, _, _, _) in enumerate(items):
            g = total(part, rows, cols)
            w_ref, m_ref, v_ref = state[3 * i:3 * i + 3]
            delta, new_m, new_v = _adam_update(g, w_ref[...], m_ref[...], v_ref[...])
            outs[4 * i][...] = g
            outs[4 * i + 1][...] = delta
            outs[4 * i + 2][...] = new_m
            outs[4 * i + 3][...] = new_v
        for j, (part, rows, cols) in enumerate(sums):
            outs[4 * n_i + j][...] = total(part, rows, cols)

    ins = list(parts) + [a for item in items for a in item[3:]]
    out_shapes = [item[3].shape for item in items for _ in range(4)]
    out_shapes += [(rows.stop - rows.start, cols.stop - cols.start) for _, rows, cols in sums]
    out = _call(body, (1,), [_whole(a.shape) for a in ins], [_whole(s) for s in out_shapes],
                [_sds(s, _F32) for s in out_shapes], name, tokens=tokens)(*ins)
    return [out[4 * i:4 * i + 4] for i in range(n_i)], out[4 * n_i:]


def _adamw(parts, w, m, v, name, tokens=()):
    rows, cols = w.shape
    tr = _row_tile(rows)
    n_parts = parts.shape[0]

    def body(p_ref, w_ref, m_ref, v_ref, g_ref, d_ref, nm_ref, nv_ref):
        g = p_ref[0].astype(_F32)
        for s in range(1, n_parts):
            g = g + p_ref[s].astype(_F32)
        new_m = ADAM_B1 * m_ref[...] + (1.0 - ADAM_B1) * g
        new_v = ADAM_B2 * v_ref[...] + (1.0 - ADAM_B2) * (g * g)
        m_hat = new_m / (1.0 - ADAM_B1 ** ADAM_STEP)
        v_hat = new_v / (1.0 - ADAM_B2 ** ADAM_STEP)
        g_ref[...] = g
        d_ref[...] = -ADAM_LR * (m_hat / (jnp.sqrt(v_hat) + ADAM_EPS) + ADAM_WD * w_ref[...])
        nm_ref[...] = new_m
        nv_ref[...] = new_v

    blk = _rows(tr, cols)
    return _call(body, (rows // tr,),
                 [pl.BlockSpec((n_parts, tr, cols), lambda i: (0, i, 0)), blk, blk, blk],
                 [blk] * 4, [_sds((rows, cols), _F32)] * 4, name, tokens=tokens)(parts, w, m, v)


_SMALL = ("g_pre_mix", "ssm_lambda_re", "ssm_lambda_im", "ssm_log_dt", "ssm_b_re", "ssm_b_im",
          "ssm_c_re", "ssm_c_im", "ssm_d", "b_glu", "attn_sinks", "g_ssm_out", "g_attn_out",
          "g_post_mix", "g_pre_ffn", "g_post_ffn")
_BIG = ("w_in", "w_glu", "w_out", "w_gate_up", "w_down")
_WEIGHTS = ("g_pre_mix", "w_in", "ssm_lambda_re", "ssm_lambda_im", "ssm_log_dt", "ssm_b_re", "ssm_b_im",
            "ssm_c_re", "ssm_c_im", "ssm_d", "w_glu", "b_glu", "attn_sinks", "g_ssm_out", "g_attn_out",
            "w_out", "g_post_mix", "g_pre_ffn", "w_gate_up", "w_down", "g_post_ffn")
_LANES = 128


_SHAPE_2D = {
    "g_pre_mix": (1, D_MODEL), "ssm_lambda_re": (SSM_GROUPS, SSM_STATE), "ssm_lambda_im": (SSM_GROUPS, SSM_STATE),
    "ssm_log_dt": (1, SSM_GROUPS), "ssm_b_re": (SSM_WIDTH, SSM_STATE), "ssm_b_im": (SSM_WIDTH, SSM_STATE),
    "ssm_c_re": (SSM_WIDTH, SSM_STATE), "ssm_c_im": (SSM_WIDTH, SSM_STATE), "ssm_d": (SSM_GROUPS, SSM_GROUP),
    "b_glu": (1, 2 * SSM_WIDTH), "attn_sinks": (1, N_Q_HEADS), "g_ssm_out": (1, SSM_WIDTH),
    "g_attn_out": (1, ATTN_WIDTH), "g_post_mix": (1, D_MODEL), "g_pre_ffn": (1, D_MODEL), "g_post_ffn": (1, D_MODEL)}
_ROW_WIDTH = {"g_pre_mix": D_MODEL, "b_glu": 2 * SSM_WIDTH, "attn_sinks": _LANES, "g_ssm_out": SSM_WIDTH,
              "g_attn_out": ATTN_WIDTH, "g_post_mix": D_MODEL, "g_pre_ffn": D_MODEL, "g_post_ffn": D_MODEL,
              "loss": _LANES}
_DENSE = ()
_PER_GROUP_TRANSPOSED = ("ssm_b_re", "ssm_b_im")


def _to_2d(name, a):
    if name in _PER_GROUP_TRANSPOSED:
        a = a.reshape(SSM_GROUPS, SSM_STATE, SSM_GROUP).transpose(0, 2, 1)
    return a.reshape(_SHAPE_2D[name])


def _from_2d(name, a, shape):
    if name in _PER_GROUP_TRANSPOSED:
        a = a.reshape(SSM_GROUPS, SSM_GROUP, SSM_STATE).transpose(0, 2, 1)
    return a.reshape(shape)


def _row_slots(names):
    slots, row, col = {}, 0, 0
    for n in names:
        width = _ROW_WIDTH[n]
        if col + width > D_MODEL:
            row, col = row + 1, 0
        slots[n] = (row, col, width)
        col += width
    return slots


def _stack_rows(named, slots):
    n_rows = -(-(max(r for r, _, _ in slots.values()) + 1) // 8) * 8
    lines = []
    for r in range(n_rows):
        pieces = [named[n] for n, (row, _, _) in slots.items() if row == r]
        used = sum(p.shape[1] for p in pieces)
        if used < D_MODEL:
            pieces.append(jnp.zeros((1, D_MODEL - used), _F32))
        lines.append(jnp.concatenate(pieces, axis=1) if len(pieces) > 1 else pieces[0])
    return jnp.concatenate(lines, axis=0)


def kernel(x, positions, g_pre_mix, w_in, ssm_lambda_re, ssm_lambda_im, ssm_log_dt, ssm_b_re, ssm_b_im, ssm_c_re, ssm_c_im, ssm_d, w_glu, b_glu, attn_sinks, g_ssm_out, g_attn_out, w_out, g_post_mix, g_pre_ffn, w_gate_up, w_down, g_post_ffn, loss_target, m_g_pre_mix, m_w_in, m_ssm_lambda_re, m_ssm_lambda_im, m_ssm_log_dt, m_ssm_b_re, m_ssm_b_im, m_ssm_c_re, m_ssm_c_im, m_ssm_d, m_w_glu, m_b_glu, m_attn_sinks, m_g_ssm_out, m_g_attn_out, m_w_out, m_g_post_mix, m_g_pre_ffn, m_w_gate_up, m_w_down, m_g_post_ffn, v_g_pre_mix, v_w_in, v_ssm_lambda_re, v_ssm_lambda_im, v_ssm_log_dt, v_ssm_b_re, v_ssm_b_im, v_ssm_c_re, v_ssm_c_im, v_ssm_d, v_w_glu, v_b_glu, v_attn_sinks, v_g_ssm_out, v_g_attn_out, v_w_out, v_g_post_mix, v_g_pre_ffn, v_w_gate_up, v_w_down, v_g_post_ffn):
    w = dict(g_pre_mix=g_pre_mix, w_in=w_in, ssm_lambda_re=ssm_lambda_re, ssm_lambda_im=ssm_lambda_im,
             ssm_log_dt=ssm_log_dt, ssm_b_re=ssm_b_re, ssm_b_im=ssm_b_im, ssm_c_re=ssm_c_re, ssm_c_im=ssm_c_im,
             ssm_d=ssm_d, w_glu=w_glu, b_glu=b_glu, attn_sinks=attn_sinks, g_ssm_out=g_ssm_out,
             g_attn_out=g_attn_out, w_out=w_out, g_post_mix=g_post_mix, g_pre_ffn=g_pre_ffn,
             w_gate_up=w_gate_up, w_down=w_down, g_post_ffn=g_post_ffn)
    m = dict(g_pre_mix=m_g_pre_mix, w_in=m_w_in, ssm_lambda_re=m_ssm_lambda_re, ssm_lambda_im=m_ssm_lambda_im,
             ssm_log_dt=m_ssm_log_dt, ssm_b_re=m_ssm_b_re, ssm_b_im=m_ssm_b_im, ssm_c_re=m_ssm_c_re,
             ssm_c_im=m_ssm_c_im, ssm_d=m_ssm_d, w_glu=m_w_glu, b_glu=m_b_glu, attn_sinks=m_attn_sinks,
             g_ssm_out=m_g_ssm_out, g_attn_out=m_g_attn_out, w_out=m_w_out, g_post_mix=m_g_post_mix,
             g_pre_ffn=m_g_pre_ffn, w_gate_up=m_w_gate_up, w_down=m_w_down, g_post_ffn=m_g_post_ffn)
    v = dict(g_pre_mix=v_g_pre_mix, w_in=v_w_in, ssm_lambda_re=v_ssm_lambda_re, ssm_lambda_im=v_ssm_lambda_im,
             ssm_log_dt=v_ssm_log_dt, ssm_b_re=v_ssm_b_re, ssm_b_im=v_ssm_b_im, ssm_c_re=v_ssm_c_re,
             ssm_c_im=v_ssm_c_im, ssm_d=v_ssm_d, w_glu=v_w_glu, b_glu=v_b_glu, attn_sinks=v_attn_sinks,
             g_ssm_out=v_g_ssm_out, g_attn_out=v_g_attn_out, w_out=v_w_out, g_post_mix=v_g_post_mix,
             g_pre_ffn=v_g_pre_ffn, w_gate_up=v_w_gate_up, w_down=v_w_down, g_post_ffn=v_g_post_ffn)

    transposed = ("w_in", "w_glu", "w_gate_up")
    native_transposed = ("w_in", "w_gate_up")
    shard = {n: (w[n][0].T if n in transposed else w[n][0]).astype(_BF16) for n in _BIG}
    gathered = {}
    for names, lands in (
            (("w_in",), _sequencer_exchange([shard["w_in"]], [False], "gather_w_in", 1)),
            (("w_glu", "w_out"), _sequencer_exchange([shard["w_glu"], shard["w_out"]], [False] * 2, "gather_mix", 2)),
            (("w_gate_up", "w_down"), _sequencer_gather([shard["w_gate_up"], shard["w_down"]], "gather_ffn", 3))):
        gathered.update({n: a.reshape(-1, a.shape[2]) for n, a in zip(names, lands)})

    def fetch(names, after):
        del after
        return [gathered[n] for n in names]

    sent = []

    def publish(named):
        big = [n for n in named if n in _BIG]
        rows = [n for n in named if n in _ROW_WIDTH]
        dense = [n for n in named if n in _DENSE]
        plain = [n for n in named if n not in big + rows + dense]
        sources = [named[n].reshape(N_DEV, -1, named[n].shape[1]) for n in big]
        slots = _row_slots(rows)
        if rows:
            sources.append(_stack_rows(named, slots))
        sources += [named[n].reshape(-1, _LANES) for n in dense] + [named[n] for n in plain]
        flags = [True] * len(big) + [False] * (len(sources) - len(big))
        cid = 4 + len(sent)
        sent.append((big, slots, dense, plain, _sequencer_exchange(sources, flags, "grads_%d" % cid, cid)))
        return [named[n] for n in big]

    p = {n: w[n] for n in _SMALL}
    grad_x = _local_step(x[0], positions[0], loss_target[0], p, fetch, publish)

    state = {n: [_to_2d(n, a) for a in (w[n], m[n], v[n])] for n in _SMALL}
    result = {}
    total_loss = None
    chain = []
    for big, slots, dense, plain, lands in sent:
        lands = list(lands)
        after = list(chain)
        for name in big:
            part = lands.pop(0)
            if name in native_transposed:
                updated = _adamw(part, w[name][0].T, m[name][0].T, v[name][0].T, "adamw_" + name, after)
                result[name] = [a.T[None] for a in updated]
                chain = [updated[3]]
                continue
            if name in transposed:
                part = _sum_parts(part, "sum_" + name, after).T[None]
            updated = _adamw(part, w[name][0], m[name][0], v[name][0], "adamw_" + name, after)
            result[name] = [a[None] for a in updated]
            chain = [updated[3]]
        parts, items, sums, names = [], [], [], []
        if slots:
            parts.append(lands.pop(0))
            for name, (row, col, _) in slots.items():
                if name == "loss":
                    sums.append((0, slice(row, row + 1), slice(col, col + _LANES)))
                else:
                    items.append((0, slice(row, row + 1), slice(col, col + _SHAPE_2D[name][1]), *state[name]))
                    names.append(name)
        for name in dense:
            part = lands.pop(0).reshape((N_DEV,) + _SHAPE_2D[name])
            result[name] = _adamw(part, *state[name], "adamw_" + name, after)
            chain = [result[name][3]]
        for name in plain:
            packed = _SSM_PACK if name == "ssm_pack" else {name: (0, _SHAPE_2D[name][0], 0, _SHAPE_2D[name][1])}
            for member, (first, rows_n, lane, cols_n) in packed.items():
                items.append((len(parts), slice(first, first + rows_n), slice(lane, lane + cols_n), *state[member]))
                names.append(member)
            parts.append(lands.pop(0))
        if items:
            updated, summed = _adamw_small(parts, items, sums, "adamw_small_" + names[0], after)
            chain = [updated[0][3]]
            result.update(dict(zip(names, updated)))
            if summed:
                total_loss = summed[0][0, 0]

    out = [total_loss, grad_x[None]]
    for kind in range(4):
        out += [_from_2d(n, result[n][kind], w[n].shape) for n in _WEIGHTS]
    return tuple(out)
```

```python
import functools
import math

import numpy as np
import jax
import jax.numpy as jnp
from jax import lax
from jax.experimental import pallas as pl
from jax.experimental.pallas import tpu as pltpu
from jax.experimental.pallas import tpu_sc as plsc

D_MODEL = 1024
SSM_WIDTH = 512
SSM_GROUP = 16
SSM_GROUPS = 32
SSM_STATE = 64
N_STATE = SSM_GROUPS * SSM_STATE
ATTN_WIDTH = 512
HEAD_DIM = 64
N_Q_HEADS = 8
N_KV_HEADS = 2
Q_PER_KV = 4
KV_WIDTH = 128
IN_WIDTH = 1280
BLOCK = 128
ROPE_DIM = 16
ROPE_THETA = 500000.0
D_FF = 2816
NORM_EPS = 1e-6
MASK_VALUE = -1e30
ADAM_LR = 0.001
ADAM_B1 = 0.9
ADAM_B2 = 0.999
ADAM_EPS = 1e-08
ADAM_WD = 0.01
ADAM_STEP = 10

N_DEV = 8
SCAN_CHUNKS = 8
SCAN_COLS = 512
TOKEN_TILE = 256
VMEM_LIMIT = 56 * 1024 * 1024

_F32 = jnp.float32
_BF16 = jnp.bfloat16
_MXU = jnp.bfloat16

_NN = ((1,), (0,))
_NT = ((1,), (1,))
_TN = ((0,), (0,))


def _dot(a, b, dims):
    return lax.dot_general(a.astype(_MXU), b.astype(_MXU), (dims, ((), ())),
                           preferred_element_type=_F32)


def _dot_exact(a, b, dims):
    return lax.dot_general(a.astype(_F32), b.astype(_F32), (dims, ((), ())),
                           precision=lax.Precision.HIGHEST, preferred_element_type=_F32)


def _iota(shape, dim):
    return lax.broadcasted_iota(jnp.int32, shape, dim)


def _rms_fwd(x, g):
    r = lax.rsqrt(jnp.mean(x * x, axis=-1, keepdims=True) + NORM_EPS)
    return x * r * g, r


def _rms_bwd(dy, x, g, r):
    a = dy * g
    xn = x * r
    dx = r * (a - xn * jnp.mean(a * xn, axis=-1, keepdims=True))
    dg = jnp.sum(dy * xn, axis=0, keepdims=True)
    return dx, dg


def _call(body, grid, in_specs, out_specs, out_shape, name, scratch=(), tokens=()):
    params = pltpu.CompilerParams(dimension_semantics=("arbitrary",) * len(grid),
                                  vmem_limit_bytes=VMEM_LIMIT)
    n_in, n_tok = len(in_specs), len(tokens)

    def run(*refs):
        return body(*refs[:n_in], *refs[n_in + n_tok:])

    call = pl.pallas_call(run, grid=grid,
                          in_specs=list(in_specs) + [pl.BlockSpec(memory_space=pl.ANY)] * n_tok,
                          out_specs=out_specs, out_shape=out_shape, scratch_shapes=list(scratch),
                          compiler_params=params, name=name)
    return lambda *args: call(*args, *tokens)


def _rows(tm, n):
    return pl.BlockSpec((tm, n), lambda i: (i, 0))


def _whole(shape):
    nd = len(shape)
    return pl.BlockSpec(shape, lambda i: (0,) * nd)


def _sds(shape, dtype):
    return jax.ShapeDtypeStruct(shape, dtype)


def _tile(L):
    return min(TOKEN_TILE, L)


def _chunk_tile(L):
    return L // SCAN_CHUNKS


def _chunk_block(L, n):
    return pl.BlockSpec((_chunk_tile(L), n), lambda i: (0, i))


def _chunk_shape(L, n):
    return (_chunk_tile(L), SCAN_CHUNKS * n)


def _accumulate(ref, val, first):
    @pl.when(first)
    def _():
        ref[...] = val

    @pl.when(jnp.logical_not(first))
    def _():
        ref[...] += val


def _rope_rows():
    half = ROPE_DIM // 2
    inv = (np.float32(ROPE_THETA) ** (-np.arange(half, dtype=np.float32) * np.float32(2.0) / np.float32(ROPE_DIM))).astype(np.float32)
    col = np.arange(KV_WIDTH) % HEAD_DIM
    freq = np.where(col < ROPE_DIM, inv[col % half], 0.0).astype(np.float32)
    sign = np.where(col < half, -1.0, np.where(col < ROPE_DIM, 1.0, 0.0)).astype(np.float32)
    return freq[None, :], sign[None, :]


def _rope_tables(pos_col):
    L = pos_col.shape[0]
    tm = _tile(L)
    freq, sign = _rope_rows()

    def body(pos_ref, freq_ref, sign_ref, cos_ref, sin_ref):
        ang = pos_ref[...].astype(_F32) * freq_ref[...]
        cos_ref[...] = jnp.cos(ang)
        sin_ref[...] = jnp.sin(ang) * sign_ref[...]

    return _call(body, (L // tm,),
                 [_rows(tm, 1), _whole((1, KV_WIDTH)), _whole((1, KV_WIDTH))],
                 [_rows(tm, KV_WIDTH), _rows(tm, KV_WIDTH)],
                 [_sds((L, KV_WIDTH), _F32)] * 2, "rope_tables")(pos_col, jnp.asarray(freq), jnp.asarray(sign))


def _widen(t, width):
    return t if width == KV_WIDTH else jnp.concatenate([t] * (width // KV_WIDTH), axis=1)


def _rope_partner(t):
    w = t.shape[1]
    in_head = _iota((1, w), 1) & (HEAD_DIM - 1)
    second = jnp.where(in_head < ROPE_DIM, pltpu.roll(t, ROPE_DIM // 2, 1), 0.0)
    return jnp.where(in_head < ROPE_DIM // 2, pltpu.roll(t, w - ROPE_DIM // 2, 1), second)


def _rope_apply(t, cos_t, sin_t):
    w = t.shape[1]
    return t * _widen(cos_t, w) + _rope_partner(t) * _widen(sin_t, w)


def _rope_transpose(dt, cos_t, sin_t):
    w = dt.shape[1]
    return dt * _widen(cos_t, w) + _rope_partner(dt * _widen(sin_t, w))


def _in_proj(x, g_pre_mix, w_in, cos_t, sin_t):
    L = x.shape[0]
    tm = _chunk_tile(L)

    def body(x_ref, g_ref, w_ref, cos_ref, sin_ref, hn_ref, u_ref, q_ref, k_ref, v_ref):
        hn, _ = _rms_fwd(x_ref[...], g_ref[...])
        hn = hn.astype(_BF16)
        hn_ref[...] = hn
        proj = _dot(hn, w_ref[...], _NT)
        u_ref[...] = proj[:, :SSM_WIDTH]
        q = proj[:, SSM_WIDTH:SSM_WIDTH + ATTN_WIDTH]
        k = proj[:, SSM_WIDTH + ATTN_WIDTH:SSM_WIDTH + ATTN_WIDTH + KV_WIDTH]
        cos_v, sin_v = cos_ref[...], sin_ref[...]
        q_ref[...] = _rope_apply(q, cos_v, sin_v).astype(_BF16)
        k_ref[...] = _rope_apply(k, cos_v, sin_v).astype(_BF16)
        v_ref[...] = proj[:, SSM_WIDTH + ATTN_WIDTH + KV_WIDTH:].astype(_BF16)

    return _call(body, (L // tm,),
                 [_rows(tm, D_MODEL), _whole((1, D_MODEL)), _whole((IN_WIDTH, D_MODEL)),
                  _rows(tm, KV_WIDTH), _rows(tm, KV_WIDTH)],
                 [_rows(tm, D_MODEL), _chunk_block(L, SSM_WIDTH), _rows(tm, ATTN_WIDTH),
                  _rows(tm, KV_WIDTH), _rows(tm, KV_WIDTH)],
                 [_sds((L, D_MODEL), _BF16), _sds(_chunk_shape(L, SSM_WIDTH), _F32), _sds((L, ATTN_WIDTH), _BF16),
                  _sds((L, KV_WIDTH), _BF16), _sds((L, KV_WIDTH), _BF16)],
                 "in_proj")(x, g_pre_mix, w_in, cos_t, sin_t)


def _s5_discretize(lam_re, lam_im, log_dt):
    lr = jnp.minimum(lam_re, -1e-4)
    li = lam_im
    dt = jnp.exp(log_dt)
    mag = jnp.exp(lr * dt)
    ar = mag * jnp.cos(li * dt)
    ai = mag * jnp.sin(li * dt)
    den = lr * lr + li * li
    fr = ((ar - 1.0) * lr + ai * li) / den
    fi = (ai * lr - (ar - 1.0) * li) / den
    return ar, ai, fr, fi


def _s5_bbar(lam_re, lam_im, log_dt, b_re, b_im):
    ar, ai, fr, fi = _s5_discretize(lam_re, lam_im, log_dt)
    return ar, ai, fr * b_re - fi * b_im, fr * b_im + fi * b_re


def _spread_masks():
    e16 = (_iota((SSM_GROUP, SSM_WIDTH), 1) & (SSM_GROUP - 1)) == _iota((SSM_GROUP, SSM_WIDTH), 0)
    e64 = (_iota((SSM_STATE, N_STATE), 1) & (SSM_STATE - 1)) == _iota((SSM_STATE, N_STATE), 0)
    mask_b = (_iota((N_STATE, SSM_WIDTH), 0) >> 6) == (_iota((N_STATE, SSM_WIDTH), 1) >> 4)
    mask_c = (_iota((SSM_WIDTH, N_STATE), 0) >> 4) == (_iota((SSM_WIDTH, N_STATE), 1) >> 6)
    return e16.astype(_F32), e64.astype(_F32), mask_b, mask_c


SUPER = 4
SB_STATE = N_STATE // SUPER
SB_WIDTH = SSM_WIDTH // SUPER


def _sb_state(k):
    return slice(SB_STATE * k, SB_STATE * (k + 1))


def _sb_width(k):
    return slice(SB_WIDTH * k, SB_WIDTH * (k + 1))


def _dt_column(log_dt_row):
    eye = _iota((SSM_GROUPS, SSM_GROUPS), 0) == _iota((SSM_GROUPS, SSM_GROUPS), 1)
    return jnp.sum(jnp.where(eye, log_dt_row, 0.0), axis=1, keepdims=True)


def _group_masks():
    e64 = ((_iota((SSM_STATE, N_STATE), 1) & (SSM_STATE - 1)) == _iota((SSM_STATE, N_STATE), 0)).astype(_F32)
    own = _iota((SSM_GROUPS, N_STATE), 0) == (_iota((SSM_GROUPS, N_STATE), 1) >> 6)
    return e64, own


def _rows_of_group():
    return ((_iota((SSM_WIDTH, SSM_GROUPS), 0) >> 4) == _iota((SSM_WIDTH, SSM_GROUPS), 1)).astype(_F32)


def _ssm_prep(lam_re, lam_im, log_dt, b_re, b_im, c_re, c_im):
    def body(lr_ref, li_ref, ld_ref, bre, bim, cre, cim, ar_ref, ai_ref, btr, bti, ctr, cti):
        ar, ai, fr, fi = _s5_discretize(lr_ref[...], li_ref[...], _dt_column(ld_ref[...]))
        e64, own = _group_masks()
        mask_c = (_iota((SSM_WIDTH, N_STATE), 0) >> 4) == (_iota((SSM_WIDTH, N_STATE), 1) >> 6)

        def to_row(t):
            return jnp.sum(jnp.where(own, _dot_exact(t, e64, _NN), 0.0), axis=0, keepdims=True)

        def fold(m):
            full = jnp.where(mask_c, _dot(m, e64, _NN), 0.0)
            return sum(full[_sb_width(k), :] for k in range(SUPER)).astype(_BF16)

        ar_ref[...] = to_row(ar)
        ai_ref[...] = to_row(ai)
        spread = _rows_of_group()
        fr_t = _dot_exact(spread, fr, _NN)
        fi_t = _dot_exact(spread, fi, _NN)
        btr[...] = fold(fr_t * bre[...] - fi_t * bim[...])
        bti[...] = fold(fr_t * bim[...] + fi_t * bre[...])
        ctr[...] = fold(cre[...])
        cti[...] = fold(cim[...])

    row = (1, N_STATE)
    ins = [lam_re, lam_im, log_dt, b_re, b_im, c_re, c_im]
    return _call(body, (1,), [_whole(a.shape) for a in ins],
                 [_whole(row), _whole(row)] + [_whole((SB_WIDTH, N_STATE))] * 4,
                 [_sds(row, _F32), _sds(row, _F32)] + [_sds((SB_WIDTH, N_STATE), _BF16)] * 4,
                 "ssm_prep")(*ins)


def _ssm_bu(u, bt_re, bt_im):
    L = u.shape[0]
    tm = _tile(L)

    def body(u_ref, br_ref, bi_ref, or_ref, oi_ref):
        for k in range(SUPER):
            ub = u_ref[:, _sb_width(k)].astype(_BF16)
            or_ref[:, _sb_state(k)] = _dot(ub, br_ref[:, _sb_state(k)], _NN)
            oi_ref[:, _sb_state(k)] = _dot(ub, bi_ref[:, _sb_state(k)], _NN)

    return _call(body, (L // tm,),
                 [_rows(tm, SSM_WIDTH), _whole((SB_WIDTH, N_STATE)), _whole((SB_WIDTH, N_STATE))],
                 [_rows(tm, N_STATE), _rows(tm, N_STATE)],
                 [_sds((L, N_STATE), _F32)] * 2, "ssm_bu")(u, bt_re, bt_im)


def _complex_power(ar, ai, n):
    def step(_, c):
        pr, pi = c
        return pr * ar - pi * ai, pr * ai + pi * ar
    return lax.fori_loop(0, n, step, (jnp.ones_like(ar), jnp.zeros_like(ai)))


def _chunk_carries(er, ei, pr, pi, reverse):
    rows = _iota(er.shape, 0)
    sr = jnp.zeros_like(pr)
    si = jnp.zeros_like(pi)
    out_r = jnp.zeros_like(er)
    out_i = jnp.zeros_like(ei)
    order = range(SCAN_CHUNKS - 1, 0, -1) if reverse else range(SCAN_CHUNKS - 1)
    for c in order:
        e_r = er[c:c + 1, :]
        e_i = ei[c:c + 1, :]
        sr, si = pr * sr - pi * si + e_r, pr * si + pi * sr + e_i
        nxt = c - 1 if reverse else c + 1
        out_r = jnp.where(rows == nxt, sr, out_r)
        out_i = jnp.where(rows == nxt, si, out_i)
    return out_r, out_i


def _scan_fwd(b_re, b_im, a_re, a_im):
    T = b_re.shape[0]
    W = SCAN_COLS
    blk = pl.BlockSpec((T, SCAN_CHUNKS, W), lambda j: (0, 0, j))
    vec = pl.BlockSpec((1, W), lambda j: (0, j))

    def body(br_ref, bi_ref, ar_ref, ai_ref, xr_ref, xi_ref):
        ar, ai = ar_ref[...], ai_ref[...]
        ar8 = jnp.broadcast_to(ar, (SCAN_CHUNKS, W))
        ai8 = jnp.broadcast_to(ai, (SCAN_CHUNKS, W))

        def local(t, c):
            cr, ci = c
            return ar8 * cr - ai8 * ci + br_ref[t], ar8 * ci + ai8 * cr + bi_ref[t]

        zero = jnp.zeros((SCAN_CHUNKS, W), _F32)
        er, ei = lax.fori_loop(0, T, local, (zero, zero))
        pr, pi = _complex_power(ar, ai, T)
        sr, si = _chunk_carries(er, ei, pr, pi, reverse=False)

        def final(t, c):
            nr, ni = local(t, c)
            xr_ref[t] = nr
            xi_ref[t] = ni
            return nr, ni

        lax.fori_loop(0, T, final, (sr, si))

    shape = _sds(b_re.shape, _F32)
    return _call(body, (N_STATE // W,), [blk, blk, vec, vec], [blk, blk], [shape, shape],
                 "scan_fwd")(b_re, b_im, a_re, a_im)


def _scan_bwd(dx_re, dx_im, x_re, x_im, a_re, a_im, tokens=()):
    T = dx_re.shape[0]
    W = SCAN_COLS
    blk = pl.BlockSpec((T, SCAN_CHUNKS, W), lambda j: (0, 0, j))
    vec = pl.BlockSpec((1, W), lambda j: (0, j))

    def body(dr_ref, di_ref, xr_ref, xi_ref, ar_ref, ai_ref, lr_ref, li_ref, dar_ref, dai_ref):
        ar, ai = ar_ref[...], ai_ref[...]
        ar8 = jnp.broadcast_to(ar, (SCAN_CHUNKS, W))
        ai8 = jnp.broadcast_to(ai, (SCAN_CHUNKS, W))

        def local(t, c):
            cr, ci = c
            return ar8 * cr + ai8 * ci + dr_ref[t], ar8 * ci - ai8 * cr + di_ref[t]

        zero = jnp.zeros((SCAN_CHUNKS, W), _F32)
        er, ei = lax.fori_loop(0, T, lambda k, c: local(T - 1 - k, c), (zero, zero))
        pr, pi = _complex_power(ar, -ai, T)
        sr, si = _chunk_carries(er, ei, pr, pi, reverse=True)

        def grad_a(acc, nr, ni, xpr, xpi):
            return acc[0] + nr * xpr + ni * xpi, acc[1] + ni * xpr - nr * xpi

        def final(k, c):
            t = T - 1 - k
            nr, ni = local(t, c[:2])
            lr_ref[t] = nr
            li_ref[t] = ni
            gr, gi = grad_a(c[2:], nr, ni, xr_ref[t - 1], xi_ref[t - 1])
            return nr, ni, gr, gi

        cr, ci, gr, gi = lax.fori_loop(0, T - 1, final, (sr, si, zero, zero))
        nr, ni = local(0, (cr, ci))
        lr_ref[0] = nr
        li_ref[0] = ni
        first = _iota((SCAN_CHUNKS, W), 0) == 0
        xpr = jnp.where(first, 0.0, pltpu.roll(xr_ref[T - 1], 1, 0))
        xpi = jnp.where(first, 0.0, pltpu.roll(xi_ref[T - 1], 1, 0))
        gr, gi = grad_a((gr, gi), nr, ni, xpr, xpi)
        dar_ref[...] = jnp.sum(gr, axis=0, keepdims=True)
        dai_ref[...] = jnp.sum(gi, axis=0, keepdims=True)

    shape = _sds(dx_re.shape, _F32)
    row = _sds((1, N_STATE), _F32)
    return _call(body, (N_STATE // W,), [blk, blk, blk, blk, vec, vec], [blk, blk, vec, vec],
                 [shape, shape, row, row], "scan_bwd", tokens=tokens)(dx_re, dx_im, x_re, x_im, a_re, a_im)


_GELU_K = math.sqrt(2.0 / math.pi)
_GELU_C = 0.044715


def _gelu(y):
    return 0.5 * y * (1.0 + jnp.tanh(_GELU_K * (y + _GELU_C * y * y * y)))


def _gelu_grad(y):
    t = jnp.tanh(_GELU_K * (y + _GELU_C * y * y * y))
    return 0.5 * (1.0 + t) + 0.5 * y * (1.0 - t * t) * _GELU_K * (1.0 + 3.0 * _GELU_C * y * y)


def _ssm_out(x_re, x_im, u, ct_re, ct_im, d_row, w_glu, b_glu, g_ssm):
    L = u.shape[0]
    tm = _tile(L)

    def body(xr_ref, xi_ref, u_ref, cr_ref, ci_ref, d_ref, w_ref, b_ref, g_ref, y_ref, z_ref, n_ref):
        cx = [_dot(xr_ref[:, _sb_state(k)], cr_ref[:, _sb_state(k)], _NT)
              - _dot(xi_ref[:, _sb_state(k)], ci_ref[:, _sb_state(k)], _NT) for k in range(SUPER)]
        y = jnp.concatenate(cx, axis=1) + d_ref[...] * u_ref[...]
        y_ref[...] = y
        z = _dot(_gelu(y), w_ref[...], _NT) + b_ref[...]
        z_ref[...] = z
        out = z[:, :SSM_WIDTH] * jax.nn.sigmoid(z[:, SSM_WIDTH:])
        n, _ = _rms_fwd(out, g_ref[...])
        n_ref[...] = n.astype(_BF16)

    return _call(body, (L // tm,),
                 [_rows(tm, N_STATE), _rows(tm, N_STATE), _rows(tm, SSM_WIDTH),
                  _whole((SB_WIDTH, N_STATE)), _whole((SB_WIDTH, N_STATE)), _whole((1, SSM_WIDTH)),
                  _whole((2 * SSM_WIDTH, SSM_WIDTH)), _whole((1, 2 * SSM_WIDTH)), _whole((1, SSM_WIDTH))],
                 [_rows(tm, SSM_WIDTH), _rows(tm, 2 * SSM_WIDTH), _rows(tm, SSM_WIDTH)],
                 [_sds((L, SSM_WIDTH), _F32), _sds((L, 2 * SSM_WIDTH), _F32), _sds((L, SSM_WIDTH), _BF16)],
                 "ssm_out")(x_re, x_im, u, ct_re, ct_im, d_row, w_glu, b_glu, g_ssm)


def _ssm_out_bwd(dn, y, z, u, ct_re, ct_im, d_row, w_glu, g_ssm):
    L = u.shape[0]
    tm = _tile(L)

    def body(dn_ref, y_ref, z_ref, u_ref, cr_ref, ci_ref, d_ref, w_ref, g_ref,
             gy_ref, dz_ref, dy_ref, dud_ref, dxr_ref, dxi_ref, dg_ref, db_ref, dd_ref):
        first = pl.program_id(0) == 0
        z = z_ref[...]
        z1, z2 = z[:, :SSM_WIDTH], z[:, SSM_WIDTH:]
        sig = jax.nn.sigmoid(z2)
        out = z1 * sig
        g = g_ref[...]
        _, r = _rms_fwd(out, g)
        dout, dg = _rms_bwd(dn_ref[...], out, g, r)
        _accumulate(dg_ref, dg, first)
        dz = jnp.concatenate([dout * sig, dout * z1 * sig * (1.0 - sig)], axis=1)
        _accumulate(db_ref, jnp.sum(dz, axis=0, keepdims=True), first)
        dzb = dz.astype(_BF16)
        dz_ref[...] = dzb
        y = y_ref[...]
        gy_ref[...] = _gelu(y).astype(_BF16)
        dy = _dot(dzb, w_ref[...], _NN) * _gelu_grad(y)
        u = u_ref[...]
        _accumulate(dd_ref, jnp.sum(dy * u, axis=0, keepdims=True), first)
        dud_ref[...] = d_ref[...] * dy
        dyb = dy.astype(_BF16)
        dy_ref[...] = dyb
        for k in range(SUPER):
            dxr_ref[:, _sb_state(k)] = _dot(dyb[:, _sb_width(k)], cr_ref[:, _sb_state(k)], _NN)
            dxi_ref[:, _sb_state(k)] = -_dot(dyb[:, _sb_width(k)], ci_ref[:, _sb_state(k)], _NN)

    row = _whole((1, SSM_WIDTH))
    return _call(body, (L // tm,),
                 [_rows(tm, SSM_WIDTH), _rows(tm, SSM_WIDTH), _rows(tm, 2 * SSM_WIDTH), _rows(tm, SSM_WIDTH),
                  _whole((SB_WIDTH, N_STATE)), _whole((SB_WIDTH, N_STATE)), row,
                  _whole((2 * SSM_WIDTH, SSM_WIDTH)), row],
                 [_rows(tm, SSM_WIDTH), _rows(tm, 2 * SSM_WIDTH), _rows(tm, SSM_WIDTH), _rows(tm, SSM_WIDTH),
                  _rows(tm, N_STATE), _rows(tm, N_STATE), row, _whole((1, 2 * SSM_WIDTH)), row],
                 [_sds((L, SSM_WIDTH), _BF16), _sds((L, 2 * SSM_WIDTH), _BF16), _sds((L, SSM_WIDTH), _BF16),
                  _sds((L, SSM_WIDTH), _F32), _sds((L, N_STATE), _F32), _sds((L, N_STATE), _F32),
                  _sds((1, SSM_WIDTH), _F32), _sds((1, 2 * SSM_WIDTH), _F32), _sds((1, SSM_WIDTH), _F32)],
                 "ssm_out_bwd")(dn, y, z, u, ct_re, ct_im, d_row, w_glu, g_ssm)


def _ssm_du(lam_re, lam_im, bt_re, bt_im, dud):
    L = dud.shape[0]
    tm = _tile(L)

    def body(lr_ref, li_ref, br_ref, bi_ref, dud_ref, du_ref):
        for k in range(SUPER):
            du_ref[:, _sb_width(k)] = (_dot(lr_ref[:, _sb_state(k)], br_ref[:, _sb_state(k)], _NT)
                                       + _dot(li_ref[:, _sb_state(k)], bi_ref[:, _sb_state(k)], _NT)
                                       + dud_ref[:, _sb_width(k)])

    return _call(body, (L // tm,),
                 [_rows(tm, N_STATE), _rows(tm, N_STATE), _whole((SB_WIDTH, N_STATE)),
                  _whole((SB_WIDTH, N_STATE)), _rows(tm, SSM_WIDTH)],
                 _rows(tm, SSM_WIDTH), _sds((L, SSM_WIDTH), _F32), "ssm_du")(lam_re, lam_im, bt_re, bt_im, dud)


def _ssm_weight_grads(dy, x_re, x_im, lam_re, lam_im, u):
    L = u.shape[0]

    def body(dy_ref, xr_ref, xi_ref, lr_ref, li_ref, u_ref, dcr_ref, dci_ref, dbr_ref, dbi_ref):
        dyb = dy_ref[...]
        ub = u_ref[...].astype(_BF16)
        dcr_ref[...] = _dot(dyb, xr_ref[...], _TN)
        dci_ref[...] = _dot(dyb, xi_ref[...], _TN)
        dbr_ref[...] = _dot(ub, lr_ref[...], _TN)
        dbi_ref[...] = _dot(ub, li_ref[...], _TN)

    width = pl.BlockSpec((L, SB_WIDTH), lambda k: (0, k))
    state = pl.BlockSpec((L, SB_STATE), lambda k: (0, k))
    out = pl.BlockSpec((SB_WIDTH, SB_STATE), lambda k: (0, k))
    return _call(body, (SUPER,), [width, state, state, state, state, width], [out] * 4,
                 [_sds((SB_WIDTH, N_STATE), _F32)] * 4,
                 "ssm_weight_grads")(dy, x_re, x_im, lam_re, lam_im, u)


_SSM_PACK = {"ssm_b_re": (0, SSM_WIDTH, 0, SSM_STATE), "ssm_c_re": (0, SSM_WIDTH, 64, SSM_STATE),
             "ssm_b_im": (512, SSM_WIDTH, 0, SSM_STATE), "ssm_c_im": (512, SSM_WIDTH, 64, SSM_STATE),
             "ssm_lambda_re": (1024, SSM_GROUPS, 0, SSM_STATE), "ssm_lambda_im": (1024, SSM_GROUPS, 64, SSM_STATE),
             "ssm_d": (1056, SSM_GROUPS, 0, SSM_GROUP), "ssm_log_dt": (1088, 1, 0, SSM_GROUPS)}
_PACK_TILE = 16
_SSM_PACK_ROWS = 1088 + _PACK_TILE


def _ssm_param_bwd(da_re, da_im, dbt_re, dbt_im, dct_re, dct_im, lam_re, lam_im, log_dt, b_re, b_im, g_d):
    def body(dar, dai, dbr, dbi, dcr, dci, lr_ref, li_ref, ld_ref, bre_ref, bim_ref, gd_ref, pack_ref):
        lane_in = _iota((SSM_STATE, _LANES), 0)
        lane_out = _iota((SSM_STATE, _LANES), 1)
        low = (lane_out == lane_in).astype(_F32)
        high = (lane_out == lane_in + SSM_STATE).astype(_F32)

        def side_by_side(a, b):
            return _dot_exact(a, low, _NN) + _dot_exact(b, high, _NN)

        tail = _SSM_PACK["ssm_d"][0]
        pack_ref[tail:, :] = jnp.zeros((_SSM_PACK_ROWS - tail, _LANES), _BF16)
        pack_ref[tail:tail + SSM_GROUPS, 0:SSM_GROUP] = gd_ref[...].astype(_BF16)
        own_c = (_iota((SB_WIDTH, SB_STATE), 0) >> 4) == (_iota((SB_WIDTH, SB_STATE), 1) >> 6)

        def unfold(ref):
            blocks = []
            for k in range(SUPER):
                t = jnp.where(own_c, ref[:, _sb_state(k)], 0.0)
                t = sum(t[:, 128 * i:128 * (i + 1)] for i in range(SB_STATE // 128))
                blocks.append((t + pltpu.roll(t, SSM_STATE, 1))[:, :SSM_STATE])
            return jnp.concatenate(blocks, axis=0)

        dbb_re, dbb_im = unfold(dbr), unfold(dbi)
        b_re, b_im = bre_ref[...], bim_ref[...]
        dt_col = _dt_column(ld_ref[...])
        (_, _, fr, fi), vjp = jax.vjp(_s5_discretize, lr_ref[...], li_ref[...], dt_col)
        spread = _rows_of_group()
        fr_t = _dot_exact(spread, fr, _NN)
        fi_t = _dot_exact(spread, fi, _NN)
        pack_ref[0:SSM_WIDTH, :] = side_by_side(fr_t * dbb_re + fi_t * dbb_im, unfold(dcr)).astype(_BF16)
        pack_ref[SSM_WIDTH:2 * SSM_WIDTH, :] = side_by_side(fr_t * dbb_im - fi_t * dbb_re, -unfold(dci)).astype(_BF16)
        d_fr = _dot_exact(spread, dbb_re * b_re + dbb_im * b_im, _TN)
        d_fi = _dot_exact(spread, dbb_im * b_re - dbb_re * b_im, _TN)
        e64, own = _group_masks()

        def from_row(ref):
            return _dot_exact(jnp.where(own, ref[...], 0.0), e64, _NT)

        d_lr, d_li, d_dt = vjp((from_row(dar), from_row(dai), d_fr, d_fi))
        lam_rows = _SSM_PACK["ssm_lambda_re"][0]
        pack_ref[lam_rows:lam_rows + SSM_GROUPS, :] = side_by_side(d_lr, d_li).astype(_BF16)
        eye = (_iota((SSM_GROUPS, SSM_GROUPS), 0) == _iota((SSM_GROUPS, SSM_GROUPS), 1)).astype(_F32)
        dt_row = _SSM_PACK["ssm_log_dt"][0]
        pack_ref[dt_row:dt_row + _PACK_TILE, 0:SSM_GROUPS] = _dot_exact(
            jnp.broadcast_to(d_dt, (SSM_GROUPS, 128)), eye, _TN)[0:_PACK_TILE].astype(_BF16)

    ins = [da_re, da_im, dbt_re, dbt_im, dct_re, dct_im, lam_re, lam_im, log_dt, b_re, b_im, g_d]
    out = (_SSM_PACK_ROWS, _LANES)
    return _call(body, (1,), [_whole(a.shape) for a in ins], _whole(out), _sds(out, _BF16), "ssm_param_bwd")(*ins)


def _head_spread(j):
    r = _iota((KV_WIDTH, 256), 0)
    c = _iota((KV_WIDTH, 256), 1)
    return (r == HEAD_DIM * j + (c & (HEAD_DIM - 1))).astype(_BF16)


STACK = Q_PER_KV * BLOCK


def _stack_heads(t):
    lane_head = _iota((1, 256), 1) >> 6
    return jnp.concatenate([jnp.where(lane_head == g, t, jnp.zeros_like(t)) for g in range(Q_PER_KV)], axis=0)


def _unstack_heads(t):
    lane_head = _iota((1, 256), 1) >> 6
    return sum(jnp.where(lane_head == g, t[BLOCK * g:BLOCK * (g + 1)], 0.0) for g in range(Q_PER_KV))


def _stacked_sinks(sink_ref, j):
    block = _iota((STACK, 1), 0) >> 7
    col = jnp.full((STACK, 1), sink_ref[Q_PER_KV * j], _F32)
    for g in range(1, Q_PER_KV):
        col = jnp.where(block == g, sink_ref[Q_PER_KV * j + g], col)
    return col


def _fold_heads(t, j):
    t = t[:, :KV_WIDTH] + t[:, KV_WIDTH:]
    t = t + pltpu.roll(t, HEAD_DIM, 1)
    return jnp.where((_iota((1, KV_WIDTH), 1) >> 6) == j, t, 0.0)


def _attn_scores(q_stacked, kt, blk, sink):
    s = _dot(q_stacked, kt, _NT) * (HEAD_DIM ** -0.5)
    qi = _iota((STACK, 2 * BLOCK), 0) & (BLOCK - 1)
    kj = _iota((STACK, 2 * BLOCK), 1)
    rel = qi + BLOCK - kj
    valid = (rel >= 0) & (rel < BLOCK) & (blk * BLOCK - BLOCK + kj >= 0)
    s = jnp.where(valid, s, MASK_VALUE)
    m = jnp.maximum(jnp.max(s, axis=-1, keepdims=True), sink)
    p = jnp.exp(s - m)
    e_sink = jnp.exp(sink - m)
    den = jnp.sum(p, axis=-1, keepdims=True) + e_sink
    return p / den, e_sink / den


def _attn_specs():
    prev = lambda i: (jnp.maximum(i - 1, 0), 0)
    cur = lambda i: (i, 0)
    kv = [pl.BlockSpec((BLOCK, KV_WIDTH), prev), pl.BlockSpec((BLOCK, KV_WIDTH), cur)]
    return [pl.BlockSpec((BLOCK, ATTN_WIDTH), cur)] + kv + kv


def _attn_fwd(q, k, v, sinks, g_attn):
    L = q.shape[0]

    def body(q_ref, kp_ref, kc_ref, vp_ref, vc_ref, sink_ref, g_ref, o_ref, n_ref):
        blk = pl.program_id(0)
        kwin = jnp.concatenate([kp_ref[...], kc_ref[...]], axis=0)
        vwin = jnp.concatenate([vp_ref[...], vc_ref[...]], axis=0)
        halves = []
        for j in range(N_KV_HEADS):
            spread = _head_spread(j)
            kt = _dot(kwin, spread, _NN).astype(_BF16)
            vt = _dot(vwin, spread, _NN).astype(_BF16)
            qs = _stack_heads(q_ref[:, 256 * j:256 * (j + 1)])
            p, _ = _attn_scores(qs, kt, blk, _stacked_sinks(sink_ref, j))
            halves.append(_unstack_heads(_dot(p, vt, _NN)))
        o = jnp.concatenate(halves, axis=1)
        o_ref[...] = o
        n, _ = _rms_fwd(o, g_ref[...])
        n_ref[...] = n.astype(_BF16)

    cur = lambda i: (i, 0)
    return _call(body, (L // BLOCK,),
                 _attn_specs() + [pl.BlockSpec(memory_space=pltpu.SMEM), _whole((1, ATTN_WIDTH))],
                 [pl.BlockSpec((BLOCK, ATTN_WIDTH), cur)] * 2,
                 [_sds((L, ATTN_WIDTH), _F32), _sds((L, ATTN_WIDTH), _BF16)],
                 "attn_fwd")(q, k, k, v, v, sinks, g_attn)


def _attn_bwd(q, k, v, o, dn, sinks, g_attn):
    L = q.shape[0]

    def body(q_ref, kp_ref, kc_ref, vp_ref, vc_ref, o_ref, dn_ref, sink_ref, g_ref,
             dq_ref, dk_ref, dv_ref, dsink_ref, dg_ref):
        blk = pl.program_id(0)
        first = blk == 0

        @pl.when(first)
        def _():
            dk_ref[...] = jnp.zeros_like(dk_ref)
            dv_ref[...] = jnp.zeros_like(dv_ref)
            dsink_ref[...] = jnp.zeros_like(dsink_ref)

        o = o_ref[...]
        g = g_ref[...]
        _, r = _rms_fwd(o, g)
        do, dg = _rms_bwd(dn_ref[...], o, g, r)
        _accumulate(dg_ref, dg, first)
        kwin = jnp.concatenate([kp_ref[...], kc_ref[...]], axis=0)
        vwin = jnp.concatenate([vp_ref[...], vc_ref[...]], axis=0)
        lane = _iota((1, 128), 1)
        dsink = jnp.zeros((1, 128), _F32)
        dkwin = jnp.zeros((2 * BLOCK, KV_WIDTH), _F32)
        dvwin = jnp.zeros((2 * BLOCK, KV_WIDTH), _F32)
        dq_halves = []
        for j in range(N_KV_HEADS):
            spread = _head_spread(j)
            kt = _dot(kwin, spread, _NN).astype(_BF16)
            vt = _dot(vwin, spread, _NN).astype(_BF16)
            qs = _stack_heads(q_ref[:, 256 * j:256 * (j + 1)])
            dos = _stack_heads(do[:, 256 * j:256 * (j + 1)]).astype(_BF16)
            p, p_sink = _attn_scores(qs, kt, blk, _stacked_sinks(sink_ref, j))
            dp = _dot(dos, vt, _NT)
            delta = jnp.sum(p * dp, axis=-1, keepdims=True)
            ds = (p * (dp - delta) * (HEAD_DIM ** -0.5)).astype(_BF16)
            sink_term = p_sink * delta
            for g in range(Q_PER_KV):
                head_sum = jnp.sum(sink_term[BLOCK * g:BLOCK * (g + 1)], axis=0, keepdims=True)
                dsink = dsink - jnp.where(lane == Q_PER_KV * j + g, head_sum, 0.0)
            dvwin = dvwin + _fold_heads(_dot(p, dos, _TN), j)
            dkwin = dkwin + _fold_heads(_dot(ds, qs, _TN), j)
            dq_halves.append(_unstack_heads(_dot(ds, kt, _NN)))
        dq_ref[...] = jnp.concatenate(dq_halves, axis=1)
        dsink_ref[...] += dsink
        prev = pl.ds(pl.multiple_of(jnp.maximum(blk - 1, 0) * BLOCK, BLOCK), BLOCK)
        cur = pl.ds(pl.multiple_of(blk * BLOCK, BLOCK), BLOCK)
        dk_ref[prev, :] += dkwin[:BLOCK]
        dk_ref[cur, :] += dkwin[BLOCK:]
        dv_ref[prev, :] += dvwin[:BLOCK]
        dv_ref[cur, :] += dvwin[BLOCK:]

    cur = lambda i: (i, 0)
    blk_q = pl.BlockSpec((BLOCK, ATTN_WIDTH), cur)
    return _call(body, (L // BLOCK,),
                 _attn_specs() + [blk_q, blk_q, pl.BlockSpec(memory_space=pltpu.SMEM), _whole((1, ATTN_WIDTH))],
                 [blk_q, _whole((L, KV_WIDTH)), _whole((L, KV_WIDTH)), _whole((1, 128)), _whole((1, ATTN_WIDTH))],
                 [_sds((L, ATTN_WIDTH), _F32), _sds((L, KV_WIDTH), _F32), _sds((L, KV_WIDTH), _F32),
                  _sds((1, 128), _F32), _sds((1, ATTN_WIDTH), _F32)],
                 "attn_bwd")(q, k, k, v, v, o, dn, sinks, g_attn)


def _out_proj(n_ssm, n_attn, x, w_out, g_post_mix, g_pre_ffn):
    L = x.shape[0]
    tm = _chunk_tile(L)

    def body(ns_ref, na_ref, x_ref, w_ref, g1_ref, g2_ref, merged_ref, mo_ref, h1_ref, hn2_ref):
        merged = jnp.concatenate([ns_ref[...], na_ref[...]], axis=1)
        merged_ref[...] = merged
        mo = _dot(merged, w_ref[...], _NN)
        mo_ref[...] = mo
        n, _ = _rms_fwd(mo, g1_ref[...])
        h1 = x_ref[...] + n
        h1_ref[...] = h1
        hn2, _ = _rms_fwd(h1, g2_ref[...])
        hn2_ref[...] = hn2.astype(_BF16)

    row = _whole((1, D_MODEL))
    return _call(body, (L // tm,),
                 [_chunk_block(L, SSM_WIDTH), _rows(tm, ATTN_WIDTH), _rows(tm, D_MODEL), _whole((D_MODEL, D_MODEL)),
                  row, row],
                 [_rows(tm, D_MODEL)] * 4,
                 [_sds((L, D_MODEL), _BF16), _sds((L, D_MODEL), _F32), _sds((L, D_MODEL), _F32), _sds((L, D_MODEL), _BF16)],
                 "out_proj")(n_ssm, n_attn, x, w_out, g_post_mix, g_pre_ffn)


def _ffn(hn2, h1, target, w_gate_up, w_down, g_pre_ffn, g_post_ffn):
    L = h1.shape[0]
    tm = _tile(L)
    half = D_FF // 2

    def body(hn2_ref, h1_ref, tgt_ref, wgu_hbm, wd_hbm, g2_ref, g3_ref,
             act_ref, dgu_ref, dff_ref, dh1_ref, loss_ref, dg3_ref, dg2_ref,
             wgu, wd, gu, sem):
        first = pl.program_id(0) == 0

        @pl.when(first)
        def _():
            c1 = pltpu.make_async_copy(wgu_hbm, wgu, sem.at[0])
            c2 = pltpu.make_async_copy(wd_hbm, wd, sem.at[1])
            c1.start()
            c2.start()
            c1.wait()
            c2.wait()

        hn2 = hn2_ref[...]
        ff = jnp.zeros((tm, D_MODEL), _F32)
        for c in range(2):
            gate = _dot(hn2, wgu[half * c:half * (c + 1), :], _NT)
            up = _dot(hn2, wgu[D_FF + half * c:D_FF + half * (c + 1), :], _NT)
            gu[:, half * c:half * (c + 1)] = gate
            gu[:, D_FF + half * c:D_FF + half * (c + 1)] = up
            act = gate * jax.nn.sigmoid(gate) * up
            act_ref[half * c:half * (c + 1), :] = act.T.astype(_BF16)
            ff = ff + _dot(act, wd[half * c:half * (c + 1), :], _NN)
        g3 = g3_ref[...]
        n, r = _rms_fwd(ff, g3)
        h1 = h1_ref[...]
        err = h1 + n - tgt_ref[...]
        loss = 0.5 * jnp.sum(jnp.mean(err * err, axis=-1, keepdims=True), axis=0, keepdims=True)
        _accumulate(loss_ref, jnp.broadcast_to(loss, (1, 128)), first)
        dh2 = err * (1.0 / D_MODEL)
        dff, dg3 = _rms_bwd(dh2, ff, g3, r)
        _accumulate(dg3_ref, dg3, first)
        dffb = dff.astype(_BF16)
        dff_ref[...] = dffb
        dhn2 = jnp.zeros((tm, D_MODEL), _F32)
        for c in range(2):
            dact = _dot(dffb, wd[half * c:half * (c + 1), :], _NT)
            gate = gu[:, half * c:half * (c + 1)]
            up = gu[:, D_FF + half * c:D_FF + half * (c + 1)]
            sig = jax.nn.sigmoid(gate)
            silu = gate * sig
            dgate = dact * up * (sig + silu * (1.0 - sig))
            dup = dact * silu
            dgu_ref[half * c:half * (c + 1), :] = dgate.T.astype(_BF16)
            dgu_ref[D_FF + half * c:D_FF + half * (c + 1), :] = dup.T.astype(_BF16)
            dhn2 = dhn2 + _dot(dgate, wgu[half * c:half * (c + 1), :], _NN)
            dhn2 = dhn2 + _dot(dup, wgu[D_FF + half * c:D_FF + half * (c + 1), :], _NN)
        g2 = g2_ref[...]
        _, r2 = _rms_fwd(h1, g2)
        dh1, dg2 = _rms_bwd(dhn2, h1, g2, r2)
        _accumulate(dg2_ref, dg2, first)
        dh1_ref[...] = dh2 + dh1

    row = _whole((1, D_MODEL))
    anyspace = pl.BlockSpec(memory_space=pl.ANY)
    return _call(body, (L // tm,),
                 [_rows(tm, D_MODEL), _rows(tm, D_MODEL), _rows(tm, D_MODEL), anyspace, anyspace, row, row],
                 [pl.BlockSpec((D_FF, tm), lambda i: (0, i)), pl.BlockSpec((2 * D_FF, tm), lambda i: (0, i)),
                  _rows(tm, D_MODEL), _rows(tm, D_MODEL), _whole((1, 128)), row, row],
                 [_sds((D_FF, L), _BF16), _sds((2 * D_FF, L), _BF16), _sds((L, D_MODEL), _BF16),
                  _sds((L, D_MODEL), _F32), _sds((1, 128), _F32), _sds((1, D_MODEL), _F32), _sds((1, D_MODEL), _F32)],
                 "ffn",
                 scratch=[pltpu.VMEM((2 * D_FF, D_MODEL), _BF16), pltpu.VMEM((D_FF, D_MODEL), _BF16),
                          pltpu.VMEM((tm, 2 * D_FF), _F32), pltpu.SemaphoreType.DMA((2,))],
                 )(hn2, h1, target, w_gate_up, w_down, g_pre_ffn, g_post_ffn)


def _out_proj_bwd(dh1, mo, w_out, g_post_mix, tokens=()):
    L = dh1.shape[0]
    tm = _chunk_tile(L)

    def body(dh1_ref, mo_ref, w_ref, g_ref, dmo_ref, dns_ref, dna_ref, dg_ref):
        first = pl.program_id(0) == 0
        mo = mo_ref[...]
        g = g_ref[...]
        _, r = _rms_fwd(mo, g)
        dmo, dg = _rms_bwd(dh1_ref[...], mo, g, r)
        _accumulate(dg_ref, dg, first)
        dmob = dmo.astype(_BF16)
        dmo_ref[...] = dmob
        dmerged = _dot(dmob, w_ref[...], _NT)
        dns_ref[...] = dmerged[:, :SSM_WIDTH]
        dna_ref[...] = dmerged[:, SSM_WIDTH:]

    row = _whole((1, D_MODEL))
    return _call(body, (L // tm,),
                 [_rows(tm, D_MODEL), _rows(tm, D_MODEL), _whole((D_MODEL, D_MODEL)), row],
                 [_rows(tm, D_MODEL), _chunk_block(L, SSM_WIDTH), _rows(tm, ATTN_WIDTH), row],
                 [_sds((L, D_MODEL), _BF16), _sds(_chunk_shape(L, SSM_WIDTH), _F32), _sds((L, ATTN_WIDTH), _F32),
                  _sds((1, D_MODEL), _F32)],
                 "out_proj_bwd", tokens=tokens)(dh1, mo, w_out, g_post_mix)


def _in_proj_bwd(du, dq, dk, dv, cos_t, sin_t, x, dh1, g_pre_mix, w_in, tokens=()):
    L = x.shape[0]
    tm = _chunk_tile(L)

    def body(du_ref, dq_ref, dk_ref, dv_ref, cos_ref, sin_ref, x_ref, dh1_ref, g_ref, w_ref,
             dproj_ref, dx_ref, dg_ref):
        first = pl.program_id(0) == 0
        cos_v, sin_v = cos_ref[...], sin_ref[...]
        dproj = jnp.concatenate([du_ref[...], _rope_transpose(dq_ref[...], cos_v, sin_v),
                                 _rope_transpose(dk_ref[...], cos_v, sin_v), dv_ref[...]], axis=1).astype(_BF16)
        dproj_ref[...] = dproj
        dhn = _dot(dproj, w_ref[...], _NN)
        x = x_ref[...]
        g = g_ref[...]
        _, r = _rms_fwd(x, g)
        dx, dg = _rms_bwd(dhn, x, g, r)
        _accumulate(dg_ref, dg, first)
        dx_ref[...] = dh1_ref[...] + dx

    row = _whole((1, D_MODEL))
    return _call(body, (L // tm,),
                 [_chunk_block(L, SSM_WIDTH), _rows(tm, ATTN_WIDTH), _rows(tm, KV_WIDTH), _rows(tm, KV_WIDTH),
                  _rows(tm, KV_WIDTH), _rows(tm, KV_WIDTH), _rows(tm, D_MODEL), _rows(tm, D_MODEL), row,
                  _whole((IN_WIDTH, D_MODEL))],
                 [_rows(tm, IN_WIDTH), _rows(tm, D_MODEL), row],
                 [_sds((L, IN_WIDTH), _BF16), _sds((L, D_MODEL), _F32), _sds((1, D_MODEL), _F32)],
                 "in_proj_bwd", tokens=tokens)(du, dq, dk, dv, cos_t, sin_t, x, dh1, g_pre_mix, w_in)


def _matmul_nn(a, b, out_dtype, name):
    M, K = a.shape
    N = b.shape[1]
    tm = next(t for t in (512, 256, 128) if M % t == 0)
    tn = N if N <= D_MODEL else next(t for t in (512, 256, 128) if N % t == 0)

    def body(a_ref, b_ref, o_ref):
        o_ref[...] = _dot(a_ref[...], b_ref[...], _NN).astype(out_dtype)

    params = pltpu.CompilerParams(dimension_semantics=("arbitrary", "arbitrary"), vmem_limit_bytes=VMEM_LIMIT)
    return pl.pallas_call(body, grid=(M // tm, N // tn),
                          in_specs=[pl.BlockSpec((tm, K), lambda i, j: (i, 0)),
                                    pl.BlockSpec((K, tn), lambda i, j: (0, j))],
                          out_specs=pl.BlockSpec((tm, tn), lambda i, j: (i, j)),
                          out_shape=_sds((M, N), out_dtype), compiler_params=params, name=name)(a, b)


def _matmul_tn(a, b, out_dtype, name, scale=1.0):
    K, M = a.shape
    N = b.shape[1]
    tm = next(t for t in (512, 256, 128) if M % t == 0)
    tn = N if N <= D_MODEL else next(t for t in (512, 256, 128) if N % t == 0)

    def body(a_ref, b_ref, o_ref):
        acc = _dot(a_ref[...], b_ref[...], _TN)
        o_ref[...] = (acc if scale == 1.0 else acc * scale).astype(out_dtype)

    params = pltpu.CompilerParams(dimension_semantics=("arbitrary", "arbitrary"), vmem_limit_bytes=VMEM_LIMIT)
    return pl.pallas_call(body, grid=(M // tm, N // tn),
                          in_specs=[pl.BlockSpec((K, tm), lambda i, j: (0, i)),
                                    pl.BlockSpec((K, tn), lambda i, j: (0, j))],
                          out_specs=pl.BlockSpec((tm, tn), lambda i, j: (i, j)),
                          out_shape=_sds((M, N), out_dtype), compiler_params=params, name=name)(a, b)


def _local_step(x, pos, target, p, fetch, publish):
    L = x.shape[0]
    T = L // SCAN_CHUNKS
    cos_t, sin_t = _rope_tables(pos.reshape(L, 1))
    w_in, = fetch(("w_in",), None)
    hn, u, q, k, v = _in_proj(x, p["g_pre_mix"], w_in, cos_t, sin_t)

    ssm = {n: _to_2d(n, p[n]) for n in ("ssm_lambda_re", "ssm_lambda_im", "ssm_log_dt", "ssm_b_re", "ssm_b_im",
                                        "ssm_c_re", "ssm_c_im")}
    d_row = p["ssm_d"].reshape(1, SSM_WIDTH)
    a_re, a_im, bt_re, bt_im, ct_re, ct_im = _ssm_prep(
        ssm["ssm_lambda_re"], ssm["ssm_lambda_im"], ssm["ssm_log_dt"], ssm["ssm_b_re"], ssm["ssm_b_im"],
        ssm["ssm_c_re"], ssm["ssm_c_im"])

    u_c = u.reshape(L, SSM_WIDTH)
    bu_re, bu_im = _ssm_bu(u_c, bt_re, bt_im)
    x_re, x_im = _scan_fwd(bu_re.reshape(T, SCAN_CHUNKS, N_STATE), bu_im.reshape(T, SCAN_CHUNKS, N_STATE), a_re, a_im)
    w_glu, = fetch(("w_glu",), x_re)
    y, z, n_ssm_c = _ssm_out(x_re.reshape(L, N_STATE), x_im.reshape(L, N_STATE), u_c, ct_re, ct_im, d_row,
                             w_glu, p["b_glu"], p["g_ssm_out"])
    n_ssm = n_ssm_c.reshape(_chunk_shape(L, SSM_WIDTH))

    sinks = p["attn_sinks"].reshape(N_Q_HEADS)
    o, n_attn = _attn_fwd(q, k, v, sinks, p["g_attn_out"])
    w_out, = fetch(("w_out",), n_attn)
    merged, mo, h1, hn2 = _out_proj(n_ssm, n_attn, x, w_out, p["g_post_mix"], p["g_pre_ffn"])
    w_gate_up, w_down = fetch(("w_gate_up", "w_down"), hn2)
    act_t, dgu_t, dff, dh1, loss, dg_post_ffn, dg_pre_ffn = _ffn(
        hn2, h1, target, w_gate_up, w_down, p["g_pre_ffn"], p["g_post_ffn"])
    grads = {"g_post_ffn": dg_post_ffn, "g_pre_ffn": dg_pre_ffn}
    tokens = publish({"w_down": _matmul_nn(act_t, dff, _BF16, "grad_w_down"),
                      "w_gate_up": _matmul_nn(dgu_t, hn2, _BF16, "grad_w_gate_up")})

    dmo, dn_ssm, dn_attn, grads["g_post_mix"] = _out_proj_bwd(dh1, mo, w_out, p["g_post_mix"], tokens)
    grad_w_out = _matmul_tn(merged, dmo, _BF16, "grad_w_out")

    dq, dk, dv, dsink, grads["g_attn_out"] = _attn_bwd(q, k, v, o, dn_attn, sinks, p["g_attn_out"])
    grads["attn_sinks"] = dsink

    gy, dz, dy, dud, dx_re, dx_im, grads["g_ssm_out"], grads["b_glu"], dd = _ssm_out_bwd(
        dn_ssm.reshape(L, SSM_WIDTH), y, z, u_c, ct_re, ct_im, d_row, w_glu, p["g_ssm_out"])
    grads.update(w_out=grad_w_out, w_glu=_matmul_tn(dz, gy, _BF16, "grad_w_glu"))
    lam_re, lam_im, da_re, da_im = _scan_bwd(dx_re.reshape(T, SCAN_CHUNKS, N_STATE), dx_im.reshape(T, SCAN_CHUNKS, N_STATE),
                                             x_re, x_im, a_re, a_im, [grads["w_out"], grads["w_glu"]])
    lam_re = lam_re.reshape(L, N_STATE)
    lam_im = lam_im.reshape(L, N_STATE)
    dct_re, dct_im, dbt_re, dbt_im = _ssm_weight_grads(
        dy, x_re.reshape(L, N_STATE), x_im.reshape(L, N_STATE), lam_re, lam_im, u_c)
    ssm_pack = _ssm_param_bwd(
        da_re, da_im, dbt_re, dbt_im, dct_re, dct_im,
        ssm["ssm_lambda_re"], ssm["ssm_lambda_im"], ssm["ssm_log_dt"], ssm["ssm_b_re"], ssm["ssm_b_im"],
        dd.reshape(SSM_GROUPS, SSM_GROUP))
    grads.update(ssm_pack=ssm_pack, loss=loss)
    publish(grads)

    du = _ssm_du(lam_re, lam_im, bt_re, bt_im, dud).reshape(_chunk_shape(L, SSM_WIDTH))
    dproj, grad_x, g_pre_mix = _in_proj_bwd(du, dq, dk, dv, cos_t, sin_t, x, dh1, p["g_pre_mix"], w_in, [ssm_pack])
    publish({"g_pre_mix": g_pre_mix, "w_in": _matmul_tn(dproj, hn, _BF16, "grad_w_in")})
    return grad_x


_MESH = pl.DeviceIdType.MESH
_PEERS = N_DEV - 1


def _mesh_pos():
    return lax.axis_index("x"), lax.axis_index("y"), lax.axis_index("c")


def _dev_index(px, py, pc):
    return 4 * px + 2 * py + pc


def _all_gather(shards, out_dtype, name):
    n = len(shards)

    def body(*refs):
        ins, outs, stages = refs[:n], refs[n:2 * n], refs[2 * n:3 * n]
        send_sems, recv_sems, local_sems = refs[3 * n:]
        x, y, c = _mesh_pos()
        me, sibling = (x, y, c), (x, y, 1 - c)
        chips = [(1 - x, y), (x, 1 - y), (1 - x, 1 - y)]

        def copy(w, k, block, to, src=None):
            slot = outs[w].at[_dev_index(*block)]
            return pltpu.make_async_remote_copy(
                src_ref=slot if src is None else src, dst_ref=slot,
                send_sem=send_sems.at[_PEERS * w + k], recv_sem=recv_sems.at[_PEERS * w + k],
                device_id=to, device_id_type=_MESH)

        for w in range(n):
            stages[w][...] = ins[w][...].astype(out_dtype)
        mine, first, passed = [], [], []
        for w in range(n):
            cp = pltpu.make_async_copy(stages[w], outs[w].at[_dev_index(*me)], local_sems.at[w])
            cp.start()
            mine.append(cp)
            sends = [copy(w, 0, me, sibling, src=stages[w])]
            sends += [copy(w, 1 + j, me, (*chip, c), src=stages[w]) for j, chip in enumerate(chips)]
            for cp in sends:
                cp.start()
            first += sends
        for w in range(n):
            for j, chip in enumerate(chips):
                copy(w, 1 + j, (*chip, c), me).wait_recv()
                cp = copy(w, 4 + j, (*chip, c), sibling)
                cp.start()
                passed.append(cp)
        for w in range(n):
            copy(w, 0, sibling, me).wait_recv()
            for j, chip in enumerate(chips):
                copy(w, 4 + j, (*chip, 1 - c), me).wait_recv()
        for cp in first + passed:
            cp.wait_send()
        for cp in mine:
            cp.wait()

    return pl.pallas_call(
        body, name=name,
        out_shape=[_sds((N_DEV,) + s.shape, out_dtype) for s in shards],
        in_specs=[pl.BlockSpec(memory_space=pltpu.VMEM)] * n,
        out_specs=[pl.BlockSpec(memory_space=pl.ANY)] * n,
        scratch_shapes=[pltpu.VMEM(s.shape, out_dtype) for s in shards]
        + [pltpu.SemaphoreType.DMA((_PEERS * n,)), pltpu.SemaphoreType.DMA((_PEERS * n,)),
           pltpu.SemaphoreType.DMA((n,))],
        compiler_params=pltpu.CompilerParams(vmem_limit_bytes=VMEM_LIMIT),
    )(*shards)


_HBM_SPEC = pl.BlockSpec(memory_space=pltpu.HBM)
_SEM_SPEC = pl.BlockSpec(memory_space=pltpu.SEMAPHORE)
_DATAFLOW = pltpu.SideEffectType.DATAFLOW_SIDE_EFFECTING


def _peer(x, y, c, r):
    return (x ^ ((r >> 2) & 1), y ^ ((r >> 1) & 1), c ^ (r & 1))


def _hbm(a):
    return pltpu.with_memory_space_constraint(a, pltpu.HBM)


def _send_start(sources, blocked, name):
    n = len(sources)
    lands = [lax.empty((N_DEV,) + (s.shape[1:] if blocked else s.shape), s.dtype) for s in sources]

    def body(*refs):
        srcs, zones = refs[:n], refs[n:2 * n]
        send_sems, recv_sems = refs[2 * n:3 * n], refs[3 * n:4 * n]
        token, local_sems = refs[6 * n], refs[6 * n + 1]
        x, y, c = _mesh_pos()
        me = _dev_index(x, y, c)
        local = []
        for w in range(n):
            cp = pltpu.make_async_copy(srcs[w].at[me] if blocked else srcs[w], zones[w].at[me], local_sems.at[w])
            cp.start()
            local.append(cp)
            for r in range(1, N_DEV):
                peer = _peer(x, y, c, r)
                pltpu.make_async_remote_copy(
                    src_ref=srcs[w].at[_dev_index(*peer)] if blocked else srcs[w], dst_ref=zones[w].at[me],
                    send_sem=send_sems[w].at[r - 1], recv_sem=recv_sems[w].at[r - 1],
                    device_id=peer, device_id_type=_MESH).start()
        for cp in local:
            cp.wait()
        token[...] = jnp.zeros_like(token)

    sems = [pltpu.SemaphoreType.DMA((_PEERS,))] * (2 * n)
    out = pl.pallas_call(
        body, name=name,
        out_shape=sems + [pltpu.HBM(a.shape, a.dtype) for a in list(sources) + lands] + [_sds((8, 128), _F32)],
        in_specs=[_HBM_SPEC] * (2 * n),
        out_specs=[_SEM_SPEC] * (2 * n) + [_HBM_SPEC] * (2 * n) + [pl.BlockSpec(memory_space=pltpu.VMEM)],
        input_output_aliases={i: 2 * n + i for i in range(2 * n)},
        scratch_shapes=[pltpu.SemaphoreType.DMA((n,))],
        compiler_params=pltpu.CompilerParams(has_side_effects=_DATAFLOW),
    )(*[_hbm(a) for a in sources], *[_hbm(a) for a in lands])
    return out[:n], out[n:2 * n], out[2 * n:3 * n], out[3 * n:4 * n], out[4 * n]


def _send_wait(send_sems, recv_sems, sources, lands, after, blocked, name):
    n = len(sources)

    def body(*refs):
        srcs, zones = refs[:n], refs[n:2 * n]
        sends, recvs = refs[2 * n:3 * n], refs[3 * n:4 * n]
        x, y, c = _mesh_pos()
        for w in range(n):
            for r in range(1, N_DEV):
                peer = _peer(x, y, c, r)
                idx = _dev_index(*peer)
                cp = pltpu.make_async_remote_copy(
                    src_ref=srcs[w].at[idx] if blocked else srcs[w], dst_ref=zones[w].at[idx],
                    send_sem=sends[w].at[r - 1], recv_sem=recvs[w].at[r - 1],
                    device_id=peer, device_id_type=_MESH)
                cp.wait_send()
                cp.wait_recv()

    out = pl.pallas_call(
        body, name=name,
        out_shape=[pltpu.HBM(a.shape, a.dtype) for a in list(sources) + list(lands)],
        in_specs=[_HBM_SPEC] * (2 * n) + [_SEM_SPEC] * (2 * n) + [pl.BlockSpec(memory_space=pl.ANY)],
        out_specs=[_HBM_SPEC] * (2 * n),
        input_output_aliases={i: i for i in range(2 * n)},
        compiler_params=pltpu.CompilerParams(has_side_effects=_DATAFLOW),
    )(*sources, *lands, *send_sems, *recv_sems, after)
    return out[n:]


def _sequencer_exchange(sources, blocked, name, collective_id):
    n = len(sources)
    flags = blocked

    def body(*refs):
        srcs, zones = refs[:n], refs[n:2 * n]
        send_sems, recv_sems, local_sems = refs[2 * n:]
        x, y, c = _mesh_pos()
        me = _dev_index(x, y, c)
        barrier = pltpu.get_barrier_semaphore()
        for r in range(1, N_DEV):
            pl.semaphore_signal(barrier, inc=1, device_id=_peer(x, y, c, r), device_id_type=_MESH)
        pl.semaphore_wait(barrier, _PEERS)
        local, sends, recvs = [], [], []
        for w in range(n):
            cp = pltpu.make_async_copy(srcs[w].at[me] if flags[w] else srcs[w], zones[w].at[me], local_sems.at[w])
            cp.start()
            local.append(cp)
            for r in range(1, N_DEV):
                peer = _peer(x, y, c, r)
                idx = _dev_index(*peer)
                k = _PEERS * w + r - 1
                src = srcs[w].at[idx] if flags[w] else srcs[w]
                send = pltpu.make_async_remote_copy(
                    src_ref=src, dst_ref=zones[w].at[me], send_sem=send_sems.at[k], recv_sem=recv_sems.at[k],
                    device_id=peer, device_id_type=_MESH)
                send.start()
                sends.append(send)
                recvs.append(pltpu.make_async_remote_copy(
                    src_ref=src, dst_ref=zones[w].at[idx], send_sem=send_sems.at[k], recv_sem=recv_sems.at[k],
                    device_id=peer, device_id_type=_MESH))
        for cp in recvs:
            cp.wait_recv()
        for cp in sends:
            cp.wait_send()
        for cp in local:
            cp.wait()

    return pl.kernel(
        body, name=name,
        out_type=[_sds((N_DEV,) + (s.shape[1:] if f else s.shape), s.dtype) for s, f in zip(sources, flags)],
        mesh=plsc.ScalarSubcoreMesh(axis_name="sequencer", num_cores=1),
        scratch_types=[pltpu.SemaphoreType.DMA((_PEERS * n,)), pltpu.SemaphoreType.DMA((_PEERS * n,)),
                       pltpu.SemaphoreType.DMA((n,))],
        compiler_params=pltpu.CompilerParams(collective_id=collective_id),
    )(*sources)


def _sequencer_gather(shards, name, collective_id):
    n = len(shards)
    fan = 4

    def body(*refs):
        srcs, zones = refs[:n], refs[n:2 * n]
        send_sems, recv_sems, local_sems = refs[2 * n:]
        x, y, c = _mesh_pos()
        me, sibling = (x, y, c), (x, y, 1 - c)
        chips = [(1 - x, y), (x, 1 - y), (1 - x, 1 - y)]
        barrier = pltpu.get_barrier_semaphore()
        for peer in [sibling] + [(*chip, c) for chip in chips]:
            pl.semaphore_signal(barrier, inc=1, device_id=peer, device_id_type=_MESH)
        pl.semaphore_wait(barrier, fan)

        def copy(w, k, block, to, src=None):
            slot = zones[w].at[_dev_index(*block)]
            return pltpu.make_async_remote_copy(
                src_ref=slot if src is None else src, dst_ref=slot,
                send_sem=send_sems.at[_PEERS * w + k], recv_sem=recv_sems.at[_PEERS * w + k],
                device_id=to, device_id_type=_MESH)

        mine, first, passed = [], [], []
        for w in range(n):
            cp = pltpu.make_async_copy(srcs[w], zones[w].at[_dev_index(*me)], local_sems.at[w])
            cp.start()
            mine.append(cp)
            sends = [copy(w, 0, me, sibling, src=srcs[w])]
            sends += [copy(w, 1 + j, me, (*chip, c), src=srcs[w]) for j, chip in enumerate(chips)]
            for cp in sends:
                cp.start()
            first += sends
        for w in range(n):
            for j, chip in enumerate(chips):
                copy(w, 1 + j, (*chip, c), me).wait_recv()
                cp = copy(w, fan + j, (*chip, c), sibling)
                cp.start()
                passed.append(cp)
        for w in range(n):
            copy(w, 0, sibling, me).wait_recv()
            for j, chip in enumerate(chips):
                copy(w, fan + j, (*chip, 1 - c), me).wait_recv()
        for cp in first + passed:
            cp.wait_send()
        for cp in mine:
            cp.wait()

    return pl.kernel(
        body, name=name, out_type=[_sds((N_DEV,) + s.shape, s.dtype) for s in shards],
        mesh=plsc.ScalarSubcoreMesh(axis_name="sequencer", num_cores=1),
        scratch_types=[pltpu.SemaphoreType.DMA((_PEERS * n,)), pltpu.SemaphoreType.DMA((_PEERS * n,)),
                       pltpu.SemaphoreType.DMA((n,))],
        compiler_params=pltpu.CompilerParams(collective_id=collective_id),
    )(*shards)


def _row_tile(rows):
    return next(t for t in range(min(rows, 256), 0, -16) if rows % t == 0)


def _sum_parts(parts, name, tokens=()):
    _, rows, cols = parts.shape
    tr = _row_tile(rows)

    def body(p_ref, g_ref):
        g = p_ref[0].astype(_F32)
        for s in range(1, N_DEV):
            g = g + p_ref[s].astype(_F32)
        g_ref[...] = g

    return _call(body, (rows // tr,), [pl.BlockSpec((N_DEV, tr, cols), lambda i: (0, i, 0))],
                 _rows(tr, cols), _sds((rows, cols), _F32), name, tokens=tokens)(parts)


def _adam_update(g, w, m, v):
    new_m = ADAM_B1 * m + (1.0 - ADAM_B1) * g
    new_v = ADAM_B2 * v + (1.0 - ADAM_B2) * (g * g)
    m_hat = new_m / (1.0 - ADAM_B1 ** ADAM_STEP)
    v_hat = new_v / (1.0 - ADAM_B2 ** ADAM_STEP)
    return -ADAM_LR * (m_hat / (jnp.sqrt(v_hat) + ADAM_EPS) + ADAM_WD * w), new_m, new_v


def _adamw_small(parts, items, sums, name, tokens=()):
    n_p, n_i = len(parts), len(items)

    def body(*refs):
        p_refs, state, outs = refs[:n_p], refs[n_p:n_p + 3 * n_i], refs[n_p + 3 * n_i:]

        def total(part, rows, cols):
            shift = cols.start % _LANES
            window = slice(cols.start - shift, cols.start - shift + _LANES) if shift else cols
            n_rows = rows.stop - rows.start
            narrow = p_refs[part].dtype.itemsize < 4 and n_rows % _PACK_TILE
            tile = slice(rows.start, rows.start + _PACK_TILE) if narrow else rows
            g = p_refs[part][0, tile, window].astype(_F32)
            for s in range(1, N_DEV):
                g = g + p_refs[part][s, tile, window].astype(_F32)
            g = g[:n_rows] if narrow else g
            return pltpu.roll(g, _LANES - shift, 1)[:, :cols.stop - cols.start] if shift else g

        for i, (part, rows, cols, _, _, _) in enumerate(items):
            g = total(part, rows, cols)
            w_ref, m_ref, v_ref = state[3 * i:3 * i + 3]
            delta, new_m, new_v = _adam_update(g, w_ref[...], m_ref[...], v_ref[...])
            outs[4 * i][...] = g
            outs[4 * i + 1][...] = delta
            outs[4 * i + 2][...] = new_m
            outs[4 * i + 3][...] = new_v
        for j, (part, rows, cols) in enumerate(sums):
            outs[4 * n_i + j][...] = total(part, rows, cols)

    ins = list(parts) + [a for item in items for a in item[3:]]
    out_shapes = [item[3].shape for item in items for _ in range(4)]
    out_shapes += [(rows.stop - rows.start, cols.stop - cols.start) for _, rows, cols in sums]
    out = _call(body, (1,), [_whole(a.shape) for a in ins], [_whole(s) for s in out_shapes],
                [_sds(s, _F32) for s in out_shapes], name, tokens=tokens)(*ins)
    return [out[4 * i:4 * i + 4] for i in range(n_i)], out[4 * n_i:]


def _adamw(parts, w, m, v, name, tokens=()):
    rows, cols = w.shape
    tr = _row_tile(rows)
    n_parts = parts.shape[0]

    def body(p_ref, w_ref, m_ref, v_ref, g_ref, d_ref, nm_ref, nv_ref):
        g = p_ref[0].astype(_F32)
        for s in range(1, n_parts):
            g = g + p_ref[s].astype(_F32)
        new_m = ADAM_B1 * m_ref[...] + (1.0 - ADAM_B1) * g
        new_v = ADAM_B2 * v_ref[...] + (1.0 - ADAM_B2) * (g * g)
        m_hat = new_m / (1.0 - ADAM_B1 ** ADAM_STEP)
        v_hat = new_v / (1.0 - ADAM_B2 ** ADAM_STEP)
        g_ref[...] = g
        d_ref[...] = -ADAM_LR * (m_hat / (jnp.sqrt(v_hat) + ADAM_EPS) + ADAM_WD * w_ref[...])
        nm_ref[...] = new_m
        nv_ref[...] = new_v

    blk = _rows(tr, cols)
    return _call(body, (rows // tr,),
                 [pl.BlockSpec((n_parts, tr, cols), lambda i: (0, i, 0)), blk, blk, blk],
                 [blk] * 4, [_sds((rows, cols), _F32)] * 4, name, tokens=tokens)(parts, w, m, v)


_SMALL = ("g_pre_mix", "ssm_lambda_re", "ssm_lambda_im", "ssm_log_dt", "ssm_b_re", "ssm_b_im",
          "ssm_c_re", "ssm_c_im", "ssm_d", "b_glu", "attn_sinks", "g_ssm_out", "g_attn_out",
          "g_post_mix", "g_pre_ffn", "g_post_ffn")
_BIG = ("w_in", "w_glu", "w_out", "w_gate_up", "w_down")
_WEIGHTS = ("g_pre_mix", "w_in", "ssm_lambda_re", "ssm_lambda_im", "ssm_log_dt", "ssm_b_re", "ssm_b_im",
            "ssm_c_re", "ssm_c_im", "ssm_d", "w_glu", "b_glu", "attn_sinks", "g_ssm_out", "g_attn_out",
            "w_out", "g_post_mix", "g_pre_ffn", "w_gate_up", "w_down", "g_post_ffn")
_LANES = 128


_SHAPE_2D = {
    "g_pre_mix": (1, D_MODEL), "ssm_lambda_re": (SSM_GROUPS, SSM_STATE), "ssm_lambda_im": (SSM_GROUPS, SSM_STATE),
    "ssm_log_dt": (1, SSM_GROUPS), "ssm_b_re": (SSM_WIDTH, SSM_STATE), "ssm_b_im": (SSM_WIDTH, SSM_STATE),
    "ssm_c_re": (SSM_WIDTH, SSM_STATE), "ssm_c_im": (SSM_WIDTH, SSM_STATE), "ssm_d": (SSM_GROUPS, SSM_GROUP),
    "b_glu": (1, 2 * SSM_WIDTH), "attn_sinks": (1, N_Q_HEADS), "g_ssm_out": (1, SSM_WIDTH),
    "g_attn_out": (1, ATTN_WIDTH), "g_post_mix": (1, D_MODEL), "g_pre_ffn": (1, D_MODEL), "g_post_ffn": (1, D_MODEL)}
_ROW_WIDTH = {"g_pre_mix": D_MODEL, "b_glu": 2 * SSM_WIDTH, "attn_sinks": _LANES, "g_ssm_out": SSM_WIDTH,
              "g_attn_out": ATTN_WIDTH, "g_post_mix": D_MODEL, "g_pre_ffn": D_MODEL, "g_post_ffn": D_MODEL,
              "loss": _LANES}
_DENSE = ()
_PER_GROUP_TRANSPOSED = ("ssm_b_re", "ssm_b_im")


def _to_2d(name, a):
    if name in _PER_GROUP_TRANSPOSED:
        a = a.reshape(SSM_GROUPS, SSM_STATE, SSM_GROUP).transpose(0, 2, 1)
    return a.reshape(_SHAPE_2D[name])


def _from_2d(name, a, shape):
    if name in _PER_GROUP_TRANSPOSED:
        a = a.reshape(SSM_GROUPS, SSM_GROUP, SSM_STATE).transpose(0, 2, 1)
    return a.reshape(shape)


def _row_slots(names):
    slots, row, col = {}, 0, 0
    for n in names:
        width = _ROW_WIDTH[n]
        if col + width > D_MODEL:
            row, col = row + 1, 0
        slots[n] = (row, col, width)
        col += width
    return slots


def _stack_rows(named, slots):
    n_rows = -(-(max(r for r, _, _ in slots.values()) + 1) // 8) * 8
    lines = []
    for r in range(n_rows):
        pieces = [named[n] for n, (row, _, _) in slots.items() if row == r]
        used = sum(p.shape[1] for p in pieces)
        if used < D_MODEL:
            pieces.append(jnp.zeros((1, D_MODEL - used), _F32))
        lines.append(jnp.concatenate(pieces, axis=1) if len(pieces) > 1 else pieces[0])
    return jnp.concatenate(lines, axis=0)


def kernel(x, positions, g_pre_mix, w_in, ssm_lambda_re, ssm_lambda_im, ssm_log_dt, ssm_b_re, ssm_b_im, ssm_c_re, ssm_c_im, ssm_d, w_glu, b_glu, attn_sinks, g_ssm_out, g_attn_out, w_out, g_post_mix, g_pre_ffn, w_gate_up, w_down, g_post_ffn, loss_target, m_g_pre_mix, m_w_in, m_ssm_lambda_re, m_ssm_lambda_im, m_ssm_log_dt, m_ssm_b_re, m_ssm_b_im, m_ssm_c_re, m_ssm_c_im, m_ssm_d, m_w_glu, m_b_glu, m_attn_sinks, m_g_ssm_out, m_g_attn_out, m_w_out, m_g_post_mix, m_g_pre_ffn, m_w_gate_up, m_w_down, m_g_post_ffn, v_g_pre_mix, v_w_in, v_ssm_lambda_re, v_ssm_lambda_im, v_ssm_log_dt, v_ssm_b_re, v_ssm_b_im, v_ssm_c_re, v_ssm_c_im, v_ssm_d, v_w_glu, v_b_glu, v_attn_sinks, v_g_ssm_out, v_g_attn_out, v_w_out, v_g_post_mix, v_g_pre_ffn, v_w_gate_up, v_w_down, v_g_post_ffn):
    w = dict(g_pre_mix=g_pre_mix, w_in=w_in, ssm_lambda_re=ssm_lambda_re, ssm_lambda_im=ssm_lambda_im,
             ssm_log_dt=ssm_log_dt, ssm_b_re=ssm_b_re, ssm_b_im=ssm_b_im, ssm_c_re=ssm_c_re, ssm_c_im=ssm_c_im,
             ssm_d=ssm_d, w_glu=w_glu, b_glu=b_glu, attn_sinks=attn_sinks, g_ssm_out=g_ssm_out,
             g_attn_out=g_attn_out, w_out=w_out, g_post_mix=g_post_mix, g_pre_ffn=g_pre_ffn,
             w_gate_up=w_gate_up, w_down=w_down, g_post_ffn=g_post_ffn)
    m = dict(g_pre_mix=m_g_pre_mix, w_in=m_w_in, ssm_lambda_re=m_ssm_lambda_re, ssm_lambda_im=m_ssm_lambda_im,
             ssm_log_dt=m_ssm_log_dt, ssm_b_re=m_ssm_b_re, ssm_b_im=m_ssm_b_im, ssm_c_re=m_ssm_c_re,
             ssm_c_im=m_ssm_c_im, ssm_d=m_ssm_d, w_glu=m_w_glu, b_glu=m_b_glu, attn_sinks=m_attn_sinks,
             g_ssm_out=m_g_ssm_out, g_attn_out=m_g_attn_out, w_out=m_w_out, g_post_mix=m_g_post_mix,
             g_pre_ffn=m_g_pre_ffn, w_gate_up=m_w_gate_up, w_down=m_w_down, g_post_ffn=m_g_post_ffn)
    v = dict(g_pre_mix=v_g_pre_mix, w_in=v_w_in, ssm_lambda_re=v_ssm_lambda_re, ssm_lambda_im=v_ssm_lambda_im,
             ssm_log_dt=v_ssm_log_dt, ssm_b_re=v_ssm_b_re, ssm_b_im=v_ssm_b_im, ssm_c_re=v_ssm_c_re,
             ssm_c_im=v_ssm_c_im, ssm_d=v_ssm_d, w_glu=v_w_glu, b_glu=v_b_glu, attn_sinks=v_attn_sinks,
             g_ssm_out=v_g_ssm_out, g_attn_out=v_g_attn_out, w_out=v_w_out, g_post_mix=v_g_post_mix,
             g_pre_ffn=v_g_pre_ffn, w_gate_up=v_w_gate_up, w_down=v_w_down, g_post_ffn=v_g_post_ffn)

    transposed = ("w_in", "w_glu", "w_gate_up")
    native_transposed = ("w_in", "w_gate_up")
    shard = {n: (w[n][0].T if n in transposed else w[n][0]).astype(_BF16) for n in _BIG}
    gathered = {}
    for names, lands in (
            (("w_in",), _sequencer_exchange([shard["w_in"]], [False], "gather_w_in", 1)),
            (("w_glu", "w_out"), _sequencer_exchange([shard["w_glu"], shard["w_out"]], [False] * 2, "gather_mix", 2)),
            (("w_gate_up", "w_down"), _sequencer_gather([shard["w_gate_up"], shard["w_down"]], "gather_ffn", 3))):
        gathered.update({n: a.reshape(-1, a.shape[2]) for n, a in zip(names, lands)})

    def fetch(names, after):
        del after
        return [gathered[n] for n in names]

    sent = []

    def publish(named):
        big = [n for n in named if n in _BIG]
        rows = [n for n in named if n in _ROW_WIDTH]
        dense = [n for n in named if n in _DENSE]
        plain = [n for n in named if n not in big + rows + dense]
        sources = [named[n].reshape(N_DEV, -1, named[n].shape[1]) for n in big]
        slots = _row_slots(rows)
        if rows:
            sources.append(_stack_rows(named, slots))
        sources += [named[n].reshape(-1, _LANES) for n in dense] + [named[n] for n in plain]
        flags = [True] * len(big) + [False] * (len(sources) - len(big))
        cid = 4 + len(sent)
        sent.append((big, slots, dense, plain, _sequencer_exchange(sources, flags, "grads_%d" % cid, cid)))
        return [named[n] for n in big]

    p = {n: w[n] for n in _SMALL}
    grad_x = _local_step(x[0], positions[0], loss_target[0], p, fetch, publish)

    state = {n: [_to_2d(n, a) for a in (w[n], m[n], v[n])] for n in _SMALL}
    result = {}
    total_loss = None
    chain = []
    for big, slots, dense, plain, lands in sent:
        lands = list(lands)
        after = list(chain)
        for name in big:
            part = lands.pop(0)
            if name in native_transposed:
                updated = _adamw(part, w[name][0].T, m[name][0].T, v[name][0].T, "adamw_" + name, after)
                result[name] = [a.T[None] for a in updated]
                chain = [updated[3]]
                continue
            if name in transposed:
                part = _sum_parts(part, "sum_" + name, after).T[None]
            updated = _adamw(part, w[name][0], m[name][0], v[name][0], "adamw_" + name, after)
            result[name] = [a[None] for a in updated]
            chain = [updated[3]]
        parts, items, sums, names = [], [], [], []
        if slots:
            parts.append(lands.pop(0))
            for name, (row, col, _) in slots.items():
                if name == "loss":
                    sums.append((0, slice(row, row + 1), slice(col, col + _LANES)))
                else:
                    items.append((0, slice(row, row + 1), slice(col, col + _SHAPE_2D[name][1]), *state[name]))
                    names.append(name)
        for name in dense:
            part = lands.pop(0).reshape((N_DEV,) + _SHAPE_2D[name])
            result[name] = _adamw(part, *state[name], "adamw_" + name, after)
            chain = [result[name][3]]
        for name in plain:
            packed = _SSM_PACK if name == "ssm_pack" else {name: (0, _SHAPE_2D[name][0], 0, _SHAPE_2D[name][1])}
            for member, (first, rows_n, lane, cols_n) in packed.items():
                items.append((len(parts), slice(first, first + rows_n), slice(lane, lane + cols_n), *state[member]))
                names.append(member)
            parts.append(lands.pop(0))
        if items:
            updated, summed = _adamw_small(parts, items, sums, "adamw_small_" + names[0], after)
            chain = [updated[0][3]]
            result.update(dict(zip(names, updated)))
            if summed:
                total_loss = summed[0][0, 0]

    out = [total_loss, grad_x[None]]
    for kind in range(4):
        out += [_from_2d(n, result[n][kind], w[n].shape) for n in _WEIGHTS]
    return tuple(out)
```

```python
import functools
import math

import numpy as np
import jax
import jax.numpy as jnp
from jax import lax
from jax.experimental import pallas as pl
from jax.experimental.pallas import tpu as pltpu
from jax.experimental.pallas import tpu_sc as plsc

D_MODEL = 1024
SSM_WIDTH = 512
SSM_GROUP = 16
SSM_GROUPS = 32
SSM_STATE = 64
N_STATE = SSM_GROUPS * SSM_STATE
ATTN_WIDTH = 512
HEAD_DIM = 64
N_Q_HEADS = 8
N_KV_HEADS = 2
Q_PER_KV = 4
KV_WIDTH = 128
IN_WIDTH = 1280
BLOCK = 128
ROPE_DIM = 16
ROPE_THETA = 500000.0
D_FF = 2816
NORM_EPS = 1e-6
MASK_VALUE = -1e30
ADAM_LR = 0.001
ADAM_B1 = 0.9
ADAM_B2 = 0.999
ADAM_EPS = 1e-08
ADAM_WD = 0.01
ADAM_STEP = 10

N_DEV = 8
SCAN_CHUNKS = 8
SCAN_COLS = 512
TOKEN_TILE = 256
VMEM_LIMIT = 56 * 1024 * 1024

_F32 = jnp.float32
_BF16 = jnp.bfloat16
_MXU = jnp.bfloat16

_NN = ((1,), (0,))
_NT = ((1,), (1,))
_TN = ((0,), (0,))


def _dot(a, b, dims):
    return lax.dot_general(a.astype(_MXU), b.astype(_MXU), (dims, ((), ())),
                           preferred_element_type=_F32)


def _dot_exact(a, b, dims):
    return lax.dot_general(a.astype(_F32), b.astype(_F32), (dims, ((), ())),
                           precision=lax.Precision.HIGHEST, preferred_element_type=_F32)


def _iota(shape, dim):
    return lax.broadcasted_iota(jnp.int32, shape, dim)


def _rms_fwd(x, g):
    r = lax.rsqrt(jnp.mean(x * x, axis=-1, keepdims=True) + NORM_EPS)
    return x * r * g, r


def _rms_bwd(dy, x, g, r):
    a = dy * g
    xn = x * r
    dx = r * (a - xn * jnp.mean(a * xn, axis=-1, keepdims=True))
    dg = jnp.sum(dy * xn, axis=0, keepdims=True)
    return dx, dg


def _call(body, grid, in_specs, out_specs, out_shape, name, scratch=(), tokens=()):
    params = pltpu.CompilerParams(dimension_semantics=("arbitrary",) * len(grid),
                                  vmem_limit_bytes=VMEM_LIMIT)
    n_in, n_tok = len(in_specs), len(tokens)

    def run(*refs):
        return body(*refs[:n_in], *refs[n_in + n_tok:])

    call = pl.pallas_call(run, grid=grid,
                          in_specs=list(in_specs) + [pl.BlockSpec(memory_space=pl.ANY)] * n_tok,
                          out_specs=out_specs, out_shape=out_shape, scratch_shapes=list(scratch),
                          compiler_params=params, name=name)
    return lambda *args: call(*args, *tokens)


def _rows(tm, n):
    return pl.BlockSpec((tm, n), lambda i: (i, 0))


def _whole(shape):
    nd = len(shape)
    return pl.BlockSpec(shape, lambda i: (0,) * nd)


def _sds(shape, dtype):
    return jax.ShapeDtypeStruct(shape, dtype)


def _tile(L):
    return min(TOKEN_TILE, L)


def _chunk_tile(L):
    return L // SCAN_CHUNKS


def _chunk_block(L, n):
    return pl.BlockSpec((_chunk_tile(L), n), lambda i: (0, i))


def _chunk_shape(L, n):
    return (_chunk_tile(L), SCAN_CHUNKS * n)


def _accumulate(ref, val, first):
    @pl.when(first)
    def _():
        ref[...] = val

    @pl.when(jnp.logical_not(first))
    def _():
        ref[...] += val


def _rope_rows():
    half = ROPE_DIM // 2
    inv = (np.float32(ROPE_THETA) ** (-np.arange(half, dtype=np.float32) * np.float32(2.0) / np.float32(ROPE_DIM))).astype(np.float32)
    col = np.arange(KV_WIDTH) % HEAD_DIM
    freq = np.where(col < ROPE_DIM, inv[col % half], 0.0).astype(np.float32)
    sign = np.where(col < half, -1.0, np.where(col < ROPE_DIM, 1.0, 0.0)).astype(np.float32)
    return freq[None, :], sign[None, :]


def _rope_tables(pos_col):
    L = pos_col.shape[0]
    tm = _tile(L)
    freq, sign = _rope_rows()

    def body(pos_ref, freq_ref, sign_ref, cos_ref, sin_ref):
        ang = pos_ref[...].astype(_F32) * freq_ref[...]
        cos_ref[...] = jnp.cos(ang)
        sin_ref[...] = jnp.sin(ang) * sign_ref[...]

    return _call(body, (L // tm,),
                 [_rows(tm, 1), _whole((1, KV_WIDTH)), _whole((1, KV_WIDTH))],
                 [_rows(tm, KV_WIDTH), _rows(tm, KV_WIDTH)],
                 [_sds((L, KV_WIDTH), _F32)] * 2, "rope_tables")(pos_col, jnp.asarray(freq), jnp.asarray(sign))


def _widen(t, width):
    return t if width == KV_WIDTH else jnp.concatenate([t] * (width // KV_WIDTH), axis=1)


def _rope_partner(t):
    w = t.shape[1]
    in_head = _iota((1, w), 1) & (HEAD_DIM - 1)
    second = jnp.where(in_head < ROPE_DIM, pltpu.roll(t, ROPE_DIM // 2, 1), 0.0)
    return jnp.where(in_head < ROPE_DIM // 2, pltpu.roll(t, w - ROPE_DIM // 2, 1), second)


def _rope_apply(t, cos_t, sin_t):
    w = t.shape[1]
    return t * _widen(cos_t, w) + _rope_partner(t) * _widen(sin_t, w)


def _rope_transpose(dt, cos_t, sin_t):
    w = dt.shape[1]
    return dt * _widen(cos_t, w) + _rope_partner(dt * _widen(sin_t, w))


def _in_proj(x, g_pre_mix, w_in, cos_t, sin_t):
    L = x.shape[0]
    tm = _chunk_tile(L)

    def body(x_ref, g_ref, w_ref, cos_ref, sin_ref, hn_ref, u_ref, q_ref, k_ref, v_ref):
        hn, _ = _rms_fwd(x_ref[...], g_ref[...])
        hn = hn.astype(_BF16)
        hn_ref[...] = hn
        proj = _dot(hn, w_ref[...], _NT)
        u_ref[...] = proj[:, :SSM_WIDTH]
        q = proj[:, SSM_WIDTH:SSM_WIDTH + ATTN_WIDTH]
        k = proj[:, SSM_WIDTH + ATTN_WIDTH:SSM_WIDTH + ATTN_WIDTH + KV_WIDTH]
        cos_v, sin_v = cos_ref[...], sin_ref[...]
        q_ref[...] = _rope_apply(q, cos_v, sin_v).astype(_BF16)
        k_ref[...] = _rope_apply(k, cos_v, sin_v).astype(_BF16)
        v_ref[...] = proj[:, SSM_WIDTH + ATTN_WIDTH + KV_WIDTH:].astype(_BF16)

    return _call(body, (L // tm,),
                 [_rows(tm, D_MODEL), _whole((1, D_MODEL)), _whole((IN_WIDTH, D_MODEL)),
                  _rows(tm, KV_WIDTH), _rows(tm, KV_WIDTH)],
                 [_rows(tm, D_MODEL), _chunk_block(L, SSM_WIDTH), _rows(tm, ATTN_WIDTH),
                  _rows(tm, KV_WIDTH), _rows(tm, KV_WIDTH)],
                 [_sds((L, D_MODEL), _BF16), _sds(_chunk_shape(L, SSM_WIDTH), _F32), _sds((L, ATTN_WIDTH), _BF16),
                  _sds((L, KV_WIDTH), _BF16), _sds((L, KV_WIDTH), _BF16)],
                 "in_proj")(x, g_pre_mix, w_in, cos_t, sin_t)


def _s5_discretize(lam_re, lam_im, log_dt):
    lr = jnp.minimum(lam_re, -1e-4)
    li = lam_im
    dt = jnp.exp(log_dt)
    mag = jnp.exp(lr * dt)
    ar = mag * jnp.cos(li * dt)
    ai = mag * jnp.sin(li * dt)
    den = lr * lr + li * li
    fr = ((ar - 1.0) * lr + ai * li) / den
    fi = (ai * lr - (ar - 1.0) * li) / den
    return ar, ai, fr, fi


def _s5_bbar(lam_re, lam_im, log_dt, b_re, b_im):
    ar, ai, fr, fi = _s5_discretize(lam_re, lam_im, log_dt)
    return ar, ai, fr * b_re - fi * b_im, fr * b_im + fi * b_re


def _spread_masks():
    e16 = (_iota((SSM_GROUP, SSM_WIDTH), 1) & (SSM_GROUP - 1)) == _iota((SSM_GROUP, SSM_WIDTH), 0)
    e64 = (_iota((SSM_STATE, N_STATE), 1) & (SSM_STATE - 1)) == _iota((SSM_STATE, N_STATE), 0)
    mask_b = (_iota((N_STATE, SSM_WIDTH), 0) >> 6) == (_iota((N_STATE, SSM_WIDTH), 1) >> 4)
    mask_c = (_iota((SSM_WIDTH, N_STATE), 0) >> 4) == (_iota((SSM_WIDTH, N_STATE), 1) >> 6)
    return e16.astype(_F32), e64.astype(_F32), mask_b, mask_c


SUPER = 4
SB_STATE = N_STATE // SUPER
SB_WIDTH = SSM_WIDTH // SUPER


def _sb_state(k):
    return slice(SB_STATE * k, SB_STATE * (k + 1))


def _sb_width(k):
    return slice(SB_WIDTH * k, SB_WIDTH * (k + 1))


def _dt_column(log_dt_row):
    eye = _iota((SSM_GROUPS, SSM_GROUPS), 0) == _iota((SSM_GROUPS, SSM_GROUPS), 1)
    return jnp.sum(jnp.where(eye, log_dt_row, 0.0), axis=1, keepdims=True)


def _group_masks():
    e64 = ((_iota((SSM_STATE, N_STATE), 1) & (SSM_STATE - 1)) == _iota((SSM_STATE, N_STATE), 0)).astype(_F32)
    own = _iota((SSM_GROUPS, N_STATE), 0) == (_iota((SSM_GROUPS, N_STATE), 1) >> 6)
    return e64, own


def _rows_of_group():
    return ((_iota((SSM_WIDTH, SSM_GROUPS), 0) >> 4) == _iota((SSM_WIDTH, SSM_GROUPS), 1)).astype(_F32)


def _ssm_prep(lam_re, lam_im, log_dt, b_re, b_im, c_re, c_im):
    def body(lr_ref, li_ref, ld_ref, bre, bim, cre, cim, ar_ref, ai_ref, btr, bti, ctr, cti):
        ar, ai, fr, fi = _s5_discretize(lr_ref[...], li_ref[...], _dt_column(ld_ref[...]))
        e64, own = _group_masks()
        mask_c = (_iota((SSM_WIDTH, N_STATE), 0) >> 4) == (_iota((SSM_WIDTH, N_STATE), 1) >> 6)

        def to_row(t):
            return jnp.sum(jnp.where(own, _dot_exact(t, e64, _NN), 0.0), axis=0, keepdims=True)

        def fold(m):
            full = jnp.where(mask_c, _dot(m, e64, _NN), 0.0)
            return sum(full[_sb_width(k), :] for k in range(SUPER)).astype(_BF16)

        ar_ref[...] = to_row(ar)
        ai_ref[...] = to_row(ai)
        spread = _rows_of_group()
        fr_t = _dot_exact(spread, fr, _NN)
        fi_t = _dot_exact(spread, fi, _NN)
        btr[...] = fold(fr_t * bre[...] - fi_t * bim[...])
        bti[...] = fold(fr_t * bim[...] + fi_t * bre[...])
        ctr[...] = fold(cre[...])
        cti[...] = fold(cim[...])

    row = (1, N_STATE)
    ins = [lam_re, lam_im, log_dt, b_re, b_im, c_re, c_im]
    return _call(body, (1,), [_whole(a.shape) for a in ins],
                 [_whole(row), _whole(row)] + [_whole((SB_WIDTH, N_STATE))] * 4,
                 [_sds(row, _F32), _sds(row, _F32)] + [_sds((SB_WIDTH, N_STATE), _BF16)] * 4,
                 "ssm_prep")(*ins)


def _ssm_bu(u, bt_re, bt_im):
    L = u.shape[0]
    tm = _tile(L)

    def body(u_ref, br_ref, bi_ref, or_ref, oi_ref):
        for k in range(SUPER):
            ub = u_ref[:, _sb_width(k)].astype(_BF16)
            or_ref[:, _sb_state(k)] = _dot(ub, br_ref[:, _sb_state(k)], _NN)
            oi_ref[:, _sb_state(k)] = _dot(ub, bi_ref[:, _sb_state(k)], _NN)

    return _call(body, (L // tm,),
                 [_rows(tm, SSM_WIDTH), _whole((SB_WIDTH, N_STATE)), _whole((SB_WIDTH, N_STATE))],
                 [_rows(tm, N_STATE), _rows(tm, N_STATE)],
                 [_sds((L, N_STATE), _F32)] * 2, "ssm_bu")(u, bt_re, bt_im)


def _complex_power(ar, ai, n):
    def step(_, c):
        pr, pi = c
        return pr * ar - pi * ai, pr * ai + pi * ar
    return lax.fori_loop(0, n, step, (jnp.ones_like(ar), jnp.zeros_like(ai)))


def _chunk_carries(er, ei, pr, pi, reverse):
    rows = _iota(er.shape, 0)
    sr = jnp.zeros_like(pr)
    si = jnp.zeros_like(pi)
    out_r = jnp.zeros_like(er)
    out_i = jnp.zeros_like(ei)
    order = range(SCAN_CHUNKS - 1, 0, -1) if reverse else range(SCAN_CHUNKS - 1)
    for c in order:
        e_r = er[c:c + 1, :]
        e_i = ei[c:c + 1, :]
        sr, si = pr * sr - pi * si + e_r, pr * si + pi * sr + e_i
        nxt = c - 1 if reverse else c + 1
        out_r = jnp.where(rows == nxt, sr, out_r)
        out_i = jnp.where(rows == nxt, si, out_i)
    return out_r, out_i


def _scan_fwd(b_re, b_im, a_re, a_im):
    T = b_re.shape[0]
    W = SCAN_COLS
    blk = pl.BlockSpec((T, SCAN_CHUNKS, W), lambda j: (0, 0, j))
    vec = pl.BlockSpec((1, W), lambda j: (0, j))

    def body(br_ref, bi_ref, ar_ref, ai_ref, xr_ref, xi_ref):
        ar, ai = ar_ref[...], ai_ref[...]
        ar8 = jnp.broadcast_to(ar, (SCAN_CHUNKS, W))
        ai8 = jnp.broadcast_to(ai, (SCAN_CHUNKS, W))

        def local(t, c):
            cr, ci = c
            return ar8 * cr - ai8 * ci + br_ref[t], ar8 * ci + ai8 * cr + bi_ref[t]

        zero = jnp.zeros((SCAN_CHUNKS, W), _F32)
        er, ei = lax.fori_loop(0, T, local, (zero, zero))
        pr, pi = _complex_power(ar, ai, T)
        sr, si = _chunk_carries(er, ei, pr, pi, reverse=False)

        def final(t, c):
            nr, ni = local(t, c)
            xr_ref[t] = nr
            xi_ref[t] = ni
            return nr, ni

        lax.fori_loop(0, T, final, (sr, si))

    shape = _sds(b_re.shape, _F32)
    return _call(body, (N_STATE // W,), [blk, blk, vec, vec], [blk, blk], [shape, shape],
                 "scan_fwd")(b_re, b_im, a_re, a_im)


def _scan_bwd(dx_re, dx_im, x_re, x_im, a_re, a_im, tokens=()):
    T = dx_re.shape[0]
    W = SCAN_COLS
    blk = pl.BlockSpec((T, SCAN_CHUNKS, W), lambda j: (0, 0, j))
    vec = pl.BlockSpec((1, W), lambda j: (0, j))

    def body(dr_ref, di_ref, xr_ref, xi_ref, ar_ref, ai_ref, lr_ref, li_ref, dar_ref, dai_ref):
        ar, ai = ar_ref[...], ai_ref[...]
        ar8 = jnp.broadcast_to(ar, (SCAN_CHUNKS, W))
        ai8 = jnp.broadcast_to(ai, (SCAN_CHUNKS, W))

        def local(t, c):
            cr, ci = c
            return ar8 * cr + ai8 * ci + dr_ref[t], ar8 * ci - ai8 * cr + di_ref[t]

        zero = jnp.zeros((SCAN_CHUNKS, W), _F32)
        er, ei = lax.fori_loop(0, T, lambda k, c: local(T - 1 - k, c), (zero, zero))
        pr, pi = _complex_power(ar, -ai, T)
        sr, si = _chunk_carries(er, ei, pr, pi, reverse=True)

        def grad_a(acc, nr, ni, xpr, xpi):
            return acc[0] + nr * xpr + ni * xpi, acc[1] + ni * xpr - nr * xpi

        def final(k, c):
            t = T - 1 - k
            nr, ni = local(t, c[:2])
            lr_ref[t] = nr
            li_ref[t] = ni
            gr, gi = grad_a(c[2:], nr, ni, xr_ref[t - 1], xi_ref[t - 1])
            return nr, ni, gr, gi

        cr, ci, gr, gi = lax.fori_loop(0, T - 1, final, (sr, si, zero, zero))
        nr, ni = local(0, (cr, ci))
        lr_ref[0] = nr
        li_ref[0] = ni
        first = _iota((SCAN_CHUNKS, W), 0) == 0
        xpr = jnp.where(first, 0.0, pltpu.roll(xr_ref[T - 1], 1, 0))
        xpi = jnp.where(first, 0.0, pltpu.roll(xi_ref[T - 1], 1, 0))
        gr, gi = grad_a((gr, gi), nr, ni, xpr, xpi)
        dar_ref[...] = jnp.sum(gr, axis=0, keepdims=True)
        dai_ref[...] = jnp.sum(gi, axis=0, keepdims=True)

    shape = _sds(dx_re.shape, _F32)
    row = _sds((1, N_STATE), _F32)
    return _call(body, (N_STATE // W,), [blk, blk, blk, blk, vec, vec], [blk, blk, vec, vec],
                 [shape, shape, row, row], "scan_bwd", tokens=tokens)(dx_re, dx_im, x_re, x_im, a_re, a_im)


_GELU_K = math.sqrt(2.0 / math.pi)
_GELU_C = 0.044715


def _gelu(y):
    return 0.5 * y * (1.0 + jnp.tanh(_GELU_K * (y + _GELU_C * y * y * y)))


def _gelu_grad(y):
    t = jnp.tanh(_GELU_K * (y + _GELU_C * y * y * y))
    return 0.5 * (1.0 + t) + 0.5 * y * (1.0 - t * t) * _GELU_K * (1.0 + 3.0 * _GELU_C * y * y)


def _ssm_out(x_re, x_im, u, ct_re, ct_im, d_row, w_glu, b_glu, g_ssm):
    L = u.shape[0]
    tm = _tile(L)

    def body(xr_ref, xi_ref, u_ref, cr_ref, ci_ref, d_ref, w_ref, b_ref, g_ref, y_ref, z_ref, n_ref):
        cx = [_dot(xr_ref[:, _sb_state(k)], cr_ref[:, _sb_state(k)], _NT)
              - _dot(xi_ref[:, _sb_state(k)], ci_ref[:, _sb_state(k)], _NT) for k in range(SUPER)]
        y = jnp.concatenate(cx, axis=1) + d_ref[...] * u_ref[...]
        y_ref[...] = y
        z = _dot(_gelu(y), w_ref[...], _NT) + b_ref[...]
        z_ref[...] = z
        out = z[:, :SSM_WIDTH] * jax.nn.sigmoid(z[:, SSM_WIDTH:])
        n, _ = _rms_fwd(out, g_ref[...])
        n_ref[...] = n.astype(_BF16)

    return _call(body, (L // tm,),
                 [_rows(tm, N_STATE), _rows(tm, N_STATE), _rows(tm, SSM_WIDTH),
                  _whole((SB_WIDTH, N_STATE)), _whole((SB_WIDTH, N_STATE)), _whole((1, SSM_WIDTH)),
                  _whole((2 * SSM_WIDTH, SSM_WIDTH)), _whole((1, 2 * SSM_WIDTH)), _whole((1, SSM_WIDTH))],
                 [_rows(tm, SSM_WIDTH), _rows(tm, 2 * SSM_WIDTH), _rows(tm, SSM_WIDTH)],
                 [_sds((L, SSM_WIDTH), _F32), _sds((L, 2 * SSM_WIDTH), _F32), _sds((L, SSM_WIDTH), _BF16)],
                 "ssm_out")(x_re, x_im, u, ct_re, ct_im, d_row, w_glu, b_glu, g_ssm)


def _ssm_out_bwd(dn, y, z, u, ct_re, ct_im, d_row, w_glu, g_ssm):
    L = u.shape[0]
    tm = _tile(L)

    def body(dn_ref, y_ref, z_ref, u_ref, cr_ref, ci_ref, d_ref, w_ref, g_ref,
             gy_ref, dz_ref, dy_ref, dud_ref, dxr_ref, dxi_ref, dg_ref, db_ref, dd_ref):
        first = pl.program_id(0) == 0
        z = z_ref[...]
        z1, z2 = z[:, :SSM_WIDTH], z[:, SSM_WIDTH:]
        sig = jax.nn.sigmoid(z2)
        out = z1 * sig
        g = g_ref[...]
        _, r = _rms_fwd(out, g)
        dout, dg = _rms_bwd(dn_ref[...], out, g, r)
        _accumulate(dg_ref, dg, first)
        dz = jnp.concatenate([dout * sig, dout * z1 * sig * (1.0 - sig)], axis=1)
        _accumulate(db_ref, jnp.sum(dz, axis=0, keepdims=True), first)
        dzb = dz.astype(_BF16)
        dz_ref[...] = dzb
        y = y_ref[...]
        gy_ref[...] = _gelu(y).astype(_BF16)
        dy = _dot(dzb, w_ref[...], _NN) * _gelu_grad(y)
        u = u_ref[...]
        _accumulate(dd_ref, jnp.sum(dy * u, axis=0, keepdims=True), first)
        dud_ref[...] = d_ref[...] * dy
        dyb = dy.astype(_BF16)
        dy_ref[...] = dyb
        for k in range(SUPER):
            dxr_ref[:, _sb_state(k)] = _dot(dyb[:, _sb_width(k)], cr_ref[:, _sb_state(k)], _NN)
            dxi_ref[:, _sb_state(k)] = -_dot(dyb[:, _sb_width(k)], ci_ref[:, _sb_state(k)], _NN)

    row = _whole((1, SSM_WIDTH))
    return _call(body, (L // tm,),
                 [_rows(tm, SSM_WIDTH), _rows(tm, SSM_WIDTH), _rows(tm, 2 * SSM_WIDTH), _rows(tm, SSM_WIDTH),
                  _whole((SB_WIDTH, N_STATE)), _whole((SB_WIDTH, N_STATE)), row,
                  _whole((2 * SSM_WIDTH, SSM_WIDTH)), row],
                 [_rows(tm, SSM_WIDTH), _rows(tm, 2 * SSM_WIDTH), _rows(tm, SSM_WIDTH), _rows(tm, SSM_WIDTH),
                  _rows(tm, N_STATE), _rows(tm, N_STATE), row, _whole((1, 2 * SSM_WIDTH)), row],
                 [_sds((L, SSM_WIDTH), _BF16), _sds((L, 2 * SSM_WIDTH), _BF16), _sds((L, SSM_WIDTH), _BF16),
                  _sds((L, SSM_WIDTH), _F32), _sds((L, N_STATE), _F32), _sds((L, N_STATE), _F32),
                  _sds((1, SSM_WIDTH), _F32), _sds((1, 2 * SSM_WIDTH), _F32), _sds((1, SSM_WIDTH), _F32)],
                 "ssm_out_bwd")(dn, y, z, u, ct_re, ct_im, d_row, w_glu, g_ssm)


def _ssm_du(lam_re, lam_im, bt_re, bt_im, dud):
    L = dud.shape[0]
    tm = _tile(L)

    def body(lr_ref, li_ref, br_ref, bi_ref, dud_ref, du_ref):
        for k in range(SUPER):
            du_ref[:, _sb_width(k)] = (_dot(lr_ref[:, _sb_state(k)], br_ref[:, _sb_state(k)], _NT)
                                       + _dot(li_ref[:, _sb_state(k)], bi_ref[:, _sb_state(k)], _NT)
                                       + dud_ref[:, _sb_width(k)])

    return _call(body, (L // tm,),
                 [_rows(tm, N_STATE), _rows(tm, N_STATE), _whole((SB_WIDTH, N_STATE)),
                  _whole((SB_WIDTH, N_STATE)), _rows(tm, SSM_WIDTH)],
                 _rows(tm, SSM_WIDTH), _sds((L, SSM_WIDTH), _F32), "ssm_du")(lam_re, lam_im, bt_re, bt_im, dud)


def _ssm_weight_grads(dy, x_re, x_im, lam_re, lam_im, u):
    L = u.shape[0]

    def body(dy_ref, xr_ref, xi_ref, lr_ref, li_ref, u_ref, dcr_ref, dci_ref, dbr_ref, dbi_ref):
        dyb = dy_ref[...]
        ub = u_ref[...].astype(_BF16)
        dcr_ref[...] = _dot(dyb, xr_ref[...], _TN)
        dci_ref[...] = _dot(dyb, xi_ref[...], _TN)
        dbr_ref[...] = _dot(ub, lr_ref[...], _TN)
        dbi_ref[...] = _dot(ub, li_ref[...], _TN)

    width = pl.BlockSpec((L, SB_WIDTH), lambda k: (0, k))
    state = pl.BlockSpec((L, SB_STATE), lambda k: (0, k))
    out = pl.BlockSpec((SB_WIDTH, SB_STATE), lambda k: (0, k))
    return _call(body, (SUPER,), [width, state, state, state, state, width], [out] * 4,
                 [_sds((SB_WIDTH, N_STATE), _F32)] * 4,
                 "ssm_weight_grads")(dy, x_re, x_im, lam_re, lam_im, u)


_SSM_PACK = {"ssm_b_re": (0, SSM_WIDTH, 0, SSM_STATE), "ssm_c_re": (0, SSM_WIDTH, 64, SSM_STATE),
             "ssm_b_im": (512, SSM_WIDTH, 0, SSM_STATE), "ssm_c_im": (512, SSM_WIDTH, 64, SSM_STATE),
             "ssm_lambda_re": (1024, SSM_GROUPS, 0, SSM_STATE), "ssm_lambda_im": (1024, SSM_GROUPS, 64, SSM_STATE),
             "ssm_d": (1056, SSM_GROUPS, 0, SSM_GROUP), "ssm_log_dt": (1088, 1, 0, SSM_GROUPS)}
_PACK_TILE = 16
_SSM_PACK_ROWS = 1088 + _PACK_TILE


def _ssm_param_bwd(da_re, da_im, dbt_re, dbt_im, dct_re, dct_im, lam_re, lam_im, log_dt, b_re, b_im, g_d):
    def body(dar, dai, dbr, dbi, dcr, dci, lr_ref, li_ref, ld_ref, bre_ref, bim_ref, gd_ref, pack_ref):
        lane_in = _iota((SSM_STATE, _LANES), 0)
        lane_out = _iota((SSM_STATE, _LANES), 1)
        low = (lane_out == lane_in).astype(_F32)
        high = (lane_out == lane_in + SSM_STATE).astype(_F32)

        def side_by_side(a, b):
            return _dot_exact(a, low, _NN) + _dot_exact(b, high, _NN)

        tail = _SSM_PACK["ssm_d"][0]
        pack_ref[tail:, :] = jnp.zeros((_SSM_PACK_ROWS - tail, _LANES), _BF16)
        pack_ref[tail:tail + SSM_GROUPS, 0:SSM_GROUP] = gd_ref[...].astype(_BF16)
        own_c = (_iota((SB_WIDTH, SB_STATE), 0) >> 4) == (_iota((SB_WIDTH, SB_STATE), 1) >> 6)

        def unfold(ref):
            blocks = []
            for k in range(SUPER):
                t = jnp.where(own_c, ref[:, _sb_state(k)], 0.0)
                t = sum(t[:, 128 * i:128 * (i + 1)] for i in range(SB_STATE // 128))
                blocks.append((t + pltpu.roll(t, SSM_STATE, 1))[:, :SSM_STATE])
            return jnp.concatenate(blocks, axis=0)

        dbb_re, dbb_im = unfold(dbr), unfold(dbi)
        b_re, b_im = bre_ref[...], bim_ref[...]
        dt_col = _dt_column(ld_ref[...])
        (_, _, fr, fi), vjp = jax.vjp(_s5_discretize, lr_ref[...], li_ref[...], dt_col)
        spread = _rows_of_group()
        fr_t = _dot_exact(spread, fr, _NN)
        fi_t = _dot_exact(spread, fi, _NN)
        pack_ref[0:SSM_WIDTH, :] = side_by_side(fr_t * dbb_re + fi_t * dbb_im, unfold(dcr)).astype(_BF16)
        pack_ref[SSM_WIDTH:2 * SSM_WIDTH, :] = side_by_side(fr_t * dbb_im - fi_t * dbb_re, -unfold(dci)).astype(_BF16)
        d_fr = _dot_exact(spread, dbb_re * b_re + dbb_im * b_im, _TN)
        d_fi = _dot_exact(spread, dbb_im * b_re - dbb_re * b_im, _TN)
        e64, own = _group_masks()

        def from_row(ref):
            return _dot_exact(jnp.where(own, ref[...], 0.0), e64, _NT)

        d_lr, d_li, d_dt = vjp((from_row(dar), from_row(dai), d_fr, d_fi))
        lam_rows = _SSM_PACK["ssm_lambda_re"][0]
        pack_ref[lam_rows:lam_rows + SSM_GROUPS, :] = side_by_side(d_lr, d_li).astype(_BF16)
        eye = (_iota((SSM_GROUPS, SSM_GROUPS), 0) == _iota((SSM_GROUPS, SSM_GROUPS), 1)).astype(_F32)
        dt_row = _SSM_PACK["ssm_log_dt"][0]
        pack_ref[dt_row:dt_row + _PACK_TILE, 0:SSM_GROUPS] = _dot_exact(
            jnp.broadcast_to(d_dt, (SSM_GROUPS, 128)), eye, _TN)[0:_PACK_TILE].astype(_BF16)

    ins = [da_re, da_im, dbt_re, dbt_im, dct_re, dct_im, lam_re, lam_im, log_dt, b_re, b_im, g_d]
    out = (_SSM_PACK_ROWS, _LANES)
    return _call(body, (1,), [_whole(a.shape) for a in ins], _whole(out), _sds(out, _BF16), "ssm_param_bwd")(*ins)


def _head_spread(j):
    r = _iota((KV_WIDTH, 256), 0)
    c = _iota((KV_WIDTH, 256), 1)
    return (r == HEAD_DIM * j + (c & (HEAD_DIM - 1))).astype(_BF16)


STACK = Q_PER_KV * BLOCK


def _stack_heads(t):
    lane_head = _iota((1, 256), 1) >> 6
    return jnp.concatenate([jnp.where(lane_head == g, t, jnp.zeros_like(t)) for g in range(Q_PER_KV)], axis=0)


def _unstack_heads(t):
    lane_head = _iota((1, 256), 1) >> 6
    return sum(jnp.where(lane_head == g, t[BLOCK * g:BLOCK * (g + 1)], 0.0) for g in range(Q_PER_KV))


def _stacked_sinks(sink_ref, j):
    block = _iota((STACK, 1), 0) >> 7
    col = jnp.full((STACK, 1), sink_ref[Q_PER_KV * j], _F32)
    for g in range(1, Q_PER_KV):
        col = jnp.where(block == g, sink_ref[Q_PER_KV * j + g], col)
    return col


def _fold_heads(t, j):
    t = t[:, :KV_WIDTH] + t[:, KV_WIDTH:]
    t = t + pltpu.roll(t, HEAD_DIM, 1)
    return jnp.where((_iota((1, KV_WIDTH), 1) >> 6) == j, t, 0.0)


def _attn_scores(q_stacked, kt, blk, sink):
    s = _dot(q_stacked, kt, _NT) * (HEAD_DIM ** -0.5)
    qi = _iota((STACK, 2 * BLOCK), 0) & (BLOCK - 1)
    kj = _iota((STACK, 2 * BLOCK), 1)
    rel = qi + BLOCK - kj
    valid = (rel >= 0) & (rel < BLOCK) & (blk * BLOCK - BLOCK + kj >= 0)
    s = jnp.where(valid, s, MASK_VALUE)
    m = jnp.maximum(jnp.max(s, axis=-1, keepdims=True), sink)
    p = jnp.exp(s - m)
    e_sink = jnp.exp(sink - m)
    den = jnp.sum(p, axis=-1, keepdims=True) + e_sink
    return p / den, e_sink / den


def _attn_specs():
    prev = lambda i: (jnp.maximum(i - 1, 0), 0)
    cur = lambda i: (i, 0)
    kv = [pl.BlockSpec((BLOCK, KV_WIDTH), prev), pl.BlockSpec((BLOCK, KV_WIDTH), cur)]
    return [pl.BlockSpec((BLOCK, ATTN_WIDTH), cur)] + kv + kv


def _attn_fwd(q, k, v, sinks, g_attn):
    L = q.shape[0]

    def body(q_ref, kp_ref, kc_ref, vp_ref, vc_ref, sink_ref, g_ref, o_ref, n_ref):
        blk = pl.program_id(0)
        kwin = jnp.concatenate([kp_ref[...], kc_ref[...]], axis=0)
        vwin = jnp.concatenate([vp_ref[...], vc_ref[...]], axis=0)
        halves = []
        for j in range(N_KV_HEADS):
            spread = _head_spread(j)
            kt = _dot(kwin, spread, _NN).astype(_BF16)
            vt = _dot(vwin, spread, _NN).astype(_BF16)
            qs = _stack_heads(q_ref[:, 256 * j:256 * (j + 1)])
            p, _ = _attn_scores(qs, kt, blk, _stacked_sinks(sink_ref, j))
            halves.append(_unstack_heads(_dot(p, vt, _NN)))
        o = jnp.concatenate(halves, axis=1)
        o_ref[...] = o
        n, _ = _rms_fwd(o, g_ref[...])
        n_ref[...] = n.astype(_BF16)

    cur = lambda i: (i, 0)
    return _call(body, (L // BLOCK,),
                 _attn_specs() + [pl.BlockSpec(memory_space=pltpu.SMEM), _whole((1, ATTN_WIDTH))],
                 [pl.BlockSpec((BLOCK, ATTN_WIDTH), cur)] * 2,
                 [_sds((L, ATTN_WIDTH), _F32), _sds((L, ATTN_WIDTH), _BF16)],
                 "attn_fwd")(q, k, k, v, v, sinks, g_attn)


def _attn_bwd(q, k, v, o, dn, sinks, g_attn):
    L = q.shape[0]

    def body(q_ref, kp_ref, kc_ref, vp_ref, vc_ref, o_ref, dn_ref, sink_ref, g_ref,
             dq_ref, dk_ref, dv_ref, dsink_ref, dg_ref):
        blk = pl.program_id(0)
        first = blk == 0

        @pl.when(first)
        def _():
            dk_ref[...] = jnp.zeros_like(dk_ref)
            dv_ref[...] = jnp.zeros_like(dv_ref)
            dsink_ref[...] = jnp.zeros_like(dsink_ref)

        o = o_ref[...]
        g = g_ref[...]
        _, r = _rms_fwd(o, g)
        do, dg = _rms_bwd(dn_ref[...], o, g, r)
        _accumulate(dg_ref, dg, first)
        kwin = jnp.concatenate([kp_ref[...], kc_ref[...]], axis=0)
        vwin = jnp.concatenate([vp_ref[...], vc_ref[...]], axis=0)
        lane = _iota((1, 128), 1)
        dsink = jnp.zeros((1, 128), _F32)
        dkwin = jnp.zeros((2 * BLOCK, KV_WIDTH), _F32)
        dvwin = jnp.zeros((2 * BLOCK, KV_WIDTH), _F32)
        dq_halves = []
        for j in range(N_KV_HEADS):
            spread = _head_spread(j)
            kt = _dot(kwin, spread, _NN).astype(_BF16)
            vt = _dot(vwin, spread, _NN).astype(_BF16)
            qs = _stack_heads(q_ref[:, 256 * j:256 * (j + 1)])
            dos = _stack_heads(do[:, 256 * j:256 * (j + 1)]).astype(_BF16)
            p, p_sink = _attn_scores(qs, kt, blk, _stacked_sinks(sink_ref, j))
            dp = _dot(dos, vt, _NT)
            delta = jnp.sum(p * dp, axis=-1, keepdims=True)
            ds = (p * (dp - delta) * (HEAD_DIM ** -0.5)).astype(_BF16)
            sink_term = p_sink * delta
            for g in range(Q_PER_KV):
                head_sum = jnp.sum(sink_term[BLOCK * g:BLOCK * (g + 1)], axis=0, keepdims=True)
                dsink = dsink - jnp.where(lane == Q_PER_KV * j + g, head_sum, 0.0)
            dvwin = dvwin + _fold_heads(_dot(p, dos, _TN), j)
            dkwin = dkwin + _fold_heads(_dot(ds, qs, _TN), j)
            dq_halves.append(_unstack_heads(_dot(ds, kt, _NN)))
        dq_ref[...] = jnp.concatenate(dq_halves, axis=1)
        dsink_ref[...] += dsink
        prev = pl.ds(pl.multiple_of(jnp.maximum(blk - 1, 0) * BLOCK, BLOCK), BLOCK)
        cur = pl.ds(pl.multiple_of(blk * BLOCK, BLOCK), BLOCK)
        dk_ref[prev, :] += dkwin[:BLOCK]
        dk_ref[cur, :] += dkwin[BLOCK:]
        dv_ref[prev, :] += dvwin[:BLOCK]
        dv_ref[cur, :] += dvwin[BLOCK:]

    cur = lambda i: (i, 0)
    blk_q = pl.BlockSpec((BLOCK, ATTN_WIDTH), cur)
    return _call(body, (L // BLOCK,),
                 _attn_specs() + [blk_q, blk_q, pl.BlockSpec(memory_space=pltpu.SMEM), _whole((1, ATTN_WIDTH))],
                 [blk_q, _whole((L, KV_WIDTH)), _whole((L, KV_WIDTH)), _whole((1, 128)), _whole((1, ATTN_WIDTH))],
                 [_sds((L, ATTN_WIDTH), _F32), _sds((L, KV_WIDTH), _F32), _sds((L, KV_WIDTH), _F32),
                  _sds((1, 128), _F32), _sds((1, ATTN_WIDTH), _F32)],
                 "attn_bwd")(q, k, k, v, v, o, dn, sinks, g_attn)


def _out_proj(n_ssm, n_attn, x, w_out, g_post_mix, g_pre_ffn):
    L = x.shape[0]
    tm = _chunk_tile(L)

    def body(ns_ref, na_ref, x_ref, w_ref, g1_ref, g2_ref, merged_ref, mo_ref, h1_ref, hn2_ref):
        merged = jnp.concatenate([ns_ref[...], na_ref[...]], axis=1)
        merged_ref[...] = merged
        mo = _dot(merged, w_ref[...], _NN)
        mo_ref[...] = mo
        n, _ = _rms_fwd(mo, g1_ref[...])
        h1 = x_ref[...] + n
        h1_ref[...] = h1
        hn2, _ = _rms_fwd(h1, g2_ref[...])
        hn2_ref[...] = hn2.astype(_BF16)

    row = _whole((1, D_MODEL))
    return _call(body, (L // tm,),
                 [_chunk_block(L, SSM_WIDTH), _rows(tm, ATTN_WIDTH), _rows(tm, D_MODEL), _whole((D_MODEL, D_MODEL)),
                  row, row],
                 [_rows(tm, D_MODEL)] * 4,
                 [_sds((L, D_MODEL), _BF16), _sds((L, D_MODEL), _F32), _sds((L, D_MODEL), _F32), _sds((L, D_MODEL), _BF16)],
                 "out_proj")(n_ssm, n_attn, x, w_out, g_post_mix, g_pre_ffn)


def _ffn(hn2, h1, target, w_gate_up, w_down, g_pre_ffn, g_post_ffn):
    L = h1.shape[0]
    tm = _tile(L)
    half = D_FF // 2

    def body(hn2_ref, h1_ref, tgt_ref, wgu_hbm, wd_hbm, g2_ref, g3_ref,
             act_ref, dgu_ref, dff_ref, dh1_ref, loss_ref, dg3_ref, dg2_ref,
             wgu, wd, gu, sem):
        first = pl.program_id(0) == 0

        @pl.when(first)
        def _():
            c1 = pltpu.make_async_copy(wgu_hbm, wgu, sem.at[0])
            c2 = pltpu.make_async_copy(wd_hbm, wd, sem.at[1])
            c1.start()
            c2.start()
            c1.wait()
            c2.wait()

        hn2 = hn2_ref[...]
        ff = jnp.zeros((tm, D_MODEL), _F32)
        for c in range(2):
            gate = _dot(hn2, wgu[half * c:half * (c + 1), :], _NT)
            up = _dot(hn2, wgu[D_FF + half * c:D_FF + half * (c + 1), :], _NT)
            gu[:, half * c:half * (c + 1)] = gate
            gu[:, D_FF + half * c:D_FF + half * (c + 1)] = up
            act = gate * jax.nn.sigmoid(gate) * up
            act_ref[half * c:half * (c + 1), :] = act.T.astype(_BF16)
            ff = ff + _dot(act, wd[half * c:half * (c + 1), :], _NN)
        g3 = g3_ref[...]
        n, r = _rms_fwd(ff, g3)
        h1 = h1_ref[...]
        err = h1 + n - tgt_ref[...]
        loss = 0.5 * jnp.sum(jnp.mean(err * err, axis=-1, keepdims=True), axis=0, keepdims=True)
        _accumulate(loss_ref, jnp.broadcast_to(loss, (1, 128)), first)
        dh2 = err * (1.0 / D_MODEL)
        dff, dg3 = _rms_bwd(dh2, ff, g3, r)
        _accumulate(dg3_ref, dg3, first)
        dffb = dff.astype(_BF16)
        dff_ref[...] = dffb
        dhn2 = jnp.zeros((tm, D_MODEL), _F32)
        for c in range(2):
            dact = _dot(dffb, wd[half * c:half * (c + 1), :], _NT)
            gate = gu[:, half * c:half * (c + 1)]
            up = gu[:, D_FF + half * c:D_FF + half * (c + 1)]
            sig = jax.nn.sigmoid(gate)
            silu = gate * sig
            dgate = dact * up * (sig + silu * (1.0 - sig))
            dup = dact * silu
            dgu_ref[half * c:half * (c + 1), :] = dgate.T.astype(_BF16)
            dgu_ref[D_FF + half * c:D_FF + half * (c + 1), :] = dup.T.astype(_BF16)
            dhn2 = dhn2 + _dot(dgate, wgu[half * c:half * (c + 1), :], _NN)
            dhn2 = dhn2 + _dot(dup, wgu[D_FF + half * c:D_FF + half * (c + 1), :], _NN)
        g2 = g2_ref[...]
        _, r2 = _rms_fwd(h1, g2)
        dh1, dg2 = _rms_bwd(dhn2, h1, g2, r2)
        _accumulate(dg2_ref, dg2, first)
        dh1_ref[...] = dh2 + dh1

    row = _whole((1, D_MODEL))
    anyspace = pl.BlockSpec(memory_space=pl.ANY)
    return _call(body, (L // tm,),
                 [_rows(tm, D_MODEL), _rows(tm, D_MODEL), _rows(tm, D_MODEL), anyspace, anyspace, row, row],
                 [pl.BlockSpec((D_FF, tm), lambda i: (0, i)), pl.BlockSpec((2 * D_FF, tm), lambda i: (0, i)),
                  _rows(tm, D_MODEL), _rows(tm, D_MODEL), _whole((1, 128)), row, row],
                 [_sds((D_FF, L), _BF16), _sds((2 * D_FF, L), _BF16), _sds((L, D_MODEL), _BF16),
                  _sds((L, D_MODEL), _F32), _sds((1, 128), _F32), _sds((1, D_MODEL), _F32), _sds((1, D_MODEL), _F32)],
                 "ffn",
                 scratch=[pltpu.VMEM((2 * D_FF, D_MODEL), _BF16), pltpu.VMEM((D_FF, D_MODEL), _BF16),
                          pltpu.VMEM((tm, 2 * D_FF), _F32), pltpu.SemaphoreType.DMA((2,))],
                 )(hn2, h1, target, w_gate_up, w_down, g_pre_ffn, g_post_ffn)


def _out_proj_bwd(dh1, mo, w_out, g_post_mix, tokens=()):
    L = dh1.shape[0]
    tm = _chunk_tile(L)

    def body(dh1_ref, mo_ref, w_ref, g_ref, dmo_ref, dns_ref, dna_ref, dg_ref):
        first = pl.program_id(0) == 0
        mo = mo_ref[...]
        g = g_ref[...]
        _, r = _rms_fwd(mo, g)
        dmo, dg = _rms_bwd(dh1_ref[...], mo, g, r)
        _accumulate(dg_ref, dg, first)
        dmob = dmo.astype(_BF16)
        dmo_ref[...] = dmob
        dmerged = _dot(dmob, w_ref[...], _NT)
        dns_ref[...] = dmerged[:, :SSM_WIDTH]
        dna_ref[...] = dmerged[:, SSM_WIDTH:]

    row = _whole((1, D_MODEL))
    return _call(body, (L // tm,),
                 [_rows(tm, D_MODEL), _rows(tm, D_MODEL), _whole((D_MODEL, D_MODEL)), row],
                 [_rows(tm, D_MODEL), _chunk_block(L, SSM_WIDTH), _rows(tm, ATTN_WIDTH), row],
                 [_sds((L, D_MODEL), _BF16), _sds(_chunk_shape(L, SSM_WIDTH), _F32), _sds((L, ATTN_WIDTH), _F32),
                  _sds((1, D_MODEL), _F32)],
                 "out_proj_bwd", tokens=tokens)(dh1, mo, w_out, g_post_mix)


def _in_proj_bwd(du, dq, dk, dv, cos_t, sin_t, x, dh1, g_pre_mix, w_in, tokens=()):
    L = x.shape[0]
    tm = _chunk_tile(L)

    def body(du_ref, dq_ref, dk_ref, dv_ref, cos_ref, sin_ref, x_ref, dh1_ref, g_ref, w_ref,
             dproj_ref, dx_ref, dg_ref):
        first = pl.program_id(0) == 0
        cos_v, sin_v = cos_ref[...], sin_ref[...]
        dproj = jnp.concatenate([du_ref[...], _rope_transpose(dq_ref[...], cos_v, sin_v),
                                 _rope_transpose(dk_ref[...], cos_v, sin_v), dv_ref[...]], axis=1).astype(_BF16)
        dproj_ref[...] = dproj
        dhn = _dot(dproj, w_ref[...], _NN)
        x = x_ref[...]
        g = g_ref[...]
        _, r = _rms_fwd(x, g)
        dx, dg = _rms_bwd(dhn, x, g, r)
        _accumulate(dg_ref, dg, first)
        dx_ref[...] = dh1_ref[...] + dx

    row = _whole((1, D_MODEL))
    return _call(body, (L // tm,),
                 [_chunk_block(L, SSM_WIDTH), _rows(tm, ATTN_WIDTH), _rows(tm, KV_WIDTH), _rows(tm, KV_WIDTH),
                  _rows(tm, KV_WIDTH), _rows(tm, KV_WIDTH), _rows(tm, D_MODEL), _rows(tm, D_MODEL), row,
                  _whole((IN_WIDTH, D_MODEL))],
                 [_rows(tm, IN_WIDTH), _rows(tm, D_MODEL), row],
                 [_sds((L, IN_WIDTH), _BF16), _sds((L, D_MODEL), _F32), _sds((1, D_MODEL), _F32)],
                 "in_proj_bwd", tokens=tokens)(du, dq, dk, dv, cos_t, sin_t, x, dh1, g_pre_mix, w_in)


def _matmul_nn(a, b, out_dtype, name):
    M, K = a.shape
    N = b.shape[1]
    tm = next(t for t in (512, 256, 128) if M % t == 0)
    tn = N if N <= D_MODEL else next(t for t in (512, 256, 128) if N % t == 0)

    def body(a_ref, b_ref, o_ref):
        o_ref[...] = _dot(a_ref[...], b_ref[...], _NN).astype(out_dtype)

    params = pltpu.CompilerParams(dimension_semantics=("arbitrary", "arbitrary"), vmem_limit_bytes=VMEM_LIMIT)
    return pl.pallas_call(body, grid=(M // tm, N // tn),
                          in_specs=[pl.BlockSpec((tm, K), lambda i, j: (i, 0)),
                                    pl.BlockSpec((K, tn), lambda i, j: (0, j))],
                          out_specs=pl.BlockSpec((tm, tn), lambda i, j: (i, j)),
                          out_shape=_sds((M, N), out_dtype), compiler_params=params, name=name)(a, b)


def _matmul_tn(a, b, out_dtype, name, scale=1.0):
    K, M = a.shape
    N = b.shape[1]
    tm = next(t for t in (512, 256, 128) if M % t == 0)
    tn = N if N <= D_MODEL else next(t for t in (512, 256, 128) if N % t == 0)

    def body(a_ref, b_ref, o_ref):
        acc = _dot(a_ref[...], b_ref[...], _TN)
        o_ref[...] = (acc if scale == 1.0 else acc * scale).astype(out_dtype)

    params = pltpu.CompilerParams(dimension_semantics=("arbitrary", "arbitrary"), vmem_limit_bytes=VMEM_LIMIT)
    return pl.pallas_call(body, grid=(M // tm, N // tn),
                          in_specs=[pl.BlockSpec((K, tm), lambda i, j: (0, i)),
                                    pl.BlockSpec((K, tn), lambda i, j: (0, j))],
                          out_specs=pl.BlockSpec((tm, tn), lambda i, j: (i, j)),
                          out_shape=_sds((M, N), out_dtype), compiler_params=params, name=name)(a, b)


def _local_step(x, pos, target, p, fetch, publish, progress):
    L = x.shape[0]
    T = L // SCAN_CHUNKS
    cos_t, sin_t = _rope_tables(pos.reshape(L, 1))
    w_in, = fetch(("w_in",), None)
    hn, u, q, k, v = _in_proj(x, p["g_pre_mix"], w_in, cos_t, sin_t)

    ssm = {n: _to_2d(n, p[n]) for n in ("ssm_lambda_re", "ssm_lambda_im", "ssm_log_dt", "ssm_b_re", "ssm_b_im",
                                        "ssm_c_re", "ssm_c_im")}
    d_row = p["ssm_d"].reshape(1, SSM_WIDTH)
    a_re, a_im, bt_re, bt_im, ct_re, ct_im = _ssm_prep(
        ssm["ssm_lambda_re"], ssm["ssm_lambda_im"], ssm["ssm_log_dt"], ssm["ssm_b_re"], ssm["ssm_b_im"],
        ssm["ssm_c_re"], ssm["ssm_c_im"])

    u_c = u.reshape(L, SSM_WIDTH)
    bu_re, bu_im = _ssm_bu(u_c, bt_re, bt_im)
    x_re, x_im = _scan_fwd(bu_re.reshape(T, SCAN_CHUNKS, N_STATE), bu_im.reshape(T, SCAN_CHUNKS, N_STATE), a_re, a_im)
    w_glu, = fetch(("w_glu",), x_re)
    y, z, n_ssm_c = _ssm_out(x_re.reshape(L, N_STATE), x_im.reshape(L, N_STATE), u_c, ct_re, ct_im, d_row,
                             w_glu, p["b_glu"], p["g_ssm_out"])
    n_ssm = n_ssm_c.reshape(_chunk_shape(L, SSM_WIDTH))

    sinks = p["attn_sinks"].reshape(N_Q_HEADS)
    o, n_attn = _attn_fwd(q, k, v, sinks, p["g_attn_out"])
    w_out, = fetch(("w_out",), n_attn)
    merged, mo, h1, hn2 = _out_proj(n_ssm, n_attn, x, w_out, p["g_post_mix"], p["g_pre_ffn"])
    w_gate_up, w_down = fetch(("w_gate_up", "w_down"), hn2)
    act_t, dgu_t, dff, dh1, loss, dg_post_ffn, dg_pre_ffn = _ffn(
        hn2, h1, target, w_gate_up, w_down, p["g_pre_ffn"], p["g_post_ffn"])
    grads = {"g_post_ffn": dg_post_ffn, "g_pre_ffn": dg_pre_ffn}
    tokens = publish({"w_down": _matmul_nn(act_t, dff, _BF16, "grad_w_down"),
                      "w_gate_up": _matmul_nn(dgu_t, hn2, _BF16, "grad_w_gate_up")})

    dmo, dn_ssm, dn_attn, grads["g_post_mix"] = _out_proj_bwd(dh1, mo, w_out, p["g_post_mix"], tokens)
    grad_w_out = _matmul_tn(merged, dmo, _BF16, "grad_w_out")

    dq, dk, dv, dsink, grads["g_attn_out"] = _attn_bwd(q, k, v, o, dn_attn, sinks, p["g_attn_out"])
    grads["attn_sinks"] = dsink

    gy, dz, dy, dud, dx_re, dx_im, grads["g_ssm_out"], grads["b_glu"], dd = _ssm_out_bwd(
        dn_ssm.reshape(L, SSM_WIDTH), y, z, u_c, ct_re, ct_im, d_row, w_glu, p["g_ssm_out"])
    grads.update(w_out=grad_w_out, w_glu=_matmul_tn(dz, gy, _BF16, "grad_w_glu"))
    tokens = [grads["w_out"], grads["w_glu"]] + progress(dy)
    lam_re, lam_im, da_re, da_im = _scan_bwd(dx_re.reshape(T, SCAN_CHUNKS, N_STATE), dx_im.reshape(T, SCAN_CHUNKS, N_STATE),
                                             x_re, x_im, a_re, a_im, tokens)
    lam_re = lam_re.reshape(L, N_STATE)
    lam_im = lam_im.reshape(L, N_STATE)
    dct_re, dct_im, dbt_re, dbt_im = _ssm_weight_grads(
        dy, x_re.reshape(L, N_STATE), x_im.reshape(L, N_STATE), lam_re, lam_im, u_c)
    ssm_pack = _ssm_param_bwd(
        da_re, da_im, dbt_re, dbt_im, dct_re, dct_im,
        ssm["ssm_lambda_re"], ssm["ssm_lambda_im"], ssm["ssm_log_dt"], ssm["ssm_b_re"], ssm["ssm_b_im"],
        dd.reshape(SSM_GROUPS, SSM_GROUP))
    grads.update(ssm_pack=ssm_pack, loss=loss)
    publish(grads)

    du = _ssm_du(lam_re, lam_im, bt_re, bt_im, dud).reshape(_chunk_shape(L, SSM_WIDTH))
    dproj, grad_x, g_pre_mix = _in_proj_bwd(du, dq, dk, dv, cos_t, sin_t, x, dh1, p["g_pre_mix"], w_in, [ssm_pack])
    publish({"g_pre_mix": g_pre_mix, "w_in": _matmul_tn(dproj, hn, _BF16, "grad_w_in")})
    return grad_x


_MESH = pl.DeviceIdType.MESH
_PEERS = N_DEV - 1


def _mesh_pos():
    return lax.axis_index("x"), lax.axis_index("y"), lax.axis_index("c")


def _dev_index(px, py, pc):
    return 4 * px + 2 * py + pc


def _all_gather(shards, out_dtype, name):
    n = len(shards)

    def body(*refs):
        ins, outs, stages = refs[:n], refs[n:2 * n], refs[2 * n:3 * n]
        send_sems, recv_sems, local_sems = refs[3 * n:]
        x, y, c = _mesh_pos()
        me, sibling = (x, y, c), (x, y, 1 - c)
        chips = [(1 - x, y), (x, 1 - y), (1 - x, 1 - y)]

        def copy(w, k, block, to, src=None):
            slot = outs[w].at[_dev_index(*block)]
            return pltpu.make_async_remote_copy(
                src_ref=slot if src is None else src, dst_ref=slot,
                send_sem=send_sems.at[_PEERS * w + k], recv_sem=recv_sems.at[_PEERS * w + k],
                device_id=to, device_id_type=_MESH)

        for w in range(n):
            stages[w][...] = ins[w][...].astype(out_dtype)
        mine, first, passed = [], [], []
        for w in range(n):
            cp = pltpu.make_async_copy(stages[w], outs[w].at[_dev_index(*me)], local_sems.at[w])
            cp.start()
            mine.append(cp)
            sends = [copy(w, 0, me, sibling, src=stages[w])]
            sends += [copy(w, 1 + j, me, (*chip, c), src=stages[w]) for j, chip in enumerate(chips)]
            for cp in sends:
                cp.start()
            first += sends
        for w in range(n):
            for j, chip in enumerate(chips):
                copy(w, 1 + j, (*chip, c), me).wait_recv()
                cp = copy(w, 4 + j, (*chip, c), sibling)
                cp.start()
                passed.append(cp)
        for w in range(n):
            copy(w, 0, sibling, me).wait_recv()
            for j, chip in enumerate(chips):
                copy(w, 4 + j, (*chip, 1 - c), me).wait_recv()
        for cp in first + passed:
            cp.wait_send()
        for cp in mine:
            cp.wait()

    return pl.pallas_call(
        body, name=name,
        out_shape=[_sds((N_DEV,) + s.shape, out_dtype) for s in shards],
        in_specs=[pl.BlockSpec(memory_space=pltpu.VMEM)] * n,
        out_specs=[pl.BlockSpec(memory_space=pl.ANY)] * n,
        scratch_shapes=[pltpu.VMEM(s.shape, out_dtype) for s in shards]
        + [pltpu.SemaphoreType.DMA((_PEERS * n,)), pltpu.SemaphoreType.DMA((_PEERS * n,)),
           pltpu.SemaphoreType.DMA((n,))],
        compiler_params=pltpu.CompilerParams(vmem_limit_bytes=VMEM_LIMIT),
    )(*shards)


_HBM_SPEC = pl.BlockSpec(memory_space=pltpu.HBM)
_SEM_SPEC = pl.BlockSpec(memory_space=pltpu.SEMAPHORE)
_DATAFLOW = pltpu.SideEffectType.DATAFLOW_SIDE_EFFECTING


def _peer(x, y, c, r):
    return (x ^ ((r >> 2) & 1), y ^ ((r >> 1) & 1), c ^ (r & 1))


def _hbm(a):
    return pltpu.with_memory_space_constraint(a, pltpu.HBM)


def _send_start(sources, blocked, name):
    n = len(sources)
    lands = [lax.empty((N_DEV,) + (s.shape[1:] if blocked else s.shape), s.dtype) for s in sources]

    def body(*refs):
        srcs, zones = refs[:n], refs[n:2 * n]
        send_sems, recv_sems = refs[2 * n:3 * n], refs[3 * n:4 * n]
        token, local_sems = refs[6 * n], refs[6 * n + 1]
        x, y, c = _mesh_pos()
        me = _dev_index(x, y, c)
        local = []
        for w in range(n):
            cp = pltpu.make_async_copy(srcs[w].at[me] if blocked else srcs[w], zones[w].at[me], local_sems.at[w])
            cp.start()
            local.append(cp)
            for r in range(1, N_DEV):
                peer = _peer(x, y, c, r)
                pltpu.make_async_remote_copy(
                    src_ref=srcs[w].at[_dev_index(*peer)] if blocked else srcs[w], dst_ref=zones[w].at[me],
                    send_sem=send_sems[w].at[r - 1], recv_sem=recv_sems[w].at[r - 1],
                    device_id=peer, device_id_type=_MESH).start()
        for cp in local:
            cp.wait()
        token[...] = jnp.zeros_like(token)

    sems = [pltpu.SemaphoreType.DMA((_PEERS,))] * (2 * n)
    out = pl.pallas_call(
        body, name=name,
        out_shape=sems + [pltpu.HBM(a.shape, a.dtype) for a in list(sources) + lands] + [_sds((8, 128), _F32)],
        in_specs=[_HBM_SPEC] * (2 * n),
        out_specs=[_SEM_SPEC] * (2 * n) + [_HBM_SPEC] * (2 * n) + [pl.BlockSpec(memory_space=pltpu.VMEM)],
        input_output_aliases={i: 2 * n + i for i in range(2 * n)},
        scratch_shapes=[pltpu.SemaphoreType.DMA((n,))],
        compiler_params=pltpu.CompilerParams(has_side_effects=_DATAFLOW),
    )(*[_hbm(a) for a in sources], *[_hbm(a) for a in lands])
    return out[:n], out[n:2 * n], out[2 * n:3 * n], out[3 * n:4 * n], out[4 * n]


def _send_wait(send_sems, recv_sems, sources, lands, after, blocked, name):
    n = len(sources)

    def body(*refs):
        srcs, zones = refs[:n], refs[n:2 * n]
        sends, recvs = refs[2 * n:3 * n], refs[3 * n:4 * n]
        x, y, c = _mesh_pos()
        for w in range(n):
            for r in range(1, N_DEV):
                peer = _peer(x, y, c, r)
                idx = _dev_index(*peer)
                cp = pltpu.make_async_remote_copy(
                    src_ref=srcs[w].at[idx] if blocked else srcs[w], dst_ref=zones[w].at[idx],
                    send_sem=sends[w].at[r - 1], recv_sem=recvs[w].at[r - 1],
                    device_id=peer, device_id_type=_MESH)
                cp.wait_send()
                cp.wait_recv()

    out = pl.pallas_call(
        body, name=name,
        out_shape=[pltpu.HBM(a.shape, a.dtype) for a in list(sources) + list(lands)],
        in_specs=[_HBM_SPEC] * (2 * n) + [_SEM_SPEC] * (2 * n) + [pl.BlockSpec(memory_space=pl.ANY)],
        out_specs=[_HBM_SPEC] * (2 * n),
        input_output_aliases={i: i for i in range(2 * n)},
        compiler_params=pltpu.CompilerParams(has_side_effects=_DATAFLOW),
    )(*sources, *lands, *send_sems, *recv_sems, after)
    return out[n:]


def _sequencer_exchange(sources, blocked, name, collective_id):
    n = len(sources)
    flags = blocked

    def body(*refs):
        srcs, zones = refs[:n], refs[n:2 * n]
        send_sems, recv_sems, local_sems = refs[2 * n:]
        x, y, c = _mesh_pos()
        me = _dev_index(x, y, c)
        barrier = pltpu.get_barrier_semaphore()
        for r in range(1, N_DEV):
            pl.semaphore_signal(barrier, inc=1, device_id=_peer(x, y, c, r), device_id_type=_MESH)
        pl.semaphore_wait(barrier, _PEERS)
        local, sends, recvs = [], [], []
        for w in range(n):
            cp = pltpu.make_async_copy(srcs[w].at[me] if flags[w] else srcs[w], zones[w].at[me], local_sems.at[w])
            cp.start()
            local.append(cp)
            for r in range(1, N_DEV):
                peer = _peer(x, y, c, r)
                idx = _dev_index(*peer)
                k = _PEERS * w + r - 1
                src = srcs[w].at[idx] if flags[w] else srcs[w]
                send = pltpu.make_async_remote_copy(
                    src_ref=src, dst_ref=zones[w].at[me], send_sem=send_sems.at[k], recv_sem=recv_sems.at[k],
                    device_id=peer, device_id_type=_MESH)
                send.start()
                sends.append(send)
                recvs.append(pltpu.make_async_remote_copy(
                    src_ref=src, dst_ref=zones[w].at[idx], send_sem=send_sems.at[k], recv_sem=recv_sems.at[k],
                    device_id=peer, device_id_type=_MESH))
        for cp in recvs:
            cp.wait_recv()
        for cp in sends:
            cp.wait_send()
        for cp in local:
            cp.wait()

    return pl.kernel(
        body, name=name,
        out_type=[_sds((N_DEV,) + (s.shape[1:] if f else s.shape), s.dtype) for s, f in zip(sources, flags)],
        mesh=plsc.ScalarSubcoreMesh(axis_name="sequencer", num_cores=1),
        scratch_types=[pltpu.SemaphoreType.DMA((_PEERS * n,)), pltpu.SemaphoreType.DMA((_PEERS * n,)),
                       pltpu.SemaphoreType.DMA((n,))],
        compiler_params=pltpu.CompilerParams(collective_id=collective_id),
    )(*sources)


def _sequencer_gather(shards, name, collective_id):
    n = len(shards)
    fan = 4

    def body(*refs):
        srcs, zones = refs[:n], refs[n:2 * n]
        send_sems, recv_sems, local_sems = refs[2 * n:]
        x, y, c = _mesh_pos()
        me, sibling = (x, y, c), (x, y, 1 - c)
        chips = [(1 - x, y), (x, 1 - y), (1 - x, 1 - y)]
        barrier = pltpu.get_barrier_semaphore()
        for peer in [sibling] + [(*chip, c) for chip in chips]:
            pl.semaphore_signal(barrier, inc=1, device_id=peer, device_id_type=_MESH)
        pl.semaphore_wait(barrier, fan)

        def copy(w, k, block, to, src=None):
            slot = zones[w].at[_dev_index(*block)]
            return pltpu.make_async_remote_copy(
                src_ref=slot if src is None else src, dst_ref=slot,
                send_sem=send_sems.at[_PEERS * w + k], recv_sem=recv_sems.at[_PEERS * w + k],
                device_id=to, device_id_type=_MESH)

        mine, first, passed = [], [], []
        for w in range(n):
            cp = pltpu.make_async_copy(srcs[w], zones[w].at[_dev_index(*me)], local_sems.at[w])
            cp.start()
            mine.append(cp)
            sends = [copy(w, 0, me, sibling, src=srcs[w])]
            sends += [copy(w, 1 + j, me, (*chip, c), src=srcs[w]) for j, chip in enumerate(chips)]
            for cp in sends:
                cp.start()
            first += sends
        for w in range(n):
            for j, chip in enumerate(chips):
                copy(w, 1 + j, (*chip, c), me).wait_recv()
                cp = copy(w, fan + j, (*chip, c), sibling)
                cp.start()
                passed.append(cp)
        for w in range(n):
            copy(w, 0, sibling, me).wait_recv()
            for j, chip in enumerate(chips):
                copy(w, fan + j, (*chip, 1 - c), me).wait_recv()
        for cp in first + passed:
            cp.wait_send()
        for cp in mine:
            cp.wait()

    return pl.kernel(
        body, name=name, out_type=[_sds((N_DEV,) + s.shape, s.dtype) for s in shards],
        mesh=plsc.ScalarSubcoreMesh(axis_name="sequencer", num_cores=1),
        scratch_types=[pltpu.SemaphoreType.DMA((_PEERS * n,)), pltpu.SemaphoreType.DMA((_PEERS * n,)),
                       pltpu.SemaphoreType.DMA((n,))],
        compiler_params=pltpu.CompilerParams(collective_id=collective_id),
    )(*shards)


N_CHIPS = N_DEV // 2


def _sequencer_pair_exchange(sources, name, collective_id):
    n = len(sources)

    def body(*refs):
        srcs, zones = refs[:n], refs[n:2 * n]
        send_sems, recv_sems = refs[2 * n:]
        x, y, c = _mesh_pos()
        sibling = (x, y, 1 - c)
        barrier = pltpu.get_barrier_semaphore()
        pl.semaphore_signal(barrier, inc=1, device_id=sibling, device_id_type=_MESH)
        pl.semaphore_wait(barrier, 1)
        copies = []
        for w in range(n):
            for j in range(N_CHIPS):
                k = N_CHIPS * w + j
                cp = pltpu.make_async_remote_copy(
                    src_ref=srcs[w].at[2 * j + 1 - c], dst_ref=zones[w].at[j],
                    send_sem=send_sems.at[k], recv_sem=recv_sems.at[k], device_id=sibling, device_id_type=_MESH)
                cp.start()
                copies.append(cp)
        for cp in copies:
            cp.wait_recv()
        for cp in copies:
            cp.wait_send()

    return pl.kernel(
        body, name=name, out_type=[_sds((N_CHIPS,) + s.shape[1:], s.dtype) for s in sources],
        mesh=plsc.ScalarSubcoreMesh(axis_name="sequencer", num_cores=1),
        scratch_types=[pltpu.SemaphoreType.DMA((N_CHIPS * n,)), pltpu.SemaphoreType.DMA((N_CHIPS * n,))],
        compiler_params=pltpu.CompilerParams(collective_id=collective_id),
    )(*sources)


def _pair_sum(source, received, core, name, tokens=()):
    _, rows, cols = source.shape
    tr = _row_tile(rows)

    def body(core_ref, s_ref, r_ref, o_ref):
        c = core_ref[0]
        for j in range(N_CHIPS):
            o_ref[j] = (s_ref[2 * j + c].astype(_F32) + r_ref[j].astype(_F32)).astype(o_ref.dtype)

    return _call(body, (rows // tr,),
                 [pl.BlockSpec(memory_space=pltpu.SMEM), pl.BlockSpec((N_DEV, tr, cols), lambda i: (0, i, 0)),
                  pl.BlockSpec((N_CHIPS, tr, cols), lambda i: (0, i, 0))],
                 pl.BlockSpec((N_CHIPS, tr, cols), lambda i: (0, i, 0)),
                 _sds((N_CHIPS, rows, cols), source.dtype), name, tokens=tokens)(core, source, received)


def _sequencer_chip_exchange(partials, name, collective_id):
    n = len(partials)
    others = N_CHIPS - 1

    def body(*refs):
        srcs, zones = refs[:n], refs[n:2 * n]
        send_sems, recv_sems, local_sems = refs[2 * n:]
        x, y, c = _mesh_pos()
        mine = 2 * x + y
        peers = [(x ^ (r >> 1), y ^ (r & 1), c) for r in range(1, N_CHIPS)]
        barrier = pltpu.get_barrier_semaphore()
        for peer in peers:
            pl.semaphore_signal(barrier, inc=1, device_id=peer, device_id_type=_MESH)
        pl.semaphore_wait(barrier, others)
        local, sends, recvs = [], [], []
        for w in range(n):
            cp = pltpu.make_async_copy(srcs[w].at[mine], zones[w].at[mine], local_sems.at[w])
            cp.start()
            local.append(cp)
            for r, peer in enumerate(peers):
                theirs = 2 * peer[0] + peer[1]
                k = others * w + r
                send = pltpu.make_async_remote_copy(
                    src_ref=srcs[w].at[theirs], dst_ref=zones[w].at[mine],
                    send_sem=send_sems.at[k], recv_sem=recv_sems.at[k], device_id=peer, device_id_type=_MESH)
                send.start()
                sends.append(send)
                recvs.append(pltpu.make_async_remote_copy(
                    src_ref=srcs[w].at[theirs], dst_ref=zones[w].at[theirs],
                    send_sem=send_sems.at[k], recv_sem=recv_sems.at[k], device_id=peer, device_id_type=_MESH))
        for cp in recvs:
            cp.wait_recv()
        for cp in sends:
            cp.wait_send()
        for cp in local:
            cp.wait()

    return pl.kernel(
        body, name=name, out_type=[_sds(s.shape, s.dtype) for s in partials],
        mesh=plsc.ScalarSubcoreMesh(axis_name="sequencer", num_cores=1),
        scratch_types=[pltpu.SemaphoreType.DMA((others * n,)), pltpu.SemaphoreType.DMA((others * n,)),
                       pltpu.SemaphoreType.DMA((n,))],
        compiler_params=pltpu.CompilerParams(collective_id=collective_id),
    )(*partials)


def _row_tile(rows):
    return next(t for t in range(min(rows, 256), 0, -16) if rows % t == 0)


def _sum_parts(parts, name, tokens=()):
    _, rows, cols = parts.shape
    tr = _row_tile(rows)

    def body(p_ref, g_ref):
        g = p_ref[0].astype(_F32)
        for s in range(1, N_DEV):
            g = g + p_ref[s].astype(_F32)
        g_ref[...] = g

    return _call(body, (rows // tr,), [pl.BlockSpec((N_DEV, tr, cols), lambda i: (0, i, 0))],
                 _rows(tr, cols), _sds((rows, cols), _F32), name, tokens=tokens)(parts)


def _adam_update(g, w, m, v):
    new_m = ADAM_B1 * m + (1.0 - ADAM_B1) * g
    new_v = ADAM_B2 * v + (1.0 - ADAM_B2) * (g * g)
    m_hat = new_m / (1.0 - ADAM_B1 ** ADAM_STEP)
    v_hat = new_v / (1.0 - ADAM_B2 ** ADAM_STEP)
    return -ADAM_LR * (m_hat / (jnp.sqrt(v_hat) + ADAM_EPS) + ADAM_WD * w), new_m, new_v


def _adamw_small(parts, items, sums, name, tokens=()):
    n_p, n_i = len(parts), len(items)

    def body(*refs):
        p_refs, state, outs = refs[:n_p], refs[n_p:n_p + 3 * n_i], refs[n_p + 3 * n_i:]

        def total(part, rows, cols):
            shift = cols.start % _LANES
            window = slice(cols.start - shift, cols.start - shift + _LANES) if shift else cols
            n_rows = rows.stop - rows.start
            narrow = p_refs[part].dtype.itemsize < 4 and n_rows % _PACK_TILE
            tile = slice(rows.start, rows.start + _PACK_TILE) if narrow else rows
            g = p_refs[part][0, tile, window].astype(_F32)
            for s in range(1, N_DEV):
                g = g + p_refs[part][s, tile, window].astype(_F32)
            g = g[:n_rows] if narrow else g
            return pltpu.roll(g, _LANES - shift, 1)[:, :cols.stop - cols.start] if shift else g

        for i, (part, rows, cols, _, _, _) in enumerate(items):
            g = total(part, rows, cols)
            w_ref, m_ref, v_ref = state[3 * i:3 * i + 3]
            delta, new_m, new_v = _adam_update(g, w_ref[...], m_ref[...], v_ref[...])
            outs[4 * i][...] = g
            outs[4 * i + 1][...] = delta
            outs[4 * i + 2][...] = new_m
            outs[4 * i + 3][...] = new_v
        for j, (part, rows, cols) in enumerate(sums):
            outs[4 * n_i + j][...] = total(part, rows, cols)

    ins = list(parts) + [a for item in items for a in item[3:]]
    out_shapes = [item[3].shape for item in items for _ in range(4)]
    out_shapes += [(rows.stop - rows.start, cols.stop - cols.start) for _, rows, cols in sums]
    out = _call(body, (1,), [_whole(a.shape) for a in ins], [_whole(s) for s in out_shapes],
                [_sds(s, _F32) for s in out_shapes], name, tokens=tokens)(*ins)
    return [out[4 * i:4 * i + 4] for i in range(n_i)], out[4 * n_i:]


def _adamw(parts, w, m, v, name, tokens=()):
    rows, cols = w.shape
    tr = _row_tile(rows)
    n_parts = parts.shape[0]

    def body(p_ref, w_ref, m_ref, v_ref, g_ref, d_ref, nm_ref, nv_ref):
        g = p_ref[0].astype(_F32)
        for s in range(1, n_parts):
            g = g + p_ref[s].astype(_F32)
        new_m = ADAM_B1 * m_ref[...] + (1.0 - ADAM_B1) * g
        new_v = ADAM_B2 * v_ref[...] + (1.0 - ADAM_B2) * (g * g)
        m_hat = new_m / (1.0 - ADAM_B1 ** ADAM_STEP)
        v_hat = new_v / (1.0 - ADAM_B2 ** ADAM_STEP)
        g_ref[...] = g
        d_ref[...] = -ADAM_LR * (m_hat / (jnp.sqrt(v_hat) + ADAM_EPS) + ADAM_WD * w_ref[...])
        nm_ref[...] = new_m
        nv_ref[...] = new_v

    blk = _rows(tr, cols)
    return _call(body, (rows // tr,),
                 [pl.BlockSpec((n_parts, tr, cols), lambda i: (0, i, 0)), blk, blk, blk],
                 [blk] * 4, [_sds((rows, cols), _F32)] * 4, name, tokens=tokens)(parts, w, m, v)


_SMALL = ("g_pre_mix", "ssm_lambda_re", "ssm_lambda_im", "ssm_log_dt", "ssm_b_re", "ssm_b_im",
          "ssm_c_re", "ssm_c_im", "ssm_d", "b_glu", "attn_sinks", "g_ssm_out", "g_attn_out",
          "g_post_mix", "g_pre_ffn", "g_post_ffn")
_BIG = ("w_in", "w_glu", "w_out", "w_gate_up", "w_down")
_WEIGHTS = ("g_pre_mix", "w_in", "ssm_lambda_re", "ssm_lambda_im", "ssm_log_dt", "ssm_b_re", "ssm_b_im",
            "ssm_c_re", "ssm_c_im", "ssm_d", "w_glu", "b_glu", "attn_sinks", "g_ssm_out", "g_attn_out",
            "w_out", "g_post_mix", "g_pre_ffn", "w_gate_up", "w_down", "g_post_ffn")
_LANES = 128


_SHAPE_2D = {
    "g_pre_mix": (1, D_MODEL), "ssm_lambda_re": (SSM_GROUPS, SSM_STATE), "ssm_lambda_im": (SSM_GROUPS, SSM_STATE),
    "ssm_log_dt": (1, SSM_GROUPS), "ssm_b_re": (SSM_WIDTH, SSM_STATE), "ssm_b_im": (SSM_WIDTH, SSM_STATE),
    "ssm_c_re": (SSM_WIDTH, SSM_STATE), "ssm_c_im": (SSM_WIDTH, SSM_STATE), "ssm_d": (SSM_GROUPS, SSM_GROUP),
    "b_glu": (1, 2 * SSM_WIDTH), "attn_sinks": (1, N_Q_HEADS), "g_ssm_out": (1, SSM_WIDTH),
    "g_attn_out": (1, ATTN_WIDTH), "g_post_mix": (1, D_MODEL), "g_pre_ffn": (1, D_MODEL), "g_post_ffn": (1, D_MODEL)}
_ROW_WIDTH = {"g_pre_mix": D_MODEL, "b_glu": 2 * SSM_WIDTH, "attn_sinks": _LANES, "g_ssm_out": SSM_WIDTH,
              "g_attn_out": ATTN_WIDTH, "g_post_mix": D_MODEL, "g_pre_ffn": D_MODEL, "g_post_ffn": D_MODEL,
              "loss": _LANES}
_DENSE = ()
_PER_GROUP_TRANSPOSED = ("ssm_b_re", "ssm_b_im")


def _to_2d(name, a):
    if name in _PER_GROUP_TRANSPOSED:
        a = a.reshape(SSM_GROUPS, SSM_STATE, SSM_GROUP).transpose(0, 2, 1)
    return a.reshape(_SHAPE_2D[name])


def _from_2d(name, a, shape):
    if name in _PER_GROUP_TRANSPOSED:
        a = a.reshape(SSM_GROUPS, SSM_GROUP, SSM_STATE).transpose(0, 2, 1)
    return a.reshape(shape)


def _row_slots(names):
    slots, row, col = {}, 0, 0
    for n in names:
        width = _ROW_WIDTH[n]
        if col + width > D_MODEL:
            row, col = row + 1, 0
        slots[n] = (row, col, width)
        col += width
    return slots


def _stack_rows(named, slots):
    n_rows = -(-(max(r for r, _, _ in slots.values()) + 1) // 8) * 8
    lines = []
    for r in range(n_rows):
        pieces = [named[n] for n, (row, _, _) in slots.items() if row == r]
        used = sum(p.shape[1] for p in pieces)
        if used < D_MODEL:
            pieces.append(jnp.zeros((1, D_MODEL - used), _F32))
        lines.append(jnp.concatenate(pieces, axis=1) if len(pieces) > 1 else pieces[0])
    return jnp.concatenate(lines, axis=0)


def kernel(x, positions, g_pre_mix, w_in, ssm_lambda_re, ssm_lambda_im, ssm_log_dt, ssm_b_re, ssm_b_im, ssm_c_re, ssm_c_im, ssm_d, w_glu, b_glu, attn_sinks, g_ssm_out, g_attn_out, w_out, g_post_mix, g_pre_ffn, w_gate_up, w_down, g_post_ffn, loss_target, m_g_pre_mix, m_w_in, m_ssm_lambda_re, m_ssm_lambda_im, m_ssm_log_dt, m_ssm_b_re, m_ssm_b_im, m_ssm_c_re, m_ssm_c_im, m_ssm_d, m_w_glu, m_b_glu, m_attn_sinks, m_g_ssm_out, m_g_attn_out, m_w_out, m_g_post_mix, m_g_pre_ffn, m_w_gate_up, m_w_down, m_g_post_ffn, v_g_pre_mix, v_w_in, v_ssm_lambda_re, v_ssm_lambda_im, v_ssm_log_dt, v_ssm_b_re, v_ssm_b_im, v_ssm_c_re, v_ssm_c_im, v_ssm_d, v_w_glu, v_b_glu, v_attn_sinks, v_g_ssm_out, v_g_attn_out, v_w_out, v_g_post_mix, v_g_pre_ffn, v_w_gate_up, v_w_down, v_g_post_ffn):
    w = dict(g_pre_mix=g_pre_mix, w_in=w_in, ssm_lambda_re=ssm_lambda_re, ssm_lambda_im=ssm_lambda_im,
             ssm_log_dt=ssm_log_dt, ssm_b_re=ssm_b_re, ssm_b_im=ssm_b_im, ssm_c_re=ssm_c_re, ssm_c_im=ssm_c_im,
             ssm_d=ssm_d, w_glu=w_glu, b_glu=b_glu, attn_sinks=attn_sinks, g_ssm_out=g_ssm_out,
             g_attn_out=g_attn_out, w_out=w_out, g_post_mix=g_post_mix, g_pre_ffn=g_pre_ffn,
             w_gate_up=w_gate_up, w_down=w_down, g_post_ffn=g_post_ffn)
    m = dict(g_pre_mix=m_g_pre_mix, w_in=m_w_in, ssm_lambda_re=m_ssm_lambda_re, ssm_lambda_im=m_ssm_lambda_im,
             ssm_log_dt=m_ssm_log_dt, ssm_b_re=m_ssm_b_re, ssm_b_im=m_ssm_b_im, ssm_c_re=m_ssm_c_re,
             ssm_c_im=m_ssm_c_im, ssm_d=m_ssm_d, w_glu=m_w_glu, b_glu=m_b_glu, attn_sinks=m_attn_sinks,
             g_ssm_out=m_g_ssm_out, g_attn_out=m_g_attn_out, w_out=m_w_out, g_post_mix=m_g_post_mix,
             g_pre_ffn=m_g_pre_ffn, w_gate_up=m_w_gate_up, w_down=m_w_down, g_post_ffn=m_g_post_ffn)
    v = dict(g_pre_mix=v_g_pre_mix, w_in=v_w_in, ssm_lambda_re=v_ssm_lambda_re, ssm_lambda_im=v_ssm_lambda_im,
             ssm_log_dt=v_ssm_log_dt, ssm_b_re=v_ssm_b_re, ssm_b_im=v_ssm_b_im, ssm_c_re=v_ssm_c_re,
             ssm_c_im=v_ssm_c_im, ssm_d=v_ssm_d, w_glu=v_w_glu, b_glu=v_b_glu, attn_sinks=v_attn_sinks,
             g_ssm_out=v_g_ssm_out, g_attn_out=v_g_attn_out, w_out=v_w_out, g_post_mix=v_g_post_mix,
             g_pre_ffn=v_g_pre_ffn, w_gate_up=v_w_gate_up, w_down=v_w_down, g_post_ffn=v_g_post_ffn)

    transposed = ("w_in", "w_glu", "w_gate_up")
    native_transposed = ("w_in", "w_gate_up")
    shard = {n: (w[n][0].T if n in transposed else w[n][0]).astype(_BF16) for n in _BIG}
    gathered = {}
    for names, lands in (
            (("w_in",), _sequencer_exchange([shard["w_in"]], [False], "gather_w_in", 1)),
            (("w_glu", "w_out"), _sequencer_exchange([shard["w_glu"], shard["w_out"]], [False] * 2, "gather_mix", 2)),
            (("w_gate_up", "w_down"), _sequencer_gather([shard["w_gate_up"], shard["w_down"]], "gather_ffn", 3))):
        gathered.update({n: a.reshape(-1, a.shape[2]) for n, a in zip(names, lands)})

    def fetch(names, after):
        del after
        return [gathered[n] for n in names]

    sent = []
    ids = iter(range(4, 16))
    two_step = {}

    def publish(named):
        big = [n for n in named if n in _BIG]
        if set(big) == {"w_gate_up", "w_down"}:
            blocks = [named[n].reshape(N_DEV, -1, named[n].shape[1]) for n in big]
            two_step.update(names=big, blocks=blocks,
                            received=_sequencer_pair_exchange(blocks, "grads_pair", next(ids)))
            return [named[n] for n in big]
        rows = [n for n in named if n in _ROW_WIDTH]
        dense = [n for n in named if n in _DENSE]
        plain = [n for n in named if n not in big + rows + dense]
        sources = [named[n].reshape(N_DEV, -1, named[n].shape[1]) for n in big]
        slots = _row_slots(rows)
        if rows:
            sources.append(_stack_rows(named, slots))
        sources += [named[n].reshape(-1, _LANES) for n in dense] + [named[n] for n in plain]
        flags = [True] * len(big) + [False] * (len(sources) - len(big))
        cid = next(ids)
        sent.append((big, slots, dense, plain, _sequencer_exchange(sources, flags, "grads_%d" % cid, cid)))
        return [named[n] for n in big]

    def progress(after):
        core = lax.axis_index("c").astype(jnp.int32).reshape(1)
        partials = [_pair_sum(b, r, core, "pair_sum_" + n, [after])
                    for n, b, r in zip(two_step["names"], two_step["blocks"], two_step["received"])]
        sent.append((two_step["names"], {}, [], [], _sequencer_chip_exchange(partials, "grads_chips", next(ids))))
        return partials

    p = {n: w[n] for n in _SMALL}
    grad_x = _local_step(x[0], positions[0], loss_target[0], p, fetch, publish, progress)

    state = {n: [_to_2d(n, a) for a in (w[n], m[n], v[n])] for n in _SMALL}
    result = {}
    total_loss = None
    chain = []
    for big, slots, dense, plain, lands in sent:
        lands = list(lands)
        after = list(chain)
        for name in big:
            part = lands.pop(0)
            if name in native_transposed:
                updated = _adamw(part, w[name][0].T, m[name][0].T, v[name][0].T, "adamw_" + name, after)
                result[name] = [a.T[None] for a in updated]
                chain = [updated[3]]
                continue
            if name in transposed:
                part = _sum_parts(part, "sum_" + name, after).T[None]
            updated = _adamw(part, w[name][0], m[name][0], v[name][0], "adamw_" + name, after)
            result[name] = [a[None] for a in updated]
            chain = [updated[3]]
        parts, items, sums, names = [], [], [], []
        if slots:
            parts.append(lands.pop(0))
            for name, (row, col, _) in slots.items():
                if name == "loss":
                    sums.append((0, slice(row, row + 1), slice(col, col + _LANES)))
                else:
                    items.append((0, slice(row, row + 1), slice(col, col + _SHAPE_2D[name][1]), *state[name]))
                    names.append(name)
        for name in dense:
            part = lands.pop(0).reshape((N_DEV,) + _SHAPE_2D[name])
            result[name] = _adamw(part, *state[name], "adamw_" + name, after)
            chain = [result[name][3]]
        for name in plain:
            packed = _SSM_PACK if name == "ssm_pack" else {name: (0, _SHAPE_2D[name][0], 0, _SHAPE_2D[name][1])}
            for member, (first, rows_n, lane, cols_n) in packed.items():
                items.append((len(parts), slice(first, first + rows_n), slice(lane, lane + cols_n), *state[member]))
                names.append(member)
            parts.append(lands.pop(0))
        if items:
            updated, summed = _adamw_small(parts, items, sums, "adamw_small_" + names[0], after)
            chain = [updated[0][3]]
            result.update(dict(zip(names, updated)))
            if summed:
                total_loss = summed[0][0, 0]

    out = [total_loss, grad_x[None]]
    for kind in range(4):
        out += [_from_2d(n, result[n][kind], w[n].shape) for n in _WEIGHTS]
    return tuple(out)
```

```python
import functools
import math

import numpy as np
import jax
import jax.numpy as jnp
from jax import lax
from jax.experimental import pallas as pl
from jax.experimental.pallas import tpu as pltpu
from jax.experimental.pallas import tpu_sc as plsc

D_MODEL = 1024
SSM_WIDTH = 512
SSM_GROUP = 16
SSM_GROUPS = 32
SSM_STATE = 64
N_STATE = SSM_GROUPS * SSM_STATE
ATTN_WIDTH = 512
HEAD_DIM = 64
N_Q_HEADS = 8
N_KV_HEADS = 2
Q_PER_KV = 4
KV_WIDTH = 128
IN_WIDTH = 1280
BLOCK = 128
ROPE_DIM = 16
ROPE_THETA = 500000.0
D_FF = 2816
NORM_EPS = 1e-6
MASK_VALUE = -1e30
ADAM_LR = 0.001
ADAM_B1 = 0.9
ADAM_B2 = 0.999
ADAM_EPS = 1e-08
ADAM_WD = 0.01
ADAM_STEP = 10

N_DEV = 8
SCAN_CHUNKS = 8
SCAN_COLS = 512
TOKEN_TILE = 256
VMEM_LIMIT = 56 * 1024 * 1024

_F32 = jnp.float32
_BF16 = jnp.bfloat16
_MXU = jnp.bfloat16

_NN = ((1,), (0,))
_NT = ((1,), (1,))
_TN = ((0,), (0,))


def _dot(a, b, dims):
    return lax.dot_general(a.astype(_MXU), b.astype(_MXU), (dims, ((), ())),
                           preferred_element_type=_F32)


def _dot_exact(a, b, dims):
    return lax.dot_general(a.astype(_F32), b.astype(_F32), (dims, ((), ())),
                           precision=lax.Precision.HIGHEST, preferred_element_type=_F32)


def _iota(shape, dim):
    return lax.broadcasted_iota(jnp.int32, shape, dim)


def _rms_fwd(x, g):
    r = lax.rsqrt(jnp.mean(x * x, axis=-1, keepdims=True) + NORM_EPS)
    return x * r * g, r


def _rms_bwd(dy, x, g, r):
    a = dy * g
    xn = x * r
    dx = r * (a - xn * jnp.mean(a * xn, axis=-1, keepdims=True))
    dg = jnp.sum(dy * xn, axis=0, keepdims=True)
    return dx, dg


def _call(body, grid, in_specs, out_specs, out_shape, name, scratch=(), tokens=()):
    params = pltpu.CompilerParams(dimension_semantics=("arbitrary",) * len(grid),
                                  vmem_limit_bytes=VMEM_LIMIT)
    n_in, n_tok = len(in_specs), len(tokens)

    def run(*refs):
        return body(*refs[:n_in], *refs[n_in + n_tok:])

    call = pl.pallas_call(run, grid=grid,
                          in_specs=list(in_specs) + [pl.BlockSpec(memory_space=pl.ANY)] * n_tok,
                          out_specs=out_specs, out_shape=out_shape, scratch_shapes=list(scratch),
                          compiler_params=params, name=name)
    return lambda *args: call(*args, *tokens)


def _rows(tm, n):
    return pl.BlockSpec((tm, n), lambda i: (i, 0))


def _whole(shape):
    nd = len(shape)
    return pl.BlockSpec(shape, lambda i: (0,) * nd)


def _sds(shape, dtype):
    return jax.ShapeDtypeStruct(shape, dtype)


def _tile(L):
    return min(TOKEN_TILE, L)


def _chunk_tile(L):
    return L // SCAN_CHUNKS


def _chunk_block(L, n):
    return pl.BlockSpec((_chunk_tile(L), n), lambda i: (0, i))


def _chunk_shape(L, n):
    return (_chunk_tile(L), SCAN_CHUNKS * n)


def _accumulate(ref, val, first):
    @pl.when(first)
    def _():
        ref[...] = val

    @pl.when(jnp.logical_not(first))
    def _():
        ref[...] += val


def _rope_rows():
    half = ROPE_DIM // 2
    inv = (np.float32(ROPE_THETA) ** (-np.arange(half, dtype=np.float32) * np.float32(2.0) / np.float32(ROPE_DIM))).astype(np.float32)
    col = np.arange(KV_WIDTH) % HEAD_DIM
    freq = np.where(col < ROPE_DIM, inv[col % half], 0.0).astype(np.float32)
    sign = np.where(col < half, -1.0, np.where(col < ROPE_DIM, 1.0, 0.0)).astype(np.float32)
    return freq[None, :], sign[None, :]


def _rope_tables(pos_col):
    L = pos_col.shape[0]
    tm = _tile(L)
    freq, sign = _rope_rows()

    def body(pos_ref, freq_ref, sign_ref, cos_ref, sin_ref):
        ang = pos_ref[...].astype(_F32) * freq_ref[...]
        cos_ref[...] = jnp.cos(ang)
        sin_ref[...] = jnp.sin(ang) * sign_ref[...]

    return _call(body, (L // tm,),
                 [_rows(tm, 1), _whole((1, KV_WIDTH)), _whole((1, KV_WIDTH))],
                 [_rows(tm, KV_WIDTH), _rows(tm, KV_WIDTH)],
                 [_sds((L, KV_WIDTH), _F32)] * 2, "rope_tables")(pos_col, jnp.asarray(freq), jnp.asarray(sign))


def _widen(t, width):
    return t if width == KV_WIDTH else jnp.concatenate([t] * (width // KV_WIDTH), axis=1)


def _rope_partner(t):
    w = t.shape[1]
    in_head = _iota((1, w), 1) & (HEAD_DIM - 1)
    second = jnp.where(in_head < ROPE_DIM, pltpu.roll(t, ROPE_DIM // 2, 1), 0.0)
    return jnp.where(in_head < ROPE_DIM // 2, pltpu.roll(t, w - ROPE_DIM // 2, 1), second)


def _rope_apply(t, cos_t, sin_t):
    w = t.shape[1]
    return t * _widen(cos_t, w) + _rope_partner(t) * _widen(sin_t, w)


def _rope_transpose(dt, cos_t, sin_t):
    w = dt.shape[1]
    return dt * _widen(cos_t, w) + _rope_partner(dt * _widen(sin_t, w))


def _in_proj(x, g_pre_mix, w_in, cos_t, sin_t):
    L = x.shape[0]
    tm = _chunk_tile(L)

    def body(x_ref, g_ref, w_ref, cos_ref, sin_ref, hn_ref, u_ref, q_ref, k_ref, v_ref):
        hn, _ = _rms_fwd(x_ref[...], g_ref[...])
        hn = hn.astype(_BF16)
        hn_ref[...] = hn
        proj = _dot(hn, w_ref[...], _NT)
        u_ref[...] = proj[:, :SSM_WIDTH]
        q = proj[:, SSM_WIDTH:SSM_WIDTH + ATTN_WIDTH]
        k = proj[:, SSM_WIDTH + ATTN_WIDTH:SSM_WIDTH + ATTN_WIDTH + KV_WIDTH]
        cos_v, sin_v = cos_ref[...], sin_ref[...]
        q_ref[...] = _rope_apply(q, cos_v, sin_v).astype(_BF16)
        k_ref[...] = _rope_apply(k, cos_v, sin_v).astype(_BF16)
        v_ref[...] = proj[:, SSM_WIDTH + ATTN_WIDTH + KV_WIDTH:].astype(_BF16)

    return _call(body, (L // tm,),
                 [_rows(tm, D_MODEL), _whole((1, D_MODEL)), _whole((IN_WIDTH, D_MODEL)),
                  _rows(tm, KV_WIDTH), _rows(tm, KV_WIDTH)],
                 [_rows(tm, D_MODEL), _chunk_block(L, SSM_WIDTH), _rows(tm, ATTN_WIDTH),
                  _rows(tm, KV_WIDTH), _rows(tm, KV_WIDTH)],
                 [_sds((L, D_MODEL), _BF16), _sds(_chunk_shape(L, SSM_WIDTH), _F32), _sds((L, ATTN_WIDTH), _BF16),
                  _sds((L, KV_WIDTH), _BF16), _sds((L, KV_WIDTH), _BF16)],
                 "in_proj")(x, g_pre_mix, w_in, cos_t, sin_t)


def _s5_discretize(lam_re, lam_im, log_dt):
    lr = jnp.minimum(lam_re, -1e-4)
    li = lam_im
    dt = jnp.exp(log_dt)
    mag = jnp.exp(lr * dt)
    ar = mag * jnp.cos(li * dt)
    ai = mag * jnp.sin(li * dt)
    den = lr * lr + li * li
    fr = ((ar - 1.0) * lr + ai * li) / den
    fi = (ai * lr - (ar - 1.0) * li) / den
    return ar, ai, fr, fi


def _s5_bbar(lam_re, lam_im, log_dt, b_re, b_im):
    ar, ai, fr, fi = _s5_discretize(lam_re, lam_im, log_dt)
    return ar, ai, fr * b_re - fi * b_im, fr * b_im + fi * b_re


def _spread_masks():
    e16 = (_iota((SSM_GROUP, SSM_WIDTH), 1) & (SSM_GROUP - 1)) == _iota((SSM_GROUP, SSM_WIDTH), 0)
    e64 = (_iota((SSM_STATE, N_STATE), 1) & (SSM_STATE - 1)) == _iota((SSM_STATE, N_STATE), 0)
    mask_b = (_iota((N_STATE, SSM_WIDTH), 0) >> 6) == (_iota((N_STATE, SSM_WIDTH), 1) >> 4)
    mask_c = (_iota((SSM_WIDTH, N_STATE), 0) >> 4) == (_iota((SSM_WIDTH, N_STATE), 1) >> 6)
    return e16.astype(_F32), e64.astype(_F32), mask_b, mask_c


SUPER = 4
SB_STATE = N_STATE // SUPER
SB_WIDTH = SSM_WIDTH // SUPER


def _sb_state(k):
    return slice(SB_STATE * k, SB_STATE * (k + 1))


def _sb_width(k):
    return slice(SB_WIDTH * k, SB_WIDTH * (k + 1))


def _dt_column(log_dt_row):
    eye = _iota((SSM_GROUPS, SSM_GROUPS), 0) == _iota((SSM_GROUPS, SSM_GROUPS), 1)
    return jnp.sum(jnp.where(eye, log_dt_row, 0.0), axis=1, keepdims=True)


def _group_masks():
    e64 = ((_iota((SSM_STATE, N_STATE), 1) & (SSM_STATE - 1)) == _iota((SSM_STATE, N_STATE), 0)).astype(_F32)
    own = _iota((SSM_GROUPS, N_STATE), 0) == (_iota((SSM_GROUPS, N_STATE), 1) >> 6)
    return e64, own


def _rows_of_group():
    return ((_iota((SSM_WIDTH, SSM_GROUPS), 0) >> 4) == _iota((SSM_WIDTH, SSM_GROUPS), 1)).astype(_F32)


def _ssm_prep(lam_re, lam_im, log_dt, b_re, b_im, c_re, c_im):
    def body(lr_ref, li_ref, ld_ref, bre, bim, cre, cim, ar_ref, ai_ref, btr, bti, ctr, cti):
        ar, ai, fr, fi = _s5_discretize(lr_ref[...], li_ref[...], _dt_column(ld_ref[...]))
        e64, own = _group_masks()
        mask_c = (_iota((SSM_WIDTH, N_STATE), 0) >> 4) == (_iota((SSM_WIDTH, N_STATE), 1) >> 6)

        def to_row(t):
            return jnp.sum(jnp.where(own, _dot_exact(t, e64, _NN), 0.0), axis=0, keepdims=True)

        def fold(m):
            full = jnp.where(mask_c, _dot(m, e64, _NN), 0.0)
            return sum(full[_sb_width(k), :] for k in range(SUPER)).astype(_BF16)

        ar_ref[...] = to_row(ar)
        ai_ref[...] = to_row(ai)
        spread = _rows_of_group()
        fr_t = _dot_exact(spread, fr, _NN)
        fi_t = _dot_exact(spread, fi, _NN)
        btr[...] = fold(fr_t * bre[...] - fi_t * bim[...])
        bti[...] = fold(fr_t * bim[...] + fi_t * bre[...])
        ctr[...] = fold(cre[...])
        cti[...] = fold(cim[...])

    row = (1, N_STATE)
    ins = [lam_re, lam_im, log_dt, b_re, b_im, c_re, c_im]
    return _call(body, (1,), [_whole(a.shape) for a in ins],
                 [_whole(row), _whole(row)] + [_whole((SB_WIDTH, N_STATE))] * 4,
                 [_sds(row, _F32), _sds(row, _F32)] + [_sds((SB_WIDTH, N_STATE), _BF16)] * 4,
                 "ssm_prep")(*ins)


def _ssm_bu(u, bt_re, bt_im):
    L = u.shape[0]
    tm = _tile(L)

    def body(u_ref, br_ref, bi_ref, or_ref, oi_ref):
        for k in range(SUPER):
            ub = u_ref[:, _sb_width(k)].astype(_BF16)
            or_ref[:, _sb_state(k)] = _dot(ub, br_ref[:, _sb_state(k)], _NN)
            oi_ref[:, _sb_state(k)] = _dot(ub, bi_ref[:, _sb_state(k)], _NN)

    return _call(body, (L // tm,),
                 [_rows(tm, SSM_WIDTH), _whole((SB_WIDTH, N_STATE)), _whole((SB_WIDTH, N_STATE))],
                 [_rows(tm, N_STATE), _rows(tm, N_STATE)],
                 [_sds((L, N_STATE), _F32)] * 2, "ssm_bu")(u, bt_re, bt_im)


def _complex_power(ar, ai, n):
    def step(_, c):
        pr, pi = c
        return pr * ar - pi * ai, pr * ai + pi * ar
    return lax.fori_loop(0, n, step, (jnp.ones_like(ar), jnp.zeros_like(ai)))


def _chunk_carries(er, ei, pr, pi, reverse):
    rows = _iota(er.shape, 0)
    sr = jnp.zeros_like(pr)
    si = jnp.zeros_like(pi)
    out_r = jnp.zeros_like(er)
    out_i = jnp.zeros_like(ei)
    order = range(SCAN_CHUNKS - 1, 0, -1) if reverse else range(SCAN_CHUNKS - 1)
    for c in order:
        e_r = er[c:c + 1, :]
        e_i = ei[c:c + 1, :]
        sr, si = pr * sr - pi * si + e_r, pr * si + pi * sr + e_i
        nxt = c - 1 if reverse else c + 1
        out_r = jnp.where(rows == nxt, sr, out_r)
        out_i = jnp.where(rows == nxt, si, out_i)
    return out_r, out_i


def _scan_fwd(b_re, b_im, a_re, a_im):
    T = b_re.shape[0]
    W = SCAN_COLS
    blk = pl.BlockSpec((T, SCAN_CHUNKS, W), lambda j: (0, 0, j))
    vec = pl.BlockSpec((1, W), lambda j: (0, j))

    def body(br_ref, bi_ref, ar_ref, ai_ref, xr_ref, xi_ref):
        ar, ai = ar_ref[...], ai_ref[...]
        ar8 = jnp.broadcast_to(ar, (SCAN_CHUNKS, W))
        ai8 = jnp.broadcast_to(ai, (SCAN_CHUNKS, W))

        def local(t, c):
            cr, ci = c
            return ar8 * cr - ai8 * ci + br_ref[t], ar8 * ci + ai8 * cr + bi_ref[t]

        zero = jnp.zeros((SCAN_CHUNKS, W), _F32)
        er, ei = lax.fori_loop(0, T, local, (zero, zero))
        pr, pi = _complex_power(ar, ai, T)
        sr, si = _chunk_carries(er, ei, pr, pi, reverse=False)

        def final(t, c):
            nr, ni = local(t, c)
            xr_ref[t] = nr
            xi_ref[t] = ni
            return nr, ni

        lax.fori_loop(0, T, final, (sr, si))

    shape = _sds(b_re.shape, _F32)
    return _call(body, (N_STATE // W,), [blk, blk, vec, vec], [blk, blk], [shape, shape],
                 "scan_fwd")(b_re, b_im, a_re, a_im)


def _scan_bwd(dx_re, dx_im, x_re, x_im, a_re, a_im, tokens=()):
    T = dx_re.shape[0]
    W = SCAN_COLS
    blk = pl.BlockSpec((T, SCAN_CHUNKS, W), lambda j: (0, 0, j))
    vec = pl.BlockSpec((1, W), lambda j: (0, j))

    def body(dr_ref, di_ref, xr_ref, xi_ref, ar_ref, ai_ref, lr_ref, li_ref, dar_ref, dai_ref):
        ar, ai = ar_ref[...], ai_ref[...]
        ar8 = jnp.broadcast_to(ar, (SCAN_CHUNKS, W))
        ai8 = jnp.broadcast_to(ai, (SCAN_CHUNKS, W))

        def local(t, c):
            cr, ci = c
            return ar8 * cr + ai8 * ci + dr_ref[t], ar8 * ci - ai8 * cr + di_ref[t]

        zero = jnp.zeros((SCAN_CHUNKS, W), _F32)
        er, ei = lax.fori_loop(0, T, lambda k, c: local(T - 1 - k, c), (zero, zero))
        pr, pi = _complex_power(ar, -ai, T)
        sr, si = _chunk_carries(er, ei, pr, pi, reverse=True)

        def grad_a(acc, nr, ni, xpr, xpi):
            return acc[0] + nr * xpr + ni * xpi, acc[1] + ni * xpr - nr * xpi

        def final(k, c):
            t = T - 1 - k
            nr, ni = local(t, c[:2])
            lr_ref[t] = nr
            li_ref[t] = ni
            gr, gi = grad_a(c[2:], nr, ni, xr_ref[t - 1], xi_ref[t - 1])
            return nr, ni, gr, gi

        cr, ci, gr, gi = lax.fori_loop(0, T - 1, final, (sr, si, zero, zero))
        nr, ni = local(0, (cr, ci))
        lr_ref[0] = nr
        li_ref[0] = ni
        first = _iota((SCAN_CHUNKS, W), 0) == 0
        xpr = jnp.where(first, 0.0, pltpu.roll(xr_ref[T - 1], 1, 0))
        xpi = jnp.where(first, 0.0, pltpu.roll(xi_ref[T - 1], 1, 0))
        gr, gi = grad_a((gr, gi), nr, ni, xpr, xpi)
        dar_ref[...] = jnp.sum(gr, axis=0, keepdims=True)
        dai_ref[...] = jnp.sum(gi, axis=0, keepdims=True)

    shape = _sds(dx_re.shape, _F32)
    row = _sds((1, N_STATE), _F32)
    return _call(body, (N_STATE // W,), [blk, blk, blk, blk, vec, vec], [blk, blk, vec, vec],
                 [shape, shape, row, row], "scan_bwd", tokens=tokens)(dx_re, dx_im, x_re, x_im, a_re, a_im)


_GELU_K = math.sqrt(2.0 / math.pi)
_GELU_C = 0.044715


def _gelu(y):
    return 0.5 * y * (1.0 + jnp.tanh(_GELU_K * (y + _GELU_C * y * y * y)))


def _gelu_grad(y):
    t = jnp.tanh(_GELU_K * (y + _GELU_C * y * y * y))
    return 0.5 * (1.0 + t) + 0.5 * y * (1.0 - t * t) * _GELU_K * (1.0 + 3.0 * _GELU_C * y * y)


def _ssm_out(x_re, x_im, u, ct_re, ct_im, d_row, w_glu, b_glu, g_ssm):
    L = u.shape[0]
    tm = _tile(L)

    def body(xr_ref, xi_ref, u_ref, cr_ref, ci_ref, d_ref, w_ref, b_ref, g_ref, y_ref, z_ref, n_ref):
        cx = [_dot(xr_ref[:, _sb_state(k)], cr_ref[:, _sb_state(k)], _NT)
              - _dot(xi_ref[:, _sb_state(k)], ci_ref[:, _sb_state(k)], _NT) for k in range(SUPER)]
        y = jnp.concatenate(cx, axis=1) + d_ref[...] * u_ref[...]
        y_ref[...] = y
        z = _dot(_gelu(y), w_ref[...], _NT) + b_ref[...]
        z_ref[...] = z
        out = z[:, :SSM_WIDTH] * jax.nn.sigmoid(z[:, SSM_WIDTH:])
        n, _ = _rms_fwd(out, g_ref[...])
        n_ref[...] = n.astype(_BF16)

    return _call(body, (L // tm,),
                 [_rows(tm, N_STATE), _rows(tm, N_STATE), _rows(tm, SSM_WIDTH),
                  _whole((SB_WIDTH, N_STATE)), _whole((SB_WIDTH, N_STATE)), _whole((1, SSM_WIDTH)),
                  _whole((2 * SSM_WIDTH, SSM_WIDTH)), _whole((1, 2 * SSM_WIDTH)), _whole((1, SSM_WIDTH))],
                 [_rows(tm, SSM_WIDTH), _rows(tm, 2 * SSM_WIDTH), _rows(tm, SSM_WIDTH)],
                 [_sds((L, SSM_WIDTH), _F32), _sds((L, 2 * SSM_WIDTH), _F32), _sds((L, SSM_WIDTH), _BF16)],
                 "ssm_out")(x_re, x_im, u, ct_re, ct_im, d_row, w_glu, b_glu, g_ssm)


def _ssm_out_bwd(dn, y, z, u, ct_re, ct_im, d_row, w_glu, g_ssm):
    L = u.shape[0]
    tm = _tile(L)

    def body(dn_ref, y_ref, z_ref, u_ref, cr_ref, ci_ref, d_ref, w_ref, g_ref,
             gy_ref, dz_ref, dy_ref, dud_ref, dxr_ref, dxi_ref, dg_ref, db_ref, dd_ref):
        first = pl.program_id(0) == 0
        z = z_ref[...]
        z1, z2 = z[:, :SSM_WIDTH], z[:, SSM_WIDTH:]
        sig = jax.nn.sigmoid(z2)
        out = z1 * sig
        g = g_ref[...]
        _, r = _rms_fwd(out, g)
        dout, dg = _rms_bwd(dn_ref[...], out, g, r)
        _accumulate(dg_ref, dg, first)
        dz = jnp.concatenate([dout * sig, dout * z1 * sig * (1.0 - sig)], axis=1)
        _accumulate(db_ref, jnp.sum(dz, axis=0, keepdims=True), first)
        dzb = dz.astype(_BF16)
        dz_ref[...] = dzb
        y = y_ref[...]
        gy_ref[...] = _gelu(y).astype(_BF16)
        dy = _dot(dzb, w_ref[...], _NN) * _gelu_grad(y)
        u = u_ref[...]
        _accumulate(dd_ref, jnp.sum(dy * u, axis=0, keepdims=True), first)
        dud_ref[...] = d_ref[...] * dy
        dyb = dy.astype(_BF16)
        dy_ref[...] = dyb
        for k in range(SUPER):
            dxr_ref[:, _sb_state(k)] = _dot(dyb[:, _sb_width(k)], cr_ref[:, _sb_state(k)], _NN)
            dxi_ref[:, _sb_state(k)] = -_dot(dyb[:, _sb_width(k)], ci_ref[:, _sb_state(k)], _NN)

    row = _whole((1, SSM_WIDTH))
    return _call(body, (L // tm,),
                 [_rows(tm, SSM_WIDTH), _rows(tm, SSM_WIDTH), _rows(tm, 2 * SSM_WIDTH), _rows(tm, SSM_WIDTH),
                  _whole((SB_WIDTH, N_STATE)), _whole((SB_WIDTH, N_STATE)), row,
                  _whole((2 * SSM_WIDTH, SSM_WIDTH)), row],
                 [_rows(tm, SSM_WIDTH), _rows(tm, 2 * SSM_WIDTH), _rows(tm, SSM_WIDTH), _rows(tm, SSM_WIDTH),
                  _rows(tm, N_STATE), _rows(tm, N_STATE), row, _whole((1, 2 * SSM_WIDTH)), row],
                 [_sds((L, SSM_WIDTH), _BF16), _sds((L, 2 * SSM_WIDTH), _BF16), _sds((L, SSM_WIDTH), _BF16),
                  _sds((L, SSM_WIDTH), _F32), _sds((L, N_STATE), _F32), _sds((L, N_STATE), _F32),
                  _sds((1, SSM_WIDTH), _F32), _sds((1, 2 * SSM_WIDTH), _F32), _sds((1, SSM_WIDTH), _F32)],
                 "ssm_out_bwd")(dn, y, z, u, ct_re, ct_im, d_row, w_glu, g_ssm)


def _ssm_du(lam_re, lam_im, bt_re, bt_im, dud):
    L = dud.shape[0]
    tm = _tile(L)

    def body(lr_ref, li_ref, br_ref, bi_ref, dud_ref, du_ref):
        for k in range(SUPER):
            du_ref[:, _sb_width(k)] = (_dot(lr_ref[:, _sb_state(k)], br_ref[:, _sb_state(k)], _NT)
                                       + _dot(li_ref[:, _sb_state(k)], bi_ref[:, _sb_state(k)], _NT)
                                       + dud_ref[:, _sb_width(k)])

    return _call(body, (L // tm,),
                 [_rows(tm, N_STATE), _rows(tm, N_STATE), _whole((SB_WIDTH, N_STATE)),
                  _whole((SB_WIDTH, N_STATE)), _rows(tm, SSM_WIDTH)],
                 _rows(tm, SSM_WIDTH), _sds((L, SSM_WIDTH), _F32), "ssm_du")(lam_re, lam_im, bt_re, bt_im, dud)


def _ssm_weight_grads(dy, x_re, x_im, lam_re, lam_im, u):
    L = u.shape[0]

    def body(dy_ref, xr_ref, xi_ref, lr_ref, li_ref, u_ref, dcr_ref, dci_ref, dbr_ref, dbi_ref):
        dyb = dy_ref[...]
        ub = u_ref[...].astype(_BF16)
        dcr_ref[...] = _dot(dyb, xr_ref[...], _TN)
        dci_ref[...] = _dot(dyb, xi_ref[...], _TN)
        dbr_ref[...] = _dot(ub, lr_ref[...], _TN)
        dbi_ref[...] = _dot(ub, li_ref[...], _TN)

    width = pl.BlockSpec((L, SB_WIDTH), lambda k: (0, k))
    state = pl.BlockSpec((L, SB_STATE), lambda k: (0, k))
    out = pl.BlockSpec((SB_WIDTH, SB_STATE), lambda k: (0, k))
    return _call(body, (SUPER,), [width, state, state, state, state, width], [out] * 4,
                 [_sds((SB_WIDTH, N_STATE), _F32)] * 4,
                 "ssm_weight_grads")(dy, x_re, x_im, lam_re, lam_im, u)


_SSM_PACK = {"ssm_b_re": (0, SSM_WIDTH, 0, SSM_STATE), "ssm_c_re": (0, SSM_WIDTH, 64, SSM_STATE),
             "ssm_b_im": (512, SSM_WIDTH, 0, SSM_STATE), "ssm_c_im": (512, SSM_WIDTH, 64, SSM_STATE),
             "ssm_lambda_re": (1024, SSM_GROUPS, 0, SSM_STATE), "ssm_lambda_im": (1024, SSM_GROUPS, 64, SSM_STATE),
             "ssm_d": (1056, SSM_GROUPS, 0, SSM_GROUP), "ssm_log_dt": (1088, 1, 0, SSM_GROUPS)}
_PACK_TILE = 16
_SSM_PACK_ROWS = 1088 + _PACK_TILE


def _ssm_param_bwd(da_re, da_im, dbt_re, dbt_im, dct_re, dct_im, lam_re, lam_im, log_dt, b_re, b_im, g_d):
    def body(dar, dai, dbr, dbi, dcr, dci, lr_ref, li_ref, ld_ref, bre_ref, bim_ref, gd_ref, pack_ref):
        lane_in = _iota((SSM_STATE, _LANES), 0)
        lane_out = _iota((SSM_STATE, _LANES), 1)
        low = (lane_out == lane_in).astype(_F32)
        high = (lane_out == lane_in + SSM_STATE).astype(_F32)

        def side_by_side(a, b):
            return _dot_exact(a, low, _NN) + _dot_exact(b, high, _NN)

        tail = _SSM_PACK["ssm_d"][0]
        pack_ref[tail:, :] = jnp.zeros((_SSM_PACK_ROWS - tail, _LANES), _BF16)
        pack_ref[tail:tail + SSM_GROUPS, 0:SSM_GROUP] = gd_ref[...].astype(_BF16)
        own_c = (_iota((SB_WIDTH, SB_STATE), 0) >> 4) == (_iota((SB_WIDTH, SB_STATE), 1) >> 6)

        def unfold(ref):
            blocks = []
            for k in range(SUPER):
                t = jnp.where(own_c, ref[:, _sb_state(k)], 0.0)
                t = sum(t[:, 128 * i:128 * (i + 1)] for i in range(SB_STATE // 128))
                blocks.append((t + pltpu.roll(t, SSM_STATE, 1))[:, :SSM_STATE])
            return jnp.concatenate(blocks, axis=0)

        dbb_re, dbb_im = unfold(dbr), unfold(dbi)
        b_re, b_im = bre_ref[...], bim_ref[...]
        dt_col = _dt_column(ld_ref[...])
        (_, _, fr, fi), vjp = jax.vjp(_s5_discretize, lr_ref[...], li_ref[...], dt_col)
        spread = _rows_of_group()
        fr_t = _dot_exact(spread, fr, _NN)
        fi_t = _dot_exact(spread, fi, _NN)
        pack_ref[0:SSM_WIDTH, :] = side_by_side(fr_t * dbb_re + fi_t * dbb_im, unfold(dcr)).astype(_BF16)
        pack_ref[SSM_WIDTH:2 * SSM_WIDTH, :] = side_by_side(fr_t * dbb_im - fi_t * dbb_re, -unfold(dci)).astype(_BF16)
        d_fr = _dot_exact(spread, dbb_re * b_re + dbb_im * b_im, _TN)
        d_fi = _dot_exact(spread, dbb_im * b_re - dbb_re * b_im, _TN)
        e64, own = _group_masks()

        def from_row(ref):
            return _dot_exact(jnp.where(own, ref[...], 0.0), e64, _NT)

        d_lr, d_li, d_dt = vjp((from_row(dar), from_row(dai), d_fr, d_fi))
        lam_rows = _SSM_PACK["ssm_lambda_re"][0]
        pack_ref[lam_rows:lam_rows + SSM_GROUPS, :] = side_by_side(d_lr, d_li).astype(_BF16)
        eye = (_iota((SSM_GROUPS, SSM_GROUPS), 0) == _iota((SSM_GROUPS, SSM_GROUPS), 1)).astype(_F32)
        dt_row = _SSM_PACK["ssm_log_dt"][0]
        pack_ref[dt_row:dt_row + _PACK_TILE, 0:SSM_GROUPS] = _dot_exact(
            jnp.broadcast_to(d_dt, (SSM_GROUPS, 128)), eye, _TN)[0:_PACK_TILE].astype(_BF16)

    ins = [da_re, da_im, dbt_re, dbt_im, dct_re, dct_im, lam_re, lam_im, log_dt, b_re, b_im, g_d]
    out = (_SSM_PACK_ROWS, _LANES)
    return _call(body, (1,), [_whole(a.shape) for a in ins], _whole(out), _sds(out, _BF16), "ssm_param_bwd")(*ins)


def _head_spread(j):
    r = _iota((KV_WIDTH, 256), 0)
    c = _iota((KV_WIDTH, 256), 1)
    return (r == HEAD_DIM * j + (c & (HEAD_DIM - 1))).astype(_BF16)


STACK = Q_PER_KV * BLOCK


def _stack_heads(t):
    lane_head = _iota((1, 256), 1) >> 6
    return jnp.concatenate([jnp.where(lane_head == g, t, jnp.zeros_like(t)) for g in range(Q_PER_KV)], axis=0)


def _unstack_heads(t):
    lane_head = _iota((1, 256), 1) >> 6
    return sum(jnp.where(lane_head == g, t[BLOCK * g:BLOCK * (g + 1)], 0.0) for g in range(Q_PER_KV))


def _stacked_sinks(sink_ref, j):
    block = _iota((STACK, 1), 0) >> 7
    col = jnp.full((STACK, 1), sink_ref[Q_PER_KV * j], _F32)
    for g in range(1, Q_PER_KV):
        col = jnp.where(block == g, sink_ref[Q_PER_KV * j + g], col)
    return col


def _fold_heads(t, j):
    t = t[:, :KV_WIDTH] + t[:, KV_WIDTH:]
    t = t + pltpu.roll(t, HEAD_DIM, 1)
    return jnp.where((_iota((1, KV_WIDTH), 1) >> 6) == j, t, 0.0)


def _attn_scores(q_stacked, kt, blk, sink):
    s = _dot(q_stacked, kt, _NT) * (HEAD_DIM ** -0.5)
    qi = _iota((STACK, 2 * BLOCK), 0) & (BLOCK - 1)
    kj = _iota((STACK, 2 * BLOCK), 1)
    rel = qi + BLOCK - kj
    valid = (rel >= 0) & (rel < BLOCK) & (blk * BLOCK - BLOCK + kj >= 0)
    s = jnp.where(valid, s, MASK_VALUE)
    m = jnp.maximum(jnp.max(s, axis=-1, keepdims=True), sink)
    p = jnp.exp(s - m)
    e_sink = jnp.exp(sink - m)
    den = jnp.sum(p, axis=-1, keepdims=True) + e_sink
    return p / den, e_sink / den


def _attn_specs():
    prev = lambda i: (jnp.maximum(i - 1, 0), 0)
    cur = lambda i: (i, 0)
    kv = [pl.BlockSpec((BLOCK, KV_WIDTH), prev), pl.BlockSpec((BLOCK, KV_WIDTH), cur)]
    return [pl.BlockSpec((BLOCK, ATTN_WIDTH), cur)] + kv + kv


def _attn_fwd(q, k, v, sinks, g_attn):
    L = q.shape[0]

    def body(q_ref, kp_ref, kc_ref, vp_ref, vc_ref, sink_ref, g_ref, o_ref, n_ref):
        blk = pl.program_id(0)
        kwin = jnp.concatenate([kp_ref[...], kc_ref[...]], axis=0)
        vwin = jnp.concatenate([vp_ref[...], vc_ref[...]], axis=0)
        halves = []
        for j in range(N_KV_HEADS):
            spread = _head_spread(j)
            kt = _dot(kwin, spread, _NN).astype(_BF16)
            vt = _dot(vwin, spread, _NN).astype(_BF16)
            qs = _stack_heads(q_ref[:, 256 * j:256 * (j + 1)])
            p, _ = _attn_scores(qs, kt, blk, _stacked_sinks(sink_ref, j))
            halves.append(_unstack_heads(_dot(p, vt, _NN)))
        o = jnp.concatenate(halves, axis=1)
        o_ref[...] = o
        n, _ = _rms_fwd(o, g_ref[...])
        n_ref[...] = n.astype(_BF16)

    cur = lambda i: (i, 0)
    return _call(body, (L // BLOCK,),
                 _attn_specs() + [pl.BlockSpec(memory_space=pltpu.SMEM), _whole((1, ATTN_WIDTH))],
                 [pl.BlockSpec((BLOCK, ATTN_WIDTH), cur)] * 2,
                 [_sds((L, ATTN_WIDTH), _F32), _sds((L, ATTN_WIDTH), _BF16)],
                 "attn_fwd")(q, k, k, v, v, sinks, g_attn)


def _attn_bwd(q, k, v, o, dn, sinks, g_attn):
    L = q.shape[0]

    def body(q_ref, kp_ref, kc_ref, vp_ref, vc_ref, o_ref, dn_ref, sink_ref, g_ref,
             dq_ref, dk_ref, dv_ref, dsink_ref, dg_ref):
        blk = pl.program_id(0)
        first = blk == 0

        @pl.when(first)
        def _():
            dk_ref[...] = jnp.zeros_like(dk_ref)
            dv_ref[...] = jnp.zeros_like(dv_ref)
            dsink_ref[...] = jnp.zeros_like(dsink_ref)

        o = o_ref[...]
        g = g_ref[...]
        _, r = _rms_fwd(o, g)
        do, dg = _rms_bwd(dn_ref[...], o, g, r)
        _accumulate(dg_ref, dg, first)
        kwin = jnp.concatenate([kp_ref[...], kc_ref[...]], axis=0)
        vwin = jnp.concatenate([vp_ref[...], vc_ref[...]], axis=0)
        lane = _iota((1, 128), 1)
        dsink = jnp.zeros((1, 128), _F32)
        dkwin = jnp.zeros((2 * BLOCK, KV_WIDTH), _F32)
        dvwin = jnp.zeros((2 * BLOCK, KV_WIDTH), _F32)
        dq_halves = []
        for j in range(N_KV_HEADS):
            spread = _head_spread(j)
            kt = _dot(kwin, spread, _NN).astype(_BF16)
            vt = _dot(vwin, spread, _NN).astype(_BF16)
            qs = _stack_heads(q_ref[:, 256 * j:256 * (j + 1)])
            dos = _stack_heads(do[:, 256 * j:256 * (j + 1)]).astype(_BF16)
            p, p_sink = _attn_scores(qs, kt, blk, _stacked_sinks(sink_ref, j))
            dp = _dot(dos, vt, _NT)
            delta = jnp.sum(p * dp, axis=-1, keepdims=True)
            ds = (p * (dp - delta) * (HEAD_DIM ** -0.5)).astype(_BF16)
            sink_term = p_sink * delta
            for g in range(Q_PER_KV):
                head_sum = jnp.sum(sink_term[BLOCK * g:BLOCK * (g + 1)], axis=0, keepdims=True)
                dsink = dsink - jnp.where(lane == Q_PER_KV * j + g, head_sum, 0.0)
            dvwin = dvwin + _fold_heads(_dot(p, dos, _TN), j)
            dkwin = dkwin + _fold_heads(_dot(ds, qs, _TN), j)
            dq_halves.append(_unstack_heads(_dot(ds, kt, _NN)))
        dq_ref[...] = jnp.concatenate(dq_halves, axis=1)
        dsink_ref[...] += dsink
        prev = pl.ds(pl.multiple_of(jnp.maximum(blk - 1, 0) * BLOCK, BLOCK), BLOCK)
        cur = pl.ds(pl.multiple_of(blk * BLOCK, BLOCK), BLOCK)
        dk_ref[prev, :] += dkwin[:BLOCK]
        dk_ref[cur, :] += dkwin[BLOCK:]
        dv_ref[prev, :] += dvwin[:BLOCK]
        dv_ref[cur, :] += dvwin[BLOCK:]

    cur = lambda i: (i, 0)
    blk_q = pl.BlockSpec((BLOCK, ATTN_WIDTH), cur)
    return _call(body, (L // BLOCK,),
                 _attn_specs() + [blk_q, blk_q, pl.BlockSpec(memory_space=pltpu.SMEM), _whole((1, ATTN_WIDTH))],
                 [blk_q, _whole((L, KV_WIDTH)), _whole((L, KV_WIDTH)), _whole((1, 128)), _whole((1, ATTN_WIDTH))],
                 [_sds((L, ATTN_WIDTH), _F32), _sds((L, KV_WIDTH), _F32), _sds((L, KV_WIDTH), _F32),
                  _sds((1, 128), _F32), _sds((1, ATTN_WIDTH), _F32)],
                 "attn_bwd")(q, k, k, v, v, o, dn, sinks, g_attn)


def _out_proj(n_ssm, n_attn, x, w_out, g_post_mix, g_pre_ffn):
    L = x.shape[0]
    tm = _chunk_tile(L)

    def body(ns_ref, na_ref, x_ref, w_ref, g1_ref, g2_ref, merged_ref, mo_ref, h1_ref, hn2_ref):
        merged = jnp.concatenate([ns_ref[...], na_ref[...]], axis=1)
        merged_ref[...] = merged
        mo = _dot(merged, w_ref[...], _NN)
        mo_ref[...] = mo
        n, _ = _rms_fwd(mo, g1_ref[...])
        h1 = x_ref[...] + n
        h1_ref[...] = h1
        hn2, _ = _rms_fwd(h1, g2_ref[...])
        hn2_ref[...] = hn2.astype(_BF16)

    row = _whole((1, D_MODEL))
    return _call(body, (L // tm,),
                 [_chunk_block(L, SSM_WIDTH), _rows(tm, ATTN_WIDTH), _rows(tm, D_MODEL), _whole((D_MODEL, D_MODEL)),
                  row, row],
                 [_rows(tm, D_MODEL)] * 4,
                 [_sds((L, D_MODEL), _BF16), _sds((L, D_MODEL), _F32), _sds((L, D_MODEL), _F32), _sds((L, D_MODEL), _BF16)],
                 "out_proj")(n_ssm, n_attn, x, w_out, g_post_mix, g_pre_ffn)


def _ffn(hn2, h1, target, w_gate_up, w_down, g_pre_ffn, g_post_ffn):
    L = h1.shape[0]
    tm = _tile(L)
    half = D_FF // 2

    def body(hn2_ref, h1_ref, tgt_ref, wgu_hbm, wd_hbm, g2_ref, g3_ref,
             act_ref, dgu_ref, dff_ref, dh1_ref, loss_ref, dg3_ref, dg2_ref,
             wgu, wd, gu, sem):
        first = pl.program_id(0) == 0

        @pl.when(first)
        def _():
            c1 = pltpu.make_async_copy(wgu_hbm, wgu, sem.at[0])
            c2 = pltpu.make_async_copy(wd_hbm, wd, sem.at[1])
            c1.start()
            c2.start()
            c1.wait()
            c2.wait()

        hn2 = hn2_ref[...]
        ff = jnp.zeros((tm, D_MODEL), _F32)
        for c in range(2):
            gate = _dot(hn2, wgu[half * c:half * (c + 1), :], _NT)
            up = _dot(hn2, wgu[D_FF + half * c:D_FF + half * (c + 1), :], _NT)
            gu[:, half * c:half * (c + 1)] = gate
            gu[:, D_FF + half * c:D_FF + half * (c + 1)] = up
            act = gate * jax.nn.sigmoid(gate) * up
            act_ref[half * c:half * (c + 1), :] = act.T.astype(_BF16)
            ff = ff + _dot(act, wd[half * c:half * (c + 1), :], _NN)
        g3 = g3_ref[...]
        n, r = _rms_fwd(ff, g3)
        h1 = h1_ref[...]
        err = h1 + n - tgt_ref[...]
        loss = 0.5 * jnp.sum(jnp.mean(err * err, axis=-1, keepdims=True), axis=0, keepdims=True)
        _accumulate(loss_ref, jnp.broadcast_to(loss, (1, 128)), first)
        dh2 = err * (1.0 / D_MODEL)
        dff, dg3 = _rms_bwd(dh2, ff, g3, r)
        _accumulate(dg3_ref, dg3, first)
        dffb = dff.astype(_BF16)
        dff_ref[...] = dffb
        dhn2 = jnp.zeros((tm, D_MODEL), _F32)
        for c in range(2):
            dact = _dot(dffb, wd[half * c:half * (c + 1), :], _NT)
            gate = gu[:, half * c:half * (c + 1)]
            up = gu[:, D_FF + half * c:D_FF + half * (c + 1)]
            sig = jax.nn.sigmoid(gate)
            silu = gate * sig
            dgate = dact * up * (sig + silu * (1.0 - sig))
            dup = dact * silu
            dgu_ref[half * c:half * (c + 1), :] = dgate.T.astype(_BF16)
            dgu_ref[D_FF + half * c:D_FF + half * (c + 1), :] = dup.T.astype(_BF16)
            dhn2 = dhn2 + _dot(dgate, wgu[half * c:half * (c + 1), :], _NN)
            dhn2 = dhn2 + _dot(dup, wgu[D_FF + half * c:D_FF + half * (c + 1), :], _NN)
        g2 = g2_ref[...]
        _, r2 = _rms_fwd(h1, g2)
        dh1, dg2 = _rms_bwd(dhn2, h1, g2, r2)
        _accumulate(dg2_ref, dg2, first)
        dh1_ref[...] = dh2 + dh1

    row = _whole((1, D_MODEL))
    anyspace = pl.BlockSpec(memory_space=pl.ANY)
    return _call(body, (L // tm,),
                 [_rows(tm, D_MODEL), _rows(tm, D_MODEL), _rows(tm, D_MODEL), anyspace, anyspace, row, row],
                 [pl.BlockSpec((D_FF, tm), lambda i: (0, i)), pl.BlockSpec((2 * D_FF, tm), lambda i: (0, i)),
                  _rows(tm, D_MODEL), _rows(tm, D_MODEL), _whole((1, 128)), row, row],
                 [_sds((D_FF, L), _BF16), _sds((2 * D_FF, L), _BF16), _sds((L, D_MODEL), _BF16),
                  _sds((L, D_MODEL), _F32), _sds((1, 128), _F32), _sds((1, D_MODEL), _F32), _sds((1, D_MODEL), _F32)],
                 "ffn",
                 scratch=[pltpu.VMEM((2 * D_FF, D_MODEL), _BF16), pltpu.VMEM((D_FF, D_MODEL), _BF16),
                          pltpu.VMEM((tm, 2 * D_FF), _F32), pltpu.SemaphoreType.DMA((2,))],
                 )(hn2, h1, target, w_gate_up, w_down, g_pre_ffn, g_post_ffn)


def _out_proj_bwd(dh1, mo, w_out, g_post_mix, tokens=()):
    L = dh1.shape[0]
    tm = _chunk_tile(L)

    def body(dh1_ref, mo_ref, w_ref, g_ref, dmo_ref, dns_ref, dna_ref, dg_ref):
        first = pl.program_id(0) == 0
        mo = mo_ref[...]
        g = g_ref[...]
        _, r = _rms_fwd(mo, g)
        dmo, dg = _rms_bwd(dh1_ref[...], mo, g, r)
        _accumulate(dg_ref, dg, first)
        dmob = dmo.astype(_BF16)
        dmo_ref[...] = dmob
        dmerged = _dot(dmob, w_ref[...], _NT)
        dns_ref[...] = dmerged[:, :SSM_WIDTH]
        dna_ref[...] = dmerged[:, SSM_WIDTH:]

    row = _whole((1, D_MODEL))
    return _call(body, (L // tm,),
                 [_rows(tm, D_MODEL), _rows(tm, D_MODEL), _whole((D_MODEL, D_MODEL)), row],
                 [_rows(tm, D_MODEL), _chunk_block(L, SSM_WIDTH), _rows(tm, ATTN_WIDTH), row],
                 [_sds((L, D_MODEL), _BF16), _sds(_chunk_shape(L, SSM_WIDTH), _F32), _sds((L, ATTN_WIDTH), _F32),
                  _sds((1, D_MODEL), _F32)],
                 "out_proj_bwd", tokens=tokens)(dh1, mo, w_out, g_post_mix)


def _in_proj_bwd(du, dq, dk, dv, cos_t, sin_t, x, dh1, g_pre_mix, w_in, tokens=()):
    L = x.shape[0]
    tm = _chunk_tile(L)

    def body(du_ref, dq_ref, dk_ref, dv_ref, cos_ref, sin_ref, x_ref, dh1_ref, g_ref, w_ref,
             dproj_ref, dx_ref, dg_ref):
        first = pl.program_id(0) == 0
        cos_v, sin_v = cos_ref[...], sin_ref[...]
        dproj = jnp.concatenate([du_ref[...], _rope_transpose(dq_ref[...], cos_v, sin_v),
                                 _rope_transpose(dk_ref[...], cos_v, sin_v), dv_ref[...]], axis=1).astype(_BF16)
        dproj_ref[...] = dproj
        dhn = _dot(dproj, w_ref[...], _NN)
        x = x_ref[...]
        g = g_ref[...]
        _, r = _rms_fwd(x, g)
        dx, dg = _rms_bwd(dhn, x, g, r)
        _accumulate(dg_ref, dg, first)
        dx_ref[...] = dh1_ref[...] + dx

    row = _whole((1, D_MODEL))
    return _call(body, (L // tm,),
                 [_chunk_block(L, SSM_WIDTH), _rows(tm, ATTN_WIDTH), _rows(tm, KV_WIDTH), _rows(tm, KV_WIDTH),
                  _rows(tm, KV_WIDTH), _rows(tm, KV_WIDTH), _rows(tm, D_MODEL), _rows(tm, D_MODEL), row,
                  _whole((IN_WIDTH, D_MODEL))],
                 [_rows(tm, IN_WIDTH), _rows(tm, D_MODEL), row],
                 [_sds((L, IN_WIDTH), _BF16), _sds((L, D_MODEL), _F32), _sds((1, D_MODEL), _F32)],
                 "in_proj_bwd", tokens=tokens)(du, dq, dk, dv, cos_t, sin_t, x, dh1, g_pre_mix, w_in)


def _matmul_nn(a, b, out_dtype, name):
    M, K = a.shape
    N = b.shape[1]
    tm = next(t for t in (512, 256, 128) if M % t == 0)
    tn = N if N <= D_MODEL else next(t for t in (512, 256, 128) if N % t == 0)

    def body(a_ref, b_ref, o_ref):
        o_ref[...] = _dot(a_ref[...], b_ref[...], _NN).astype(out_dtype)

    params = pltpu.CompilerParams(dimension_semantics=("arbitrary", "arbitrary"), vmem_limit_bytes=VMEM_LIMIT)
    return pl.pallas_call(body, grid=(M // tm, N // tn),
                          in_specs=[pl.BlockSpec((tm, K), lambda i, j: (i, 0)),
                                    pl.BlockSpec((K, tn), lambda i, j: (0, j))],
                          out_specs=pl.BlockSpec((tm, tn), lambda i, j: (i, j)),
                          out_shape=_sds((M, N), out_dtype), compiler_params=params, name=name)(a, b)


def _matmul_tn(a, b, out_dtype, name, scale=1.0):
    K, M = a.shape
    N = b.shape[1]
    tm = next(t for t in (512, 256, 128) if M % t == 0)
    tn = N if N <= D_MODEL else next(t for t in (512, 256, 128) if N % t == 0)

    def body(a_ref, b_ref, o_ref):
        acc = _dot(a_ref[...], b_ref[...], _TN)
        o_ref[...] = (acc if scale == 1.0 else acc * scale).astype(out_dtype)

    params = pltpu.CompilerParams(dimension_semantics=("arbitrary", "arbitrary"), vmem_limit_bytes=VMEM_LIMIT)
    return pl.pallas_call(body, grid=(M // tm, N // tn),
                          in_specs=[pl.BlockSpec((K, tm), lambda i, j: (0, i)),
                                    pl.BlockSpec((K, tn), lambda i, j: (0, j))],
                          out_specs=pl.BlockSpec((tm, tn), lambda i, j: (i, j)),
                          out_shape=_sds((M, N), out_dtype), compiler_params=params, name=name)(a, b)


def _local_step(x, pos, target, p, fetch, publish, progress):
    L = x.shape[0]
    T = L // SCAN_CHUNKS
    cos_t, sin_t = _rope_tables(pos.reshape(L, 1))
    w_in, = fetch(("w_in",), None)
    hn, u, q, k, v = _in_proj(x, p["g_pre_mix"], w_in, cos_t, sin_t)

    ssm = {n: _to_2d(n, p[n]) for n in ("ssm_lambda_re", "ssm_lambda_im", "ssm_log_dt", "ssm_b_re", "ssm_b_im",
                                        "ssm_c_re", "ssm_c_im")}
    d_row = p["ssm_d"].reshape(1, SSM_WIDTH)
    a_re, a_im, bt_re, bt_im, ct_re, ct_im = _ssm_prep(
        ssm["ssm_lambda_re"], ssm["ssm_lambda_im"], ssm["ssm_log_dt"], ssm["ssm_b_re"], ssm["ssm_b_im"],
        ssm["ssm_c_re"], ssm["ssm_c_im"])

    u_c = u.reshape(L, SSM_WIDTH)
    bu_re, bu_im = _ssm_bu(u_c, bt_re, bt_im)
    x_re, x_im = _scan_fwd(bu_re.reshape(T, SCAN_CHUNKS, N_STATE), bu_im.reshape(T, SCAN_CHUNKS, N_STATE), a_re, a_im)
    w_glu, = fetch(("w_glu",), x_re)
    y, z, n_ssm_c = _ssm_out(x_re.reshape(L, N_STATE), x_im.reshape(L, N_STATE), u_c, ct_re, ct_im, d_row,
                             w_glu, p["b_glu"], p["g_ssm_out"])
    n_ssm = n_ssm_c.reshape(_chunk_shape(L, SSM_WIDTH))

    sinks = p["attn_sinks"].reshape(N_Q_HEADS)
    o, n_attn = _attn_fwd(q, k, v, sinks, p["g_attn_out"])
    w_out, = fetch(("w_out",), n_attn)
    merged, mo, h1, hn2 = _out_proj(n_ssm, n_attn, x, w_out, p["g_post_mix"], p["g_pre_ffn"])
    w_gate_up, w_down = fetch(("w_gate_up", "w_down"), hn2)
    act_t, dgu_t, dff, dh1, loss, dg_post_ffn, dg_pre_ffn = _ffn(
        hn2, h1, target, w_gate_up, w_down, p["g_pre_ffn"], p["g_post_ffn"])
    grads = {"g_post_ffn": dg_post_ffn, "g_pre_ffn": dg_pre_ffn}
    tokens = publish({"w_down": _matmul_nn(act_t, dff, _BF16, "grad_w_down"),
                      "w_gate_up": _matmul_nn(dgu_t, hn2, _BF16, "grad_w_gate_up")})

    dmo, dn_ssm, dn_attn, grads["g_post_mix"] = _out_proj_bwd(dh1, mo, w_out, p["g_post_mix"], tokens)
    grad_w_out = _matmul_tn(merged, dmo, _BF16, "grad_w_out")

    dq, dk, dv, dsink, grads["g_attn_out"] = _attn_bwd(q, k, v, o, dn_attn, sinks, p["g_attn_out"])
    grads["attn_sinks"] = dsink

    gy, dz, dy, dud, dx_re, dx_im, grads["g_ssm_out"], grads["b_glu"], dd = _ssm_out_bwd(
        dn_ssm.reshape(L, SSM_WIDTH), y, z, u_c, ct_re, ct_im, d_row, w_glu, p["g_ssm_out"])
    grads.update(w_out=grad_w_out, w_glu=_matmul_tn(dz, gy, _BF16, "grad_w_glu"))
    tokens = [grads["w_out"], grads["w_glu"]] + progress(dy)
    lam_re, lam_im, da_re, da_im = _scan_bwd(dx_re.reshape(T, SCAN_CHUNKS, N_STATE), dx_im.reshape(T, SCAN_CHUNKS, N_STATE),
                                             x_re, x_im, a_re, a_im, tokens)
    lam_re = lam_re.reshape(L, N_STATE)
    lam_im = lam_im.reshape(L, N_STATE)
    dct_re, dct_im, dbt_re, dbt_im = _ssm_weight_grads(
        dy, x_re.reshape(L, N_STATE), x_im.reshape(L, N_STATE), lam_re, lam_im, u_c)
    ssm_pack = _ssm_param_bwd(
        da_re, da_im, dbt_re, dbt_im, dct_re, dct_im,
        ssm["ssm_lambda_re"], ssm["ssm_lambda_im"], ssm["ssm_log_dt"], ssm["ssm_b_re"], ssm["ssm_b_im"],
        dd.reshape(SSM_GROUPS, SSM_GROUP))
    grads.update(ssm_pack=ssm_pack, loss=loss)
    publish(grads)

    du = _ssm_du(lam_re, lam_im, bt_re, bt_im, dud).reshape(_chunk_shape(L, SSM_WIDTH))
    dproj, grad_x, g_pre_mix = _in_proj_bwd(du, dq, dk, dv, cos_t, sin_t, x, dh1, p["g_pre_mix"], w_in, [ssm_pack])
    publish({"g_pre_mix": g_pre_mix, "w_in": _matmul_tn(dproj, hn, _BF16, "grad_w_in")})
    return grad_x


_MESH = pl.DeviceIdType.MESH
_PEERS = N_DEV - 1


def _mesh_pos():
    return lax.axis_index("x"), lax.axis_index("y"), lax.axis_index("c")


def _dev_index(px, py, pc):
    return 4 * px + 2 * py + pc


def _all_gather(shards, out_dtype, name):
    n = len(shards)

    def body(*refs):
        ins, outs, stages = refs[:n], refs[n:2 * n], refs[2 * n:3 * n]
        send_sems, recv_sems, local_sems = refs[3 * n:]
        x, y, c = _mesh_pos()
        me, sibling = (x, y, c), (x, y, 1 - c)
        chips = [(1 - x, y), (x, 1 - y), (1 - x, 1 - y)]

        def copy(w, k, block, to, src=None):
            slot = outs[w].at[_dev_index(*block)]
            return pltpu.make_async_remote_copy(
                src_ref=slot if src is None else src, dst_ref=slot,
                send_sem=send_sems.at[_PEERS * w + k], recv_sem=recv_sems.at[_PEERS * w + k],
                device_id=to, device_id_type=_MESH)

        for w in range(n):
            stages[w][...] = ins[w][...].astype(out_dtype)
        mine, first, passed = [], [], []
        for w in range(n):
            cp = pltpu.make_async_copy(stages[w], outs[w].at[_dev_index(*me)], local_sems.at[w])
            cp.start()
            mine.append(cp)
            sends = [copy(w, 0, me, sibling, src=stages[w])]
            sends += [copy(w, 1 + j, me, (*chip, c), src=stages[w]) for j, chip in enumerate(chips)]
            for cp in sends:
                cp.start()
            first += sends
        for w in range(n):
            for j, chip in enumerate(chips):
                copy(w, 1 + j, (*chip, c), me).wait_recv()
                cp = copy(w, 4 + j, (*chip, c), sibling)
                cp.start()
                passed.append(cp)
        for w in range(n):
            copy(w, 0, sibling, me).wait_recv()
            for j, chip in enumerate(chips):
                copy(w, 4 + j, (*chip, 1 - c), me).wait_recv()
        for cp in first + passed:
            cp.wait_send()
        for cp in mine:
            cp.wait()

    return pl.pallas_call(
        body, name=name,
        out_shape=[_sds((N_DEV,) + s.shape, out_dtype) for s in shards],
        in_specs=[pl.BlockSpec(memory_space=pltpu.VMEM)] * n,
        out_specs=[pl.BlockSpec(memory_space=pl.ANY)] * n,
        scratch_shapes=[pltpu.VMEM(s.shape, out_dtype) for s in shards]
        + [pltpu.SemaphoreType.DMA((_PEERS * n,)), pltpu.SemaphoreType.DMA((_PEERS * n,)),
           pltpu.SemaphoreType.DMA((n,))],
        compiler_params=pltpu.CompilerParams(vmem_limit_bytes=VMEM_LIMIT),
    )(*shards)


_HBM_SPEC = pl.BlockSpec(memory_space=pltpu.HBM)
_SEM_SPEC = pl.BlockSpec(memory_space=pltpu.SEMAPHORE)
_DATAFLOW = pltpu.SideEffectType.DATAFLOW_SIDE_EFFECTING


def _peer(x, y, c, r):
    return (x ^ ((r >> 2) & 1), y ^ ((r >> 1) & 1), c ^ (r & 1))


def _hbm(a):
    return pltpu.with_memory_space_constraint(a, pltpu.HBM)


def _send_start(sources, blocked, name):
    n = len(sources)
    lands = [lax.empty((N_DEV,) + (s.shape[1:] if blocked else s.shape), s.dtype) for s in sources]

    def body(*refs):
        srcs, zones = refs[:n], refs[n:2 * n]
        send_sems, recv_sems = refs[2 * n:3 * n], refs[3 * n:4 * n]
        token, local_sems = refs[6 * n], refs[6 * n + 1]
        x, y, c = _mesh_pos()
        me = _dev_index(x, y, c)
        local = []
        for w in range(n):
            cp = pltpu.make_async_copy(srcs[w].at[me] if blocked else srcs[w], zones[w].at[me], local_sems.at[w])
            cp.start()
            local.append(cp)
            for r in range(1, N_DEV):
                peer = _peer(x, y, c, r)
                pltpu.make_async_remote_copy(
                    src_ref=srcs[w].at[_dev_index(*peer)] if blocked else srcs[w], dst_ref=zones[w].at[me],
                    send_sem=send_sems[w].at[r - 1], recv_sem=recv_sems[w].at[r - 1],
                    device_id=peer, device_id_type=_MESH).start()
        for cp in local:
            cp.wait()
        token[...] = jnp.zeros_like(token)

    sems = [pltpu.SemaphoreType.DMA((_PEERS,))] * (2 * n)
    out = pl.pallas_call(
        body, name=name,
        out_shape=sems + [pltpu.HBM(a.shape, a.dtype) for a in list(sources) + lands] + [_sds((8, 128), _F32)],
        in_specs=[_HBM_SPEC] * (2 * n),
        out_specs=[_SEM_SPEC] * (2 * n) + [_HBM_SPEC] * (2 * n) + [pl.BlockSpec(memory_space=pltpu.VMEM)],
        input_output_aliases={i: 2 * n + i for i in range(2 * n)},
        scratch_shapes=[pltpu.SemaphoreType.DMA((n,))],
        compiler_params=pltpu.CompilerParams(has_side_effects=_DATAFLOW),
    )(*[_hbm(a) for a in sources], *[_hbm(a) for a in lands])
    return out[:n], out[n:2 * n], out[2 * n:3 * n], out[3 * n:4 * n], out[4 * n]


def _send_wait(send_sems, recv_sems, sources, lands, after, blocked, name):
    n = len(sources)

    def body(*refs):
        srcs, zones = refs[:n], refs[n:2 * n]
        sends, recvs = refs[2 * n:3 * n], refs[3 * n:4 * n]
        x, y, c = _mesh_pos()
        for w in range(n):
            for r in range(1, N_DEV):
                peer = _peer(x, y, c, r)
                idx = _dev_index(*peer)
                cp = pltpu.make_async_remote_copy(
                    src_ref=srcs[w].at[idx] if blocked else srcs[w], dst_ref=zones[w].at[idx],
                    send_sem=sends[w].at[r - 1], recv_sem=recvs[w].at[r - 1],
                    device_id=peer, device_id_type=_MESH)
                cp.wait_send()
                cp.wait_recv()

    out = pl.pallas_call(
        body, name=name,
        out_shape=[pltpu.HBM(a.shape, a.dtype) for a in list(sources) + list(lands)],
        in_specs=[_HBM_SPEC] * (2 * n) + [_SEM_SPEC] * (2 * n) + [pl.BlockSpec(memory_space=pl.ANY)],
        out_specs=[_HBM_SPEC] * (2 * n),
        input_output_aliases={i: i for i in range(2 * n)},
        compiler_params=pltpu.CompilerParams(has_side_effects=_DATAFLOW),
    )(*sources, *lands, *send_sems, *recv_sems, after)
    return out[n:]


def _sequencer_exchange(sources, blocked, name, collective_id):
    n = len(sources)
    flags = blocked

    def body(*refs):
        srcs, zones = refs[:n], refs[n:2 * n]
        send_sems, recv_sems, local_sems = refs[2 * n:]
        x, y, c = _mesh_pos()
        me = _dev_index(x, y, c)
        barrier = pltpu.get_barrier_semaphore()
        for r in range(1, N_DEV):
            pl.semaphore_signal(barrier, inc=1, device_id=_peer(x, y, c, r), device_id_type=_MESH)
        pl.semaphore_wait(barrier, _PEERS)
        local, sends, recvs = [], [], []
        for w in range(n):
            cp = pltpu.make_async_copy(srcs[w].at[me] if flags[w] else srcs[w], zones[w].at[me], local_sems.at[w])
            cp.start()
            local.append(cp)
            for r in range(1, N_DEV):
                peer = _peer(x, y, c, r)
                idx = _dev_index(*peer)
                k = _PEERS * w + r - 1
                src = srcs[w].at[idx] if flags[w] else srcs[w]
                send = pltpu.make_async_remote_copy(
                    src_ref=src, dst_ref=zones[w].at[me], send_sem=send_sems.at[k], recv_sem=recv_sems.at[k],
                    device_id=peer, device_id_type=_MESH)
                send.start()
                sends.append(send)
                recvs.append(pltpu.make_async_remote_copy(
                    src_ref=src, dst_ref=zones[w].at[idx], send_sem=send_sems.at[k], recv_sem=recv_sems.at[k],
                    device_id=peer, device_id_type=_MESH))
        for cp in recvs:
            cp.wait_recv()
        for cp in sends:
            cp.wait_send()
        for cp in local:
            cp.wait()

    return pl.kernel(
        body, name=name,
        out_type=[_sds((N_DEV,) + (s.shape[1:] if f else s.shape), s.dtype) for s, f in zip(sources, flags)],
        mesh=plsc.ScalarSubcoreMesh(axis_name="sequencer", num_cores=1),
        scratch_types=[pltpu.SemaphoreType.DMA((_PEERS * n,)), pltpu.SemaphoreType.DMA((_PEERS * n,)),
                       pltpu.SemaphoreType.DMA((n,))],
        compiler_params=pltpu.CompilerParams(collective_id=collective_id),
    )(*sources)


def _sequencer_gather(shards, name, collective_id):
    n = len(shards)
    fan = 4

    def body(*refs):
        srcs, zones = refs[:n], refs[n:2 * n]
        send_sems, recv_sems, local_sems = refs[2 * n:]
        x, y, c = _mesh_pos()
        me, sibling = (x, y, c), (x, y, 1 - c)
        chips = [(1 - x, y), (x, 1 - y), (1 - x, 1 - y)]
        barrier = pltpu.get_barrier_semaphore()
        for peer in [sibling] + [(*chip, c) for chip in chips]:
            pl.semaphore_signal(barrier, inc=1, device_id=peer, device_id_type=_MESH)
        pl.semaphore_wait(barrier, fan)

        def copy(w, k, block, to, src=None):
            slot = zones[w].at[_dev_index(*block)]
            return pltpu.make_async_remote_copy(
                src_ref=slot if src is None else src, dst_ref=slot,
                send_sem=send_sems.at[_PEERS * w + k], recv_sem=recv_sems.at[_PEERS * w + k],
                device_id=to, device_id_type=_MESH)

        mine, first, passed = [], [], []
        for w in range(n):
            cp = pltpu.make_async_copy(srcs[w], zones[w].at[_dev_index(*me)], local_sems.at[w])
            cp.start()
            mine.append(cp)
            sends = [copy(w, 0, me, sibling, src=srcs[w])]
            sends += [copy(w, 1 + j, me, (*chip, c), src=srcs[w]) for j, chip in enumerate(chips)]
            for cp in sends:
                cp.start()
            first += sends
        for w in range(n):
            for j, chip in enumerate(chips):
                copy(w, 1 + j, (*chip, c), me).wait_recv()
                cp = copy(w, fan + j, (*chip, c), sibling)
                cp.start()
                passed.append(cp)
        for w in range(n):
            copy(w, 0, sibling, me).wait_recv()
            for j, chip in enumerate(chips):
                copy(w, fan + j, (*chip, 1 - c), me).wait_recv()
        for cp in first + passed:
            cp.wait_send()
        for cp in mine:
            cp.wait()

    return pl.kernel(
        body, name=name, out_type=[_sds((N_DEV,) + s.shape, s.dtype) for s in shards],
        mesh=plsc.ScalarSubcoreMesh(axis_name="sequencer", num_cores=1),
        scratch_types=[pltpu.SemaphoreType.DMA((_PEERS * n,)), pltpu.SemaphoreType.DMA((_PEERS * n,)),
                       pltpu.SemaphoreType.DMA((n,))],
        compiler_params=pltpu.CompilerParams(collective_id=collective_id),
    )(*shards)


N_CHIPS = N_DEV // 2


def _sequencer_pair_exchange(sources, name, collective_id):
    n = len(sources)

    def body(*refs):
        srcs, zones = refs[:n], refs[n:2 * n]
        send_sems, recv_sems = refs[2 * n:]
        x, y, c = _mesh_pos()
        sibling = (x, y, 1 - c)
        barrier = pltpu.get_barrier_semaphore()
        pl.semaphore_signal(barrier, inc=1, device_id=sibling, device_id_type=_MESH)
        pl.semaphore_wait(barrier, 1)
        copies = []
        for w in range(n):
            for j in range(N_CHIPS):
                k = N_CHIPS * w + j
                cp = pltpu.make_async_remote_copy(
                    src_ref=srcs[w].at[2 * j + 1 - c], dst_ref=zones[w].at[j],
                    send_sem=send_sems.at[k], recv_sem=recv_sems.at[k], device_id=sibling, device_id_type=_MESH)
                cp.start()
                copies.append(cp)
        for cp in copies:
            cp.wait_recv()
        for cp in copies:
            cp.wait_send()

    return pl.kernel(
        body, name=name, out_type=[_sds((N_CHIPS,) + s.shape[1:], s.dtype) for s in sources],
        mesh=plsc.ScalarSubcoreMesh(axis_name="sequencer", num_cores=1),
        scratch_types=[pltpu.SemaphoreType.DMA((N_CHIPS * n,)), pltpu.SemaphoreType.DMA((N_CHIPS * n,))],
        compiler_params=pltpu.CompilerParams(collective_id=collective_id),
    )(*sources)


def _pair_sum(source, received, core, name, tokens=()):
    _, rows, cols = source.shape
    tr = _row_tile(rows)

    def body(core_ref, s_ref, r_ref, o_ref):
        c = core_ref[0]
        for j in range(N_CHIPS):
            o_ref[j] = (s_ref[2 * j + c].astype(_F32) + r_ref[j].astype(_F32)).astype(o_ref.dtype)

    return _call(body, (rows // tr,),
                 [pl.BlockSpec(memory_space=pltpu.SMEM), pl.BlockSpec((N_DEV, tr, cols), lambda i: (0, i, 0)),
                  pl.BlockSpec((N_CHIPS, tr, cols), lambda i: (0, i, 0))],
                 pl.BlockSpec((N_CHIPS, tr, cols), lambda i: (0, i, 0)),
                 _sds((N_CHIPS, rows, cols), source.dtype), name, tokens=tokens)(core, source, received)


def _sequencer_chip_exchange(partials, name, collective_id):
    n = len(partials)
    others = N_CHIPS - 1

    def body(*refs):
        srcs, zones = refs[:n], refs[n:2 * n]
        send_sems, recv_sems, local_sems = refs[2 * n:]
        x, y, c = _mesh_pos()
        mine = 2 * x + y
        peers = [(x ^ (r >> 1), y ^ (r & 1), c) for r in range(1, N_CHIPS)]
        barrier = pltpu.get_barrier_semaphore()
        for peer in peers:
            pl.semaphore_signal(barrier, inc=1, device_id=peer, device_id_type=_MESH)
        pl.semaphore_wait(barrier, others)
        local, sends, recvs = [], [], []
        for w in range(n):
            cp = pltpu.make_async_copy(srcs[w].at[mine], zones[w].at[mine], local_sems.at[w])
            cp.start()
            local.append(cp)
            for r, peer in enumerate(peers):
                theirs = 2 * peer[0] + peer[1]
                k = others * w + r
                send = pltpu.make_async_remote_copy(
                    src_ref=srcs[w].at[theirs], dst_ref=zones[w].at[mine],
                    send_sem=send_sems.at[k], recv_sem=recv_sems.at[k], device_id=peer, device_id_type=_MESH)
                send.start()
                sends.append(send)
                recvs.append(pltpu.make_async_remote_copy(
                    src_ref=srcs[w].at[theirs], dst_ref=zones[w].at[theirs],
                    send_sem=send_sems.at[k], recv_sem=recv_sems.at[k], device_id=peer, device_id_type=_MESH))
        for cp in recvs:
            cp.wait_recv()
        for cp in sends:
            cp.wait_send()
        for cp in local:
            cp.wait()

    return pl.kernel(
        body, name=name, out_type=[_sds(s.shape, s.dtype) for s in partials],
        mesh=plsc.ScalarSubcoreMesh(axis_name="sequencer", num_cores=1),
        scratch_types=[pltpu.SemaphoreType.DMA((others * n,)), pltpu.SemaphoreType.DMA((others * n,)),
                       pltpu.SemaphoreType.DMA((n,))],
        compiler_params=pltpu.CompilerParams(collective_id=collective_id),
    )(*partials)


def _row_tile(rows):
    return next(t for t in range(min(rows, 256), 0, -16) if rows % t == 0)


def _sum_parts(parts, name, tokens=()):
    _, rows, cols = parts.shape
    tr = _row_tile(rows)

    def body(p_ref, g_ref):
        g = p_ref[0].astype(_F32)
        for s in range(1, N_DEV):
            g = g + p_ref[s].astype(_F32)
        g_ref[...] = g

    return _call(body, (rows // tr,), [pl.BlockSpec((N_DEV, tr, cols), lambda i: (0, i, 0))],
                 _rows(tr, cols), _sds((rows, cols), _F32), name, tokens=tokens)(parts)


def _adam_update(g, w, m, v):
    new_m = ADAM_B1 * m + (1.0 - ADAM_B1) * g
    new_v = ADAM_B2 * v + (1.0 - ADAM_B2) * (g * g)
    m_hat = new_m / (1.0 - ADAM_B1 ** ADAM_STEP)
    v_hat = new_v / (1.0 - ADAM_B2 ** ADAM_STEP)
    return -ADAM_LR * (m_hat / (jnp.sqrt(v_hat) + ADAM_EPS) + ADAM_WD * w), new_m, new_v


def _adamw_small(parts, items, sums, name, tokens=()):
    n_p, n_i = len(parts), len(items)

    def body(*refs):
        p_refs, state, outs = refs[:n_p], refs[n_p:n_p + 3 * n_i], refs[n_p + 3 * n_i:]

        def total(part, rows, cols):
            shift = cols.start % _LANES
            window = slice(cols.start - shift, cols.start - shift + _LANES) if shift else cols
            n_rows = rows.stop - rows.start
            narrow = p_refs[part].dtype.itemsize < 4 and n_rows % _PACK_TILE
            tile = slice(rows.start, rows.start + _PACK_TILE) if narrow else rows
            g = p_refs[part][0, tile, window].astype(_F32)
            for s in range(1, N_DEV):
                g = g + p_refs[part][s, tile, window].astype(_F32)
            g = g[:n_rows] if narrow else g
            return pltpu.roll(g, _LANES - shift, 1)[:, :cols.stop - cols.start] if shift else g

        for i, (part, rows, cols, _, _, _) in enumerate(items):
            g = total(part, rows, cols)
            w_ref, m_ref, v_ref = state[3 * i:3 * i + 3]
            delta, new_m, new_v = _adam_update(g, w_ref[...], m_ref[...], v_ref[...])
            outs[4 * i][...] = g
            outs[4 * i + 1][...] = delta
            outs[4 * i + 2][...] = new_m
            outs[4 * i + 3][...] = new_v
        for j, (part, rows, cols) in enumerate(sums):
            outs[4 * n_i + j][...] = total(part, rows, cols)

    ins = list(parts) + [a for item in items for a in item[3:]]
    out_shapes = [item[3].shape for item in items for _ in range(4)]
    out_shapes += [(rows.stop - rows.start, cols.stop - cols.start) for _, rows, cols in sums]
    out = _call(body, (1,), [_whole(a.shape) for a in ins], [_whole(s) for s in out_shapes],
                [_sds(s, _F32) for s in out_shapes], name, tokens=tokens)(*ins)
    return [out[4 * i:4 * i + 4] for i in range(n_i)], out[4 * n_i:]


def _adamw(parts, w, m, v, name, tokens=()):
    rows, cols = w.shape
    tr = _row_tile(rows)
    n_parts = parts.shape[0]

    def body(p_ref, w_ref, m_ref, v_ref, g_ref, d_ref, nm_ref, nv_ref):
        g = p_ref[0].astype(_F32)
        for s in range(1, n_parts):
            g = g + p_ref[s].astype(_F32)
        new_m = ADAM_B1 * m_ref[...] + (1.0 - ADAM_B1) * g
        new_v = ADAM_B2 * v_ref[...] + (1.0 - ADAM_B2) * (g * g)
        m_hat = new_m / (1.0 - ADAM_B1 ** ADAM_STEP)
        v_hat = new_v / (1.0 - ADAM_B2 ** ADAM_STEP)
        g_ref[...] = g
        d_ref[...] = -ADAM_LR * (m_hat / (jnp.sqrt(v_hat) + ADAM_EPS) + ADAM_WD * w_ref[...])
        nm_ref[...] = new_m
        nv_ref[...] = new_v

    blk = _rows(tr, cols)
    return _call(body, (rows // tr,),
                 [pl.BlockSpec((n_parts, tr, cols), lambda i: (0, i, 0)), blk, blk, blk],
                 [blk] * 4, [_sds((rows, cols), _F32)] * 4, name, tokens=tokens)(parts, w, m, v)


_SMALL = ("g_pre_mix", "ssm_lambda_re", "ssm_lambda_im", "ssm_log_dt", "ssm_b_re", "ssm_b_im",
          "ssm_c_re", "ssm_c_im", "ssm_d", "b_glu", "attn_sinks", "g_ssm_out", "g_attn_out",
          "g_post_mix", "g_pre_ffn", "g_post_ffn")
_BIG = ("w_in", "w_glu", "w_out", "w_gate_up", "w_down")
_WEIGHTS = ("g_pre_mix", "w_in", "ssm_lambda_re", "ssm_lambda_im", "ssm_log_dt", "ssm_b_re", "ssm_b_im",
            "ssm_c_re", "ssm_c_im", "ssm_d", "w_glu", "b_glu", "attn_sinks", "g_ssm_out", "g_attn_out",
            "w_out", "g_post_mix", "g_pre_ffn", "w_gate_up", "w_down", "g_post_ffn")
_LANES = 128


_SHAPE_2D = {
    "g_pre_mix": (1, D_MODEL), "ssm_lambda_re": (SSM_GROUPS, SSM_STATE), "ssm_lambda_im": (SSM_GROUPS, SSM_STATE),
    "ssm_log_dt": (1, SSM_GROUPS), "ssm_b_re": (SSM_WIDTH, SSM_STATE), "ssm_b_im": (SSM_WIDTH, SSM_STATE),
    "ssm_c_re": (SSM_WIDTH, SSM_STATE), "ssm_c_im": (SSM_WIDTH, SSM_STATE), "ssm_d": (SSM_GROUPS, SSM_GROUP),
    "b_glu": (1, 2 * SSM_WIDTH), "attn_sinks": (1, N_Q_HEADS), "g_ssm_out": (1, SSM_WIDTH),
    "g_attn_out": (1, ATTN_WIDTH), "g_post_mix": (1, D_MODEL), "g_pre_ffn": (1, D_MODEL), "g_post_ffn": (1, D_MODEL)}
_ROW_WIDTH = {"g_pre_mix": D_MODEL, "b_glu": 2 * SSM_WIDTH, "attn_sinks": _LANES, "g_ssm_out": SSM_WIDTH,
              "g_attn_out": ATTN_WIDTH, "g_post_mix": D_MODEL, "g_pre_ffn": D_MODEL, "g_post_ffn": D_MODEL,
              "loss": _LANES}
_DENSE = ()
_PER_GROUP_TRANSPOSED = ("ssm_b_re", "ssm_b_im")


def _to_2d(name, a):
    if name in _PER_GROUP_TRANSPOSED:
        a = a.reshape(SSM_GROUPS, SSM_STATE, SSM_GROUP).transpose(0, 2, 1)
    return a.reshape(_SHAPE_2D[name])


def _from_2d(name, a, shape):
    if name in _PER_GROUP_TRANSPOSED:
        a = a.reshape(SSM_GROUPS, SSM_GROUP, SSM_STATE).transpose(0, 2, 1)
    return a.reshape(shape)


def _row_slots(names):
    slots, row, col = {}, 0, 0
    for n in names:
        width = _ROW_WIDTH[n]
        if col + width > D_MODEL:
            row, col = row + 1, 0
        slots[n] = (row, col, width)
        col += width
    return slots


def _stack_rows(named, slots):
    n_rows = -(-(max(r for r, _, _ in slots.values()) + 1) // 8) * 8
    lines = []
    for r in range(n_rows):
        pieces = [named[n] for n, (row, _, _) in slots.items() if row == r]
        used = sum(p.shape[1] for p in pieces)
        if used < D_MODEL:
            pieces.append(jnp.zeros((1, D_MODEL - used), _F32))
        lines.append(jnp.concatenate(pieces, axis=1) if len(pieces) > 1 else pieces[0])
    return jnp.concatenate(lines, axis=0)


def kernel(x, positions, g_pre_mix, w_in, ssm_lambda_re, ssm_lambda_im, ssm_log_dt, ssm_b_re, ssm_b_im, ssm_c_re, ssm_c_im, ssm_d, w_glu, b_glu, attn_sinks, g_ssm_out, g_attn_out, w_out, g_post_mix, g_pre_ffn, w_gate_up, w_down, g_post_ffn, loss_target, m_g_pre_mix, m_w_in, m_ssm_lambda_re, m_ssm_lambda_im, m_ssm_log_dt, m_ssm_b_re, m_ssm_b_im, m_ssm_c_re, m_ssm_c_im, m_ssm_d, m_w_glu, m_b_glu, m_attn_sinks, m_g_ssm_out, m_g_attn_out, m_w_out, m_g_post_mix, m_g_pre_ffn, m_w_gate_up, m_w_down, m_g_post_ffn, v_g_pre_mix, v_w_in, v_ssm_lambda_re, v_ssm_lambda_im, v_ssm_log_dt, v_ssm_b_re, v_ssm_b_im, v_ssm_c_re, v_ssm_c_im, v_ssm_d, v_w_glu, v_b_glu, v_attn_sinks, v_g_ssm_out, v_g_attn_out, v_w_out, v_g_post_mix, v_g_pre_ffn, v_w_gate_up, v_w_down, v_g_post_ffn):
    w = dict(g_pre_mix=g_pre_mix, w_in=w_in, ssm_lambda_re=ssm_lambda_re, ssm_lambda_im=ssm_lambda_im,
             ssm_log_dt=ssm_log_dt, ssm_b_re=ssm_b_re, ssm_b_im=ssm_b_im, ssm_c_re=ssm_c_re, ssm_c_im=ssm_c_im,
             ssm_d=ssm_d, w_glu=w_glu, b_glu=b_glu, attn_sinks=attn_sinks, g_ssm_out=g_ssm_out,
             g_attn_out=g_attn_out, w_out=w_out, g_post_mix=g_post_mix, g_pre_ffn=g_pre_ffn,
             w_gate_up=w_gate_up, w_down=w_down, g_post_ffn=g_post_ffn)
    m = dict(g_pre_mix=m_g_pre_mix, w_in=m_w_in, ssm_lambda_re=m_ssm_lambda_re, ssm_lambda_im=m_ssm_lambda_im,
             ssm_log_dt=m_ssm_log_dt, ssm_b_re=m_ssm_b_re, ssm_b_im=m_ssm_b_im, ssm_c_re=m_ssm_c_re,
             ssm_c_im=m_ssm_c_im, ssm_d=m_ssm_d, w_glu=m_w_glu, b_glu=m_b_glu, attn_sinks=m_attn_sinks,
             g_ssm_out=m_g_ssm_out, g_attn_out=m_g_attn_out, w_out=m_w_out, g_post_mix=m_g_post_mix,
             g_pre_ffn=m_g_pre_ffn, w_gate_up=m_w_gate_up, w_down=m_w_down, g_post_ffn=m_g_post_ffn)
    v = dict(g_pre_mix=v_g_pre_mix, w_in=v_w_in, ssm_lambda_re=v_ssm_lambda_re, ssm_lambda_im=v_ssm_lambda_im,
             ssm_log_dt=v_ssm_log_dt, ssm_b_re=v_ssm_b_re, ssm_b_im=v_ssm_b_im, ssm_c_re=v_ssm_c_re,
             ssm_c_im=v_ssm_c_im, ssm_d=v_ssm_d, w_glu=v_w_glu, b_glu=v_b_glu, attn_sinks=v_attn_sinks,
             g_ssm_out=v_g_ssm_out, g_attn_out=v_g_attn_out, w_out=v_w_out, g_post_mix=v_g_post_mix,
             g_pre_ffn=v_g_pre_ffn, w_gate_up=v_w_gate_up, w_down=v_w_down, g_post_ffn=v_g_post_ffn)

    transposed = ("w_in", "w_glu", "w_gate_up")
    native_transposed = ("w_in", "w_gate_up")
    shard = {n: (w[n][0].T if n in transposed else w[n][0]).astype(_BF16) for n in _BIG}
    gathered = {}
    for names, lands in (
            (("w_in",), _sequencer_exchange([shard["w_in"]], [False], "gather_w_in", 1)),
            (("w_glu", "w_out"), _sequencer_exchange([shard["w_glu"], shard["w_out"]], [False] * 2, "gather_mix", 2)),
            (("w_gate_up", "w_down"), _sequencer_gather([shard["w_gate_up"], shard["w_down"]], "gather_ffn", 3))):
        gathered.update({n: a.reshape(-1, a.shape[2]) for n, a in zip(names, lands)})

    def fetch(names, after):
        del after
        return [gathered[n] for n in names]

    sent = []
    ids = iter(range(4, 16))
    two_step = {}

    def publish(named):
        big = [n for n in named if n in _BIG]
        if set(big) == {"w_gate_up", "w_down"}:
            blocks = [named[n].reshape(N_DEV, -1, named[n].shape[1]) for n in big]
            two_step.update(names=big, blocks=blocks,
                            received=_sequencer_pair_exchange(blocks, "grads_pair", next(ids)))
            return [named[n] for n in big]
        rows = [n for n in named if n in _ROW_WIDTH]
        dense = [n for n in named if n in _DENSE]
        plain = [n for n in named if n not in big + rows + dense]
        sources = [named[n].reshape(N_DEV, -1, named[n].shape[1]) for n in big]
        slots = _row_slots(rows)
        if rows:
            sources.append(_stack_rows(named, slots))
        sources += [named[n].reshape(-1, _LANES) for n in dense] + [named[n] for n in plain]
        flags = [True] * len(big) + [False] * (len(sources) - len(big))
        cid = next(ids)
        sent.append((big, slots, dense, plain, _sequencer_exchange(sources, flags, "grads_%d" % cid, cid)))
        return [named[n] for n in big]

    def progress(after):
        core = lax.axis_index("c").astype(jnp.int32).reshape(1)
        partials = [_pair_sum(b, r, core, "pair_sum_" + n, [after])
                    for n, b, r in zip(two_step["names"], two_step["blocks"], two_step["received"])]
        sent.append((two_step["names"], {}, [], [], _sequencer_chip_exchange(partials, "grads_chips", next(ids))))
        return partials

    p = {n: w[n] for n in _SMALL}
    grad_x = _local_step(x[0], positions[0], loss_target[0], p, fetch, publish, progress)

    state = {n: [_to_2d(n, a) for a in (w[n], m[n], v[n])] for n in _SMALL}
    result = {}
    total_loss = None
    chain = []
    for big, slots, dense, plain, lands in sent:
        lands = list(lands)
        after = list(chain)
        for name in big:
            part = lands.pop(0)
            if name in native_transposed:
                updated = _adamw(part, w[name][0].T, m[name][0].T, v[name][0].T, "adamw_" + name, after)
                result[name] = [a.T[None] for a in updated]
                chain.append(updated[3])
                continue
            if name in transposed:
                part = _sum_parts(part, "sum_" + name, after).T[None]
            updated = _adamw(part, w[name][0], m[name][0], v[name][0], "adamw_" + name, after)
            result[name] = [a[None] for a in updated]
            chain.append(updated[3])
        parts, items, sums, names = [], [], [], []
        if slots:
            parts.append(lands.pop(0))
            for name, (row, col, _) in slots.items():
                if name == "loss":
                    sums.append((0, slice(row, row + 1), slice(col, col + _LANES)))
                else:
                    items.append((0, slice(row, row + 1), slice(col, col + _SHAPE_2D[name][1]), *state[name]))
                    names.append(name)
        for name in dense:
            part = lands.pop(0).reshape((N_DEV,) + _SHAPE_2D[name])
            result[name] = _adamw(part, *state[name], "adamw_" + name, after)
            chain.append(result[name][3])
        for name in plain:
            packed = _SSM_PACK if name == "ssm_pack" else {name: (0, _SHAPE_2D[name][0], 0, _SHAPE_2D[name][1])}
            for member, (first, rows_n, lane, cols_n) in packed.items():
                items.append((len(parts), slice(first, first + rows_n), slice(lane, lane + cols_n), *state[member]))
                names.append(member)
            parts.append(lands.pop(0))
        if items:
            updated, summed = _adamw_small(parts, items, sums, "adamw_small_" + names[0], after)
            chain.append(updated[0][3])
            result.update(dict(zip(names, updated)))
            if summed:
                total_loss = summed[0][0, 0]

    out = [total_loss, grad_x[None]]
    for kind in range(4):
        out += [_from_2d(n, result[n][kind], w[n].shape) for n in _WEIGHTS]
    return tuple(out)
```

```python
import functools
import math

import numpy as np
import jax
import jax.numpy as jnp
from jax import lax
from jax.experimental import pallas as pl
from jax.experimental.pallas import tpu as pltpu
from jax.experimental.pallas import tpu_sc as plsc

D_MODEL = 1024
SSM_WIDTH = 512
SSM_GROUP = 16
SSM_GROUPS = 32
SSM_STATE = 64
N_STATE = SSM_GROUPS * SSM_STATE
ATTN_WIDTH = 512
HEAD_DIM = 64
N_Q_HEADS = 8
N_KV_HEADS = 2
Q_PER_KV = 4
KV_WIDTH = 128
IN_WIDTH = 1280
BLOCK = 128
ROPE_DIM = 16
ROPE_THETA = 500000.0
D_FF = 2816
NORM_EPS = 1e-6
MASK_VALUE = -1e30
ADAM_LR = 0.001
ADAM_B1 = 0.9
ADAM_B2 = 0.999
ADAM_EPS = 1e-08
ADAM_WD = 0.01
ADAM_STEP = 10

N_DEV = 8
SCAN_CHUNKS = 8
SCAN_COLS = 512
SCAN_UNROLL = 8
TOKEN_TILE = 256
VMEM_LIMIT = 56 * 1024 * 1024

_F32 = jnp.float32
_BF16 = jnp.bfloat16
_MXU = jnp.bfloat16

_NN = ((1,), (0,))
_NT = ((1,), (1,))
_TN = ((0,), (0,))


def _dot(a, b, dims):
    return lax.dot_general(a.astype(_MXU), b.astype(_MXU), (dims, ((), ())),
                           preferred_element_type=_F32)


def _dot_exact(a, b, dims):
    return lax.dot_general(a.astype(_F32), b.astype(_F32), (dims, ((), ())),
                           precision=lax.Precision.HIGHEST, preferred_element_type=_F32)


def _iota(shape, dim):
    return lax.broadcasted_iota(jnp.int32, shape, dim)


def _rms_fwd(x, g):
    r = lax.rsqrt(jnp.mean(x * x, axis=-1, keepdims=True) + NORM_EPS)
    return x * r * g, r


def _rms_bwd(dy, x, g, r):
    a = dy * g
    xn = x * r
    dx = r * (a - xn * jnp.mean(a * xn, axis=-1, keepdims=True))
    dg = jnp.sum(dy * xn, axis=0, keepdims=True)
    return dx, dg


def _call(body, grid, in_specs, out_specs, out_shape, name, scratch=(), tokens=()):
    params = pltpu.CompilerParams(dimension_semantics=("arbitrary",) * len(grid),
                                  vmem_limit_bytes=VMEM_LIMIT)
    n_in, n_tok = len(in_specs), len(tokens)

    def run(*refs):
        return body(*refs[:n_in], *refs[n_in + n_tok:])

    call = pl.pallas_call(run, grid=grid,
                          in_specs=list(in_specs) + [pl.BlockSpec(memory_space=pl.ANY)] * n_tok,
                          out_specs=out_specs, out_shape=out_shape, scratch_shapes=list(scratch),
                          compiler_params=params, name=name)
    return lambda *args: call(*args, *tokens)


def _rows(tm, n):
    return pl.BlockSpec((tm, n), lambda i: (i, 0))


def _whole(shape):
    nd = len(shape)
    return pl.BlockSpec(shape, lambda i: (0,) * nd)


def _sds(shape, dtype):
    return jax.ShapeDtypeStruct(shape, dtype)


def _tile(L):
    return min(TOKEN_TILE, L)


def _chunk_tile(L):
    return L // SCAN_CHUNKS


def _chunk_block(L, n):
    return pl.BlockSpec((_chunk_tile(L), n), lambda i: (0, i))


def _chunk_shape(L, n):
    return (_chunk_tile(L), SCAN_CHUNKS * n)


def _accumulate(ref, val, first):
    @pl.when(first)
    def _():
        ref[...] = val

    @pl.when(jnp.logical_not(first))
    def _():
        ref[...] += val


def _rope_rows():
    half = ROPE_DIM // 2
    inv = (np.float32(ROPE_THETA) ** (-np.arange(half, dtype=np.float32) * np.float32(2.0) / np.float32(ROPE_DIM))).astype(np.float32)
    col = np.arange(KV_WIDTH) % HEAD_DIM
    freq = np.where(col < ROPE_DIM, inv[col % half], 0.0).astype(np.float32)
    sign = np.where(col < half, -1.0, np.where(col < ROPE_DIM, 1.0, 0.0)).astype(np.float32)
    return freq[None, :], sign[None, :]


def _rope_tables(pos_col):
    L = pos_col.shape[0]
    tm = _tile(L)
    freq, sign = _rope_rows()

    def body(pos_ref, freq_ref, sign_ref, cos_ref, sin_ref):
        ang = pos_ref[...].astype(_F32) * freq_ref[...]
        cos_ref[...] = jnp.cos(ang)
        sin_ref[...] = jnp.sin(ang) * sign_ref[...]

    return _call(body, (L // tm,),
                 [_rows(tm, 1), _whole((1, KV_WIDTH)), _whole((1, KV_WIDTH))],
                 [_rows(tm, KV_WIDTH), _rows(tm, KV_WIDTH)],
                 [_sds((L, KV_WIDTH), _F32)] * 2, "rope_tables")(pos_col, jnp.asarray(freq), jnp.asarray(sign))


def _widen(t, width):
    return t if width == KV_WIDTH else jnp.concatenate([t] * (width // KV_WIDTH), axis=1)


def _rope_partner(t):
    w = t.shape[1]
    in_head = _iota((1, w), 1) & (HEAD_DIM - 1)
    second = jnp.where(in_head < ROPE_DIM, pltpu.roll(t, ROPE_DIM // 2, 1), 0.0)
    return jnp.where(in_head < ROPE_DIM // 2, pltpu.roll(t, w - ROPE_DIM // 2, 1), second)


def _rope_apply(t, cos_t, sin_t):
    w = t.shape[1]
    return t * _widen(cos_t, w) + _rope_partner(t) * _widen(sin_t, w)


def _rope_transpose(dt, cos_t, sin_t):
    w = dt.shape[1]
    return dt * _widen(cos_t, w) + _rope_partner(dt * _widen(sin_t, w))


def _in_proj(x, g_pre_mix, w_in, cos_t, sin_t):
    L = x.shape[0]
    tm = _chunk_tile(L)

    def body(x_ref, g_ref, w_ref, cos_ref, sin_ref, hn_ref, u_ref, q_ref, k_ref, v_ref):
        hn, _ = _rms_fwd(x_ref[...], g_ref[...])
        hn = hn.astype(_BF16)
        hn_ref[...] = hn
        proj = _dot(hn, w_ref[...], _NT)
        u_ref[...] = proj[:, :SSM_WIDTH]
        q = proj[:, SSM_WIDTH:SSM_WIDTH + ATTN_WIDTH]
        k = proj[:, SSM_WIDTH + ATTN_WIDTH:SSM_WIDTH + ATTN_WIDTH + KV_WIDTH]
        cos_v, sin_v = cos_ref[...], sin_ref[...]
        q_ref[...] = _rope_apply(q, cos_v, sin_v).astype(_BF16)
        k_ref[...] = _rope_apply(k, cos_v, sin_v).astype(_BF16)
        v_ref[...] = proj[:, SSM_WIDTH + ATTN_WIDTH + KV_WIDTH:].astype(_BF16)

    return _call(body, (L // tm,),
                 [_rows(tm, D_MODEL), _whole((1, D_MODEL)), _whole((IN_WIDTH, D_MODEL)),
                  _rows(tm, KV_WIDTH), _rows(tm, KV_WIDTH)],
                 [_rows(tm, D_MODEL), _chunk_block(L, SSM_WIDTH), _rows(tm, ATTN_WIDTH),
                  _rows(tm, KV_WIDTH), _rows(tm, KV_WIDTH)],
                 [_sds((L, D_MODEL), _BF16), _sds(_chunk_shape(L, SSM_WIDTH), _F32), _sds((L, ATTN_WIDTH), _BF16),
                  _sds((L, KV_WIDTH), _BF16), _sds((L, KV_WIDTH), _BF16)],
                 "in_proj")(x, g_pre_mix, w_in, cos_t, sin_t)


def _s5_discretize(lam_re, lam_im, log_dt):
    lr = jnp.minimum(lam_re, -1e-4)
    li = lam_im
    dt = jnp.exp(log_dt)
    mag = jnp.exp(lr * dt)
    ar = mag * jnp.cos(li * dt)
    ai = mag * jnp.sin(li * dt)
    den = lr * lr + li * li
    fr = ((ar - 1.0) * lr + ai * li) / den
    fi = (ai * lr - (ar - 1.0) * li) / den
    return ar, ai, fr, fi


def _s5_bbar(lam_re, lam_im, log_dt, b_re, b_im):
    ar, ai, fr, fi = _s5_discretize(lam_re, lam_im, log_dt)
    return ar, ai, fr * b_re - fi * b_im, fr * b_im + fi * b_re


def _spread_masks():
    e16 = (_iota((SSM_GROUP, SSM_WIDTH), 1) & (SSM_GROUP - 1)) == _iota((SSM_GROUP, SSM_WIDTH), 0)
    e64 = (_iota((SSM_STATE, N_STATE), 1) & (SSM_STATE - 1)) == _iota((SSM_STATE, N_STATE), 0)
    mask_b = (_iota((N_STATE, SSM_WIDTH), 0) >> 6) == (_iota((N_STATE, SSM_WIDTH), 1) >> 4)
    mask_c = (_iota((SSM_WIDTH, N_STATE), 0) >> 4) == (_iota((SSM_WIDTH, N_STATE), 1) >> 6)
    return e16.astype(_F32), e64.astype(_F32), mask_b, mask_c


SUPER = 4
SB_STATE = N_STATE // SUPER
SB_WIDTH = SSM_WIDTH // SUPER


def _sb_state(k):
    return slice(SB_STATE * k, SB_STATE * (k + 1))


def _sb_width(k):
    return slice(SB_WIDTH * k, SB_WIDTH * (k + 1))


def _dt_column(log_dt_row):
    eye = _iota((SSM_GROUPS, SSM_GROUPS), 0) == _iota((SSM_GROUPS, SSM_GROUPS), 1)
    return jnp.sum(jnp.where(eye, log_dt_row, 0.0), axis=1, keepdims=True)


def _group_masks():
    e64 = ((_iota((SSM_STATE, N_STATE), 1) & (SSM_STATE - 1)) == _iota((SSM_STATE, N_STATE), 0)).astype(_F32)
    own = _iota((SSM_GROUPS, N_STATE), 0) == (_iota((SSM_GROUPS, N_STATE), 1) >> 6)
    return e64, own


def _rows_of_group():
    return ((_iota((SSM_WIDTH, SSM_GROUPS), 0) >> 4) == _iota((SSM_WIDTH, SSM_GROUPS), 1)).astype(_F32)


def _ssm_prep(lam_re, lam_im, log_dt, b_re, b_im, c_re, c_im):
    def body(lr_ref, li_ref, ld_ref, bre, bim, cre, cim, ar_ref, ai_ref, btr, bti, ctr, cti):
        ar, ai, fr, fi = _s5_discretize(lr_ref[...], li_ref[...], _dt_column(ld_ref[...]))
        e64, own = _group_masks()
        mask_c = (_iota((SSM_WIDTH, N_STATE), 0) >> 4) == (_iota((SSM_WIDTH, N_STATE), 1) >> 6)

        def to_row(t):
            return jnp.sum(jnp.where(own, _dot_exact(t, e64, _NN), 0.0), axis=0, keepdims=True)

        def fold(m):
            full = jnp.where(mask_c, _dot(m, e64, _NN), 0.0)
            return sum(full[_sb_width(k), :] for k in range(SUPER)).astype(_BF16)

        ar_ref[...] = to_row(ar)
        ai_ref[...] = to_row(ai)
        spread = _rows_of_group()
        fr_t = _dot_exact(spread, fr, _NN)
        fi_t = _dot_exact(spread, fi, _NN)
        btr[...] = fold(fr_t * bre[...] - fi_t * bim[...])
        bti[...] = fold(fr_t * bim[...] + fi_t * bre[...])
        ctr[...] = fold(cre[...])
        cti[...] = fold(cim[...])

    row = (1, N_STATE)
    ins = [lam_re, lam_im, log_dt, b_re, b_im, c_re, c_im]
    return _call(body, (1,), [_whole(a.shape) for a in ins],
                 [_whole(row), _whole(row)] + [_whole((SB_WIDTH, N_STATE))] * 4,
                 [_sds(row, _F32), _sds(row, _F32)] + [_sds((SB_WIDTH, N_STATE), _BF16)] * 4,
                 "ssm_prep")(*ins)


def _ssm_bu(u, bt_re, bt_im):
    L = u.shape[0]
    tm = _tile(L)

    def body(u_ref, br_ref, bi_ref, or_ref, oi_ref):
        for k in range(SUPER):
            ub = u_ref[:, _sb_width(k)].astype(_BF16)
            or_ref[:, _sb_state(k)] = _dot(ub, br_ref[:, _sb_state(k)], _NN)
            oi_ref[:, _sb_state(k)] = _dot(ub, bi_ref[:, _sb_state(k)], _NN)

    return _call(body, (L // tm,),
                 [_rows(tm, SSM_WIDTH), _whole((SB_WIDTH, N_STATE)), _whole((SB_WIDTH, N_STATE))],
                 [_rows(tm, N_STATE), _rows(tm, N_STATE)],
                 [_sds((L, N_STATE), _F32)] * 2, "ssm_bu")(u, bt_re, bt_im)


def _complex_power(ar, ai, n):
    def step(_, c):
        pr, pi = c
        return pr * ar - pi * ai, pr * ai + pi * ar
    return lax.fori_loop(0, n, step, (jnp.ones_like(ar), jnp.zeros_like(ai)))


def _chunk_carries(er, ei, pr, pi, reverse):
    rows = _iota(er.shape, 0)
    sr = jnp.zeros_like(pr)
    si = jnp.zeros_like(pi)
    out_r = jnp.zeros_like(er)
    out_i = jnp.zeros_like(ei)
    order = range(SCAN_CHUNKS - 1, 0, -1) if reverse else range(SCAN_CHUNKS - 1)
    for c in order:
        e_r = er[c:c + 1, :]
        e_i = ei[c:c + 1, :]
        sr, si = pr * sr - pi * si + e_r, pr * si + pi * sr + e_i
        nxt = c - 1 if reverse else c + 1
        out_r = jnp.where(rows == nxt, sr, out_r)
        out_i = jnp.where(rows == nxt, si, out_i)
    return out_r, out_i


def _scan_fwd(b_re, b_im, a_re, a_im):
    T = b_re.shape[0]
    W = SCAN_COLS
    blk = pl.BlockSpec((T, SCAN_CHUNKS, W), lambda j: (0, 0, j))
    vec = pl.BlockSpec((1, W), lambda j: (0, j))

    def body(br_ref, bi_ref, ar_ref, ai_ref, xr_ref, xi_ref):
        ar, ai = ar_ref[...], ai_ref[...]
        ar8 = jnp.broadcast_to(ar, (SCAN_CHUNKS, W))
        ai8 = jnp.broadcast_to(ai, (SCAN_CHUNKS, W))

        def local(t, c):
            cr, ci = c
            return ar8 * cr - ai8 * ci + br_ref[t], ar8 * ci + ai8 * cr + bi_ref[t]

        zero = jnp.zeros((SCAN_CHUNKS, W), _F32)
        er, ei = lax.fori_loop(0, T, local, (zero, zero), unroll=SCAN_UNROLL)
        pr, pi = _complex_power(ar, ai, T)
        sr, si = _chunk_carries(er, ei, pr, pi, reverse=False)

        def final(t, c):
            nr, ni = local(t, c)
            xr_ref[t] = nr
            xi_ref[t] = ni
            return nr, ni

        lax.fori_loop(0, T, final, (sr, si), unroll=SCAN_UNROLL)

    shape = _sds(b_re.shape, _F32)
    return _call(body, (N_STATE // W,), [blk, blk, vec, vec], [blk, blk], [shape, shape],
                 "scan_fwd")(b_re, b_im, a_re, a_im)


def _scan_bwd(dx_re, dx_im, x_re, x_im, a_re, a_im, tokens=()):
    T = dx_re.shape[0]
    W = SCAN_COLS
    blk = pl.BlockSpec((T, SCAN_CHUNKS, W), lambda j: (0, 0, j))
    vec = pl.BlockSpec((1, W), lambda j: (0, j))

    def body(dr_ref, di_ref, xr_ref, xi_ref, ar_ref, ai_ref, lr_ref, li_ref, dar_ref, dai_ref):
        ar, ai = ar_ref[...], ai_ref[...]
        ar8 = jnp.broadcast_to(ar, (SCAN_CHUNKS, W))
        ai8 = jnp.broadcast_to(ai, (SCAN_CHUNKS, W))

        def local(t, c):
            cr, ci = c
            return ar8 * cr + ai8 * ci + dr_ref[t], ar8 * ci - ai8 * cr + di_ref[t]

        zero = jnp.zeros((SCAN_CHUNKS, W), _F32)
        er, ei = lax.fori_loop(0, T, lambda k, c: local(T - 1 - k, c), (zero, zero), unroll=SCAN_UNROLL)
        pr, pi = _complex_power(ar, -ai, T)
        sr, si = _chunk_carries(er, ei, pr, pi, reverse=True)

        def grad_a(acc, nr, ni, xpr, xpi):
            return acc[0] + nr * xpr + ni * xpi, acc[1] + ni * xpr - nr * xpi

        def final(k, c):
            t = T - 1 - k
            nr, ni = local(t, c[:2])
            lr_ref[t] = nr
            li_ref[t] = ni
            gr, gi = grad_a(c[2:], nr, ni, xr_ref[t - 1], xi_ref[t - 1])
            return nr, ni, gr, gi

        cr, ci, gr, gi = lax.fori_loop(0, T - 1, final, (sr, si, zero, zero), unroll=SCAN_UNROLL)
        nr, ni = local(0, (cr, ci))
        lr_ref[0] = nr
        li_ref[0] = ni
        first = _iota((SCAN_CHUNKS, W), 0) == 0
        xpr = jnp.where(first, 0.0, pltpu.roll(xr_ref[T - 1], 1, 0))
        xpi = jnp.where(first, 0.0, pltpu.roll(xi_ref[T - 1], 1, 0))
        gr, gi = grad_a((gr, gi), nr, ni, xpr, xpi)
        dar_ref[...] = jnp.sum(gr, axis=0, keepdims=True)
        dai_ref[...] = jnp.sum(gi, axis=0, keepdims=True)

    shape = _sds(dx_re.shape, _F32)
    row = _sds((1, N_STATE), _F32)
    return _call(body, (N_STATE // W,), [blk, blk, blk, blk, vec, vec], [blk, blk, vec, vec],
                 [shape, shape, row, row], "scan_bwd", tokens=tokens)(dx_re, dx_im, x_re, x_im, a_re, a_im)


_GELU_K = math.sqrt(2.0 / math.pi)
_GELU_C = 0.044715


def _gelu(y):
    return 0.5 * y * (1.0 + jnp.tanh(_GELU_K * (y + _GELU_C * y * y * y)))


def _gelu_grad(y):
    t = jnp.tanh(_GELU_K * (y + _GELU_C * y * y * y))
    return 0.5 * (1.0 + t) + 0.5 * y * (1.0 - t * t) * _GELU_K * (1.0 + 3.0 * _GELU_C * y * y)


def _ssm_out(x_re, x_im, u, ct_re, ct_im, d_row, w_glu, b_glu, g_ssm):
    L = u.shape[0]
    tm = _tile(L)

    def body(xr_ref, xi_ref, u_ref, cr_ref, ci_ref, d_ref, w_ref, b_ref, g_ref, y_ref, z_ref, n_ref):
        cx = [_dot(xr_ref[:, _sb_state(k)], cr_ref[:, _sb_state(k)], _NT)
              - _dot(xi_ref[:, _sb_state(k)], ci_ref[:, _sb_state(k)], _NT) for k in range(SUPER)]
        y = jnp.concatenate(cx, axis=1) + d_ref[...] * u_ref[...]
        y_ref[...] = y
        z = _dot(_gelu(y), w_ref[...], _NT) + b_ref[...]
        z_ref[...] = z
        out = z[:, :SSM_WIDTH] * jax.nn.sigmoid(z[:, SSM_WIDTH:])
        n, _ = _rms_fwd(out, g_ref[...])
        n_ref[...] = n.astype(_BF16)

    return _call(body, (L // tm,),
                 [_rows(tm, N_STATE), _rows(tm, N_STATE), _rows(tm, SSM_WIDTH),
                  _whole((SB_WIDTH, N_STATE)), _whole((SB_WIDTH, N_STATE)), _whole((1, SSM_WIDTH)),
                  _whole((2 * SSM_WIDTH, SSM_WIDTH)), _whole((1, 2 * SSM_WIDTH)), _whole((1, SSM_WIDTH))],
                 [_rows(tm, SSM_WIDTH), _rows(tm, 2 * SSM_WIDTH), _rows(tm, SSM_WIDTH)],
                 [_sds((L, SSM_WIDTH), _F32), _sds((L, 2 * SSM_WIDTH), _F32), _sds((L, SSM_WIDTH), _BF16)],
                 "ssm_out")(x_re, x_im, u, ct_re, ct_im, d_row, w_glu, b_glu, g_ssm)


def _ssm_out_bwd(dn, y, z, u, ct_re, ct_im, d_row, w_glu, g_ssm):
    L = u.shape[0]
    tm = _tile(L)

    def body(dn_ref, y_ref, z_ref, u_ref, cr_ref, ci_ref, d_ref, w_ref, g_ref,
             gy_ref, dz_ref, dy_ref, dud_ref, dxr_ref, dxi_ref, dg_ref, db_ref, dd_ref):
        first = pl.program_id(0) == 0
        z = z_ref[...]
        z1, z2 = z[:, :SSM_WIDTH], z[:, SSM_WIDTH:]
        sig = jax.nn.sigmoid(z2)
        out = z1 * sig
        g = g_ref[...]
        _, r = _rms_fwd(out, g)
        dout, dg = _rms_bwd(dn_ref[...], out, g, r)
        _accumulate(dg_ref, dg, first)
        dz = jnp.concatenate([dout * sig, dout * z1 * sig * (1.0 - sig)], axis=1)
        _accumulate(db_ref, jnp.sum(dz, axis=0, keepdims=True), first)
        dzb = dz.astype(_BF16)
        dz_ref[...] = dzb
        y = y_ref[...]
        gy_ref[...] = _gelu(y).astype(_BF16)
        dy = _dot(dzb, w_ref[...], _NN) * _gelu_grad(y)
        u = u_ref[...]
        _accumulate(dd_ref, jnp.sum(dy * u, axis=0, keepdims=True), first)
        dud_ref[...] = d_ref[...] * dy
        dyb = dy.astype(_BF16)
        dy_ref[...] = dyb
        for k in range(SUPER):
            dxr_ref[:, _sb_state(k)] = _dot(dyb[:, _sb_width(k)], cr_ref[:, _sb_state(k)], _NN)
            dxi_ref[:, _sb_state(k)] = -_dot(dyb[:, _sb_width(k)], ci_ref[:, _sb_state(k)], _NN)

    row = _whole((1, SSM_WIDTH))
    return _call(body, (L // tm,),
                 [_rows(tm, SSM_WIDTH), _rows(tm, SSM_WIDTH), _rows(tm, 2 * SSM_WIDTH), _rows(tm, SSM_WIDTH),
                  _whole((SB_WIDTH, N_STATE)), _whole((SB_WIDTH, N_STATE)), row,
                  _whole((2 * SSM_WIDTH, SSM_WIDTH)), row],
                 [_rows(tm, SSM_WIDTH), _rows(tm, 2 * SSM_WIDTH), _rows(tm, SSM_WIDTH), _rows(tm, SSM_WIDTH),
                  _rows(tm, N_STATE), _rows(tm, N_STATE), row, _whole((1, 2 * SSM_WIDTH)), row],
                 [_sds((L, SSM_WIDTH), _BF16), _sds((L, 2 * SSM_WIDTH), _BF16), _sds((L, SSM_WIDTH), _BF16),
                  _sds((L, SSM_WIDTH), _F32), _sds((L, N_STATE), _F32), _sds((L, N_STATE), _F32),
                  _sds((1, SSM_WIDTH), _F32), _sds((1, 2 * SSM_WIDTH), _F32), _sds((1, SSM_WIDTH), _F32)],
                 "ssm_out_bwd")(dn, y, z, u, ct_re, ct_im, d_row, w_glu, g_ssm)


def _ssm_du(lam_re, lam_im, bt_re, bt_im, dud):
    L = dud.shape[0]
    tm = _tile(L)

    def body(lr_ref, li_ref, br_ref, bi_ref, dud_ref, du_ref):
        for k in range(SUPER):
            du_ref[:, _sb_width(k)] = (_dot(lr_ref[:, _sb_state(k)], br_ref[:, _sb_state(k)], _NT)
                                       + _dot(li_ref[:, _sb_state(k)], bi_ref[:, _sb_state(k)], _NT)
                                       + dud_ref[:, _sb_width(k)])

    return _call(body, (L // tm,),
                 [_rows(tm, N_STATE), _rows(tm, N_STATE), _whole((SB_WIDTH, N_STATE)),
                  _whole((SB_WIDTH, N_STATE)), _rows(tm, SSM_WIDTH)],
                 _rows(tm, SSM_WIDTH), _sds((L, SSM_WIDTH), _F32), "ssm_du")(lam_re, lam_im, bt_re, bt_im, dud)


def _ssm_weight_grads(dy, x_re, x_im, lam_re, lam_im, u):
    L = u.shape[0]

    def body(dy_ref, xr_ref, xi_ref, lr_ref, li_ref, u_ref, dcr_ref, dci_ref, dbr_ref, dbi_ref):
        dyb = dy_ref[...]
        ub = u_ref[...].astype(_BF16)
        dcr_ref[...] = _dot(dyb, xr_ref[...], _TN)
        dci_ref[...] = _dot(dyb, xi_ref[...], _TN)
        dbr_ref[...] = _dot(ub, lr_ref[...], _TN)
        dbi_ref[...] = _dot(ub, li_ref[...], _TN)

    width = pl.BlockSpec((L, SB_WIDTH), lambda k: (0, k))
    state = pl.BlockSpec((L, SB_STATE), lambda k: (0, k))
    out = pl.BlockSpec((SB_WIDTH, SB_STATE), lambda k: (0, k))
    return _call(body, (SUPER,), [width, state, state, state, state, width], [out] * 4,
                 [_sds((SB_WIDTH, N_STATE), _F32)] * 4,
                 "ssm_weight_grads")(dy, x_re, x_im, lam_re, lam_im, u)


_SSM_PACK = {"ssm_b_re": (0, SSM_WIDTH, 0, SSM_STATE), "ssm_c_re": (0, SSM_WIDTH, 64, SSM_STATE),
             "ssm_b_im": (512, SSM_WIDTH, 0, SSM_STATE), "ssm_c_im": (512, SSM_WIDTH, 64, SSM_STATE),
             "ssm_lambda_re": (1024, SSM_GROUPS, 0, SSM_STATE), "ssm_lambda_im": (1024, SSM_GROUPS, 64, SSM_STATE),
             "ssm_d": (1056, SSM_GROUPS, 0, SSM_GROUP), "ssm_log_dt": (1088, 1, 0, SSM_GROUPS)}
_PACK_TILE = 16
_SSM_PACK_ROWS = 1088 + _PACK_TILE


def _ssm_param_bwd(da_re, da_im, dbt_re, dbt_im, dct_re, dct_im, lam_re, lam_im, log_dt, b_re, b_im, g_d):
    def body(dar, dai, dbr, dbi, dcr, dci, lr_ref, li_ref, ld_ref, bre_ref, bim_ref, gd_ref, pack_ref):
        lane_in = _iota((SSM_STATE, _LANES), 0)
        lane_out = _iota((SSM_STATE, _LANES), 1)
        low = (lane_out == lane_in).astype(_F32)
        high = (lane_out == lane_in + SSM_STATE).astype(_F32)

        def side_by_side(a, b):
            return _dot_exact(a, low, _NN) + _dot_exact(b, high, _NN)

        tail = _SSM_PACK["ssm_d"][0]
        pack_ref[tail:, :] = jnp.zeros((_SSM_PACK_ROWS - tail, _LANES), _BF16)
        pack_ref[tail:tail + SSM_GROUPS, 0:SSM_GROUP] = gd_ref[...].astype(_BF16)
        own_c = (_iota((SB_WIDTH, SB_STATE), 0) >> 4) == (_iota((SB_WIDTH, SB_STATE), 1) >> 6)

        def unfold(ref):
            blocks = []
            for k in range(SUPER):
                t = jnp.where(own_c, ref[:, _sb_state(k)], 0.0)
                t = sum(t[:, 128 * i:128 * (i + 1)] for i in range(SB_STATE // 128))
                blocks.append((t + pltpu.roll(t, SSM_STATE, 1))[:, :SSM_STATE])
            return jnp.concatenate(blocks, axis=0)

        dbb_re, dbb_im = unfold(dbr), unfold(dbi)
        b_re, b_im = bre_ref[...], bim_ref[...]
        dt_col = _dt_column(ld_ref[...])
        (_, _, fr, fi), vjp = jax.vjp(_s5_discretize, lr_ref[...], li_ref[...], dt_col)
        spread = _rows_of_group()
        fr_t = _dot_exact(spread, fr, _NN)
        fi_t = _dot_exact(spread, fi, _NN)
        pack_ref[0:SSM_WIDTH, :] = side_by_side(fr_t * dbb_re + fi_t * dbb_im, unfold(dcr)).astype(_BF16)
        pack_ref[SSM_WIDTH:2 * SSM_WIDTH, :] = side_by_side(fr_t * dbb_im - fi_t * dbb_re, -unfold(dci)).astype(_BF16)
        d_fr = _dot_exact(spread, dbb_re * b_re + dbb_im * b_im, _TN)
        d_fi = _dot_exact(spread, dbb_im * b_re - dbb_re * b_im, _TN)
        e64, own = _group_masks()

        def from_row(ref):
            return _dot_exact(jnp.where(own, ref[...], 0.0), e64, _NT)

        d_lr, d_li, d_dt = vjp((from_row(dar), from_row(dai), d_fr, d_fi))
        lam_rows = _SSM_PACK["ssm_lambda_re"][0]
        pack_ref[lam_rows:lam_rows + SSM_GROUPS, :] = side_by_side(d_lr, d_li).astype(_BF16)
        eye = (_iota((SSM_GROUPS, SSM_GROUPS), 0) == _iota((SSM_GROUPS, SSM_GROUPS), 1)).astype(_F32)
        dt_row = _SSM_PACK["ssm_log_dt"][0]
        pack_ref[dt_row:dt_row + _PACK_TILE, 0:SSM_GROUPS] = _dot_exact(
            jnp.broadcast_to(d_dt, (SSM_GROUPS, 128)), eye, _TN)[0:_PACK_TILE].astype(_BF16)

    ins = [da_re, da_im, dbt_re, dbt_im, dct_re, dct_im, lam_re, lam_im, log_dt, b_re, b_im, g_d]
    out = (_SSM_PACK_ROWS, _LANES)
    return _call(body, (1,), [_whole(a.shape) for a in ins], _whole(out), _sds(out, _BF16), "ssm_param_bwd")(*ins)


def _head_spread(j):
    r = _iota((KV_WIDTH, 256), 0)
    c = _iota((KV_WIDTH, 256), 1)
    return (r == HEAD_DIM * j + (c & (HEAD_DIM - 1))).astype(_BF16)


STACK = Q_PER_KV * BLOCK


def _stack_heads(t):
    lane_head = _iota((1, 256), 1) >> 6
    return jnp.concatenate([jnp.where(lane_head == g, t, jnp.zeros_like(t)) for g in range(Q_PER_KV)], axis=0)


def _unstack_heads(t):
    lane_head = _iota((1, 256), 1) >> 6
    return sum(jnp.where(lane_head == g, t[BLOCK * g:BLOCK * (g + 1)], 0.0) for g in range(Q_PER_KV))


def _stacked_sinks(sink_ref, j):
    block = _iota((STACK, 1), 0) >> 7
    col = jnp.full((STACK, 1), sink_ref[Q_PER_KV * j], _F32)
    for g in range(1, Q_PER_KV):
        col = jnp.where(block == g, sink_ref[Q_PER_KV * j + g], col)
    return col


def _fold_heads(t, j):
    t = t[:, :KV_WIDTH] + t[:, KV_WIDTH:]
    t = t + pltpu.roll(t, HEAD_DIM, 1)
    return jnp.where((_iota((1, KV_WIDTH), 1) >> 6) == j, t, 0.0)


def _attn_scores(q_stacked, kt, blk, sink):
    s = _dot(q_stacked, kt, _NT) * (HEAD_DIM ** -0.5)
    qi = _iota((STACK, 2 * BLOCK), 0) & (BLOCK - 1)
    kj = _iota((STACK, 2 * BLOCK), 1)
    rel = qi + BLOCK - kj
    valid = (rel >= 0) & (rel < BLOCK) & (blk * BLOCK - BLOCK + kj >= 0)
    s = jnp.where(valid, s, MASK_VALUE)
    m = jnp.maximum(jnp.max(s, axis=-1, keepdims=True), sink)
    p = jnp.exp(s - m)
    e_sink = jnp.exp(sink - m)
    den = jnp.sum(p, axis=-1, keepdims=True) + e_sink
    return p / den, e_sink / den


def _attn_specs():
    prev = lambda i: (jnp.maximum(i - 1, 0), 0)
    cur = lambda i: (i, 0)
    kv = [pl.BlockSpec((BLOCK, KV_WIDTH), prev), pl.BlockSpec((BLOCK, KV_WIDTH), cur)]
    return [pl.BlockSpec((BLOCK, ATTN_WIDTH), cur)] + kv + kv


def _attn_fwd(q, k, v, sinks, g_attn):
    L = q.shape[0]

    def body(q_ref, kp_ref, kc_ref, vp_ref, vc_ref, sink_ref, g_ref, o_ref, n_ref):
        blk = pl.program_id(0)
        kwin = jnp.concatenate([kp_ref[...], kc_ref[...]], axis=0)
        vwin = jnp.concatenate([vp_ref[...], vc_ref[...]], axis=0)
        halves = []
        for j in range(N_KV_HEADS):
            spread = _head_spread(j)
            kt = _dot(kwin, spread, _NN).astype(_BF16)
            vt = _dot(vwin, spread, _NN).astype(_BF16)
            qs = _stack_heads(q_ref[:, 256 * j:256 * (j + 1)])
            p, _ = _attn_scores(qs, kt, blk, _stacked_sinks(sink_ref, j))
            halves.append(_unstack_heads(_dot(p, vt, _NN)))
        o = jnp.concatenate(halves, axis=1)
        o_ref[...] = o
        n, _ = _rms_fwd(o, g_ref[...])
        n_ref[...] = n.astype(_BF16)

    cur = lambda i: (i, 0)
    return _call(body, (L // BLOCK,),
                 _attn_specs() + [pl.BlockSpec(memory_space=pltpu.SMEM), _whole((1, ATTN_WIDTH))],
                 [pl.BlockSpec((BLOCK, ATTN_WIDTH), cur)] * 2,
                 [_sds((L, ATTN_WIDTH), _F32), _sds((L, ATTN_WIDTH), _BF16)],
                 "attn_fwd")(q, k, k, v, v, sinks, g_attn)


def _attn_bwd(q, k, v, o, dn, sinks, g_attn):
    L = q.shape[0]

    def body(q_ref, kp_ref, kc_ref, vp_ref, vc_ref, o_ref, dn_ref, sink_ref, g_ref,
             dq_ref, dk_ref, dv_ref, dsink_ref, dg_ref):
        blk = pl.program_id(0)
        first = blk == 0

        @pl.when(first)
        def _():
            dk_ref[...] = jnp.zeros_like(dk_ref)
            dv_ref[...] = jnp.zeros_like(dv_ref)
            dsink_ref[...] = jnp.zeros_like(dsink_ref)

        o = o_ref[...]
        g = g_ref[...]
        _, r = _rms_fwd(o, g)
        do, dg = _rms_bwd(dn_ref[...], o, g, r)
        _accumulate(dg_ref, dg, first)
        kwin = jnp.concatenate([kp_ref[...], kc_ref[...]], axis=0)
        vwin = jnp.concatenate([vp_ref[...], vc_ref[...]], axis=0)
        lane = _iota((1, 128), 1)
        dsink = jnp.zeros((1, 128), _F32)
        dkwin = jnp.zeros((2 * BLOCK, KV_WIDTH), _F32)
        dvwin = jnp.zeros((2 * BLOCK, KV_WIDTH), _F32)
        dq_halves = []
        for j in range(N_KV_HEADS):
            spread = _head_spread(j)
            kt = _dot(kwin, spread, _NN).astype(_BF16)
            vt = _dot(vwin, spread, _NN).astype(_BF16)
            qs = _stack_heads(q_ref[:, 256 * j:256 * (j + 1)])
            dos = _stack_heads(do[:, 256 * j:256 * (j + 1)]).astype(_BF16)
            p, p_sink = _attn_scores(qs, kt, blk, _stacked_sinks(sink_ref, j))
            dp = _dot(dos, vt, _NT)
            delta = jnp.sum(p * dp, axis=-1, keepdims=True)
            ds = (p * (dp - delta) * (HEAD_DIM ** -0.5)).astype(_BF16)
            sink_term = p_sink * delta
            for g in range(Q_PER_KV):
                head_sum = jnp.sum(sink_term[BLOCK * g:BLOCK * (g + 1)], axis=0, keepdims=True)
                dsink = dsink - jnp.where(lane == Q_PER_KV * j + g, head_sum, 0.0)
            dvwin = dvwin + _fold_heads(_dot(p, dos, _TN), j)
            dkwin = dkwin + _fold_heads(_dot(ds, qs, _TN), j)
            dq_halves.append(_unstack_heads(_dot(ds, kt, _NN)))
        dq_ref[...] = jnp.concatenate(dq_halves, axis=1)
        dsink_ref[...] += dsink
        prev = pl.ds(pl.multiple_of(jnp.maximum(blk - 1, 0) * BLOCK, BLOCK), BLOCK)
        cur = pl.ds(pl.multiple_of(blk * BLOCK, BLOCK), BLOCK)
        dk_ref[prev, :] += dkwin[:BLOCK]
        dk_ref[cur, :] += dkwin[BLOCK:]
        dv_ref[prev, :] += dvwin[:BLOCK]
        dv_ref[cur, :] += dvwin[BLOCK:]

    cur = lambda i: (i, 0)
    blk_q = pl.BlockSpec((BLOCK, ATTN_WIDTH), cur)
    return _call(body, (L // BLOCK,),
                 _attn_specs() + [blk_q, blk_q, pl.BlockSpec(memory_space=pltpu.SMEM), _whole((1, ATTN_WIDTH))],
                 [blk_q, _whole((L, KV_WIDTH)), _whole((L, KV_WIDTH)), _whole((1, 128)), _whole((1, ATTN_WIDTH))],
                 [_sds((L, ATTN_WIDTH), _F32), _sds((L, KV_WIDTH), _F32), _sds((L, KV_WIDTH), _F32),
                  _sds((1, 128), _F32), _sds((1, ATTN_WIDTH), _F32)],
                 "attn_bwd")(q, k, k, v, v, o, dn, sinks, g_attn)


def _out_proj(n_ssm, n_attn, x, w_out, g_post_mix, g_pre_ffn):
    L = x.shape[0]
    tm = _chunk_tile(L)

    def body(ns_ref, na_ref, x_ref, w_ref, g1_ref, g2_ref, merged_ref, mo_ref, h1_ref, hn2_ref):
        merged = jnp.concatenate([ns_ref[...], na_ref[...]], axis=1)
        merged_ref[...] = merged
        mo = _dot(merged, w_ref[...], _NN)
        mo_ref[...] = mo
        n, _ = _rms_fwd(mo, g1_ref[...])
        h1 = x_ref[...] + n
        h1_ref[...] = h1
        hn2, _ = _rms_fwd(h1, g2_ref[...])
        hn2_ref[...] = hn2.astype(_BF16)

    row = _whole((1, D_MODEL))
    return _call(body, (L // tm,),
                 [_chunk_block(L, SSM_WIDTH), _rows(tm, ATTN_WIDTH), _rows(tm, D_MODEL), _whole((D_MODEL, D_MODEL)),
                  row, row],
                 [_rows(tm, D_MODEL)] * 4,
                 [_sds((L, D_MODEL), _BF16), _sds((L, D_MODEL), _F32), _sds((L, D_MODEL), _F32), _sds((L, D_MODEL), _BF16)],
                 "out_proj")(n_ssm, n_attn, x, w_out, g_post_mix, g_pre_ffn)


def _ffn(hn2, h1, target, w_gate_up, w_down, g_pre_ffn, g_post_ffn):
    L = h1.shape[0]
    tm = _tile(L)
    half = D_FF // 2

    def body(hn2_ref, h1_ref, tgt_ref, wgu_hbm, wd_hbm, g2_ref, g3_ref,
             act_ref, dgu_ref, dff_ref, dh1_ref, loss_ref, dg3_ref, dg2_ref,
             wgu, wd, gu, sem):
        first = pl.program_id(0) == 0

        @pl.when(first)
        def _():
            c1 = pltpu.make_async_copy(wgu_hbm, wgu, sem.at[0])
            c2 = pltpu.make_async_copy(wd_hbm, wd, sem.at[1])
            c1.start()
            c2.start()
            c1.wait()
            c2.wait()

        hn2 = hn2_ref[...]
        ff = jnp.zeros((tm, D_MODEL), _F32)
        for c in range(2):
            gate = _dot(hn2, wgu[half * c:half * (c + 1), :], _NT)
            up = _dot(hn2, wgu[D_FF + half * c:D_FF + half * (c + 1), :], _NT)
            gu[:, half * c:half * (c + 1)] = gate
            gu[:, D_FF + half * c:D_FF + half * (c + 1)] = up
            act = gate * jax.nn.sigmoid(gate) * up
            act_ref[half * c:half * (c + 1), :] = act.T.astype(_BF16)
            ff = ff + _dot(act, wd[half * c:half * (c + 1), :], _NN)
        g3 = g3_ref[...]
        n, r = _rms_fwd(ff, g3)
        h1 = h1_ref[...]
        err = h1 + n - tgt_ref[...]
        loss = 0.5 * jnp.sum(jnp.mean(err * err, axis=-1, keepdims=True), axis=0, keepdims=True)
        _accumulate(loss_ref, jnp.broadcast_to(loss, (1, 128)), first)
        dh2 = err * (1.0 / D_MODEL)
        dff, dg3 = _rms_bwd(dh2, ff, g3, r)
        _accumulate(dg3_ref, dg3, first)
        dffb = dff.astype(_BF16)
        dff_ref[...] = dffb
        dhn2 = jnp.zeros((tm, D_MODEL), _F32)
        for c in range(2):
            dact = _dot(dffb, wd[half * c:half * (c + 1), :], _NT)
            gate = gu[:, half * c:half * (c + 1)]
            up = gu[:, D_FF + half * c:D_FF + half * (c + 1)]
            sig = jax.nn.sigmoid(gate)
            silu = gate * sig
            dgate = dact * up * (sig + silu * (1.0 - sig))
            dup = dact * silu
            dgu_ref[half * c:half * (c + 1), :] = dgate.T.astype(_BF16)
            dgu_ref[D_FF + half * c:D_FF + half * (c + 1), :] = dup.T.astype(_BF16)
            dhn2 = dhn2 + _dot(dgate, wgu[half * c:half * (c + 1), :], _NN)
            dhn2 = dhn2 + _dot(dup, wgu[D_FF + half * c:D_FF + half * (c + 1), :], _NN)
        g2 = g2_ref[...]
        _, r2 = _rms_fwd(h1, g2)
        dh1, dg2 = _rms_bwd(dhn2, h1, g2, r2)
        _accumulate(dg2_ref, dg2, first)
        dh1_ref[...] = dh2 + dh1

    row = _whole((1, D_MODEL))
    anyspace = pl.BlockSpec(memory_space=pl.ANY)
    return _call(body, (L // tm,),
                 [_rows(tm, D_MODEL), _rows(tm, D_MODEL), _rows(tm, D_MODEL), anyspace, anyspace, row, row],
                 [pl.BlockSpec((D_FF, tm), lambda i: (0, i)), pl.BlockSpec((2 * D_FF, tm), lambda i: (0, i)),
                  _rows(tm, D_MODEL), _rows(tm, D_MODEL), _whole((1, 128)), row, row],
                 [_sds((D_FF, L), _BF16), _sds((2 * D_FF, L), _BF16), _sds((L, D_MODEL), _BF16),
                  _sds((L, D_MODEL), _F32), _sds((1, 128), _F32), _sds((1, D_MODEL), _F32), _sds((1, D_MODEL), _F32)],
                 "ffn",
                 scratch=[pltpu.VMEM((2 * D_FF, D_MODEL), _BF16), pltpu.VMEM((D_FF, D_MODEL), _BF16),
                          pltpu.VMEM((tm, 2 * D_FF), _F32), pltpu.SemaphoreType.DMA((2,))],
                 )(hn2, h1, target, w_gate_up, w_down, g_pre_ffn, g_post_ffn)


def _out_proj_bwd(dh1, mo, w_out, g_post_mix, tokens=()):
    L = dh1.shape[0]
    tm = _chunk_tile(L)

    def body(dh1_ref, mo_ref, w_ref, g_ref, dmo_ref, dns_ref, dna_ref, dg_ref):
        first = pl.program_id(0) == 0
        mo = mo_ref[...]
        g = g_ref[...]
        _, r = _rms_fwd(mo, g)
        dmo, dg = _rms_bwd(dh1_ref[...], mo, g, r)
        _accumulate(dg_ref, dg, first)
        dmob = dmo.astype(_BF16)
        dmo_ref[...] = dmob
        dmerged = _dot(dmob, w_ref[...], _NT)
        dns_ref[...] = dmerged[:, :SSM_WIDTH]
        dna_ref[...] = dmerged[:, SSM_WIDTH:]

    row = _whole((1, D_MODEL))
    return _call(body, (L // tm,),
                 [_rows(tm, D_MODEL), _rows(tm, D_MODEL), _whole((D_MODEL, D_MODEL)), row],
                 [_rows(tm, D_MODEL), _chunk_block(L, SSM_WIDTH), _rows(tm, ATTN_WIDTH), row],
                 [_sds((L, D_MODEL), _BF16), _sds(_chunk_shape(L, SSM_WIDTH), _F32), _sds((L, ATTN_WIDTH), _F32),
                  _sds((1, D_MODEL), _F32)],
                 "out_proj_bwd", tokens=tokens)(dh1, mo, w_out, g_post_mix)


def _in_proj_bwd(du, dq, dk, dv, cos_t, sin_t, x, dh1, g_pre_mix, w_in, tokens=()):
    L = x.shape[0]
    tm = _chunk_tile(L)

    def body(du_ref, dq_ref, dk_ref, dv_ref, cos_ref, sin_ref, x_ref, dh1_ref, g_ref, w_ref,
             dproj_ref, dx_ref, dg_ref):
        first = pl.program_id(0) == 0
        cos_v, sin_v = cos_ref[...], sin_ref[...]
        dproj = jnp.concatenate([du_ref[...], _rope_transpose(dq_ref[...], cos_v, sin_v),
                                 _rope_transpose(dk_ref[...], cos_v, sin_v), dv_ref[...]], axis=1).astype(_BF16)
        dproj_ref[...] = dproj
        dhn = _dot(dproj, w_ref[...], _NN)
        x = x_ref[...]
        g = g_ref[...]
        _, r = _rms_fwd(x, g)
        dx, dg = _rms_bwd(dhn, x, g, r)
        _accumulate(dg_ref, dg, first)
        dx_ref[...] = dh1_ref[...] + dx

    row = _whole((1, D_MODEL))
    return _call(body, (L // tm,),
                 [_chunk_block(L, SSM_WIDTH), _rows(tm, ATTN_WIDTH), _rows(tm, KV_WIDTH), _rows(tm, KV_WIDTH),
                  _rows(tm, KV_WIDTH), _rows(tm, KV_WIDTH), _rows(tm, D_MODEL), _rows(tm, D_MODEL), row,
                  _whole((IN_WIDTH, D_MODEL))],
                 [_rows(tm, IN_WIDTH), _rows(tm, D_MODEL), row],
                 [_sds((L, IN_WIDTH), _BF16), _sds((L, D_MODEL), _F32), _sds((1, D_MODEL), _F32)],
                 "in_proj_bwd", tokens=tokens)(du, dq, dk, dv, cos_t, sin_t, x, dh1, g_pre_mix, w_in)


def _matmul_nn(a, b, out_dtype, name):
    M, K = a.shape
    N = b.shape[1]
    tm = next(t for t in (512, 256, 128) if M % t == 0)
    tn = N if N <= D_MODEL else next(t for t in (512, 256, 128) if N % t == 0)

    def body(a_ref, b_ref, o_ref):
        o_ref[...] = _dot(a_ref[...], b_ref[...], _NN).astype(out_dtype)

    params = pltpu.CompilerParams(dimension_semantics=("arbitrary", "arbitrary"), vmem_limit_bytes=VMEM_LIMIT)
    return pl.pallas_call(body, grid=(M // tm, N // tn),
                          in_specs=[pl.BlockSpec((tm, K), lambda i, j: (i, 0)),
                                    pl.BlockSpec((K, tn), lambda i, j: (0, j))],
                          out_specs=pl.BlockSpec((tm, tn), lambda i, j: (i, j)),
                          out_shape=_sds((M, N), out_dtype), compiler_params=params, name=name)(a, b)


def _matmul_tn(a, b, out_dtype, name, scale=1.0):
    K, M = a.shape
    N = b.shape[1]
    tm = next(t for t in (512, 256, 128) if M % t == 0)
    tn = N if N <= D_MODEL else next(t for t in (512, 256, 128) if N % t == 0)

    def body(a_ref, b_ref, o_ref):
        acc = _dot(a_ref[...], b_ref[...], _TN)
        o_ref[...] = (acc if scale == 1.0 else acc * scale).astype(out_dtype)

    params = pltpu.CompilerParams(dimension_semantics=("arbitrary", "arbitrary"), vmem_limit_bytes=VMEM_LIMIT)
    return pl.pallas_call(body, grid=(M // tm, N // tn),
                          in_specs=[pl.BlockSpec((K, tm), lambda i, j: (0, i)),
                                    pl.BlockSpec((K, tn), lambda i, j: (0, j))],
                          out_specs=pl.BlockSpec((tm, tn), lambda i, j: (i, j)),
                          out_shape=_sds((M, N), out_dtype), compiler_params=params, name=name)(a, b)


def _local_step(x, pos, target, p, fetch, publish, progress):
    L = x.shape[0]
    T = L // SCAN_CHUNKS
    cos_t, sin_t = _rope_tables(pos.reshape(L, 1))
    w_in, = fetch(("w_in",), None)
    hn, u, q, k, v = _in_proj(x, p["g_pre_mix"], w_in, cos_t, sin_t)

    ssm = {n: _to_2d(n, p[n]) for n in ("ssm_lambda_re", "ssm_lambda_im", "ssm_log_dt", "ssm_b_re", "ssm_b_im",
                                        "ssm_c_re", "ssm_c_im")}
    d_row = p["ssm_d"].reshape(1, SSM_WIDTH)
    a_re, a_im, bt_re, bt_im, ct_re, ct_im = _ssm_prep(
        ssm["ssm_lambda_re"], ssm["ssm_lambda_im"], ssm["ssm_log_dt"], ssm["ssm_b_re"], ssm["ssm_b_im"],
        ssm["ssm_c_re"], ssm["ssm_c_im"])

    u_c = u.reshape(L, SSM_WIDTH)
    bu_re, bu_im = _ssm_bu(u_c, bt_re, bt_im)
    x_re, x_im = _scan_fwd(bu_re.reshape(T, SCAN_CHUNKS, N_STATE), bu_im.reshape(T, SCAN_CHUNKS, N_STATE), a_re, a_im)
    w_glu, = fetch(("w_glu",), x_re)
    y, z, n_ssm_c = _ssm_out(x_re.reshape(L, N_STATE), x_im.reshape(L, N_STATE), u_c, ct_re, ct_im, d_row,
                             w_glu, p["b_glu"], p["g_ssm_out"])
    n_ssm = n_ssm_c.reshape(_chunk_shape(L, SSM_WIDTH))

    sinks = p["attn_sinks"].reshape(N_Q_HEADS)
    o, n_attn = _attn_fwd(q, k, v, sinks, p["g_attn_out"])
    w_out, = fetch(("w_out",), n_attn)
    merged, mo, h1, hn2 = _out_proj(n_ssm, n_attn, x, w_out, p["g_post_mix"], p["g_pre_ffn"])
    w_gate_up, w_down = fetch(("w_gate_up", "w_down"), hn2)
    act_t, dgu_t, dff, dh1, loss, dg_post_ffn, dg_pre_ffn = _ffn(
        hn2, h1, target, w_gate_up, w_down, p["g_pre_ffn"], p["g_post_ffn"])
    grads = {"g_post_ffn": dg_post_ffn, "g_pre_ffn": dg_pre_ffn}
    tokens = publish({"w_down": _matmul_nn(act_t, dff, _BF16, "grad_w_down"),
                      "w_gate_up": _matmul_nn(dgu_t, hn2, _BF16, "grad_w_gate_up")})

    dmo, dn_ssm, dn_attn, grads["g_post_mix"] = _out_proj_bwd(dh1, mo, w_out, p["g_post_mix"], tokens)
    grad_w_out = _matmul_tn(merged, dmo, _BF16, "grad_w_out")

    dq, dk, dv, dsink, grads["g_attn_out"] = _attn_bwd(q, k, v, o, dn_attn, sinks, p["g_attn_out"])
    grads["attn_sinks"] = dsink

    gy, dz, dy, dud, dx_re, dx_im, grads["g_ssm_out"], grads["b_glu"], dd = _ssm_out_bwd(
        dn_ssm.reshape(L, SSM_WIDTH), y, z, u_c, ct_re, ct_im, d_row, w_glu, p["g_ssm_out"])
    grads.update(w_out=grad_w_out, w_glu=_matmul_tn(dz, gy, _BF16, "grad_w_glu"))
    tokens = [grads["w_out"], grads["w_glu"]] + progress(dy)
    lam_re, lam_im, da_re, da_im = _scan_bwd(dx_re.reshape(T, SCAN_CHUNKS, N_STATE), dx_im.reshape(T, SCAN_CHUNKS, N_STATE),
                                             x_re, x_im, a_re, a_im, tokens)
    lam_re = lam_re.reshape(L, N_STATE)
    lam_im = lam_im.reshape(L, N_STATE)
    dct_re, dct_im, dbt_re, dbt_im = _ssm_weight_grads(
        dy, x_re.reshape(L, N_STATE), x_im.reshape(L, N_STATE), lam_re, lam_im, u_c)
    ssm_pack = _ssm_param_bwd(
        da_re, da_im, dbt_re, dbt_im, dct_re, dct_im,
        ssm["ssm_lambda_re"], ssm["ssm_lambda_im"], ssm["ssm_log_dt"], ssm["ssm_b_re"], ssm["ssm_b_im"],
        dd.reshape(SSM_GROUPS, SSM_GROUP))
    grads.update(ssm_pack=ssm_pack, loss=loss)
    publish(grads)

    du = _ssm_du(lam_re, lam_im, bt_re, bt_im, dud).reshape(_chunk_shape(L, SSM_WIDTH))
    dproj, grad_x, g_pre_mix = _in_proj_bwd(du, dq, dk, dv, cos_t, sin_t, x, dh1, p["g_pre_mix"], w_in, [ssm_pack])
    publish({"g_pre_mix": g_pre_mix, "w_in": _matmul_tn(dproj, hn, _BF16, "grad_w_in")})
    return grad_x


_MESH = pl.DeviceIdType.MESH
_PEERS = N_DEV - 1


def _mesh_pos():
    return lax.axis_index("x"), lax.axis_index("y"), lax.axis_index("c")


def _dev_index(px, py, pc):
    return 4 * px + 2 * py + pc


def _all_gather(shards, out_dtype, name):
    n = len(shards)

    def body(*refs):
        ins, outs, stages = refs[:n], refs[n:2 * n], refs[2 * n:3 * n]
        send_sems, recv_sems, local_sems = refs[3 * n:]
        x, y, c = _mesh_pos()
        me, sibling = (x, y, c), (x, y, 1 - c)
        chips = [(1 - x, y), (x, 1 - y), (1 - x, 1 - y)]

        def copy(w, k, block, to, src=None):
            slot = outs[w].at[_dev_index(*block)]
            return pltpu.make_async_remote_copy(
                src_ref=slot if src is None else src, dst_ref=slot,
                send_sem=send_sems.at[_PEERS * w + k], recv_sem=recv_sems.at[_PEERS * w + k],
                device_id=to, device_id_type=_MESH)

        for w in range(n):
            stages[w][...] = ins[w][...].astype(out_dtype)
        mine, first, passed = [], [], []
        for w in range(n):
            cp = pltpu.make_async_copy(stages[w], outs[w].at[_dev_index(*me)], local_sems.at[w])
            cp.start()
            mine.append(cp)
            sends = [copy(w, 0, me, sibling, src=stages[w])]
            sends += [copy(w, 1 + j, me, (*chip, c), src=stages[w]) for j, chip in enumerate(chips)]
            for cp in sends:
                cp.start()
            first += sends
        for w in range(n):
            for j, chip in enumerate(chips):
                copy(w, 1 + j, (*chip, c), me).wait_recv()
                cp = copy(w, 4 + j, (*chip, c), sibling)
                cp.start()
                passed.append(cp)
        for w in range(n):
            copy(w, 0, sibling, me).wait_recv()
            for j, chip in enumerate(chips):
                copy(w, 4 + j, (*chip, 1 - c), me).wait_recv()
        for cp in first + passed:
            cp.wait_send()
        for cp in mine:
            cp.wait()

    return pl.pallas_call(
        body, name=name,
        out_shape=[_sds((N_DEV,) + s.shape, out_dtype) for s in shards],
        in_specs=[pl.BlockSpec(memory_space=pltpu.VMEM)] * n,
        out_specs=[pl.BlockSpec(memory_space=pl.ANY)] * n,
        scratch_shapes=[pltpu.VMEM(s.shape, out_dtype) for s in shards]
        + [pltpu.SemaphoreType.DMA((_PEERS * n,)), pltpu.SemaphoreType.DMA((_PEERS * n,)),
           pltpu.SemaphoreType.DMA((n,))],
        compiler_params=pltpu.CompilerParams(vmem_limit_bytes=VMEM_LIMIT),
    )(*shards)


_HBM_SPEC = pl.BlockSpec(memory_space=pltpu.HBM)
_SEM_SPEC = pl.BlockSpec(memory_space=pltpu.SEMAPHORE)
_DATAFLOW = pltpu.SideEffectType.DATAFLOW_SIDE_EFFECTING


def _peer(x, y, c, r):
    return (x ^ ((r >> 2) & 1), y ^ ((r >> 1) & 1), c ^ (r & 1))


def _hbm(a):
    return pltpu.with_memory_space_constraint(a, pltpu.HBM)


def _send_start(sources, blocked, name):
    n = len(sources)
    lands = [lax.empty((N_DEV,) + (s.shape[1:] if blocked else s.shape), s.dtype) for s in sources]

    def body(*refs):
        srcs, zones = refs[:n], refs[n:2 * n]
        send_sems, recv_sems = refs[2 * n:3 * n], refs[3 * n:4 * n]
        token, local_sems = refs[6 * n], refs[6 * n + 1]
        x, y, c = _mesh_pos()
        me = _dev_index(x, y, c)
        local = []
        for w in range(n):
            cp = pltpu.make_async_copy(srcs[w].at[me] if blocked else srcs[w], zones[w].at[me], local_sems.at[w])
            cp.start()
            local.append(cp)
            for r in range(1, N_DEV):
                peer = _peer(x, y, c, r)
                pltpu.make_async_remote_copy(
                    src_ref=srcs[w].at[_dev_index(*peer)] if blocked else srcs[w], dst_ref=zones[w].at[me],
                    send_sem=send_sems[w].at[r - 1], recv_sem=recv_sems[w].at[r - 1],
                    device_id=peer, device_id_type=_MESH).start()
        for cp in local:
            cp.wait()
        token[...] = jnp.zeros_like(token)

    sems = [pltpu.SemaphoreType.DMA((_PEERS,))] * (2 * n)
    out = pl.pallas_call(
        body, name=name,
        out_shape=sems + [pltpu.HBM(a.shape, a.dtype) for a in list(sources) + lands] + [_sds((8, 128), _F32)],
        in_specs=[_HBM_SPEC] * (2 * n),
        out_specs=[_SEM_SPEC] * (2 * n) + [_HBM_SPEC] * (2 * n) + [pl.BlockSpec(memory_space=pltpu.VMEM)],
        input_output_aliases={i: 2 * n + i for i in range(2 * n)},
        scratch_shapes=[pltpu.SemaphoreType.DMA((n,))],
        compiler_params=pltpu.CompilerParams(has_side_effects=_DATAFLOW),
    )(*[_hbm(a) for a in sources], *[_hbm(a) for a in lands])
    return out[:n], out[n:2 * n], out[2 * n:3 * n], out[3 * n:4 * n], out[4 * n]


def _send_wait(send_sems, recv_sems, sources, lands, after, blocked, name):
    n = len(sources)

    def body(*refs):
        srcs, zones = refs[:n], refs[n:2 * n]
        sends, recvs = refs[2 * n:3 * n], refs[3 * n:4 * n]
        x, y, c = _mesh_pos()
        for w in range(n):
            for r in range(1, N_DEV):
                peer = _peer(x, y, c, r)
                idx = _dev_index(*peer)
                cp = pltpu.make_async_remote_copy(
                    src_ref=srcs[w].at[idx] if blocked else srcs[w], dst_ref=zones[w].at[idx],
                    send_sem=sends[w].at[r - 1], recv_sem=recvs[w].at[r - 1],
                    device_id=peer, device_id_type=_MESH)
                cp.wait_send()
                cp.wait_recv()

    out = pl.pallas_call(
        body, name=name,
        out_shape=[pltpu.HBM(a.shape, a.dtype) for a in list(sources) + list(lands)],
        in_specs=[_HBM_SPEC] * (2 * n) + [_SEM_SPEC] * (2 * n) + [pl.BlockSpec(memory_space=pl.ANY)],
        out_specs=[_HBM_SPEC] * (2 * n),
        input_output_aliases={i: i for i in range(2 * n)},
        compiler_params=pltpu.CompilerParams(has_side_effects=_DATAFLOW),
    )(*sources, *lands, *send_sems, *recv_sems, after)
    return out[n:]


def _sequencer_exchange(sources, blocked, name, collective_id):
    n = len(sources)
    flags = blocked

    def body(*refs):
        srcs, zones = refs[:n], refs[n:2 * n]
        send_sems, recv_sems, local_sems = refs[2 * n:]
        x, y, c = _mesh_pos()
        me = _dev_index(x, y, c)
        barrier = pltpu.get_barrier_semaphore()
        for r in range(1, N_DEV):
            pl.semaphore_signal(barrier, inc=1, device_id=_peer(x, y, c, r), device_id_type=_MESH)
        pl.semaphore_wait(barrier, _PEERS)
        local, sends, recvs = [], [], []
        for w in range(n):
            cp = pltpu.make_async_copy(srcs[w].at[me] if flags[w] else srcs[w], zones[w].at[me], local_sems.at[w])
            cp.start()
            local.append(cp)
            for r in range(1, N_DEV):
                peer = _peer(x, y, c, r)
                idx = _dev_index(*peer)
                k = _PEERS * w + r - 1
                src = srcs[w].at[idx] if flags[w] else srcs[w]
                send = pltpu.make_async_remote_copy(
                    src_ref=src, dst_ref=zones[w].at[me], send_sem=send_sems.at[k], recv_sem=recv_sems.at[k],
                    device_id=peer, device_id_type=_MESH)
                send.start()
                sends.append(send)
                recvs.append(pltpu.make_async_remote_copy(
                    src_ref=src, dst_ref=zones[w].at[idx], send_sem=send_sems.at[k], recv_sem=recv_sems.at[k],
                    device_id=peer, device_id_type=_MESH))
        for cp in recvs:
            cp.wait_recv()
        for cp in sends:
            cp.wait_send()
        for cp in local:
            cp.wait()

    return pl.kernel(
        body, name=name,
        out_type=[_sds((N_DEV,) + (s.shape[1:] if f else s.shape), s.dtype) for s, f in zip(sources, flags)],
        mesh=plsc.ScalarSubcoreMesh(axis_name="sequencer", num_cores=1),
        scratch_types=[pltpu.SemaphoreType.DMA((_PEERS * n,)), pltpu.SemaphoreType.DMA((_PEERS * n,)),
                       pltpu.SemaphoreType.DMA((n,))],
        compiler_params=pltpu.CompilerParams(collective_id=collective_id),
    )(*sources)


def _sequencer_gather(shards, name, collective_id):
    n = len(shards)
    fan = 4

    def body(*refs):
        srcs, zones = refs[:n], refs[n:2 * n]
        send_sems, recv_sems, local_sems = refs[2 * n:]
        x, y, c = _mesh_pos()
        me, sibling = (x, y, c), (x, y, 1 - c)
        chips = [(1 - x, y), (x, 1 - y), (1 - x, 1 - y)]
        barrier = pltpu.get_barrier_semaphore()
        for peer in [sibling] + [(*chip, c) for chip in chips]:
            pl.semaphore_signal(barrier, inc=1, device_id=peer, device_id_type=_MESH)
        pl.semaphore_wait(barrier, fan)

        def copy(w, k, block, to, src=None):
            slot = zones[w].at[_dev_index(*block)]
            return pltpu.make_async_remote_copy(
                src_ref=slot if src is None else src, dst_ref=slot,
                send_sem=send_sems.at[_PEERS * w + k], recv_sem=recv_sems.at[_PEERS * w + k],
                device_id=to, device_id_type=_MESH)

        mine, first, passed = [], [], []
        for w in range(n):
            cp = pltpu.make_async_copy(srcs[w], zones[w].at[_dev_index(*me)], local_sems.at[w])
            cp.start()
            mine.append(cp)
            sends = [copy(w, 0, me, sibling, src=srcs[w])]
            sends += [copy(w, 1 + j, me, (*chip, c), src=srcs[w]) for j, chip in enumerate(chips)]
            for cp in sends:
                cp.start()
            first += sends
        for w in range(n):
            for j, chip in enumerate(chips):
                copy(w, 1 + j, (*chip, c), me).wait_recv()
                cp = copy(w, fan + j, (*chip, c), sibling)
                cp.start()
                passed.append(cp)
        for w in range(n):
            copy(w, 0, sibling, me).wait_recv()
            for j, chip in enumerate(chips):
                copy(w, fan + j, (*chip, 1 - c), me).wait_recv()
        for cp in first + passed:
            cp.wait_send()
        for cp in mine:
            cp.wait()

    return pl.kernel(
        body, name=name, out_type=[_sds((N_DEV,) + s.shape, s.dtype) for s in shards],
        mesh=plsc.ScalarSubcoreMesh(axis_name="sequencer", num_cores=1),
        scratch_types=[pltpu.SemaphoreType.DMA((_PEERS * n,)), pltpu.SemaphoreType.DMA((_PEERS * n,)),
                       pltpu.SemaphoreType.DMA((n,))],
        compiler_params=pltpu.CompilerParams(collective_id=collective_id),
    )(*shards)


N_CHIPS = N_DEV // 2


def _sequencer_pair_exchange(sources, name, collective_id):
    n = len(sources)

    def body(*refs):
        srcs, zones = refs[:n], refs[n:2 * n]
        send_sems, recv_sems = refs[2 * n:]
        x, y, c = _mesh_pos()
        sibling = (x, y, 1 - c)
        barrier = pltpu.get_barrier_semaphore()
        pl.semaphore_signal(barrier, inc=1, device_id=sibling, device_id_type=_MESH)
        pl.semaphore_wait(barrier, 1)
        copies = []
        for w in range(n):
            for j in range(N_CHIPS):
                k = N_CHIPS * w + j
                cp = pltpu.make_async_remote_copy(
                    src_ref=srcs[w].at[2 * j + 1 - c], dst_ref=zones[w].at[j],
                    send_sem=send_sems.at[k], recv_sem=recv_sems.at[k], device_id=sibling, device_id_type=_MESH)
                cp.start()
                copies.append(cp)
        for cp in copies:
            cp.wait_recv()
        for cp in copies:
            cp.wait_send()

    return pl.kernel(
        body, name=name, out_type=[_sds((N_CHIPS,) + s.shape[1:], s.dtype) for s in sources],
        mesh=plsc.ScalarSubcoreMesh(axis_name="sequencer", num_cores=1),
        scratch_types=[pltpu.SemaphoreType.DMA((N_CHIPS * n,)), pltpu.SemaphoreType.DMA((N_CHIPS * n,))],
        compiler_params=pltpu.CompilerParams(collective_id=collective_id),
    )(*sources)


def _pair_sum(source, received, core, name, tokens=()):
    _, rows, cols = source.shape
    tr = _row_tile(rows)

    def body(core_ref, s_ref, r_ref, o_ref):
        c = core_ref[0]
        for j in range(N_CHIPS):
            o_ref[j] = (s_ref[2 * j + c].astype(_F32) + r_ref[j].astype(_F32)).astype(o_ref.dtype)

    return _call(body, (rows // tr,),
                 [pl.BlockSpec(memory_space=pltpu.SMEM), pl.BlockSpec((N_DEV, tr, cols), lambda i: (0, i, 0)),
                  pl.BlockSpec((N_CHIPS, tr, cols), lambda i: (0, i, 0))],
                 pl.BlockSpec((N_CHIPS, tr, cols), lambda i: (0, i, 0)),
                 _sds((N_CHIPS, rows, cols), source.dtype), name, tokens=tokens)(core, source, received)


def _sequencer_chip_exchange(partials, name, collective_id):
    n = len(partials)
    others = N_CHIPS - 1

    def body(*refs):
        srcs, zones = refs[:n], refs[n:2 * n]
        send_sems, recv_sems, local_sems = refs[2 * n:]
        x, y, c = _mesh_pos()
        mine = 2 * x + y
        peers = [(x ^ (r >> 1), y ^ (r & 1), c) for r in range(1, N_CHIPS)]
        barrier = pltpu.get_barrier_semaphore()
        for peer in peers:
            pl.semaphore_signal(barrier, inc=1, device_id=peer, device_id_type=_MESH)
        pl.semaphore_wait(barrier, others)
        local, sends, recvs = [], [], []
        for w in range(n):
            cp = pltpu.make_async_copy(srcs[w].at[mine], zones[w].at[mine], local_sems.at[w])
            cp.start()
            local.append(cp)
            for r, peer in enumerate(peers):
                theirs = 2 * peer[0] + peer[1]
                k = others * w + r
                send = pltpu.make_async_remote_copy(
                    src_ref=srcs[w].at[theirs], dst_ref=zones[w].at[mine],
                    send_sem=send_sems.at[k], recv_sem=recv_sems.at[k], device_id=peer, device_id_type=_MESH)
                send.start()
                sends.append(send)
                recvs.append(pltpu.make_async_remote_copy(
                    src_ref=srcs[w].at[theirs], dst_ref=zones[w].at[theirs],
                    send_sem=send_sems.at[k], recv_sem=recv_sems.at[k], device_id=peer, device_id_type=_MESH))
        for cp in recvs:
            cp.wait_recv()
        for cp in sends:
            cp.wait_send()
        for cp in local:
            cp.wait()

    return pl.kernel(
        body, name=name, out_type=[_sds(s.shape, s.dtype) for s in partials],
        mesh=plsc.ScalarSubcoreMesh(axis_name="sequencer", num_cores=1),
        scratch_types=[pltpu.SemaphoreType.DMA((others * n,)), pltpu.SemaphoreType.DMA((others * n,)),
                       pltpu.SemaphoreType.DMA((n,))],
        compiler_params=pltpu.CompilerParams(collective_id=collective_id),
    )(*partials)


def _row_tile(rows):
    return next(t for t in range(min(rows, 256), 0, -16) if rows % t == 0)


def _sum_parts(parts, name, tokens=()):
    _, rows, cols = parts.shape
    tr = _row_tile(rows)

    def body(p_ref, g_ref):
        g = p_ref[0].astype(_F32)
        for s in range(1, N_DEV):
            g = g + p_ref[s].astype(_F32)
        g_ref[...] = g

    return _call(body, (rows // tr,), [pl.BlockSpec((N_DEV, tr, cols), lambda i: (0, i, 0))],
                 _rows(tr, cols), _sds((rows, cols), _F32), name, tokens=tokens)(parts)


def _adam_update(g, w, m, v):
    new_m = ADAM_B1 * m + (1.0 - ADAM_B1) * g
    new_v = ADAM_B2 * v + (1.0 - ADAM_B2) * (g * g)
    m_hat = new_m / (1.0 - ADAM_B1 ** ADAM_STEP)
    v_hat = new_v / (1.0 - ADAM_B2 ** ADAM_STEP)
    return -ADAM_LR * (m_hat / (jnp.sqrt(v_hat) + ADAM_EPS) + ADAM_WD * w), new_m, new_v


def _adamw_small(parts, items, sums, name, tokens=()):
    n_p, n_i = len(parts), len(items)

    def body(*refs):
        p_refs, state, outs = refs[:n_p], refs[n_p:n_p + 3 * n_i], refs[n_p + 3 * n_i:]

        def total(part, rows, cols):
            shift = cols.start % _LANES
            window = slice(cols.start - shift, cols.start - shift + _LANES) if shift else cols
            n_rows = rows.stop - rows.start
            narrow = p_refs[part].dtype.itemsize < 4 and n_rows % _PACK_TILE
            tile = slice(rows.start, rows.start + _PACK_TILE) if narrow else rows
            g = p_refs[part][0, tile, window].astype(_F32)
            for s in range(1, N_DEV):
                g = g + p_refs[part][s, tile, window].astype(_F32)
            g = g[:n_rows] if narrow else g
            return pltpu.roll(g, _LANES - shift, 1)[:, :cols.stop - cols.start] if shift else g

        for i, (part, rows, cols, _, _, _) in enumerate(items):
            g = total(part, rows, cols)
            w_ref, m_ref, v_ref = state[3 * i:3 * i + 3]
            delta, new_m, new_v = _adam_update(g, w_ref[...], m_ref[...], v_ref[...])
            outs[4 * i][...] = g
            outs[4 * i + 1][...] = delta
            outs[4 * i + 2][...] = new_m
            outs[4 * i + 3][...] = new_v
        for j, (part, rows, cols) in enumerate(sums):
            outs[4 * n_i + j][...] = total(part, rows, cols)

    ins = list(parts) + [a for item in items for a in item[3:]]
    out_shapes = [item[3].shape for item in items for _ in range(4)]
    out_shapes += [(rows.stop - rows.start, cols.stop - cols.start) for _, rows, cols in sums]
    out = _call(body, (1,), [_whole(a.shape) for a in ins], [_whole(s) for s in out_shapes],
                [_sds(s, _F32) for s in out_shapes], name, tokens=tokens)(*ins)
    return [out[4 * i:4 * i + 4] for i in range(n_i)], out[4 * n_i:]


def _adamw(parts, w, m, v, name, tokens=()):
    rows, cols = w.shape
    tr = _row_tile(rows)
    n_parts = parts.shape[0]

    def body(p_ref, w_ref, m_ref, v_ref, g_ref, d_ref, nm_ref, nv_ref):
        g = p_ref[0].astype(_F32)
        for s in range(1, n_parts):
            g = g + p_ref[s].astype(_F32)
        new_m = ADAM_B1 * m_ref[...] + (1.0 - ADAM_B1) * g
        new_v = ADAM_B2 * v_ref[...] + (1.0 - ADAM_B2) * (g * g)
        m_hat = new_m / (1.0 - ADAM_B1 ** ADAM_STEP)
        v_hat = new_v / (1.0 - ADAM_B2 ** ADAM_STEP)
        g_ref[...] = g
        d_ref[...] = -ADAM_LR * (m_hat / (jnp.sqrt(v_hat) + ADAM_EPS) + ADAM_WD * w_ref[...])
        nm_ref[...] = new_m
        nv_ref[...] = new_v

    blk = _rows(tr, cols)
    return _call(body, (rows // tr,),
                 [pl.BlockSpec((n_parts, tr, cols), lambda i: (0, i, 0)), blk, blk, blk],
                 [blk] * 4, [_sds((rows, cols), _F32)] * 4, name, tokens=tokens)(parts, w, m, v)


_SMALL = ("g_pre_mix", "ssm_lambda_re", "ssm_lambda_im", "ssm_log_dt", "ssm_b_re", "ssm_b_im",
          "ssm_c_re", "ssm_c_im", "ssm_d", "b_glu", "attn_sinks", "g_ssm_out", "g_attn_out",
          "g_post_mix", "g_pre_ffn", "g_post_ffn")
_BIG = ("w_in", "w_glu", "w_out", "w_gate_up", "w_down")
_WEIGHTS = ("g_pre_mix", "w_in", "ssm_lambda_re", "ssm_lambda_im", "ssm_log_dt", "ssm_b_re", "ssm_b_im",
            "ssm_c_re", "ssm_c_im", "ssm_d", "w_glu", "b_glu", "attn_sinks", "g_ssm_out", "g_attn_out",
            "w_out", "g_post_mix", "g_pre_ffn", "w_gate_up", "w_down", "g_post_ffn")
_LANES = 128


_SHAPE_2D = {
    "g_pre_mix": (1, D_MODEL), "ssm_lambda_re": (SSM_GROUPS, SSM_STATE), "ssm_lambda_im": (SSM_GROUPS, SSM_STATE),
    "ssm_log_dt": (1, SSM_GROUPS), "ssm_b_re": (SSM_WIDTH, SSM_STATE), "ssm_b_im": (SSM_WIDTH, SSM_STATE),
    "ssm_c_re": (SSM_WIDTH, SSM_STATE), "ssm_c_im": (SSM_WIDTH, SSM_STATE), "ssm_d": (SSM_GROUPS, SSM_GROUP),
    "b_glu": (1, 2 * SSM_WIDTH), "attn_sinks": (1, N_Q_HEADS), "g_ssm_out": (1, SSM_WIDTH),
    "g_attn_out": (1, ATTN_WIDTH), "g_post_mix": (1, D_MODEL), "g_pre_ffn": (1, D_MODEL), "g_post_ffn": (1, D_MODEL)}
_ROW_WIDTH = {"g_pre_mix": D_MODEL, "b_glu": 2 * SSM_WIDTH, "attn_sinks": _LANES, "g_ssm_out": SSM_WIDTH,
              "g_attn_out": ATTN_WIDTH, "g_post_mix": D_MODEL, "g_pre_ffn": D_MODEL, "g_post_ffn": D_MODEL,
              "loss": _LANES}
_DENSE = ()
_PER_GROUP_TRANSPOSED = ("ssm_b_re", "ssm_b_im")


def _to_2d(name, a):
    if name in _PER_GROUP_TRANSPOSED:
        a = a.reshape(SSM_GROUPS, SSM_STATE, SSM_GROUP).transpose(0, 2, 1)
    return a.reshape(_SHAPE_2D[name])


def _from_2d(name, a, shape):
    if name in _PER_GROUP_TRANSPOSED:
        a = a.reshape(SSM_GROUPS, SSM_GROUP, SSM_STATE).transpose(0, 2, 1)
    return a.reshape(shape)


def _row_slots(names):
    slots, row, col = {}, 0, 0
    for n in names:
        width = _ROW_WIDTH[n]
        if col + width > D_MODEL:
            row, col = row + 1, 0
        slots[n] = (row, col, width)
        col += width
    return slots


def _stack_rows(named, slots):
    n_rows = -(-(max(r for r, _, _ in slots.values()) + 1) // 8) * 8
    lines = []
    for r in range(n_rows):
        pieces = [named[n] for n, (row, _, _) in slots.items() if row == r]
        used = sum(p.shape[1] for p in pieces)
        if used < D_MODEL:
            pieces.append(jnp.zeros((1, D_MODEL - used), _F32))
        lines.append(jnp.concatenate(pieces, axis=1) if len(pieces) > 1 else pieces[0])
    return jnp.concatenate(lines, axis=0)


def kernel(x, positions, g_pre_mix, w_in, ssm_lambda_re, ssm_lambda_im, ssm_log_dt, ssm_b_re, ssm_b_im, ssm_c_re, ssm_c_im, ssm_d, w_glu, b_glu, attn_sinks, g_ssm_out, g_attn_out, w_out, g_post_mix, g_pre_ffn, w_gate_up, w_down, g_post_ffn, loss_target, m_g_pre_mix, m_w_in, m_ssm_lambda_re, m_ssm_lambda_im, m_ssm_log_dt, m_ssm_b_re, m_ssm_b_im, m_ssm_c_re, m_ssm_c_im, m_ssm_d, m_w_glu, m_b_glu, m_attn_sinks, m_g_ssm_out, m_g_attn_out, m_w_out, m_g_post_mix, m_g_pre_ffn, m_w_gate_up, m_w_down, m_g_post_ffn, v_g_pre_mix, v_w_in, v_ssm_lambda_re, v_ssm_lambda_im, v_ssm_log_dt, v_ssm_b_re, v_ssm_b_im, v_ssm_c_re, v_ssm_c_im, v_ssm_d, v_w_glu, v_b_glu, v_attn_sinks, v_g_ssm_out, v_g_attn_out, v_w_out, v_g_post_mix, v_g_pre_ffn, v_w_gate_up, v_w_down, v_g_post_ffn):
    w = dict(g_pre_mix=g_pre_mix, w_in=w_in, ssm_lambda_re=ssm_lambda_re, ssm_lambda_im=ssm_lambda_im,
             ssm_log_dt=ssm_log_dt, ssm_b_re=ssm_b_re, ssm_b_im=ssm_b_im, ssm_c_re=ssm_c_re, ssm_c_im=ssm_c_im,
             ssm_d=ssm_d, w_glu=w_glu, b_glu=b_glu, attn_sinks=attn_sinks, g_ssm_out=g_ssm_out,
             g_attn_out=g_attn_out, w_out=w_out, g_post_mix=g_post_mix, g_pre_ffn=g_pre_ffn,
             w_gate_up=w_gate_up, w_down=w_down, g_post_ffn=g_post_ffn)
    m = dict(g_pre_mix=m_g_pre_mix, w_in=m_w_in, ssm_lambda_re=m_ssm_lambda_re, ssm_lambda_im=m_ssm_lambda_im,
             ssm_log_dt=m_ssm_log_dt, ssm_b_re=m_ssm_b_re, ssm_b_im=m_ssm_b_im, ssm_c_re=m_ssm_c_re,
             ssm_c_im=m_ssm_c_im, ssm_d=m_ssm_d, w_glu=m_w_glu, b_glu=m_b_glu, attn_sinks=m_attn_sinks,
             g_ssm_out=m_g_ssm_out, g_attn_out=m_g_attn_out, w_out=m_w_out, g_post_mix=m_g_post_mix,
             g_pre_ffn=m_g_pre_ffn, w_gate_up=m_w_gate_up, w_down=m_w_down, g_post_ffn=m_g_post_ffn)
    v = dict(g_pre_mix=v_g_pre_mix, w_in=v_w_in, ssm_lambda_re=v_ssm_lambda_re, ssm_lambda_im=v_ssm_lambda_im,
             ssm_log_dt=v_ssm_log_dt, ssm_b_re=v_ssm_b_re, ssm_b_im=v_ssm_b_im, ssm_c_re=v_ssm_c_re,
             ssm_c_im=v_ssm_c_im, ssm_d=v_ssm_d, w_glu=v_w_glu, b_glu=v_b_glu, attn_sinks=v_attn_sinks,
             g_ssm_out=v_g_ssm_out, g_attn_out=v_g_attn_out, w_out=v_w_out, g_post_mix=v_g_post_mix,
             g_pre_ffn=v_g_pre_ffn, w_gate_up=v_w_gate_up, w_down=v_w_down, g_post_ffn=v_g_post_ffn)

    transposed = ("w_in", "w_glu", "w_gate_up")
    native_transposed = ("w_in", "w_gate_up")
    shard = {n: (w[n][0].T if n in transposed else w[n][0]).astype(_BF16) for n in _BIG}
    gathered = {}
    for names, lands in (
            (("w_in",), _sequencer_exchange([shard["w_in"]], [False], "gather_w_in", 1)),
            (("w_glu", "w_out"), _sequencer_exchange([shard["w_glu"], shard["w_out"]], [False] * 2, "gather_mix", 2)),
            (("w_gate_up", "w_down"), _sequencer_gather([shard["w_gate_up"], shard["w_down"]], "gather_ffn", 3))):
        gathered.update({n: a.reshape(-1, a.shape[2]) for n, a in zip(names, lands)})

    def fetch(names, after):
        del after
        return [gathered[n] for n in names]

    sent = []
    ids = iter(range(4, 16))
    two_step = {}

    def publish(named):
        big = [n for n in named if n in _BIG]
        if set(big) == {"w_gate_up", "w_down"}:
            blocks = [named[n].reshape(N_DEV, -1, named[n].shape[1]) for n in big]
            two_step.update(names=big, blocks=blocks,
                            received=_sequencer_pair_exchange(blocks, "grads_pair", next(ids)))
            return [named[n] for n in big]
        rows = [n for n in named if n in _ROW_WIDTH]
        dense = [n for n in named if n in _DENSE]
        plain = [n for n in named if n not in big + rows + dense]
        sources = [named[n].reshape(N_DEV, -1, named[n].shape[1]) for n in big]
        slots = _row_slots(rows)
        if rows:
            sources.append(_stack_rows(named, slots))
        sources += [named[n].reshape(-1, _LANES) for n in dense] + [named[n] for n in plain]
        flags = [True] * len(big) + [False] * (len(sources) - len(big))
        cid = next(ids)
        sent.append((big, slots, dense, plain, _sequencer_exchange(sources, flags, "grads_%d" % cid, cid)))
        return [named[n] for n in big]

    def progress(after):
        core = lax.axis_index("c").astype(jnp.int32).reshape(1)
        partials = [_pair_sum(b, r, core, "pair_sum_" + n, [after])
                    for n, b, r in zip(two_step["names"], two_step["blocks"], two_step["received"])]
        sent.append((two_step["names"], {}, [], [], _sequencer_chip_exchange(partials, "grads_chips", next(ids))))
        return partials

    p = {n: w[n] for n in _SMALL}
    grad_x = _local_step(x[0], positions[0], loss_target[0], p, fetch, publish, progress)

    state = {n: [_to_2d(n, a) for a in (w[n], m[n], v[n])] for n in _SMALL}
    result = {}
    total_loss = None
    chain = []
    for big, slots, dense, plain, lands in sent:
        lands = list(lands)
        after = list(chain)
        for name in big:
            part = lands.pop(0)
            if name in native_transposed:
                updated = _adamw(part, w[name][0].T, m[name][0].T, v[name][0].T, "adamw_" + name, after)
                result[name] = [a.T[None] for a in updated]
                chain.append(updated[3])
                continue
            if name in transposed:
                part = _sum_parts(part, "sum_" + name, after).T[None]
            updated = _adamw(part, w[name][0], m[name][0], v[name][0], "adamw_" + name, after)
            result[name] = [a[None] for a in updated]
            chain.append(updated[3])
        parts, items, sums, names = [], [], [], []
        if slots:
            parts.append(lands.pop(0))
            for name, (row, col, _) in slots.items():
                if name == "loss":
                    sums.append((0, slice(row, row + 1), slice(col, col + _LANES)))
                else:
                    items.append((0, slice(row, row + 1), slice(col, col + _SHAPE_2D[name][1]), *state[name]))
                    names.append(name)
        for name in dense:
            part = lands.pop(0).reshape((N_DEV,) + _SHAPE_2D[name])
            result[name] = _adamw(part, *state[name], "adamw_" + name, after)
            chain.append(result[name][3])
        for name in plain:
            packed = _SSM_PACK if name == "ssm_pack" else {name: (0, _SHAPE_2D[name][0], 0, _SHAPE_2D[name][1])}
            for member, (first, rows_n, lane, cols_n) in packed.items():
                items.append((len(parts), slice(first, first + rows_n), slice(lane, lane + cols_n), *state[member]))
                names.append(member)
            parts.append(lands.pop(0))
        if items:
            updated, summed = _adamw_small(parts, items, sums, "adamw_small_" + names[0], after)
            chain.append(updated[0][3])
            result.update(dict(zip(names, updated)))
            if summed:
                total_loss = summed[0][0, 0]

    out = [total_loss, grad_x[None]]
    for kind in range(4):
        out += [_from_2d(n, result[n][kind], w[n].shape) for n in _WEIGHTS]
    return tuple(out)
```

```python
import functools
import math

import numpy as np
import jax
import jax.numpy as jnp
from jax import lax
from jax.experimental import pallas as pl
from jax.experimental.pallas import tpu as pltpu
from jax.experimental.pallas import tpu_sc as plsc

D_MODEL = 1024
SSM_WIDTH = 512
SSM_GROUP = 16
SSM_GROUPS = 32
SSM_STATE = 64
N_STATE = SSM_GROUPS * SSM_STATE
ATTN_WIDTH = 512
HEAD_DIM = 64
N_Q_HEADS = 8
N_KV_HEADS = 2
Q_PER_KV = 4
KV_WIDTH = 128
IN_WIDTH = 1280
BLOCK = 128
ROPE_DIM = 16
ROPE_THETA = 500000.0
D_FF = 2816
NORM_EPS = 1e-6
MASK_VALUE = -1e30
ADAM_LR = 0.001
ADAM_B1 = 0.9
ADAM_B2 = 0.999
ADAM_EPS = 1e-08
ADAM_WD = 0.01
ADAM_STEP = 10

N_DEV = 8
SCAN_CHUNKS = 8
SCAN_COLS = 512
SCAN_UNROLL = 8
TOKEN_TILE = 256
VMEM_LIMIT = 56 * 1024 * 1024

_F32 = jnp.float32
_BF16 = jnp.bfloat16
_MXU = jnp.bfloat16

_NN = ((1,), (0,))
_NT = ((1,), (1,))
_TN = ((0,), (0,))


def _dot(a, b, dims):
    return lax.dot_general(a.astype(_MXU), b.astype(_MXU), (dims, ((), ())),
                           preferred_element_type=_F32)


def _dot_exact(a, b, dims):
    return lax.dot_general(a.astype(_F32), b.astype(_F32), (dims, ((), ())),
                           precision=lax.Precision.HIGHEST, preferred_element_type=_F32)


def _iota(shape, dim):
    return lax.broadcasted_iota(jnp.int32, shape, dim)


def _rms_fwd(x, g):
    r = lax.rsqrt(jnp.mean(x * x, axis=-1, keepdims=True) + NORM_EPS)
    return x * r * g, r


def _rms_bwd(dy, x, g, r):
    a = dy * g
    xn = x * r
    dx = r * (a - xn * jnp.mean(a * xn, axis=-1, keepdims=True))
    dg = jnp.sum(dy * xn, axis=0, keepdims=True)
    return dx, dg


def _call(body, grid, in_specs, out_specs, out_shape, name, scratch=(), tokens=()):
    params = pltpu.CompilerParams(dimension_semantics=("arbitrary",) * len(grid),
                                  vmem_limit_bytes=VMEM_LIMIT)
    n_in, n_tok = len(in_specs), len(tokens)

    def run(*refs):
        return body(*refs[:n_in], *refs[n_in + n_tok:])

    call = pl.pallas_call(run, grid=grid,
                          in_specs=list(in_specs) + [pl.BlockSpec(memory_space=pl.ANY)] * n_tok,
                          out_specs=out_specs, out_shape=out_shape, scratch_shapes=list(scratch),
                          compiler_params=params, name=name)
    return lambda *args: call(*args, *tokens)


def _rows(tm, n):
    return pl.BlockSpec((tm, n), lambda i: (i, 0))


def _whole(shape):
    nd = len(shape)
    return pl.BlockSpec(shape, lambda i: (0,) * nd)


def _sds(shape, dtype):
    return jax.ShapeDtypeStruct(shape, dtype)


def _tile(L):
    return min(TOKEN_TILE, L)


def _chunk_tile(L):
    return L // SCAN_CHUNKS


def _chunk_block(L, n):
    return pl.BlockSpec((_chunk_tile(L), n), lambda i: (0, i))


def _chunk_shape(L, n):
    return (_chunk_tile(L), SCAN_CHUNKS * n)


def _accumulate(ref, val, first):
    @pl.when(first)
    def _():
        ref[...] = val

    @pl.when(jnp.logical_not(first))
    def _():
        ref[...] += val


def _rope_rows():
    half = ROPE_DIM // 2
    inv = (np.float32(ROPE_THETA) ** (-np.arange(half, dtype=np.float32) * np.float32(2.0) / np.float32(ROPE_DIM))).astype(np.float32)
    col = np.arange(KV_WIDTH) % HEAD_DIM
    freq = np.where(col < ROPE_DIM, inv[col % half], 0.0).astype(np.float32)
    sign = np.where(col < half, -1.0, np.where(col < ROPE_DIM, 1.0, 0.0)).astype(np.float32)
    return freq[None, :], sign[None, :]


def _rope_tables(pos_col):
    L = pos_col.shape[0]
    tm = _tile(L)
    freq, sign = _rope_rows()

    def body(pos_ref, freq_ref, sign_ref, cos_ref, sin_ref):
        ang = pos_ref[...].astype(_F32) * freq_ref[...]
        cos_ref[...] = jnp.cos(ang)
        sin_ref[...] = jnp.sin(ang) * sign_ref[...]

    return _call(body, (L // tm,),
                 [_rows(tm, 1), _whole((1, KV_WIDTH)), _whole((1, KV_WIDTH))],
                 [_rows(tm, KV_WIDTH), _rows(tm, KV_WIDTH)],
                 [_sds((L, KV_WIDTH), _F32)] * 2, "rope_tables")(pos_col, jnp.asarray(freq), jnp.asarray(sign))


def _widen(t, width):
    return t if width == KV_WIDTH else jnp.concatenate([t] * (width // KV_WIDTH), axis=1)


def _rope_partner(t):
    w = t.shape[1]
    in_head = _iota((1, w), 1) & (HEAD_DIM - 1)
    second = jnp.where(in_head < ROPE_DIM, pltpu.roll(t, ROPE_DIM // 2, 1), 0.0)
    return jnp.where(in_head < ROPE_DIM // 2, pltpu.roll(t, w - ROPE_DIM // 2, 1), second)


def _rope_apply(t, cos_t, sin_t):
    w = t.shape[1]
    return t * _widen(cos_t, w) + _rope_partner(t) * _widen(sin_t, w)


def _rope_transpose(dt, cos_t, sin_t):
    w = dt.shape[1]
    return dt * _widen(cos_t, w) + _rope_partner(dt * _widen(sin_t, w))


def _in_proj(x, g_pre_mix, w_in, cos_t, sin_t):
    L = x.shape[0]
    tm = _chunk_tile(L)

    def body(x_ref, g_ref, w_ref, cos_ref, sin_ref, hn_ref, u_ref, q_ref, k_ref, v_ref):
        hn, _ = _rms_fwd(x_ref[...], g_ref[...])
        hn = hn.astype(_BF16)
        hn_ref[...] = hn
        proj = _dot(hn, w_ref[...], _NT)
        u_ref[...] = proj[:, :SSM_WIDTH]
        q = proj[:, SSM_WIDTH:SSM_WIDTH + ATTN_WIDTH]
        k = proj[:, SSM_WIDTH + ATTN_WIDTH:SSM_WIDTH + ATTN_WIDTH + KV_WIDTH]
        cos_v, sin_v = cos_ref[...], sin_ref[...]
        q_ref[...] = _rope_apply(q, cos_v, sin_v).astype(_BF16)
        k_ref[...] = _rope_apply(k, cos_v, sin_v).astype(_BF16)
        v_ref[...] = proj[:, SSM_WIDTH + ATTN_WIDTH + KV_WIDTH:].astype(_BF16)

    return _call(body, (L // tm,),
                 [_rows(tm, D_MODEL), _whole((1, D_MODEL)), _whole((IN_WIDTH, D_MODEL)),
                  _rows(tm, KV_WIDTH), _rows(tm, KV_WIDTH)],
                 [_rows(tm, D_MODEL), _chunk_block(L, SSM_WIDTH), _rows(tm, ATTN_WIDTH),
                  _rows(tm, KV_WIDTH), _rows(tm, KV_WIDTH)],
                 [_sds((L, D_MODEL), _BF16), _sds(_chunk_shape(L, SSM_WIDTH), _F32), _sds((L, ATTN_WIDTH), _BF16),
                  _sds((L, KV_WIDTH), _BF16), _sds((L, KV_WIDTH), _BF16)],
                 "in_proj")(x, g_pre_mix, w_in, cos_t, sin_t)


def _s5_discretize(lam_re, lam_im, log_dt):
    lr = jnp.minimum(lam_re, -1e-4)
    li = lam_im
    dt = jnp.exp(log_dt)
    mag = jnp.exp(lr * dt)
    ar = mag * jnp.cos(li * dt)
    ai = mag * jnp.sin(li * dt)
    den = lr * lr + li * li
    fr = ((ar - 1.0) * lr + ai * li) / den
    fi = (ai * lr - (ar - 1.0) * li) / den
    return ar, ai, fr, fi


def _s5_bbar(lam_re, lam_im, log_dt, b_re, b_im):
    ar, ai, fr, fi = _s5_discretize(lam_re, lam_im, log_dt)
    return ar, ai, fr * b_re - fi * b_im, fr * b_im + fi * b_re


def _spread_masks():
    e16 = (_iota((SSM_GROUP, SSM_WIDTH), 1) & (SSM_GROUP - 1)) == _iota((SSM_GROUP, SSM_WIDTH), 0)
    e64 = (_iota((SSM_STATE, N_STATE), 1) & (SSM_STATE - 1)) == _iota((SSM_STATE, N_STATE), 0)
    mask_b = (_iota((N_STATE, SSM_WIDTH), 0) >> 6) == (_iota((N_STATE, SSM_WIDTH), 1) >> 4)
    mask_c = (_iota((SSM_WIDTH, N_STATE), 0) >> 4) == (_iota((SSM_WIDTH, N_STATE), 1) >> 6)
    return e16.astype(_F32), e64.astype(_F32), mask_b, mask_c


SUPER = 4
SB_STATE = N_STATE // SUPER
SB_WIDTH = SSM_WIDTH // SUPER


def _sb_state(k):
    return slice(SB_STATE * k, SB_STATE * (k + 1))


def _sb_width(k):
    return slice(SB_WIDTH * k, SB_WIDTH * (k + 1))


def _dt_column(log_dt_row):
    eye = _iota((SSM_GROUPS, SSM_GROUPS), 0) == _iota((SSM_GROUPS, SSM_GROUPS), 1)
    return jnp.sum(jnp.where(eye, log_dt_row, 0.0), axis=1, keepdims=True)


def _group_masks():
    e64 = ((_iota((SSM_STATE, N_STATE), 1) & (SSM_STATE - 1)) == _iota((SSM_STATE, N_STATE), 0)).astype(_F32)
    own = _iota((SSM_GROUPS, N_STATE), 0) == (_iota((SSM_GROUPS, N_STATE), 1) >> 6)
    return e64, own


def _rows_of_group():
    return ((_iota((SSM_WIDTH, SSM_GROUPS), 0) >> 4) == _iota((SSM_WIDTH, SSM_GROUPS), 1)).astype(_F32)


def _ssm_prep(lam_re, lam_im, log_dt, b_re, b_im, c_re, c_im):
    def body(lr_ref, li_ref, ld_ref, bre, bim, cre, cim, ar_ref, ai_ref, btr, bti, ctr, cti):
        ar, ai, fr, fi = _s5_discretize(lr_ref[...], li_ref[...], _dt_column(ld_ref[...]))
        e64, own = _group_masks()
        mask_c = (_iota((SSM_WIDTH, N_STATE), 0) >> 4) == (_iota((SSM_WIDTH, N_STATE), 1) >> 6)

        def to_row(t):
            return jnp.sum(jnp.where(own, _dot_exact(t, e64, _NN), 0.0), axis=0, keepdims=True)

        def fold(m):
            full = jnp.where(mask_c, _dot(m, e64, _NN), 0.0)
            return sum(full[_sb_width(k), :] for k in range(SUPER)).astype(_BF16)

        ar_ref[...] = to_row(ar)
        ai_ref[...] = to_row(ai)
        spread = _rows_of_group()
        fr_t = _dot_exact(spread, fr, _NN)
        fi_t = _dot_exact(spread, fi, _NN)
        btr[...] = fold(fr_t * bre[...] - fi_t * bim[...])
        bti[...] = fold(fr_t * bim[...] + fi_t * bre[...])
        ctr[...] = fold(cre[...])
        cti[...] = fold(cim[...])

    row = (1, N_STATE)
    ins = [lam_re, lam_im, log_dt, b_re, b_im, c_re, c_im]
    return _call(body, (1,), [_whole(a.shape) for a in ins],
                 [_whole(row), _whole(row)] + [_whole((SB_WIDTH, N_STATE))] * 4,
                 [_sds(row, _F32), _sds(row, _F32)] + [_sds((SB_WIDTH, N_STATE), _BF16)] * 4,
                 "ssm_prep")(*ins)


def _ssm_bu(u, bt_re, bt_im):
    L = u.shape[0]
    tm = _tile(L)

    def body(u_ref, br_ref, bi_ref, or_ref, oi_ref):
        for k in range(SUPER):
            ub = u_ref[:, _sb_width(k)].astype(_BF16)
            or_ref[:, _sb_state(k)] = _dot(ub, br_ref[:, _sb_state(k)], _NN)
            oi_ref[:, _sb_state(k)] = _dot(ub, bi_ref[:, _sb_state(k)], _NN)

    return _call(body, (L // tm,),
                 [_rows(tm, SSM_WIDTH), _whole((SB_WIDTH, N_STATE)), _whole((SB_WIDTH, N_STATE))],
                 [_rows(tm, N_STATE), _rows(tm, N_STATE)],
                 [_sds((L, N_STATE), _F32)] * 2, "ssm_bu")(u, bt_re, bt_im)


def _complex_power(ar, ai, n):
    def step(_, c):
        pr, pi = c
        return pr * ar - pi * ai, pr * ai + pi * ar
    return lax.fori_loop(0, n, step, (jnp.ones_like(ar), jnp.zeros_like(ai)))


def _chunk_carries(er, ei, pr, pi, reverse):
    rows = _iota(er.shape, 0)
    sr = jnp.zeros_like(pr)
    si = jnp.zeros_like(pi)
    out_r = jnp.zeros_like(er)
    out_i = jnp.zeros_like(ei)
    order = range(SCAN_CHUNKS - 1, 0, -1) if reverse else range(SCAN_CHUNKS - 1)
    for c in order:
        e_r = er[c:c + 1, :]
        e_i = ei[c:c + 1, :]
        sr, si = pr * sr - pi * si + e_r, pr * si + pi * sr + e_i
        nxt = c - 1 if reverse else c + 1
        out_r = jnp.where(rows == nxt, sr, out_r)
        out_i = jnp.where(rows == nxt, si, out_i)
    return out_r, out_i


def _scan_fwd(b_re, b_im, a_re, a_im):
    T = b_re.shape[0]
    W = SCAN_COLS
    blk = pl.BlockSpec((T, SCAN_CHUNKS, W), lambda j: (0, 0, j))
    vec = pl.BlockSpec((1, W), lambda j: (0, j))

    def body(br_ref, bi_ref, ar_ref, ai_ref, xr_ref, xi_ref):
        ar, ai = ar_ref[...], ai_ref[...]
        ar8 = jnp.broadcast_to(ar, (SCAN_CHUNKS, W))
        ai8 = jnp.broadcast_to(ai, (SCAN_CHUNKS, W))

        def local(t, c):
            cr, ci = c
            return ar8 * cr - ai8 * ci + br_ref[t], ar8 * ci + ai8 * cr + bi_ref[t]

        zero = jnp.zeros((SCAN_CHUNKS, W), _F32)
        er, ei = lax.fori_loop(0, T, local, (zero, zero), unroll=SCAN_UNROLL)
        pr, pi = _complex_power(ar, ai, T)
        sr, si = _chunk_carries(er, ei, pr, pi, reverse=False)

        def final(t, c):
            nr, ni = local(t, c)
            xr_ref[t] = nr
            xi_ref[t] = ni
            return nr, ni

        lax.fori_loop(0, T, final, (sr, si), unroll=SCAN_UNROLL)

    shape = _sds(b_re.shape, _F32)
    return _call(body, (N_STATE // W,), [blk, blk, vec, vec], [blk, blk], [shape, shape],
                 "scan_fwd")(b_re, b_im, a_re, a_im)


def _scan_bwd(dx_re, dx_im, x_re, x_im, a_re, a_im, tokens=()):
    T = dx_re.shape[0]
    W = SCAN_COLS
    blk = pl.BlockSpec((T, SCAN_CHUNKS, W), lambda j: (0, 0, j))
    vec = pl.BlockSpec((1, W), lambda j: (0, j))

    def body(dr_ref, di_ref, xr_ref, xi_ref, ar_ref, ai_ref, lr_ref, li_ref, dar_ref, dai_ref):
        ar, ai = ar_ref[...], ai_ref[...]
        ar8 = jnp.broadcast_to(ar, (SCAN_CHUNKS, W))
        ai8 = jnp.broadcast_to(ai, (SCAN_CHUNKS, W))

        def local(t, c):
            cr, ci = c
            return ar8 * cr + ai8 * ci + dr_ref[t], ar8 * ci - ai8 * cr + di_ref[t]

        zero = jnp.zeros((SCAN_CHUNKS, W), _F32)
        er, ei = lax.fori_loop(0, T, lambda k, c: local(T - 1 - k, c), (zero, zero), unroll=SCAN_UNROLL)
        pr, pi = _complex_power(ar, -ai, T)
        sr, si = _chunk_carries(er, ei, pr, pi, reverse=True)

        def grad_a(acc, nr, ni, xpr, xpi):
            return acc[0] + nr * xpr + ni * xpi, acc[1] + ni * xpr - nr * xpi

        def final(k, c):
            t = T - 1 - k
            nr, ni = local(t, c[:2])
            lr_ref[t] = nr
            li_ref[t] = ni
            gr, gi = grad_a(c[2:], nr, ni, xr_ref[t - 1], xi_ref[t - 1])
            return nr, ni, gr, gi

        cr, ci, gr, gi = lax.fori_loop(0, T - 1, final, (sr, si, zero, zero), unroll=SCAN_UNROLL)
        nr, ni = local(0, (cr, ci))
        lr_ref[0] = nr
        li_ref[0] = ni
        first = _iota((SCAN_CHUNKS, W), 0) == 0
        xpr = jnp.where(first, 0.0, pltpu.roll(xr_ref[T - 1], 1, 0))
        xpi = jnp.where(first, 0.0, pltpu.roll(xi_ref[T - 1], 1, 0))
        gr, gi = grad_a((gr, gi), nr, ni, xpr, xpi)
        dar_ref[...] = jnp.sum(gr, axis=0, keepdims=True)
        dai_ref[...] = jnp.sum(gi, axis=0, keepdims=True)

    shape = _sds(dx_re.shape, _F32)
    row = _sds((1, N_STATE), _F32)
    return _call(body, (N_STATE // W,), [blk, blk, blk, blk, vec, vec], [blk, blk, vec, vec],
                 [shape, shape, row, row], "scan_bwd", tokens=tokens)(dx_re, dx_im, x_re, x_im, a_re, a_im)


_GELU_K = math.sqrt(2.0 / math.pi)
_GELU_C = 0.044715


def _gelu(y):
    return 0.5 * y * (1.0 + jnp.tanh(_GELU_K * (y + _GELU_C * y * y * y)))


def _gelu_grad(y):
    t = jnp.tanh(_GELU_K * (y + _GELU_C * y * y * y))
    return 0.5 * (1.0 + t) + 0.5 * y * (1.0 - t * t) * _GELU_K * (1.0 + 3.0 * _GELU_C * y * y)


def _step_rows(t):
    return pl.ds(pl.multiple_of(t * SCAN_CHUNKS, SCAN_CHUNKS), SCAN_CHUNKS)


def _scan_in_place(br, bi, ar, ai, T):
    W = br.shape[1]
    ar8 = jnp.broadcast_to(ar, (SCAN_CHUNKS, W))
    ai8 = jnp.broadcast_to(ai, (SCAN_CHUNKS, W))

    def local(t, c):
        cr, ci = c
        rows = _step_rows(t)
        return ar8 * cr - ai8 * ci + br[rows, :], ar8 * ci + ai8 * cr + bi[rows, :]

    zero = jnp.zeros((SCAN_CHUNKS, W), _F32)
    er, ei = lax.fori_loop(0, T, local, (zero, zero), unroll=SCAN_UNROLL)
    pr, pi = _complex_power(ar, ai, T)
    sr, si = _chunk_carries(er, ei, pr, pi, reverse=False)

    def final(t, c):
        nr, ni = local(t, c)
        rows = _step_rows(t)
        br[rows, :] = nr
        bi[rows, :] = ni
        return nr, ni

    lax.fori_loop(0, T, final, (sr, si), unroll=SCAN_UNROLL)


def _scan_reverse_in_place(dr, di, xr, xi, ar, ai, T):
    W = dr.shape[1]
    ar8 = jnp.broadcast_to(ar, (SCAN_CHUNKS, W))
    ai8 = jnp.broadcast_to(ai, (SCAN_CHUNKS, W))

    def local(t, c):
        cr, ci = c
        rows = _step_rows(t)
        return ar8 * cr + ai8 * ci + dr[rows, :], ar8 * ci - ai8 * cr + di[rows, :]

    zero = jnp.zeros((SCAN_CHUNKS, W), _F32)
    er, ei = lax.fori_loop(0, T, lambda k, c: local(T - 1 - k, c), (zero, zero), unroll=SCAN_UNROLL)
    pr, pi = _complex_power(ar, -ai, T)
    sr, si = _chunk_carries(er, ei, pr, pi, reverse=True)

    def grad_a(acc, nr, ni, xpr, xpi):
        return acc[0] + nr * xpr + ni * xpi, acc[1] + ni * xpr - nr * xpi

    def final(k, c):
        t = T - 1 - k
        nr, ni = local(t, c[:2])
        rows = _step_rows(t)
        dr[rows, :] = nr
        di[rows, :] = ni
        before = _step_rows(t - 1)
        gr, gi = grad_a(c[2:], nr, ni, xr[before, :], xi[before, :])
        return nr, ni, gr, gi

    cr, ci, gr, gi = lax.fori_loop(0, T - 1, final, (sr, si, zero, zero), unroll=SCAN_UNROLL)
    nr, ni = local(0, (cr, ci))
    dr[_step_rows(0), :] = nr
    di[_step_rows(0), :] = ni
    first = _iota((SCAN_CHUNKS, W), 0) == 0
    last = _step_rows(T - 1)
    xpr = jnp.where(first, 0.0, pltpu.roll(xr[last, :], 1, 0))
    xpi = jnp.where(first, 0.0, pltpu.roll(xi[last, :], 1, 0))
    gr, gi = grad_a((gr, gi), nr, ni, xpr, xpi)
    return jnp.sum(gr, axis=0, keepdims=True), jnp.sum(gi, axis=0, keepdims=True)


def _ssm_super_specs(L):
    width = pl.BlockSpec((L, SB_WIDTH), lambda k: (0, k))
    matrix = pl.BlockSpec((SB_WIDTH, SB_STATE), lambda k: (0, k))
    row = pl.BlockSpec((1, SB_STATE), lambda k: (0, k))
    return width, matrix, row


def _ssm_states(u_ref, br_ref, bi_ref, ar_ref, ai_ref, xr, xi, T):
    ub = u_ref[...].astype(_BF16)
    xr[...] = _dot(ub, br_ref[...], _NN)
    xi[...] = _dot(ub, bi_ref[...], _NN)
    _scan_in_place(xr, xi, ar_ref[...], ai_ref[...], T)


def _ssm_core_fwd(u, bt_re, bt_im, ct_re, ct_im, a_re, a_im):
    L = u.shape[0]
    T = L // SCAN_CHUNKS

    def body(u_ref, br_ref, bi_ref, cr_ref, ci_ref, ar_ref, ai_ref, y_ref, xr, xi):
        _ssm_states(u_ref, br_ref, bi_ref, ar_ref, ai_ref, xr, xi, T)
        y_ref[...] = _dot(xr[...], cr_ref[...], _NT) - _dot(xi[...], ci_ref[...], _NT)

    width, matrix, row = _ssm_super_specs(L)
    return _call(body, (SUPER,), [width, matrix, matrix, matrix, matrix, row, row], width,
                 _sds((L, SSM_WIDTH), _F32), "ssm_core_fwd",
                 scratch=[pltpu.VMEM((L, SB_STATE), _F32)] * 2)(u, bt_re, bt_im, ct_re, ct_im, a_re, a_im)


def _ssm_core_bwd(u, dy, dud, bt_re, bt_im, ct_re, ct_im, a_re, a_im, tokens=()):
    L = u.shape[0]
    T = L // SCAN_CHUNKS

    def body(u_ref, dy_ref, dud_ref, br_ref, bi_ref, cr_ref, ci_ref, ar_ref, ai_ref,
             du_ref, dcr_ref, dci_ref, dbr_ref, dbi_ref, dar_ref, dai_ref, xr, xi, lr, li):
        _ssm_states(u_ref, br_ref, bi_ref, ar_ref, ai_ref, xr, xi, T)
        dyb = dy_ref[...]
        lr[...] = _dot(dyb, cr_ref[...], _NN)
        li[...] = -_dot(dyb, ci_ref[...], _NN)
        da_re, da_im = _scan_reverse_in_place(lr, li, xr, xi, ar_ref[...], ai_ref[...], T)
        dar_ref[...] = da_re
        dai_ref[...] = da_im
        du_ref[...] = _dot(lr[...], br_ref[...], _NT) + _dot(li[...], bi_ref[...], _NT) + dud_ref[...]
        ub = u_ref[...].astype(_BF16)
        dcr_ref[...] = _dot(dyb, xr[...], _TN)
        dci_ref[...] = _dot(dyb, xi[...], _TN)
        dbr_ref[...] = _dot(ub, lr[...], _TN)
        dbi_ref[...] = _dot(ub, li[...], _TN)

    width, matrix, row = _ssm_super_specs(L)
    return _call(body, (SUPER,), [width, width, width, matrix, matrix, matrix, matrix, row, row],
                 [width] + [matrix] * 4 + [row] * 2,
                 [_sds((L, SSM_WIDTH), _F32)] + [_sds((SB_WIDTH, N_STATE), _F32)] * 4 + [_sds((1, N_STATE), _F32)] * 2,
                 "ssm_core_bwd", scratch=[pltpu.VMEM((L, SB_STATE), _F32)] * 4,
                 tokens=tokens)(u, dy, dud, bt_re, bt_im, ct_re, ct_im, a_re, a_im)


def _ssm_out(cx, u, d_row, w_glu, b_glu, g_ssm):
    L = u.shape[0]
    tm = _tile(L)

    def body(cx_ref, u_ref, d_ref, w_ref, b_ref, g_ref, y_ref, z_ref, n_ref):
        y = cx_ref[...] + d_ref[...] * u_ref[...]
        y_ref[...] = y
        z = _dot(_gelu(y), w_ref[...], _NT) + b_ref[...]
        z_ref[...] = z
        out = z[:, :SSM_WIDTH] * jax.nn.sigmoid(z[:, SSM_WIDTH:])
        n, _ = _rms_fwd(out, g_ref[...])
        n_ref[...] = n.astype(_BF16)

    return _call(body, (L // tm,),
                 [_rows(tm, SSM_WIDTH), _rows(tm, SSM_WIDTH), _whole((1, SSM_WIDTH)),
                  _whole((2 * SSM_WIDTH, SSM_WIDTH)), _whole((1, 2 * SSM_WIDTH)), _whole((1, SSM_WIDTH))],
                 [_rows(tm, SSM_WIDTH), _rows(tm, 2 * SSM_WIDTH), _rows(tm, SSM_WIDTH)],
                 [_sds((L, SSM_WIDTH), _F32), _sds((L, 2 * SSM_WIDTH), _F32), _sds((L, SSM_WIDTH), _BF16)],
                 "ssm_out")(cx, u, d_row, w_glu, b_glu, g_ssm)


def _ssm_out_bwd(dn, y, z, u, d_row, w_glu, g_ssm):
    L = u.shape[0]
    tm = _tile(L)

    def body(dn_ref, y_ref, z_ref, u_ref, d_ref, w_ref, g_ref,
             gy_ref, dz_ref, dy_ref, dud_ref, dg_ref, db_ref, dd_ref):
        first = pl.program_id(0) == 0
        z = z_ref[...]
        z1, z2 = z[:, :SSM_WIDTH], z[:, SSM_WIDTH:]
        sig = jax.nn.sigmoid(z2)
        out = z1 * sig
        g = g_ref[...]
        _, r = _rms_fwd(out, g)
        dout, dg = _rms_bwd(dn_ref[...], out, g, r)
        _accumulate(dg_ref, dg, first)
        dz = jnp.concatenate([dout * sig, dout * z1 * sig * (1.0 - sig)], axis=1)
        _accumulate(db_ref, jnp.sum(dz, axis=0, keepdims=True), first)
        dzb = dz.astype(_BF16)
        dz_ref[...] = dzb
        y = y_ref[...]
        gy_ref[...] = _gelu(y).astype(_BF16)
        dy = _dot(dzb, w_ref[...], _NN) * _gelu_grad(y)
        u = u_ref[...]
        _accumulate(dd_ref, jnp.sum(dy * u, axis=0, keepdims=True), first)
        dud_ref[...] = d_ref[...] * dy
        dy_ref[...] = dy.astype(_BF16)

    row = _whole((1, SSM_WIDTH))
    return _call(body, (L // tm,),
                 [_rows(tm, SSM_WIDTH), _rows(tm, SSM_WIDTH), _rows(tm, 2 * SSM_WIDTH), _rows(tm, SSM_WIDTH),
                  row, _whole((2 * SSM_WIDTH, SSM_WIDTH)), row],
                 [_rows(tm, SSM_WIDTH), _rows(tm, 2 * SSM_WIDTH), _rows(tm, SSM_WIDTH), _rows(tm, SSM_WIDTH),
                  row, _whole((1, 2 * SSM_WIDTH)), row],
                 [_sds((L, SSM_WIDTH), _BF16), _sds((L, 2 * SSM_WIDTH), _BF16), _sds((L, SSM_WIDTH), _BF16),
                  _sds((L, SSM_WIDTH), _F32),
                  _sds((1, SSM_WIDTH), _F32), _sds((1, 2 * SSM_WIDTH), _F32), _sds((1, SSM_WIDTH), _F32)],
                 "ssm_out_bwd")(dn, y, z, u, d_row, w_glu, g_ssm)


def _ssm_du(lam_re, lam_im, bt_re, bt_im, dud):
    L = dud.shape[0]
    tm = _tile(L)

    def body(lr_ref, li_ref, br_ref, bi_ref, dud_ref, du_ref):
        for k in range(SUPER):
            du_ref[:, _sb_width(k)] = (_dot(lr_ref[:, _sb_state(k)], br_ref[:, _sb_state(k)], _NT)
                                       + _dot(li_ref[:, _sb_state(k)], bi_ref[:, _sb_state(k)], _NT)
                                       + dud_ref[:, _sb_width(k)])

    return _call(body, (L // tm,),
                 [_rows(tm, N_STATE), _rows(tm, N_STATE), _whole((SB_WIDTH, N_STATE)),
                  _whole((SB_WIDTH, N_STATE)), _rows(tm, SSM_WIDTH)],
                 _rows(tm, SSM_WIDTH), _sds((L, SSM_WIDTH), _F32), "ssm_du")(lam_re, lam_im, bt_re, bt_im, dud)


def _ssm_weight_grads(dy, x_re, x_im, lam_re, lam_im, u):
    L = u.shape[0]

    def body(dy_ref, xr_ref, xi_ref, lr_ref, li_ref, u_ref, dcr_ref, dci_ref, dbr_ref, dbi_ref):
        dyb = dy_ref[...]
        ub = u_ref[...].astype(_BF16)
        dcr_ref[...] = _dot(dyb, xr_ref[...], _TN)
        dci_ref[...] = _dot(dyb, xi_ref[...], _TN)
        dbr_ref[...] = _dot(ub, lr_ref[...], _TN)
        dbi_ref[...] = _dot(ub, li_ref[...], _TN)

    width = pl.BlockSpec((L, SB_WIDTH), lambda k: (0, k))
    state = pl.BlockSpec((L, SB_STATE), lambda k: (0, k))
    out = pl.BlockSpec((SB_WIDTH, SB_STATE), lambda k: (0, k))
    return _call(body, (SUPER,), [width, state, state, state, state, width], [out] * 4,
                 [_sds((SB_WIDTH, N_STATE), _F32)] * 4,
                 "ssm_weight_grads")(dy, x_re, x_im, lam_re, lam_im, u)


_SSM_PACK = {"ssm_b_re": (0, SSM_WIDTH, 0, SSM_STATE), "ssm_c_re": (0, SSM_WIDTH, 64, SSM_STATE),
             "ssm_b_im": (512, SSM_WIDTH, 0, SSM_STATE), "ssm_c_im": (512, SSM_WIDTH, 64, SSM_STATE),
             "ssm_lambda_re": (1024, SSM_GROUPS, 0, SSM_STATE), "ssm_lambda_im": (1024, SSM_GROUPS, 64, SSM_STATE),
             "ssm_d": (1056, SSM_GROUPS, 0, SSM_GROUP), "ssm_log_dt": (1088, 1, 0, SSM_GROUPS)}
_PACK_TILE = 16
_SSM_PACK_ROWS = 1088 + _PACK_TILE


def _ssm_param_bwd(da_re, da_im, dbt_re, dbt_im, dct_re, dct_im, lam_re, lam_im, log_dt, b_re, b_im, g_d):
    def body(dar, dai, dbr, dbi, dcr, dci, lr_ref, li_ref, ld_ref, bre_ref, bim_ref, gd_ref, pack_ref):
        lane_in = _iota((SSM_STATE, _LANES), 0)
        lane_out = _iota((SSM_STATE, _LANES), 1)
        low = (lane_out == lane_in).astype(_F32)
        high = (lane_out == lane_in + SSM_STATE).astype(_F32)

        def side_by_side(a, b):
            return _dot_exact(a, low, _NN) + _dot_exact(b, high, _NN)

        tail = _SSM_PACK["ssm_d"][0]
        pack_ref[tail:, :] = jnp.zeros((_SSM_PACK_ROWS - tail, _LANES), _BF16)
        pack_ref[tail:tail + SSM_GROUPS, 0:SSM_GROUP] = gd_ref[...].astype(_BF16)
        own_c = (_iota((SB_WIDTH, SB_STATE), 0) >> 4) == (_iota((SB_WIDTH, SB_STATE), 1) >> 6)

        def unfold(ref):
            blocks = []
            for k in range(SUPER):
                t = jnp.where(own_c, ref[:, _sb_state(k)], 0.0)
                t = sum(t[:, 128 * i:128 * (i + 1)] for i in range(SB_STATE // 128))
                blocks.append((t + pltpu.roll(t, SSM_STATE, 1))[:, :SSM_STATE])
            return jnp.concatenate(blocks, axis=0)

        dbb_re, dbb_im = unfold(dbr), unfold(dbi)
        b_re, b_im = bre_ref[...], bim_ref[...]
        dt_col = _dt_column(ld_ref[...])
        (_, _, fr, fi), vjp = jax.vjp(_s5_discretize, lr_ref[...], li_ref[...], dt_col)
        spread = _rows_of_group()
        fr_t = _dot_exact(spread, fr, _NN)
        fi_t = _dot_exact(spread, fi, _NN)
        pack_ref[0:SSM_WIDTH, :] = side_by_side(fr_t * dbb_re + fi_t * dbb_im, unfold(dcr)).astype(_BF16)
        pack_ref[SSM_WIDTH:2 * SSM_WIDTH, :] = side_by_side(fr_t * dbb_im - fi_t * dbb_re, -unfold(dci)).astype(_BF16)
        d_fr = _dot_exact(spread, dbb_re * b_re + dbb_im * b_im, _TN)
        d_fi = _dot_exact(spread, dbb_im * b_re - dbb_re * b_im, _TN)
        e64, own = _group_masks()

        def from_row(ref):
            return _dot_exact(jnp.where(own, ref[...], 0.0), e64, _NT)

        d_lr, d_li, d_dt = vjp((from_row(dar), from_row(dai), d_fr, d_fi))
        lam_rows = _SSM_PACK["ssm_lambda_re"][0]
        pack_ref[lam_rows:lam_rows + SSM_GROUPS, :] = side_by_side(d_lr, d_li).astype(_BF16)
        eye = (_iota((SSM_GROUPS, SSM_GROUPS), 0) == _iota((SSM_GROUPS, SSM_GROUPS), 1)).astype(_F32)
        dt_row = _SSM_PACK["ssm_log_dt"][0]
        pack_ref[dt_row:dt_row + _PACK_TILE, 0:SSM_GROUPS] = _dot_exact(
            jnp.broadcast_to(d_dt, (SSM_GROUPS, 128)), eye, _TN)[0:_PACK_TILE].astype(_BF16)

    ins = [da_re, da_im, dbt_re, dbt_im, dct_re, dct_im, lam_re, lam_im, log_dt, b_re, b_im, g_d]
    out = (_SSM_PACK_ROWS, _LANES)
    return _call(body, (1,), [_whole(a.shape) for a in ins], _whole(out), _sds(out, _BF16), "ssm_param_bwd")(*ins)


def _head_spread(j):
    r = _iota((KV_WIDTH, 256), 0)
    c = _iota((KV_WIDTH, 256), 1)
    return (r == HEAD_DIM * j + (c & (HEAD_DIM - 1))).astype(_BF16)


STACK = Q_PER_KV * BLOCK


def _stack_heads(t):
    lane_head = _iota((1, 256), 1) >> 6
    return jnp.concatenate([jnp.where(lane_head == g, t, jnp.zeros_like(t)) for g in range(Q_PER_KV)], axis=0)


def _unstack_heads(t):
    lane_head = _iota((1, 256), 1) >> 6
    return sum(jnp.where(lane_head == g, t[BLOCK * g:BLOCK * (g + 1)], 0.0) for g in range(Q_PER_KV))


def _stacked_sinks(sink_ref, j):
    block = _iota((STACK, 1), 0) >> 7
    col = jnp.full((STACK, 1), sink_ref[Q_PER_KV * j], _F32)
    for g in range(1, Q_PER_KV):
        col = jnp.where(block == g, sink_ref[Q_PER_KV * j + g], col)
    return col


def _fold_heads(t, j):
    t = t[:, :KV_WIDTH] + t[:, KV_WIDTH:]
    t = t + pltpu.roll(t, HEAD_DIM, 1)
    return jnp.where((_iota((1, KV_WIDTH), 1) >> 6) == j, t, 0.0)


def _attn_scores(q_stacked, kt, blk, sink):
    s = _dot(q_stacked, kt, _NT) * (HEAD_DIM ** -0.5)
    qi = _iota((STACK, 2 * BLOCK), 0) & (BLOCK - 1)
    kj = _iota((STACK, 2 * BLOCK), 1)
    rel = qi + BLOCK - kj
    valid = (rel >= 0) & (rel < BLOCK) & (blk * BLOCK - BLOCK + kj >= 0)
    s = jnp.where(valid, s, MASK_VALUE)
    m = jnp.maximum(jnp.max(s, axis=-1, keepdims=True), sink)
    p = jnp.exp(s - m)
    e_sink = jnp.exp(sink - m)
    den = jnp.sum(p, axis=-1, keepdims=True) + e_sink
    return p / den, e_sink / den


def _attn_specs():
    prev = lambda i: (jnp.maximum(i - 1, 0), 0)
    cur = lambda i: (i, 0)
    kv = [pl.BlockSpec((BLOCK, KV_WIDTH), prev), pl.BlockSpec((BLOCK, KV_WIDTH), cur)]
    return [pl.BlockSpec((BLOCK, ATTN_WIDTH), cur)] + kv + kv


def _attn_fwd(q, k, v, sinks, g_attn):
    L = q.shape[0]

    def body(q_ref, kp_ref, kc_ref, vp_ref, vc_ref, sink_ref, g_ref, o_ref, n_ref):
        blk = pl.program_id(0)
        kwin = jnp.concatenate([kp_ref[...], kc_ref[...]], axis=0)
        vwin = jnp.concatenate([vp_ref[...], vc_ref[...]], axis=0)
        halves = []
        for j in range(N_KV_HEADS):
            spread = _head_spread(j)
            kt = _dot(kwin, spread, _NN).astype(_BF16)
            vt = _dot(vwin, spread, _NN).astype(_BF16)
            qs = _stack_heads(q_ref[:, 256 * j:256 * (j + 1)])
            p, _ = _attn_scores(qs, kt, blk, _stacked_sinks(sink_ref, j))
            halves.append(_unstack_heads(_dot(p, vt, _NN)))
        o = jnp.concatenate(halves, axis=1)
        o_ref[...] = o
        n, _ = _rms_fwd(o, g_ref[...])
        n_ref[...] = n.astype(_BF16)

    cur = lambda i: (i, 0)
    return _call(body, (L // BLOCK,),
                 _attn_specs() + [pl.BlockSpec(memory_space=pltpu.SMEM), _whole((1, ATTN_WIDTH))],
                 [pl.BlockSpec((BLOCK, ATTN_WIDTH), cur)] * 2,
                 [_sds((L, ATTN_WIDTH), _F32), _sds((L, ATTN_WIDTH), _BF16)],
                 "attn_fwd")(q, k, k, v, v, sinks, g_attn)


def _attn_bwd(q, k, v, o, dn, sinks, g_attn):
    L = q.shape[0]

    def body(q_ref, kp_ref, kc_ref, vp_ref, vc_ref, o_ref, dn_ref, sink_ref, g_ref,
             dq_ref, dk_ref, dv_ref, dsink_ref, dg_ref):
        blk = pl.program_id(0)
        first = blk == 0

        @pl.when(first)
        def _():
            dk_ref[...] = jnp.zeros_like(dk_ref)
            dv_ref[...] = jnp.zeros_like(dv_ref)
            dsink_ref[...] = jnp.zeros_like(dsink_ref)

        o = o_ref[...]
        g = g_ref[...]
        _, r = _rms_fwd(o, g)
        do, dg = _rms_bwd(dn_ref[...], o, g, r)
        _accumulate(dg_ref, dg, first)
        kwin = jnp.concatenate([kp_ref[...], kc_ref[...]], axis=0)
        vwin = jnp.concatenate([vp_ref[...], vc_ref[...]], axis=0)
        lane = _iota((1, 128), 1)
        dsink = jnp.zeros((1, 128), _F32)
        dkwin = jnp.zeros((2 * BLOCK, KV_WIDTH), _F32)
        dvwin = jnp.zeros((2 * BLOCK, KV_WIDTH), _F32)
        dq_halves = []
        for j in range(N_KV_HEADS):
            spread = _head_spread(j)
            kt = _dot(kwin, spread, _NN).astype(_BF16)
            vt = _dot(vwin, spread, _NN).astype(_BF16)
            qs = _stack_heads(q_ref[:, 256 * j:256 * (j + 1)])
            dos = _stack_heads(do[:, 256 * j:256 * (j + 1)]).astype(_BF16)
            p, p_sink = _attn_scores(qs, kt, blk, _stacked_sinks(sink_ref, j))
            dp = _dot(dos, vt, _NT)
            delta = jnp.sum(p * dp, axis=-1, keepdims=True)
            ds = (p * (dp - delta) * (HEAD_DIM ** -0.5)).astype(_BF16)
            sink_term = p_sink * delta
            for g in range(Q_PER_KV):
                head_sum = jnp.sum(sink_term[BLOCK * g:BLOCK * (g + 1)], axis=0, keepdims=True)
                dsink = dsink - jnp.where(lane == Q_PER_KV * j + g, head_sum, 0.0)
            dvwin = dvwin + _fold_heads(_dot(p, dos, _TN), j)
            dkwin = dkwin + _fold_heads(_dot(ds, qs, _TN), j)
            dq_halves.append(_unstack_heads(_dot(ds, kt, _NN)))
        dq_ref[...] = jnp.concatenate(dq_halves, axis=1)
        dsink_ref[...] += dsink
        prev = pl.ds(pl.multiple_of(jnp.maximum(blk - 1, 0) * BLOCK, BLOCK), BLOCK)
        cur = pl.ds(pl.multiple_of(blk * BLOCK, BLOCK), BLOCK)
        dk_ref[prev, :] += dkwin[:BLOCK]
        dk_ref[cur, :] += dkwin[BLOCK:]
        dv_ref[prev, :] += dvwin[:BLOCK]
        dv_ref[cur, :] += dvwin[BLOCK:]

    cur = lambda i: (i, 0)
    blk_q = pl.BlockSpec((BLOCK, ATTN_WIDTH), cur)
    return _call(body, (L // BLOCK,),
                 _attn_specs() + [blk_q, blk_q, pl.BlockSpec(memory_space=pltpu.SMEM), _whole((1, ATTN_WIDTH))],
                 [blk_q, _whole((L, KV_WIDTH)), _whole((L, KV_WIDTH)), _whole((1, 128)), _whole((1, ATTN_WIDTH))],
                 [_sds((L, ATTN_WIDTH), _F32), _sds((L, KV_WIDTH), _F32), _sds((L, KV_WIDTH), _F32),
                  _sds((1, 128), _F32), _sds((1, ATTN_WIDTH), _F32)],
                 "attn_bwd")(q, k, k, v, v, o, dn, sinks, g_attn)


def _out_proj(n_ssm, n_attn, x, w_out, g_post_mix, g_pre_ffn):
    L = x.shape[0]
    tm = _chunk_tile(L)

    def body(ns_ref, na_ref, x_ref, w_ref, g1_ref, g2_ref, merged_ref, mo_ref, h1_ref, hn2_ref):
        merged = jnp.concatenate([ns_ref[...], na_ref[...]], axis=1)
        merged_ref[...] = merged
        mo = _dot(merged, w_ref[...], _NN)
        mo_ref[...] = mo
        n, _ = _rms_fwd(mo, g1_ref[...])
        h1 = x_ref[...] + n
        h1_ref[...] = h1
        hn2, _ = _rms_fwd(h1, g2_ref[...])
        hn2_ref[...] = hn2.astype(_BF16)

    row = _whole((1, D_MODEL))
    return _call(body, (L // tm,),
                 [_chunk_block(L, SSM_WIDTH), _rows(tm, ATTN_WIDTH), _rows(tm, D_MODEL), _whole((D_MODEL, D_MODEL)),
                  row, row],
                 [_rows(tm, D_MODEL)] * 4,
                 [_sds((L, D_MODEL), _BF16), _sds((L, D_MODEL), _F32), _sds((L, D_MODEL), _F32), _sds((L, D_MODEL), _BF16)],
                 "out_proj")(n_ssm, n_attn, x, w_out, g_post_mix, g_pre_ffn)


def _ffn(hn2, h1, target, w_gate_up, w_down, g_pre_ffn, g_post_ffn):
    L = h1.shape[0]
    tm = _tile(L)
    half = D_FF // 2

    def body(hn2_ref, h1_ref, tgt_ref, wgu_hbm, wd_hbm, g2_ref, g3_ref,
             act_ref, dgu_ref, dff_ref, dh1_ref, loss_ref, dg3_ref, dg2_ref,
             wgu, wd, gu, sem):
        first = pl.program_id(0) == 0

        @pl.when(first)
        def _():
            c1 = pltpu.make_async_copy(wgu_hbm, wgu, sem.at[0])
            c2 = pltpu.make_async_copy(wd_hbm, wd, sem.at[1])
            c1.start()
            c2.start()
            c1.wait()
            c2.wait()

        hn2 = hn2_ref[...]
        ff = jnp.zeros((tm, D_MODEL), _F32)
        for c in range(2):
            gate = _dot(hn2, wgu[half * c:half * (c + 1), :], _NT)
            up = _dot(hn2, wgu[D_FF + half * c:D_FF + half * (c + 1), :], _NT)
            gu[:, half * c:half * (c + 1)] = gate
            gu[:, D_FF + half * c:D_FF + half * (c + 1)] = up
            act = gate * jax.nn.sigmoid(gate) * up
            act_ref[half * c:half * (c + 1), :] = act.T.astype(_BF16)
            ff = ff + _dot(act, wd[half * c:half * (c + 1), :], _NN)
        g3 = g3_ref[...]
        n, r = _rms_fwd(ff, g3)
        h1 = h1_ref[...]
        err = h1 + n - tgt_ref[...]
        loss = 0.5 * jnp.sum(jnp.mean(err * err, axis=-1, keepdims=True), axis=0, keepdims=True)
        _accumulate(loss_ref, jnp.broadcast_to(loss, (1, 128)), first)
        dh2 = err * (1.0 / D_MODEL)
        dff, dg3 = _rms_bwd(dh2, ff, g3, r)
        _accumulate(dg3_ref, dg3, first)
        dffb = dff.astype(_BF16)
        dff_ref[...] = dffb
        dhn2 = jnp.zeros((tm, D_MODEL), _F32)
        for c in range(2):
            dact = _dot(dffb, wd[half * c:half * (c + 1), :], _NT)
            gate = gu[:, half * c:half * (c + 1)]
            up = gu[:, D_FF + half * c:D_FF + half * (c + 1)]
            sig = jax.nn.sigmoid(gate)
            silu = gate * sig
            dgate = dact * up * (sig + silu * (1.0 - sig))
            dup = dact * silu
            dgu_ref[half * c:half * (c + 1), :] = dgate.T.astype(_BF16)
            dgu_ref[D_FF + half * c:D_FF + half * (c + 1), :] = dup.T.astype(_BF16)
            dhn2 = dhn2 + _dot(dgate, wgu[half * c:half * (c + 1), :], _NN)
            dhn2 = dhn2 + _dot(dup, wgu[D_FF + half * c:D_FF + half * (c + 1), :], _NN)
        g2 = g2_ref[...]
        _, r2 = _rms_fwd(h1, g2)
        dh1, dg2 = _rms_bwd(dhn2, h1, g2, r2)
        _accumulate(dg2_ref, dg2, first)
        dh1_ref[...] = dh2 + dh1

    row = _whole((1, D_MODEL))
    anyspace = pl.BlockSpec(memory_space=pl.ANY)
    return _call(body, (L // tm,),
                 [_rows(tm, D_MODEL), _rows(tm, D_MODEL), _rows(tm, D_MODEL), anyspace, anyspace, row, row],
                 [pl.BlockSpec((D_FF, tm), lambda i: (0, i)), pl.BlockSpec((2 * D_FF, tm), lambda i: (0, i)),
                  _rows(tm, D_MODEL), _rows(tm, D_MODEL), _whole((1, 128)), row, row],
                 [_sds((D_FF, L), _BF16), _sds((2 * D_FF, L), _BF16), _sds((L, D_MODEL), _BF16),
                  _sds((L, D_MODEL), _F32), _sds((1, 128), _F32), _sds((1, D_MODEL), _F32), _sds((1, D_MODEL), _F32)],
                 "ffn",
                 scratch=[pltpu.VMEM((2 * D_FF, D_MODEL), _BF16), pltpu.VMEM((D_FF, D_MODEL), _BF16),
                          pltpu.VMEM((tm, 2 * D_FF), _F32), pltpu.SemaphoreType.DMA((2,))],
                 )(hn2, h1, target, w_gate_up, w_down, g_pre_ffn, g_post_ffn)


def _out_proj_bwd(dh1, mo, w_out, g_post_mix, tokens=()):
    L = dh1.shape[0]
    tm = _chunk_tile(L)

    def body(dh1_ref, mo_ref, w_ref, g_ref, dmo_ref, dns_ref, dna_ref, dg_ref):
        first = pl.program_id(0) == 0
        mo = mo_ref[...]
        g = g_ref[...]
        _, r = _rms_fwd(mo, g)
        dmo, dg = _rms_bwd(dh1_ref[...], mo, g, r)
        _accumulate(dg_ref, dg, first)
        dmob = dmo.astype(_BF16)
        dmo_ref[...] = dmob
        dmerged = _dot(dmob, w_ref[...], _NT)
        dns_ref[...] = dmerged[:, :SSM_WIDTH]
        dna_ref[...] = dmerged[:, SSM_WIDTH:]

    row = _whole((1, D_MODEL))
    return _call(body, (L // tm,),
                 [_rows(tm, D_MODEL), _rows(tm, D_MODEL), _whole((D_MODEL, D_MODEL)), row],
                 [_rows(tm, D_MODEL), _chunk_block(L, SSM_WIDTH), _rows(tm, ATTN_WIDTH), row],
                 [_sds((L, D_MODEL), _BF16), _sds(_chunk_shape(L, SSM_WIDTH), _F32), _sds((L, ATTN_WIDTH), _F32),
                  _sds((1, D_MODEL), _F32)],
                 "out_proj_bwd", tokens=tokens)(dh1, mo, w_out, g_post_mix)


def _in_proj_bwd(du, dq, dk, dv, cos_t, sin_t, x, dh1, g_pre_mix, w_in, tokens=()):
    L = x.shape[0]
    tm = _chunk_tile(L)

    def body(du_ref, dq_ref, dk_ref, dv_ref, cos_ref, sin_ref, x_ref, dh1_ref, g_ref, w_ref,
             dproj_ref, dx_ref, dg_ref):
        first = pl.program_id(0) == 0
        cos_v, sin_v = cos_ref[...], sin_ref[...]
        dproj = jnp.concatenate([du_ref[...], _rope_transpose(dq_ref[...], cos_v, sin_v),
                                 _rope_transpose(dk_ref[...], cos_v, sin_v), dv_ref[...]], axis=1).astype(_BF16)
        dproj_ref[...] = dproj
        dhn = _dot(dproj, w_ref[...], _NN)
        x = x_ref[...]
        g = g_ref[...]
        _, r = _rms_fwd(x, g)
        dx, dg = _rms_bwd(dhn, x, g, r)
        _accumulate(dg_ref, dg, first)
        dx_ref[...] = dh1_ref[...] + dx

    row = _whole((1, D_MODEL))
    return _call(body, (L // tm,),
                 [_chunk_block(L, SSM_WIDTH), _rows(tm, ATTN_WIDTH), _rows(tm, KV_WIDTH), _rows(tm, KV_WIDTH),
                  _rows(tm, KV_WIDTH), _rows(tm, KV_WIDTH), _rows(tm, D_MODEL), _rows(tm, D_MODEL), row,
                  _whole((IN_WIDTH, D_MODEL))],
                 [_rows(tm, IN_WIDTH), _rows(tm, D_MODEL), row],
                 [_sds((L, IN_WIDTH), _BF16), _sds((L, D_MODEL), _F32), _sds((1, D_MODEL), _F32)],
                 "in_proj_bwd", tokens=tokens)(du, dq, dk, dv, cos_t, sin_t, x, dh1, g_pre_mix, w_in)


def _matmul_nn(a, b, out_dtype, name):
    M, K = a.shape
    N = b.shape[1]
    tm = next(t for t in (512, 256, 128) if M % t == 0)
    tn = N if N <= D_MODEL else next(t for t in (512, 256, 128) if N % t == 0)

    def body(a_ref, b_ref, o_ref):
        o_ref[...] = _dot(a_ref[...], b_ref[...], _NN).astype(out_dtype)

    params = pltpu.CompilerParams(dimension_semantics=("arbitrary", "arbitrary"), vmem_limit_bytes=VMEM_LIMIT)
    return pl.pallas_call(body, grid=(M // tm, N // tn),
                          in_specs=[pl.BlockSpec((tm, K), lambda i, j: (i, 0)),
                                    pl.BlockSpec((K, tn), lambda i, j: (0, j))],
                          out_specs=pl.BlockSpec((tm, tn), lambda i, j: (i, j)),
                          out_shape=_sds((M, N), out_dtype), compiler_params=params, name=name)(a, b)


def _matmul_tn(a, b, out_dtype, name, scale=1.0):
    K, M = a.shape
    N = b.shape[1]
    tm = next(t for t in (512, 256, 128) if M % t == 0)
    tn = N if N <= D_MODEL else next(t for t in (512, 256, 128) if N % t == 0)

    def body(a_ref, b_ref, o_ref):
        acc = _dot(a_ref[...], b_ref[...], _TN)
        o_ref[...] = (acc if scale == 1.0 else acc * scale).astype(out_dtype)

    params = pltpu.CompilerParams(dimension_semantics=("arbitrary", "arbitrary"), vmem_limit_bytes=VMEM_LIMIT)
    return pl.pallas_call(body, grid=(M // tm, N // tn),
                          in_specs=[pl.BlockSpec((K, tm), lambda i, j: (0, i)),
                                    pl.BlockSpec((K, tn), lambda i, j: (0, j))],
                          out_specs=pl.BlockSpec((tm, tn), lambda i, j: (i, j)),
                          out_shape=_sds((M, N), out_dtype), compiler_params=params, name=name)(a, b)


def _local_step(x, pos, target, p, fetch, publish, progress):
    L = x.shape[0]
    T = L // SCAN_CHUNKS
    cos_t, sin_t = _rope_tables(pos.reshape(L, 1))
    w_in, = fetch(("w_in",), None)
    hn, u, q, k, v = _in_proj(x, p["g_pre_mix"], w_in, cos_t, sin_t)

    ssm = {n: _to_2d(n, p[n]) for n in ("ssm_lambda_re", "ssm_lambda_im", "ssm_log_dt", "ssm_b_re", "ssm_b_im",
                                        "ssm_c_re", "ssm_c_im")}
    d_row = p["ssm_d"].reshape(1, SSM_WIDTH)
    a_re, a_im, bt_re, bt_im, ct_re, ct_im = _ssm_prep(
        ssm["ssm_lambda_re"], ssm["ssm_lambda_im"], ssm["ssm_log_dt"], ssm["ssm_b_re"], ssm["ssm_b_im"],
        ssm["ssm_c_re"], ssm["ssm_c_im"])

    u_c = u.reshape(L, SSM_WIDTH)
    cx = _ssm_core_fwd(u_c, bt_re, bt_im, ct_re, ct_im, a_re, a_im)
    w_glu, = fetch(("w_glu",), cx)
    y, z, n_ssm_c = _ssm_out(cx, u_c, d_row, w_glu, p["b_glu"], p["g_ssm_out"])
    n_ssm = n_ssm_c.reshape(_chunk_shape(L, SSM_WIDTH))

    sinks = p["attn_sinks"].reshape(N_Q_HEADS)
    o, n_attn = _attn_fwd(q, k, v, sinks, p["g_attn_out"])
    w_out, = fetch(("w_out",), n_attn)
    merged, mo, h1, hn2 = _out_proj(n_ssm, n_attn, x, w_out, p["g_post_mix"], p["g_pre_ffn"])
    w_gate_up, w_down = fetch(("w_gate_up", "w_down"), hn2)
    act_t, dgu_t, dff, dh1, loss, dg_post_ffn, dg_pre_ffn = _ffn(
        hn2, h1, target, w_gate_up, w_down, p["g_pre_ffn"], p["g_post_ffn"])
    grads = {"g_post_ffn": dg_post_ffn, "g_pre_ffn": dg_pre_ffn}
    tokens = publish({"w_down": _matmul_nn(act_t, dff, _BF16, "grad_w_down"),
                      "w_gate_up": _matmul_nn(dgu_t, hn2, _BF16, "grad_w_gate_up")})

    dmo, dn_ssm, dn_attn, grads["g_post_mix"] = _out_proj_bwd(dh1, mo, w_out, p["g_post_mix"], tokens)
    grad_w_out = _matmul_tn(merged, dmo, _BF16, "grad_w_out")

    dq, dk, dv, dsink, grads["g_attn_out"] = _attn_bwd(q, k, v, o, dn_attn, sinks, p["g_attn_out"])
    grads["attn_sinks"] = dsink

    gy, dz, dy, dud, grads["g_ssm_out"], grads["b_glu"], dd = _ssm_out_bwd(
        dn_ssm.reshape(L, SSM_WIDTH), y, z, u_c, d_row, w_glu, p["g_ssm_out"])
    grads.update(w_out=grad_w_out, w_glu=_matmul_tn(dz, gy, _BF16, "grad_w_glu"))
    tokens = [grads["w_out"], grads["w_glu"]] + progress(dy)
    du_c, dct_re, dct_im, dbt_re, dbt_im, da_re, da_im = _ssm_core_bwd(
        u_c, dy, dud, bt_re, bt_im, ct_re, ct_im, a_re, a_im, tokens)
    ssm_pack = _ssm_param_bwd(
        da_re, da_im, dbt_re, dbt_im, dct_re, dct_im,
        ssm["ssm_lambda_re"], ssm["ssm_lambda_im"], ssm["ssm_log_dt"], ssm["ssm_b_re"], ssm["ssm_b_im"],
        dd.reshape(SSM_GROUPS, SSM_GROUP))
    grads.update(ssm_pack=ssm_pack, loss=loss)
    publish(grads)

    du = du_c.reshape(_chunk_shape(L, SSM_WIDTH))
    dproj, grad_x, g_pre_mix = _in_proj_bwd(du, dq, dk, dv, cos_t, sin_t, x, dh1, p["g_pre_mix"], w_in, [ssm_pack])
    publish({"g_pre_mix": g_pre_mix, "w_in": _matmul_tn(dproj, hn, _BF16, "grad_w_in")})
    return grad_x


_MESH = pl.DeviceIdType.MESH
_PEERS = N_DEV - 1


def _mesh_pos():
    return lax.axis_index("x"), lax.axis_index("y"), lax.axis_index("c")


def _dev_index(px, py, pc):
    return 4 * px + 2 * py + pc


def _all_gather(shards, out_dtype, name):
    n = len(shards)

    def body(*refs):
        ins, outs, stages = refs[:n], refs[n:2 * n], refs[2 * n:3 * n]
        send_sems, recv_sems, local_sems = refs[3 * n:]
        x, y, c = _mesh_pos()
        me, sibling = (x, y, c), (x, y, 1 - c)
        chips = [(1 - x, y), (x, 1 - y), (1 - x, 1 - y)]

        def copy(w, k, block, to, src=None):
            slot = outs[w].at[_dev_index(*block)]
            return pltpu.make_async_remote_copy(
                src_ref=slot if src is None else src, dst_ref=slot,
                send_sem=send_sems.at[_PEERS * w + k], recv_sem=recv_sems.at[_PEERS * w + k],
                device_id=to, device_id_type=_MESH)

        for w in range(n):
            stages[w][...] = ins[w][...].astype(out_dtype)
        mine, first, passed = [], [], []
        for w in range(n):
            cp = pltpu.make_async_copy(stages[w], outs[w].at[_dev_index(*me)], local_sems.at[w])
            cp.start()
            mine.append(cp)
            sends = [copy(w, 0, me, sibling, src=stages[w])]
            sends += [copy(w, 1 + j, me, (*chip, c), src=stages[w]) for j, chip in enumerate(chips)]
            for cp in sends:
                cp.start()
            first += sends
        for w in range(n):
            for j, chip in enumerate(chips):
                copy(w, 1 + j, (*chip, c), me).wait_recv()
                cp = copy(w, 4 + j, (*chip, c), sibling)
                cp.start()
                passed.append(cp)
        for w in range(n):
            copy(w, 0, sibling, me).wait_recv()
            for j, chip in enumerate(chips):
                copy(w, 4 + j, (*chip, 1 - c), me).wait_recv()
        for cp in first + passed:
            cp.wait_send()
        for cp in mine:
            cp.wait()

    return pl.pallas_call(
        body, name=name,
        out_shape=[_sds((N_DEV,) + s.shape, out_dtype) for s in shards],
        in_specs=[pl.BlockSpec(memory_space=pltpu.VMEM)] * n,
        out_specs=[pl.BlockSpec(memory_space=pl.ANY)] * n,
        scratch_shapes=[pltpu.VMEM(s.shape, out_dtype) for s in shards]
        + [pltpu.SemaphoreType.DMA((_PEERS * n,)), pltpu.SemaphoreType.DMA((_PEERS * n,)),
           pltpu.SemaphoreType.DMA((n,))],
        compiler_params=pltpu.CompilerParams(vmem_limit_bytes=VMEM_LIMIT),
    )(*shards)


_HBM_SPEC = pl.BlockSpec(memory_space=pltpu.HBM)
_SEM_SPEC = pl.BlockSpec(memory_space=pltpu.SEMAPHORE)
_DATAFLOW = pltpu.SideEffectType.DATAFLOW_SIDE_EFFECTING


def _peer(x, y, c, r):
    return (x ^ ((r >> 2) & 1), y ^ ((r >> 1) & 1), c ^ (r & 1))


def _hbm(a):
    return pltpu.with_memory_space_constraint(a, pltpu.HBM)


def _send_start(sources, blocked, name):
    n = len(sources)
    lands = [lax.empty((N_DEV,) + (s.shape[1:] if blocked else s.shape), s.dtype) for s in sources]

    def body(*refs):
        srcs, zones = refs[:n], refs[n:2 * n]
        send_sems, recv_sems = refs[2 * n:3 * n], refs[3 * n:4 * n]
        token, local_sems = refs[6 * n], refs[6 * n + 1]
        x, y, c = _mesh_pos()
        me = _dev_index(x, y, c)
        local = []
        for w in range(n):
            cp = pltpu.make_async_copy(srcs[w].at[me] if blocked else srcs[w], zones[w].at[me], local_sems.at[w])
            cp.start()
            local.append(cp)
            for r in range(1, N_DEV):
                peer = _peer(x, y, c, r)
                pltpu.make_async_remote_copy(
                    src_ref=srcs[w].at[_dev_index(*peer)] if blocked else srcs[w], dst_ref=zones[w].at[me],
                    send_sem=send_sems[w].at[r - 1], recv_sem=recv_sems[w].at[r - 1],
                    device_id=peer, device_id_type=_MESH).start()
        for cp in local:
            cp.wait()
        token[...] = jnp.zeros_like(token)

    sems = [pltpu.SemaphoreType.DMA((_PEERS,))] * (2 * n)
    out = pl.pallas_call(
        body, name=name,
        out_shape=sems + [pltpu.HBM(a.shape, a.dtype) for a in list(sources) + lands] + [_sds((8, 128), _F32)],
        in_specs=[_HBM_SPEC] * (2 * n),
        out_specs=[_SEM_SPEC] * (2 * n) + [_HBM_SPEC] * (2 * n) + [pl.BlockSpec(memory_space=pltpu.VMEM)],
        input_output_aliases={i: 2 * n + i for i in range(2 * n)},
        scratch_shapes=[pltpu.SemaphoreType.DMA((n,))],
        compiler_params=pltpu.CompilerParams(has_side_effects=_DATAFLOW),
    )(*[_hbm(a) for a in sources], *[_hbm(a) for a in lands])
    return out[:n], out[n:2 * n], out[2 * n:3 * n], out[3 * n:4 * n], out[4 * n]


def _send_wait(send_sems, recv_sems, sources, lands, after, blocked, name):
    n = len(sources)

    def body(*refs):
        srcs, zones = refs[:n], refs[n:2 * n]
        sends, recvs = refs[2 * n:3 * n], refs[3 * n:4 * n]
        x, y, c = _mesh_pos()
        for w in range(n):
            for r in range(1, N_DEV):
                peer = _peer(x, y, c, r)
                idx = _dev_index(*peer)
                cp = pltpu.make_async_remote_copy(
                    src_ref=srcs[w].at[idx] if blocked else srcs[w], dst_ref=zones[w].at[idx],
                    send_sem=sends[w].at[r - 1], recv_sem=recvs[w].at[r - 1],
                    device_id=peer, device_id_type=_MESH)
                cp.wait_send()
                cp.wait_recv()

    out = pl.pallas_call(
        body, name=name,
        out_shape=[pltpu.HBM(a.shape, a.dtype) for a in list(sources) + list(lands)],
        in_specs=[_HBM_SPEC] * (2 * n) + [_SEM_SPEC] * (2 * n) + [pl.BlockSpec(memory_space=pl.ANY)],
        out_specs=[_HBM_SPEC] * (2 * n),
        input_output_aliases={i: i for i in range(2 * n)},
        compiler_params=pltpu.CompilerParams(has_side_effects=_DATAFLOW),
    )(*sources, *lands, *send_sems, *recv_sems, after)
    return out[n:]


def _sequencer_exchange(sources, blocked, name, collective_id):
    n = len(sources)
    flags = blocked

    def body(*refs):
        srcs, zones = refs[:n], refs[n:2 * n]
        send_sems, recv_sems, local_sems = refs[2 * n:]
        x, y, c = _mesh_pos()
        me = _dev_index(x, y, c)
        barrier = pltpu.get_barrier_semaphore()
        for r in range(1, N_DEV):
            pl.semaphore_signal(barrier, inc=1, device_id=_peer(x, y, c, r), device_id_type=_MESH)
        pl.semaphore_wait(barrier, _PEERS)
        local, sends, recvs = [], [], []
        for w in range(n):
            cp = pltpu.make_async_copy(srcs[w].at[me] if flags[w] else srcs[w], zones[w].at[me], local_sems.at[w])
            cp.start()
            local.append(cp)
            for r in range(1, N_DEV):
                peer = _peer(x, y, c, r)
                idx = _dev_index(*peer)
                k = _PEERS * w + r - 1
                src = srcs[w].at[idx] if flags[w] else srcs[w]
                send = pltpu.make_async_remote_copy(
                    src_ref=src, dst_ref=zones[w].at[me], send_sem=send_sems.at[k], recv_sem=recv_sems.at[k],
                    device_id=peer, device_id_type=_MESH)
                send.start()
                sends.append(send)
                recvs.append(pltpu.make_async_remote_copy(
                    src_ref=src, dst_ref=zones[w].at[idx], send_sem=send_sems.at[k], recv_sem=recv_sems.at[k],
                    device_id=peer, device_id_type=_MESH))
        for cp in recvs:
            cp.wait_recv()
        for cp in sends:
            cp.wait_send()
        for cp in local:
            cp.wait()

    return pl.kernel(
        body, name=name,
        out_type=[_sds((N_DEV,) + (s.shape[1:] if f else s.shape), s.dtype) for s, f in zip(sources, flags)],
        mesh=plsc.ScalarSubcoreMesh(axis_name="sequencer", num_cores=1),
        scratch_types=[pltpu.SemaphoreType.DMA((_PEERS * n,)), pltpu.SemaphoreType.DMA((_PEERS * n,)),
                       pltpu.SemaphoreType.DMA((n,))],
        compiler_params=pltpu.CompilerParams(collective_id=collective_id),
    )(*sources)


def _sequencer_gather(shards, name, collective_id):
    n = len(shards)
    fan = 4

    def body(*refs):
        srcs, zones = refs[:n], refs[n:2 * n]
        send_sems, recv_sems, local_sems = refs[2 * n:]
        x, y, c = _mesh_pos()
        me, sibling = (x, y, c), (x, y, 1 - c)
        chips = [(1 - x, y), (x, 1 - y), (1 - x, 1 - y)]
        barrier = pltpu.get_barrier_semaphore()
        for peer in [sibling] + [(*chip, c) for chip in chips]:
            pl.semaphore_signal(barrier, inc=1, device_id=peer, device_id_type=_MESH)
        pl.semaphore_wait(barrier, fan)

        def copy(w, k, block, to, src=None):
            slot = zones[w].at[_dev_index(*block)]
            return pltpu.make_async_remote_copy(
                src_ref=slot if src is None else src, dst_ref=slot,
                send_sem=send_sems.at[_PEERS * w + k], recv_sem=recv_sems.at[_PEERS * w + k],
                device_id=to, device_id_type=_MESH)

        mine, first, passed = [], [], []
        for w in range(n):
            cp = pltpu.make_async_copy(srcs[w], zones[w].at[_dev_index(*me)], local_sems.at[w])
            cp.start()
            mine.append(cp)
            sends = [copy(w, 0, me, sibling, src=srcs[w])]
            sends += [copy(w, 1 + j, me, (*chip, c), src=srcs[w]) for j, chip in enumerate(chips)]
            for cp in sends:
                cp.start()
            first += sends
        for w in range(n):
            for j, chip in enumerate(chips):
                copy(w, 1 + j, (*chip, c), me).wait_recv()
                cp = copy(w, fan + j, (*chip, c), sibling)
                cp.start()
                passed.append(cp)
        for w in range(n):
            copy(w, 0, sibling, me).wait_recv()
            for j, chip in enumerate(chips):
                copy(w, fan + j, (*chip, 1 - c), me).wait_recv()
        for cp in first + passed:
            cp.wait_send()
        for cp in mine:
            cp.wait()

    return pl.kernel(
        body, name=name, out_type=[_sds((N_DEV,) + s.shape, s.dtype) for s in shards],
        mesh=plsc.ScalarSubcoreMesh(axis_name="sequencer", num_cores=1),
        scratch_types=[pltpu.SemaphoreType.DMA((_PEERS * n,)), pltpu.SemaphoreType.DMA((_PEERS * n,)),
                       pltpu.SemaphoreType.DMA((n,))],
        compiler_params=pltpu.CompilerParams(collective_id=collective_id),
    )(*shards)


N_CHIPS = N_DEV // 2


def _sequencer_pair_exchange(sources, name, collective_id):
    n = len(sources)

    def body(*refs):
        srcs, zones = refs[:n], refs[n:2 * n]
        send_sems, recv_sems = refs[2 * n:]
        x, y, c = _mesh_pos()
        sibling = (x, y, 1 - c)
        barrier = pltpu.get_barrier_semaphore()
        pl.semaphore_signal(barrier, inc=1, device_id=sibling, device_id_type=_MESH)
        pl.semaphore_wait(barrier, 1)
        copies = []
        for w in range(n):
            for j in range(N_CHIPS):
                k = N_CHIPS * w + j
                cp = pltpu.make_async_remote_copy(
                    src_ref=srcs[w].at[2 * j + 1 - c], dst_ref=zones[w].at[j],
                    send_sem=send_sems.at[k], recv_sem=recv_sems.at[k], device_id=sibling, device_id_type=_MESH)
                cp.start()
                copies.append(cp)
        for cp in copies:
            cp.wait_recv()
        for cp in copies:
            cp.wait_send()

    return pl.kernel(
        body, name=name, out_type=[_sds((N_CHIPS,) + s.shape[1:], s.dtype) for s in sources],
        mesh=plsc.ScalarSubcoreMesh(axis_name="sequencer", num_cores=1),
        scratch_types=[pltpu.SemaphoreType.DMA((N_CHIPS * n,)), pltpu.SemaphoreType.DMA((N_CHIPS * n,))],
        compiler_params=pltpu.CompilerParams(collective_id=collective_id),
    )(*sources)


def _pair_sum(source, received, core, name, tokens=()):
    _, rows, cols = source.shape
    tr = _row_tile(rows)

    def body(core_ref, s_ref, r_ref, o_ref):
        c = core_ref[0]
        for j in range(N_CHIPS):
            o_ref[j] = (s_ref[2 * j + c].astype(_F32) + r_ref[j].astype(_F32)).astype(o_ref.dtype)

    return _call(body, (rows // tr,),
                 [pl.BlockSpec(memory_space=pltpu.SMEM), pl.BlockSpec((N_DEV, tr, cols), lambda i: (0, i, 0)),
                  pl.BlockSpec((N_CHIPS, tr, cols), lambda i: (0, i, 0))],
                 pl.BlockSpec((N_CHIPS, tr, cols), lambda i: (0, i, 0)),
                 _sds((N_CHIPS, rows, cols), source.dtype), name, tokens=tokens)(core, source, received)


def _sequencer_chip_exchange(partials, name, collective_id):
    n = len(partials)
    others = N_CHIPS - 1

    def body(*refs):
        srcs, zones = refs[:n], refs[n:2 * n]
        send_sems, recv_sems, local_sems = refs[2 * n:]
        x, y, c = _mesh_pos()
        mine = 2 * x + y
        peers = [(x ^ (r >> 1), y ^ (r & 1), c) for r in range(1, N_CHIPS)]
        barrier = pltpu.get_barrier_semaphore()
        for peer in peers:
            pl.semaphore_signal(barrier, inc=1, device_id=peer, device_id_type=_MESH)
        pl.semaphore_wait(barrier, others)
        local, sends, recvs = [], [], []
        for w in range(n):
            cp = pltpu.make_async_copy(srcs[w].at[mine], zones[w].at[mine], local_sems.at[w])
            cp.start()
            local.append(cp)
            for r, peer in enumerate(peers):
                theirs = 2 * peer[0] + peer[1]
                k = others * w + r
                send = pltpu.make_async_remote_copy(
                    src_ref=srcs[w].at[theirs], dst_ref=zones[w].at[mine],
                    send_sem=send_sems.at[k], recv_sem=recv_sems.at[k], device_id=peer, device_id_type=_MESH)
                send.start()
                sends.append(send)
                recvs.append(pltpu.make_async_remote_copy(
                    src_ref=srcs[w].at[theirs], dst_ref=zones[w].at[theirs],
                    send_sem=send_sems.at[k], recv_sem=recv_sems.at[k], device_id=peer, device_id_type=_MESH))
        for cp in recvs:
            cp.wait_recv()
        for cp in sends:
            cp.wait_send()
        for cp in local:
            cp.wait()

    return pl.kernel(
        body, name=name, out_type=[_sds(s.shape, s.dtype) for s in partials],
        mesh=plsc.ScalarSubcoreMesh(axis_name="sequencer", num_cores=1),
        scratch_types=[pltpu.SemaphoreType.DMA((others * n,)), pltpu.SemaphoreType.DMA((others * n,)),
                       pltpu.SemaphoreType.DMA((n,))],
        compiler_params=pltpu.CompilerParams(collective_id=collective_id),
    )(*partials)


def _row_tile(rows):
    return next(t for t in range(min(rows, 256), 0, -16) if rows % t == 0)


def _sum_parts(parts, name, tokens=()):
    _, rows, cols = parts.shape
    tr = _row_tile(rows)

    def body(p_ref, g_ref):
        g = p_ref[0].astype(_F32)
        for s in range(1, N_DEV):
            g = g + p_ref[s].astype(_F32)
        g_ref[...] = g

    return _call(body, (rows // tr,), [pl.BlockSpec((N_DEV, tr, cols), lambda i: (0, i, 0))],
                 _rows(tr, cols), _sds((rows, cols), _F32), name, tokens=tokens)(parts)


def _adam_update(g, w, m, v):
    new_m = ADAM_B1 * m + (1.0 - ADAM_B1) * g
    new_v = ADAM_B2 * v + (1.0 - ADAM_B2) * (g * g)
    m_hat = new_m / (1.0 - ADAM_B1 ** ADAM_STEP)
    v_hat = new_v / (1.0 - ADAM_B2 ** ADAM_STEP)
    return -ADAM_LR * (m_hat / (jnp.sqrt(v_hat) + ADAM_EPS) + ADAM_WD * w), new_m, new_v


def _adamw_small(parts, items, sums, name, tokens=()):
    n_p, n_i = len(parts), len(items)

    def body(*refs):
        p_refs, state, outs = refs[:n_p], refs[n_p:n_p + 3 * n_i], refs[n_p + 3 * n_i:]

        def total(part, rows, cols):
            shift = cols.start % _LANES
            window = slice(cols.start - shift, cols.start - shift + _LANES) if shift else cols
            n_rows = rows.stop - rows.start
            narrow = p_refs[part].dtype.itemsize < 4 and n_rows % _PACK_TILE
            tile = slice(rows.start, rows.start + _PACK_TILE) if narrow else rows
            g = p_refs[part][0, tile, window].astype(_F32)
            for s in range(1, N_DEV):
                g = g + p_refs[part][s, tile, window].astype(_F32)
            g = g[:n_rows] if narrow else g
            return pltpu.roll(g, _LANES - shift, 1)[:, :cols.stop - cols.start] if shift else g

        for i, (part, rows, cols, _, _, _) in enumerate(items):
            g = total(part, rows, cols)
            w_ref, m_ref, v_ref = state[3 * i:3 * i + 3]
            delta, new_m, new_v = _adam_update(g, w_ref[...], m_ref[...], v_ref[...])
            outs[4 * i][...] = g
            outs[4 * i + 1][...] = delta
            outs[4 * i + 2][...] = new_m
            outs[4 * i + 3][...] = new_v
        for j, (part, rows, cols) in enumerate(sums):
            outs[4 * n_i + j][...] = total(part, rows, cols)

    ins = list(parts) + [a for item in items for a in item[3:]]
    out_shapes = [item[3].shape for item in items for _ in range(4)]
    out_shapes += [(rows.stop - rows.start, cols.stop - cols.start) for _, rows, cols in sums]
    out = _call(body, (1,), [_whole(a.shape) for a in ins], [_whole(s) for s in out_shapes],
                [_sds(s, _F32) for s in out_shapes], name, tokens=tokens)(*ins)
    return [out[4 * i:4 * i + 4] for i in range(n_i)], out[4 * n_i:]


def _adamw(parts, w, m, v, name, tokens=()):
    rows, cols = w.shape
    tr = _row_tile(rows)
    n_parts = parts.shape[0]

    def body(p_ref, w_ref, m_ref, v_ref, g_ref, d_ref, nm_ref, nv_ref):
        g = p_ref[0].astype(_F32)
        for s in range(1, n_parts):
            g = g + p_ref[s].astype(_F32)
        new_m = ADAM_B1 * m_ref[...] + (1.0 - ADAM_B1) * g
        new_v = ADAM_B2 * v_ref[...] + (1.0 - ADAM_B2) * (g * g)
        m_hat = new_m / (1.0 - ADAM_B1 ** ADAM_STEP)
        v_hat = new_v / (1.0 - ADAM_B2 ** ADAM_STEP)
        g_ref[...] = g
        d_ref[...] = -ADAM_LR * (m_hat / (jnp.sqrt(v_hat) + ADAM_EPS) + ADAM_WD * w_ref[...])
        nm_ref[...] = new_m
        nv_ref[...] = new_v

    blk = _rows(tr, cols)
    return _call(body, (rows // tr,),
                 [pl.BlockSpec((n_parts, tr, cols), lambda i: (0, i, 0)), blk, blk, blk],
                 [blk] * 4, [_sds((rows, cols), _F32)] * 4, name, tokens=tokens)(parts, w, m, v)


_SMALL = ("g_pre_mix", "ssm_lambda_re", "ssm_lambda_im", "ssm_log_dt", "ssm_b_re", "ssm_b_im",
          "ssm_c_re", "ssm_c_im", "ssm_d", "b_glu", "attn_sinks", "g_ssm_out", "g_attn_out",
          "g_post_mix", "g_pre_ffn", "g_post_ffn")
_BIG = ("w_in", "w_glu", "w_out", "w_gate_up", "w_down")
_WEIGHTS = ("g_pre_mix", "w_in", "ssm_lambda_re", "ssm_lambda_im", "ssm_log_dt", "ssm_b_re", "ssm_b_im",
            "ssm_c_re", "ssm_c_im", "ssm_d", "w_glu", "b_glu", "attn_sinks", "g_ssm_out", "g_attn_out",
            "w_out", "g_post_mix", "g_pre_ffn", "w_gate_up", "w_down", "g_post_ffn")
_LANES = 128


_SHAPE_2D = {
    "g_pre_mix": (1, D_MODEL), "ssm_lambda_re": (SSM_GROUPS, SSM_STATE), "ssm_lambda_im": (SSM_GROUPS, SSM_STATE),
    "ssm_log_dt": (1, SSM_GROUPS), "ssm_b_re": (SSM_WIDTH, SSM_STATE), "ssm_b_im": (SSM_WIDTH, SSM_STATE),
    "ssm_c_re": (SSM_WIDTH, SSM_STATE), "ssm_c_im": (SSM_WIDTH, SSM_STATE), "ssm_d": (SSM_GROUPS, SSM_GROUP),
    "b_glu": (1, 2 * SSM_WIDTH), "attn_sinks": (1, N_Q_HEADS), "g_ssm_out": (1, SSM_WIDTH),
    "g_attn_out": (1, ATTN_WIDTH), "g_post_mix": (1, D_MODEL), "g_pre_ffn": (1, D_MODEL), "g_post_ffn": (1, D_MODEL)}
_ROW_WIDTH = {"g_pre_mix": D_MODEL, "b_glu": 2 * SSM_WIDTH, "attn_sinks": _LANES, "g_ssm_out": SSM_WIDTH,
              "g_attn_out": ATTN_WIDTH, "g_post_mix": D_MODEL, "g_pre_ffn": D_MODEL, "g_post_ffn": D_MODEL,
              "loss": _LANES}
_DENSE = ()
_PER_GROUP_TRANSPOSED = ("ssm_b_re", "ssm_b_im")


def _to_2d(name, a):
    if name in _PER_GROUP_TRANSPOSED:
        a = a.reshape(SSM_GROUPS, SSM_STATE, SSM_GROUP).transpose(0, 2, 1)
    return a.reshape(_SHAPE_2D[name])


def _from_2d(name, a, shape):
    if name in _PER_GROUP_TRANSPOSED:
        a = a.reshape(SSM_GROUPS, SSM_GROUP, SSM_STATE).transpose(0, 2, 1)
    return a.reshape(shape)


def _row_slots(names):
    slots, row, col = {}, 0, 0
    for n in names:
        width = _ROW_WIDTH[n]
        if col + width > D_MODEL:
            row, col = row + 1, 0
        slots[n] = (row, col, width)
        col += width
    return slots


def _stack_rows(named, slots):
    n_rows = -(-(max(r for r, _, _ in slots.values()) + 1) // 8) * 8
    lines = []
    for r in range(n_rows):
        pieces = [named[n] for n, (row, _, _) in slots.items() if row == r]
        used = sum(p.shape[1] for p in pieces)
        if used < D_MODEL:
            pieces.append(jnp.zeros((1, D_MODEL - used), _F32))
        lines.append(jnp.concatenate(pieces, axis=1) if len(pieces) > 1 else pieces[0])
    return jnp.concatenate(lines, axis=0)


def kernel(x, positions, g_pre_mix, w_in, ssm_lambda_re, ssm_lambda_im, ssm_log_dt, ssm_b_re, ssm_b_im, ssm_c_re, ssm_c_im, ssm_d, w_glu, b_glu, attn_sinks, g_ssm_out, g_attn_out, w_out, g_post_mix, g_pre_ffn, w_gate_up, w_down, g_post_ffn, loss_target, m_g_pre_mix, m_w_in, m_ssm_lambda_re, m_ssm_lambda_im, m_ssm_log_dt, m_ssm_b_re, m_ssm_b_im, m_ssm_c_re, m_ssm_c_im, m_ssm_d, m_w_glu, m_b_glu, m_attn_sinks, m_g_ssm_out, m_g_attn_out, m_w_out, m_g_post_mix, m_g_pre_ffn, m_w_gate_up, m_w_down, m_g_post_ffn, v_g_pre_mix, v_w_in, v_ssm_lambda_re, v_ssm_lambda_im, v_ssm_log_dt, v_ssm_b_re, v_ssm_b_im, v_ssm_c_re, v_ssm_c_im, v_ssm_d, v_w_glu, v_b_glu, v_attn_sinks, v_g_ssm_out, v_g_attn_out, v_w_out, v_g_post_mix, v_g_pre_ffn, v_w_gate_up, v_w_down, v_g_post_ffn):
    w = dict(g_pre_mix=g_pre_mix, w_in=w_in, ssm_lambda_re=ssm_lambda_re, ssm_lambda_im=ssm_lambda_im,
             ssm_log_dt=ssm_log_dt, ssm_b_re=ssm_b_re, ssm_b_im=ssm_b_im, ssm_c_re=ssm_c_re, ssm_c_im=ssm_c_im,
             ssm_d=ssm_d, w_glu=w_glu, b_glu=b_glu, attn_sinks=attn_sinks, g_ssm_out=g_ssm_out,
             g_attn_out=g_attn_out, w_out=w_out, g_post_mix=g_post_mix, g_pre_ffn=g_pre_ffn,
             w_gate_up=w_gate_up, w_down=w_down, g_post_ffn=g_post_ffn)
    m = dict(g_pre_mix=m_g_pre_mix, w_in=m_w_in, ssm_lambda_re=m_ssm_lambda_re, ssm_lambda_im=m_ssm_lambda_im,
             ssm_log_dt=m_ssm_log_dt, ssm_b_re=m_ssm_b_re, ssm_b_im=m_ssm_b_im, ssm_c_re=m_ssm_c_re,
             ssm_c_im=m_ssm_c_im, ssm_d=m_ssm_d, w_glu=m_w_glu, b_glu=m_b_glu, attn_sinks=m_attn_sinks,
             g_ssm_out=m_g_ssm_out, g_attn_out=m_g_attn_out, w_out=m_w_out, g_post_mix=m_g_post_mix,
             g_pre_ffn=m_g_pre_ffn, w_gate_up=m_w_gate_up, w_down=m_w_down, g_post_ffn=m_g_post_ffn)
    v = dict(g_pre_mix=v_g_pre_mix, w_in=v_w_in, ssm_lambda_re=v_ssm_lambda_re, ssm_lambda_im=v_ssm_lambda_im,
             ssm_log_dt=v_ssm_log_dt, ssm_b_re=v_ssm_b_re, ssm_b_im=v_ssm_b_im, ssm_c_re=v_ssm_c_re,
             ssm_c_im=v_ssm_c_im, ssm_d=v_ssm_d, w_glu=v_w_glu, b_glu=v_b_glu, attn_sinks=v_attn_sinks,
             g_ssm_out=v_g_ssm_out, g_attn_out=v_g_attn_out, w_out=v_w_out, g_post_mix=v_g_post_mix,
             g_pre_ffn=v_g_pre_ffn, w_gate_up=v_w_gate_up, w_down=v_w_down, g_post_ffn=v_g_post_ffn)

    transposed = ("w_in", "w_glu", "w_gate_up")
    native_transposed = ("w_in", "w_gate_up")
    shard = {n: (w[n][0].T if n in transposed else w[n][0]).astype(_BF16) for n in _BIG}
    gathered = {}
    for names, lands in (
            (("w_in",), _sequencer_exchange([shard["w_in"]], [False], "gather_w_in", 1)),
            (("w_glu", "w_out"), _sequencer_exchange([shard["w_glu"], shard["w_out"]], [False] * 2, "gather_mix", 2)),
            (("w_gate_up", "w_down"), _sequencer_gather([shard["w_gate_up"], shard["w_down"]], "gather_ffn", 3))):
        gathered.update({n: a.reshape(-1, a.shape[2]) for n, a in zip(names, lands)})

    def fetch(names, after):
        del after
        return [gathered[n] for n in names]

    sent = []
    ids = iter(range(4, 16))
    two_step = {}

    def publish(named):
        big = [n for n in named if n in _BIG]
        if set(big) == {"w_gate_up", "w_down"}:
            blocks = [named[n].reshape(N_DEV, -1, named[n].shape[1]) for n in big]
            two_step.update(names=big, blocks=blocks,
                            received=_sequencer_pair_exchange(blocks, "grads_pair", next(ids)))
            return [named[n] for n in big]
        rows = [n for n in named if n in _ROW_WIDTH]
        dense = [n for n in named if n in _DENSE]
        plain = [n for n in named if n not in big + rows + dense]
        sources = [named[n].reshape(N_DEV, -1, named[n].shape[1]) for n in big]
        slots = _row_slots(rows)
        if rows:
            sources.append(_stack_rows(named, slots))
        sources += [named[n].reshape(-1, _LANES) for n in dense] + [named[n] for n in plain]
        flags = [True] * len(big) + [False] * (len(sources) - len(big))
        cid = next(ids)
        sent.append((big, slots, dense, plain, _sequencer_exchange(sources, flags, "grads_%d" % cid, cid)))
        return [named[n] for n in big]

    def progress(after):
        core = lax.axis_index("c").astype(jnp.int32).reshape(1)
        partials = [_pair_sum(b, r, core, "pair_sum_" + n, [after])
                    for n, b, r in zip(two_step["names"], two_step["blocks"], two_step["received"])]
        sent.append((two_step["names"], {}, [], [], _sequencer_chip_exchange(partials, "grads_chips", next(ids))))
        return partials

    p = {n: w[n] for n in _SMALL}
    grad_x = _local_step(x[0], positions[0], loss_target[0], p, fetch, publish, progress)

    state = {n: [_to_2d(n, a) for a in (w[n], m[n], v[n])] for n in _SMALL}
    result = {}
    total_loss = None
    chain = []
    for big, slots, dense, plain, lands in sent:
        lands = list(lands)
        after = list(chain)
        for name in big:
            part = lands.pop(0)
            if name in native_transposed:
                updated = _adamw(part, w[name][0].T, m[name][0].T, v[name][0].T, "adamw_" + name, after)
                result[name] = [a.T[None] for a in updated]
                chain.append(updated[3])
                continue
            if name in transposed:
                part = _sum_parts(part, "sum_" + name, after).T[None]
            updated = _adamw(part, w[name][0], m[name][0], v[name][0], "adamw_" + name, after)
            result[name] = [a[None] for a in updated]
            chain.append(updated[3])
        parts, items, sums, names = [], [], [], []
        if slots:
            parts.append(lands.pop(0))
            for name, (row, col, _) in slots.items():
                if name == "loss":
                    sums.append((0, slice(row, row + 1), slice(col, col + _LANES)))
                else:
                    items.append((0, slice(row, row + 1), slice(col, col + _SHAPE_2D[name][1]), *state[name]))
                    names.append(name)
        for name in dense:
            part = lands.pop(0).reshape((N_DEV,) + _SHAPE_2D[name])
            result[name] = _adamw(part, *state[name], "adamw_" + name, after)
            chain.append(result[name][3])
        for name in plain:
            packed = _SSM_PACK if name == "ssm_pack" else {name: (0, _SHAPE_2D[name][0], 0, _SHAPE_2D[name][1])}
            for member, (first, rows_n, lane, cols_n) in packed.items():
                items.append((len(parts), slice(first, first + rows_n), slice(lane, lane + cols_n), *state[member]))
                names.append(member)
            parts.append(lands.pop(0))
        if items:
            updated, summed = _adamw_small(parts, items, sums, "adamw_small_" + names[0], after)
            chain.append(updated[0][3])
            result.update(dict(zip(names, updated)))
            if summed:
                total_loss = summed[0][0, 0]

    out = [total_loss, grad_x[None]]
    for kind in range(4):
        out += [_from_2d(n, result[n][kind], w[n].shape) for n in _WEIGHTS]
    return tuple(out)
```

```python
import functools
import math

import numpy as np
import jax
import jax.numpy as jnp
from jax import lax
from jax.experimental import pallas as pl
from jax.experimental.pallas import tpu as pltpu
from jax.experimental.pallas import tpu_sc as plsc

D_MODEL = 1024
SSM_WIDTH = 512
SSM_GROUP = 16
SSM_GROUPS = 32
SSM_STATE = 64
N_STATE = SSM_GROUPS * SSM_STATE
ATTN_WIDTH = 512
HEAD_DIM = 64
N_Q_HEADS = 8
N_KV_HEADS = 2
Q_PER_KV = 4
KV_WIDTH = 128
IN_WIDTH = 1280
BLOCK = 128
ROPE_DIM = 16
ROPE_THETA = 500000.0
D_FF = 2816
NORM_EPS = 1e-6
MASK_VALUE = -1e30
ADAM_LR = 0.001
ADAM_B1 = 0.9
ADAM_B2 = 0.999
ADAM_EPS = 1e-08
ADAM_WD = 0.01
ADAM_STEP = 10

N_DEV = 8
SCAN_CHUNKS = 8
SCAN_COLS = 512
SCAN_UNROLL = 8
TOKEN_TILE = 256
VMEM_LIMIT = 56 * 1024 * 1024

_F32 = jnp.float32
_BF16 = jnp.bfloat16
_MXU = jnp.bfloat16

_NN = ((1,), (0,))
_NT = ((1,), (1,))
_TN = ((0,), (0,))


def _dot(a, b, dims):
    return lax.dot_general(a.astype(_MXU), b.astype(_MXU), (dims, ((), ())),
                           preferred_element_type=_F32)


def _dot_exact(a, b, dims):
    return lax.dot_general(a.astype(_F32), b.astype(_F32), (dims, ((), ())),
                           precision=lax.Precision.HIGHEST, preferred_element_type=_F32)


def _iota(shape, dim):
    return lax.broadcasted_iota(jnp.int32, shape, dim)


def _rms_fwd(x, g):
    r = lax.rsqrt(jnp.mean(x * x, axis=-1, keepdims=True) + NORM_EPS)
    return x * r * g, r


def _rms_bwd(dy, x, g, r):
    a = dy * g
    xn = x * r
    dx = r * (a - xn * jnp.mean(a * xn, axis=-1, keepdims=True))
    dg = jnp.sum(dy * xn, axis=0, keepdims=True)
    return dx, dg


def _call(body, grid, in_specs, out_specs, out_shape, name, scratch=(), tokens=()):
    params = pltpu.CompilerParams(dimension_semantics=("arbitrary",) * len(grid),
                                  vmem_limit_bytes=VMEM_LIMIT)
    n_in, n_tok = len(in_specs), len(tokens)

    def run(*refs):
        return body(*refs[:n_in], *refs[n_in + n_tok:])

    call = pl.pallas_call(run, grid=grid,
                          in_specs=list(in_specs) + [pl.BlockSpec(memory_space=pl.ANY)] * n_tok,
                          out_specs=out_specs, out_shape=out_shape, scratch_shapes=list(scratch),
                          compiler_params=params, name=name)
    return lambda *args: call(*args, *tokens)


def _rows(tm, n):
    return pl.BlockSpec((tm, n), lambda i: (i, 0))


def _whole(shape):
    nd = len(shape)
    return pl.BlockSpec(shape, lambda i: (0,) * nd)


def _sds(shape, dtype):
    return jax.ShapeDtypeStruct(shape, dtype)


def _tile(L):
    return min(TOKEN_TILE, L)


def _chunk_tile(L):
    return L // SCAN_CHUNKS


def _chunk_block(L, n):
    return pl.BlockSpec((_chunk_tile(L), n), lambda i: (0, i))


def _chunk_shape(L, n):
    return (_chunk_tile(L), SCAN_CHUNKS * n)


def _accumulate(ref, val, first):
    @pl.when(first)
    def _():
        ref[...] = val

    @pl.when(jnp.logical_not(first))
    def _():
        ref[...] += val


def _rope_rows():
    half = ROPE_DIM // 2
    inv = (np.float32(ROPE_THETA) ** (-np.arange(half, dtype=np.float32) * np.float32(2.0) / np.float32(ROPE_DIM))).astype(np.float32)
    col = np.arange(KV_WIDTH) % HEAD_DIM
    freq = np.where(col < ROPE_DIM, inv[col % half], 0.0).astype(np.float32)
    sign = np.where(col < half, -1.0, np.where(col < ROPE_DIM, 1.0, 0.0)).astype(np.float32)
    return freq[None, :], sign[None, :]


def _rope_tables(pos_col):
    L = pos_col.shape[0]
    tm = _tile(L)
    freq, sign = _rope_rows()

    def body(pos_ref, freq_ref, sign_ref, cos_ref, sin_ref):
        ang = pos_ref[...].astype(_F32) * freq_ref[...]
        cos_ref[...] = jnp.cos(ang)
        sin_ref[...] = jnp.sin(ang) * sign_ref[...]

    return _call(body, (L // tm,),
                 [_rows(tm, 1), _whole((1, KV_WIDTH)), _whole((1, KV_WIDTH))],
                 [_rows(tm, KV_WIDTH), _rows(tm, KV_WIDTH)],
                 [_sds((L, KV_WIDTH), _F32)] * 2, "rope_tables")(pos_col, jnp.asarray(freq), jnp.asarray(sign))


def _widen(t, width):
    return t if width == KV_WIDTH else jnp.concatenate([t] * (width // KV_WIDTH), axis=1)


def _rope_partner(t):
    w = t.shape[1]
    in_head = _iota((1, w), 1) & (HEAD_DIM - 1)
    second = jnp.where(in_head < ROPE_DIM, pltpu.roll(t, ROPE_DIM // 2, 1), 0.0)
    return jnp.where(in_head < ROPE_DIM // 2, pltpu.roll(t, w - ROPE_DIM // 2, 1), second)


def _rope_apply(t, cos_t, sin_t):
    w = t.shape[1]
    return t * _widen(cos_t, w) + _rope_partner(t) * _widen(sin_t, w)


def _rope_transpose(dt, cos_t, sin_t):
    w = dt.shape[1]
    return dt * _widen(cos_t, w) + _rope_partner(dt * _widen(sin_t, w))


def _in_proj(x, g_pre_mix, w_in, cos_t, sin_t):
    L = x.shape[0]
    tm = _chunk_tile(L)

    def body(x_ref, g_ref, w_ref, cos_ref, sin_ref, hn_ref, u_ref, q_ref, k_ref, v_ref):
        hn, _ = _rms_fwd(x_ref[...], g_ref[...])
        hn = hn.astype(_BF16)
        hn_ref[...] = hn
        proj = _dot(hn, w_ref[...], _NT)
        u_ref[...] = proj[:, :SSM_WIDTH]
        q = proj[:, SSM_WIDTH:SSM_WIDTH + ATTN_WIDTH]
        k = proj[:, SSM_WIDTH + ATTN_WIDTH:SSM_WIDTH + ATTN_WIDTH + KV_WIDTH]
        cos_v, sin_v = cos_ref[...], sin_ref[...]
        q_ref[...] = _rope_apply(q, cos_v, sin_v).astype(_BF16)
        k_ref[...] = _rope_apply(k, cos_v, sin_v).astype(_BF16)
        v_ref[...] = proj[:, SSM_WIDTH + ATTN_WIDTH + KV_WIDTH:].astype(_BF16)

    return _call(body, (L // tm,),
                 [_rows(tm, D_MODEL), _whole((1, D_MODEL)), _whole((IN_WIDTH, D_MODEL)),
                  _rows(tm, KV_WIDTH), _rows(tm, KV_WIDTH)],
                 [_rows(tm, D_MODEL), _chunk_block(L, SSM_WIDTH), _rows(tm, ATTN_WIDTH),
                  _rows(tm, KV_WIDTH), _rows(tm, KV_WIDTH)],
                 [_sds((L, D_MODEL), _BF16), _sds(_chunk_shape(L, SSM_WIDTH), _F32), _sds((L, ATTN_WIDTH), _BF16),
                  _sds((L, KV_WIDTH), _BF16), _sds((L, KV_WIDTH), _BF16)],
                 "in_proj")(x, g_pre_mix, w_in, cos_t, sin_t)


def _s5_discretize(lam_re, lam_im, log_dt):
    lr = jnp.minimum(lam_re, -1e-4)
    li = lam_im
    dt = jnp.exp(log_dt)
    mag = jnp.exp(lr * dt)
    ar = mag * jnp.cos(li * dt)
    ai = mag * jnp.sin(li * dt)
    den = lr * lr + li * li
    fr = ((ar - 1.0) * lr + ai * li) / den
    fi = (ai * lr - (ar - 1.0) * li) / den
    return ar, ai, fr, fi


def _s5_bbar(lam_re, lam_im, log_dt, b_re, b_im):
    ar, ai, fr, fi = _s5_discretize(lam_re, lam_im, log_dt)
    return ar, ai, fr * b_re - fi * b_im, fr * b_im + fi * b_re


def _spread_masks():
    e16 = (_iota((SSM_GROUP, SSM_WIDTH), 1) & (SSM_GROUP - 1)) == _iota((SSM_GROUP, SSM_WIDTH), 0)
    e64 = (_iota((SSM_STATE, N_STATE), 1) & (SSM_STATE - 1)) == _iota((SSM_STATE, N_STATE), 0)
    mask_b = (_iota((N_STATE, SSM_WIDTH), 0) >> 6) == (_iota((N_STATE, SSM_WIDTH), 1) >> 4)
    mask_c = (_iota((SSM_WIDTH, N_STATE), 0) >> 4) == (_iota((SSM_WIDTH, N_STATE), 1) >> 6)
    return e16.astype(_F32), e64.astype(_F32), mask_b, mask_c


SUPER = 4
SB_STATE = N_STATE // SUPER
SB_WIDTH = SSM_WIDTH // SUPER


def _sb_state(k):
    return slice(SB_STATE * k, SB_STATE * (k + 1))


def _sb_width(k):
    return slice(SB_WIDTH * k, SB_WIDTH * (k + 1))


def _dt_column(log_dt_row):
    eye = _iota((SSM_GROUPS, SSM_GROUPS), 0) == _iota((SSM_GROUPS, SSM_GROUPS), 1)
    return jnp.sum(jnp.where(eye, log_dt_row, 0.0), axis=1, keepdims=True)


def _group_masks():
    e64 = ((_iota((SSM_STATE, N_STATE), 1) & (SSM_STATE - 1)) == _iota((SSM_STATE, N_STATE), 0)).astype(_F32)
    own = _iota((SSM_GROUPS, N_STATE), 0) == (_iota((SSM_GROUPS, N_STATE), 1) >> 6)
    return e64, own


def _rows_of_group():
    return ((_iota((SSM_WIDTH, SSM_GROUPS), 0) >> 4) == _iota((SSM_WIDTH, SSM_GROUPS), 1)).astype(_F32)


def _ssm_prep(lam_re, lam_im, log_dt, b_re, b_im, c_re, c_im):
    def body(lr_ref, li_ref, ld_ref, bre, bim, cre, cim, ar_ref, ai_ref, btr, bti, ctr, cti):
        ar, ai, fr, fi = _s5_discretize(lr_ref[...], li_ref[...], _dt_column(ld_ref[...]))
        e64, own = _group_masks()
        mask_c = (_iota((SSM_WIDTH, N_STATE), 0) >> 4) == (_iota((SSM_WIDTH, N_STATE), 1) >> 6)

        def to_row(t):
            return jnp.sum(jnp.where(own, _dot_exact(t, e64, _NN), 0.0), axis=0, keepdims=True)

        def fold(m):
            full = jnp.where(mask_c, _dot(m, e64, _NN), 0.0)
            return sum(full[_sb_width(k), :] for k in range(SUPER)).astype(_BF16)

        ar_ref[...] = to_row(ar)
        ai_ref[...] = to_row(ai)
        spread = _rows_of_group()
        fr_t = _dot_exact(spread, fr, _NN)
        fi_t = _dot_exact(spread, fi, _NN)
        btr[...] = fold(fr_t * bre[...] - fi_t * bim[...])
        bti[...] = fold(fr_t * bim[...] + fi_t * bre[...])
        ctr[...] = fold(cre[...])
        cti[...] = fold(cim[...])

    row = (1, N_STATE)
    ins = [lam_re, lam_im, log_dt, b_re, b_im, c_re, c_im]
    return _call(body, (1,), [_whole(a.shape) for a in ins],
                 [_whole(row), _whole(row)] + [_whole((SB_WIDTH, N_STATE))] * 4,
                 [_sds(row, _F32), _sds(row, _F32)] + [_sds((SB_WIDTH, N_STATE), _BF16)] * 4,
                 "ssm_prep")(*ins)


def _ssm_bu(u, bt_re, bt_im):
    L = u.shape[0]
    tm = _tile(L)

    def body(u_ref, br_ref, bi_ref, or_ref, oi_ref):
        for k in range(SUPER):
            ub = u_ref[:, _sb_width(k)].astype(_BF16)
            or_ref[:, _sb_state(k)] = _dot(ub, br_ref[:, _sb_state(k)], _NN)
            oi_ref[:, _sb_state(k)] = _dot(ub, bi_ref[:, _sb_state(k)], _NN)

    return _call(body, (L // tm,),
                 [_rows(tm, SSM_WIDTH), _whole((SB_WIDTH, N_STATE)), _whole((SB_WIDTH, N_STATE))],
                 [_rows(tm, N_STATE), _rows(tm, N_STATE)],
                 [_sds((L, N_STATE), _F32)] * 2, "ssm_bu")(u, bt_re, bt_im)


def _complex_power(ar, ai, n):
    def step(_, c):
        pr, pi = c
        return pr * ar - pi * ai, pr * ai + pi * ar
    return lax.fori_loop(0, n, step, (jnp.ones_like(ar), jnp.zeros_like(ai)))


def _chunk_carries(er, ei, pr, pi, reverse):
    rows = _iota(er.shape, 0)
    sr = jnp.zeros_like(pr)
    si = jnp.zeros_like(pi)
    out_r = jnp.zeros_like(er)
    out_i = jnp.zeros_like(ei)
    order = range(SCAN_CHUNKS - 1, 0, -1) if reverse else range(SCAN_CHUNKS - 1)
    for c in order:
        e_r = er[c:c + 1, :]
        e_i = ei[c:c + 1, :]
        sr, si = pr * sr - pi * si + e_r, pr * si + pi * sr + e_i
        nxt = c - 1 if reverse else c + 1
        out_r = jnp.where(rows == nxt, sr, out_r)
        out_i = jnp.where(rows == nxt, si, out_i)
    return out_r, out_i


def _scan_fwd(b_re, b_im, a_re, a_im):
    T = b_re.shape[0]
    W = SCAN_COLS
    blk = pl.BlockSpec((T, SCAN_CHUNKS, W), lambda j: (0, 0, j))
    vec = pl.BlockSpec((1, W), lambda j: (0, j))

    def body(br_ref, bi_ref, ar_ref, ai_ref, xr_ref, xi_ref):
        ar, ai = ar_ref[...], ai_ref[...]
        ar8 = jnp.broadcast_to(ar, (SCAN_CHUNKS, W))
        ai8 = jnp.broadcast_to(ai, (SCAN_CHUNKS, W))

        def local(t, c):
            cr, ci = c
            return ar8 * cr - ai8 * ci + br_ref[t], ar8 * ci + ai8 * cr + bi_ref[t]

        zero = jnp.zeros((SCAN_CHUNKS, W), _F32)
        er, ei = lax.fori_loop(0, T, local, (zero, zero), unroll=SCAN_UNROLL)
        pr, pi = _complex_power(ar, ai, T)
        sr, si = _chunk_carries(er, ei, pr, pi, reverse=False)

        def final(t, c):
            nr, ni = local(t, c)
            xr_ref[t] = nr
            xi_ref[t] = ni
            return nr, ni

        lax.fori_loop(0, T, final, (sr, si), unroll=SCAN_UNROLL)

    shape = _sds(b_re.shape, _F32)
    return _call(body, (N_STATE // W,), [blk, blk, vec, vec], [blk, blk], [shape, shape],
                 "scan_fwd")(b_re, b_im, a_re, a_im)


def _scan_bwd(dx_re, dx_im, x_re, x_im, a_re, a_im, tokens=()):
    T = dx_re.shape[0]
    W = SCAN_COLS
    blk = pl.BlockSpec((T, SCAN_CHUNKS, W), lambda j: (0, 0, j))
    vec = pl.BlockSpec((1, W), lambda j: (0, j))

    def body(dr_ref, di_ref, xr_ref, xi_ref, ar_ref, ai_ref, lr_ref, li_ref, dar_ref, dai_ref):
        ar, ai = ar_ref[...], ai_ref[...]
        ar8 = jnp.broadcast_to(ar, (SCAN_CHUNKS, W))
        ai8 = jnp.broadcast_to(ai, (SCAN_CHUNKS, W))

        def local(t, c):
            cr, ci = c
            return ar8 * cr + ai8 * ci + dr_ref[t], ar8 * ci - ai8 * cr + di_ref[t]

        zero = jnp.zeros((SCAN_CHUNKS, W), _F32)
        er, ei = lax.fori_loop(0, T, lambda k, c: local(T - 1 - k, c), (zero, zero), unroll=SCAN_UNROLL)
        pr, pi = _complex_power(ar, -ai, T)
        sr, si = _chunk_carries(er, ei, pr, pi, reverse=True)

        def grad_a(acc, nr, ni, xpr, xpi):
            return acc[0] + nr * xpr + ni * xpi, acc[1] + ni * xpr - nr * xpi

        def final(k, c):
            t = T - 1 - k
            nr, ni = local(t, c[:2])
            lr_ref[t] = nr
            li_ref[t] = ni
            gr, gi = grad_a(c[2:], nr, ni, xr_ref[t - 1], xi_ref[t - 1])
            return nr, ni, gr, gi

        cr, ci, gr, gi = lax.fori_loop(0, T - 1, final, (sr, si, zero, zero), unroll=SCAN_UNROLL)
        nr, ni = local(0, (cr, ci))
        lr_ref[0] = nr
        li_ref[0] = ni
        first = _iota((SCAN_CHUNKS, W), 0) == 0
        xpr = jnp.where(first, 0.0, pltpu.roll(xr_ref[T - 1], 1, 0))
        xpi = jnp.where(first, 0.0, pltpu.roll(xi_ref[T - 1], 1, 0))
        gr, gi = grad_a((gr, gi), nr, ni, xpr, xpi)
        dar_ref[...] = jnp.sum(gr, axis=0, keepdims=True)
        dai_ref[...] = jnp.sum(gi, axis=0, keepdims=True)

    shape = _sds(dx_re.shape, _F32)
    row = _sds((1, N_STATE), _F32)
    return _call(body, (N_STATE // W,), [blk, blk, blk, blk, vec, vec], [blk, blk, vec, vec],
                 [shape, shape, row, row], "scan_bwd", tokens=tokens)(dx_re, dx_im, x_re, x_im, a_re, a_im)


_GELU_K = math.sqrt(2.0 / math.pi)
_GELU_C = 0.044715


def _gelu(y):
    return 0.5 * y * (1.0 + jnp.tanh(_GELU_K * (y + _GELU_C * y * y * y)))


def _gelu_grad(y):
    t = jnp.tanh(_GELU_K * (y + _GELU_C * y * y * y))
    return 0.5 * (1.0 + t) + 0.5 * y * (1.0 - t * t) * _GELU_K * (1.0 + 3.0 * _GELU_C * y * y)


def _step_rows(t):
    return pl.ds(pl.multiple_of(t * SCAN_CHUNKS, SCAN_CHUNKS), SCAN_CHUNKS)


def _scan_in_place(br, bi, ar, ai, T):
    W = br.shape[1]
    ar8 = jnp.broadcast_to(ar, (SCAN_CHUNKS, W))
    ai8 = jnp.broadcast_to(ai, (SCAN_CHUNKS, W))

    def local(t, c):
        cr, ci = c
        rows = _step_rows(t)
        return ar8 * cr - ai8 * ci + br[rows, :], ar8 * ci + ai8 * cr + bi[rows, :]

    zero = jnp.zeros((SCAN_CHUNKS, W), _F32)
    er, ei = lax.fori_loop(0, T, local, (zero, zero), unroll=SCAN_UNROLL)
    pr, pi = _complex_power(ar, ai, T)
    sr, si = _chunk_carries(er, ei, pr, pi, reverse=False)

    def final(t, c):
        nr, ni = local(t, c)
        rows = _step_rows(t)
        br[rows, :] = nr
        bi[rows, :] = ni
        return nr, ni

    lax.fori_loop(0, T, final, (sr, si), unroll=SCAN_UNROLL)


def _scan_reverse_in_place(dr, di, xr, xi, ar, ai, T):
    W = dr.shape[1]
    ar8 = jnp.broadcast_to(ar, (SCAN_CHUNKS, W))
    ai8 = jnp.broadcast_to(ai, (SCAN_CHUNKS, W))

    def local(t, c):
        cr, ci = c
        rows = _step_rows(t)
        return ar8 * cr + ai8 * ci + dr[rows, :], ar8 * ci - ai8 * cr + di[rows, :]

    zero = jnp.zeros((SCAN_CHUNKS, W), _F32)
    er, ei = lax.fori_loop(0, T, lambda k, c: local(T - 1 - k, c), (zero, zero), unroll=SCAN_UNROLL)
    pr, pi = _complex_power(ar, -ai, T)
    sr, si = _chunk_carries(er, ei, pr, pi, reverse=True)

    def grad_a(acc, nr, ni, xpr, xpi):
        return acc[0] + nr * xpr + ni * xpi, acc[1] + ni * xpr - nr * xpi

    def final(k, c):
        t = T - 1 - k
        nr, ni = local(t, c[:2])
        rows = _step_rows(t)
        dr[rows, :] = nr
        di[rows, :] = ni
        before = _step_rows(t - 1)
        gr, gi = grad_a(c[2:], nr, ni, xr[before, :], xi[before, :])
        return nr, ni, gr, gi

    cr, ci, gr, gi = lax.fori_loop(0, T - 1, final, (sr, si, zero, zero), unroll=SCAN_UNROLL)
    nr, ni = local(0, (cr, ci))
    dr[_step_rows(0), :] = nr
    di[_step_rows(0), :] = ni
    first = _iota((SCAN_CHUNKS, W), 0) == 0
    last = _step_rows(T - 1)
    xpr = jnp.where(first, 0.0, pltpu.roll(xr[last, :], 1, 0))
    xpi = jnp.where(first, 0.0, pltpu.roll(xi[last, :], 1, 0))
    gr, gi = grad_a((gr, gi), nr, ni, xpr, xpi)
    return jnp.sum(gr, axis=0, keepdims=True), jnp.sum(gi, axis=0, keepdims=True)


def _ssm_super_specs(L):
    width = pl.BlockSpec((L, SB_WIDTH), lambda k: (0, k))
    matrix = pl.BlockSpec((SB_WIDTH, SB_STATE), lambda k: (0, k))
    row = pl.BlockSpec((1, SB_STATE), lambda k: (0, k))
    return width, matrix, row


def _ssm_states(u_ref, br_ref, bi_ref, ar_ref, ai_ref, xr, xi, T):
    ub = u_ref[...].astype(_BF16)
    xr[...] = _dot(ub, br_ref[...], _NN)
    xi[...] = _dot(ub, bi_ref[...], _NN)
    _scan_in_place(xr, xi, ar_ref[...], ai_ref[...], T)


def _ssm_core_fwd(u, bt_re, bt_im, ct_re, ct_im, a_re, a_im):
    L = u.shape[0]
    T = L // SCAN_CHUNKS

    def body(u_ref, br_ref, bi_ref, cr_ref, ci_ref, ar_ref, ai_ref, y_ref, xr, xi):
        _ssm_states(u_ref, br_ref, bi_ref, ar_ref, ai_ref, xr, xi, T)
        y_ref[...] = _dot(xr[...], cr_ref[...], _NT) - _dot(xi[...], ci_ref[...], _NT)

    width, matrix, row = _ssm_super_specs(L)
    return _call(body, (SUPER,), [width, matrix, matrix, matrix, matrix, row, row], width,
                 _sds((L, SSM_WIDTH), _F32), "ssm_core_fwd",
                 scratch=[pltpu.VMEM((L, SB_STATE), _F32)] * 2)(u, bt_re, bt_im, ct_re, ct_im, a_re, a_im)


def _ssm_core_bwd(u, dy, dud, bt_re, bt_im, ct_re, ct_im, a_re, a_im, tokens=()):
    L = u.shape[0]
    T = L // SCAN_CHUNKS

    def body(u_ref, dy_ref, dud_ref, br_ref, bi_ref, cr_ref, ci_ref, ar_ref, ai_ref,
             du_ref, dcr_ref, dci_ref, dbr_ref, dbi_ref, dar_ref, dai_ref, xr, xi, lr, li):
        _ssm_states(u_ref, br_ref, bi_ref, ar_ref, ai_ref, xr, xi, T)
        dyb = dy_ref[...]
        lr[...] = _dot(dyb, cr_ref[...], _NN)
        li[...] = -_dot(dyb, ci_ref[...], _NN)
        da_re, da_im = _scan_reverse_in_place(lr, li, xr, xi, ar_ref[...], ai_ref[...], T)
        dar_ref[...] = da_re
        dai_ref[...] = da_im
        du_ref[...] = _dot(lr[...], br_ref[...], _NT) + _dot(li[...], bi_ref[...], _NT) + dud_ref[...]
        ub = u_ref[...].astype(_BF16)
        dcr_ref[...] = _dot(dyb, xr[...], _TN)
        dci_ref[...] = _dot(dyb, xi[...], _TN)
        dbr_ref[...] = _dot(ub, lr[...], _TN)
        dbi_ref[...] = _dot(ub, li[...], _TN)

    width, matrix, row = _ssm_super_specs(L)
    return _call(body, (SUPER,), [width, width, width, matrix, matrix, matrix, matrix, row, row],
                 [width] + [matrix] * 4 + [row] * 2,
                 [_sds((L, SSM_WIDTH), _F32)] + [_sds((SB_WIDTH, N_STATE), _F32)] * 4 + [_sds((1, N_STATE), _F32)] * 2,
                 "ssm_core_bwd", scratch=[pltpu.VMEM((L, SB_STATE), _F32)] * 4,
                 tokens=tokens)(u, dy, dud, bt_re, bt_im, ct_re, ct_im, a_re, a_im)


def _ssm_out(cx, u, d_row, w_glu, b_glu, g_ssm):
    L = u.shape[0]
    tm = _tile(L)

    def body(cx_ref, u_ref, d_ref, w_ref, b_ref, g_ref, y_ref, z_ref, n_ref):
        y = cx_ref[...] + d_ref[...] * u_ref[...]
        y_ref[...] = y
        z = _dot(_gelu(y), w_ref[...], _NT) + b_ref[...]
        z_ref[...] = z
        out = z[:, :SSM_WIDTH] * jax.nn.sigmoid(z[:, SSM_WIDTH:])
        n, _ = _rms_fwd(out, g_ref[...])
        n_ref[...] = n.astype(_BF16)

    return _call(body, (L // tm,),
                 [_rows(tm, SSM_WIDTH), _rows(tm, SSM_WIDTH), _whole((1, SSM_WIDTH)),
                  _whole((2 * SSM_WIDTH, SSM_WIDTH)), _whole((1, 2 * SSM_WIDTH)), _whole((1, SSM_WIDTH))],
                 [_rows(tm, SSM_WIDTH), _rows(tm, 2 * SSM_WIDTH), _rows(tm, SSM_WIDTH)],
                 [_sds((L, SSM_WIDTH), _F32), _sds((L, 2 * SSM_WIDTH), _F32), _sds((L, SSM_WIDTH), _BF16)],
                 "ssm_out")(cx, u, d_row, w_glu, b_glu, g_ssm)


def _ssm_out_bwd(dn, y, z, u, d_row, w_glu, g_ssm):
    L = u.shape[0]
    tm = _tile(L)

    def body(dn_ref, y_ref, z_ref, u_ref, d_ref, w_ref, g_ref,
             gy_ref, dz_ref, dy_ref, dud_ref, dg_ref, db_ref, dd_ref):
        first = pl.program_id(0) == 0
        z = z_ref[...]
        z1, z2 = z[:, :SSM_WIDTH], z[:, SSM_WIDTH:]
        sig = jax.nn.sigmoid(z2)
        out = z1 * sig
        g = g_ref[...]
        _, r = _rms_fwd(out, g)
        dout, dg = _rms_bwd(dn_ref[...], out, g, r)
        _accumulate(dg_ref, dg, first)
        dz = jnp.concatenate([dout * sig, dout * z1 * sig * (1.0 - sig)], axis=1)
        _accumulate(db_ref, jnp.sum(dz, axis=0, keepdims=True), first)
        dzb = dz.astype(_BF16)
        dz_ref[...] = dzb
        y = y_ref[...]
        gy_ref[...] = _gelu(y).astype(_BF16)
        dy = _dot(dzb, w_ref[...], _NN) * _gelu_grad(y)
        u = u_ref[...]
        _accumulate(dd_ref, jnp.sum(dy * u, axis=0, keepdims=True), first)
        dud_ref[...] = d_ref[...] * dy
        dy_ref[...] = dy.astype(_BF16)

    row = _whole((1, SSM_WIDTH))
    return _call(body, (L // tm,),
                 [_rows(tm, SSM_WIDTH), _rows(tm, SSM_WIDTH), _rows(tm, 2 * SSM_WIDTH), _rows(tm, SSM_WIDTH),
                  row, _whole((2 * SSM_WIDTH, SSM_WIDTH)), row],
                 [_rows(tm, SSM_WIDTH), _rows(tm, 2 * SSM_WIDTH), _rows(tm, SSM_WIDTH), _rows(tm, SSM_WIDTH),
                  row, _whole((1, 2 * SSM_WIDTH)), row],
                 [_sds((L, SSM_WIDTH), _BF16), _sds((L, 2 * SSM_WIDTH), _BF16), _sds((L, SSM_WIDTH), _BF16),
                  _sds((L, SSM_WIDTH), _F32),
                  _sds((1, SSM_WIDTH), _F32), _sds((1, 2 * SSM_WIDTH), _F32), _sds((1, SSM_WIDTH), _F32)],
                 "ssm_out_bwd")(dn, y, z, u, d_row, w_glu, g_ssm)


def _ssm_du(lam_re, lam_im, bt_re, bt_im, dud):
    L = dud.shape[0]
    tm = _tile(L)

    def body(lr_ref, li_ref, br_ref, bi_ref, dud_ref, du_ref):
        for k in range(SUPER):
            du_ref[:, _sb_width(k)] = (_dot(lr_ref[:, _sb_state(k)], br_ref[:, _sb_state(k)], _NT)
                                       + _dot(li_ref[:, _sb_state(k)], bi_ref[:, _sb_state(k)], _NT)
                                       + dud_ref[:, _sb_width(k)])

    return _call(body, (L // tm,),
                 [_rows(tm, N_STATE), _rows(tm, N_STATE), _whole((SB_WIDTH, N_STATE)),
                  _whole((SB_WIDTH, N_STATE)), _rows(tm, SSM_WIDTH)],
                 _rows(tm, SSM_WIDTH), _sds((L, SSM_WIDTH), _F32), "ssm_du")(lam_re, lam_im, bt_re, bt_im, dud)


def _ssm_weight_grads(dy, x_re, x_im, lam_re, lam_im, u):
    L = u.shape[0]

    def body(dy_ref, xr_ref, xi_ref, lr_ref, li_ref, u_ref, dcr_ref, dci_ref, dbr_ref, dbi_ref):
        dyb = dy_ref[...]
        ub = u_ref[...].astype(_BF16)
        dcr_ref[...] = _dot(dyb, xr_ref[...], _TN)
        dci_ref[...] = _dot(dyb, xi_ref[...], _TN)
        dbr_ref[...] = _dot(ub, lr_ref[...], _TN)
        dbi_ref[...] = _dot(ub, li_ref[...], _TN)

    width = pl.BlockSpec((L, SB_WIDTH), lambda k: (0, k))
    state = pl.BlockSpec((L, SB_STATE), lambda k: (0, k))
    out = pl.BlockSpec((SB_WIDTH, SB_STATE), lambda k: (0, k))
    return _call(body, (SUPER,), [width, state, state, state, state, width], [out] * 4,
                 [_sds((SB_WIDTH, N_STATE), _F32)] * 4,
                 "ssm_weight_grads")(dy, x_re, x_im, lam_re, lam_im, u)


_SSM_PACK = {"ssm_b_re": (0, SSM_WIDTH, 0, SSM_STATE), "ssm_c_re": (0, SSM_WIDTH, 64, SSM_STATE),
             "ssm_b_im": (512, SSM_WIDTH, 0, SSM_STATE), "ssm_c_im": (512, SSM_WIDTH, 64, SSM_STATE),
             "ssm_lambda_re": (1024, SSM_GROUPS, 0, SSM_STATE), "ssm_lambda_im": (1024, SSM_GROUPS, 64, SSM_STATE),
             "ssm_d": (1056, SSM_GROUPS, 0, SSM_GROUP), "ssm_log_dt": (1088, 1, 0, SSM_GROUPS)}
_PACK_TILE = 16
_SSM_PACK_ROWS = 1088 + _PACK_TILE


def _ssm_param_bwd(da_re, da_im, dbt_re, dbt_im, dct_re, dct_im, lam_re, lam_im, log_dt, b_re, b_im, g_d):
    def body(dar, dai, dbr, dbi, dcr, dci, lr_ref, li_ref, ld_ref, bre_ref, bim_ref, gd_ref, pack_ref):
        lane_in = _iota((SSM_STATE, _LANES), 0)
        lane_out = _iota((SSM_STATE, _LANES), 1)
        low = (lane_out == lane_in).astype(_F32)
        high = (lane_out == lane_in + SSM_STATE).astype(_F32)

        def side_by_side(a, b):
            return _dot_exact(a, low, _NN) + _dot_exact(b, high, _NN)

        tail = _SSM_PACK["ssm_d"][0]
        pack_ref[tail:, :] = jnp.zeros((_SSM_PACK_ROWS - tail, _LANES), _BF16)
        pack_ref[tail:tail + SSM_GROUPS, 0:SSM_GROUP] = gd_ref[...].astype(_BF16)
        own_c = (_iota((SB_WIDTH, SB_STATE), 0) >> 4) == (_iota((SB_WIDTH, SB_STATE), 1) >> 6)

        def unfold(ref):
            blocks = []
            for k in range(SUPER):
                t = jnp.where(own_c, ref[:, _sb_state(k)], 0.0)
                t = sum(t[:, 128 * i:128 * (i + 1)] for i in range(SB_STATE // 128))
                blocks.append((t + pltpu.roll(t, SSM_STATE, 1))[:, :SSM_STATE])
            return jnp.concatenate(blocks, axis=0)

        dbb_re, dbb_im = unfold(dbr), unfold(dbi)
        b_re, b_im = bre_ref[...], bim_ref[...]
        dt_col = _dt_column(ld_ref[...])
        (_, _, fr, fi), vjp = jax.vjp(_s5_discretize, lr_ref[...], li_ref[...], dt_col)
        spread = _rows_of_group()
        fr_t = _dot_exact(spread, fr, _NN)
        fi_t = _dot_exact(spread, fi, _NN)
        pack_ref[0:SSM_WIDTH, :] = side_by_side(fr_t * dbb_re + fi_t * dbb_im, unfold(dcr)).astype(_BF16)
        pack_ref[SSM_WIDTH:2 * SSM_WIDTH, :] = side_by_side(fr_t * dbb_im - fi_t * dbb_re, -unfold(dci)).astype(_BF16)
        d_fr = _dot_exact(spread, dbb_re * b_re + dbb_im * b_im, _TN)
        d_fi = _dot_exact(spread, dbb_im * b_re - dbb_re * b_im, _TN)
        e64, own = _group_masks()

        def from_row(ref):
            return _dot_exact(jnp.where(own, ref[...], 0.0), e64, _NT)

        d_lr, d_li, d_dt = vjp((from_row(dar), from_row(dai), d_fr, d_fi))
        lam_rows = _SSM_PACK["ssm_lambda_re"][0]
        pack_ref[lam_rows:lam_rows + SSM_GROUPS, :] = side_by_side(d_lr, d_li).astype(_BF16)
        eye = (_iota((SSM_GROUPS, SSM_GROUPS), 0) == _iota((SSM_GROUPS, SSM_GROUPS), 1)).astype(_F32)
        dt_row = _SSM_PACK["ssm_log_dt"][0]
        pack_ref[dt_row:dt_row + _PACK_TILE, 0:SSM_GROUPS] = _dot_exact(
            jnp.broadcast_to(d_dt, (SSM_GROUPS, 128)), eye, _TN)[0:_PACK_TILE].astype(_BF16)

    ins = [da_re, da_im, dbt_re, dbt_im, dct_re, dct_im, lam_re, lam_im, log_dt, b_re, b_im, g_d]
    out = (_SSM_PACK_ROWS, _LANES)
    return _call(body, (1,), [_whole(a.shape) for a in ins], _whole(out), _sds(out, _BF16), "ssm_param_bwd")(*ins)


def _head_spread(j):
    r = _iota((KV_WIDTH, 256), 0)
    c = _iota((KV_WIDTH, 256), 1)
    return (r == HEAD_DIM * j + (c & (HEAD_DIM - 1))).astype(_BF16)


STACK = Q_PER_KV * BLOCK


def _stack_heads(t):
    lane_head = _iota((1, 256), 1) >> 6
    return jnp.concatenate([jnp.where(lane_head == g, t, jnp.zeros_like(t)) for g in range(Q_PER_KV)], axis=0)


def _unstack_heads(t):
    lane_head = _iota((1, 256), 1) >> 6
    return sum(jnp.where(lane_head == g, t[BLOCK * g:BLOCK * (g + 1)], 0.0) for g in range(Q_PER_KV))


def _stacked_sinks(sink_ref, j):
    block = _iota((STACK, 1), 0) >> 7
    col = jnp.full((STACK, 1), sink_ref[Q_PER_KV * j], _F32)
    for g in range(1, Q_PER_KV):
        col = jnp.where(block == g, sink_ref[Q_PER_KV * j + g], col)
    return col


def _fold_heads(t, j):
    t = t[:, :KV_WIDTH] + t[:, KV_WIDTH:]
    t = t + pltpu.roll(t, HEAD_DIM, 1)
    return jnp.where((_iota((1, KV_WIDTH), 1) >> 6) == j, t, 0.0)


def _attn_scores(q_stacked, kt, blk, sink):
    s = _dot(q_stacked, kt, _NT) * (HEAD_DIM ** -0.5)
    qi = _iota((STACK, 2 * BLOCK), 0) & (BLOCK - 1)
    kj = _iota((STACK, 2 * BLOCK), 1)
    rel = qi + BLOCK - kj
    valid = (rel >= 0) & (rel < BLOCK) & (blk * BLOCK - BLOCK + kj >= 0)
    s = jnp.where(valid, s, MASK_VALUE)
    m = jnp.maximum(jnp.max(s, axis=-1, keepdims=True), sink)
    p = jnp.exp(s - m)
    e_sink = jnp.exp(sink - m)
    den = jnp.sum(p, axis=-1, keepdims=True) + e_sink
    return p / den, e_sink / den


def _attn_specs():
    prev = lambda i: (jnp.maximum(i - 1, 0), 0)
    cur = lambda i: (i, 0)
    kv = [pl.BlockSpec((BLOCK, KV_WIDTH), prev), pl.BlockSpec((BLOCK, KV_WIDTH), cur)]
    return [pl.BlockSpec((BLOCK, ATTN_WIDTH), cur)] + kv + kv


def _attn_fwd(q, k, v, sinks, g_attn):
    L = q.shape[0]

    def body(q_ref, kp_ref, kc_ref, vp_ref, vc_ref, sink_ref, g_ref, o_ref, n_ref):
        blk = pl.program_id(0)
        kwin = jnp.concatenate([kp_ref[...], kc_ref[...]], axis=0)
        vwin = jnp.concatenate([vp_ref[...], vc_ref[...]], axis=0)
        halves = []
        for j in range(N_KV_HEADS):
            spread = _head_spread(j)
            kt = _dot(kwin, spread, _NN).astype(_BF16)
            vt = _dot(vwin, spread, _NN).astype(_BF16)
            qs = _stack_heads(q_ref[:, 256 * j:256 * (j + 1)])
            p, _ = _attn_scores(qs, kt, blk, _stacked_sinks(sink_ref, j))
            halves.append(_unstack_heads(_dot(p, vt, _NN)))
        o = jnp.concatenate(halves, axis=1)
        o_ref[...] = o
        n, _ = _rms_fwd(o, g_ref[...])
        n_ref[...] = n.astype(_BF16)

    cur = lambda i: (i, 0)
    return _call(body, (L // BLOCK,),
                 _attn_specs() + [pl.BlockSpec(memory_space=pltpu.SMEM), _whole((1, ATTN_WIDTH))],
                 [pl.BlockSpec((BLOCK, ATTN_WIDTH), cur)] * 2,
                 [_sds((L, ATTN_WIDTH), _F32), _sds((L, ATTN_WIDTH), _BF16)],
                 "attn_fwd")(q, k, k, v, v, sinks, g_attn)


def _attn_bwd(q, k, v, o, dn, sinks, g_attn):
    L = q.shape[0]

    def body(q_ref, kp_ref, kc_ref, vp_ref, vc_ref, o_ref, dn_ref, sink_ref, g_ref,
             dq_ref, dk_ref, dv_ref, dsink_ref, dg_ref):
        blk = pl.program_id(0)
        first = blk == 0

        @pl.when(first)
        def _():
            dk_ref[...] = jnp.zeros_like(dk_ref)
            dv_ref[...] = jnp.zeros_like(dv_ref)
            dsink_ref[...] = jnp.zeros_like(dsink_ref)

        o = o_ref[...]
        g = g_ref[...]
        _, r = _rms_fwd(o, g)
        do, dg = _rms_bwd(dn_ref[...], o, g, r)
        _accumulate(dg_ref, dg, first)
        kwin = jnp.concatenate([kp_ref[...], kc_ref[...]], axis=0)
        vwin = jnp.concatenate([vp_ref[...], vc_ref[...]], axis=0)
        lane = _iota((1, 128), 1)
        dsink = jnp.zeros((1, 128), _F32)
        dkwin = jnp.zeros((2 * BLOCK, KV_WIDTH), _F32)
        dvwin = jnp.zeros((2 * BLOCK, KV_WIDTH), _F32)
        dq_halves = []
        for j in range(N_KV_HEADS):
            spread = _head_spread(j)
            kt = _dot(kwin, spread, _NN).astype(_BF16)
            vt = _dot(vwin, spread, _NN).astype(_BF16)
            qs = _stack_heads(q_ref[:, 256 * j:256 * (j + 1)])
            dos = _stack_heads(do[:, 256 * j:256 * (j + 1)]).astype(_BF16)
            p, p_sink = _attn_scores(qs, kt, blk, _stacked_sinks(sink_ref, j))
            dp = _dot(dos, vt, _NT)
            delta = jnp.sum(p * dp, axis=-1, keepdims=True)
            ds = (p * (dp - delta) * (HEAD_DIM ** -0.5)).astype(_BF16)
            sink_term = p_sink * delta
            for g in range(Q_PER_KV):
                head_sum = jnp.sum(sink_term[BLOCK * g:BLOCK * (g + 1)], axis=0, keepdims=True)
                dsink = dsink - jnp.where(lane == Q_PER_KV * j + g, head_sum, 0.0)
            dvwin = dvwin + _fold_heads(_dot(p, dos, _TN), j)
            dkwin = dkwin + _fold_heads(_dot(ds, qs, _TN), j)
            dq_halves.append(_unstack_heads(_dot(ds, kt, _NN)))
        dq_ref[...] = jnp.concatenate(dq_halves, axis=1)
        dsink_ref[...] += dsink
        prev = pl.ds(pl.multiple_of(jnp.maximum(blk - 1, 0) * BLOCK, BLOCK), BLOCK)
        cur = pl.ds(pl.multiple_of(blk * BLOCK, BLOCK), BLOCK)
        dk_ref[prev, :] += dkwin[:BLOCK]
        dk_ref[cur, :] += dkwin[BLOCK:]
        dv_ref[prev, :] += dvwin[:BLOCK]
        dv_ref[cur, :] += dvwin[BLOCK:]

    cur = lambda i: (i, 0)
    blk_q = pl.BlockSpec((BLOCK, ATTN_WIDTH), cur)
    return _call(body, (L // BLOCK,),
                 _attn_specs() + [blk_q, blk_q, pl.BlockSpec(memory_space=pltpu.SMEM), _whole((1, ATTN_WIDTH))],
                 [blk_q, _whole((L, KV_WIDTH)), _whole((L, KV_WIDTH)), _whole((1, 128)), _whole((1, ATTN_WIDTH))],
                 [_sds((L, ATTN_WIDTH), _F32), _sds((L, KV_WIDTH), _F32), _sds((L, KV_WIDTH), _F32),
                  _sds((1, 128), _F32), _sds((1, ATTN_WIDTH), _F32)],
                 "attn_bwd")(q, k, k, v, v, o, dn, sinks, g_attn)


def _out_proj(n_ssm, n_attn, x, w_out, g_post_mix, g_pre_ffn):
    L = x.shape[0]
    tm = _chunk_tile(L)

    def body(ns_ref, na_ref, x_ref, w_ref, g1_ref, g2_ref, merged_ref, mo_ref, h1_ref, hn2_ref):
        merged = jnp.concatenate([ns_ref[...], na_ref[...]], axis=1)
        merged_ref[...] = merged
        mo = _dot(merged, w_ref[...], _NN)
        mo_ref[...] = mo
        n, _ = _rms_fwd(mo, g1_ref[...])
        h1 = x_ref[...] + n
        h1_ref[...] = h1
        hn2, _ = _rms_fwd(h1, g2_ref[...])
        hn2_ref[...] = hn2.astype(_BF16)

    row = _whole((1, D_MODEL))
    return _call(body, (L // tm,),
                 [_chunk_block(L, SSM_WIDTH), _rows(tm, ATTN_WIDTH), _rows(tm, D_MODEL), _whole((D_MODEL, D_MODEL)),
                  row, row],
                 [_rows(tm, D_MODEL)] * 4,
                 [_sds((L, D_MODEL), _BF16), _sds((L, D_MODEL), _F32), _sds((L, D_MODEL), _F32), _sds((L, D_MODEL), _BF16)],
                 "out_proj")(n_ssm, n_attn, x, w_out, g_post_mix, g_pre_ffn)


def _ffn(hn2, h1, target, w_gate_up, w_down, g_pre_ffn, g_post_ffn):
    L = h1.shape[0]
    tm = _tile(L)
    half = D_FF // 2

    def body(hn2_ref, h1_ref, tgt_ref, wgu_hbm, wd_hbm, g2_ref, g3_ref,
             act_ref, dgu_ref, dff_ref, dh1_ref, loss_ref, dg3_ref, dg2_ref,
             wgu, wd, gu, sem):
        first = pl.program_id(0) == 0

        @pl.when(first)
        def _():
            c1 = pltpu.make_async_copy(wgu_hbm, wgu, sem.at[0])
            c2 = pltpu.make_async_copy(wd_hbm, wd, sem.at[1])
            c1.start()
            c2.start()
            c1.wait()
            c2.wait()

        hn2 = hn2_ref[...]
        ff = jnp.zeros((tm, D_MODEL), _F32)
        for c in range(2):
            gate = _dot(hn2, wgu[half * c:half * (c + 1), :], _NT)
            up = _dot(hn2, wgu[D_FF + half * c:D_FF + half * (c + 1), :], _NT)
            gu[:, half * c:half * (c + 1)] = gate
            gu[:, D_FF + half * c:D_FF + half * (c + 1)] = up
            act = gate * jax.nn.sigmoid(gate) * up
            act_ref[half * c:half * (c + 1), :] = act.T.astype(_BF16)
            ff = ff + _dot(act, wd[half * c:half * (c + 1), :], _NN)
        g3 = g3_ref[...]
        n, r = _rms_fwd(ff, g3)
        h1 = h1_ref[...]
        err = h1 + n - tgt_ref[...]
        loss = 0.5 * jnp.sum(jnp.mean(err * err, axis=-1, keepdims=True), axis=0, keepdims=True)
        _accumulate(loss_ref, jnp.broadcast_to(loss, (1, 128)), first)
        dh2 = err * (1.0 / D_MODEL)
        dff, dg3 = _rms_bwd(dh2, ff, g3, r)
        _accumulate(dg3_ref, dg3, first)
        dffb = dff.astype(_BF16)
        dff_ref[...] = dffb
        dhn2 = jnp.zeros((tm, D_MODEL), _F32)
        for c in range(2):
            dact = _dot(dffb, wd[half * c:half * (c + 1), :], _NT)
            gate = gu[:, half * c:half * (c + 1)]
            up = gu[:, D_FF + half * c:D_FF + half * (c + 1)]
            sig = jax.nn.sigmoid(gate)
            silu = gate * sig
            dgate = dact * up * (sig + silu * (1.0 - sig))
            dup = dact * silu
            dgu_ref[half * c:half * (c + 1), :] = dgate.T.astype(_BF16)
            dgu_ref[D_FF + half * c:D_FF + half * (c + 1), :] = dup.T.astype(_BF16)
            dhn2 = dhn2 + _dot(dgate, wgu[half * c:half * (c + 1), :], _NN)
            dhn2 = dhn2 + _dot(dup, wgu[D_FF + half * c:D_FF + half * (c + 1), :], _NN)
        g2 = g2_ref[...]
        _, r2 = _rms_fwd(h1, g2)
        dh1, dg2 = _rms_bwd(dhn2, h1, g2, r2)
        _accumulate(dg2_ref, dg2, first)
        dh1_ref[...] = dh2 + dh1

    row = _whole((1, D_MODEL))
    anyspace = pl.BlockSpec(memory_space=pl.ANY)
    return _call(body, (L // tm,),
                 [_rows(tm, D_MODEL), _rows(tm, D_MODEL), _rows(tm, D_MODEL), anyspace, anyspace, row, row],
                 [pl.BlockSpec((D_FF, tm), lambda i: (0, i)), pl.BlockSpec((2 * D_FF, tm), lambda i: (0, i)),
                  _rows(tm, D_MODEL), _rows(tm, D_MODEL), _whole((1, 128)), row, row],
                 [_sds((D_FF, L), _BF16), _sds((2 * D_FF, L), _BF16), _sds((L, D_MODEL), _BF16),
                  _sds((L, D_MODEL), _F32), _sds((1, 128), _F32), _sds((1, D_MODEL), _F32), _sds((1, D_MODEL), _F32)],
                 "ffn",
                 scratch=[pltpu.VMEM((2 * D_FF, D_MODEL), _BF16), pltpu.VMEM((D_FF, D_MODEL), _BF16),
                          pltpu.VMEM((tm, 2 * D_FF), _F32), pltpu.SemaphoreType.DMA((2,))],
                 )(hn2, h1, target, w_gate_up, w_down, g_pre_ffn, g_post_ffn)


def _out_proj_bwd(dh1, mo, w_out, g_post_mix, tokens=()):
    L = dh1.shape[0]
    tm = _chunk_tile(L)

    def body(dh1_ref, mo_ref, w_ref, g_ref, dmo_ref, dns_ref, dna_ref, dg_ref):
        first = pl.program_id(0) == 0
        mo = mo_ref[...]
        g = g_ref[...]
        _, r = _rms_fwd(mo, g)
        dmo, dg = _rms_bwd(dh1_ref[...], mo, g, r)
        _accumulate(dg_ref, dg, first)
        dmob = dmo.astype(_BF16)
        dmo_ref[...] = dmob
        dmerged = _dot(dmob, w_ref[...], _NT)
        dns_ref[...] = dmerged[:, :SSM_WIDTH]
        dna_ref[...] = dmerged[:, SSM_WIDTH:]

    row = _whole((1, D_MODEL))
    return _call(body, (L // tm,),
                 [_rows(tm, D_MODEL), _rows(tm, D_MODEL), _whole((D_MODEL, D_MODEL)), row],
                 [_rows(tm, D_MODEL), _chunk_block(L, SSM_WIDTH), _rows(tm, ATTN_WIDTH), row],
                 [_sds((L, D_MODEL), _BF16), _sds(_chunk_shape(L, SSM_WIDTH), _F32), _sds((L, ATTN_WIDTH), _F32),
                  _sds((1, D_MODEL), _F32)],
                 "out_proj_bwd", tokens=tokens)(dh1, mo, w_out, g_post_mix)


def _in_proj_bwd(du, dq, dk, dv, cos_t, sin_t, x, dh1, g_pre_mix, w_in, tokens=()):
    L = x.shape[0]
    tm = _chunk_tile(L)

    def body(du_ref, dq_ref, dk_ref, dv_ref, cos_ref, sin_ref, x_ref, dh1_ref, g_ref, w_ref,
             dproj_ref, dx_ref, dg_ref):
        first = pl.program_id(0) == 0
        cos_v, sin_v = cos_ref[...], sin_ref[...]
        dproj = jnp.concatenate([du_ref[...], _rope_transpose(dq_ref[...], cos_v, sin_v),
                                 _rope_transpose(dk_ref[...], cos_v, sin_v), dv_ref[...]], axis=1).astype(_BF16)
        dproj_ref[...] = dproj
        dhn = _dot(dproj, w_ref[...], _NN)
        x = x_ref[...]
        g = g_ref[...]
        _, r = _rms_fwd(x, g)
        dx, dg = _rms_bwd(dhn, x, g, r)
        _accumulate(dg_ref, dg, first)
        dx_ref[...] = dh1_ref[...] + dx

    row = _whole((1, D_MODEL))
    return _call(body, (L // tm,),
                 [_chunk_block(L, SSM_WIDTH), _rows(tm, ATTN_WIDTH), _rows(tm, KV_WIDTH), _rows(tm, KV_WIDTH),
                  _rows(tm, KV_WIDTH), _rows(tm, KV_WIDTH), _rows(tm, D_MODEL), _rows(tm, D_MODEL), row,
                  _whole((IN_WIDTH, D_MODEL))],
                 [_rows(tm, IN_WIDTH), _rows(tm, D_MODEL), row],
                 [_sds((L, IN_WIDTH), _BF16), _sds((L, D_MODEL), _F32), _sds((1, D_MODEL), _F32)],
                 "in_proj_bwd", tokens=tokens)(du, dq, dk, dv, cos_t, sin_t, x, dh1, g_pre_mix, w_in)


def _matmul_nn(a, b, out_dtype, name):
    M, K = a.shape
    N = b.shape[1]
    tm = next(t for t in (512, 256, 128) if M % t == 0)
    tn = N if N <= D_MODEL else next(t for t in (512, 256, 128) if N % t == 0)

    def body(a_ref, b_ref, o_ref):
        o_ref[...] = _dot(a_ref[...], b_ref[...], _NN).astype(out_dtype)

    params = pltpu.CompilerParams(dimension_semantics=("arbitrary", "arbitrary"), vmem_limit_bytes=VMEM_LIMIT)
    return pl.pallas_call(body, grid=(M // tm, N // tn),
                          in_specs=[pl.BlockSpec((tm, K), lambda i, j: (i, 0)),
                                    pl.BlockSpec((K, tn), lambda i, j: (0, j))],
                          out_specs=pl.BlockSpec((tm, tn), lambda i, j: (i, j)),
                          out_shape=_sds((M, N), out_dtype), compiler_params=params, name=name)(a, b)


def _matmul_tn(a, b, out_dtype, name, scale=1.0):
    K, M = a.shape
    N = b.shape[1]
    tm = next(t for t in (512, 256, 128) if M % t == 0)
    tn = N if N <= D_MODEL else next(t for t in (512, 256, 128) if N % t == 0)

    def body(a_ref, b_ref, o_ref):
        acc = _dot(a_ref[...], b_ref[...], _TN)
        o_ref[...] = (acc if scale == 1.0 else acc * scale).astype(out_dtype)

    params = pltpu.CompilerParams(dimension_semantics=("arbitrary", "arbitrary"), vmem_limit_bytes=VMEM_LIMIT)
    return pl.pallas_call(body, grid=(M // tm, N // tn),
                          in_specs=[pl.BlockSpec((K, tm), lambda i, j: (0, i)),
                                    pl.BlockSpec((K, tn), lambda i, j: (0, j))],
                          out_specs=pl.BlockSpec((tm, tn), lambda i, j: (i, j)),
                          out_shape=_sds((M, N), out_dtype), compiler_params=params, name=name)(a, b)


def _local_step(x, pos, target, p, fetch, publish, progress):
    L = x.shape[0]
    T = L // SCAN_CHUNKS
    cos_t, sin_t = _rope_tables(pos.reshape(L, 1))
    w_in, = fetch(("w_in",), None)
    hn, u, q, k, v = _in_proj(x, p["g_pre_mix"], w_in, cos_t, sin_t)

    ssm = {n: _to_2d(n, p[n]) for n in ("ssm_lambda_re", "ssm_lambda_im", "ssm_log_dt", "ssm_b_re", "ssm_b_im",
                                        "ssm_c_re", "ssm_c_im")}
    d_row = p["ssm_d"].reshape(1, SSM_WIDTH)
    a_re, a_im, bt_re, bt_im, ct_re, ct_im = _ssm_prep(
        ssm["ssm_lambda_re"], ssm["ssm_lambda_im"], ssm["ssm_log_dt"], ssm["ssm_b_re"], ssm["ssm_b_im"],
        ssm["ssm_c_re"], ssm["ssm_c_im"])

    u_c = u.reshape(L, SSM_WIDTH)
    cx = _ssm_core_fwd(u_c, bt_re, bt_im, ct_re, ct_im, a_re, a_im)
    w_glu, = fetch(("w_glu",), cx)
    y, z, n_ssm_c = _ssm_out(cx, u_c, d_row, w_glu, p["b_glu"], p["g_ssm_out"])
    n_ssm = n_ssm_c.reshape(_chunk_shape(L, SSM_WIDTH))

    sinks = p["attn_sinks"].reshape(N_Q_HEADS)
    o, n_attn = _attn_fwd(q, k, v, sinks, p["g_attn_out"])
    w_out, = fetch(("w_out",), n_attn)
    merged, mo, h1, hn2 = _out_proj(n_ssm, n_attn, x, w_out, p["g_post_mix"], p["g_pre_ffn"])
    w_gate_up, w_down = fetch(("w_gate_up", "w_down"), hn2)
    act_t, dgu_t, dff, dh1, loss, dg_post_ffn, dg_pre_ffn = _ffn(
        hn2, h1, target, w_gate_up, w_down, p["g_pre_ffn"], p["g_post_ffn"])
    grads = {"g_post_ffn": dg_post_ffn, "g_pre_ffn": dg_pre_ffn}
    tokens = publish({"w_down": _matmul_nn(act_t, dff, _BF16, "grad_w_down"),
                      "w_gate_up": _matmul_nn(dgu_t, hn2, _BF16, "grad_w_gate_up")})

    dmo, dn_ssm, dn_attn, grads["g_post_mix"] = _out_proj_bwd(dh1, mo, w_out, p["g_post_mix"], tokens)
    grad_w_out = _matmul_tn(merged, dmo, _BF16, "grad_w_out")

    dq, dk, dv, dsink, grads["g_attn_out"] = _attn_bwd(q, k, v, o, dn_attn, sinks, p["g_attn_out"])
    grads["attn_sinks"] = dsink

    gy, dz, dy, dud, grads["g_ssm_out"], grads["b_glu"], dd = _ssm_out_bwd(
        dn_ssm.reshape(L, SSM_WIDTH), y, z, u_c, d_row, w_glu, p["g_ssm_out"])
    tokens = progress(dy)
    tokens += publish({"w_out": grad_w_out, "w_glu": _matmul_tn(dz, gy, _BF16, "grad_w_glu")})
    du_c, dct_re, dct_im, dbt_re, dbt_im, da_re, da_im = _ssm_core_bwd(
        u_c, dy, dud, bt_re, bt_im, ct_re, ct_im, a_re, a_im, tokens)
    ssm_pack = _ssm_param_bwd(
        da_re, da_im, dbt_re, dbt_im, dct_re, dct_im,
        ssm["ssm_lambda_re"], ssm["ssm_lambda_im"], ssm["ssm_log_dt"], ssm["ssm_b_re"], ssm["ssm_b_im"],
        dd.reshape(SSM_GROUPS, SSM_GROUP))
    grads.update(ssm_pack=ssm_pack, loss=loss)
    publish(grads)

    du = du_c.reshape(_chunk_shape(L, SSM_WIDTH))
    dproj, grad_x, g_pre_mix = _in_proj_bwd(du, dq, dk, dv, cos_t, sin_t, x, dh1, p["g_pre_mix"], w_in, [ssm_pack])
    publish({"g_pre_mix": g_pre_mix, "w_in": _matmul_tn(dproj, hn, _BF16, "grad_w_in")})
    return grad_x


_MESH = pl.DeviceIdType.MESH
_PEERS = N_DEV - 1


def _mesh_pos():
    return lax.axis_index("x"), lax.axis_index("y"), lax.axis_index("c")


def _dev_index(px, py, pc):
    return 4 * px + 2 * py + pc


def _all_gather(shards, out_dtype, name):
    n = len(shards)

    def body(*refs):
        ins, outs, stages = refs[:n], refs[n:2 * n], refs[2 * n:3 * n]
        send_sems, recv_sems, local_sems = refs[3 * n:]
        x, y, c = _mesh_pos()
        me, sibling = (x, y, c), (x, y, 1 - c)
        chips = [(1 - x, y), (x, 1 - y), (1 - x, 1 - y)]

        def copy(w, k, block, to, src=None):
            slot = outs[w].at[_dev_index(*block)]
            return pltpu.make_async_remote_copy(
                src_ref=slot if src is None else src, dst_ref=slot,
                send_sem=send_sems.at[_PEERS * w + k], recv_sem=recv_sems.at[_PEERS * w + k],
                device_id=to, device_id_type=_MESH)

        for w in range(n):
            stages[w][...] = ins[w][...].astype(out_dtype)
        mine, first, passed = [], [], []
        for w in range(n):
            cp = pltpu.make_async_copy(stages[w], outs[w].at[_dev_index(*me)], local_sems.at[w])
            cp.start()
            mine.append(cp)
            sends = [copy(w, 0, me, sibling, src=stages[w])]
            sends += [copy(w, 1 + j, me, (*chip, c), src=stages[w]) for j, chip in enumerate(chips)]
            for cp in sends:
                cp.start()
            first += sends
        for w in range(n):
            for j, chip in enumerate(chips):
                copy(w, 1 + j, (*chip, c), me).wait_recv()
                cp = copy(w, 4 + j, (*chip, c), sibling)
                cp.start()
                passed.append(cp)
        for w in range(n):
            copy(w, 0, sibling, me).wait_recv()
            for j, chip in enumerate(chips):
                copy(w, 4 + j, (*chip, 1 - c), me).wait_recv()
        for cp in first + passed:
            cp.wait_send()
        for cp in mine:
            cp.wait()

    return pl.pallas_call(
        body, name=name,
        out_shape=[_sds((N_DEV,) + s.shape, out_dtype) for s in shards],
        in_specs=[pl.BlockSpec(memory_space=pltpu.VMEM)] * n,
        out_specs=[pl.BlockSpec(memory_space=pl.ANY)] * n,
        scratch_shapes=[pltpu.VMEM(s.shape, out_dtype) for s in shards]
        + [pltpu.SemaphoreType.DMA((_PEERS * n,)), pltpu.SemaphoreType.DMA((_PEERS * n,)),
           pltpu.SemaphoreType.DMA((n,))],
        compiler_params=pltpu.CompilerParams(vmem_limit_bytes=VMEM_LIMIT),
    )(*shards)


_HBM_SPEC = pl.BlockSpec(memory_space=pltpu.HBM)
_SEM_SPEC = pl.BlockSpec(memory_space=pltpu.SEMAPHORE)
_DATAFLOW = pltpu.SideEffectType.DATAFLOW_SIDE_EFFECTING


def _peer(x, y, c, r):
    return (x ^ ((r >> 2) & 1), y ^ ((r >> 1) & 1), c ^ (r & 1))


def _hbm(a):
    return pltpu.with_memory_space_constraint(a, pltpu.HBM)


def _send_start(sources, blocked, name):
    n = len(sources)
    lands = [lax.empty((N_DEV,) + (s.shape[1:] if blocked else s.shape), s.dtype) for s in sources]

    def body(*refs):
        srcs, zones = refs[:n], refs[n:2 * n]
        send_sems, recv_sems = refs[2 * n:3 * n], refs[3 * n:4 * n]
        token, local_sems = refs[6 * n], refs[6 * n + 1]
        x, y, c = _mesh_pos()
        me = _dev_index(x, y, c)
        local = []
        for w in range(n):
            cp = pltpu.make_async_copy(srcs[w].at[me] if blocked else srcs[w], zones[w].at[me], local_sems.at[w])
            cp.start()
            local.append(cp)
            for r in range(1, N_DEV):
                peer = _peer(x, y, c, r)
                pltpu.make_async_remote_copy(
                    src_ref=srcs[w].at[_dev_index(*peer)] if blocked else srcs[w], dst_ref=zones[w].at[me],
                    send_sem=send_sems[w].at[r - 1], recv_sem=recv_sems[w].at[r - 1],
                    device_id=peer, device_id_type=_MESH).start()
        for cp in local:
            cp.wait()
        token[...] = jnp.zeros_like(token)

    sems = [pltpu.SemaphoreType.DMA((_PEERS,))] * (2 * n)
    out = pl.pallas_call(
        body, name=name,
        out_shape=sems + [pltpu.HBM(a.shape, a.dtype) for a in list(sources) + lands] + [_sds((8, 128), _F32)],
        in_specs=[_HBM_SPEC] * (2 * n),
        out_specs=[_SEM_SPEC] * (2 * n) + [_HBM_SPEC] * (2 * n) + [pl.BlockSpec(memory_space=pltpu.VMEM)],
        input_output_aliases={i: 2 * n + i for i in range(2 * n)},
        scratch_shapes=[pltpu.SemaphoreType.DMA((n,))],
        compiler_params=pltpu.CompilerParams(has_side_effects=_DATAFLOW),
    )(*[_hbm(a) for a in sources], *[_hbm(a) for a in lands])
    return out[:n], out[n:2 * n], out[2 * n:3 * n], out[3 * n:4 * n], out[4 * n]


def _send_wait(send_sems, recv_sems, sources, lands, after, blocked, name):
    n = len(sources)

    def body(*refs):
        srcs, zones = refs[:n], refs[n:2 * n]
        sends, recvs = refs[2 * n:3 * n], refs[3 * n:4 * n]
        x, y, c = _mesh_pos()
        for w in range(n):
            for r in range(1, N_DEV):
                peer = _peer(x, y, c, r)
                idx = _dev_index(*peer)
                cp = pltpu.make_async_remote_copy(
                    src_ref=srcs[w].at[idx] if blocked else srcs[w], dst_ref=zones[w].at[idx],
                    send_sem=sends[w].at[r - 1], recv_sem=recvs[w].at[r - 1],
                    device_id=peer, device_id_type=_MESH)
                cp.wait_send()
                cp.wait_recv()

    out = pl.pallas_call(
        body, name=name,
        out_shape=[pltpu.HBM(a.shape, a.dtype) for a in list(sources) + list(lands)],
        in_specs=[_HBM_SPEC] * (2 * n) + [_SEM_SPEC] * (2 * n) + [pl.BlockSpec(memory_space=pl.ANY)],
        out_specs=[_HBM_SPEC] * (2 * n),
        input_output_aliases={i: i for i in range(2 * n)},
        compiler_params=pltpu.CompilerParams(has_side_effects=_DATAFLOW),
    )(*sources, *lands, *send_sems, *recv_sems, after)
    return out[n:]


def _sequencer_exchange(sources, blocked, name, collective_id):
    n = len(sources)
    flags = blocked

    def body(*refs):
        srcs, zones = refs[:n], refs[n:2 * n]
        send_sems, recv_sems, local_sems = refs[2 * n:]
        x, y, c = _mesh_pos()
        me = _dev_index(x, y, c)
        barrier = pltpu.get_barrier_semaphore()
        for r in range(1, N_DEV):
            pl.semaphore_signal(barrier, inc=1, device_id=_peer(x, y, c, r), device_id_type=_MESH)
        pl.semaphore_wait(barrier, _PEERS)
        local, sends, recvs = [], [], []
        for w in range(n):
            cp = pltpu.make_async_copy(srcs[w].at[me] if flags[w] else srcs[w], zones[w].at[me], local_sems.at[w])
            cp.start()
            local.append(cp)
            for r in range(1, N_DEV):
                peer = _peer(x, y, c, r)
                idx = _dev_index(*peer)
                k = _PEERS * w + r - 1
                src = srcs[w].at[idx] if flags[w] else srcs[w]
                send = pltpu.make_async_remote_copy(
                    src_ref=src, dst_ref=zones[w].at[me], send_sem=send_sems.at[k], recv_sem=recv_sems.at[k],
                    device_id=peer, device_id_type=_MESH)
                send.start()
                sends.append(send)
                recvs.append(pltpu.make_async_remote_copy(
                    src_ref=src, dst_ref=zones[w].at[idx], send_sem=send_sems.at[k], recv_sem=recv_sems.at[k],
                    device_id=peer, device_id_type=_MESH))
        for cp in recvs:
            cp.wait_recv()
        for cp in sends:
            cp.wait_send()
        for cp in local:
            cp.wait()

    return pl.kernel(
        body, name=name,
        out_type=[_sds((N_DEV,) + (s.shape[1:] if f else s.shape), s.dtype) for s, f in zip(sources, flags)],
        mesh=plsc.ScalarSubcoreMesh(axis_name="sequencer", num_cores=1),
        scratch_types=[pltpu.SemaphoreType.DMA((_PEERS * n,)), pltpu.SemaphoreType.DMA((_PEERS * n,)),
                       pltpu.SemaphoreType.DMA((n,))],
        compiler_params=pltpu.CompilerParams(collective_id=collective_id),
    )(*sources)


def _sequencer_gather(shards, name, collective_id):
    n = len(shards)
    fan = 4

    def body(*refs):
        srcs, zones = refs[:n], refs[n:2 * n]
        send_sems, recv_sems, local_sems = refs[2 * n:]
        x, y, c = _mesh_pos()
        me, sibling = (x, y, c), (x, y, 1 - c)
        chips = [(1 - x, y), (x, 1 - y), (1 - x, 1 - y)]
        barrier = pltpu.get_barrier_semaphore()
        for peer in [sibling] + [(*chip, c) for chip in chips]:
            pl.semaphore_signal(barrier, inc=1, device_id=peer, device_id_type=_MESH)
        pl.semaphore_wait(barrier, fan)

        def copy(w, k, block, to, src=None):
            slot = zones[w].at[_dev_index(*block)]
            return pltpu.make_async_remote_copy(
                src_ref=slot if src is None else src, dst_ref=slot,
                send_sem=send_sems.at[_PEERS * w + k], recv_sem=recv_sems.at[_PEERS * w + k],
                device_id=to, device_id_type=_MESH)

        mine, first, passed = [], [], []
        for w in range(n):
            cp = pltpu.make_async_copy(srcs[w], zones[w].at[_dev_index(*me)], local_sems.at[w])
            cp.start()
            mine.append(cp)
            sends = [copy(w, 0, me, sibling, src=srcs[w])]
            sends += [copy(w, 1 + j, me, (*chip, c), src=srcs[w]) for j, chip in enumerate(chips)]
            for cp in sends:
                cp.start()
            first += sends
        for w in range(n):
            for j, chip in enumerate(chips):
                copy(w, 1 + j, (*chip, c), me).wait_recv()
                cp = copy(w, fan + j, (*chip, c), sibling)
                cp.start()
                passed.append(cp)
        for w in range(n):
            copy(w, 0, sibling, me).wait_recv()
            for j, chip in enumerate(chips):
                copy(w, fan + j, (*chip, 1 - c), me).wait_recv()
        for cp in first + passed:
            cp.wait_send()
        for cp in mine:
            cp.wait()

    return pl.kernel(
        body, name=name, out_type=[_sds((N_DEV,) + s.shape, s.dtype) for s in shards],
        mesh=plsc.ScalarSubcoreMesh(axis_name="sequencer", num_cores=1),
        scratch_types=[pltpu.SemaphoreType.DMA((_PEERS * n,)), pltpu.SemaphoreType.DMA((_PEERS * n,)),
                       pltpu.SemaphoreType.DMA((n,))],
        compiler_params=pltpu.CompilerParams(collective_id=collective_id),
    )(*shards)


N_CHIPS = N_DEV // 2


def _sequencer_pair_exchange(sources, name, collective_id):
    n = len(sources)

    def body(*refs):
        srcs, zones = refs[:n], refs[n:2 * n]
        send_sems, recv_sems = refs[2 * n:]
        x, y, c = _mesh_pos()
        sibling = (x, y, 1 - c)
        barrier = pltpu.get_barrier_semaphore()
        pl.semaphore_signal(barrier, inc=1, device_id=sibling, device_id_type=_MESH)
        pl.semaphore_wait(barrier, 1)
        copies = []
        for w in range(n):
            for j in range(N_CHIPS):
                k = N_CHIPS * w + j
                cp = pltpu.make_async_remote_copy(
                    src_ref=srcs[w].at[2 * j + 1 - c], dst_ref=zones[w].at[j],
                    send_sem=send_sems.at[k], recv_sem=recv_sems.at[k], device_id=sibling, device_id_type=_MESH)
                cp.start()
                copies.append(cp)
        for cp in copies:
            cp.wait_recv()
        for cp in copies:
            cp.wait_send()

    return pl.kernel(
        body, name=name, out_type=[_sds((N_CHIPS,) + s.shape[1:], s.dtype) for s in sources],
        mesh=plsc.ScalarSubcoreMesh(axis_name="sequencer", num_cores=1),
        scratch_types=[pltpu.SemaphoreType.DMA((N_CHIPS * n,)), pltpu.SemaphoreType.DMA((N_CHIPS * n,))],
        compiler_params=pltpu.CompilerParams(collective_id=collective_id),
    )(*sources)


def _pair_sum(source, received, core, name, tokens=()):
    _, rows, cols = source.shape
    tr = _row_tile(rows)

    def body(core_ref, s_ref, r_ref, o_ref):
        c = core_ref[0]
        for j in range(N_CHIPS):
            o_ref[j] = (s_ref[2 * j + c].astype(_F32) + r_ref[j].astype(_F32)).astype(o_ref.dtype)

    return _call(body, (rows // tr,),
                 [pl.BlockSpec(memory_space=pltpu.SMEM), pl.BlockSpec((N_DEV, tr, cols), lambda i: (0, i, 0)),
                  pl.BlockSpec((N_CHIPS, tr, cols), lambda i: (0, i, 0))],
                 pl.BlockSpec((N_CHIPS, tr, cols), lambda i: (0, i, 0)),
                 _sds((N_CHIPS, rows, cols), source.dtype), name, tokens=tokens)(core, source, received)


def _sequencer_chip_exchange(partials, name, collective_id):
    n = len(partials)
    others = N_CHIPS - 1

    def body(*refs):
        srcs, zones = refs[:n], refs[n:2 * n]
        send_sems, recv_sems, local_sems = refs[2 * n:]
        x, y, c = _mesh_pos()
        mine = 2 * x + y
        peers = [(x ^ (r >> 1), y ^ (r & 1), c) for r in range(1, N_CHIPS)]
        barrier = pltpu.get_barrier_semaphore()
        for peer in peers:
            pl.semaphore_signal(barrier, inc=1, device_id=peer, device_id_type=_MESH)
        pl.semaphore_wait(barrier, others)
        local, sends, recvs = [], [], []
        for w in range(n):
            cp = pltpu.make_async_copy(srcs[w].at[mine], zones[w].at[mine], local_sems.at[w])
            cp.start()
            local.append(cp)
            for r, peer in enumerate(peers):
                theirs = 2 * peer[0] + peer[1]
                k = others * w + r
                send = pltpu.make_async_remote_copy(
                    src_ref=srcs[w].at[theirs], dst_ref=zones[w].at[mine],
                    send_sem=send_sems.at[k], recv_sem=recv_sems.at[k], device_id=peer, device_id_type=_MESH)
                send.start()
                sends.append(send)
                recvs.append(pltpu.make_async_remote_copy(
                    src_ref=srcs[w].at[theirs], dst_ref=zones[w].at[theirs],
                    send_sem=send_sems.at[k], recv_sem=recv_sems.at[k], device_id=peer, device_id_type=_MESH))
        for cp in recvs:
            cp.wait_recv()
        for cp in sends:
            cp.wait_send()
        for cp in local:
            cp.wait()

    return pl.kernel(
        body, name=name, out_type=[_sds(s.shape, s.dtype) for s in partials],
        mesh=plsc.ScalarSubcoreMesh(axis_name="sequencer", num_cores=1),
        scratch_types=[pltpu.SemaphoreType.DMA((others * n,)), pltpu.SemaphoreType.DMA((others * n,)),
                       pltpu.SemaphoreType.DMA((n,))],
        compiler_params=pltpu.CompilerParams(collective_id=collective_id),
    )(*partials)


def _row_tile(rows):
    return next(t for t in range(min(rows, 256), 0, -16) if rows % t == 0)


def _sum_parts(parts, name, tokens=()):
    _, rows, cols = parts.shape
    tr = _row_tile(rows)

    def body(p_ref, g_ref):
        g = p_ref[0].astype(_F32)
        for s in range(1, N_DEV):
            g = g + p_ref[s].astype(_F32)
        g_ref[...] = g

    return _call(body, (rows // tr,), [pl.BlockSpec((N_DEV, tr, cols), lambda i: (0, i, 0))],
                 _rows(tr, cols), _sds((rows, cols), _F32), name, tokens=tokens)(parts)


def _adam_update(g, w, m, v):
    new_m = ADAM_B1 * m + (1.0 - ADAM_B1) * g
    new_v = ADAM_B2 * v + (1.0 - ADAM_B2) * (g * g)
    m_hat = new_m / (1.0 - ADAM_B1 ** ADAM_STEP)
    v_hat = new_v / (1.0 - ADAM_B2 ** ADAM_STEP)
    return -ADAM_LR * (m_hat / (jnp.sqrt(v_hat) + ADAM_EPS) + ADAM_WD * w), new_m, new_v


def _adamw_small(parts, items, sums, name, tokens=()):
    n_p, n_i = len(parts), len(items)

    def body(*refs):
        p_refs, state, outs = refs[:n_p], refs[n_p:n_p + 3 * n_i], refs[n_p + 3 * n_i:]

        def total(part, rows, cols):
            shift = cols.start % _LANES
            window = slice(cols.start - shift, cols.start - shift + _LANES) if shift else cols
            n_rows = rows.stop - rows.start
            narrow = p_refs[part].dtype.itemsize < 4 and n_rows % _PACK_TILE
            tile = slice(rows.start, rows.start + _PACK_TILE) if narrow else rows
            g = p_refs[part][0, tile, window].astype(_F32)
            for s in range(1, N_DEV):
                g = g + p_refs[part][s, tile, window].astype(_F32)
            g = g[:n_rows] if narrow else g
            return pltpu.roll(g, _LANES - shift, 1)[:, :cols.stop - cols.start] if shift else g

        for i, (part, rows, cols, _, _, _) in enumerate(items):
            g = total(part, rows, cols)
            w_ref, m_ref, v_ref = state[3 * i:3 * i + 3]
            delta, new_m, new_v = _adam_update(g, w_ref[...], m_ref[...], v_ref[...])
            outs[4 * i][...] = g
            outs[4 * i + 1][...] = delta
            outs[4 * i + 2][...] = new_m
            outs[4 * i + 3][...] = new_v
        for j, (part, rows, cols) in enumerate(sums):
            outs[4 * n_i + j][...] = total(part, rows, cols)

    ins = list(parts) + [a for item in items for a in item[3:]]
    out_shapes = [item[3].shape for item in items for _ in range(4)]
    out_shapes += [(rows.stop - rows.start, cols.stop - cols.start) for _, rows, cols in sums]
    out = _call(body, (1,), [_whole(a.shape) for a in ins], [_whole(s) for s in out_shapes],
                [_sds(s, _F32) for s in out_shapes], name, tokens=tokens)(*ins)
    return [out[4 * i:4 * i + 4] for i in range(n_i)], out[4 * n_i:]


def _adamw(parts, w, m, v, name, tokens=()):
    rows, cols = w.shape
    tr = _row_tile(rows)
    n_parts = parts.shape[0]

    def body(p_ref, w_ref, m_ref, v_ref, g_ref, d_ref, nm_ref, nv_ref):
        g = p_ref[0].astype(_F32)
        for s in range(1, n_parts):
            g = g + p_ref[s].astype(_F32)
        new_m = ADAM_B1 * m_ref[...] + (1.0 - ADAM_B1) * g
        new_v = ADAM_B2 * v_ref[...] + (1.0 - ADAM_B2) * (g * g)
        m_hat = new_m / (1.0 - ADAM_B1 ** ADAM_STEP)
        v_hat = new_v / (1.0 - ADAM_B2 ** ADAM_STEP)
        g_ref[...] = g
        d_ref[...] = -ADAM_LR * (m_hat / (jnp.sqrt(v_hat) + ADAM_EPS) + ADAM_WD * w_ref[...])
        nm_ref[...] = new_m
        nv_ref[...] = new_v

    blk = _rows(tr, cols)
    return _call(body, (rows // tr,),
                 [pl.BlockSpec((n_parts, tr, cols), lambda i: (0, i, 0)), blk, blk, blk],
                 [blk] * 4, [_sds((rows, cols), _F32)] * 4, name, tokens=tokens)(parts, w, m, v)


_SMALL = ("g_pre_mix", "ssm_lambda_re", "ssm_lambda_im", "ssm_log_dt", "ssm_b_re", "ssm_b_im",
          "ssm_c_re", "ssm_c_im", "ssm_d", "b_glu", "attn_sinks", "g_ssm_out", "g_attn_out",
          "g_post_mix", "g_pre_ffn", "g_post_ffn")
_BIG = ("w_in", "w_glu", "w_out", "w_gate_up", "w_down")
_WEIGHTS = ("g_pre_mix", "w_in", "ssm_lambda_re", "ssm_lambda_im", "ssm_log_dt", "ssm_b_re", "ssm_b_im",
            "ssm_c_re", "ssm_c_im", "ssm_d", "w_glu", "b_glu", "attn_sinks", "g_ssm_out", "g_attn_out",
            "w_out", "g_post_mix", "g_pre_ffn", "w_gate_up", "w_down", "g_post_ffn")
_LANES = 128


_SHAPE_2D = {
    "g_pre_mix": (1, D_MODEL), "ssm_lambda_re": (SSM_GROUPS, SSM_STATE), "ssm_lambda_im": (SSM_GROUPS, SSM_STATE),
    "ssm_log_dt": (1, SSM_GROUPS), "ssm_b_re": (SSM_WIDTH, SSM_STATE), "ssm_b_im": (SSM_WIDTH, SSM_STATE),
    "ssm_c_re": (SSM_WIDTH, SSM_STATE), "ssm_c_im": (SSM_WIDTH, SSM_STATE), "ssm_d": (SSM_GROUPS, SSM_GROUP),
    "b_glu": (1, 2 * SSM_WIDTH), "attn_sinks": (1, N_Q_HEADS), "g_ssm_out": (1, SSM_WIDTH),
    "g_attn_out": (1, ATTN_WIDTH), "g_post_mix": (1, D_MODEL), "g_pre_ffn": (1, D_MODEL), "g_post_ffn": (1, D_MODEL)}
_ROW_WIDTH = {"g_pre_mix": D_MODEL, "b_glu": 2 * SSM_WIDTH, "attn_sinks": _LANES, "g_ssm_out": SSM_WIDTH,
              "g_attn_out": ATTN_WIDTH, "g_post_mix": D_MODEL, "g_pre_ffn": D_MODEL, "g_post_ffn": D_MODEL,
              "loss": _LANES}
_DENSE = ()
_PER_GROUP_TRANSPOSED = ("ssm_b_re", "ssm_b_im")


def _to_2d(name, a):
    if name in _PER_GROUP_TRANSPOSED:
        a = a.reshape(SSM_GROUPS, SSM_STATE, SSM_GROUP).transpose(0, 2, 1)
    return a.reshape(_SHAPE_2D[name])


def _from_2d(name, a, shape):
    if name in _PER_GROUP_TRANSPOSED:
        a = a.reshape(SSM_GROUPS, SSM_GROUP, SSM_STATE).transpose(0, 2, 1)
    return a.reshape(shape)


def _row_slots(names):
    slots, row, col = {}, 0, 0
    for n in names:
        width = _ROW_WIDTH[n]
        if col + width > D_MODEL:
            row, col = row + 1, 0
        slots[n] = (row, col, width)
        col += width
    return slots


def _stack_rows(named, slots):
    n_rows = -(-(max(r for r, _, _ in slots.values()) + 1) // 8) * 8
    lines = []
    for r in range(n_rows):
        pieces = [named[n] for n, (row, _, _) in slots.items() if row == r]
        used = sum(p.shape[1] for p in pieces)
        if used < D_MODEL:
            pieces.append(jnp.zeros((1, D_MODEL - used), _F32))
        lines.append(jnp.concatenate(pieces, axis=1) if len(pieces) > 1 else pieces[0])
    return jnp.concatenate(lines, axis=0)


def kernel(x, positions, g_pre_mix, w_in, ssm_lambda_re, ssm_lambda_im, ssm_log_dt, ssm_b_re, ssm_b_im, ssm_c_re, ssm_c_im, ssm_d, w_glu, b_glu, attn_sinks, g_ssm_out, g_attn_out, w_out, g_post_mix, g_pre_ffn, w_gate_up, w_down, g_post_ffn, loss_target, m_g_pre_mix, m_w_in, m_ssm_lambda_re, m_ssm_lambda_im, m_ssm_log_dt, m_ssm_b_re, m_ssm_b_im, m_ssm_c_re, m_ssm_c_im, m_ssm_d, m_w_glu, m_b_glu, m_attn_sinks, m_g_ssm_out, m_g_attn_out, m_w_out, m_g_post_mix, m_g_pre_ffn, m_w_gate_up, m_w_down, m_g_post_ffn, v_g_pre_mix, v_w_in, v_ssm_lambda_re, v_ssm_lambda_im, v_ssm_log_dt, v_ssm_b_re, v_ssm_b_im, v_ssm_c_re, v_ssm_c_im, v_ssm_d, v_w_glu, v_b_glu, v_attn_sinks, v_g_ssm_out, v_g_attn_out, v_w_out, v_g_post_mix, v_g_pre_ffn, v_w_gate_up, v_w_down, v_g_post_ffn):
    w = dict(g_pre_mix=g_pre_mix, w_in=w_in, ssm_lambda_re=ssm_lambda_re, ssm_lambda_im=ssm_lambda_im,
             ssm_log_dt=ssm_log_dt, ssm_b_re=ssm_b_re, ssm_b_im=ssm_b_im, ssm_c_re=ssm_c_re, ssm_c_im=ssm_c_im,
             ssm_d=ssm_d, w_glu=w_glu, b_glu=b_glu, attn_sinks=attn_sinks, g_ssm_out=g_ssm_out,
             g_attn_out=g_attn_out, w_out=w_out, g_post_mix=g_post_mix, g_pre_ffn=g_pre_ffn,
             w_gate_up=w_gate_up, w_down=w_down, g_post_ffn=g_post_ffn)
    m = dict(g_pre_mix=m_g_pre_mix, w_in=m_w_in, ssm_lambda_re=m_ssm_lambda_re, ssm_lambda_im=m_ssm_lambda_im,
             ssm_log_dt=m_ssm_log_dt, ssm_b_re=m_ssm_b_re, ssm_b_im=m_ssm_b_im, ssm_c_re=m_ssm_c_re,
             ssm_c_im=m_ssm_c_im, ssm_d=m_ssm_d, w_glu=m_w_glu, b_glu=m_b_glu, attn_sinks=m_attn_sinks,
             g_ssm_out=m_g_ssm_out, g_attn_out=m_g_attn_out, w_out=m_w_out, g_post_mix=m_g_post_mix,
             g_pre_ffn=m_g_pre_ffn, w_gate_up=m_w_gate_up, w_down=m_w_down, g_post_ffn=m_g_post_ffn)
    v = dict(g_pre_mix=v_g_pre_mix, w_in=v_w_in, ssm_lambda_re=v_ssm_lambda_re, ssm_lambda_im=v_ssm_lambda_im,
             ssm_log_dt=v_ssm_log_dt, ssm_b_re=v_ssm_b_re, ssm_b_im=v_ssm_b_im, ssm_c_re=v_ssm_c_re,
             ssm_c_im=v_ssm_c_im, ssm_d=v_ssm_d, w_glu=v_w_glu, b_glu=v_b_glu, attn_sinks=v_attn_sinks,
             g_ssm_out=v_g_ssm_out, g_attn_out=v_g_attn_out, w_out=v_w_out, g_post_mix=v_g_post_mix,
             g_pre_ffn=v_g_pre_ffn, w_gate_up=v_w_gate_up, w_down=v_w_down, g_post_ffn=v_g_post_ffn)

    transposed = ("w_in", "w_glu", "w_gate_up")
    native_transposed = ("w_in", "w_gate_up")
    shard = {n: (w[n][0].T if n in transposed else w[n][0]).astype(_BF16) for n in _BIG}
    gathered = {}
    for cid, names in enumerate((("w_in", "w_glu", "w_out"), ("w_gate_up", "w_down")), start=1):
        lands = _sequencer_gather([shard[n] for n in names], "gather_" + names[0], cid)
        gathered.update({n: a.reshape(-1, a.shape[2]) for n, a in zip(names, lands)})

    def fetch(names, after):
        del after
        return [gathered[n] for n in names]

    sent = []
    ids = iter(range(4, 16))
    two_step = {}

    def publish(named):
        big = [n for n in named if n in _BIG]
        if set(big) == {"w_gate_up", "w_down"}:
            blocks = [named[n].reshape(N_DEV, -1, named[n].shape[1]) for n in big]
            two_step.update(names=big, blocks=blocks,
                            received=_sequencer_pair_exchange(blocks, "grads_pair", next(ids)))
            return [named[n] for n in big]
        rows = [n for n in named if n in _ROW_WIDTH]
        dense = [n for n in named if n in _DENSE]
        plain = [n for n in named if n not in big + rows + dense]
        sources = [named[n].reshape(N_DEV, -1, named[n].shape[1]) for n in big]
        slots = _row_slots(rows)
        if rows:
            sources.append(_stack_rows(named, slots))
        sources += [named[n].reshape(-1, _LANES) for n in dense] + [named[n] for n in plain]
        flags = [True] * len(big) + [False] * (len(sources) - len(big))
        cid = next(ids)
        sent.append((big, slots, dense, plain, _sequencer_exchange(sources, flags, "grads_%d" % cid, cid)))
        return [named[n] for n in big]

    def progress(after):
        core = lax.axis_index("c").astype(jnp.int32).reshape(1)
        partials = [_pair_sum(b, r, core, "pair_sum_" + n, [after])
                    for n, b, r in zip(two_step["names"], two_step["blocks"], two_step["received"])]
        sent.append((two_step["names"], {}, [], [], _sequencer_chip_exchange(partials, "grads_chips", next(ids))))
        return partials

    p = {n: w[n] for n in _SMALL}
    grad_x = _local_step(x[0], positions[0], loss_target[0], p, fetch, publish, progress)

    state = {n: [_to_2d(n, a) for a in (w[n], m[n], v[n])] for n in _SMALL}
    result = {}
    total_loss = None
    chain = []
    for big, slots, dense, plain, lands in sent:
        lands = list(lands)
        after = list(chain)
        for name in big:
            part = lands.pop(0)
            if name in native_transposed:
                updated = _adamw(part, w[name][0].T, m[name][0].T, v[name][0].T, "adamw_" + name, after)
                result[name] = [a.T[None] for a in updated]
                chain.append(updated[3])
                continue
            if name in transposed:
                part = _sum_parts(part, "sum_" + name, after).T[None]
            updated = _adamw(part, w[name][0], m[name][0], v[name][0], "adamw_" + name, after)
            result[name] = [a[None] for a in updated]
            chain.append(updated[3])
        parts, items, sums, names = [], [], [], []
        if slots:
            parts.append(lands.pop(0))
            for name, (row, col, _) in slots.items():
                if name == "loss":
                    sums.append((0, slice(row, row + 1), slice(col, col + _LANES)))
                else:
                    items.append((0, slice(row, row + 1), slice(col, col + _SHAPE_2D[name][1]), *state[name]))
                    names.append(name)
        for name in dense:
            part = lands.pop(0).reshape((N_DEV,) + _SHAPE_2D[name])
            result[name] = _adamw(part, *state[name], "adamw_" + name, after)
            chain.append(result[name][3])
        for name in plain:
            packed = _SSM_PACK if name == "ssm_pack" else {name: (0, _SHAPE_2D[name][0], 0, _SHAPE_2D[name][1])}
            for member, (first, rows_n, lane, cols_n) in packed.items():
                items.append((len(parts), slice(first, first + rows_n), slice(lane, lane + cols_n), *state[member]))
                names.append(member)
            parts.append(lands.pop(0))
        if items:
            updated, summed = _adamw_small(parts, items, sums, "adamw_small_" + names[0], after)
            chain.append(updated[0][3])
            result.update(dict(zip(names, updated)))
            if summed:
                total_loss = summed[0][0, 0]

    out = [total_loss, grad_x[None]]
    for kind in range(4):
        out += [_from_2d(n, result[n][kind], w[n].shape) for n in _WEIGHTS]
    return tuple(out)
```

```python
import functools
import math

import numpy as np
import jax
import jax.numpy as jnp
from jax import lax
from jax.experimental import pallas as pl
from jax.experimental.pallas import tpu as pltpu
from jax.experimental.pallas import tpu_sc as plsc

D_MODEL = 1024
SSM_WIDTH = 512
SSM_GROUP = 16
SSM_GROUPS = 32
SSM_STATE = 64
N_STATE = SSM_GROUPS * SSM_STATE
ATTN_WIDTH = 512
HEAD_DIM = 64
N_Q_HEADS = 8
N_KV_HEADS = 2
Q_PER_KV = 4
KV_WIDTH = 128
IN_WIDTH = 1280
BLOCK = 128
ROPE_DIM = 16
ROPE_THETA = 500000.0
D_FF = 2816
NORM_EPS = 1e-6
MASK_VALUE = -1e30
ADAM_LR = 0.001
ADAM_B1 = 0.9
ADAM_B2 = 0.999
ADAM_EPS = 1e-08
ADAM_WD = 0.01
ADAM_STEP = 10

N_DEV = 8
SCAN_CHUNKS = 8
SCAN_COLS = 512
SCAN_UNROLL = 8
FFN_CHUNK = 2816
TOKEN_TILE = 256
VMEM_LIMIT = 56 * 1024 * 1024

_F32 = jnp.float32
_BF16 = jnp.bfloat16
_MXU = jnp.bfloat16

_NN = ((1,), (0,))
_NT = ((1,), (1,))
_TN = ((0,), (0,))


def _dot(a, b, dims):
    return lax.dot_general(a.astype(_MXU), b.astype(_MXU), (dims, ((), ())),
                           preferred_element_type=_F32)


def _dot_exact(a, b, dims):
    return lax.dot_general(a.astype(_F32), b.astype(_F32), (dims, ((), ())),
                           precision=lax.Precision.HIGHEST, preferred_element_type=_F32)


def _iota(shape, dim):
    return lax.broadcasted_iota(jnp.int32, shape, dim)


def _rms_fwd(x, g):
    r = lax.rsqrt(jnp.mean(x * x, axis=-1, keepdims=True) + NORM_EPS)
    return x * r * g, r


def _rms_bwd(dy, x, g, r):
    a = dy * g
    xn = x * r
    dx = r * (a - xn * jnp.mean(a * xn, axis=-1, keepdims=True))
    dg = jnp.sum(dy * xn, axis=0, keepdims=True)
    return dx, dg


def _call(body, grid, in_specs, out_specs, out_shape, name, scratch=(), tokens=()):
    params = pltpu.CompilerParams(dimension_semantics=("arbitrary",) * len(grid),
                                  vmem_limit_bytes=VMEM_LIMIT)
    n_in, n_tok = len(in_specs), len(tokens)

    def run(*refs):
        return body(*refs[:n_in], *refs[n_in + n_tok:])

    call = pl.pallas_call(run, grid=grid,
                          in_specs=list(in_specs) + [pl.BlockSpec(memory_space=pl.ANY)] * n_tok,
                          out_specs=out_specs, out_shape=out_shape, scratch_shapes=list(scratch),
                          compiler_params=params, name=name)
    return lambda *args: call(*args, *tokens)


def _rows(tm, n):
    return pl.BlockSpec((tm, n), lambda i: (i, 0))


def _whole(shape):
    nd = len(shape)
    return pl.BlockSpec(shape, lambda i: (0,) * nd)


def _sds(shape, dtype):
    return jax.ShapeDtypeStruct(shape, dtype)


def _tile(L):
    return min(TOKEN_TILE, L)


def _chunk_tile(L):
    return L // SCAN_CHUNKS


def _chunk_block(L, n):
    return pl.BlockSpec((_chunk_tile(L), n), lambda i: (0, i))


def _chunk_shape(L, n):
    return (_chunk_tile(L), SCAN_CHUNKS * n)


def _accumulate(ref, val, first):
    @pl.when(first)
    def _():
        ref[...] = val

    @pl.when(jnp.logical_not(first))
    def _():
        ref[...] += val


def _rope_rows():
    half = ROPE_DIM // 2
    inv = (np.float32(ROPE_THETA) ** (-np.arange(half, dtype=np.float32) * np.float32(2.0) / np.float32(ROPE_DIM))).astype(np.float32)
    col = np.arange(KV_WIDTH) % HEAD_DIM
    freq = np.where(col < ROPE_DIM, inv[col % half], 0.0).astype(np.float32)
    sign = np.where(col < half, -1.0, np.where(col < ROPE_DIM, 1.0, 0.0)).astype(np.float32)
    return freq[None, :], sign[None, :]


def _rope_tables(pos_col):
    L = pos_col.shape[0]
    tm = _tile(L)
    freq, sign = _rope_rows()

    def body(pos_ref, freq_ref, sign_ref, cos_ref, sin_ref):
        ang = pos_ref[...].astype(_F32) * freq_ref[...]
        cos_ref[...] = jnp.cos(ang)
        sin_ref[...] = jnp.sin(ang) * sign_ref[...]

    return _call(body, (L // tm,),
                 [_rows(tm, 1), _whole((1, KV_WIDTH)), _whole((1, KV_WIDTH))],
                 [_rows(tm, KV_WIDTH), _rows(tm, KV_WIDTH)],
                 [_sds((L, KV_WIDTH), _F32)] * 2, "rope_tables")(pos_col, jnp.asarray(freq), jnp.asarray(sign))


def _widen(t, width):
    return t if width == KV_WIDTH else jnp.concatenate([t] * (width // KV_WIDTH), axis=1)


def _rope_partner(t):
    w = t.shape[1]
    in_head = _iota((1, w), 1) & (HEAD_DIM - 1)
    second = jnp.where(in_head < ROPE_DIM, pltpu.roll(t, ROPE_DIM // 2, 1), 0.0)
    return jnp.where(in_head < ROPE_DIM // 2, pltpu.roll(t, w - ROPE_DIM // 2, 1), second)


def _rope_apply(t, cos_t, sin_t):
    w = t.shape[1]
    return t * _widen(cos_t, w) + _rope_partner(t) * _widen(sin_t, w)


def _rope_transpose(dt, cos_t, sin_t):
    w = dt.shape[1]
    return dt * _widen(cos_t, w) + _rope_partner(dt * _widen(sin_t, w))


def _in_proj(x, g_pre_mix, w_in, cos_t, sin_t):
    L = x.shape[0]
    tm = _chunk_tile(L)

    def body(x_ref, g_ref, w_ref, cos_ref, sin_ref, hn_ref, u_ref, q_ref, k_ref, v_ref):
        hn, _ = _rms_fwd(x_ref[...], g_ref[...])
        hn = hn.astype(_BF16)
        hn_ref[...] = hn
        proj = _dot(hn, w_ref[...], _NT)
        u_ref[...] = proj[:, :SSM_WIDTH]
        q = proj[:, SSM_WIDTH:SSM_WIDTH + ATTN_WIDTH]
        k = proj[:, SSM_WIDTH + ATTN_WIDTH:SSM_WIDTH + ATTN_WIDTH + KV_WIDTH]
        cos_v, sin_v = cos_ref[...], sin_ref[...]
        q_ref[...] = _rope_apply(q, cos_v, sin_v).astype(_BF16)
        k_ref[...] = _rope_apply(k, cos_v, sin_v).astype(_BF16)
        v_ref[...] = proj[:, SSM_WIDTH + ATTN_WIDTH + KV_WIDTH:].astype(_BF16)

    return _call(body, (L // tm,),
                 [_rows(tm, D_MODEL), _whole((1, D_MODEL)), _whole((IN_WIDTH, D_MODEL)),
                  _rows(tm, KV_WIDTH), _rows(tm, KV_WIDTH)],
                 [_rows(tm, D_MODEL), _chunk_block(L, SSM_WIDTH), _rows(tm, ATTN_WIDTH),
                  _rows(tm, KV_WIDTH), _rows(tm, KV_WIDTH)],
                 [_sds((L, D_MODEL), _BF16), _sds(_chunk_shape(L, SSM_WIDTH), _F32), _sds((L, ATTN_WIDTH), _BF16),
                  _sds((L, KV_WIDTH), _BF16), _sds((L, KV_WIDTH), _BF16)],
                 "in_proj")(x, g_pre_mix, w_in, cos_t, sin_t)


def _s5_discretize(lam_re, lam_im, log_dt):
    lr = jnp.minimum(lam_re, -1e-4)
    li = lam_im
    dt = jnp.exp(log_dt)
    mag = jnp.exp(lr * dt)
    ar = mag * jnp.cos(li * dt)
    ai = mag * jnp.sin(li * dt)
    den = lr * lr + li * li
    fr = ((ar - 1.0) * lr + ai * li) / den
    fi = (ai * lr - (ar - 1.0) * li) / den
    return ar, ai, fr, fi


def _s5_bbar(lam_re, lam_im, log_dt, b_re, b_im):
    ar, ai, fr, fi = _s5_discretize(lam_re, lam_im, log_dt)
    return ar, ai, fr * b_re - fi * b_im, fr * b_im + fi * b_re


def _spread_masks():
    e16 = (_iota((SSM_GROUP, SSM_WIDTH), 1) & (SSM_GROUP - 1)) == _iota((SSM_GROUP, SSM_WIDTH), 0)
    e64 = (_iota((SSM_STATE, N_STATE), 1) & (SSM_STATE - 1)) == _iota((SSM_STATE, N_STATE), 0)
    mask_b = (_iota((N_STATE, SSM_WIDTH), 0) >> 6) == (_iota((N_STATE, SSM_WIDTH), 1) >> 4)
    mask_c = (_iota((SSM_WIDTH, N_STATE), 0) >> 4) == (_iota((SSM_WIDTH, N_STATE), 1) >> 6)
    return e16.astype(_F32), e64.astype(_F32), mask_b, mask_c


SUPER = 4
SB_STATE = N_STATE // SUPER
SB_WIDTH = SSM_WIDTH // SUPER


def _sb_state(k):
    return slice(SB_STATE * k, SB_STATE * (k + 1))


def _sb_width(k):
    return slice(SB_WIDTH * k, SB_WIDTH * (k + 1))


def _dt_column(log_dt_row):
    eye = _iota((SSM_GROUPS, SSM_GROUPS), 0) == _iota((SSM_GROUPS, SSM_GROUPS), 1)
    return jnp.sum(jnp.where(eye, log_dt_row, 0.0), axis=1, keepdims=True)


def _group_masks():
    e64 = ((_iota((SSM_STATE, N_STATE), 1) & (SSM_STATE - 1)) == _iota((SSM_STATE, N_STATE), 0)).astype(_F32)
    own = _iota((SSM_GROUPS, N_STATE), 0) == (_iota((SSM_GROUPS, N_STATE), 1) >> 6)
    return e64, own


def _rows_of_group():
    return ((_iota((SSM_WIDTH, SSM_GROUPS), 0) >> 4) == _iota((SSM_WIDTH, SSM_GROUPS), 1)).astype(_F32)


def _ssm_prep(lam_re, lam_im, log_dt, b_re, b_im, c_re, c_im):
    def body(lr_ref, li_ref, ld_ref, bre, bim, cre, cim, ar_ref, ai_ref, btr, bti, ctr, cti):
        ar, ai, fr, fi = _s5_discretize(lr_ref[...], li_ref[...], _dt_column(ld_ref[...]))
        e64, own = _group_masks()
        mask_c = (_iota((SSM_WIDTH, N_STATE), 0) >> 4) == (_iota((SSM_WIDTH, N_STATE), 1) >> 6)

        def to_row(t):
            return jnp.sum(jnp.where(own, _dot_exact(t, e64, _NN), 0.0), axis=0, keepdims=True)

        def fold(m):
            full = jnp.where(mask_c, _dot(m, e64, _NN), 0.0)
            return sum(full[_sb_width(k), :] for k in range(SUPER)).astype(_BF16)

        ar_ref[...] = to_row(ar)
        ai_ref[...] = to_row(ai)
        spread = _rows_of_group()
        fr_t = _dot_exact(spread, fr, _NN)
        fi_t = _dot_exact(spread, fi, _NN)
        btr[...] = fold(fr_t * bre[...] - fi_t * bim[...])
        bti[...] = fold(fr_t * bim[...] + fi_t * bre[...])
        ctr[...] = fold(cre[...])
        cti[...] = fold(cim[...])

    row = (1, N_STATE)
    ins = [lam_re, lam_im, log_dt, b_re, b_im, c_re, c_im]
    return _call(body, (1,), [_whole(a.shape) for a in ins],
                 [_whole(row), _whole(row)] + [_whole((SB_WIDTH, N_STATE))] * 4,
                 [_sds(row, _F32), _sds(row, _F32)] + [_sds((SB_WIDTH, N_STATE), _BF16)] * 4,
                 "ssm_prep")(*ins)


def _ssm_bu(u, bt_re, bt_im):
    L = u.shape[0]
    tm = _tile(L)

    def body(u_ref, br_ref, bi_ref, or_ref, oi_ref):
        for k in range(SUPER):
            ub = u_ref[:, _sb_width(k)].astype(_BF16)
            or_ref[:, _sb_state(k)] = _dot(ub, br_ref[:, _sb_state(k)], _NN)
            oi_ref[:, _sb_state(k)] = _dot(ub, bi_ref[:, _sb_state(k)], _NN)

    return _call(body, (L // tm,),
                 [_rows(tm, SSM_WIDTH), _whole((SB_WIDTH, N_STATE)), _whole((SB_WIDTH, N_STATE))],
                 [_rows(tm, N_STATE), _rows(tm, N_STATE)],
                 [_sds((L, N_STATE), _F32)] * 2, "ssm_bu")(u, bt_re, bt_im)


def _complex_power(ar, ai, n):
    def step(_, c):
        pr, pi = c
        return pr * ar - pi * ai, pr * ai + pi * ar
    return lax.fori_loop(0, n, step, (jnp.ones_like(ar), jnp.zeros_like(ai)))


def _chunk_carries(er, ei, pr, pi, reverse):
    rows = _iota(er.shape, 0)
    sr = jnp.zeros_like(pr)
    si = jnp.zeros_like(pi)
    out_r = jnp.zeros_like(er)
    out_i = jnp.zeros_like(ei)
    order = range(SCAN_CHUNKS - 1, 0, -1) if reverse else range(SCAN_CHUNKS - 1)
    for c in order:
        e_r = er[c:c + 1, :]
        e_i = ei[c:c + 1, :]
        sr, si = pr * sr - pi * si + e_r, pr * si + pi * sr + e_i
        nxt = c - 1 if reverse else c + 1
        out_r = jnp.where(rows == nxt, sr, out_r)
        out_i = jnp.where(rows == nxt, si, out_i)
    return out_r, out_i


def _scan_fwd(b_re, b_im, a_re, a_im):
    T = b_re.shape[0]
    W = SCAN_COLS
    blk = pl.BlockSpec((T, SCAN_CHUNKS, W), lambda j: (0, 0, j))
    vec = pl.BlockSpec((1, W), lambda j: (0, j))

    def body(br_ref, bi_ref, ar_ref, ai_ref, xr_ref, xi_ref):
        ar, ai = ar_ref[...], ai_ref[...]
        ar8 = jnp.broadcast_to(ar, (SCAN_CHUNKS, W))
        ai8 = jnp.broadcast_to(ai, (SCAN_CHUNKS, W))

        def local(t, c):
            cr, ci = c
            return ar8 * cr - ai8 * ci + br_ref[t], ar8 * ci + ai8 * cr + bi_ref[t]

        zero = jnp.zeros((SCAN_CHUNKS, W), _F32)
        er, ei = lax.fori_loop(0, T, local, (zero, zero), unroll=SCAN_UNROLL)
        pr, pi = _complex_power(ar, ai, T)
        sr, si = _chunk_carries(er, ei, pr, pi, reverse=False)

        def final(t, c):
            nr, ni = local(t, c)
            xr_ref[t] = nr
            xi_ref[t] = ni
            return nr, ni

        lax.fori_loop(0, T, final, (sr, si), unroll=SCAN_UNROLL)

    shape = _sds(b_re.shape, _F32)
    return _call(body, (N_STATE // W,), [blk, blk, vec, vec], [blk, blk], [shape, shape],
                 "scan_fwd")(b_re, b_im, a_re, a_im)


def _scan_bwd(dx_re, dx_im, x_re, x_im, a_re, a_im, tokens=()):
    T = dx_re.shape[0]
    W = SCAN_COLS
    blk = pl.BlockSpec((T, SCAN_CHUNKS, W), lambda j: (0, 0, j))
    vec = pl.BlockSpec((1, W), lambda j: (0, j))

    def body(dr_ref, di_ref, xr_ref, xi_ref, ar_ref, ai_ref, lr_ref, li_ref, dar_ref, dai_ref):
        ar, ai = ar_ref[...], ai_ref[...]
        ar8 = jnp.broadcast_to(ar, (SCAN_CHUNKS, W))
        ai8 = jnp.broadcast_to(ai, (SCAN_CHUNKS, W))

        def local(t, c):
            cr, ci = c
            return ar8 * cr + ai8 * ci + dr_ref[t], ar8 * ci - ai8 * cr + di_ref[t]

        zero = jnp.zeros((SCAN_CHUNKS, W), _F32)
        er, ei = lax.fori_loop(0, T, lambda k, c: local(T - 1 - k, c), (zero, zero), unroll=SCAN_UNROLL)
        pr, pi = _complex_power(ar, -ai, T)
        sr, si = _chunk_carries(er, ei, pr, pi, reverse=True)

        def grad_a(acc, nr, ni, xpr, xpi):
            return acc[0] + nr * xpr + ni * xpi, acc[1] + ni * xpr - nr * xpi

        def final(k, c):
            t = T - 1 - k
            nr, ni = local(t, c[:2])
            lr_ref[t] = nr
            li_ref[t] = ni
            gr, gi = grad_a(c[2:], nr, ni, xr_ref[t - 1], xi_ref[t - 1])
            return nr, ni, gr, gi

        cr, ci, gr, gi = lax.fori_loop(0, T - 1, final, (sr, si, zero, zero), unroll=SCAN_UNROLL)
        nr, ni = local(0, (cr, ci))
        lr_ref[0] = nr
        li_ref[0] = ni
        first = _iota((SCAN_CHUNKS, W), 0) == 0
        xpr = jnp.where(first, 0.0, pltpu.roll(xr_ref[T - 1], 1, 0))
        xpi = jnp.where(first, 0.0, pltpu.roll(xi_ref[T - 1], 1, 0))
        gr, gi = grad_a((gr, gi), nr, ni, xpr, xpi)
        dar_ref[...] = jnp.sum(gr, axis=0, keepdims=True)
        dai_ref[...] = jnp.sum(gi, axis=0, keepdims=True)

    shape = _sds(dx_re.shape, _F32)
    row = _sds((1, N_STATE), _F32)
    return _call(body, (N_STATE // W,), [blk, blk, blk, blk, vec, vec], [blk, blk, vec, vec],
                 [shape, shape, row, row], "scan_bwd", tokens=tokens)(dx_re, dx_im, x_re, x_im, a_re, a_im)


_GELU_K = math.sqrt(2.0 / math.pi)
_GELU_C = 0.044715


def _gelu(y):
    return 0.5 * y * (1.0 + jnp.tanh(_GELU_K * (y + _GELU_C * y * y * y)))


def _gelu_grad(y):
    t = jnp.tanh(_GELU_K * (y + _GELU_C * y * y * y))
    return 0.5 * (1.0 + t) + 0.5 * y * (1.0 - t * t) * _GELU_K * (1.0 + 3.0 * _GELU_C * y * y)


def _step_rows(t):
    return pl.ds(pl.multiple_of(t * SCAN_CHUNKS, SCAN_CHUNKS), SCAN_CHUNKS)


def _scan_in_place(br, bi, ar, ai, T):
    W = br.shape[1]
    ar8 = jnp.broadcast_to(ar, (SCAN_CHUNKS, W))
    ai8 = jnp.broadcast_to(ai, (SCAN_CHUNKS, W))

    def local(t, c):
        cr, ci = c
        rows = _step_rows(t)
        return ar8 * cr - ai8 * ci + br[rows, :], ar8 * ci + ai8 * cr + bi[rows, :]

    zero = jnp.zeros((SCAN_CHUNKS, W), _F32)
    er, ei = lax.fori_loop(0, T, local, (zero, zero), unroll=SCAN_UNROLL)
    pr, pi = _complex_power(ar, ai, T)
    sr, si = _chunk_carries(er, ei, pr, pi, reverse=False)

    def final(t, c):
        nr, ni = local(t, c)
        rows = _step_rows(t)
        br[rows, :] = nr
        bi[rows, :] = ni
        return nr, ni

    lax.fori_loop(0, T, final, (sr, si), unroll=SCAN_UNROLL)


def _scan_reverse_in_place(dr, di, xr, xi, ar, ai, T):
    W = dr.shape[1]
    ar8 = jnp.broadcast_to(ar, (SCAN_CHUNKS, W))
    ai8 = jnp.broadcast_to(ai, (SCAN_CHUNKS, W))

    def local(t, c):
        cr, ci = c
        rows = _step_rows(t)
        return ar8 * cr + ai8 * ci + dr[rows, :], ar8 * ci - ai8 * cr + di[rows, :]

    zero = jnp.zeros((SCAN_CHUNKS, W), _F32)
    er, ei = lax.fori_loop(0, T, lambda k, c: local(T - 1 - k, c), (zero, zero), unroll=SCAN_UNROLL)
    pr, pi = _complex_power(ar, -ai, T)
    sr, si = _chunk_carries(er, ei, pr, pi, reverse=True)

    def grad_a(acc, nr, ni, xpr, xpi):
        return acc[0] + nr * xpr + ni * xpi, acc[1] + ni * xpr - nr * xpi

    def final(k, c):
        t = T - 1 - k
        nr, ni = local(t, c[:2])
        rows = _step_rows(t)
        dr[rows, :] = nr
        di[rows, :] = ni
        before = _step_rows(t - 1)
        gr, gi = grad_a(c[2:], nr, ni, xr[before, :], xi[before, :])
        return nr, ni, gr, gi

    cr, ci, gr, gi = lax.fori_loop(0, T - 1, final, (sr, si, zero, zero), unroll=SCAN_UNROLL)
    nr, ni = local(0, (cr, ci))
    dr[_step_rows(0), :] = nr
    di[_step_rows(0), :] = ni
    first = _iota((SCAN_CHUNKS, W), 0) == 0
    last = _step_rows(T - 1)
    xpr = jnp.where(first, 0.0, pltpu.roll(xr[last, :], 1, 0))
    xpi = jnp.where(first, 0.0, pltpu.roll(xi[last, :], 1, 0))
    gr, gi = grad_a((gr, gi), nr, ni, xpr, xpi)
    return jnp.sum(gr, axis=0, keepdims=True), jnp.sum(gi, axis=0, keepdims=True)


def _ssm_super_specs(L):
    width = pl.BlockSpec((L, SB_WIDTH), lambda k: (0, k))
    matrix = pl.BlockSpec((SB_WIDTH, SB_STATE), lambda k: (0, k))
    row = pl.BlockSpec((1, SB_STATE), lambda k: (0, k))
    return width, matrix, row


def _ssm_states(u_ref, br_ref, bi_ref, ar_ref, ai_ref, xr, xi, T):
    ub = u_ref[...].astype(_BF16)
    xr[...] = _dot(ub, br_ref[...], _NN)
    xi[...] = _dot(ub, bi_ref[...], _NN)
    _scan_in_place(xr, xi, ar_ref[...], ai_ref[...], T)


def _ssm_core_fwd(u, bt_re, bt_im, ct_re, ct_im, a_re, a_im):
    L = u.shape[0]
    T = L // SCAN_CHUNKS

    def body(u_ref, br_ref, bi_ref, cr_ref, ci_ref, ar_ref, ai_ref, y_ref, xr, xi):
        _ssm_states(u_ref, br_ref, bi_ref, ar_ref, ai_ref, xr, xi, T)
        y_ref[...] = _dot(xr[...], cr_ref[...], _NT) - _dot(xi[...], ci_ref[...], _NT)

    width, matrix, row = _ssm_super_specs(L)
    return _call(body, (SUPER,), [width, matrix, matrix, matrix, matrix, row, row], width,
                 _sds((L, SSM_WIDTH), _F32), "ssm_core_fwd",
                 scratch=[pltpu.VMEM((L, SB_STATE), _F32)] * 2)(u, bt_re, bt_im, ct_re, ct_im, a_re, a_im)


def _ssm_core_bwd(u, dy, dud, bt_re, bt_im, ct_re, ct_im, a_re, a_im, tokens=()):
    L = u.shape[0]
    T = L // SCAN_CHUNKS

    def body(u_ref, dy_ref, dud_ref, br_ref, bi_ref, cr_ref, ci_ref, ar_ref, ai_ref,
             du_ref, dcr_ref, dci_ref, dbr_ref, dbi_ref, dar_ref, dai_ref, xr, xi, lr, li):
        _ssm_states(u_ref, br_ref, bi_ref, ar_ref, ai_ref, xr, xi, T)
        dyb = dy_ref[...]
        lr[...] = _dot(dyb, cr_ref[...], _NN)
        li[...] = -_dot(dyb, ci_ref[...], _NN)
        da_re, da_im = _scan_reverse_in_place(lr, li, xr, xi, ar_ref[...], ai_ref[...], T)
        dar_ref[...] = da_re
        dai_ref[...] = da_im
        du_ref[...] = _dot(lr[...], br_ref[...], _NT) + _dot(li[...], bi_ref[...], _NT) + dud_ref[...]
        ub = u_ref[...].astype(_BF16)
        dcr_ref[...] = _dot(dyb, xr[...], _TN)
        dci_ref[...] = _dot(dyb, xi[...], _TN)
        dbr_ref[...] = _dot(ub, lr[...], _TN)
        dbi_ref[...] = _dot(ub, li[...], _TN)

    width, matrix, row = _ssm_super_specs(L)
    return _call(body, (SUPER,), [width, width, width, matrix, matrix, matrix, matrix, row, row],
                 [width] + [matrix] * 4 + [row] * 2,
                 [_sds((L, SSM_WIDTH), _F32)] + [_sds((SB_WIDTH, N_STATE), _F32)] * 4 + [_sds((1, N_STATE), _F32)] * 2,
                 "ssm_core_bwd", scratch=[pltpu.VMEM((L, SB_STATE), _F32)] * 4,
                 tokens=tokens)(u, dy, dud, bt_re, bt_im, ct_re, ct_im, a_re, a_im)


def _ssm_out(cx, u, d_row, w_glu, b_glu, g_ssm):
    L = u.shape[0]
    tm = _tile(L)

    def body(cx_ref, u_ref, d_ref, w_ref, b_ref, g_ref, y_ref, z_ref, n_ref):
        y = cx_ref[...] + d_ref[...] * u_ref[...]
        y_ref[...] = y
        z = _dot(_gelu(y), w_ref[...], _NT) + b_ref[...]
        z_ref[...] = z
        out = z[:, :SSM_WIDTH] * jax.nn.sigmoid(z[:, SSM_WIDTH:])
        n, _ = _rms_fwd(out, g_ref[...])
        n_ref[...] = n.astype(_BF16)

    return _call(body, (L // tm,),
                 [_rows(tm, SSM_WIDTH), _rows(tm, SSM_WIDTH), _whole((1, SSM_WIDTH)),
                  _whole((2 * SSM_WIDTH, SSM_WIDTH)), _whole((1, 2 * SSM_WIDTH)), _whole((1, SSM_WIDTH))],
                 [_rows(tm, SSM_WIDTH), _rows(tm, 2 * SSM_WIDTH), _rows(tm, SSM_WIDTH)],
                 [_sds((L, SSM_WIDTH), _F32), _sds((L, 2 * SSM_WIDTH), _F32), _sds((L, SSM_WIDTH), _BF16)],
                 "ssm_out")(cx, u, d_row, w_glu, b_glu, g_ssm)


def _ssm_out_bwd(dn, y, z, u, d_row, w_glu, g_ssm):
    L = u.shape[0]
    tm = _tile(L)

    def body(dn_ref, y_ref, z_ref, u_ref, d_ref, w_ref, g_ref,
             gy_ref, dz_ref, dy_ref, dud_ref, dg_ref, db_ref, dd_ref):
        first = pl.program_id(0) == 0
        z = z_ref[...]
        z1, z2 = z[:, :SSM_WIDTH], z[:, SSM_WIDTH:]
        sig = jax.nn.sigmoid(z2)
        out = z1 * sig
        g = g_ref[...]
        _, r = _rms_fwd(out, g)
        dout, dg = _rms_bwd(dn_ref[...], out, g, r)
        _accumulate(dg_ref, dg, first)
        dz = jnp.concatenate([dout * sig, dout * z1 * sig * (1.0 - sig)], axis=1)
        _accumulate(db_ref, jnp.sum(dz, axis=0, keepdims=True), first)
        dzb = dz.astype(_BF16)
        dz_ref[...] = dzb
        y = y_ref[...]
        gy_ref[...] = _gelu(y).astype(_BF16)
        dy = _dot(dzb, w_ref[...], _NN) * _gelu_grad(y)
        u = u_ref[...]
        _accumulate(dd_ref, jnp.sum(dy * u, axis=0, keepdims=True), first)
        dud_ref[...] = d_ref[...] * dy
        dy_ref[...] = dy.astype(_BF16)

    row = _whole((1, SSM_WIDTH))
    return _call(body, (L // tm,),
                 [_rows(tm, SSM_WIDTH), _rows(tm, SSM_WIDTH), _rows(tm, 2 * SSM_WIDTH), _rows(tm, SSM_WIDTH),
                  row, _whole((2 * SSM_WIDTH, SSM_WIDTH)), row],
                 [_rows(tm, SSM_WIDTH), _rows(tm, 2 * SSM_WIDTH), _rows(tm, SSM_WIDTH), _rows(tm, SSM_WIDTH),
                  row, _whole((1, 2 * SSM_WIDTH)), row],
                 [_sds((L, SSM_WIDTH), _BF16), _sds((L, 2 * SSM_WIDTH), _BF16), _sds((L, SSM_WIDTH), _BF16),
                  _sds((L, SSM_WIDTH), _F32),
                  _sds((1, SSM_WIDTH), _F32), _sds((1, 2 * SSM_WIDTH), _F32), _sds((1, SSM_WIDTH), _F32)],
                 "ssm_out_bwd")(dn, y, z, u, d_row, w_glu, g_ssm)


def _ssm_du(lam_re, lam_im, bt_re, bt_im, dud):
    L = dud.shape[0]
    tm = _tile(L)

    def body(lr_ref, li_ref, br_ref, bi_ref, dud_ref, du_ref):
        for k in range(SUPER):
            du_ref[:, _sb_width(k)] = (_dot(lr_ref[:, _sb_state(k)], br_ref[:, _sb_state(k)], _NT)
                                       + _dot(li_ref[:, _sb_state(k)], bi_ref[:, _sb_state(k)], _NT)
                                       + dud_ref[:, _sb_width(k)])

    return _call(body, (L // tm,),
                 [_rows(tm, N_STATE), _rows(tm, N_STATE), _whole((SB_WIDTH, N_STATE)),
                  _whole((SB_WIDTH, N_STATE)), _rows(tm, SSM_WIDTH)],
                 _rows(tm, SSM_WIDTH), _sds((L, SSM_WIDTH), _F32), "ssm_du")(lam_re, lam_im, bt_re, bt_im, dud)


def _ssm_weight_grads(dy, x_re, x_im, lam_re, lam_im, u):
    L = u.shape[0]

    def body(dy_ref, xr_ref, xi_ref, lr_ref, li_ref, u_ref, dcr_ref, dci_ref, dbr_ref, dbi_ref):
        dyb = dy_ref[...]
        ub = u_ref[...].astype(_BF16)
        dcr_ref[...] = _dot(dyb, xr_ref[...], _TN)
        dci_ref[...] = _dot(dyb, xi_ref[...], _TN)
        dbr_ref[...] = _dot(ub, lr_ref[...], _TN)
        dbi_ref[...] = _dot(ub, li_ref[...], _TN)

    width = pl.BlockSpec((L, SB_WIDTH), lambda k: (0, k))
    state = pl.BlockSpec((L, SB_STATE), lambda k: (0, k))
    out = pl.BlockSpec((SB_WIDTH, SB_STATE), lambda k: (0, k))
    return _call(body, (SUPER,), [width, state, state, state, state, width], [out] * 4,
                 [_sds((SB_WIDTH, N_STATE), _F32)] * 4,
                 "ssm_weight_grads")(dy, x_re, x_im, lam_re, lam_im, u)


_SSM_PACK = {"ssm_b_re": (0, SSM_WIDTH, 0, SSM_STATE), "ssm_c_re": (0, SSM_WIDTH, 64, SSM_STATE),
             "ssm_b_im": (512, SSM_WIDTH, 0, SSM_STATE), "ssm_c_im": (512, SSM_WIDTH, 64, SSM_STATE),
             "ssm_lambda_re": (1024, SSM_GROUPS, 0, SSM_STATE), "ssm_lambda_im": (1024, SSM_GROUPS, 64, SSM_STATE),
             "ssm_d": (1056, SSM_GROUPS, 0, SSM_GROUP), "ssm_log_dt": (1088, 1, 0, SSM_GROUPS)}
_PACK_TILE = 16
_SSM_PACK_ROWS = 1088 + _PACK_TILE


def _ssm_param_bwd(da_re, da_im, dbt_re, dbt_im, dct_re, dct_im, lam_re, lam_im, log_dt, b_re, b_im, g_d):
    def body(dar, dai, dbr, dbi, dcr, dci, lr_ref, li_ref, ld_ref, bre_ref, bim_ref, gd_ref, pack_ref):
        lane_in = _iota((SSM_STATE, _LANES), 0)
        lane_out = _iota((SSM_STATE, _LANES), 1)
        low = (lane_out == lane_in).astype(_F32)
        high = (lane_out == lane_in + SSM_STATE).astype(_F32)

        def side_by_side(a, b):
            return _dot_exact(a, low, _NN) + _dot_exact(b, high, _NN)

        tail = _SSM_PACK["ssm_d"][0]
        pack_ref[tail:, :] = jnp.zeros((_SSM_PACK_ROWS - tail, _LANES), _BF16)
        pack_ref[tail:tail + SSM_GROUPS, 0:SSM_GROUP] = gd_ref[...].astype(_BF16)
        own_c = (_iota((SB_WIDTH, SB_STATE), 0) >> 4) == (_iota((SB_WIDTH, SB_STATE), 1) >> 6)

        def unfold(ref):
            blocks = []
            for k in range(SUPER):
                t = jnp.where(own_c, ref[:, _sb_state(k)], 0.0)
                t = sum(t[:, 128 * i:128 * (i + 1)] for i in range(SB_STATE // 128))
                blocks.append((t + pltpu.roll(t, SSM_STATE, 1))[:, :SSM_STATE])
            return jnp.concatenate(blocks, axis=0)

        dbb_re, dbb_im = unfold(dbr), unfold(dbi)
        b_re, b_im = bre_ref[...], bim_ref[...]
        dt_col = _dt_column(ld_ref[...])
        (_, _, fr, fi), vjp = jax.vjp(_s5_discretize, lr_ref[...], li_ref[...], dt_col)
        spread = _rows_of_group()
        fr_t = _dot_exact(spread, fr, _NN)
        fi_t = _dot_exact(spread, fi, _NN)
        pack_ref[0:SSM_WIDTH, :] = side_by_side(fr_t * dbb_re + fi_t * dbb_im, unfold(dcr)).astype(_BF16)
        pack_ref[SSM_WIDTH:2 * SSM_WIDTH, :] = side_by_side(fr_t * dbb_im - fi_t * dbb_re, -unfold(dci)).astype(_BF16)
        d_fr = _dot_exact(spread, dbb_re * b_re + dbb_im * b_im, _TN)
        d_fi = _dot_exact(spread, dbb_im * b_re - dbb_re * b_im, _TN)
        e64, own = _group_masks()

        def from_row(ref):
            return _dot_exact(jnp.where(own, ref[...], 0.0), e64, _NT)

        d_lr, d_li, d_dt = vjp((from_row(dar), from_row(dai), d_fr, d_fi))
        lam_rows = _SSM_PACK["ssm_lambda_re"][0]
        pack_ref[lam_rows:lam_rows + SSM_GROUPS, :] = side_by_side(d_lr, d_li).astype(_BF16)
        eye = (_iota((SSM_GROUPS, SSM_GROUPS), 0) == _iota((SSM_GROUPS, SSM_GROUPS), 1)).astype(_F32)
        dt_row = _SSM_PACK["ssm_log_dt"][0]
        pack_ref[dt_row:dt_row + _PACK_TILE, 0:SSM_GROUPS] = _dot_exact(
            jnp.broadcast_to(d_dt, (SSM_GROUPS, 128)), eye, _TN)[0:_PACK_TILE].astype(_BF16)

    ins = [da_re, da_im, dbt_re, dbt_im, dct_re, dct_im, lam_re, lam_im, log_dt, b_re, b_im, g_d]
    out = (_SSM_PACK_ROWS, _LANES)
    return _call(body, (1,), [_whole(a.shape) for a in ins], _whole(out), _sds(out, _BF16), "ssm_param_bwd")(*ins)


def _head_spread(j):
    r = _iota((KV_WIDTH, 256), 0)
    c = _iota((KV_WIDTH, 256), 1)
    return (r == HEAD_DIM * j + (c & (HEAD_DIM - 1))).astype(_BF16)


STACK = Q_PER_KV * BLOCK


def _stack_heads(t):
    lane_head = _iota((1, 256), 1) >> 6
    return jnp.concatenate([jnp.where(lane_head == g, t, jnp.zeros_like(t)) for g in range(Q_PER_KV)], axis=0)


def _unstack_heads(t):
    lane_head = _iota((1, 256), 1) >> 6
    return sum(jnp.where(lane_head == g, t[BLOCK * g:BLOCK * (g + 1)], 0.0) for g in range(Q_PER_KV))


def _stacked_sinks(sink_ref, j):
    block = _iota((STACK, 1), 0) >> 7
    col = jnp.full((STACK, 1), sink_ref[Q_PER_KV * j], _F32)
    for g in range(1, Q_PER_KV):
        col = jnp.where(block == g, sink_ref[Q_PER_KV * j + g], col)
    return col


def _fold_heads(t, j):
    t = t[:, :KV_WIDTH] + t[:, KV_WIDTH:]
    t = t + pltpu.roll(t, HEAD_DIM, 1)
    return jnp.where((_iota((1, KV_WIDTH), 1) >> 6) == j, t, 0.0)


def _attn_scores(q_stacked, kt, blk, sink):
    s = _dot(q_stacked, kt, _NT) * (HEAD_DIM ** -0.5)
    qi = _iota((STACK, 2 * BLOCK), 0) & (BLOCK - 1)
    kj = _iota((STACK, 2 * BLOCK), 1)
    rel = qi + BLOCK - kj
    valid = (rel >= 0) & (rel < BLOCK) & (blk * BLOCK - BLOCK + kj >= 0)
    s = jnp.where(valid, s, MASK_VALUE)
    m = jnp.maximum(jnp.max(s, axis=-1, keepdims=True), sink)
    p = jnp.exp(s - m)
    e_sink = jnp.exp(sink - m)
    den = jnp.sum(p, axis=-1, keepdims=True) + e_sink
    return p / den, e_sink / den


def _attn_specs():
    prev = lambda i: (jnp.maximum(i - 1, 0), 0)
    cur = lambda i: (i, 0)
    kv = [pl.BlockSpec((BLOCK, KV_WIDTH), prev), pl.BlockSpec((BLOCK, KV_WIDTH), cur)]
    return [pl.BlockSpec((BLOCK, ATTN_WIDTH), cur)] + kv + kv


def _attn_fwd(q, k, v, sinks, g_attn):
    L = q.shape[0]

    def body(q_ref, kp_ref, kc_ref, vp_ref, vc_ref, sink_ref, g_ref, o_ref, n_ref):
        blk = pl.program_id(0)
        kwin = jnp.concatenate([kp_ref[...], kc_ref[...]], axis=0)
        vwin = jnp.concatenate([vp_ref[...], vc_ref[...]], axis=0)
        halves = []
        for j in range(N_KV_HEADS):
            spread = _head_spread(j)
            kt = _dot(kwin, spread, _NN).astype(_BF16)
            vt = _dot(vwin, spread, _NN).astype(_BF16)
            qs = _stack_heads(q_ref[:, 256 * j:256 * (j + 1)])
            p, _ = _attn_scores(qs, kt, blk, _stacked_sinks(sink_ref, j))
            halves.append(_unstack_heads(_dot(p, vt, _NN)))
        o = jnp.concatenate(halves, axis=1)
        o_ref[...] = o
        n, _ = _rms_fwd(o, g_ref[...])
        n_ref[...] = n.astype(_BF16)

    cur = lambda i: (i, 0)
    return _call(body, (L // BLOCK,),
                 _attn_specs() + [pl.BlockSpec(memory_space=pltpu.SMEM), _whole((1, ATTN_WIDTH))],
                 [pl.BlockSpec((BLOCK, ATTN_WIDTH), cur)] * 2,
                 [_sds((L, ATTN_WIDTH), _F32), _sds((L, ATTN_WIDTH), _BF16)],
                 "attn_fwd")(q, k, k, v, v, sinks, g_attn)


def _attn_bwd(q, k, v, o, dn, sinks, g_attn):
    L = q.shape[0]

    def body(q_ref, kp_ref, kc_ref, vp_ref, vc_ref, o_ref, dn_ref, sink_ref, g_ref,
             dq_ref, dk_ref, dv_ref, dsink_ref, dg_ref):
        blk = pl.program_id(0)
        first = blk == 0

        @pl.when(first)
        def _():
            dk_ref[...] = jnp.zeros_like(dk_ref)
            dv_ref[...] = jnp.zeros_like(dv_ref)
            dsink_ref[...] = jnp.zeros_like(dsink_ref)

        o = o_ref[...]
        g = g_ref[...]
        _, r = _rms_fwd(o, g)
        do, dg = _rms_bwd(dn_ref[...], o, g, r)
        _accumulate(dg_ref, dg, first)
        kwin = jnp.concatenate([kp_ref[...], kc_ref[...]], axis=0)
        vwin = jnp.concatenate([vp_ref[...], vc_ref[...]], axis=0)
        lane = _iota((1, 128), 1)
        dsink = jnp.zeros((1, 128), _F32)
        dkwin = jnp.zeros((2 * BLOCK, KV_WIDTH), _F32)
        dvwin = jnp.zeros((2 * BLOCK, KV_WIDTH), _F32)
        dq_halves = []
        for j in range(N_KV_HEADS):
            spread = _head_spread(j)
            kt = _dot(kwin, spread, _NN).astype(_BF16)
            vt = _dot(vwin, spread, _NN).astype(_BF16)
            qs = _stack_heads(q_ref[:, 256 * j:256 * (j + 1)])
            dos = _stack_heads(do[:, 256 * j:256 * (j + 1)]).astype(_BF16)
            p, p_sink = _attn_scores(qs, kt, blk, _stacked_sinks(sink_ref, j))
            dp = _dot(dos, vt, _NT)
            delta = jnp.sum(p * dp, axis=-1, keepdims=True)
            ds = (p * (dp - delta) * (HEAD_DIM ** -0.5)).astype(_BF16)
            sink_term = p_sink * delta
            for g in range(Q_PER_KV):
                head_sum = jnp.sum(sink_term[BLOCK * g:BLOCK * (g + 1)], axis=0, keepdims=True)
                dsink = dsink - jnp.where(lane == Q_PER_KV * j + g, head_sum, 0.0)
            dvwin = dvwin + _fold_heads(_dot(p, dos, _TN), j)
            dkwin = dkwin + _fold_heads(_dot(ds, qs, _TN), j)
            dq_halves.append(_unstack_heads(_dot(ds, kt, _NN)))
        dq_ref[...] = jnp.concatenate(dq_halves, axis=1)
        dsink_ref[...] += dsink
        prev = pl.ds(pl.multiple_of(jnp.maximum(blk - 1, 0) * BLOCK, BLOCK), BLOCK)
        cur = pl.ds(pl.multiple_of(blk * BLOCK, BLOCK), BLOCK)
        dk_ref[prev, :] += dkwin[:BLOCK]
        dk_ref[cur, :] += dkwin[BLOCK:]
        dv_ref[prev, :] += dvwin[:BLOCK]
        dv_ref[cur, :] += dvwin[BLOCK:]

    cur = lambda i: (i, 0)
    blk_q = pl.BlockSpec((BLOCK, ATTN_WIDTH), cur)
    return _call(body, (L // BLOCK,),
                 _attn_specs() + [blk_q, blk_q, pl.BlockSpec(memory_space=pltpu.SMEM), _whole((1, ATTN_WIDTH))],
                 [blk_q, _whole((L, KV_WIDTH)), _whole((L, KV_WIDTH)), _whole((1, 128)), _whole((1, ATTN_WIDTH))],
                 [_sds((L, ATTN_WIDTH), _F32), _sds((L, KV_WIDTH), _F32), _sds((L, KV_WIDTH), _F32),
                  _sds((1, 128), _F32), _sds((1, ATTN_WIDTH), _F32)],
                 "attn_bwd")(q, k, k, v, v, o, dn, sinks, g_attn)


def _out_proj(n_ssm, n_attn, x, w_out, g_post_mix, g_pre_ffn):
    L = x.shape[0]
    tm = _chunk_tile(L)

    def body(ns_ref, na_ref, x_ref, w_ref, g1_ref, g2_ref, merged_ref, mo_ref, h1_ref, hn2_ref):
        merged = jnp.concatenate([ns_ref[...], na_ref[...]], axis=1)
        merged_ref[...] = merged
        mo = _dot(merged, w_ref[...], _NN)
        mo_ref[...] = mo
        n, _ = _rms_fwd(mo, g1_ref[...])
        h1 = x_ref[...] + n
        h1_ref[...] = h1
        hn2, _ = _rms_fwd(h1, g2_ref[...])
        hn2_ref[...] = hn2.astype(_BF16)

    row = _whole((1, D_MODEL))
    return _call(body, (L // tm,),
                 [_chunk_block(L, SSM_WIDTH), _rows(tm, ATTN_WIDTH), _rows(tm, D_MODEL), _whole((D_MODEL, D_MODEL)),
                  row, row],
                 [_rows(tm, D_MODEL)] * 4,
                 [_sds((L, D_MODEL), _BF16), _sds((L, D_MODEL), _F32), _sds((L, D_MODEL), _F32), _sds((L, D_MODEL), _BF16)],
                 "out_proj")(n_ssm, n_attn, x, w_out, g_post_mix, g_pre_ffn)


def _ffn(hn2, h1, target, w_gate_up, w_down, g_pre_ffn, g_post_ffn):
    L = h1.shape[0]
    tm = _tile(L)
    half = FFN_CHUNK

    def body(hn2_ref, h1_ref, tgt_ref, wgu_hbm, wd_hbm, g2_ref, g3_ref,
             act_ref, dgu_ref, dff_ref, dh1_ref, loss_ref, dg3_ref, dg2_ref,
             wgu, wd, gu, sem):
        first = pl.program_id(0) == 0

        @pl.when(first)
        def _():
            c1 = pltpu.make_async_copy(wgu_hbm, wgu, sem.at[0])
            c2 = pltpu.make_async_copy(wd_hbm, wd, sem.at[1])
            c1.start()
            c2.start()
            c1.wait()
            c2.wait()

        hn2 = hn2_ref[...]
        ff = jnp.zeros((tm, D_MODEL), _F32)
        for c in range(D_FF // half):
            gate = _dot(hn2, wgu[half * c:half * (c + 1), :], _NT)
            up = _dot(hn2, wgu[D_FF + half * c:D_FF + half * (c + 1), :], _NT)
            gu[:, half * c:half * (c + 1)] = gate
            gu[:, D_FF + half * c:D_FF + half * (c + 1)] = up
            act = gate * jax.nn.sigmoid(gate) * up
            act_ref[half * c:half * (c + 1), :] = act.T.astype(_BF16)
            ff = ff + _dot(act, wd[half * c:half * (c + 1), :], _NN)
        g3 = g3_ref[...]
        n, r = _rms_fwd(ff, g3)
        h1 = h1_ref[...]
        err = h1 + n - tgt_ref[...]
        loss = 0.5 * jnp.sum(jnp.mean(err * err, axis=-1, keepdims=True), axis=0, keepdims=True)
        _accumulate(loss_ref, jnp.broadcast_to(loss, (1, 128)), first)
        dh2 = err * (1.0 / D_MODEL)
        dff, dg3 = _rms_bwd(dh2, ff, g3, r)
        _accumulate(dg3_ref, dg3, first)
        dffb = dff.astype(_BF16)
        dff_ref[...] = dffb
        dhn2 = jnp.zeros((tm, D_MODEL), _F32)
        for c in range(D_FF // half):
            dact = _dot(dffb, wd[half * c:half * (c + 1), :], _NT)
            gate = gu[:, half * c:half * (c + 1)]
            up = gu[:, D_FF + half * c:D_FF + half * (c + 1)]
            sig = jax.nn.sigmoid(gate)
            silu = gate * sig
            dgate = dact * up * (sig + silu * (1.0 - sig))
            dup = dact * silu
            dgu_ref[half * c:half * (c + 1), :] = dgate.T.astype(_BF16)
            dgu_ref[D_FF + half * c:D_FF + half * (c + 1), :] = dup.T.astype(_BF16)
            dhn2 = dhn2 + _dot(dgate, wgu[half * c:half * (c + 1), :], _NN)
            dhn2 = dhn2 + _dot(dup, wgu[D_FF + half * c:D_FF + half * (c + 1), :], _NN)
        g2 = g2_ref[...]
        _, r2 = _rms_fwd(h1, g2)
        dh1, dg2 = _rms_bwd(dhn2, h1, g2, r2)
        _accumulate(dg2_ref, dg2, first)
        dh1_ref[...] = dh2 + dh1

    row = _whole((1, D_MODEL))
    anyspace = pl.BlockSpec(memory_space=pl.ANY)
    return _call(body, (L // tm,),
                 [_rows(tm, D_MODEL), _rows(tm, D_MODEL), _rows(tm, D_MODEL), anyspace, anyspace, row, row],
                 [pl.BlockSpec((D_FF, tm), lambda i: (0, i)), pl.BlockSpec((2 * D_FF, tm), lambda i: (0, i)),
                  _rows(tm, D_MODEL), _rows(tm, D_MODEL), _whole((1, 128)), row, row],
                 [_sds((D_FF, L), _BF16), _sds((2 * D_FF, L), _BF16), _sds((L, D_MODEL), _BF16),
                  _sds((L, D_MODEL), _F32), _sds((1, 128), _F32), _sds((1, D_MODEL), _F32), _sds((1, D_MODEL), _F32)],
                 "ffn",
                 scratch=[pltpu.VMEM((2 * D_FF, D_MODEL), _BF16), pltpu.VMEM((D_FF, D_MODEL), _BF16),
                          pltpu.VMEM((tm, 2 * D_FF), _F32), pltpu.SemaphoreType.DMA((2,))],
                 )(hn2, h1, target, w_gate_up, w_down, g_pre_ffn, g_post_ffn)


def _out_proj_bwd(dh1, mo, w_out, g_post_mix, tokens=()):
    L = dh1.shape[0]
    tm = _chunk_tile(L)

    def body(dh1_ref, mo_ref, w_ref, g_ref, dmo_ref, dns_ref, dna_ref, dg_ref):
        first = pl.program_id(0) == 0
        mo = mo_ref[...]
        g = g_ref[...]
        _, r = _rms_fwd(mo, g)
        dmo, dg = _rms_bwd(dh1_ref[...], mo, g, r)
        _accumulate(dg_ref, dg, first)
        dmob = dmo.astype(_BF16)
        dmo_ref[...] = dmob
        dmerged = _dot(dmob, w_ref[...], _NT)
        dns_ref[...] = dmerged[:, :SSM_WIDTH]
        dna_ref[...] = dmerged[:, SSM_WIDTH:]

    row = _whole((1, D_MODEL))
    return _call(body, (L // tm,),
                 [_rows(tm, D_MODEL), _rows(tm, D_MODEL), _whole((D_MODEL, D_MODEL)), row],
                 [_rows(tm, D_MODEL), _chunk_block(L, SSM_WIDTH), _rows(tm, ATTN_WIDTH), row],
                 [_sds((L, D_MODEL), _BF16), _sds(_chunk_shape(L, SSM_WIDTH), _F32), _sds((L, ATTN_WIDTH), _F32),
                  _sds((1, D_MODEL), _F32)],
                 "out_proj_bwd", tokens=tokens)(dh1, mo, w_out, g_post_mix)


def _in_proj_bwd(du, dq, dk, dv, cos_t, sin_t, x, dh1, g_pre_mix, w_in, tokens=()):
    L = x.shape[0]
    tm = _chunk_tile(L)

    def body(du_ref, dq_ref, dk_ref, dv_ref, cos_ref, sin_ref, x_ref, dh1_ref, g_ref, w_ref,
             dproj_ref, dx_ref, dg_ref):
        first = pl.program_id(0) == 0
        cos_v, sin_v = cos_ref[...], sin_ref[...]
        dproj = jnp.concatenate([du_ref[...], _rope_transpose(dq_ref[...], cos_v, sin_v),
                                 _rope_transpose(dk_ref[...], cos_v, sin_v), dv_ref[...]], axis=1).astype(_BF16)
        dproj_ref[...] = dproj
        dhn = _dot(dproj, w_ref[...], _NN)
        x = x_ref[...]
        g = g_ref[...]
        _, r = _rms_fwd(x, g)
        dx, dg = _rms_bwd(dhn, x, g, r)
        _accumulate(dg_ref, dg, first)
        dx_ref[...] = dh1_ref[...] + dx

    row = _whole((1, D_MODEL))
    return _call(body, (L // tm,),
                 [_chunk_block(L, SSM_WIDTH), _rows(tm, ATTN_WIDTH), _rows(tm, KV_WIDTH), _rows(tm, KV_WIDTH),
                  _rows(tm, KV_WIDTH), _rows(tm, KV_WIDTH), _rows(tm, D_MODEL), _rows(tm, D_MODEL), row,
                  _whole((IN_WIDTH, D_MODEL))],
                 [_rows(tm, IN_WIDTH), _rows(tm, D_MODEL), row],
                 [_sds((L, IN_WIDTH), _BF16), _sds((L, D_MODEL), _F32), _sds((1, D_MODEL), _F32)],
                 "in_proj_bwd", tokens=tokens)(du, dq, dk, dv, cos_t, sin_t, x, dh1, g_pre_mix, w_in)


def _matmul_nn(a, b, out_dtype, name):
    M, K = a.shape
    N = b.shape[1]
    tm = next(t for t in (512, 256, 128) if M % t == 0)
    tn = N if N <= D_MODEL else next(t for t in (512, 256, 128) if N % t == 0)

    def body(a_ref, b_ref, o_ref):
        o_ref[...] = _dot(a_ref[...], b_ref[...], _NN).astype(out_dtype)

    params = pltpu.CompilerParams(dimension_semantics=("arbitrary", "arbitrary"), vmem_limit_bytes=VMEM_LIMIT)
    return pl.pallas_call(body, grid=(M // tm, N // tn),
                          in_specs=[pl.BlockSpec((tm, K), lambda i, j: (i, 0)),
                                    pl.BlockSpec((K, tn), lambda i, j: (0, j))],
                          out_specs=pl.BlockSpec((tm, tn), lambda i, j: (i, j)),
                          out_shape=_sds((M, N), out_dtype), compiler_params=params, name=name)(a, b)


def _matmul_tn(a, b, out_dtype, name, scale=1.0):
    K, M = a.shape
    N = b.shape[1]
    tm = next(t for t in (512, 256, 128) if M % t == 0)
    tn = N if N <= D_MODEL else next(t for t in (512, 256, 128) if N % t == 0)

    def body(a_ref, b_ref, o_ref):
        acc = _dot(a_ref[...], b_ref[...], _TN)
        o_ref[...] = (acc if scale == 1.0 else acc * scale).astype(out_dtype)

    params = pltpu.CompilerParams(dimension_semantics=("arbitrary", "arbitrary"), vmem_limit_bytes=VMEM_LIMIT)
    return pl.pallas_call(body, grid=(M // tm, N // tn),
                          in_specs=[pl.BlockSpec((K, tm), lambda i, j: (0, i)),
                                    pl.BlockSpec((K, tn), lambda i, j: (0, j))],
                          out_specs=pl.BlockSpec((tm, tn), lambda i, j: (i, j)),
                          out_shape=_sds((M, N), out_dtype), compiler_params=params, name=name)(a, b)


def _local_step(x, pos, target, p, fetch, publish, progress):
    L = x.shape[0]
    T = L // SCAN_CHUNKS
    cos_t, sin_t = _rope_tables(pos.reshape(L, 1))
    w_in, = fetch(("w_in",), None)
    hn, u, q, k, v = _in_proj(x, p["g_pre_mix"], w_in, cos_t, sin_t)

    ssm = {n: _to_2d(n, p[n]) for n in ("ssm_lambda_re", "ssm_lambda_im", "ssm_log_dt", "ssm_b_re", "ssm_b_im",
                                        "ssm_c_re", "ssm_c_im")}
    d_row = p["ssm_d"].reshape(1, SSM_WIDTH)
    a_re, a_im, bt_re, bt_im, ct_re, ct_im = _ssm_prep(
        ssm["ssm_lambda_re"], ssm["ssm_lambda_im"], ssm["ssm_log_dt"], ssm["ssm_b_re"], ssm["ssm_b_im"],
        ssm["ssm_c_re"], ssm["ssm_c_im"])

    u_c = u.reshape(L, SSM_WIDTH)
    cx = _ssm_core_fwd(u_c, bt_re, bt_im, ct_re, ct_im, a_re, a_im)
    w_glu, = fetch(("w_glu",), cx)
    y, z, n_ssm_c = _ssm_out(cx, u_c, d_row, w_glu, p["b_glu"], p["g_ssm_out"])
    n_ssm = n_ssm_c.reshape(_chunk_shape(L, SSM_WIDTH))

    sinks = p["attn_sinks"].reshape(N_Q_HEADS)
    o, n_attn = _attn_fwd(q, k, v, sinks, p["g_attn_out"])
    w_out, = fetch(("w_out",), n_attn)
    merged, mo, h1, hn2 = _out_proj(n_ssm, n_attn, x, w_out, p["g_post_mix"], p["g_pre_ffn"])
    w_gate_up, w_down = fetch(("w_gate_up", "w_down"), hn2)
    act_t, dgu_t, dff, dh1, loss, dg_post_ffn, dg_pre_ffn = _ffn(
        hn2, h1, target, w_gate_up, w_down, p["g_pre_ffn"], p["g_post_ffn"])
    grads = {"g_post_ffn": dg_post_ffn, "g_pre_ffn": dg_pre_ffn}
    tokens = publish({"w_down": _matmul_nn(act_t, dff, _BF16, "grad_w_down"),
                      "w_gate_up": _matmul_nn(dgu_t, hn2, _BF16, "grad_w_gate_up")})

    dmo, dn_ssm, dn_attn, grads["g_post_mix"] = _out_proj_bwd(dh1, mo, w_out, p["g_post_mix"], tokens)
    grad_w_out = _matmul_tn(merged, dmo, _BF16, "grad_w_out")

    dq, dk, dv, dsink, grads["g_attn_out"] = _attn_bwd(q, k, v, o, dn_attn, sinks, p["g_attn_out"])
    grads["attn_sinks"] = dsink

    gy, dz, dy, dud, grads["g_ssm_out"], grads["b_glu"], dd = _ssm_out_bwd(
        dn_ssm.reshape(L, SSM_WIDTH), y, z, u_c, d_row, w_glu, p["g_ssm_out"])
    tokens = progress(dy)
    tokens += publish({"w_out": grad_w_out, "w_glu": _matmul_tn(dz, gy, _BF16, "grad_w_glu")})
    du_c, dct_re, dct_im, dbt_re, dbt_im, da_re, da_im = _ssm_core_bwd(
        u_c, dy, dud, bt_re, bt_im, ct_re, ct_im, a_re, a_im, tokens)
    ssm_pack = _ssm_param_bwd(
        da_re, da_im, dbt_re, dbt_im, dct_re, dct_im,
        ssm["ssm_lambda_re"], ssm["ssm_lambda_im"], ssm["ssm_log_dt"], ssm["ssm_b_re"], ssm["ssm_b_im"],
        dd.reshape(SSM_GROUPS, SSM_GROUP))
    grads.update(ssm_pack=ssm_pack, loss=loss)
    publish(grads)

    du = du_c.reshape(_chunk_shape(L, SSM_WIDTH))
    dproj, grad_x, g_pre_mix = _in_proj_bwd(du, dq, dk, dv, cos_t, sin_t, x, dh1, p["g_pre_mix"], w_in, [ssm_pack])
    publish({"g_pre_mix": g_pre_mix, "w_in": _matmul_tn(dproj, hn, _BF16, "grad_w_in")})
    return grad_x


_MESH = pl.DeviceIdType.MESH
_PEERS = N_DEV - 1


def _mesh_pos():
    return lax.axis_index("x"), lax.axis_index("y"), lax.axis_index("c")


def _dev_index(px, py, pc):
    return 4 * px + 2 * py + pc


def _all_gather(shards, out_dtype, name):
    n = len(shards)

    def body(*refs):
        ins, outs, stages = refs[:n], refs[n:2 * n], refs[2 * n:3 * n]
        send_sems, recv_sems, local_sems = refs[3 * n:]
        x, y, c = _mesh_pos()
        me, sibling = (x, y, c), (x, y, 1 - c)
        chips = [(1 - x, y), (x, 1 - y), (1 - x, 1 - y)]

        def copy(w, k, block, to, src=None):
            slot = outs[w].at[_dev_index(*block)]
            return pltpu.make_async_remote_copy(
                src_ref=slot if src is None else src, dst_ref=slot,
                send_sem=send_sems.at[_PEERS * w + k], recv_sem=recv_sems.at[_PEERS * w + k],
                device_id=to, device_id_type=_MESH)

        for w in range(n):
            stages[w][...] = ins[w][...].astype(out_dtype)
        mine, first, passed = [], [], []
        for w in range(n):
            cp = pltpu.make_async_copy(stages[w], outs[w].at[_dev_index(*me)], local_sems.at[w])
            cp.start()
            mine.append(cp)
            sends = [copy(w, 0, me, sibling, src=stages[w])]
            sends += [copy(w, 1 + j, me, (*chip, c), src=stages[w]) for j, chip in enumerate(chips)]
            for cp in sends:
                cp.start()
            first += sends
        for w in range(n):
            for j, chip in enumerate(chips):
                copy(w, 1 + j, (*chip, c), me).wait_recv()
                cp = copy(w, 4 + j, (*chip, c), sibling)
                cp.start()
                passed.append(cp)
        for w in range(n):
            copy(w, 0, sibling, me).wait_recv()
            for j, chip in enumerate(chips):
                copy(w, 4 + j, (*chip, 1 - c), me).wait_recv()
        for cp in first + passed:
            cp.wait_send()
        for cp in mine:
            cp.wait()

    return pl.pallas_call(
        body, name=name,
        out_shape=[_sds((N_DEV,) + s.shape, out_dtype) for s in shards],
        in_specs=[pl.BlockSpec(memory_space=pltpu.VMEM)] * n,
        out_specs=[pl.BlockSpec(memory_space=pl.ANY)] * n,
        scratch_shapes=[pltpu.VMEM(s.shape, out_dtype) for s in shards]
        + [pltpu.SemaphoreType.DMA((_PEERS * n,)), pltpu.SemaphoreType.DMA((_PEERS * n,)),
           pltpu.SemaphoreType.DMA((n,))],
        compiler_params=pltpu.CompilerParams(vmem_limit_bytes=VMEM_LIMIT),
    )(*shards)


_HBM_SPEC = pl.BlockSpec(memory_space=pltpu.HBM)
_SEM_SPEC = pl.BlockSpec(memory_space=pltpu.SEMAPHORE)
_DATAFLOW = pltpu.SideEffectType.DATAFLOW_SIDE_EFFECTING


def _peer(x, y, c, r):
    return (x ^ ((r >> 2) & 1), y ^ ((r >> 1) & 1), c ^ (r & 1))


def _hbm(a):
    return pltpu.with_memory_space_constraint(a, pltpu.HBM)


def _send_start(sources, blocked, name):
    n = len(sources)
    lands = [lax.empty((N_DEV,) + (s.shape[1:] if blocked else s.shape), s.dtype) for s in sources]

    def body(*refs):
        srcs, zones = refs[:n], refs[n:2 * n]
        send_sems, recv_sems = refs[2 * n:3 * n], refs[3 * n:4 * n]
        token, local_sems = refs[6 * n], refs[6 * n + 1]
        x, y, c = _mesh_pos()
        me = _dev_index(x, y, c)
        local = []
        for w in range(n):
            cp = pltpu.make_async_copy(srcs[w].at[me] if blocked else srcs[w], zones[w].at[me], local_sems.at[w])
            cp.start()
            local.append(cp)
            for r in range(1, N_DEV):
                peer = _peer(x, y, c, r)
                pltpu.make_async_remote_copy(
                    src_ref=srcs[w].at[_dev_index(*peer)] if blocked else srcs[w], dst_ref=zones[w].at[me],
                    send_sem=send_sems[w].at[r - 1], recv_sem=recv_sems[w].at[r - 1],
                    device_id=peer, device_id_type=_MESH).start()
        for cp in local:
            cp.wait()
        token[...] = jnp.zeros_like(token)

    sems = [pltpu.SemaphoreType.DMA((_PEERS,))] * (2 * n)
    out = pl.pallas_call(
        body, name=name,
        out_shape=sems + [pltpu.HBM(a.shape, a.dtype) for a in list(sources) + lands] + [_sds((8, 128), _F32)],
        in_specs=[_HBM_SPEC] * (2 * n),
        out_specs=[_SEM_SPEC] * (2 * n) + [_HBM_SPEC] * (2 * n) + [pl.BlockSpec(memory_space=pltpu.VMEM)],
        input_output_aliases={i: 2 * n + i for i in range(2 * n)},
        scratch_shapes=[pltpu.SemaphoreType.DMA((n,))],
        compiler_params=pltpu.CompilerParams(has_side_effects=_DATAFLOW),
    )(*[_hbm(a) for a in sources], *[_hbm(a) for a in lands])
    return out[:n], out[n:2 * n], out[2 * n:3 * n], out[3 * n:4 * n], out[4 * n]


def _send_wait(send_sems, recv_sems, sources, lands, after, blocked, name):
    n = len(sources)

    def body(*refs):
        srcs, zones = refs[:n], refs[n:2 * n]
        sends, recvs = refs[2 * n:3 * n], refs[3 * n:4 * n]
        x, y, c = _mesh_pos()
        for w in range(n):
            for r in range(1, N_DEV):
                peer = _peer(x, y, c, r)
                idx = _dev_index(*peer)
                cp = pltpu.make_async_remote_copy(
                    src_ref=srcs[w].at[idx] if blocked else srcs[w], dst_ref=zones[w].at[idx],
                    send_sem=sends[w].at[r - 1], recv_sem=recvs[w].at[r - 1],
                    device_id=peer, device_id_type=_MESH)
                cp.wait_send()
                cp.wait_recv()

    out = pl.pallas_call(
        body, name=name,
        out_shape=[pltpu.HBM(a.shape, a.dtype) for a in list(sources) + list(lands)],
        in_specs=[_HBM_SPEC] * (2 * n) + [_SEM_SPEC] * (2 * n) + [pl.BlockSpec(memory_space=pl.ANY)],
        out_specs=[_HBM_SPEC] * (2 * n),
        input_output_aliases={i: i for i in range(2 * n)},
        compiler_params=pltpu.CompilerParams(has_side_effects=_DATAFLOW),
    )(*sources, *lands, *send_sems, *recv_sems, after)
    return out[n:]


def _sequencer_exchange(sources, blocked, name, collective_id):
    n = len(sources)
    flags = blocked

    def body(*refs):
        srcs, zones = refs[:n], refs[n:2 * n]
        send_sems, recv_sems, local_sems = refs[2 * n:]
        x, y, c = _mesh_pos()
        me = _dev_index(x, y, c)
        barrier = pltpu.get_barrier_semaphore()
        for r in range(1, N_DEV):
            pl.semaphore_signal(barrier, inc=1, device_id=_peer(x, y, c, r), device_id_type=_MESH)
        pl.semaphore_wait(barrier, _PEERS)
        local, sends, recvs = [], [], []
        for w in range(n):
            cp = pltpu.make_async_copy(srcs[w].at[me] if flags[w] else srcs[w], zones[w].at[me], local_sems.at[w])
            cp.start()
            local.append(cp)
            for r in range(1, N_DEV):
                peer = _peer(x, y, c, r)
                idx = _dev_index(*peer)
                k = _PEERS * w + r - 1
                src = srcs[w].at[idx] if flags[w] else srcs[w]
                send = pltpu.make_async_remote_copy(
                    src_ref=src, dst_ref=zones[w].at[me], send_sem=send_sems.at[k], recv_sem=recv_sems.at[k],
                    device_id=peer, device_id_type=_MESH)
                send.start()
                sends.append(send)
                recvs.append(pltpu.make_async_remote_copy(
                    src_ref=src, dst_ref=zones[w].at[idx], send_sem=send_sems.at[k], recv_sem=recv_sems.at[k],
                    device_id=peer, device_id_type=_MESH))
        for cp in recvs:
            cp.wait_recv()
        for cp in sends:
            cp.wait_send()
        for cp in local:
            cp.wait()

    return pl.kernel(
        body, name=name,
        out_type=[_sds((N_DEV,) + (s.shape[1:] if f else s.shape), s.dtype) for s, f in zip(sources, flags)],
        mesh=plsc.ScalarSubcoreMesh(axis_name="sequencer", num_cores=1),
        scratch_types=[pltpu.SemaphoreType.DMA((_PEERS * n,)), pltpu.SemaphoreType.DMA((_PEERS * n,)),
                       pltpu.SemaphoreType.DMA((n,))],
        compiler_params=pltpu.CompilerParams(collective_id=collective_id),
    )(*sources)


def _sequencer_gather(shards, name, collective_id):
    n = len(shards)
    fan = 4

    def body(*refs):
        srcs, zones = refs[:n], refs[n:2 * n]
        send_sems, recv_sems, local_sems = refs[2 * n:]
        x, y, c = _mesh_pos()
        me, sibling = (x, y, c), (x, y, 1 - c)
        chips = [(1 - x, y), (x, 1 - y), (1 - x, 1 - y)]
        barrier = pltpu.get_barrier_semaphore()
        for peer in [sibling] + [(*chip, c) for chip in chips]:
            pl.semaphore_signal(barrier, inc=1, device_id=peer, device_id_type=_MESH)
        pl.semaphore_wait(barrier, fan)

        def copy(w, k, block, to, src=None):
            slot = zones[w].at[_dev_index(*block)]
            return pltpu.make_async_remote_copy(
                src_ref=slot if src is None else src, dst_ref=slot,
                send_sem=send_sems.at[_PEERS * w + k], recv_sem=recv_sems.at[_PEERS * w + k],
                device_id=to, device_id_type=_MESH)

        mine, first, passed = [], [], []
        for w in range(n):
            cp = pltpu.make_async_copy(srcs[w], zones[w].at[_dev_index(*me)], local_sems.at[w])
            cp.start()
            mine.append(cp)
            sends = [copy(w, 0, me, sibling, src=srcs[w])]
            sends += [copy(w, 1 + j, me, (*chip, c), src=srcs[w]) for j, chip in enumerate(chips)]
            for cp in sends:
                cp.start()
            first += sends
        for w in range(n):
            for j, chip in enumerate(chips):
                copy(w, 1 + j, (*chip, c), me).wait_recv()
                cp = copy(w, fan + j, (*chip, c), sibling)
                cp.start()
                passed.append(cp)
        for w in range(n):
            copy(w, 0, sibling, me).wait_recv()
            for j, chip in enumerate(chips):
                copy(w, fan + j, (*chip, 1 - c), me).wait_recv()
        for cp in first + passed:
            cp.wait_send()
        for cp in mine:
            cp.wait()

    return pl.kernel(
        body, name=name, out_type=[_sds((N_DEV,) + s.shape, s.dtype) for s in shards],
        mesh=plsc.ScalarSubcoreMesh(axis_name="sequencer", num_cores=1),
        scratch_types=[pltpu.SemaphoreType.DMA((_PEERS * n,)), pltpu.SemaphoreType.DMA((_PEERS * n,)),
                       pltpu.SemaphoreType.DMA((n,))],
        compiler_params=pltpu.CompilerParams(collective_id=collective_id),
    )(*shards)


N_CHIPS = N_DEV // 2


def _sequencer_pair_exchange(sources, name, collective_id):
    n = len(sources)

    def body(*refs):
        srcs, zones = refs[:n], refs[n:2 * n]
        send_sems, recv_sems = refs[2 * n:]
        x, y, c = _mesh_pos()
        sibling = (x, y, 1 - c)
        barrier = pltpu.get_barrier_semaphore()
        pl.semaphore_signal(barrier, inc=1, device_id=sibling, device_id_type=_MESH)
        pl.semaphore_wait(barrier, 1)
        copies = []
        for w in range(n):
            for j in range(N_CHIPS):
                k = N_CHIPS * w + j
                cp = pltpu.make_async_remote_copy(
                    src_ref=srcs[w].at[2 * j + 1 - c], dst_ref=zones[w].at[j],
                    send_sem=send_sems.at[k], recv_sem=recv_sems.at[k], device_id=sibling, device_id_type=_MESH)
                cp.start()
                copies.append(cp)
        for cp in copies:
            cp.wait_recv()
        for cp in copies:
            cp.wait_send()

    return pl.kernel(
        body, name=name, out_type=[_sds((N_CHIPS,) + s.shape[1:], s.dtype) for s in sources],
        mesh=plsc.ScalarSubcoreMesh(axis_name="sequencer", num_cores=1),
        scratch_types=[pltpu.SemaphoreType.DMA((N_CHIPS * n,)), pltpu.SemaphoreType.DMA((N_CHIPS * n,))],
        compiler_params=pltpu.CompilerParams(collective_id=collective_id),
    )(*sources)


def _pair_sum(source, received, core, name, tokens=()):
    _, rows, cols = source.shape
    tr = _row_tile(rows)

    def body(core_ref, s_ref, r_ref, o_ref):
        c = core_ref[0]
        for j in range(N_CHIPS):
            o_ref[j] = (s_ref[2 * j + c].astype(_F32) + r_ref[j].astype(_F32)).astype(o_ref.dtype)

    return _call(body, (rows // tr,),
                 [pl.BlockSpec(memory_space=pltpu.SMEM), pl.BlockSpec((N_DEV, tr, cols), lambda i: (0, i, 0)),
                  pl.BlockSpec((N_CHIPS, tr, cols), lambda i: (0, i, 0))],
                 pl.BlockSpec((N_CHIPS, tr, cols), lambda i: (0, i, 0)),
                 _sds((N_CHIPS, rows, cols), source.dtype), name, tokens=tokens)(core, source, received)


def _sequencer_chip_exchange(partials, name, collective_id):
    n = len(partials)
    others = N_CHIPS - 1

    def body(*refs):
        srcs, zones = refs[:n], refs[n:2 * n]
        send_sems, recv_sems, local_sems = refs[2 * n:]
        x, y, c = _mesh_pos()
        mine = 2 * x + y
        peers = [(x ^ (r >> 1), y ^ (r & 1), c) for r in range(1, N_CHIPS)]
        barrier = pltpu.get_barrier_semaphore()
        for peer in peers:
            pl.semaphore_signal(barrier, inc=1, device_id=peer, device_id_type=_MESH)
        pl.semaphore_wait(barrier, others)
        local, sends, recvs = [], [], []
        for w in range(n):
            cp = pltpu.make_async_copy(srcs[w].at[mine], zones[w].at[mine], local_sems.at[w])
            cp.start()
            local.append(cp)
            for r, peer in enumerate(peers):
                theirs = 2 * peer[0] + peer[1]
                k = others * w + r
                send = pltpu.make_async_remote_copy(
                    src_ref=srcs[w].at[theirs], dst_ref=zones[w].at[mine],
                    send_sem=send_sems.at[k], recv_sem=recv_sems.at[k], device_id=peer, device_id_type=_MESH)
                send.start()
                sends.append(send)
                recvs.append(pltpu.make_async_remote_copy(
                    src_ref=srcs[w].at[theirs], dst_ref=zones[w].at[theirs],
                    send_sem=send_sems.at[k], recv_sem=recv_sems.at[k], device_id=peer, device_id_type=_MESH))
        for cp in recvs:
            cp.wait_recv()
        for cp in sends:
            cp.wait_send()
        for cp in local:
            cp.wait()

    return pl.kernel(
        body, name=name, out_type=[_sds(s.shape, s.dtype) for s in partials],
        mesh=plsc.ScalarSubcoreMesh(axis_name="sequencer", num_cores=1),
        scratch_types=[pltpu.SemaphoreType.DMA((others * n,)), pltpu.SemaphoreType.DMA((others * n,)),
                       pltpu.SemaphoreType.DMA((n,))],
        compiler_params=pltpu.CompilerParams(collective_id=collective_id),
    )(*partials)


def _row_tile(rows):
    return next(t for t in range(min(rows, 256), 0, -16) if rows % t == 0)


def _sum_parts(parts, name, tokens=()):
    _, rows, cols = parts.shape
    tr = _row_tile(rows)

    def body(p_ref, g_ref):
        g = p_ref[0].astype(_F32)
        for s in range(1, N_DEV):
            g = g + p_ref[s].astype(_F32)
        g_ref[...] = g

    return _call(body, (rows // tr,), [pl.BlockSpec((N_DEV, tr, cols), lambda i: (0, i, 0))],
                 _rows(tr, cols), _sds((rows, cols), _F32), name, tokens=tokens)(parts)


def _adam_update(g, w, m, v):
    new_m = ADAM_B1 * m + (1.0 - ADAM_B1) * g
    new_v = ADAM_B2 * v + (1.0 - ADAM_B2) * (g * g)
    m_hat = new_m / (1.0 - ADAM_B1 ** ADAM_STEP)
    v_hat = new_v / (1.0 - ADAM_B2 ** ADAM_STEP)
    return -ADAM_LR * (m_hat / (jnp.sqrt(v_hat) + ADAM_EPS) + ADAM_WD * w), new_m, new_v


def _adamw_small(parts, items, sums, name, tokens=()):
    n_p, n_i = len(parts), len(items)

    def body(*refs):
        p_refs, state, outs = refs[:n_p], refs[n_p:n_p + 3 * n_i], refs[n_p + 3 * n_i:]

        def total(part, rows, cols):
            shift = cols.start % _LANES
            window = slice(cols.start - shift, cols.start - shift + _LANES) if shift else cols
            n_rows = rows.stop - rows.start
            narrow = p_refs[part].dtype.itemsize < 4 and n_rows % _PACK_TILE
            tile = slice(rows.start, rows.start + _PACK_TILE) if narrow else rows
            g = p_refs[part][0, tile, window].astype(_F32)
            for s in range(1, N_DEV):
                g = g + p_refs[part][s, tile, window].astype(_F32)
            g = g[:n_rows] if narrow else g
            return pltpu.roll(g, _LANES - shift, 1)[:, :cols.stop - cols.start] if shift else g

        for i, (part, rows, cols, _, _, _) in enumerate(items):
            g = total(part, rows, cols)
            w_ref, m_ref, v_ref = state[3 * i:3 * i + 3]
            delta, new_m, new_v = _adam_update(g, w_ref[...], m_ref[...], v_ref[...])
            outs[4 * i][...] = g
            outs[4 * i + 1][...] = delta
            outs[4 * i + 2][...] = new_m
            outs[4 * i + 3][...] = new_v
        for j, (part, rows, cols) in enumerate(sums):
            outs[4 * n_i + j][...] = total(part, rows, cols)

    ins = list(parts) + [a for item in items for a in item[3:]]
    out_shapes = [item[3].shape for item in items for _ in range(4)]
    out_shapes += [(rows.stop - rows.start, cols.stop - cols.start) for _, rows, cols in sums]
    out = _call(body, (1,), [_whole(a.shape) for a in ins], [_whole(s) for s in out_shapes],
                [_sds(s, _F32) for s in out_shapes], name, tokens=tokens)(*ins)
    return [out[4 * i:4 * i + 4] for i in range(n_i)], out[4 * n_i:]


def _adamw(parts, w, m, v, name, tokens=()):
    rows, cols = w.shape
    tr = _row_tile(rows)
    n_parts = parts.shape[0]

    def body(p_ref, w_ref, m_ref, v_ref, g_ref, d_ref, nm_ref, nv_ref):
        g = p_ref[0].astype(_F32)
        for s in range(1, n_parts):
            g = g + p_ref[s].astype(_F32)
        new_m = ADAM_B1 * m_ref[...] + (1.0 - ADAM_B1) * g
        new_v = ADAM_B2 * v_ref[...] + (1.0 - ADAM_B2) * (g * g)
        m_hat = new_m / (1.0 - ADAM_B1 ** ADAM_STEP)
        v_hat = new_v / (1.0 - ADAM_B2 ** ADAM_STEP)
        g_ref[...] = g
        d_ref[...] = -ADAM_LR * (m_hat / (jnp.sqrt(v_hat) + ADAM_EPS) + ADAM_WD * w_ref[...])
        nm_ref[...] = new_m
        nv_ref[...] = new_v

    blk = _rows(tr, cols)
    return _call(body, (rows // tr,),
                 [pl.BlockSpec((n_parts, tr, cols), lambda i: (0, i, 0)), blk, blk, blk],
                 [blk] * 4, [_sds((rows, cols), _F32)] * 4, name, tokens=tokens)(parts, w, m, v)


_SMALL = ("g_pre_mix", "ssm_lambda_re", "ssm_lambda_im", "ssm_log_dt", "ssm_b_re", "ssm_b_im",
          "ssm_c_re", "ssm_c_im", "ssm_d", "b_glu", "attn_sinks", "g_ssm_out", "g_attn_out",
          "g_post_mix", "g_pre_ffn", "g_post_ffn")
_BIG = ("w_in", "w_glu", "w_out", "w_gate_up", "w_down")
_WEIGHTS = ("g_pre_mix", "w_in", "ssm_lambda_re", "ssm_lambda_im", "ssm_log_dt", "ssm_b_re", "ssm_b_im",
            "ssm_c_re", "ssm_c_im", "ssm_d", "w_glu", "b_glu", "attn_sinks", "g_ssm_out", "g_attn_out",
            "w_out", "g_post_mix", "g_pre_ffn", "w_gate_up", "w_down", "g_post_ffn")
_LANES = 128


_SHAPE_2D = {
    "g_pre_mix": (1, D_MODEL), "ssm_lambda_re": (SSM_GROUPS, SSM_STATE), "ssm_lambda_im": (SSM_GROUPS, SSM_STATE),
    "ssm_log_dt": (1, SSM_GROUPS), "ssm_b_re": (SSM_WIDTH, SSM_STATE), "ssm_b_im": (SSM_WIDTH, SSM_STATE),
    "ssm_c_re": (SSM_WIDTH, SSM_STATE), "ssm_c_im": (SSM_WIDTH, SSM_STATE), "ssm_d": (SSM_GROUPS, SSM_GROUP),
    "b_glu": (1, 2 * SSM_WIDTH), "attn_sinks": (1, N_Q_HEADS), "g_ssm_out": (1, SSM_WIDTH),
    "g_attn_out": (1, ATTN_WIDTH), "g_post_mix": (1, D_MODEL), "g_pre_ffn": (1, D_MODEL), "g_post_ffn": (1, D_MODEL)}
_ROW_WIDTH = {"g_pre_mix": D_MODEL, "b_glu": 2 * SSM_WIDTH, "attn_sinks": _LANES, "g_ssm_out": SSM_WIDTH,
              "g_attn_out": ATTN_WIDTH, "g_post_mix": D_MODEL, "g_pre_ffn": D_MODEL, "g_post_ffn": D_MODEL,
              "loss": _LANES}
_DENSE = ()
_PER_GROUP_TRANSPOSED = ("ssm_b_re", "ssm_b_im")


def _to_2d(name, a):
    if name in _PER_GROUP_TRANSPOSED:
        a = a.reshape(SSM_GROUPS, SSM_STATE, SSM_GROUP).transpose(0, 2, 1)
    return a.reshape(_SHAPE_2D[name])


def _from_2d(name, a, shape):
    if name in _PER_GROUP_TRANSPOSED:
        a = a.reshape(SSM_GROUPS, SSM_GROUP, SSM_STATE).transpose(0, 2, 1)
    return a.reshape(shape)


def _row_slots(names):
    slots, row, col = {}, 0, 0
    for n in names:
        width = _ROW_WIDTH[n]
        if col + width > D_MODEL:
            row, col = row + 1, 0
        slots[n] = (row, col, width)
        col += width
    return slots


def _stack_rows(named, slots):
    n_rows = -(-(max(r for r, _, _ in slots.values()) + 1) // 8) * 8
    lines = []
    for r in range(n_rows):
        pieces = [named[n] for n, (row, _, _) in slots.items() if row == r]
        used = sum(p.shape[1] for p in pieces)
        if used < D_MODEL:
            pieces.append(jnp.zeros((1, D_MODEL - used), _F32))
        lines.append(jnp.concatenate(pieces, axis=1) if len(pieces) > 1 else pieces[0])
    return jnp.concatenate(lines, axis=0)


def kernel(x, positions, g_pre_mix, w_in, ssm_lambda_re, ssm_lambda_im, ssm_log_dt, ssm_b_re, ssm_b_im, ssm_c_re, ssm_c_im, ssm_d, w_glu, b_glu, attn_sinks, g_ssm_out, g_attn_out, w_out, g_post_mix, g_pre_ffn, w_gate_up, w_down, g_post_ffn, loss_target, m_g_pre_mix, m_w_in, m_ssm_lambda_re, m_ssm_lambda_im, m_ssm_log_dt, m_ssm_b_re, m_ssm_b_im, m_ssm_c_re, m_ssm_c_im, m_ssm_d, m_w_glu, m_b_glu, m_attn_sinks, m_g_ssm_out, m_g_attn_out, m_w_out, m_g_post_mix, m_g_pre_ffn, m_w_gate_up, m_w_down, m_g_post_ffn, v_g_pre_mix, v_w_in, v_ssm_lambda_re, v_ssm_lambda_im, v_ssm_log_dt, v_ssm_b_re, v_ssm_b_im, v_ssm_c_re, v_ssm_c_im, v_ssm_d, v_w_glu, v_b_glu, v_attn_sinks, v_g_ssm_out, v_g_attn_out, v_w_out, v_g_post_mix, v_g_pre_ffn, v_w_gate_up, v_w_down, v_g_post_ffn):
    w = dict(g_pre_mix=g_pre_mix, w_in=w_in, ssm_lambda_re=ssm_lambda_re, ssm_lambda_im=ssm_lambda_im,
             ssm_log_dt=ssm_log_dt, ssm_b_re=ssm_b_re, ssm_b_im=ssm_b_im, ssm_c_re=ssm_c_re, ssm_c_im=ssm_c_im,
             ssm_d=ssm_d, w_glu=w_glu, b_glu=b_glu, attn_sinks=attn_sinks, g_ssm_out=g_ssm_out,
             g_attn_out=g_attn_out, w_out=w_out, g_post_mix=g_post_mix, g_pre_ffn=g_pre_ffn,
             w_gate_up=w_gate_up, w_down=w_down, g_post_ffn=g_post_ffn)
    m = dict(g_pre_mix=m_g_pre_mix, w_in=m_w_in, ssm_lambda_re=m_ssm_lambda_re, ssm_lambda_im=m_ssm_lambda_im,
             ssm_log_dt=m_ssm_log_dt, ssm_b_re=m_ssm_b_re, ssm_b_im=m_ssm_b_im, ssm_c_re=m_ssm_c_re,
             ssm_c_im=m_ssm_c_im, ssm_d=m_ssm_d, w_glu=m_w_glu, b_glu=m_b_glu, attn_sinks=m_attn_sinks,
             g_ssm_out=m_g_ssm_out, g_attn_out=m_g_attn_out, w_out=m_w_out, g_post_mix=m_g_post_mix,
             g_pre_ffn=m_g_pre_ffn, w_gate_up=m_w_gate_up, w_down=m_w_down, g_post_ffn=m_g_post_ffn)
    v = dict(g_pre_mix=v_g_pre_mix, w_in=v_w_in, ssm_lambda_re=v_ssm_lambda_re, ssm_lambda_im=v_ssm_lambda_im,
             ssm_log_dt=v_ssm_log_dt, ssm_b_re=v_ssm_b_re, ssm_b_im=v_ssm_b_im, ssm_c_re=v_ssm_c_re,
             ssm_c_im=v_ssm_c_im, ssm_d=v_ssm_d, w_glu=v_w_glu, b_glu=v_b_glu, attn_sinks=v_attn_sinks,
             g_ssm_out=v_g_ssm_out, g_attn_out=v_g_attn_out, w_out=v_w_out, g_post_mix=v_g_post_mix,
             g_pre_ffn=v_g_pre_ffn, w_gate_up=v_w_gate_up, w_down=v_w_down, g_post_ffn=v_g_post_ffn)

    transposed = ("w_in", "w_glu", "w_gate_up")
    native_transposed = ("w_in", "w_gate_up")
    shard = {n: (w[n][0].T if n in transposed else w[n][0]).astype(_BF16) for n in _BIG}
    gathered = {}
    for cid, names in enumerate((("w_in", "w_glu", "w_out"), ("w_gate_up", "w_down")), start=1):
        lands = _sequencer_gather([shard[n] for n in names], "gather_" + names[0], cid)
        gathered.update({n: a.reshape(-1, a.shape[2]) for n, a in zip(names, lands)})

    def fetch(names, after):
        del after
        return [gathered[n] for n in names]

    sent = []
    ids = iter(range(4, 16))
    two_step = {}

    def publish(named):
        big = [n for n in named if n in _BIG]
        if set(big) == {"w_gate_up", "w_down"}:
            blocks = [named[n].reshape(N_DEV, -1, named[n].shape[1]) for n in big]
            two_step.update(names=big, blocks=blocks,
                            received=_sequencer_pair_exchange(blocks, "grads_pair", next(ids)))
            return [named[n] for n in big]
        rows = [n for n in named if n in _ROW_WIDTH]
        dense = [n for n in named if n in _DENSE]
        plain = [n for n in named if n not in big + rows + dense]
        sources = [named[n].reshape(N_DEV, -1, named[n].shape[1]) for n in big]
        slots = _row_slots(rows)
        if rows:
            sources.append(_stack_rows(named, slots))
        sources += [named[n].reshape(-1, _LANES) for n in dense] + [named[n] for n in plain]
        flags = [True] * len(big) + [False] * (len(sources) - len(big))
        cid = next(ids)
        sent.append((big, slots, dense, plain, _sequencer_exchange(sources, flags, "grads_%d" % cid, cid)))
        return [named[n] for n in big]

    def progress(after):
        core = lax.axis_index("c").astype(jnp.int32).reshape(1)
        partials = [_pair_sum(b, r, core, "pair_sum_" + n, [after])
                    for n, b, r in zip(two_step["names"], two_step["blocks"], two_step["received"])]
        sent.append((two_step["names"], {}, [], [], _sequencer_chip_exchange(partials, "grads_chips", next(ids))))
        return partials

    p = {n: w[n] for n in _SMALL}
    grad_x = _local_step(x[0], positions[0], loss_target[0], p, fetch, publish, progress)

    state = {n: [_to_2d(n, a) for a in (w[n], m[n], v[n])] for n in _SMALL}
    result = {}
    total_loss = None
    chain = []
    for big, slots, dense, plain, lands in sent:
        lands = list(lands)
        after = list(chain)
        for name in big:
            part = lands.pop(0)
            if name in native_transposed:
                updated = _adamw(part, w[name][0].T, m[name][0].T, v[name][0].T, "adamw_" + name, after)
                result[name] = [a.T[None] for a in updated]
                chain.append(updated[3])
                continue
            if name in transposed:
                part = _sum_parts(part, "sum_" + name, after).T[None]
            updated = _adamw(part, w[name][0], m[name][0], v[name][0], "adamw_" + name, after)
            result[name] = [a[None] for a in updated]
            chain.append(updated[3])
        parts, items, sums, names = [], [], [], []
        if slots:
            parts.append(lands.pop(0))
            for name, (row, col, _) in slots.items():
                if name == "loss":
                    sums.append((0, slice(row, row + 1), slice(col, col + _LANES)))
                else:
                    items.append((0, slice(row, row + 1), slice(col, col + _SHAPE_2D[name][1]), *state[name]))
                    names.append(name)
        for name in dense:
            part = lands.pop(0).reshape((N_DEV,) + _SHAPE_2D[name])
            result[name] = _adamw(part, *state[name], "adamw_" + name, after)
            chain.append(result[name][3])
        for name in plain:
            packed = _SSM_PACK if name == "ssm_pack" else {name: (0, _SHAPE_2D[name][0], 0, _SHAPE_2D[name][1])}
            for member, (first, rows_n, lane, cols_n) in packed.items():
                items.append((len(parts), slice(first, first + rows_n), slice(lane, lane + cols_n), *state[member]))
                names.append(member)
            parts.append(lands.pop(0))
        if items:
            updated, summed = _adamw_small(parts, items, sums, "adamw_small_" + names[0], after)
            chain.append(updated[0][3])
            result.update(dict(zip(names, updated)))
            if summed:
                total_loss = summed[0][0, 0]

    out = [total_loss, grad_x[None]]
    for kind in range(4):
        out += [_from_2d(n, result[n][kind], w[n].shape) for n in _WEIGHTS]
    return tuple(out)
```

```python
import math

import numpy as np
import jax
import jax.numpy as jnp
from jax import lax
from jax.experimental import pallas as pl
from jax.experimental.pallas import tpu as pltpu
from jax.experimental.pallas import tpu_sc as plsc

D_MODEL = 1024
SSM_WIDTH = 512
SSM_GROUP = 16
SSM_GROUPS = 32
SSM_STATE = 64
N_STATE = SSM_GROUPS * SSM_STATE
ATTN_WIDTH = 512
HEAD_DIM = 64
N_Q_HEADS = 8
N_KV_HEADS = 2
Q_PER_KV = 4
KV_WIDTH = 128
IN_WIDTH = 1280
BLOCK = 128
ROPE_DIM = 16
ROPE_THETA = 500000.0
D_FF = 2816
NORM_EPS = 1e-6
MASK_VALUE = -1e30
ADAM_LR = 0.001
ADAM_B1 = 0.9
ADAM_B2 = 0.999
ADAM_EPS = 1e-08
ADAM_WD = 0.01
ADAM_STEP = 10

N_DEV = 8
SCAN_CHUNKS = 8
SCAN_UNROLL = 8
FFN_CHUNK = 2816
TOKEN_TILE = 256
VMEM_LIMIT = 56 * 1024 * 1024

_F32 = jnp.float32
_BF16 = jnp.bfloat16
_MXU = jnp.bfloat16

_NN = ((1,), (0,))
_NT = ((1,), (1,))
_TN = ((0,), (0,))


def _dot(a, b, dims):
    return lax.dot_general(a.astype(_MXU), b.astype(_MXU), (dims, ((), ())),
                           preferred_element_type=_F32)


def _dot_exact(a, b, dims):
    return lax.dot_general(a.astype(_F32), b.astype(_F32), (dims, ((), ())),
                           precision=lax.Precision.HIGHEST, preferred_element_type=_F32)


def _iota(shape, dim):
    return lax.broadcasted_iota(jnp.int32, shape, dim)


def _rms_fwd(x, g):
    r = lax.rsqrt(jnp.mean(x * x, axis=-1, keepdims=True) + NORM_EPS)
    return x * r * g, r


def _rms_bwd(dy, x, g, r):
    a = dy * g
    xn = x * r
    dx = r * (a - xn * jnp.mean(a * xn, axis=-1, keepdims=True))
    dg = jnp.sum(dy * xn, axis=0, keepdims=True)
    return dx, dg


def _call(body, grid, in_specs, out_specs, out_shape, name, scratch=(), tokens=()):
    params = pltpu.CompilerParams(dimension_semantics=("arbitrary",) * len(grid),
                                  vmem_limit_bytes=VMEM_LIMIT)
    n_in, n_tok = len(in_specs), len(tokens)

    def run(*refs):
        return body(*refs[:n_in], *refs[n_in + n_tok:])

    call = pl.pallas_call(run, grid=grid,
                          in_specs=list(in_specs) + [pl.BlockSpec(memory_space=pl.ANY)] * n_tok,
                          out_specs=out_specs, out_shape=out_shape, scratch_shapes=list(scratch),
                          compiler_params=params, name=name)
    return lambda *args: call(*args, *tokens)


def _rows(tm, n):
    return pl.BlockSpec((tm, n), lambda i: (i, 0))


def _whole(shape):
    nd = len(shape)
    return pl.BlockSpec(shape, lambda i: (0,) * nd)


def _sds(shape, dtype):
    return jax.ShapeDtypeStruct(shape, dtype)


def _tile(L):
    return min(TOKEN_TILE, L)


def _chunk_tile(L):
    return L // SCAN_CHUNKS


def _chunk_block(L, n):
    return pl.BlockSpec((_chunk_tile(L), n), lambda i: (0, i))


def _chunk_shape(L, n):
    return (_chunk_tile(L), SCAN_CHUNKS * n)


def _accumulate(ref, val, first):
    @pl.when(first)
    def _():
        ref[...] = val

    @pl.when(jnp.logical_not(first))
    def _():
        ref[...] += val


def _rope_rows():
    half = ROPE_DIM // 2
    inv = (np.float32(ROPE_THETA) ** (-np.arange(half, dtype=np.float32) * np.float32(2.0) / np.float32(ROPE_DIM))).astype(np.float32)
    col = np.arange(KV_WIDTH) % HEAD_DIM
    freq = np.where(col < ROPE_DIM, inv[col % half], 0.0).astype(np.float32)
    sign = np.where(col < half, -1.0, np.where(col < ROPE_DIM, 1.0, 0.0)).astype(np.float32)
    return freq[None, :], sign[None, :]


def _rope_tables(pos_col):
    L = pos_col.shape[0]
    tm = _tile(L)
    freq, sign = _rope_rows()

    def body(pos_ref, freq_ref, sign_ref, cos_ref, sin_ref):
        ang = pos_ref[...].astype(_F32) * freq_ref[...]
        cos_ref[...] = jnp.cos(ang)
        sin_ref[...] = jnp.sin(ang) * sign_ref[...]

    return _call(body, (L // tm,),
                 [_rows(tm, 1), _whole((1, KV_WIDTH)), _whole((1, KV_WIDTH))],
                 [_rows(tm, KV_WIDTH), _rows(tm, KV_WIDTH)],
                 [_sds((L, KV_WIDTH), _F32)] * 2, "rope_tables")(pos_col, jnp.asarray(freq), jnp.asarray(sign))


def _widen(t, width):
    return t if width == KV_WIDTH else jnp.concatenate([t] * (width // KV_WIDTH), axis=1)


def _rope_partner(t):
    w = t.shape[1]
    in_head = _iota((1, w), 1) & (HEAD_DIM - 1)
    second = jnp.where(in_head < ROPE_DIM, pltpu.roll(t, ROPE_DIM // 2, 1), 0.0)
    return jnp.where(in_head < ROPE_DIM // 2, pltpu.roll(t, w - ROPE_DIM // 2, 1), second)


def _rope_apply(t, cos_t, sin_t):
    w = t.shape[1]
    return t * _widen(cos_t, w) + _rope_partner(t) * _widen(sin_t, w)


def _rope_transpose(dt, cos_t, sin_t):
    w = dt.shape[1]
    return dt * _widen(cos_t, w) + _rope_partner(dt * _widen(sin_t, w))


def _in_proj(x, g_pre_mix, w_in, cos_t, sin_t):
    L = x.shape[0]
    tm = _chunk_tile(L)

    def body(x_ref, g_ref, w_ref, cos_ref, sin_ref, hn_ref, u_ref, q_ref, k_ref, v_ref):
        hn, _ = _rms_fwd(x_ref[...], g_ref[...])
        hn = hn.astype(_BF16)
        hn_ref[...] = hn
        proj = _dot(hn, w_ref[...], _NT)
        u_ref[...] = proj[:, :SSM_WIDTH]
        q = proj[:, SSM_WIDTH:SSM_WIDTH + ATTN_WIDTH]
        k = proj[:, SSM_WIDTH + ATTN_WIDTH:SSM_WIDTH + ATTN_WIDTH + KV_WIDTH]
        cos_v, sin_v = cos_ref[...], sin_ref[...]
        q_ref[...] = _rope_apply(q, cos_v, sin_v).astype(_BF16)
        k_ref[...] = _rope_apply(k, cos_v, sin_v).astype(_BF16)
        v_ref[...] = proj[:, SSM_WIDTH + ATTN_WIDTH + KV_WIDTH:].astype(_BF16)

    return _call(body, (L // tm,),
                 [_rows(tm, D_MODEL), _whole((1, D_MODEL)), _whole((IN_WIDTH, D_MODEL)),
                  _rows(tm, KV_WIDTH), _rows(tm, KV_WIDTH)],
                 [_rows(tm, D_MODEL), _chunk_block(L, SSM_WIDTH), _rows(tm, ATTN_WIDTH),
                  _rows(tm, KV_WIDTH), _rows(tm, KV_WIDTH)],
                 [_sds((L, D_MODEL), _BF16), _sds(_chunk_shape(L, SSM_WIDTH), _F32), _sds((L, ATTN_WIDTH), _BF16),
                  _sds((L, KV_WIDTH), _BF16), _sds((L, KV_WIDTH), _BF16)],
                 "in_proj")(x, g_pre_mix, w_in, cos_t, sin_t)


def _s5_discretize(lam_re, lam_im, log_dt):
    lr = jnp.minimum(lam_re, -1e-4)
    li = lam_im
    dt = jnp.exp(log_dt)
    mag = jnp.exp(lr * dt)
    ar = mag * jnp.cos(li * dt)
    ai = mag * jnp.sin(li * dt)
    den = lr * lr + li * li
    fr = ((ar - 1.0) * lr + ai * li) / den
    fi = (ai * lr - (ar - 1.0) * li) / den
    return ar, ai, fr, fi


SUPER = 4
SB_STATE = N_STATE // SUPER
SB_WIDTH = SSM_WIDTH // SUPER


def _sb_state(k):
    return slice(SB_STATE * k, SB_STATE * (k + 1))


def _sb_width(k):
    return slice(SB_WIDTH * k, SB_WIDTH * (k + 1))


def _dt_column(log_dt_row):
    eye = _iota((SSM_GROUPS, SSM_GROUPS), 0) == _iota((SSM_GROUPS, SSM_GROUPS), 1)
    return jnp.sum(jnp.where(eye, log_dt_row, 0.0), axis=1, keepdims=True)


def _group_masks():
    e64 = ((_iota((SSM_STATE, N_STATE), 1) & (SSM_STATE - 1)) == _iota((SSM_STATE, N_STATE), 0)).astype(_F32)
    own = _iota((SSM_GROUPS, N_STATE), 0) == (_iota((SSM_GROUPS, N_STATE), 1) >> 6)
    return e64, own


def _rows_of_group():
    return ((_iota((SSM_WIDTH, SSM_GROUPS), 0) >> 4) == _iota((SSM_WIDTH, SSM_GROUPS), 1)).astype(_F32)


def _ssm_prep(lam_re, lam_im, log_dt, b_re, b_im, c_re, c_im):
    def body(lr_ref, li_ref, ld_ref, bre, bim, cre, cim, ar_ref, ai_ref, btr, bti, ctr, cti):
        ar, ai, fr, fi = _s5_discretize(lr_ref[...], li_ref[...], _dt_column(ld_ref[...]))
        e64, own = _group_masks()
        mask_c = (_iota((SSM_WIDTH, N_STATE), 0) >> 4) == (_iota((SSM_WIDTH, N_STATE), 1) >> 6)

        def to_row(t):
            return jnp.sum(jnp.where(own, _dot_exact(t, e64, _NN), 0.0), axis=0, keepdims=True)

        def fold(m):
            full = jnp.where(mask_c, _dot(m, e64, _NN), 0.0)
            return sum(full[_sb_width(k), :] for k in range(SUPER)).astype(_BF16)

        ar_ref[...] = to_row(ar)
        ai_ref[...] = to_row(ai)
        spread = _rows_of_group()
        fr_t = _dot_exact(spread, fr, _NN)
        fi_t = _dot_exact(spread, fi, _NN)
        btr[...] = fold(fr_t * bre[...] - fi_t * bim[...])
        bti[...] = fold(fr_t * bim[...] + fi_t * bre[...])
        ctr[...] = fold(cre[...])
        cti[...] = fold(cim[...])

    row = (1, N_STATE)
    ins = [lam_re, lam_im, log_dt, b_re, b_im, c_re, c_im]
    return _call(body, (1,), [_whole(a.shape) for a in ins],
                 [_whole(row), _whole(row)] + [_whole((SB_WIDTH, N_STATE))] * 4,
                 [_sds(row, _F32), _sds(row, _F32)] + [_sds((SB_WIDTH, N_STATE), _BF16)] * 4,
                 "ssm_prep")(*ins)


def _complex_power(ar, ai, n):
    def step(_, c):
        pr, pi = c
        return pr * ar - pi * ai, pr * ai + pi * ar
    return lax.fori_loop(0, n, step, (jnp.ones_like(ar), jnp.zeros_like(ai)))


def _chunk_carries(er, ei, pr, pi, reverse):
    rows = _iota(er.shape, 0)
    sr = jnp.zeros_like(pr)
    si = jnp.zeros_like(pi)
    out_r = jnp.zeros_like(er)
    out_i = jnp.zeros_like(ei)
    order = range(SCAN_CHUNKS - 1, 0, -1) if reverse else range(SCAN_CHUNKS - 1)
    for c in order:
        e_r = er[c:c + 1, :]
        e_i = ei[c:c + 1, :]
        sr, si = pr * sr - pi * si + e_r, pr * si + pi * sr + e_i
        nxt = c - 1 if reverse else c + 1
        out_r = jnp.where(rows == nxt, sr, out_r)
        out_i = jnp.where(rows == nxt, si, out_i)
    return out_r, out_i


_GELU_K = math.sqrt(2.0 / math.pi)
_GELU_C = 0.044715


def _gelu(y):
    return 0.5 * y * (1.0 + jnp.tanh(_GELU_K * (y + _GELU_C * y * y * y)))


def _gelu_grad(y):
    t = jnp.tanh(_GELU_K * (y + _GELU_C * y * y * y))
    return 0.5 * (1.0 + t) + 0.5 * y * (1.0 - t * t) * _GELU_K * (1.0 + 3.0 * _GELU_C * y * y)


def _step_rows(t):
    return pl.ds(pl.multiple_of(t * SCAN_CHUNKS, SCAN_CHUNKS), SCAN_CHUNKS)


def _scan_in_place(br, bi, ar, ai, T):
    W = br.shape[1]
    ar8 = jnp.broadcast_to(ar, (SCAN_CHUNKS, W))
    ai8 = jnp.broadcast_to(ai, (SCAN_CHUNKS, W))

    def local(t, c):
        cr, ci = c
        rows = _step_rows(t)
        return ar8 * cr - ai8 * ci + br[rows, :], ar8 * ci + ai8 * cr + bi[rows, :]

    zero = jnp.zeros((SCAN_CHUNKS, W), _F32)
    er, ei = lax.fori_loop(0, T, local, (zero, zero), unroll=SCAN_UNROLL)
    pr, pi = _complex_power(ar, ai, T)
    sr, si = _chunk_carries(er, ei, pr, pi, reverse=False)

    def final(t, c):
        nr, ni = local(t, c)
        rows = _step_rows(t)
        br[rows, :] = nr
        bi[rows, :] = ni
        return nr, ni

    lax.fori_loop(0, T, final, (sr, si), unroll=SCAN_UNROLL)


def _scan_reverse_in_place(dr, di, xr, xi, ar, ai, T):
    W = dr.shape[1]
    ar8 = jnp.broadcast_to(ar, (SCAN_CHUNKS, W))
    ai8 = jnp.broadcast_to(ai, (SCAN_CHUNKS, W))

    def local(t, c):
        cr, ci = c
        rows = _step_rows(t)
        return ar8 * cr + ai8 * ci + dr[rows, :], ar8 * ci - ai8 * cr + di[rows, :]

    zero = jnp.zeros((SCAN_CHUNKS, W), _F32)
    er, ei = lax.fori_loop(0, T, lambda k, c: local(T - 1 - k, c), (zero, zero), unroll=SCAN_UNROLL)
    pr, pi = _complex_power(ar, -ai, T)
    sr, si = _chunk_carries(er, ei, pr, pi, reverse=True)

    def grad_a(acc, nr, ni, xpr, xpi):
        return acc[0] + nr * xpr + ni * xpi, acc[1] + ni * xpr - nr * xpi

    def final(k, c):
        t = T - 1 - k
        nr, ni = local(t, c[:2])
        rows = _step_rows(t)
        dr[rows, :] = nr
        di[rows, :] = ni
        before = _step_rows(t - 1)
        gr, gi = grad_a(c[2:], nr, ni, xr[before, :], xi[before, :])
        return nr, ni, gr, gi

    cr, ci, gr, gi = lax.fori_loop(0, T - 1, final, (sr, si, zero, zero), unroll=SCAN_UNROLL)
    nr, ni = local(0, (cr, ci))
    dr[_step_rows(0), :] = nr
    di[_step_rows(0), :] = ni
    first = _iota((SCAN_CHUNKS, W), 0) == 0
    last = _step_rows(T - 1)
    xpr = jnp.where(first, 0.0, pltpu.roll(xr[last, :], 1, 0))
    xpi = jnp.where(first, 0.0, pltpu.roll(xi[last, :], 1, 0))
    gr, gi = grad_a((gr, gi), nr, ni, xpr, xpi)
    return jnp.sum(gr, axis=0, keepdims=True), jnp.sum(gi, axis=0, keepdims=True)


def _ssm_super_specs(L):
    width = pl.BlockSpec((L, SB_WIDTH), lambda k: (0, k))
    matrix = pl.BlockSpec((SB_WIDTH, SB_STATE), lambda k: (0, k))
    row = pl.BlockSpec((1, SB_STATE), lambda k: (0, k))
    return width, matrix, row


def _ssm_states(u_ref, br_ref, bi_ref, ar_ref, ai_ref, xr, xi, T):
    ub = u_ref[...].astype(_BF16)
    xr[...] = _dot(ub, br_ref[...], _NN)
    xi[...] = _dot(ub, bi_ref[...], _NN)
    _scan_in_place(xr, xi, ar_ref[...], ai_ref[...], T)


def _ssm_core_fwd(u, bt_re, bt_im, ct_re, ct_im, a_re, a_im):
    L = u.shape[0]
    T = L // SCAN_CHUNKS

    def body(u_ref, br_ref, bi_ref, cr_ref, ci_ref, ar_ref, ai_ref, y_ref, xr, xi):
        _ssm_states(u_ref, br_ref, bi_ref, ar_ref, ai_ref, xr, xi, T)
        y_ref[...] = _dot(xr[...], cr_ref[...], _NT) - _dot(xi[...], ci_ref[...], _NT)

    width, matrix, row = _ssm_super_specs(L)
    return _call(body, (SUPER,), [width, matrix, matrix, matrix, matrix, row, row], width,
                 _sds((L, SSM_WIDTH), _F32), "ssm_core_fwd",
                 scratch=[pltpu.VMEM((L, SB_STATE), _F32)] * 2)(u, bt_re, bt_im, ct_re, ct_im, a_re, a_im)


def _ssm_core_bwd(u, dy, dud, bt_re, bt_im, ct_re, ct_im, a_re, a_im, tokens=()):
    L = u.shape[0]
    T = L // SCAN_CHUNKS

    def body(u_ref, dy_ref, dud_ref, br_ref, bi_ref, cr_ref, ci_ref, ar_ref, ai_ref,
             du_ref, dcr_ref, dci_ref, dbr_ref, dbi_ref, dar_ref, dai_ref, xr, xi, lr, li):
        _ssm_states(u_ref, br_ref, bi_ref, ar_ref, ai_ref, xr, xi, T)
        dyb = dy_ref[...]
        lr[...] = _dot(dyb, cr_ref[...], _NN)
        li[...] = -_dot(dyb, ci_ref[...], _NN)
        da_re, da_im = _scan_reverse_in_place(lr, li, xr, xi, ar_ref[...], ai_ref[...], T)
        dar_ref[...] = da_re
        dai_ref[...] = da_im
        du_ref[...] = _dot(lr[...], br_ref[...], _NT) + _dot(li[...], bi_ref[...], _NT) + dud_ref[...]
        ub = u_ref[...].astype(_BF16)
        dcr_ref[...] = _dot(dyb, xr[...], _TN)
        dci_ref[...] = _dot(dyb, xi[...], _TN)
        dbr_ref[...] = _dot(ub, lr[...], _TN)
        dbi_ref[...] = _dot(ub, li[...], _TN)

    width, matrix, row = _ssm_super_specs(L)
    return _call(body, (SUPER,), [width, width, width, matrix, matrix, matrix, matrix, row, row],
                 [width] + [matrix] * 4 + [row] * 2,
                 [_sds((L, SSM_WIDTH), _F32)] + [_sds((SB_WIDTH, N_STATE), _F32)] * 4 + [_sds((1, N_STATE), _F32)] * 2,
                 "ssm_core_bwd", scratch=[pltpu.VMEM((L, SB_STATE), _F32)] * 4,
                 tokens=tokens)(u, dy, dud, bt_re, bt_im, ct_re, ct_im, a_re, a_im)


def _ssm_out(cx, u, d_row, w_glu, b_glu, g_ssm):
    L = u.shape[0]
    tm = _tile(L)

    def body(cx_ref, u_ref, d_ref, w_ref, b_ref, g_ref, y_ref, z_ref, n_ref):
        y = cx_ref[...] + d_ref[...] * u_ref[...]
        y_ref[...] = y
        z = _dot(_gelu(y), w_ref[...], _NT) + b_ref[...]
        z_ref[...] = z
        out = z[:, :SSM_WIDTH] * jax.nn.sigmoid(z[:, SSM_WIDTH:])
        n, _ = _rms_fwd(out, g_ref[...])
        n_ref[...] = n.astype(_BF16)

    return _call(body, (L // tm,),
                 [_rows(tm, SSM_WIDTH), _rows(tm, SSM_WIDTH), _whole((1, SSM_WIDTH)),
                  _whole((2 * SSM_WIDTH, SSM_WIDTH)), _whole((1, 2 * SSM_WIDTH)), _whole((1, SSM_WIDTH))],
                 [_rows(tm, SSM_WIDTH), _rows(tm, 2 * SSM_WIDTH), _rows(tm, SSM_WIDTH)],
                 [_sds((L, SSM_WIDTH), _F32), _sds((L, 2 * SSM_WIDTH), _F32), _sds((L, SSM_WIDTH), _BF16)],
                 "ssm_out")(cx, u, d_row, w_glu, b_glu, g_ssm)


def _ssm_out_bwd(dn, y, z, u, d_row, w_glu, g_ssm):
    L = u.shape[0]
    tm = _tile(L)

    def body(dn_ref, y_ref, z_ref, u_ref, d_ref, w_ref, g_ref,
             gy_ref, dz_ref, dy_ref, dud_ref, dg_ref, db_ref, dd_ref):
        first = pl.program_id(0) == 0
        z = z_ref[...]
        z1, z2 = z[:, :SSM_WIDTH], z[:, SSM_WIDTH:]
        sig = jax.nn.sigmoid(z2)
        out = z1 * sig
        g = g_ref[...]
        _, r = _rms_fwd(out, g)
        dout, dg = _rms_bwd(dn_ref[...], out, g, r)
        _accumulate(dg_ref, dg, first)
        dz = jnp.concatenate([dout * sig, dout * z1 * sig * (1.0 - sig)], axis=1)
        _accumulate(db_ref, jnp.sum(dz, axis=0, keepdims=True), first)
        dzb = dz.astype(_BF16)
        dz_ref[...] = dzb
        y = y_ref[...]
        gy_ref[...] = _gelu(y).astype(_BF16)
        dy = _dot(dzb, w_ref[...], _NN) * _gelu_grad(y)
        u = u_ref[...]
        _accumulate(dd_ref, jnp.sum(dy * u, axis=0, keepdims=True), first)
        dud_ref[...] = d_ref[...] * dy
        dy_ref[...] = dy.astype(_BF16)

    row = _whole((1, SSM_WIDTH))
    return _call(body, (L // tm,),
                 [_rows(tm, SSM_WIDTH), _rows(tm, SSM_WIDTH), _rows(tm, 2 * SSM_WIDTH), _rows(tm, SSM_WIDTH),
                  row, _whole((2 * SSM_WIDTH, SSM_WIDTH)), row],
                 [_rows(tm, SSM_WIDTH), _rows(tm, 2 * SSM_WIDTH), _rows(tm, SSM_WIDTH), _rows(tm, SSM_WIDTH),
                  row, _whole((1, 2 * SSM_WIDTH)), row],
                 [_sds((L, SSM_WIDTH), _BF16), _sds((L, 2 * SSM_WIDTH), _BF16), _sds((L, SSM_WIDTH), _BF16),
                  _sds((L, SSM_WIDTH), _F32),
                  _sds((1, SSM_WIDTH), _F32), _sds((1, 2 * SSM_WIDTH), _F32), _sds((1, SSM_WIDTH), _F32)],
                 "ssm_out_bwd")(dn, y, z, u, d_row, w_glu, g_ssm)


_SSM_PACK = {"ssm_b_re": (0, SSM_WIDTH, 0, SSM_STATE), "ssm_c_re": (0, SSM_WIDTH, 64, SSM_STATE),
             "ssm_b_im": (512, SSM_WIDTH, 0, SSM_STATE), "ssm_c_im": (512, SSM_WIDTH, 64, SSM_STATE),
             "ssm_lambda_re": (1024, SSM_GROUPS, 0, SSM_STATE), "ssm_lambda_im": (1024, SSM_GROUPS, 64, SSM_STATE),
             "ssm_d": (1056, SSM_GROUPS, 0, SSM_GROUP), "ssm_log_dt": (1088, 1, 0, SSM_GROUPS)}
_PACK_TILE = 16
_SSM_PACK_ROWS = 1088 + _PACK_TILE


def _ssm_param_bwd(da_re, da_im, dbt_re, dbt_im, dct_re, dct_im, lam_re, lam_im, log_dt, b_re, b_im, g_d):
    def body(dar, dai, dbr, dbi, dcr, dci, lr_ref, li_ref, ld_ref, bre_ref, bim_ref, gd_ref, pack_ref):
        lane_in = _iota((SSM_STATE, _LANES), 0)
        lane_out = _iota((SSM_STATE, _LANES), 1)
        low = (lane_out == lane_in).astype(_F32)
        high = (lane_out == lane_in + SSM_STATE).astype(_F32)

        def side_by_side(a, b):
            return _dot_exact(a, low, _NN) + _dot_exact(b, high, _NN)

        tail = _SSM_PACK["ssm_d"][0]
        pack_ref[tail:, :] = jnp.zeros((_SSM_PACK_ROWS - tail, _LANES), _BF16)
        pack_ref[tail:tail + SSM_GROUPS, 0:SSM_GROUP] = gd_ref[...].astype(_BF16)
        own_c = (_iota((SB_WIDTH, SB_STATE), 0) >> 4) == (_iota((SB_WIDTH, SB_STATE), 1) >> 6)

        def unfold(ref):
            blocks = []
            for k in range(SUPER):
                t = jnp.where(own_c, ref[:, _sb_state(k)], 0.0)
                t = sum(t[:, 128 * i:128 * (i + 1)] for i in range(SB_STATE // 128))
                blocks.append((t + pltpu.roll(t, SSM_STATE, 1))[:, :SSM_STATE])
            return jnp.concatenate(blocks, axis=0)

        dbb_re, dbb_im = unfold(dbr), unfold(dbi)
        b_re, b_im = bre_ref[...], bim_ref[...]
        dt_col = _dt_column(ld_ref[...])
        (_, _, fr, fi), vjp = jax.vjp(_s5_discretize, lr_ref[...], li_ref[...], dt_col)
        spread = _rows_of_group()
        fr_t = _dot_exact(spread, fr, _NN)
        fi_t = _dot_exact(spread, fi, _NN)
        pack_ref[0:SSM_WIDTH, :] = side_by_side(fr_t * dbb_re + fi_t * dbb_im, unfold(dcr)).astype(_BF16)
        pack_ref[SSM_WIDTH:2 * SSM_WIDTH, :] = side_by_side(fr_t * dbb_im - fi_t * dbb_re, -unfold(dci)).astype(_BF16)
        d_fr = _dot_exact(spread, dbb_re * b_re + dbb_im * b_im, _TN)
        d_fi = _dot_exact(spread, dbb_im * b_re - dbb_re * b_im, _TN)
        e64, own = _group_masks()

        def from_row(ref):
            return _dot_exact(jnp.where(own, ref[...], 0.0), e64, _NT)

        d_lr, d_li, d_dt = vjp((from_row(dar), from_row(dai), d_fr, d_fi))
        lam_rows = _SSM_PACK["ssm_lambda_re"][0]
        pack_ref[lam_rows:lam_rows + SSM_GROUPS, :] = side_by_side(d_lr, d_li).astype(_BF16)
        eye = (_iota((SSM_GROUPS, SSM_GROUPS), 0) == _iota((SSM_GROUPS, SSM_GROUPS), 1)).astype(_F32)
        dt_row = _SSM_PACK["ssm_log_dt"][0]
        pack_ref[dt_row:dt_row + _PACK_TILE, 0:SSM_GROUPS] = _dot_exact(
            jnp.broadcast_to(d_dt, (SSM_GROUPS, 128)), eye, _TN)[0:_PACK_TILE].astype(_BF16)

    ins = [da_re, da_im, dbt_re, dbt_im, dct_re, dct_im, lam_re, lam_im, log_dt, b_re, b_im, g_d]
    out = (_SSM_PACK_ROWS, _LANES)
    return _call(body, (1,), [_whole(a.shape) for a in ins], _whole(out), _sds(out, _BF16), "ssm_param_bwd")(*ins)


def _head_spread(j):
    r = _iota((KV_WIDTH, 256), 0)
    c = _iota((KV_WIDTH, 256), 1)
    return (r == HEAD_DIM * j + (c & (HEAD_DIM - 1))).astype(_BF16)


STACK = Q_PER_KV * BLOCK


def _stack_heads(t):
    lane_head = _iota((1, 256), 1) >> 6
    return jnp.concatenate([jnp.where(lane_head == g, t, jnp.zeros_like(t)) for g in range(Q_PER_KV)], axis=0)


def _unstack_heads(t):
    lane_head = _iota((1, 256), 1) >> 6
    return sum(jnp.where(lane_head == g, t[BLOCK * g:BLOCK * (g + 1)], 0.0) for g in range(Q_PER_KV))


def _stacked_sinks(sink_ref, j):
    block = _iota((STACK, 1), 0) >> 7
    col = jnp.full((STACK, 1), sink_ref[Q_PER_KV * j], _F32)
    for g in range(1, Q_PER_KV):
        col = jnp.where(block == g, sink_ref[Q_PER_KV * j + g], col)
    return col


def _fold_heads(t, j):
    t = t[:, :KV_WIDTH] + t[:, KV_WIDTH:]
    t = t + pltpu.roll(t, HEAD_DIM, 1)
    return jnp.where((_iota((1, KV_WIDTH), 1) >> 6) == j, t, 0.0)


def _attn_scores(q_stacked, kt, blk, sink):
    s = _dot(q_stacked, kt, _NT) * (HEAD_DIM ** -0.5)
    qi = _iota((STACK, 2 * BLOCK), 0) & (BLOCK - 1)
    kj = _iota((STACK, 2 * BLOCK), 1)
    rel = qi + BLOCK - kj
    valid = (rel >= 0) & (rel < BLOCK) & (blk * BLOCK - BLOCK + kj >= 0)
    s = jnp.where(valid, s, MASK_VALUE)
    m = jnp.maximum(jnp.max(s, axis=-1, keepdims=True), sink)
    p = jnp.exp(s - m)
    e_sink = jnp.exp(sink - m)
    den = jnp.sum(p, axis=-1, keepdims=True) + e_sink
    return p / den, e_sink / den


def _attn_specs():
    prev = lambda i: (jnp.maximum(i - 1, 0), 0)
    cur = lambda i: (i, 0)
    kv = [pl.BlockSpec((BLOCK, KV_WIDTH), prev), pl.BlockSpec((BLOCK, KV_WIDTH), cur)]
    return [pl.BlockSpec((BLOCK, ATTN_WIDTH), cur)] + kv + kv


def _attn_fwd(q, k, v, sinks, g_attn):
    L = q.shape[0]

    def body(q_ref, kp_ref, kc_ref, vp_ref, vc_ref, sink_ref, g_ref, o_ref, n_ref):
        blk = pl.program_id(0)
        kwin = jnp.concatenate([kp_ref[...], kc_ref[...]], axis=0)
        vwin = jnp.concatenate([vp_ref[...], vc_ref[...]], axis=0)
        halves = []
        for j in range(N_KV_HEADS):
            spread = _head_spread(j)
            kt = _dot(kwin, spread, _NN).astype(_BF16)
            vt = _dot(vwin, spread, _NN).astype(_BF16)
            qs = _stack_heads(q_ref[:, 256 * j:256 * (j + 1)])
            p, _ = _attn_scores(qs, kt, blk, _stacked_sinks(sink_ref, j))
            halves.append(_unstack_heads(_dot(p, vt, _NN)))
        o = jnp.concatenate(halves, axis=1)
        o_ref[...] = o
        n, _ = _rms_fwd(o, g_ref[...])
        n_ref[...] = n.astype(_BF16)

    cur = lambda i: (i, 0)
    return _call(body, (L // BLOCK,),
                 _attn_specs() + [pl.BlockSpec(memory_space=pltpu.SMEM), _whole((1, ATTN_WIDTH))],
                 [pl.BlockSpec((BLOCK, ATTN_WIDTH), cur)] * 2,
                 [_sds((L, ATTN_WIDTH), _F32), _sds((L, ATTN_WIDTH), _BF16)],
                 "attn_fwd")(q, k, k, v, v, sinks, g_attn)


def _attn_bwd(q, k, v, o, dn, sinks, g_attn):
    L = q.shape[0]

    def body(q_ref, kp_ref, kc_ref, vp_ref, vc_ref, o_ref, dn_ref, sink_ref, g_ref,
             dq_ref, dk_ref, dv_ref, dsink_ref, dg_ref):
        blk = pl.program_id(0)
        first = blk == 0

        @pl.when(first)
        def _():
            dk_ref[...] = jnp.zeros_like(dk_ref)
            dv_ref[...] = jnp.zeros_like(dv_ref)
            dsink_ref[...] = jnp.zeros_like(dsink_ref)

        o = o_ref[...]
        g = g_ref[...]
        _, r = _rms_fwd(o, g)
        do, dg = _rms_bwd(dn_ref[...], o, g, r)
        _accumulate(dg_ref, dg, first)
        kwin = jnp.concatenate([kp_ref[...], kc_ref[...]], axis=0)
        vwin = jnp.concatenate([vp_ref[...], vc_ref[...]], axis=0)
        lane = _iota((1, 128), 1)
        dsink = jnp.zeros((1, 128), _F32)
        dkwin = jnp.zeros((2 * BLOCK, KV_WIDTH), _F32)
        dvwin = jnp.zeros((2 * BLOCK, KV_WIDTH), _F32)
        dq_halves = []
        for j in range(N_KV_HEADS):
            spread = _head_spread(j)
            kt = _dot(kwin, spread, _NN).astype(_BF16)
            vt = _dot(vwin, spread, _NN).astype(_BF16)
            qs = _stack_heads(q_ref[:, 256 * j:256 * (j + 1)])
            dos = _stack_heads(do[:, 256 * j:256 * (j + 1)]).astype(_BF16)
            p, p_sink = _attn_scores(qs, kt, blk, _stacked_sinks(sink_ref, j))
            dp = _dot(dos, vt, _NT)
            delta = jnp.sum(p * dp, axis=-1, keepdims=True)
            ds = (p * (dp - delta) * (HEAD_DIM ** -0.5)).astype(_BF16)
            sink_term = p_sink * delta
            for g in range(Q_PER_KV):
                head_sum = jnp.sum(sink_term[BLOCK * g:BLOCK * (g + 1)], axis=0, keepdims=True)
                dsink = dsink - jnp.where(lane == Q_PER_KV * j + g, head_sum, 0.0)
            dvwin = dvwin + _fold_heads(_dot(p, dos, _TN), j)
            dkwin = dkwin + _fold_heads(_dot(ds, qs, _TN), j)
            dq_halves.append(_unstack_heads(_dot(ds, kt, _NN)))
        dq_ref[...] = jnp.concatenate(dq_halves, axis=1)
        dsink_ref[...] += dsink
        prev = pl.ds(pl.multiple_of(jnp.maximum(blk - 1, 0) * BLOCK, BLOCK), BLOCK)
        cur = pl.ds(pl.multiple_of(blk * BLOCK, BLOCK), BLOCK)
        dk_ref[prev, :] += dkwin[:BLOCK]
        dk_ref[cur, :] += dkwin[BLOCK:]
        dv_ref[prev, :] += dvwin[:BLOCK]
        dv_ref[cur, :] += dvwin[BLOCK:]

    cur = lambda i: (i, 0)
    blk_q = pl.BlockSpec((BLOCK, ATTN_WIDTH), cur)
    return _call(body, (L // BLOCK,),
                 _attn_specs() + [blk_q, blk_q, pl.BlockSpec(memory_space=pltpu.SMEM), _whole((1, ATTN_WIDTH))],
                 [blk_q, _whole((L, KV_WIDTH)), _whole((L, KV_WIDTH)), _whole((1, 128)), _whole((1, ATTN_WIDTH))],
                 [_sds((L, ATTN_WIDTH), _F32), _sds((L, KV_WIDTH), _F32), _sds((L, KV_WIDTH), _F32),
                  _sds((1, 128), _F32), _sds((1, ATTN_WIDTH), _F32)],
                 "attn_bwd")(q, k, k, v, v, o, dn, sinks, g_attn)


def _out_proj(n_ssm, n_attn, x, w_out, g_post_mix, g_pre_ffn):
    L = x.shape[0]
    tm = _chunk_tile(L)

    def body(ns_ref, na_ref, x_ref, w_ref, g1_ref, g2_ref, merged_ref, mo_ref, h1_ref, hn2_ref):
        merged = jnp.concatenate([ns_ref[...], na_ref[...]], axis=1)
        merged_ref[...] = merged
        mo = _dot(merged, w_ref[...], _NN)
        mo_ref[...] = mo
        n, _ = _rms_fwd(mo, g1_ref[...])
        h1 = x_ref[...] + n
        h1_ref[...] = h1
        hn2, _ = _rms_fwd(h1, g2_ref[...])
        hn2_ref[...] = hn2.astype(_BF16)

    row = _whole((1, D_MODEL))
    return _call(body, (L // tm,),
                 [_chunk_block(L, SSM_WIDTH), _rows(tm, ATTN_WIDTH), _rows(tm, D_MODEL), _whole((D_MODEL, D_MODEL)),
                  row, row],
                 [_rows(tm, D_MODEL)] * 4,
                 [_sds((L, D_MODEL), _BF16), _sds((L, D_MODEL), _F32), _sds((L, D_MODEL), _F32), _sds((L, D_MODEL), _BF16)],
                 "out_proj")(n_ssm, n_attn, x, w_out, g_post_mix, g_pre_ffn)


def _ffn(hn2, h1, target, w_gate_up, w_down, g_pre_ffn, g_post_ffn):
    L = h1.shape[0]
    tm = _tile(L)
    half = FFN_CHUNK

    def body(hn2_ref, h1_ref, tgt_ref, wgu_hbm, wd_hbm, g2_ref, g3_ref,
             act_ref, dgu_ref, dff_ref, dh1_ref, loss_ref, dg3_ref, dg2_ref,
             wgu, wd, gu, sem):
        first = pl.program_id(0) == 0

        @pl.when(first)
        def _():
            c1 = pltpu.make_async_copy(wgu_hbm, wgu, sem.at[0])
            c2 = pltpu.make_async_copy(wd_hbm, wd, sem.at[1])
            c1.start()
            c2.start()
            c1.wait()
            c2.wait()

        hn2 = hn2_ref[...]
        ff = jnp.zeros((tm, D_MODEL), _F32)
        for c in range(D_FF // half):
            gate = _dot(hn2, wgu[half * c:half * (c + 1), :], _NT)
            up = _dot(hn2, wgu[D_FF + half * c:D_FF + half * (c + 1), :], _NT)
            gu[:, half * c:half * (c + 1)] = gate
            gu[:, D_FF + half * c:D_FF + half * (c + 1)] = up
            act = gate * jax.nn.sigmoid(gate) * up
            act_ref[half * c:half * (c + 1), :] = act.T.astype(_BF16)
            ff = ff + _dot(act, wd[half * c:half * (c + 1), :], _NN)
        g3 = g3_ref[...]
        n, r = _rms_fwd(ff, g3)
        h1 = h1_ref[...]
        err = h1 + n - tgt_ref[...]
        loss = 0.5 * jnp.sum(jnp.mean(err * err, axis=-1, keepdims=True), axis=0, keepdims=True)
        _accumulate(loss_ref, jnp.broadcast_to(loss, (1, 128)), first)
        dh2 = err * (1.0 / D_MODEL)
        dff, dg3 = _rms_bwd(dh2, ff, g3, r)
        _accumulate(dg3_ref, dg3, first)
        dffb = dff.astype(_BF16)
        dff_ref[...] = dffb
        dhn2 = jnp.zeros((tm, D_MODEL), _F32)
        for c in range(D_FF // half):
            dact = _dot(dffb, wd[half * c:half * (c + 1), :], _NT)
            gate = gu[:, half * c:half * (c + 1)]
            up = gu[:, D_FF + half * c:D_FF + half * (c + 1)]
            sig = jax.nn.sigmoid(gate)
            silu = gate * sig
            dgate = dact * up * (sig + silu * (1.0 - sig))
            dup = dact * silu
            dgu_ref[half * c:half * (c + 1), :] = dgate.T.astype(_BF16)
            dgu_ref[D_FF + half * c:D_FF + half * (c + 1), :] = dup.T.astype(_BF16)
            dhn2 = dhn2 + _dot(dgate, wgu[half * c:half * (c + 1), :], _NN)
            dhn2 = dhn2 + _dot(dup, wgu[D_FF + half * c:D_FF + half * (c + 1), :], _NN)
        g2 = g2_ref[...]
        _, r2 = _rms_fwd(h1, g2)
        dh1, dg2 = _rms_bwd(dhn2, h1, g2, r2)
        _accumulate(dg2_ref, dg2, first)
        dh1_ref[...] = dh2 + dh1

    row = _whole((1, D_MODEL))
    anyspace = pl.BlockSpec(memory_space=pl.ANY)
    return _call(body, (L // tm,),
                 [_rows(tm, D_MODEL), _rows(tm, D_MODEL), _rows(tm, D_MODEL), anyspace, anyspace, row, row],
                 [pl.BlockSpec((D_FF, tm), lambda i: (0, i)), pl.BlockSpec((2 * D_FF, tm), lambda i: (0, i)),
                  _rows(tm, D_MODEL), _rows(tm, D_MODEL), _whole((1, 128)), row, row],
                 [_sds((D_FF, L), _BF16), _sds((2 * D_FF, L), _BF16), _sds((L, D_MODEL), _BF16),
                  _sds((L, D_MODEL), _F32), _sds((1, 128), _F32), _sds((1, D_MODEL), _F32), _sds((1, D_MODEL), _F32)],
                 "ffn",
                 scratch=[pltpu.VMEM((2 * D_FF, D_MODEL), _BF16), pltpu.VMEM((D_FF, D_MODEL), _BF16),
                          pltpu.VMEM((tm, 2 * D_FF), _F32), pltpu.SemaphoreType.DMA((2,))],
                 )(hn2, h1, target, w_gate_up, w_down, g_pre_ffn, g_post_ffn)


def _out_proj_bwd(dh1, mo, w_out, g_post_mix, tokens=()):
    L = dh1.shape[0]
    tm = _chunk_tile(L)

    def body(dh1_ref, mo_ref, w_ref, g_ref, dmo_ref, dns_ref, dna_ref, dg_ref):
        first = pl.program_id(0) == 0
        mo = mo_ref[...]
        g = g_ref[...]
        _, r = _rms_fwd(mo, g)
        dmo, dg = _rms_bwd(dh1_ref[...], mo, g, r)
        _accumulate(dg_ref, dg, first)
        dmob = dmo.astype(_BF16)
        dmo_ref[...] = dmob
        dmerged = _dot(dmob, w_ref[...], _NT)
        dns_ref[...] = dmerged[:, :SSM_WIDTH]
        dna_ref[...] = dmerged[:, SSM_WIDTH:]

    row = _whole((1, D_MODEL))
    return _call(body, (L // tm,),
                 [_rows(tm, D_MODEL), _rows(tm, D_MODEL), _whole((D_MODEL, D_MODEL)), row],
                 [_rows(tm, D_MODEL), _chunk_block(L, SSM_WIDTH), _rows(tm, ATTN_WIDTH), row],
                 [_sds((L, D_MODEL), _BF16), _sds(_chunk_shape(L, SSM_WIDTH), _F32), _sds((L, ATTN_WIDTH), _F32),
                  _sds((1, D_MODEL), _F32)],
                 "out_proj_bwd", tokens=tokens)(dh1, mo, w_out, g_post_mix)


def _in_proj_bwd(du, dq, dk, dv, cos_t, sin_t, x, dh1, g_pre_mix, w_in, tokens=()):
    L = x.shape[0]
    tm = _chunk_tile(L)

    def body(du_ref, dq_ref, dk_ref, dv_ref, cos_ref, sin_ref, x_ref, dh1_ref, g_ref, w_ref,
             dproj_ref, dx_ref, dg_ref):
        first = pl.program_id(0) == 0
        cos_v, sin_v = cos_ref[...], sin_ref[...]
        dproj = jnp.concatenate([du_ref[...], _rope_transpose(dq_ref[...], cos_v, sin_v),
                                 _rope_transpose(dk_ref[...], cos_v, sin_v), dv_ref[...]], axis=1).astype(_BF16)
        dproj_ref[...] = dproj
        dhn = _dot(dproj, w_ref[...], _NN)
        x = x_ref[...]
        g = g_ref[...]
        _, r = _rms_fwd(x, g)
        dx, dg = _rms_bwd(dhn, x, g, r)
        _accumulate(dg_ref, dg, first)
        dx_ref[...] = dh1_ref[...] + dx

    row = _whole((1, D_MODEL))
    return _call(body, (L // tm,),
                 [_chunk_block(L, SSM_WIDTH), _rows(tm, ATTN_WIDTH), _rows(tm, KV_WIDTH), _rows(tm, KV_WIDTH),
                  _rows(tm, KV_WIDTH), _rows(tm, KV_WIDTH), _rows(tm, D_MODEL), _rows(tm, D_MODEL), row,
                  _whole((IN_WIDTH, D_MODEL))],
                 [_rows(tm, IN_WIDTH), _rows(tm, D_MODEL), row],
                 [_sds((L, IN_WIDTH), _BF16), _sds((L, D_MODEL), _F32), _sds((1, D_MODEL), _F32)],
                 "in_proj_bwd", tokens=tokens)(du, dq, dk, dv, cos_t, sin_t, x, dh1, g_pre_mix, w_in)


def _matmul_nn(a, b, out_dtype, name):
    M, K = a.shape
    N = b.shape[1]
    tm = next(t for t in (512, 256, 128) if M % t == 0)
    tn = N if N <= D_MODEL else next(t for t in (512, 256, 128) if N % t == 0)

    def body(a_ref, b_ref, o_ref):
        o_ref[...] = _dot(a_ref[...], b_ref[...], _NN).astype(out_dtype)

    params = pltpu.CompilerParams(dimension_semantics=("arbitrary", "arbitrary"), vmem_limit_bytes=VMEM_LIMIT)
    return pl.pallas_call(body, grid=(M // tm, N // tn),
                          in_specs=[pl.BlockSpec((tm, K), lambda i, j: (i, 0)),
                                    pl.BlockSpec((K, tn), lambda i, j: (0, j))],
                          out_specs=pl.BlockSpec((tm, tn), lambda i, j: (i, j)),
                          out_shape=_sds((M, N), out_dtype), compiler_params=params, name=name)(a, b)


def _matmul_tn(a, b, out_dtype, name, scale=1.0):
    K, M = a.shape
    N = b.shape[1]
    tm = next(t for t in (512, 256, 128) if M % t == 0)
    tn = N if N <= D_MODEL else next(t for t in (512, 256, 128) if N % t == 0)

    def body(a_ref, b_ref, o_ref):
        acc = _dot(a_ref[...], b_ref[...], _TN)
        o_ref[...] = (acc if scale == 1.0 else acc * scale).astype(out_dtype)

    params = pltpu.CompilerParams(dimension_semantics=("arbitrary", "arbitrary"), vmem_limit_bytes=VMEM_LIMIT)
    return pl.pallas_call(body, grid=(M // tm, N // tn),
                          in_specs=[pl.BlockSpec((K, tm), lambda i, j: (0, i)),
                                    pl.BlockSpec((K, tn), lambda i, j: (0, j))],
                          out_specs=pl.BlockSpec((tm, tn), lambda i, j: (i, j)),
                          out_shape=_sds((M, N), out_dtype), compiler_params=params, name=name)(a, b)


def _local_step(x, pos, target, p, fetch, publish, progress):
    L = x.shape[0]
    T = L // SCAN_CHUNKS
    cos_t, sin_t = _rope_tables(pos.reshape(L, 1))
    w_in, = fetch(("w_in",), None)
    hn, u, q, k, v = _in_proj(x, p["g_pre_mix"], w_in, cos_t, sin_t)

    ssm = {n: _to_2d(n, p[n]) for n in ("ssm_lambda_re", "ssm_lambda_im", "ssm_log_dt", "ssm_b_re", "ssm_b_im",
                                        "ssm_c_re", "ssm_c_im")}
    d_row = p["ssm_d"].reshape(1, SSM_WIDTH)
    a_re, a_im, bt_re, bt_im, ct_re, ct_im = _ssm_prep(
        ssm["ssm_lambda_re"], ssm["ssm_lambda_im"], ssm["ssm_log_dt"], ssm["ssm_b_re"], ssm["ssm_b_im"],
        ssm["ssm_c_re"], ssm["ssm_c_im"])

    u_c = u.reshape(L, SSM_WIDTH)
    cx = _ssm_core_fwd(u_c, bt_re, bt_im, ct_re, ct_im, a_re, a_im)
    w_glu, = fetch(("w_glu",), cx)
    y, z, n_ssm_c = _ssm_out(cx, u_c, d_row, w_glu, p["b_glu"], p["g_ssm_out"])
    n_ssm = n_ssm_c.reshape(_chunk_shape(L, SSM_WIDTH))

    sinks = p["attn_sinks"].reshape(N_Q_HEADS)
    o, n_attn = _attn_fwd(q, k, v, sinks, p["g_attn_out"])
    w_out, = fetch(("w_out",), n_attn)
    merged, mo, h1, hn2 = _out_proj(n_ssm, n_attn, x, w_out, p["g_post_mix"], p["g_pre_ffn"])
    w_gate_up, w_down = fetch(("w_gate_up", "w_down"), hn2)
    act_t, dgu_t, dff, dh1, loss, dg_post_ffn, dg_pre_ffn = _ffn(
        hn2, h1, target, w_gate_up, w_down, p["g_pre_ffn"], p["g_post_ffn"])
    grads = {"g_post_ffn": dg_post_ffn, "g_pre_ffn": dg_pre_ffn}
    tokens = publish({"w_down": _matmul_nn(act_t, dff, _BF16, "grad_w_down"),
                      "w_gate_up": _matmul_nn(dgu_t, hn2, _BF16, "grad_w_gate_up")})

    dmo, dn_ssm, dn_attn, grads["g_post_mix"] = _out_proj_bwd(dh1, mo, w_out, p["g_post_mix"], tokens)
    grad_w_out = _matmul_tn(merged, dmo, _BF16, "grad_w_out")

    dq, dk, dv, dsink, grads["g_attn_out"] = _attn_bwd(q, k, v, o, dn_attn, sinks, p["g_attn_out"])
    grads["attn_sinks"] = dsink

    gy, dz, dy, dud, grads["g_ssm_out"], grads["b_glu"], dd = _ssm_out_bwd(
        dn_ssm.reshape(L, SSM_WIDTH), y, z, u_c, d_row, w_glu, p["g_ssm_out"])
    tokens = progress(dy)
    tokens += publish({"w_out": grad_w_out, "w_glu": _matmul_tn(dz, gy, _BF16, "grad_w_glu")})
    du_c, dct_re, dct_im, dbt_re, dbt_im, da_re, da_im = _ssm_core_bwd(
        u_c, dy, dud, bt_re, bt_im, ct_re, ct_im, a_re, a_im, tokens)
    ssm_pack = _ssm_param_bwd(
        da_re, da_im, dbt_re, dbt_im, dct_re, dct_im,
        ssm["ssm_lambda_re"], ssm["ssm_lambda_im"], ssm["ssm_log_dt"], ssm["ssm_b_re"], ssm["ssm_b_im"],
        dd.reshape(SSM_GROUPS, SSM_GROUP))
    grads.update(ssm_pack=ssm_pack, loss=loss)
    publish(grads)

    du = du_c.reshape(_chunk_shape(L, SSM_WIDTH))
    dproj, grad_x, g_pre_mix = _in_proj_bwd(du, dq, dk, dv, cos_t, sin_t, x, dh1, p["g_pre_mix"], w_in, [ssm_pack])
    publish({"g_pre_mix": g_pre_mix, "w_in": _matmul_tn(dproj, hn, _BF16, "grad_w_in")})
    return grad_x


_MESH = pl.DeviceIdType.MESH
_PEERS = N_DEV - 1


def _mesh_pos():
    return lax.axis_index("x"), lax.axis_index("y"), lax.axis_index("c")


def _dev_index(px, py, pc):
    return 4 * px + 2 * py + pc


def _peer(x, y, c, r):
    return (x ^ ((r >> 2) & 1), y ^ ((r >> 1) & 1), c ^ (r & 1))


def _sequencer_exchange(sources, blocked, name, collective_id):
    n = len(sources)
    flags = blocked

    def body(*refs):
        srcs, zones = refs[:n], refs[n:2 * n]
        send_sems, recv_sems, local_sems = refs[2 * n:]
        x, y, c = _mesh_pos()
        me = _dev_index(x, y, c)
        barrier = pltpu.get_barrier_semaphore()
        for r in range(1, N_DEV):
            pl.semaphore_signal(barrier, inc=1, device_id=_peer(x, y, c, r), device_id_type=_MESH)
        pl.semaphore_wait(barrier, _PEERS)
        local, sends, recvs = [], [], []
        for w in range(n):
            cp = pltpu.make_async_copy(srcs[w].at[me] if flags[w] else srcs[w], zones[w].at[me], local_sems.at[w])
            cp.start()
            local.append(cp)
            for r in range(1, N_DEV):
                peer = _peer(x, y, c, r)
                idx = _dev_index(*peer)
                k = _PEERS * w + r - 1
                src = srcs[w].at[idx] if flags[w] else srcs[w]
                send = pltpu.make_async_remote_copy(
                    src_ref=src, dst_ref=zones[w].at[me], send_sem=send_sems.at[k], recv_sem=recv_sems.at[k],
                    device_id=peer, device_id_type=_MESH)
                send.start()
                sends.append(send)
                recvs.append(pltpu.make_async_remote_copy(
                    src_ref=src, dst_ref=zones[w].at[idx], send_sem=send_sems.at[k], recv_sem=recv_sems.at[k],
                    device_id=peer, device_id_type=_MESH))
        for cp in recvs:
            cp.wait_recv()
        for cp in sends:
            cp.wait_send()
        for cp in local:
            cp.wait()

    return pl.kernel(
        body, name=name,
        out_type=[_sds((N_DEV,) + (s.shape[1:] if f else s.shape), s.dtype) for s, f in zip(sources, flags)],
        mesh=plsc.ScalarSubcoreMesh(axis_name="sequencer", num_cores=1),
        scratch_types=[pltpu.SemaphoreType.DMA((_PEERS * n,)), pltpu.SemaphoreType.DMA((_PEERS * n,)),
                       pltpu.SemaphoreType.DMA((n,))],
        compiler_params=pltpu.CompilerParams(collective_id=collective_id),
    )(*sources)


def _sequencer_gather(shards, name, collective_id):
    n = len(shards)
    fan = 4

    def body(*refs):
        srcs, zones = refs[:n], refs[n:2 * n]
        send_sems, recv_sems, local_sems = refs[2 * n:]
        x, y, c = _mesh_pos()
        me, sibling = (x, y, c), (x, y, 1 - c)
        chips = [(1 - x, y), (x, 1 - y), (1 - x, 1 - y)]
        barrier = pltpu.get_barrier_semaphore()
        for peer in [sibling] + [(*chip, c) for chip in chips]:
            pl.semaphore_signal(barrier, inc=1, device_id=peer, device_id_type=_MESH)
        pl.semaphore_wait(barrier, fan)

        def copy(w, k, block, to, src=None):
            slot = zones[w].at[_dev_index(*block)]
            return pltpu.make_async_remote_copy(
                src_ref=slot if src is None else src, dst_ref=slot,
                send_sem=send_sems.at[_PEERS * w + k], recv_sem=recv_sems.at[_PEERS * w + k],
                device_id=to, device_id_type=_MESH)

        mine, first, passed = [], [], []
        for w in range(n):
            cp = pltpu.make_async_copy(srcs[w], zones[w].at[_dev_index(*me)], local_sems.at[w])
            cp.start()
            mine.append(cp)
            sends = [copy(w, 0, me, sibling, src=srcs[w])]
            sends += [copy(w, 1 + j, me, (*chip, c), src=srcs[w]) for j, chip in enumerate(chips)]
            for cp in sends:
                cp.start()
            first += sends
        for w in range(n):
            for j, chip in enumerate(chips):
                copy(w, 1 + j, (*chip, c), me).wait_recv()
                cp = copy(w, fan + j, (*chip, c), sibling)
                cp.start()
                passed.append(cp)
        for w in range(n):
            copy(w, 0, sibling, me).wait_recv()
            for j, chip in enumerate(chips):
                copy(w, fan + j, (*chip, 1 - c), me).wait_recv()
        for cp in first + passed:
            cp.wait_send()
        for cp in mine:
            cp.wait()

    return pl.kernel(
        body, name=name, out_type=[_sds((N_DEV,) + s.shape, s.dtype) for s in shards],
        mesh=plsc.ScalarSubcoreMesh(axis_name="sequencer", num_cores=1),
        scratch_types=[pltpu.SemaphoreType.DMA((_PEERS * n,)), pltpu.SemaphoreType.DMA((_PEERS * n,)),
                       pltpu.SemaphoreType.DMA((n,))],
        compiler_params=pltpu.CompilerParams(collective_id=collective_id),
    )(*shards)


N_CHIPS = N_DEV // 2


def _sequencer_pair_exchange(sources, name, collective_id):
    n = len(sources)

    def body(*refs):
        srcs, zones = refs[:n], refs[n:2 * n]
        send_sems, recv_sems = refs[2 * n:]
        x, y, c = _mesh_pos()
        sibling = (x, y, 1 - c)
        barrier = pltpu.get_barrier_semaphore()
        pl.semaphore_signal(barrier, inc=1, device_id=sibling, device_id_type=_MESH)
        pl.semaphore_wait(barrier, 1)
        copies = []
        for w in range(n):
            for j in range(N_CHIPS):
                k = N_CHIPS * w + j
                cp = pltpu.make_async_remote_copy(
                    src_ref=srcs[w].at[2 * j + 1 - c], dst_ref=zones[w].at[j],
                    send_sem=send_sems.at[k], recv_sem=recv_sems.at[k], device_id=sibling, device_id_type=_MESH)
                cp.start()
                copies.append(cp)
        for cp in copies:
            cp.wait_recv()
        for cp in copies:
            cp.wait_send()

    return pl.kernel(
        body, name=name, out_type=[_sds((N_CHIPS,) + s.shape[1:], s.dtype) for s in sources],
        mesh=plsc.ScalarSubcoreMesh(axis_name="sequencer", num_cores=1),
        scratch_types=[pltpu.SemaphoreType.DMA((N_CHIPS * n,)), pltpu.SemaphoreType.DMA((N_CHIPS * n,))],
        compiler_params=pltpu.CompilerParams(collective_id=collective_id),
    )(*sources)


def _pair_sum(source, received, core, name, tokens=()):
    _, rows, cols = source.shape
    tr = _row_tile(rows)

    def body(core_ref, s_ref, r_ref, o_ref):
        c = core_ref[0]
        for j in range(N_CHIPS):
            o_ref[j] = (s_ref[2 * j + c].astype(_F32) + r_ref[j].astype(_F32)).astype(o_ref.dtype)

    return _call(body, (rows // tr,),
                 [pl.BlockSpec(memory_space=pltpu.SMEM), pl.BlockSpec((N_DEV, tr, cols), lambda i: (0, i, 0)),
                  pl.BlockSpec((N_CHIPS, tr, cols), lambda i: (0, i, 0))],
                 pl.BlockSpec((N_CHIPS, tr, cols), lambda i: (0, i, 0)),
                 _sds((N_CHIPS, rows, cols), source.dtype), name, tokens=tokens)(core, source, received)


def _sequencer_chip_exchange(partials, name, collective_id):
    n = len(partials)
    others = N_CHIPS - 1

    def body(*refs):
        srcs, zones = refs[:n], refs[n:2 * n]
        send_sems, recv_sems, local_sems = refs[2 * n:]
        x, y, c = _mesh_pos()
        mine = 2 * x + y
        peers = [(x ^ (r >> 1), y ^ (r & 1), c) for r in range(1, N_CHIPS)]
        barrier = pltpu.get_barrier_semaphore()
        for peer in peers:
            pl.semaphore_signal(barrier, inc=1, device_id=peer, device_id_type=_MESH)
        pl.semaphore_wait(barrier, others)
        local, sends, recvs = [], [], []
        for w in range(n):
            cp = pltpu.make_async_copy(srcs[w].at[mine], zones[w].at[mine], local_sems.at[w])
            cp.start()
            local.append(cp)
            for r, peer in enumerate(peers):
                theirs = 2 * peer[0] + peer[1]
                k = others * w + r
                send = pltpu.make_async_remote_copy(
                    src_ref=srcs[w].at[theirs], dst_ref=zones[w].at[mine],
                    send_sem=send_sems.at[k], recv_sem=recv_sems.at[k], device_id=peer, device_id_type=_MESH)
                send.start()
                sends.append(send)
                recvs.append(pltpu.make_async_remote_copy(
                    src_ref=srcs[w].at[theirs], dst_ref=zones[w].at[theirs],
                    send_sem=send_sems.at[k], recv_sem=recv_sems.at[k], device_id=peer, device_id_type=_MESH))
        for cp in recvs:
            cp.wait_recv()
        for cp in sends:
            cp.wait_send()
        for cp in local:
            cp.wait()

    return pl.kernel(
        body, name=name, out_type=[_sds(s.shape, s.dtype) for s in partials],
        mesh=plsc.ScalarSubcoreMesh(axis_name="sequencer", num_cores=1),
        scratch_types=[pltpu.SemaphoreType.DMA((others * n,)), pltpu.SemaphoreType.DMA((others * n,)),
                       pltpu.SemaphoreType.DMA((n,))],
        compiler_params=pltpu.CompilerParams(collective_id=collective_id),
    )(*partials)


def _row_tile(rows):
    return next(t for t in range(min(rows, 256), 0, -16) if rows % t == 0)


def _sum_parts(parts, name, tokens=()):
    _, rows, cols = parts.shape
    tr = _row_tile(rows)

    def body(p_ref, g_ref):
        g = p_ref[0].astype(_F32)
        for s in range(1, N_DEV):
            g = g + p_ref[s].astype(_F32)
        g_ref[...] = g

    return _call(body, (rows // tr,), [pl.BlockSpec((N_DEV, tr, cols), lambda i: (0, i, 0))],
                 _rows(tr, cols), _sds((rows, cols), _F32), name, tokens=tokens)(parts)


def _adam_update(g, w, m, v):
    new_m = ADAM_B1 * m + (1.0 - ADAM_B1) * g
    new_v = ADAM_B2 * v + (1.0 - ADAM_B2) * (g * g)
    m_hat = new_m / (1.0 - ADAM_B1 ** ADAM_STEP)
    v_hat = new_v / (1.0 - ADAM_B2 ** ADAM_STEP)
    return -ADAM_LR * (m_hat / (jnp.sqrt(v_hat) + ADAM_EPS) + ADAM_WD * w), new_m, new_v


def _adamw_small(parts, items, sums, name, tokens=()):
    n_p, n_i = len(parts), len(items)

    def body(*refs):
        p_refs, state, outs = refs[:n_p], refs[n_p:n_p + 3 * n_i], refs[n_p + 3 * n_i:]

        def total(part, rows, cols):
            shift = cols.start % _LANES
            window = slice(cols.start - shift, cols.start - shift + _LANES) if shift else cols
            n_rows = rows.stop - rows.start
            narrow = p_refs[part].dtype.itemsize < 4 and n_rows % _PACK_TILE
            tile = slice(rows.start, rows.start + _PACK_TILE) if narrow else rows
            g = p_refs[part][0, tile, window].astype(_F32)
            for s in range(1, N_DEV):
                g = g + p_refs[part][s, tile, window].astype(_F32)
            g = g[:n_rows] if narrow else g
            return pltpu.roll(g, _LANES - shift, 1)[:, :cols.stop - cols.start] if shift else g

        for i, (part, rows, cols, _, _, _) in enumerate(items):
            g = total(part, rows, cols)
            w_ref, m_ref, v_ref = state[3 * i:3 * i + 3]
            delta, new_m, new_v = _adam_update(g, w_ref[...], m_ref[...], v_ref[...])
            outs[4 * i][...] = g
            outs[4 * i + 1][...] = delta
            outs[4 * i + 2][...] = new_m
            outs[4 * i + 3][...] = new_v
        for j, (part, rows, cols) in enumerate(sums):
            outs[4 * n_i + j][...] = total(part, rows, cols)

    ins = list(parts) + [a for item in items for a in item[3:]]
    out_shapes = [item[3].shape for item in items for _ in range(4)]
    out_shapes += [(rows.stop - rows.start, cols.stop - cols.start) for _, rows, cols in sums]
    out = _call(body, (1,), [_whole(a.shape) for a in ins], [_whole(s) for s in out_shapes],
                [_sds(s, _F32) for s in out_shapes], name, tokens=tokens)(*ins)
    return [out[4 * i:4 * i + 4] for i in range(n_i)], out[4 * n_i:]


def _adamw(parts, w, m, v, name, tokens=()):
    rows, cols = w.shape
    tr = _row_tile(rows)
    n_parts = parts.shape[0]

    def body(p_ref, w_ref, m_ref, v_ref, g_ref, d_ref, nm_ref, nv_ref):
        g = p_ref[0].astype(_F32)
        for s in range(1, n_parts):
            g = g + p_ref[s].astype(_F32)
        new_m = ADAM_B1 * m_ref[...] + (1.0 - ADAM_B1) * g
        new_v = ADAM_B2 * v_ref[...] + (1.0 - ADAM_B2) * (g * g)
        m_hat = new_m / (1.0 - ADAM_B1 ** ADAM_STEP)
        v_hat = new_v / (1.0 - ADAM_B2 ** ADAM_STEP)
        g_ref[...] = g
        d_ref[...] = -ADAM_LR * (m_hat / (jnp.sqrt(v_hat) + ADAM_EPS) + ADAM_WD * w_ref[...])
        nm_ref[...] = new_m
        nv_ref[...] = new_v

    blk = _rows(tr, cols)
    return _call(body, (rows // tr,),
                 [pl.BlockSpec((n_parts, tr, cols), lambda i: (0, i, 0)), blk, blk, blk],
                 [blk] * 4, [_sds((rows, cols), _F32)] * 4, name, tokens=tokens)(parts, w, m, v)


_SMALL = ("g_pre_mix", "ssm_lambda_re", "ssm_lambda_im", "ssm_log_dt", "ssm_b_re", "ssm_b_im",
          "ssm_c_re", "ssm_c_im", "ssm_d", "b_glu", "attn_sinks", "g_ssm_out", "g_attn_out",
          "g_post_mix", "g_pre_ffn", "g_post_ffn")
_BIG = ("w_in", "w_glu", "w_out", "w_gate_up", "w_down")
_WEIGHTS = ("g_pre_mix", "w_in", "ssm_lambda_re", "ssm_lambda_im", "ssm_log_dt", "ssm_b_re", "ssm_b_im",
            "ssm_c_re", "ssm_c_im", "ssm_d", "w_glu", "b_glu", "attn_sinks", "g_ssm_out", "g_attn_out",
            "w_out", "g_post_mix", "g_pre_ffn", "w_gate_up", "w_down", "g_post_ffn")
_LANES = 128


_SHAPE_2D = {
    "g_pre_mix": (1, D_MODEL), "ssm_lambda_re": (SSM_GROUPS, SSM_STATE), "ssm_lambda_im": (SSM_GROUPS, SSM_STATE),
    "ssm_log_dt": (1, SSM_GROUPS), "ssm_b_re": (SSM_WIDTH, SSM_STATE), "ssm_b_im": (SSM_WIDTH, SSM_STATE),
    "ssm_c_re": (SSM_WIDTH, SSM_STATE), "ssm_c_im": (SSM_WIDTH, SSM_STATE), "ssm_d": (SSM_GROUPS, SSM_GROUP),
    "b_glu": (1, 2 * SSM_WIDTH), "attn_sinks": (1, N_Q_HEADS), "g_ssm_out": (1, SSM_WIDTH),
    "g_attn_out": (1, ATTN_WIDTH), "g_post_mix": (1, D_MODEL), "g_pre_ffn": (1, D_MODEL), "g_post_ffn": (1, D_MODEL)}
_ROW_WIDTH = {"g_pre_mix": D_MODEL, "b_glu": 2 * SSM_WIDTH, "attn_sinks": _LANES, "g_ssm_out": SSM_WIDTH,
              "g_attn_out": ATTN_WIDTH, "g_post_mix": D_MODEL, "g_pre_ffn": D_MODEL, "g_post_ffn": D_MODEL,
              "loss": _LANES}
_PER_GROUP_TRANSPOSED = ("ssm_b_re", "ssm_b_im")


def _to_2d(name, a):
    if name in _PER_GROUP_TRANSPOSED:
        a = a.reshape(SSM_GROUPS, SSM_STATE, SSM_GROUP).transpose(0, 2, 1)
    return a.reshape(_SHAPE_2D[name])


def _from_2d(name, a, shape):
    if name in _PER_GROUP_TRANSPOSED:
        a = a.reshape(SSM_GROUPS, SSM_GROUP, SSM_STATE).transpose(0, 2, 1)
    return a.reshape(shape)


def _row_slots(names):
    slots, row, col = {}, 0, 0
    for n in names:
        width = _ROW_WIDTH[n]
        if col + width > D_MODEL:
            row, col = row + 1, 0
        slots[n] = (row, col, width)
        col += width
    return slots


def _stack_rows(named, slots):
    n_rows = -(-(max(r for r, _, _ in slots.values()) + 1) // 8) * 8
    lines = []
    for r in range(n_rows):
        pieces = [named[n] for n, (row, _, _) in slots.items() if row == r]
        used = sum(p.shape[1] for p in pieces)
        if used < D_MODEL:
            pieces.append(jnp.zeros((1, D_MODEL - used), _F32))
        lines.append(jnp.concatenate(pieces, axis=1) if len(pieces) > 1 else pieces[0])
    return jnp.concatenate(lines, axis=0)


def kernel(x, positions, g_pre_mix, w_in, ssm_lambda_re, ssm_lambda_im, ssm_log_dt, ssm_b_re, ssm_b_im, ssm_c_re, ssm_c_im, ssm_d, w_glu, b_glu, attn_sinks, g_ssm_out, g_attn_out, w_out, g_post_mix, g_pre_ffn, w_gate_up, w_down, g_post_ffn, loss_target, m_g_pre_mix, m_w_in, m_ssm_lambda_re, m_ssm_lambda_im, m_ssm_log_dt, m_ssm_b_re, m_ssm_b_im, m_ssm_c_re, m_ssm_c_im, m_ssm_d, m_w_glu, m_b_glu, m_attn_sinks, m_g_ssm_out, m_g_attn_out, m_w_out, m_g_post_mix, m_g_pre_ffn, m_w_gate_up, m_w_down, m_g_post_ffn, v_g_pre_mix, v_w_in, v_ssm_lambda_re, v_ssm_lambda_im, v_ssm_log_dt, v_ssm_b_re, v_ssm_b_im, v_ssm_c_re, v_ssm_c_im, v_ssm_d, v_w_glu, v_b_glu, v_attn_sinks, v_g_ssm_out, v_g_attn_out, v_w_out, v_g_post_mix, v_g_pre_ffn, v_w_gate_up, v_w_down, v_g_post_ffn):
    w = dict(g_pre_mix=g_pre_mix, w_in=w_in, ssm_lambda_re=ssm_lambda_re, ssm_lambda_im=ssm_lambda_im,
             ssm_log_dt=ssm_log_dt, ssm_b_re=ssm_b_re, ssm_b_im=ssm_b_im, ssm_c_re=ssm_c_re, ssm_c_im=ssm_c_im,
             ssm_d=ssm_d, w_glu=w_glu, b_glu=b_glu, attn_sinks=attn_sinks, g_ssm_out=g_ssm_out,
             g_attn_out=g_attn_out, w_out=w_out, g_post_mix=g_post_mix, g_pre_ffn=g_pre_ffn,
             w_gate_up=w_gate_up, w_down=w_down, g_post_ffn=g_post_ffn)
    m = dict(g_pre_mix=m_g_pre_mix, w_in=m_w_in, ssm_lambda_re=m_ssm_lambda_re, ssm_lambda_im=m_ssm_lambda_im,
             ssm_log_dt=m_ssm_log_dt, ssm_b_re=m_ssm_b_re, ssm_b_im=m_ssm_b_im, ssm_c_re=m_ssm_c_re,
             ssm_c_im=m_ssm_c_im, ssm_d=m_ssm_d, w_glu=m_w_glu, b_glu=m_b_glu, attn_sinks=m_attn_sinks,
             g_ssm_out=m_g_ssm_out, g_attn_out=m_g_attn_out, w_out=m_w_out, g_post_mix=m_g_post_mix,
             g_pre_ffn=m_g_pre_ffn, w_gate_up=m_w_gate_up, w_down=m_w_down, g_post_ffn=m_g_post_ffn)
    v = dict(g_pre_mix=v_g_pre_mix, w_in=v_w_in, ssm_lambda_re=v_ssm_lambda_re, ssm_lambda_im=v_ssm_lambda_im,
             ssm_log_dt=v_ssm_log_dt, ssm_b_re=v_ssm_b_re, ssm_b_im=v_ssm_b_im, ssm_c_re=v_ssm_c_re,
             ssm_c_im=v_ssm_c_im, ssm_d=v_ssm_d, w_glu=v_w_glu, b_glu=v_b_glu, attn_sinks=v_attn_sinks,
             g_ssm_out=v_g_ssm_out, g_attn_out=v_g_attn_out, w_out=v_w_out, g_post_mix=v_g_post_mix,
             g_pre_ffn=v_g_pre_ffn, w_gate_up=v_w_gate_up, w_down=v_w_down, g_post_ffn=v_g_post_ffn)

    transposed = ("w_in", "w_glu", "w_gate_up")
    native_transposed = ("w_in", "w_gate_up")
    shard = {n: (w[n][0].T if n in transposed else w[n][0]).astype(_BF16) for n in _BIG}
    gathered = {}
    for cid, names in enumerate((("w_in",), ("w_glu", "w_out"), ("w_gate_up", "w_down")), start=1):
        lands = _sequencer_gather([shard[n] for n in names], "gather_" + names[0], cid)
        gathered.update({n: a.reshape(-1, a.shape[2]) for n, a in zip(names, lands)})

    def fetch(names, after):
        del after
        return [gathered[n] for n in names]

    sent = []
    ids = iter(range(4, 16))
    two_step = {}

    def publish(named):
        big = [n for n in named if n in _BIG]
        if set(big) == {"w_gate_up", "w_down"}:
            blocks = [named[n].reshape(N_DEV, -1, named[n].shape[1]) for n in big]
            two_step.update(names=big, blocks=blocks,
                            received=_sequencer_pair_exchange(blocks, "grads_pair", next(ids)))
            return [named[n] for n in big]
        rows = [n for n in named if n in _ROW_WIDTH]
        plain = [n for n in named if n not in big + rows]
        sources = [named[n].reshape(N_DEV, -1, named[n].shape[1]) for n in big]
        slots = _row_slots(rows)
        if rows:
            sources.append(_stack_rows(named, slots))
        sources += [named[n] for n in plain]
        flags = [True] * len(big) + [False] * (len(sources) - len(big))
        cid = next(ids)
        sent.append((big, slots, plain, _sequencer_exchange(sources, flags, "grads_%d" % cid, cid)))
        return [named[n] for n in big]

    def progress(after):
        core = lax.axis_index("c").astype(jnp.int32).reshape(1)
        partials = [_pair_sum(b, r, core, "pair_sum_" + n, [after])
                    for n, b, r in zip(two_step["names"], two_step["blocks"], two_step["received"])]
        sent.append((two_step["names"], {}, [], _sequencer_chip_exchange(partials, "grads_chips", next(ids))))
        return partials

    p = {n: w[n] for n in _SMALL}
    grad_x = _local_step(x[0], positions[0], loss_target[0], p, fetch, publish, progress)

    state = {n: [_to_2d(n, a) for a in (w[n], m[n], v[n])] for n in _SMALL}
    result = {}
    total_loss = None
    chain = []
    for big, slots, plain, lands in sent:
        lands = list(lands)
        after = list(chain)
        for name in big:
            part = lands.pop(0)
            if name in native_transposed:
                updated = _adamw(part, w[name][0].T, m[name][0].T, v[name][0].T, "adamw_" + name, after)
                result[name] = [a.T[None] for a in updated]
                chain.append(updated[3])
                continue
            if name in transposed:
                part = _sum_parts(part, "sum_" + name, after).T[None]
            updated = _adamw(part, w[name][0], m[name][0], v[name][0], "adamw_" + name, after)
            result[name] = [a[None] for a in updated]
            chain.append(updated[3])
        parts, items, sums, names = [], [], [], []
        if slots:
            parts.append(lands.pop(0))
            for name, (row, col, _) in slots.items():
                if name == "loss":
                    sums.append((0, slice(row, row + 1), slice(col, col + _LANES)))
                else:
                    items.append((0, slice(row, row + 1), slice(col, col + _SHAPE_2D[name][1]), *state[name]))
                    names.append(name)
        for name in plain:
            packed = _SSM_PACK if name == "ssm_pack" else {name: (0, _SHAPE_2D[name][0], 0, _SHAPE_2D[name][1])}
            for member, (first, rows_n, lane, cols_n) in packed.items():
                items.append((len(parts), slice(first, first + rows_n), slice(lane, lane + cols_n), *state[member]))
                names.append(member)
            parts.append(lands.pop(0))
        if items:
            updated, summed = _adamw_small(parts, items, sums, "adamw_small_" + names[0], after)
            chain.append(updated[0][3])
            result.update(dict(zip(names, updated)))
            if summed:
                total_loss = summed[0][0, 0]

    out = [total_loss, grad_x[None]]
    for kind in range(4):
        out += [_from_2d(n, result[n][kind], w[n].shape) for n in _WEIGHTS]
    return tuple(out)
```

```python
import math

import numpy as np
import jax
import jax.numpy as jnp
from jax import lax
from jax.experimental import pallas as pl
from jax.experimental.pallas import tpu as pltpu
from jax.experimental.pallas import tpu_sc as plsc

D_MODEL = 1024
SSM_WIDTH = 512
SSM_GROUP = 16
SSM_GROUPS = 32
SSM_STATE = 64
N_STATE = SSM_GROUPS * SSM_STATE
ATTN_WIDTH = 512
HEAD_DIM = 64
N_Q_HEADS = 8
N_KV_HEADS = 2
Q_PER_KV = 4
KV_WIDTH = 128
IN_WIDTH = 1280
BLOCK = 128
ROPE_DIM = 16
ROPE_THETA = 500000.0
D_FF = 2816
NORM_EPS = 1e-6
MASK_VALUE = -1e30
ADAM_LR = 0.001
ADAM_B1 = 0.9
ADAM_B2 = 0.999
ADAM_EPS = 1e-08
ADAM_WD = 0.01
ADAM_STEP = 10

N_DEV = 8
SCAN_CHUNKS = 8
SCAN_UNROLL = 8
FFN_CHUNK = 2816
TOKEN_TILE = 256
VMEM_LIMIT = 56 * 1024 * 1024

_F32 = jnp.float32
_BF16 = jnp.bfloat16
_MXU = jnp.bfloat16

_NN = ((1,), (0,))
_NT = ((1,), (1,))
_TN = ((0,), (0,))


def _dot(a, b, dims):
    return lax.dot_general(a.astype(_MXU), b.astype(_MXU), (dims, ((), ())),
                           preferred_element_type=_F32)


def _dot_exact(a, b, dims):
    return lax.dot_general(a.astype(_F32), b.astype(_F32), (dims, ((), ())),
                           precision=lax.Precision.HIGHEST, preferred_element_type=_F32)


def _iota(shape, dim):
    return lax.broadcasted_iota(jnp.int32, shape, dim)


def _rms_fwd(x, g):
    r = lax.rsqrt(jnp.mean(x * x, axis=-1, keepdims=True) + NORM_EPS)
    return x * r * g, r


def _rms_bwd(dy, x, g, r):
    a = dy * g
    xn = x * r
    dx = r * (a - xn * jnp.mean(a * xn, axis=-1, keepdims=True))
    dg = jnp.sum(dy * xn, axis=0, keepdims=True)
    return dx, dg


def _call(body, grid, in_specs, out_specs, out_shape, name, scratch=(), tokens=()):
    params = pltpu.CompilerParams(dimension_semantics=("arbitrary",) * len(grid),
                                  vmem_limit_bytes=VMEM_LIMIT)
    n_in, n_tok = len(in_specs), len(tokens)

    def run(*refs):
        return body(*refs[:n_in], *refs[n_in + n_tok:])

    call = pl.pallas_call(run, grid=grid,
                          in_specs=list(in_specs) + [pl.BlockSpec(memory_space=pl.ANY)] * n_tok,
                          out_specs=out_specs, out_shape=out_shape, scratch_shapes=list(scratch),
                          compiler_params=params, name=name)
    return lambda *args: call(*args, *tokens)


def _rows(tm, n):
    return pl.BlockSpec((tm, n), lambda i: (i, 0))


def _whole(shape):
    nd = len(shape)
    return pl.BlockSpec(shape, lambda i: (0,) * nd)


def _sds(shape, dtype):
    return jax.ShapeDtypeStruct(shape, dtype)


def _tile(L):
    return min(TOKEN_TILE, L)


def _chunk_tile(L):
    return L // SCAN_CHUNKS


def _chunk_block(L, n):
    return pl.BlockSpec((_chunk_tile(L), n), lambda i: (0, i))


def _chunk_shape(L, n):
    return (_chunk_tile(L), SCAN_CHUNKS * n)


def _accumulate(ref, val, first):
    @pl.when(first)
    def _():
        ref[...] = val

    @pl.when(jnp.logical_not(first))
    def _():
        ref[...] += val


def _rope_rows():
    half = ROPE_DIM // 2
    inv = (np.float32(ROPE_THETA) ** (-np.arange(half, dtype=np.float32) * np.float32(2.0) / np.float32(ROPE_DIM))).astype(np.float32)
    col = np.arange(KV_WIDTH) % HEAD_DIM
    freq = np.where(col < ROPE_DIM, inv[col % half], 0.0).astype(np.float32)
    sign = np.where(col < half, -1.0, np.where(col < ROPE_DIM, 1.0, 0.0)).astype(np.float32)
    return freq[None, :], sign[None, :]


def _rope_tables(pos_col):
    L = pos_col.shape[0]
    tm = _tile(L)
    freq, sign = _rope_rows()

    def body(pos_ref, freq_ref, sign_ref, cos_ref, sin_ref):
        ang = pos_ref[...].astype(_F32) * freq_ref[...]
        cos_ref[...] = jnp.cos(ang)
        sin_ref[...] = jnp.sin(ang) * sign_ref[...]

    return _call(body, (L // tm,),
                 [_rows(tm, 1), _whole((1, KV_WIDTH)), _whole((1, KV_WIDTH))],
                 [_rows(tm, KV_WIDTH), _rows(tm, KV_WIDTH)],
                 [_sds((L, KV_WIDTH), _F32)] * 2, "rope_tables")(pos_col, jnp.asarray(freq), jnp.asarray(sign))


def _widen(t, width):
    return t if width == KV_WIDTH else jnp.concatenate([t] * (width // KV_WIDTH), axis=1)


def _rope_partner(t):
    w = t.shape[1]
    in_head = _iota((1, w), 1) & (HEAD_DIM - 1)
    second = jnp.where(in_head < ROPE_DIM, pltpu.roll(t, ROPE_DIM // 2, 1), 0.0)
    return jnp.where(in_head < ROPE_DIM // 2, pltpu.roll(t, w - ROPE_DIM // 2, 1), second)


def _rope_apply(t, cos_t, sin_t):
    w = t.shape[1]
    return t * _widen(cos_t, w) + _rope_partner(t) * _widen(sin_t, w)


def _rope_transpose(dt, cos_t, sin_t):
    w = dt.shape[1]
    return dt * _widen(cos_t, w) + _rope_partner(dt * _widen(sin_t, w))


def _in_proj(x, g_pre_mix, w_in, cos_t, sin_t):
    L = x.shape[0]
    tm = _chunk_tile(L)

    def body(x_ref, g_ref, w_ref, cos_ref, sin_ref, hn_ref, u_ref, q_ref, k_ref, v_ref):
        hn, _ = _rms_fwd(x_ref[...], g_ref[...])
        hn = hn.astype(_BF16)
        hn_ref[...] = hn
        proj = _dot(hn, w_ref[...], _NT)
        u_ref[...] = proj[:, :SSM_WIDTH]
        q = proj[:, SSM_WIDTH:SSM_WIDTH + ATTN_WIDTH]
        k = proj[:, SSM_WIDTH + ATTN_WIDTH:SSM_WIDTH + ATTN_WIDTH + KV_WIDTH]
        cos_v, sin_v = cos_ref[...], sin_ref[...]
        q_ref[...] = _rope_apply(q, cos_v, sin_v).astype(_BF16)
        k_ref[...] = _rope_apply(k, cos_v, sin_v).astype(_BF16)
        v_ref[...] = proj[:, SSM_WIDTH + ATTN_WIDTH + KV_WIDTH:].astype(_BF16)

    return _call(body, (L // tm,),
                 [_rows(tm, D_MODEL), _whole((1, D_MODEL)), _whole((IN_WIDTH, D_MODEL)),
                  _rows(tm, KV_WIDTH), _rows(tm, KV_WIDTH)],
                 [_rows(tm, D_MODEL), _chunk_block(L, SSM_WIDTH), _rows(tm, ATTN_WIDTH),
                  _rows(tm, KV_WIDTH), _rows(tm, KV_WIDTH)],
                 [_sds((L, D_MODEL), _BF16), _sds(_chunk_shape(L, SSM_WIDTH), _F32), _sds((L, ATTN_WIDTH), _BF16),
                  _sds((L, KV_WIDTH), _BF16), _sds((L, KV_WIDTH), _BF16)],
                 "in_proj")(x, g_pre_mix, w_in, cos_t, sin_t)


def _s5_discretize(lam_re, lam_im, log_dt):
    lr = jnp.minimum(lam_re, -1e-4)
    li = lam_im
    dt = jnp.exp(log_dt)
    mag = jnp.exp(lr * dt)
    ar = mag * jnp.cos(li * dt)
    ai = mag * jnp.sin(li * dt)
    den = lr * lr + li * li
    fr = ((ar - 1.0) * lr + ai * li) / den
    fi = (ai * lr - (ar - 1.0) * li) / den
    return ar, ai, fr, fi


SUPER = 4
SB_STATE = N_STATE // SUPER
SB_WIDTH = SSM_WIDTH // SUPER


def _sb_state(k):
    return slice(SB_STATE * k, SB_STATE * (k + 1))


def _sb_width(k):
    return slice(SB_WIDTH * k, SB_WIDTH * (k + 1))


def _dt_column(log_dt_row):
    eye = _iota((SSM_GROUPS, SSM_GROUPS), 0) == _iota((SSM_GROUPS, SSM_GROUPS), 1)
    return jnp.sum(jnp.where(eye, log_dt_row, 0.0), axis=1, keepdims=True)


def _group_masks():
    e64 = ((_iota((SSM_STATE, N_STATE), 1) & (SSM_STATE - 1)) == _iota((SSM_STATE, N_STATE), 0)).astype(_F32)
    own = _iota((SSM_GROUPS, N_STATE), 0) == (_iota((SSM_GROUPS, N_STATE), 1) >> 6)
    return e64, own


def _rows_of_group():
    return ((_iota((SSM_WIDTH, SSM_GROUPS), 0) >> 4) == _iota((SSM_WIDTH, SSM_GROUPS), 1)).astype(_F32)


def _ssm_prep(lam_re, lam_im, log_dt, b_re, b_im, c_re, c_im):
    def body(lr_ref, li_ref, ld_ref, bre, bim, cre, cim, ar_ref, ai_ref, btr, bti, ctr, cti):
        ar, ai, fr, fi = _s5_discretize(lr_ref[...], li_ref[...], _dt_column(ld_ref[...]))
        e64, own = _group_masks()
        mask_c = (_iota((SSM_WIDTH, N_STATE), 0) >> 4) == (_iota((SSM_WIDTH, N_STATE), 1) >> 6)

        def to_row(t):
            return jnp.sum(jnp.where(own, _dot_exact(t, e64, _NN), 0.0), axis=0, keepdims=True)

        def fold(m):
            full = jnp.where(mask_c, _dot(m, e64, _NN), 0.0)
            return sum(full[_sb_width(k), :] for k in range(SUPER)).astype(_BF16)

        ar_ref[...] = to_row(ar)
        ai_ref[...] = to_row(ai)
        spread = _rows_of_group()
        fr_t = _dot_exact(spread, fr, _NN)
        fi_t = _dot_exact(spread, fi, _NN)
        btr[...] = fold(fr_t * bre[...] - fi_t * bim[...])
        bti[...] = fold(fr_t * bim[...] + fi_t * bre[...])
        ctr[...] = fold(cre[...])
        cti[...] = fold(cim[...])

    row = (1, N_STATE)
    ins = [lam_re, lam_im, log_dt, b_re, b_im, c_re, c_im]
    return _call(body, (1,), [_whole(a.shape) for a in ins],
                 [_whole(row), _whole(row)] + [_whole((SB_WIDTH, N_STATE))] * 4,
                 [_sds(row, _F32), _sds(row, _F32)] + [_sds((SB_WIDTH, N_STATE), _BF16)] * 4,
                 "ssm_prep")(*ins)


def _complex_power(ar, ai, n):
    pr, pi = jnp.ones_like(ar), jnp.zeros_like(ai)
    while n:
        if n & 1:
            pr, pi = pr * ar - pi * ai, pr * ai + pi * ar
        ar, ai = ar * ar - ai * ai, 2.0 * ar * ai
        n >>= 1
    return pr, pi


def _chunk_carries(er, ei, pr, pi, reverse):
    rows = _iota(er.shape, 0)
    sr = jnp.zeros_like(pr)
    si = jnp.zeros_like(pi)
    out_r = jnp.zeros_like(er)
    out_i = jnp.zeros_like(ei)
    order = range(SCAN_CHUNKS - 1, 0, -1) if reverse else range(SCAN_CHUNKS - 1)
    for c in order:
        e_r = er[c:c + 1, :]
        e_i = ei[c:c + 1, :]
        sr, si = pr * sr - pi * si + e_r, pr * si + pi * sr + e_i
        nxt = c - 1 if reverse else c + 1
        out_r = jnp.where(rows == nxt, sr, out_r)
        out_i = jnp.where(rows == nxt, si, out_i)
    return out_r, out_i


_GELU_K = math.sqrt(2.0 / math.pi)
_GELU_C = 0.044715


def _gelu(y):
    return 0.5 * y * (1.0 + jnp.tanh(_GELU_K * (y + _GELU_C * y * y * y)))


def _gelu_grad(y):
    t = jnp.tanh(_GELU_K * (y + _GELU_C * y * y * y))
    return 0.5 * (1.0 + t) + 0.5 * y * (1.0 - t * t) * _GELU_K * (1.0 + 3.0 * _GELU_C * y * y)


def _step_rows(t):
    return pl.ds(pl.multiple_of(t * SCAN_CHUNKS, SCAN_CHUNKS), SCAN_CHUNKS)


def _scan_in_place(br, bi, ar, ai, T, carries=None):
    W = br.shape[1]
    ar8 = jnp.broadcast_to(ar, (SCAN_CHUNKS, W))
    ai8 = jnp.broadcast_to(ai, (SCAN_CHUNKS, W))

    def local(t, c):
        cr, ci = c
        rows = _step_rows(t)
        return ar8 * cr - ai8 * ci + br[rows, :], ar8 * ci + ai8 * cr + bi[rows, :]

    if carries is None:
        zero = jnp.zeros((SCAN_CHUNKS, W), _F32)
        er, ei = lax.fori_loop(0, T, local, (zero, zero), unroll=SCAN_UNROLL)
        pr, pi = _complex_power(ar, ai, T)
        carries = _chunk_carries(er, ei, pr, pi, reverse=False)

    def final(t, c):
        nr, ni = local(t, c)
        rows = _step_rows(t)
        br[rows, :] = nr
        bi[rows, :] = ni
        return nr, ni

    lax.fori_loop(0, T, final, carries, unroll=SCAN_UNROLL)
    return carries


def _scan_reverse_in_place(dr, di, xr, xi, ar, ai, T):
    W = dr.shape[1]
    ar8 = jnp.broadcast_to(ar, (SCAN_CHUNKS, W))
    ai8 = jnp.broadcast_to(ai, (SCAN_CHUNKS, W))

    def local(t, c):
        cr, ci = c
        rows = _step_rows(t)
        return ar8 * cr + ai8 * ci + dr[rows, :], ar8 * ci - ai8 * cr + di[rows, :]

    zero = jnp.zeros((SCAN_CHUNKS, W), _F32)
    er, ei = lax.fori_loop(0, T, lambda k, c: local(T - 1 - k, c), (zero, zero), unroll=SCAN_UNROLL)
    pr, pi = _complex_power(ar, -ai, T)
    sr, si = _chunk_carries(er, ei, pr, pi, reverse=True)

    def grad_a(acc, nr, ni, xpr, xpi):
        return acc[0] + nr * xpr + ni * xpi, acc[1] + ni * xpr - nr * xpi

    def final(k, c):
        t = T - 1 - k
        nr, ni = local(t, c[:2])
        rows = _step_rows(t)
        dr[rows, :] = nr
        di[rows, :] = ni
        before = _step_rows(t - 1)
        gr, gi = grad_a(c[2:], nr, ni, xr[before, :], xi[before, :])
        return nr, ni, gr, gi

    cr, ci, gr, gi = lax.fori_loop(0, T - 1, final, (sr, si, zero, zero), unroll=SCAN_UNROLL)
    nr, ni = local(0, (cr, ci))
    dr[_step_rows(0), :] = nr
    di[_step_rows(0), :] = ni
    first = _iota((SCAN_CHUNKS, W), 0) == 0
    last = _step_rows(T - 1)
    xpr = jnp.where(first, 0.0, pltpu.roll(xr[last, :], 1, 0))
    xpi = jnp.where(first, 0.0, pltpu.roll(xi[last, :], 1, 0))
    gr, gi = grad_a((gr, gi), nr, ni, xpr, xpi)
    return jnp.sum(gr, axis=0, keepdims=True), jnp.sum(gi, axis=0, keepdims=True)


def _ssm_super_specs(L):
    width = pl.BlockSpec((L, SB_WIDTH), lambda k: (0, k))
    matrix = pl.BlockSpec((SB_WIDTH, SB_STATE), lambda k: (0, k))
    row = pl.BlockSpec((1, SB_STATE), lambda k: (0, k))
    return width, matrix, row


def _ssm_states(u_ref, br_ref, bi_ref, ar_ref, ai_ref, xr, xi, T, carries=None):
    ub = u_ref[...].astype(_BF16)
    xr[...] = _dot(ub, br_ref[...], _NN)
    xi[...] = _dot(ub, bi_ref[...], _NN)
    return _scan_in_place(xr, xi, ar_ref[...], ai_ref[...], T, carries)


def _ssm_core_fwd(u, bt_re, bt_im, ct_re, ct_im, a_re, a_im):
    L = u.shape[0]
    T = L // SCAN_CHUNKS

    def body(u_ref, br_ref, bi_ref, cr_ref, ci_ref, ar_ref, ai_ref, y_ref, sr_ref, si_ref, xr, xi):
        sr_ref[...], si_ref[...] = _ssm_states(u_ref, br_ref, bi_ref, ar_ref, ai_ref, xr, xi, T)
        y_ref[...] = _dot(xr[...], cr_ref[...], _NT) - _dot(xi[...], ci_ref[...], _NT)

    width, matrix, row = _ssm_super_specs(L)
    carry = pl.BlockSpec((SCAN_CHUNKS, SB_STATE), lambda k: (0, k))
    return _call(body, (SUPER,), [width, matrix, matrix, matrix, matrix, row, row], [width, carry, carry],
                 [_sds((L, SSM_WIDTH), _F32)] + [_sds((SCAN_CHUNKS, N_STATE), _F32)] * 2, "ssm_core_fwd",
                 scratch=[pltpu.VMEM((L, SB_STATE), _F32)] * 2)(u, bt_re, bt_im, ct_re, ct_im, a_re, a_im)


def _ssm_core_bwd(u, dy, dud, carry_re, carry_im, bt_re, bt_im, ct_re, ct_im, a_re, a_im, tokens=()):
    L = u.shape[0]
    T = L // SCAN_CHUNKS

    def body(u_ref, dy_ref, dud_ref, sr_ref, si_ref, br_ref, bi_ref, cr_ref, ci_ref, ar_ref, ai_ref,
             du_ref, dcr_ref, dci_ref, dbr_ref, dbi_ref, dar_ref, dai_ref, xr, xi, lr, li):
        _ssm_states(u_ref, br_ref, bi_ref, ar_ref, ai_ref, xr, xi, T, (sr_ref[...], si_ref[...]))
        dyb = dy_ref[...]
        lr[...] = _dot(dyb, cr_ref[...], _NN)
        li[...] = -_dot(dyb, ci_ref[...], _NN)
        da_re, da_im = _scan_reverse_in_place(lr, li, xr, xi, ar_ref[...], ai_ref[...], T)
        dar_ref[...] = da_re
        dai_ref[...] = da_im
        du_ref[...] = _dot(lr[...], br_ref[...], _NT) + _dot(li[...], bi_ref[...], _NT) + dud_ref[...]
        ub = u_ref[...].astype(_BF16)
        dcr_ref[...] = _dot(dyb, xr[...], _TN)
        dci_ref[...] = _dot(dyb, xi[...], _TN)
        dbr_ref[...] = _dot(ub, lr[...], _TN)
        dbi_ref[...] = _dot(ub, li[...], _TN)

    width, matrix, row = _ssm_super_specs(L)
    carry = pl.BlockSpec((SCAN_CHUNKS, SB_STATE), lambda k: (0, k))
    return _call(body, (SUPER,), [width, width, width, carry, carry, matrix, matrix, matrix, matrix, row, row],
                 [width] + [matrix] * 4 + [row] * 2,
                 [_sds((L, SSM_WIDTH), _F32)] + [_sds((SB_WIDTH, N_STATE), _F32)] * 4 + [_sds((1, N_STATE), _F32)] * 2,
                 "ssm_core_bwd", scratch=[pltpu.VMEM((L, SB_STATE), _F32)] * 4,
                 tokens=tokens)(u, dy, dud, carry_re, carry_im, bt_re, bt_im, ct_re, ct_im, a_re, a_im)


def _ssm_out(cx, u, d_row, w_glu, b_glu, g_ssm):
    L = u.shape[0]
    tm = _tile(L)

    def body(cx_ref, u_ref, d_ref, w_ref, b_ref, g_ref, y_ref, z_ref, n_ref):
        y = cx_ref[...] + d_ref[...] * u_ref[...]
        y_ref[...] = y
        z = _dot(_gelu(y), w_ref[...], _NT) + b_ref[...]
        z_ref[...] = z
        out = z[:, :SSM_WIDTH] * jax.nn.sigmoid(z[:, SSM_WIDTH:])
        n, _ = _rms_fwd(out, g_ref[...])
        n_ref[...] = n.astype(_BF16)

    return _call(body, (L // tm,),
                 [_rows(tm, SSM_WIDTH), _rows(tm, SSM_WIDTH), _whole((1, SSM_WIDTH)),
                  _whole((2 * SSM_WIDTH, SSM_WIDTH)), _whole((1, 2 * SSM_WIDTH)), _whole((1, SSM_WIDTH))],
                 [_rows(tm, SSM_WIDTH), _rows(tm, 2 * SSM_WIDTH), _rows(tm, SSM_WIDTH)],
                 [_sds((L, SSM_WIDTH), _F32), _sds((L, 2 * SSM_WIDTH), _F32), _sds((L, SSM_WIDTH), _BF16)],
                 "ssm_out")(cx, u, d_row, w_glu, b_glu, g_ssm)


def _ssm_out_bwd(dn, y, z, u, d_row, w_glu, g_ssm):
    L = u.shape[0]
    tm = _tile(L)

    def body(dn_ref, y_ref, z_ref, u_ref, d_ref, w_ref, g_ref,
             gy_ref, dz_ref, dy_ref, dud_ref, dg_ref, db_ref, dd_ref):
        first = pl.program_id(0) == 0
        z = z_ref[...]
        z1, z2 = z[:, :SSM_WIDTH], z[:, SSM_WIDTH:]
        sig = jax.nn.sigmoid(z2)
        out = z1 * sig
        g = g_ref[...]
        _, r = _rms_fwd(out, g)
        dout, dg = _rms_bwd(dn_ref[...], out, g, r)
        _accumulate(dg_ref, dg, first)
        dz = jnp.concatenate([dout * sig, dout * z1 * sig * (1.0 - sig)], axis=1)
        _accumulate(db_ref, jnp.sum(dz, axis=0, keepdims=True), first)
        dzb = dz.astype(_BF16)
        dz_ref[...] = dzb
        y = y_ref[...]
        gy_ref[...] = _gelu(y).astype(_BF16)
        dy = _dot(dzb, w_ref[...], _NN) * _gelu_grad(y)
        u = u_ref[...]
        _accumulate(dd_ref, jnp.sum(dy * u, axis=0, keepdims=True), first)
        dud_ref[...] = d_ref[...] * dy
        dy_ref[...] = dy.astype(_BF16)

    row = _whole((1, SSM_WIDTH))
    return _call(body, (L // tm,),
                 [_rows(tm, SSM_WIDTH), _rows(tm, SSM_WIDTH), _rows(tm, 2 * SSM_WIDTH), _rows(tm, SSM_WIDTH),
                  row, _whole((2 * SSM_WIDTH, SSM_WIDTH)), row],
                 [_rows(tm, SSM_WIDTH), _rows(tm, 2 * SSM_WIDTH), _rows(tm, SSM_WIDTH), _rows(tm, SSM_WIDTH),
                  row, _whole((1, 2 * SSM_WIDTH)), row],
                 [_sds((L, SSM_WIDTH), _BF16), _sds((L, 2 * SSM_WIDTH), _BF16), _sds((L, SSM_WIDTH), _BF16),
                  _sds((L, SSM_WIDTH), _F32),
                  _sds((1, SSM_WIDTH), _F32), _sds((1, 2 * SSM_WIDTH), _F32), _sds((1, SSM_WIDTH), _F32)],
                 "ssm_out_bwd")(dn, y, z, u, d_row, w_glu, g_ssm)


_SSM_PACK = {"ssm_b_re": (0, SSM_WIDTH, 0, SSM_STATE), "ssm_c_re": (0, SSM_WIDTH, 64, SSM_STATE),
             "ssm_b_im": (512, SSM_WIDTH, 0, SSM_STATE), "ssm_c_im": (512, SSM_WIDTH, 64, SSM_STATE),
             "ssm_lambda_re": (1024, SSM_GROUPS, 0, SSM_STATE), "ssm_lambda_im": (1024, SSM_GROUPS, 64, SSM_STATE),
             "ssm_d": (1056, SSM_GROUPS, 0, SSM_GROUP), "ssm_log_dt": (1088, 1, 0, SSM_GROUPS)}
_PACK_TILE = 16
_SSM_PACK_ROWS = 1088 + _PACK_TILE


def _ssm_param_bwd(da_re, da_im, dbt_re, dbt_im, dct_re, dct_im, lam_re, lam_im, log_dt, b_re, b_im, g_d):
    def body(dar, dai, dbr, dbi, dcr, dci, lr_ref, li_ref, ld_ref, bre_ref, bim_ref, gd_ref, pack_ref):
        lane_in = _iota((SSM_STATE, _LANES), 0)
        lane_out = _iota((SSM_STATE, _LANES), 1)
        low = (lane_out == lane_in).astype(_F32)
        high = (lane_out == lane_in + SSM_STATE).astype(_F32)

        def side_by_side(a, b):
            return _dot_exact(a, low, _NN) + _dot_exact(b, high, _NN)

        tail = _SSM_PACK["ssm_d"][0]
        pack_ref[tail:, :] = jnp.zeros((_SSM_PACK_ROWS - tail, _LANES), _BF16)
        pack_ref[tail:tail + SSM_GROUPS, 0:SSM_GROUP] = gd_ref[...].astype(_BF16)
        own_c = (_iota((SB_WIDTH, SB_STATE), 0) >> 4) == (_iota((SB_WIDTH, SB_STATE), 1) >> 6)

        def unfold(ref):
            blocks = []
            for k in range(SUPER):
                t = jnp.where(own_c, ref[:, _sb_state(k)], 0.0)
                t = sum(t[:, 128 * i:128 * (i + 1)] for i in range(SB_STATE // 128))
                blocks.append((t + pltpu.roll(t, SSM_STATE, 1))[:, :SSM_STATE])
            return jnp.concatenate(blocks, axis=0)

        dbb_re, dbb_im = unfold(dbr), unfold(dbi)
        b_re, b_im = bre_ref[...], bim_ref[...]
        dt_col = _dt_column(ld_ref[...])
        (_, _, fr, fi), vjp = jax.vjp(_s5_discretize, lr_ref[...], li_ref[...], dt_col)
        spread = _rows_of_group()
        fr_t = _dot_exact(spread, fr, _NN)
        fi_t = _dot_exact(spread, fi, _NN)
        pack_ref[0:SSM_WIDTH, :] = side_by_side(fr_t * dbb_re + fi_t * dbb_im, unfold(dcr)).astype(_BF16)
        pack_ref[SSM_WIDTH:2 * SSM_WIDTH, :] = side_by_side(fr_t * dbb_im - fi_t * dbb_re, -unfold(dci)).astype(_BF16)
        d_fr = _dot_exact(spread, dbb_re * b_re + dbb_im * b_im, _TN)
        d_fi = _dot_exact(spread, dbb_im * b_re - dbb_re * b_im, _TN)
        e64, own = _group_masks()

        def from_row(ref):
            return _dot_exact(jnp.where(own, ref[...], 0.0), e64, _NT)

        d_lr, d_li, d_dt = vjp((from_row(dar), from_row(dai), d_fr, d_fi))
        lam_rows = _SSM_PACK["ssm_lambda_re"][0]
        pack_ref[lam_rows:lam_rows + SSM_GROUPS, :] = side_by_side(d_lr, d_li).astype(_BF16)
        eye = (_iota((SSM_GROUPS, SSM_GROUPS), 0) == _iota((SSM_GROUPS, SSM_GROUPS), 1)).astype(_F32)
        dt_row = _SSM_PACK["ssm_log_dt"][0]
        pack_ref[dt_row:dt_row + _PACK_TILE, 0:SSM_GROUPS] = _dot_exact(
            jnp.broadcast_to(d_dt, (SSM_GROUPS, 128)), eye, _TN)[0:_PACK_TILE].astype(_BF16)

    ins = [da_re, da_im, dbt_re, dbt_im, dct_re, dct_im, lam_re, lam_im, log_dt, b_re, b_im, g_d]
    out = (_SSM_PACK_ROWS, _LANES)
    return _call(body, (1,), [_whole(a.shape) for a in ins], _whole(out), _sds(out, _BF16), "ssm_param_bwd")(*ins)


def _head_spread(j):
    r = _iota((KV_WIDTH, 256), 0)
    c = _iota((KV_WIDTH, 256), 1)
    return (r == HEAD_DIM * j + (c & (HEAD_DIM - 1))).astype(_BF16)


STACK = Q_PER_KV * BLOCK


def _stack_heads(t):
    lane_head = _iota((1, 256), 1) >> 6
    return jnp.concatenate([jnp.where(lane_head == g, t, jnp.zeros_like(t)) for g in range(Q_PER_KV)], axis=0)


def _unstack_heads(t):
    lane_head = _iota((1, 256), 1) >> 6
    return sum(jnp.where(lane_head == g, t[BLOCK * g:BLOCK * (g + 1)], 0.0) for g in range(Q_PER_KV))


def _stacked_sinks(sink_ref, j):
    block = _iota((STACK, 1), 0) >> 7
    col = jnp.full((STACK, 1), sink_ref[Q_PER_KV * j], _F32)
    for g in range(1, Q_PER_KV):
        col = jnp.where(block == g, sink_ref[Q_PER_KV * j + g], col)
    return col


def _fold_heads(t, j):
    t = t[:, :KV_WIDTH] + t[:, KV_WIDTH:]
    t = t + pltpu.roll(t, HEAD_DIM, 1)
    return jnp.where((_iota((1, KV_WIDTH), 1) >> 6) == j, t, 0.0)


def _attn_scores(q_stacked, kt, blk, sink):
    s = _dot(q_stacked, kt, _NT) * (HEAD_DIM ** -0.5)
    qi = _iota((STACK, 2 * BLOCK), 0) & (BLOCK - 1)
    kj = _iota((STACK, 2 * BLOCK), 1)
    rel = qi + BLOCK - kj
    valid = (rel >= 0) & (rel < BLOCK) & (blk * BLOCK - BLOCK + kj >= 0)
    s = jnp.where(valid, s, MASK_VALUE)
    m = jnp.maximum(jnp.max(s, axis=-1, keepdims=True), sink)
    p = jnp.exp(s - m)
    e_sink = jnp.exp(sink - m)
    den = jnp.sum(p, axis=-1, keepdims=True) + e_sink
    return p / den, e_sink / den


def _attn_specs():
    prev = lambda i: (jnp.maximum(i - 1, 0), 0)
    cur = lambda i: (i, 0)
    kv = [pl.BlockSpec((BLOCK, KV_WIDTH), prev), pl.BlockSpec((BLOCK, KV_WIDTH), cur)]
    return [pl.BlockSpec((BLOCK, ATTN_WIDTH), cur)] + kv + kv


def _attn_fwd(q, k, v, sinks, g_attn):
    L = q.shape[0]

    def body(q_ref, kp_ref, kc_ref, vp_ref, vc_ref, sink_ref, g_ref, o_ref, n_ref):
        blk = pl.program_id(0)
        kwin = jnp.concatenate([kp_ref[...], kc_ref[...]], axis=0)
        vwin = jnp.concatenate([vp_ref[...], vc_ref[...]], axis=0)
        halves = []
        for j in range(N_KV_HEADS):
            spread = _head_spread(j)
            kt = _dot(kwin, spread, _NN).astype(_BF16)
            vt = _dot(vwin, spread, _NN).astype(_BF16)
            qs = _stack_heads(q_ref[:, 256 * j:256 * (j + 1)])
            p, _ = _attn_scores(qs, kt, blk, _stacked_sinks(sink_ref, j))
            halves.append(_unstack_heads(_dot(p, vt, _NN)))
        o = jnp.concatenate(halves, axis=1)
        o_ref[...] = o
        n, _ = _rms_fwd(o, g_ref[...])
        n_ref[...] = n.astype(_BF16)

    cur = lambda i: (i, 0)
    return _call(body, (L // BLOCK,),
                 _attn_specs() + [pl.BlockSpec(memory_space=pltpu.SMEM), _whole((1, ATTN_WIDTH))],
                 [pl.BlockSpec((BLOCK, ATTN_WIDTH), cur)] * 2,
                 [_sds((L, ATTN_WIDTH), _F32), _sds((L, ATTN_WIDTH), _BF16)],
                 "attn_fwd")(q, k, k, v, v, sinks, g_attn)


def _attn_bwd(q, k, v, o, dn, sinks, g_attn):
    L = q.shape[0]

    def body(q_ref, kp_ref, kc_ref, vp_ref, vc_ref, o_ref, dn_ref, sink_ref, g_ref,
             dq_ref, dk_ref, dv_ref, dsink_ref, dg_ref):
        blk = pl.program_id(0)
        first = blk == 0

        @pl.when(first)
        def _():
            dk_ref[...] = jnp.zeros_like(dk_ref)
            dv_ref[...] = jnp.zeros_like(dv_ref)
            dsink_ref[...] = jnp.zeros_like(dsink_ref)

        o = o_ref[...]
        g = g_ref[...]
        _, r = _rms_fwd(o, g)
        do, dg = _rms_bwd(dn_ref[...], o, g, r)
        _accumulate(dg_ref, dg, first)
        kwin = jnp.concatenate([kp_ref[...], kc_ref[...]], axis=0)
        vwin = jnp.concatenate([vp_ref[...], vc_ref[...]], axis=0)
        lane = _iota((1, 128), 1)
        dsink = jnp.zeros((1, 128), _F32)
        dkwin = jnp.zeros((2 * BLOCK, KV_WIDTH), _F32)
        dvwin = jnp.zeros((2 * BLOCK, KV_WIDTH), _F32)
        dq_halves = []
        for j in range(N_KV_HEADS):
            spread = _head_spread(j)
            kt = _dot(kwin, spread, _NN).astype(_BF16)
            vt = _dot(vwin, spread, _NN).astype(_BF16)
            qs = _stack_heads(q_ref[:, 256 * j:256 * (j + 1)])
            dos = _stack_heads(do[:, 256 * j:256 * (j + 1)]).astype(_BF16)
            p, p_sink = _attn_scores(qs, kt, blk, _stacked_sinks(sink_ref, j))
            dp = _dot(dos, vt, _NT)
            delta = jnp.sum(p * dp, axis=-1, keepdims=True)
            ds = (p * (dp - delta) * (HEAD_DIM ** -0.5)).astype(_BF16)
            sink_term = p_sink * delta
            for g in range(Q_PER_KV):
                head_sum = jnp.sum(sink_term[BLOCK * g:BLOCK * (g + 1)], axis=0, keepdims=True)
                dsink = dsink - jnp.where(lane == Q_PER_KV * j + g, head_sum, 0.0)
            dvwin = dvwin + _fold_heads(_dot(p, dos, _TN), j)
            dkwin = dkwin + _fold_heads(_dot(ds, qs, _TN), j)
            dq_halves.append(_unstack_heads(_dot(ds, kt, _NN)))
        dq_ref[...] = jnp.concatenate(dq_halves, axis=1)
        dsink_ref[...] += dsink
        prev = pl.ds(pl.multiple_of(jnp.maximum(blk - 1, 0) * BLOCK, BLOCK), BLOCK)
        cur = pl.ds(pl.multiple_of(blk * BLOCK, BLOCK), BLOCK)
        dk_ref[prev, :] += dkwin[:BLOCK]
        dk_ref[cur, :] += dkwin[BLOCK:]
        dv_ref[prev, :] += dvwin[:BLOCK]
        dv_ref[cur, :] += dvwin[BLOCK:]

    cur = lambda i: (i, 0)
    blk_q = pl.BlockSpec((BLOCK, ATTN_WIDTH), cur)
    return _call(body, (L // BLOCK,),
                 _attn_specs() + [blk_q, blk_q, pl.BlockSpec(memory_space=pltpu.SMEM), _whole((1, ATTN_WIDTH))],
                 [blk_q, _whole((L, KV_WIDTH)), _whole((L, KV_WIDTH)), _whole((1, 128)), _whole((1, ATTN_WIDTH))],
                 [_sds((L, ATTN_WIDTH), _F32), _sds((L, KV_WIDTH), _F32), _sds((L, KV_WIDTH), _F32),
                  _sds((1, 128), _F32), _sds((1, ATTN_WIDTH), _F32)],
                 "attn_bwd")(q, k, k, v, v, o, dn, sinks, g_attn)


def _out_proj(n_ssm, n_attn, x, w_out, g_post_mix, g_pre_ffn):
    L = x.shape[0]
    tm = _chunk_tile(L)

    def body(ns_ref, na_ref, x_ref, w_ref, g1_ref, g2_ref, merged_ref, mo_ref, h1_ref, hn2_ref):
        merged = jnp.concatenate([ns_ref[...], na_ref[...]], axis=1)
        merged_ref[...] = merged
        mo = _dot(merged, w_ref[...], _NN)
        mo_ref[...] = mo
        n, _ = _rms_fwd(mo, g1_ref[...])
        h1 = x_ref[...] + n
        h1_ref[...] = h1
        hn2, _ = _rms_fwd(h1, g2_ref[...])
        hn2_ref[...] = hn2.astype(_BF16)

    row = _whole((1, D_MODEL))
    return _call(body, (L // tm,),
                 [_chunk_block(L, SSM_WIDTH), _rows(tm, ATTN_WIDTH), _rows(tm, D_MODEL), _whole((D_MODEL, D_MODEL)),
                  row, row],
                 [_rows(tm, D_MODEL)] * 4,
                 [_sds((L, D_MODEL), _BF16), _sds((L, D_MODEL), _F32), _sds((L, D_MODEL), _F32), _sds((L, D_MODEL), _BF16)],
                 "out_proj")(n_ssm, n_attn, x, w_out, g_post_mix, g_pre_ffn)


def _ffn(hn2, h1, target, w_gate_up, w_down, g_pre_ffn, g_post_ffn):
    L = h1.shape[0]
    tm = _tile(L)
    half = FFN_CHUNK

    def body(hn2_ref, h1_ref, tgt_ref, wgu_hbm, wd_hbm, g2_ref, g3_ref,
             act_ref, dgu_ref, dff_ref, dh1_ref, loss_ref, dg3_ref, dg2_ref,
             wgu, wd, gu, sem):
        first = pl.program_id(0) == 0

        @pl.when(first)
        def _():
            c1 = pltpu.make_async_copy(wgu_hbm, wgu, sem.at[0])
            c2 = pltpu.make_async_copy(wd_hbm, wd, sem.at[1])
            c1.start()
            c2.start()
            c1.wait()
            c2.wait()

        hn2 = hn2_ref[...]
        ff = jnp.zeros((tm, D_MODEL), _F32)
        for c in range(D_FF // half):
            gate = _dot(hn2, wgu[half * c:half * (c + 1), :], _NT)
            up = _dot(hn2, wgu[D_FF + half * c:D_FF + half * (c + 1), :], _NT)
            gu[:, half * c:half * (c + 1)] = gate
            gu[:, D_FF + half * c:D_FF + half * (c + 1)] = up
            act = gate * jax.nn.sigmoid(gate) * up
            act_ref[half * c:half * (c + 1), :] = act.T.astype(_BF16)
            ff = ff + _dot(act, wd[half * c:half * (c + 1), :], _NN)
        g3 = g3_ref[...]
        n, r = _rms_fwd(ff, g3)
        h1 = h1_ref[...]
        err = h1 + n - tgt_ref[...]
        loss = 0.5 * jnp.sum(jnp.mean(err * err, axis=-1, keepdims=True), axis=0, keepdims=True)
        _accumulate(loss_ref, jnp.broadcast_to(loss, (1, 128)), first)
        dh2 = err * (1.0 / D_MODEL)
        dff, dg3 = _rms_bwd(dh2, ff, g3, r)
        _accumulate(dg3_ref, dg3, first)
        dffb = dff.astype(_BF16)
        dff_ref[...] = dffb
        dhn2 = jnp.zeros((tm, D_MODEL), _F32)
        for c in range(D_FF // half):
            dact = _dot(dffb, wd[half * c:half * (c + 1), :], _NT)
            gate = gu[:, half * c:half * (c + 1)]
            up = gu[:, D_FF + half * c:D_FF + half * (c + 1)]
            sig = jax.nn.sigmoid(gate)
            silu = gate * sig
            dgate = dact * up * (sig + silu * (1.0 - sig))
            dup = dact * silu
            dgu_ref[half * c:half * (c + 1), :] = dgate.T.astype(_BF16)
            dgu_ref[D_FF + half * c:D_FF + half * (c + 1), :] = dup.T.astype(_BF16)
            dhn2 = dhn2 + _dot(dgate, wgu[half * c:half * (c + 1), :], _NN)
            dhn2 = dhn2 + _dot(dup, wgu[D_FF + half * c:D_FF + half * (c + 1), :], _NN)
        g2 = g2_ref[...]
        _, r2 = _rms_fwd(h1, g2)
        dh1, dg2 = _rms_bwd(dhn2, h1, g2, r2)
        _accumulate(dg2_ref, dg2, first)
        dh1_ref[...] = dh2 + dh1

    row = _whole((1, D_MODEL))
    anyspace = pl.BlockSpec(memory_space=pl.ANY)
    return _call(body, (L // tm,),
                 [_rows(tm, D_MODEL), _rows(tm, D_MODEL), _rows(tm, D_MODEL), anyspace, anyspace, row, row],
                 [pl.BlockSpec((D_FF, tm), lambda i: (0, i)), pl.BlockSpec((2 * D_FF, tm), lambda i: (0, i)),
                  _rows(tm, D_MODEL), _rows(tm, D_MODEL), _whole((1, 128)), row, row],
                 [_sds((D_FF, L), _BF16), _sds((2 * D_FF, L), _BF16), _sds((L, D_MODEL), _BF16),
                  _sds((L, D_MODEL), _F32), _sds((1, 128), _F32), _sds((1, D_MODEL), _F32), _sds((1, D_MODEL), _F32)],
                 "ffn",
                 scratch=[pltpu.VMEM((2 * D_FF, D_MODEL), _BF16), pltpu.VMEM((D_FF, D_MODEL), _BF16),
                          pltpu.VMEM((tm, 2 * D_FF), _F32), pltpu.SemaphoreType.DMA((2,))],
                 )(hn2, h1, target, w_gate_up, w_down, g_pre_ffn, g_post_ffn)


def _out_proj_bwd(dh1, mo, w_out, g_post_mix, tokens=()):
    L = dh1.shape[0]
    tm = _chunk_tile(L)

    def body(dh1_ref, mo_ref, w_ref, g_ref, dmo_ref, dns_ref, dna_ref, dg_ref):
        first = pl.program_id(0) == 0
        mo = mo_ref[...]
        g = g_ref[...]
        _, r = _rms_fwd(mo, g)
        dmo, dg = _rms_bwd(dh1_ref[...], mo, g, r)
        _accumulate(dg_ref, dg, first)
        dmob = dmo.astype(_BF16)
        dmo_ref[...] = dmob
        dmerged = _dot(dmob, w_ref[...], _NT)
        dns_ref[...] = dmerged[:, :SSM_WIDTH]
        dna_ref[...] = dmerged[:, SSM_WIDTH:]

    row = _whole((1, D_MODEL))
    return _call(body, (L // tm,),
                 [_rows(tm, D_MODEL), _rows(tm, D_MODEL), _whole((D_MODEL, D_MODEL)), row],
                 [_rows(tm, D_MODEL), _chunk_block(L, SSM_WIDTH), _rows(tm, ATTN_WIDTH), row],
                 [_sds((L, D_MODEL), _BF16), _sds(_chunk_shape(L, SSM_WIDTH), _F32), _sds((L, ATTN_WIDTH), _F32),
                  _sds((1, D_MODEL), _F32)],
                 "out_proj_bwd", tokens=tokens)(dh1, mo, w_out, g_post_mix)


def _in_proj_bwd(du, dq, dk, dv, cos_t, sin_t, x, dh1, g_pre_mix, w_in, tokens=()):
    L = x.shape[0]
    tm = _chunk_tile(L)

    def body(du_ref, dq_ref, dk_ref, dv_ref, cos_ref, sin_ref, x_ref, dh1_ref, g_ref, w_ref,
             dproj_ref, dx_ref, dg_ref):
        first = pl.program_id(0) == 0
        cos_v, sin_v = cos_ref[...], sin_ref[...]
        dproj = jnp.concatenate([du_ref[...], _rope_transpose(dq_ref[...], cos_v, sin_v),
                                 _rope_transpose(dk_ref[...], cos_v, sin_v), dv_ref[...]], axis=1).astype(_BF16)
        dproj_ref[...] = dproj
        dhn = _dot(dproj, w_ref[...], _NN)
        x = x_ref[...]
        g = g_ref[...]
        _, r = _rms_fwd(x, g)
        dx, dg = _rms_bwd(dhn, x, g, r)
        _accumulate(dg_ref, dg, first)
        dx_ref[...] = dh1_ref[...] + dx

    row = _whole((1, D_MODEL))
    return _call(body, (L // tm,),
                 [_chunk_block(L, SSM_WIDTH), _rows(tm, ATTN_WIDTH), _rows(tm, KV_WIDTH), _rows(tm, KV_WIDTH),
                  _rows(tm, KV_WIDTH), _rows(tm, KV_WIDTH), _rows(tm, D_MODEL), _rows(tm, D_MODEL), row,
                  _whole((IN_WIDTH, D_MODEL))],
                 [_rows(tm, IN_WIDTH), _rows(tm, D_MODEL), row],
                 [_sds((L, IN_WIDTH), _BF16), _sds((L, D_MODEL), _F32), _sds((1, D_MODEL), _F32)],
                 "in_proj_bwd", tokens=tokens)(du, dq, dk, dv, cos_t, sin_t, x, dh1, g_pre_mix, w_in)


def _matmul_nn(a, b, out_dtype, name):
    M, K = a.shape
    N = b.shape[1]
    tm = next(t for t in (704, 512, 256, 128) if M % t == 0)
    tn = N if N <= D_MODEL else next(t for t in (512, 256, 128) if N % t == 0)

    def body(a_ref, b_ref, o_ref):
        o_ref[...] = _dot(a_ref[...], b_ref[...], _NN).astype(out_dtype)

    params = pltpu.CompilerParams(dimension_semantics=("arbitrary", "arbitrary"), vmem_limit_bytes=VMEM_LIMIT)
    return pl.pallas_call(body, grid=(M // tm, N // tn),
                          in_specs=[pl.BlockSpec((tm, K), lambda i, j: (i, 0)),
                                    pl.BlockSpec((K, tn), lambda i, j: (0, j))],
                          out_specs=pl.BlockSpec((tm, tn), lambda i, j: (i, j)),
                          out_shape=_sds((M, N), out_dtype), compiler_params=params, name=name)(a, b)


def _matmul_tn(a, b, out_dtype, name, scale=1.0):
    K, M = a.shape
    N = b.shape[1]
    tm = next(t for t in (512, 256, 128) if M % t == 0)
    tn = N if N <= D_MODEL else next(t for t in (512, 256, 128) if N % t == 0)

    def body(a_ref, b_ref, o_ref):
        acc = _dot(a_ref[...], b_ref[...], _TN)
        o_ref[...] = (acc if scale == 1.0 else acc * scale).astype(out_dtype)

    params = pltpu.CompilerParams(dimension_semantics=("arbitrary", "arbitrary"), vmem_limit_bytes=VMEM_LIMIT)
    return pl.pallas_call(body, grid=(M // tm, N // tn),
                          in_specs=[pl.BlockSpec((K, tm), lambda i, j: (0, i)),
                                    pl.BlockSpec((K, tn), lambda i, j: (0, j))],
                          out_specs=pl.BlockSpec((tm, tn), lambda i, j: (i, j)),
                          out_shape=_sds((M, N), out_dtype), compiler_params=params, name=name)(a, b)


def _local_step(x, pos, target, p, fetch, publish, progress):
    L = x.shape[0]
    T = L // SCAN_CHUNKS
    cos_t, sin_t = _rope_tables(pos.reshape(L, 1))
    w_in, = fetch(("w_in",), None)
    hn, u, q, k, v = _in_proj(x, p["g_pre_mix"], w_in, cos_t, sin_t)

    ssm = {n: _to_2d(n, p[n]) for n in ("ssm_lambda_re", "ssm_lambda_im", "ssm_log_dt", "ssm_b_re", "ssm_b_im",
                                        "ssm_c_re", "ssm_c_im")}
    d_row = p["ssm_d"].reshape(1, SSM_WIDTH)
    a_re, a_im, bt_re, bt_im, ct_re, ct_im = _ssm_prep(
        ssm["ssm_lambda_re"], ssm["ssm_lambda_im"], ssm["ssm_log_dt"], ssm["ssm_b_re"], ssm["ssm_b_im"],
        ssm["ssm_c_re"], ssm["ssm_c_im"])

    u_c = u.reshape(L, SSM_WIDTH)
    cx, carry_re, carry_im = _ssm_core_fwd(u_c, bt_re, bt_im, ct_re, ct_im, a_re, a_im)
    w_glu, = fetch(("w_glu",), cx)
    y, z, n_ssm_c = _ssm_out(cx, u_c, d_row, w_glu, p["b_glu"], p["g_ssm_out"])
    n_ssm = n_ssm_c.reshape(_chunk_shape(L, SSM_WIDTH))

    sinks = p["attn_sinks"].reshape(N_Q_HEADS)
    o, n_attn = _attn_fwd(q, k, v, sinks, p["g_attn_out"])
    w_out, = fetch(("w_out",), n_attn)
    merged, mo, h1, hn2 = _out_proj(n_ssm, n_attn, x, w_out, p["g_post_mix"], p["g_pre_ffn"])
    w_gate_up, w_down = fetch(("w_gate_up", "w_down"), hn2)
    act_t, dgu_t, dff, dh1, loss, dg_post_ffn, dg_pre_ffn = _ffn(
        hn2, h1, target, w_gate_up, w_down, p["g_pre_ffn"], p["g_post_ffn"])
    grads = {"g_post_ffn": dg_post_ffn, "g_pre_ffn": dg_pre_ffn}
    tokens = publish({"w_down": _matmul_nn(act_t, dff, _BF16, "grad_w_down"),
                      "w_gate_up": _matmul_nn(dgu_t, hn2, _BF16, "grad_w_gate_up")})

    dmo, dn_ssm, dn_attn, grads["g_post_mix"] = _out_proj_bwd(dh1, mo, w_out, p["g_post_mix"], tokens)
    grad_w_out = _matmul_tn(merged, dmo, _BF16, "grad_w_out")

    dq, dk, dv, dsink, grads["g_attn_out"] = _attn_bwd(q, k, v, o, dn_attn, sinks, p["g_attn_out"])
    grads["attn_sinks"] = dsink

    gy, dz, dy, dud, grads["g_ssm_out"], grads["b_glu"], dd = _ssm_out_bwd(
        dn_ssm.reshape(L, SSM_WIDTH), y, z, u_c, d_row, w_glu, p["g_ssm_out"])
    tokens = progress(dy)
    tokens += publish({"w_out": grad_w_out, "w_glu": _matmul_tn(dz, gy, _BF16, "grad_w_glu")})
    du_c, dct_re, dct_im, dbt_re, dbt_im, da_re, da_im = _ssm_core_bwd(
        u_c, dy, dud, carry_re, carry_im, bt_re, bt_im, ct_re, ct_im, a_re, a_im, tokens)
    ssm_pack = _ssm_param_bwd(
        da_re, da_im, dbt_re, dbt_im, dct_re, dct_im,
        ssm["ssm_lambda_re"], ssm["ssm_lambda_im"], ssm["ssm_log_dt"], ssm["ssm_b_re"], ssm["ssm_b_im"],
        dd.reshape(SSM_GROUPS, SSM_GROUP))
    grads.update(ssm_pack=ssm_pack, loss=loss)
    publish(grads)

    du = du_c.reshape(_chunk_shape(L, SSM_WIDTH))
    dproj, grad_x, g_pre_mix = _in_proj_bwd(du, dq, dk, dv, cos_t, sin_t, x, dh1, p["g_pre_mix"], w_in, [ssm_pack])
    publish({"g_pre_mix": g_pre_mix, "w_in": _matmul_tn(dproj, hn, _BF16, "grad_w_in")})
    return grad_x


_MESH = pl.DeviceIdType.MESH
_PEERS = N_DEV - 1


def _mesh_pos():
    return lax.axis_index("x"), lax.axis_index("y"), lax.axis_index("c")


def _dev_index(px, py, pc):
    return 4 * px + 2 * py + pc


def _peer(x, y, c, r):
    return (x ^ ((r >> 2) & 1), y ^ ((r >> 1) & 1), c ^ (r & 1))


def _sequencer_exchange(sources, blocked, name, collective_id):
    n = len(sources)
    flags = blocked

    def body(*refs):
        srcs, zones = refs[:n], refs[n:2 * n]
        send_sems, recv_sems, local_sems = refs[2 * n:]
        x, y, c = _mesh_pos()
        me = _dev_index(x, y, c)
        barrier = pltpu.get_barrier_semaphore()
        for r in range(1, N_DEV):
            pl.semaphore_signal(barrier, inc=1, device_id=_peer(x, y, c, r), device_id_type=_MESH)
        pl.semaphore_wait(barrier, _PEERS)
        local, sends, recvs = [], [], []
        for w in range(n):
            cp = pltpu.make_async_copy(srcs[w].at[me] if flags[w] else srcs[w], zones[w].at[me], local_sems.at[w])
            cp.start()
            local.append(cp)
            for r in range(1, N_DEV):
                peer = _peer(x, y, c, r)
                idx = _dev_index(*peer)
                k = _PEERS * w + r - 1
                src = srcs[w].at[idx] if flags[w] else srcs[w]
                send = pltpu.make_async_remote_copy(
                    src_ref=src, dst_ref=zones[w].at[me], send_sem=send_sems.at[k], recv_sem=recv_sems.at[k],
                    device_id=peer, device_id_type=_MESH)
                send.start()
                sends.append(send)
                recvs.append(pltpu.make_async_remote_copy(
                    src_ref=src, dst_ref=zones[w].at[idx], send_sem=send_sems.at[k], recv_sem=recv_sems.at[k],
                    device_id=peer, device_id_type=_MESH))
        for cp in recvs:
            cp.wait_recv()
        for cp in sends:
            cp.wait_send()
        for cp in local:
            cp.wait()

    return pl.kernel(
        body, name=name,
        out_type=[_sds((N_DEV,) + (s.shape[1:] if f else s.shape), s.dtype) for s, f in zip(sources, flags)],
        mesh=plsc.ScalarSubcoreMesh(axis_name="sequencer", num_cores=1),
        scratch_types=[pltpu.SemaphoreType.DMA((_PEERS * n,)), pltpu.SemaphoreType.DMA((_PEERS * n,)),
                       pltpu.SemaphoreType.DMA((n,))],
        compiler_params=pltpu.CompilerParams(collective_id=collective_id),
    )(*sources)


def _sequencer_gather(shards, name, collective_id):
    n = len(shards)
    fan = 4

    def body(*refs):
        srcs, zones = refs[:n], refs[n:2 * n]
        send_sems, recv_sems, local_sems = refs[2 * n:]
        x, y, c = _mesh_pos()
        me, sibling = (x, y, c), (x, y, 1 - c)
        chips = [(1 - x, y), (x, 1 - y), (1 - x, 1 - y)]
        barrier = pltpu.get_barrier_semaphore()
        for peer in [sibling] + [(*chip, c) for chip in chips]:
            pl.semaphore_signal(barrier, inc=1, device_id=peer, device_id_type=_MESH)
        pl.semaphore_wait(barrier, fan)

        def copy(w, k, block, to, src=None):
            slot = zones[w].at[_dev_index(*block)]
            return pltpu.make_async_remote_copy(
                src_ref=slot if src is None else src, dst_ref=slot,
                send_sem=send_sems.at[_PEERS * w + k], recv_sem=recv_sems.at[_PEERS * w + k],
                device_id=to, device_id_type=_MESH)

        mine, first, passed = [], [], []
        for w in range(n):
            cp = pltpu.make_async_copy(srcs[w], zones[w].at[_dev_index(*me)], local_sems.at[w])
            cp.start()
            mine.append(cp)
            sends = [copy(w, 0, me, sibling, src=srcs[w])]
            sends += [copy(w, 1 + j, me, (*chip, c), src=srcs[w]) for j, chip in enumerate(chips)]
            for cp in sends:
                cp.start()
            first += sends
        for w in range(n):
            for j, chip in enumerate(chips):
                copy(w, 1 + j, (*chip, c), me).wait_recv()
                cp = copy(w, fan + j, (*chip, c), sibling)
                cp.start()
                passed.append(cp)
        for w in range(n):
            copy(w, 0, sibling, me).wait_recv()
            for j, chip in enumerate(chips):
                copy(w, fan + j, (*chip, 1 - c), me).wait_recv()
        for cp in first + passed:
            cp.wait_send()
        for cp in mine:
            cp.wait()

    return pl.kernel(
        body, name=name, out_type=[_sds((N_DEV,) + s.shape, s.dtype) for s in shards],
        mesh=plsc.ScalarSubcoreMesh(axis_name="sequencer", num_cores=1),
        scratch_types=[pltpu.SemaphoreType.DMA((_PEERS * n,)), pltpu.SemaphoreType.DMA((_PEERS * n,)),
                       pltpu.SemaphoreType.DMA((n,))],
        compiler_params=pltpu.CompilerParams(collective_id=collective_id),
    )(*shards)


N_CHIPS = N_DEV // 2


def _sequencer_pair_exchange(sources, name, collective_id):
    n = len(sources)

    def body(*refs):
        srcs, zones = refs[:n], refs[n:2 * n]
        send_sems, recv_sems = refs[2 * n:]
        x, y, c = _mesh_pos()
        sibling = (x, y, 1 - c)
        barrier = pltpu.get_barrier_semaphore()
        pl.semaphore_signal(barrier, inc=1, device_id=sibling, device_id_type=_MESH)
        pl.semaphore_wait(barrier, 1)
        copies = []
        for w in range(n):
            for j in range(N_CHIPS):
                k = N_CHIPS * w + j
                cp = pltpu.make_async_remote_copy(
                    src_ref=srcs[w].at[2 * j + 1 - c], dst_ref=zones[w].at[j],
                    send_sem=send_sems.at[k], recv_sem=recv_sems.at[k], device_id=sibling, device_id_type=_MESH)
                cp.start()
                copies.append(cp)
        for cp in copies:
            cp.wait_recv()
        for cp in copies:
            cp.wait_send()

    return pl.kernel(
        body, name=name, out_type=[_sds((N_CHIPS,) + s.shape[1:], s.dtype) for s in sources],
        mesh=plsc.ScalarSubcoreMesh(axis_name="sequencer", num_cores=1),
        scratch_types=[pltpu.SemaphoreType.DMA((N_CHIPS * n,)), pltpu.SemaphoreType.DMA((N_CHIPS * n,))],
        compiler_params=pltpu.CompilerParams(collective_id=collective_id),
    )(*sources)


def _pair_sum(source, received, core, name, tokens=()):
    _, rows, cols = source.shape
    tr = _row_tile(rows)

    def body(core_ref, s_ref, r_ref, o_ref):
        c = core_ref[0]
        for j in range(N_CHIPS):
            o_ref[j] = (s_ref[2 * j + c].astype(_F32) + r_ref[j].astype(_F32)).astype(o_ref.dtype)

    return _call(body, (rows // tr,),
                 [pl.BlockSpec(memory_space=pltpu.SMEM), pl.BlockSpec((N_DEV, tr, cols), lambda i: (0, i, 0)),
                  pl.BlockSpec((N_CHIPS, tr, cols), lambda i: (0, i, 0))],
                 pl.BlockSpec((N_CHIPS, tr, cols), lambda i: (0, i, 0)),
                 _sds((N_CHIPS, rows, cols), source.dtype), name, tokens=tokens)(core, source, received)


def _sequencer_chip_exchange(partials, name, collective_id):
    n = len(partials)
    others = N_CHIPS - 1

    def body(*refs):
        srcs, zones = refs[:n], refs[n:2 * n]
        send_sems, recv_sems, local_sems = refs[2 * n:]
        x, y, c = _mesh_pos()
        mine = 2 * x + y
        peers = [(x ^ (r >> 1), y ^ (r & 1), c) for r in range(1, N_CHIPS)]
        barrier = pltpu.get_barrier_semaphore()
        for peer in peers:
            pl.semaphore_signal(barrier, inc=1, device_id=peer, device_id_type=_MESH)
        pl.semaphore_wait(barrier, others)
        local, sends, recvs = [], [], []
        for w in range(n):
            cp = pltpu.make_async_copy(srcs[w].at[mine], zones[w].at[mine], local_sems.at[w])
            cp.start()
            local.append(cp)
            for r, peer in enumerate(peers):
                theirs = 2 * peer[0] + peer[1]
                k = others * w + r
                send = pltpu.make_async_remote_copy(
                    src_ref=srcs[w].at[theirs], dst_ref=zones[w].at[mine],
                    send_sem=send_sems.at[k], recv_sem=recv_sems.at[k], device_id=peer, device_id_type=_MESH)
                send.start()
                sends.append(send)
                recvs.append(pltpu.make_async_remote_copy(
                    src_ref=srcs[w].at[theirs], dst_ref=zones[w].at[theirs],
                    send_sem=send_sems.at[k], recv_sem=recv_sems.at[k], device_id=peer, device_id_type=_MESH))
        for cp in recvs:
            cp.wait_recv()
        for cp in sends:
            cp.wait_send()
        for cp in local:
            cp.wait()

    return pl.kernel(
        body, name=name, out_type=[_sds(s.shape, s.dtype) for s in partials],
        mesh=plsc.ScalarSubcoreMesh(axis_name="sequencer", num_cores=1),
        scratch_types=[pltpu.SemaphoreType.DMA((others * n,)), pltpu.SemaphoreType.DMA((others * n,)),
                       pltpu.SemaphoreType.DMA((n,))],
        compiler_params=pltpu.CompilerParams(collective_id=collective_id),
    )(*partials)


def _row_tile(rows):
    return next(t for t in range(min(rows, 256), 0, -16) if rows % t == 0)


def _sum_parts(parts, name, tokens=()):
    _, rows, cols = parts.shape
    tr = _row_tile(rows)

    def body(p_ref, g_ref):
        g = p_ref[0].astype(_F32)
        for s in range(1, N_DEV):
            g = g + p_ref[s].astype(_F32)
        g_ref[...] = g

    return _call(body, (rows // tr,), [pl.BlockSpec((N_DEV, tr, cols), lambda i: (0, i, 0))],
                 _rows(tr, cols), _sds((rows, cols), _F32), name, tokens=tokens)(parts)


def _adam_update(g, w, m, v):
    new_m = ADAM_B1 * m + (1.0 - ADAM_B1) * g
    new_v = ADAM_B2 * v + (1.0 - ADAM_B2) * (g * g)
    m_hat = new_m / (1.0 - ADAM_B1 ** ADAM_STEP)
    v_hat = new_v / (1.0 - ADAM_B2 ** ADAM_STEP)
    return -ADAM_LR * (m_hat / (jnp.sqrt(v_hat) + ADAM_EPS) + ADAM_WD * w), new_m, new_v


def _adamw_small(parts, items, sums, name, tokens=()):
    n_p, n_i = len(parts), len(items)

    def body(*refs):
        p_refs, state, outs = refs[:n_p], refs[n_p:n_p + 3 * n_i], refs[n_p + 3 * n_i:]

        def total(part, rows, cols):
            shift = cols.start % _LANES
            window = slice(cols.start - shift, cols.start - shift + _LANES) if shift else cols
            n_rows = rows.stop - rows.start
            narrow = p_refs[part].dtype.itemsize < 4 and n_rows % _PACK_TILE
            tile = slice(rows.start, rows.start + _PACK_TILE) if narrow else rows
            g = p_refs[part][0, tile, window].astype(_F32)
            for s in range(1, N_DEV):
                g = g + p_refs[part][s, tile, window].astype(_F32)
            g = g[:n_rows] if narrow else g
            return pltpu.roll(g, _LANES - shift, 1)[:, :cols.stop - cols.start] if shift else g

        for i, (part, rows, cols, _, _, _) in enumerate(items):
            g = total(part, rows, cols)
            w_ref, m_ref, v_ref = state[3 * i:3 * i + 3]
            delta, new_m, new_v = _adam_update(g, w_ref[...], m_ref[...], v_ref[...])
            outs[4 * i][...] = g
            outs[4 * i + 1][...] = delta
            outs[4 * i + 2][...] = new_m
            outs[4 * i + 3][...] = new_v
        for j, (part, rows, cols) in enumerate(sums):
            outs[4 * n_i + j][...] = total(part, rows, cols)

    ins = list(parts) + [a for item in items for a in item[3:]]
    out_shapes = [item[3].shape for item in items for _ in range(4)]
    out_shapes += [(rows.stop - rows.start, cols.stop - cols.start) for _, rows, cols in sums]
    out = _call(body, (1,), [_whole(a.shape) for a in ins], [_whole(s) for s in out_shapes],
                [_sds(s, _F32) for s in out_shapes], name, tokens=tokens)(*ins)
    return [out[4 * i:4 * i + 4] for i in range(n_i)], out[4 * n_i:]


def _adamw(parts, w, m, v, name, tokens=()):
    rows, cols = w.shape
    tr = _row_tile(rows)
    n_parts = parts.shape[0]

    def body(p_ref, w_ref, m_ref, v_ref, g_ref, d_ref, nm_ref, nv_ref):
        g = p_ref[0].astype(_F32)
        for s in range(1, n_parts):
            g = g + p_ref[s].astype(_F32)
        new_m = ADAM_B1 * m_ref[...] + (1.0 - ADAM_B1) * g
        new_v = ADAM_B2 * v_ref[...] + (1.0 - ADAM_B2) * (g * g)
        m_hat = new_m / (1.0 - ADAM_B1 ** ADAM_STEP)
        v_hat = new_v / (1.0 - ADAM_B2 ** ADAM_STEP)
        g_ref[...] = g
        d_ref[...] = -ADAM_LR * (m_hat / (jnp.sqrt(v_hat) + ADAM_EPS) + ADAM_WD * w_ref[...])
        nm_ref[...] = new_m
        nv_ref[...] = new_v

    blk = _rows(tr, cols)
    return _call(body, (rows // tr,),
                 [pl.BlockSpec((n_parts, tr, cols), lambda i: (0, i, 0)), blk, blk, blk],
                 [blk] * 4, [_sds((rows, cols), _F32)] * 4, name, tokens=tokens)(parts, w, m, v)


_SMALL = ("g_pre_mix", "ssm_lambda_re", "ssm_lambda_im", "ssm_log_dt", "ssm_b_re", "ssm_b_im",
          "ssm_c_re", "ssm_c_im", "ssm_d", "b_glu", "attn_sinks", "g_ssm_out", "g_attn_out",
          "g_post_mix", "g_pre_ffn", "g_post_ffn")
_BIG = ("w_in", "w_glu", "w_out", "w_gate_up", "w_down")
_WEIGHTS = ("g_pre_mix", "w_in", "ssm_lambda_re", "ssm_lambda_im", "ssm_log_dt", "ssm_b_re", "ssm_b_im",
            "ssm_c_re", "ssm_c_im", "ssm_d", "w_glu", "b_glu", "attn_sinks", "g_ssm_out", "g_attn_out",
            "w_out", "g_post_mix", "g_pre_ffn", "w_gate_up", "w_down", "g_post_ffn")
_LANES = 128


_SHAPE_2D = {
    "g_pre_mix": (1, D_MODEL), "ssm_lambda_re": (SSM_GROUPS, SSM_STATE), "ssm_lambda_im": (SSM_GROUPS, SSM_STATE),
    "ssm_log_dt": (1, SSM_GROUPS), "ssm_b_re": (SSM_WIDTH, SSM_STATE), "ssm_b_im": (SSM_WIDTH, SSM_STATE),
    "ssm_c_re": (SSM_WIDTH, SSM_STATE), "ssm_c_im": (SSM_WIDTH, SSM_STATE), "ssm_d": (SSM_GROUPS, SSM_GROUP),
    "b_glu": (1, 2 * SSM_WIDTH), "attn_sinks": (1, N_Q_HEADS), "g_ssm_out": (1, SSM_WIDTH),
    "g_attn_out": (1, ATTN_WIDTH), "g_post_mix": (1, D_MODEL), "g_pre_ffn": (1, D_MODEL), "g_post_ffn": (1, D_MODEL)}
_ROW_WIDTH = {"g_pre_mix": D_MODEL, "b_glu": 2 * SSM_WIDTH, "attn_sinks": _LANES, "g_ssm_out": SSM_WIDTH,
              "g_attn_out": ATTN_WIDTH, "g_post_mix": D_MODEL, "g_pre_ffn": D_MODEL, "g_post_ffn": D_MODEL,
              "loss": _LANES}
_PER_GROUP_TRANSPOSED = ("ssm_b_re", "ssm_b_im")


def _to_2d(name, a):
    if name in _PER_GROUP_TRANSPOSED:
        a = a.reshape(SSM_GROUPS, SSM_STATE, SSM_GROUP).transpose(0, 2, 1)
    return a.reshape(_SHAPE_2D[name])


def _from_2d(name, a, shape):
    if name in _PER_GROUP_TRANSPOSED:
        a = a.reshape(SSM_GROUPS, SSM_GROUP, SSM_STATE).transpose(0, 2, 1)
    return a.reshape(shape)


def _row_slots(names):
    slots, row, col = {}, 0, 0
    for n in names:
        width = _ROW_WIDTH[n]
        if col + width > D_MODEL:
            row, col = row + 1, 0
        slots[n] = (row, col, width)
        col += width
    return slots


def _stack_rows(named, slots):
    n_rows = -(-(max(r for r, _, _ in slots.values()) + 1) // 8) * 8
    lines = []
    for r in range(n_rows):
        pieces = [named[n] for n, (row, _, _) in slots.items() if row == r]
        used = sum(p.shape[1] for p in pieces)
        if used < D_MODEL:
            pieces.append(jnp.zeros((1, D_MODEL - used), _F32))
        lines.append(jnp.concatenate(pieces, axis=1) if len(pieces) > 1 else pieces[0])
    return jnp.concatenate(lines, axis=0)


def kernel(x, positions, g_pre_mix, w_in, ssm_lambda_re, ssm_lambda_im, ssm_log_dt, ssm_b_re, ssm_b_im, ssm_c_re, ssm_c_im, ssm_d, w_glu, b_glu, attn_sinks, g_ssm_out, g_attn_out, w_out, g_post_mix, g_pre_ffn, w_gate_up, w_down, g_post_ffn, loss_target, m_g_pre_mix, m_w_in, m_ssm_lambda_re, m_ssm_lambda_im, m_ssm_log_dt, m_ssm_b_re, m_ssm_b_im, m_ssm_c_re, m_ssm_c_im, m_ssm_d, m_w_glu, m_b_glu, m_attn_sinks, m_g_ssm_out, m_g_attn_out, m_w_out, m_g_post_mix, m_g_pre_ffn, m_w_gate_up, m_w_down, m_g_post_ffn, v_g_pre_mix, v_w_in, v_ssm_lambda_re, v_ssm_lambda_im, v_ssm_log_dt, v_ssm_b_re, v_ssm_b_im, v_ssm_c_re, v_ssm_c_im, v_ssm_d, v_w_glu, v_b_glu, v_attn_sinks, v_g_ssm_out, v_g_attn_out, v_w_out, v_g_post_mix, v_g_pre_ffn, v_w_gate_up, v_w_down, v_g_post_ffn):
    w = dict(g_pre_mix=g_pre_mix, w_in=w_in, ssm_lambda_re=ssm_lambda_re, ssm_lambda_im=ssm_lambda_im,
             ssm_log_dt=ssm_log_dt, ssm_b_re=ssm_b_re, ssm_b_im=ssm_b_im, ssm_c_re=ssm_c_re, ssm_c_im=ssm_c_im,
             ssm_d=ssm_d, w_glu=w_glu, b_glu=b_glu, attn_sinks=attn_sinks, g_ssm_out=g_ssm_out,
             g_attn_out=g_attn_out, w_out=w_out, g_post_mix=g_post_mix, g_pre_ffn=g_pre_ffn,
             w_gate_up=w_gate_up, w_down=w_down, g_post_ffn=g_post_ffn)
    m = dict(g_pre_mix=m_g_pre_mix, w_in=m_w_in, ssm_lambda_re=m_ssm_lambda_re, ssm_lambda_im=m_ssm_lambda_im,
             ssm_log_dt=m_ssm_log_dt, ssm_b_re=m_ssm_b_re, ssm_b_im=m_ssm_b_im, ssm_c_re=m_ssm_c_re,
             ssm_c_im=m_ssm_c_im, ssm_d=m_ssm_d, w_glu=m_w_glu, b_glu=m_b_glu, attn_sinks=m_attn_sinks,
             g_ssm_out=m_g_ssm_out, g_attn_out=m_g_attn_out, w_out=m_w_out, g_post_mix=m_g_post_mix,
             g_pre_ffn=m_g_pre_ffn, w_gate_up=m_w_gate_up, w_down=m_w_down, g_post_ffn=m_g_post_ffn)
    v = dict(g_pre_mix=v_g_pre_mix, w_in=v_w_in, ssm_lambda_re=v_ssm_lambda_re, ssm_lambda_im=v_ssm_lambda_im,
             ssm_log_dt=v_ssm_log_dt, ssm_b_re=v_ssm_b_re, ssm_b_im=v_ssm_b_im, ssm_c_re=v_ssm_c_re,
             ssm_c_im=v_ssm_c_im, ssm_d=v_ssm_d, w_glu=v_w_glu, b_glu=v_b_glu, attn_sinks=v_attn_sinks,
             g_ssm_out=v_g_ssm_out, g_attn_out=v_g_attn_out, w_out=v_w_out, g_post_mix=v_g_post_mix,
             g_pre_ffn=v_g_pre_ffn, w_gate_up=v_w_gate_up, w_down=v_w_down, g_post_ffn=v_g_post_ffn)

    transposed = ("w_in", "w_glu", "w_gate_up")
    native_transposed = ("w_in", "w_gate_up")
    shard = {n: (w[n][0].T if n in transposed else w[n][0]).astype(_BF16) for n in _BIG}
    gathered = {}
    for cid, names in enumerate((("w_in",), ("w_glu", "w_out"), ("w_gate_up", "w_down")), start=1):
        lands = _sequencer_gather([shard[n] for n in names], "gather_" + names[0], cid)
        gathered.update({n: a.reshape(-1, a.shape[2]) for n, a in zip(names, lands)})

    def fetch(names, after):
        del after
        return [gathered[n] for n in names]

    sent = []
    ids = iter(range(4, 16))
    two_step = {}

    def publish(named):
        big = [n for n in named if n in _BIG]
        if set(big) == {"w_gate_up", "w_down"}:
            blocks = [named[n].reshape(N_DEV, -1, named[n].shape[1]) for n in big]
            two_step.update(names=big, blocks=blocks,
                            received=_sequencer_pair_exchange(blocks, "grads_pair", next(ids)))
            return [named[n] for n in big]
        rows = [n for n in named if n in _ROW_WIDTH]
        plain = [n for n in named if n not in big + rows]
        sources = [named[n].reshape(N_DEV, -1, named[n].shape[1]) for n in big]
        slots = _row_slots(rows)
        if rows:
            sources.append(_stack_rows(named, slots))
        sources += [named[n] for n in plain]
        flags = [True] * len(big) + [False] * (len(sources) - len(big))
        cid = next(ids)
        sent.append((big, slots, plain, _sequencer_exchange(sources, flags, "grads_%d" % cid, cid)))
        return [named[n] for n in big]

    def progress(after):
        core = lax.axis_index("c").astype(jnp.int32).reshape(1)
        partials = [_pair_sum(b, r, core, "pair_sum_" + n, [after])
                    for n, b, r in zip(two_step["names"], two_step["blocks"], two_step["received"])]
        sent.append((two_step["names"], {}, [], _sequencer_chip_exchange(partials, "grads_chips", next(ids))))
        return partials

    p = {n: w[n] for n in _SMALL}
    grad_x = _local_step(x[0], positions[0], loss_target[0], p, fetch, publish, progress)

    state = {n: [_to_2d(n, a) for a in (w[n], m[n], v[n])] for n in _SMALL}
    result = {}
    total_loss = None
    chain = []
    for big, slots, plain, lands in sent:
        lands = list(lands)
        after = list(chain)
        for name in big:
            part = lands.pop(0)
            if name in native_transposed:
                updated = _adamw(part, w[name][0].T, m[name][0].T, v[name][0].T, "adamw_" + name, after)
                result[name] = [a.T[None] for a in updated]
                chain.append(updated[3])
                continue
            if name in transposed:
                part = _sum_parts(part, "sum_" + name, after).T[None]
            updated = _adamw(part, w[name][0], m[name][0], v[name][0], "adamw_" + name, after)
            result[name] = [a[None] for a in updated]
            chain.append(updated[3])
        parts, items, sums, names = [], [], [], []
        if slots:
            parts.append(lands.pop(0))
            for name, (row, col, _) in slots.items():
                if name == "loss":
                    sums.append((0, slice(row, row + 1), slice(col, col + _LANES)))
                else:
                    items.append((0, slice(row, row + 1), slice(col, col + _SHAPE_2D[name][1]), *state[name]))
                    names.append(name)
        for name in plain:
            packed = _SSM_PACK if name == "ssm_pack" else {name: (0, _SHAPE_2D[name][0], 0, _SHAPE_2D[name][1])}
            for member, (first, rows_n, lane, cols_n) in packed.items():
                items.append((len(parts), slice(first, first + rows_n), slice(lane, lane + cols_n), *state[member]))
                names.append(member)
            parts.append(lands.pop(0))
        if items:
            updated, summed = _adamw_small(parts, items, sums, "adamw_small_" + names[0], after)
            chain.append(updated[0][3])
            result.update(dict(zip(names, updated)))
            if summed:
                total_loss = summed[0][0, 0]

    out = [total_loss, grad_x[None]]
    for kind in range(4):
        out += [_from_2d(n, result[n][kind], w[n].shape) for n in _WEIGHTS]
    return tuple(out)
```

```python
import math

import numpy as np
import jax
import jax.numpy as jnp
from jax import lax
from jax.experimental import pallas as pl
from jax.experimental.pallas import tpu as pltpu
from jax.experimental.pallas import tpu_sc as plsc

D_MODEL = 1024
SSM_WIDTH = 512
SSM_GROUP = 16
SSM_GROUPS = 32
SSM_STATE = 64
N_STATE = SSM_GROUPS * SSM_STATE
ATTN_WIDTH = 512
HEAD_DIM = 64
N_Q_HEADS = 8
N_KV_HEADS = 2
Q_PER_KV = 4
KV_WIDTH = 128
IN_WIDTH = 1280
BLOCK = 128
ROPE_DIM = 16
ROPE_THETA = 500000.0
D_FF = 2816
NORM_EPS = 1e-6
MASK_VALUE = -1e30
ADAM_LR = 0.001
ADAM_B1 = 0.9
ADAM_B2 = 0.999
ADAM_EPS = 1e-08
ADAM_WD = 0.01
ADAM_STEP = 10

N_DEV = 8
SCAN_CHUNKS = 8
SCAN_UNROLL = 8
FFN_CHUNK = 2816
TOKEN_TILE = 256
VMEM_LIMIT = 56 * 1024 * 1024

_F32 = jnp.float32
_BF16 = jnp.bfloat16
_MXU = jnp.bfloat16

_NN = ((1,), (0,))
_NT = ((1,), (1,))
_TN = ((0,), (0,))


def _dot(a, b, dims):
    return lax.dot_general(a.astype(_MXU), b.astype(_MXU), (dims, ((), ())),
                           preferred_element_type=_F32)


def _dot_exact(a, b, dims):
    return lax.dot_general(a.astype(_F32), b.astype(_F32), (dims, ((), ())),
                           precision=lax.Precision.HIGHEST, preferred_element_type=_F32)


def _iota(shape, dim):
    return lax.broadcasted_iota(jnp.int32, shape, dim)


def _rms_fwd(x, g):
    r = lax.rsqrt(jnp.mean(x * x, axis=-1, keepdims=True) + NORM_EPS)
    return x * r * g, r


def _rms_bwd(dy, x, g, r):
    a = dy * g
    xn = x * r
    dx = r * (a - xn * jnp.mean(a * xn, axis=-1, keepdims=True))
    dg = jnp.sum(dy * xn, axis=0, keepdims=True)
    return dx, dg


def _call(body, grid, in_specs, out_specs, out_shape, name, scratch=(), tokens=()):
    params = pltpu.CompilerParams(dimension_semantics=("arbitrary",) * len(grid),
                                  vmem_limit_bytes=VMEM_LIMIT)
    n_in, n_tok = len(in_specs), len(tokens)

    def run(*refs):
        return body(*refs[:n_in], *refs[n_in + n_tok:])

    call = pl.pallas_call(run, grid=grid,
                          in_specs=list(in_specs) + [pl.BlockSpec(memory_space=pl.ANY)] * n_tok,
                          out_specs=out_specs, out_shape=out_shape, scratch_shapes=list(scratch),
                          compiler_params=params, name=name)
    return lambda *args: call(*args, *tokens)


def _rows(tm, n):
    return pl.BlockSpec((tm, n), lambda i: (i, 0))


def _whole(shape):
    nd = len(shape)
    return pl.BlockSpec(shape, lambda i: (0,) * nd)


def _sds(shape, dtype):
    return jax.ShapeDtypeStruct(shape, dtype)


def _tile(L):
    return min(TOKEN_TILE, L)


def _chunk_tile(L):
    return L // SCAN_CHUNKS


def _chunk_block(L, n):
    return pl.BlockSpec((_chunk_tile(L), n), lambda i: (0, i))


def _chunk_shape(L, n):
    return (_chunk_tile(L), SCAN_CHUNKS * n)


def _accumulate(ref, val, first):
    @pl.when(first)
    def _():
        ref[...] = val

    @pl.when(jnp.logical_not(first))
    def _():
        ref[...] += val


def _rope_rows():
    half = ROPE_DIM // 2
    inv = (np.float32(ROPE_THETA) ** (-np.arange(half, dtype=np.float32) * np.float32(2.0) / np.float32(ROPE_DIM))).astype(np.float32)
    col = np.arange(KV_WIDTH) % HEAD_DIM
    freq = np.where(col < ROPE_DIM, inv[col % half], 0.0).astype(np.float32)
    sign = np.where(col < half, -1.0, np.where(col < ROPE_DIM, 1.0, 0.0)).astype(np.float32)
    return freq[None, :], sign[None, :]


def _rope_tables(pos_col):
    L = pos_col.shape[0]
    tm = _tile(L)
    freq, sign = _rope_rows()

    def body(pos_ref, freq_ref, sign_ref, cos_ref, sin_ref):
        ang = pos_ref[...].astype(_F32) * freq_ref[...]
        cos_ref[...] = jnp.cos(ang)
        sin_ref[...] = jnp.sin(ang) * sign_ref[...]

    return _call(body, (L // tm,),
                 [_rows(tm, 1), _whole((1, KV_WIDTH)), _whole((1, KV_WIDTH))],
                 [_rows(tm, KV_WIDTH), _rows(tm, KV_WIDTH)],
                 [_sds((L, KV_WIDTH), _F32)] * 2, "rope_tables")(pos_col, jnp.asarray(freq), jnp.asarray(sign))


def _widen(t, width):
    return t if width == KV_WIDTH else jnp.concatenate([t] * (width // KV_WIDTH), axis=1)


def _rope_partner(t):
    w = t.shape[1]
    in_head = _iota((1, w), 1) & (HEAD_DIM - 1)
    second = jnp.where(in_head < ROPE_DIM, pltpu.roll(t, ROPE_DIM // 2, 1), 0.0)
    return jnp.where(in_head < ROPE_DIM // 2, pltpu.roll(t, w - ROPE_DIM // 2, 1), second)


def _rope_apply(t, cos_t, sin_t):
    w = t.shape[1]
    return t * _widen(cos_t, w) + _rope_partner(t) * _widen(sin_t, w)


def _rope_transpose(dt, cos_t, sin_t):
    w = dt.shape[1]
    return dt * _widen(cos_t, w) + _rope_partner(dt * _widen(sin_t, w))


def _in_proj(x, g_pre_mix, w_in, cos_t, sin_t):
    L = x.shape[0]
    tm = _chunk_tile(L)

    def body(x_ref, g_ref, w_ref, cos_ref, sin_ref, hn_ref, u_ref, q_ref, k_ref, v_ref):
        hn, _ = _rms_fwd(x_ref[...], g_ref[...])
        hn = hn.astype(_BF16)
        hn_ref[...] = hn
        proj = _dot(hn, w_ref[...], _NT)
        u_ref[...] = proj[:, :SSM_WIDTH]
        q = proj[:, SSM_WIDTH:SSM_WIDTH + ATTN_WIDTH]
        k = proj[:, SSM_WIDTH + ATTN_WIDTH:SSM_WIDTH + ATTN_WIDTH + KV_WIDTH]
        cos_v, sin_v = cos_ref[...], sin_ref[...]
        q_ref[...] = _rope_apply(q, cos_v, sin_v).astype(_BF16)
        k_ref[...] = _rope_apply(k, cos_v, sin_v).astype(_BF16)
        v_ref[...] = proj[:, SSM_WIDTH + ATTN_WIDTH + KV_WIDTH:].astype(_BF16)

    return _call(body, (L // tm,),
                 [_rows(tm, D_MODEL), _whole((1, D_MODEL)), _whole((IN_WIDTH, D_MODEL)),
                  _rows(tm, KV_WIDTH), _rows(tm, KV_WIDTH)],
                 [_rows(tm, D_MODEL), _chunk_block(L, SSM_WIDTH), _rows(tm, ATTN_WIDTH),
                  _rows(tm, KV_WIDTH), _rows(tm, KV_WIDTH)],
                 [_sds((L, D_MODEL), _BF16), _sds(_chunk_shape(L, SSM_WIDTH), _F32), _sds((L, ATTN_WIDTH), _BF16),
                  _sds((L, KV_WIDTH), _BF16), _sds((L, KV_WIDTH), _BF16)],
                 "in_proj")(x, g_pre_mix, w_in, cos_t, sin_t)


def _s5_discretize(lam_re, lam_im, log_dt):
    lr = jnp.minimum(lam_re, -1e-4)
    li = lam_im
    dt = jnp.exp(log_dt)
    mag = jnp.exp(lr * dt)
    ar = mag * jnp.cos(li * dt)
    ai = mag * jnp.sin(li * dt)
    den = lr * lr + li * li
    fr = ((ar - 1.0) * lr + ai * li) / den
    fi = (ai * lr - (ar - 1.0) * li) / den
    return ar, ai, fr, fi


SUPER = 4
SB_STATE = N_STATE // SUPER
SB_WIDTH = SSM_WIDTH // SUPER


def _sb_state(k):
    return slice(SB_STATE * k, SB_STATE * (k + 1))


def _sb_width(k):
    return slice(SB_WIDTH * k, SB_WIDTH * (k + 1))


def _dt_column(log_dt_row):
    eye = _iota((SSM_GROUPS, SSM_GROUPS), 0) == _iota((SSM_GROUPS, SSM_GROUPS), 1)
    return jnp.sum(jnp.where(eye, log_dt_row, 0.0), axis=1, keepdims=True)


def _group_masks():
    e64 = ((_iota((SSM_STATE, N_STATE), 1) & (SSM_STATE - 1)) == _iota((SSM_STATE, N_STATE), 0)).astype(_F32)
    own = _iota((SSM_GROUPS, N_STATE), 0) == (_iota((SSM_GROUPS, N_STATE), 1) >> 6)
    return e64, own


def _rows_of_group():
    return ((_iota((SSM_WIDTH, SSM_GROUPS), 0) >> 4) == _iota((SSM_WIDTH, SSM_GROUPS), 1)).astype(_F32)


def _ssm_prep(lam_re, lam_im, log_dt, b_re, b_im, c_re, c_im):
    def body(lr_ref, li_ref, ld_ref, bre, bim, cre, cim, ar_ref, ai_ref, btr, bti, ctr, cti):
        ar, ai, fr, fi = _s5_discretize(lr_ref[...], li_ref[...], _dt_column(ld_ref[...]))
        e64, own = _group_masks()
        mask_c = (_iota((SSM_WIDTH, N_STATE), 0) >> 4) == (_iota((SSM_WIDTH, N_STATE), 1) >> 6)

        def to_row(t):
            return jnp.sum(jnp.where(own, _dot_exact(t, e64, _NN), 0.0), axis=0, keepdims=True)

        def fold(m):
            full = jnp.where(mask_c, _dot(m, e64, _NN), 0.0)
            return sum(full[_sb_width(k), :] for k in range(SUPER)).astype(_BF16)

        ar_ref[...] = to_row(ar)
        ai_ref[...] = to_row(ai)
        spread = _rows_of_group()
        fr_t = _dot_exact(spread, fr, _NN)
        fi_t = _dot_exact(spread, fi, _NN)
        btr[...] = fold(fr_t * bre[...] - fi_t * bim[...])
        bti[...] = fold(fr_t * bim[...] + fi_t * bre[...])
        ctr[...] = fold(cre[...])
        cti[...] = fold(cim[...])

    row = (1, N_STATE)
    ins = [lam_re, lam_im, log_dt, b_re, b_im, c_re, c_im]
    return _call(body, (1,), [_whole(a.shape) for a in ins],
                 [_whole(row), _whole(row)] + [_whole((SB_WIDTH, N_STATE))] * 4,
                 [_sds(row, _F32), _sds(row, _F32)] + [_sds((SB_WIDTH, N_STATE), _BF16)] * 4,
                 "ssm_prep")(*ins)


def _complex_power(ar, ai, n):
    pr, pi = jnp.ones_like(ar), jnp.zeros_like(ai)
    while n:
        if n & 1:
            pr, pi = pr * ar - pi * ai, pr * ai + pi * ar
        ar, ai = ar * ar - ai * ai, 2.0 * ar * ai
        n >>= 1
    return pr, pi


def _chunk_carries(er, ei, pr, pi, reverse):
    rows = _iota(er.shape, 0)
    sr = jnp.zeros_like(pr)
    si = jnp.zeros_like(pi)
    out_r = jnp.zeros_like(er)
    out_i = jnp.zeros_like(ei)
    order = range(SCAN_CHUNKS - 1, 0, -1) if reverse else range(SCAN_CHUNKS - 1)
    for c in order:
        e_r = er[c:c + 1, :]
        e_i = ei[c:c + 1, :]
        sr, si = pr * sr - pi * si + e_r, pr * si + pi * sr + e_i
        nxt = c - 1 if reverse else c + 1
        out_r = jnp.where(rows == nxt, sr, out_r)
        out_i = jnp.where(rows == nxt, si, out_i)
    return out_r, out_i


_GELU_K = math.sqrt(2.0 / math.pi)
_GELU_C = 0.044715


def _gelu(y):
    return 0.5 * y * (1.0 + jnp.tanh(_GELU_K * (y + _GELU_C * y * y * y)))


def _gelu_grad(y):
    t = jnp.tanh(_GELU_K * (y + _GELU_C * y * y * y))
    return 0.5 * (1.0 + t) + 0.5 * y * (1.0 - t * t) * _GELU_K * (1.0 + 3.0 * _GELU_C * y * y)


def _step_rows(t):
    return pl.ds(pl.multiple_of(t * SCAN_CHUNKS, SCAN_CHUNKS), SCAN_CHUNKS)


def _scan_in_place(br, bi, ar, ai, T, carries=None):
    W = br.shape[1]
    ar8 = jnp.broadcast_to(ar, (SCAN_CHUNKS, W))
    ai8 = jnp.broadcast_to(ai, (SCAN_CHUNKS, W))

    def local(t, c):
        cr, ci = c
        rows = _step_rows(t)
        return ar8 * cr - ai8 * ci + br[rows, :], ar8 * ci + ai8 * cr + bi[rows, :]

    if carries is None:
        zero = jnp.zeros((SCAN_CHUNKS, W), _F32)
        er, ei = lax.fori_loop(0, T, local, (zero, zero), unroll=SCAN_UNROLL)
        pr, pi = _complex_power(ar, ai, T)
        carries = _chunk_carries(er, ei, pr, pi, reverse=False)

    def final(t, c):
        nr, ni = local(t, c)
        rows = _step_rows(t)
        br[rows, :] = nr
        bi[rows, :] = ni
        return nr, ni

    lax.fori_loop(0, T, final, carries, unroll=SCAN_UNROLL)
    return carries


def _scan_reverse_in_place(dr, di, xr, xi, ar, ai, T):
    W = dr.shape[1]
    ar8 = jnp.broadcast_to(ar, (SCAN_CHUNKS, W))
    ai8 = jnp.broadcast_to(ai, (SCAN_CHUNKS, W))

    def local(t, c):
        cr, ci = c
        rows = _step_rows(t)
        return ar8 * cr + ai8 * ci + dr[rows, :], ar8 * ci - ai8 * cr + di[rows, :]

    zero = jnp.zeros((SCAN_CHUNKS, W), _F32)
    er, ei = lax.fori_loop(0, T, lambda k, c: local(T - 1 - k, c), (zero, zero), unroll=SCAN_UNROLL)
    pr, pi = _complex_power(ar, -ai, T)
    sr, si = _chunk_carries(er, ei, pr, pi, reverse=True)

    def grad_a(acc, nr, ni, xpr, xpi):
        return acc[0] + nr * xpr + ni * xpi, acc[1] + ni * xpr - nr * xpi

    def final(k, c):
        t = T - 1 - k
        nr, ni = local(t, c[:2])
        rows = _step_rows(t)
        dr[rows, :] = nr
        di[rows, :] = ni
        before = _step_rows(t - 1)
        gr, gi = grad_a(c[2:], nr, ni, xr[before, :], xi[before, :])
        return nr, ni, gr, gi

    cr, ci, gr, gi = lax.fori_loop(0, T - 1, final, (sr, si, zero, zero), unroll=SCAN_UNROLL)
    nr, ni = local(0, (cr, ci))
    dr[_step_rows(0), :] = nr
    di[_step_rows(0), :] = ni
    first = _iota((SCAN_CHUNKS, W), 0) == 0
    last = _step_rows(T - 1)
    xpr = jnp.where(first, 0.0, pltpu.roll(xr[last, :], 1, 0))
    xpi = jnp.where(first, 0.0, pltpu.roll(xi[last, :], 1, 0))
    gr, gi = grad_a((gr, gi), nr, ni, xpr, xpi)
    return jnp.sum(gr, axis=0, keepdims=True), jnp.sum(gi, axis=0, keepdims=True)


def _ssm_super_specs(L):
    width = pl.BlockSpec((L, SB_WIDTH), lambda k: (0, k))
    matrix = pl.BlockSpec((SB_WIDTH, SB_STATE), lambda k: (0, k))
    row = pl.BlockSpec((1, SB_STATE), lambda k: (0, k))
    return width, matrix, row


def _ssm_states(u_ref, br_ref, bi_ref, ar_ref, ai_ref, xr, xi, T, carries=None):
    ub = u_ref[...].astype(_BF16)
    xr[...] = _dot(ub, br_ref[...], _NN)
    xi[...] = _dot(ub, bi_ref[...], _NN)
    return _scan_in_place(xr, xi, ar_ref[...], ai_ref[...], T, carries)


def _ssm_core_fwd(u, bt_re, bt_im, ct_re, ct_im, a_re, a_im):
    L = u.shape[0]
    T = L // SCAN_CHUNKS

    def body(u_ref, br_ref, bi_ref, cr_ref, ci_ref, ar_ref, ai_ref, y_ref, sr_ref, si_ref, xr, xi):
        sr_ref[...], si_ref[...] = _ssm_states(u_ref, br_ref, bi_ref, ar_ref, ai_ref, xr, xi, T)
        y_ref[...] = _dot(xr[...], cr_ref[...], _NT) - _dot(xi[...], ci_ref[...], _NT)

    width, matrix, row = _ssm_super_specs(L)
    carry = pl.BlockSpec((SCAN_CHUNKS, SB_STATE), lambda k: (0, k))
    return _call(body, (SUPER,), [width, matrix, matrix, matrix, matrix, row, row], [width, carry, carry],
                 [_sds((L, SSM_WIDTH), _F32)] + [_sds((SCAN_CHUNKS, N_STATE), _F32)] * 2, "ssm_core_fwd",
                 scratch=[pltpu.VMEM((L, SB_STATE), _F32)] * 2)(u, bt_re, bt_im, ct_re, ct_im, a_re, a_im)


def _ssm_core_bwd(u, dy, dud, carry_re, carry_im, bt_re, bt_im, ct_re, ct_im, a_re, a_im, tokens=()):
    L = u.shape[0]
    T = L // SCAN_CHUNKS

    def body(u_ref, dy_ref, dud_ref, sr_ref, si_ref, br_ref, bi_ref, cr_ref, ci_ref, ar_ref, ai_ref,
             du_ref, dcr_ref, dci_ref, dbr_ref, dbi_ref, dar_ref, dai_ref, xr, xi, lr, li):
        _ssm_states(u_ref, br_ref, bi_ref, ar_ref, ai_ref, xr, xi, T, (sr_ref[...], si_ref[...]))
        dyb = dy_ref[...]
        lr[...] = _dot(dyb, cr_ref[...], _NN)
        li[...] = -_dot(dyb, ci_ref[...], _NN)
        da_re, da_im = _scan_reverse_in_place(lr, li, xr, xi, ar_ref[...], ai_ref[...], T)
        dar_ref[...] = da_re
        dai_ref[...] = da_im
        du_ref[...] = _dot(lr[...], br_ref[...], _NT) + _dot(li[...], bi_ref[...], _NT) + dud_ref[...]
        ub = u_ref[...].astype(_BF16)
        dcr_ref[...] = _dot(dyb, xr[...], _TN)
        dci_ref[...] = _dot(dyb, xi[...], _TN)
        dbr_ref[...] = _dot(ub, lr[...], _TN)
        dbi_ref[...] = _dot(ub, li[...], _TN)

    width, matrix, row = _ssm_super_specs(L)
    carry = pl.BlockSpec((SCAN_CHUNKS, SB_STATE), lambda k: (0, k))
    return _call(body, (SUPER,), [width, width, width, carry, carry, matrix, matrix, matrix, matrix, row, row],
                 [width] + [matrix] * 4 + [row] * 2,
                 [_sds((L, SSM_WIDTH), _F32)] + [_sds((SB_WIDTH, N_STATE), _F32)] * 4 + [_sds((1, N_STATE), _F32)] * 2,
                 "ssm_core_bwd", scratch=[pltpu.VMEM((L, SB_STATE), _F32)] * 4,
                 tokens=tokens)(u, dy, dud, carry_re, carry_im, bt_re, bt_im, ct_re, ct_im, a_re, a_im)


def _ssm_out(cx, u, d_row, w_glu, b_glu, g_ssm):
    L = u.shape[0]
    tm = _tile(L)

    def body(cx_ref, u_ref, d_ref, w_ref, b_ref, g_ref, y_ref, z_ref, n_ref, stage):
        y = cx_ref[...] + d_ref[...] * u_ref[...]
        y_ref[...] = y
        z = _dot(_gelu(y), w_ref[...], _NT) + b_ref[...]
        z_ref[...] = z
        out = z[:, :SSM_WIDTH] * jax.nn.sigmoid(z[:, SSM_WIDTH:])
        n, _ = _rms_fwd(out, g_ref[...])
        for k in range(SSM_WIDTH // _LANES):
            stage[k] = n[:, _LANES * k:_LANES * (k + 1)]
            for c in range(SCAN_CHUNKS):
                rows = stage[k, pl.ds(c, tm // SCAN_CHUNKS, stride=SCAN_CHUNKS), :]
                lane = SSM_WIDTH * c + _LANES * k
                n_ref[:, lane:lane + _LANES] = rows.astype(_BF16)

    return _call(body, (L // tm,),
                 [_rows(tm, SSM_WIDTH), _rows(tm, SSM_WIDTH), _whole((1, SSM_WIDTH)),
                  _whole((2 * SSM_WIDTH, SSM_WIDTH)), _whole((1, 2 * SSM_WIDTH)), _whole((1, SSM_WIDTH))],
                 [_rows(tm, SSM_WIDTH), _rows(tm, 2 * SSM_WIDTH), _rows(tm // SCAN_CHUNKS, SCAN_CHUNKS * SSM_WIDTH)],
                 [_sds((L, SSM_WIDTH), _F32), _sds((L, 2 * SSM_WIDTH), _F32), _sds(_chunk_shape(L, SSM_WIDTH), _BF16)],
                 "ssm_out", scratch=[pltpu.VMEM((SSM_WIDTH // _LANES, tm, _LANES), _F32)])(
        cx, u, d_row, w_glu, b_glu, g_ssm)


def _ssm_out_bwd(dn, y, z, u, d_row, w_glu, g_ssm):
    L = u.shape[0]
    tm = _tile(L)

    def body(dn_ref, y_ref, z_ref, u_ref, d_ref, w_ref, g_ref,
             gy_ref, dz_ref, dy_ref, dud_ref, dg_ref, db_ref, dd_ref, stage):
        first = pl.program_id(0) == 0
        for k in range(SSM_WIDTH // _LANES):
            for c in range(SCAN_CHUNKS):
                lane = SSM_WIDTH * c + _LANES * k
                stage[k, pl.ds(c, tm // SCAN_CHUNKS, stride=SCAN_CHUNKS), :] = dn_ref[:, lane:lane + _LANES]
        dn = jnp.concatenate([stage[k] for k in range(SSM_WIDTH // _LANES)], axis=1)
        z = z_ref[...]
        z1, z2 = z[:, :SSM_WIDTH], z[:, SSM_WIDTH:]
        sig = jax.nn.sigmoid(z2)
        out = z1 * sig
        g = g_ref[...]
        _, r = _rms_fwd(out, g)
        dout, dg = _rms_bwd(dn, out, g, r)
        _accumulate(dg_ref, dg, first)
        dz = jnp.concatenate([dout * sig, dout * z1 * sig * (1.0 - sig)], axis=1)
        _accumulate(db_ref, jnp.sum(dz, axis=0, keepdims=True), first)
        dzb = dz.astype(_BF16)
        dz_ref[...] = dzb
        y = y_ref[...]
        gy_ref[...] = _gelu(y).astype(_BF16)
        dy = _dot(dzb, w_ref[...], _NN) * _gelu_grad(y)
        u = u_ref[...]
        _accumulate(dd_ref, jnp.sum(dy * u, axis=0, keepdims=True), first)
        dud_ref[...] = d_ref[...] * dy
        dy_ref[...] = dy.astype(_BF16)

    row = _whole((1, SSM_WIDTH))
    return _call(body, (L // tm,),
                 [_rows(tm // SCAN_CHUNKS, SCAN_CHUNKS * SSM_WIDTH), _rows(tm, SSM_WIDTH), _rows(tm, 2 * SSM_WIDTH),
                  _rows(tm, SSM_WIDTH), row, _whole((2 * SSM_WIDTH, SSM_WIDTH)), row],
                 [_rows(tm, SSM_WIDTH), _rows(tm, 2 * SSM_WIDTH), _rows(tm, SSM_WIDTH), _rows(tm, SSM_WIDTH),
                  row, _whole((1, 2 * SSM_WIDTH)), row],
                 [_sds((L, SSM_WIDTH), _BF16), _sds((L, 2 * SSM_WIDTH), _BF16), _sds((L, SSM_WIDTH), _BF16),
                  _sds((L, SSM_WIDTH), _F32),
                  _sds((1, SSM_WIDTH), _F32), _sds((1, 2 * SSM_WIDTH), _F32), _sds((1, SSM_WIDTH), _F32)],
                 "ssm_out_bwd", scratch=[pltpu.VMEM((SSM_WIDTH // _LANES, tm, _LANES), _F32)])(
        dn, y, z, u, d_row, w_glu, g_ssm)


_SSM_PACK = {"ssm_b_re": (0, SSM_WIDTH, 0, SSM_STATE), "ssm_c_re": (0, SSM_WIDTH, 64, SSM_STATE),
             "ssm_b_im": (512, SSM_WIDTH, 0, SSM_STATE), "ssm_c_im": (512, SSM_WIDTH, 64, SSM_STATE),
             "ssm_lambda_re": (1024, SSM_GROUPS, 0, SSM_STATE), "ssm_lambda_im": (1024, SSM_GROUPS, 64, SSM_STATE),
             "ssm_d": (1056, SSM_GROUPS, 0, SSM_GROUP), "ssm_log_dt": (1088, 1, 0, SSM_GROUPS)}
_PACK_TILE = 16
_SSM_PACK_ROWS = 1088 + _PACK_TILE


def _ssm_param_bwd(da_re, da_im, dbt_re, dbt_im, dct_re, dct_im, lam_re, lam_im, log_dt, b_re, b_im, g_d):
    def body(dar, dai, dbr, dbi, dcr, dci, lr_ref, li_ref, ld_ref, bre_ref, bim_ref, gd_ref, pack_ref):
        lane_in = _iota((SSM_STATE, _LANES), 0)
        lane_out = _iota((SSM_STATE, _LANES), 1)
        low = (lane_out == lane_in).astype(_F32)
        high = (lane_out == lane_in + SSM_STATE).astype(_F32)

        def side_by_side(a, b):
            return _dot_exact(a, low, _NN) + _dot_exact(b, high, _NN)

        tail = _SSM_PACK["ssm_d"][0]
        pack_ref[tail:, :] = jnp.zeros((_SSM_PACK_ROWS - tail, _LANES), _BF16)
        pack_ref[tail:tail + SSM_GROUPS, 0:SSM_GROUP] = gd_ref[...].astype(_BF16)
        own_c = (_iota((SB_WIDTH, SB_STATE), 0) >> 4) == (_iota((SB_WIDTH, SB_STATE), 1) >> 6)

        def unfold(ref):
            blocks = []
            for k in range(SUPER):
                t = jnp.where(own_c, ref[:, _sb_state(k)], 0.0)
                t = sum(t[:, 128 * i:128 * (i + 1)] for i in range(SB_STATE // 128))
                blocks.append((t + pltpu.roll(t, SSM_STATE, 1))[:, :SSM_STATE])
            return jnp.concatenate(blocks, axis=0)

        dbb_re, dbb_im = unfold(dbr), unfold(dbi)
        b_re, b_im = bre_ref[...], bim_ref[...]
        dt_col = _dt_column(ld_ref[...])
        (_, _, fr, fi), vjp = jax.vjp(_s5_discretize, lr_ref[...], li_ref[...], dt_col)
        spread = _rows_of_group()
        fr_t = _dot_exact(spread, fr, _NN)
        fi_t = _dot_exact(spread, fi, _NN)
        pack_ref[0:SSM_WIDTH, :] = side_by_side(fr_t * dbb_re + fi_t * dbb_im, unfold(dcr)).astype(_BF16)
        pack_ref[SSM_WIDTH:2 * SSM_WIDTH, :] = side_by_side(fr_t * dbb_im - fi_t * dbb_re, -unfold(dci)).astype(_BF16)
        d_fr = _dot_exact(spread, dbb_re * b_re + dbb_im * b_im, _TN)
        d_fi = _dot_exact(spread, dbb_im * b_re - dbb_re * b_im, _TN)
        e64, own = _group_masks()

        def from_row(ref):
            return _dot_exact(jnp.where(own, ref[...], 0.0), e64, _NT)

        d_lr, d_li, d_dt = vjp((from_row(dar), from_row(dai), d_fr, d_fi))
        lam_rows = _SSM_PACK["ssm_lambda_re"][0]
        pack_ref[lam_rows:lam_rows + SSM_GROUPS, :] = side_by_side(d_lr, d_li).astype(_BF16)
        eye = (_iota((SSM_GROUPS, SSM_GROUPS), 0) == _iota((SSM_GROUPS, SSM_GROUPS), 1)).astype(_F32)
        dt_row = _SSM_PACK["ssm_log_dt"][0]
        pack_ref[dt_row:dt_row + _PACK_TILE, 0:SSM_GROUPS] = _dot_exact(
            jnp.broadcast_to(d_dt, (SSM_GROUPS, 128)), eye, _TN)[0:_PACK_TILE].astype(_BF16)

    ins = [da_re, da_im, dbt_re, dbt_im, dct_re, dct_im, lam_re, lam_im, log_dt, b_re, b_im, g_d]
    out = (_SSM_PACK_ROWS, _LANES)
    return _call(body, (1,), [_whole(a.shape) for a in ins], _whole(out), _sds(out, _BF16), "ssm_param_bwd")(*ins)


def _head_spread(j):
    r = _iota((KV_WIDTH, 256), 0)
    c = _iota((KV_WIDTH, 256), 1)
    return (r == HEAD_DIM * j + (c & (HEAD_DIM - 1))).astype(_BF16)


STACK = Q_PER_KV * BLOCK


def _stack_heads(t):
    lane_head = _iota((1, 256), 1) >> 6
    return jnp.concatenate([jnp.where(lane_head == g, t, jnp.zeros_like(t)) for g in range(Q_PER_KV)], axis=0)


def _unstack_heads(t):
    lane_head = _iota((1, 256), 1) >> 6
    return sum(jnp.where(lane_head == g, t[BLOCK * g:BLOCK * (g + 1)], 0.0) for g in range(Q_PER_KV))


def _stacked_sinks(sink_ref, j):
    block = _iota((STACK, 1), 0) >> 7
    col = jnp.full((STACK, 1), sink_ref[Q_PER_KV * j], _F32)
    for g in range(1, Q_PER_KV):
        col = jnp.where(block == g, sink_ref[Q_PER_KV * j + g], col)
    return col


def _fold_heads(t, j):
    t = t[:, :KV_WIDTH] + t[:, KV_WIDTH:]
    t = t + pltpu.roll(t, HEAD_DIM, 1)
    return jnp.where((_iota((1, KV_WIDTH), 1) >> 6) == j, t, 0.0)


def _attn_scores(q_stacked, kt, blk, sink):
    s = _dot(q_stacked, kt, _NT) * (HEAD_DIM ** -0.5)
    qi = _iota((STACK, 2 * BLOCK), 0) & (BLOCK - 1)
    kj = _iota((STACK, 2 * BLOCK), 1)
    rel = qi + BLOCK - kj
    valid = (rel >= 0) & (rel < BLOCK) & (blk * BLOCK - BLOCK + kj >= 0)
    s = jnp.where(valid, s, MASK_VALUE)
    m = jnp.maximum(jnp.max(s, axis=-1, keepdims=True), sink)
    p = jnp.exp(s - m)
    e_sink = jnp.exp(sink - m)
    den = jnp.sum(p, axis=-1, keepdims=True) + e_sink
    return p / den, e_sink / den


def _attn_specs():
    prev = lambda i: (jnp.maximum(i - 1, 0), 0)
    cur = lambda i: (i, 0)
    kv = [pl.BlockSpec((BLOCK, KV_WIDTH), prev), pl.BlockSpec((BLOCK, KV_WIDTH), cur)]
    return [pl.BlockSpec((BLOCK, ATTN_WIDTH), cur)] + kv + kv


def _attn_fwd(q, k, v, sinks, g_attn):
    L = q.shape[0]

    def body(q_ref, kp_ref, kc_ref, vp_ref, vc_ref, sink_ref, g_ref, o_ref, n_ref):
        blk = pl.program_id(0)
        kwin = jnp.concatenate([kp_ref[...], kc_ref[...]], axis=0)
        vwin = jnp.concatenate([vp_ref[...], vc_ref[...]], axis=0)
        halves = []
        for j in range(N_KV_HEADS):
            spread = _head_spread(j)
            kt = _dot(kwin, spread, _NN).astype(_BF16)
            vt = _dot(vwin, spread, _NN).astype(_BF16)
            qs = _stack_heads(q_ref[:, 256 * j:256 * (j + 1)])
            p, _ = _attn_scores(qs, kt, blk, _stacked_sinks(sink_ref, j))
            halves.append(_unstack_heads(_dot(p, vt, _NN)))
        o = jnp.concatenate(halves, axis=1)
        o_ref[...] = o
        n, _ = _rms_fwd(o, g_ref[...])
        n_ref[...] = n.astype(_BF16)

    cur = lambda i: (i, 0)
    return _call(body, (L // BLOCK,),
                 _attn_specs() + [pl.BlockSpec(memory_space=pltpu.SMEM), _whole((1, ATTN_WIDTH))],
                 [pl.BlockSpec((BLOCK, ATTN_WIDTH), cur)] * 2,
                 [_sds((L, ATTN_WIDTH), _F32), _sds((L, ATTN_WIDTH), _BF16)],
                 "attn_fwd")(q, k, k, v, v, sinks, g_attn)


def _attn_bwd(q, k, v, o, dn, sinks, g_attn):
    L = q.shape[0]

    def body(q_ref, kp_ref, kc_ref, vp_ref, vc_ref, o_ref, dn_ref, sink_ref, g_ref,
             dq_ref, dk_ref, dv_ref, dsink_ref, dg_ref):
        blk = pl.program_id(0)
        first = blk == 0

        @pl.when(first)
        def _():
            dk_ref[...] = jnp.zeros_like(dk_ref)
            dv_ref[...] = jnp.zeros_like(dv_ref)
            dsink_ref[...] = jnp.zeros_like(dsink_ref)

        o = o_ref[...]
        g = g_ref[...]
        _, r = _rms_fwd(o, g)
        do, dg = _rms_bwd(dn_ref[...], o, g, r)
        _accumulate(dg_ref, dg, first)
        kwin = jnp.concatenate([kp_ref[...], kc_ref[...]], axis=0)
        vwin = jnp.concatenate([vp_ref[...], vc_ref[...]], axis=0)
        lane = _iota((1, 128), 1)
        dsink = jnp.zeros((1, 128), _F32)
        dkwin = jnp.zeros((2 * BLOCK, KV_WIDTH), _F32)
        dvwin = jnp.zeros((2 * BLOCK, KV_WIDTH), _F32)
        dq_halves = []
        for j in range(N_KV_HEADS):
            spread = _head_spread(j)
            kt = _dot(kwin, spread, _NN).astype(_BF16)
            vt = _dot(vwin, spread, _NN).astype(_BF16)
            qs = _stack_heads(q_ref[:, 256 * j:256 * (j + 1)])
            dos = _stack_heads(do[:, 256 * j:256 * (j + 1)]).astype(_BF16)
            p, p_sink = _attn_scores(qs, kt, blk, _stacked_sinks(sink_ref, j))
            dp = _dot(dos, vt, _NT)
            delta = jnp.sum(p * dp, axis=-1, keepdims=True)
            ds = (p * (dp - delta) * (HEAD_DIM ** -0.5)).astype(_BF16)
            sink_term = p_sink * delta
            for g in range(Q_PER_KV):
                head_sum = jnp.sum(sink_term[BLOCK * g:BLOCK * (g + 1)], axis=0, keepdims=True)
                dsink = dsink - jnp.where(lane == Q_PER_KV * j + g, head_sum, 0.0)
            dvwin = dvwin + _fold_heads(_dot(p, dos, _TN), j)
            dkwin = dkwin + _fold_heads(_dot(ds, qs, _TN), j)
            dq_halves.append(_unstack_heads(_dot(ds, kt, _NN)))
        dq_ref[...] = jnp.concatenate(dq_halves, axis=1)
        dsink_ref[...] += dsink
        prev = pl.ds(pl.multiple_of(jnp.maximum(blk - 1, 0) * BLOCK, BLOCK), BLOCK)
        cur = pl.ds(pl.multiple_of(blk * BLOCK, BLOCK), BLOCK)
        dk_ref[prev, :] += dkwin[:BLOCK]
        dk_ref[cur, :] += dkwin[BLOCK:]
        dv_ref[prev, :] += dvwin[:BLOCK]
        dv_ref[cur, :] += dvwin[BLOCK:]

    cur = lambda i: (i, 0)
    blk_q = pl.BlockSpec((BLOCK, ATTN_WIDTH), cur)
    return _call(body, (L // BLOCK,),
                 _attn_specs() + [blk_q, blk_q, pl.BlockSpec(memory_space=pltpu.SMEM), _whole((1, ATTN_WIDTH))],
                 [blk_q, _whole((L, KV_WIDTH)), _whole((L, KV_WIDTH)), _whole((1, 128)), _whole((1, ATTN_WIDTH))],
                 [_sds((L, ATTN_WIDTH), _F32), _sds((L, KV_WIDTH), _F32), _sds((L, KV_WIDTH), _F32),
                  _sds((1, 128), _F32), _sds((1, ATTN_WIDTH), _F32)],
                 "attn_bwd")(q, k, k, v, v, o, dn, sinks, g_attn)


def _out_proj(n_ssm, n_attn, x, w_out, g_post_mix, g_pre_ffn):
    L = x.shape[0]
    tm = _chunk_tile(L)

    def body(ns_ref, na_ref, x_ref, w_ref, g1_ref, g2_ref, merged_ref, mo_ref, h1_ref, hn2_ref):
        merged = jnp.concatenate([ns_ref[...], na_ref[...]], axis=1)
        merged_ref[...] = merged
        mo = _dot(merged, w_ref[...], _NN)
        mo_ref[...] = mo
        n, _ = _rms_fwd(mo, g1_ref[...])
        h1 = x_ref[...] + n
        h1_ref[...] = h1
        hn2, _ = _rms_fwd(h1, g2_ref[...])
        hn2_ref[...] = hn2.astype(_BF16)

    row = _whole((1, D_MODEL))
    return _call(body, (L // tm,),
                 [_chunk_block(L, SSM_WIDTH), _rows(tm, ATTN_WIDTH), _rows(tm, D_MODEL), _whole((D_MODEL, D_MODEL)),
                  row, row],
                 [_rows(tm, D_MODEL)] * 4,
                 [_sds((L, D_MODEL), _BF16), _sds((L, D_MODEL), _F32), _sds((L, D_MODEL), _F32), _sds((L, D_MODEL), _BF16)],
                 "out_proj")(n_ssm, n_attn, x, w_out, g_post_mix, g_pre_ffn)


def _ffn(hn2, h1, target, w_gate_up, w_down, g_pre_ffn, g_post_ffn):
    L = h1.shape[0]
    tm = _tile(L)
    half = FFN_CHUNK

    def body(hn2_ref, h1_ref, tgt_ref, wgu_hbm, wd_hbm, g2_ref, g3_ref,
             act_ref, dgu_ref, dff_ref, dh1_ref, loss_ref, dg3_ref, dg2_ref,
             wgu, wd, gu, sem):
        first = pl.program_id(0) == 0

        @pl.when(first)
        def _():
            c1 = pltpu.make_async_copy(wgu_hbm, wgu, sem.at[0])
            c2 = pltpu.make_async_copy(wd_hbm, wd, sem.at[1])
            c1.start()
            c2.start()
            c1.wait()
            c2.wait()

        hn2 = hn2_ref[...]
        ff = jnp.zeros((tm, D_MODEL), _F32)
        for c in range(D_FF // half):
            gate = _dot(hn2, wgu[half * c:half * (c + 1), :], _NT)
            up = _dot(hn2, wgu[D_FF + half * c:D_FF + half * (c + 1), :], _NT)
            gu[:, half * c:half * (c + 1)] = gate
            gu[:, D_FF + half * c:D_FF + half * (c + 1)] = up
            act = gate * jax.nn.sigmoid(gate) * up
            act_ref[half * c:half * (c + 1), :] = act.T.astype(_BF16)
            ff = ff + _dot(act, wd[half * c:half * (c + 1), :], _NN)
        g3 = g3_ref[...]
        n, r = _rms_fwd(ff, g3)
        h1 = h1_ref[...]
        err = h1 + n - tgt_ref[...]
        loss = 0.5 * jnp.sum(jnp.mean(err * err, axis=-1, keepdims=True), axis=0, keepdims=True)
        _accumulate(loss_ref, jnp.broadcast_to(loss, (1, 128)), first)
        dh2 = err * (1.0 / D_MODEL)
        dff, dg3 = _rms_bwd(dh2, ff, g3, r)
        _accumulate(dg3_ref, dg3, first)
        dffb = dff.astype(_BF16)
        dff_ref[...] = dffb
        dhn2 = jnp.zeros((tm, D_MODEL), _F32)
        for c in range(D_FF // half):
            dact = _dot(dffb, wd[half * c:half * (c + 1), :], _NT)
            gate = gu[:, half * c:half * (c + 1)]
            up = gu[:, D_FF + half * c:D_FF + half * (c + 1)]
            sig = jax.nn.sigmoid(gate)
            silu = gate * sig
            dgate = dact * up * (sig + silu * (1.0 - sig))
            dup = dact * silu
            dgu_ref[half * c:half * (c + 1), :] = dgate.T.astype(_BF16)
            dgu_ref[D_FF + half * c:D_FF + half * (c + 1), :] = dup.T.astype(_BF16)
            dhn2 = dhn2 + _dot(dgate, wgu[half * c:half * (c + 1), :], _NN)
            dhn2 = dhn2 + _dot(dup, wgu[D_FF + half * c:D_FF + half * (c + 1), :], _NN)
        g2 = g2_ref[...]
        _, r2 = _rms_fwd(h1, g2)
        dh1, dg2 = _rms_bwd(dhn2, h1, g2, r2)
        _accumulate(dg2_ref, dg2, first)
        dh1_ref[...] = dh2 + dh1

    row = _whole((1, D_MODEL))
    anyspace = pl.BlockSpec(memory_space=pl.ANY)
    return _call(body, (L // tm,),
                 [_rows(tm, D_MODEL), _rows(tm, D_MODEL), _rows(tm, D_MODEL), anyspace, anyspace, row, row],
                 [pl.BlockSpec((D_FF, tm), lambda i: (0, i)), pl.BlockSpec((2 * D_FF, tm), lambda i: (0, i)),
                  _rows(tm, D_MODEL), _rows(tm, D_MODEL), _whole((1, 128)), row, row],
                 [_sds((D_FF, L), _BF16), _sds((2 * D_FF, L), _BF16), _sds((L, D_MODEL), _BF16),
                  _sds((L, D_MODEL), _F32), _sds((1, 128), _F32), _sds((1, D_MODEL), _F32), _sds((1, D_MODEL), _F32)],
                 "ffn",
                 scratch=[pltpu.VMEM((2 * D_FF, D_MODEL), _BF16), pltpu.VMEM((D_FF, D_MODEL), _BF16),
                          pltpu.VMEM((tm, 2 * D_FF), _F32), pltpu.SemaphoreType.DMA((2,))],
                 )(hn2, h1, target, w_gate_up, w_down, g_pre_ffn, g_post_ffn)


def _out_proj_bwd(dh1, mo, w_out, g_post_mix, tokens=()):
    L = dh1.shape[0]
    tm = _chunk_tile(L)

    def body(dh1_ref, mo_ref, w_ref, g_ref, dmo_ref, dns_ref, dna_ref, dg_ref):
        first = pl.program_id(0) == 0
        mo = mo_ref[...]
        g = g_ref[...]
        _, r = _rms_fwd(mo, g)
        dmo, dg = _rms_bwd(dh1_ref[...], mo, g, r)
        _accumulate(dg_ref, dg, first)
        dmob = dmo.astype(_BF16)
        dmo_ref[...] = dmob
        dmerged = _dot(dmob, w_ref[...], _NT)
        dns_ref[...] = dmerged[:, :SSM_WIDTH]
        dna_ref[...] = dmerged[:, SSM_WIDTH:]

    row = _whole((1, D_MODEL))
    return _call(body, (L // tm,),
                 [_rows(tm, D_MODEL), _rows(tm, D_MODEL), _whole((D_MODEL, D_MODEL)), row],
                 [_rows(tm, D_MODEL), _chunk_block(L, SSM_WIDTH), _rows(tm, ATTN_WIDTH), row],
                 [_sds((L, D_MODEL), _BF16), _sds(_chunk_shape(L, SSM_WIDTH), _F32), _sds((L, ATTN_WIDTH), _F32),
                  _sds((1, D_MODEL), _F32)],
                 "out_proj_bwd", tokens=tokens)(dh1, mo, w_out, g_post_mix)


def _in_proj_bwd(du, dq, dk, dv, cos_t, sin_t, x, dh1, g_pre_mix, w_in, tokens=()):
    L = x.shape[0]
    tm = _chunk_tile(L)

    def body(du_ref, dq_ref, dk_ref, dv_ref, cos_ref, sin_ref, x_ref, dh1_ref, g_ref, w_ref,
             dproj_ref, dx_ref, dg_ref):
        first = pl.program_id(0) == 0
        cos_v, sin_v = cos_ref[...], sin_ref[...]
        dproj = jnp.concatenate([du_ref[...], _rope_transpose(dq_ref[...], cos_v, sin_v),
                                 _rope_transpose(dk_ref[...], cos_v, sin_v), dv_ref[...]], axis=1).astype(_BF16)
        dproj_ref[...] = dproj
        dhn = _dot(dproj, w_ref[...], _NN)
        x = x_ref[...]
        g = g_ref[...]
        _, r = _rms_fwd(x, g)
        dx, dg = _rms_bwd(dhn, x, g, r)
        _accumulate(dg_ref, dg, first)
        dx_ref[...] = dh1_ref[...] + dx

    row = _whole((1, D_MODEL))
    return _call(body, (L // tm,),
                 [_chunk_block(L, SSM_WIDTH), _rows(tm, ATTN_WIDTH), _rows(tm, KV_WIDTH), _rows(tm, KV_WIDTH),
                  _rows(tm, KV_WIDTH), _rows(tm, KV_WIDTH), _rows(tm, D_MODEL), _rows(tm, D_MODEL), row,
                  _whole((IN_WIDTH, D_MODEL))],
                 [_rows(tm, IN_WIDTH), _rows(tm, D_MODEL), row],
                 [_sds((L, IN_WIDTH), _BF16), _sds((L, D_MODEL), _F32), _sds((1, D_MODEL), _F32)],
                 "in_proj_bwd", tokens=tokens)(du, dq, dk, dv, cos_t, sin_t, x, dh1, g_pre_mix, w_in)


def _matmul_nn(a, b, out_dtype, name):
    M, K = a.shape
    N = b.shape[1]
    tm = next(t for t in (704, 512, 256, 128) if M % t == 0)
    tn = N if N <= D_MODEL else next(t for t in (512, 256, 128) if N % t == 0)

    def body(a_ref, b_ref, o_ref):
        o_ref[...] = _dot(a_ref[...], b_ref[...], _NN).astype(out_dtype)

    params = pltpu.CompilerParams(dimension_semantics=("arbitrary", "arbitrary"), vmem_limit_bytes=VMEM_LIMIT)
    return pl.pallas_call(body, grid=(M // tm, N // tn),
                          in_specs=[pl.BlockSpec((tm, K), lambda i, j: (i, 0)),
                                    pl.BlockSpec((K, tn), lambda i, j: (0, j))],
                          out_specs=pl.BlockSpec((tm, tn), lambda i, j: (i, j)),
                          out_shape=_sds((M, N), out_dtype), compiler_params=params, name=name)(a, b)


def _matmul_tn(a, b, out_dtype, name, scale=1.0):
    K, M = a.shape
    N = b.shape[1]
    tm = next(t for t in (512, 256, 128) if M % t == 0)
    tn = N if N <= D_MODEL else next(t for t in (512, 256, 128) if N % t == 0)

    def body(a_ref, b_ref, o_ref):
        acc = _dot(a_ref[...], b_ref[...], _TN)
        o_ref[...] = (acc if scale == 1.0 else acc * scale).astype(out_dtype)

    params = pltpu.CompilerParams(dimension_semantics=("arbitrary", "arbitrary"), vmem_limit_bytes=VMEM_LIMIT)
    return pl.pallas_call(body, grid=(M // tm, N // tn),
                          in_specs=[pl.BlockSpec((K, tm), lambda i, j: (0, i)),
                                    pl.BlockSpec((K, tn), lambda i, j: (0, j))],
                          out_specs=pl.BlockSpec((tm, tn), lambda i, j: (i, j)),
                          out_shape=_sds((M, N), out_dtype), compiler_params=params, name=name)(a, b)


def _local_step(x, pos, target, p, fetch, publish, progress):
    L = x.shape[0]
    T = L // SCAN_CHUNKS
    cos_t, sin_t = _rope_tables(pos.reshape(L, 1))
    w_in, = fetch(("w_in",), None)
    hn, u, q, k, v = _in_proj(x, p["g_pre_mix"], w_in, cos_t, sin_t)

    ssm = {n: _to_2d(n, p[n]) for n in ("ssm_lambda_re", "ssm_lambda_im", "ssm_log_dt", "ssm_b_re", "ssm_b_im",
                                        "ssm_c_re", "ssm_c_im")}
    d_row = p["ssm_d"].reshape(1, SSM_WIDTH)
    a_re, a_im, bt_re, bt_im, ct_re, ct_im = _ssm_prep(
        ssm["ssm_lambda_re"], ssm["ssm_lambda_im"], ssm["ssm_log_dt"], ssm["ssm_b_re"], ssm["ssm_b_im"],
        ssm["ssm_c_re"], ssm["ssm_c_im"])

    u_c = u.reshape(L, SSM_WIDTH)
    cx, carry_re, carry_im = _ssm_core_fwd(u_c, bt_re, bt_im, ct_re, ct_im, a_re, a_im)
    w_glu, = fetch(("w_glu",), cx)
    y, z, n_ssm = _ssm_out(cx, u_c, d_row, w_glu, p["b_glu"], p["g_ssm_out"])

    sinks = p["attn_sinks"].reshape(N_Q_HEADS)
    o, n_attn = _attn_fwd(q, k, v, sinks, p["g_attn_out"])
    w_out, = fetch(("w_out",), n_attn)
    merged, mo, h1, hn2 = _out_proj(n_ssm, n_attn, x, w_out, p["g_post_mix"], p["g_pre_ffn"])
    w_gate_up, w_down = fetch(("w_gate_up", "w_down"), hn2)
    act_t, dgu_t, dff, dh1, loss, dg_post_ffn, dg_pre_ffn = _ffn(
        hn2, h1, target, w_gate_up, w_down, p["g_pre_ffn"], p["g_post_ffn"])
    grads = {"g_post_ffn": dg_post_ffn, "g_pre_ffn": dg_pre_ffn}
    tokens = publish({"w_down": _matmul_nn(act_t, dff, _BF16, "grad_w_down"),
                      "w_gate_up": _matmul_nn(dgu_t, hn2, _BF16, "grad_w_gate_up")})

    dmo, dn_ssm, dn_attn, grads["g_post_mix"] = _out_proj_bwd(dh1, mo, w_out, p["g_post_mix"], tokens)
    grad_w_out = _matmul_tn(merged, dmo, _BF16, "grad_w_out")

    dq, dk, dv, dsink, grads["g_attn_out"] = _attn_bwd(q, k, v, o, dn_attn, sinks, p["g_attn_out"])
    grads["attn_sinks"] = dsink

    gy, dz, dy, dud, grads["g_ssm_out"], grads["b_glu"], dd = _ssm_out_bwd(
        dn_ssm, y, z, u_c, d_row, w_glu, p["g_ssm_out"])
    tokens = progress(dy)
    tokens += publish({"w_out": grad_w_out, "w_glu": _matmul_tn(dz, gy, _BF16, "grad_w_glu")})
    du_c, dct_re, dct_im, dbt_re, dbt_im, da_re, da_im = _ssm_core_bwd(
        u_c, dy, dud, carry_re, carry_im, bt_re, bt_im, ct_re, ct_im, a_re, a_im, tokens)
    ssm_pack = _ssm_param_bwd(
        da_re, da_im, dbt_re, dbt_im, dct_re, dct_im,
        ssm["ssm_lambda_re"], ssm["ssm_lambda_im"], ssm["ssm_log_dt"], ssm["ssm_b_re"], ssm["ssm_b_im"],
        dd.reshape(SSM_GROUPS, SSM_GROUP))
    grads.update(ssm_pack=ssm_pack, loss=loss)
    publish(grads)

    du = du_c.reshape(_chunk_shape(L, SSM_WIDTH))
    dproj, grad_x, g_pre_mix = _in_proj_bwd(du, dq, dk, dv, cos_t, sin_t, x, dh1, p["g_pre_mix"], w_in, [ssm_pack])
    publish({"g_pre_mix": g_pre_mix, "w_in": _matmul_tn(dproj, hn, _BF16, "grad_w_in")})
    return grad_x


_MESH = pl.DeviceIdType.MESH
_PEERS = N_DEV - 1


def _mesh_pos():
    return lax.axis_index("x"), lax.axis_index("y"), lax.axis_index("c")


def _dev_index(px, py, pc):
    return 4 * px + 2 * py + pc


def _peer(x, y, c, r):
    return (x ^ ((r >> 2) & 1), y ^ ((r >> 1) & 1), c ^ (r & 1))


def _sequencer_exchange(sources, blocked, name, collective_id):
    n = len(sources)
    flags = blocked

    def body(*refs):
        srcs, zones = refs[:n], refs[n:2 * n]
        send_sems, recv_sems, local_sems = refs[2 * n:]
        x, y, c = _mesh_pos()
        me = _dev_index(x, y, c)
        barrier = pltpu.get_barrier_semaphore()
        for r in range(1, N_DEV):
            pl.semaphore_signal(barrier, inc=1, device_id=_peer(x, y, c, r), device_id_type=_MESH)
        pl.semaphore_wait(barrier, _PEERS)
        local, sends, recvs = [], [], []
        for w in range(n):
            cp = pltpu.make_async_copy(srcs[w].at[me] if flags[w] else srcs[w], zones[w].at[me], local_sems.at[w])
            cp.start()
            local.append(cp)
            for r in range(1, N_DEV):
                peer = _peer(x, y, c, r)
                idx = _dev_index(*peer)
                k = _PEERS * w + r - 1
                src = srcs[w].at[idx] if flags[w] else srcs[w]
                send = pltpu.make_async_remote_copy(
                    src_ref=src, dst_ref=zones[w].at[me], send_sem=send_sems.at[k], recv_sem=recv_sems.at[k],
                    device_id=peer, device_id_type=_MESH)
                send.start()
                sends.append(send)
                recvs.append(pltpu.make_async_remote_copy(
                    src_ref=src, dst_ref=zones[w].at[idx], send_sem=send_sems.at[k], recv_sem=recv_sems.at[k],
                    device_id=peer, device_id_type=_MESH))
        for cp in recvs:
            cp.wait_recv()
        for cp in sends:
            cp.wait_send()
        for cp in local:
            cp.wait()

    return pl.kernel(
        body, name=name,
        out_type=[_sds((N_DEV,) + (s.shape[1:] if f else s.shape), s.dtype) for s, f in zip(sources, flags)],
        mesh=plsc.ScalarSubcoreMesh(axis_name="sequencer", num_cores=1),
        scratch_types=[pltpu.SemaphoreType.DMA((_PEERS * n,)), pltpu.SemaphoreType.DMA((_PEERS * n,)),
                       pltpu.SemaphoreType.DMA((n,))],
        compiler_params=pltpu.CompilerParams(collective_id=collective_id),
    )(*sources)


def _sequencer_gather(shards, name, collective_id):
    n = len(shards)
    fan = 4

    def body(*refs):
        srcs, zones = refs[:n], refs[n:2 * n]
        send_sems, recv_sems, local_sems = refs[2 * n:]
        x, y, c = _mesh_pos()
        me, sibling = (x, y, c), (x, y, 1 - c)
        chips = [(1 - x, y), (x, 1 - y), (1 - x, 1 - y)]
        barrier = pltpu.get_barrier_semaphore()
        for peer in [sibling] + [(*chip, c) for chip in chips]:
            pl.semaphore_signal(barrier, inc=1, device_id=peer, device_id_type=_MESH)
        pl.semaphore_wait(barrier, fan)

        def copy(w, k, block, to, src=None):
            slot = zones[w].at[_dev_index(*block)]
            return pltpu.make_async_remote_copy(
                src_ref=slot if src is None else src, dst_ref=slot,
                send_sem=send_sems.at[_PEERS * w + k], recv_sem=recv_sems.at[_PEERS * w + k],
                device_id=to, device_id_type=_MESH)

        mine, first, passed = [], [], []
        for w in range(n):
            cp = pltpu.make_async_copy(srcs[w], zones[w].at[_dev_index(*me)], local_sems.at[w])
            cp.start()
            mine.append(cp)
            sends = [copy(w, 0, me, sibling, src=srcs[w])]
            sends += [copy(w, 1 + j, me, (*chip, c), src=srcs[w]) for j, chip in enumerate(chips)]
            for cp in sends:
                cp.start()
            first += sends
        for w in range(n):
            for j, chip in enumerate(chips):
                copy(w, 1 + j, (*chip, c), me).wait_recv()
                cp = copy(w, fan + j, (*chip, c), sibling)
                cp.start()
                passed.append(cp)
        for w in range(n):
            copy(w, 0, sibling, me).wait_recv()
            for j, chip in enumerate(chips):
                copy(w, fan + j, (*chip, 1 - c), me).wait_recv()
        for cp in first + passed:
            cp.wait_send()
        for cp in mine:
            cp.wait()

    return pl.kernel(
        body, name=name, out_type=[_sds((N_DEV,) + s.shape, s.dtype) for s in shards],
        mesh=plsc.ScalarSubcoreMesh(axis_name="sequencer", num_cores=1),
        scratch_types=[pltpu.SemaphoreType.DMA((_PEERS * n,)), pltpu.SemaphoreType.DMA((_PEERS * n,)),
                       pltpu.SemaphoreType.DMA((n,))],
        compiler_params=pltpu.CompilerParams(collective_id=collective_id),
    )(*shards)


N_CHIPS = N_DEV // 2


def _sequencer_pair_exchange(sources, name, collective_id):
    n = len(sources)

    def body(*refs):
        srcs, zones = refs[:n], refs[n:2 * n]
        send_sems, recv_sems = refs[2 * n:]
        x, y, c = _mesh_pos()
        sibling = (x, y, 1 - c)
        barrier = pltpu.get_barrier_semaphore()
        pl.semaphore_signal(barrier, inc=1, device_id=sibling, device_id_type=_MESH)
        pl.semaphore_wait(barrier, 1)
        copies = []
        for w in range(n):
            for j in range(N_CHIPS):
                k = N_CHIPS * w + j
                cp = pltpu.make_async_remote_copy(
                    src_ref=srcs[w].at[2 * j + 1 - c], dst_ref=zones[w].at[j],
                    send_sem=send_sems.at[k], recv_sem=recv_sems.at[k], device_id=sibling, device_id_type=_MESH)
                cp.start()
                copies.append(cp)
        for cp in copies:
            cp.wait_recv()
        for cp in copies:
            cp.wait_send()

    return pl.kernel(
        body, name=name, out_type=[_sds((N_CHIPS,) + s.shape[1:], s.dtype) for s in sources],
        mesh=plsc.ScalarSubcoreMesh(axis_name="sequencer", num_cores=1),
        scratch_types=[pltpu.SemaphoreType.DMA((N_CHIPS * n,)), pltpu.SemaphoreType.DMA((N_CHIPS * n,))],
        compiler_params=pltpu.CompilerParams(collective_id=collective_id),
    )(*sources)


def _pair_sum(source, received, core, name, tokens=()):
    _, rows, cols = source.shape
    tr = _row_tile(rows)
    n_tok = len(tokens)

    def body(core_ref, s_ref, r_ref, *rest):
        o_ref = rest[n_tok]
        o_ref[...] = (s_ref[...].astype(_F32) + r_ref[...].astype(_F32)).astype(o_ref.dtype)

    quarter = pl.BlockSpec((N_CHIPS, tr, cols), lambda i, core_ref: (0, i, 0))
    mine = pl.BlockSpec((N_CHIPS, None, tr, cols), lambda i, core_ref: (0, core_ref[0], i, 0))
    spec = pltpu.PrefetchScalarGridSpec(
        num_scalar_prefetch=1, grid=(rows // tr,),
        in_specs=[mine, quarter] + [pl.BlockSpec(memory_space=pl.ANY)] * n_tok, out_specs=quarter)
    params = pltpu.CompilerParams(dimension_semantics=("arbitrary",), vmem_limit_bytes=VMEM_LIMIT)
    return pl.pallas_call(body, grid_spec=spec, out_shape=_sds((N_CHIPS, rows, cols), source.dtype),
                          compiler_params=params, name=name)(
        core, source.reshape(N_CHIPS, 2, rows, cols), received, *tokens)


def _sequencer_chip_exchange(partials, name, collective_id):
    n = len(partials)
    others = N_CHIPS - 1

    def body(*refs):
        srcs, zones = refs[:n], refs[n:2 * n]
        send_sems, recv_sems, local_sems = refs[2 * n:]
        x, y, c = _mesh_pos()
        mine = 2 * x + y
        peers = [(x ^ (r >> 1), y ^ (r & 1), c) for r in range(1, N_CHIPS)]
        barrier = pltpu.get_barrier_semaphore()
        for peer in peers:
            pl.semaphore_signal(barrier, inc=1, device_id=peer, device_id_type=_MESH)
        pl.semaphore_wait(barrier, others)
        local, sends, recvs = [], [], []
        for w in range(n):
            cp = pltpu.make_async_copy(srcs[w].at[mine], zones[w].at[mine], local_sems.at[w])
            cp.start()
            local.append(cp)
            for r, peer in enumerate(peers):
                theirs = 2 * peer[0] + peer[1]
                k = others * w + r
                send = pltpu.make_async_remote_copy(
                    src_ref=srcs[w].at[theirs], dst_ref=zones[w].at[mine],
                    send_sem=send_sems.at[k], recv_sem=recv_sems.at[k], device_id=peer, device_id_type=_MESH)
                send.start()
                sends.append(send)
                recvs.append(pltpu.make_async_remote_copy(
                    src_ref=srcs[w].at[theirs], dst_ref=zones[w].at[theirs],
                    send_sem=send_sems.at[k], recv_sem=recv_sems.at[k], device_id=peer, device_id_type=_MESH))
        for cp in recvs:
            cp.wait_recv()
        for cp in sends:
            cp.wait_send()
        for cp in local:
            cp.wait()

    return pl.kernel(
        body, name=name, out_type=[_sds(s.shape, s.dtype) for s in partials],
        mesh=plsc.ScalarSubcoreMesh(axis_name="sequencer", num_cores=1),
        scratch_types=[pltpu.SemaphoreType.DMA((others * n,)), pltpu.SemaphoreType.DMA((others * n,)),
                       pltpu.SemaphoreType.DMA((n,))],
        compiler_params=pltpu.CompilerParams(collective_id=collective_id),
    )(*partials)


def _row_tile(rows):
    return next(t for t in range(min(rows, 256), 0, -16) if rows % t == 0)


def _sum_parts(parts, name, tokens=()):
    _, rows, cols = parts.shape
    tr = _row_tile(rows)

    def body(p_ref, g_ref):
        g = p_ref[0].astype(_F32)
        for s in range(1, N_DEV):
            g = g + p_ref[s].astype(_F32)
        g_ref[...] = g

    return _call(body, (rows // tr,), [pl.BlockSpec((N_DEV, tr, cols), lambda i: (0, i, 0))],
                 _rows(tr, cols), _sds((rows, cols), _F32), name, tokens=tokens)(parts)


def _adam_update(g, w, m, v):
    new_m = ADAM_B1 * m + (1.0 - ADAM_B1) * g
    new_v = ADAM_B2 * v + (1.0 - ADAM_B2) * (g * g)
    m_hat = new_m / (1.0 - ADAM_B1 ** ADAM_STEP)
    v_hat = new_v / (1.0 - ADAM_B2 ** ADAM_STEP)
    return -ADAM_LR * (m_hat / (jnp.sqrt(v_hat) + ADAM_EPS) + ADAM_WD * w), new_m, new_v


def _adamw_small(parts, items, sums, name, tokens=()):
    n_p, n_i = len(parts), len(items)

    def body(*refs):
        p_refs, state, outs = refs[:n_p], refs[n_p:n_p + 3 * n_i], refs[n_p + 3 * n_i:]

        def total(part, rows, cols):
            shift = cols.start % _LANES
            window = slice(cols.start - shift, cols.start - shift + _LANES) if shift else cols
            n_rows = rows.stop - rows.start
            narrow = p_refs[part].dtype.itemsize < 4 and n_rows % _PACK_TILE
            tile = slice(rows.start, rows.start + _PACK_TILE) if narrow else rows
            g = p_refs[part][0, tile, window].astype(_F32)
            for s in range(1, N_DEV):
                g = g + p_refs[part][s, tile, window].astype(_F32)
            g = g[:n_rows] if narrow else g
            return pltpu.roll(g, _LANES - shift, 1)[:, :cols.stop - cols.start] if shift else g

        for i, (part, rows, cols, _, _, _) in enumerate(items):
            g = total(part, rows, cols)
            w_ref, m_ref, v_ref = state[3 * i:3 * i + 3]
            delta, new_m, new_v = _adam_update(g, w_ref[...], m_ref[...], v_ref[...])
            outs[4 * i][...] = g
            outs[4 * i + 1][...] = delta
            outs[4 * i + 2][...] = new_m
            outs[4 * i + 3][...] = new_v
        for j, (part, rows, cols) in enumerate(sums):
            outs[4 * n_i + j][...] = total(part, rows, cols)

    ins = list(parts) + [a for item in items for a in item[3:]]
    out_shapes = [item[3].shape for item in items for _ in range(4)]
    out_shapes += [(rows.stop - rows.start, cols.stop - cols.start) for _, rows, cols in sums]
    out = _call(body, (1,), [_whole(a.shape) for a in ins], [_whole(s) for s in out_shapes],
                [_sds(s, _F32) for s in out_shapes], name, tokens=tokens)(*ins)
    return [out[4 * i:4 * i + 4] for i in range(n_i)], out[4 * n_i:]


def _adamw(parts, w, m, v, name, tokens=()):
    rows, cols = w.shape
    tr = _row_tile(rows)
    n_parts = parts.shape[0]

    def body(p_ref, w_ref, m_ref, v_ref, g_ref, d_ref, nm_ref, nv_ref):
        g = p_ref[0].astype(_F32)
        for s in range(1, n_parts):
            g = g + p_ref[s].astype(_F32)
        new_m = ADAM_B1 * m_ref[...] + (1.0 - ADAM_B1) * g
        new_v = ADAM_B2 * v_ref[...] + (1.0 - ADAM_B2) * (g * g)
        m_hat = new_m / (1.0 - ADAM_B1 ** ADAM_STEP)
        v_hat = new_v / (1.0 - ADAM_B2 ** ADAM_STEP)
        g_ref[...] = g
        d_ref[...] = -ADAM_LR * (m_hat / (jnp.sqrt(v_hat) + ADAM_EPS) + ADAM_WD * w_ref[...])
        nm_ref[...] = new_m
        nv_ref[...] = new_v

    blk = _rows(tr, cols)
    return _call(body, (rows // tr,),
                 [pl.BlockSpec((n_parts, tr, cols), lambda i: (0, i, 0)), blk, blk, blk],
                 [blk] * 4, [_sds((rows, cols), _F32)] * 4, name, tokens=tokens)(parts, w, m, v)


_SMALL = ("g_pre_mix", "ssm_lambda_re", "ssm_lambda_im", "ssm_log_dt", "ssm_b_re", "ssm_b_im",
          "ssm_c_re", "ssm_c_im", "ssm_d", "b_glu", "attn_sinks", "g_ssm_out", "g_attn_out",
          "g_post_mix", "g_pre_ffn", "g_post_ffn")
_BIG = ("w_in", "w_glu", "w_out", "w_gate_up", "w_down")
_WEIGHTS = ("g_pre_mix", "w_in", "ssm_lambda_re", "ssm_lambda_im", "ssm_log_dt", "ssm_b_re", "ssm_b_im",
            "ssm_c_re", "ssm_c_im", "ssm_d", "w_glu", "b_glu", "attn_sinks", "g_ssm_out", "g_attn_out",
            "w_out", "g_post_mix", "g_pre_ffn", "w_gate_up", "w_down", "g_post_ffn")
_LANES = 128


_SHAPE_2D = {
    "g_pre_mix": (1, D_MODEL), "ssm_lambda_re": (SSM_GROUPS, SSM_STATE), "ssm_lambda_im": (SSM_GROUPS, SSM_STATE),
    "ssm_log_dt": (1, SSM_GROUPS), "ssm_b_re": (SSM_WIDTH, SSM_STATE), "ssm_b_im": (SSM_WIDTH, SSM_STATE),
    "ssm_c_re": (SSM_WIDTH, SSM_STATE), "ssm_c_im": (SSM_WIDTH, SSM_STATE), "ssm_d": (SSM_GROUPS, SSM_GROUP),
    "b_glu": (1, 2 * SSM_WIDTH), "attn_sinks": (1, N_Q_HEADS), "g_ssm_out": (1, SSM_WIDTH),
    "g_attn_out": (1, ATTN_WIDTH), "g_post_mix": (1, D_MODEL), "g_pre_ffn": (1, D_MODEL), "g_post_ffn": (1, D_MODEL)}
_ROW_WIDTH = {"g_pre_mix": D_MODEL, "b_glu": 2 * SSM_WIDTH, "attn_sinks": _LANES, "g_ssm_out": SSM_WIDTH,
              "g_attn_out": ATTN_WIDTH, "g_post_mix": D_MODEL, "g_pre_ffn": D_MODEL, "g_post_ffn": D_MODEL,
              "loss": _LANES}
_PER_GROUP_TRANSPOSED = ("ssm_b_re", "ssm_b_im")


def _to_2d(name, a):
    if name in _PER_GROUP_TRANSPOSED:
        a = a.reshape(SSM_GROUPS, SSM_STATE, SSM_GROUP).transpose(0, 2, 1)
    return a.reshape(_SHAPE_2D[name])


def _from_2d(name, a, shape):
    if name in _PER_GROUP_TRANSPOSED:
        a = a.reshape(SSM_GROUPS, SSM_GROUP, SSM_STATE).transpose(0, 2, 1)
    return a.reshape(shape)


def _row_slots(names):
    slots, row, col = {}, 0, 0
    for n in names:
        width = _ROW_WIDTH[n]
        if col + width > D_MODEL:
            row, col = row + 1, 0
        slots[n] = (row, col, width)
        col += width
    return slots


def _stack_rows(named, slots):
    n_rows = -(-(max(r for r, _, _ in slots.values()) + 1) // 8) * 8
    lines = []
    for r in range(n_rows):
        pieces = [named[n] for n, (row, _, _) in slots.items() if row == r]
        used = sum(p.shape[1] for p in pieces)
        if used < D_MODEL:
            pieces.append(jnp.zeros((1, D_MODEL - used), _F32))
        lines.append(jnp.concatenate(pieces, axis=1) if len(pieces) > 1 else pieces[0])
    return jnp.concatenate(lines, axis=0)


def kernel(x, positions, g_pre_mix, w_in, ssm_lambda_re, ssm_lambda_im, ssm_log_dt, ssm_b_re, ssm_b_im, ssm_c_re, ssm_c_im, ssm_d, w_glu, b_glu, attn_sinks, g_ssm_out, g_attn_out, w_out, g_post_mix, g_pre_ffn, w_gate_up, w_down, g_post_ffn, loss_target, m_g_pre_mix, m_w_in, m_ssm_lambda_re, m_ssm_lambda_im, m_ssm_log_dt, m_ssm_b_re, m_ssm_b_im, m_ssm_c_re, m_ssm_c_im, m_ssm_d, m_w_glu, m_b_glu, m_attn_sinks, m_g_ssm_out, m_g_attn_out, m_w_out, m_g_post_mix, m_g_pre_ffn, m_w_gate_up, m_w_down, m_g_post_ffn, v_g_pre_mix, v_w_in, v_ssm_lambda_re, v_ssm_lambda_im, v_ssm_log_dt, v_ssm_b_re, v_ssm_b_im, v_ssm_c_re, v_ssm_c_im, v_ssm_d, v_w_glu, v_b_glu, v_attn_sinks, v_g_ssm_out, v_g_attn_out, v_w_out, v_g_post_mix, v_g_pre_ffn, v_w_gate_up, v_w_down, v_g_post_ffn):
    w = dict(g_pre_mix=g_pre_mix, w_in=w_in, ssm_lambda_re=ssm_lambda_re, ssm_lambda_im=ssm_lambda_im,
             ssm_log_dt=ssm_log_dt, ssm_b_re=ssm_b_re, ssm_b_im=ssm_b_im, ssm_c_re=ssm_c_re, ssm_c_im=ssm_c_im,
             ssm_d=ssm_d, w_glu=w_glu, b_glu=b_glu, attn_sinks=attn_sinks, g_ssm_out=g_ssm_out,
             g_attn_out=g_attn_out, w_out=w_out, g_post_mix=g_post_mix, g_pre_ffn=g_pre_ffn,
             w_gate_up=w_gate_up, w_down=w_down, g_post_ffn=g_post_ffn)
    m = dict(g_pre_mix=m_g_pre_mix, w_in=m_w_in, ssm_lambda_re=m_ssm_lambda_re, ssm_lambda_im=m_ssm_lambda_im,
             ssm_log_dt=m_ssm_log_dt, ssm_b_re=m_ssm_b_re, ssm_b_im=m_ssm_b_im, ssm_c_re=m_ssm_c_re,
             ssm_c_im=m_ssm_c_im, ssm_d=m_ssm_d, w_glu=m_w_glu, b_glu=m_b_glu, attn_sinks=m_attn_sinks,
             g_ssm_out=m_g_ssm_out, g_attn_out=m_g_attn_out, w_out=m_w_out, g_post_mix=m_g_post_mix,
             g_pre_ffn=m_g_pre_ffn, w_gate_up=m_w_gate_up, w_down=m_w_down, g_post_ffn=m_g_post_ffn)
    v = dict(g_pre_mix=v_g_pre_mix, w_in=v_w_in, ssm_lambda_re=v_ssm_lambda_re, ssm_lambda_im=v_ssm_lambda_im,
             ssm_log_dt=v_ssm_log_dt, ssm_b_re=v_ssm_b_re, ssm_b_im=v_ssm_b_im, ssm_c_re=v_ssm_c_re,
             ssm_c_im=v_ssm_c_im, ssm_d=v_ssm_d, w_glu=v_w_glu, b_glu=v_b_glu, attn_sinks=v_attn_sinks,
             g_ssm_out=v_g_ssm_out, g_attn_out=v_g_attn_out, w_out=v_w_out, g_post_mix=v_g_post_mix,
             g_pre_ffn=v_g_pre_ffn, w_gate_up=v_w_gate_up, w_down=v_w_down, g_post_ffn=v_g_post_ffn)

    transposed = ("w_in", "w_glu", "w_gate_up")
    native_transposed = ("w_in", "w_gate_up")
    shard = {n: (w[n][0].T if n in transposed else w[n][0]).astype(_BF16) for n in _BIG}
    gathered = {}
    for cid, names in enumerate((("w_in",), ("w_glu", "w_out"), ("w_gate_up", "w_down")), start=1):
        lands = _sequencer_gather([shard[n] for n in names], "gather_" + names[0], cid)
        gathered.update({n: a.reshape(-1, a.shape[2]) for n, a in zip(names, lands)})

    def fetch(names, after):
        del after
        return [gathered[n] for n in names]

    sent = []
    ids = iter(range(4, 16))
    two_step = {}

    def publish(named):
        big = [n for n in named if n in _BIG]
        if set(big) == {"w_gate_up", "w_down"}:
            blocks = [named[n].reshape(N_DEV, -1, named[n].shape[1]) for n in big]
            two_step.update(names=big, blocks=blocks,
                            received=_sequencer_pair_exchange(blocks, "grads_pair", next(ids)))
            return [named[n] for n in big]
        rows = [n for n in named if n in _ROW_WIDTH]
        plain = [n for n in named if n not in big + rows]
        sources = [named[n].reshape(N_DEV, -1, named[n].shape[1]) for n in big]
        slots = _row_slots(rows)
        if rows:
            sources.append(_stack_rows(named, slots))
        sources += [named[n] for n in plain]
        flags = [True] * len(big) + [False] * (len(sources) - len(big))
        cid = next(ids)
        sent.append((big, slots, plain, _sequencer_exchange(sources, flags, "grads_%d" % cid, cid)))
        return [named[n] for n in big]

    def progress(after):
        core = lax.axis_index("c").astype(jnp.int32).reshape(1)
        partials = [_pair_sum(b, r, core, "pair_sum_" + n, [after])
                    for n, b, r in zip(two_step["names"], two_step["blocks"], two_step["received"])]
        sent.append((two_step["names"], {}, [], _sequencer_chip_exchange(partials, "grads_chips", next(ids))))
        return partials

    p = {n: w[n] for n in _SMALL}
    grad_x = _local_step(x[0], positions[0], loss_target[0], p, fetch, publish, progress)

    state = {n: [_to_2d(n, a) for a in (w[n], m[n], v[n])] for n in _SMALL}
    result = {}
    total_loss = None
    chain = []
    for big, slots, plain, lands in sent:
        lands = list(lands)
        after = list(chain)
        for name in big:
            part = lands.pop(0)
            if name in native_transposed:
                updated = _adamw(part, w[name][0].T, m[name][0].T, v[name][0].T, "adamw_" + name, after)
                result[name] = [a.T[None] for a in updated]
                chain.append(updated[3])
                continue
            if name in transposed:
                part = _sum_parts(part, "sum_" + name, after).T[None]
            updated = _adamw(part, w[name][0], m[name][0], v[name][0], "adamw_" + name, after)
            result[name] = [a[None] for a in updated]
            chain.append(updated[3])
        parts, items, sums, names = [], [], [], []
        if slots:
            parts.append(lands.pop(0))
            for name, (row, col, _) in slots.items():
                if name == "loss":
                    sums.append((0, slice(row, row + 1), slice(col, col + _LANES)))
                else:
                    items.append((0, slice(row, row + 1), slice(col, col + _SHAPE_2D[name][1]), *state[name]))
                    names.append(name)
        for name in plain:
            packed = _SSM_PACK if name == "ssm_pack" else {name: (0, _SHAPE_2D[name][0], 0, _SHAPE_2D[name][1])}
            for member, (first, rows_n, lane, cols_n) in packed.items():
                items.append((len(parts), slice(first, first + rows_n), slice(lane, lane + cols_n), *state[member]))
                names.append(member)
            parts.append(lands.pop(0))
        if items:
            updated, summed = _adamw_small(parts, items, sums, "adamw_small_" + names[0], after)
            chain.append(updated[0][3])
            result.update(dict(zip(names, updated)))
            if summed:
                total_loss = summed[0][0, 0]

    out = [total_loss, grad_x[None]]
    for kind in range(4):
        out += [_from_2d(n, result[n][kind], w[n].shape) for n in _WEIGHTS]
    return tuple(out)
```

```python
import math

import numpy as np
import jax
import jax.numpy as jnp
from jax import lax
from jax.experimental import pallas as pl
from jax.experimental.pallas import tpu as pltpu
from jax.experimental.pallas import tpu_sc as plsc

D_MODEL = 1024
SSM_WIDTH = 512
SSM_GROUP = 16
SSM_GROUPS = 32
SSM_STATE = 64
N_STATE = SSM_GROUPS * SSM_STATE
ATTN_WIDTH = 512
HEAD_DIM = 64
N_Q_HEADS = 8
N_KV_HEADS = 2
Q_PER_KV = 4
KV_WIDTH = 128
IN_WIDTH = 1280
BLOCK = 128
ROPE_DIM = 16
ROPE_THETA = 500000.0
D_FF = 2816
NORM_EPS = 1e-6
MASK_VALUE = -1e30
ADAM_LR = 0.001
ADAM_B1 = 0.9
ADAM_B2 = 0.999
ADAM_EPS = 1e-08
ADAM_WD = 0.01
ADAM_STEP = 10

N_DEV = 8
SCAN_CHUNKS = 8
SCAN_UNROLL = 8
FFN_CHUNK = 2816
TOKEN_TILE = 256
VMEM_LIMIT = 56 * 1024 * 1024

_F32 = jnp.float32
_BF16 = jnp.bfloat16
_MXU = jnp.bfloat16

_NN = ((1,), (0,))
_NT = ((1,), (1,))
_TN = ((0,), (0,))


def _dot(a, b, dims):
    return lax.dot_general(a.astype(_MXU), b.astype(_MXU), (dims, ((), ())),
                           preferred_element_type=_F32)


def _dot_exact(a, b, dims):
    return lax.dot_general(a.astype(_F32), b.astype(_F32), (dims, ((), ())),
                           precision=lax.Precision.HIGHEST, preferred_element_type=_F32)


def _iota(shape, dim):
    return lax.broadcasted_iota(jnp.int32, shape, dim)


def _rms_fwd(x, g):
    r = lax.rsqrt(jnp.mean(x * x, axis=-1, keepdims=True) + NORM_EPS)
    return x * r * g, r


def _rms_bwd(dy, x, g, r):
    a = dy * g
    xn = x * r
    dx = r * (a - xn * jnp.mean(a * xn, axis=-1, keepdims=True))
    dg = jnp.sum(dy * xn, axis=0, keepdims=True)
    return dx, dg


def _call(body, grid, in_specs, out_specs, out_shape, name, scratch=(), tokens=()):
    params = pltpu.CompilerParams(dimension_semantics=("arbitrary",) * len(grid),
                                  vmem_limit_bytes=VMEM_LIMIT)
    n_in, n_tok = len(in_specs), len(tokens)

    def run(*refs):
        return body(*refs[:n_in], *refs[n_in + n_tok:])

    call = pl.pallas_call(run, grid=grid,
                          in_specs=list(in_specs) + [pl.BlockSpec(memory_space=pl.ANY)] * n_tok,
                          out_specs=out_specs, out_shape=out_shape, scratch_shapes=list(scratch),
                          compiler_params=params, name=name)
    return lambda *args: call(*args, *tokens)


def _rows(tm, n):
    return pl.BlockSpec((tm, n), lambda i: (i, 0))


def _whole(shape):
    nd = len(shape)
    return pl.BlockSpec(shape, lambda i: (0,) * nd)


def _sds(shape, dtype):
    return jax.ShapeDtypeStruct(shape, dtype)


def _tile(L):
    return min(TOKEN_TILE, L)


def _chunk_tile(L):
    return L // SCAN_CHUNKS


def _chunk_block(L, n):
    return pl.BlockSpec((_chunk_tile(L), n), lambda i: (0, i))


def _chunk_shape(L, n):
    return (_chunk_tile(L), SCAN_CHUNKS * n)


def _accumulate(ref, val, first):
    @pl.when(first)
    def _():
        ref[...] = val

    @pl.when(jnp.logical_not(first))
    def _():
        ref[...] += val


def _rope_rows():
    half = ROPE_DIM // 2
    inv = (np.float32(ROPE_THETA) ** (-np.arange(half, dtype=np.float32) * np.float32(2.0) / np.float32(ROPE_DIM))).astype(np.float32)
    col = np.arange(KV_WIDTH) % HEAD_DIM
    freq = np.where(col < ROPE_DIM, inv[col % half], 0.0).astype(np.float32)
    sign = np.where(col < half, -1.0, np.where(col < ROPE_DIM, 1.0, 0.0)).astype(np.float32)
    return freq[None, :], sign[None, :]


def _rope_tables(pos_col):
    L = pos_col.shape[0]
    tm = _tile(L)
    freq, sign = _rope_rows()

    def body(pos_ref, freq_ref, sign_ref, cos_ref, sin_ref):
        ang = pos_ref[...].astype(_F32) * freq_ref[...]
        cos_ref[...] = jnp.cos(ang)
        sin_ref[...] = jnp.sin(ang) * sign_ref[...]

    return _call(body, (L // tm,),
                 [_rows(tm, 1), _whole((1, KV_WIDTH)), _whole((1, KV_WIDTH))],
                 [_rows(tm, KV_WIDTH), _rows(tm, KV_WIDTH)],
                 [_sds((L, KV_WIDTH), _F32)] * 2, "rope_tables")(pos_col, jnp.asarray(freq), jnp.asarray(sign))


def _widen(t, width):
    return t if width == KV_WIDTH else jnp.concatenate([t] * (width // KV_WIDTH), axis=1)


def _rope_partner(t):
    w = t.shape[1]
    in_head = _iota((1, w), 1) & (HEAD_DIM - 1)
    second = jnp.where(in_head < ROPE_DIM, pltpu.roll(t, ROPE_DIM // 2, 1), 0.0)
    return jnp.where(in_head < ROPE_DIM // 2, pltpu.roll(t, w - ROPE_DIM // 2, 1), second)


def _rope_apply(t, cos_t, sin_t):
    w = t.shape[1]
    return t * _widen(cos_t, w) + _rope_partner(t) * _widen(sin_t, w)


def _rope_transpose(dt, cos_t, sin_t):
    w = dt.shape[1]
    return dt * _widen(cos_t, w) + _rope_partner(dt * _widen(sin_t, w))


def _in_proj(x, g_pre_mix, w_in, cos_t, sin_t):
    L = x.shape[0]
    tm = _chunk_tile(L)

    def body(x_ref, g_ref, w_ref, cos_ref, sin_ref, hn_ref, u_ref, q_ref, k_ref, v_ref):
        hn, _ = _rms_fwd(x_ref[...], g_ref[...])
        hn = hn.astype(_BF16)
        hn_ref[...] = hn
        proj = _dot(hn, w_ref[...], _NT)
        u_ref[...] = proj[:, :SSM_WIDTH]
        q = proj[:, SSM_WIDTH:SSM_WIDTH + ATTN_WIDTH]
        k = proj[:, SSM_WIDTH + ATTN_WIDTH:SSM_WIDTH + ATTN_WIDTH + KV_WIDTH]
        cos_v, sin_v = cos_ref[...], sin_ref[...]
        q_ref[...] = _rope_apply(q, cos_v, sin_v).astype(_BF16)
        k_ref[...] = _rope_apply(k, cos_v, sin_v).astype(_BF16)
        v_ref[...] = proj[:, SSM_WIDTH + ATTN_WIDTH + KV_WIDTH:].astype(_BF16)

    return _call(body, (L // tm,),
                 [_rows(tm, D_MODEL), _whole((1, D_MODEL)), _whole((IN_WIDTH, D_MODEL)),
                  _rows(tm, KV_WIDTH), _rows(tm, KV_WIDTH)],
                 [_rows(tm, D_MODEL), _chunk_block(L, SSM_WIDTH), _rows(tm, ATTN_WIDTH),
                  _rows(tm, KV_WIDTH), _rows(tm, KV_WIDTH)],
                 [_sds((L, D_MODEL), _BF16), _sds(_chunk_shape(L, SSM_WIDTH), _F32), _sds((L, ATTN_WIDTH), _BF16),
                  _sds((L, KV_WIDTH), _BF16), _sds((L, KV_WIDTH), _BF16)],
                 "in_proj")(x, g_pre_mix, w_in, cos_t, sin_t)


def _s5_discretize(lam_re, lam_im, log_dt):
    lr = jnp.minimum(lam_re, -1e-4)
    li = lam_im
    dt = jnp.exp(log_dt)
    mag = jnp.exp(lr * dt)
    ar = mag * jnp.cos(li * dt)
    ai = mag * jnp.sin(li * dt)
    den = lr * lr + li * li
    fr = ((ar - 1.0) * lr + ai * li) / den
    fi = (ai * lr - (ar - 1.0) * li) / den
    return ar, ai, fr, fi


SUPER = 4
SB_STATE = N_STATE // SUPER
SB_WIDTH = SSM_WIDTH // SUPER


def _sb_state(k):
    return slice(SB_STATE * k, SB_STATE * (k + 1))


def _sb_width(k):
    return slice(SB_WIDTH * k, SB_WIDTH * (k + 1))


def _dt_column(log_dt_row):
    eye = _iota((SSM_GROUPS, SSM_GROUPS), 0) == _iota((SSM_GROUPS, SSM_GROUPS), 1)
    return jnp.sum(jnp.where(eye, log_dt_row, 0.0), axis=1, keepdims=True)


def _group_masks():
    e64 = ((_iota((SSM_STATE, N_STATE), 1) & (SSM_STATE - 1)) == _iota((SSM_STATE, N_STATE), 0)).astype(_F32)
    own = _iota((SSM_GROUPS, N_STATE), 0) == (_iota((SSM_GROUPS, N_STATE), 1) >> 6)
    return e64, own


def _rows_of_group():
    return ((_iota((SSM_WIDTH, SSM_GROUPS), 0) >> 4) == _iota((SSM_WIDTH, SSM_GROUPS), 1)).astype(_F32)


def _ssm_prep(lam_re, lam_im, log_dt, b_re, b_im, c_re, c_im):
    def body(lr_ref, li_ref, ld_ref, bre, bim, cre, cim, ar_ref, ai_ref, btr, bti, ctr, cti):
        ar, ai, fr, fi = _s5_discretize(lr_ref[...], li_ref[...], _dt_column(ld_ref[...]))
        e64, own = _group_masks()
        mask_c = (_iota((SSM_WIDTH, N_STATE), 0) >> 4) == (_iota((SSM_WIDTH, N_STATE), 1) >> 6)

        def to_row(t):
            return jnp.sum(jnp.where(own, _dot_exact(t, e64, _NN), 0.0), axis=0, keepdims=True)

        def fold(m):
            full = jnp.where(mask_c, _dot(m, e64, _NN), 0.0)
            return sum(full[_sb_width(k), :] for k in range(SUPER)).astype(_BF16)

        ar_ref[...] = to_row(ar)
        ai_ref[...] = to_row(ai)
        spread = _rows_of_group()
        fr_t = _dot_exact(spread, fr, _NN)
        fi_t = _dot_exact(spread, fi, _NN)
        btr[...] = fold(fr_t * bre[...] - fi_t * bim[...])
        bti[...] = fold(fr_t * bim[...] + fi_t * bre[...])
        ctr[...] = fold(cre[...])
        cti[...] = fold(cim[...])

    row = (1, N_STATE)
    ins = [lam_re, lam_im, log_dt, b_re, b_im, c_re, c_im]
    return _call(body, (1,), [_whole(a.shape) for a in ins],
                 [_whole(row), _whole(row)] + [_whole((SB_WIDTH, N_STATE))] * 4,
                 [_sds(row, _F32), _sds(row, _F32)] + [_sds((SB_WIDTH, N_STATE), _BF16)] * 4,
                 "ssm_prep")(*ins)


def _complex_power(ar, ai, n):
    pr, pi = jnp.ones_like(ar), jnp.zeros_like(ai)
    while n:
        if n & 1:
            pr, pi = pr * ar - pi * ai, pr * ai + pi * ar
        ar, ai = ar * ar - ai * ai, 2.0 * ar * ai
        n >>= 1
    return pr, pi


def _chunk_carries(er, ei, pr, pi, reverse):
    rows = _iota(er.shape, 0)
    sr = jnp.zeros_like(pr)
    si = jnp.zeros_like(pi)
    out_r = jnp.zeros_like(er)
    out_i = jnp.zeros_like(ei)
    order = range(SCAN_CHUNKS - 1, 0, -1) if reverse else range(SCAN_CHUNKS - 1)
    for c in order:
        e_r = er[c:c + 1, :]
        e_i = ei[c:c + 1, :]
        sr, si = pr * sr - pi * si + e_r, pr * si + pi * sr + e_i
        nxt = c - 1 if reverse else c + 1
        out_r = jnp.where(rows == nxt, sr, out_r)
        out_i = jnp.where(rows == nxt, si, out_i)
    return out_r, out_i


_GELU_K = math.sqrt(2.0 / math.pi)
_GELU_C = 0.044715


def _gelu(y):
    return 0.5 * y * (1.0 + jnp.tanh(_GELU_K * (y + _GELU_C * y * y * y)))


def _gelu_grad(y):
    t = jnp.tanh(_GELU_K * (y + _GELU_C * y * y * y))
    return 0.5 * (1.0 + t) + 0.5 * y * (1.0 - t * t) * _GELU_K * (1.0 + 3.0 * _GELU_C * y * y)


def _step_rows(t):
    return pl.ds(pl.multiple_of(t * SCAN_CHUNKS, SCAN_CHUNKS), SCAN_CHUNKS)


def _scan_in_place(br, bi, ar, ai, T, carries=None):
    W = br.shape[1]
    ar8 = jnp.broadcast_to(ar, (SCAN_CHUNKS, W))
    ai8 = jnp.broadcast_to(ai, (SCAN_CHUNKS, W))

    def local(t, c):
        cr, ci = c
        rows = _step_rows(t)
        return ar8 * cr - ai8 * ci + br[rows, :], ar8 * ci + ai8 * cr + bi[rows, :]

    if carries is None:
        zero = jnp.zeros((SCAN_CHUNKS, W), _F32)
        er, ei = lax.fori_loop(0, T, local, (zero, zero), unroll=SCAN_UNROLL)
        pr, pi = _complex_power(ar, ai, T)
        carries = _chunk_carries(er, ei, pr, pi, reverse=False)

    def final(t, c):
        nr, ni = local(t, c)
        rows = _step_rows(t)
        br[rows, :] = nr
        bi[rows, :] = ni
        return nr, ni

    lax.fori_loop(0, T, final, carries, unroll=SCAN_UNROLL)
    return carries


def _scan_reverse_in_place(dr, di, xr, xi, ar, ai, T):
    W = dr.shape[1]
    ar8 = jnp.broadcast_to(ar, (SCAN_CHUNKS, W))
    ai8 = jnp.broadcast_to(ai, (SCAN_CHUNKS, W))

    def local(t, c):
        cr, ci = c
        rows = _step_rows(t)
        return ar8 * cr + ai8 * ci + dr[rows, :], ar8 * ci - ai8 * cr + di[rows, :]

    zero = jnp.zeros((SCAN_CHUNKS, W), _F32)
    er, ei = lax.fori_loop(0, T, lambda k, c: local(T - 1 - k, c), (zero, zero), unroll=SCAN_UNROLL)
    pr, pi = _complex_power(ar, -ai, T)
    sr, si = _chunk_carries(er, ei, pr, pi, reverse=True)

    def grad_a(acc, nr, ni, xpr, xpi):
        return acc[0] + nr * xpr + ni * xpi, acc[1] + ni * xpr - nr * xpi

    def final(k, c):
        t = T - 1 - k
        nr, ni = local(t, c[:2])
        rows = _step_rows(t)
        dr[rows, :] = nr
        di[rows, :] = ni
        before = _step_rows(t - 1)
        gr, gi = grad_a(c[2:], nr, ni, xr[before, :], xi[before, :])
        return nr, ni, gr, gi

    cr, ci, gr, gi = lax.fori_loop(0, T - 1, final, (sr, si, zero, zero), unroll=SCAN_UNROLL)
    nr, ni = local(0, (cr, ci))
    dr[_step_rows(0), :] = nr
    di[_step_rows(0), :] = ni
    first = _iota((SCAN_CHUNKS, W), 0) == 0
    last = _step_rows(T - 1)
    xpr = jnp.where(first, 0.0, pltpu.roll(xr[last, :], 1, 0))
    xpi = jnp.where(first, 0.0, pltpu.roll(xi[last, :], 1, 0))
    gr, gi = grad_a((gr, gi), nr, ni, xpr, xpi)
    return jnp.sum(gr, axis=0, keepdims=True), jnp.sum(gi, axis=0, keepdims=True)


def _ssm_super_specs(L):
    width = pl.BlockSpec((L, SB_WIDTH), lambda k: (0, k))
    matrix = pl.BlockSpec((SB_WIDTH, SB_STATE), lambda k: (0, k))
    row = pl.BlockSpec((1, SB_STATE), lambda k: (0, k))
    return width, matrix, row


def _ssm_states(u_ref, br_ref, bi_ref, ar_ref, ai_ref, xr, xi, T, carries=None):
    ub = u_ref[...].astype(_BF16)
    xr[...] = _dot(ub, br_ref[...], _NN)
    xi[...] = _dot(ub, bi_ref[...], _NN)
    return _scan_in_place(xr, xi, ar_ref[...], ai_ref[...], T, carries)


def _ssm_core_fwd(u, bt_re, bt_im, ct_re, ct_im, a_re, a_im):
    L = u.shape[0]
    T = L // SCAN_CHUNKS

    def body(u_ref, br_ref, bi_ref, cr_ref, ci_ref, ar_ref, ai_ref, y_ref, sr_ref, si_ref, xr, xi):
        sr_ref[...], si_ref[...] = _ssm_states(u_ref, br_ref, bi_ref, ar_ref, ai_ref, xr, xi, T)
        y_ref[...] = _dot(xr[...], cr_ref[...], _NT) - _dot(xi[...], ci_ref[...], _NT)

    width, matrix, row = _ssm_super_specs(L)
    carry = pl.BlockSpec((SCAN_CHUNKS, SB_STATE), lambda k: (0, k))
    return _call(body, (SUPER,), [width, matrix, matrix, matrix, matrix, row, row], [width, carry, carry],
                 [_sds((L, SSM_WIDTH), _F32)] + [_sds((SCAN_CHUNKS, N_STATE), _F32)] * 2, "ssm_core_fwd",
                 scratch=[pltpu.VMEM((L, SB_STATE), _F32)] * 2)(u, bt_re, bt_im, ct_re, ct_im, a_re, a_im)


def _ssm_core_bwd(u, dy, dud, carry_re, carry_im, bt_re, bt_im, ct_re, ct_im, a_re, a_im, tokens=()):
    L = u.shape[0]
    T = L // SCAN_CHUNKS

    def body(u_ref, dy_ref, dud_ref, sr_ref, si_ref, br_ref, bi_ref, cr_ref, ci_ref, ar_ref, ai_ref,
             du_ref, dcr_ref, dci_ref, dbr_ref, dbi_ref, dar_ref, dai_ref, xr, xi, lr, li):
        _ssm_states(u_ref, br_ref, bi_ref, ar_ref, ai_ref, xr, xi, T, (sr_ref[...], si_ref[...]))
        dyb = dy_ref[...]
        lr[...] = _dot(dyb, cr_ref[...], _NN)
        li[...] = -_dot(dyb, ci_ref[...], _NN)
        da_re, da_im = _scan_reverse_in_place(lr, li, xr, xi, ar_ref[...], ai_ref[...], T)
        dar_ref[...] = da_re
        dai_ref[...] = da_im
        du_ref[...] = _dot(lr[...], br_ref[...], _NT) + _dot(li[...], bi_ref[...], _NT) + dud_ref[...]
        ub = u_ref[...].astype(_BF16)
        dcr_ref[...] = _dot(dyb, xr[...], _TN)
        dci_ref[...] = _dot(dyb, xi[...], _TN)
        dbr_ref[...] = _dot(ub, lr[...], _TN)
        dbi_ref[...] = _dot(ub, li[...], _TN)

    width, matrix, row = _ssm_super_specs(L)
    carry = pl.BlockSpec((SCAN_CHUNKS, SB_STATE), lambda k: (0, k))
    return _call(body, (SUPER,), [width, width, width, carry, carry, matrix, matrix, matrix, matrix, row, row],
                 [width] + [matrix] * 4 + [row] * 2,
                 [_sds((L, SSM_WIDTH), _F32)] + [_sds((SB_WIDTH, N_STATE), _F32)] * 4 + [_sds((1, N_STATE), _F32)] * 2,
                 "ssm_core_bwd", scratch=[pltpu.VMEM((L, SB_STATE), _F32)] * 4,
                 tokens=tokens)(u, dy, dud, carry_re, carry_im, bt_re, bt_im, ct_re, ct_im, a_re, a_im)


def _ssm_out(cx, u, d_row, w_glu, b_glu, g_ssm):
    L = u.shape[0]
    tm = _tile(L)

    def body(cx_ref, u_ref, d_ref, w_ref, b_ref, g_ref, y_ref, z_ref, n_ref, stage):
        y = cx_ref[...] + d_ref[...] * u_ref[...]
        y_ref[...] = y
        z = _dot(_gelu(y), w_ref[...], _NT) + b_ref[...]
        z_ref[...] = z
        out = z[:, :SSM_WIDTH] * jax.nn.sigmoid(z[:, SSM_WIDTH:])
        n, _ = _rms_fwd(out, g_ref[...])
        for k in range(SSM_WIDTH // _LANES):
            stage[k] = n[:, _LANES * k:_LANES * (k + 1)]
            for c in range(SCAN_CHUNKS):
                rows = stage[k, pl.ds(c, tm // SCAN_CHUNKS, stride=SCAN_CHUNKS), :]
                lane = SSM_WIDTH * c + _LANES * k
                n_ref[:, lane:lane + _LANES] = rows.astype(_BF16)

    return _call(body, (L // tm,),
                 [_rows(tm, SSM_WIDTH), _rows(tm, SSM_WIDTH), _whole((1, SSM_WIDTH)),
                  _whole((2 * SSM_WIDTH, SSM_WIDTH)), _whole((1, 2 * SSM_WIDTH)), _whole((1, SSM_WIDTH))],
                 [_rows(tm, SSM_WIDTH), _rows(tm, 2 * SSM_WIDTH), _rows(tm // SCAN_CHUNKS, SCAN_CHUNKS * SSM_WIDTH)],
                 [_sds((L, SSM_WIDTH), _F32), _sds((L, 2 * SSM_WIDTH), _F32), _sds(_chunk_shape(L, SSM_WIDTH), _BF16)],
                 "ssm_out", scratch=[pltpu.VMEM((SSM_WIDTH // _LANES, tm, _LANES), _F32)])(
        cx, u, d_row, w_glu, b_glu, g_ssm)


def _ssm_out_bwd(dn, y, z, u, d_row, w_glu, g_ssm):
    L = u.shape[0]
    tm = _tile(L)

    def body(dn_ref, y_ref, z_ref, u_ref, d_ref, w_ref, g_ref,
             gy_ref, dz_ref, dy_ref, dud_ref, dg_ref, db_ref, dd_ref, stage):
        first = pl.program_id(0) == 0
        for k in range(SSM_WIDTH // _LANES):
            for c in range(SCAN_CHUNKS):
                lane = SSM_WIDTH * c + _LANES * k
                stage[k, pl.ds(c, tm // SCAN_CHUNKS, stride=SCAN_CHUNKS), :] = dn_ref[:, lane:lane + _LANES]
        dn = jnp.concatenate([stage[k] for k in range(SSM_WIDTH // _LANES)], axis=1)
        z = z_ref[...]
        z1, z2 = z[:, :SSM_WIDTH], z[:, SSM_WIDTH:]
        sig = jax.nn.sigmoid(z2)
        out = z1 * sig
        g = g_ref[...]
        _, r = _rms_fwd(out, g)
        dout, dg = _rms_bwd(dn, out, g, r)
        _accumulate(dg_ref, dg, first)
        dz = jnp.concatenate([dout * sig, dout * z1 * sig * (1.0 - sig)], axis=1)
        _accumulate(db_ref, jnp.sum(dz, axis=0, keepdims=True), first)
        dzb = dz.astype(_BF16)
        dz_ref[...] = dzb
        y = y_ref[...]
        gy_ref[...] = _gelu(y).astype(_BF16)
        dy = _dot(dzb, w_ref[...], _NN) * _gelu_grad(y)
        u = u_ref[...]
        _accumulate(dd_ref, jnp.sum(dy * u, axis=0, keepdims=True), first)
        dud_ref[...] = d_ref[...] * dy
        dy_ref[...] = dy.astype(_BF16)

    row = _whole((1, SSM_WIDTH))
    return _call(body, (L // tm,),
                 [_rows(tm // SCAN_CHUNKS, SCAN_CHUNKS * SSM_WIDTH), _rows(tm, SSM_WIDTH), _rows(tm, 2 * SSM_WIDTH),
                  _rows(tm, SSM_WIDTH), row, _whole((2 * SSM_WIDTH, SSM_WIDTH)), row],
                 [_rows(tm, SSM_WIDTH), _rows(tm, 2 * SSM_WIDTH), _rows(tm, SSM_WIDTH), _rows(tm, SSM_WIDTH),
                  row, _whole((1, 2 * SSM_WIDTH)), row],
                 [_sds((L, SSM_WIDTH), _BF16), _sds((L, 2 * SSM_WIDTH), _BF16), _sds((L, SSM_WIDTH), _BF16),
                  _sds((L, SSM_WIDTH), _F32),
                  _sds((1, SSM_WIDTH), _F32), _sds((1, 2 * SSM_WIDTH), _F32), _sds((1, SSM_WIDTH), _F32)],
                 "ssm_out_bwd", scratch=[pltpu.VMEM((SSM_WIDTH // _LANES, tm, _LANES), _F32)])(
        dn, y, z, u, d_row, w_glu, g_ssm)


_SSM_PACK = {"ssm_b_re": (0, SSM_WIDTH, 0, SSM_STATE), "ssm_c_re": (0, SSM_WIDTH, 64, SSM_STATE),
             "ssm_b_im": (512, SSM_WIDTH, 0, SSM_STATE), "ssm_c_im": (512, SSM_WIDTH, 64, SSM_STATE),
             "ssm_lambda_re": (1024, SSM_GROUPS, 0, SSM_STATE), "ssm_lambda_im": (1024, SSM_GROUPS, 64, SSM_STATE),
             "ssm_d": (1056, SSM_GROUPS, 0, SSM_GROUP), "ssm_log_dt": (1088, 1, 0, SSM_GROUPS)}
_PACK_TILE = 16
_SSM_PACK_ROWS = 1088 + _PACK_TILE


def _ssm_param_bwd(da_re, da_im, dbt_re, dbt_im, dct_re, dct_im, lam_re, lam_im, log_dt, b_re, b_im, g_d):
    def body(dar, dai, dbr, dbi, dcr, dci, lr_ref, li_ref, ld_ref, bre_ref, bim_ref, gd_ref, pack_ref):
        lane_in = _iota((SSM_STATE, _LANES), 0)
        lane_out = _iota((SSM_STATE, _LANES), 1)
        low = (lane_out == lane_in).astype(_F32)
        high = (lane_out == lane_in + SSM_STATE).astype(_F32)

        def side_by_side(a, b):
            return _dot_exact(a, low, _NN) + _dot_exact(b, high, _NN)

        tail = _SSM_PACK["ssm_d"][0]
        pack_ref[tail:, :] = jnp.zeros((_SSM_PACK_ROWS - tail, _LANES), _BF16)
        pack_ref[tail:tail + SSM_GROUPS, 0:SSM_GROUP] = gd_ref[...].astype(_BF16)
        own_c = (_iota((SB_WIDTH, SB_STATE), 0) >> 4) == (_iota((SB_WIDTH, SB_STATE), 1) >> 6)

        def unfold(ref):
            blocks = []
            for k in range(SUPER):
                t = jnp.where(own_c, ref[:, _sb_state(k)], 0.0)
                t = sum(t[:, 128 * i:128 * (i + 1)] for i in range(SB_STATE // 128))
                blocks.append((t + pltpu.roll(t, SSM_STATE, 1))[:, :SSM_STATE])
            return jnp.concatenate(blocks, axis=0)

        dbb_re, dbb_im = unfold(dbr), unfold(dbi)
        b_re, b_im = bre_ref[...], bim_ref[...]
        dt_col = _dt_column(ld_ref[...])
        (_, _, fr, fi), vjp = jax.vjp(_s5_discretize, lr_ref[...], li_ref[...], dt_col)
        spread = _rows_of_group()
        fr_t = _dot_exact(spread, fr, _NN)
        fi_t = _dot_exact(spread, fi, _NN)
        pack_ref[0:SSM_WIDTH, :] = side_by_side(fr_t * dbb_re + fi_t * dbb_im, unfold(dcr)).astype(_BF16)
        pack_ref[SSM_WIDTH:2 * SSM_WIDTH, :] = side_by_side(fr_t * dbb_im - fi_t * dbb_re, -unfold(dci)).astype(_BF16)
        d_fr = _dot_exact(spread, dbb_re * b_re + dbb_im * b_im, _TN)
        d_fi = _dot_exact(spread, dbb_im * b_re - dbb_re * b_im, _TN)
        e64, own = _group_masks()

        def from_row(ref):
            return _dot_exact(jnp.where(own, ref[...], 0.0), e64, _NT)

        d_lr, d_li, d_dt = vjp((from_row(dar), from_row(dai), d_fr, d_fi))
        lam_rows = _SSM_PACK["ssm_lambda_re"][0]
        pack_ref[lam_rows:lam_rows + SSM_GROUPS, :] = side_by_side(d_lr, d_li).astype(_BF16)
        eye = (_iota((SSM_GROUPS, SSM_GROUPS), 0) == _iota((SSM_GROUPS, SSM_GROUPS), 1)).astype(_F32)
        dt_row = _SSM_PACK["ssm_log_dt"][0]
        pack_ref[dt_row:dt_row + _PACK_TILE, 0:SSM_GROUPS] = _dot_exact(
            jnp.broadcast_to(d_dt, (SSM_GROUPS, 128)), eye, _TN)[0:_PACK_TILE].astype(_BF16)

    ins = [da_re, da_im, dbt_re, dbt_im, dct_re, dct_im, lam_re, lam_im, log_dt, b_re, b_im, g_d]
    out = (_SSM_PACK_ROWS, _LANES)
    return _call(body, (1,), [_whole(a.shape) for a in ins], _whole(out), _sds(out, _BF16), "ssm_param_bwd")(*ins)


def _head_spread(j):
    r = _iota((KV_WIDTH, 256), 0)
    c = _iota((KV_WIDTH, 256), 1)
    return (r == HEAD_DIM * j + (c & (HEAD_DIM - 1))).astype(_BF16)


STACK = Q_PER_KV * BLOCK


def _stack_heads(t):
    lane_head = _iota((1, 256), 1) >> 6
    return jnp.concatenate([jnp.where(lane_head == g, t, jnp.zeros_like(t)) for g in range(Q_PER_KV)], axis=0)


def _unstack_heads(t):
    lane_head = _iota((1, 256), 1) >> 6
    return sum(jnp.where(lane_head == g, t[BLOCK * g:BLOCK * (g + 1)], 0.0) for g in range(Q_PER_KV))


def _stacked_sinks(sink_ref, j):
    block = _iota((STACK, 1), 0) >> 7
    col = jnp.full((STACK, 1), sink_ref[Q_PER_KV * j], _F32)
    for g in range(1, Q_PER_KV):
        col = jnp.where(block == g, sink_ref[Q_PER_KV * j + g], col)
    return col


def _fold_heads(t, j):
    t = t[:, :KV_WIDTH] + t[:, KV_WIDTH:]
    t = t + pltpu.roll(t, HEAD_DIM, 1)
    return jnp.where((_iota((1, KV_WIDTH), 1) >> 6) == j, t, 0.0)


def _attn_scores(q_stacked, kt, blk, sink):
    s = _dot(q_stacked, kt, _NT) * (HEAD_DIM ** -0.5)
    qi = _iota((STACK, 2 * BLOCK), 0) & (BLOCK - 1)
    kj = _iota((STACK, 2 * BLOCK), 1)
    rel = qi + BLOCK - kj
    valid = (rel >= 0) & (rel < BLOCK) & (blk * BLOCK - BLOCK + kj >= 0)
    s = jnp.where(valid, s, MASK_VALUE)
    m = jnp.maximum(jnp.max(s, axis=-1, keepdims=True), sink)
    p = jnp.exp(s - m)
    e_sink = jnp.exp(sink - m)
    den = jnp.sum(p, axis=-1, keepdims=True) + e_sink
    return p / den, e_sink / den


def _sink_slot():
    return _iota((STACK, 2 * BLOCK), 1) == 0


def _prob_block():
    return pl.BlockSpec((None, N_KV_HEADS, STACK, 2 * BLOCK), lambda i: (i, 0, 0, 0))


def _attn_specs():
    prev = lambda i: (jnp.maximum(i - 1, 0), 0)
    cur = lambda i: (i, 0)
    kv = [pl.BlockSpec((BLOCK, KV_WIDTH), prev), pl.BlockSpec((BLOCK, KV_WIDTH), cur)]
    return [pl.BlockSpec((BLOCK, ATTN_WIDTH), cur)] + kv + kv


def _attn_fwd(q, k, v, sinks, g_attn):
    L = q.shape[0]

    def body(q_ref, kp_ref, kc_ref, vp_ref, vc_ref, sink_ref, g_ref, o_ref, n_ref, p_ref):
        blk = pl.program_id(0)
        kwin = jnp.concatenate([kp_ref[...], kc_ref[...]], axis=0)
        vwin = jnp.concatenate([vp_ref[...], vc_ref[...]], axis=0)
        halves = []
        for j in range(N_KV_HEADS):
            spread = _head_spread(j)
            kt = _dot(kwin, spread, _NN).astype(_BF16)
            vt = _dot(vwin, spread, _NN).astype(_BF16)
            qs = _stack_heads(q_ref[:, 256 * j:256 * (j + 1)])
            p, p_sink = _attn_scores(qs, kt, blk, _stacked_sinks(sink_ref, j))
            p_ref[j] = jnp.where(_sink_slot(), p_sink, p)
            halves.append(_unstack_heads(_dot(p, vt, _NN)))
        o = jnp.concatenate(halves, axis=1)
        o_ref[...] = o
        n, _ = _rms_fwd(o, g_ref[...])
        n_ref[...] = n.astype(_BF16)

    cur = lambda i: (i, 0)
    return _call(body, (L // BLOCK,),
                 _attn_specs() + [pl.BlockSpec(memory_space=pltpu.SMEM), _whole((1, ATTN_WIDTH))],
                 [pl.BlockSpec((BLOCK, ATTN_WIDTH), cur)] * 2 + [_prob_block()],
                 [_sds((L, ATTN_WIDTH), _F32), _sds((L, ATTN_WIDTH), _BF16),
                  _sds((L // BLOCK, N_KV_HEADS, STACK, 2 * BLOCK), _F32)],
                 "attn_fwd")(q, k, k, v, v, sinks, g_attn)


def _attn_bwd(q, k, v, o, dn, probs, g_attn):
    L = q.shape[0]

    def body(q_ref, kp_ref, kc_ref, vp_ref, vc_ref, o_ref, dn_ref, p_ref, g_ref,
             dq_ref, dk_ref, dv_ref, dsink_ref, dg_ref):
        blk = pl.program_id(0)
        first = blk == 0

        @pl.when(first)
        def _():
            dk_ref[...] = jnp.zeros_like(dk_ref)
            dv_ref[...] = jnp.zeros_like(dv_ref)
            dsink_ref[...] = jnp.zeros_like(dsink_ref)

        o = o_ref[...]
        g = g_ref[...]
        _, r = _rms_fwd(o, g)
        do, dg = _rms_bwd(dn_ref[...], o, g, r)
        _accumulate(dg_ref, dg, first)
        kwin = jnp.concatenate([kp_ref[...], kc_ref[...]], axis=0)
        vwin = jnp.concatenate([vp_ref[...], vc_ref[...]], axis=0)
        lane = _iota((1, 128), 1)
        dsink = jnp.zeros((1, 128), _F32)
        dkwin = jnp.zeros((2 * BLOCK, KV_WIDTH), _F32)
        dvwin = jnp.zeros((2 * BLOCK, KV_WIDTH), _F32)
        dq_halves = []
        for j in range(N_KV_HEADS):
            spread = _head_spread(j)
            kt = _dot(kwin, spread, _NN).astype(_BF16)
            vt = _dot(vwin, spread, _NN).astype(_BF16)
            qs = _stack_heads(q_ref[:, 256 * j:256 * (j + 1)])
            dos = _stack_heads(do[:, 256 * j:256 * (j + 1)]).astype(_BF16)
            saved = p_ref[j]
            p_sink = saved[:, 0:1]
            p = jnp.where(_sink_slot(), 0.0, saved)
            dp = _dot(dos, vt, _NT)
            delta = jnp.sum(p * dp, axis=-1, keepdims=True)
            ds = (p * (dp - delta) * (HEAD_DIM ** -0.5)).astype(_BF16)
            sink_term = p_sink * delta
            for g in range(Q_PER_KV):
                head_sum = jnp.sum(sink_term[BLOCK * g:BLOCK * (g + 1)], axis=0, keepdims=True)
                dsink = dsink - jnp.where(lane == Q_PER_KV * j + g, head_sum, 0.0)
            dvwin = dvwin + _fold_heads(_dot(p, dos, _TN), j)
            dkwin = dkwin + _fold_heads(_dot(ds, qs, _TN), j)
            dq_halves.append(_unstack_heads(_dot(ds, kt, _NN)))
        dq_ref[...] = jnp.concatenate(dq_halves, axis=1)
        dsink_ref[...] += dsink
        prev = pl.ds(pl.multiple_of(jnp.maximum(blk - 1, 0) * BLOCK, BLOCK), BLOCK)
        cur = pl.ds(pl.multiple_of(blk * BLOCK, BLOCK), BLOCK)
        dk_ref[prev, :] += dkwin[:BLOCK]
        dk_ref[cur, :] += dkwin[BLOCK:]
        dv_ref[prev, :] += dvwin[:BLOCK]
        dv_ref[cur, :] += dvwin[BLOCK:]

    cur = lambda i: (i, 0)
    blk_q = pl.BlockSpec((BLOCK, ATTN_WIDTH), cur)
    return _call(body, (L // BLOCK,),
                 _attn_specs() + [blk_q, blk_q, _prob_block(), _whole((1, ATTN_WIDTH))],
                 [blk_q, _whole((L, KV_WIDTH)), _whole((L, KV_WIDTH)), _whole((1, 128)), _whole((1, ATTN_WIDTH))],
                 [_sds((L, ATTN_WIDTH), _F32), _sds((L, KV_WIDTH), _F32), _sds((L, KV_WIDTH), _F32),
                  _sds((1, 128), _F32), _sds((1, ATTN_WIDTH), _F32)],
                 "attn_bwd")(q, k, k, v, v, o, dn, probs, g_attn)


def _out_proj(n_ssm, n_attn, x, w_out, g_post_mix, g_pre_ffn):
    L = x.shape[0]
    tm = _chunk_tile(L)

    def body(ns_ref, na_ref, x_ref, w_ref, g1_ref, g2_ref, merged_ref, mo_ref, h1_ref, hn2_ref):
        merged = jnp.concatenate([ns_ref[...], na_ref[...]], axis=1)
        merged_ref[...] = merged
        mo = _dot(merged, w_ref[...], _NN)
        mo_ref[...] = mo
        n, _ = _rms_fwd(mo, g1_ref[...])
        h1 = x_ref[...] + n
        h1_ref[...] = h1
        hn2, _ = _rms_fwd(h1, g2_ref[...])
        hn2_ref[...] = hn2.astype(_BF16)

    row = _whole((1, D_MODEL))
    return _call(body, (L // tm,),
                 [_chunk_block(L, SSM_WIDTH), _rows(tm, ATTN_WIDTH), _rows(tm, D_MODEL), _whole((D_MODEL, D_MODEL)),
                  row, row],
                 [_rows(tm, D_MODEL)] * 4,
                 [_sds((L, D_MODEL), _BF16), _sds((L, D_MODEL), _F32), _sds((L, D_MODEL), _F32), _sds((L, D_MODEL), _BF16)],
                 "out_proj")(n_ssm, n_attn, x, w_out, g_post_mix, g_pre_ffn)


def _ffn(hn2, h1, target, w_gate_up, w_down, g_pre_ffn, g_post_ffn):
    L = h1.shape[0]
    tm = _tile(L)
    half = FFN_CHUNK

    def body(hn2_ref, h1_ref, tgt_ref, wgu_hbm, wd_hbm, g2_ref, g3_ref,
             act_ref, dgu_ref, dff_ref, dh1_ref, loss_ref, dg3_ref, dg2_ref,
             wgu, wd, gu, sem):
        first = pl.program_id(0) == 0

        @pl.when(first)
        def _():
            c1 = pltpu.make_async_copy(wgu_hbm, wgu, sem.at[0])
            c2 = pltpu.make_async_copy(wd_hbm, wd, sem.at[1])
            c1.start()
            c2.start()
            c1.wait()
            c2.wait()

        hn2 = hn2_ref[...]
        ff = jnp.zeros((tm, D_MODEL), _F32)
        for c in range(D_FF // half):
            gate = _dot(hn2, wgu[half * c:half * (c + 1), :], _NT)
            up = _dot(hn2, wgu[D_FF + half * c:D_FF + half * (c + 1), :], _NT)
            gu[:, half * c:half * (c + 1)] = gate
            gu[:, D_FF + half * c:D_FF + half * (c + 1)] = up
            act = gate * jax.nn.sigmoid(gate) * up
            act_ref[half * c:half * (c + 1), :] = act.T.astype(_BF16)
            ff = ff + _dot(act, wd[half * c:half * (c + 1), :], _NN)
        g3 = g3_ref[...]
        n, r = _rms_fwd(ff, g3)
        h1 = h1_ref[...]
        err = h1 + n - tgt_ref[...]
        loss = 0.5 * jnp.sum(jnp.mean(err * err, axis=-1, keepdims=True), axis=0, keepdims=True)
        _accumulate(loss_ref, jnp.broadcast_to(loss, (1, 128)), first)
        dh2 = err * (1.0 / D_MODEL)
        dff, dg3 = _rms_bwd(dh2, ff, g3, r)
        _accumulate(dg3_ref, dg3, first)
        dffb = dff.astype(_BF16)
        dff_ref[...] = dffb
        dhn2 = jnp.zeros((tm, D_MODEL), _F32)
        for c in range(D_FF // half):
            dact = _dot(dffb, wd[half * c:half * (c + 1), :], _NT)
            gate = gu[:, half * c:half * (c + 1)]
            up = gu[:, D_FF + half * c:D_FF + half * (c + 1)]
            sig = jax.nn.sigmoid(gate)
            silu = gate * sig
            dgate = dact * up * (sig + silu * (1.0 - sig))
            dup = dact * silu
            dgu_ref[half * c:half * (c + 1), :] = dgate.T.astype(_BF16)
            dgu_ref[D_FF + half * c:D_FF + half * (c + 1), :] = dup.T.astype(_BF16)
            dhn2 = dhn2 + _dot(dgate, wgu[half * c:half * (c + 1), :], _NN)
            dhn2 = dhn2 + _dot(dup, wgu[D_FF + half * c:D_FF + half * (c + 1), :], _NN)
        g2 = g2_ref[...]
        _, r2 = _rms_fwd(h1, g2)
        dh1, dg2 = _rms_bwd(dhn2, h1, g2, r2)
        _accumulate(dg2_ref, dg2, first)
        dh1_ref[...] = dh2 + dh1

    row = _whole((1, D_MODEL))
    anyspace = pl.BlockSpec(memory_space=pl.ANY)
    return _call(body, (L // tm,),
                 [_rows(tm, D_MODEL), _rows(tm, D_MODEL), _rows(tm, D_MODEL), anyspace, anyspace, row, row],
                 [pl.BlockSpec((D_FF, tm), lambda i: (0, i)), pl.BlockSpec((2 * D_FF, tm), lambda i: (0, i)),
                  _rows(tm, D_MODEL), _rows(tm, D_MODEL), _whole((1, 128)), row, row],
                 [_sds((D_FF, L), _BF16), _sds((2 * D_FF, L), _BF16), _sds((L, D_MODEL), _BF16),
                  _sds((L, D_MODEL), _F32), _sds((1, 128), _F32), _sds((1, D_MODEL), _F32), _sds((1, D_MODEL), _F32)],
                 "ffn",
                 scratch=[pltpu.VMEM((2 * D_FF, D_MODEL), _BF16), pltpu.VMEM((D_FF, D_MODEL), _BF16),
                          pltpu.VMEM((tm, 2 * D_FF), _F32), pltpu.SemaphoreType.DMA((2,))],
                 )(hn2, h1, target, w_gate_up, w_down, g_pre_ffn, g_post_ffn)


def _out_proj_bwd(dh1, mo, w_out, g_post_mix, tokens=()):
    L = dh1.shape[0]
    tm = _chunk_tile(L)

    def body(dh1_ref, mo_ref, w_ref, g_ref, dmo_ref, dns_ref, dna_ref, dg_ref):
        first = pl.program_id(0) == 0
        mo = mo_ref[...]
        g = g_ref[...]
        _, r = _rms_fwd(mo, g)
        dmo, dg = _rms_bwd(dh1_ref[...], mo, g, r)
        _accumulate(dg_ref, dg, first)
        dmob = dmo.astype(_BF16)
        dmo_ref[...] = dmob
        dmerged = _dot(dmob, w_ref[...], _NT)
        dns_ref[...] = dmerged[:, :SSM_WIDTH]
        dna_ref[...] = dmerged[:, SSM_WIDTH:]

    row = _whole((1, D_MODEL))
    return _call(body, (L // tm,),
                 [_rows(tm, D_MODEL), _rows(tm, D_MODEL), _whole((D_MODEL, D_MODEL)), row],
                 [_rows(tm, D_MODEL), _chunk_block(L, SSM_WIDTH), _rows(tm, ATTN_WIDTH), row],
                 [_sds((L, D_MODEL), _BF16), _sds(_chunk_shape(L, SSM_WIDTH), _F32), _sds((L, ATTN_WIDTH), _F32),
                  _sds((1, D_MODEL), _F32)],
                 "out_proj_bwd", tokens=tokens)(dh1, mo, w_out, g_post_mix)


def _in_proj_bwd(du, dq, dk, dv, cos_t, sin_t, x, dh1, g_pre_mix, w_in, tokens=()):
    L = x.shape[0]
    tm = _chunk_tile(L)

    def body(du_ref, dq_ref, dk_ref, dv_ref, cos_ref, sin_ref, x_ref, dh1_ref, g_ref, w_ref,
             dproj_ref, dx_ref, dg_ref):
        first = pl.program_id(0) == 0
        cos_v, sin_v = cos_ref[...], sin_ref[...]
        dproj = jnp.concatenate([du_ref[...], _rope_transpose(dq_ref[...], cos_v, sin_v),
                                 _rope_transpose(dk_ref[...], cos_v, sin_v), dv_ref[...]], axis=1).astype(_BF16)
        dproj_ref[...] = dproj
        dhn = _dot(dproj, w_ref[...], _NN)
        x = x_ref[...]
        g = g_ref[...]
        _, r = _rms_fwd(x, g)
        dx, dg = _rms_bwd(dhn, x, g, r)
        _accumulate(dg_ref, dg, first)
        dx_ref[...] = dh1_ref[...] + dx

    row = _whole((1, D_MODEL))
    return _call(body, (L // tm,),
                 [_chunk_block(L, SSM_WIDTH), _rows(tm, ATTN_WIDTH), _rows(tm, KV_WIDTH), _rows(tm, KV_WIDTH),
                  _rows(tm, KV_WIDTH), _rows(tm, KV_WIDTH), _rows(tm, D_MODEL), _rows(tm, D_MODEL), row,
                  _whole((IN_WIDTH, D_MODEL))],
                 [_rows(tm, IN_WIDTH), _rows(tm, D_MODEL), row],
                 [_sds((L, IN_WIDTH), _BF16), _sds((L, D_MODEL), _F32), _sds((1, D_MODEL), _F32)],
                 "in_proj_bwd", tokens=tokens)(du, dq, dk, dv, cos_t, sin_t, x, dh1, g_pre_mix, w_in)


def _matmul_nn(a, b, out_dtype, name):
    M, K = a.shape
    N = b.shape[1]
    tm = next(t for t in (704, 512, 256, 128) if M % t == 0)
    tn = N if N <= D_MODEL else next(t for t in (512, 256, 128) if N % t == 0)

    def body(a_ref, b_ref, o_ref):
        o_ref[...] = _dot(a_ref[...], b_ref[...], _NN).astype(out_dtype)

    params = pltpu.CompilerParams(dimension_semantics=("arbitrary", "arbitrary"), vmem_limit_bytes=VMEM_LIMIT)
    return pl.pallas_call(body, grid=(M // tm, N // tn),
                          in_specs=[pl.BlockSpec((tm, K), lambda i, j: (i, 0)),
                                    pl.BlockSpec((K, tn), lambda i, j: (0, j))],
                          out_specs=pl.BlockSpec((tm, tn), lambda i, j: (i, j)),
                          out_shape=_sds((M, N), out_dtype), compiler_params=params, name=name)(a, b)


def _matmul_tn(a, b, out_dtype, name, scale=1.0):
    K, M = a.shape
    N = b.shape[1]
    tm = next(t for t in (512, 256, 128) if M % t == 0)
    tn = N if N <= D_MODEL else next(t for t in (512, 256, 128) if N % t == 0)

    def body(a_ref, b_ref, o_ref):
        acc = _dot(a_ref[...], b_ref[...], _TN)
        o_ref[...] = (acc if scale == 1.0 else acc * scale).astype(out_dtype)

    params = pltpu.CompilerParams(dimension_semantics=("arbitrary", "arbitrary"), vmem_limit_bytes=VMEM_LIMIT)
    return pl.pallas_call(body, grid=(M // tm, N // tn),
                          in_specs=[pl.BlockSpec((K, tm), lambda i, j: (0, i)),
                                    pl.BlockSpec((K, tn), lambda i, j: (0, j))],
                          out_specs=pl.BlockSpec((tm, tn), lambda i, j: (i, j)),
                          out_shape=_sds((M, N), out_dtype), compiler_params=params, name=name)(a, b)


def _local_step(x, pos, target, p, fetch, publish, progress):
    L = x.shape[0]
    T = L // SCAN_CHUNKS
    cos_t, sin_t = _rope_tables(pos.reshape(L, 1))
    w_in, = fetch(("w_in",), None)
    hn, u, q, k, v = _in_proj(x, p["g_pre_mix"], w_in, cos_t, sin_t)

    ssm = {n: _to_2d(n, p[n]) for n in ("ssm_lambda_re", "ssm_lambda_im", "ssm_log_dt", "ssm_b_re", "ssm_b_im",
                                        "ssm_c_re", "ssm_c_im")}
    d_row = p["ssm_d"].reshape(1, SSM_WIDTH)
    a_re, a_im, bt_re, bt_im, ct_re, ct_im = _ssm_prep(
        ssm["ssm_lambda_re"], ssm["ssm_lambda_im"], ssm["ssm_log_dt"], ssm["ssm_b_re"], ssm["ssm_b_im"],
        ssm["ssm_c_re"], ssm["ssm_c_im"])

    u_c = u.reshape(L, SSM_WIDTH)
    cx, carry_re, carry_im = _ssm_core_fwd(u_c, bt_re, bt_im, ct_re, ct_im, a_re, a_im)
    w_glu, = fetch(("w_glu",), cx)
    y, z, n_ssm = _ssm_out(cx, u_c, d_row, w_glu, p["b_glu"], p["g_ssm_out"])

    sinks = p["attn_sinks"].reshape(N_Q_HEADS)
    o, n_attn, probs = _attn_fwd(q, k, v, sinks, p["g_attn_out"])
    w_out, = fetch(("w_out",), n_attn)
    merged, mo, h1, hn2 = _out_proj(n_ssm, n_attn, x, w_out, p["g_post_mix"], p["g_pre_ffn"])
    w_gate_up, w_down = fetch(("w_gate_up", "w_down"), hn2)
    act_t, dgu_t, dff, dh1, loss, dg_post_ffn, dg_pre_ffn = _ffn(
        hn2, h1, target, w_gate_up, w_down, p["g_pre_ffn"], p["g_post_ffn"])
    grads = {"g_post_ffn": dg_post_ffn, "g_pre_ffn": dg_pre_ffn}
    tokens = publish({"w_down": _matmul_nn(act_t, dff, _BF16, "grad_w_down"),
                      "w_gate_up": _matmul_nn(dgu_t, hn2, _BF16, "grad_w_gate_up")})

    dmo, dn_ssm, dn_attn, grads["g_post_mix"] = _out_proj_bwd(dh1, mo, w_out, p["g_post_mix"], tokens)
    grad_w_out = _matmul_tn(merged, dmo, _BF16, "grad_w_out")

    dq, dk, dv, dsink, grads["g_attn_out"] = _attn_bwd(q, k, v, o, dn_attn, probs, p["g_attn_out"])
    grads["attn_sinks"] = dsink

    gy, dz, dy, dud, grads["g_ssm_out"], grads["b_glu"], dd = _ssm_out_bwd(
        dn_ssm, y, z, u_c, d_row, w_glu, p["g_ssm_out"])
    tokens = progress(dy)
    tokens += publish({"w_out": grad_w_out, "w_glu": _matmul_tn(dz, gy, _BF16, "grad_w_glu")})
    du_c, dct_re, dct_im, dbt_re, dbt_im, da_re, da_im = _ssm_core_bwd(
        u_c, dy, dud, carry_re, carry_im, bt_re, bt_im, ct_re, ct_im, a_re, a_im, tokens)
    ssm_pack = _ssm_param_bwd(
        da_re, da_im, dbt_re, dbt_im, dct_re, dct_im,
        ssm["ssm_lambda_re"], ssm["ssm_lambda_im"], ssm["ssm_log_dt"], ssm["ssm_b_re"], ssm["ssm_b_im"],
        dd.reshape(SSM_GROUPS, SSM_GROUP))
    grads.update(ssm_pack=ssm_pack, loss=loss)
    publish(grads)

    du = du_c.reshape(_chunk_shape(L, SSM_WIDTH))
    dproj, grad_x, g_pre_mix = _in_proj_bwd(du, dq, dk, dv, cos_t, sin_t, x, dh1, p["g_pre_mix"], w_in, [ssm_pack])
    publish({"g_pre_mix": g_pre_mix, "w_in": _matmul_tn(dproj, hn, _BF16, "grad_w_in")})
    return grad_x


_MESH = pl.DeviceIdType.MESH
_PEERS = N_DEV - 1


def _mesh_pos():
    return lax.axis_index("x"), lax.axis_index("y"), lax.axis_index("c")


def _dev_index(px, py, pc):
    return 4 * px + 2 * py + pc


def _peer(x, y, c, r):
    return (x ^ ((r >> 2) & 1), y ^ ((r >> 1) & 1), c ^ (r & 1))


def _sequencer_exchange(sources, blocked, name, collective_id):
    n = len(sources)
    flags = blocked

    def body(*refs):
        srcs, zones = refs[:n], refs[n:2 * n]
        send_sems, recv_sems, local_sems = refs[2 * n:]
        x, y, c = _mesh_pos()
        me = _dev_index(x, y, c)
        barrier = pltpu.get_barrier_semaphore()
        for r in range(1, N_DEV):
            pl.semaphore_signal(barrier, inc=1, device_id=_peer(x, y, c, r), device_id_type=_MESH)
        pl.semaphore_wait(barrier, _PEERS)
        local, sends, recvs = [], [], []
        for w in range(n):
            cp = pltpu.make_async_copy(srcs[w].at[me] if flags[w] else srcs[w], zones[w].at[me], local_sems.at[w])
            cp.start()
            local.append(cp)
            for r in range(1, N_DEV):
                peer = _peer(x, y, c, r)
                idx = _dev_index(*peer)
                k = _PEERS * w + r - 1
                src = srcs[w].at[idx] if flags[w] else srcs[w]
                send = pltpu.make_async_remote_copy(
                    src_ref=src, dst_ref=zones[w].at[me], send_sem=send_sems.at[k], recv_sem=recv_sems.at[k],
                    device_id=peer, device_id_type=_MESH)
                send.start()
                sends.append(send)
                recvs.append(pltpu.make_async_remote_copy(
                    src_ref=src, dst_ref=zones[w].at[idx], send_sem=send_sems.at[k], recv_sem=recv_sems.at[k],
                    device_id=peer, device_id_type=_MESH))
        for cp in recvs:
            cp.wait_recv()
        for cp in sends:
            cp.wait_send()
        for cp in local:
            cp.wait()

    return pl.kernel(
        body, name=name,
        out_type=[_sds((N_DEV,) + (s.shape[1:] if f else s.shape), s.dtype) for s, f in zip(sources, flags)],
        mesh=plsc.ScalarSubcoreMesh(axis_name="sequencer", num_cores=1),
        scratch_types=[pltpu.SemaphoreType.DMA((_PEERS * n,)), pltpu.SemaphoreType.DMA((_PEERS * n,)),
                       pltpu.SemaphoreType.DMA((n,))],
        compiler_params=pltpu.CompilerParams(collective_id=collective_id),
    )(*sources)


def _sequencer_gather(shards, name, collective_id):
    n = len(shards)
    fan = 4

    def body(*refs):
        srcs, zones = refs[:n], refs[n:2 * n]
        send_sems, recv_sems, local_sems = refs[2 * n:]
        x, y, c = _mesh_pos()
        me, sibling = (x, y, c), (x, y, 1 - c)
        chips = [(1 - x, y), (x, 1 - y), (1 - x, 1 - y)]
        barrier = pltpu.get_barrier_semaphore()
        for peer in [sibling] + [(*chip, c) for chip in chips]:
            pl.semaphore_signal(barrier, inc=1, device_id=peer, device_id_type=_MESH)
        pl.semaphore_wait(barrier, fan)

        def copy(w, k, block, to, src=None):
            slot = zones[w].at[_dev_index(*block)]
            return pltpu.make_async_remote_copy(
                src_ref=slot if src is None else src, dst_ref=slot,
                send_sem=send_sems.at[_PEERS * w + k], recv_sem=recv_sems.at[_PEERS * w + k],
                device_id=to, device_id_type=_MESH)

        mine, first, passed = [], [], []
        for w in range(n):
            cp = pltpu.make_async_copy(srcs[w], zones[w].at[_dev_index(*me)], local_sems.at[w])
            cp.start()
            mine.append(cp)
            sends = [copy(w, 0, me, sibling, src=srcs[w])]
            sends += [copy(w, 1 + j, me, (*chip, c), src=srcs[w]) for j, chip in enumerate(chips)]
            for cp in sends:
                cp.start()
            first += sends
        for w in range(n):
            for j, chip in enumerate(chips):
                copy(w, 1 + j, (*chip, c), me).wait_recv()
                cp = copy(w, fan + j, (*chip, c), sibling)
                cp.start()
                passed.append(cp)
        for w in range(n):
            copy(w, 0, sibling, me).wait_recv()
            for j, chip in enumerate(chips):
                copy(w, fan + j, (*chip, 1 - c), me).wait_recv()
        for cp in first + passed:
            cp.wait_send()
        for cp in mine:
            cp.wait()

    return pl.kernel(
        body, name=name, out_type=[_sds((N_DEV,) + s.shape, s.dtype) for s in shards],
        mesh=plsc.ScalarSubcoreMesh(axis_name="sequencer", num_cores=1),
        scratch_types=[pltpu.SemaphoreType.DMA((_PEERS * n,)), pltpu.SemaphoreType.DMA((_PEERS * n,)),
                       pltpu.SemaphoreType.DMA((n,))],
        compiler_params=pltpu.CompilerParams(collective_id=collective_id),
    )(*shards)


N_CHIPS = N_DEV // 2


def _sequencer_pair_exchange(sources, name, collective_id):
    n = len(sources)

    def body(*refs):
        srcs, zones = refs[:n], refs[n:2 * n]
        send_sems, recv_sems = refs[2 * n:]
        x, y, c = _mesh_pos()
        sibling = (x, y, 1 - c)
        barrier = pltpu.get_barrier_semaphore()
        pl.semaphore_signal(barrier, inc=1, device_id=sibling, device_id_type=_MESH)
        pl.semaphore_wait(barrier, 1)
        copies = []
        for w in range(n):
            for j in range(N_CHIPS):
                k = N_CHIPS * w + j
                cp = pltpu.make_async_remote_copy(
                    src_ref=srcs[w].at[2 * j + 1 - c], dst_ref=zones[w].at[j],
                    send_sem=send_sems.at[k], recv_sem=recv_sems.at[k], device_id=sibling, device_id_type=_MESH)
                cp.start()
                copies.append(cp)
        for cp in copies:
            cp.wait_recv()
        for cp in copies:
            cp.wait_send()

    return pl.kernel(
        body, name=name, out_type=[_sds((N_CHIPS,) + s.shape[1:], s.dtype) for s in sources],
        mesh=plsc.ScalarSubcoreMesh(axis_name="sequencer", num_cores=1),
        scratch_types=[pltpu.SemaphoreType.DMA((N_CHIPS * n,)), pltpu.SemaphoreType.DMA((N_CHIPS * n,))],
        compiler_params=pltpu.CompilerParams(collective_id=collective_id),
    )(*sources)


def _pair_sum(source, received, core, name, tokens=()):
    _, rows, cols = source.shape
    tr = _row_tile(rows)
    n_tok = len(tokens)

    def body(core_ref, s_ref, r_ref, *rest):
        o_ref = rest[n_tok]
        o_ref[...] = (s_ref[...].astype(_F32) + r_ref[...].astype(_F32)).astype(o_ref.dtype)

    quarter = pl.BlockSpec((N_CHIPS, tr, cols), lambda i, core_ref: (0, i, 0))
    mine = pl.BlockSpec((N_CHIPS, None, tr, cols), lambda i, core_ref: (0, core_ref[0], i, 0))
    spec = pltpu.PrefetchScalarGridSpec(
        num_scalar_prefetch=1, grid=(rows // tr,),
        in_specs=[mine, quarter] + [pl.BlockSpec(memory_space=pl.ANY)] * n_tok, out_specs=quarter)
    params = pltpu.CompilerParams(dimension_semantics=("arbitrary",), vmem_limit_bytes=VMEM_LIMIT)
    return pl.pallas_call(body, grid_spec=spec, out_shape=_sds((N_CHIPS, rows, cols), source.dtype),
                          compiler_params=params, name=name)(
        core, source.reshape(N_CHIPS, 2, rows, cols), received, *tokens)


def _sequencer_chip_exchange(partials, name, collective_id):
    n = len(partials)
    others = N_CHIPS - 1

    def body(*refs):
        srcs, zones = refs[:n], refs[n:2 * n]
        send_sems, recv_sems, local_sems = refs[2 * n:]
        x, y, c = _mesh_pos()
        mine = 2 * x + y
        peers = [(x ^ (r >> 1), y ^ (r & 1), c) for r in range(1, N_CHIPS)]
        barrier = pltpu.get_barrier_semaphore()
        for peer in peers:
            pl.semaphore_signal(barrier, inc=1, device_id=peer, device_id_type=_MESH)
        pl.semaphore_wait(barrier, others)
        local, sends, recvs = [], [], []
        for w in range(n):
            cp = pltpu.make_async_copy(srcs[w].at[mine], zones[w].at[mine], local_sems.at[w])
            cp.start()
            local.append(cp)
            for r, peer in enumerate(peers):
                theirs = 2 * peer[0] + peer[1]
                k = others * w + r
                send = pltpu.make_async_remote_copy(
                    src_ref=srcs[w].at[theirs], dst_ref=zones[w].at[mine],
                    send_sem=send_sems.at[k], recv_sem=recv_sems.at[k], device_id=peer, device_id_type=_MESH)
                send.start()
                sends.append(send)
                recvs.append(pltpu.make_async_remote_copy(
                    src_ref=srcs[w].at[theirs], dst_ref=zones[w].at[theirs],
                    send_sem=send_sems.at[k], recv_sem=recv_sems.at[k], device_id=peer, device_id_type=_MESH))
        for cp in recvs:
            cp.wait_recv()
        for cp in sends:
            cp.wait_send()
        for cp in local:
            cp.wait()

    return pl.kernel(
        body, name=name, out_type=[_sds(s.shape, s.dtype) for s in partials],
        mesh=plsc.ScalarSubcoreMesh(axis_name="sequencer", num_cores=1),
        scratch_types=[pltpu.SemaphoreType.DMA((others * n,)), pltpu.SemaphoreType.DMA((others * n,)),
                       pltpu.SemaphoreType.DMA((n,))],
        compiler_params=pltpu.CompilerParams(collective_id=collective_id),
    )(*partials)


def _row_tile(rows):
    return next(t for t in range(min(rows, 256), 0, -16) if rows % t == 0)


def _sum_parts(parts, name, tokens=()):
    _, rows, cols = parts.shape
    tr = _row_tile(rows)

    def body(p_ref, g_ref):
        g = p_ref[0].astype(_F32)
        for s in range(1, N_DEV):
            g = g + p_ref[s].astype(_F32)
        g_ref[...] = g

    return _call(body, (rows // tr,), [pl.BlockSpec((N_DEV, tr, cols), lambda i: (0, i, 0))],
                 _rows(tr, cols), _sds((rows, cols), _F32), name, tokens=tokens)(parts)


def _adam_update(g, w, m, v):
    new_m = ADAM_B1 * m + (1.0 - ADAM_B1) * g
    new_v = ADAM_B2 * v + (1.0 - ADAM_B2) * (g * g)
    m_hat = new_m / (1.0 - ADAM_B1 ** ADAM_STEP)
    v_hat = new_v / (1.0 - ADAM_B2 ** ADAM_STEP)
    return -ADAM_LR * (m_hat / (jnp.sqrt(v_hat) + ADAM_EPS) + ADAM_WD * w), new_m, new_v


def _adamw_small(parts, items, sums, name, tokens=()):
    n_p, n_i = len(parts), len(items)

    def body(*refs):
        p_refs, state, outs = refs[:n_p], refs[n_p:n_p + 3 * n_i], refs[n_p + 3 * n_i:]

        def total(part, rows, cols):
            shift = cols.start % _LANES
            window = slice(cols.start - shift, cols.start - shift + _LANES) if shift else cols
            n_rows = rows.stop - rows.start
            narrow = p_refs[part].dtype.itemsize < 4 and n_rows % _PACK_TILE
            tile = slice(rows.start, rows.start + _PACK_TILE) if narrow else rows
            g = p_refs[part][0, tile, window].astype(_F32)
            for s in range(1, N_DEV):
                g = g + p_refs[part][s, tile, window].astype(_F32)
            g = g[:n_rows] if narrow else g
            return pltpu.roll(g, _LANES - shift, 1)[:, :cols.stop - cols.start] if shift else g

        for i, (part, rows, cols, _, _, _) in enumerate(items):
            g = total(part, rows, cols)
            w_ref, m_ref, v_ref = state[3 * i:3 * i + 3]
            delta, new_m, new_v = _adam_update(g, w_ref[...], m_ref[...], v_ref[...])
            outs[4 * i][...] = g
            outs[4 * i + 1][...] = delta
            outs[4 * i + 2][...] = new_m
            outs[4 * i + 3][...] = new_v
        for j, (part, rows, cols) in enumerate(sums):
            outs[4 * n_i + j][...] = total(part, rows, cols)

    ins = list(parts) + [a for item in items for a in item[3:]]
    out_shapes = [item[3].shape for item in items for _ in range(4)]
    out_shapes += [(rows.stop - rows.start, cols.stop - cols.start) for _, rows, cols in sums]
    out = _call(body, (1,), [_whole(a.shape) for a in ins], [_whole(s) for s in out_shapes],
                [_sds(s, _F32) for s in out_shapes], name, tokens=tokens)(*ins)
    return [out[4 * i:4 * i + 4] for i in range(n_i)], out[4 * n_i:]


def _adamw(parts, w, m, v, name, tokens=()):
    rows, cols = w.shape
    tr = _row_tile(rows)
    n_parts = parts.shape[0]

    def body(p_ref, w_ref, m_ref, v_ref, g_ref, d_ref, nm_ref, nv_ref):
        g = p_ref[0].astype(_F32)
        for s in range(1, n_parts):
            g = g + p_ref[s].astype(_F32)
        new_m = ADAM_B1 * m_ref[...] + (1.0 - ADAM_B1) * g
        new_v = ADAM_B2 * v_ref[...] + (1.0 - ADAM_B2) * (g * g)
        m_hat = new_m / (1.0 - ADAM_B1 ** ADAM_STEP)
        v_hat = new_v / (1.0 - ADAM_B2 ** ADAM_STEP)
        g_ref[...] = g
        d_ref[...] = -ADAM_LR * (m_hat / (jnp.sqrt(v_hat) + ADAM_EPS) + ADAM_WD * w_ref[...])
        nm_ref[...] = new_m
        nv_ref[...] = new_v

    blk = _rows(tr, cols)
    return _call(body, (rows // tr,),
                 [pl.BlockSpec((n_parts, tr, cols), lambda i: (0, i, 0)), blk, blk, blk],
                 [blk] * 4, [_sds((rows, cols), _F32)] * 4, name, tokens=tokens)(parts, w, m, v)


_SMALL = ("g_pre_mix", "ssm_lambda_re", "ssm_lambda_im", "ssm_log_dt", "ssm_b_re", "ssm_b_im",
          "ssm_c_re", "ssm_c_im", "ssm_d", "b_glu", "attn_sinks", "g_ssm_out", "g_attn_out",
          "g_post_mix", "g_pre_ffn", "g_post_ffn")
_BIG = ("w_in", "w_glu", "w_out", "w_gate_up", "w_down")
_WEIGHTS = ("g_pre_mix", "w_in", "ssm_lambda_re", "ssm_lambda_im", "ssm_log_dt", "ssm_b_re", "ssm_b_im",
            "ssm_c_re", "ssm_c_im", "ssm_d", "w_glu", "b_glu", "attn_sinks", "g_ssm_out", "g_attn_out",
            "w_out", "g_post_mix", "g_pre_ffn", "w_gate_up", "w_down", "g_post_ffn")
_LANES = 128


_SHAPE_2D = {
    "g_pre_mix": (1, D_MODEL), "ssm_lambda_re": (SSM_GROUPS, SSM_STATE), "ssm_lambda_im": (SSM_GROUPS, SSM_STATE),
    "ssm_log_dt": (1, SSM_GROUPS), "ssm_b_re": (SSM_WIDTH, SSM_STATE), "ssm_b_im": (SSM_WIDTH, SSM_STATE),
    "ssm_c_re": (SSM_WIDTH, SSM_STATE), "ssm_c_im": (SSM_WIDTH, SSM_STATE), "ssm_d": (SSM_GROUPS, SSM_GROUP),
    "b_glu": (1, 2 * SSM_WIDTH), "attn_sinks": (1, N_Q_HEADS), "g_ssm_out": (1, SSM_WIDTH),
    "g_attn_out": (1, ATTN_WIDTH), "g_post_mix": (1, D_MODEL), "g_pre_ffn": (1, D_MODEL), "g_post_ffn": (1, D_MODEL)}
_ROW_WIDTH = {"g_pre_mix": D_MODEL, "b_glu": 2 * SSM_WIDTH, "attn_sinks": _LANES, "g_ssm_out": SSM_WIDTH,
              "g_attn_out": ATTN_WIDTH, "g_post_mix": D_MODEL, "g_pre_ffn": D_MODEL, "g_post_ffn": D_MODEL,
              "loss": _LANES}
_PER_GROUP_TRANSPOSED = ("ssm_b_re", "ssm_b_im")


def _to_2d(name, a):
    if name in _PER_GROUP_TRANSPOSED:
        a = a.reshape(SSM_GROUPS, SSM_STATE, SSM_GROUP).transpose(0, 2, 1)
    return a.reshape(_SHAPE_2D[name])


def _from_2d(name, a, shape):
    if name in _PER_GROUP_TRANSPOSED:
        a = a.reshape(SSM_GROUPS, SSM_GROUP, SSM_STATE).transpose(0, 2, 1)
    return a.reshape(shape)


def _row_slots(names):
    slots, row, col = {}, 0, 0
    for n in names:
        width = _ROW_WIDTH[n]
        if col + width > D_MODEL:
            row, col = row + 1, 0
        slots[n] = (row, col, width)
        col += width
    return slots


def _stack_rows(named, slots):
    n_rows = -(-(max(r for r, _, _ in slots.values()) + 1) // 8) * 8
    lines = []
    for r in range(n_rows):
        pieces = [named[n] for n, (row, _, _) in slots.items() if row == r]
        used = sum(p.shape[1] for p in pieces)
        if used < D_MODEL:
            pieces.append(jnp.zeros((1, D_MODEL - used), _F32))
        lines.append(jnp.concatenate(pieces, axis=1) if len(pieces) > 1 else pieces[0])
    return jnp.concatenate(lines, axis=0)


def kernel(x, positions, g_pre_mix, w_in, ssm_lambda_re, ssm_lambda_im, ssm_log_dt, ssm_b_re, ssm_b_im, ssm_c_re, ssm_c_im, ssm_d, w_glu, b_glu, attn_sinks, g_ssm_out, g_attn_out, w_out, g_post_mix, g_pre_ffn, w_gate_up, w_down, g_post_ffn, loss_target, m_g_pre_mix, m_w_in, m_ssm_lambda_re, m_ssm_lambda_im, m_ssm_log_dt, m_ssm_b_re, m_ssm_b_im, m_ssm_c_re, m_ssm_c_im, m_ssm_d, m_w_glu, m_b_glu, m_attn_sinks, m_g_ssm_out, m_g_attn_out, m_w_out, m_g_post_mix, m_g_pre_ffn, m_w_gate_up, m_w_down, m_g_post_ffn, v_g_pre_mix, v_w_in, v_ssm_lambda_re, v_ssm_lambda_im, v_ssm_log_dt, v_ssm_b_re, v_ssm_b_im, v_ssm_c_re, v_ssm_c_im, v_ssm_d, v_w_glu, v_b_glu, v_attn_sinks, v_g_ssm_out, v_g_attn_out, v_w_out, v_g_post_mix, v_g_pre_ffn, v_w_gate_up, v_w_down, v_g_post_ffn):
    w = dict(g_pre_mix=g_pre_mix, w_in=w_in, ssm_lambda_re=ssm_lambda_re, ssm_lambda_im=ssm_lambda_im,
             ssm_log_dt=ssm_log_dt, ssm_b_re=ssm_b_re, ssm_b_im=ssm_b_im, ssm_c_re=ssm_c_re, ssm_c_im=ssm_c_im,
             ssm_d=ssm_d, w_glu=w_glu, b_glu=b_glu, attn_sinks=attn_sinks, g_ssm_out=g_ssm_out,
             g_attn_out=g_attn_out, w_out=w_out, g_post_mix=g_post_mix, g_pre_ffn=g_pre_ffn,
             w_gate_up=w_gate_up, w_down=w_down, g_post_ffn=g_post_ffn)
    m = dict(g_pre_mix=m_g_pre_mix, w_in=m_w_in, ssm_lambda_re=m_ssm_lambda_re, ssm_lambda_im=m_ssm_lambda_im,
             ssm_log_dt=m_ssm_log_dt, ssm_b_re=m_ssm_b_re, ssm_b_im=m_ssm_b_im, ssm_c_re=m_ssm_c_re,
             ssm_c_im=m_ssm_c_im, ssm_d=m_ssm_d, w_glu=m_w_glu, b_glu=m_b_glu, attn_sinks=m_attn_sinks,
             g_ssm_out=m_g_ssm_out, g_attn_out=m_g_attn_out, w_out=m_w_out, g_post_mix=m_g_post_mix,
             g_pre_ffn=m_g_pre_ffn, w_gate_up=m_w_gate_up, w_down=m_w_down, g_post_ffn=m_g_post_ffn)
    v = dict(g_pre_mix=v_g_pre_mix, w_in=v_w_in, ssm_lambda_re=v_ssm_lambda_re, ssm_lambda_im=v_ssm_lambda_im,
             ssm_log_dt=v_ssm_log_dt, ssm_b_re=v_ssm_b_re, ssm_b_im=v_ssm_b_im, ssm_c_re=v_ssm_c_re,
             ssm_c_im=v_ssm_c_im, ssm_d=v_ssm_d, w_glu=v_w_glu, b_glu=v_b_glu, attn_sinks=v_attn_sinks,
             g_ssm_out=v_g_ssm_out, g_attn_out=v_g_attn_out, w_out=v_w_out, g_post_mix=v_g_post_mix,
             g_pre_ffn=v_g_pre_ffn, w_gate_up=v_w_gate_up, w_down=v_w_down, g_post_ffn=v_g_post_ffn)

    transposed = ("w_in", "w_glu", "w_gate_up")
    native_transposed = ("w_in", "w_gate_up")
    shard = {n: (w[n][0].T if n in transposed else w[n][0]).astype(_BF16) for n in _BIG}
    gathered = {}
    for cid, names in enumerate((("w_in",), ("w_glu", "w_out"), ("w_gate_up", "w_down")), start=1):
        lands = _sequencer_gather([shard[n] for n in names], "gather_" + names[0], cid)
        gathered.update({n: a.reshape(-1, a.shape[2]) for n, a in zip(names, lands)})

    def fetch(names, after):
        del after
        return [gathered[n] for n in names]

    sent = []
    ids = iter(range(4, 16))
    two_step = {}

    def publish(named):
        big = [n for n in named if n in _BIG]
        if set(big) == {"w_gate_up", "w_down"}:
            blocks = [named[n].reshape(N_DEV, -1, named[n].shape[1]) for n in big]
            two_step.update(names=big, blocks=blocks,
                            received=_sequencer_pair_exchange(blocks, "grads_pair", next(ids)))
            return [named[n] for n in big]
        rows = [n for n in named if n in _ROW_WIDTH]
        plain = [n for n in named if n not in big + rows]
        sources = [named[n].reshape(N_DEV, -1, named[n].shape[1]) for n in big]
        slots = _row_slots(rows)
        if rows:
            sources.append(_stack_rows(named, slots))
        sources += [named[n] for n in plain]
        flags = [True] * len(big) + [False] * (len(sources) - len(big))
        cid = next(ids)
        sent.append((big, slots, plain, _sequencer_exchange(sources, flags, "grads_%d" % cid, cid)))
        return [named[n] for n in big]

    def progress(after):
        core = lax.axis_index("c").astype(jnp.int32).reshape(1)
        partials = [_pair_sum(b, r, core, "pair_sum_" + n, [after])
                    for n, b, r in zip(two_step["names"], two_step["blocks"], two_step["received"])]
        sent.append((two_step["names"], {}, [], _sequencer_chip_exchange(partials, "grads_chips", next(ids))))
        return partials

    p = {n: w[n] for n in _SMALL}
    grad_x = _local_step(x[0], positions[0], loss_target[0], p, fetch, publish, progress)

    state = {n: [_to_2d(n, a) for a in (w[n], m[n], v[n])] for n in _SMALL}
    result = {}
    total_loss = None
    chain = []
    for big, slots, plain, lands in sent:
        lands = list(lands)
        after = list(chain)
        for name in big:
            part = lands.pop(0)
            if name in native_transposed:
                updated = _adamw(part, w[name][0].T, m[name][0].T, v[name][0].T, "adamw_" + name, after)
                result[name] = [a.T[None] for a in updated]
                chain.append(updated[3])
                continue
            if name in transposed:
                part = _sum_parts(part, "sum_" + name, after).T[None]
            updated = _adamw(part, w[name][0], m[name][0], v[name][0], "adamw_" + name, after)
            result[name] = [a[None] for a in updated]
            chain.append(updated[3])
        parts, items, sums, names = [], [], [], []
        if slots:
            parts.append(lands.pop(0))
            for name, (row, col, _) in slots.items():
                if name == "loss":
                    sums.append((0, slice(row, row + 1), slice(col, col + _LANES)))
                else:
                    items.append((0, slice(row, row + 1), slice(col, col + _SHAPE_2D[name][1]), *state[name]))
                    names.append(name)
        for name in plain:
            packed = _SSM_PACK if name == "ssm_pack" else {name: (0, _SHAPE_2D[name][0], 0, _SHAPE_2D[name][1])}
            for member, (first, rows_n, lane, cols_n) in packed.items():
                items.append((len(parts), slice(first, first + rows_n), slice(lane, lane + cols_n), *state[member]))
                names.append(member)
            parts.append(lands.pop(0))
        if items:
            updated, summed = _adamw_small(parts, items, sums, "adamw_small_" + names[0], after)
            chain.append(updated[0][3])
            result.update(dict(zip(names, updated)))
            if summed:
                total_loss = summed[0][0, 0]

    out = [total_loss, grad_x[None]]
    for kind in range(4):
        out += [_from_2d(n, result[n][kind], w[n].shape) for n in _WEIGHTS]
    return tuple(out)
```

```python
import math

import numpy as np
import jax
import jax.numpy as jnp
from jax import lax
from jax.experimental import pallas as pl
from jax.experimental.pallas import tpu as pltpu
from jax.experimental.pallas import tpu_sc as plsc

D_MODEL = 1024
SSM_WIDTH = 512
SSM_GROUP = 16
SSM_GROUPS = 32
SSM_STATE = 64
N_STATE = SSM_GROUPS * SSM_STATE
ATTN_WIDTH = 512
HEAD_DIM = 64
N_Q_HEADS = 8
N_KV_HEADS = 2
Q_PER_KV = 4
KV_WIDTH = 128
IN_WIDTH = 1280
BLOCK = 128
ROPE_DIM = 16
ROPE_THETA = 500000.0
D_FF = 2816
NORM_EPS = 1e-6
MASK_VALUE = -1e30
ADAM_LR = 0.001
ADAM_B1 = 0.9
ADAM_B2 = 0.999
ADAM_EPS = 1e-08
ADAM_WD = 0.01
ADAM_STEP = 10

N_DEV = 8
SCAN_CHUNKS = 8
SCAN_UNROLL = 8
FFN_CHUNK = 2816
TOKEN_TILE = 256
VMEM_LIMIT = 56 * 1024 * 1024

_F32 = jnp.float32
_BF16 = jnp.bfloat16
_MXU = jnp.bfloat16

_NN = ((1,), (0,))
_NT = ((1,), (1,))
_TN = ((0,), (0,))


def _dot(a, b, dims):
    return lax.dot_general(a.astype(_MXU), b.astype(_MXU), (dims, ((), ())),
                           preferred_element_type=_F32)


def _dot_exact(a, b, dims):
    return lax.dot_general(a.astype(_F32), b.astype(_F32), (dims, ((), ())),
                           precision=lax.Precision.HIGHEST, preferred_element_type=_F32)


def _iota(shape, dim):
    return lax.broadcasted_iota(jnp.int32, shape, dim)


def _rms_fwd(x, g):
    r = lax.rsqrt(jnp.mean(x * x, axis=-1, keepdims=True) + NORM_EPS)
    return x * r * g, r


def _rms_bwd(dy, x, g, r):
    a = dy * g
    xn = x * r
    dx = r * (a - xn * jnp.mean(a * xn, axis=-1, keepdims=True))
    dg = jnp.sum(dy * xn, axis=0, keepdims=True)
    return dx, dg


def _call(body, grid, in_specs, out_specs, out_shape, name, scratch=(), tokens=()):
    params = pltpu.CompilerParams(dimension_semantics=("arbitrary",) * len(grid),
                                  vmem_limit_bytes=VMEM_LIMIT)
    n_in, n_tok = len(in_specs), len(tokens)

    def run(*refs):
        return body(*refs[:n_in], *refs[n_in + n_tok:])

    call = pl.pallas_call(run, grid=grid,
                          in_specs=list(in_specs) + [pl.BlockSpec(memory_space=pl.ANY)] * n_tok,
                          out_specs=out_specs, out_shape=out_shape, scratch_shapes=list(scratch),
                          compiler_params=params, name=name)
    return lambda *args: call(*args, *tokens)


def _rows(tm, n):
    return pl.BlockSpec((tm, n), lambda i: (i, 0))


def _whole(shape):
    nd = len(shape)
    return pl.BlockSpec(shape, lambda i: (0,) * nd)


def _sds(shape, dtype):
    return jax.ShapeDtypeStruct(shape, dtype)


def _tile(L):
    return min(TOKEN_TILE, L)


def _chunk_tile(L):
    return L // SCAN_CHUNKS


def _chunk_block(L, n):
    return pl.BlockSpec((_chunk_tile(L), n), lambda i: (0, i))


def _chunk_shape(L, n):
    return (_chunk_tile(L), SCAN_CHUNKS * n)


def _accumulate(ref, val, first):
    @pl.when(first)
    def _():
        ref[...] = val

    @pl.when(jnp.logical_not(first))
    def _():
        ref[...] += val


def _rope_rows():
    half = ROPE_DIM // 2
    inv = (np.float32(ROPE_THETA) ** (-np.arange(half, dtype=np.float32) * np.float32(2.0) / np.float32(ROPE_DIM))).astype(np.float32)
    col = np.arange(KV_WIDTH) % HEAD_DIM
    freq = np.where(col < ROPE_DIM, inv[col % half], 0.0).astype(np.float32)
    sign = np.where(col < half, -1.0, np.where(col < ROPE_DIM, 1.0, 0.0)).astype(np.float32)
    return freq[None, :], sign[None, :]


def _rope_tables(pos_col):
    L = pos_col.shape[0]
    tm = _tile(L)
    freq, sign = _rope_rows()

    def body(pos_ref, freq_ref, sign_ref, cos_ref, sin_ref):
        ang = pos_ref[...].astype(_F32) * freq_ref[...]
        cos_ref[...] = jnp.cos(ang)
        sin_ref[...] = jnp.sin(ang) * sign_ref[...]

    return _call(body, (L // tm,),
                 [_rows(tm, 1), _whole((1, KV_WIDTH)), _whole((1, KV_WIDTH))],
                 [_rows(tm, KV_WIDTH), _rows(tm, KV_WIDTH)],
                 [_sds((L, KV_WIDTH), _F32)] * 2, "rope_tables")(pos_col, jnp.asarray(freq), jnp.asarray(sign))


def _widen(t, width):
    return t if width == KV_WIDTH else jnp.concatenate([t] * (width // KV_WIDTH), axis=1)


def _rope_partner(t):
    w = t.shape[1]
    in_head = _iota((1, w), 1) & (HEAD_DIM - 1)
    second = jnp.where(in_head < ROPE_DIM, pltpu.roll(t, ROPE_DIM // 2, 1), 0.0)
    return jnp.where(in_head < ROPE_DIM // 2, pltpu.roll(t, w - ROPE_DIM // 2, 1), second)


def _rope_apply(t, cos_t, sin_t):
    w = t.shape[1]
    return t * _widen(cos_t, w) + _rope_partner(t) * _widen(sin_t, w)


def _rope_transpose(dt, cos_t, sin_t):
    w = dt.shape[1]
    return dt * _widen(cos_t, w) + _rope_partner(dt * _widen(sin_t, w))


def _in_proj(x, g_pre_mix, w_in, cos_t, sin_t):
    L = x.shape[0]
    tm = _chunk_tile(L)

    def body(x_ref, g_ref, w_ref, cos_ref, sin_ref, hn_ref, u_ref, q_ref, k_ref, v_ref):
        hn, _ = _rms_fwd(x_ref[...], g_ref[...])
        hn = hn.astype(_BF16)
        hn_ref[...] = hn
        proj = _dot(hn, w_ref[...], _NT)
        u_ref[...] = proj[:, :SSM_WIDTH]
        q = proj[:, SSM_WIDTH:SSM_WIDTH + ATTN_WIDTH]
        k = proj[:, SSM_WIDTH + ATTN_WIDTH:SSM_WIDTH + ATTN_WIDTH + KV_WIDTH]
        cos_v, sin_v = cos_ref[...], sin_ref[...]
        q_ref[...] = _rope_apply(q, cos_v, sin_v).astype(_BF16)
        k_ref[...] = _rope_apply(k, cos_v, sin_v).astype(_BF16)
        v_ref[...] = proj[:, SSM_WIDTH + ATTN_WIDTH + KV_WIDTH:].astype(_BF16)

    return _call(body, (L // tm,),
                 [_rows(tm, D_MODEL), _whole((1, D_MODEL)), _whole((IN_WIDTH, D_MODEL)),
                  _rows(tm, KV_WIDTH), _rows(tm, KV_WIDTH)],
                 [_rows(tm, D_MODEL), _chunk_block(L, SSM_WIDTH), _rows(tm, ATTN_WIDTH),
                  _rows(tm, KV_WIDTH), _rows(tm, KV_WIDTH)],
                 [_sds((L, D_MODEL), _BF16), _sds(_chunk_shape(L, SSM_WIDTH), _F32), _sds((L, ATTN_WIDTH), _BF16),
                  _sds((L, KV_WIDTH), _BF16), _sds((L, KV_WIDTH), _BF16)],
                 "in_proj")(x, g_pre_mix, w_in, cos_t, sin_t)


def _s5_discretize(lam_re, lam_im, log_dt):
    lr = jnp.minimum(lam_re, -1e-4)
    li = lam_im
    dt = jnp.exp(log_dt)
    mag = jnp.exp(lr * dt)
    ar = mag * jnp.cos(li * dt)
    ai = mag * jnp.sin(li * dt)
    den = lr * lr + li * li
    fr = ((ar - 1.0) * lr + ai * li) / den
    fi = (ai * lr - (ar - 1.0) * li) / den
    return ar, ai, fr, fi


SUPER = 4
SB_STATE = N_STATE // SUPER
SB_WIDTH = SSM_WIDTH // SUPER


def _sb_state(k):
    return slice(SB_STATE * k, SB_STATE * (k + 1))


def _sb_width(k):
    return slice(SB_WIDTH * k, SB_WIDTH * (k + 1))


def _dt_column(log_dt_row):
    eye = _iota((SSM_GROUPS, SSM_GROUPS), 0) == _iota((SSM_GROUPS, SSM_GROUPS), 1)
    return jnp.sum(jnp.where(eye, log_dt_row, 0.0), axis=1, keepdims=True)


def _group_masks():
    e64 = ((_iota((SSM_STATE, N_STATE), 1) & (SSM_STATE - 1)) == _iota((SSM_STATE, N_STATE), 0)).astype(_F32)
    own = _iota((SSM_GROUPS, N_STATE), 0) == (_iota((SSM_GROUPS, N_STATE), 1) >> 6)
    return e64, own


def _rows_of_group():
    return ((_iota((SSM_WIDTH, SSM_GROUPS), 0) >> 4) == _iota((SSM_WIDTH, SSM_GROUPS), 1)).astype(_F32)


def _ssm_prep(lam_re, lam_im, log_dt, b_re, b_im, c_re, c_im):
    def body(lr_ref, li_ref, ld_ref, bre, bim, cre, cim, ar_ref, ai_ref, btr, bti, ctr, cti):
        ar, ai, fr, fi = _s5_discretize(lr_ref[...], li_ref[...], _dt_column(ld_ref[...]))
        e64, own = _group_masks()
        mask_c = (_iota((SSM_WIDTH, N_STATE), 0) >> 4) == (_iota((SSM_WIDTH, N_STATE), 1) >> 6)

        def to_row(t):
            return jnp.sum(jnp.where(own, _dot_exact(t, e64, _NN), 0.0), axis=0, keepdims=True)

        def fold(m):
            full = jnp.where(mask_c, _dot(m, e64, _NN), 0.0)
            return sum(full[_sb_width(k), :] for k in range(SUPER)).astype(_BF16)

        ar_ref[...] = to_row(ar)
        ai_ref[...] = to_row(ai)
        spread = _rows_of_group()
        fr_t = _dot_exact(spread, fr, _NN)
        fi_t = _dot_exact(spread, fi, _NN)
        btr[...] = fold(fr_t * bre[...] - fi_t * bim[...])
        bti[...] = fold(fr_t * bim[...] + fi_t * bre[...])
        ctr[...] = fold(cre[...])
        cti[...] = fold(cim[...])

    row = (1, N_STATE)
    ins = [lam_re, lam_im, log_dt, b_re, b_im, c_re, c_im]
    return _call(body, (1,), [_whole(a.shape) for a in ins],
                 [_whole(row), _whole(row)] + [_whole((SB_WIDTH, N_STATE))] * 4,
                 [_sds(row, _F32), _sds(row, _F32)] + [_sds((SB_WIDTH, N_STATE), _BF16)] * 4,
                 "ssm_prep")(*ins)


def _complex_power(ar, ai, n):
    pr, pi = jnp.ones_like(ar), jnp.zeros_like(ai)
    while n:
        if n & 1:
            pr, pi = pr * ar - pi * ai, pr * ai + pi * ar
        ar, ai = ar * ar - ai * ai, 2.0 * ar * ai
        n >>= 1
    return pr, pi


def _chunk_carries(er, ei, pr, pi, reverse):
    rows = _iota(er.shape, 0)
    sr = jnp.zeros_like(pr)
    si = jnp.zeros_like(pi)
    out_r = jnp.zeros_like(er)
    out_i = jnp.zeros_like(ei)
    order = range(SCAN_CHUNKS - 1, 0, -1) if reverse else range(SCAN_CHUNKS - 1)
    for c in order:
        e_r = er[c:c + 1, :]
        e_i = ei[c:c + 1, :]
        sr, si = pr * sr - pi * si + e_r, pr * si + pi * sr + e_i
        nxt = c - 1 if reverse else c + 1
        out_r = jnp.where(rows == nxt, sr, out_r)
        out_i = jnp.where(rows == nxt, si, out_i)
    return out_r, out_i


_GELU_K = math.sqrt(2.0 / math.pi)
_GELU_C = 0.044715


def _gelu(y):
    return 0.5 * y * (1.0 + jnp.tanh(_GELU_K * (y + _GELU_C * y * y * y)))


def _gelu_grad(y):
    t = jnp.tanh(_GELU_K * (y + _GELU_C * y * y * y))
    return 0.5 * (1.0 + t) + 0.5 * y * (1.0 - t * t) * _GELU_K * (1.0 + 3.0 * _GELU_C * y * y)


def _step_rows(t):
    return pl.ds(pl.multiple_of(t * SCAN_CHUNKS, SCAN_CHUNKS), SCAN_CHUNKS)


def _scan_in_place(br, bi, ar, ai, T, carries=None):
    W = br.shape[1]
    ar8 = jnp.broadcast_to(ar, (SCAN_CHUNKS, W))
    ai8 = jnp.broadcast_to(ai, (SCAN_CHUNKS, W))

    def local(t, c):
        cr, ci = c
        rows = _step_rows(t)
        return ar8 * cr - ai8 * ci + br[rows, :], ar8 * ci + ai8 * cr + bi[rows, :]

    if carries is None:
        zero = jnp.zeros((SCAN_CHUNKS, W), _F32)
        er, ei = lax.fori_loop(0, T, local, (zero, zero), unroll=SCAN_UNROLL)
        pr, pi = _complex_power(ar, ai, T)
        carries = _chunk_carries(er, ei, pr, pi, reverse=False)

    def final(t, c):
        nr, ni = local(t, c)
        rows = _step_rows(t)
        br[rows, :] = nr
        bi[rows, :] = ni
        return nr, ni

    lax.fori_loop(0, T, final, carries, unroll=SCAN_UNROLL)
    return carries


def _scan_reverse_in_place(dr, di, xr, xi, ar, ai, T):
    W = dr.shape[1]
    ar8 = jnp.broadcast_to(ar, (SCAN_CHUNKS, W))
    ai8 = jnp.broadcast_to(ai, (SCAN_CHUNKS, W))

    def local(t, c):
        cr, ci = c
        rows = _step_rows(t)
        return ar8 * cr + ai8 * ci + dr[rows, :], ar8 * ci - ai8 * cr + di[rows, :]

    zero = jnp.zeros((SCAN_CHUNKS, W), _F32)
    er, ei = lax.fori_loop(0, T, lambda k, c: local(T - 1 - k, c), (zero, zero), unroll=SCAN_UNROLL)
    pr, pi = _complex_power(ar, -ai, T)
    sr, si = _chunk_carries(er, ei, pr, pi, reverse=True)

    def grad_a(acc, nr, ni, xpr, xpi):
        return acc[0] + nr * xpr + ni * xpi, acc[1] + ni * xpr - nr * xpi

    def final(k, c):
        t = T - 1 - k
        nr, ni = local(t, c[:2])
        rows = _step_rows(t)
        dr[rows, :] = nr
        di[rows, :] = ni
        before = _step_rows(t - 1)
        gr, gi = grad_a(c[2:], nr, ni, xr[before, :], xi[before, :])
        return nr, ni, gr, gi

    cr, ci, gr, gi = lax.fori_loop(0, T - 1, final, (sr, si, zero, zero), unroll=SCAN_UNROLL)
    nr, ni = local(0, (cr, ci))
    dr[_step_rows(0), :] = nr
    di[_step_rows(0), :] = ni
    first = _iota((SCAN_CHUNKS, W), 0) == 0
    last = _step_rows(T - 1)
    xpr = jnp.where(first, 0.0, pltpu.roll(xr[last, :], 1, 0))
    xpi = jnp.where(first, 0.0, pltpu.roll(xi[last, :], 1, 0))
    gr, gi = grad_a((gr, gi), nr, ni, xpr, xpi)
    return jnp.sum(gr, axis=0, keepdims=True), jnp.sum(gi, axis=0, keepdims=True)


def _ssm_super_specs(L):
    width = pl.BlockSpec((L, SB_WIDTH), lambda k: (0, k))
    matrix = pl.BlockSpec((SB_WIDTH, SB_STATE), lambda k: (0, k))
    row = pl.BlockSpec((1, SB_STATE), lambda k: (0, k))
    return width, matrix, row


def _ssm_states(u_ref, br_ref, bi_ref, ar_ref, ai_ref, xr, xi, T, carries=None):
    ub = u_ref[...].astype(_BF16)
    xr[...] = _dot(ub, br_ref[...], _NN)
    xi[...] = _dot(ub, bi_ref[...], _NN)
    return _scan_in_place(xr, xi, ar_ref[...], ai_ref[...], T, carries)


def _ssm_core_fwd(u, bt_re, bt_im, ct_re, ct_im, a_re, a_im):
    L = u.shape[0]
    T = L // SCAN_CHUNKS

    def body(u_ref, br_ref, bi_ref, cr_ref, ci_ref, ar_ref, ai_ref, y_ref, sr_ref, si_ref, xr, xi):
        sr_ref[...], si_ref[...] = _ssm_states(u_ref, br_ref, bi_ref, ar_ref, ai_ref, xr, xi, T)
        y_ref[...] = _dot(xr[...], cr_ref[...], _NT) - _dot(xi[...], ci_ref[...], _NT)

    width, matrix, row = _ssm_super_specs(L)
    carry = pl.BlockSpec((SCAN_CHUNKS, SB_STATE), lambda k: (0, k))
    return _call(body, (SUPER,), [width, matrix, matrix, matrix, matrix, row, row], [width, carry, carry],
                 [_sds((L, SSM_WIDTH), _F32)] + [_sds((SCAN_CHUNKS, N_STATE), _F32)] * 2, "ssm_core_fwd",
                 scratch=[pltpu.VMEM((L, SB_STATE), _F32)] * 2)(u, bt_re, bt_im, ct_re, ct_im, a_re, a_im)


def _ssm_core_bwd(u, dy, dud, carry_re, carry_im, bt_re, bt_im, ct_re, ct_im, a_re, a_im, tokens=()):
    L = u.shape[0]
    T = L // SCAN_CHUNKS

    def body(u_ref, dy_ref, dud_ref, sr_ref, si_ref, br_ref, bi_ref, cr_ref, ci_ref, ar_ref, ai_ref,
             du_ref, dcr_ref, dci_ref, dbr_ref, dbi_ref, dar_ref, dai_ref, xr, xi, lr, li):
        _ssm_states(u_ref, br_ref, bi_ref, ar_ref, ai_ref, xr, xi, T, (sr_ref[...], si_ref[...]))
        dyb = dy_ref[...]
        lr[...] = _dot(dyb, cr_ref[...], _NN)
        li[...] = -_dot(dyb, ci_ref[...], _NN)
        da_re, da_im = _scan_reverse_in_place(lr, li, xr, xi, ar_ref[...], ai_ref[...], T)
        dar_ref[...] = da_re
        dai_ref[...] = da_im
        du_ref[...] = _dot(lr[...], br_ref[...], _NT) + _dot(li[...], bi_ref[...], _NT) + dud_ref[...]
        ub = u_ref[...].astype(_BF16)
        dcr_ref[...] = _dot(dyb, xr[...], _TN)
        dci_ref[...] = _dot(dyb, xi[...], _TN)
        dbr_ref[...] = _dot(ub, lr[...], _TN)
        dbi_ref[...] = _dot(ub, li[...], _TN)

    width, matrix, row = _ssm_super_specs(L)
    carry = pl.BlockSpec((SCAN_CHUNKS, SB_STATE), lambda k: (0, k))
    return _call(body, (SUPER,), [width, width, width, carry, carry, matrix, matrix, matrix, matrix, row, row],
                 [width] + [matrix] * 4 + [row] * 2,
                 [_sds((L, SSM_WIDTH), _F32)] + [_sds((SB_WIDTH, N_STATE), _F32)] * 4 + [_sds((1, N_STATE), _F32)] * 2,
                 "ssm_core_bwd", scratch=[pltpu.VMEM((L, SB_STATE), _F32)] * 4,
                 tokens=tokens)(u, dy, dud, carry_re, carry_im, bt_re, bt_im, ct_re, ct_im, a_re, a_im)


def _ssm_out(cx, u, d_row, w_glu, b_glu, g_ssm):
    L = u.shape[0]
    tm = _tile(L)

    def body(cx_ref, u_ref, d_ref, w_ref, b_ref, g_ref, y_ref, z_ref, n_ref, stage):
        y = cx_ref[...] + d_ref[...] * u_ref[...]
        y_ref[...] = y
        z = _dot(_gelu(y), w_ref[...], _NT) + b_ref[...]
        z_ref[...] = z
        out = z[:, :SSM_WIDTH] * jax.nn.sigmoid(z[:, SSM_WIDTH:])
        n, _ = _rms_fwd(out, g_ref[...])
        for k in range(SSM_WIDTH // _LANES):
            stage[k] = n[:, _LANES * k:_LANES * (k + 1)]
            for c in range(SCAN_CHUNKS):
                rows = stage[k, pl.ds(c, tm // SCAN_CHUNKS, stride=SCAN_CHUNKS), :]
                lane = SSM_WIDTH * c + _LANES * k
                n_ref[:, lane:lane + _LANES] = rows.astype(_BF16)

    return _call(body, (L // tm,),
                 [_rows(tm, SSM_WIDTH), _rows(tm, SSM_WIDTH), _whole((1, SSM_WIDTH)),
                  _whole((2 * SSM_WIDTH, SSM_WIDTH)), _whole((1, 2 * SSM_WIDTH)), _whole((1, SSM_WIDTH))],
                 [_rows(tm, SSM_WIDTH), _rows(tm, 2 * SSM_WIDTH), _rows(tm // SCAN_CHUNKS, SCAN_CHUNKS * SSM_WIDTH)],
                 [_sds((L, SSM_WIDTH), _F32), _sds((L, 2 * SSM_WIDTH), _F32), _sds(_chunk_shape(L, SSM_WIDTH), _BF16)],
                 "ssm_out", scratch=[pltpu.VMEM((SSM_WIDTH // _LANES, tm, _LANES), _F32)])(
        cx, u, d_row, w_glu, b_glu, g_ssm)


def _ssm_out_bwd(dn, y, z, u, d_row, w_glu, g_ssm, tokens=()):
    L = u.shape[0]
    tm = _tile(L)

    def body(dn_ref, y_ref, z_ref, u_ref, d_ref, w_ref, g_ref,
             gy_ref, dz_ref, dy_ref, dud_ref, dg_ref, db_ref, dd_ref, stage):
        first = pl.program_id(0) == 0
        for k in range(SSM_WIDTH // _LANES):
            for c in range(SCAN_CHUNKS):
                lane = SSM_WIDTH * c + _LANES * k
                stage[k, pl.ds(c, tm // SCAN_CHUNKS, stride=SCAN_CHUNKS), :] = dn_ref[:, lane:lane + _LANES]
        dn = jnp.concatenate([stage[k] for k in range(SSM_WIDTH // _LANES)], axis=1)
        z = z_ref[...]
        z1, z2 = z[:, :SSM_WIDTH], z[:, SSM_WIDTH:]
        sig = jax.nn.sigmoid(z2)
        out = z1 * sig
        g = g_ref[...]
        _, r = _rms_fwd(out, g)
        dout, dg = _rms_bwd(dn, out, g, r)
        _accumulate(dg_ref, dg, first)
        dz = jnp.concatenate([dout * sig, dout * z1 * sig * (1.0 - sig)], axis=1)
        _accumulate(db_ref, jnp.sum(dz, axis=0, keepdims=True), first)
        dzb = dz.astype(_BF16)
        dz_ref[...] = dzb
        y = y_ref[...]
        gy_ref[...] = _gelu(y).astype(_BF16)
        dy = _dot(dzb, w_ref[...], _NN) * _gelu_grad(y)
        u = u_ref[...]
        _accumulate(dd_ref, jnp.sum(dy * u, axis=0, keepdims=True), first)
        dud_ref[...] = d_ref[...] * dy
        dy_ref[...] = dy.astype(_BF16)

    row = _whole((1, SSM_WIDTH))
    return _call(body, (L // tm,),
                 [_rows(tm // SCAN_CHUNKS, SCAN_CHUNKS * SSM_WIDTH), _rows(tm, SSM_WIDTH), _rows(tm, 2 * SSM_WIDTH),
                  _rows(tm, SSM_WIDTH), row, _whole((2 * SSM_WIDTH, SSM_WIDTH)), row],
                 [_rows(tm, SSM_WIDTH), _rows(tm, 2 * SSM_WIDTH), _rows(tm, SSM_WIDTH), _rows(tm, SSM_WIDTH),
                  row, _whole((1, 2 * SSM_WIDTH)), row],
                 [_sds((L, SSM_WIDTH), _BF16), _sds((L, 2 * SSM_WIDTH), _BF16), _sds((L, SSM_WIDTH), _BF16),
                  _sds((L, SSM_WIDTH), _F32),
                  _sds((1, SSM_WIDTH), _F32), _sds((1, 2 * SSM_WIDTH), _F32), _sds((1, SSM_WIDTH), _F32)],
                 "ssm_out_bwd", scratch=[pltpu.VMEM((SSM_WIDTH // _LANES, tm, _LANES), _F32)], tokens=tokens)(
        dn, y, z, u, d_row, w_glu, g_ssm)


_SSM_PACK = {"ssm_b_re": (0, SSM_WIDTH, 0, SSM_STATE), "ssm_c_re": (0, SSM_WIDTH, 64, SSM_STATE),
             "ssm_b_im": (512, SSM_WIDTH, 0, SSM_STATE), "ssm_c_im": (512, SSM_WIDTH, 64, SSM_STATE),
             "ssm_lambda_re": (1024, SSM_GROUPS, 0, SSM_STATE), "ssm_lambda_im": (1024, SSM_GROUPS, 64, SSM_STATE),
             "ssm_d": (1056, SSM_GROUPS, 0, SSM_GROUP), "ssm_log_dt": (1088, 1, 0, SSM_GROUPS)}
_PACK_TILE = 16
_SSM_PACK_ROWS = 1088 + _PACK_TILE


def _ssm_param_bwd(da_re, da_im, dbt_re, dbt_im, dct_re, dct_im, lam_re, lam_im, log_dt, b_re, b_im, g_d):
    def body(dar, dai, dbr, dbi, dcr, dci, lr_ref, li_ref, ld_ref, bre_ref, bim_ref, gd_ref, pack_ref):
        lane_in = _iota((SSM_STATE, _LANES), 0)
        lane_out = _iota((SSM_STATE, _LANES), 1)
        low = (lane_out == lane_in).astype(_F32)
        high = (lane_out == lane_in + SSM_STATE).astype(_F32)

        def side_by_side(a, b):
            return _dot_exact(a, low, _NN) + _dot_exact(b, high, _NN)

        tail = _SSM_PACK["ssm_d"][0]
        pack_ref[tail:, :] = jnp.zeros((_SSM_PACK_ROWS - tail, _LANES), _BF16)
        pack_ref[tail:tail + SSM_GROUPS, 0:SSM_GROUP] = gd_ref[...].astype(_BF16)
        own_c = (_iota((SB_WIDTH, SB_STATE), 0) >> 4) == (_iota((SB_WIDTH, SB_STATE), 1) >> 6)

        def unfold(ref):
            blocks = []
            for k in range(SUPER):
                t = jnp.where(own_c, ref[:, _sb_state(k)], 0.0)
                t = sum(t[:, 128 * i:128 * (i + 1)] for i in range(SB_STATE // 128))
                blocks.append((t + pltpu.roll(t, SSM_STATE, 1))[:, :SSM_STATE])
            return jnp.concatenate(blocks, axis=0)

        dbb_re, dbb_im = unfold(dbr), unfold(dbi)
        b_re, b_im = bre_ref[...], bim_ref[...]
        dt_col = _dt_column(ld_ref[...])
        (_, _, fr, fi), vjp = jax.vjp(_s5_discretize, lr_ref[...], li_ref[...], dt_col)
        spread = _rows_of_group()
        fr_t = _dot_exact(spread, fr, _NN)
        fi_t = _dot_exact(spread, fi, _NN)
        pack_ref[0:SSM_WIDTH, :] = side_by_side(fr_t * dbb_re + fi_t * dbb_im, unfold(dcr)).astype(_BF16)
        pack_ref[SSM_WIDTH:2 * SSM_WIDTH, :] = side_by_side(fr_t * dbb_im - fi_t * dbb_re, -unfold(dci)).astype(_BF16)
        d_fr = _dot_exact(spread, dbb_re * b_re + dbb_im * b_im, _TN)
        d_fi = _dot_exact(spread, dbb_im * b_re - dbb_re * b_im, _TN)
        e64, own = _group_masks()

        def from_row(ref):
            return _dot_exact(jnp.where(own, ref[...], 0.0), e64, _NT)

        d_lr, d_li, d_dt = vjp((from_row(dar), from_row(dai), d_fr, d_fi))
        lam_rows = _SSM_PACK["ssm_lambda_re"][0]
        pack_ref[lam_rows:lam_rows + SSM_GROUPS, :] = side_by_side(d_lr, d_li).astype(_BF16)
        eye = (_iota((SSM_GROUPS, SSM_GROUPS), 0) == _iota((SSM_GROUPS, SSM_GROUPS), 1)).astype(_F32)
        dt_row = _SSM_PACK["ssm_log_dt"][0]
        pack_ref[dt_row:dt_row + _PACK_TILE, 0:SSM_GROUPS] = _dot_exact(
            jnp.broadcast_to(d_dt, (SSM_GROUPS, 128)), eye, _TN)[0:_PACK_TILE].astype(_BF16)

    ins = [da_re, da_im, dbt_re, dbt_im, dct_re, dct_im, lam_re, lam_im, log_dt, b_re, b_im, g_d]
    out = (_SSM_PACK_ROWS, _LANES)
    return _call(body, (1,), [_whole(a.shape) for a in ins], _whole(out), _sds(out, _BF16), "ssm_param_bwd")(*ins)


def _head_spread(j):
    r = _iota((KV_WIDTH, 256), 0)
    c = _iota((KV_WIDTH, 256), 1)
    return (r == HEAD_DIM * j + (c & (HEAD_DIM - 1))).astype(_BF16)


STACK = Q_PER_KV * BLOCK


def _stack_heads(t):
    lane_head = _iota((1, 256), 1) >> 6
    return jnp.concatenate([jnp.where(lane_head == g, t, jnp.zeros_like(t)) for g in range(Q_PER_KV)], axis=0)


def _unstack_heads(t):
    lane_head = _iota((1, 256), 1) >> 6
    return sum(jnp.where(lane_head == g, t[BLOCK * g:BLOCK * (g + 1)], 0.0) for g in range(Q_PER_KV))


def _stacked_sinks(sink_ref, j):
    block = _iota((STACK, 1), 0) >> 7
    col = jnp.full((STACK, 1), sink_ref[Q_PER_KV * j], _F32)
    for g in range(1, Q_PER_KV):
        col = jnp.where(block == g, sink_ref[Q_PER_KV * j + g], col)
    return col


def _fold_heads(t, j):
    t = t[:, :KV_WIDTH] + t[:, KV_WIDTH:]
    t = t + pltpu.roll(t, HEAD_DIM, 1)
    return jnp.where((_iota((1, KV_WIDTH), 1) >> 6) == j, t, 0.0)


def _attn_scores(q_stacked, kt, blk, sink):
    s = _dot(q_stacked, kt, _NT) * (HEAD_DIM ** -0.5)
    qi = _iota((STACK, 2 * BLOCK), 0) & (BLOCK - 1)
    kj = _iota((STACK, 2 * BLOCK), 1)
    rel = qi + BLOCK - kj
    valid = (rel >= 0) & (rel < BLOCK) & (blk * BLOCK - BLOCK + kj >= 0)
    s = jnp.where(valid, s, MASK_VALUE)
    m = jnp.maximum(jnp.max(s, axis=-1, keepdims=True), sink)
    p = jnp.exp(s - m)
    e_sink = jnp.exp(sink - m)
    den = jnp.sum(p, axis=-1, keepdims=True) + e_sink
    return p / den, e_sink / den


def _sink_slot():
    return _iota((STACK, 2 * BLOCK), 1) == 0


def _prob_block():
    return pl.BlockSpec((None, N_KV_HEADS, STACK, 2 * BLOCK), lambda i: (i, 0, 0, 0))


def _attn_specs():
    prev = lambda i: (jnp.maximum(i - 1, 0), 0)
    cur = lambda i: (i, 0)
    kv = [pl.BlockSpec((BLOCK, KV_WIDTH), prev), pl.BlockSpec((BLOCK, KV_WIDTH), cur)]
    return [pl.BlockSpec((BLOCK, ATTN_WIDTH), cur)] + kv + kv


def _attn_fwd(q, k, v, sinks, g_attn):
    L = q.shape[0]

    def body(q_ref, kp_ref, kc_ref, vp_ref, vc_ref, sink_ref, g_ref, o_ref, n_ref, p_ref):
        blk = pl.program_id(0)
        kwin = jnp.concatenate([kp_ref[...], kc_ref[...]], axis=0)
        vwin = jnp.concatenate([vp_ref[...], vc_ref[...]], axis=0)
        halves = []
        for j in range(N_KV_HEADS):
            spread = _head_spread(j)
            kt = _dot(kwin, spread, _NN).astype(_BF16)
            vt = _dot(vwin, spread, _NN).astype(_BF16)
            qs = _stack_heads(q_ref[:, 256 * j:256 * (j + 1)])
            p, p_sink = _attn_scores(qs, kt, blk, _stacked_sinks(sink_ref, j))
            p_ref[j] = jnp.where(_sink_slot(), p_sink, p)
            halves.append(_unstack_heads(_dot(p, vt, _NN)))
        o = jnp.concatenate(halves, axis=1)
        o_ref[...] = o
        n, _ = _rms_fwd(o, g_ref[...])
        n_ref[...] = n.astype(_BF16)

    cur = lambda i: (i, 0)
    return _call(body, (L // BLOCK,),
                 _attn_specs() + [pl.BlockSpec(memory_space=pltpu.SMEM), _whole((1, ATTN_WIDTH))],
                 [pl.BlockSpec((BLOCK, ATTN_WIDTH), cur)] * 2 + [_prob_block()],
                 [_sds((L, ATTN_WIDTH), _F32), _sds((L, ATTN_WIDTH), _BF16),
                  _sds((L // BLOCK, N_KV_HEADS, STACK, 2 * BLOCK), _F32)],
                 "attn_fwd")(q, k, k, v, v, sinks, g_attn)


def _attn_bwd(q, k, v, o, dn, probs, g_attn):
    L = q.shape[0]

    def body(q_ref, kp_ref, kc_ref, vp_ref, vc_ref, o_ref, dn_ref, p_ref, g_ref,
             dq_ref, dk_ref, dv_ref, dsink_ref, dg_ref):
        blk = pl.program_id(0)
        first = blk == 0

        @pl.when(first)
        def _():
            dk_ref[...] = jnp.zeros_like(dk_ref)
            dv_ref[...] = jnp.zeros_like(dv_ref)
            dsink_ref[...] = jnp.zeros_like(dsink_ref)

        o = o_ref[...]
        g = g_ref[...]
        _, r = _rms_fwd(o, g)
        do, dg = _rms_bwd(dn_ref[...], o, g, r)
        _accumulate(dg_ref, dg, first)
        kwin = jnp.concatenate([kp_ref[...], kc_ref[...]], axis=0)
        vwin = jnp.concatenate([vp_ref[...], vc_ref[...]], axis=0)
        lane = _iota((1, 128), 1)
        dsink = jnp.zeros((1, 128), _F32)
        dkwin = jnp.zeros((2 * BLOCK, KV_WIDTH), _F32)
        dvwin = jnp.zeros((2 * BLOCK, KV_WIDTH), _F32)
        dq_halves = []
        for j in range(N_KV_HEADS):
            spread = _head_spread(j)
            kt = _dot(kwin, spread, _NN).astype(_BF16)
            vt = _dot(vwin, spread, _NN).astype(_BF16)
            qs = _stack_heads(q_ref[:, 256 * j:256 * (j + 1)])
            dos = _stack_heads(do[:, 256 * j:256 * (j + 1)]).astype(_BF16)
            saved = p_ref[j]
            p_sink = saved[:, 0:1]
            p = jnp.where(_sink_slot(), 0.0, saved)
            dp = _dot(dos, vt, _NT)
            delta = jnp.sum(p * dp, axis=-1, keepdims=True)
            ds = (p * (dp - delta) * (HEAD_DIM ** -0.5)).astype(_BF16)
            sink_term = p_sink * delta
            for g in range(Q_PER_KV):
                head_sum = jnp.sum(sink_term[BLOCK * g:BLOCK * (g + 1)], axis=0, keepdims=True)
                dsink = dsink - jnp.where(lane == Q_PER_KV * j + g, head_sum, 0.0)
            dvwin = dvwin + _fold_heads(_dot(p, dos, _TN), j)
            dkwin = dkwin + _fold_heads(_dot(ds, qs, _TN), j)
            dq_halves.append(_unstack_heads(_dot(ds, kt, _NN)))
        dq_ref[...] = jnp.concatenate(dq_halves, axis=1)
        dsink_ref[...] += dsink
        prev = pl.ds(pl.multiple_of(jnp.maximum(blk - 1, 0) * BLOCK, BLOCK), BLOCK)
        cur = pl.ds(pl.multiple_of(blk * BLOCK, BLOCK), BLOCK)
        dk_ref[prev, :] += dkwin[:BLOCK]
        dk_ref[cur, :] += dkwin[BLOCK:]
        dv_ref[prev, :] += dvwin[:BLOCK]
        dv_ref[cur, :] += dvwin[BLOCK:]

    cur = lambda i: (i, 0)
    blk_q = pl.BlockSpec((BLOCK, ATTN_WIDTH), cur)
    return _call(body, (L // BLOCK,),
                 _attn_specs() + [blk_q, blk_q, _prob_block(), _whole((1, ATTN_WIDTH))],
                 [blk_q, _whole((L, KV_WIDTH)), _whole((L, KV_WIDTH)), _whole((1, 128)), _whole((1, ATTN_WIDTH))],
                 [_sds((L, ATTN_WIDTH), _F32), _sds((L, KV_WIDTH), _F32), _sds((L, KV_WIDTH), _F32),
                  _sds((1, 128), _F32), _sds((1, ATTN_WIDTH), _F32)],
                 "attn_bwd")(q, k, k, v, v, o, dn, probs, g_attn)


def _out_proj(n_ssm, n_attn, x, w_out, g_post_mix, g_pre_ffn):
    L = x.shape[0]
    tm = _chunk_tile(L)

    def body(ns_ref, na_ref, x_ref, w_ref, g1_ref, g2_ref, merged_ref, mo_ref, h1_ref, hn2_ref):
        merged = jnp.concatenate([ns_ref[...], na_ref[...]], axis=1)
        merged_ref[...] = merged
        mo = _dot(merged, w_ref[...], _NN)
        mo_ref[...] = mo
        n, _ = _rms_fwd(mo, g1_ref[...])
        h1 = x_ref[...] + n
        h1_ref[...] = h1
        hn2, _ = _rms_fwd(h1, g2_ref[...])
        hn2_ref[...] = hn2.astype(_BF16)

    row = _whole((1, D_MODEL))
    return _call(body, (L // tm,),
                 [_chunk_block(L, SSM_WIDTH), _rows(tm, ATTN_WIDTH), _rows(tm, D_MODEL), _whole((D_MODEL, D_MODEL)),
                  row, row],
                 [_rows(tm, D_MODEL)] * 4,
                 [_sds((L, D_MODEL), _BF16), _sds((L, D_MODEL), _F32), _sds((L, D_MODEL), _F32), _sds((L, D_MODEL), _BF16)],
                 "out_proj")(n_ssm, n_attn, x, w_out, g_post_mix, g_pre_ffn)


def _ffn(hn2, h1, target, w_gate_up, w_down, g_pre_ffn, g_post_ffn):
    L = h1.shape[0]
    tm = _tile(L)
    half = FFN_CHUNK

    def body(hn2_ref, h1_ref, tgt_ref, wgu_hbm, wd_hbm, g2_ref, g3_ref,
             act_ref, dgu_ref, dff_ref, dh1_ref, loss_ref, dg3_ref, dg2_ref,
             wgu, wd, gu, sem):
        first = pl.program_id(0) == 0

        @pl.when(first)
        def _():
            c1 = pltpu.make_async_copy(wgu_hbm, wgu, sem.at[0])
            c2 = pltpu.make_async_copy(wd_hbm, wd, sem.at[1])
            c1.start()
            c2.start()
            c1.wait()
            c2.wait()

        hn2 = hn2_ref[...]
        ff = jnp.zeros((tm, D_MODEL), _F32)
        for c in range(D_FF // half):
            gate = _dot(hn2, wgu[half * c:half * (c + 1), :], _NT)
            up = _dot(hn2, wgu[D_FF + half * c:D_FF + half * (c + 1), :], _NT)
            gu[:, half * c:half * (c + 1)] = gate
            gu[:, D_FF + half * c:D_FF + half * (c + 1)] = up
            act = gate * jax.nn.sigmoid(gate) * up
            act_ref[half * c:half * (c + 1), :] = act.T.astype(_BF16)
            ff = ff + _dot(act, wd[half * c:half * (c + 1), :], _NN)
        g3 = g3_ref[...]
        n, r = _rms_fwd(ff, g3)
        h1 = h1_ref[...]
        err = h1 + n - tgt_ref[...]
        loss = 0.5 * jnp.sum(jnp.mean(err * err, axis=-1, keepdims=True), axis=0, keepdims=True)
        _accumulate(loss_ref, jnp.broadcast_to(loss, (1, 128)), first)
        dh2 = err * (1.0 / D_MODEL)
        dff, dg3 = _rms_bwd(dh2, ff, g3, r)
        _accumulate(dg3_ref, dg3, first)
        dffb = dff.astype(_BF16)
        dff_ref[...] = dffb
        dhn2 = jnp.zeros((tm, D_MODEL), _F32)
        for c in range(D_FF // half):
            dact = _dot(dffb, wd[half * c:half * (c + 1), :], _NT)
            gate = gu[:, half * c:half * (c + 1)]
            up = gu[:, D_FF + half * c:D_FF + half * (c + 1)]
            sig = jax.nn.sigmoid(gate)
            silu = gate * sig
            dgate = dact * up * (sig + silu * (1.0 - sig))
            dup = dact * silu
            dgu_ref[half * c:half * (c + 1), :] = dgate.T.astype(_BF16)
            dgu_ref[D_FF + half * c:D_FF + half * (c + 1), :] = dup.T.astype(_BF16)
            dhn2 = dhn2 + _dot(dgate, wgu[half * c:half * (c + 1), :], _NN)
            dhn2 = dhn2 + _dot(dup, wgu[D_FF + half * c:D_FF + half * (c + 1), :], _NN)
        g2 = g2_ref[...]
        _, r2 = _rms_fwd(h1, g2)
        dh1, dg2 = _rms_bwd(dhn2, h1, g2, r2)
        _accumulate(dg2_ref, dg2, first)
        dh1_ref[...] = dh2 + dh1

    row = _whole((1, D_MODEL))
    anyspace = pl.BlockSpec(memory_space=pl.ANY)
    return _call(body, (L // tm,),
                 [_rows(tm, D_MODEL), _rows(tm, D_MODEL), _rows(tm, D_MODEL), anyspace, anyspace, row, row],
                 [pl.BlockSpec((D_FF, tm), lambda i: (0, i)), pl.BlockSpec((2 * D_FF, tm), lambda i: (0, i)),
                  _rows(tm, D_MODEL), _rows(tm, D_MODEL), _whole((1, 128)), row, row],
                 [_sds((D_FF, L), _BF16), _sds((2 * D_FF, L), _BF16), _sds((L, D_MODEL), _BF16),
                  _sds((L, D_MODEL), _F32), _sds((1, 128), _F32), _sds((1, D_MODEL), _F32), _sds((1, D_MODEL), _F32)],
                 "ffn",
                 scratch=[pltpu.VMEM((2 * D_FF, D_MODEL), _BF16), pltpu.VMEM((D_FF, D_MODEL), _BF16),
                          pltpu.VMEM((tm, 2 * D_FF), _F32), pltpu.SemaphoreType.DMA((2,))],
                 )(hn2, h1, target, w_gate_up, w_down, g_pre_ffn, g_post_ffn)


def _out_proj_bwd(dh1, mo, w_out, g_post_mix, tokens=()):
    L = dh1.shape[0]
    tm = _chunk_tile(L)

    def body(dh1_ref, mo_ref, w_ref, g_ref, dmo_ref, dns_ref, dna_ref, dg_ref):
        first = pl.program_id(0) == 0
        mo = mo_ref[...]
        g = g_ref[...]
        _, r = _rms_fwd(mo, g)
        dmo, dg = _rms_bwd(dh1_ref[...], mo, g, r)
        _accumulate(dg_ref, dg, first)
        dmob = dmo.astype(_BF16)
        dmo_ref[...] = dmob
        dmerged = _dot(dmob, w_ref[...], _NT)
        dns_ref[...] = dmerged[:, :SSM_WIDTH]
        dna_ref[...] = dmerged[:, SSM_WIDTH:]

    row = _whole((1, D_MODEL))
    return _call(body, (L // tm,),
                 [_rows(tm, D_MODEL), _rows(tm, D_MODEL), _whole((D_MODEL, D_MODEL)), row],
                 [_rows(tm, D_MODEL), _chunk_block(L, SSM_WIDTH), _rows(tm, ATTN_WIDTH), row],
                 [_sds((L, D_MODEL), _BF16), _sds(_chunk_shape(L, SSM_WIDTH), _F32), _sds((L, ATTN_WIDTH), _F32),
                  _sds((1, D_MODEL), _F32)],
                 "out_proj_bwd", tokens=tokens)(dh1, mo, w_out, g_post_mix)


def _in_proj_bwd(du, dq, dk, dv, cos_t, sin_t, x, dh1, g_pre_mix, w_in, tokens=()):
    L = x.shape[0]
    tm = _chunk_tile(L)

    def body(du_ref, dq_ref, dk_ref, dv_ref, cos_ref, sin_ref, x_ref, dh1_ref, g_ref, w_ref,
             dproj_ref, dx_ref, dg_ref):
        first = pl.program_id(0) == 0
        cos_v, sin_v = cos_ref[...], sin_ref[...]
        dproj = jnp.concatenate([du_ref[...], _rope_transpose(dq_ref[...], cos_v, sin_v),
                                 _rope_transpose(dk_ref[...], cos_v, sin_v), dv_ref[...]], axis=1).astype(_BF16)
        dproj_ref[...] = dproj
        dhn = _dot(dproj, w_ref[...], _NN)
        x = x_ref[...]
        g = g_ref[...]
        _, r = _rms_fwd(x, g)
        dx, dg = _rms_bwd(dhn, x, g, r)
        _accumulate(dg_ref, dg, first)
        dx_ref[...] = dh1_ref[...] + dx

    row = _whole((1, D_MODEL))
    return _call(body, (L // tm,),
                 [_chunk_block(L, SSM_WIDTH), _rows(tm, ATTN_WIDTH), _rows(tm, KV_WIDTH), _rows(tm, KV_WIDTH),
                  _rows(tm, KV_WIDTH), _rows(tm, KV_WIDTH), _rows(tm, D_MODEL), _rows(tm, D_MODEL), row,
                  _whole((IN_WIDTH, D_MODEL))],
                 [_rows(tm, IN_WIDTH), _rows(tm, D_MODEL), row],
                 [_sds((L, IN_WIDTH), _BF16), _sds((L, D_MODEL), _F32), _sds((1, D_MODEL), _F32)],
                 "in_proj_bwd", tokens=tokens)(du, dq, dk, dv, cos_t, sin_t, x, dh1, g_pre_mix, w_in)


def _matmul_nn(a, b, out_dtype, name):
    M, K = a.shape
    N = b.shape[1]
    tm = next(t for t in (704, 512, 256, 128) if M % t == 0)
    tn = N if N <= D_MODEL else next(t for t in (512, 256, 128) if N % t == 0)

    def body(a_ref, b_ref, o_ref):
        o_ref[...] = _dot(a_ref[...], b_ref[...], _NN).astype(out_dtype)

    params = pltpu.CompilerParams(dimension_semantics=("arbitrary", "arbitrary"), vmem_limit_bytes=VMEM_LIMIT)
    return pl.pallas_call(body, grid=(M // tm, N // tn),
                          in_specs=[pl.BlockSpec((tm, K), lambda i, j: (i, 0)),
                                    pl.BlockSpec((K, tn), lambda i, j: (0, j))],
                          out_specs=pl.BlockSpec((tm, tn), lambda i, j: (i, j)),
                          out_shape=_sds((M, N), out_dtype), compiler_params=params, name=name)(a, b)


def _matmul_tn(a, b, out_dtype, name, scale=1.0):
    K, M = a.shape
    N = b.shape[1]
    tm = next(t for t in (512, 256, 128) if M % t == 0)
    tn = N if N <= D_MODEL else next(t for t in (512, 256, 128) if N % t == 0)

    def body(a_ref, b_ref, o_ref):
        acc = _dot(a_ref[...], b_ref[...], _TN)
        o_ref[...] = (acc if scale == 1.0 else acc * scale).astype(out_dtype)

    params = pltpu.CompilerParams(dimension_semantics=("arbitrary", "arbitrary"), vmem_limit_bytes=VMEM_LIMIT)
    return pl.pallas_call(body, grid=(M // tm, N // tn),
                          in_specs=[pl.BlockSpec((K, tm), lambda i, j: (0, i)),
                                    pl.BlockSpec((K, tn), lambda i, j: (0, j))],
                          out_specs=pl.BlockSpec((tm, tn), lambda i, j: (i, j)),
                          out_shape=_sds((M, N), out_dtype), compiler_params=params, name=name)(a, b)


def _local_step(x, pos, target, p, fetch, publish, progress):
    L = x.shape[0]
    T = L // SCAN_CHUNKS
    cos_t, sin_t = _rope_tables(pos.reshape(L, 1))
    w_in, = fetch(("w_in",), None)
    hn, u, q, k, v = _in_proj(x, p["g_pre_mix"], w_in, cos_t, sin_t)

    ssm = {n: _to_2d(n, p[n]) for n in ("ssm_lambda_re", "ssm_lambda_im", "ssm_log_dt", "ssm_b_re", "ssm_b_im",
                                        "ssm_c_re", "ssm_c_im")}
    d_row = p["ssm_d"].reshape(1, SSM_WIDTH)
    a_re, a_im, bt_re, bt_im, ct_re, ct_im = _ssm_prep(
        ssm["ssm_lambda_re"], ssm["ssm_lambda_im"], ssm["ssm_log_dt"], ssm["ssm_b_re"], ssm["ssm_b_im"],
        ssm["ssm_c_re"], ssm["ssm_c_im"])

    u_c = u.reshape(L, SSM_WIDTH)
    cx, carry_re, carry_im = _ssm_core_fwd(u_c, bt_re, bt_im, ct_re, ct_im, a_re, a_im)
    w_glu, = fetch(("w_glu",), cx)
    y, z, n_ssm = _ssm_out(cx, u_c, d_row, w_glu, p["b_glu"], p["g_ssm_out"])

    sinks = p["attn_sinks"].reshape(N_Q_HEADS)
    o, n_attn, probs = _attn_fwd(q, k, v, sinks, p["g_attn_out"])
    w_out, = fetch(("w_out",), n_attn)
    merged, mo, h1, hn2 = _out_proj(n_ssm, n_attn, x, w_out, p["g_post_mix"], p["g_pre_ffn"])
    w_gate_up, w_down = fetch(("w_gate_up", "w_down"), hn2)
    act_t, dgu_t, dff, dh1, loss, dg_post_ffn, dg_pre_ffn = _ffn(
        hn2, h1, target, w_gate_up, w_down, p["g_pre_ffn"], p["g_post_ffn"])
    grads = {"g_post_ffn": dg_post_ffn, "g_pre_ffn": dg_pre_ffn}
    tokens = publish({"w_down": _matmul_nn(act_t, dff, _BF16, "grad_w_down"),
                      "w_gate_up": _matmul_nn(dgu_t, hn2, _BF16, "grad_w_gate_up")})

    dmo, dn_ssm, dn_attn, grads["g_post_mix"] = _out_proj_bwd(dh1, mo, w_out, p["g_post_mix"], tokens)
    grad_w_out = _matmul_tn(merged, dmo, _BF16, "grad_w_out")

    dq, dk, dv, dsink, grads["g_attn_out"] = _attn_bwd(q, k, v, o, dn_attn, probs, p["g_attn_out"])
    grads["attn_sinks"] = dsink

    gy, dz, dy, dud, grads["g_ssm_out"], grads["b_glu"], dd = _ssm_out_bwd(
        dn_ssm, y, z, u_c, d_row, w_glu, p["g_ssm_out"], progress(dmo))
    tokens = publish({"w_out": grad_w_out, "w_glu": _matmul_tn(dz, gy, _BF16, "grad_w_glu")})
    du_c, dct_re, dct_im, dbt_re, dbt_im, da_re, da_im = _ssm_core_bwd(
        u_c, dy, dud, carry_re, carry_im, bt_re, bt_im, ct_re, ct_im, a_re, a_im, tokens)
    ssm_pack = _ssm_param_bwd(
        da_re, da_im, dbt_re, dbt_im, dct_re, dct_im,
        ssm["ssm_lambda_re"], ssm["ssm_lambda_im"], ssm["ssm_log_dt"], ssm["ssm_b_re"], ssm["ssm_b_im"],
        dd.reshape(SSM_GROUPS, SSM_GROUP))
    grads.update(ssm_pack=ssm_pack, loss=loss)
    publish(grads)

    du = du_c.reshape(_chunk_shape(L, SSM_WIDTH))
    dproj, grad_x, g_pre_mix = _in_proj_bwd(du, dq, dk, dv, cos_t, sin_t, x, dh1, p["g_pre_mix"], w_in, [ssm_pack])
    publish({"g_pre_mix": g_pre_mix, "w_in": _matmul_tn(dproj, hn, _BF16, "grad_w_in")})
    return grad_x


_MESH = pl.DeviceIdType.MESH
_PEERS = N_DEV - 1


def _mesh_pos():
    return lax.axis_index("x"), lax.axis_index("y"), lax.axis_index("c")


def _dev_index(px, py, pc):
    return 4 * px + 2 * py + pc


def _peer(x, y, c, r):
    return (x ^ ((r >> 2) & 1), y ^ ((r >> 1) & 1), c ^ (r & 1))


def _sequencer_exchange(sources, blocked, name, collective_id):
    n = len(sources)
    flags = blocked

    def body(*refs):
        srcs, zones = refs[:n], refs[n:2 * n]
        send_sems, recv_sems, local_sems = refs[2 * n:]
        x, y, c = _mesh_pos()
        me = _dev_index(x, y, c)
        barrier = pltpu.get_barrier_semaphore()
        for r in range(1, N_DEV):
            pl.semaphore_signal(barrier, inc=1, device_id=_peer(x, y, c, r), device_id_type=_MESH)
        pl.semaphore_wait(barrier, _PEERS)
        local, sends, recvs = [], [], []
        for w in range(n):
            cp = pltpu.make_async_copy(srcs[w].at[me] if flags[w] else srcs[w], zones[w].at[me], local_sems.at[w])
            cp.start()
            local.append(cp)
            for r in range(1, N_DEV):
                peer = _peer(x, y, c, r)
                idx = _dev_index(*peer)
                k = _PEERS * w + r - 1
                src = srcs[w].at[idx] if flags[w] else srcs[w]
                send = pltpu.make_async_remote_copy(
                    src_ref=src, dst_ref=zones[w].at[me], send_sem=send_sems.at[k], recv_sem=recv_sems.at[k],
                    device_id=peer, device_id_type=_MESH)
                send.start()
                sends.append(send)
                recvs.append(pltpu.make_async_remote_copy(
                    src_ref=src, dst_ref=zones[w].at[idx], send_sem=send_sems.at[k], recv_sem=recv_sems.at[k],
                    device_id=peer, device_id_type=_MESH))
        for cp in recvs:
            cp.wait_recv()
        for cp in sends:
            cp.wait_send()
        for cp in local:
            cp.wait()

    return pl.kernel(
        body, name=name,
        out_type=[_sds((N_DEV,) + (s.shape[1:] if f else s.shape), s.dtype) for s, f in zip(sources, flags)],
        mesh=plsc.ScalarSubcoreMesh(axis_name="sequencer", num_cores=1),
        scratch_types=[pltpu.SemaphoreType.DMA((_PEERS * n,)), pltpu.SemaphoreType.DMA((_PEERS * n,)),
                       pltpu.SemaphoreType.DMA((n,))],
        compiler_params=pltpu.CompilerParams(collective_id=collective_id),
    )(*sources)


def _sequencer_gather(shards, name, collective_id):
    n = len(shards)
    fan = 4

    def body(*refs):
        srcs, zones = refs[:n], refs[n:2 * n]
        send_sems, recv_sems, local_sems = refs[2 * n:]
        x, y, c = _mesh_pos()
        me, sibling = (x, y, c), (x, y, 1 - c)
        chips = [(1 - x, y), (x, 1 - y), (1 - x, 1 - y)]
        barrier = pltpu.get_barrier_semaphore()
        for peer in [sibling] + [(*chip, c) for chip in chips]:
            pl.semaphore_signal(barrier, inc=1, device_id=peer, device_id_type=_MESH)
        pl.semaphore_wait(barrier, fan)

        def copy(w, k, block, to, src=None):
            slot = zones[w].at[_dev_index(*block)]
            return pltpu.make_async_remote_copy(
                src_ref=slot if src is None else src, dst_ref=slot,
                send_sem=send_sems.at[_PEERS * w + k], recv_sem=recv_sems.at[_PEERS * w + k],
                device_id=to, device_id_type=_MESH)

        mine, first, passed = [], [], []
        for w in range(n):
            cp = pltpu.make_async_copy(srcs[w], zones[w].at[_dev_index(*me)], local_sems.at[w])
            cp.start()
            mine.append(cp)
            sends = [copy(w, 0, me, sibling, src=srcs[w])]
            sends += [copy(w, 1 + j, me, (*chip, c), src=srcs[w]) for j, chip in enumerate(chips)]
            for cp in sends:
                cp.start()
            first += sends
        for w in range(n):
            for j, chip in enumerate(chips):
                copy(w, 1 + j, (*chip, c), me).wait_recv()
                cp = copy(w, fan + j, (*chip, c), sibling)
                cp.start()
                passed.append(cp)
        for w in range(n):
            copy(w, 0, sibling, me).wait_recv()
            for j, chip in enumerate(chips):
                copy(w, fan + j, (*chip, 1 - c), me).wait_recv()
        for cp in first + passed:
            cp.wait_send()
        for cp in mine:
            cp.wait()

    return pl.kernel(
        body, name=name, out_type=[_sds((N_DEV,) + s.shape, s.dtype) for s in shards],
        mesh=plsc.ScalarSubcoreMesh(axis_name="sequencer", num_cores=1),
        scratch_types=[pltpu.SemaphoreType.DMA((_PEERS * n,)), pltpu.SemaphoreType.DMA((_PEERS * n,)),
                       pltpu.SemaphoreType.DMA((n,))],
        compiler_params=pltpu.CompilerParams(collective_id=collective_id),
    )(*shards)


N_CHIPS = N_DEV // 2


def _sequencer_pair_exchange(sources, name, collective_id):
    n = len(sources)

    def body(*refs):
        srcs, zones = refs[:n], refs[n:2 * n]
        send_sems, recv_sems = refs[2 * n:]
        x, y, c = _mesh_pos()
        sibling = (x, y, 1 - c)
        barrier = pltpu.get_barrier_semaphore()
        pl.semaphore_signal(barrier, inc=1, device_id=sibling, device_id_type=_MESH)
        pl.semaphore_wait(barrier, 1)
        copies = []
        for w in range(n):
            for j in range(N_CHIPS):
                k = N_CHIPS * w + j
                cp = pltpu.make_async_remote_copy(
                    src_ref=srcs[w].at[2 * j + 1 - c], dst_ref=zones[w].at[j],
                    send_sem=send_sems.at[k], recv_sem=recv_sems.at[k], device_id=sibling, device_id_type=_MESH)
                cp.start()
                copies.append(cp)
        for cp in copies:
            cp.wait_recv()
        for cp in copies:
            cp.wait_send()

    return pl.kernel(
        body, name=name, out_type=[_sds((N_CHIPS,) + s.shape[1:], s.dtype) for s in sources],
        mesh=plsc.ScalarSubcoreMesh(axis_name="sequencer", num_cores=1),
        scratch_types=[pltpu.SemaphoreType.DMA((N_CHIPS * n,)), pltpu.SemaphoreType.DMA((N_CHIPS * n,))],
        compiler_params=pltpu.CompilerParams(collective_id=collective_id),
    )(*sources)


def _pair_sum(source, received, core, name, tokens=()):
    _, rows, cols = source.shape
    tr = _row_tile(rows)
    n_tok = len(tokens)

    def body(core_ref, s_ref, r_ref, *rest):
        o_ref = rest[n_tok]
        o_ref[...] = (s_ref[...].astype(_F32) + r_ref[...].astype(_F32)).astype(o_ref.dtype)

    quarter = pl.BlockSpec((N_CHIPS, tr, cols), lambda i, core_ref: (0, i, 0))
    mine = pl.BlockSpec((N_CHIPS, None, tr, cols), lambda i, core_ref: (0, core_ref[0], i, 0))
    spec = pltpu.PrefetchScalarGridSpec(
        num_scalar_prefetch=1, grid=(rows // tr,),
        in_specs=[mine, quarter] + [pl.BlockSpec(memory_space=pl.ANY)] * n_tok, out_specs=quarter)
    params = pltpu.CompilerParams(dimension_semantics=("arbitrary",), vmem_limit_bytes=VMEM_LIMIT)
    return pl.pallas_call(body, grid_spec=spec, out_shape=_sds((N_CHIPS, rows, cols), source.dtype),
                          compiler_params=params, name=name)(
        core, source.reshape(N_CHIPS, 2, rows, cols), received, *tokens)


def _sequencer_chip_exchange(partials, name, collective_id):
    n = len(partials)
    others = N_CHIPS - 1

    def body(*refs):
        srcs, zones = refs[:n], refs[n:2 * n]
        send_sems, recv_sems, local_sems = refs[2 * n:]
        x, y, c = _mesh_pos()
        mine = 2 * x + y
        peers = [(x ^ (r >> 1), y ^ (r & 1), c) for r in range(1, N_CHIPS)]
        barrier = pltpu.get_barrier_semaphore()
        for peer in peers:
            pl.semaphore_signal(barrier, inc=1, device_id=peer, device_id_type=_MESH)
        pl.semaphore_wait(barrier, others)
        local, sends, recvs = [], [], []
        for w in range(n):
            cp = pltpu.make_async_copy(srcs[w].at[mine], zones[w].at[mine], local_sems.at[w])
            cp.start()
            local.append(cp)
            for r, peer in enumerate(peers):
                theirs = 2 * peer[0] + peer[1]
                k = others * w + r
                send = pltpu.make_async_remote_copy(
                    src_ref=srcs[w].at[theirs], dst_ref=zones[w].at[mine],
                    send_sem=send_sems.at[k], recv_sem=recv_sems.at[k], device_id=peer, device_id_type=_MESH)
                send.start()
                sends.append(send)
                recvs.append(pltpu.make_async_remote_copy(
                    src_ref=srcs[w].at[theirs], dst_ref=zones[w].at[theirs],
                    send_sem=send_sems.at[k], recv_sem=recv_sems.at[k], device_id=peer, device_id_type=_MESH))
        for cp in recvs:
            cp.wait_recv()
        for cp in sends:
            cp.wait_send()
        for cp in local:
            cp.wait()

    return pl.kernel(
        body, name=name, out_type=[_sds(s.shape, s.dtype) for s in partials],
        mesh=plsc.ScalarSubcoreMesh(axis_name="sequencer", num_cores=1),
        scratch_types=[pltpu.SemaphoreType.DMA((others * n,)), pltpu.SemaphoreType.DMA((others * n,)),
                       pltpu.SemaphoreType.DMA((n,))],
        compiler_params=pltpu.CompilerParams(collective_id=collective_id),
    )(*partials)


def _row_tile(rows):
    return next(t for t in range(min(rows, 256), 0, -16) if rows % t == 0)


def _sum_parts(parts, name, tokens=()):
    _, rows, cols = parts.shape
    tr = _row_tile(rows)

    def body(p_ref, g_ref):
        g = p_ref[0].astype(_F32)
        for s in range(1, N_DEV):
            g = g + p_ref[s].astype(_F32)
        g_ref[...] = g

    return _call(body, (rows // tr,), [pl.BlockSpec((N_DEV, tr, cols), lambda i: (0, i, 0))],
                 _rows(tr, cols), _sds((rows, cols), _F32), name, tokens=tokens)(parts)


def _adam_update(g, w, m, v):
    new_m = ADAM_B1 * m + (1.0 - ADAM_B1) * g
    new_v = ADAM_B2 * v + (1.0 - ADAM_B2) * (g * g)
    m_hat = new_m / (1.0 - ADAM_B1 ** ADAM_STEP)
    v_hat = new_v / (1.0 - ADAM_B2 ** ADAM_STEP)
    return -ADAM_LR * (m_hat / (jnp.sqrt(v_hat) + ADAM_EPS) + ADAM_WD * w), new_m, new_v


def _adamw_small(parts, items, sums, name, tokens=()):
    n_p, n_i = len(parts), len(items)

    def body(*refs):
        p_refs, state, outs = refs[:n_p], refs[n_p:n_p + 3 * n_i], refs[n_p + 3 * n_i:]

        def total(part, rows, cols):
            shift = cols.start % _LANES
            window = slice(cols.start - shift, cols.start - shift + _LANES) if shift else cols
            n_rows = rows.stop - rows.start
            narrow = p_refs[part].dtype.itemsize < 4 and n_rows % _PACK_TILE
            tile = slice(rows.start, rows.start + _PACK_TILE) if narrow else rows
            g = p_refs[part][0, tile, window].astype(_F32)
            for s in range(1, N_DEV):
                g = g + p_refs[part][s, tile, window].astype(_F32)
            g = g[:n_rows] if narrow else g
            return pltpu.roll(g, _LANES - shift, 1)[:, :cols.stop - cols.start] if shift else g

        for i, (part, rows, cols, _, _, _) in enumerate(items):
            g = total(part, rows, cols)
            w_ref, m_ref, v_ref = state[3 * i:3 * i + 3]
            delta, new_m, new_v = _adam_update(g, w_ref[...], m_ref[...], v_ref[...])
            outs[4 * i][...] = g
            outs[4 * i + 1][...] = delta
            outs[4 * i + 2][...] = new_m
            outs[4 * i + 3][...] = new_v
        for j, (part, rows, cols) in enumerate(sums):
            outs[4 * n_i + j][...] = total(part, rows, cols)

    ins = list(parts) + [a for item in items for a in item[3:]]
    out_shapes = [item[3].shape for item in items for _ in range(4)]
    out_shapes += [(rows.stop - rows.start, cols.stop - cols.start) for _, rows, cols in sums]
    out = _call(body, (1,), [_whole(a.shape) for a in ins], [_whole(s) for s in out_shapes],
                [_sds(s, _F32) for s in out_shapes], name, tokens=tokens)(*ins)
    return [out[4 * i:4 * i + 4] for i in range(n_i)], out[4 * n_i:]


def _adamw(parts, w, m, v, name, tokens=()):
    rows, cols = w.shape
    tr = _row_tile(rows)
    n_parts = parts.shape[0]

    def body(p_ref, w_ref, m_ref, v_ref, g_ref, d_ref, nm_ref, nv_ref):
        g = p_ref[0].astype(_F32)
        for s in range(1, n_parts):
            g = g + p_ref[s].astype(_F32)
        new_m = ADAM_B1 * m_ref[...] + (1.0 - ADAM_B1) * g
        new_v = ADAM_B2 * v_ref[...] + (1.0 - ADAM_B2) * (g * g)
        m_hat = new_m / (1.0 - ADAM_B1 ** ADAM_STEP)
        v_hat = new_v / (1.0 - ADAM_B2 ** ADAM_STEP)
        g_ref[...] = g
        d_ref[...] = -ADAM_LR * (m_hat / (jnp.sqrt(v_hat) + ADAM_EPS) + ADAM_WD * w_ref[...])
        nm_ref[...] = new_m
        nv_ref[...] = new_v

    blk = _rows(tr, cols)
    return _call(body, (rows // tr,),
                 [pl.BlockSpec((n_parts, tr, cols), lambda i: (0, i, 0)), blk, blk, blk],
                 [blk] * 4, [_sds((rows, cols), _F32)] * 4, name, tokens=tokens)(parts, w, m, v)


_SMALL = ("g_pre_mix", "ssm_lambda_re", "ssm_lambda_im", "ssm_log_dt", "ssm_b_re", "ssm_b_im",
          "ssm_c_re", "ssm_c_im", "ssm_d", "b_glu", "attn_sinks", "g_ssm_out", "g_attn_out",
          "g_post_mix", "g_pre_ffn", "g_post_ffn")
_BIG = ("w_in", "w_glu", "w_out", "w_gate_up", "w_down")
_WEIGHTS = ("g_pre_mix", "w_in", "ssm_lambda_re", "ssm_lambda_im", "ssm_log_dt", "ssm_b_re", "ssm_b_im",
            "ssm_c_re", "ssm_c_im", "ssm_d", "w_glu", "b_glu", "attn_sinks", "g_ssm_out", "g_attn_out",
            "w_out", "g_post_mix", "g_pre_ffn", "w_gate_up", "w_down", "g_post_ffn")
_LANES = 128


_SHAPE_2D = {
    "g_pre_mix": (1, D_MODEL), "ssm_lambda_re": (SSM_GROUPS, SSM_STATE), "ssm_lambda_im": (SSM_GROUPS, SSM_STATE),
    "ssm_log_dt": (1, SSM_GROUPS), "ssm_b_re": (SSM_WIDTH, SSM_STATE), "ssm_b_im": (SSM_WIDTH, SSM_STATE),
    "ssm_c_re": (SSM_WIDTH, SSM_STATE), "ssm_c_im": (SSM_WIDTH, SSM_STATE), "ssm_d": (SSM_GROUPS, SSM_GROUP),
    "b_glu": (1, 2 * SSM_WIDTH), "attn_sinks": (1, N_Q_HEADS), "g_ssm_out": (1, SSM_WIDTH),
    "g_attn_out": (1, ATTN_WIDTH), "g_post_mix": (1, D_MODEL), "g_pre_ffn": (1, D_MODEL), "g_post_ffn": (1, D_MODEL)}
_ROW_WIDTH = {"g_pre_mix": D_MODEL, "b_glu": 2 * SSM_WIDTH, "attn_sinks": _LANES, "g_ssm_out": SSM_WIDTH,
              "g_attn_out": ATTN_WIDTH, "g_post_mix": D_MODEL, "g_pre_ffn": D_MODEL, "g_post_ffn": D_MODEL,
              "loss": _LANES}
_PER_GROUP_TRANSPOSED = ("ssm_b_re", "ssm_b_im")


def _to_2d(name, a):
    if name in _PER_GROUP_TRANSPOSED:
        a = a.reshape(SSM_GROUPS, SSM_STATE, SSM_GROUP).transpose(0, 2, 1)
    return a.reshape(_SHAPE_2D[name])


def _from_2d(name, a, shape):
    if name in _PER_GROUP_TRANSPOSED:
        a = a.reshape(SSM_GROUPS, SSM_GROUP, SSM_STATE).transpose(0, 2, 1)
    return a.reshape(shape)


def _row_slots(names):
    slots, row, col = {}, 0, 0
    for n in names:
        width = _ROW_WIDTH[n]
        if col + width > D_MODEL:
            row, col = row + 1, 0
        slots[n] = (row, col, width)
        col += width
    return slots


def _stack_rows(named, slots):
    n_rows = -(-(max(r for r, _, _ in slots.values()) + 1) // 8) * 8
    lines = []
    for r in range(n_rows):
        pieces = [named[n] for n, (row, _, _) in slots.items() if row == r]
        used = sum(p.shape[1] for p in pieces)
        if used < D_MODEL:
            pieces.append(jnp.zeros((1, D_MODEL - used), _F32))
        lines.append(jnp.concatenate(pieces, axis=1) if len(pieces) > 1 else pieces[0])
    return jnp.concatenate(lines, axis=0)


def kernel(x, positions, g_pre_mix, w_in, ssm_lambda_re, ssm_lambda_im, ssm_log_dt, ssm_b_re, ssm_b_im, ssm_c_re, ssm_c_im, ssm_d, w_glu, b_glu, attn_sinks, g_ssm_out, g_attn_out, w_out, g_post_mix, g_pre_ffn, w_gate_up, w_down, g_post_ffn, loss_target, m_g_pre_mix, m_w_in, m_ssm_lambda_re, m_ssm_lambda_im, m_ssm_log_dt, m_ssm_b_re, m_ssm_b_im, m_ssm_c_re, m_ssm_c_im, m_ssm_d, m_w_glu, m_b_glu, m_attn_sinks, m_g_ssm_out, m_g_attn_out, m_w_out, m_g_post_mix, m_g_pre_ffn, m_w_gate_up, m_w_down, m_g_post_ffn, v_g_pre_mix, v_w_in, v_ssm_lambda_re, v_ssm_lambda_im, v_ssm_log_dt, v_ssm_b_re, v_ssm_b_im, v_ssm_c_re, v_ssm_c_im, v_ssm_d, v_w_glu, v_b_glu, v_attn_sinks, v_g_ssm_out, v_g_attn_out, v_w_out, v_g_post_mix, v_g_pre_ffn, v_w_gate_up, v_w_down, v_g_post_ffn):
    w = dict(g_pre_mix=g_pre_mix, w_in=w_in, ssm_lambda_re=ssm_lambda_re, ssm_lambda_im=ssm_lambda_im,
             ssm_log_dt=ssm_log_dt, ssm_b_re=ssm_b_re, ssm_b_im=ssm_b_im, ssm_c_re=ssm_c_re, ssm_c_im=ssm_c_im,
             ssm_d=ssm_d, w_glu=w_glu, b_glu=b_glu, attn_sinks=attn_sinks, g_ssm_out=g_ssm_out,
             g_attn_out=g_attn_out, w_out=w_out, g_post_mix=g_post_mix, g_pre_ffn=g_pre_ffn,
             w_gate_up=w_gate_up, w_down=w_down, g_post_ffn=g_post_ffn)
    m = dict(g_pre_mix=m_g_pre_mix, w_in=m_w_in, ssm_lambda_re=m_ssm_lambda_re, ssm_lambda_im=m_ssm_lambda_im,
             ssm_log_dt=m_ssm_log_dt, ssm_b_re=m_ssm_b_re, ssm_b_im=m_ssm_b_im, ssm_c_re=m_ssm_c_re,
             ssm_c_im=m_ssm_c_im, ssm_d=m_ssm_d, w_glu=m_w_glu, b_glu=m_b_glu, attn_sinks=m_attn_sinks,
             g_ssm_out=m_g_ssm_out, g_attn_out=m_g_attn_out, w_out=m_w_out, g_post_mix=m_g_post_mix,
             g_pre_ffn=m_g_pre_ffn, w_gate_up=m_w_gate_up, w_down=m_w_down, g_post_ffn=m_g_post_ffn)
    v = dict(g_pre_mix=v_g_pre_mix, w_in=v_w_in, ssm_lambda_re=v_ssm_lambda_re, ssm_lambda_im=v_ssm_lambda_im,
             ssm_log_dt=v_ssm_log_dt, ssm_b_re=v_ssm_b_re, ssm_b_im=v_ssm_b_im, ssm_c_re=v_ssm_c_re,
             ssm_c_im=v_ssm_c_im, ssm_d=v_ssm_d, w_glu=v_w_glu, b_glu=v_b_glu, attn_sinks=v_attn_sinks,
             g_ssm_out=v_g_ssm_out, g_attn_out=v_g_attn_out, w_out=v_w_out, g_post_mix=v_g_post_mix,
             g_pre_ffn=v_g_pre_ffn, w_gate_up=v_w_gate_up, w_down=v_w_down, g_post_ffn=v_g_post_ffn)

    transposed = ("w_in", "w_glu", "w_gate_up")
    native_transposed = ("w_in", "w_gate_up")
    shard = {n: (w[n][0].T if n in transposed else w[n][0]).astype(_BF16) for n in _BIG}
    gathered = {}
    for cid, names in enumerate((("w_in",), ("w_glu", "w_out"), ("w_gate_up", "w_down")), start=1):
        lands = _sequencer_gather([shard[n] for n in names], "gather_" + names[0], cid)
        gathered.update({n: a.reshape(-1, a.shape[2]) for n, a in zip(names, lands)})

    def fetch(names, after):
        del after
        return [gathered[n] for n in names]

    sent = []
    ids = iter(range(4, 16))
    two_step = {}

    def publish(named):
        big = [n for n in named if n in _BIG]
        if set(big) == {"w_gate_up", "w_down"}:
            blocks = [named[n].reshape(N_DEV, -1, named[n].shape[1]) for n in big]
            two_step.update(names=big, blocks=blocks,
                            received=_sequencer_pair_exchange(blocks, "grads_pair", next(ids)))
            return [named[n] for n in big]
        rows = [n for n in named if n in _ROW_WIDTH]
        plain = [n for n in named if n not in big + rows]
        sources = [named[n].reshape(N_DEV, -1, named[n].shape[1]) for n in big]
        slots = _row_slots(rows)
        if rows:
            sources.append(_stack_rows(named, slots))
        sources += [named[n] for n in plain]
        flags = [True] * len(big) + [False] * (len(sources) - len(big))
        cid = next(ids)
        sent.append((big, slots, plain, _sequencer_exchange(sources, flags, "grads_%d" % cid, cid)))
        return [named[n] for n in big]

    def progress(after):
        core = lax.axis_index("c").astype(jnp.int32).reshape(1)
        partials = [_pair_sum(b, r, core, "pair_sum_" + n, [after])
                    for n, b, r in zip(two_step["names"], two_step["blocks"], two_step["received"])]
        sent.append((two_step["names"], {}, [], _sequencer_chip_exchange(partials, "grads_chips", next(ids))))
        return partials

    p = {n: w[n] for n in _SMALL}
    grad_x = _local_step(x[0], positions[0], loss_target[0], p, fetch, publish, progress)

    state = {n: [_to_2d(n, a) for a in (w[n], m[n], v[n])] for n in _SMALL}
    result = {}
    total_loss = None
    chain = []
    for big, slots, plain, lands in sent:
        lands = list(lands)
        after = list(chain)
        for name in big:
            part = lands.pop(0)
            if name in native_transposed:
                updated = _adamw(part, w[name][0].T, m[name][0].T, v[name][0].T, "adamw_" + name, after)
                result[name] = [a.T[None] for a in updated]
                chain.append(updated[3])
                continue
            if name in transposed:
                part = _sum_parts(part, "sum_" + name, after).T[None]
            updated = _adamw(part, w[name][0], m[name][0], v[name][0], "adamw_" + name, after)
            result[name] = [a[None] for a in updated]
            chain.append(updated[3])
        parts, items, sums, names = [], [], [], []
        if slots:
            parts.append(lands.pop(0))
            for name, (row, col, _) in slots.items():
                if name == "loss":
                    sums.append((0, slice(row, row + 1), slice(col, col + _LANES)))
                else:
                    items.append((0, slice(row, row + 1), slice(col, col + _SHAPE_2D[name][1]), *state[name]))
                    names.append(name)
        for name in plain:
            packed = _SSM_PACK if name == "ssm_pack" else {name: (0, _SHAPE_2D[name][0], 0, _SHAPE_2D[name][1])}
            for member, (first, rows_n, lane, cols_n) in packed.items():
                items.append((len(parts), slice(first, first + rows_n), slice(lane, lane + cols_n), *state[member]))
                names.append(member)
            parts.append(lands.pop(0))
        if items:
            updated, summed = _adamw_small(parts, items, sums, "adamw_small_" + names[0], after)
            chain.append(updated[0][3])
            result.update(dict(zip(names, updated)))
            if summed:
                total_loss = summed[0][0, 0]

    out = [total_loss, grad_x[None]]
    for kind in range(4):
        out += [_from_2d(n, result[n][kind], w[n].shape) for n in _WEIGHTS]
    return tuple(out)
```

```python
import math

import numpy as np
import jax
import jax.numpy as jnp
from jax import lax
from jax.experimental import pallas as pl
from jax.experimental.pallas import tpu as pltpu
from jax.experimental.pallas import tpu_sc as plsc

D_MODEL = 1024
SSM_WIDTH = 512
SSM_GROUP = 16
SSM_GROUPS = 32
SSM_STATE = 64
N_STATE = SSM_GROUPS * SSM_STATE
ATTN_WIDTH = 512
HEAD_DIM = 64
N_Q_HEADS = 8
N_KV_HEADS = 2
Q_PER_KV = 4
KV_WIDTH = 128
IN_WIDTH = 1280
BLOCK = 128
ROPE_DIM = 16
ROPE_THETA = 500000.0
D_FF = 2816
NORM_EPS = 1e-6
MASK_VALUE = -1e30
ADAM_LR = 0.001
ADAM_B1 = 0.9
ADAM_B2 = 0.999
ADAM_EPS = 1e-08
ADAM_WD = 0.01
ADAM_STEP = 10

N_DEV = 8
SCAN_CHUNKS = 8
SCAN_UNROLL = 8
FFN_CHUNK = 2816
TOKEN_TILE = 256
VMEM_LIMIT = 56 * 1024 * 1024

_F32 = jnp.float32
_BF16 = jnp.bfloat16
_MXU = jnp.bfloat16

_NN = ((1,), (0,))
_NT = ((1,), (1,))
_TN = ((0,), (0,))


def _dot(a, b, dims):
    return lax.dot_general(a.astype(_MXU), b.astype(_MXU), (dims, ((), ())),
                           preferred_element_type=_F32)


def _dot_exact(a, b, dims):
    return lax.dot_general(a.astype(_F32), b.astype(_F32), (dims, ((), ())),
                           precision=lax.Precision.HIGHEST, preferred_element_type=_F32)


def _iota(shape, dim):
    return lax.broadcasted_iota(jnp.int32, shape, dim)


def _rms_fwd(x, g):
    r = lax.rsqrt(jnp.mean(x * x, axis=-1, keepdims=True) + NORM_EPS)
    return x * r * g, r


def _rms_bwd(dy, x, g, r):
    a = dy * g
    xn = x * r
    dx = r * (a - xn * jnp.mean(a * xn, axis=-1, keepdims=True))
    dg = jnp.sum(dy * xn, axis=0, keepdims=True)
    return dx, dg


def _call(body, grid, in_specs, out_specs, out_shape, name, scratch=(), tokens=()):
    params = pltpu.CompilerParams(dimension_semantics=("arbitrary",) * len(grid),
                                  vmem_limit_bytes=VMEM_LIMIT)
    n_in, n_tok = len(in_specs), len(tokens)

    def run(*refs):
        return body(*refs[:n_in], *refs[n_in + n_tok:])

    call = pl.pallas_call(run, grid=grid,
                          in_specs=list(in_specs) + [pl.BlockSpec(memory_space=pl.ANY)] * n_tok,
                          out_specs=out_specs, out_shape=out_shape, scratch_shapes=list(scratch),
                          compiler_params=params, name=name)
    return lambda *args: call(*args, *tokens)


def _rows(tm, n):
    return pl.BlockSpec((tm, n), lambda i: (i, 0))


def _whole(shape):
    nd = len(shape)
    return pl.BlockSpec(shape, lambda i: (0,) * nd)


def _sds(shape, dtype):
    return jax.ShapeDtypeStruct(shape, dtype)


def _tile(L):
    return min(TOKEN_TILE, L)


def _chunk_tile(L):
    return L // SCAN_CHUNKS


def _chunk_block(L, n):
    return pl.BlockSpec((_chunk_tile(L), n), lambda i: (0, i))


def _chunk_shape(L, n):
    return (_chunk_tile(L), SCAN_CHUNKS * n)


def _accumulate(ref, val, first):
    @pl.when(first)
    def _():
        ref[...] = val

    @pl.when(jnp.logical_not(first))
    def _():
        ref[...] += val


def _rope_rows():
    half = ROPE_DIM // 2
    inv = (np.float32(ROPE_THETA) ** (-np.arange(half, dtype=np.float32) * np.float32(2.0) / np.float32(ROPE_DIM))).astype(np.float32)
    col = np.arange(KV_WIDTH) % HEAD_DIM
    freq = np.where(col < ROPE_DIM, inv[col % half], 0.0).astype(np.float32)
    sign = np.where(col < half, -1.0, np.where(col < ROPE_DIM, 1.0, 0.0)).astype(np.float32)
    return freq[None, :], sign[None, :]


def _rope_tables(pos_col):
    L = pos_col.shape[0]
    tm = _tile(L)
    freq, sign = _rope_rows()

    def body(pos_ref, freq_ref, sign_ref, cos_ref, sin_ref):
        ang = pos_ref[...].astype(_F32) * freq_ref[...]
        cos_ref[...] = jnp.cos(ang)
        sin_ref[...] = jnp.sin(ang) * sign_ref[...]

    return _call(body, (L // tm,),
                 [_rows(tm, 1), _whole((1, KV_WIDTH)), _whole((1, KV_WIDTH))],
                 [_rows(tm, KV_WIDTH), _rows(tm, KV_WIDTH)],
                 [_sds((L, KV_WIDTH), _F32)] * 2, "rope_tables")(pos_col, jnp.asarray(freq), jnp.asarray(sign))


def _widen(t, width):
    return t if width == KV_WIDTH else jnp.concatenate([t] * (width // KV_WIDTH), axis=1)


def _rope_partner(t):
    w = t.shape[1]
    in_head = _iota((1, w), 1) & (HEAD_DIM - 1)
    second = jnp.where(in_head < ROPE_DIM, pltpu.roll(t, ROPE_DIM // 2, 1), 0.0)
    return jnp.where(in_head < ROPE_DIM // 2, pltpu.roll(t, w - ROPE_DIM // 2, 1), second)


def _rope_apply(t, cos_t, sin_t):
    w = t.shape[1]
    return t * _widen(cos_t, w) + _rope_partner(t) * _widen(sin_t, w)


def _rope_transpose(dt, cos_t, sin_t):
    w = dt.shape[1]
    return dt * _widen(cos_t, w) + _rope_partner(dt * _widen(sin_t, w))


def _in_proj(x, g_pre_mix, w_in, cos_t, sin_t):
    L = x.shape[0]
    tm = _chunk_tile(L)

    def body(x_ref, g_ref, w_ref, cos_ref, sin_ref, hn_ref, u_ref, q_ref, k_ref, v_ref):
        hn, _ = _rms_fwd(x_ref[...], g_ref[...])
        hn = hn.astype(_BF16)
        hn_ref[...] = hn
        proj = _dot(hn, w_ref[...], _NT)
        u_ref[...] = proj[:, :SSM_WIDTH]
        q = proj[:, SSM_WIDTH:SSM_WIDTH + ATTN_WIDTH]
        k = proj[:, SSM_WIDTH + ATTN_WIDTH:SSM_WIDTH + ATTN_WIDTH + KV_WIDTH]
        cos_v, sin_v = cos_ref[...], sin_ref[...]
        q_ref[...] = _rope_apply(q, cos_v, sin_v).astype(_BF16)
        k_ref[...] = _rope_apply(k, cos_v, sin_v).astype(_BF16)
        v_ref[...] = proj[:, SSM_WIDTH + ATTN_WIDTH + KV_WIDTH:].astype(_BF16)

    return _call(body, (L // tm,),
                 [_rows(tm, D_MODEL), _whole((1, D_MODEL)), _whole((IN_WIDTH, D_MODEL)),
                  _rows(tm, KV_WIDTH), _rows(tm, KV_WIDTH)],
                 [_rows(tm, D_MODEL), _chunk_block(L, SSM_WIDTH), _rows(tm, ATTN_WIDTH),
                  _rows(tm, KV_WIDTH), _rows(tm, KV_WIDTH)],
                 [_sds((L, D_MODEL), _BF16), _sds(_chunk_shape(L, SSM_WIDTH), _F32), _sds((L, ATTN_WIDTH), _BF16),
                  _sds((L, KV_WIDTH), _BF16), _sds((L, KV_WIDTH), _BF16)],
                 "in_proj")(x, g_pre_mix, w_in, cos_t, sin_t)


def _s5_discretize(lam_re, lam_im, log_dt):
    lr = jnp.minimum(lam_re, -1e-4)
    li = lam_im
    dt = jnp.exp(log_dt)
    mag = jnp.exp(lr * dt)
    ar = mag * jnp.cos(li * dt)
    ai = mag * jnp.sin(li * dt)
    den = lr * lr + li * li
    fr = ((ar - 1.0) * lr + ai * li) / den
    fi = (ai * lr - (ar - 1.0) * li) / den
    return ar, ai, fr, fi


SUPER = 4
SB_STATE = N_STATE // SUPER
SB_WIDTH = SSM_WIDTH // SUPER


def _sb_state(k):
    return slice(SB_STATE * k, SB_STATE * (k + 1))


def _sb_width(k):
    return slice(SB_WIDTH * k, SB_WIDTH * (k + 1))


def _dt_column(log_dt_row):
    eye = _iota((SSM_GROUPS, SSM_GROUPS), 0) == _iota((SSM_GROUPS, SSM_GROUPS), 1)
    return jnp.sum(jnp.where(eye, log_dt_row, 0.0), axis=1, keepdims=True)


def _group_masks():
    e64 = ((_iota((SSM_STATE, N_STATE), 1) & (SSM_STATE - 1)) == _iota((SSM_STATE, N_STATE), 0)).astype(_F32)
    own = _iota((SSM_GROUPS, N_STATE), 0) == (_iota((SSM_GROUPS, N_STATE), 1) >> 6)
    return e64, own


def _rows_of_group():
    return ((_iota((SSM_WIDTH, SSM_GROUPS), 0) >> 4) == _iota((SSM_WIDTH, SSM_GROUPS), 1)).astype(_F32)


def _ssm_prep(lam_re, lam_im, log_dt, b_re, b_im, c_re, c_im):
    def body(lr_ref, li_ref, ld_ref, bre, bim, cre, cim, ar_ref, ai_ref, btr, bti, ctr, cti):
        ar, ai, fr, fi = _s5_discretize(lr_ref[...], li_ref[...], _dt_column(ld_ref[...]))
        e64, own = _group_masks()
        mask_c = (_iota((SSM_WIDTH, N_STATE), 0) >> 4) == (_iota((SSM_WIDTH, N_STATE), 1) >> 6)

        def to_row(t):
            return jnp.sum(jnp.where(own, _dot_exact(t, e64, _NN), 0.0), axis=0, keepdims=True)

        def fold(m):
            full = jnp.where(mask_c, _dot(m, e64, _NN), 0.0)
            return sum(full[_sb_width(k), :] for k in range(SUPER)).astype(_BF16)

        ar_ref[...] = to_row(ar)
        ai_ref[...] = to_row(ai)
        spread = _rows_of_group()
        fr_t = _dot_exact(spread, fr, _NN)
        fi_t = _dot_exact(spread, fi, _NN)
        btr[...] = fold(fr_t * bre[...] - fi_t * bim[...])
        bti[...] = fold(fr_t * bim[...] + fi_t * bre[...])
        ctr[...] = fold(cre[...])
        cti[...] = fold(cim[...])

    row = (1, N_STATE)
    ins = [lam_re, lam_im, log_dt, b_re, b_im, c_re, c_im]
    return _call(body, (1,), [_whole(a.shape) for a in ins],
                 [_whole(row), _whole(row)] + [_whole((SB_WIDTH, N_STATE))] * 4,
                 [_sds(row, _F32), _sds(row, _F32)] + [_sds((SB_WIDTH, N_STATE), _BF16)] * 4,
                 "ssm_prep")(*ins)


def _complex_power(ar, ai, n):
    pr, pi = jnp.ones_like(ar), jnp.zeros_like(ai)
    while n:
        if n & 1:
            pr, pi = pr * ar - pi * ai, pr * ai + pi * ar
        ar, ai = ar * ar - ai * ai, 2.0 * ar * ai
        n >>= 1
    return pr, pi


def _chunk_carries(er, ei, pr, pi, reverse):
    rows = _iota(er.shape, 0)
    sr = jnp.zeros_like(pr)
    si = jnp.zeros_like(pi)
    out_r = jnp.zeros_like(er)
    out_i = jnp.zeros_like(ei)
    order = range(SCAN_CHUNKS - 1, 0, -1) if reverse else range(SCAN_CHUNKS - 1)
    for c in order:
        e_r = er[c:c + 1, :]
        e_i = ei[c:c + 1, :]
        sr, si = pr * sr - pi * si + e_r, pr * si + pi * sr + e_i
        nxt = c - 1 if reverse else c + 1
        out_r = jnp.where(rows == nxt, sr, out_r)
        out_i = jnp.where(rows == nxt, si, out_i)
    return out_r, out_i


_GELU_K = math.sqrt(2.0 / math.pi)
_GELU_C = 0.044715


def _gelu(y):
    return 0.5 * y * (1.0 + jnp.tanh(_GELU_K * (y + _GELU_C * y * y * y)))


def _gelu_grad(y):
    t = jnp.tanh(_GELU_K * (y + _GELU_C * y * y * y))
    return 0.5 * (1.0 + t) + 0.5 * y * (1.0 - t * t) * _GELU_K * (1.0 + 3.0 * _GELU_C * y * y)


def _step_rows(t):
    return pl.ds(pl.multiple_of(t * SCAN_CHUNKS, SCAN_CHUNKS), SCAN_CHUNKS)


def _scan_in_place(br, bi, ar, ai, T, carries=None):
    W = br.shape[1]
    ar8 = jnp.broadcast_to(ar, (SCAN_CHUNKS, W))
    ai8 = jnp.broadcast_to(ai, (SCAN_CHUNKS, W))

    def local(t, c):
        cr, ci = c
        rows = _step_rows(t)
        return ar8 * cr - ai8 * ci + br[rows, :], ar8 * ci + ai8 * cr + bi[rows, :]

    if carries is None:
        zero = jnp.zeros((SCAN_CHUNKS, W), _F32)
        er, ei = lax.fori_loop(0, T, local, (zero, zero), unroll=SCAN_UNROLL)
        pr, pi = _complex_power(ar, ai, T)
        carries = _chunk_carries(er, ei, pr, pi, reverse=False)

    def final(t, c):
        nr, ni = local(t, c)
        rows = _step_rows(t)
        br[rows, :] = nr
        bi[rows, :] = ni
        return nr, ni

    lax.fori_loop(0, T, final, carries, unroll=SCAN_UNROLL)
    return carries


def _scan_reverse_in_place(dr, di, xr, xi, ar, ai, T):
    W = dr.shape[1]
    ar8 = jnp.broadcast_to(ar, (SCAN_CHUNKS, W))
    ai8 = jnp.broadcast_to(ai, (SCAN_CHUNKS, W))

    def local(t, c):
        cr, ci = c
        rows = _step_rows(t)
        return ar8 * cr + ai8 * ci + dr[rows, :], ar8 * ci - ai8 * cr + di[rows, :]

    zero = jnp.zeros((SCAN_CHUNKS, W), _F32)
    er, ei = lax.fori_loop(0, T, lambda k, c: local(T - 1 - k, c), (zero, zero), unroll=SCAN_UNROLL)
    pr, pi = _complex_power(ar, -ai, T)
    sr, si = _chunk_carries(er, ei, pr, pi, reverse=True)

    def grad_a(acc, nr, ni, xpr, xpi):
        return acc[0] + nr * xpr + ni * xpi, acc[1] + ni * xpr - nr * xpi

    def final(k, c):
        t = T - 1 - k
        nr, ni = local(t, c[:2])
        rows = _step_rows(t)
        dr[rows, :] = nr
        di[rows, :] = ni
        before = _step_rows(t - 1)
        gr, gi = grad_a(c[2:], nr, ni, xr[before, :], xi[before, :])
        return nr, ni, gr, gi

    cr, ci, gr, gi = lax.fori_loop(0, T - 1, final, (sr, si, zero, zero), unroll=SCAN_UNROLL)
    nr, ni = local(0, (cr, ci))
    dr[_step_rows(0), :] = nr
    di[_step_rows(0), :] = ni
    first = _iota((SCAN_CHUNKS, W), 0) == 0
    last = _step_rows(T - 1)
    xpr = jnp.where(first, 0.0, pltpu.roll(xr[last, :], 1, 0))
    xpi = jnp.where(first, 0.0, pltpu.roll(xi[last, :], 1, 0))
    gr, gi = grad_a((gr, gi), nr, ni, xpr, xpi)
    return jnp.sum(gr, axis=0, keepdims=True), jnp.sum(gi, axis=0, keepdims=True)


def _ssm_super_specs(L):
    width = pl.BlockSpec((L, SB_WIDTH), lambda k: (0, k))
    matrix = pl.BlockSpec((SB_WIDTH, SB_STATE), lambda k: (0, k))
    row = pl.BlockSpec((1, SB_STATE), lambda k: (0, k))
    return width, matrix, row


def _ssm_states(u_ref, br_ref, bi_ref, ar_ref, ai_ref, xr, xi, T, carries=None):
    ub = u_ref[...].astype(_BF16)
    xr[...] = _dot(ub, br_ref[...], _NN)
    xi[...] = _dot(ub, bi_ref[...], _NN)
    return _scan_in_place(xr, xi, ar_ref[...], ai_ref[...], T, carries)


def _ssm_core_fwd(u, bt_re, bt_im, ct_re, ct_im, a_re, a_im):
    L = u.shape[0]
    T = L // SCAN_CHUNKS

    def body(u_ref, br_ref, bi_ref, cr_ref, ci_ref, ar_ref, ai_ref, y_ref, sr_ref, si_ref, xr, xi):
        sr_ref[...], si_ref[...] = _ssm_states(u_ref, br_ref, bi_ref, ar_ref, ai_ref, xr, xi, T)
        y_ref[...] = _dot(xr[...], cr_ref[...], _NT) - _dot(xi[...], ci_ref[...], _NT)

    width, matrix, row = _ssm_super_specs(L)
    carry = pl.BlockSpec((SCAN_CHUNKS, SB_STATE), lambda k: (0, k))
    return _call(body, (SUPER,), [width, matrix, matrix, matrix, matrix, row, row], [width, carry, carry],
                 [_sds((L, SSM_WIDTH), _F32)] + [_sds((SCAN_CHUNKS, N_STATE), _F32)] * 2, "ssm_core_fwd",
                 scratch=[pltpu.VMEM((L, SB_STATE), _F32)] * 2)(u, bt_re, bt_im, ct_re, ct_im, a_re, a_im)


def _ssm_core_bwd(u, dy, dud, carry_re, carry_im, bt_re, bt_im, ct_re, ct_im, a_re, a_im, tokens=()):
    L = u.shape[0]
    T = L // SCAN_CHUNKS

    def body(u_ref, dy_ref, dud_ref, sr_ref, si_ref, br_ref, bi_ref, cr_ref, ci_ref, ar_ref, ai_ref,
             du_ref, dcr_ref, dci_ref, dbr_ref, dbi_ref, dar_ref, dai_ref, xr, xi, lr, li):
        _ssm_states(u_ref, br_ref, bi_ref, ar_ref, ai_ref, xr, xi, T, (sr_ref[...], si_ref[...]))
        dyb = dy_ref[...]
        lr[...] = _dot(dyb, cr_ref[...], _NN)
        li[...] = -_dot(dyb, ci_ref[...], _NN)
        da_re, da_im = _scan_reverse_in_place(lr, li, xr, xi, ar_ref[...], ai_ref[...], T)
        dar_ref[...] = da_re
        dai_ref[...] = da_im
        du_ref[...] = _dot(lr[...], br_ref[...], _NT) + _dot(li[...], bi_ref[...], _NT) + dud_ref[...]
        ub = u_ref[...].astype(_BF16)
        dcr_ref[...] = _dot(dyb, xr[...], _TN)
        dci_ref[...] = _dot(dyb, xi[...], _TN)
        dbr_ref[...] = _dot(ub, lr[...], _TN)
        dbi_ref[...] = _dot(ub, li[...], _TN)

    width, matrix, row = _ssm_super_specs(L)
    carry = pl.BlockSpec((SCAN_CHUNKS, SB_STATE), lambda k: (0, k))
    return _call(body, (SUPER,), [width, width, width, carry, carry, matrix, matrix, matrix, matrix, row, row],
                 [width] + [matrix] * 4 + [row] * 2,
                 [_sds((L, SSM_WIDTH), _F32)] + [_sds((SB_WIDTH, N_STATE), _F32)] * 4 + [_sds((1, N_STATE), _F32)] * 2,
                 "ssm_core_bwd", scratch=[pltpu.VMEM((L, SB_STATE), _F32)] * 4,
                 tokens=tokens)(u, dy, dud, carry_re, carry_im, bt_re, bt_im, ct_re, ct_im, a_re, a_im)


def _ssm_out(cx, u, d_row, w_glu, b_glu, g_ssm):
    L = u.shape[0]
    tm = _tile(L)

    def body(cx_ref, u_ref, d_ref, w_ref, b_ref, g_ref, y_ref, z_ref, n_ref, stage):
        y = cx_ref[...] + d_ref[...] * u_ref[...]
        y_ref[...] = y
        z = _dot(_gelu(y), w_ref[...], _NT) + b_ref[...]
        z_ref[...] = z
        out = z[:, :SSM_WIDTH] * jax.nn.sigmoid(z[:, SSM_WIDTH:])
        n, _ = _rms_fwd(out, g_ref[...])
        for k in range(SSM_WIDTH // _LANES):
            stage[k] = n[:, _LANES * k:_LANES * (k + 1)]
            for c in range(SCAN_CHUNKS):
                rows = stage[k, pl.ds(c, tm // SCAN_CHUNKS, stride=SCAN_CHUNKS), :]
                lane = SSM_WIDTH * c + _LANES * k
                n_ref[:, lane:lane + _LANES] = rows.astype(_BF16)

    return _call(body, (L // tm,),
                 [_rows(tm, SSM_WIDTH), _rows(tm, SSM_WIDTH), _whole((1, SSM_WIDTH)),
                  _whole((2 * SSM_WIDTH, SSM_WIDTH)), _whole((1, 2 * SSM_WIDTH)), _whole((1, SSM_WIDTH))],
                 [_rows(tm, SSM_WIDTH), _rows(tm, 2 * SSM_WIDTH), _rows(tm // SCAN_CHUNKS, SCAN_CHUNKS * SSM_WIDTH)],
                 [_sds((L, SSM_WIDTH), _F32), _sds((L, 2 * SSM_WIDTH), _F32), _sds(_chunk_shape(L, SSM_WIDTH), _BF16)],
                 "ssm_out", scratch=[pltpu.VMEM((SSM_WIDTH // _LANES, tm, _LANES), _F32)])(
        cx, u, d_row, w_glu, b_glu, g_ssm)


def _ssm_out_bwd(dn, y, z, u, d_row, w_glu, g_ssm, tokens=()):
    L = u.shape[0]
    tm = _tile(L)

    def body(dn_ref, y_ref, z_ref, u_ref, d_ref, w_ref, g_ref,
             gy_ref, dz_ref, dy_ref, dud_ref, dg_ref, db_ref, dd_ref, stage):
        first = pl.program_id(0) == 0
        for k in range(SSM_WIDTH // _LANES):
            for c in range(SCAN_CHUNKS):
                lane = SSM_WIDTH * c + _LANES * k
                stage[k, pl.ds(c, tm // SCAN_CHUNKS, stride=SCAN_CHUNKS), :] = dn_ref[:, lane:lane + _LANES]
        dn = jnp.concatenate([stage[k] for k in range(SSM_WIDTH // _LANES)], axis=1)
        z = z_ref[...]
        z1, z2 = z[:, :SSM_WIDTH], z[:, SSM_WIDTH:]
        sig = jax.nn.sigmoid(z2)
        out = z1 * sig
        g = g_ref[...]
        _, r = _rms_fwd(out, g)
        dout, dg = _rms_bwd(dn, out, g, r)
        _accumulate(dg_ref, dg, first)
        dz = jnp.concatenate([dout * sig, dout * z1 * sig * (1.0 - sig)], axis=1)
        _accumulate(db_ref, jnp.sum(dz, axis=0, keepdims=True), first)
        dzb = dz.astype(_BF16)
        dz_ref[...] = dzb
        y = y_ref[...]
        gy_ref[...] = _gelu(y).astype(_BF16)
        dy = _dot(dzb, w_ref[...], _NN) * _gelu_grad(y)
        u = u_ref[...]
        _accumulate(dd_ref, jnp.sum(dy * u, axis=0, keepdims=True), first)
        dud_ref[...] = d_ref[...] * dy
        dy_ref[...] = dy.astype(_BF16)

    row = _whole((1, SSM_WIDTH))
    return _call(body, (L // tm,),
                 [_rows(tm // SCAN_CHUNKS, SCAN_CHUNKS * SSM_WIDTH), _rows(tm, SSM_WIDTH), _rows(tm, 2 * SSM_WIDTH),
                  _rows(tm, SSM_WIDTH), row, _whole((2 * SSM_WIDTH, SSM_WIDTH)), row],
                 [_rows(tm, SSM_WIDTH), _rows(tm, 2 * SSM_WIDTH), _rows(tm, SSM_WIDTH), _rows(tm, SSM_WIDTH),
                  row, _whole((1, 2 * SSM_WIDTH)), row],
                 [_sds((L, SSM_WIDTH), _BF16), _sds((L, 2 * SSM_WIDTH), _BF16), _sds((L, SSM_WIDTH), _BF16),
                  _sds((L, SSM_WIDTH), _F32),
                  _sds((1, SSM_WIDTH), _F32), _sds((1, 2 * SSM_WIDTH), _F32), _sds((1, SSM_WIDTH), _F32)],
                 "ssm_out_bwd", scratch=[pltpu.VMEM((SSM_WIDTH // _LANES, tm, _LANES), _F32)], tokens=tokens)(
        dn, y, z, u, d_row, w_glu, g_ssm)


_SSM_PACK = {"ssm_b_re": (0, SSM_WIDTH, 0, SSM_STATE), "ssm_c_re": (0, SSM_WIDTH, 64, SSM_STATE),
             "ssm_b_im": (512, SSM_WIDTH, 0, SSM_STATE), "ssm_c_im": (512, SSM_WIDTH, 64, SSM_STATE),
             "ssm_lambda_re": (1024, SSM_GROUPS, 0, SSM_STATE), "ssm_lambda_im": (1024, SSM_GROUPS, 64, SSM_STATE),
             "ssm_d": (1056, SSM_GROUPS, 0, SSM_GROUP), "ssm_log_dt": (1088, 1, 0, SSM_GROUPS)}
_PACK_TILE = 16
_SSM_PACK_ROWS = 1088 + _PACK_TILE


def _ssm_param_bwd(da_re, da_im, dbt_re, dbt_im, dct_re, dct_im, lam_re, lam_im, log_dt, b_re, b_im, g_d):
    def body(dar, dai, dbr, dbi, dcr, dci, lr_ref, li_ref, ld_ref, bre_ref, bim_ref, gd_ref, pack_ref):
        lane_in = _iota((SSM_STATE, _LANES), 0)
        lane_out = _iota((SSM_STATE, _LANES), 1)
        low = (lane_out == lane_in).astype(_F32)
        high = (lane_out == lane_in + SSM_STATE).astype(_F32)

        def side_by_side(a, b):
            return _dot_exact(a, low, _NN) + _dot_exact(b, high, _NN)

        tail = _SSM_PACK["ssm_d"][0]
        pack_ref[tail:, :] = jnp.zeros((_SSM_PACK_ROWS - tail, _LANES), _BF16)
        pack_ref[tail:tail + SSM_GROUPS, 0:SSM_GROUP] = gd_ref[...].astype(_BF16)
        own_c = (_iota((SB_WIDTH, SB_STATE), 0) >> 4) == (_iota((SB_WIDTH, SB_STATE), 1) >> 6)

        def unfold(ref):
            blocks = []
            for k in range(SUPER):
                t = jnp.where(own_c, ref[:, _sb_state(k)], 0.0)
                t = sum(t[:, 128 * i:128 * (i + 1)] for i in range(SB_STATE // 128))
                blocks.append((t + pltpu.roll(t, SSM_STATE, 1))[:, :SSM_STATE])
            return jnp.concatenate(blocks, axis=0)

        dbb_re, dbb_im = unfold(dbr), unfold(dbi)
        b_re, b_im = bre_ref[...], bim_ref[...]
        dt_col = _dt_column(ld_ref[...])
        (_, _, fr, fi), vjp = jax.vjp(_s5_discretize, lr_ref[...], li_ref[...], dt_col)
        spread = _rows_of_group()
        fr_t = _dot_exact(spread, fr, _NN)
        fi_t = _dot_exact(spread, fi, _NN)
        pack_ref[0:SSM_WIDTH, :] = side_by_side(fr_t * dbb_re + fi_t * dbb_im, unfold(dcr)).astype(_BF16)
        pack_ref[SSM_WIDTH:2 * SSM_WIDTH, :] = side_by_side(fr_t * dbb_im - fi_t * dbb_re, -unfold(dci)).astype(_BF16)
        d_fr = _dot_exact(spread, dbb_re * b_re + dbb_im * b_im, _TN)
        d_fi = _dot_exact(spread, dbb_im * b_re - dbb_re * b_im, _TN)
        e64, own = _group_masks()

        def from_row(ref):
            return _dot_exact(jnp.where(own, ref[...], 0.0), e64, _NT)

        d_lr, d_li, d_dt = vjp((from_row(dar), from_row(dai), d_fr, d_fi))
        lam_rows = _SSM_PACK["ssm_lambda_re"][0]
        pack_ref[lam_rows:lam_rows + SSM_GROUPS, :] = side_by_side(d_lr, d_li).astype(_BF16)
        eye = (_iota((SSM_GROUPS, SSM_GROUPS), 0) == _iota((SSM_GROUPS, SSM_GROUPS), 1)).astype(_F32)
        dt_row = _SSM_PACK["ssm_log_dt"][0]
        pack_ref[dt_row:dt_row + _PACK_TILE, 0:SSM_GROUPS] = _dot_exact(
            jnp.broadcast_to(d_dt, (SSM_GROUPS, 128)), eye, _TN)[0:_PACK_TILE].astype(_BF16)

    ins = [da_re, da_im, dbt_re, dbt_im, dct_re, dct_im, lam_re, lam_im, log_dt, b_re, b_im, g_d]
    out = (_SSM_PACK_ROWS, _LANES)
    return _call(body, (1,), [_whole(a.shape) for a in ins], _whole(out), _sds(out, _BF16), "ssm_param_bwd")(*ins)


def _head_spread(j):
    r = _iota((KV_WIDTH, 256), 0)
    c = _iota((KV_WIDTH, 256), 1)
    return (r == HEAD_DIM * j + (c & (HEAD_DIM - 1))).astype(_BF16)


STACK = Q_PER_KV * BLOCK


def _stack_heads(t):
    lane_head = _iota((1, 256), 1) >> 6
    return jnp.concatenate([jnp.where(lane_head == g, t, jnp.zeros_like(t)) for g in range(Q_PER_KV)], axis=0)


def _unstack_heads(t):
    lane_head = _iota((1, 256), 1) >> 6
    return sum(jnp.where(lane_head == g, t[BLOCK * g:BLOCK * (g + 1)], 0.0) for g in range(Q_PER_KV))


def _stacked_sinks(sink_ref, j):
    block = _iota((STACK, 1), 0) >> 7
    col = jnp.full((STACK, 1), sink_ref[Q_PER_KV * j], _F32)
    for g in range(1, Q_PER_KV):
        col = jnp.where(block == g, sink_ref[Q_PER_KV * j + g], col)
    return col


def _fold_heads(t, j):
    t = t[:, :KV_WIDTH] + t[:, KV_WIDTH:]
    t = t + pltpu.roll(t, HEAD_DIM, 1)
    return jnp.where((_iota((1, KV_WIDTH), 1) >> 6) == j, t, 0.0)


def _attn_scores(q_stacked, kt, blk, sink):
    s = _dot(q_stacked, kt, _NT) * (HEAD_DIM ** -0.5)
    qi = _iota((STACK, 2 * BLOCK), 0) & (BLOCK - 1)
    kj = _iota((STACK, 2 * BLOCK), 1)
    rel = qi + BLOCK - kj
    valid = (rel >= 0) & (rel < BLOCK) & (blk * BLOCK - BLOCK + kj >= 0)
    s = jnp.where(valid, s, MASK_VALUE)
    m = jnp.maximum(jnp.max(s, axis=-1, keepdims=True), sink)
    p = jnp.exp(s - m)
    e_sink = jnp.exp(sink - m)
    den = jnp.sum(p, axis=-1, keepdims=True) + e_sink
    return p / den, e_sink / den


def _sink_slot():
    return _iota((STACK, 2 * BLOCK), 1) == 0


def _prob_block():
    return pl.BlockSpec((None, N_KV_HEADS, STACK, 2 * BLOCK), lambda i: (i, 0, 0, 0))


def _attn_specs():
    prev = lambda i: (jnp.maximum(i - 1, 0), 0)
    cur = lambda i: (i, 0)
    kv = [pl.BlockSpec((BLOCK, KV_WIDTH), prev), pl.BlockSpec((BLOCK, KV_WIDTH), cur)]
    return [pl.BlockSpec((BLOCK, ATTN_WIDTH), cur)] + kv + kv


def _attn_fwd(q, k, v, sinks, g_attn):
    L = q.shape[0]

    def body(q_ref, kp_ref, kc_ref, vp_ref, vc_ref, sink_ref, g_ref, o_ref, n_ref, p_ref):
        blk = pl.program_id(0)
        kwin = jnp.concatenate([kp_ref[...], kc_ref[...]], axis=0)
        vwin = jnp.concatenate([vp_ref[...], vc_ref[...]], axis=0)
        halves = []
        for j in range(N_KV_HEADS):
            spread = _head_spread(j)
            kt = _dot(kwin, spread, _NN).astype(_BF16)
            vt = _dot(vwin, spread, _NN).astype(_BF16)
            qs = _stack_heads(q_ref[:, 256 * j:256 * (j + 1)])
            p, p_sink = _attn_scores(qs, kt, blk, _stacked_sinks(sink_ref, j))
            p_ref[j] = jnp.where(_sink_slot(), p_sink, p)
            halves.append(_unstack_heads(_dot(p, vt, _NN)))
        o = jnp.concatenate(halves, axis=1)
        o_ref[...] = o
        n, _ = _rms_fwd(o, g_ref[...])
        n_ref[...] = n.astype(_BF16)

    cur = lambda i: (i, 0)
    return _call(body, (L // BLOCK,),
                 _attn_specs() + [pl.BlockSpec(memory_space=pltpu.SMEM), _whole((1, ATTN_WIDTH))],
                 [pl.BlockSpec((BLOCK, ATTN_WIDTH), cur)] * 2 + [_prob_block()],
                 [_sds((L, ATTN_WIDTH), _F32), _sds((L, ATTN_WIDTH), _BF16),
                  _sds((L // BLOCK, N_KV_HEADS, STACK, 2 * BLOCK), _F32)],
                 "attn_fwd")(q, k, k, v, v, sinks, g_attn)


def _attn_bwd(q, k, v, o, dn, probs, g_attn):
    L = q.shape[0]

    def body(q_ref, kp_ref, kc_ref, vp_ref, vc_ref, o_ref, dn_ref, p_ref, g_ref,
             dq_ref, dk_ref, dv_ref, dsink_ref, dg_ref):
        blk = pl.program_id(0)
        first = blk == 0

        @pl.when(first)
        def _():
            dk_ref[...] = jnp.zeros_like(dk_ref)
            dv_ref[...] = jnp.zeros_like(dv_ref)
            dsink_ref[...] = jnp.zeros_like(dsink_ref)

        o = o_ref[...]
        g = g_ref[...]
        _, r = _rms_fwd(o, g)
        do, dg = _rms_bwd(dn_ref[...], o, g, r)
        _accumulate(dg_ref, dg, first)
        kwin = jnp.concatenate([kp_ref[...], kc_ref[...]], axis=0)
        vwin = jnp.concatenate([vp_ref[...], vc_ref[...]], axis=0)
        lane = _iota((1, 128), 1)
        dsink = jnp.zeros((1, 128), _F32)
        dkwin = jnp.zeros((2 * BLOCK, KV_WIDTH), _F32)
        dvwin = jnp.zeros((2 * BLOCK, KV_WIDTH), _F32)
        dq_halves = []
        for j in range(N_KV_HEADS):
            spread = _head_spread(j)
            kt = _dot(kwin, spread, _NN).astype(_BF16)
            vt = _dot(vwin, spread, _NN).astype(_BF16)
            qs = _stack_heads(q_ref[:, 256 * j:256 * (j + 1)])
            dos = _stack_heads(do[:, 256 * j:256 * (j + 1)]).astype(_BF16)
            saved = p_ref[j]
            p_sink = saved[:, 0:1]
            p = jnp.where(_sink_slot(), 0.0, saved)
            dp = _dot(dos, vt, _NT)
            delta = jnp.sum(p * dp, axis=-1, keepdims=True)
            ds = (p * (dp - delta) * (HEAD_DIM ** -0.5)).astype(_BF16)
            sink_term = p_sink * delta
            for g in range(Q_PER_KV):
                head_sum = jnp.sum(sink_term[BLOCK * g:BLOCK * (g + 1)], axis=0, keepdims=True)
                dsink = dsink - jnp.where(lane == Q_PER_KV * j + g, head_sum, 0.0)
            dvwin = dvwin + _fold_heads(_dot(p, dos, _TN), j)
            dkwin = dkwin + _fold_heads(_dot(ds, qs, _TN), j)
            dq_halves.append(_unstack_heads(_dot(ds, kt, _NN)))
        dq_ref[...] = jnp.concatenate(dq_halves, axis=1)
        dsink_ref[...] += dsink
        prev = pl.ds(pl.multiple_of(jnp.maximum(blk - 1, 0) * BLOCK, BLOCK), BLOCK)
        cur = pl.ds(pl.multiple_of(blk * BLOCK, BLOCK), BLOCK)
        dk_ref[prev, :] += dkwin[:BLOCK]
        dk_ref[cur, :] += dkwin[BLOCK:]
        dv_ref[prev, :] += dvwin[:BLOCK]
        dv_ref[cur, :] += dvwin[BLOCK:]

    cur = lambda i: (i, 0)
    blk_q = pl.BlockSpec((BLOCK, ATTN_WIDTH), cur)
    return _call(body, (L // BLOCK,),
                 _attn_specs() + [blk_q, blk_q, _prob_block(), _whole((1, ATTN_WIDTH))],
                 [blk_q, _whole((L, KV_WIDTH)), _whole((L, KV_WIDTH)), _whole((1, 128)), _whole((1, ATTN_WIDTH))],
                 [_sds((L, ATTN_WIDTH), _F32), _sds((L, KV_WIDTH), _F32), _sds((L, KV_WIDTH), _F32),
                  _sds((1, 128), _F32), _sds((1, ATTN_WIDTH), _F32)],
                 "attn_bwd")(q, k, k, v, v, o, dn, probs, g_attn)


def _out_proj(n_ssm, n_attn, x, w_out, g_post_mix, g_pre_ffn):
    L = x.shape[0]
    tm = _chunk_tile(L)

    def body(ns_ref, na_ref, x_ref, w_ref, g1_ref, g2_ref, merged_ref, mo_ref, h1_ref, hn2_ref):
        merged = jnp.concatenate([ns_ref[...], na_ref[...]], axis=1)
        merged_ref[...] = merged
        mo = _dot(merged, w_ref[...], _NN)
        mo_ref[...] = mo
        n, _ = _rms_fwd(mo, g1_ref[...])
        h1 = x_ref[...] + n
        h1_ref[...] = h1
        hn2, _ = _rms_fwd(h1, g2_ref[...])
        hn2_ref[...] = hn2.astype(_BF16)

    row = _whole((1, D_MODEL))
    return _call(body, (L // tm,),
                 [_chunk_block(L, SSM_WIDTH), _rows(tm, ATTN_WIDTH), _rows(tm, D_MODEL), _whole((D_MODEL, D_MODEL)),
                  row, row],
                 [_rows(tm, D_MODEL)] * 4,
                 [_sds((L, D_MODEL), _BF16), _sds((L, D_MODEL), _F32), _sds((L, D_MODEL), _F32), _sds((L, D_MODEL), _BF16)],
                 "out_proj")(n_ssm, n_attn, x, w_out, g_post_mix, g_pre_ffn)


def _ffn(hn2, h1, target, w_gate_up, w_down, g_pre_ffn, g_post_ffn):
    L = h1.shape[0]
    tm = _tile(L)
    half = FFN_CHUNK

    def body(hn2_ref, h1_ref, tgt_ref, wgu_hbm, wd_hbm, g2_ref, g3_ref,
             act_ref, dgu_ref, dff_ref, dh1_ref, loss_ref, dg3_ref, dg2_ref,
             wgu, wd, gu, sem):
        first = pl.program_id(0) == 0

        @pl.when(first)
        def _():
            c1 = pltpu.make_async_copy(wgu_hbm, wgu, sem.at[0])
            c2 = pltpu.make_async_copy(wd_hbm, wd, sem.at[1])
            c1.start()
            c2.start()
            c1.wait()
            c2.wait()

        hn2 = hn2_ref[...]
        ff = jnp.zeros((tm, D_MODEL), _F32)
        for c in range(D_FF // half):
            gate = _dot(hn2, wgu[half * c:half * (c + 1), :], _NT)
            up = _dot(hn2, wgu[D_FF + half * c:D_FF + half * (c + 1), :], _NT)
            gu[:, half * c:half * (c + 1)] = gate
            gu[:, D_FF + half * c:D_FF + half * (c + 1)] = up
            act = gate * jax.nn.sigmoid(gate) * up
            act_ref[half * c:half * (c + 1), :] = act.T.astype(_BF16)
            ff = ff + _dot(act, wd[half * c:half * (c + 1), :], _NN)
        g3 = g3_ref[...]
        n, r = _rms_fwd(ff, g3)
        h1 = h1_ref[...]
        err = h1 + n - tgt_ref[...]
        loss = 0.5 * jnp.sum(jnp.mean(err * err, axis=-1, keepdims=True), axis=0, keepdims=True)
        _accumulate(loss_ref, jnp.broadcast_to(loss, (1, 128)), first)
        dh2 = err * (1.0 / D_MODEL)
        dff, dg3 = _rms_bwd(dh2, ff, g3, r)
        _accumulate(dg3_ref, dg3, first)
        dffb = dff.astype(_BF16)
        dff_ref[...] = dffb
        dhn2 = jnp.zeros((tm, D_MODEL), _F32)
        for c in range(D_FF // half):
            dact = _dot(dffb, wd[half * c:half * (c + 1), :], _NT)
            gate = gu[:, half * c:half * (c + 1)]
            up = gu[:, D_FF + half * c:D_FF + half * (c + 1)]
            sig = jax.nn.sigmoid(gate)
            silu = gate * sig
            dgate = dact * up * (sig + silu * (1.0 - sig))
            dup = dact * silu
            dgu_ref[half * c:half * (c + 1), :] = dgate.T.astype(_BF16)
            dgu_ref[D_FF + half * c:D_FF + half * (c + 1), :] = dup.T.astype(_BF16)
            dhn2 = dhn2 + _dot(dgate, wgu[half * c:half * (c + 1), :], _NN)
            dhn2 = dhn2 + _dot(dup, wgu[D_FF + half * c:D_FF + half * (c + 1), :], _NN)
        g2 = g2_ref[...]
        _, r2 = _rms_fwd(h1, g2)
        dh1, dg2 = _rms_bwd(dhn2, h1, g2, r2)
        _accumulate(dg2_ref, dg2, first)
        dh1_ref[...] = dh2 + dh1

    row = _whole((1, D_MODEL))
    anyspace = pl.BlockSpec(memory_space=pl.ANY)
    return _call(body, (L // tm,),
                 [_rows(tm, D_MODEL), _rows(tm, D_MODEL), _rows(tm, D_MODEL), anyspace, anyspace, row, row],
                 [pl.BlockSpec((D_FF, tm), lambda i: (0, i)), pl.BlockSpec((2 * D_FF, tm), lambda i: (0, i)),
                  _rows(tm, D_MODEL), _rows(tm, D_MODEL), _whole((1, 128)), row, row],
                 [_sds((D_FF, L), _BF16), _sds((2 * D_FF, L), _BF16), _sds((L, D_MODEL), _BF16),
                  _sds((L, D_MODEL), _F32), _sds((1, 128), _F32), _sds((1, D_MODEL), _F32), _sds((1, D_MODEL), _F32)],
                 "ffn",
                 scratch=[pltpu.VMEM((2 * D_FF, D_MODEL), _BF16), pltpu.VMEM((D_FF, D_MODEL), _BF16),
                          pltpu.VMEM((tm, 2 * D_FF), _F32), pltpu.SemaphoreType.DMA((2,))],
                 )(hn2, h1, target, w_gate_up, w_down, g_pre_ffn, g_post_ffn)


def _out_proj_bwd(dh1, mo, w_out, g_post_mix, tokens=()):
    L = dh1.shape[0]
    tm = _chunk_tile(L)

    def body(dh1_ref, mo_ref, w_ref, g_ref, dmo_ref, dns_ref, dna_ref, dg_ref):
        first = pl.program_id(0) == 0
        mo = mo_ref[...]
        g = g_ref[...]
        _, r = _rms_fwd(mo, g)
        dmo, dg = _rms_bwd(dh1_ref[...], mo, g, r)
        _accumulate(dg_ref, dg, first)
        dmob = dmo.astype(_BF16)
        dmo_ref[...] = dmob
        dmerged = _dot(dmob, w_ref[...], _NT)
        dns_ref[...] = dmerged[:, :SSM_WIDTH]
        dna_ref[...] = dmerged[:, SSM_WIDTH:]

    row = _whole((1, D_MODEL))
    return _call(body, (L // tm,),
                 [_rows(tm, D_MODEL), _rows(tm, D_MODEL), _whole((D_MODEL, D_MODEL)), row],
                 [_rows(tm, D_MODEL), _chunk_block(L, SSM_WIDTH), _rows(tm, ATTN_WIDTH), row],
                 [_sds((L, D_MODEL), _BF16), _sds(_chunk_shape(L, SSM_WIDTH), _F32), _sds((L, ATTN_WIDTH), _F32),
                  _sds((1, D_MODEL), _F32)],
                 "out_proj_bwd", tokens=tokens)(dh1, mo, w_out, g_post_mix)


def _in_proj_bwd(du, dq, dk, dv, cos_t, sin_t, x, dh1, g_pre_mix, w_in, tokens=()):
    L = x.shape[0]
    tm = _chunk_tile(L)

    def body(du_ref, dq_ref, dk_ref, dv_ref, cos_ref, sin_ref, x_ref, dh1_ref, g_ref, w_ref,
             dproj_ref, dx_ref, dg_ref):
        first = pl.program_id(0) == 0
        cos_v, sin_v = cos_ref[...], sin_ref[...]
        dproj = jnp.concatenate([du_ref[...], _rope_transpose(dq_ref[...], cos_v, sin_v),
                                 _rope_transpose(dk_ref[...], cos_v, sin_v), dv_ref[...]], axis=1).astype(_BF16)
        dproj_ref[...] = dproj
        dhn = _dot(dproj, w_ref[...], _NN)
        x = x_ref[...]
        g = g_ref[...]
        _, r = _rms_fwd(x, g)
        dx, dg = _rms_bwd(dhn, x, g, r)
        _accumulate(dg_ref, dg, first)
        dx_ref[...] = dh1_ref[...] + dx

    row = _whole((1, D_MODEL))
    return _call(body, (L // tm,),
                 [_chunk_block(L, SSM_WIDTH), _rows(tm, ATTN_WIDTH), _rows(tm, KV_WIDTH), _rows(tm, KV_WIDTH),
                  _rows(tm, KV_WIDTH), _rows(tm, KV_WIDTH), _rows(tm, D_MODEL), _rows(tm, D_MODEL), row,
                  _whole((IN_WIDTH, D_MODEL))],
                 [_rows(tm, IN_WIDTH), _rows(tm, D_MODEL), row],
                 [_sds((L, IN_WIDTH), _BF16), _sds((L, D_MODEL), _F32), _sds((1, D_MODEL), _F32)],
                 "in_proj_bwd", tokens=tokens)(du, dq, dk, dv, cos_t, sin_t, x, dh1, g_pre_mix, w_in)


def _matmul_nn(a, b, out_dtype, name):
    M, K = a.shape
    N = b.shape[1]
    tm = next(t for t in (704, 512, 256, 128) if M % t == 0)
    tn = N if N <= D_MODEL else next(t for t in (512, 256, 128) if N % t == 0)

    def body(a_ref, b_ref, o_ref):
        o_ref[...] = _dot(a_ref[...], b_ref[...], _NN).astype(out_dtype)

    params = pltpu.CompilerParams(dimension_semantics=("arbitrary", "arbitrary"), vmem_limit_bytes=VMEM_LIMIT)
    return pl.pallas_call(body, grid=(M // tm, N // tn),
                          in_specs=[pl.BlockSpec((tm, K), lambda i, j: (i, 0)),
                                    pl.BlockSpec((K, tn), lambda i, j: (0, j))],
                          out_specs=pl.BlockSpec((tm, tn), lambda i, j: (i, j)),
                          out_shape=_sds((M, N), out_dtype), compiler_params=params, name=name)(a, b)


def _matmul_tn(a, b, out_dtype, name, scale=1.0):
    K, M = a.shape
    N = b.shape[1]
    tm = next(t for t in (512, 256, 128) if M % t == 0)
    tn = N if N <= D_MODEL else next(t for t in (512, 256, 128) if N % t == 0)

    def body(a_ref, b_ref, o_ref):
        acc = _dot(a_ref[...], b_ref[...], _TN)
        o_ref[...] = (acc if scale == 1.0 else acc * scale).astype(out_dtype)

    params = pltpu.CompilerParams(dimension_semantics=("arbitrary", "arbitrary"), vmem_limit_bytes=VMEM_LIMIT)
    return pl.pallas_call(body, grid=(M // tm, N // tn),
                          in_specs=[pl.BlockSpec((K, tm), lambda i, j: (0, i)),
                                    pl.BlockSpec((K, tn), lambda i, j: (0, j))],
                          out_specs=pl.BlockSpec((tm, tn), lambda i, j: (i, j)),
                          out_shape=_sds((M, N), out_dtype), compiler_params=params, name=name)(a, b)


def _local_step(x, pos, target, p, fetch, publish, progress):
    L = x.shape[0]
    T = L // SCAN_CHUNKS
    cos_t, sin_t = _rope_tables(pos.reshape(L, 1))
    w_in, = fetch(("w_in",), None)
    hn, u, q, k, v = _in_proj(x, p["g_pre_mix"], w_in, cos_t, sin_t)

    ssm = {n: _to_2d(n, p[n]) for n in ("ssm_lambda_re", "ssm_lambda_im", "ssm_log_dt", "ssm_b_re", "ssm_b_im",
                                        "ssm_c_re", "ssm_c_im")}
    d_row = p["ssm_d"].reshape(1, SSM_WIDTH)
    a_re, a_im, bt_re, bt_im, ct_re, ct_im = _ssm_prep(
        ssm["ssm_lambda_re"], ssm["ssm_lambda_im"], ssm["ssm_log_dt"], ssm["ssm_b_re"], ssm["ssm_b_im"],
        ssm["ssm_c_re"], ssm["ssm_c_im"])

    u_c = u.reshape(L, SSM_WIDTH)
    cx, carry_re, carry_im = _ssm_core_fwd(u_c, bt_re, bt_im, ct_re, ct_im, a_re, a_im)
    w_glu, = fetch(("w_glu",), cx)
    y, z, n_ssm = _ssm_out(cx, u_c, d_row, w_glu, p["b_glu"], p["g_ssm_out"])

    sinks = p["attn_sinks"].reshape(N_Q_HEADS)
    o, n_attn, probs = _attn_fwd(q, k, v, sinks, p["g_attn_out"])
    w_out, = fetch(("w_out",), n_attn)
    merged, mo, h1, hn2 = _out_proj(n_ssm, n_attn, x, w_out, p["g_post_mix"], p["g_pre_ffn"])
    w_gate_up, w_down = fetch(("w_gate_up", "w_down"), hn2)
    act_t, dgu_t, dff, dh1, loss, dg_post_ffn, dg_pre_ffn = _ffn(
        hn2, h1, target, w_gate_up, w_down, p["g_pre_ffn"], p["g_post_ffn"])
    grads = {"g_post_ffn": dg_post_ffn, "g_pre_ffn": dg_pre_ffn}
    tokens = publish({"w_down": _matmul_nn(act_t, dff, _BF16, "grad_w_down"),
                      "w_gate_up": _matmul_nn(dgu_t, hn2, _BF16, "grad_w_gate_up")})

    dmo, dn_ssm, dn_attn, grads["g_post_mix"] = _out_proj_bwd(dh1, mo, w_out, p["g_post_mix"], tokens)
    grad_w_out = _matmul_tn(merged, dmo, _BF16, "grad_w_out")

    dq, dk, dv, dsink, grads["g_attn_out"] = _attn_bwd(q, k, v, o, dn_attn, probs, p["g_attn_out"])
    grads["attn_sinks"] = dsink

    gy, dz, dy, dud, grads["g_ssm_out"], grads["b_glu"], dd = _ssm_out_bwd(
        dn_ssm, y, z, u_c, d_row, w_glu, p["g_ssm_out"], progress(dmo))
    tokens = publish({"w_out": grad_w_out, "w_glu": _matmul_tn(dz, gy, _BF16, "grad_w_glu")})
    du_c, dct_re, dct_im, dbt_re, dbt_im, da_re, da_im = _ssm_core_bwd(
        u_c, dy, dud, carry_re, carry_im, bt_re, bt_im, ct_re, ct_im, a_re, a_im, tokens)
    ssm_pack = _ssm_param_bwd(
        da_re, da_im, dbt_re, dbt_im, dct_re, dct_im,
        ssm["ssm_lambda_re"], ssm["ssm_lambda_im"], ssm["ssm_log_dt"], ssm["ssm_b_re"], ssm["ssm_b_im"],
        dd.reshape(SSM_GROUPS, SSM_GROUP))
    grads.update(ssm_pack=ssm_pack, loss=loss)
    publish(grads)

    du = du_c.reshape(_chunk_shape(L, SSM_WIDTH))
    dproj, grad_x, g_pre_mix = _in_proj_bwd(du, dq, dk, dv, cos_t, sin_t, x, dh1, p["g_pre_mix"], w_in, [ssm_pack])
    publish({"g_pre_mix": g_pre_mix, "w_in": _matmul_tn(dproj, hn, _BF16, "grad_w_in")})
    return grad_x


_MESH = pl.DeviceIdType.MESH
_PEERS = N_DEV - 1


def _mesh_pos():
    return lax.axis_index("x"), lax.axis_index("y"), lax.axis_index("c")


def _dev_index(px, py, pc):
    return 4 * px + 2 * py + pc


def _peer(x, y, c, r):
    return (x ^ ((r >> 2) & 1), y ^ ((r >> 1) & 1), c ^ (r & 1))


def _sequencer_exchange(sources, blocked, name, collective_id):
    n = len(sources)
    flags = blocked

    def body(*refs):
        srcs, zones = refs[:n], refs[n:2 * n]
        send_sems, recv_sems, local_sems = refs[2 * n:]
        x, y, c = _mesh_pos()
        me = _dev_index(x, y, c)
        barrier = pltpu.get_barrier_semaphore()
        for r in range(1, N_DEV):
            pl.semaphore_signal(barrier, inc=1, device_id=_peer(x, y, c, r), device_id_type=_MESH)
        pl.semaphore_wait(barrier, _PEERS)
        local, sends, recvs = [], [], []
        for w in range(n):
            cp = pltpu.make_async_copy(srcs[w].at[me] if flags[w] else srcs[w], zones[w].at[me], local_sems.at[w])
            cp.start()
            local.append(cp)
            for r in range(1, N_DEV):
                peer = _peer(x, y, c, r)
                idx = _dev_index(*peer)
                k = _PEERS * w + r - 1
                src = srcs[w].at[idx] if flags[w] else srcs[w]
                send = pltpu.make_async_remote_copy(
                    src_ref=src, dst_ref=zones[w].at[me], send_sem=send_sems.at[k], recv_sem=recv_sems.at[k],
                    device_id=peer, device_id_type=_MESH)
                send.start()
                sends.append(send)
                recvs.append(pltpu.make_async_remote_copy(
                    src_ref=src, dst_ref=zones[w].at[idx], send_sem=send_sems.at[k], recv_sem=recv_sems.at[k],
                    device_id=peer, device_id_type=_MESH))
        for cp in recvs:
            cp.wait_recv()
        for cp in sends:
            cp.wait_send()
        for cp in local:
            cp.wait()

    return pl.kernel(
        body, name=name,
        out_type=[_sds((N_DEV,) + (s.shape[1:] if f else s.shape), s.dtype) for s, f in zip(sources, flags)],
        mesh=plsc.ScalarSubcoreMesh(axis_name="sequencer", num_cores=1),
        scratch_types=[pltpu.SemaphoreType.DMA((_PEERS * n,)), pltpu.SemaphoreType.DMA((_PEERS * n,)),
                       pltpu.SemaphoreType.DMA((n,))],
        compiler_params=pltpu.CompilerParams(collective_id=collective_id),
    )(*sources)


def _sequencer_gather(shards, name, collective_id):
    n = len(shards)
    fan = 4

    def body(*refs):
        srcs, zones = refs[:n], refs[n:2 * n]
        send_sems, recv_sems, local_sems = refs[2 * n:]
        x, y, c = _mesh_pos()
        me, sibling = (x, y, c), (x, y, 1 - c)
        chips = [(1 - x, y), (x, 1 - y), (1 - x, 1 - y)]
        barrier = pltpu.get_barrier_semaphore()
        for peer in [sibling] + [(*chip, c) for chip in chips]:
            pl.semaphore_signal(barrier, inc=1, device_id=peer, device_id_type=_MESH)
        pl.semaphore_wait(barrier, fan)

        def copy(w, k, block, to, src=None):
            slot = zones[w].at[_dev_index(*block)]
            return pltpu.make_async_remote_copy(
                src_ref=slot if src is None else src, dst_ref=slot,
                send_sem=send_sems.at[_PEERS * w + k], recv_sem=recv_sems.at[_PEERS * w + k],
                device_id=to, device_id_type=_MESH)

        mine, first, passed = [], [], []
        for w in range(n):
            cp = pltpu.make_async_copy(srcs[w], zones[w].at[_dev_index(*me)], local_sems.at[w])
            cp.start()
            mine.append(cp)
            sends = [copy(w, 0, me, sibling, src=srcs[w])]
            sends += [copy(w, 1 + j, me, (*chip, c), src=srcs[w]) for j, chip in enumerate(chips)]
            for cp in sends:
                cp.start()
            first += sends
        for w in range(n):
            for j, chip in enumerate(chips):
                copy(w, 1 + j, (*chip, c), me).wait_recv()
                cp = copy(w, fan + j, (*chip, c), sibling)
                cp.start()
                passed.append(cp)
        for w in range(n):
            copy(w, 0, sibling, me).wait_recv()
            for j, chip in enumerate(chips):
                copy(w, fan + j, (*chip, 1 - c), me).wait_recv()
        for cp in first + passed:
            cp.wait_send()
        for cp in mine:
            cp.wait()

    return pl.kernel(
        body, name=name, out_type=[_sds((N_DEV,) + s.shape, s.dtype) for s in shards],
        mesh=plsc.ScalarSubcoreMesh(axis_name="sequencer", num_cores=1),
        scratch_types=[pltpu.SemaphoreType.DMA((_PEERS * n,)), pltpu.SemaphoreType.DMA((_PEERS * n,)),
                       pltpu.SemaphoreType.DMA((n,))],
        compiler_params=pltpu.CompilerParams(collective_id=collective_id),
    )(*shards)


N_CHIPS = N_DEV // 2


def _sequencer_pair_exchange(sources, name, collective_id):
    n = len(sources)

    def body(*refs):
        srcs, zones = refs[:n], refs[n:2 * n]
        send_sems, recv_sems = refs[2 * n:]
        x, y, c = _mesh_pos()
        sibling = (x, y, 1 - c)
        barrier = pltpu.get_barrier_semaphore()
        pl.semaphore_signal(barrier, inc=1, device_id=sibling, device_id_type=_MESH)
        pl.semaphore_wait(barrier, 1)
        copies = []
        for w in range(n):
            for j in range(N_CHIPS):
                k = N_CHIPS * w + j
                cp = pltpu.make_async_remote_copy(
                    src_ref=srcs[w].at[2 * j + 1 - c], dst_ref=zones[w].at[j],
                    send_sem=send_sems.at[k], recv_sem=recv_sems.at[k], device_id=sibling, device_id_type=_MESH)
                cp.start()
                copies.append(cp)
        for cp in copies:
            cp.wait_recv()
        for cp in copies:
            cp.wait_send()

    return pl.kernel(
        body, name=name, out_type=[_sds((N_CHIPS,) + s.shape[1:], s.dtype) for s in sources],
        mesh=plsc.ScalarSubcoreMesh(axis_name="sequencer", num_cores=1),
        scratch_types=[pltpu.SemaphoreType.DMA((N_CHIPS * n,)), pltpu.SemaphoreType.DMA((N_CHIPS * n,))],
        compiler_params=pltpu.CompilerParams(collective_id=collective_id),
    )(*sources)


def _pair_sum(source, received, core, name, tokens=()):
    _, rows, cols = source.shape
    tr = _row_tile(rows)
    n_tok = len(tokens)

    def body(core_ref, s_ref, r_ref, *rest):
        o_ref = rest[n_tok]
        o_ref[...] = (s_ref[...].astype(_F32) + r_ref[...].astype(_F32)).astype(o_ref.dtype)

    quarter = pl.BlockSpec((N_CHIPS, tr, cols), lambda i, core_ref: (0, i, 0))
    mine = pl.BlockSpec((N_CHIPS, None, tr, cols), lambda i, core_ref: (0, core_ref[0], i, 0))
    spec = pltpu.PrefetchScalarGridSpec(
        num_scalar_prefetch=1, grid=(rows // tr,),
        in_specs=[mine, quarter] + [pl.BlockSpec(memory_space=pl.ANY)] * n_tok, out_specs=quarter)
    params = pltpu.CompilerParams(dimension_semantics=("arbitrary",), vmem_limit_bytes=VMEM_LIMIT)
    return pl.pallas_call(body, grid_spec=spec, out_shape=_sds((N_CHIPS, rows, cols), source.dtype),
                          compiler_params=params, name=name)(
        core, source.reshape(N_CHIPS, 2, rows, cols), received, *tokens)


def _sequencer_chip_exchange(partials, name, collective_id):
    n = len(partials)
    others = N_CHIPS - 1

    def body(*refs):
        srcs, zones = refs[:n], refs[n:2 * n]
        send_sems, recv_sems, local_sems = refs[2 * n:]
        x, y, c = _mesh_pos()
        mine = 2 * x + y
        peers = [(x ^ (r >> 1), y ^ (r & 1), c) for r in range(1, N_CHIPS)]
        barrier = pltpu.get_barrier_semaphore()
        for peer in peers:
            pl.semaphore_signal(barrier, inc=1, device_id=peer, device_id_type=_MESH)
        pl.semaphore_wait(barrier, others)
        local, sends, recvs = [], [], []
        for w in range(n):
            cp = pltpu.make_async_copy(srcs[w].at[mine], zones[w].at[mine], local_sems.at[w])
            cp.start()
            local.append(cp)
            for r, peer in enumerate(peers):
                theirs = 2 * peer[0] + peer[1]
                k = others * w + r
                send = pltpu.make_async_remote_copy(
                    src_ref=srcs[w].at[theirs], dst_ref=zones[w].at[mine],
                    send_sem=send_sems.at[k], recv_sem=recv_sems.at[k], device_id=peer, device_id_type=_MESH)
                send.start()
                sends.append(send)
                recvs.append(pltpu.make_async_remote_copy(
                    src_ref=srcs[w].at[theirs], dst_ref=zones[w].at[theirs],
                    send_sem=send_sems.at[k], recv_sem=recv_sems.at[k], device_id=peer, device_id_type=_MESH))
        for cp in recvs:
            cp.wait_recv()
        for cp in sends:
            cp.wait_send()
        for cp in local:
            cp.wait()

    return pl.kernel(
        body, name=name, out_type=[_sds(s.shape, s.dtype) for s in partials],
        mesh=plsc.ScalarSubcoreMesh(axis_name="sequencer", num_cores=1),
        scratch_types=[pltpu.SemaphoreType.DMA((others * n,)), pltpu.SemaphoreType.DMA((others * n,)),
                       pltpu.SemaphoreType.DMA((n,))],
        compiler_params=pltpu.CompilerParams(collective_id=collective_id),
    )(*partials)


def _row_tile(rows):
    return next(t for t in range(min(rows, 256), 0, -16) if rows % t == 0)


def _sum_parts(parts, name, tokens=()):
    _, rows, cols = parts.shape
    tr = _row_tile(rows)

    def body(p_ref, g_ref):
        g = p_ref[0].astype(_F32)
        for s in range(1, N_DEV):
            g = g + p_ref[s].astype(_F32)
        g_ref[...] = g

    return _call(body, (rows // tr,), [pl.BlockSpec((N_DEV, tr, cols), lambda i: (0, i, 0))],
                 _rows(tr, cols), _sds((rows, cols), _F32), name, tokens=tokens)(parts)


def _adam_update(g, w, m, v):
    new_m = ADAM_B1 * m + (1.0 - ADAM_B1) * g
    new_v = ADAM_B2 * v + (1.0 - ADAM_B2) * (g * g)
    m_hat = new_m / (1.0 - ADAM_B1 ** ADAM_STEP)
    v_hat = new_v / (1.0 - ADAM_B2 ** ADAM_STEP)
    return -ADAM_LR * (m_hat / (jnp.sqrt(v_hat) + ADAM_EPS) + ADAM_WD * w), new_m, new_v


def _adamw_small(parts, items, sums, name, tokens=()):
    n_p, n_i = len(parts), len(items)

    def body(*refs):
        p_refs, state, outs = refs[:n_p], refs[n_p:n_p + 3 * n_i], refs[n_p + 3 * n_i:]

        def total(part, rows, cols):
            shift = cols.start % _LANES
            window = slice(cols.start - shift, cols.start - shift + _LANES) if shift else cols
            n_rows = rows.stop - rows.start
            narrow = p_refs[part].dtype.itemsize < 4 and n_rows % _PACK_TILE
            tile = slice(rows.start, rows.start + _PACK_TILE) if narrow else rows
            g = p_refs[part][0, tile, window].astype(_F32)
            for s in range(1, N_DEV):
                g = g + p_refs[part][s, tile, window].astype(_F32)
            g = g[:n_rows] if narrow else g
            return pltpu.roll(g, _LANES - shift, 1)[:, :cols.stop - cols.start] if shift else g

        for i, (part, rows, cols, _, _, _) in enumerate(items):
            g = total(part, rows, cols)
            w_ref, m_ref, v_ref = state[3 * i:3 * i + 3]
            delta, new_m, new_v = _adam_update(g, w_ref[...], m_ref[...], v_ref[...])
            outs[4 * i][...] = g
            outs[4 * i + 1][...] = delta
            outs[4 * i + 2][...] = new_m
            outs[4 * i + 3][...] = new_v
        for j, (part, rows, cols) in enumerate(sums):
            outs[4 * n_i + j][...] = total(part, rows, cols)

    ins = list(parts) + [a for item in items for a in item[3:]]
    out_shapes = [item[3].shape for item in items for _ in range(4)]
    out_shapes += [(rows.stop - rows.start, cols.stop - cols.start) for _, rows, cols in sums]
    out = _call(body, (1,), [_whole(a.shape) for a in ins], [_whole(s) for s in out_shapes],
                [_sds(s, _F32) for s in out_shapes], name, tokens=tokens)(*ins)
    return [out[4 * i:4 * i + 4] for i in range(n_i)], out[4 * n_i:]


def _adamw(parts, w, m, v, name, tokens=()):
    rows, cols = w.shape
    tr = _row_tile(rows)
    n_parts = parts.shape[0]

    def body(p_ref, w_ref, m_ref, v_ref, g_ref, d_ref, nm_ref, nv_ref):
        g = p_ref[0].astype(_F32)
        for s in range(1, n_parts):
            g = g + p_ref[s].astype(_F32)
        new_m = ADAM_B1 * m_ref[...] + (1.0 - ADAM_B1) * g
        new_v = ADAM_B2 * v_ref[...] + (1.0 - ADAM_B2) * (g * g)
        m_hat = new_m / (1.0 - ADAM_B1 ** ADAM_STEP)
        v_hat = new_v / (1.0 - ADAM_B2 ** ADAM_STEP)
        g_ref[...] = g
        d_ref[...] = -ADAM_LR * (m_hat / (jnp.sqrt(v_hat) + ADAM_EPS) + ADAM_WD * w_ref[...])
        nm_ref[...] = new_m
        nv_ref[...] = new_v

    blk = _rows(tr, cols)
    return _call(body, (rows // tr,),
                 [pl.BlockSpec((n_parts, tr, cols), lambda i: (0, i, 0)), blk, blk, blk],
                 [blk] * 4, [_sds((rows, cols), _F32)] * 4, name, tokens=tokens)(parts, w, m, v)


_SMALL = ("g_pre_mix", "ssm_lambda_re", "ssm_lambda_im", "ssm_log_dt", "ssm_b_re", "ssm_b_im",
          "ssm_c_re", "ssm_c_im", "ssm_d", "b_glu", "attn_sinks", "g_ssm_out", "g_attn_out",
          "g_post_mix", "g_pre_ffn", "g_post_ffn")
_BIG = ("w_in", "w_glu", "w_out", "w_gate_up", "w_down")
_WEIGHTS = ("g_pre_mix", "w_in", "ssm_lambda_re", "ssm_lambda_im", "ssm_log_dt", "ssm_b_re", "ssm_b_im",
            "ssm_c_re", "ssm_c_im", "ssm_d", "w_glu", "b_glu", "attn_sinks", "g_ssm_out", "g_attn_out",
            "w_out", "g_post_mix", "g_pre_ffn", "w_gate_up", "w_down", "g_post_ffn")
_LANES = 128


_SHAPE_2D = {
    "g_pre_mix": (1, D_MODEL), "ssm_lambda_re": (SSM_GROUPS, SSM_STATE), "ssm_lambda_im": (SSM_GROUPS, SSM_STATE),
    "ssm_log_dt": (1, SSM_GROUPS), "ssm_b_re": (SSM_WIDTH, SSM_STATE), "ssm_b_im": (SSM_WIDTH, SSM_STATE),
    "ssm_c_re": (SSM_WIDTH, SSM_STATE), "ssm_c_im": (SSM_WIDTH, SSM_STATE), "ssm_d": (SSM_GROUPS, SSM_GROUP),
    "b_glu": (1, 2 * SSM_WIDTH), "attn_sinks": (1, N_Q_HEADS), "g_ssm_out": (1, SSM_WIDTH),
    "g_attn_out": (1, ATTN_WIDTH), "g_post_mix": (1, D_MODEL), "g_pre_ffn": (1, D_MODEL), "g_post_ffn": (1, D_MODEL)}
_ROW_WIDTH = {"g_pre_mix": D_MODEL, "b_glu": 2 * SSM_WIDTH, "attn_sinks": _LANES, "g_ssm_out": SSM_WIDTH,
              "g_attn_out": ATTN_WIDTH, "g_post_mix": D_MODEL, "g_pre_ffn": D_MODEL, "g_post_ffn": D_MODEL,
              "loss": _LANES}
_PER_GROUP_TRANSPOSED = ("ssm_b_re", "ssm_b_im")


def _to_2d(name, a):
    if name in _PER_GROUP_TRANSPOSED:
        a = a.reshape(SSM_GROUPS, SSM_STATE, SSM_GROUP).transpose(0, 2, 1)
    return a.reshape(_SHAPE_2D[name])


def _from_2d(name, a, shape):
    if name in _PER_GROUP_TRANSPOSED:
        a = a.reshape(SSM_GROUPS, SSM_GROUP, SSM_STATE).transpose(0, 2, 1)
    return a.reshape(shape)


def _row_slots(names):
    slots, row, col = {}, 0, 0
    for n in names:
        width = _ROW_WIDTH[n]
        if col + width > D_MODEL:
            row, col = row + 1, 0
        slots[n] = (row, col, width)
        col += width
    return slots


def _stack_rows(named, slots):
    n_rows = -(-(max(r for r, _, _ in slots.values()) + 1) // 8) * 8
    lines = []
    for r in range(n_rows):
        pieces = [named[n] for n, (row, _, _) in slots.items() if row == r]
        used = sum(p.shape[1] for p in pieces)
        if used < D_MODEL:
            pieces.append(jnp.zeros((1, D_MODEL - used), _F32))
        lines.append(jnp.concatenate(pieces, axis=1) if len(pieces) > 1 else pieces[0])
    return jnp.concatenate(lines, axis=0)


def kernel(x, positions, g_pre_mix, w_in, ssm_lambda_re, ssm_lambda_im, ssm_log_dt, ssm_b_re, ssm_b_im, ssm_c_re, ssm_c_im, ssm_d, w_glu, b_glu, attn_sinks, g_ssm_out, g_attn_out, w_out, g_post_mix, g_pre_ffn, w_gate_up, w_down, g_post_ffn, loss_target, m_g_pre_mix, m_w_in, m_ssm_lambda_re, m_ssm_lambda_im, m_ssm_log_dt, m_ssm_b_re, m_ssm_b_im, m_ssm_c_re, m_ssm_c_im, m_ssm_d, m_w_glu, m_b_glu, m_attn_sinks, m_g_ssm_out, m_g_attn_out, m_w_out, m_g_post_mix, m_g_pre_ffn, m_w_gate_up, m_w_down, m_g_post_ffn, v_g_pre_mix, v_w_in, v_ssm_lambda_re, v_ssm_lambda_im, v_ssm_log_dt, v_ssm_b_re, v_ssm_b_im, v_ssm_c_re, v_ssm_c_im, v_ssm_d, v_w_glu, v_b_glu, v_attn_sinks, v_g_ssm_out, v_g_attn_out, v_w_out, v_g_post_mix, v_g_pre_ffn, v_w_gate_up, v_w_down, v_g_post_ffn):
    w = dict(g_pre_mix=g_pre_mix, w_in=w_in, ssm_lambda_re=ssm_lambda_re, ssm_lambda_im=ssm_lambda_im,
             ssm_log_dt=ssm_log_dt, ssm_b_re=ssm_b_re, ssm_b_im=ssm_b_im, ssm_c_re=ssm_c_re, ssm_c_im=ssm_c_im,
             ssm_d=ssm_d, w_glu=w_glu, b_glu=b_glu, attn_sinks=attn_sinks, g_ssm_out=g_ssm_out,
             g_attn_out=g_attn_out, w_out=w_out, g_post_mix=g_post_mix, g_pre_ffn=g_pre_ffn,
             w_gate_up=w_gate_up, w_down=w_down, g_post_ffn=g_post_ffn)
    m = dict(g_pre_mix=m_g_pre_mix, w_in=m_w_in, ssm_lambda_re=m_ssm_lambda_re, ssm_lambda_im=m_ssm_lambda_im,
             ssm_log_dt=m_ssm_log_dt, ssm_b_re=m_ssm_b_re, ssm_b_im=m_ssm_b_im, ssm_c_re=m_ssm_c_re,
             ssm_c_im=m_ssm_c_im, ssm_d=m_ssm_d, w_glu=m_w_glu, b_glu=m_b_glu, attn_sinks=m_attn_sinks,
             g_ssm_out=m_g_ssm_out, g_attn_out=m_g_attn_out, w_out=m_w_out, g_post_mix=m_g_post_mix,
             g_pre_ffn=m_g_pre_ffn, w_gate_up=m_w_gate_up, w_down=m_w_down, g_post_ffn=m_g_post_ffn)
    v = dict(g_pre_mix=v_g_pre_mix, w_in=v_w_in, ssm_lambda_re=v_ssm_lambda_re, ssm_lambda_im=v_ssm_lambda_im,
             ssm_log_dt=v_ssm_log_dt, ssm_b_re=v_ssm_b_re, ssm_b_im=v_ssm_b_im, ssm_c_re=v_ssm_c_re,
             ssm_c_im=v_ssm_c_im, ssm_d=v_ssm_d, w_glu=v_w_glu, b_glu=v_b_glu, attn_sinks=v_attn_sinks,
             g_ssm_out=v_g_ssm_out, g_attn_out=v_g_attn_out, w_out=v_w_out, g_post_mix=v_g_post_mix,
             g_pre_ffn=v_g_pre_ffn, w_gate_up=v_w_gate_up, w_down=v_w_down, g_post_ffn=v_g_post_ffn)

    transposed = ("w_in", "w_glu", "w_gate_up")
    native_transposed = ("w_in", "w_gate_up")
    shard = {n: (w[n][0].T if n in transposed else w[n][0]).astype(_BF16) for n in _BIG}
    gathered = {}
    for cid, names in enumerate((("w_in",), ("w_glu", "w_out"), ("w_gate_up", "w_down")), start=1):
        lands = _sequencer_gather([shard[n] for n in names], "gather_" + names[0], cid)
        gathered.update({n: a.reshape(-1, a.shape[2]) for n, a in zip(names, lands)})

    def fetch(names, after):
        del after
        return [gathered[n] for n in names]

    sent = []
    ids = iter(range(4, 16))
    two_step = {}

    def publish(named):
        big = [n for n in named if n in _BIG]
        if set(big) == {"w_gate_up", "w_down"}:
            blocks = [named[n].reshape(N_DEV, -1, named[n].shape[1]) for n in big]
            two_step.update(names=big, blocks=blocks,
                            received=_sequencer_pair_exchange(blocks, "grads_pair", next(ids)))
            return [named[n] for n in big]
        rows = [n for n in named if n in _ROW_WIDTH]
        plain = [n for n in named if n not in big + rows]
        sources = [named[n].reshape(N_DEV, -1, named[n].shape[1]) for n in big]
        slots = _row_slots(rows)
        if rows:
            sources.append(_stack_rows(named, slots))
        sources += [named[n] for n in plain]
        flags = [True] * len(big) + [False] * (len(sources) - len(big))
        cid = next(ids)
        if big:
            lands = _sequencer_exchange(sources, flags, "grads_%d" % cid, cid)
        else:
            lands = _sequencer_gather(sources, "grads_%d" % cid, cid)
        sent.append((big, slots, plain, lands))
        return [named[n] for n in big]

    def progress(after):
        core = lax.axis_index("c").astype(jnp.int32).reshape(1)
        partials = [_pair_sum(b, r, core, "pair_sum_" + n, [after])
                    for n, b, r in zip(two_step["names"], two_step["blocks"], two_step["received"])]
        sent.append((two_step["names"], {}, [], _sequencer_chip_exchange(partials, "grads_chips", next(ids))))
        return partials

    p = {n: w[n] for n in _SMALL}
    grad_x = _local_step(x[0], positions[0], loss_target[0], p, fetch, publish, progress)

    state = {n: [_to_2d(n, a) for a in (w[n], m[n], v[n])] for n in _SMALL}
    result = {}
    total_loss = None
    chain = []
    for big, slots, plain, lands in sent:
        lands = list(lands)
        after = list(chain)
        for name in big:
            part = lands.pop(0)
            if name in native_transposed:
                updated = _adamw(part, w[name][0].T, m[name][0].T, v[name][0].T, "adamw_" + name, after)
                result[name] = [a.T[None] for a in updated]
                chain.append(updated[3])
                continue
            if name in transposed:
                part = _sum_parts(part, "sum_" + name, after).T[None]
            updated = _adamw(part, w[name][0], m[name][0], v[name][0], "adamw_" + name, after)
            result[name] = [a[None] for a in updated]
            chain.append(updated[3])
        parts, items, sums, names = [], [], [], []
        if slots:
            parts.append(lands.pop(0))
            for name, (row, col, _) in slots.items():
                if name == "loss":
                    sums.append((0, slice(row, row + 1), slice(col, col + _LANES)))
                else:
                    items.append((0, slice(row, row + 1), slice(col, col + _SHAPE_2D[name][1]), *state[name]))
                    names.append(name)
        for name in plain:
            packed = _SSM_PACK if name == "ssm_pack" else {name: (0, _SHAPE_2D[name][0], 0, _SHAPE_2D[name][1])}
            for member, (first, rows_n, lane, cols_n) in packed.items():
                items.append((len(parts), slice(first, first + rows_n), slice(lane, lane + cols_n), *state[member]))
                names.append(member)
            parts.append(lands.pop(0))
        if items:
            updated, summed = _adamw_small(parts, items, sums, "adamw_small_" + names[0], after)
            chain.append(updated[0][3])
            result.update(dict(zip(names, updated)))
            if summed:
                total_loss = summed[0][0, 0]

    out = [total_loss, grad_x[None]]
    for kind in range(4):
        out += [_from_2d(n, result[n][kind], w[n].shape) for n in _WEIGHTS]
    return tuple(out)
```

```python
import math

import numpy as np
import jax
import jax.numpy as jnp
from jax import lax
from jax.experimental import pallas as pl
from jax.experimental.pallas import tpu as pltpu
from jax.experimental.pallas import tpu_sc as plsc

D_MODEL = 1024
SSM_WIDTH = 512
SSM_GROUP = 16
SSM_GROUPS = 32
SSM_STATE = 64
N_STATE = SSM_GROUPS * SSM_STATE
ATTN_WIDTH = 512
HEAD_DIM = 64
N_Q_HEADS = 8
N_KV_HEADS = 2
Q_PER_KV = 4
KV_WIDTH = 128
IN_WIDTH = 1280
BLOCK = 128
ROPE_DIM = 16
ROPE_THETA = 500000.0
D_FF = 2816
NORM_EPS = 1e-6
MASK_VALUE = -1e30
ADAM_LR = 0.001
ADAM_B1 = 0.9
ADAM_B2 = 0.999
ADAM_EPS = 1e-08
ADAM_WD = 0.01
ADAM_STEP = 10

N_DEV = 8
SCAN_CHUNKS = 8
SCAN_UNROLL = 8
FFN_CHUNK = 2816
TOKEN_TILE = 256
VMEM_LIMIT = 56 * 1024 * 1024

_F32 = jnp.float32
_BF16 = jnp.bfloat16
_MXU = jnp.bfloat16

_NN = ((1,), (0,))
_NT = ((1,), (1,))
_TN = ((0,), (0,))


def _dot(a, b, dims):
    return lax.dot_general(a.astype(_MXU), b.astype(_MXU), (dims, ((), ())),
                           preferred_element_type=_F32)


def _dot_exact(a, b, dims):
    return lax.dot_general(a.astype(_F32), b.astype(_F32), (dims, ((), ())),
                           precision=lax.Precision.HIGHEST, preferred_element_type=_F32)


def _iota(shape, dim):
    return lax.broadcasted_iota(jnp.int32, shape, dim)


def _rms_fwd(x, g):
    r = lax.rsqrt(jnp.mean(x * x, axis=-1, keepdims=True) + NORM_EPS)
    return x * r * g, r


def _rms_bwd(dy, x, g, r):
    a = dy * g
    xn = x * r
    dx = r * (a - xn * jnp.mean(a * xn, axis=-1, keepdims=True))
    dg = jnp.sum(dy * xn, axis=0, keepdims=True)
    return dx, dg


def _call(body, grid, in_specs, out_specs, out_shape, name, scratch=(), tokens=()):
    params = pltpu.CompilerParams(dimension_semantics=("arbitrary",) * len(grid),
                                  vmem_limit_bytes=VMEM_LIMIT)
    n_in, n_tok = len(in_specs), len(tokens)

    def run(*refs):
        return body(*refs[:n_in], *refs[n_in + n_tok:])

    call = pl.pallas_call(run, grid=grid,
                          in_specs=list(in_specs) + [pl.BlockSpec(memory_space=pl.ANY)] * n_tok,
                          out_specs=out_specs, out_shape=out_shape, scratch_shapes=list(scratch),
                          compiler_params=params, name=name)
    return lambda *args: call(*args, *tokens)


def _rows(tm, n):
    return pl.BlockSpec((tm, n), lambda i: (i, 0))


def _whole(shape):
    nd = len(shape)
    return pl.BlockSpec(shape, lambda i: (0,) * nd)


def _sds(shape, dtype):
    return jax.ShapeDtypeStruct(shape, dtype)


def _tile(L):
    return min(TOKEN_TILE, L)


def _chunk_tile(L):
    return L // SCAN_CHUNKS


def _chunk_block(L, n):
    return pl.BlockSpec((_chunk_tile(L), n), lambda i: (0, i))


def _chunk_shape(L, n):
    return (_chunk_tile(L), SCAN_CHUNKS * n)


def _accumulate(ref, val, first):
    @pl.when(first)
    def _():
        ref[...] = val

    @pl.when(jnp.logical_not(first))
    def _():
        ref[...] += val


def _rope_rows():
    half = ROPE_DIM // 2
    inv = (np.float32(ROPE_THETA) ** (-np.arange(half, dtype=np.float32) * np.float32(2.0) / np.float32(ROPE_DIM))).astype(np.float32)
    col = np.arange(KV_WIDTH) % HEAD_DIM
    freq = np.where(col < ROPE_DIM, inv[col % half], 0.0).astype(np.float32)
    sign = np.where(col < half, -1.0, np.where(col < ROPE_DIM, 1.0, 0.0)).astype(np.float32)
    return freq[None, :], sign[None, :]


def _rope_tables(pos_col):
    L = pos_col.shape[0]
    tm = _tile(L)
    freq, sign = _rope_rows()

    def body(pos_ref, freq_ref, sign_ref, cos_ref, sin_ref):
        ang = pos_ref[...].astype(_F32) * freq_ref[...]
        cos_ref[...] = jnp.cos(ang)
        sin_ref[...] = jnp.sin(ang) * sign_ref[...]

    return _call(body, (L // tm,),
                 [_rows(tm, 1), _whole((1, KV_WIDTH)), _whole((1, KV_WIDTH))],
                 [_rows(tm, KV_WIDTH), _rows(tm, KV_WIDTH)],
                 [_sds((L, KV_WIDTH), _F32)] * 2, "rope_tables")(pos_col, jnp.asarray(freq), jnp.asarray(sign))


def _widen(t, width):
    return t if width == KV_WIDTH else jnp.concatenate([t] * (width // KV_WIDTH), axis=1)


def _rope_partner(t):
    w = t.shape[1]
    in_head = _iota((1, w), 1) & (HEAD_DIM - 1)
    second = jnp.where(in_head < ROPE_DIM, pltpu.roll(t, ROPE_DIM // 2, 1), 0.0)
    return jnp.where(in_head < ROPE_DIM // 2, pltpu.roll(t, w - ROPE_DIM // 2, 1), second)


def _rope_apply(t, cos_t, sin_t):
    w = t.shape[1]
    return t * _widen(cos_t, w) + _rope_partner(t) * _widen(sin_t, w)


def _rope_transpose(dt, cos_t, sin_t):
    w = dt.shape[1]
    return dt * _widen(cos_t, w) + _rope_partner(dt * _widen(sin_t, w))


def _in_proj(x, g_pre_mix, w_in, cos_t, sin_t):
    L = x.shape[0]
    tm = _chunk_tile(L)

    def body(x_ref, g_ref, w_ref, cos_ref, sin_ref, hn_ref, u_ref, q_ref, k_ref, v_ref):
        hn, _ = _rms_fwd(x_ref[...], g_ref[...])
        hn = hn.astype(_BF16)
        hn_ref[...] = hn
        proj = _dot(hn, w_ref[...], _NT)
        u_ref[...] = proj[:, :SSM_WIDTH]
        q = proj[:, SSM_WIDTH:SSM_WIDTH + ATTN_WIDTH]
        k = proj[:, SSM_WIDTH + ATTN_WIDTH:SSM_WIDTH + ATTN_WIDTH + KV_WIDTH]
        cos_v, sin_v = cos_ref[...], sin_ref[...]
        q_ref[...] = _rope_apply(q, cos_v, sin_v).astype(_BF16)
        k_ref[...] = _rope_apply(k, cos_v, sin_v).astype(_BF16)
        v_ref[...] = proj[:, SSM_WIDTH + ATTN_WIDTH + KV_WIDTH:].astype(_BF16)

    return _call(body, (L // tm,),
                 [_rows(tm, D_MODEL), _whole((1, D_MODEL)), _whole((IN_WIDTH, D_MODEL)),
                  _rows(tm, KV_WIDTH), _rows(tm, KV_WIDTH)],
                 [_rows(tm, D_MODEL), _chunk_block(L, SSM_WIDTH), _rows(tm, ATTN_WIDTH),
                  _rows(tm, KV_WIDTH), _rows(tm, KV_WIDTH)],
                 [_sds((L, D_MODEL), _BF16), _sds(_chunk_shape(L, SSM_WIDTH), _F32), _sds((L, ATTN_WIDTH), _BF16),
                  _sds((L, KV_WIDTH), _BF16), _sds((L, KV_WIDTH), _BF16)],
                 "in_proj")(x, g_pre_mix, w_in, cos_t, sin_t)


def _s5_discretize(lam_re, lam_im, log_dt):
    lr = jnp.minimum(lam_re, -1e-4)
    li = lam_im
    dt = jnp.exp(log_dt)
    mag = jnp.exp(lr * dt)
    ar = mag * jnp.cos(li * dt)
    ai = mag * jnp.sin(li * dt)
    den = lr * lr + li * li
    fr = ((ar - 1.0) * lr + ai * li) / den
    fi = (ai * lr - (ar - 1.0) * li) / den
    return ar, ai, fr, fi


SUPER = 4
SB_STATE = N_STATE // SUPER
SB_WIDTH = SSM_WIDTH // SUPER


def _sb_state(k):
    return slice(SB_STATE * k, SB_STATE * (k + 1))


def _sb_width(k):
    return slice(SB_WIDTH * k, SB_WIDTH * (k + 1))


def _dt_column(log_dt_row):
    eye = _iota((SSM_GROUPS, SSM_GROUPS), 0) == _iota((SSM_GROUPS, SSM_GROUPS), 1)
    return jnp.sum(jnp.where(eye, log_dt_row, 0.0), axis=1, keepdims=True)


def _group_masks():
    e64 = ((_iota((SSM_STATE, N_STATE), 1) & (SSM_STATE - 1)) == _iota((SSM_STATE, N_STATE), 0)).astype(_F32)
    own = _iota((SSM_GROUPS, N_STATE), 0) == (_iota((SSM_GROUPS, N_STATE), 1) >> 6)
    return e64, own


def _rows_of_group():
    return ((_iota((SSM_WIDTH, SSM_GROUPS), 0) >> 4) == _iota((SSM_WIDTH, SSM_GROUPS), 1)).astype(_F32)


def _ssm_prep(lam_re, lam_im, log_dt, b_re, b_im, c_re, c_im):
    def body(lr_ref, li_ref, ld_ref, bre, bim, cre, cim, ar_ref, ai_ref, btr, bti, ctr, cti):
        ar, ai, fr, fi = _s5_discretize(lr_ref[...], li_ref[...], _dt_column(ld_ref[...]))
        e64, own = _group_masks()
        mask_c = (_iota((SSM_WIDTH, N_STATE), 0) >> 4) == (_iota((SSM_WIDTH, N_STATE), 1) >> 6)

        def to_row(t):
            return jnp.sum(jnp.where(own, _dot_exact(t, e64, _NN), 0.0), axis=0, keepdims=True)

        def fold(m):
            full = jnp.where(mask_c, _dot(m, e64, _NN), 0.0)
            return sum(full[_sb_width(k), :] for k in range(SUPER)).astype(_BF16)

        ar_ref[...] = to_row(ar)
        ai_ref[...] = to_row(ai)
        spread = _rows_of_group()
        fr_t = _dot_exact(spread, fr, _NN)
        fi_t = _dot_exact(spread, fi, _NN)
        btr[...] = fold(fr_t * bre[...] - fi_t * bim[...])
        bti[...] = fold(fr_t * bim[...] + fi_t * bre[...])
        ctr[...] = fold(cre[...])
        cti[...] = fold(cim[...])

    row = (1, N_STATE)
    ins = [lam_re, lam_im, log_dt, b_re, b_im, c_re, c_im]
    return _call(body, (1,), [_whole(a.shape) for a in ins],
                 [_whole(row), _whole(row)] + [_whole((SB_WIDTH, N_STATE))] * 4,
                 [_sds(row, _F32), _sds(row, _F32)] + [_sds((SB_WIDTH, N_STATE), _BF16)] * 4,
                 "ssm_prep")(*ins)


def _complex_power(ar, ai, n):
    pr, pi = jnp.ones_like(ar), jnp.zeros_like(ai)
    while n:
        if n & 1:
            pr, pi = pr * ar - pi * ai, pr * ai + pi * ar
        ar, ai = ar * ar - ai * ai, 2.0 * ar * ai
        n >>= 1
    return pr, pi


def _chunk_carries(er, ei, pr, pi, reverse):
    rows = _iota(er.shape, 0)
    sr = jnp.zeros_like(pr)
    si = jnp.zeros_like(pi)
    out_r = jnp.zeros_like(er)
    out_i = jnp.zeros_like(ei)
    order = range(SCAN_CHUNKS - 1, 0, -1) if reverse else range(SCAN_CHUNKS - 1)
    for c in order:
        e_r = er[c:c + 1, :]
        e_i = ei[c:c + 1, :]
        sr, si = pr * sr - pi * si + e_r, pr * si + pi * sr + e_i
        nxt = c - 1 if reverse else c + 1
        out_r = jnp.where(rows == nxt, sr, out_r)
        out_i = jnp.where(rows == nxt, si, out_i)
    return out_r, out_i


_GELU_K = math.sqrt(2.0 / math.pi)
_GELU_C = 0.044715


def _gelu(y):
    return 0.5 * y * (1.0 + jnp.tanh(_GELU_K * (y + _GELU_C * y * y * y)))


def _gelu_grad(y):
    t = jnp.tanh(_GELU_K * (y + _GELU_C * y * y * y))
    return 0.5 * (1.0 + t) + 0.5 * y * (1.0 - t * t) * _GELU_K * (1.0 + 3.0 * _GELU_C * y * y)


def _step_rows(t):
    return pl.ds(pl.multiple_of(t * SCAN_CHUNKS, SCAN_CHUNKS), SCAN_CHUNKS)


def _scan_in_place(br, bi, ar, ai, T):
    W = br.shape[1]
    ar8 = jnp.broadcast_to(ar, (SCAN_CHUNKS, W))
    ai8 = jnp.broadcast_to(ai, (SCAN_CHUNKS, W))

    def local(t, c):
        cr, ci = c
        rows = _step_rows(t)
        return ar8 * cr - ai8 * ci + br[rows, :], ar8 * ci + ai8 * cr + bi[rows, :]

    zero = jnp.zeros((SCAN_CHUNKS, W), _F32)
    er, ei = lax.fori_loop(0, T, local, (zero, zero), unroll=SCAN_UNROLL)
    pr, pi = _complex_power(ar, ai, T)
    carries = _chunk_carries(er, ei, pr, pi, reverse=False)

    def final(t, c):
        nr, ni = local(t, c)
        rows = _step_rows(t)
        br[rows, :] = nr
        bi[rows, :] = ni
        return nr, ni

    lax.fori_loop(0, T, final, carries, unroll=SCAN_UNROLL)


def _scan_reverse_in_place(dr, di, xr, xi, ar, ai, T):
    W = dr.shape[1]
    ar8 = jnp.broadcast_to(ar, (SCAN_CHUNKS, W))
    ai8 = jnp.broadcast_to(ai, (SCAN_CHUNKS, W))

    def local(t, c):
        cr, ci = c
        rows = _step_rows(t)
        return ar8 * cr + ai8 * ci + dr[rows, :], ar8 * ci - ai8 * cr + di[rows, :]

    zero = jnp.zeros((SCAN_CHUNKS, W), _F32)
    er, ei = lax.fori_loop(0, T, lambda k, c: local(T - 1 - k, c), (zero, zero), unroll=SCAN_UNROLL)
    pr, pi = _complex_power(ar, -ai, T)
    sr, si = _chunk_carries(er, ei, pr, pi, reverse=True)

    def grad_a(acc, nr, ni, xpr, xpi):
        return acc[0] + nr * xpr + ni * xpi, acc[1] + ni * xpr - nr * xpi

    def final(k, c):
        t = T - 1 - k
        nr, ni = local(t, c[:2])
        rows = _step_rows(t)
        dr[rows, :] = nr
        di[rows, :] = ni
        before = _step_rows(t - 1)
        gr, gi = grad_a(c[2:], nr, ni, xr[before, :], xi[before, :])
        return nr, ni, gr, gi

    cr, ci, gr, gi = lax.fori_loop(0, T - 1, final, (sr, si, zero, zero), unroll=SCAN_UNROLL)
    nr, ni = local(0, (cr, ci))
    dr[_step_rows(0), :] = nr
    di[_step_rows(0), :] = ni
    first = _iota((SCAN_CHUNKS, W), 0) == 0
    last = _step_rows(T - 1)
    xpr = jnp.where(first, 0.0, pltpu.roll(xr[last, :], 1, 0))
    xpi = jnp.where(first, 0.0, pltpu.roll(xi[last, :], 1, 0))
    gr, gi = grad_a((gr, gi), nr, ni, xpr, xpi)
    return jnp.sum(gr, axis=0, keepdims=True), jnp.sum(gi, axis=0, keepdims=True)


def _ssm_super_specs(L):
    width = pl.BlockSpec((L, SB_WIDTH), lambda k: (0, k))
    matrix = pl.BlockSpec((SB_WIDTH, SB_STATE), lambda k: (0, k))
    row = pl.BlockSpec((1, SB_STATE), lambda k: (0, k))
    return width, matrix, row


def _ssm_core_fwd(u, bt_re, bt_im, ct_re, ct_im, a_re, a_im):
    L = u.shape[0]
    T = L // SCAN_CHUNKS

    def body(u_ref, br_ref, bi_ref, cr_ref, ci_ref, ar_ref, ai_ref, y_ref, xr, xi):
        ub = u_ref[...].astype(_BF16)
        xr[...] = _dot(ub, br_ref[...], _NN)
        xi[...] = _dot(ub, bi_ref[...], _NN)
        _scan_in_place(xr, xi, ar_ref[...], ai_ref[...], T)
        y_ref[...] = _dot(xr[...], cr_ref[...], _NT) - _dot(xi[...], ci_ref[...], _NT)

    width, matrix, row = _ssm_super_specs(L)
    state = pl.BlockSpec((L, SB_STATE), lambda k: (0, k))
    return _call(body, (SUPER,), [width, matrix, matrix, matrix, matrix, row, row], [width, state, state],
                 [_sds((L, SSM_WIDTH), _F32)] + [_sds((L, N_STATE), _F32)] * 2,
                 "ssm_core_fwd")(u, bt_re, bt_im, ct_re, ct_im, a_re, a_im)


def _ssm_core_bwd(u, dy, dud, x_re, x_im, bt_re, bt_im, ct_re, ct_im, a_re, a_im, tokens=()):
    L = u.shape[0]
    T = L // SCAN_CHUNKS

    def body(u_ref, dy_ref, dud_ref, xr, xi, br_ref, bi_ref, cr_ref, ci_ref, ar_ref, ai_ref,
             du_ref, dcr_ref, dci_ref, dbr_ref, dbi_ref, dar_ref, dai_ref, lr, li):
        dyb = dy_ref[...]
        lr[...] = _dot(dyb, cr_ref[...], _NN)
        li[...] = -_dot(dyb, ci_ref[...], _NN)
        da_re, da_im = _scan_reverse_in_place(lr, li, xr, xi, ar_ref[...], ai_ref[...], T)
        dar_ref[...] = da_re
        dai_ref[...] = da_im
        du_ref[...] = _dot(lr[...], br_ref[...], _NT) + _dot(li[...], bi_ref[...], _NT) + dud_ref[...]
        ub = u_ref[...].astype(_BF16)
        dcr_ref[...] = _dot(dyb, xr[...], _TN)
        dci_ref[...] = _dot(dyb, xi[...], _TN)
        dbr_ref[...] = _dot(ub, lr[...], _TN)
        dbi_ref[...] = _dot(ub, li[...], _TN)

    width, matrix, row = _ssm_super_specs(L)
    state = pl.BlockSpec((L, SB_STATE), lambda k: (0, k))
    return _call(body, (SUPER,), [width, width, width, state, state, matrix, matrix, matrix, matrix, row, row],
                 [width] + [matrix] * 4 + [row] * 2,
                 [_sds((L, SSM_WIDTH), _F32)] + [_sds((SB_WIDTH, N_STATE), _F32)] * 4 + [_sds((1, N_STATE), _F32)] * 2,
                 "ssm_core_bwd", scratch=[pltpu.VMEM((L, SB_STATE), _F32)] * 2,
                 tokens=tokens)(u, dy, dud, x_re, x_im, bt_re, bt_im, ct_re, ct_im, a_re, a_im)


def _ssm_out(cx, u, d_row, w_glu, b_glu, g_ssm):
    L = u.shape[0]
    tm = _tile(L)

    def body(cx_ref, u_ref, d_ref, w_ref, b_ref, g_ref, y_ref, z_ref, n_ref, stage):
        y = cx_ref[...] + d_ref[...] * u_ref[...]
        y_ref[...] = y
        z = _dot(_gelu(y), w_ref[...], _NT) + b_ref[...]
        z_ref[...] = z
        out = z[:, :SSM_WIDTH] * jax.nn.sigmoid(z[:, SSM_WIDTH:])
        n, _ = _rms_fwd(out, g_ref[...])
        for k in range(SSM_WIDTH // _LANES):
            stage[k] = n[:, _LANES * k:_LANES * (k + 1)]
            for c in range(SCAN_CHUNKS):
                rows = stage[k, pl.ds(c, tm // SCAN_CHUNKS, stride=SCAN_CHUNKS), :]
                lane = SSM_WIDTH * c + _LANES * k
                n_ref[:, lane:lane + _LANES] = rows.astype(_BF16)

    return _call(body, (L // tm,),
                 [_rows(tm, SSM_WIDTH), _rows(tm, SSM_WIDTH), _whole((1, SSM_WIDTH)),
                  _whole((2 * SSM_WIDTH, SSM_WIDTH)), _whole((1, 2 * SSM_WIDTH)), _whole((1, SSM_WIDTH))],
                 [_rows(tm, SSM_WIDTH), _rows(tm, 2 * SSM_WIDTH), _rows(tm // SCAN_CHUNKS, SCAN_CHUNKS * SSM_WIDTH)],
                 [_sds((L, SSM_WIDTH), _F32), _sds((L, 2 * SSM_WIDTH), _F32), _sds(_chunk_shape(L, SSM_WIDTH), _BF16)],
                 "ssm_out", scratch=[pltpu.VMEM((SSM_WIDTH // _LANES, tm, _LANES), _F32)])(
        cx, u, d_row, w_glu, b_glu, g_ssm)


def _ssm_out_bwd(dn, y, z, u, d_row, w_glu, g_ssm, tokens=()):
    L = u.shape[0]
    tm = _tile(L)

    def body(dn_ref, y_ref, z_ref, u_ref, d_ref, w_ref, g_ref,
             gy_ref, dz_ref, dy_ref, dud_ref, dg_ref, db_ref, dd_ref, stage):
        first = pl.program_id(0) == 0
        for k in range(SSM_WIDTH // _LANES):
            for c in range(SCAN_CHUNKS):
                lane = SSM_WIDTH * c + _LANES * k
                stage[k, pl.ds(c, tm // SCAN_CHUNKS, stride=SCAN_CHUNKS), :] = dn_ref[:, lane:lane + _LANES]
        dn = jnp.concatenate([stage[k] for k in range(SSM_WIDTH // _LANES)], axis=1)
        z = z_ref[...]
        z1, z2 = z[:, :SSM_WIDTH], z[:, SSM_WIDTH:]
        sig = jax.nn.sigmoid(z2)
        out = z1 * sig
        g = g_ref[...]
        _, r = _rms_fwd(out, g)
        dout, dg = _rms_bwd(dn, out, g, r)
        _accumulate(dg_ref, dg, first)
        dz = jnp.concatenate([dout * sig, dout * z1 * sig * (1.0 - sig)], axis=1)
        _accumulate(db_ref, jnp.sum(dz, axis=0, keepdims=True), first)
        dzb = dz.astype(_BF16)
        dz_ref[...] = dzb
        y = y_ref[...]
        gy_ref[...] = _gelu(y).astype(_BF16)
        dy = _dot(dzb, w_ref[...], _NN) * _gelu_grad(y)
        u = u_ref[...]
        _accumulate(dd_ref, jnp.sum(dy * u, axis=0, keepdims=True), first)
        dud_ref[...] = d_ref[...] * dy
        dy_ref[...] = dy.astype(_BF16)

    row = _whole((1, SSM_WIDTH))
    return _call(body, (L // tm,),
                 [_rows(tm // SCAN_CHUNKS, SCAN_CHUNKS * SSM_WIDTH), _rows(tm, SSM_WIDTH), _rows(tm, 2 * SSM_WIDTH),
                  _rows(tm, SSM_WIDTH), row, _whole((2 * SSM_WIDTH, SSM_WIDTH)), row],
                 [_rows(tm, SSM_WIDTH), _rows(tm, 2 * SSM_WIDTH), _rows(tm, SSM_WIDTH), _rows(tm, SSM_WIDTH),
                  row, _whole((1, 2 * SSM_WIDTH)), row],
                 [_sds((L, SSM_WIDTH), _BF16), _sds((L, 2 * SSM_WIDTH), _BF16), _sds((L, SSM_WIDTH), _BF16),
                  _sds((L, SSM_WIDTH), _F32),
                  _sds((1, SSM_WIDTH), _F32), _sds((1, 2 * SSM_WIDTH), _F32), _sds((1, SSM_WIDTH), _F32)],
                 "ssm_out_bwd", scratch=[pltpu.VMEM((SSM_WIDTH // _LANES, tm, _LANES), _F32)], tokens=tokens)(
        dn, y, z, u, d_row, w_glu, g_ssm)


_SSM_PACK = {"ssm_b_re": (0, SSM_WIDTH, 0, SSM_STATE), "ssm_c_re": (0, SSM_WIDTH, 64, SSM_STATE),
             "ssm_b_im": (512, SSM_WIDTH, 0, SSM_STATE), "ssm_c_im": (512, SSM_WIDTH, 64, SSM_STATE),
             "ssm_lambda_re": (1024, SSM_GROUPS, 0, SSM_STATE), "ssm_lambda_im": (1024, SSM_GROUPS, 64, SSM_STATE),
             "ssm_d": (1056, SSM_GROUPS, 0, SSM_GROUP), "ssm_log_dt": (1088, 1, 0, SSM_GROUPS)}
_PACK_TILE = 16
_SSM_PACK_ROWS = 1088 + _PACK_TILE


def _ssm_param_bwd(da_re, da_im, dbt_re, dbt_im, dct_re, dct_im, lam_re, lam_im, log_dt, b_re, b_im, g_d):
    def body(dar, dai, dbr, dbi, dcr, dci, lr_ref, li_ref, ld_ref, bre_ref, bim_ref, gd_ref, pack_ref):
        lane_in = _iota((SSM_STATE, _LANES), 0)
        lane_out = _iota((SSM_STATE, _LANES), 1)
        low = (lane_out == lane_in).astype(_F32)
        high = (lane_out == lane_in + SSM_STATE).astype(_F32)

        def side_by_side(a, b):
            return _dot_exact(a, low, _NN) + _dot_exact(b, high, _NN)

        tail = _SSM_PACK["ssm_d"][0]
        pack_ref[tail:, :] = jnp.zeros((_SSM_PACK_ROWS - tail, _LANES), _BF16)
        pack_ref[tail:tail + SSM_GROUPS, 0:SSM_GROUP] = gd_ref[...].astype(_BF16)
        own_c = (_iota((SB_WIDTH, SB_STATE), 0) >> 4) == (_iota((SB_WIDTH, SB_STATE), 1) >> 6)

        def unfold(ref):
            blocks = []
            for k in range(SUPER):
                t = jnp.where(own_c, ref[:, _sb_state(k)], 0.0)
                t = sum(t[:, 128 * i:128 * (i + 1)] for i in range(SB_STATE // 128))
                blocks.append((t + pltpu.roll(t, SSM_STATE, 1))[:, :SSM_STATE])
            return jnp.concatenate(blocks, axis=0)

        dbb_re, dbb_im = unfold(dbr), unfold(dbi)
        b_re, b_im = bre_ref[...], bim_ref[...]
        dt_col = _dt_column(ld_ref[...])
        (_, _, fr, fi), vjp = jax.vjp(_s5_discretize, lr_ref[...], li_ref[...], dt_col)
        spread = _rows_of_group()
        fr_t = _dot_exact(spread, fr, _NN)
        fi_t = _dot_exact(spread, fi, _NN)
        pack_ref[0:SSM_WIDTH, :] = side_by_side(fr_t * dbb_re + fi_t * dbb_im, unfold(dcr)).astype(_BF16)
        pack_ref[SSM_WIDTH:2 * SSM_WIDTH, :] = side_by_side(fr_t * dbb_im - fi_t * dbb_re, -unfold(dci)).astype(_BF16)
        d_fr = _dot_exact(spread, dbb_re * b_re + dbb_im * b_im, _TN)
        d_fi = _dot_exact(spread, dbb_im * b_re - dbb_re * b_im, _TN)
        e64, own = _group_masks()

        def from_row(ref):
            return _dot_exact(jnp.where(own, ref[...], 0.0), e64, _NT)

        d_lr, d_li, d_dt = vjp((from_row(dar), from_row(dai), d_fr, d_fi))
        lam_rows = _SSM_PACK["ssm_lambda_re"][0]
        pack_ref[lam_rows:lam_rows + SSM_GROUPS, :] = side_by_side(d_lr, d_li).astype(_BF16)
        eye = (_iota((SSM_GROUPS, SSM_GROUPS), 0) == _iota((SSM_GROUPS, SSM_GROUPS), 1)).astype(_F32)
        dt_row = _SSM_PACK["ssm_log_dt"][0]
        pack_ref[dt_row:dt_row + _PACK_TILE, 0:SSM_GROUPS] = _dot_exact(
            jnp.broadcast_to(d_dt, (SSM_GROUPS, 128)), eye, _TN)[0:_PACK_TILE].astype(_BF16)

    ins = [da_re, da_im, dbt_re, dbt_im, dct_re, dct_im, lam_re, lam_im, log_dt, b_re, b_im, g_d]
    out = (_SSM_PACK_ROWS, _LANES)
    return _call(body, (1,), [_whole(a.shape) for a in ins], _whole(out), _sds(out, _BF16), "ssm_param_bwd")(*ins)


def _head_spread(j):
    r = _iota((KV_WIDTH, 256), 0)
    c = _iota((KV_WIDTH, 256), 1)
    return (r == HEAD_DIM * j + (c & (HEAD_DIM - 1))).astype(_BF16)


STACK = Q_PER_KV * BLOCK


def _stack_heads(t):
    lane_head = _iota((1, 256), 1) >> 6
    return jnp.concatenate([jnp.where(lane_head == g, t, jnp.zeros_like(t)) for g in range(Q_PER_KV)], axis=0)


def _unstack_heads(t):
    lane_head = _iota((1, 256), 1) >> 6
    return sum(jnp.where(lane_head == g, t[BLOCK * g:BLOCK * (g + 1)], 0.0) for g in range(Q_PER_KV))


def _stacked_sinks(sink_ref, j):
    block = _iota((STACK, 1), 0) >> 7
    col = jnp.full((STACK, 1), sink_ref[Q_PER_KV * j], _F32)
    for g in range(1, Q_PER_KV):
        col = jnp.where(block == g, sink_ref[Q_PER_KV * j + g], col)
    return col


def _fold_heads(t, j):
    t = t[:, :KV_WIDTH] + t[:, KV_WIDTH:]
    t = t + pltpu.roll(t, HEAD_DIM, 1)
    return jnp.where((_iota((1, KV_WIDTH), 1) >> 6) == j, t, 0.0)


def _attn_scores(q_stacked, kt, blk, sink):
    s = _dot(q_stacked, kt, _NT) * (HEAD_DIM ** -0.5)
    qi = _iota((STACK, 2 * BLOCK), 0) & (BLOCK - 1)
    kj = _iota((STACK, 2 * BLOCK), 1)
    rel = qi + BLOCK - kj
    valid = (rel >= 0) & (rel < BLOCK) & (blk * BLOCK - BLOCK + kj >= 0)
    s = jnp.where(valid, s, MASK_VALUE)
    m = jnp.maximum(jnp.max(s, axis=-1, keepdims=True), sink)
    p = jnp.exp(s - m)
    e_sink = jnp.exp(sink - m)
    den = jnp.sum(p, axis=-1, keepdims=True) + e_sink
    return p / den, e_sink / den


def _sink_slot():
    return _iota((STACK, 2 * BLOCK), 1) == 0


def _prob_block():
    return pl.BlockSpec((None, N_KV_HEADS, STACK, 2 * BLOCK), lambda i: (i, 0, 0, 0))


def _attn_specs():
    prev = lambda i: (jnp.maximum(i - 1, 0), 0)
    cur = lambda i: (i, 0)
    kv = [pl.BlockSpec((BLOCK, KV_WIDTH), prev), pl.BlockSpec((BLOCK, KV_WIDTH), cur)]
    return [pl.BlockSpec((BLOCK, ATTN_WIDTH), cur)] + kv + kv


def _attn_fwd(q, k, v, sinks, g_attn):
    L = q.shape[0]

    def body(q_ref, kp_ref, kc_ref, vp_ref, vc_ref, sink_ref, g_ref, o_ref, n_ref, p_ref):
        blk = pl.program_id(0)
        kwin = jnp.concatenate([kp_ref[...], kc_ref[...]], axis=0)
        vwin = jnp.concatenate([vp_ref[...], vc_ref[...]], axis=0)
        halves = []
        for j in range(N_KV_HEADS):
            spread = _head_spread(j)
            kt = _dot(kwin, spread, _NN).astype(_BF16)
            vt = _dot(vwin, spread, _NN).astype(_BF16)
            qs = _stack_heads(q_ref[:, 256 * j:256 * (j + 1)])
            p, p_sink = _attn_scores(qs, kt, blk, _stacked_sinks(sink_ref, j))
            p_ref[j] = jnp.where(_sink_slot(), p_sink, p)
            halves.append(_unstack_heads(_dot(p, vt, _NN)))
        o = jnp.concatenate(halves, axis=1)
        o_ref[...] = o
        n, _ = _rms_fwd(o, g_ref[...])
        n_ref[...] = n.astype(_BF16)

    cur = lambda i: (i, 0)
    return _call(body, (L // BLOCK,),
                 _attn_specs() + [pl.BlockSpec(memory_space=pltpu.SMEM), _whole((1, ATTN_WIDTH))],
                 [pl.BlockSpec((BLOCK, ATTN_WIDTH), cur)] * 2 + [_prob_block()],
                 [_sds((L, ATTN_WIDTH), _F32), _sds((L, ATTN_WIDTH), _BF16),
                  _sds((L // BLOCK, N_KV_HEADS, STACK, 2 * BLOCK), _F32)],
                 "attn_fwd")(q, k, k, v, v, sinks, g_attn)


def _attn_bwd(q, k, v, o, dn, probs, g_attn):
    L = q.shape[0]

    def body(q_ref, kp_ref, kc_ref, vp_ref, vc_ref, o_ref, dn_ref, p_ref, g_ref,
             dq_ref, dk_ref, dv_ref, dsink_ref, dg_ref):
        blk = pl.program_id(0)
        first = blk == 0

        @pl.when(first)
        def _():
            dk_ref[...] = jnp.zeros_like(dk_ref)
            dv_ref[...] = jnp.zeros_like(dv_ref)
            dsink_ref[...] = jnp.zeros_like(dsink_ref)

        o = o_ref[...]
        g = g_ref[...]
        _, r = _rms_fwd(o, g)
        do, dg = _rms_bwd(dn_ref[...], o, g, r)
        _accumulate(dg_ref, dg, first)
        kwin = jnp.concatenate([kp_ref[...], kc_ref[...]], axis=0)
        vwin = jnp.concatenate([vp_ref[...], vc_ref[...]], axis=0)
        lane = _iota((1, 128), 1)
        dsink = jnp.zeros((1, 128), _F32)
        dkwin = jnp.zeros((2 * BLOCK, KV_WIDTH), _F32)
        dvwin = jnp.zeros((2 * BLOCK, KV_WIDTH), _F32)
        dq_halves = []
        for j in range(N_KV_HEADS):
            spread = _head_spread(j)
            kt = _dot(kwin, spread, _NN).astype(_BF16)
            vt = _dot(vwin, spread, _NN).astype(_BF16)
            qs = _stack_heads(q_ref[:, 256 * j:256 * (j + 1)])
            dos = _stack_heads(do[:, 256 * j:256 * (j + 1)]).astype(_BF16)
            saved = p_ref[j]
            p_sink = saved[:, 0:1]
            p = jnp.where(_sink_slot(), 0.0, saved)
            dp = _dot(dos, vt, _NT)
            delta = jnp.sum(p * dp, axis=-1, keepdims=True)
            ds = (p * (dp - delta) * (HEAD_DIM ** -0.5)).astype(_BF16)
            sink_term = p_sink * delta
            for g in range(Q_PER_KV):
                head_sum = jnp.sum(sink_term[BLOCK * g:BLOCK * (g + 1)], axis=0, keepdims=True)
                dsink = dsink - jnp.where(lane == Q_PER_KV * j + g, head_sum, 0.0)
            dvwin = dvwin + _fold_heads(_dot(p, dos, _TN), j)
            dkwin = dkwin + _fold_heads(_dot(ds, qs, _TN), j)
            dq_halves.append(_unstack_heads(_dot(ds, kt, _NN)))
        dq_ref[...] = jnp.concatenate(dq_halves, axis=1)
        dsink_ref[...] += dsink
        prev = pl.ds(pl.multiple_of(jnp.maximum(blk - 1, 0) * BLOCK, BLOCK), BLOCK)
        cur = pl.ds(pl.multiple_of(blk * BLOCK, BLOCK), BLOCK)
        dk_ref[prev, :] += dkwin[:BLOCK]
        dk_ref[cur, :] += dkwin[BLOCK:]
        dv_ref[prev, :] += dvwin[:BLOCK]
        dv_ref[cur, :] += dvwin[BLOCK:]

    cur = lambda i: (i, 0)
    blk_q = pl.BlockSpec((BLOCK, ATTN_WIDTH), cur)
    return _call(body, (L // BLOCK,),
                 _attn_specs() + [blk_q, blk_q, _prob_block(), _whole((1, ATTN_WIDTH))],
                 [blk_q, _whole((L, KV_WIDTH)), _whole((L, KV_WIDTH)), _whole((1, 128)), _whole((1, ATTN_WIDTH))],
                 [_sds((L, ATTN_WIDTH), _F32), _sds((L, KV_WIDTH), _F32), _sds((L, KV_WIDTH), _F32),
                  _sds((1, 128), _F32), _sds((1, ATTN_WIDTH), _F32)],
                 "attn_bwd")(q, k, k, v, v, o, dn, probs, g_attn)


def _out_proj(n_ssm, n_attn, x, w_out, g_post_mix, g_pre_ffn):
    L = x.shape[0]
    tm = _chunk_tile(L)

    def body(ns_ref, na_ref, x_ref, w_ref, g1_ref, g2_ref, merged_ref, mo_ref, h1_ref, hn2_ref):
        merged = jnp.concatenate([ns_ref[...], na_ref[...]], axis=1)
        merged_ref[...] = merged
        mo = _dot(merged, w_ref[...], _NN)
        mo_ref[...] = mo
        n, _ = _rms_fwd(mo, g1_ref[...])
        h1 = x_ref[...] + n
        h1_ref[...] = h1
        hn2, _ = _rms_fwd(h1, g2_ref[...])
        hn2_ref[...] = hn2.astype(_BF16)

    row = _whole((1, D_MODEL))
    return _call(body, (L // tm,),
                 [_chunk_block(L, SSM_WIDTH), _rows(tm, ATTN_WIDTH), _rows(tm, D_MODEL), _whole((D_MODEL, D_MODEL)),
                  row, row],
                 [_rows(tm, D_MODEL)] * 4,
                 [_sds((L, D_MODEL), _BF16), _sds((L, D_MODEL), _F32), _sds((L, D_MODEL), _F32), _sds((L, D_MODEL), _BF16)],
                 "out_proj")(n_ssm, n_attn, x, w_out, g_post_mix, g_pre_ffn)


def _ffn(hn2, h1, target, w_gate_up, w_down, g_pre_ffn, g_post_ffn):
    L = h1.shape[0]
    tm = _tile(L)
    half = FFN_CHUNK

    def body(hn2_ref, h1_ref, tgt_ref, wgu_hbm, wd_hbm, g2_ref, g3_ref,
             act_ref, dgu_ref, dff_ref, dh1_ref, loss_ref, dg3_ref, dg2_ref,
             wgu, wd, gu, sem):
        first = pl.program_id(0) == 0

        @pl.when(first)
        def _():
            c1 = pltpu.make_async_copy(wgu_hbm, wgu, sem.at[0])
            c2 = pltpu.make_async_copy(wd_hbm, wd, sem.at[1])
            c1.start()
            c2.start()
            c1.wait()
            c2.wait()

        hn2 = hn2_ref[...]
        ff = jnp.zeros((tm, D_MODEL), _F32)
        for c in range(D_FF // half):
            gate = _dot(hn2, wgu[half * c:half * (c + 1), :], _NT)
            up = _dot(hn2, wgu[D_FF + half * c:D_FF + half * (c + 1), :], _NT)
            gu[:, half * c:half * (c + 1)] = gate
            gu[:, D_FF + half * c:D_FF + half * (c + 1)] = up
            act = gate * jax.nn.sigmoid(gate) * up
            act_ref[half * c:half * (c + 1), :] = act.T.astype(_BF16)
            ff = ff + _dot(act, wd[half * c:half * (c + 1), :], _NN)
        g3 = g3_ref[...]
        n, r = _rms_fwd(ff, g3)
        h1 = h1_ref[...]
        err = h1 + n - tgt_ref[...]
        loss = 0.5 * jnp.sum(jnp.mean(err * err, axis=-1, keepdims=True), axis=0, keepdims=True)
        _accumulate(loss_ref, jnp.broadcast_to(loss, (1, 128)), first)
        dh2 = err * (1.0 / D_MODEL)
        dff, dg3 = _rms_bwd(dh2, ff, g3, r)
        _accumulate(dg3_ref, dg3, first)
        dffb = dff.astype(_BF16)
        dff_ref[...] = dffb
        dhn2 = jnp.zeros((tm, D_MODEL), _F32)
        for c in range(D_FF // half):
            dact = _dot(dffb, wd[half * c:half * (c + 1), :], _NT)
            gate = gu[:, half * c:half * (c + 1)]
            up = gu[:, D_FF + half * c:D_FF + half * (c + 1)]
            sig = jax.nn.sigmoid(gate)
            silu = gate * sig
            dgate = dact * up * (sig + silu * (1.0 - sig))
            dup = dact * silu
            dgu_ref[half * c:half * (c + 1), :] = dgate.T.astype(_BF16)
            dgu_ref[D_FF + half * c:D_FF + half * (c + 1), :] = dup.T.astype(_BF16)
            dhn2 = dhn2 + _dot(dgate, wgu[half * c:half * (c + 1), :], _NN)
            dhn2 = dhn2 + _dot(dup, wgu[D_FF + half * c:D_FF + half * (c + 1), :], _NN)
        g2 = g2_ref[...]
        _, r2 = _rms_fwd(h1, g2)
        dh1, dg2 = _rms_bwd(dhn2, h1, g2, r2)
        _accumulate(dg2_ref, dg2, first)
        dh1_ref[...] = dh2 + dh1

    row = _whole((1, D_MODEL))
    anyspace = pl.BlockSpec(memory_space=pl.ANY)
    return _call(body, (L // tm,),
                 [_rows(tm, D_MODEL), _rows(tm, D_MODEL), _rows(tm, D_MODEL), anyspace, anyspace, row, row],
                 [pl.BlockSpec((D_FF, tm), lambda i: (0, i)), pl.BlockSpec((2 * D_FF, tm), lambda i: (0, i)),
                  _rows(tm, D_MODEL), _rows(tm, D_MODEL), _whole((1, 128)), row, row],
                 [_sds((D_FF, L), _BF16), _sds((2 * D_FF, L), _BF16), _sds((L, D_MODEL), _BF16),
                  _sds((L, D_MODEL), _F32), _sds((1, 128), _F32), _sds((1, D_MODEL), _F32), _sds((1, D_MODEL), _F32)],
                 "ffn",
                 scratch=[pltpu.VMEM((2 * D_FF, D_MODEL), _BF16), pltpu.VMEM((D_FF, D_MODEL), _BF16),
                          pltpu.VMEM((tm, 2 * D_FF), _F32), pltpu.SemaphoreType.DMA((2,))],
                 )(hn2, h1, target, w_gate_up, w_down, g_pre_ffn, g_post_ffn)


def _out_proj_bwd(dh1, mo, w_out, g_post_mix, tokens=()):
    L = dh1.shape[0]
    tm = _chunk_tile(L)

    def body(dh1_ref, mo_ref, w_ref, g_ref, dmo_ref, dns_ref, dna_ref, dg_ref):
        first = pl.program_id(0) == 0
        mo = mo_ref[...]
        g = g_ref[...]
        _, r = _rms_fwd(mo, g)
        dmo, dg = _rms_bwd(dh1_ref[...], mo, g, r)
        _accumulate(dg_ref, dg, first)
        dmob = dmo.astype(_BF16)
        dmo_ref[...] = dmob
        dmerged = _dot(dmob, w_ref[...], _NT)
        dns_ref[...] = dmerged[:, :SSM_WIDTH]
        dna_ref[...] = dmerged[:, SSM_WIDTH:]

    row = _whole((1, D_MODEL))
    return _call(body, (L // tm,),
                 [_rows(tm, D_MODEL), _rows(tm, D_MODEL), _whole((D_MODEL, D_MODEL)), row],
                 [_rows(tm, D_MODEL), _chunk_block(L, SSM_WIDTH), _rows(tm, ATTN_WIDTH), row],
                 [_sds((L, D_MODEL), _BF16), _sds(_chunk_shape(L, SSM_WIDTH), _F32), _sds((L, ATTN_WIDTH), _F32),
                  _sds((1, D_MODEL), _F32)],
                 "out_proj_bwd", tokens=tokens)(dh1, mo, w_out, g_post_mix)


def _in_proj_bwd(du, dq, dk, dv, cos_t, sin_t, x, dh1, g_pre_mix, w_in, tokens=()):
    L = x.shape[0]
    tm = _chunk_tile(L)

    def body(du_ref, dq_ref, dk_ref, dv_ref, cos_ref, sin_ref, x_ref, dh1_ref, g_ref, w_ref,
             dproj_ref, dx_ref, dg_ref):
        first = pl.program_id(0) == 0
        cos_v, sin_v = cos_ref[...], sin_ref[...]
        dproj = jnp.concatenate([du_ref[...], _rope_transpose(dq_ref[...], cos_v, sin_v),
                                 _rope_transpose(dk_ref[...], cos_v, sin_v), dv_ref[...]], axis=1).astype(_BF16)
        dproj_ref[...] = dproj
        dhn = _dot(dproj, w_ref[...], _NN)
        x = x_ref[...]
        g = g_ref[...]
        _, r = _rms_fwd(x, g)
        dx, dg = _rms_bwd(dhn, x, g, r)
        _accumulate(dg_ref, dg, first)
        dx_ref[...] = dh1_ref[...] + dx

    row = _whole((1, D_MODEL))
    return _call(body, (L // tm,),
                 [_chunk_block(L, SSM_WIDTH), _rows(tm, ATTN_WIDTH), _rows(tm, KV_WIDTH), _rows(tm, KV_WIDTH),
                  _rows(tm, KV_WIDTH), _rows(tm, KV_WIDTH), _rows(tm, D_MODEL), _rows(tm, D_MODEL), row,
                  _whole((IN_WIDTH, D_MODEL))],
                 [_rows(tm, IN_WIDTH), _rows(tm, D_MODEL), row],
                 [_sds((L, IN_WIDTH), _BF16), _sds((L, D_MODEL), _F32), _sds((1, D_MODEL), _F32)],
                 "in_proj_bwd", tokens=tokens)(du, dq, dk, dv, cos_t, sin_t, x, dh1, g_pre_mix, w_in)


def _matmul_nn(a, b, out_dtype, name):
    M, K = a.shape
    N = b.shape[1]
    tm = next(t for t in (704, 512, 256, 128) if M % t == 0)
    tn = N if N <= D_MODEL else next(t for t in (512, 256, 128) if N % t == 0)

    def body(a_ref, b_ref, o_ref):
        o_ref[...] = _dot(a_ref[...], b_ref[...], _NN).astype(out_dtype)

    params = pltpu.CompilerParams(dimension_semantics=("arbitrary", "arbitrary"), vmem_limit_bytes=VMEM_LIMIT)
    return pl.pallas_call(body, grid=(M // tm, N // tn),
                          in_specs=[pl.BlockSpec((tm, K), lambda i, j: (i, 0)),
                                    pl.BlockSpec((K, tn), lambda i, j: (0, j))],
                          out_specs=pl.BlockSpec((tm, tn), lambda i, j: (i, j)),
                          out_shape=_sds((M, N), out_dtype), compiler_params=params, name=name)(a, b)


def _matmul_tn(a, b, out_dtype, name, scale=1.0):
    K, M = a.shape
    N = b.shape[1]
    tm = next(t for t in (512, 256, 128) if M % t == 0)
    tn = N if N <= D_MODEL else next(t for t in (512, 256, 128) if N % t == 0)

    def body(a_ref, b_ref, o_ref):
        acc = _dot(a_ref[...], b_ref[...], _TN)
        o_ref[...] = (acc if scale == 1.0 else acc * scale).astype(out_dtype)

    params = pltpu.CompilerParams(dimension_semantics=("arbitrary", "arbitrary"), vmem_limit_bytes=VMEM_LIMIT)
    return pl.pallas_call(body, grid=(M // tm, N // tn),
                          in_specs=[pl.BlockSpec((K, tm), lambda i, j: (0, i)),
                                    pl.BlockSpec((K, tn), lambda i, j: (0, j))],
                          out_specs=pl.BlockSpec((tm, tn), lambda i, j: (i, j)),
                          out_shape=_sds((M, N), out_dtype), compiler_params=params, name=name)(a, b)


def _local_step(x, pos, target, p, fetch, publish, progress):
    L = x.shape[0]
    T = L // SCAN_CHUNKS
    cos_t, sin_t = _rope_tables(pos.reshape(L, 1))
    w_in, = fetch(("w_in",), None)
    hn, u, q, k, v = _in_proj(x, p["g_pre_mix"], w_in, cos_t, sin_t)

    ssm = {n: _to_2d(n, p[n]) for n in ("ssm_lambda_re", "ssm_lambda_im", "ssm_log_dt", "ssm_b_re", "ssm_b_im",
                                        "ssm_c_re", "ssm_c_im")}
    d_row = p["ssm_d"].reshape(1, SSM_WIDTH)
    a_re, a_im, bt_re, bt_im, ct_re, ct_im = _ssm_prep(
        ssm["ssm_lambda_re"], ssm["ssm_lambda_im"], ssm["ssm_log_dt"], ssm["ssm_b_re"], ssm["ssm_b_im"],
        ssm["ssm_c_re"], ssm["ssm_c_im"])

    u_c = u.reshape(L, SSM_WIDTH)
    cx, x_re, x_im = _ssm_core_fwd(u_c, bt_re, bt_im, ct_re, ct_im, a_re, a_im)
    w_glu, = fetch(("w_glu",), cx)
    y, z, n_ssm = _ssm_out(cx, u_c, d_row, w_glu, p["b_glu"], p["g_ssm_out"])

    sinks = p["attn_sinks"].reshape(N_Q_HEADS)
    o, n_attn, probs = _attn_fwd(q, k, v, sinks, p["g_attn_out"])
    w_out, = fetch(("w_out",), n_attn)
    merged, mo, h1, hn2 = _out_proj(n_ssm, n_attn, x, w_out, p["g_post_mix"], p["g_pre_ffn"])
    w_gate_up, w_down = fetch(("w_gate_up", "w_down"), hn2)
    act_t, dgu_t, dff, dh1, loss, dg_post_ffn, dg_pre_ffn = _ffn(
        hn2, h1, target, w_gate_up, w_down, p["g_pre_ffn"], p["g_post_ffn"])
    grads = {"g_post_ffn": dg_post_ffn, "g_pre_ffn": dg_pre_ffn}
    tokens = publish({"w_down": _matmul_nn(act_t, dff, _BF16, "grad_w_down"),
                      "w_gate_up": _matmul_nn(dgu_t, hn2, _BF16, "grad_w_gate_up")})

    dmo, dn_ssm, dn_attn, grads["g_post_mix"] = _out_proj_bwd(dh1, mo, w_out, p["g_post_mix"], tokens)
    grad_w_out = _matmul_tn(merged, dmo, _BF16, "grad_w_out")

    dq, dk, dv, dsink, grads["g_attn_out"] = _attn_bwd(q, k, v, o, dn_attn, probs, p["g_attn_out"])
    grads["attn_sinks"] = dsink

    gy, dz, dy, dud, grads["g_ssm_out"], grads["b_glu"], dd = _ssm_out_bwd(
        dn_ssm, y, z, u_c, d_row, w_glu, p["g_ssm_out"], progress(dmo))
    tokens = publish({"w_out": grad_w_out, "w_glu": _matmul_tn(dz, gy, _BF16, "grad_w_glu")})
    du_c, dct_re, dct_im, dbt_re, dbt_im, da_re, da_im = _ssm_core_bwd(
        u_c, dy, dud, x_re, x_im, bt_re, bt_im, ct_re, ct_im, a_re, a_im, tokens)
    ssm_pack = _ssm_param_bwd(
        da_re, da_im, dbt_re, dbt_im, dct_re, dct_im,
        ssm["ssm_lambda_re"], ssm["ssm_lambda_im"], ssm["ssm_log_dt"], ssm["ssm_b_re"], ssm["ssm_b_im"],
        dd.reshape(SSM_GROUPS, SSM_GROUP))
    grads.update(ssm_pack=ssm_pack, loss=loss)
    publish(grads)

    du = du_c.reshape(_chunk_shape(L, SSM_WIDTH))
    dproj, grad_x, g_pre_mix = _in_proj_bwd(du, dq, dk, dv, cos_t, sin_t, x, dh1, p["g_pre_mix"], w_in, [ssm_pack])
    publish({"g_pre_mix": g_pre_mix, "w_in": _matmul_tn(dproj, hn, _BF16, "grad_w_in")})
    return grad_x


_MESH = pl.DeviceIdType.MESH
_PEERS = N_DEV - 1


def _mesh_pos():
    return lax.axis_index("x"), lax.axis_index("y"), lax.axis_index("c")


def _dev_index(px, py, pc):
    return 4 * px + 2 * py + pc


def _peer(x, y, c, r):
    return (x ^ ((r >> 2) & 1), y ^ ((r >> 1) & 1), c ^ (r & 1))


def _sequencer_exchange(sources, blocked, name, collective_id):
    n = len(sources)
    flags = blocked

    def body(*refs):
        srcs, zones = refs[:n], refs[n:2 * n]
        send_sems, recv_sems, local_sems = refs[2 * n:]
        x, y, c = _mesh_pos()
        me = _dev_index(x, y, c)
        barrier = pltpu.get_barrier_semaphore()
        for r in range(1, N_DEV):
            pl.semaphore_signal(barrier, inc=1, device_id=_peer(x, y, c, r), device_id_type=_MESH)
        pl.semaphore_wait(barrier, _PEERS)
        local, sends, recvs = [], [], []
        for w in range(n):
            cp = pltpu.make_async_copy(srcs[w].at[me] if flags[w] else srcs[w], zones[w].at[me], local_sems.at[w])
            cp.start()
            local.append(cp)
            for r in range(1, N_DEV):
                peer = _peer(x, y, c, r)
                idx = _dev_index(*peer)
                k = _PEERS * w + r - 1
                src = srcs[w].at[idx] if flags[w] else srcs[w]
                send = pltpu.make_async_remote_copy(
                    src_ref=src, dst_ref=zones[w].at[me], send_sem=send_sems.at[k], recv_sem=recv_sems.at[k],
                    device_id=peer, device_id_type=_MESH)
                send.start()
                sends.append(send)
                recvs.append(pltpu.make_async_remote_copy(
                    src_ref=src, dst_ref=zones[w].at[idx], send_sem=send_sems.at[k], recv_sem=recv_sems.at[k],
                    device_id=peer, device_id_type=_MESH))
        for cp in recvs:
            cp.wait_recv()
        for cp in sends:
            cp.wait_send()
        for cp in local:
            cp.wait()

    return pl.kernel(
        body, name=name,
        out_type=[_sds((N_DEV,) + (s.shape[1:] if f else s.shape), s.dtype) for s, f in zip(sources, flags)],
        mesh=plsc.ScalarSubcoreMesh(axis_name="sequencer", num_cores=1),
        scratch_types=[pltpu.SemaphoreType.DMA((_PEERS * n,)), pltpu.SemaphoreType.DMA((_PEERS * n,)),
                       pltpu.SemaphoreType.DMA((n,))],
        compiler_params=pltpu.CompilerParams(collective_id=collective_id),
    )(*sources)


def _sequencer_gather(shards, name, collective_id):
    n = len(shards)
    fan = 4

    def body(*refs):
        srcs, zones = refs[:n], refs[n:2 * n]
        send_sems, recv_sems, local_sems = refs[2 * n:]
        x, y, c = _mesh_pos()
        me, sibling = (x, y, c), (x, y, 1 - c)
        chips = [(1 - x, y), (x, 1 - y), (1 - x, 1 - y)]
        barrier = pltpu.get_barrier_semaphore()
        for peer in [sibling] + [(*chip, c) for chip in chips]:
            pl.semaphore_signal(barrier, inc=1, device_id=peer, device_id_type=_MESH)
        pl.semaphore_wait(barrier, fan)

        def copy(w, k, block, to, src=None):
            slot = zones[w].at[_dev_index(*block)]
            return pltpu.make_async_remote_copy(
                src_ref=slot if src is None else src, dst_ref=slot,
                send_sem=send_sems.at[_PEERS * w + k], recv_sem=recv_sems.at[_PEERS * w + k],
                device_id=to, device_id_type=_MESH)

        mine, first, passed = [], [], []
        for w in range(n):
            cp = pltpu.make_async_copy(srcs[w], zones[w].at[_dev_index(*me)], local_sems.at[w])
            cp.start()
            mine.append(cp)
            sends = [copy(w, 0, me, sibling, src=srcs[w])]
            sends += [copy(w, 1 + j, me, (*chip, c), src=srcs[w]) for j, chip in enumerate(chips)]
            for cp in sends:
                cp.start()
            first += sends
        for w in range(n):
            for j, chip in enumerate(chips):
                copy(w, 1 + j, (*chip, c), me).wait_recv()
                cp = copy(w, fan + j, (*chip, c), sibling)
                cp.start()
                passed.append(cp)
        for w in range(n):
            copy(w, 0, sibling, me).wait_recv()
            for j, chip in enumerate(chips):
                copy(w, fan + j, (*chip, 1 - c), me).wait_recv()
        for cp in first + passed:
            cp.wait_send()
        for cp in mine:
            cp.wait()

    return pl.kernel(
        body, name=name, out_type=[_sds((N_DEV,) + s.shape, s.dtype) for s in shards],
        mesh=plsc.ScalarSubcoreMesh(axis_name="sequencer", num_cores=1),
        scratch_types=[pltpu.SemaphoreType.DMA((_PEERS * n,)), pltpu.SemaphoreType.DMA((_PEERS * n,)),
                       pltpu.SemaphoreType.DMA((n,))],
        compiler_params=pltpu.CompilerParams(collective_id=collective_id),
    )(*shards)


N_CHIPS = N_DEV // 2


def _sequencer_pair_exchange(sources, name, collective_id):
    n = len(sources)

    def body(*refs):
        srcs, zones = refs[:n], refs[n:2 * n]
        send_sems, recv_sems = refs[2 * n:]
        x, y, c = _mesh_pos()
        sibling = (x, y, 1 - c)
        barrier = pltpu.get_barrier_semaphore()
        pl.semaphore_signal(barrier, inc=1, device_id=sibling, device_id_type=_MESH)
        pl.semaphore_wait(barrier, 1)
        copies = []
        for w in range(n):
            for j in range(N_CHIPS):
                k = N_CHIPS * w + j
                cp = pltpu.make_async_remote_copy(
                    src_ref=srcs[w].at[2 * j + 1 - c], dst_ref=zones[w].at[j],
                    send_sem=send_sems.at[k], recv_sem=recv_sems.at[k], device_id=sibling, device_id_type=_MESH)
                cp.start()
                copies.append(cp)
        for cp in copies:
            cp.wait_recv()
        for cp in copies:
            cp.wait_send()

    return pl.kernel(
        body, name=name, out_type=[_sds((N_CHIPS,) + s.shape[1:], s.dtype) for s in sources],
        mesh=plsc.ScalarSubcoreMesh(axis_name="sequencer", num_cores=1),
        scratch_types=[pltpu.SemaphoreType.DMA((N_CHIPS * n,)), pltpu.SemaphoreType.DMA((N_CHIPS * n,))],
        compiler_params=pltpu.CompilerParams(collective_id=collective_id),
    )(*sources)


def _pair_sum(source, received, core, name, tokens=()):
    _, rows, cols = source.shape
    tr = _row_tile(rows)
    n_tok = len(tokens)

    def body(core_ref, s_ref, r_ref, *rest):
        o_ref = rest[n_tok]
        o_ref[...] = (s_ref[...].astype(_F32) + r_ref[...].astype(_F32)).astype(o_ref.dtype)

    quarter = pl.BlockSpec((N_CHIPS, tr, cols), lambda i, core_ref: (0, i, 0))
    mine = pl.BlockSpec((N_CHIPS, None, tr, cols), lambda i, core_ref: (0, core_ref[0], i, 0))
    spec = pltpu.PrefetchScalarGridSpec(
        num_scalar_prefetch=1, grid=(rows // tr,),
        in_specs=[mine, quarter] + [pl.BlockSpec(memory_space=pl.ANY)] * n_tok, out_specs=quarter)
    params = pltpu.CompilerParams(dimension_semantics=("arbitrary",), vmem_limit_bytes=VMEM_LIMIT)
    return pl.pallas_call(body, grid_spec=spec, out_shape=_sds((N_CHIPS, rows, cols), source.dtype),
                          compiler_params=params, name=name)(
        core, source.reshape(N_CHIPS, 2, rows, cols), received, *tokens)


def _sequencer_chip_exchange(partials, name, collective_id):
    n = len(partials)
    others = N_CHIPS - 1

    def body(*refs):
        srcs, zones = refs[:n], refs[n:2 * n]
        send_sems, recv_sems, local_sems = refs[2 * n:]
        x, y, c = _mesh_pos()
        mine = 2 * x + y
        peers = [(x ^ (r >> 1), y ^ (r & 1), c) for r in range(1, N_CHIPS)]
        barrier = pltpu.get_barrier_semaphore()
        for peer in peers:
            pl.semaphore_signal(barrier, inc=1, device_id=peer, device_id_type=_MESH)
        pl.semaphore_wait(barrier, others)
        local, sends, recvs = [], [], []
        for w in range(n):
            cp = pltpu.make_async_copy(srcs[w].at[mine], zones[w].at[mine], local_sems.at[w])
            cp.start()
            local.append(cp)
            for r, peer in enumerate(peers):
                theirs = 2 * peer[0] + peer[1]
                k = others * w + r
                send = pltpu.make_async_remote_copy(
                    src_ref=srcs[w].at[theirs], dst_ref=zones[w].at[mine],
                    send_sem=send_sems.at[k], recv_sem=recv_sems.at[k], device_id=peer, device_id_type=_MESH)
                send.start()
                sends.append(send)
                recvs.append(pltpu.make_async_remote_copy(
                    src_ref=srcs[w].at[theirs], dst_ref=zones[w].at[theirs],
                    send_sem=send_sems.at[k], recv_sem=recv_sems.at[k], device_id=peer, device_id_type=_MESH))
        for cp in recvs:
            cp.wait_recv()
        for cp in sends:
            cp.wait_send()
        for cp in local:
            cp.wait()

    return pl.kernel(
        body, name=name, out_type=[_sds(s.shape, s.dtype) for s in partials],
        mesh=plsc.ScalarSubcoreMesh(axis_name="sequencer", num_cores=1),
        scratch_types=[pltpu.SemaphoreType.DMA((others * n,)), pltpu.SemaphoreType.DMA((others * n,)),
                       pltpu.SemaphoreType.DMA((n,))],
        compiler_params=pltpu.CompilerParams(collective_id=collective_id),
    )(*partials)


def _row_tile(rows):
    return next(t for t in range(min(rows, 256), 0, -16) if rows % t == 0)


def _sum_parts(parts, name, tokens=()):
    _, rows, cols = parts.shape
    tr = _row_tile(rows)

    def body(p_ref, g_ref):
        g = p_ref[0].astype(_F32)
        for s in range(1, N_DEV):
            g = g + p_ref[s].astype(_F32)
        g_ref[...] = g

    return _call(body, (rows // tr,), [pl.BlockSpec((N_DEV, tr, cols), lambda i: (0, i, 0))],
                 _rows(tr, cols), _sds((rows, cols), _F32), name, tokens=tokens)(parts)


def _adam_update(g, w, m, v):
    new_m = ADAM_B1 * m + (1.0 - ADAM_B1) * g
    new_v = ADAM_B2 * v + (1.0 - ADAM_B2) * (g * g)
    m_hat = new_m / (1.0 - ADAM_B1 ** ADAM_STEP)
    v_hat = new_v / (1.0 - ADAM_B2 ** ADAM_STEP)
    return -ADAM_LR * (m_hat / (jnp.sqrt(v_hat) + ADAM_EPS) + ADAM_WD * w), new_m, new_v


def _adamw_small(parts, items, sums, name, tokens=()):
    n_p, n_i = len(parts), len(items)

    def body(*refs):
        p_refs, state, outs = refs[:n_p], refs[n_p:n_p + 3 * n_i], refs[n_p + 3 * n_i:]

        def total(part, rows, cols):
            shift = cols.start % _LANES
            window = slice(cols.start - shift, cols.start - shift + _LANES) if shift else cols
            n_rows = rows.stop - rows.start
            narrow = p_refs[part].dtype.itemsize < 4 and n_rows % _PACK_TILE
            tile = slice(rows.start, rows.start + _PACK_TILE) if narrow else rows
            g = p_refs[part][0, tile, window].astype(_F32)
            for s in range(1, N_DEV):
                g = g + p_refs[part][s, tile, window].astype(_F32)
            g = g[:n_rows] if narrow else g
            return pltpu.roll(g, _LANES - shift, 1)[:, :cols.stop - cols.start] if shift else g

        for i, (part, rows, cols, _, _, _) in enumerate(items):
            g = total(part, rows, cols)
            w_ref, m_ref, v_ref = state[3 * i:3 * i + 3]
            delta, new_m, new_v = _adam_update(g, w_ref[...], m_ref[...], v_ref[...])
            outs[4 * i][...] = g
            outs[4 * i + 1][...] = delta
            outs[4 * i + 2][...] = new_m
            outs[4 * i + 3][...] = new_v
        for j, (part, rows, cols) in enumerate(sums):
            outs[4 * n_i + j][...] = total(part, rows, cols)

    ins = list(parts) + [a for item in items for a in item[3:]]
    out_shapes = [item[3].shape for item in items for _ in range(4)]
    out_shapes += [(rows.stop - rows.start, cols.stop - cols.start) for _, rows, cols in sums]
    out = _call(body, (1,), [_whole(a.shape) for a in ins], [_whole(s) for s in out_shapes],
                [_sds(s, _F32) for s in out_shapes], name, tokens=tokens)(*ins)
    return [out[4 * i:4 * i + 4] for i in range(n_i)], out[4 * n_i:]


def _adamw(parts, w, m, v, name, tokens=()):
    rows, cols = w.shape
    tr = _row_tile(rows)
    n_parts = parts.shape[0]

    def body(p_ref, w_ref, m_ref, v_ref, g_ref, d_ref, nm_ref, nv_ref):
        g = p_ref[0].astype(_F32)
        for s in range(1, n_parts):
            g = g + p_ref[s].astype(_F32)
        new_m = ADAM_B1 * m_ref[...] + (1.0 - ADAM_B1) * g
        new_v = ADAM_B2 * v_ref[...] + (1.0 - ADAM_B2) * (g * g)
        m_hat = new_m / (1.0 - ADAM_B1 ** ADAM_STEP)
        v_hat = new_v / (1.0 - ADAM_B2 ** ADAM_STEP)
        g_ref[...] = g
        d_ref[...] = -ADAM_LR * (m_hat / (jnp.sqrt(v_hat) + ADAM_EPS) + ADAM_WD * w_ref[...])
        nm_ref[...] = new_m
        nv_ref[...] = new_v

    blk = _rows(tr, cols)
    return _call(body, (rows // tr,),
                 [pl.BlockSpec((n_parts, tr, cols), lambda i: (0, i, 0)), blk, blk, blk],
                 [blk] * 4, [_sds((rows, cols), _F32)] * 4, name, tokens=tokens)(parts, w, m, v)


_SMALL = ("g_pre_mix", "ssm_lambda_re", "ssm_lambda_im", "ssm_log_dt", "ssm_b_re", "ssm_b_im",
          "ssm_c_re", "ssm_c_im", "ssm_d", "b_glu", "attn_sinks", "g_ssm_out", "g_attn_out",
          "g_post_mix", "g_pre_ffn", "g_post_ffn")
_BIG = ("w_in", "w_glu", "w_out", "w_gate_up", "w_down")
_WEIGHTS = ("g_pre_mix", "w_in", "ssm_lambda_re", "ssm_lambda_im", "ssm_log_dt", "ssm_b_re", "ssm_b_im",
            "ssm_c_re", "ssm_c_im", "ssm_d", "w_glu", "b_glu", "attn_sinks", "g_ssm_out", "g_attn_out",
            "w_out", "g_post_mix", "g_pre_ffn", "w_gate_up", "w_down", "g_post_ffn")
_LANES = 128


_SHAPE_2D = {
    "g_pre_mix": (1, D_MODEL), "ssm_lambda_re": (SSM_GROUPS, SSM_STATE), "ssm_lambda_im": (SSM_GROUPS, SSM_STATE),
    "ssm_log_dt": (1, SSM_GROUPS), "ssm_b_re": (SSM_WIDTH, SSM_STATE), "ssm_b_im": (SSM_WIDTH, SSM_STATE),
    "ssm_c_re": (SSM_WIDTH, SSM_STATE), "ssm_c_im": (SSM_WIDTH, SSM_STATE), "ssm_d": (SSM_GROUPS, SSM_GROUP),
    "b_glu": (1, 2 * SSM_WIDTH), "attn_sinks": (1, N_Q_HEADS), "g_ssm_out": (1, SSM_WIDTH),
    "g_attn_out": (1, ATTN_WIDTH), "g_post_mix": (1, D_MODEL), "g_pre_ffn": (1, D_MODEL), "g_post_ffn": (1, D_MODEL)}
_ROW_WIDTH = {"g_pre_mix": D_MODEL, "b_glu": 2 * SSM_WIDTH, "attn_sinks": _LANES, "g_ssm_out": SSM_WIDTH,
              "g_attn_out": ATTN_WIDTH, "g_post_mix": D_MODEL, "g_pre_ffn": D_MODEL, "g_post_ffn": D_MODEL,
              "loss": _LANES}
_PER_GROUP_TRANSPOSED = ("ssm_b_re", "ssm_b_im")


def _to_2d(name, a):
    if name in _PER_GROUP_TRANSPOSED:
        a = a.reshape(SSM_GROUPS, SSM_STATE, SSM_GROUP).transpose(0, 2, 1)
    return a.reshape(_SHAPE_2D[name])


def _from_2d(name, a, shape):
    if name in _PER_GROUP_TRANSPOSED:
        a = a.reshape(SSM_GROUPS, SSM_GROUP, SSM_STATE).transpose(0, 2, 1)
    return a.reshape(shape)


def _row_slots(names):
    slots, row, col = {}, 0, 0
    for n in names:
        width = _ROW_WIDTH[n]
        if col + width > D_MODEL:
            row, col = row + 1, 0
        slots[n] = (row, col, width)
        col += width
    return slots


def _stack_rows(named, slots):
    n_rows = -(-(max(r for r, _, _ in slots.values()) + 1) // 8) * 8
    lines = []
    for r in range(n_rows):
        pieces = [named[n] for n, (row, _, _) in slots.items() if row == r]
        used = sum(p.shape[1] for p in pieces)
        if used < D_MODEL:
            pieces.append(jnp.zeros((1, D_MODEL - used), _F32))
        lines.append(jnp.concatenate(pieces, axis=1) if len(pieces) > 1 else pieces[0])
    return jnp.concatenate(lines, axis=0)


def kernel(x, positions, g_pre_mix, w_in, ssm_lambda_re, ssm_lambda_im, ssm_log_dt, ssm_b_re, ssm_b_im, ssm_c_re, ssm_c_im, ssm_d, w_glu, b_glu, attn_sinks, g_ssm_out, g_attn_out, w_out, g_post_mix, g_pre_ffn, w_gate_up, w_down, g_post_ffn, loss_target, m_g_pre_mix, m_w_in, m_ssm_lambda_re, m_ssm_lambda_im, m_ssm_log_dt, m_ssm_b_re, m_ssm_b_im, m_ssm_c_re, m_ssm_c_im, m_ssm_d, m_w_glu, m_b_glu, m_attn_sinks, m_g_ssm_out, m_g_attn_out, m_w_out, m_g_post_mix, m_g_pre_ffn, m_w_gate_up, m_w_down, m_g_post_ffn, v_g_pre_mix, v_w_in, v_ssm_lambda_re, v_ssm_lambda_im, v_ssm_log_dt, v_ssm_b_re, v_ssm_b_im, v_ssm_c_re, v_ssm_c_im, v_ssm_d, v_w_glu, v_b_glu, v_attn_sinks, v_g_ssm_out, v_g_attn_out, v_w_out, v_g_post_mix, v_g_pre_ffn, v_w_gate_up, v_w_down, v_g_post_ffn):
    w = dict(g_pre_mix=g_pre_mix, w_in=w_in, ssm_lambda_re=ssm_lambda_re, ssm_lambda_im=ssm_lambda_im,
             ssm_log_dt=ssm_log_dt, ssm_b_re=ssm_b_re, ssm_b_im=ssm_b_im, ssm_c_re=ssm_c_re, ssm_c_im=ssm_c_im,
             ssm_d=ssm_d, w_glu=w_glu, b_glu=b_glu, attn_sinks=attn_sinks, g_ssm_out=g_ssm_out,
             g_attn_out=g_attn_out, w_out=w_out, g_post_mix=g_post_mix, g_pre_ffn=g_pre_ffn,
             w_gate_up=w_gate_up, w_down=w_down, g_post_ffn=g_post_ffn)
    m = dict(g_pre_mix=m_g_pre_mix, w_in=m_w_in, ssm_lambda_re=m_ssm_lambda_re, ssm_lambda_im=m_ssm_lambda_im,
             ssm_log_dt=m_ssm_log_dt, ssm_b_re=m_ssm_b_re, ssm_b_im=m_ssm_b_im, ssm_c_re=m_ssm_c_re,
             ssm_c_im=m_ssm_c_im, ssm_d=m_ssm_d, w_glu=m_w_glu, b_glu=m_b_glu, attn_sinks=m_attn_sinks,
             g_ssm_out=m_g_ssm_out, g_attn_out=m_g_attn_out, w_out=m_w_out, g_post_mix=m_g_post_mix,
             g_pre_ffn=m_g_pre_ffn, w_gate_up=m_w_gate_up, w_down=m_w_down, g_post_ffn=m_g_post_ffn)
    v = dict(g_pre_mix=v_g_pre_mix, w_in=v_w_in, ssm_lambda_re=v_ssm_lambda_re, ssm_lambda_im=v_ssm_lambda_im,
             ssm_log_dt=v_ssm_log_dt, ssm_b_re=v_ssm_b_re, ssm_b_im=v_ssm_b_im, ssm_c_re=v_ssm_c_re,
             ssm_c_im=v_ssm_c_im, ssm_d=v_ssm_d, w_glu=v_w_glu, b_glu=v_b_glu, attn_sinks=v_attn_sinks,
             g_ssm_out=v_g_ssm_out, g_attn_out=v_g_attn_out, w_out=v_w_out, g_post_mix=v_g_post_mix,
             g_pre_ffn=v_g_pre_ffn, w_gate_up=v_w_gate_up, w_down=v_w_down, g_post_ffn=v_g_post_ffn)

    transposed = ("w_in", "w_glu", "w_gate_up")
    native_transposed = ("w_in", "w_gate_up")
    shard = {n: (w[n][0].T if n in transposed else w[n][0]).astype(_BF16) for n in _BIG}
    gathered = {}
    for cid, names in enumerate((("w_in",), ("w_glu", "w_out"), ("w_gate_up", "w_down")), start=1):
        lands = _sequencer_gather([shard[n] for n in names], "gather_" + names[0], cid)
        gathered.update({n: a.reshape(-1, a.shape[2]) for n, a in zip(names, lands)})

    def fetch(names, after):
        del after
        return [gathered[n] for n in names]

    sent = []
    ids = iter(range(4, 16))
    two_step = {}

    def publish(named):
        big = [n for n in named if n in _BIG]
        if set(big) == {"w_gate_up", "w_down"}:
            blocks = [named[n].reshape(N_DEV, -1, named[n].shape[1]) for n in big]
            two_step.update(names=big, blocks=blocks,
                            received=_sequencer_pair_exchange(blocks, "grads_pair", next(ids)))
            return [named[n] for n in big]
        rows = [n for n in named if n in _ROW_WIDTH]
        plain = [n for n in named if n not in big + rows]
        sources = [named[n].reshape(N_DEV, -1, named[n].shape[1]) for n in big]
        slots = _row_slots(rows)
        if rows:
            sources.append(_stack_rows(named, slots))
        sources += [named[n] for n in plain]
        flags = [True] * len(big) + [False] * (len(sources) - len(big))
        cid = next(ids)
        if big:
            lands = _sequencer_exchange(sources, flags, "grads_%d" % cid, cid)
        else:
            lands = _sequencer_gather(sources, "grads_%d" % cid, cid)
        sent.append((big, slots, plain, lands))
        return [named[n] for n in big]

    def progress(after):
        core = lax.axis_index("c").astype(jnp.int32).reshape(1)
        partials = [_pair_sum(b, r, core, "pair_sum_" + n, [after])
                    for n, b, r in zip(two_step["names"], two_step["blocks"], two_step["received"])]
        sent.append((two_step["names"], {}, [], _sequencer_chip_exchange(partials, "grads_chips", next(ids))))
        return partials

    p = {n: w[n] for n in _SMALL}
    grad_x = _local_step(x[0], positions[0], loss_target[0], p, fetch, publish, progress)

    state = {n: [_to_2d(n, a) for a in (w[n], m[n], v[n])] for n in _SMALL}
    result = {}
    total_loss = None
    chain = []
    for big, slots, plain, lands in sent:
        lands = list(lands)
        after = list(chain)
        for name in big:
            part = lands.pop(0)
            if name in native_transposed:
                updated = _adamw(part, w[name][0].T, m[name][0].T, v[name][0].T, "adamw_" + name, after)
                result[name] = [a.T[None] for a in updated]
                chain.append(updated[3])
                continue
            if name in transposed:
                part = _sum_parts(part, "sum_" + name, after).T[None]
            updated = _adamw(part, w[name][0], m[name][0], v[name][0], "adamw_" + name, after)
            result[name] = [a[None] for a in updated]
            chain.append(updated[3])
        parts, items, sums, names = [], [], [], []
        if slots:
            parts.append(lands.pop(0))
            for name, (row, col, _) in slots.items():
                if name == "loss":
                    sums.append((0, slice(row, row + 1), slice(col, col + _LANES)))
                else:
                    items.append((0, slice(row, row + 1), slice(col, col + _SHAPE_2D[name][1]), *state[name]))
                    names.append(name)
        for name in plain:
            packed = _SSM_PACK if name == "ssm_pack" else {name: (0, _SHAPE_2D[name][0], 0, _SHAPE_2D[name][1])}
            for member, (first, rows_n, lane, cols_n) in packed.items():
                items.append((len(parts), slice(first, first + rows_n), slice(lane, lane + cols_n), *state[member]))
                names.append(member)
            parts.append(lands.pop(0))
        if items:
            updated, summed = _adamw_small(parts, items, sums, "adamw_small_" + names[0], after)
            chain.append(updated[0][3])
            result.update(dict(zip(names, updated)))
            if summed:
                total_loss = summed[0][0, 0]

    out = [total_loss, grad_x[None]]
    for kind in range(4):
        out += [_from_2d(n, result[n][kind], w[n].shape) for n in _WEIGHTS]
    return tuple(out)
```

```python
import math

import numpy as np
import jax
import jax.numpy as jnp
from jax import lax
from jax.experimental import pallas as pl
from jax.experimental.pallas import tpu as pltpu
from jax.experimental.pallas import tpu_sc as plsc

D_MODEL = 1024
SSM_WIDTH = 512
SSM_GROUP = 16
SSM_GROUPS = 32
SSM_STATE = 64
N_STATE = SSM_GROUPS * SSM_STATE
ATTN_WIDTH = 512
HEAD_DIM = 64
N_Q_HEADS = 8
N_KV_HEADS = 2
Q_PER_KV = 4
KV_WIDTH = 128
IN_WIDTH = 1280
BLOCK = 128
ROPE_DIM = 16
ROPE_THETA = 500000.0
D_FF = 2816
NORM_EPS = 1e-6
MASK_VALUE = -1e30
ADAM_LR = 0.001
ADAM_B1 = 0.9
ADAM_B2 = 0.999
ADAM_EPS = 1e-08
ADAM_WD = 0.01
ADAM_STEP = 10

N_DEV = 8
SCAN_CHUNKS = 8
SCAN_UNROLL = 8
FFN_CHUNK = 2816
TOKEN_TILE = 256
VMEM_LIMIT = 56 * 1024 * 1024

_F32 = jnp.float32
_BF16 = jnp.bfloat16
_MXU = jnp.bfloat16

_NN = ((1,), (0,))
_NT = ((1,), (1,))
_TN = ((0,), (0,))


def _dot(a, b, dims):
    return lax.dot_general(a.astype(_MXU), b.astype(_MXU), (dims, ((), ())),
                           preferred_element_type=_F32)


def _dot_exact(a, b, dims):
    return lax.dot_general(a.astype(_F32), b.astype(_F32), (dims, ((), ())),
                           precision=lax.Precision.HIGHEST, preferred_element_type=_F32)


def _iota(shape, dim):
    return lax.broadcasted_iota(jnp.int32, shape, dim)


def _rms_fwd(x, g):
    r = lax.rsqrt(jnp.mean(x * x, axis=-1, keepdims=True) + NORM_EPS)
    return x * r * g, r


def _rms_bwd(dy, x, g, r):
    a = dy * g
    xn = x * r
    dx = r * (a - xn * jnp.mean(a * xn, axis=-1, keepdims=True))
    dg = jnp.sum(dy * xn, axis=0, keepdims=True)
    return dx, dg


def _call(body, grid, in_specs, out_specs, out_shape, name, scratch=(), tokens=()):
    params = pltpu.CompilerParams(dimension_semantics=("arbitrary",) * len(grid),
                                  vmem_limit_bytes=VMEM_LIMIT)
    n_in, n_tok = len(in_specs), len(tokens)

    def run(*refs):
        return body(*refs[:n_in], *refs[n_in + n_tok:])

    call = pl.pallas_call(run, grid=grid,
                          in_specs=list(in_specs) + [pl.BlockSpec(memory_space=pl.ANY)] * n_tok,
                          out_specs=out_specs, out_shape=out_shape, scratch_shapes=list(scratch),
                          compiler_params=params, name=name)
    return lambda *args: call(*args, *tokens)


def _rows(tm, n):
    return pl.BlockSpec((tm, n), lambda i: (i, 0))


def _whole(shape):
    nd = len(shape)
    return pl.BlockSpec(shape, lambda i: (0,) * nd)


def _sds(shape, dtype):
    return jax.ShapeDtypeStruct(shape, dtype)


def _tile(L):
    return min(TOKEN_TILE, L)


def _chunk_tile(L):
    return L // SCAN_CHUNKS


def _chunk_block(L, n):
    return pl.BlockSpec((_chunk_tile(L), n), lambda i: (0, i))


def _chunk_shape(L, n):
    return (_chunk_tile(L), SCAN_CHUNKS * n)


def _accumulate(ref, val, first):
    @pl.when(first)
    def _():
        ref[...] = val

    @pl.when(jnp.logical_not(first))
    def _():
        ref[...] += val


def _rope_rows():
    half = ROPE_DIM // 2
    inv = (np.float32(ROPE_THETA) ** (-np.arange(half, dtype=np.float32) * np.float32(2.0) / np.float32(ROPE_DIM))).astype(np.float32)
    col = np.arange(KV_WIDTH) % HEAD_DIM
    freq = np.where(col < ROPE_DIM, inv[col % half], 0.0).astype(np.float32)
    sign = np.where(col < half, -1.0, np.where(col < ROPE_DIM, 1.0, 0.0)).astype(np.float32)
    return freq[None, :], sign[None, :]


def _rope_tables(pos_col):
    L = pos_col.shape[0]
    tm = _tile(L)
    freq, sign = _rope_rows()

    def body(pos_ref, freq_ref, sign_ref, cos_ref, sin_ref):
        ang = pos_ref[...].astype(_F32) * freq_ref[...]
        cos_ref[...] = jnp.cos(ang)
        sin_ref[...] = jnp.sin(ang) * sign_ref[...]

    return _call(body, (L // tm,),
                 [_rows(tm, 1), _whole((1, KV_WIDTH)), _whole((1, KV_WIDTH))],
                 [_rows(tm, KV_WIDTH), _rows(tm, KV_WIDTH)],
                 [_sds((L, KV_WIDTH), _F32)] * 2, "rope_tables")(pos_col, jnp.asarray(freq), jnp.asarray(sign))


def _widen(t, width):
    return t if width == KV_WIDTH else jnp.concatenate([t] * (width // KV_WIDTH), axis=1)


def _rope_partner(t):
    w = t.shape[1]
    in_head = _iota((1, w), 1) & (HEAD_DIM - 1)
    second = jnp.where(in_head < ROPE_DIM, pltpu.roll(t, ROPE_DIM // 2, 1), 0.0)
    return jnp.where(in_head < ROPE_DIM // 2, pltpu.roll(t, w - ROPE_DIM // 2, 1), second)


def _rope_apply(t, cos_t, sin_t):
    w = t.shape[1]
    return t * _widen(cos_t, w) + _rope_partner(t) * _widen(sin_t, w)


def _rope_transpose(dt, cos_t, sin_t):
    w = dt.shape[1]
    return dt * _widen(cos_t, w) + _rope_partner(dt * _widen(sin_t, w))


def _in_proj(x, g_pre_mix, w_in, cos_t, sin_t):
    L = x.shape[0]
    tm = _chunk_tile(L)

    def body(x_ref, g_ref, w_ref, cos_ref, sin_ref, hn_ref, u_ref, q_ref, k_ref, v_ref):
        hn, _ = _rms_fwd(x_ref[...], g_ref[...])
        hn = hn.astype(_BF16)
        hn_ref[...] = hn
        proj = _dot(hn, w_ref[...], _NT)
        u_ref[...] = proj[:, :SSM_WIDTH]
        q = proj[:, SSM_WIDTH:SSM_WIDTH + ATTN_WIDTH]
        k = proj[:, SSM_WIDTH + ATTN_WIDTH:SSM_WIDTH + ATTN_WIDTH + KV_WIDTH]
        cos_v, sin_v = cos_ref[...], sin_ref[...]
        q_ref[...] = _rope_apply(q, cos_v, sin_v).astype(_BF16)
        k_ref[...] = _rope_apply(k, cos_v, sin_v).astype(_BF16)
        v_ref[...] = proj[:, SSM_WIDTH + ATTN_WIDTH + KV_WIDTH:].astype(_BF16)

    return _call(body, (L // tm,),
                 [_rows(tm, D_MODEL), _whole((1, D_MODEL)), _whole((IN_WIDTH, D_MODEL)),
                  _rows(tm, KV_WIDTH), _rows(tm, KV_WIDTH)],
                 [_rows(tm, D_MODEL), _chunk_block(L, SSM_WIDTH), _rows(tm, ATTN_WIDTH),
                  _rows(tm, KV_WIDTH), _rows(tm, KV_WIDTH)],
                 [_sds((L, D_MODEL), _BF16), _sds(_chunk_shape(L, SSM_WIDTH), _F32), _sds((L, ATTN_WIDTH), _BF16),
                  _sds((L, KV_WIDTH), _BF16), _sds((L, KV_WIDTH), _BF16)],
                 "in_proj")(x, g_pre_mix, w_in, cos_t, sin_t)


def _s5_discretize(lam_re, lam_im, log_dt):
    lr = jnp.minimum(lam_re, -1e-4)
    li = lam_im
    dt = jnp.exp(log_dt)
    mag = jnp.exp(lr * dt)
    ar = mag * jnp.cos(li * dt)
    ai = mag * jnp.sin(li * dt)
    den = lr * lr + li * li
    fr = ((ar - 1.0) * lr + ai * li) / den
    fi = (ai * lr - (ar - 1.0) * li) / den
    return ar, ai, fr, fi


SUPER = 4
SB_STATE = N_STATE // SUPER
SB_WIDTH = SSM_WIDTH // SUPER


def _sb_state(k):
    return slice(SB_STATE * k, SB_STATE * (k + 1))


def _sb_width(k):
    return slice(SB_WIDTH * k, SB_WIDTH * (k + 1))


def _dt_column(log_dt_row):
    eye = _iota((SSM_GROUPS, SSM_GROUPS), 0) == _iota((SSM_GROUPS, SSM_GROUPS), 1)
    return jnp.sum(jnp.where(eye, log_dt_row, 0.0), axis=1, keepdims=True)


def _group_masks():
    e64 = ((_iota((SSM_STATE, N_STATE), 1) & (SSM_STATE - 1)) == _iota((SSM_STATE, N_STATE), 0)).astype(_F32)
    own = _iota((SSM_GROUPS, N_STATE), 0) == (_iota((SSM_GROUPS, N_STATE), 1) >> 6)
    return e64, own


def _rows_of_group():
    return ((_iota((SSM_WIDTH, SSM_GROUPS), 0) >> 4) == _iota((SSM_WIDTH, SSM_GROUPS), 1)).astype(_F32)


def _ssm_prep(lam_re, lam_im, log_dt, b_re, b_im, c_re, c_im):
    def body(lr_ref, li_ref, ld_ref, bre, bim, cre, cim, ar_ref, ai_ref, btr, bti, ctr, cti):
        ar, ai, fr, fi = _s5_discretize(lr_ref[...], li_ref[...], _dt_column(ld_ref[...]))
        e64, own = _group_masks()
        mask_c = (_iota((SSM_WIDTH, N_STATE), 0) >> 4) == (_iota((SSM_WIDTH, N_STATE), 1) >> 6)

        def to_row(t):
            return jnp.sum(jnp.where(own, _dot_exact(t, e64, _NN), 0.0), axis=0, keepdims=True)

        def fold(m):
            full = jnp.where(mask_c, _dot(m, e64, _NN), 0.0)
            return sum(full[_sb_width(k), :] for k in range(SUPER)).astype(_BF16)

        ar_ref[...] = to_row(ar)
        ai_ref[...] = to_row(ai)
        spread = _rows_of_group()
        fr_t = _dot_exact(spread, fr, _NN)
        fi_t = _dot_exact(spread, fi, _NN)
        btr[...] = fold(fr_t * bre[...] - fi_t * bim[...])
        bti[...] = fold(fr_t * bim[...] + fi_t * bre[...])
        ctr[...] = fold(cre[...])
        cti[...] = fold(cim[...])

    row = (1, N_STATE)
    ins = [lam_re, lam_im, log_dt, b_re, b_im, c_re, c_im]
    return _call(body, (1,), [_whole(a.shape) for a in ins],
                 [_whole(row), _whole(row)] + [_whole((SB_WIDTH, N_STATE))] * 4,
                 [_sds(row, _F32), _sds(row, _F32)] + [_sds((SB_WIDTH, N_STATE), _BF16)] * 4,
                 "ssm_prep")(*ins)


def _complex_power(ar, ai, n):
    pr, pi = jnp.ones_like(ar), jnp.zeros_like(ai)
    while n:
        if n & 1:
            pr, pi = pr * ar - pi * ai, pr * ai + pi * ar
        ar, ai = ar * ar - ai * ai, 2.0 * ar * ai
        n >>= 1
    return pr, pi


def _chunk_carries(er, ei, pr, pi, reverse):
    rows = _iota(er.shape, 0)
    sr = jnp.zeros_like(pr)
    si = jnp.zeros_like(pi)
    out_r = jnp.zeros_like(er)
    out_i = jnp.zeros_like(ei)
    order = range(SCAN_CHUNKS - 1, 0, -1) if reverse else range(SCAN_CHUNKS - 1)
    for c in order:
        e_r = er[c:c + 1, :]
        e_i = ei[c:c + 1, :]
        sr, si = pr * sr - pi * si + e_r, pr * si + pi * sr + e_i
        nxt = c - 1 if reverse else c + 1
        out_r = jnp.where(rows == nxt, sr, out_r)
        out_i = jnp.where(rows == nxt, si, out_i)
    return out_r, out_i


_GELU_K = math.sqrt(2.0 / math.pi)
_GELU_C = 0.044715


def _gelu(y):
    return 0.5 * y * (1.0 + jnp.tanh(_GELU_K * (y + _GELU_C * y * y * y)))


def _gelu_grad(y):
    t = jnp.tanh(_GELU_K * (y + _GELU_C * y * y * y))
    return 0.5 * (1.0 + t) + 0.5 * y * (1.0 - t * t) * _GELU_K * (1.0 + 3.0 * _GELU_C * y * y)


def _step_rows(t):
    return pl.ds(pl.multiple_of(t * SCAN_CHUNKS, SCAN_CHUNKS), SCAN_CHUNKS)


def _scan_in_place(br, bi, ar, ai, T):
    W = br.shape[1]
    ar8 = jnp.broadcast_to(ar, (SCAN_CHUNKS, W))
    ai8 = jnp.broadcast_to(ai, (SCAN_CHUNKS, W))

    def local(t, c):
        cr, ci = c
        rows = _step_rows(t)
        return ar8 * cr - ai8 * ci + br[rows, :], ar8 * ci + ai8 * cr + bi[rows, :]

    zero = jnp.zeros((SCAN_CHUNKS, W), _F32)
    er, ei = lax.fori_loop(0, T, local, (zero, zero), unroll=SCAN_UNROLL)
    pr, pi = _complex_power(ar, ai, T)
    carries = _chunk_carries(er, ei, pr, pi, reverse=False)

    def final(t, c):
        nr, ni = local(t, c)
        rows = _step_rows(t)
        br[rows, :] = nr
        bi[rows, :] = ni
        return nr, ni

    lax.fori_loop(0, T, final, carries, unroll=SCAN_UNROLL)


def _scan_reverse_in_place(dr, di, xr, xi, ar, ai, T):
    W = dr.shape[1]
    ar8 = jnp.broadcast_to(ar, (SCAN_CHUNKS, W))
    ai8 = jnp.broadcast_to(ai, (SCAN_CHUNKS, W))

    def local(t, c):
        cr, ci = c
        rows = _step_rows(t)
        return ar8 * cr + ai8 * ci + dr[rows, :], ar8 * ci - ai8 * cr + di[rows, :]

    zero = jnp.zeros((SCAN_CHUNKS, W), _F32)
    er, ei = lax.fori_loop(0, T, lambda k, c: local(T - 1 - k, c), (zero, zero), unroll=SCAN_UNROLL)
    pr, pi = _complex_power(ar, -ai, T)
    sr, si = _chunk_carries(er, ei, pr, pi, reverse=True)

    def grad_a(acc, nr, ni, xpr, xpi):
        return acc[0] + nr * xpr + ni * xpi, acc[1] + ni * xpr - nr * xpi

    def final(k, c):
        t = T - 1 - k
        nr, ni = local(t, c[:2])
        rows = _step_rows(t)
        dr[rows, :] = nr
        di[rows, :] = ni
        before = _step_rows(t - 1)
        gr, gi = grad_a(c[2:], nr, ni, xr[before, :], xi[before, :])
        return nr, ni, gr, gi

    cr, ci, gr, gi = lax.fori_loop(0, T - 1, final, (sr, si, zero, zero), unroll=SCAN_UNROLL)
    nr, ni = local(0, (cr, ci))
    dr[_step_rows(0), :] = nr
    di[_step_rows(0), :] = ni
    first = _iota((SCAN_CHUNKS, W), 0) == 0
    last = _step_rows(T - 1)
    xpr = jnp.where(first, 0.0, pltpu.roll(xr[last, :], 1, 0))
    xpi = jnp.where(first, 0.0, pltpu.roll(xi[last, :], 1, 0))
    gr, gi = grad_a((gr, gi), nr, ni, xpr, xpi)
    return jnp.sum(gr, axis=0, keepdims=True), jnp.sum(gi, axis=0, keepdims=True)


def _ssm_super_specs(L):
    width = pl.BlockSpec((L, SB_WIDTH), lambda k: (0, k))
    matrix = pl.BlockSpec((SB_WIDTH, SB_STATE), lambda k: (0, k))
    row = pl.BlockSpec((1, SB_STATE), lambda k: (0, k))
    return width, matrix, row


def _ssm_core_fwd(u, bt_re, bt_im, ct_re, ct_im, a_re, a_im):
    L = u.shape[0]
    T = L // SCAN_CHUNKS

    def body(u_ref, br_ref, bi_ref, cr_ref, ci_ref, ar_ref, ai_ref, y_ref, xr, xi):
        ub = u_ref[...].astype(_BF16)
        xr[...] = _dot(ub, br_ref[...], _NN)
        xi[...] = _dot(ub, bi_ref[...], _NN)
        _scan_in_place(xr, xi, ar_ref[...], ai_ref[...], T)
        y_ref[...] = _dot(xr[...], cr_ref[...], _NT) - _dot(xi[...], ci_ref[...], _NT)

    width, matrix, row = _ssm_super_specs(L)
    state = pl.BlockSpec((L, SB_STATE), lambda k: (0, k))
    return _call(body, (SUPER,), [width, matrix, matrix, matrix, matrix, row, row], [width, state, state],
                 [_sds((L, SSM_WIDTH), _F32)] + [_sds((L, N_STATE), _F32)] * 2,
                 "ssm_core_fwd")(u, bt_re, bt_im, ct_re, ct_im, a_re, a_im)


def _ssm_core_bwd(u, dy, dud, x_re, x_im, bt_re, bt_im, ct_re, ct_im, a_re, a_im, tokens=()):
    L = u.shape[0]
    T = L // SCAN_CHUNKS

    def body(u_ref, dy_ref, dud_ref, xr, xi, br_ref, bi_ref, cr_ref, ci_ref, ar_ref, ai_ref,
             du_ref, dcr_ref, dci_ref, dbr_ref, dbi_ref, dar_ref, dai_ref, lr, li):
        dyb = dy_ref[...]
        lr[...] = _dot(dyb, cr_ref[...], _NN)
        li[...] = -_dot(dyb, ci_ref[...], _NN)
        da_re, da_im = _scan_reverse_in_place(lr, li, xr, xi, ar_ref[...], ai_ref[...], T)
        dar_ref[...] = da_re
        dai_ref[...] = da_im
        du_ref[...] = _dot(lr[...], br_ref[...], _NT) + _dot(li[...], bi_ref[...], _NT) + dud_ref[...]
        ub = u_ref[...].astype(_BF16)
        dcr_ref[...] = _dot(dyb, xr[...], _TN)
        dci_ref[...] = _dot(dyb, xi[...], _TN)
        dbr_ref[...] = _dot(ub, lr[...], _TN)
        dbi_ref[...] = _dot(ub, li[...], _TN)

    width, matrix, row = _ssm_super_specs(L)
    state = pl.BlockSpec((L, SB_STATE), lambda k: (0, k))
    return _call(body, (SUPER,), [width, width, width, state, state, matrix, matrix, matrix, matrix, row, row],
                 [width] + [matrix] * 4 + [row] * 2,
                 [_sds((L, SSM_WIDTH), _F32)] + [_sds((SB_WIDTH, N_STATE), _F32)] * 4 + [_sds((1, N_STATE), _F32)] * 2,
                 "ssm_core_bwd", scratch=[pltpu.VMEM((L, SB_STATE), _F32)] * 2,
                 tokens=tokens)(u, dy, dud, x_re, x_im, bt_re, bt_im, ct_re, ct_im, a_re, a_im)


def _ssm_out(cx, u, d_row, w_glu, b_glu, g_ssm):
    L = u.shape[0]
    tm = _tile(L)

    def body(cx_ref, u_ref, d_ref, w_ref, b_ref, g_ref, y_ref, z_ref, n_ref, stage):
        y = cx_ref[...] + d_ref[...] * u_ref[...]
        y_ref[...] = y
        z = _dot(_gelu(y), w_ref[...], _NT) + b_ref[...]
        z_ref[...] = z
        out = z[:, :SSM_WIDTH] * jax.nn.sigmoid(z[:, SSM_WIDTH:])
        n, _ = _rms_fwd(out, g_ref[...])
        for k in range(SSM_WIDTH // _LANES):
            stage[k] = n[:, _LANES * k:_LANES * (k + 1)]
            for c in range(SCAN_CHUNKS):
                rows = stage[k, pl.ds(c, tm // SCAN_CHUNKS, stride=SCAN_CHUNKS), :]
                lane = SSM_WIDTH * c + _LANES * k
                n_ref[:, lane:lane + _LANES] = rows.astype(_BF16)

    return _call(body, (L // tm,),
                 [_rows(tm, SSM_WIDTH), _rows(tm, SSM_WIDTH), _whole((1, SSM_WIDTH)),
                  _whole((2 * SSM_WIDTH, SSM_WIDTH)), _whole((1, 2 * SSM_WIDTH)), _whole((1, SSM_WIDTH))],
                 [_rows(tm, SSM_WIDTH), _rows(tm, 2 * SSM_WIDTH), _rows(tm // SCAN_CHUNKS, SCAN_CHUNKS * SSM_WIDTH)],
                 [_sds((L, SSM_WIDTH), _F32), _sds((L, 2 * SSM_WIDTH), _F32), _sds(_chunk_shape(L, SSM_WIDTH), _BF16)],
                 "ssm_out", scratch=[pltpu.VMEM((SSM_WIDTH // _LANES, tm, _LANES), _F32)])(
        cx, u, d_row, w_glu, b_glu, g_ssm)


def _ssm_out_bwd(dn, y, z, u, d_row, w_glu, g_ssm, tokens=()):
    L = u.shape[0]
    tm = _tile(L)

    def body(dn_ref, y_ref, z_ref, u_ref, d_ref, w_ref, g_ref,
             gy_ref, dz_ref, dy_ref, dud_ref, dg_ref, db_ref, dd_ref, stage):
        first = pl.program_id(0) == 0
        for k in range(SSM_WIDTH // _LANES):
            for c in range(SCAN_CHUNKS):
                lane = SSM_WIDTH * c + _LANES * k
                stage[k, pl.ds(c, tm // SCAN_CHUNKS, stride=SCAN_CHUNKS), :] = dn_ref[:, lane:lane + _LANES]
        dn = jnp.concatenate([stage[k] for k in range(SSM_WIDTH // _LANES)], axis=1)
        z = z_ref[...]
        z1, z2 = z[:, :SSM_WIDTH], z[:, SSM_WIDTH:]
        sig = jax.nn.sigmoid(z2)
        out = z1 * sig
        g = g_ref[...]
        _, r = _rms_fwd(out, g)
        dout, dg = _rms_bwd(dn, out, g, r)
        _accumulate(dg_ref, dg, first)
        dz = jnp.concatenate([dout * sig, dout * z1 * sig * (1.0 - sig)], axis=1)
        _accumulate(db_ref, jnp.sum(dz, axis=0, keepdims=True), first)
        dzb = dz.astype(_BF16)
        dz_ref[...] = dzb
        y = y_ref[...]
        gy_ref[...] = _gelu(y).astype(_BF16)
        dy = _dot(dzb, w_ref[...], _NN) * _gelu_grad(y)
        u = u_ref[...]
        _accumulate(dd_ref, jnp.sum(dy * u, axis=0, keepdims=True), first)
        dud_ref[...] = d_ref[...] * dy
        dy_ref[...] = dy.astype(_BF16)

    row = _whole((1, SSM_WIDTH))
    return _call(body, (L // tm,),
                 [_rows(tm // SCAN_CHUNKS, SCAN_CHUNKS * SSM_WIDTH), _rows(tm, SSM_WIDTH), _rows(tm, 2 * SSM_WIDTH),
                  _rows(tm, SSM_WIDTH), row, _whole((2 * SSM_WIDTH, SSM_WIDTH)), row],
                 [_rows(tm, SSM_WIDTH), _rows(tm, 2 * SSM_WIDTH), _rows(tm, SSM_WIDTH), _rows(tm, SSM_WIDTH),
                  row, _whole((1, 2 * SSM_WIDTH)), row],
                 [_sds((L, SSM_WIDTH), _BF16), _sds((L, 2 * SSM_WIDTH), _BF16), _sds((L, SSM_WIDTH), _BF16),
                  _sds((L, SSM_WIDTH), _F32),
                  _sds((1, SSM_WIDTH), _F32), _sds((1, 2 * SSM_WIDTH), _F32), _sds((1, SSM_WIDTH), _F32)],
                 "ssm_out_bwd", scratch=[pltpu.VMEM((SSM_WIDTH // _LANES, tm, _LANES), _F32)], tokens=tokens)(
        dn, y, z, u, d_row, w_glu, g_ssm)


_SSM_PACK = {"ssm_b_re": (0, SSM_WIDTH, 0, SSM_STATE), "ssm_c_re": (0, SSM_WIDTH, 64, SSM_STATE),
             "ssm_b_im": (512, SSM_WIDTH, 0, SSM_STATE), "ssm_c_im": (512, SSM_WIDTH, 64, SSM_STATE),
             "ssm_lambda_re": (1024, SSM_GROUPS, 0, SSM_STATE), "ssm_lambda_im": (1024, SSM_GROUPS, 64, SSM_STATE),
             "ssm_d": (1056, SSM_GROUPS, 0, SSM_GROUP), "ssm_log_dt": (1088, 1, 0, SSM_GROUPS)}
_PACK_TILE = 16
_SSM_PACK_ROWS = 1088 + _PACK_TILE


def _ssm_param_bwd(da_re, da_im, dbt_re, dbt_im, dct_re, dct_im, lam_re, lam_im, log_dt, b_re, b_im, g_d):
    def body(dar, dai, dbr, dbi, dcr, dci, lr_ref, li_ref, ld_ref, bre_ref, bim_ref, gd_ref, pack_ref):
        lane_in = _iota((SSM_STATE, _LANES), 0)
        lane_out = _iota((SSM_STATE, _LANES), 1)
        low = (lane_out == lane_in).astype(_F32)
        high = (lane_out == lane_in + SSM_STATE).astype(_F32)

        def side_by_side(a, b):
            return _dot_exact(a, low, _NN) + _dot_exact(b, high, _NN)

        tail = _SSM_PACK["ssm_d"][0]
        pack_ref[tail:, :] = jnp.zeros((_SSM_PACK_ROWS - tail, _LANES), _BF16)
        pack_ref[tail:tail + SSM_GROUPS, 0:SSM_GROUP] = gd_ref[...].astype(_BF16)
        own_c = (_iota((SB_WIDTH, SB_STATE), 0) >> 4) == (_iota((SB_WIDTH, SB_STATE), 1) >> 6)

        def unfold(ref):
            blocks = []
            for k in range(SUPER):
                t = jnp.where(own_c, ref[:, _sb_state(k)], 0.0)
                t = sum(t[:, 128 * i:128 * (i + 1)] for i in range(SB_STATE // 128))
                blocks.append((t + pltpu.roll(t, SSM_STATE, 1))[:, :SSM_STATE])
            return jnp.concatenate(blocks, axis=0)

        dbb_re, dbb_im = unfold(dbr), unfold(dbi)
        b_re, b_im = bre_ref[...], bim_ref[...]
        dt_col = _dt_column(ld_ref[...])
        (_, _, fr, fi), vjp = jax.vjp(_s5_discretize, lr_ref[...], li_ref[...], dt_col)
        spread = _rows_of_group()
        fr_t = _dot_exact(spread, fr, _NN)
        fi_t = _dot_exact(spread, fi, _NN)
        pack_ref[0:SSM_WIDTH, :] = side_by_side(fr_t * dbb_re + fi_t * dbb_im, unfold(dcr)).astype(_BF16)
        pack_ref[SSM_WIDTH:2 * SSM_WIDTH, :] = side_by_side(fr_t * dbb_im - fi_t * dbb_re, -unfold(dci)).astype(_BF16)
        d_fr = _dot_exact(spread, dbb_re * b_re + dbb_im * b_im, _TN)
        d_fi = _dot_exact(spread, dbb_im * b_re - dbb_re * b_im, _TN)
        e64, own = _group_masks()

        def from_row(ref):
            return _dot_exact(jnp.where(own, ref[...], 0.0), e64, _NT)

        d_lr, d_li, d_dt = vjp((from_row(dar), from_row(dai), d_fr, d_fi))
        lam_rows = _SSM_PACK["ssm_lambda_re"][0]
        pack_ref[lam_rows:lam_rows + SSM_GROUPS, :] = side_by_side(d_lr, d_li).astype(_BF16)
        eye = (_iota((SSM_GROUPS, SSM_GROUPS), 0) == _iota((SSM_GROUPS, SSM_GROUPS), 1)).astype(_F32)
        dt_row = _SSM_PACK["ssm_log_dt"][0]
        pack_ref[dt_row:dt_row + _PACK_TILE, 0:SSM_GROUPS] = _dot_exact(
            jnp.broadcast_to(d_dt, (SSM_GROUPS, 128)), eye, _TN)[0:_PACK_TILE].astype(_BF16)

    ins = [da_re, da_im, dbt_re, dbt_im, dct_re, dct_im, lam_re, lam_im, log_dt, b_re, b_im, g_d]
    out = (_SSM_PACK_ROWS, _LANES)
    return _call(body, (1,), [_whole(a.shape) for a in ins], _whole(out), _sds(out, _BF16), "ssm_param_bwd")(*ins)


def _head_spread(j):
    r = _iota((KV_WIDTH, 256), 0)
    c = _iota((KV_WIDTH, 256), 1)
    return (r == HEAD_DIM * j + (c & (HEAD_DIM - 1))).astype(_BF16)


STACK = Q_PER_KV * BLOCK


def _stack_heads(t):
    lane_head = _iota((1, 256), 1) >> 6
    return jnp.concatenate([jnp.where(lane_head == g, t, jnp.zeros_like(t)) for g in range(Q_PER_KV)], axis=0)


def _unstack_heads(t):
    lane_head = _iota((1, 256), 1) >> 6
    return sum(jnp.where(lane_head == g, t[BLOCK * g:BLOCK * (g + 1)], 0.0) for g in range(Q_PER_KV))


def _stacked_sinks(sink_ref, j):
    block = _iota((STACK, 1), 0) >> 7
    col = jnp.full((STACK, 1), sink_ref[Q_PER_KV * j], _F32)
    for g in range(1, Q_PER_KV):
        col = jnp.where(block == g, sink_ref[Q_PER_KV * j + g], col)
    return col


def _fold_heads(t, j):
    t = t[:, :KV_WIDTH] + t[:, KV_WIDTH:]
    t = t + pltpu.roll(t, HEAD_DIM, 1)
    return jnp.where((_iota((1, KV_WIDTH), 1) >> 6) == j, t, 0.0)


def _attn_scores(q_stacked, kt, blk, sink):
    s = _dot(q_stacked, kt, _NT) * (HEAD_DIM ** -0.5)
    qi = _iota((STACK, 2 * BLOCK), 0) & (BLOCK - 1)
    kj = _iota((STACK, 2 * BLOCK), 1)
    rel = qi + BLOCK - kj
    valid = (rel >= 0) & (rel < BLOCK) & (blk * BLOCK - BLOCK + kj >= 0)
    s = jnp.where(valid, s, MASK_VALUE)
    m = jnp.maximum(jnp.max(s, axis=-1, keepdims=True), sink)
    p = jnp.exp(s - m)
    e_sink = jnp.exp(sink - m)
    den = jnp.sum(p, axis=-1, keepdims=True) + e_sink
    return p / den, e_sink / den


def _sink_slot():
    return _iota((STACK, 2 * BLOCK), 1) == 0


def _prob_block():
    return pl.BlockSpec((None, N_KV_HEADS, STACK, 2 * BLOCK), lambda i: (i, 0, 0, 0))


def _attn_specs():
    prev = lambda i: (jnp.maximum(i - 1, 0), 0)
    cur = lambda i: (i, 0)
    kv = [pl.BlockSpec((BLOCK, KV_WIDTH), prev), pl.BlockSpec((BLOCK, KV_WIDTH), cur)]
    return [pl.BlockSpec((BLOCK, ATTN_WIDTH), cur)] + kv + kv


def _attn_fwd(q, k, v, sinks, g_attn):
    L = q.shape[0]

    def body(q_ref, kp_ref, kc_ref, vp_ref, vc_ref, sink_ref, g_ref, o_ref, n_ref, p_ref):
        blk = pl.program_id(0)
        kwin = jnp.concatenate([kp_ref[...], kc_ref[...]], axis=0)
        vwin = jnp.concatenate([vp_ref[...], vc_ref[...]], axis=0)
        halves = []
        for j in range(N_KV_HEADS):
            spread = _head_spread(j)
            kt = _dot(kwin, spread, _NN).astype(_BF16)
            vt = _dot(vwin, spread, _NN).astype(_BF16)
            qs = _stack_heads(q_ref[:, 256 * j:256 * (j + 1)])
            p, p_sink = _attn_scores(qs, kt, blk, _stacked_sinks(sink_ref, j))
            p_ref[j] = jnp.where(_sink_slot(), p_sink, p)
            halves.append(_unstack_heads(_dot(p, vt, _NN)))
        o = jnp.concatenate(halves, axis=1)
        o_ref[...] = o
        n, _ = _rms_fwd(o, g_ref[...])
        n_ref[...] = n.astype(_BF16)

    cur = lambda i: (i, 0)
    return _call(body, (L // BLOCK,),
                 _attn_specs() + [pl.BlockSpec(memory_space=pltpu.SMEM), _whole((1, ATTN_WIDTH))],
                 [pl.BlockSpec((BLOCK, ATTN_WIDTH), cur)] * 2 + [_prob_block()],
                 [_sds((L, ATTN_WIDTH), _F32), _sds((L, ATTN_WIDTH), _BF16),
                  _sds((L // BLOCK, N_KV_HEADS, STACK, 2 * BLOCK), _F32)],
                 "attn_fwd")(q, k, k, v, v, sinks, g_attn)


def _attn_bwd(q, k, v, o, dn, probs, g_attn):
    L = q.shape[0]

    def body(q_ref, kp_ref, kc_ref, vp_ref, vc_ref, o_ref, dn_ref, p_ref, g_ref,
             dq_ref, dk_ref, dv_ref, dsink_ref, dg_ref):
        blk = pl.program_id(0)
        first = blk == 0

        @pl.when(first)
        def _():
            dk_ref[...] = jnp.zeros_like(dk_ref)
            dv_ref[...] = jnp.zeros_like(dv_ref)
            dsink_ref[...] = jnp.zeros_like(dsink_ref)

        o = o_ref[...]
        g = g_ref[...]
        _, r = _rms_fwd(o, g)
        do, dg = _rms_bwd(dn_ref[...], o, g, r)
        _accumulate(dg_ref, dg, first)
        kwin = jnp.concatenate([kp_ref[...], kc_ref[...]], axis=0)
        vwin = jnp.concatenate([vp_ref[...], vc_ref[...]], axis=0)
        lane = _iota((1, 128), 1)
        dsink = jnp.zeros((1, 128), _F32)
        dkwin = jnp.zeros((2 * BLOCK, KV_WIDTH), _F32)
        dvwin = jnp.zeros((2 * BLOCK, KV_WIDTH), _F32)
        dq_halves = []
        for j in range(N_KV_HEADS):
            spread = _head_spread(j)
            kt = _dot(kwin, spread, _NN).astype(_BF16)
            vt = _dot(vwin, spread, _NN).astype(_BF16)
            qs = _stack_heads(q_ref[:, 256 * j:256 * (j + 1)])
            dos = _stack_heads(do[:, 256 * j:256 * (j + 1)]).astype(_BF16)
            saved = p_ref[j]
            p_sink = saved[:, 0:1]
            p = jnp.where(_sink_slot(), 0.0, saved)
            dp = _dot(dos, vt, _NT)
            delta = jnp.sum(p * dp, axis=-1, keepdims=True)
            ds = (p * (dp - delta) * (HEAD_DIM ** -0.5)).astype(_BF16)
            sink_term = p_sink * delta
            for g in range(Q_PER_KV):
                head_sum = jnp.sum(sink_term[BLOCK * g:BLOCK * (g + 1)], axis=0, keepdims=True)
                dsink = dsink - jnp.where(lane == Q_PER_KV * j + g, head_sum, 0.0)
            dvwin = dvwin + _fold_heads(_dot(p, dos, _TN), j)
            dkwin = dkwin + _fold_heads(_dot(ds, qs, _TN), j)
            dq_halves.append(_unstack_heads(_dot(ds, kt, _NN)))
        dq_ref[...] = jnp.concatenate(dq_halves, axis=1)
        dsink_ref[...] += dsink
        prev = pl.ds(pl.multiple_of(jnp.maximum(blk - 1, 0) * BLOCK, BLOCK), BLOCK)
        cur = pl.ds(pl.multiple_of(blk * BLOCK, BLOCK), BLOCK)
        dk_ref[prev, :] += dkwin[:BLOCK]
        dk_ref[cur, :] += dkwin[BLOCK:]
        dv_ref[prev, :] += dvwin[:BLOCK]
        dv_ref[cur, :] += dvwin[BLOCK:]

    cur = lambda i: (i, 0)
    blk_q = pl.BlockSpec((BLOCK, ATTN_WIDTH), cur)
    return _call(body, (L // BLOCK,),
                 _attn_specs() + [blk_q, blk_q, _prob_block(), _whole((1, ATTN_WIDTH))],
                 [blk_q, _whole((L, KV_WIDTH)), _whole((L, KV_WIDTH)), _whole((1, 128)), _whole((1, ATTN_WIDTH))],
                 [_sds((L, ATTN_WIDTH), _F32), _sds((L, KV_WIDTH), _F32), _sds((L, KV_WIDTH), _F32),
                  _sds((1, 128), _F32), _sds((1, ATTN_WIDTH), _F32)],
                 "attn_bwd")(q, k, k, v, v, o, dn, probs, g_attn)


def _out_proj(n_ssm, n_attn, x, w_out, g_post_mix, g_pre_ffn):
    L = x.shape[0]
    tm = _chunk_tile(L)

    def body(ns_ref, na_ref, x_ref, w_ref, g1_ref, g2_ref, merged_ref, mo_ref, h1_ref, hn2_ref):
        merged = jnp.concatenate([ns_ref[...], na_ref[...]], axis=1)
        merged_ref[...] = merged
        mo = _dot(merged, w_ref[...], _NN)
        mo_ref[...] = mo
        n, _ = _rms_fwd(mo, g1_ref[...])
        h1 = x_ref[...] + n
        h1_ref[...] = h1
        hn2, _ = _rms_fwd(h1, g2_ref[...])
        hn2_ref[...] = hn2.astype(_BF16)

    row = _whole((1, D_MODEL))
    return _call(body, (L // tm,),
                 [_chunk_block(L, SSM_WIDTH), _rows(tm, ATTN_WIDTH), _rows(tm, D_MODEL), _whole((D_MODEL, D_MODEL)),
                  row, row],
                 [_rows(tm, D_MODEL)] * 4,
                 [_sds((L, D_MODEL), _BF16), _sds((L, D_MODEL), _F32), _sds((L, D_MODEL), _F32), _sds((L, D_MODEL), _BF16)],
                 "out_proj")(n_ssm, n_attn, x, w_out, g_post_mix, g_pre_ffn)


def _ffn(hn2, h1, target, w_gate_up, w_down, g_pre_ffn, g_post_ffn):
    L = h1.shape[0]
    tm = _tile(L)
    half = FFN_CHUNK

    def body(hn2_ref, h1_ref, tgt_ref, wgu_hbm, wd_hbm, g2_ref, g3_ref,
             act_ref, dgu_ref, dff_ref, dh1_ref, loss_ref, dg3_ref, dg2_ref,
             wgu, wd, gu, sem):
        first = pl.program_id(0) == 0

        @pl.when(first)
        def _():
            c1 = pltpu.make_async_copy(wgu_hbm, wgu, sem.at[0])
            c2 = pltpu.make_async_copy(wd_hbm, wd, sem.at[1])
            c1.start()
            c2.start()
            c1.wait()
            c2.wait()

        hn2 = hn2_ref[...]
        ff = jnp.zeros((tm, D_MODEL), _F32)
        for c in range(D_FF // half):
            gate = _dot(hn2, wgu[half * c:half * (c + 1), :], _NT)
            up = _dot(hn2, wgu[D_FF + half * c:D_FF + half * (c + 1), :], _NT)
            gu[:, half * c:half * (c + 1)] = gate
            gu[:, D_FF + half * c:D_FF + half * (c + 1)] = up
            act = gate * jax.nn.sigmoid(gate) * up
            act_ref[half * c:half * (c + 1), :] = act.T.astype(_BF16)
            ff = ff + _dot(act, wd[half * c:half * (c + 1), :], _NN)
        g3 = g3_ref[...]
        n, r = _rms_fwd(ff, g3)
        h1 = h1_ref[...]
        err = h1 + n - tgt_ref[...]
        loss = 0.5 * jnp.sum(jnp.mean(err * err, axis=-1, keepdims=True), axis=0, keepdims=True)
        _accumulate(loss_ref, jnp.broadcast_to(loss, (1, 128)), first)
        dh2 = err * (1.0 / D_MODEL)
        dff, dg3 = _rms_bwd(dh2, ff, g3, r)
        _accumulate(dg3_ref, dg3, first)
        dffb = dff.astype(_BF16)
        dff_ref[...] = dffb
        dhn2 = jnp.zeros((tm, D_MODEL), _F32)
        for c in range(D_FF // half):
            dact = _dot(dffb, wd[half * c:half * (c + 1), :], _NT)
            gate = gu[:, half * c:half * (c + 1)]
            up = gu[:, D_FF + half * c:D_FF + half * (c + 1)]
            sig = jax.nn.sigmoid(gate)
            silu = gate * sig
            dgate = dact * up * (sig + silu * (1.0 - sig))
            dup = dact * silu
            dgu_ref[half * c:half * (c + 1), :] = dgate.T.astype(_BF16)
            dgu_ref[D_FF + half * c:D_FF + half * (c + 1), :] = dup.T.astype(_BF16)
            dhn2 = dhn2 + _dot(dgate, wgu[half * c:half * (c + 1), :], _NN)
            dhn2 = dhn2 + _dot(dup, wgu[D_FF + half * c:D_FF + half * (c + 1), :], _NN)
        g2 = g2_ref[...]
        _, r2 = _rms_fwd(h1, g2)
        dh1, dg2 = _rms_bwd(dhn2, h1, g2, r2)
        _accumulate(dg2_ref, dg2, first)
        dh1_ref[...] = dh2 + dh1

    row = _whole((1, D_MODEL))
    anyspace = pl.BlockSpec(memory_space=pl.ANY)
    return _call(body, (L // tm,),
                 [_rows(tm, D_MODEL), _rows(tm, D_MODEL), _rows(tm, D_MODEL), anyspace, anyspace, row, row],
                 [pl.BlockSpec((D_FF, tm), lambda i: (0, i)), pl.BlockSpec((2 * D_FF, tm), lambda i: (0, i)),
                  _rows(tm, D_MODEL), _rows(tm, D_MODEL), _whole((1, 128)), row, row],
                 [_sds((D_FF, L), _BF16), _sds((2 * D_FF, L), _BF16), _sds((L, D_MODEL), _BF16),
                  _sds((L, D_MODEL), _F32), _sds((1, 128), _F32), _sds((1, D_MODEL), _F32), _sds((1, D_MODEL), _F32)],
                 "ffn",
                 scratch=[pltpu.VMEM((2 * D_FF, D_MODEL), _BF16), pltpu.VMEM((D_FF, D_MODEL), _BF16),
                          pltpu.VMEM((tm, 2 * D_FF), _F32), pltpu.SemaphoreType.DMA((2,))],
                 )(hn2, h1, target, w_gate_up, w_down, g_pre_ffn, g_post_ffn)


def _out_proj_bwd(dh1, mo, w_out, g_post_mix, tokens=()):
    L = dh1.shape[0]
    tm = _chunk_tile(L)

    def body(dh1_ref, mo_ref, w_ref, g_ref, dmo_ref, dns_ref, dna_ref, dg_ref):
        first = pl.program_id(0) == 0
        mo = mo_ref[...]
        g = g_ref[...]
        _, r = _rms_fwd(mo, g)
        dmo, dg = _rms_bwd(dh1_ref[...], mo, g, r)
        _accumulate(dg_ref, dg, first)
        dmob = dmo.astype(_BF16)
        dmo_ref[...] = dmob
        dmerged = _dot(dmob, w_ref[...], _NT)
        dns_ref[...] = dmerged[:, :SSM_WIDTH]
        dna_ref[...] = dmerged[:, SSM_WIDTH:]

    row = _whole((1, D_MODEL))
    return _call(body, (L // tm,),
                 [_rows(tm, D_MODEL), _rows(tm, D_MODEL), _whole((D_MODEL, D_MODEL)), row],
                 [_rows(tm, D_MODEL), _chunk_block(L, SSM_WIDTH), _rows(tm, ATTN_WIDTH), row],
                 [_sds((L, D_MODEL), _BF16), _sds(_chunk_shape(L, SSM_WIDTH), _F32), _sds((L, ATTN_WIDTH), _F32),
                  _sds((1, D_MODEL), _F32)],
                 "out_proj_bwd", tokens=tokens)(dh1, mo, w_out, g_post_mix)


def _in_proj_bwd(du, dq, dk, dv, cos_t, sin_t, x, dh1, g_pre_mix, w_in, tokens=()):
    L = x.shape[0]
    tm = _chunk_tile(L)

    def body(du_ref, dq_ref, dk_ref, dv_ref, cos_ref, sin_ref, x_ref, dh1_ref, g_ref, w_ref,
             dproj_ref, dx_ref, dg_ref):
        first = pl.program_id(0) == 0
        cos_v, sin_v = cos_ref[...], sin_ref[...]
        dproj = jnp.concatenate([du_ref[...], _rope_transpose(dq_ref[...], cos_v, sin_v),
                                 _rope_transpose(dk_ref[...], cos_v, sin_v), dv_ref[...]], axis=1).astype(_BF16)
        dproj_ref[...] = dproj
        dhn = _dot(dproj, w_ref[...], _NN)
        x = x_ref[...]
        g = g_ref[...]
        _, r = _rms_fwd(x, g)
        dx, dg = _rms_bwd(dhn, x, g, r)
        _accumulate(dg_ref, dg, first)
        dx_ref[...] = dh1_ref[...] + dx

    row = _whole((1, D_MODEL))
    return _call(body, (L // tm,),
                 [_chunk_block(L, SSM_WIDTH), _rows(tm, ATTN_WIDTH), _rows(tm, KV_WIDTH), _rows(tm, KV_WIDTH),
                  _rows(tm, KV_WIDTH), _rows(tm, KV_WIDTH), _rows(tm, D_MODEL), _rows(tm, D_MODEL), row,
                  _whole((IN_WIDTH, D_MODEL))],
                 [_rows(tm, IN_WIDTH), _rows(tm, D_MODEL), row],
                 [_sds((L, IN_WIDTH), _BF16), _sds((L, D_MODEL), _F32), _sds((1, D_MODEL), _F32)],
                 "in_proj_bwd", tokens=tokens)(du, dq, dk, dv, cos_t, sin_t, x, dh1, g_pre_mix, w_in)


def _matmul_nn(a, b, out_dtype, name):
    M, K = a.shape
    N = b.shape[1]
    tm = next(t for t in (1408, 704, 512, 256, 128) if M % t == 0)
    tn = N if N <= D_MODEL else next(t for t in (512, 256, 128) if N % t == 0)

    def body(a_ref, b_ref, o_ref):
        o_ref[...] = _dot(a_ref[...], b_ref[...], _NN).astype(out_dtype)

    params = pltpu.CompilerParams(dimension_semantics=("arbitrary", "arbitrary"), vmem_limit_bytes=VMEM_LIMIT)
    return pl.pallas_call(body, grid=(M // tm, N // tn),
                          in_specs=[pl.BlockSpec((tm, K), lambda i, j: (i, 0)),
                                    pl.BlockSpec((K, tn), lambda i, j: (0, j))],
                          out_specs=pl.BlockSpec((tm, tn), lambda i, j: (i, j)),
                          out_shape=_sds((M, N), out_dtype), compiler_params=params, name=name)(a, b)


def _matmul_tn(a, b, out_dtype, name, scale=1.0):
    K, M = a.shape
    N = b.shape[1]
    tm = next(t for t in (512, 256, 128) if M % t == 0)
    tn = N if N <= D_MODEL else next(t for t in (512, 256, 128) if N % t == 0)

    def body(a_ref, b_ref, o_ref):
        acc = _dot(a_ref[...], b_ref[...], _TN)
        o_ref[...] = (acc if scale == 1.0 else acc * scale).astype(out_dtype)

    params = pltpu.CompilerParams(dimension_semantics=("arbitrary", "arbitrary"), vmem_limit_bytes=VMEM_LIMIT)
    return pl.pallas_call(body, grid=(M // tm, N // tn),
                          in_specs=[pl.BlockSpec((K, tm), lambda i, j: (0, i)),
                                    pl.BlockSpec((K, tn), lambda i, j: (0, j))],
                          out_specs=pl.BlockSpec((tm, tn), lambda i, j: (i, j)),
                          out_shape=_sds((M, N), out_dtype), compiler_params=params, name=name)(a, b)


def _local_step(x, pos, target, p, fetch, publish, progress):
    L = x.shape[0]
    T = L // SCAN_CHUNKS
    cos_t, sin_t = _rope_tables(pos.reshape(L, 1))
    w_in, = fetch(("w_in",), None)
    hn, u, q, k, v = _in_proj(x, p["g_pre_mix"], w_in, cos_t, sin_t)

    ssm = {n: _to_2d(n, p[n]) for n in ("ssm_lambda_re", "ssm_lambda_im", "ssm_log_dt", "ssm_b_re", "ssm_b_im",
                                        "ssm_c_re", "ssm_c_im")}
    d_row = p["ssm_d"].reshape(1, SSM_WIDTH)
    a_re, a_im, bt_re, bt_im, ct_re, ct_im = _ssm_prep(
        ssm["ssm_lambda_re"], ssm["ssm_lambda_im"], ssm["ssm_log_dt"], ssm["ssm_b_re"], ssm["ssm_b_im"],
        ssm["ssm_c_re"], ssm["ssm_c_im"])

    u_c = u.reshape(L, SSM_WIDTH)
    cx, x_re, x_im = _ssm_core_fwd(u_c, bt_re, bt_im, ct_re, ct_im, a_re, a_im)
    w_glu, = fetch(("w_glu",), cx)
    y, z, n_ssm = _ssm_out(cx, u_c, d_row, w_glu, p["b_glu"], p["g_ssm_out"])

    sinks = p["attn_sinks"].reshape(N_Q_HEADS)
    o, n_attn, probs = _attn_fwd(q, k, v, sinks, p["g_attn_out"])
    w_out, = fetch(("w_out",), n_attn)
    merged, mo, h1, hn2 = _out_proj(n_ssm, n_attn, x, w_out, p["g_post_mix"], p["g_pre_ffn"])
    w_gate_up, w_down = fetch(("w_gate_up", "w_down"), hn2)
    act_t, dgu_t, dff, dh1, loss, dg_post_ffn, dg_pre_ffn = _ffn(
        hn2, h1, target, w_gate_up, w_down, p["g_pre_ffn"], p["g_post_ffn"])
    grads = {"g_post_ffn": dg_post_ffn, "g_pre_ffn": dg_pre_ffn}
    tokens = publish({"w_down": _matmul_nn(act_t, dff, _BF16, "grad_w_down"),
                      "w_gate_up": _matmul_nn(dgu_t, hn2, _BF16, "grad_w_gate_up")})

    dmo, dn_ssm, dn_attn, grads["g_post_mix"] = _out_proj_bwd(dh1, mo, w_out, p["g_post_mix"], tokens)
    grad_w_out = _matmul_tn(merged, dmo, _BF16, "grad_w_out")

    dq, dk, dv, dsink, grads["g_attn_out"] = _attn_bwd(q, k, v, o, dn_attn, probs, p["g_attn_out"])
    grads["attn_sinks"] = dsink

    gy, dz, dy, dud, grads["g_ssm_out"], grads["b_glu"], dd = _ssm_out_bwd(
        dn_ssm, y, z, u_c, d_row, w_glu, p["g_ssm_out"], progress(dmo))
    tokens = publish({"w_out": grad_w_out, "w_glu": _matmul_tn(dz, gy, _BF16, "grad_w_glu")})
    du_c, dct_re, dct_im, dbt_re, dbt_im, da_re, da_im = _ssm_core_bwd(
        u_c, dy, dud, x_re, x_im, bt_re, bt_im, ct_re, ct_im, a_re, a_im, tokens)
    ssm_pack = _ssm_param_bwd(
        da_re, da_im, dbt_re, dbt_im, dct_re, dct_im,
        ssm["ssm_lambda_re"], ssm["ssm_lambda_im"], ssm["ssm_log_dt"], ssm["ssm_b_re"], ssm["ssm_b_im"],
        dd.reshape(SSM_GROUPS, SSM_GROUP))
    grads.update(ssm_pack=ssm_pack, loss=loss)
    publish(grads)

    du = du_c.reshape(_chunk_shape(L, SSM_WIDTH))
    dproj, grad_x, g_pre_mix = _in_proj_bwd(du, dq, dk, dv, cos_t, sin_t, x, dh1, p["g_pre_mix"], w_in, [ssm_pack])
    publish({"g_pre_mix": g_pre_mix, "w_in": _matmul_tn(dproj, hn, _BF16, "grad_w_in")})
    return grad_x


_MESH = pl.DeviceIdType.MESH
_PEERS = N_DEV - 1


def _mesh_pos():
    return lax.axis_index("x"), lax.axis_index("y"), lax.axis_index("c")


def _dev_index(px, py, pc):
    return 4 * px + 2 * py + pc


def _peer(x, y, c, r):
    return (x ^ ((r >> 2) & 1), y ^ ((r >> 1) & 1), c ^ (r & 1))


def _sequencer_exchange(sources, blocked, name, collective_id):
    n = len(sources)
    flags = blocked

    def body(*refs):
        srcs, zones = refs[:n], refs[n:2 * n]
        send_sems, recv_sems, local_sems = refs[2 * n:]
        x, y, c = _mesh_pos()
        me = _dev_index(x, y, c)
        barrier = pltpu.get_barrier_semaphore()
        for r in range(1, N_DEV):
            pl.semaphore_signal(barrier, inc=1, device_id=_peer(x, y, c, r), device_id_type=_MESH)
        pl.semaphore_wait(barrier, _PEERS)
        local, sends, recvs = [], [], []
        for w in range(n):
            cp = pltpu.make_async_copy(srcs[w].at[me] if flags[w] else srcs[w], zones[w].at[me], local_sems.at[w])
            cp.start()
            local.append(cp)
            for r in range(1, N_DEV):
                peer = _peer(x, y, c, r)
                idx = _dev_index(*peer)
                k = _PEERS * w + r - 1
                src = srcs[w].at[idx] if flags[w] else srcs[w]
                send = pltpu.make_async_remote_copy(
                    src_ref=src, dst_ref=zones[w].at[me], send_sem=send_sems.at[k], recv_sem=recv_sems.at[k],
                    device_id=peer, device_id_type=_MESH)
                send.start()
                sends.append(send)
                recvs.append(pltpu.make_async_remote_copy(
                    src_ref=src, dst_ref=zones[w].at[idx], send_sem=send_sems.at[k], recv_sem=recv_sems.at[k],
                    device_id=peer, device_id_type=_MESH))
        for cp in recvs:
            cp.wait_recv()
        for cp in sends:
            cp.wait_send()
        for cp in local:
            cp.wait()

    return pl.kernel(
        body, name=name,
        out_type=[_sds((N_DEV,) + (s.shape[1:] if f else s.shape), s.dtype) for s, f in zip(sources, flags)],
        mesh=plsc.ScalarSubcoreMesh(axis_name="sequencer", num_cores=1),
        scratch_types=[pltpu.SemaphoreType.DMA((_PEERS * n,)), pltpu.SemaphoreType.DMA((_PEERS * n,)),
                       pltpu.SemaphoreType.DMA((n,))],
        compiler_params=pltpu.CompilerParams(collective_id=collective_id),
    )(*sources)


def _sequencer_gather(shards, name, collective_id):
    n = len(shards)
    fan = 4

    def body(*refs):
        srcs, zones = refs[:n], refs[n:2 * n]
        send_sems, recv_sems, local_sems = refs[2 * n:]
        x, y, c = _mesh_pos()
        me, sibling = (x, y, c), (x, y, 1 - c)
        chips = [(1 - x, y), (x, 1 - y), (1 - x, 1 - y)]
        barrier = pltpu.get_barrier_semaphore()
        for peer in [sibling] + [(*chip, c) for chip in chips]:
            pl.semaphore_signal(barrier, inc=1, device_id=peer, device_id_type=_MESH)
        pl.semaphore_wait(barrier, fan)

        def copy(w, k, block, to, src=None):
            slot = zones[w].at[_dev_index(*block)]
            return pltpu.make_async_remote_copy(
                src_ref=slot if src is None else src, dst_ref=slot,
                send_sem=send_sems.at[_PEERS * w + k], recv_sem=recv_sems.at[_PEERS * w + k],
                device_id=to, device_id_type=_MESH)

        mine, first, passed = [], [], []
        for w in range(n):
            cp = pltpu.make_async_copy(srcs[w], zones[w].at[_dev_index(*me)], local_sems.at[w])
            cp.start()
            mine.append(cp)
            sends = [copy(w, 0, me, sibling, src=srcs[w])]
            sends += [copy(w, 1 + j, me, (*chip, c), src=srcs[w]) for j, chip in enumerate(chips)]
            for cp in sends:
                cp.start()
            first += sends
        for w in range(n):
            for j, chip in enumerate(chips):
                copy(w, 1 + j, (*chip, c), me).wait_recv()
                cp = copy(w, fan + j, (*chip, c), sibling)
                cp.start()
                passed.append(cp)
        for w in range(n):
            copy(w, 0, sibling, me).wait_recv()
            for j, chip in enumerate(chips):
                copy(w, fan + j, (*chip, 1 - c), me).wait_recv()
        for cp in first + passed:
            cp.wait_send()
        for cp in mine:
            cp.wait()

    return pl.kernel(
        body, name=name, out_type=[_sds((N_DEV,) + s.shape, s.dtype) for s in shards],
        mesh=plsc.ScalarSubcoreMesh(axis_name="sequencer", num_cores=1),
        scratch_types=[pltpu.SemaphoreType.DMA((_PEERS * n,)), pltpu.SemaphoreType.DMA((_PEERS * n,)),
                       pltpu.SemaphoreType.DMA((n,))],
        compiler_params=pltpu.CompilerParams(collective_id=collective_id),
    )(*shards)


N_CHIPS = N_DEV // 2


def _sequencer_pair_exchange(sources, name, collective_id):
    n = len(sources)

    def body(*refs):
        srcs, zones = refs[:n], refs[n:2 * n]
        send_sems, recv_sems = refs[2 * n:]
        x, y, c = _mesh_pos()
        sibling = (x, y, 1 - c)
        barrier = pltpu.get_barrier_semaphore()
        pl.semaphore_signal(barrier, inc=1, device_id=sibling, device_id_type=_MESH)
        pl.semaphore_wait(barrier, 1)
        copies = []
        for w in range(n):
            for j in range(N_CHIPS):
                k = N_CHIPS * w + j
                cp = pltpu.make_async_remote_copy(
                    src_ref=srcs[w].at[2 * j + 1 - c], dst_ref=zones[w].at[j],
                    send_sem=send_sems.at[k], recv_sem=recv_sems.at[k], device_id=sibling, device_id_type=_MESH)
                cp.start()
                copies.append(cp)
        for cp in copies:
            cp.wait_recv()
        for cp in copies:
            cp.wait_send()

    return pl.kernel(
        body, name=name, out_type=[_sds((N_CHIPS,) + s.shape[1:], s.dtype) for s in sources],
        mesh=plsc.ScalarSubcoreMesh(axis_name="sequencer", num_cores=1),
        scratch_types=[pltpu.SemaphoreType.DMA((N_CHIPS * n,)), pltpu.SemaphoreType.DMA((N_CHIPS * n,))],
        compiler_params=pltpu.CompilerParams(collective_id=collective_id),
    )(*sources)


def _pair_sum(source, received, core, name, tokens=()):
    _, rows, cols = source.shape
    tr = _row_tile(rows)
    n_tok = len(tokens)

    def body(core_ref, s_ref, r_ref, *rest):
        o_ref = rest[n_tok]
        o_ref[...] = (s_ref[...].astype(_F32) + r_ref[...].astype(_F32)).astype(o_ref.dtype)

    quarter = pl.BlockSpec((N_CHIPS, tr, cols), lambda i, core_ref: (0, i, 0))
    mine = pl.BlockSpec((N_CHIPS, None, tr, cols), lambda i, core_ref: (0, core_ref[0], i, 0))
    spec = pltpu.PrefetchScalarGridSpec(
        num_scalar_prefetch=1, grid=(rows // tr,),
        in_specs=[mine, quarter] + [pl.BlockSpec(memory_space=pl.ANY)] * n_tok, out_specs=quarter)
    params = pltpu.CompilerParams(dimension_semantics=("arbitrary",), vmem_limit_bytes=VMEM_LIMIT)
    return pl.pallas_call(body, grid_spec=spec, out_shape=_sds((N_CHIPS, rows, cols), source.dtype),
                          compiler_params=params, name=name)(
        core, source.reshape(N_CHIPS, 2, rows, cols), received, *tokens)


def _sequencer_chip_exchange(partials, name, collective_id):
    n = len(partials)
    others = N_CHIPS - 1

    def body(*refs):
        srcs, zones = refs[:n], refs[n:2 * n]
        send_sems, recv_sems, local_sems = refs[2 * n:]
        x, y, c = _mesh_pos()
        mine = 2 * x + y
        peers = [(x ^ (r >> 1), y ^ (r & 1), c) for r in range(1, N_CHIPS)]
        barrier = pltpu.get_barrier_semaphore()
        for peer in peers:
            pl.semaphore_signal(barrier, inc=1, device_id=peer, device_id_type=_MESH)
        pl.semaphore_wait(barrier, others)
        local, sends, recvs = [], [], []
        for w in range(n):
            cp = pltpu.make_async_copy(srcs[w].at[mine], zones[w].at[mine], local_sems.at[w])
            cp.start()
            local.append(cp)
            for r, peer in enumerate(peers):
                theirs = 2 * peer[0] + peer[1]
                k = others * w + r
                send = pltpu.make_async_remote_copy(
                    src_ref=srcs[w].at[theirs], dst_ref=zones[w].at[mine],
                    send_sem=send_sems.at[k], recv_sem=recv_sems.at[k], device_id=peer, device_id_type=_MESH)
                send.start()
                sends.append(send)
                recvs.append(pltpu.make_async_remote_copy(
                    src_ref=srcs[w].at[theirs], dst_ref=zones[w].at[theirs],
                    send_sem=send_sems.at[k], recv_sem=recv_sems.at[k], device_id=peer, device_id_type=_MESH))
        for cp in recvs:
            cp.wait_recv()
        for cp in sends:
            cp.wait_send()
        for cp in local:
            cp.wait()

    return pl.kernel(
        body, name=name, out_type=[_sds(s.shape, s.dtype) for s in partials],
        mesh=plsc.ScalarSubcoreMesh(axis_name="sequencer", num_cores=1),
        scratch_types=[pltpu.SemaphoreType.DMA((others * n,)), pltpu.SemaphoreType.DMA((others * n,)),
                       pltpu.SemaphoreType.DMA((n,))],
        compiler_params=pltpu.CompilerParams(collective_id=collective_id),
    )(*partials)


def _row_tile(rows):
    return next(t for t in range(min(rows, 256), 0, -16) if rows % t == 0)


def _adam_update(g, w, m, v):
    new_m = ADAM_B1 * m + (1.0 - ADAM_B1) * g
    new_v = ADAM_B2 * v + (1.0 - ADAM_B2) * (g * g)
    m_hat = new_m / (1.0 - ADAM_B1 ** ADAM_STEP)
    v_hat = new_v / (1.0 - ADAM_B2 ** ADAM_STEP)
    return -ADAM_LR * (m_hat / (jnp.sqrt(v_hat) + ADAM_EPS) + ADAM_WD * w), new_m, new_v


def _adamw_small(parts, items, sums, name, tokens=()):
    n_p, n_i = len(parts), len(items)

    def body(*refs):
        p_refs, state, outs = refs[:n_p], refs[n_p:n_p + 3 * n_i], refs[n_p + 3 * n_i:]

        def total(part, rows, cols):
            shift = cols.start % _LANES
            window = slice(cols.start - shift, cols.start - shift + _LANES) if shift else cols
            n_rows = rows.stop - rows.start
            narrow = p_refs[part].dtype.itemsize < 4 and n_rows % _PACK_TILE
            tile = slice(rows.start, rows.start + _PACK_TILE) if narrow else rows
            g = p_refs[part][0, tile, window].astype(_F32)
            for s in range(1, N_DEV):
                g = g + p_refs[part][s, tile, window].astype(_F32)
            g = g[:n_rows] if narrow else g
            return pltpu.roll(g, _LANES - shift, 1)[:, :cols.stop - cols.start] if shift else g

        for i, (part, rows, cols, _, _, _) in enumerate(items):
            g = total(part, rows, cols)
            w_ref, m_ref, v_ref = state[3 * i:3 * i + 3]
            delta, new_m, new_v = _adam_update(g, w_ref[...], m_ref[...], v_ref[...])
            outs[4 * i][...] = g
            outs[4 * i + 1][...] = delta
            outs[4 * i + 2][...] = new_m
            outs[4 * i + 3][...] = new_v
        for j, (part, rows, cols) in enumerate(sums):
            outs[4 * n_i + j][...] = total(part, rows, cols)

    ins = list(parts) + [a for item in items for a in item[3:]]
    out_shapes = [item[3].shape for item in items for _ in range(4)]
    out_shapes += [(rows.stop - rows.start, cols.stop - cols.start) for _, rows, cols in sums]
    out = _call(body, (1,), [_whole(a.shape) for a in ins], [_whole(s) for s in out_shapes],
                [_sds(s, _F32) for s in out_shapes], name, tokens=tokens)(*ins)
    return [out[4 * i:4 * i + 4] for i in range(n_i)], out[4 * n_i:]


def _adamw(parts, w, m, v, name, tokens=(), transposed_parts=False):
    rows, cols = w.shape
    tr = _row_tile(rows)
    n_parts = parts.shape[0]

    def body(p_ref, w_ref, m_ref, v_ref, g_ref, d_ref, nm_ref, nv_ref):
        g = p_ref[0].astype(_F32)
        for s in range(1, n_parts):
            g = g + p_ref[s].astype(_F32)
        if transposed_parts:
            g = g.T
        new_m = ADAM_B1 * m_ref[...] + (1.0 - ADAM_B1) * g
        new_v = ADAM_B2 * v_ref[...] + (1.0 - ADAM_B2) * (g * g)
        m_hat = new_m / (1.0 - ADAM_B1 ** ADAM_STEP)
        v_hat = new_v / (1.0 - ADAM_B2 ** ADAM_STEP)
        g_ref[...] = g
        d_ref[...] = -ADAM_LR * (m_hat / (jnp.sqrt(v_hat) + ADAM_EPS) + ADAM_WD * w_ref[...])
        nm_ref[...] = new_m
        nv_ref[...] = new_v

    blk = _rows(tr, cols)
    part = (pl.BlockSpec((n_parts, cols, tr), lambda i: (0, 0, i)) if transposed_parts
            else pl.BlockSpec((n_parts, tr, cols), lambda i: (0, i, 0)))
    return _call(body, (rows // tr,), [part, blk, blk, blk],
                 [blk] * 4, [_sds((rows, cols), _F32)] * 4, name, tokens=tokens)(parts, w, m, v)


_SMALL = ("g_pre_mix", "ssm_lambda_re", "ssm_lambda_im", "ssm_log_dt", "ssm_b_re", "ssm_b_im",
          "ssm_c_re", "ssm_c_im", "ssm_d", "b_glu", "attn_sinks", "g_ssm_out", "g_attn_out",
          "g_post_mix", "g_pre_ffn", "g_post_ffn")
_BIG = ("w_in", "w_glu", "w_out", "w_gate_up", "w_down")
_WEIGHTS = ("g_pre_mix", "w_in", "ssm_lambda_re", "ssm_lambda_im", "ssm_log_dt", "ssm_b_re", "ssm_b_im",
            "ssm_c_re", "ssm_c_im", "ssm_d", "w_glu", "b_glu", "attn_sinks", "g_ssm_out", "g_attn_out",
            "w_out", "g_post_mix", "g_pre_ffn", "w_gate_up", "w_down", "g_post_ffn")
_LANES = 128


_SHAPE_2D = {
    "g_pre_mix": (1, D_MODEL), "ssm_lambda_re": (SSM_GROUPS, SSM_STATE), "ssm_lambda_im": (SSM_GROUPS, SSM_STATE),
    "ssm_log_dt": (1, SSM_GROUPS), "ssm_b_re": (SSM_WIDTH, SSM_STATE), "ssm_b_im": (SSM_WIDTH, SSM_STATE),
    "ssm_c_re": (SSM_WIDTH, SSM_STATE), "ssm_c_im": (SSM_WIDTH, SSM_STATE), "ssm_d": (SSM_GROUPS, SSM_GROUP),
    "b_glu": (1, 2 * SSM_WIDTH), "attn_sinks": (1, N_Q_HEADS), "g_ssm_out": (1, SSM_WIDTH),
    "g_attn_out": (1, ATTN_WIDTH), "g_post_mix": (1, D_MODEL), "g_pre_ffn": (1, D_MODEL), "g_post_ffn": (1, D_MODEL)}
_ROW_WIDTH = {"g_pre_mix": D_MODEL, "b_glu": 2 * SSM_WIDTH, "attn_sinks": _LANES, "g_ssm_out": SSM_WIDTH,
              "g_attn_out": ATTN_WIDTH, "g_post_mix": D_MODEL, "g_pre_ffn": D_MODEL, "g_post_ffn": D_MODEL,
              "loss": _LANES}
_PER_GROUP_TRANSPOSED = ("ssm_b_re", "ssm_b_im")


def _to_2d(name, a):
    if name in _PER_GROUP_TRANSPOSED:
        a = a.reshape(SSM_GROUPS, SSM_STATE, SSM_GROUP).transpose(0, 2, 1)
    return a.reshape(_SHAPE_2D[name])


def _from_2d(name, a, shape):
    if name in _PER_GROUP_TRANSPOSED:
        a = a.reshape(SSM_GROUPS, SSM_GROUP, SSM_STATE).transpose(0, 2, 1)
    return a.reshape(shape)


def _row_slots(names):
    slots, row, col = {}, 0, 0
    for n in names:
        width = _ROW_WIDTH[n]
        if col + width > D_MODEL:
            row, col = row + 1, 0
        slots[n] = (row, col, width)
        col += width
    return slots


def _stack_rows(named, slots):
    n_rows = -(-(max(r for r, _, _ in slots.values()) + 1) // 8) * 8
    lines = []
    for r in range(n_rows):
        pieces = [named[n] for n, (row, _, _) in slots.items() if row == r]
        used = sum(p.shape[1] for p in pieces)
        if used < D_MODEL:
            pieces.append(jnp.zeros((1, D_MODEL - used), _F32))
        lines.append(jnp.concatenate(pieces, axis=1) if len(pieces) > 1 else pieces[0])
    return jnp.concatenate(lines, axis=0)


def kernel(x, positions, g_pre_mix, w_in, ssm_lambda_re, ssm_lambda_im, ssm_log_dt, ssm_b_re, ssm_b_im, ssm_c_re, ssm_c_im, ssm_d, w_glu, b_glu, attn_sinks, g_ssm_out, g_attn_out, w_out, g_post_mix, g_pre_ffn, w_gate_up, w_down, g_post_ffn, loss_target, m_g_pre_mix, m_w_in, m_ssm_lambda_re, m_ssm_lambda_im, m_ssm_log_dt, m_ssm_b_re, m_ssm_b_im, m_ssm_c_re, m_ssm_c_im, m_ssm_d, m_w_glu, m_b_glu, m_attn_sinks, m_g_ssm_out, m_g_attn_out, m_w_out, m_g_post_mix, m_g_pre_ffn, m_w_gate_up, m_w_down, m_g_post_ffn, v_g_pre_mix, v_w_in, v_ssm_lambda_re, v_ssm_lambda_im, v_ssm_log_dt, v_ssm_b_re, v_ssm_b_im, v_ssm_c_re, v_ssm_c_im, v_ssm_d, v_w_glu, v_b_glu, v_attn_sinks, v_g_ssm_out, v_g_attn_out, v_w_out, v_g_post_mix, v_g_pre_ffn, v_w_gate_up, v_w_down, v_g_post_ffn):
    w = dict(g_pre_mix=g_pre_mix, w_in=w_in, ssm_lambda_re=ssm_lambda_re, ssm_lambda_im=ssm_lambda_im,
             ssm_log_dt=ssm_log_dt, ssm_b_re=ssm_b_re, ssm_b_im=ssm_b_im, ssm_c_re=ssm_c_re, ssm_c_im=ssm_c_im,
             ssm_d=ssm_d, w_glu=w_glu, b_glu=b_glu, attn_sinks=attn_sinks, g_ssm_out=g_ssm_out,
             g_attn_out=g_attn_out, w_out=w_out, g_post_mix=g_post_mix, g_pre_ffn=g_pre_ffn,
             w_gate_up=w_gate_up, w_down=w_down, g_post_ffn=g_post_ffn)
    m = dict(g_pre_mix=m_g_pre_mix, w_in=m_w_in, ssm_lambda_re=m_ssm_lambda_re, ssm_lambda_im=m_ssm_lambda_im,
             ssm_log_dt=m_ssm_log_dt, ssm_b_re=m_ssm_b_re, ssm_b_im=m_ssm_b_im, ssm_c_re=m_ssm_c_re,
             ssm_c_im=m_ssm_c_im, ssm_d=m_ssm_d, w_glu=m_w_glu, b_glu=m_b_glu, attn_sinks=m_attn_sinks,
             g_ssm_out=m_g_ssm_out, g_attn_out=m_g_attn_out, w_out=m_w_out, g_post_mix=m_g_post_mix,
             g_pre_ffn=m_g_pre_ffn, w_gate_up=m_w_gate_up, w_down=m_w_down, g_post_ffn=m_g_post_ffn)
    v = dict(g_pre_mix=v_g_pre_mix, w_in=v_w_in, ssm_lambda_re=v_ssm_lambda_re, ssm_lambda_im=v_ssm_lambda_im,
             ssm_log_dt=v_ssm_log_dt, ssm_b_re=v_ssm_b_re, ssm_b_im=v_ssm_b_im, ssm_c_re=v_ssm_c_re,
             ssm_c_im=v_ssm_c_im, ssm_d=v_ssm_d, w_glu=v_w_glu, b_glu=v_b_glu, attn_sinks=v_attn_sinks,
             g_ssm_out=v_g_ssm_out, g_attn_out=v_g_attn_out, w_out=v_w_out, g_post_mix=v_g_post_mix,
             g_pre_ffn=v_g_pre_ffn, w_gate_up=v_w_gate_up, w_down=v_w_down, g_post_ffn=v_g_post_ffn)

    transposed = ("w_in", "w_glu", "w_gate_up")
    native_transposed = ("w_in", "w_gate_up")
    shard = {n: (w[n][0].T if n in transposed else w[n][0]).astype(_BF16) for n in _BIG}
    gathered = {}
    for cid, names in enumerate((("w_in",), ("w_glu", "w_out"), ("w_gate_up", "w_down")), start=1):
        lands = _sequencer_gather([shard[n] for n in names], "gather_" + names[0], cid)
        gathered.update({n: a.reshape(-1, a.shape[2]) for n, a in zip(names, lands)})

    def fetch(names, after):
        del after
        return [gathered[n] for n in names]

    sent = []
    ids = iter(range(4, 16))
    two_step = {}

    def publish(named):
        big = [n for n in named if n in _BIG]
        if set(big) == {"w_gate_up", "w_down"}:
            blocks = [named[n].reshape(N_DEV, -1, named[n].shape[1]) for n in big]
            two_step.update(names=big, blocks=blocks,
                            received=_sequencer_pair_exchange(blocks, "grads_pair", next(ids)))
            return [named[n] for n in big]
        rows = [n for n in named if n in _ROW_WIDTH]
        plain = [n for n in named if n not in big + rows]
        sources = [named[n].reshape(N_DEV, -1, named[n].shape[1]) for n in big]
        slots = _row_slots(rows)
        if rows:
            sources.append(_stack_rows(named, slots))
        sources += [named[n] for n in plain]
        flags = [True] * len(big) + [False] * (len(sources) - len(big))
        cid = next(ids)
        if big:
            lands = _sequencer_exchange(sources, flags, "grads_%d" % cid, cid)
        else:
            lands = _sequencer_gather(sources, "grads_%d" % cid, cid)
        sent.append((big, slots, plain, lands))
        return [named[n] for n in big]

    def progress(after):
        core = lax.axis_index("c").astype(jnp.int32).reshape(1)
        partials = [_pair_sum(b, r, core, "pair_sum_" + n, [after])
                    for n, b, r in zip(two_step["names"], two_step["blocks"], two_step["received"])]
        sent.append((two_step["names"], {}, [], _sequencer_chip_exchange(partials, "grads_chips", next(ids))))
        return partials

    p = {n: w[n] for n in _SMALL}
    grad_x = _local_step(x[0], positions[0], loss_target[0], p, fetch, publish, progress)

    state = {n: [_to_2d(n, a) for a in (w[n], m[n], v[n])] for n in _SMALL}
    result = {}
    total_loss = None
    chain = []
    for big, slots, plain, lands in sent:
        lands = list(lands)
        after = list(chain)
        for name in big:
            part = lands.pop(0)
            if name in native_transposed:
                updated = _adamw(part, w[name][0].T, m[name][0].T, v[name][0].T, "adamw_" + name, after)
                result[name] = [a.T[None] for a in updated]
                chain.append(updated[3])
                continue
            updated = _adamw(part, w[name][0], m[name][0], v[name][0], "adamw_" + name, after,
                             transposed_parts=name in transposed)
            result[name] = [a[None] for a in updated]
            chain.append(updated[3])
        parts, items, sums, names = [], [], [], []
        if slots:
            parts.append(lands.pop(0))
            for name, (row, col, _) in slots.items():
                if name == "loss":
                    sums.append((0, slice(row, row + 1), slice(col, col + _LANES)))
                else:
                    items.append((0, slice(row, row + 1), slice(col, col + _SHAPE_2D[name][1]), *state[name]))
                    names.append(name)
        for name in plain:
            packed = _SSM_PACK if name == "ssm_pack" else {name: (0, _SHAPE_2D[name][0], 0, _SHAPE_2D[name][1])}
            for member, (first, rows_n, lane, cols_n) in packed.items():
                items.append((len(parts), slice(first, first + rows_n), slice(lane, lane + cols_n), *state[member]))
                names.append(member)
            parts.append(lands.pop(0))
        if items:
            updated, summed = _adamw_small(parts, items, sums, "adamw_small_" + names[0], after)
            chain.append(updated[0][3])
            result.update(dict(zip(names, updated)))
            if summed:
                total_loss = summed[0][0, 0]

    out = [total_loss, grad_x[None]]
    for kind in range(4):
        out += [_from_2d(n, result[n][kind], w[n].shape) for n in _WEIGHTS]
    return tuple(out)
```

```python
import math

import numpy as np
import jax
import jax.numpy as jnp
from jax import lax
from jax.experimental import pallas as pl
from jax.experimental.pallas import tpu as pltpu
from jax.experimental.pallas import tpu_sc as plsc

D_MODEL = 1024
SSM_WIDTH = 512
SSM_GROUP = 16
SSM_GROUPS = 32
SSM_STATE = 64
N_STATE = SSM_GROUPS * SSM_STATE
ATTN_WIDTH = 512
HEAD_DIM = 64
N_Q_HEADS = 8
N_KV_HEADS = 2
Q_PER_KV = 4
KV_WIDTH = 128
IN_WIDTH = 1280
BLOCK = 128
ROPE_DIM = 16
ROPE_THETA = 500000.0
D_FF = 2816
NORM_EPS = 1e-6
MASK_VALUE = -1e30
ADAM_LR = 0.001
ADAM_B1 = 0.9
ADAM_B2 = 0.999
ADAM_EPS = 1e-08
ADAM_WD = 0.01
ADAM_STEP = 10

N_DEV = 8
SCAN_CHUNKS = 8
SCAN_UNROLL = 8
FFN_CHUNK = 2816
TOKEN_TILE = 256
VMEM_LIMIT = 56 * 1024 * 1024

_F32 = jnp.float32
_BF16 = jnp.bfloat16
_MXU = jnp.bfloat16

_NN = ((1,), (0,))
_NT = ((1,), (1,))
_TN = ((0,), (0,))


def _dot(a, b, dims):
    return lax.dot_general(a.astype(_MXU), b.astype(_MXU), (dims, ((), ())),
                           preferred_element_type=_F32)


def _dot_exact(a, b, dims):
    return lax.dot_general(a.astype(_F32), b.astype(_F32), (dims, ((), ())),
                           precision=lax.Precision.HIGHEST, preferred_element_type=_F32)


def _iota(shape, dim):
    return lax.broadcasted_iota(jnp.int32, shape, dim)


def _rms_fwd(x, g):
    r = lax.rsqrt(jnp.mean(x * x, axis=-1, keepdims=True) + NORM_EPS)
    return x * r * g, r


def _rms_bwd(dy, x, g, r):
    a = dy * g
    xn = x * r
    dx = r * (a - xn * jnp.mean(a * xn, axis=-1, keepdims=True))
    dg = jnp.sum(dy * xn, axis=0, keepdims=True)
    return dx, dg


def _call(body, grid, in_specs, out_specs, out_shape, name, scratch=(), tokens=()):
    params = pltpu.CompilerParams(dimension_semantics=("arbitrary",) * len(grid),
                                  vmem_limit_bytes=VMEM_LIMIT)
    n_in, n_tok = len(in_specs), len(tokens)

    def run(*refs):
        return body(*refs[:n_in], *refs[n_in + n_tok:])

    call = pl.pallas_call(run, grid=grid,
                          in_specs=list(in_specs) + [pl.BlockSpec(memory_space=pl.ANY)] * n_tok,
                          out_specs=out_specs, out_shape=out_shape, scratch_shapes=list(scratch),
                          compiler_params=params, name=name)
    return lambda *args: call(*args, *tokens)


def _rows(tm, n):
    return pl.BlockSpec((tm, n), lambda i: (i, 0))


def _whole(shape):
    nd = len(shape)
    return pl.BlockSpec(shape, lambda i: (0,) * nd)


def _sds(shape, dtype):
    return jax.ShapeDtypeStruct(shape, dtype)


def _tile(L):
    return min(TOKEN_TILE, L)


def _chunk_tile(L):
    return L // SCAN_CHUNKS


def _chunk_block(L, n):
    return pl.BlockSpec((_chunk_tile(L), n), lambda i: (0, i))


def _chunk_shape(L, n):
    return (_chunk_tile(L), SCAN_CHUNKS * n)


def _accumulate(ref, val, first):
    @pl.when(first)
    def _():
        ref[...] = val

    @pl.when(jnp.logical_not(first))
    def _():
        ref[...] += val


def _rope_rows():
    half = ROPE_DIM // 2
    inv = (np.float32(ROPE_THETA) ** (-np.arange(half, dtype=np.float32) * np.float32(2.0) / np.float32(ROPE_DIM))).astype(np.float32)
    col = np.arange(KV_WIDTH) % HEAD_DIM
    freq = np.where(col < ROPE_DIM, inv[col % half], 0.0).astype(np.float32)
    sign = np.where(col < half, -1.0, np.where(col < ROPE_DIM, 1.0, 0.0)).astype(np.float32)
    return freq[None, :], sign[None, :]


def _rope_tables(pos_col):
    L = pos_col.shape[0]
    tm = _tile(L)
    freq, sign = _rope_rows()

    def body(pos_ref, freq_ref, sign_ref, cos_ref, sin_ref):
        ang = pos_ref[...].astype(_F32) * freq_ref[...]
        cos_ref[...] = jnp.cos(ang)
        sin_ref[...] = jnp.sin(ang) * sign_ref[...]

    return _call(body, (L // tm,),
                 [_rows(tm, 1), _whole((1, KV_WIDTH)), _whole((1, KV_WIDTH))],
                 [_rows(tm, KV_WIDTH), _rows(tm, KV_WIDTH)],
                 [_sds((L, KV_WIDTH), _F32)] * 2, "rope_tables")(pos_col, jnp.asarray(freq), jnp.asarray(sign))


def _widen(t, width):
    return t if width == KV_WIDTH else jnp.concatenate([t] * (width // KV_WIDTH), axis=1)


def _rope_partner(t):
    w = t.shape[1]
    in_head = _iota((1, w), 1) & (HEAD_DIM - 1)
    second = jnp.where(in_head < ROPE_DIM, pltpu.roll(t, ROPE_DIM // 2, 1), 0.0)
    return jnp.where(in_head < ROPE_DIM // 2, pltpu.roll(t, w - ROPE_DIM // 2, 1), second)


def _rope_apply(t, cos_t, sin_t):
    w = t.shape[1]
    return t * _widen(cos_t, w) + _rope_partner(t) * _widen(sin_t, w)


def _rope_transpose(dt, cos_t, sin_t):
    w = dt.shape[1]
    return dt * _widen(cos_t, w) + _rope_partner(dt * _widen(sin_t, w))


def _in_proj(x, g_pre_mix, w_in, cos_t, sin_t):
    L = x.shape[0]
    tm = _chunk_tile(L)

    def body(x_ref, g_ref, w_ref, cos_ref, sin_ref, hn_ref, u_ref, q_ref, k_ref, v_ref):
        hn, _ = _rms_fwd(x_ref[...], g_ref[...])
        hn = hn.astype(_BF16)
        hn_ref[...] = hn
        proj = _dot(hn, w_ref[...], _NT)
        u_ref[...] = proj[:, :SSM_WIDTH]
        q = proj[:, SSM_WIDTH:SSM_WIDTH + ATTN_WIDTH]
        k = proj[:, SSM_WIDTH + ATTN_WIDTH:SSM_WIDTH + ATTN_WIDTH + KV_WIDTH]
        cos_v, sin_v = cos_ref[...], sin_ref[...]
        q_ref[...] = _rope_apply(q, cos_v, sin_v).astype(_BF16)
        k_ref[...] = _rope_apply(k, cos_v, sin_v).astype(_BF16)
        v_ref[...] = proj[:, SSM_WIDTH + ATTN_WIDTH + KV_WIDTH:].astype(_BF16)

    return _call(body, (L // tm,),
                 [_rows(tm, D_MODEL), _whole((1, D_MODEL)), _whole((IN_WIDTH, D_MODEL)),
                  _rows(tm, KV_WIDTH), _rows(tm, KV_WIDTH)],
                 [_rows(tm, D_MODEL), _chunk_block(L, SSM_WIDTH), _rows(tm, ATTN_WIDTH),
                  _rows(tm, KV_WIDTH), _rows(tm, KV_WIDTH)],
                 [_sds((L, D_MODEL), _BF16), _sds(_chunk_shape(L, SSM_WIDTH), _F32), _sds((L, ATTN_WIDTH), _BF16),
                  _sds((L, KV_WIDTH), _BF16), _sds((L, KV_WIDTH), _BF16)],
                 "in_proj")(x, g_pre_mix, w_in, cos_t, sin_t)


def _s5_discretize(lam_re, lam_im, log_dt):
    lr = jnp.minimum(lam_re, -1e-4)
    li = lam_im
    dt = jnp.exp(log_dt)
    mag = jnp.exp(lr * dt)
    ar = mag * jnp.cos(li * dt)
    ai = mag * jnp.sin(li * dt)
    den = lr * lr + li * li
    fr = ((ar - 1.0) * lr + ai * li) / den
    fi = (ai * lr - (ar - 1.0) * li) / den
    return ar, ai, fr, fi


SUPER = 4
SB_STATE = N_STATE // SUPER
SB_WIDTH = SSM_WIDTH // SUPER


def _sb_state(k):
    return slice(SB_STATE * k, SB_STATE * (k + 1))


def _sb_width(k):
    return slice(SB_WIDTH * k, SB_WIDTH * (k + 1))


def _dt_column(log_dt_row):
    eye = _iota((SSM_GROUPS, SSM_GROUPS), 0) == _iota((SSM_GROUPS, SSM_GROUPS), 1)
    return jnp.sum(jnp.where(eye, log_dt_row, 0.0), axis=1, keepdims=True)


def _group_masks():
    e64 = ((_iota((SSM_STATE, N_STATE), 1) & (SSM_STATE - 1)) == _iota((SSM_STATE, N_STATE), 0)).astype(_F32)
    own = _iota((SSM_GROUPS, N_STATE), 0) == (_iota((SSM_GROUPS, N_STATE), 1) >> 6)
    return e64, own


def _rows_of_group():
    return ((_iota((SSM_WIDTH, SSM_GROUPS), 0) >> 4) == _iota((SSM_WIDTH, SSM_GROUPS), 1)).astype(_F32)


def _ssm_prep(lam_re, lam_im, log_dt, b_re, b_im, c_re, c_im):
    def body(lr_ref, li_ref, ld_ref, bre, bim, cre, cim, ar_ref, ai_ref, btr, bti, ctr, cti):
        ar, ai, fr, fi = _s5_discretize(lr_ref[...], li_ref[...], _dt_column(ld_ref[...]))
        e64, own = _group_masks()
        mask_c = (_iota((SSM_WIDTH, N_STATE), 0) >> 4) == (_iota((SSM_WIDTH, N_STATE), 1) >> 6)

        def to_row(t):
            return jnp.sum(jnp.where(own, _dot_exact(t, e64, _NN), 0.0), axis=0, keepdims=True)

        def fold(m):
            full = jnp.where(mask_c, _dot(m, e64, _NN), 0.0)
            return sum(full[_sb_width(k), :] for k in range(SUPER)).astype(_BF16)

        ar_ref[...] = to_row(ar)
        ai_ref[...] = to_row(ai)
        spread = _rows_of_group()
        fr_t = _dot_exact(spread, fr, _NN)
        fi_t = _dot_exact(spread, fi, _NN)
        btr[...] = fold(fr_t * bre[...] - fi_t * bim[...])
        bti[...] = fold(fr_t * bim[...] + fi_t * bre[...])
        ctr[...] = fold(cre[...])
        cti[...] = fold(cim[...])

    row = (1, N_STATE)
    ins = [lam_re, lam_im, log_dt, b_re, b_im, c_re, c_im]
    return _call(body, (1,), [_whole(a.shape) for a in ins],
                 [_whole(row), _whole(row)] + [_whole((SB_WIDTH, N_STATE))] * 4,
                 [_sds(row, _F32), _sds(row, _F32)] + [_sds((SB_WIDTH, N_STATE), _BF16)] * 4,
                 "ssm_prep")(*ins)


def _complex_power(ar, ai, n):
    pr, pi = jnp.ones_like(ar), jnp.zeros_like(ai)
    while n:
        if n & 1:
            pr, pi = pr * ar - pi * ai, pr * ai + pi * ar
        ar, ai = ar * ar - ai * ai, 2.0 * ar * ai
        n >>= 1
    return pr, pi


def _chunk_carries(er, ei, pr, pi, reverse):
    rows = _iota(er.shape, 0)
    sr = jnp.zeros_like(pr)
    si = jnp.zeros_like(pi)
    out_r = jnp.zeros_like(er)
    out_i = jnp.zeros_like(ei)
    order = range(SCAN_CHUNKS - 1, 0, -1) if reverse else range(SCAN_CHUNKS - 1)
    for c in order:
        e_r = er[c:c + 1, :]
        e_i = ei[c:c + 1, :]
        sr, si = pr * sr - pi * si + e_r, pr * si + pi * sr + e_i
        nxt = c - 1 if reverse else c + 1
        out_r = jnp.where(rows == nxt, sr, out_r)
        out_i = jnp.where(rows == nxt, si, out_i)
    return out_r, out_i


_GELU_K = math.sqrt(2.0 / math.pi)
_GELU_C = 0.044715


def _gelu(y):
    return 0.5 * y * (1.0 + jnp.tanh(_GELU_K * (y + _GELU_C * y * y * y)))


def _gelu_grad(y):
    t = jnp.tanh(_GELU_K * (y + _GELU_C * y * y * y))
    return 0.5 * (1.0 + t) + 0.5 * y * (1.0 - t * t) * _GELU_K * (1.0 + 3.0 * _GELU_C * y * y)


def _step_rows(t):
    return pl.ds(pl.multiple_of(t * SCAN_CHUNKS, SCAN_CHUNKS), SCAN_CHUNKS)


def _scan_in_place(br, bi, ar, ai, T):
    W = br.shape[1]
    ar8 = jnp.broadcast_to(ar, (SCAN_CHUNKS, W))
    ai8 = jnp.broadcast_to(ai, (SCAN_CHUNKS, W))

    def local(t, c):
        cr, ci = c
        rows = _step_rows(t)
        return ar8 * cr - ai8 * ci + br[rows, :], ar8 * ci + ai8 * cr + bi[rows, :]

    zero = jnp.zeros((SCAN_CHUNKS, W), _F32)
    er, ei = lax.fori_loop(0, T, local, (zero, zero), unroll=SCAN_UNROLL)
    pr, pi = _complex_power(ar, ai, T)
    carries = _chunk_carries(er, ei, pr, pi, reverse=False)

    def final(t, c):
        nr, ni = local(t, c)
        rows = _step_rows(t)
        br[rows, :] = nr
        bi[rows, :] = ni
        return nr, ni

    lax.fori_loop(0, T, final, carries, unroll=SCAN_UNROLL)


def _scan_reverse_in_place(dr, di, xr, xi, ar, ai, T):
    W = dr.shape[1]
    ar8 = jnp.broadcast_to(ar, (SCAN_CHUNKS, W))
    ai8 = jnp.broadcast_to(ai, (SCAN_CHUNKS, W))

    def local(t, c):
        cr, ci = c
        rows = _step_rows(t)
        return ar8 * cr + ai8 * ci + dr[rows, :], ar8 * ci - ai8 * cr + di[rows, :]

    zero = jnp.zeros((SCAN_CHUNKS, W), _F32)
    er, ei = lax.fori_loop(0, T, lambda k, c: local(T - 1 - k, c), (zero, zero), unroll=SCAN_UNROLL)
    pr, pi = _complex_power(ar, -ai, T)
    sr, si = _chunk_carries(er, ei, pr, pi, reverse=True)

    def grad_a(acc, nr, ni, xpr, xpi):
        return acc[0] + nr * xpr + ni * xpi, acc[1] + ni * xpr - nr * xpi

    def final(k, c):
        t = T - 1 - k
        nr, ni = local(t, c[:2])
        rows = _step_rows(t)
        dr[rows, :] = nr
        di[rows, :] = ni
        before = _step_rows(t - 1)
        gr, gi = grad_a(c[2:], nr, ni, xr[before, :], xi[before, :])
        return nr, ni, gr, gi

    cr, ci, gr, gi = lax.fori_loop(0, T - 1, final, (sr, si, zero, zero), unroll=SCAN_UNROLL)
    nr, ni = local(0, (cr, ci))
    dr[_step_rows(0), :] = nr
    di[_step_rows(0), :] = ni
    first = _iota((SCAN_CHUNKS, W), 0) == 0
    last = _step_rows(T - 1)
    xpr = jnp.where(first, 0.0, pltpu.roll(xr[last, :], 1, 0))
    xpi = jnp.where(first, 0.0, pltpu.roll(xi[last, :], 1, 0))
    gr, gi = grad_a((gr, gi), nr, ni, xpr, xpi)
    return jnp.sum(gr, axis=0, keepdims=True), jnp.sum(gi, axis=0, keepdims=True)


def _ssm_super_specs(L):
    width = pl.BlockSpec((L, SB_WIDTH), lambda k: (0, k))
    matrix = pl.BlockSpec((SB_WIDTH, SB_STATE), lambda k: (0, k))
    row = pl.BlockSpec((1, SB_STATE), lambda k: (0, k))
    return width, matrix, row


def _ssm_core_fwd(u, bt_re, bt_im, ct_re, ct_im, a_re, a_im):
    L = u.shape[0]
    T = L // SCAN_CHUNKS

    def body(u_ref, br_ref, bi_ref, cr_ref, ci_ref, ar_ref, ai_ref, y_ref, xr, xi):
        ub = u_ref[...].astype(_BF16)
        xr[...] = _dot(ub, br_ref[...], _NN)
        xi[...] = _dot(ub, bi_ref[...], _NN)
        _scan_in_place(xr, xi, ar_ref[...], ai_ref[...], T)
        y_ref[...] = _dot(xr[...], cr_ref[...], _NT) - _dot(xi[...], ci_ref[...], _NT)

    width, matrix, row = _ssm_super_specs(L)
    state = pl.BlockSpec((L, SB_STATE), lambda k: (0, k))
    return _call(body, (SUPER,), [width, matrix, matrix, matrix, matrix, row, row], [width, state, state],
                 [_sds((L, SSM_WIDTH), _F32)] + [_sds((L, N_STATE), _F32)] * 2,
                 "ssm_core_fwd")(u, bt_re, bt_im, ct_re, ct_im, a_re, a_im)


def _ssm_core_bwd(u, dy, dud, x_re, x_im, bt_re, bt_im, ct_re, ct_im, a_re, a_im, tokens=()):
    L = u.shape[0]
    T = L // SCAN_CHUNKS

    def body(u_ref, dy_ref, dud_ref, xr, xi, br_ref, bi_ref, cr_ref, ci_ref, ar_ref, ai_ref,
             du_ref, dcr_ref, dci_ref, dbr_ref, dbi_ref, dar_ref, dai_ref, lr, li):
        dyb = dy_ref[...]
        lr[...] = _dot(dyb, cr_ref[...], _NN)
        li[...] = -_dot(dyb, ci_ref[...], _NN)
        da_re, da_im = _scan_reverse_in_place(lr, li, xr, xi, ar_ref[...], ai_ref[...], T)
        dar_ref[...] = da_re
        dai_ref[...] = da_im
        du_ref[...] = _dot(lr[...], br_ref[...], _NT) + _dot(li[...], bi_ref[...], _NT) + dud_ref[...]
        ub = u_ref[...].astype(_BF16)
        dcr_ref[...] = _dot(dyb, xr[...], _TN)
        dci_ref[...] = _dot(dyb, xi[...], _TN)
        dbr_ref[...] = _dot(ub, lr[...], _TN)
        dbi_ref[...] = _dot(ub, li[...], _TN)

    width, matrix, row = _ssm_super_specs(L)
    state = pl.BlockSpec((L, SB_STATE), lambda k: (0, k))
    return _call(body, (SUPER,), [width, width, width, state, state, matrix, matrix, matrix, matrix, row, row],
                 [width] + [matrix] * 4 + [row] * 2,
                 [_sds((L, SSM_WIDTH), _F32)] + [_sds((SB_WIDTH, N_STATE), _F32)] * 4 + [_sds((1, N_STATE), _F32)] * 2,
                 "ssm_core_bwd", scratch=[pltpu.VMEM((L, SB_STATE), _F32)] * 2,
                 tokens=tokens)(u, dy, dud, x_re, x_im, bt_re, bt_im, ct_re, ct_im, a_re, a_im)


def _ssm_out(cx, u, d_row, w_glu, b_glu, g_ssm):
    L = u.shape[0]
    tm = _tile(L)

    def body(cx_ref, u_ref, d_ref, w_ref, b_ref, g_ref, y_ref, z_ref, n_ref, stage):
        y = cx_ref[...] + d_ref[...] * u_ref[...]
        y_ref[...] = y
        z = _dot(_gelu(y), w_ref[...], _NT) + b_ref[...]
        z_ref[...] = z
        out = z[:, :SSM_WIDTH] * jax.nn.sigmoid(z[:, SSM_WIDTH:])
        n, _ = _rms_fwd(out, g_ref[...])
        for k in range(SSM_WIDTH // _LANES):
            stage[k] = n[:, _LANES * k:_LANES * (k + 1)]
            for c in range(SCAN_CHUNKS):
                rows = stage[k, pl.ds(c, tm // SCAN_CHUNKS, stride=SCAN_CHUNKS), :]
                lane = SSM_WIDTH * c + _LANES * k
                n_ref[:, lane:lane + _LANES] = rows.astype(_BF16)

    return _call(body, (L // tm,),
                 [_rows(tm, SSM_WIDTH), _rows(tm, SSM_WIDTH), _whole((1, SSM_WIDTH)),
                  _whole((2 * SSM_WIDTH, SSM_WIDTH)), _whole((1, 2 * SSM_WIDTH)), _whole((1, SSM_WIDTH))],
                 [_rows(tm, SSM_WIDTH), _rows(tm, 2 * SSM_WIDTH), _rows(tm // SCAN_CHUNKS, SCAN_CHUNKS * SSM_WIDTH)],
                 [_sds((L, SSM_WIDTH), _F32), _sds((L, 2 * SSM_WIDTH), _F32), _sds(_chunk_shape(L, SSM_WIDTH), _BF16)],
                 "ssm_out", scratch=[pltpu.VMEM((SSM_WIDTH // _LANES, tm, _LANES), _F32)])(
        cx, u, d_row, w_glu, b_glu, g_ssm)


def _ssm_out_bwd(dn, y, z, u, d_row, w_glu, g_ssm, tokens=()):
    L = u.shape[0]
    tm = _tile(L)

    def body(dn_ref, y_ref, z_ref, u_ref, d_ref, w_ref, g_ref,
             gy_ref, dz_ref, dy_ref, dud_ref, dg_ref, db_ref, dd_ref, stage):
        first = pl.program_id(0) == 0
        for k in range(SSM_WIDTH // _LANES):
            for c in range(SCAN_CHUNKS):
                lane = SSM_WIDTH * c + _LANES * k
                stage[k, pl.ds(c, tm // SCAN_CHUNKS, stride=SCAN_CHUNKS), :] = dn_ref[:, lane:lane + _LANES]
        dn = jnp.concatenate([stage[k] for k in range(SSM_WIDTH // _LANES)], axis=1)
        z = z_ref[...]
        z1, z2 = z[:, :SSM_WIDTH], z[:, SSM_WIDTH:]
        sig = jax.nn.sigmoid(z2)
        out = z1 * sig
        g = g_ref[...]
        _, r = _rms_fwd(out, g)
        dout, dg = _rms_bwd(dn, out, g, r)
        _accumulate(dg_ref, dg, first)
        dz = jnp.concatenate([dout * sig, dout * z1 * sig * (1.0 - sig)], axis=1)
        _accumulate(db_ref, jnp.sum(dz, axis=0, keepdims=True), first)
        dzb = dz.astype(_BF16)
        dz_ref[...] = dzb
        y = y_ref[...]
        gy_ref[...] = _gelu(y).astype(_BF16)
        dy = _dot(dzb, w_ref[...], _NN) * _gelu_grad(y)
        u = u_ref[...]
        _accumulate(dd_ref, jnp.sum(dy * u, axis=0, keepdims=True), first)
        dud_ref[...] = d_ref[...] * dy
        dy_ref[...] = dy.astype(_BF16)

    row = _whole((1, SSM_WIDTH))
    return _call(body, (L // tm,),
                 [_rows(tm // SCAN_CHUNKS, SCAN_CHUNKS * SSM_WIDTH), _rows(tm, SSM_WIDTH), _rows(tm, 2 * SSM_WIDTH),
                  _rows(tm, SSM_WIDTH), row, _whole((2 * SSM_WIDTH, SSM_WIDTH)), row],
                 [_rows(tm, SSM_WIDTH), _rows(tm, 2 * SSM_WIDTH), _rows(tm, SSM_WIDTH), _rows(tm, SSM_WIDTH),
                  row, _whole((1, 2 * SSM_WIDTH)), row],
                 [_sds((L, SSM_WIDTH), _BF16), _sds((L, 2 * SSM_WIDTH), _BF16), _sds((L, SSM_WIDTH), _BF16),
                  _sds((L, SSM_WIDTH), _F32),
                  _sds((1, SSM_WIDTH), _F32), _sds((1, 2 * SSM_WIDTH), _F32), _sds((1, SSM_WIDTH), _F32)],
                 "ssm_out_bwd", scratch=[pltpu.VMEM((SSM_WIDTH // _LANES, tm, _LANES), _F32)], tokens=tokens)(
        dn, y, z, u, d_row, w_glu, g_ssm)


_SSM_PACK = {"ssm_b_re": (0, SSM_WIDTH, 0, SSM_STATE), "ssm_c_re": (0, SSM_WIDTH, 64, SSM_STATE),
             "ssm_b_im": (512, SSM_WIDTH, 0, SSM_STATE), "ssm_c_im": (512, SSM_WIDTH, 64, SSM_STATE),
             "ssm_lambda_re": (1024, SSM_GROUPS, 0, SSM_STATE), "ssm_lambda_im": (1024, SSM_GROUPS, 64, SSM_STATE),
             "ssm_d": (1056, SSM_GROUPS, 0, SSM_GROUP), "ssm_log_dt": (1088, 1, 0, SSM_GROUPS)}
_PACK_TILE = 16
_SSM_PACK_ROWS = 1088 + _PACK_TILE


def _ssm_param_bwd(da_re, da_im, dbt_re, dbt_im, dct_re, dct_im, lam_re, lam_im, log_dt, b_re, b_im, g_d):
    def body(dar, dai, dbr, dbi, dcr, dci, lr_ref, li_ref, ld_ref, bre_ref, bim_ref, gd_ref, pack_ref):
        lane_in = _iota((SSM_STATE, _LANES), 0)
        lane_out = _iota((SSM_STATE, _LANES), 1)
        low = (lane_out == lane_in).astype(_F32)
        high = (lane_out == lane_in + SSM_STATE).astype(_F32)

        def side_by_side(a, b):
            return _dot_exact(a, low, _NN) + _dot_exact(b, high, _NN)

        tail = _SSM_PACK["ssm_d"][0]
        pack_ref[tail:, :] = jnp.zeros((_SSM_PACK_ROWS - tail, _LANES), _BF16)
        pack_ref[tail:tail + SSM_GROUPS, 0:SSM_GROUP] = gd_ref[...].astype(_BF16)
        own_c = (_iota((SB_WIDTH, SB_STATE), 0) >> 4) == (_iota((SB_WIDTH, SB_STATE), 1) >> 6)

        def unfold(ref):
            blocks = []
            for k in range(SUPER):
                t = jnp.where(own_c, ref[:, _sb_state(k)], 0.0)
                t = sum(t[:, 128 * i:128 * (i + 1)] for i in range(SB_STATE // 128))
                blocks.append((t + pltpu.roll(t, SSM_STATE, 1))[:, :SSM_STATE])
            return jnp.concatenate(blocks, axis=0)

        dbb_re, dbb_im = unfold(dbr), unfold(dbi)
        b_re, b_im = bre_ref[...], bim_ref[...]
        dt_col = _dt_column(ld_ref[...])
        (_, _, fr, fi), vjp = jax.vjp(_s5_discretize, lr_ref[...], li_ref[...], dt_col)
        spread = _rows_of_group()
        fr_t = _dot_exact(spread, fr, _NN)
        fi_t = _dot_exact(spread, fi, _NN)
        pack_ref[0:SSM_WIDTH, :] = side_by_side(fr_t * dbb_re + fi_t * dbb_im, unfold(dcr)).astype(_BF16)
        pack_ref[SSM_WIDTH:2 * SSM_WIDTH, :] = side_by_side(fr_t * dbb_im - fi_t * dbb_re, -unfold(dci)).astype(_BF16)
        d_fr = _dot_exact(spread, dbb_re * b_re + dbb_im * b_im, _TN)
        d_fi = _dot_exact(spread, dbb_im * b_re - dbb_re * b_im, _TN)
        e64, own = _group_masks()

        def from_row(ref):
            return _dot_exact(jnp.where(own, ref[...], 0.0), e64, _NT)

        d_lr, d_li, d_dt = vjp((from_row(dar), from_row(dai), d_fr, d_fi))
        lam_rows = _SSM_PACK["ssm_lambda_re"][0]
        pack_ref[lam_rows:lam_rows + SSM_GROUPS, :] = side_by_side(d_lr, d_li).astype(_BF16)
        eye = (_iota((SSM_GROUPS, SSM_GROUPS), 0) == _iota((SSM_GROUPS, SSM_GROUPS), 1)).astype(_F32)
        dt_row = _SSM_PACK["ssm_log_dt"][0]
        pack_ref[dt_row:dt_row + _PACK_TILE, 0:SSM_GROUPS] = _dot_exact(
            jnp.broadcast_to(d_dt, (SSM_GROUPS, 128)), eye, _TN)[0:_PACK_TILE].astype(_BF16)

    ins = [da_re, da_im, dbt_re, dbt_im, dct_re, dct_im, lam_re, lam_im, log_dt, b_re, b_im, g_d]
    out = (_SSM_PACK_ROWS, _LANES)
    return _call(body, (1,), [_whole(a.shape) for a in ins], _whole(out), _sds(out, _BF16), "ssm_param_bwd")(*ins)


def _head_spread(j):
    r = _iota((KV_WIDTH, 256), 0)
    c = _iota((KV_WIDTH, 256), 1)
    return (r == HEAD_DIM * j + (c & (HEAD_DIM - 1))).astype(_BF16)


STACK = Q_PER_KV * BLOCK


def _stack_heads(t):
    lane_head = _iota((1, 256), 1) >> 6
    return jnp.concatenate([jnp.where(lane_head == g, t, jnp.zeros_like(t)) for g in range(Q_PER_KV)], axis=0)


def _unstack_heads(t):
    lane_head = _iota((1, 256), 1) >> 6
    return sum(jnp.where(lane_head == g, t[BLOCK * g:BLOCK * (g + 1)], 0.0) for g in range(Q_PER_KV))


def _stacked_sinks(sink_ref, j):
    block = _iota((STACK, 1), 0) >> 7
    col = jnp.full((STACK, 1), sink_ref[Q_PER_KV * j], _F32)
    for g in range(1, Q_PER_KV):
        col = jnp.where(block == g, sink_ref[Q_PER_KV * j + g], col)
    return col


def _fold_heads(t, j):
    t = t[:, :KV_WIDTH] + t[:, KV_WIDTH:]
    t = t + pltpu.roll(t, HEAD_DIM, 1)
    return jnp.where((_iota((1, KV_WIDTH), 1) >> 6) == j, t, 0.0)


def _attn_scores(q_stacked, kt, blk, sink):
    s = _dot(q_stacked, kt, _NT) * (HEAD_DIM ** -0.5)
    qi = _iota((STACK, 2 * BLOCK), 0) & (BLOCK - 1)
    kj = _iota((STACK, 2 * BLOCK), 1)
    rel = qi + BLOCK - kj
    valid = (rel >= 0) & (rel < BLOCK) & (blk * BLOCK - BLOCK + kj >= 0)
    s = jnp.where(valid, s, MASK_VALUE)
    m = jnp.maximum(jnp.max(s, axis=-1, keepdims=True), sink)
    p = jnp.exp(s - m)
    e_sink = jnp.exp(sink - m)
    den = jnp.sum(p, axis=-1, keepdims=True) + e_sink
    return p / den, e_sink / den


def _sink_slot():
    return _iota((STACK, 2 * BLOCK), 1) == 0


def _prob_block():
    return pl.BlockSpec((None, N_KV_HEADS, STACK, 2 * BLOCK), lambda i: (i, 0, 0, 0))


def _attn_specs():
    prev = lambda i: (jnp.maximum(i - 1, 0), 0)
    cur = lambda i: (i, 0)
    kv = [pl.BlockSpec((BLOCK, KV_WIDTH), prev), pl.BlockSpec((BLOCK, KV_WIDTH), cur)]
    return [pl.BlockSpec((BLOCK, ATTN_WIDTH), cur)] + kv + kv


def _attn_fwd(q, k, v, sinks, g_attn):
    L = q.shape[0]

    def body(q_ref, kp_ref, kc_ref, vp_ref, vc_ref, sink_ref, g_ref, o_ref, n_ref, p_ref):
        blk = pl.program_id(0)
        kwin = jnp.concatenate([kp_ref[...], kc_ref[...]], axis=0)
        vwin = jnp.concatenate([vp_ref[...], vc_ref[...]], axis=0)
        halves = []
        for j in range(N_KV_HEADS):
            spread = _head_spread(j)
            kt = _dot(kwin, spread, _NN).astype(_BF16)
            vt = _dot(vwin, spread, _NN).astype(_BF16)
            qs = _stack_heads(q_ref[:, 256 * j:256 * (j + 1)])
            p, p_sink = _attn_scores(qs, kt, blk, _stacked_sinks(sink_ref, j))
            p_ref[j] = jnp.where(_sink_slot(), p_sink, p)
            halves.append(_unstack_heads(_dot(p, vt, _NN)))
        o = jnp.concatenate(halves, axis=1)
        o_ref[...] = o
        n, _ = _rms_fwd(o, g_ref[...])
        n_ref[...] = n.astype(_BF16)

    cur = lambda i: (i, 0)
    return _call(body, (L // BLOCK,),
                 _attn_specs() + [pl.BlockSpec(memory_space=pltpu.SMEM), _whole((1, ATTN_WIDTH))],
                 [pl.BlockSpec((BLOCK, ATTN_WIDTH), cur)] * 2 + [_prob_block()],
                 [_sds((L, ATTN_WIDTH), _F32), _sds((L, ATTN_WIDTH), _BF16),
                  _sds((L // BLOCK, N_KV_HEADS, STACK, 2 * BLOCK), _F32)],
                 "attn_fwd")(q, k, k, v, v, sinks, g_attn)


def _attn_bwd(q, k, v, o, dn, probs, g_attn):
    L = q.shape[0]

    def body(q_ref, kp_ref, kc_ref, vp_ref, vc_ref, o_ref, dn_ref, p_ref, g_ref,
             dq_ref, dk_ref, dv_ref, dsink_ref, dg_ref):
        blk = pl.program_id(0)
        first = blk == 0

        @pl.when(first)
        def _():
            dk_ref[...] = jnp.zeros_like(dk_ref)
            dv_ref[...] = jnp.zeros_like(dv_ref)
            dsink_ref[...] = jnp.zeros_like(dsink_ref)

        o = o_ref[...]
        g = g_ref[...]
        _, r = _rms_fwd(o, g)
        do, dg = _rms_bwd(dn_ref[...], o, g, r)
        _accumulate(dg_ref, dg, first)
        kwin = jnp.concatenate([kp_ref[...], kc_ref[...]], axis=0)
        vwin = jnp.concatenate([vp_ref[...], vc_ref[...]], axis=0)
        lane = _iota((1, 128), 1)
        dsink = jnp.zeros((1, 128), _F32)
        dkwin = jnp.zeros((2 * BLOCK, KV_WIDTH), _F32)
        dvwin = jnp.zeros((2 * BLOCK, KV_WIDTH), _F32)
        dq_halves = []
        for j in range(N_KV_HEADS):
            spread = _head_spread(j)
            kt = _dot(kwin, spread, _NN).astype(_BF16)
            vt = _dot(vwin, spread, _NN).astype(_BF16)
            qs = _stack_heads(q_ref[:, 256 * j:256 * (j + 1)])
            dos = _stack_heads(do[:, 256 * j:256 * (j + 1)]).astype(_BF16)
            saved = p_ref[j]
            p_sink = saved[:, 0:1]
            p = jnp.where(_sink_slot(), 0.0, saved)
            dp = _dot(dos, vt, _NT)
            delta = jnp.sum(p * dp, axis=-1, keepdims=True)
            ds = (p * (dp - delta) * (HEAD_DIM ** -0.5)).astype(_BF16)
            sink_term = p_sink * delta
            for g in range(Q_PER_KV):
                head_sum = jnp.sum(sink_term[BLOCK * g:BLOCK * (g + 1)], axis=0, keepdims=True)
                dsink = dsink - jnp.where(lane == Q_PER_KV * j + g, head_sum, 0.0)
            dvwin = dvwin + _fold_heads(_dot(p, dos, _TN), j)
            dkwin = dkwin + _fold_heads(_dot(ds, qs, _TN), j)
            dq_halves.append(_unstack_heads(_dot(ds, kt, _NN)))
        dq_ref[...] = jnp.concatenate(dq_halves, axis=1)
        dsink_ref[...] += dsink
        prev = pl.ds(pl.multiple_of(jnp.maximum(blk - 1, 0) * BLOCK, BLOCK), BLOCK)
        cur = pl.ds(pl.multiple_of(blk * BLOCK, BLOCK), BLOCK)
        dk_ref[prev, :] += dkwin[:BLOCK]
        dk_ref[cur, :] += dkwin[BLOCK:]
        dv_ref[prev, :] += dvwin[:BLOCK]
        dv_ref[cur, :] += dvwin[BLOCK:]

    cur = lambda i: (i, 0)
    blk_q = pl.BlockSpec((BLOCK, ATTN_WIDTH), cur)
    return _call(body, (L // BLOCK,),
                 _attn_specs() + [blk_q, blk_q, _prob_block(), _whole((1, ATTN_WIDTH))],
                 [blk_q, _whole((L, KV_WIDTH)), _whole((L, KV_WIDTH)), _whole((1, 128)), _whole((1, ATTN_WIDTH))],
                 [_sds((L, ATTN_WIDTH), _F32), _sds((L, KV_WIDTH), _F32), _sds((L, KV_WIDTH), _F32),
                  _sds((1, 128), _F32), _sds((1, ATTN_WIDTH), _F32)],
                 "attn_bwd")(q, k, k, v, v, o, dn, probs, g_attn)


def _out_proj(n_ssm, n_attn, x, w_out, g_post_mix, g_pre_ffn):
    L = x.shape[0]
    tm = _chunk_tile(L)

    def body(ns_ref, na_ref, x_ref, w_ref, g1_ref, g2_ref, merged_ref, mo_ref, h1_ref, hn2_ref):
        merged = jnp.concatenate([ns_ref[...], na_ref[...]], axis=1)
        merged_ref[...] = merged
        mo = _dot(merged, w_ref[...], _NN)
        mo_ref[...] = mo
        n, _ = _rms_fwd(mo, g1_ref[...])
        h1 = x_ref[...] + n
        h1_ref[...] = h1
        hn2, _ = _rms_fwd(h1, g2_ref[...])
        hn2_ref[...] = hn2.astype(_BF16)

    row = _whole((1, D_MODEL))
    return _call(body, (L // tm,),
                 [_chunk_block(L, SSM_WIDTH), _rows(tm, ATTN_WIDTH), _rows(tm, D_MODEL), _whole((D_MODEL, D_MODEL)),
                  row, row],
                 [_rows(tm, D_MODEL)] * 4,
                 [_sds((L, D_MODEL), _BF16), _sds((L, D_MODEL), _F32), _sds((L, D_MODEL), _F32), _sds((L, D_MODEL), _BF16)],
                 "out_proj")(n_ssm, n_attn, x, w_out, g_post_mix, g_pre_ffn)


def _ffn(hn2, h1, target, w_gate_up, w_down, g_pre_ffn, g_post_ffn):
    L = h1.shape[0]
    tm = _tile(L)
    half = FFN_CHUNK

    def body(hn2_ref, h1_ref, tgt_ref, wgu_hbm, wd_hbm, g2_ref, g3_ref,
             act_ref, dgu_ref, dff_ref, dh1_ref, loss_ref, dg3_ref, dg2_ref,
             wgu, wd, gu, sem):
        first = pl.program_id(0) == 0

        @pl.when(first)
        def _():
            c1 = pltpu.make_async_copy(wgu_hbm, wgu, sem.at[0])
            c2 = pltpu.make_async_copy(wd_hbm, wd, sem.at[1])
            c1.start()
            c2.start()
            c1.wait()
            c2.wait()

        hn2 = hn2_ref[...]
        ff = jnp.zeros((tm, D_MODEL), _F32)
        for c in range(D_FF // half):
            gate = _dot(hn2, wgu[half * c:half * (c + 1), :], _NT)
            up = _dot(hn2, wgu[D_FF + half * c:D_FF + half * (c + 1), :], _NT)
            gu[:, half * c:half * (c + 1)] = gate
            gu[:, D_FF + half * c:D_FF + half * (c + 1)] = up
            act = gate * jax.nn.sigmoid(gate) * up
            act_ref[half * c:half * (c + 1), :] = act.T.astype(_BF16)
            ff = ff + _dot(act, wd[half * c:half * (c + 1), :], _NN)
        g3 = g3_ref[...]
        n, r = _rms_fwd(ff, g3)
        h1 = h1_ref[...]
        err = h1 + n - tgt_ref[...]
        loss = 0.5 * jnp.sum(jnp.mean(err * err, axis=-1, keepdims=True), axis=0, keepdims=True)
        _accumulate(loss_ref, jnp.broadcast_to(loss, (1, 128)), first)
        dh2 = err * (1.0 / D_MODEL)
        dff, dg3 = _rms_bwd(dh2, ff, g3, r)
        _accumulate(dg3_ref, dg3, first)
        dffb = dff.astype(_BF16)
        dff_ref[...] = dffb
        dhn2 = jnp.zeros((tm, D_MODEL), _F32)
        for c in range(D_FF // half):
            dact = _dot(dffb, wd[half * c:half * (c + 1), :], _NT)
            gate = gu[:, half * c:half * (c + 1)]
            up = gu[:, D_FF + half * c:D_FF + half * (c + 1)]
            sig = jax.nn.sigmoid(gate)
            silu = gate * sig
            dgate = dact * up * (sig + silu * (1.0 - sig))
            dup = dact * silu
            dgu_ref[half * c:half * (c + 1), :] = dgate.T.astype(_BF16)
            dgu_ref[D_FF + half * c:D_FF + half * (c + 1), :] = dup.T.astype(_BF16)
            dhn2 = dhn2 + _dot(dgate, wgu[half * c:half * (c + 1), :], _NN)
            dhn2 = dhn2 + _dot(dup, wgu[D_FF + half * c:D_FF + half * (c + 1), :], _NN)
        g2 = g2_ref[...]
        _, r2 = _rms_fwd(h1, g2)
        dh1, dg2 = _rms_bwd(dhn2, h1, g2, r2)
        _accumulate(dg2_ref, dg2, first)
        dh1_ref[...] = dh2 + dh1

    row = _whole((1, D_MODEL))
    anyspace = pl.BlockSpec(memory_space=pl.ANY)
    return _call(body, (L // tm,),
                 [_rows(tm, D_MODEL), _rows(tm, D_MODEL), _rows(tm, D_MODEL), anyspace, anyspace, row, row],
                 [pl.BlockSpec((D_FF, tm), lambda i: (0, i)), pl.BlockSpec((2 * D_FF, tm), lambda i: (0, i)),
                  _rows(tm, D_MODEL), _rows(tm, D_MODEL), _whole((1, 128)), row, row],
                 [_sds((D_FF, L), _BF16), _sds((2 * D_FF, L), _BF16), _sds((L, D_MODEL), _BF16),
                  _sds((L, D_MODEL), _F32), _sds((1, 128), _F32), _sds((1, D_MODEL), _F32), _sds((1, D_MODEL), _F32)],
                 "ffn",
                 scratch=[pltpu.VMEM((2 * D_FF, D_MODEL), _BF16), pltpu.VMEM((D_FF, D_MODEL), _BF16),
                          pltpu.VMEM((tm, 2 * D_FF), _F32), pltpu.SemaphoreType.DMA((2,))],
                 )(hn2, h1, target, w_gate_up, w_down, g_pre_ffn, g_post_ffn)


def _out_proj_bwd(dh1, mo, w_out, g_post_mix, tokens=()):
    L = dh1.shape[0]
    tm = _chunk_tile(L)

    def body(dh1_ref, mo_ref, w_ref, g_ref, dmo_ref, dns_ref, dna_ref, dg_ref):
        first = pl.program_id(0) == 0
        mo = mo_ref[...]
        g = g_ref[...]
        _, r = _rms_fwd(mo, g)
        dmo, dg = _rms_bwd(dh1_ref[...], mo, g, r)
        _accumulate(dg_ref, dg, first)
        dmob = dmo.astype(_BF16)
        dmo_ref[...] = dmob
        dmerged = _dot(dmob, w_ref[...], _NT)
        dns_ref[...] = dmerged[:, :SSM_WIDTH]
        dna_ref[...] = dmerged[:, SSM_WIDTH:]

    row = _whole((1, D_MODEL))
    return _call(body, (L // tm,),
                 [_rows(tm, D_MODEL), _rows(tm, D_MODEL), _whole((D_MODEL, D_MODEL)), row],
                 [_rows(tm, D_MODEL), _chunk_block(L, SSM_WIDTH), _rows(tm, ATTN_WIDTH), row],
                 [_sds((L, D_MODEL), _BF16), _sds(_chunk_shape(L, SSM_WIDTH), _F32), _sds((L, ATTN_WIDTH), _F32),
                  _sds((1, D_MODEL), _F32)],
                 "out_proj_bwd", tokens=tokens)(dh1, mo, w_out, g_post_mix)


def _in_proj_bwd(du, dq, dk, dv, cos_t, sin_t, x, dh1, g_pre_mix, w_in, tokens=()):
    L = x.shape[0]
    tm = _chunk_tile(L)

    def body(du_ref, dq_ref, dk_ref, dv_ref, cos_ref, sin_ref, x_ref, dh1_ref, g_ref, w_ref,
             dproj_ref, dx_ref, dg_ref):
        first = pl.program_id(0) == 0
        cos_v, sin_v = cos_ref[...], sin_ref[...]
        dproj = jnp.concatenate([du_ref[...], _rope_transpose(dq_ref[...], cos_v, sin_v),
                                 _rope_transpose(dk_ref[...], cos_v, sin_v), dv_ref[...]], axis=1).astype(_BF16)
        dproj_ref[...] = dproj
        dhn = _dot(dproj, w_ref[...], _NN)
        x = x_ref[...]
        g = g_ref[...]
        _, r = _rms_fwd(x, g)
        dx, dg = _rms_bwd(dhn, x, g, r)
        _accumulate(dg_ref, dg, first)
        dx_ref[...] = dh1_ref[...] + dx

    row = _whole((1, D_MODEL))
    return _call(body, (L // tm,),
                 [_chunk_block(L, SSM_WIDTH), _rows(tm, ATTN_WIDTH), _rows(tm, KV_WIDTH), _rows(tm, KV_WIDTH),
                  _rows(tm, KV_WIDTH), _rows(tm, KV_WIDTH), _rows(tm, D_MODEL), _rows(tm, D_MODEL), row,
                  _whole((IN_WIDTH, D_MODEL))],
                 [_rows(tm, IN_WIDTH), _rows(tm, D_MODEL), row],
                 [_sds((L, IN_WIDTH), _BF16), _sds((L, D_MODEL), _F32), _sds((1, D_MODEL), _F32)],
                 "in_proj_bwd", tokens=tokens)(du, dq, dk, dv, cos_t, sin_t, x, dh1, g_pre_mix, w_in)


def _matmul_nn(a, b, out_dtype, name):
    M, K = a.shape
    N = b.shape[1]
    tm = next(t for t in (704, 512, 256, 128) if M % t == 0)
    tn = N if N <= D_MODEL else next(t for t in (512, 256, 128) if N % t == 0)

    def body(a_ref, b_ref, o_ref):
        o_ref[...] = _dot(a_ref[...], b_ref[...], _NN).astype(out_dtype)

    params = pltpu.CompilerParams(dimension_semantics=("arbitrary", "arbitrary"), vmem_limit_bytes=VMEM_LIMIT)
    return pl.pallas_call(body, grid=(M // tm, N // tn),
                          in_specs=[pl.BlockSpec((tm, K), lambda i, j: (i, 0)),
                                    pl.BlockSpec((K, tn), lambda i, j: (0, j))],
                          out_specs=pl.BlockSpec((tm, tn), lambda i, j: (i, j)),
                          out_shape=_sds((M, N), out_dtype), compiler_params=params, name=name)(a, b)


def _matmul_tn(a, b, out_dtype, name, scale=1.0):
    K, M = a.shape
    N = b.shape[1]
    tm = next(t for t in (512, 256, 128) if M % t == 0)
    tn = N if N <= D_MODEL else next(t for t in (512, 256, 128) if N % t == 0)

    def body(a_ref, b_ref, o_ref):
        acc = _dot(a_ref[...], b_ref[...], _TN)
        o_ref[...] = (acc if scale == 1.0 else acc * scale).astype(out_dtype)

    params = pltpu.CompilerParams(dimension_semantics=("arbitrary", "arbitrary"), vmem_limit_bytes=VMEM_LIMIT)
    return pl.pallas_call(body, grid=(M // tm, N // tn),
                          in_specs=[pl.BlockSpec((K, tm), lambda i, j: (0, i)),
                                    pl.BlockSpec((K, tn), lambda i, j: (0, j))],
                          out_specs=pl.BlockSpec((tm, tn), lambda i, j: (i, j)),
                          out_shape=_sds((M, N), out_dtype), compiler_params=params, name=name)(a, b)


def _local_step(x, pos, target, p, fetch, publish, progress):
    L = x.shape[0]
    T = L // SCAN_CHUNKS
    cos_t, sin_t = _rope_tables(pos.reshape(L, 1))
    w_in, = fetch(("w_in",), None)
    hn, u, q, k, v = _in_proj(x, p["g_pre_mix"], w_in, cos_t, sin_t)

    ssm = {n: _to_2d(n, p[n]) for n in ("ssm_lambda_re", "ssm_lambda_im", "ssm_log_dt", "ssm_b_re", "ssm_b_im",
                                        "ssm_c_re", "ssm_c_im")}
    d_row = p["ssm_d"].reshape(1, SSM_WIDTH)
    a_re, a_im, bt_re, bt_im, ct_re, ct_im = _ssm_prep(
        ssm["ssm_lambda_re"], ssm["ssm_lambda_im"], ssm["ssm_log_dt"], ssm["ssm_b_re"], ssm["ssm_b_im"],
        ssm["ssm_c_re"], ssm["ssm_c_im"])

    u_c = u.reshape(L, SSM_WIDTH)
    cx, x_re, x_im = _ssm_core_fwd(u_c, bt_re, bt_im, ct_re, ct_im, a_re, a_im)
    w_glu, = fetch(("w_glu",), cx)
    y, z, n_ssm = _ssm_out(cx, u_c, d_row, w_glu, p["b_glu"], p["g_ssm_out"])

    sinks = p["attn_sinks"].reshape(N_Q_HEADS)
    o, n_attn, probs = _attn_fwd(q, k, v, sinks, p["g_attn_out"])
    w_out, = fetch(("w_out",), n_attn)
    merged, mo, h1, hn2 = _out_proj(n_ssm, n_attn, x, w_out, p["g_post_mix"], p["g_pre_ffn"])
    w_gate_up, w_down = fetch(("w_gate_up", "w_down"), hn2)
    act_t, dgu_t, dff, dh1, loss, dg_post_ffn, dg_pre_ffn = _ffn(
        hn2, h1, target, w_gate_up, w_down, p["g_pre_ffn"], p["g_post_ffn"])
    grads = {"g_post_ffn": dg_post_ffn, "g_pre_ffn": dg_pre_ffn}
    tokens = publish({"w_down": _matmul_nn(act_t, dff, _BF16, "grad_w_down"),
                      "w_gate_up": _matmul_nn(dgu_t, hn2, _BF16, "grad_w_gate_up")})

    dmo, dn_ssm, dn_attn, grads["g_post_mix"] = _out_proj_bwd(dh1, mo, w_out, p["g_post_mix"], tokens)
    grad_w_out = _matmul_tn(merged, dmo, _BF16, "grad_w_out")

    dq, dk, dv, dsink, grads["g_attn_out"] = _attn_bwd(q, k, v, o, dn_attn, probs, p["g_attn_out"])
    grads["attn_sinks"] = dsink

    gy, dz, dy, dud, grads["g_ssm_out"], grads["b_glu"], dd = _ssm_out_bwd(
        dn_ssm, y, z, u_c, d_row, w_glu, p["g_ssm_out"], progress(dmo))
    tokens = publish({"w_out": grad_w_out, "w_glu": _matmul_tn(dz, gy, _BF16, "grad_w_glu")})
    du_c, dct_re, dct_im, dbt_re, dbt_im, da_re, da_im = _ssm_core_bwd(
        u_c, dy, dud, x_re, x_im, bt_re, bt_im, ct_re, ct_im, a_re, a_im, tokens)
    ssm_pack = _ssm_param_bwd(
        da_re, da_im, dbt_re, dbt_im, dct_re, dct_im,
        ssm["ssm_lambda_re"], ssm["ssm_lambda_im"], ssm["ssm_log_dt"], ssm["ssm_b_re"], ssm["ssm_b_im"],
        dd.reshape(SSM_GROUPS, SSM_GROUP))
    grads.update(ssm_pack=ssm_pack, loss=loss)
    publish(grads)

    du = du_c.reshape(_chunk_shape(L, SSM_WIDTH))
    dproj, grad_x, g_pre_mix = _in_proj_bwd(du, dq, dk, dv, cos_t, sin_t, x, dh1, p["g_pre_mix"], w_in, [ssm_pack])
    publish({"g_pre_mix": g_pre_mix, "w_in": _matmul_tn(dproj, hn, _BF16, "grad_w_in")})
    return grad_x


_MESH = pl.DeviceIdType.MESH
_PEERS = N_DEV - 1


def _mesh_pos():
    return lax.axis_index("x"), lax.axis_index("y"), lax.axis_index("c")


def _dev_index(px, py, pc):
    return 4 * px + 2 * py + pc


def _peer(x, y, c, r):
    return (x ^ ((r >> 2) & 1), y ^ ((r >> 1) & 1), c ^ (r & 1))


def _sequencer_exchange(sources, blocked, name, collective_id):
    n = len(sources)
    flags = blocked

    def body(*refs):
        srcs, zones = refs[:n], refs[n:2 * n]
        send_sems, recv_sems, local_sems = refs[2 * n:]
        x, y, c = _mesh_pos()
        me = _dev_index(x, y, c)
        barrier = pltpu.get_barrier_semaphore()
        for r in range(1, N_DEV):
            pl.semaphore_signal(barrier, inc=1, device_id=_peer(x, y, c, r), device_id_type=_MESH)
        pl.semaphore_wait(barrier, _PEERS)
        local, sends, recvs = [], [], []
        for w in range(n):
            cp = pltpu.make_async_copy(srcs[w].at[me] if flags[w] else srcs[w], zones[w].at[me], local_sems.at[w])
            cp.start()
            local.append(cp)
            for r in range(1, N_DEV):
                peer = _peer(x, y, c, r)
                idx = _dev_index(*peer)
                k = _PEERS * w + r - 1
                src = srcs[w].at[idx] if flags[w] else srcs[w]
                send = pltpu.make_async_remote_copy(
                    src_ref=src, dst_ref=zones[w].at[me], send_sem=send_sems.at[k], recv_sem=recv_sems.at[k],
                    device_id=peer, device_id_type=_MESH)
                send.start()
                sends.append(send)
                recvs.append(pltpu.make_async_remote_copy(
                    src_ref=src, dst_ref=zones[w].at[idx], send_sem=send_sems.at[k], recv_sem=recv_sems.at[k],
                    device_id=peer, device_id_type=_MESH))
        for cp in recvs:
            cp.wait_recv()
        for cp in sends:
            cp.wait_send()
        for cp in local:
            cp.wait()

    return pl.kernel(
        body, name=name,
        out_type=[_sds((N_DEV,) + (s.shape[1:] if f else s.shape), s.dtype) for s, f in zip(sources, flags)],
        mesh=plsc.ScalarSubcoreMesh(axis_name="sequencer", num_cores=1),
        scratch_types=[pltpu.SemaphoreType.DMA((_PEERS * n,)), pltpu.SemaphoreType.DMA((_PEERS * n,)),
                       pltpu.SemaphoreType.DMA((n,))],
        compiler_params=pltpu.CompilerParams(collective_id=collective_id),
    )(*sources)


def _sequencer_gather(shards, name, collective_id):
    n = len(shards)
    fan = 4

    def body(*refs):
        srcs, zones = refs[:n], refs[n:2 * n]
        send_sems, recv_sems, local_sems = refs[2 * n:]
        x, y, c = _mesh_pos()
        me, sibling = (x, y, c), (x, y, 1 - c)
        chips = [(1 - x, y), (x, 1 - y), (1 - x, 1 - y)]
        barrier = pltpu.get_barrier_semaphore()
        for peer in [sibling] + [(*chip, c) for chip in chips]:
            pl.semaphore_signal(barrier, inc=1, device_id=peer, device_id_type=_MESH)
        pl.semaphore_wait(barrier, fan)

        def copy(w, k, block, to, src=None):
            slot = zones[w].at[_dev_index(*block)]
            return pltpu.make_async_remote_copy(
                src_ref=slot if src is None else src, dst_ref=slot,
                send_sem=send_sems.at[_PEERS * w + k], recv_sem=recv_sems.at[_PEERS * w + k],
                device_id=to, device_id_type=_MESH)

        mine, first, passed = [], [], []
        for w in range(n):
            cp = pltpu.make_async_copy(srcs[w], zones[w].at[_dev_index(*me)], local_sems.at[w])
            cp.start()
            mine.append(cp)
            sends = [copy(w, 0, me, sibling, src=srcs[w])]
            sends += [copy(w, 1 + j, me, (*chip, c), src=srcs[w]) for j, chip in enumerate(chips)]
            for cp in sends:
                cp.start()
            first += sends
        for w in range(n):
            for j, chip in enumerate(chips):
                copy(w, 1 + j, (*chip, c), me).wait_recv()
                cp = copy(w, fan + j, (*chip, c), sibling)
                cp.start()
                passed.append(cp)
        for w in range(n):
            copy(w, 0, sibling, me).wait_recv()
            for j, chip in enumerate(chips):
                copy(w, fan + j, (*chip, 1 - c), me).wait_recv()
        for cp in first + passed:
            cp.wait_send()
        for cp in mine:
            cp.wait()

    return pl.kernel(
        body, name=name, out_type=[_sds((N_DEV,) + s.shape, s.dtype) for s in shards],
        mesh=plsc.ScalarSubcoreMesh(axis_name="sequencer", num_cores=1),
        scratch_types=[pltpu.SemaphoreType.DMA((_PEERS * n,)), pltpu.SemaphoreType.DMA((_PEERS * n,)),
                       pltpu.SemaphoreType.DMA((n,))],
        compiler_params=pltpu.CompilerParams(collective_id=collective_id),
    )(*shards)


N_CHIPS = N_DEV // 2


def _sequencer_pair_exchange(sources, name, collective_id):
    n = len(sources)

    def body(*refs):
        srcs, zones = refs[:n], refs[n:2 * n]
        send_sems, recv_sems = refs[2 * n:]
        x, y, c = _mesh_pos()
        sibling = (x, y, 1 - c)
        barrier = pltpu.get_barrier_semaphore()
        pl.semaphore_signal(barrier, inc=1, device_id=sibling, device_id_type=_MESH)
        pl.semaphore_wait(barrier, 1)
        copies = []
        for w in range(n):
            for j in range(N_CHIPS):
                k = N_CHIPS * w + j
                cp = pltpu.make_async_remote_copy(
                    src_ref=srcs[w].at[2 * j + 1 - c], dst_ref=zones[w].at[j],
                    send_sem=send_sems.at[k], recv_sem=recv_sems.at[k], device_id=sibling, device_id_type=_MESH)
                cp.start()
                copies.append(cp)
        for cp in copies:
            cp.wait_recv()
        for cp in copies:
            cp.wait_send()

    return pl.kernel(
        body, name=name, out_type=[_sds((N_CHIPS,) + s.shape[1:], s.dtype) for s in sources],
        mesh=plsc.ScalarSubcoreMesh(axis_name="sequencer", num_cores=1),
        scratch_types=[pltpu.SemaphoreType.DMA((N_CHIPS * n,)), pltpu.SemaphoreType.DMA((N_CHIPS * n,))],
        compiler_params=pltpu.CompilerParams(collective_id=collective_id),
    )(*sources)


def _pair_sum(source, received, core, name, tokens=()):
    _, rows, cols = source.shape
    tr = _row_tile(rows)
    n_tok = len(tokens)

    def body(core_ref, s_ref, r_ref, *rest):
        o_ref = rest[n_tok]
        o_ref[...] = (s_ref[...].astype(_F32) + r_ref[...].astype(_F32)).astype(o_ref.dtype)

    quarter = pl.BlockSpec((N_CHIPS, tr, cols), lambda i, core_ref: (0, i, 0))
    mine = pl.BlockSpec((N_CHIPS, None, tr, cols), lambda i, core_ref: (0, core_ref[0], i, 0))
    spec = pltpu.PrefetchScalarGridSpec(
        num_scalar_prefetch=1, grid=(rows // tr,),
        in_specs=[mine, quarter] + [pl.BlockSpec(memory_space=pl.ANY)] * n_tok, out_specs=quarter)
    params = pltpu.CompilerParams(dimension_semantics=("arbitrary",), vmem_limit_bytes=VMEM_LIMIT)
    return pl.pallas_call(body, grid_spec=spec, out_shape=_sds((N_CHIPS, rows, cols), source.dtype),
                          compiler_params=params, name=name)(
        core, source.reshape(N_CHIPS, 2, rows, cols), received, *tokens)


def _sequencer_chip_exchange(partials, name, collective_id):
    n = len(partials)
    others = N_CHIPS - 1

    def body(*refs):
        srcs, zones = refs[:n], refs[n:2 * n]
        send_sems, recv_sems, local_sems = refs[2 * n:]
        x, y, c = _mesh_pos()
        mine = 2 * x + y
        peers = [(x ^ (r >> 1), y ^ (r & 1), c) for r in range(1, N_CHIPS)]
        barrier = pltpu.get_barrier_semaphore()
        for peer in peers:
            pl.semaphore_signal(barrier, inc=1, device_id=peer, device_id_type=_MESH)
        pl.semaphore_wait(barrier, others)
        local, sends, recvs = [], [], []
        for w in range(n):
            cp = pltpu.make_async_copy(srcs[w].at[mine], zones[w].at[mine], local_sems.at[w])
            cp.start()
            local.append(cp)
            for r, peer in enumerate(peers):
                theirs = 2 * peer[0] + peer[1]
                k = others * w + r
                send = pltpu.make_async_remote_copy(
                    src_ref=srcs[w].at[theirs], dst_ref=zones[w].at[mine],
                    send_sem=send_sems.at[k], recv_sem=recv_sems.at[k], device_id=peer, device_id_type=_MESH)
                send.start()
                sends.append(send)
                recvs.append(pltpu.make_async_remote_copy(
                    src_ref=srcs[w].at[theirs], dst_ref=zones[w].at[theirs],
                    send_sem=send_sems.at[k], recv_sem=recv_sems.at[k], device_id=peer, device_id_type=_MESH))
        for cp in recvs:
            cp.wait_recv()
        for cp in sends:
            cp.wait_send()
        for cp in local:
            cp.wait()

    return pl.kernel(
        body, name=name, out_type=[_sds(s.shape, s.dtype) for s in partials],
        mesh=plsc.ScalarSubcoreMesh(axis_name="sequencer", num_cores=1),
        scratch_types=[pltpu.SemaphoreType.DMA((others * n,)), pltpu.SemaphoreType.DMA((others * n,)),
                       pltpu.SemaphoreType.DMA((n,))],
        compiler_params=pltpu.CompilerParams(collective_id=collective_id),
    )(*partials)


def _row_tile(rows):
    return next(t for t in range(min(rows, 256), 0, -16) if rows % t == 0)


def _adam_update(g, w, m, v):
    new_m = ADAM_B1 * m + (1.0 - ADAM_B1) * g
    new_v = ADAM_B2 * v + (1.0 - ADAM_B2) * (g * g)
    m_hat = new_m / (1.0 - ADAM_B1 ** ADAM_STEP)
    v_hat = new_v / (1.0 - ADAM_B2 ** ADAM_STEP)
    return -ADAM_LR * (m_hat / (jnp.sqrt(v_hat) + ADAM_EPS) + ADAM_WD * w), new_m, new_v


def _adamw_small(parts, items, sums, name, tokens=()):
    n_p, n_i = len(parts), len(items)

    def body(*refs):
        p_refs, state, outs = refs[:n_p], refs[n_p:n_p + 3 * n_i], refs[n_p + 3 * n_i:]

        def total(part, rows, cols):
            shift = cols.start % _LANES
            window = slice(cols.start - shift, cols.start - shift + _LANES) if shift else cols
            n_rows = rows.stop - rows.start
            narrow = p_refs[part].dtype.itemsize < 4 and n_rows % _PACK_TILE
            tile = slice(rows.start, rows.start + _PACK_TILE) if narrow else rows
            g = p_refs[part][0, tile, window].astype(_F32)
            for s in range(1, N_DEV):
                g = g + p_refs[part][s, tile, window].astype(_F32)
            g = g[:n_rows] if narrow else g
            return pltpu.roll(g, _LANES - shift, 1)[:, :cols.stop - cols.start] if shift else g

        for i, (part, rows, cols, _, _, _) in enumerate(items):
            g = total(part, rows, cols)
            w_ref, m_ref, v_ref = state[3 * i:3 * i + 3]
            delta, new_m, new_v = _adam_update(g, w_ref[...], m_ref[...], v_ref[...])
            outs[4 * i][...] = g
            outs[4 * i + 1][...] = delta
            outs[4 * i + 2][...] = new_m
            outs[4 * i + 3][...] = new_v
        for j, (part, rows, cols) in enumerate(sums):
            outs[4 * n_i + j][...] = total(part, rows, cols)

    ins = list(parts) + [a for item in items for a in item[3:]]
    out_shapes = [item[3].shape for item in items for _ in range(4)]
    out_shapes += [(rows.stop - rows.start, cols.stop - cols.start) for _, rows, cols in sums]
    out = _call(body, (1,), [_whole(a.shape) for a in ins], [_whole(s) for s in out_shapes],
                [_sds(s, _F32) for s in out_shapes], name, tokens=tokens)(*ins)
    return [out[4 * i:4 * i + 4] for i in range(n_i)], out[4 * n_i:]


def _adamw(parts, w, m, v, name, tokens=(), transposed_parts=False):
    rows, cols = w.shape
    tr = _row_tile(rows)
    n_parts = parts.shape[0]

    def body(p_ref, w_ref, m_ref, v_ref, g_ref, d_ref, nm_ref, nv_ref):
        g = p_ref[0].astype(_F32)
        for s in range(1, n_parts):
            g = g + p_ref[s].astype(_F32)
        if transposed_parts:
            g = g.T
        new_m = ADAM_B1 * m_ref[...] + (1.0 - ADAM_B1) * g
        new_v = ADAM_B2 * v_ref[...] + (1.0 - ADAM_B2) * (g * g)
        m_hat = new_m / (1.0 - ADAM_B1 ** ADAM_STEP)
        v_hat = new_v / (1.0 - ADAM_B2 ** ADAM_STEP)
        g_ref[...] = g
        d_ref[...] = -ADAM_LR * (m_hat / (jnp.sqrt(v_hat) + ADAM_EPS) + ADAM_WD * w_ref[...])
        nm_ref[...] = new_m
        nv_ref[...] = new_v

    blk = _rows(tr, cols)
    part = (pl.BlockSpec((n_parts, cols, tr), lambda i: (0, 0, i)) if transposed_parts
            else pl.BlockSpec((n_parts, tr, cols), lambda i: (0, i, 0)))
    return _call(body, (rows // tr,), [part, blk, blk, blk],
                 [blk] * 4, [_sds((rows, cols), _F32)] * 4, name, tokens=tokens)(parts, w, m, v)


_SMALL = ("g_pre_mix", "ssm_lambda_re", "ssm_lambda_im", "ssm_log_dt", "ssm_b_re", "ssm_b_im",
          "ssm_c_re", "ssm_c_im", "ssm_d", "b_glu", "attn_sinks", "g_ssm_out", "g_attn_out",
          "g_post_mix", "g_pre_ffn", "g_post_ffn")
_BIG = ("w_in", "w_glu", "w_out", "w_gate_up", "w_down")
_WEIGHTS = ("g_pre_mix", "w_in", "ssm_lambda_re", "ssm_lambda_im", "ssm_log_dt", "ssm_b_re", "ssm_b_im",
            "ssm_c_re", "ssm_c_im", "ssm_d", "w_glu", "b_glu", "attn_sinks", "g_ssm_out", "g_attn_out",
            "w_out", "g_post_mix", "g_pre_ffn", "w_gate_up", "w_down", "g_post_ffn")
_LANES = 128


_SHAPE_2D = {
    "g_pre_mix": (1, D_MODEL), "ssm_lambda_re": (SSM_GROUPS, SSM_STATE), "ssm_lambda_im": (SSM_GROUPS, SSM_STATE),
    "ssm_log_dt": (1, SSM_GROUPS), "ssm_b_re": (SSM_WIDTH, SSM_STATE), "ssm_b_im": (SSM_WIDTH, SSM_STATE),
    "ssm_c_re": (SSM_WIDTH, SSM_STATE), "ssm_c_im": (SSM_WIDTH, SSM_STATE), "ssm_d": (SSM_GROUPS, SSM_GROUP),
    "b_glu": (1, 2 * SSM_WIDTH), "attn_sinks": (1, N_Q_HEADS), "g_ssm_out": (1, SSM_WIDTH),
    "g_attn_out": (1, ATTN_WIDTH), "g_post_mix": (1, D_MODEL), "g_pre_ffn": (1, D_MODEL), "g_post_ffn": (1, D_MODEL)}
_ROW_WIDTH = {"g_pre_mix": D_MODEL, "b_glu": 2 * SSM_WIDTH, "attn_sinks": _LANES, "g_ssm_out": SSM_WIDTH,
              "g_attn_out": ATTN_WIDTH, "g_post_mix": D_MODEL, "g_pre_ffn": D_MODEL, "g_post_ffn": D_MODEL,
              "loss": _LANES}
_PER_GROUP_TRANSPOSED = ("ssm_b_re", "ssm_b_im")


def _to_2d(name, a):
    if name in _PER_GROUP_TRANSPOSED:
        a = a.reshape(SSM_GROUPS, SSM_STATE, SSM_GROUP).transpose(0, 2, 1)
    return a.reshape(_SHAPE_2D[name])


def _from_2d(name, a, shape):
    if name in _PER_GROUP_TRANSPOSED:
        a = a.reshape(SSM_GROUPS, SSM_GROUP, SSM_STATE).transpose(0, 2, 1)
    return a.reshape(shape)


def _row_slots(names):
    slots, row, col = {}, 0, 0
    for n in names:
        width = _ROW_WIDTH[n]
        if col + width > D_MODEL:
            row, col = row + 1, 0
        slots[n] = (row, col, width)
        col += width
    return slots


def _stack_rows(named, slots):
    n_rows = -(-(max(r for r, _, _ in slots.values()) + 1) // 8) * 8
    lines = []
    for r in range(n_rows):
        pieces = [named[n] for n, (row, _, _) in slots.items() if row == r]
        used = sum(p.shape[1] for p in pieces)
        if used < D_MODEL:
            pieces.append(jnp.zeros((1, D_MODEL - used), _F32))
        lines.append(jnp.concatenate(pieces, axis=1) if len(pieces) > 1 else pieces[0])
    return jnp.concatenate(lines, axis=0)


def kernel(x, positions, g_pre_mix, w_in, ssm_lambda_re, ssm_lambda_im, ssm_log_dt, ssm_b_re, ssm_b_im, ssm_c_re, ssm_c_im, ssm_d, w_glu, b_glu, attn_sinks, g_ssm_out, g_attn_out, w_out, g_post_mix, g_pre_ffn, w_gate_up, w_down, g_post_ffn, loss_target, m_g_pre_mix, m_w_in, m_ssm_lambda_re, m_ssm_lambda_im, m_ssm_log_dt, m_ssm_b_re, m_ssm_b_im, m_ssm_c_re, m_ssm_c_im, m_ssm_d, m_w_glu, m_b_glu, m_attn_sinks, m_g_ssm_out, m_g_attn_out, m_w_out, m_g_post_mix, m_g_pre_ffn, m_w_gate_up, m_w_down, m_g_post_ffn, v_g_pre_mix, v_w_in, v_ssm_lambda_re, v_ssm_lambda_im, v_ssm_log_dt, v_ssm_b_re, v_ssm_b_im, v_ssm_c_re, v_ssm_c_im, v_ssm_d, v_w_glu, v_b_glu, v_attn_sinks, v_g_ssm_out, v_g_attn_out, v_w_out, v_g_post_mix, v_g_pre_ffn, v_w_gate_up, v_w_down, v_g_post_ffn):
    w = dict(g_pre_mix=g_pre_mix, w_in=w_in, ssm_lambda_re=ssm_lambda_re, ssm_lambda_im=ssm_lambda_im,
             ssm_log_dt=ssm_log_dt, ssm_b_re=ssm_b_re, ssm_b_im=ssm_b_im, ssm_c_re=ssm_c_re, ssm_c_im=ssm_c_im,
             ssm_d=ssm_d, w_glu=w_glu, b_glu=b_glu, attn_sinks=attn_sinks, g_ssm_out=g_ssm_out,
             g_attn_out=g_attn_out, w_out=w_out, g_post_mix=g_post_mix, g_pre_ffn=g_pre_ffn,
             w_gate_up=w_gate_up, w_down=w_down, g_post_ffn=g_post_ffn)
    m = dict(g_pre_mix=m_g_pre_mix, w_in=m_w_in, ssm_lambda_re=m_ssm_lambda_re, ssm_lambda_im=m_ssm_lambda_im,
             ssm_log_dt=m_ssm_log_dt, ssm_b_re=m_ssm_b_re, ssm_b_im=m_ssm_b_im, ssm_c_re=m_ssm_c_re,
             ssm_c_im=m_ssm_c_im, ssm_d=m_ssm_d, w_glu=m_w_glu, b_glu=m_b_glu, attn_sinks=m_attn_sinks,
             g_ssm_out=m_g_ssm_out, g_attn_out=m_g_attn_out, w_out=m_w_out, g_post_mix=m_g_post_mix,
             g_pre_ffn=m_g_pre_ffn, w_gate_up=m_w_gate_up, w_down=m_w_down, g_post_ffn=m_g_post_ffn)
    v = dict(g_pre_mix=v_g_pre_mix, w_in=v_w_in, ssm_lambda_re=v_ssm_lambda_re, ssm_lambda_im=v_ssm_lambda_im,
             ssm_log_dt=v_ssm_log_dt, ssm_b_re=v_ssm_b_re, ssm_b_im=v_ssm_b_im, ssm_c_re=v_ssm_c_re,
             ssm_c_im=v_ssm_c_im, ssm_d=v_ssm_d, w_glu=v_w_glu, b_glu=v_b_glu, attn_sinks=v_attn_sinks,
             g_ssm_out=v_g_ssm_out, g_attn_out=v_g_attn_out, w_out=v_w_out, g_post_mix=v_g_post_mix,
             g_pre_ffn=v_g_pre_ffn, w_gate_up=v_w_gate_up, w_down=v_w_down, g_post_ffn=v_g_post_ffn)

    transposed = ("w_in", "w_glu", "w_gate_up")
    native_transposed = ("w_in", "w_gate_up")
    shard = {n: (w[n][0].T if n in transposed else w[n][0]).astype(_BF16) for n in _BIG}
    gathered = {}
    for cid, names in enumerate((("w_in",), ("w_glu", "w_out"), ("w_gate_up", "w_down")), start=1):
        lands = _sequencer_gather([shard[n] for n in names], "gather_" + names[0], cid)
        gathered.update({n: a.reshape(-1, a.shape[2]) for n, a in zip(names, lands)})

    def fetch(names, after):
        del after
        return [gathered[n] for n in names]

    sent = []
    ids = iter(range(4, 16))
    two_step = {}

    def publish(named):
        big = [n for n in named if n in _BIG]
        if set(big) == {"w_gate_up", "w_down"}:
            blocks = [named[n].reshape(N_DEV, -1, named[n].shape[1]) for n in big]
            two_step.update(names=big, blocks=blocks,
                            received=_sequencer_pair_exchange(blocks, "grads_pair", next(ids)))
            return [named[n] for n in big]
        rows = [n for n in named if n in _ROW_WIDTH]
        plain = [n for n in named if n not in big + rows]
        sources = [named[n].reshape(N_DEV, -1, named[n].shape[1]) for n in big]
        slots = _row_slots(rows)
        if rows:
            sources.append(_stack_rows(named, slots))
        sources += [named[n] for n in plain]
        flags = [True] * len(big) + [False] * (len(sources) - len(big))
        cid = next(ids)
        if big:
            lands = _sequencer_exchange(sources, flags, "grads_%d" % cid, cid)
        else:
            lands = _sequencer_gather(sources, "grads_%d" % cid, cid)
        sent.append((big, slots, plain, lands))
        return [named[n] for n in big]

    def progress(after):
        core = lax.axis_index("c").astype(jnp.int32).reshape(1)
        partials = [_pair_sum(b, r, core, "pair_sum_" + n, [after])
                    for n, b, r in zip(two_step["names"], two_step["blocks"], two_step["received"])]
        sent.append((two_step["names"], {}, [], _sequencer_chip_exchange(partials, "grads_chips", next(ids))))
        return partials

    p = {n: w[n] for n in _SMALL}
    grad_x = _local_step(x[0], positions[0], loss_target[0], p, fetch, publish, progress)

    state = {n: [_to_2d(n, a) for a in (w[n], m[n], v[n])] for n in _SMALL}
    result = {}
    total_loss = None
    chain = []
    for big, slots, plain, lands in sent:
        lands = list(lands)
        after = list(chain)
        for name in big:
            part = lands.pop(0)
            if name in native_transposed:
                updated = _adamw(part, w[name][0].T, m[name][0].T, v[name][0].T, "adamw_" + name, after)
                result[name] = [a.T[None] for a in updated]
                chain.append(updated[3])
                continue
            updated = _adamw(part, w[name][0], m[name][0], v[name][0], "adamw_" + name, after,
                             transposed_parts=name in transposed)
            result[name] = [a[None] for a in updated]
            chain.append(updated[3])
        parts, items, sums, names = [], [], [], []
        if slots:
            parts.append(lands.pop(0))
            for name, (row, col, _) in slots.items():
                if name == "loss":
                    sums.append((0, slice(row, row + 1), slice(col, col + _LANES)))
                else:
                    items.append((0, slice(row, row + 1), slice(col, col + _SHAPE_2D[name][1]), *state[name]))
                    names.append(name)
        for name in plain:
            packed = _SSM_PACK if name == "ssm_pack" else {name: (0, _SHAPE_2D[name][0], 0, _SHAPE_2D[name][1])}
            for member, (first, rows_n, lane, cols_n) in packed.items():
                items.append((len(parts), slice(first, first + rows_n), slice(lane, lane + cols_n), *state[member]))
                names.append(member)
            parts.append(lands.pop(0))
        if items:
            updated, summed = _adamw_small(parts, items, sums, "adamw_small_" + names[0], after)
            chain.append(updated[0][3])
            result.update(dict(zip(names, updated)))
            if summed:
                total_loss = summed[0][0, 0]

    out = [total_loss, grad_x[None]]
    for kind in range(4):
        out += [_from_2d(n, result[n][kind], w[n].shape) for n in _WEIGHTS]
    return tuple(out)
```

```python
import math

import numpy as np
import jax
import jax.numpy as jnp
from jax import lax
from jax.experimental import pallas as pl
from jax.experimental.pallas import tpu as pltpu
from jax.experimental.pallas import tpu_sc as plsc

D_MODEL = 1024
SSM_WIDTH = 512
SSM_GROUP = 16
SSM_GROUPS = 32
SSM_STATE = 64
N_STATE = SSM_GROUPS * SSM_STATE
ATTN_WIDTH = 512
HEAD_DIM = 64
N_Q_HEADS = 8
N_KV_HEADS = 2
Q_PER_KV = 4
KV_WIDTH = 128
IN_WIDTH = 1280
BLOCK = 128
ROPE_DIM = 16
ROPE_THETA = 500000.0
D_FF = 2816
NORM_EPS = 1e-6
MASK_VALUE = -1e30
ADAM_LR = 0.001
ADAM_B1 = 0.9
ADAM_B2 = 0.999
ADAM_EPS = 1e-08
ADAM_WD = 0.01
ADAM_STEP = 10

N_DEV = 8
SCAN_CHUNKS = 8
SCAN_UNROLL = 16
FFN_CHUNK = 2816
TOKEN_TILE = 256
VMEM_LIMIT = 56 * 1024 * 1024

_F32 = jnp.float32
_BF16 = jnp.bfloat16
_MXU = jnp.bfloat16

_NN = ((1,), (0,))
_NT = ((1,), (1,))
_TN = ((0,), (0,))


def _dot(a, b, dims):
    return lax.dot_general(a.astype(_MXU), b.astype(_MXU), (dims, ((), ())),
                           preferred_element_type=_F32)


def _dot_exact(a, b, dims):
    return lax.dot_general(a.astype(_F32), b.astype(_F32), (dims, ((), ())),
                           precision=lax.Precision.HIGHEST, preferred_element_type=_F32)


def _iota(shape, dim):
    return lax.broadcasted_iota(jnp.int32, shape, dim)


def _rms_fwd(x, g):
    r = lax.rsqrt(jnp.mean(x * x, axis=-1, keepdims=True) + NORM_EPS)
    return x * r * g, r


def _rms_bwd(dy, x, g, r):
    a = dy * g
    xn = x * r
    dx = r * (a - xn * jnp.mean(a * xn, axis=-1, keepdims=True))
    dg = jnp.sum(dy * xn, axis=0, keepdims=True)
    return dx, dg


def _call(body, grid, in_specs, out_specs, out_shape, name, scratch=(), tokens=()):
    params = pltpu.CompilerParams(dimension_semantics=("arbitrary",) * len(grid),
                                  vmem_limit_bytes=VMEM_LIMIT)
    n_in, n_tok = len(in_specs), len(tokens)

    def run(*refs):
        return body(*refs[:n_in], *refs[n_in + n_tok:])

    call = pl.pallas_call(run, grid=grid,
                          in_specs=list(in_specs) + [pl.BlockSpec(memory_space=pl.ANY)] * n_tok,
                          out_specs=out_specs, out_shape=out_shape, scratch_shapes=list(scratch),
                          compiler_params=params, name=name)
    return lambda *args: call(*args, *tokens)


def _rows(tm, n):
    return pl.BlockSpec((tm, n), lambda i: (i, 0))


def _whole(shape):
    nd = len(shape)
    return pl.BlockSpec(shape, lambda i: (0,) * nd)


def _sds(shape, dtype):
    return jax.ShapeDtypeStruct(shape, dtype)


def _tile(L):
    return min(TOKEN_TILE, L)


def _chunk_tile(L):
    return L // SCAN_CHUNKS


def _chunk_block(L, n):
    return pl.BlockSpec((_chunk_tile(L), n), lambda i: (0, i))


def _chunk_shape(L, n):
    return (_chunk_tile(L), SCAN_CHUNKS * n)


def _accumulate(ref, val, first):
    @pl.when(first)
    def _():
        ref[...] = val

    @pl.when(jnp.logical_not(first))
    def _():
        ref[...] += val


def _rope_rows():
    half = ROPE_DIM // 2
    inv = (np.float32(ROPE_THETA) ** (-np.arange(half, dtype=np.float32) * np.float32(2.0) / np.float32(ROPE_DIM))).astype(np.float32)
    col = np.arange(KV_WIDTH) % HEAD_DIM
    freq = np.where(col < ROPE_DIM, inv[col % half], 0.0).astype(np.float32)
    sign = np.where(col < half, -1.0, np.where(col < ROPE_DIM, 1.0, 0.0)).astype(np.float32)
    return freq[None, :], sign[None, :]


def _rope_tables(pos_col):
    L = pos_col.shape[0]
    tm = _tile(L)
    freq, sign = _rope_rows()

    def body(pos_ref, freq_ref, sign_ref, cos_ref, sin_ref):
        ang = pos_ref[...].astype(_F32) * freq_ref[...]
        cos_ref[...] = jnp.cos(ang)
        sin_ref[...] = jnp.sin(ang) * sign_ref[...]

    return _call(body, (L // tm,),
                 [_rows(tm, 1), _whole((1, KV_WIDTH)), _whole((1, KV_WIDTH))],
                 [_rows(tm, KV_WIDTH), _rows(tm, KV_WIDTH)],
                 [_sds((L, KV_WIDTH), _F32)] * 2, "rope_tables")(pos_col, jnp.asarray(freq), jnp.asarray(sign))


def _widen(t, width):
    return t if width == KV_WIDTH else jnp.concatenate([t] * (width // KV_WIDTH), axis=1)


def _rope_partner(t):
    w = t.shape[1]
    in_head = _iota((1, w), 1) & (HEAD_DIM - 1)
    second = jnp.where(in_head < ROPE_DIM, pltpu.roll(t, ROPE_DIM // 2, 1), 0.0)
    return jnp.where(in_head < ROPE_DIM // 2, pltpu.roll(t, w - ROPE_DIM // 2, 1), second)


def _rope_apply(t, cos_t, sin_t):
    w = t.shape[1]
    return t * _widen(cos_t, w) + _rope_partner(t) * _widen(sin_t, w)


def _rope_transpose(dt, cos_t, sin_t):
    w = dt.shape[1]
    return dt * _widen(cos_t, w) + _rope_partner(dt * _widen(sin_t, w))


def _in_proj(x, g_pre_mix, w_in, cos_t, sin_t):
    L = x.shape[0]
    tm = _chunk_tile(L)

    def body(x_ref, g_ref, w_ref, cos_ref, sin_ref, hn_ref, u_ref, q_ref, k_ref, v_ref):
        hn, _ = _rms_fwd(x_ref[...], g_ref[...])
        hn = hn.astype(_BF16)
        hn_ref[...] = hn
        proj = _dot(hn, w_ref[...], _NT)
        u_ref[...] = proj[:, :SSM_WIDTH]
        q = proj[:, SSM_WIDTH:SSM_WIDTH + ATTN_WIDTH]
        k = proj[:, SSM_WIDTH + ATTN_WIDTH:SSM_WIDTH + ATTN_WIDTH + KV_WIDTH]
        cos_v, sin_v = cos_ref[...], sin_ref[...]
        q_ref[...] = _rope_apply(q, cos_v, sin_v).astype(_BF16)
        k_ref[...] = _rope_apply(k, cos_v, sin_v).astype(_BF16)
        v_ref[...] = proj[:, SSM_WIDTH + ATTN_WIDTH + KV_WIDTH:].astype(_BF16)

    return _call(body, (L // tm,),
                 [_rows(tm, D_MODEL), _whole((1, D_MODEL)), _whole((IN_WIDTH, D_MODEL)),
                  _rows(tm, KV_WIDTH), _rows(tm, KV_WIDTH)],
                 [_rows(tm, D_MODEL), _chunk_block(L, SSM_WIDTH), _rows(tm, ATTN_WIDTH),
                  _rows(tm, KV_WIDTH), _rows(tm, KV_WIDTH)],
                 [_sds((L, D_MODEL), _BF16), _sds(_chunk_shape(L, SSM_WIDTH), _F32), _sds((L, ATTN_WIDTH), _BF16),
                  _sds((L, KV_WIDTH), _BF16), _sds((L, KV_WIDTH), _BF16)],
                 "in_proj")(x, g_pre_mix, w_in, cos_t, sin_t)


def _s5_discretize(lam_re, lam_im, log_dt):
    lr = jnp.minimum(lam_re, -1e-4)
    li = lam_im
    dt = jnp.exp(log_dt)
    mag = jnp.exp(lr * dt)
    ar = mag * jnp.cos(li * dt)
    ai = mag * jnp.sin(li * dt)
    den = lr * lr + li * li
    fr = ((ar - 1.0) * lr + ai * li) / den
    fi = (ai * lr - (ar - 1.0) * li) / den
    return ar, ai, fr, fi


SUPER = 4
SB_STATE = N_STATE // SUPER
SB_WIDTH = SSM_WIDTH // SUPER


def _sb_state(k):
    return slice(SB_STATE * k, SB_STATE * (k + 1))


def _sb_width(k):
    return slice(SB_WIDTH * k, SB_WIDTH * (k + 1))


def _dt_column(log_dt_row):
    eye = _iota((SSM_GROUPS, SSM_GROUPS), 0) == _iota((SSM_GROUPS, SSM_GROUPS), 1)
    return jnp.sum(jnp.where(eye, log_dt_row, 0.0), axis=1, keepdims=True)


def _group_masks():
    e64 = ((_iota((SSM_STATE, N_STATE), 1) & (SSM_STATE - 1)) == _iota((SSM_STATE, N_STATE), 0)).astype(_F32)
    own = _iota((SSM_GROUPS, N_STATE), 0) == (_iota((SSM_GROUPS, N_STATE), 1) >> 6)
    return e64, own


def _rows_of_group():
    return ((_iota((SSM_WIDTH, SSM_GROUPS), 0) >> 4) == _iota((SSM_WIDTH, SSM_GROUPS), 1)).astype(_F32)


def _ssm_prep(lam_re, lam_im, log_dt, b_re, b_im, c_re, c_im):
    def body(lr_ref, li_ref, ld_ref, bre, bim, cre, cim, ar_ref, ai_ref, btr, bti, ctr, cti):
        ar, ai, fr, fi = _s5_discretize(lr_ref[...], li_ref[...], _dt_column(ld_ref[...]))
        e64, own = _group_masks()
        mask_c = (_iota((SSM_WIDTH, N_STATE), 0) >> 4) == (_iota((SSM_WIDTH, N_STATE), 1) >> 6)

        def to_row(t):
            return jnp.sum(jnp.where(own, _dot_exact(t, e64, _NN), 0.0), axis=0, keepdims=True)

        def fold(m):
            full = jnp.where(mask_c, _dot(m, e64, _NN), 0.0)
            return sum(full[_sb_width(k), :] for k in range(SUPER)).astype(_BF16)

        ar_ref[...] = to_row(ar)
        ai_ref[...] = to_row(ai)
        spread = _rows_of_group()
        fr_t = _dot_exact(spread, fr, _NN)
        fi_t = _dot_exact(spread, fi, _NN)
        btr[...] = fold(fr_t * bre[...] - fi_t * bim[...])
        bti[...] = fold(fr_t * bim[...] + fi_t * bre[...])
        ctr[...] = fold(cre[...])
        cti[...] = fold(cim[...])

    row = (1, N_STATE)
    ins = [lam_re, lam_im, log_dt, b_re, b_im, c_re, c_im]
    return _call(body, (1,), [_whole(a.shape) for a in ins],
                 [_whole(row), _whole(row)] + [_whole((SB_WIDTH, N_STATE))] * 4,
                 [_sds(row, _F32), _sds(row, _F32)] + [_sds((SB_WIDTH, N_STATE), _BF16)] * 4,
                 "ssm_prep")(*ins)


def _complex_power(ar, ai, n):
    pr, pi = jnp.ones_like(ar), jnp.zeros_like(ai)
    while n:
        if n & 1:
            pr, pi = pr * ar - pi * ai, pr * ai + pi * ar
        ar, ai = ar * ar - ai * ai, 2.0 * ar * ai
        n >>= 1
    return pr, pi


def _chunk_carries(er, ei, pr, pi, reverse):
    rows = _iota(er.shape, 0)
    sr = jnp.zeros_like(pr)
    si = jnp.zeros_like(pi)
    out_r = jnp.zeros_like(er)
    out_i = jnp.zeros_like(ei)
    order = range(SCAN_CHUNKS - 1, 0, -1) if reverse else range(SCAN_CHUNKS - 1)
    for c in order:
        e_r = er[c:c + 1, :]
        e_i = ei[c:c + 1, :]
        sr, si = pr * sr - pi * si + e_r, pr * si + pi * sr + e_i
        nxt = c - 1 if reverse else c + 1
        out_r = jnp.where(rows == nxt, sr, out_r)
        out_i = jnp.where(rows == nxt, si, out_i)
    return out_r, out_i


_GELU_K = math.sqrt(2.0 / math.pi)
_GELU_C = 0.044715


def _gelu(y):
    return 0.5 * y * (1.0 + jnp.tanh(_GELU_K * (y + _GELU_C * y * y * y)))


def _gelu_grad(y):
    t = jnp.tanh(_GELU_K * (y + _GELU_C * y * y * y))
    return 0.5 * (1.0 + t) + 0.5 * y * (1.0 - t * t) * _GELU_K * (1.0 + 3.0 * _GELU_C * y * y)


def _step_rows(t):
    return pl.ds(pl.multiple_of(t * SCAN_CHUNKS, SCAN_CHUNKS), SCAN_CHUNKS)


def _scan_in_place(br, bi, ar, ai, T):
    W = br.shape[1]
    ar8 = jnp.broadcast_to(ar, (SCAN_CHUNKS, W))
    ai8 = jnp.broadcast_to(ai, (SCAN_CHUNKS, W))

    def local(t, c):
        cr, ci = c
        rows = _step_rows(t)
        return ar8 * cr - ai8 * ci + br[rows, :], ar8 * ci + ai8 * cr + bi[rows, :]

    zero = jnp.zeros((SCAN_CHUNKS, W), _F32)
    er, ei = lax.fori_loop(0, T, local, (zero, zero), unroll=SCAN_UNROLL)
    pr, pi = _complex_power(ar, ai, T)
    carries = _chunk_carries(er, ei, pr, pi, reverse=False)

    def final(t, c):
        nr, ni = local(t, c)
        rows = _step_rows(t)
        br[rows, :] = nr
        bi[rows, :] = ni
        return nr, ni

    lax.fori_loop(0, T, final, carries, unroll=SCAN_UNROLL)


def _scan_reverse_in_place(dr, di, xr, xi, ar, ai, T):
    W = dr.shape[1]
    ar8 = jnp.broadcast_to(ar, (SCAN_CHUNKS, W))
    ai8 = jnp.broadcast_to(ai, (SCAN_CHUNKS, W))

    def local(t, c):
        cr, ci = c
        rows = _step_rows(t)
        return ar8 * cr + ai8 * ci + dr[rows, :], ar8 * ci - ai8 * cr + di[rows, :]

    zero = jnp.zeros((SCAN_CHUNKS, W), _F32)
    er, ei = lax.fori_loop(0, T, lambda k, c: local(T - 1 - k, c), (zero, zero), unroll=SCAN_UNROLL)
    pr, pi = _complex_power(ar, -ai, T)
    sr, si = _chunk_carries(er, ei, pr, pi, reverse=True)

    def grad_a(acc, nr, ni, xpr, xpi):
        return acc[0] + nr * xpr + ni * xpi, acc[1] + ni * xpr - nr * xpi

    def final(k, c):
        t = T - 1 - k
        nr, ni = local(t, c[:2])
        rows = _step_rows(t)
        dr[rows, :] = nr
        di[rows, :] = ni
        before = _step_rows(t - 1)
        gr, gi = grad_a(c[2:], nr, ni, xr[before, :], xi[before, :])
        return nr, ni, gr, gi

    cr, ci, gr, gi = lax.fori_loop(0, T - 1, final, (sr, si, zero, zero), unroll=SCAN_UNROLL)
    nr, ni = local(0, (cr, ci))
    dr[_step_rows(0), :] = nr
    di[_step_rows(0), :] = ni
    first = _iota((SCAN_CHUNKS, W), 0) == 0
    last = _step_rows(T - 1)
    xpr = jnp.where(first, 0.0, pltpu.roll(xr[last, :], 1, 0))
    xpi = jnp.where(first, 0.0, pltpu.roll(xi[last, :], 1, 0))
    gr, gi = grad_a((gr, gi), nr, ni, xpr, xpi)
    return jnp.sum(gr, axis=0, keepdims=True), jnp.sum(gi, axis=0, keepdims=True)


def _ssm_super_specs(L):
    width = pl.BlockSpec((L, SB_WIDTH), lambda k: (0, k))
    matrix = pl.BlockSpec((SB_WIDTH, SB_STATE), lambda k: (0, k))
    row = pl.BlockSpec((1, SB_STATE), lambda k: (0, k))
    return width, matrix, row


def _ssm_core_fwd(u, bt_re, bt_im, ct_re, ct_im, a_re, a_im):
    L = u.shape[0]
    T = L // SCAN_CHUNKS

    def body(u_ref, br_ref, bi_ref, cr_ref, ci_ref, ar_ref, ai_ref, y_ref, xr, xi):
        ub = u_ref[...].astype(_BF16)
        xr[...] = _dot(ub, br_ref[...], _NN)
        xi[...] = _dot(ub, bi_ref[...], _NN)
        _scan_in_place(xr, xi, ar_ref[...], ai_ref[...], T)
        y_ref[...] = _dot(xr[...], cr_ref[...], _NT) - _dot(xi[...], ci_ref[...], _NT)

    width, matrix, row = _ssm_super_specs(L)
    state = pl.BlockSpec((L, SB_STATE), lambda k: (0, k))
    return _call(body, (SUPER,), [width, matrix, matrix, matrix, matrix, row, row], [width, state, state],
                 [_sds((L, SSM_WIDTH), _F32)] + [_sds((L, N_STATE), _F32)] * 2,
                 "ssm_core_fwd")(u, bt_re, bt_im, ct_re, ct_im, a_re, a_im)


def _ssm_core_bwd(u, dy, dud, x_re, x_im, bt_re, bt_im, ct_re, ct_im, a_re, a_im, tokens=()):
    L = u.shape[0]
    T = L // SCAN_CHUNKS

    def body(u_ref, dy_ref, dud_ref, xr, xi, br_ref, bi_ref, cr_ref, ci_ref, ar_ref, ai_ref,
             du_ref, dcr_ref, dci_ref, dbr_ref, dbi_ref, dar_ref, dai_ref, lr, li):
        dyb = dy_ref[...]
        lr[...] = _dot(dyb, cr_ref[...], _NN)
        li[...] = -_dot(dyb, ci_ref[...], _NN)
        da_re, da_im = _scan_reverse_in_place(lr, li, xr, xi, ar_ref[...], ai_ref[...], T)
        dar_ref[...] = da_re
        dai_ref[...] = da_im
        du_ref[...] = _dot(lr[...], br_ref[...], _NT) + _dot(li[...], bi_ref[...], _NT) + dud_ref[...]
        ub = u_ref[...].astype(_BF16)
        dcr_ref[...] = _dot(dyb, xr[...], _TN)
        dci_ref[...] = _dot(dyb, xi[...], _TN)
        dbr_ref[...] = _dot(ub, lr[...], _TN)
        dbi_ref[...] = _dot(ub, li[...], _TN)

    width, matrix, row = _ssm_super_specs(L)
    state = pl.BlockSpec((L, SB_STATE), lambda k: (0, k))
    return _call(body, (SUPER,), [width, width, width, state, state, matrix, matrix, matrix, matrix, row, row],
                 [width] + [matrix] * 4 + [row] * 2,
                 [_sds((L, SSM_WIDTH), _F32)] + [_sds((SB_WIDTH, N_STATE), _F32)] * 4 + [_sds((1, N_STATE), _F32)] * 2,
                 "ssm_core_bwd", scratch=[pltpu.VMEM((L, SB_STATE), _F32)] * 2,
                 tokens=tokens)(u, dy, dud, x_re, x_im, bt_re, bt_im, ct_re, ct_im, a_re, a_im)


def _ssm_out(cx, u, d_row, w_glu, b_glu, g_ssm):
    L = u.shape[0]
    tm = _tile(L)

    def body(cx_ref, u_ref, d_ref, w_ref, b_ref, g_ref, y_ref, z_ref, n_ref, stage):
        y = cx_ref[...] + d_ref[...] * u_ref[...]
        y_ref[...] = y
        z = _dot(_gelu(y), w_ref[...], _NT) + b_ref[...]
        z_ref[...] = z
        out = z[:, :SSM_WIDTH] * jax.nn.sigmoid(z[:, SSM_WIDTH:])
        n, _ = _rms_fwd(out, g_ref[...])
        for k in range(SSM_WIDTH // _LANES):
            stage[k] = n[:, _LANES * k:_LANES * (k + 1)]
            for c in range(SCAN_CHUNKS):
                rows = stage[k, pl.ds(c, tm // SCAN_CHUNKS, stride=SCAN_CHUNKS), :]
                lane = SSM_WIDTH * c + _LANES * k
                n_ref[:, lane:lane + _LANES] = rows.astype(_BF16)

    return _call(body, (L // tm,),
                 [_rows(tm, SSM_WIDTH), _rows(tm, SSM_WIDTH), _whole((1, SSM_WIDTH)),
                  _whole((2 * SSM_WIDTH, SSM_WIDTH)), _whole((1, 2 * SSM_WIDTH)), _whole((1, SSM_WIDTH))],
                 [_rows(tm, SSM_WIDTH), _rows(tm, 2 * SSM_WIDTH), _rows(tm // SCAN_CHUNKS, SCAN_CHUNKS * SSM_WIDTH)],
                 [_sds((L, SSM_WIDTH), _F32), _sds((L, 2 * SSM_WIDTH), _F32), _sds(_chunk_shape(L, SSM_WIDTH), _BF16)],
                 "ssm_out", scratch=[pltpu.VMEM((SSM_WIDTH // _LANES, tm, _LANES), _F32)])(
        cx, u, d_row, w_glu, b_glu, g_ssm)


def _ssm_out_bwd(dn, y, z, u, d_row, w_glu, g_ssm, tokens=()):
    L = u.shape[0]
    tm = _tile(L)

    def body(dn_ref, y_ref, z_ref, u_ref, d_ref, w_ref, g_ref,
             gy_ref, dz_ref, dy_ref, dud_ref, dg_ref, db_ref, dd_ref, stage):
        first = pl.program_id(0) == 0
        for k in range(SSM_WIDTH // _LANES):
            for c in range(SCAN_CHUNKS):
                lane = SSM_WIDTH * c + _LANES * k
                stage[k, pl.ds(c, tm // SCAN_CHUNKS, stride=SCAN_CHUNKS), :] = dn_ref[:, lane:lane + _LANES]
        dn = jnp.concatenate([stage[k] for k in range(SSM_WIDTH // _LANES)], axis=1)
        z = z_ref[...]
        z1, z2 = z[:, :SSM_WIDTH], z[:, SSM_WIDTH:]
        sig = jax.nn.sigmoid(z2)
        out = z1 * sig
        g = g_ref[...]
        _, r = _rms_fwd(out, g)
        dout, dg = _rms_bwd(dn, out, g, r)
        _accumulate(dg_ref, dg, first)
        dz = jnp.concatenate([dout * sig, dout * z1 * sig * (1.0 - sig)], axis=1)
        _accumulate(db_ref, jnp.sum(dz, axis=0, keepdims=True), first)
        dzb = dz.astype(_BF16)
        dz_ref[...] = dzb
        y = y_ref[...]
        gy_ref[...] = _gelu(y).astype(_BF16)
        dy = _dot(dzb, w_ref[...], _NN) * _gelu_grad(y)
        u = u_ref[...]
        _accumulate(dd_ref, jnp.sum(dy * u, axis=0, keepdims=True), first)
        dud_ref[...] = d_ref[...] * dy
        dy_ref[...] = dy.astype(_BF16)

    row = _whole((1, SSM_WIDTH))
    return _call(body, (L // tm,),
                 [_rows(tm // SCAN_CHUNKS, SCAN_CHUNKS * SSM_WIDTH), _rows(tm, SSM_WIDTH), _rows(tm, 2 * SSM_WIDTH),
                  _rows(tm, SSM_WIDTH), row, _whole((2 * SSM_WIDTH, SSM_WIDTH)), row],
                 [_rows(tm, SSM_WIDTH), _rows(tm, 2 * SSM_WIDTH), _rows(tm, SSM_WIDTH), _rows(tm, SSM_WIDTH),
                  row, _whole((1, 2 * SSM_WIDTH)), row],
                 [_sds((L, SSM_WIDTH), _BF16), _sds((L, 2 * SSM_WIDTH), _BF16), _sds((L, SSM_WIDTH), _BF16),
                  _sds((L, SSM_WIDTH), _F32),
                  _sds((1, SSM_WIDTH), _F32), _sds((1, 2 * SSM_WIDTH), _F32), _sds((1, SSM_WIDTH), _F32)],
                 "ssm_out_bwd", scratch=[pltpu.VMEM((SSM_WIDTH // _LANES, tm, _LANES), _F32)], tokens=tokens)(
        dn, y, z, u, d_row, w_glu, g_ssm)


_SSM_PACK = {"ssm_b_re": (0, SSM_WIDTH, 0, SSM_STATE), "ssm_c_re": (0, SSM_WIDTH, 64, SSM_STATE),
             "ssm_b_im": (512, SSM_WIDTH, 0, SSM_STATE), "ssm_c_im": (512, SSM_WIDTH, 64, SSM_STATE),
             "ssm_lambda_re": (1024, SSM_GROUPS, 0, SSM_STATE), "ssm_lambda_im": (1024, SSM_GROUPS, 64, SSM_STATE),
             "ssm_d": (1056, SSM_GROUPS, 0, SSM_GROUP), "ssm_log_dt": (1088, 1, 0, SSM_GROUPS)}
_PACK_TILE = 16
_SSM_PACK_ROWS = 1088 + _PACK_TILE


def _ssm_param_bwd(da_re, da_im, dbt_re, dbt_im, dct_re, dct_im, lam_re, lam_im, log_dt, b_re, b_im, g_d):
    def body(dar, dai, dbr, dbi, dcr, dci, lr_ref, li_ref, ld_ref, bre_ref, bim_ref, gd_ref, pack_ref):
        lane_in = _iota((SSM_STATE, _LANES), 0)
        lane_out = _iota((SSM_STATE, _LANES), 1)
        low = (lane_out == lane_in).astype(_F32)
        high = (lane_out == lane_in + SSM_STATE).astype(_F32)

        def side_by_side(a, b):
            return _dot_exact(a, low, _NN) + _dot_exact(b, high, _NN)

        tail = _SSM_PACK["ssm_d"][0]
        pack_ref[tail:, :] = jnp.zeros((_SSM_PACK_ROWS - tail, _LANES), _BF16)
        pack_ref[tail:tail + SSM_GROUPS, 0:SSM_GROUP] = gd_ref[...].astype(_BF16)
        own_c = (_iota((SB_WIDTH, SB_STATE), 0) >> 4) == (_iota((SB_WIDTH, SB_STATE), 1) >> 6)

        def unfold(ref):
            blocks = []
            for k in range(SUPER):
                t = jnp.where(own_c, ref[:, _sb_state(k)], 0.0)
                t = sum(t[:, 128 * i:128 * (i + 1)] for i in range(SB_STATE // 128))
                blocks.append((t + pltpu.roll(t, SSM_STATE, 1))[:, :SSM_STATE])
            return jnp.concatenate(blocks, axis=0)

        dbb_re, dbb_im = unfold(dbr), unfold(dbi)
        b_re, b_im = bre_ref[...], bim_ref[...]
        dt_col = _dt_column(ld_ref[...])
        (_, _, fr, fi), vjp = jax.vjp(_s5_discretize, lr_ref[...], li_ref[...], dt_col)
        spread = _rows_of_group()
        fr_t = _dot_exact(spread, fr, _NN)
        fi_t = _dot_exact(spread, fi, _NN)
        pack_ref[0:SSM_WIDTH, :] = side_by_side(fr_t * dbb_re + fi_t * dbb_im, unfold(dcr)).astype(_BF16)
        pack_ref[SSM_WIDTH:2 * SSM_WIDTH, :] = side_by_side(fr_t * dbb_im - fi_t * dbb_re, -unfold(dci)).astype(_BF16)
        d_fr = _dot_exact(spread, dbb_re * b_re + dbb_im * b_im, _TN)
        d_fi = _dot_exact(spread, dbb_im * b_re - dbb_re * b_im, _TN)
        e64, own = _group_masks()

        def from_row(ref):
            return _dot_exact(jnp.where(own, ref[...], 0.0), e64, _NT)

        d_lr, d_li, d_dt = vjp((from_row(dar), from_row(dai), d_fr, d_fi))
        lam_rows = _SSM_PACK["ssm_lambda_re"][0]
        pack_ref[lam_rows:lam_rows + SSM_GROUPS, :] = side_by_side(d_lr, d_li).astype(_BF16)
        eye = (_iota((SSM_GROUPS, SSM_GROUPS), 0) == _iota((SSM_GROUPS, SSM_GROUPS), 1)).astype(_F32)
        dt_row = _SSM_PACK["ssm_log_dt"][0]
        pack_ref[dt_row:dt_row + _PACK_TILE, 0:SSM_GROUPS] = _dot_exact(
            jnp.broadcast_to(d_dt, (SSM_GROUPS, 128)), eye, _TN)[0:_PACK_TILE].astype(_BF16)

    ins = [da_re, da_im, dbt_re, dbt_im, dct_re, dct_im, lam_re, lam_im, log_dt, b_re, b_im, g_d]
    out = (_SSM_PACK_ROWS, _LANES)
    return _call(body, (1,), [_whole(a.shape) for a in ins], _whole(out), _sds(out, _BF16), "ssm_param_bwd")(*ins)


def _head_spread(j):
    r = _iota((KV_WIDTH, 256), 0)
    c = _iota((KV_WIDTH, 256), 1)
    return (r == HEAD_DIM * j + (c & (HEAD_DIM - 1))).astype(_BF16)


STACK = Q_PER_KV * BLOCK


def _stack_heads(t):
    lane_head = _iota((1, 256), 1) >> 6
    return jnp.concatenate([jnp.where(lane_head == g, t, jnp.zeros_like(t)) for g in range(Q_PER_KV)], axis=0)


def _unstack_heads(t):
    lane_head = _iota((1, 256), 1) >> 6
    return sum(jnp.where(lane_head == g, t[BLOCK * g:BLOCK * (g + 1)], 0.0) for g in range(Q_PER_KV))


def _stacked_sinks(sink_ref, j):
    block = _iota((STACK, 1), 0) >> 7
    col = jnp.full((STACK, 1), sink_ref[Q_PER_KV * j], _F32)
    for g in range(1, Q_PER_KV):
        col = jnp.where(block == g, sink_ref[Q_PER_KV * j + g], col)
    return col


def _fold_heads(t, j):
    t = t[:, :KV_WIDTH] + t[:, KV_WIDTH:]
    t = t + pltpu.roll(t, HEAD_DIM, 1)
    return jnp.where((_iota((1, KV_WIDTH), 1) >> 6) == j, t, 0.0)


def _attn_scores(q_stacked, kt, blk, sink):
    s = _dot(q_stacked, kt, _NT) * (HEAD_DIM ** -0.5)
    qi = _iota((STACK, 2 * BLOCK), 0) & (BLOCK - 1)
    kj = _iota((STACK, 2 * BLOCK), 1)
    rel = qi + BLOCK - kj
    valid = (rel >= 0) & (rel < BLOCK) & (blk * BLOCK - BLOCK + kj >= 0)
    s = jnp.where(valid, s, MASK_VALUE)
    m = jnp.maximum(jnp.max(s, axis=-1, keepdims=True), sink)
    p = jnp.exp(s - m)
    e_sink = jnp.exp(sink - m)
    den = jnp.sum(p, axis=-1, keepdims=True) + e_sink
    return p / den, e_sink / den


def _sink_slot():
    return _iota((STACK, 2 * BLOCK), 1) == 0


def _prob_block():
    return pl.BlockSpec((None, N_KV_HEADS, STACK, 2 * BLOCK), lambda i: (i, 0, 0, 0))


def _attn_specs():
    prev = lambda i: (jnp.maximum(i - 1, 0), 0)
    cur = lambda i: (i, 0)
    kv = [pl.BlockSpec((BLOCK, KV_WIDTH), prev), pl.BlockSpec((BLOCK, KV_WIDTH), cur)]
    return [pl.BlockSpec((BLOCK, ATTN_WIDTH), cur)] + kv + kv


def _attn_fwd(q, k, v, sinks, g_attn):
    L = q.shape[0]

    def body(q_ref, kp_ref, kc_ref, vp_ref, vc_ref, sink_ref, g_ref, o_ref, n_ref, p_ref):
        blk = pl.program_id(0)
        kwin = jnp.concatenate([kp_ref[...], kc_ref[...]], axis=0)
        vwin = jnp.concatenate([vp_ref[...], vc_ref[...]], axis=0)
        halves = []
        for j in range(N_KV_HEADS):
            spread = _head_spread(j)
            kt = _dot(kwin, spread, _NN).astype(_BF16)
            vt = _dot(vwin, spread, _NN).astype(_BF16)
            qs = _stack_heads(q_ref[:, 256 * j:256 * (j + 1)])
            p, p_sink = _attn_scores(qs, kt, blk, _stacked_sinks(sink_ref, j))
            p_ref[j] = jnp.where(_sink_slot(), p_sink, p)
            halves.append(_unstack_heads(_dot(p, vt, _NN)))
        o = jnp.concatenate(halves, axis=1)
        o_ref[...] = o
        n, _ = _rms_fwd(o, g_ref[...])
        n_ref[...] = n.astype(_BF16)

    cur = lambda i: (i, 0)
    return _call(body, (L // BLOCK,),
                 _attn_specs() + [pl.BlockSpec(memory_space=pltpu.SMEM), _whole((1, ATTN_WIDTH))],
                 [pl.BlockSpec((BLOCK, ATTN_WIDTH), cur)] * 2 + [_prob_block()],
                 [_sds((L, ATTN_WIDTH), _F32), _sds((L, ATTN_WIDTH), _BF16),
                  _sds((L // BLOCK, N_KV_HEADS, STACK, 2 * BLOCK), _F32)],
                 "attn_fwd")(q, k, k, v, v, sinks, g_attn)


def _attn_bwd(q, k, v, o, dn, probs, g_attn):
    L = q.shape[0]

    def body(q_ref, kp_ref, kc_ref, vp_ref, vc_ref, o_ref, dn_ref, p_ref, g_ref,
             dq_ref, dk_ref, dv_ref, dsink_ref, dg_ref):
        blk = pl.program_id(0)
        first = blk == 0

        @pl.when(first)
        def _():
            dk_ref[...] = jnp.zeros_like(dk_ref)
            dv_ref[...] = jnp.zeros_like(dv_ref)
            dsink_ref[...] = jnp.zeros_like(dsink_ref)

        o = o_ref[...]
        g = g_ref[...]
        _, r = _rms_fwd(o, g)
        do, dg = _rms_bwd(dn_ref[...], o, g, r)
        _accumulate(dg_ref, dg, first)
        kwin = jnp.concatenate([kp_ref[...], kc_ref[...]], axis=0)
        vwin = jnp.concatenate([vp_ref[...], vc_ref[...]], axis=0)
        lane = _iota((1, 128), 1)
        dsink = jnp.zeros((1, 128), _F32)
        dkwin = jnp.zeros((2 * BLOCK, KV_WIDTH), _F32)
        dvwin = jnp.zeros((2 * BLOCK, KV_WIDTH), _F32)
        dq_halves = []
        for j in range(N_KV_HEADS):
            spread = _head_spread(j)
            kt = _dot(kwin, spread, _NN).astype(_BF16)
            vt = _dot(vwin, spread, _NN).astype(_BF16)
            qs = _stack_heads(q_ref[:, 256 * j:256 * (j + 1)])
            dos = _stack_heads(do[:, 256 * j:256 * (j + 1)]).astype(_BF16)
            saved = p_ref[j]
            p_sink = saved[:, 0:1]
            p = jnp.where(_sink_slot(), 0.0, saved)
            dp = _dot(dos, vt, _NT)
            delta = jnp.sum(p * dp, axis=-1, keepdims=True)
            ds = (p * (dp - delta) * (HEAD_DIM ** -0.5)).astype(_BF16)
            sink_term = p_sink * delta
            for g in range(Q_PER_KV):
                head_sum = jnp.sum(sink_term[BLOCK * g:BLOCK * (g + 1)], axis=0, keepdims=True)
                dsink = dsink - jnp.where(lane == Q_PER_KV * j + g, head_sum, 0.0)
            dvwin = dvwin + _fold_heads(_dot(p, dos, _TN), j)
            dkwin = dkwin + _fold_heads(_dot(ds, qs, _TN), j)
            dq_halves.append(_unstack_heads(_dot(ds, kt, _NN)))
        dq_ref[...] = jnp.concatenate(dq_halves, axis=1)
        dsink_ref[...] += dsink
        prev = pl.ds(pl.multiple_of(jnp.maximum(blk - 1, 0) * BLOCK, BLOCK), BLOCK)
        cur = pl.ds(pl.multiple_of(blk * BLOCK, BLOCK), BLOCK)
        dk_ref[prev, :] += dkwin[:BLOCK]
        dk_ref[cur, :] += dkwin[BLOCK:]
        dv_ref[prev, :] += dvwin[:BLOCK]
        dv_ref[cur, :] += dvwin[BLOCK:]

    cur = lambda i: (i, 0)
    blk_q = pl.BlockSpec((BLOCK, ATTN_WIDTH), cur)
    return _call(body, (L // BLOCK,),
                 _attn_specs() + [blk_q, blk_q, _prob_block(), _whole((1, ATTN_WIDTH))],
                 [blk_q, _whole((L, KV_WIDTH)), _whole((L, KV_WIDTH)), _whole((1, 128)), _whole((1, ATTN_WIDTH))],
                 [_sds((L, ATTN_WIDTH), _F32), _sds((L, KV_WIDTH), _F32), _sds((L, KV_WIDTH), _F32),
                  _sds((1, 128), _F32), _sds((1, ATTN_WIDTH), _F32)],
                 "attn_bwd")(q, k, k, v, v, o, dn, probs, g_attn)


def _out_proj(n_ssm, n_attn, x, w_out, g_post_mix, g_pre_ffn):
    L = x.shape[0]
    tm = _chunk_tile(L)

    def body(ns_ref, na_ref, x_ref, w_ref, g1_ref, g2_ref, merged_ref, mo_ref, h1_ref, hn2_ref):
        merged = jnp.concatenate([ns_ref[...], na_ref[...]], axis=1)
        merged_ref[...] = merged
        mo = _dot(merged, w_ref[...], _NN)
        mo_ref[...] = mo
        n, _ = _rms_fwd(mo, g1_ref[...])
        h1 = x_ref[...] + n
        h1_ref[...] = h1
        hn2, _ = _rms_fwd(h1, g2_ref[...])
        hn2_ref[...] = hn2.astype(_BF16)

    row = _whole((1, D_MODEL))
    return _call(body, (L // tm,),
                 [_chunk_block(L, SSM_WIDTH), _rows(tm, ATTN_WIDTH), _rows(tm, D_MODEL), _whole((D_MODEL, D_MODEL)),
                  row, row],
                 [_rows(tm, D_MODEL)] * 4,
                 [_sds((L, D_MODEL), _BF16), _sds((L, D_MODEL), _F32), _sds((L, D_MODEL), _F32), _sds((L, D_MODEL), _BF16)],
                 "out_proj")(n_ssm, n_attn, x, w_out, g_post_mix, g_pre_ffn)


def _ffn(hn2, h1, target, w_gate_up, w_down, g_pre_ffn, g_post_ffn):
    L = h1.shape[0]
    tm = _tile(L)
    half = FFN_CHUNK

    def body(hn2_ref, h1_ref, tgt_ref, wgu_hbm, wd_hbm, g2_ref, g3_ref,
             act_ref, dgu_ref, dff_ref, dh1_ref, loss_ref, dg3_ref, dg2_ref,
             wgu, wd, gu, sem):
        first = pl.program_id(0) == 0

        @pl.when(first)
        def _():
            c1 = pltpu.make_async_copy(wgu_hbm, wgu, sem.at[0])
            c2 = pltpu.make_async_copy(wd_hbm, wd, sem.at[1])
            c1.start()
            c2.start()
            c1.wait()
            c2.wait()

        hn2 = hn2_ref[...]
        ff = jnp.zeros((tm, D_MODEL), _F32)
        for c in range(D_FF // half):
            gate = _dot(hn2, wgu[half * c:half * (c + 1), :], _NT)
            up = _dot(hn2, wgu[D_FF + half * c:D_FF + half * (c + 1), :], _NT)
            gu[:, half * c:half * (c + 1)] = gate
            gu[:, D_FF + half * c:D_FF + half * (c + 1)] = up
            act = gate * jax.nn.sigmoid(gate) * up
            act_ref[half * c:half * (c + 1), :] = act.T.astype(_BF16)
            ff = ff + _dot(act, wd[half * c:half * (c + 1), :], _NN)
        g3 = g3_ref[...]
        n, r = _rms_fwd(ff, g3)
        h1 = h1_ref[...]
        err = h1 + n - tgt_ref[...]
        loss = 0.5 * jnp.sum(jnp.mean(err * err, axis=-1, keepdims=True), axis=0, keepdims=True)
        _accumulate(loss_ref, jnp.broadcast_to(loss, (1, 128)), first)
        dh2 = err * (1.0 / D_MODEL)
        dff, dg3 = _rms_bwd(dh2, ff, g3, r)
        _accumulate(dg3_ref, dg3, first)
        dffb = dff.astype(_BF16)
        dff_ref[...] = dffb
        dhn2 = jnp.zeros((tm, D_MODEL), _F32)
        for c in range(D_FF // half):
            dact = _dot(dffb, wd[half * c:half * (c + 1), :], _NT)
            gate = gu[:, half * c:half * (c + 1)]
            up = gu[:, D_FF + half * c:D_FF + half * (c + 1)]
            sig = jax.nn.sigmoid(gate)
            silu = gate * sig
            dgate = dact * up * (sig + silu * (1.0 - sig))
            dup = dact * silu
            dgu_ref[half * c:half * (c + 1), :] = dgate.T.astype(_BF16)
            dgu_ref[D_FF + half * c:D_FF + half * (c + 1), :] = dup.T.astype(_BF16)
            dhn2 = dhn2 + _dot(dgate, wgu[half * c:half * (c + 1), :], _NN)
            dhn2 = dhn2 + _dot(dup, wgu[D_FF + half * c:D_FF + half * (c + 1), :], _NN)
        g2 = g2_ref[...]
        _, r2 = _rms_fwd(h1, g2)
        dh1, dg2 = _rms_bwd(dhn2, h1, g2, r2)
        _accumulate(dg2_ref, dg2, first)
        dh1_ref[...] = dh2 + dh1

    row = _whole((1, D_MODEL))
    anyspace = pl.BlockSpec(memory_space=pl.ANY)
    return _call(body, (L // tm,),
                 [_rows(tm, D_MODEL), _rows(tm, D_MODEL), _rows(tm, D_MODEL), anyspace, anyspace, row, row],
                 [pl.BlockSpec((D_FF, tm), lambda i: (0, i)), pl.BlockSpec((2 * D_FF, tm), lambda i: (0, i)),
                  _rows(tm, D_MODEL), _rows(tm, D_MODEL), _whole((1, 128)), row, row],
                 [_sds((D_FF, L), _BF16), _sds((2 * D_FF, L), _BF16), _sds((L, D_MODEL), _BF16),
                  _sds((L, D_MODEL), _F32), _sds((1, 128), _F32), _sds((1, D_MODEL), _F32), _sds((1, D_MODEL), _F32)],
                 "ffn",
                 scratch=[pltpu.VMEM((2 * D_FF, D_MODEL), _BF16), pltpu.VMEM((D_FF, D_MODEL), _BF16),
                          pltpu.VMEM((tm, 2 * D_FF), _F32), pltpu.SemaphoreType.DMA((2,))],
                 )(hn2, h1, target, w_gate_up, w_down, g_pre_ffn, g_post_ffn)


def _out_proj_bwd(dh1, mo, w_out, g_post_mix, tokens=()):
    L = dh1.shape[0]
    tm = _chunk_tile(L)

    def body(dh1_ref, mo_ref, w_ref, g_ref, dmo_ref, dns_ref, dna_ref, dg_ref):
        first = pl.program_id(0) == 0
        mo = mo_ref[...]
        g = g_ref[...]
        _, r = _rms_fwd(mo, g)
        dmo, dg = _rms_bwd(dh1_ref[...], mo, g, r)
        _accumulate(dg_ref, dg, first)
        dmob = dmo.astype(_BF16)
        dmo_ref[...] = dmob
        dmerged = _dot(dmob, w_ref[...], _NT)
        dns_ref[...] = dmerged[:, :SSM_WIDTH]
        dna_ref[...] = dmerged[:, SSM_WIDTH:]

    row = _whole((1, D_MODEL))
    return _call(body, (L // tm,),
                 [_rows(tm, D_MODEL), _rows(tm, D_MODEL), _whole((D_MODEL, D_MODEL)), row],
                 [_rows(tm, D_MODEL), _chunk_block(L, SSM_WIDTH), _rows(tm, ATTN_WIDTH), row],
                 [_sds((L, D_MODEL), _BF16), _sds(_chunk_shape(L, SSM_WIDTH), _F32), _sds((L, ATTN_WIDTH), _F32),
                  _sds((1, D_MODEL), _F32)],
                 "out_proj_bwd", tokens=tokens)(dh1, mo, w_out, g_post_mix)


def _in_proj_bwd(du, dq, dk, dv, cos_t, sin_t, x, dh1, g_pre_mix, w_in, tokens=()):
    L = x.shape[0]
    tm = _chunk_tile(L)

    def body(du_ref, dq_ref, dk_ref, dv_ref, cos_ref, sin_ref, x_ref, dh1_ref, g_ref, w_ref,
             dproj_ref, dx_ref, dg_ref):
        first = pl.program_id(0) == 0
        cos_v, sin_v = cos_ref[...], sin_ref[...]
        dproj = jnp.concatenate([du_ref[...], _rope_transpose(dq_ref[...], cos_v, sin_v),
                                 _rope_transpose(dk_ref[...], cos_v, sin_v), dv_ref[...]], axis=1).astype(_BF16)
        dproj_ref[...] = dproj
        dhn = _dot(dproj, w_ref[...], _NN)
        x = x_ref[...]
        g = g_ref[...]
        _, r = _rms_fwd(x, g)
        dx, dg = _rms_bwd(dhn, x, g, r)
        _accumulate(dg_ref, dg, first)
        dx_ref[...] = dh1_ref[...] + dx

    row = _whole((1, D_MODEL))
    return _call(body, (L // tm,),
                 [_chunk_block(L, SSM_WIDTH), _rows(tm, ATTN_WIDTH), _rows(tm, KV_WIDTH), _rows(tm, KV_WIDTH),
                  _rows(tm, KV_WIDTH), _rows(tm, KV_WIDTH), _rows(tm, D_MODEL), _rows(tm, D_MODEL), row,
                  _whole((IN_WIDTH, D_MODEL))],
                 [_rows(tm, IN_WIDTH), _rows(tm, D_MODEL), row],
                 [_sds((L, IN_WIDTH), _BF16), _sds((L, D_MODEL), _F32), _sds((1, D_MODEL), _F32)],
                 "in_proj_bwd", tokens=tokens)(du, dq, dk, dv, cos_t, sin_t, x, dh1, g_pre_mix, w_in)


def _matmul_nn(a, b, out_dtype, name):
    M, K = a.shape
    N = b.shape[1]
    tm = next(t for t in (704, 512, 256, 128) if M % t == 0)
    tn = N if N <= D_MODEL else next(t for t in (512, 256, 128) if N % t == 0)

    def body(a_ref, b_ref, o_ref):
        o_ref[...] = _dot(a_ref[...], b_ref[...], _NN).astype(out_dtype)

    params = pltpu.CompilerParams(dimension_semantics=("arbitrary", "arbitrary"), vmem_limit_bytes=VMEM_LIMIT)
    return pl.pallas_call(body, grid=(M // tm, N // tn),
                          in_specs=[pl.BlockSpec((tm, K), lambda i, j: (i, 0)),
                                    pl.BlockSpec((K, tn), lambda i, j: (0, j))],
                          out_specs=pl.BlockSpec((tm, tn), lambda i, j: (i, j)),
                          out_shape=_sds((M, N), out_dtype), compiler_params=params, name=name)(a, b)


def _matmul_tn(a, b, out_dtype, name, scale=1.0):
    K, M = a.shape
    N = b.shape[1]
    tm = next(t for t in (512, 256, 128) if M % t == 0)
    tn = N if N <= D_MODEL else next(t for t in (512, 256, 128) if N % t == 0)

    def body(a_ref, b_ref, o_ref):
        acc = _dot(a_ref[...], b_ref[...], _TN)
        o_ref[...] = (acc if scale == 1.0 else acc * scale).astype(out_dtype)

    params = pltpu.CompilerParams(dimension_semantics=("arbitrary", "arbitrary"), vmem_limit_bytes=VMEM_LIMIT)
    return pl.pallas_call(body, grid=(M // tm, N // tn),
                          in_specs=[pl.BlockSpec((K, tm), lambda i, j: (0, i)),
                                    pl.BlockSpec((K, tn), lambda i, j: (0, j))],
                          out_specs=pl.BlockSpec((tm, tn), lambda i, j: (i, j)),
                          out_shape=_sds((M, N), out_dtype), compiler_params=params, name=name)(a, b)


def _local_step(x, pos, target, p, fetch, publish, progress):
    L = x.shape[0]
    T = L // SCAN_CHUNKS
    cos_t, sin_t = _rope_tables(pos.reshape(L, 1))
    w_in, = fetch(("w_in",), None)
    hn, u, q, k, v = _in_proj(x, p["g_pre_mix"], w_in, cos_t, sin_t)

    ssm = {n: _to_2d(n, p[n]) for n in ("ssm_lambda_re", "ssm_lambda_im", "ssm_log_dt", "ssm_b_re", "ssm_b_im",
                                        "ssm_c_re", "ssm_c_im")}
    d_row = p["ssm_d"].reshape(1, SSM_WIDTH)
    a_re, a_im, bt_re, bt_im, ct_re, ct_im = _ssm_prep(
        ssm["ssm_lambda_re"], ssm["ssm_lambda_im"], ssm["ssm_log_dt"], ssm["ssm_b_re"], ssm["ssm_b_im"],
        ssm["ssm_c_re"], ssm["ssm_c_im"])

    u_c = u.reshape(L, SSM_WIDTH)
    cx, x_re, x_im = _ssm_core_fwd(u_c, bt_re, bt_im, ct_re, ct_im, a_re, a_im)
    w_glu, = fetch(("w_glu",), cx)
    y, z, n_ssm = _ssm_out(cx, u_c, d_row, w_glu, p["b_glu"], p["g_ssm_out"])

    sinks = p["attn_sinks"].reshape(N_Q_HEADS)
    o, n_attn, probs = _attn_fwd(q, k, v, sinks, p["g_attn_out"])
    w_out, = fetch(("w_out",), n_attn)
    merged, mo, h1, hn2 = _out_proj(n_ssm, n_attn, x, w_out, p["g_post_mix"], p["g_pre_ffn"])
    w_gate_up, w_down = fetch(("w_gate_up", "w_down"), hn2)
    act_t, dgu_t, dff, dh1, loss, dg_post_ffn, dg_pre_ffn = _ffn(
        hn2, h1, target, w_gate_up, w_down, p["g_pre_ffn"], p["g_post_ffn"])
    grads = {"g_post_ffn": dg_post_ffn, "g_pre_ffn": dg_pre_ffn}
    tokens = publish({"w_down": _matmul_nn(act_t, dff, _BF16, "grad_w_down"),
                      "w_gate_up": _matmul_nn(dgu_t, hn2, _BF16, "grad_w_gate_up")})

    dmo, dn_ssm, dn_attn, grads["g_post_mix"] = _out_proj_bwd(dh1, mo, w_out, p["g_post_mix"], tokens)
    grad_w_out = _matmul_tn(merged, dmo, _BF16, "grad_w_out")

    dq, dk, dv, dsink, grads["g_attn_out"] = _attn_bwd(q, k, v, o, dn_attn, probs, p["g_attn_out"])
    grads["attn_sinks"] = dsink

    gy, dz, dy, dud, grads["g_ssm_out"], grads["b_glu"], dd = _ssm_out_bwd(
        dn_ssm, y, z, u_c, d_row, w_glu, p["g_ssm_out"], progress(dmo))
    tokens = publish({"w_out": grad_w_out, "w_glu": _matmul_tn(dz, gy, _BF16, "grad_w_glu")})
    du_c, dct_re, dct_im, dbt_re, dbt_im, da_re, da_im = _ssm_core_bwd(
        u_c, dy, dud, x_re, x_im, bt_re, bt_im, ct_re, ct_im, a_re, a_im, tokens)
    ssm_pack = _ssm_param_bwd(
        da_re, da_im, dbt_re, dbt_im, dct_re, dct_im,
        ssm["ssm_lambda_re"], ssm["ssm_lambda_im"], ssm["ssm_log_dt"], ssm["ssm_b_re"], ssm["ssm_b_im"],
        dd.reshape(SSM_GROUPS, SSM_GROUP))
    grads.update(ssm_pack=ssm_pack, loss=loss)
    publish(grads)

    du = du_c.reshape(_chunk_shape(L, SSM_WIDTH))
    dproj, grad_x, g_pre_mix = _in_proj_bwd(du, dq, dk, dv, cos_t, sin_t, x, dh1, p["g_pre_mix"], w_in, [ssm_pack])
    publish({"g_pre_mix": g_pre_mix, "w_in": _matmul_tn(dproj, hn, _BF16, "grad_w_in")})
    return grad_x


_MESH = pl.DeviceIdType.MESH
_PEERS = N_DEV - 1


def _mesh_pos():
    return lax.axis_index("x"), lax.axis_index("y"), lax.axis_index("c")


def _dev_index(px, py, pc):
    return 4 * px + 2 * py + pc


def _peer(x, y, c, r):
    return (x ^ ((r >> 2) & 1), y ^ ((r >> 1) & 1), c ^ (r & 1))


def _sequencer_exchange(sources, blocked, name, collective_id):
    n = len(sources)
    flags = blocked

    def body(*refs):
        srcs, zones = refs[:n], refs[n:2 * n]
        send_sems, recv_sems, local_sems = refs[2 * n:]
        x, y, c = _mesh_pos()
        me = _dev_index(x, y, c)
        barrier = pltpu.get_barrier_semaphore()
        for r in range(1, N_DEV):
            pl.semaphore_signal(barrier, inc=1, device_id=_peer(x, y, c, r), device_id_type=_MESH)
        pl.semaphore_wait(barrier, _PEERS)
        local, sends, recvs = [], [], []
        for w in range(n):
            cp = pltpu.make_async_copy(srcs[w].at[me] if flags[w] else srcs[w], zones[w].at[me], local_sems.at[w])
            cp.start()
            local.append(cp)
            for r in range(1, N_DEV):
                peer = _peer(x, y, c, r)
                idx = _dev_index(*peer)
                k = _PEERS * w + r - 1
                src = srcs[w].at[idx] if flags[w] else srcs[w]
                send = pltpu.make_async_remote_copy(
                    src_ref=src, dst_ref=zones[w].at[me], send_sem=send_sems.at[k], recv_sem=recv_sems.at[k],
                    device_id=peer, device_id_type=_MESH)
                send.start()
                sends.append(send)
                recvs.append(pltpu.make_async_remote_copy(
                    src_ref=src, dst_ref=zones[w].at[idx], send_sem=send_sems.at[k], recv_sem=recv_sems.at[k],
                    device_id=peer, device_id_type=_MESH))
        for cp in recvs:
            cp.wait_recv()
        for cp in sends:
            cp.wait_send()
        for cp in local:
            cp.wait()

    return pl.kernel(
        body, name=name,
        out_type=[_sds((N_DEV,) + (s.shape[1:] if f else s.shape), s.dtype) for s, f in zip(sources, flags)],
        mesh=plsc.ScalarSubcoreMesh(axis_name="sequencer", num_cores=1),
        scratch_types=[pltpu.SemaphoreType.DMA((_PEERS * n,)), pltpu.SemaphoreType.DMA((_PEERS * n,)),
                       pltpu.SemaphoreType.DMA((n,))],
        compiler_params=pltpu.CompilerParams(collective_id=collective_id),
    )(*sources)


def _sequencer_gather(shards, name, collective_id):
    n = len(shards)
    fan = 4

    def body(*refs):
        srcs, zones = refs[:n], refs[n:2 * n]
        send_sems, recv_sems, local_sems = refs[2 * n:]
        x, y, c = _mesh_pos()
        me, sibling = (x, y, c), (x, y, 1 - c)
        chips = [(1 - x, y), (x, 1 - y), (1 - x, 1 - y)]
        barrier = pltpu.get_barrier_semaphore()
        for peer in [sibling] + [(*chip, c) for chip in chips]:
            pl.semaphore_signal(barrier, inc=1, device_id=peer, device_id_type=_MESH)
        pl.semaphore_wait(barrier, fan)

        def copy(w, k, block, to, src=None):
            slot = zones[w].at[_dev_index(*block)]
            return pltpu.make_async_remote_copy(
                src_ref=slot if src is None else src, dst_ref=slot,
                send_sem=send_sems.at[_PEERS * w + k], recv_sem=recv_sems.at[_PEERS * w + k],
                device_id=to, device_id_type=_MESH)

        mine, first, passed = [], [], []
        for w in range(n):
            cp = pltpu.make_async_copy(srcs[w], zones[w].at[_dev_index(*me)], local_sems.at[w])
            cp.start()
            mine.append(cp)
            sends = [copy(w, 0, me, sibling, src=srcs[w])]
            sends += [copy(w, 1 + j, me, (*chip, c), src=srcs[w]) for j, chip in enumerate(chips)]
            for cp in sends:
                cp.start()
            first += sends
        for w in range(n):
            for j, chip in enumerate(chips):
                copy(w, 1 + j, (*chip, c), me).wait_recv()
                cp = copy(w, fan + j, (*chip, c), sibling)
                cp.start()
                passed.append(cp)
        for w in range(n):
            copy(w, 0, sibling, me).wait_recv()
            for j, chip in enumerate(chips):
                copy(w, fan + j, (*chip, 1 - c), me).wait_recv()
        for cp in first + passed:
            cp.wait_send()
        for cp in mine:
            cp.wait()

    return pl.kernel(
        body, name=name, out_type=[_sds((N_DEV,) + s.shape, s.dtype) for s in shards],
        mesh=plsc.ScalarSubcoreMesh(axis_name="sequencer", num_cores=1),
        scratch_types=[pltpu.SemaphoreType.DMA((_PEERS * n,)), pltpu.SemaphoreType.DMA((_PEERS * n,)),
                       pltpu.SemaphoreType.DMA((n,))],
        compiler_params=pltpu.CompilerParams(collective_id=collective_id),
    )(*shards)


N_CHIPS = N_DEV // 2


def _sequencer_pair_exchange(sources, name, collective_id):
    n = len(sources)

    def body(*refs):
        srcs, zones = refs[:n], refs[n:2 * n]
        send_sems, recv_sems = refs[2 * n:]
        x, y, c = _mesh_pos()
        sibling = (x, y, 1 - c)
        barrier = pltpu.get_barrier_semaphore()
        pl.semaphore_signal(barrier, inc=1, device_id=sibling, device_id_type=_MESH)
        pl.semaphore_wait(barrier, 1)
        copies = []
        for w in range(n):
            for j in range(N_CHIPS):
                k = N_CHIPS * w + j
                cp = pltpu.make_async_remote_copy(
                    src_ref=srcs[w].at[2 * j + 1 - c], dst_ref=zones[w].at[j],
                    send_sem=send_sems.at[k], recv_sem=recv_sems.at[k], device_id=sibling, device_id_type=_MESH)
                cp.start()
                copies.append(cp)
        for cp in copies:
            cp.wait_recv()
        for cp in copies:
            cp.wait_send()

    return pl.kernel(
        body, name=name, out_type=[_sds((N_CHIPS,) + s.shape[1:], s.dtype) for s in sources],
        mesh=plsc.ScalarSubcoreMesh(axis_name="sequencer", num_cores=1),
        scratch_types=[pltpu.SemaphoreType.DMA((N_CHIPS * n,)), pltpu.SemaphoreType.DMA((N_CHIPS * n,))],
        compiler_params=pltpu.CompilerParams(collective_id=collective_id),
    )(*sources)


def _pair_sum(source, received, core, name, tokens=()):
    _, rows, cols = source.shape
    tr = _row_tile(rows)
    n_tok = len(tokens)

    def body(core_ref, s_ref, r_ref, *rest):
        o_ref = rest[n_tok]
        o_ref[...] = (s_ref[...].astype(_F32) + r_ref[...].astype(_F32)).astype(o_ref.dtype)

    quarter = pl.BlockSpec((N_CHIPS, tr, cols), lambda i, core_ref: (0, i, 0))
    mine = pl.BlockSpec((N_CHIPS, None, tr, cols), lambda i, core_ref: (0, core_ref[0], i, 0))
    spec = pltpu.PrefetchScalarGridSpec(
        num_scalar_prefetch=1, grid=(rows // tr,),
        in_specs=[mine, quarter] + [pl.BlockSpec(memory_space=pl.ANY)] * n_tok, out_specs=quarter)
    params = pltpu.CompilerParams(dimension_semantics=("arbitrary",), vmem_limit_bytes=VMEM_LIMIT)
    return pl.pallas_call(body, grid_spec=spec, out_shape=_sds((N_CHIPS, rows, cols), source.dtype),
                          compiler_params=params, name=name)(
        core, source.reshape(N_CHIPS, 2, rows, cols), received, *tokens)


def _sequencer_chip_exchange(partials, name, collective_id):
    n = len(partials)
    others = N_CHIPS - 1

    def body(*refs):
        srcs, zones = refs[:n], refs[n:2 * n]
        send_sems, recv_sems, local_sems = refs[2 * n:]
        x, y, c = _mesh_pos()
        mine = 2 * x + y
        peers = [(x ^ (r >> 1), y ^ (r & 1), c) for r in range(1, N_CHIPS)]
        barrier = pltpu.get_barrier_semaphore()
        for peer in peers:
            pl.semaphore_signal(barrier, inc=1, device_id=peer, device_id_type=_MESH)
        pl.semaphore_wait(barrier, others)
        local, sends, recvs = [], [], []
        for w in range(n):
            cp = pltpu.make_async_copy(srcs[w].at[mine], zones[w].at[mine], local_sems.at[w])
            cp.start()
            local.append(cp)
            for r, peer in enumerate(peers):
                theirs = 2 * peer[0] + peer[1]
                k = others * w + r
                send = pltpu.make_async_remote_copy(
                    src_ref=srcs[w].at[theirs], dst_ref=zones[w].at[mine],
                    send_sem=send_sems.at[k], recv_sem=recv_sems.at[k], device_id=peer, device_id_type=_MESH)
                send.start()
                sends.append(send)
                recvs.append(pltpu.make_async_remote_copy(
                    src_ref=srcs[w].at[theirs], dst_ref=zones[w].at[theirs],
                    send_sem=send_sems.at[k], recv_sem=recv_sems.at[k], device_id=peer, device_id_type=_MESH))
        for cp in recvs:
            cp.wait_recv()
        for cp in sends:
            cp.wait_send()
        for cp in local:
            cp.wait()

    return pl.kernel(
        body, name=name, out_type=[_sds(s.shape, s.dtype) for s in partials],
        mesh=plsc.ScalarSubcoreMesh(axis_name="sequencer", num_cores=1),
        scratch_types=[pltpu.SemaphoreType.DMA((others * n,)), pltpu.SemaphoreType.DMA((others * n,)),
                       pltpu.SemaphoreType.DMA((n,))],
        compiler_params=pltpu.CompilerParams(collective_id=collective_id),
    )(*partials)


def _row_tile(rows):
    return next(t for t in range(min(rows, 256), 0, -16) if rows % t == 0)


def _adam_update(g, w, m, v):
    new_m = ADAM_B1 * m + (1.0 - ADAM_B1) * g
    new_v = ADAM_B2 * v + (1.0 - ADAM_B2) * (g * g)
    m_hat = new_m / (1.0 - ADAM_B1 ** ADAM_STEP)
    v_hat = new_v / (1.0 - ADAM_B2 ** ADAM_STEP)
    return -ADAM_LR * (m_hat / (jnp.sqrt(v_hat) + ADAM_EPS) + ADAM_WD * w), new_m, new_v


def _adamw_small(parts, items, sums, name, tokens=()):
    n_p, n_i = len(parts), len(items)

    def body(*refs):
        p_refs, state, outs = refs[:n_p], refs[n_p:n_p + 3 * n_i], refs[n_p + 3 * n_i:]

        def total(part, rows, cols):
            shift = cols.start % _LANES
            window = slice(cols.start - shift, cols.start - shift + _LANES) if shift else cols
            n_rows = rows.stop - rows.start
            narrow = p_refs[part].dtype.itemsize < 4 and n_rows % _PACK_TILE
            tile = slice(rows.start, rows.start + _PACK_TILE) if narrow else rows
            g = p_refs[part][0, tile, window].astype(_F32)
            for s in range(1, N_DEV):
                g = g + p_refs[part][s, tile, window].astype(_F32)
            g = g[:n_rows] if narrow else g
            return pltpu.roll(g, _LANES - shift, 1)[:, :cols.stop - cols.start] if shift else g

        for i, (part, rows, cols, _, _, _) in enumerate(items):
            g = total(part, rows, cols)
            w_ref, m_ref, v_ref = state[3 * i:3 * i + 3]
            delta, new_m, new_v = _adam_update(g, w_ref[...], m_ref[...], v_ref[...])
            outs[4 * i][...] = g
            outs[4 * i + 1][...] = delta
            outs[4 * i + 2][...] = new_m
            outs[4 * i + 3][...] = new_v
        for j, (part, rows, cols) in enumerate(sums):
            outs[4 * n_i + j][...] = total(part, rows, cols)

    ins = list(parts) + [a for item in items for a in item[3:]]
    out_shapes = [item[3].shape for item in items for _ in range(4)]
    out_shapes += [(rows.stop - rows.start, cols.stop - cols.start) for _, rows, cols in sums]
    out = _call(body, (1,), [_whole(a.shape) for a in ins], [_whole(s) for s in out_shapes],
                [_sds(s, _F32) for s in out_shapes], name, tokens=tokens)(*ins)
    return [out[4 * i:4 * i + 4] for i in range(n_i)], out[4 * n_i:]


def _adamw(parts, w, m, v, name, tokens=(), transposed_parts=False):
    rows, cols = w.shape
    tr = _row_tile(rows)
    n_parts = parts.shape[0]

    def body(p_ref, w_ref, m_ref, v_ref, g_ref, d_ref, nm_ref, nv_ref):
        g = p_ref[0].astype(_F32)
        for s in range(1, n_parts):
            g = g + p_ref[s].astype(_F32)
        if transposed_parts:
            g = g.T
        new_m = ADAM_B1 * m_ref[...] + (1.0 - ADAM_B1) * g
        new_v = ADAM_B2 * v_ref[...] + (1.0 - ADAM_B2) * (g * g)
        m_hat = new_m / (1.0 - ADAM_B1 ** ADAM_STEP)
        v_hat = new_v / (1.0 - ADAM_B2 ** ADAM_STEP)
        g_ref[...] = g
        d_ref[...] = -ADAM_LR * (m_hat / (jnp.sqrt(v_hat) + ADAM_EPS) + ADAM_WD * w_ref[...])
        nm_ref[...] = new_m
        nv_ref[...] = new_v

    blk = _rows(tr, cols)
    part = (pl.BlockSpec((n_parts, cols, tr), lambda i: (0, 0, i)) if transposed_parts
            else pl.BlockSpec((n_parts, tr, cols), lambda i: (0, i, 0)))
    return _call(body, (rows // tr,), [part, blk, blk, blk],
                 [blk] * 4, [_sds((rows, cols), _F32)] * 4, name, tokens=tokens)(parts, w, m, v)


_SMALL = ("g_pre_mix", "ssm_lambda_re", "ssm_lambda_im", "ssm_log_dt", "ssm_b_re", "ssm_b_im",
          "ssm_c_re", "ssm_c_im", "ssm_d", "b_glu", "attn_sinks", "g_ssm_out", "g_attn_out",
          "g_post_mix", "g_pre_ffn", "g_post_ffn")
_BIG = ("w_in", "w_glu", "w_out", "w_gate_up", "w_down")
_WEIGHTS = ("g_pre_mix", "w_in", "ssm_lambda_re", "ssm_lambda_im", "ssm_log_dt", "ssm_b_re", "ssm_b_im",
            "ssm_c_re", "ssm_c_im", "ssm_d", "w_glu", "b_glu", "attn_sinks", "g_ssm_out", "g_attn_out",
            "w_out", "g_post_mix", "g_pre_ffn", "w_gate_up", "w_down", "g_post_ffn")
_LANES = 128


_SHAPE_2D = {
    "g_pre_mix": (1, D_MODEL), "ssm_lambda_re": (SSM_GROUPS, SSM_STATE), "ssm_lambda_im": (SSM_GROUPS, SSM_STATE),
    "ssm_log_dt": (1, SSM_GROUPS), "ssm_b_re": (SSM_WIDTH, SSM_STATE), "ssm_b_im": (SSM_WIDTH, SSM_STATE),
    "ssm_c_re": (SSM_WIDTH, SSM_STATE), "ssm_c_im": (SSM_WIDTH, SSM_STATE), "ssm_d": (SSM_GROUPS, SSM_GROUP),
    "b_glu": (1, 2 * SSM_WIDTH), "attn_sinks": (1, N_Q_HEADS), "g_ssm_out": (1, SSM_WIDTH),
    "g_attn_out": (1, ATTN_WIDTH), "g_post_mix": (1, D_MODEL), "g_pre_ffn": (1, D_MODEL), "g_post_ffn": (1, D_MODEL)}
_ROW_WIDTH = {"g_pre_mix": D_MODEL, "b_glu": 2 * SSM_WIDTH, "attn_sinks": _LANES, "g_ssm_out": SSM_WIDTH,
              "g_attn_out": ATTN_WIDTH, "g_post_mix": D_MODEL, "g_pre_ffn": D_MODEL, "g_post_ffn": D_MODEL,
              "loss": _LANES}
_PER_GROUP_TRANSPOSED = ("ssm_b_re", "ssm_b_im")


def _to_2d(name, a):
    if name in _PER_GROUP_TRANSPOSED:
        a = a.reshape(SSM_GROUPS, SSM_STATE, SSM_GROUP).transpose(0, 2, 1)
    return a.reshape(_SHAPE_2D[name])


def _from_2d(name, a, shape):
    if name in _PER_GROUP_TRANSPOSED:
        a = a.reshape(SSM_GROUPS, SSM_GROUP, SSM_STATE).transpose(0, 2, 1)
    return a.reshape(shape)


def _row_slots(names):
    slots, row, col = {}, 0, 0
    for n in names:
        width = _ROW_WIDTH[n]
        if col + width > D_MODEL:
            row, col = row + 1, 0
        slots[n] = (row, col, width)
        col += width
    return slots


def _stack_rows(named, slots):
    n_rows = -(-(max(r for r, _, _ in slots.values()) + 1) // 8) * 8
    lines = []
    for r in range(n_rows):
        pieces = [named[n] for n, (row, _, _) in slots.items() if row == r]
        used = sum(p.shape[1] for p in pieces)
        if used < D_MODEL:
            pieces.append(jnp.zeros((1, D_MODEL - used), _F32))
        lines.append(jnp.concatenate(pieces, axis=1) if len(pieces) > 1 else pieces[0])
    return jnp.concatenate(lines, axis=0)


def kernel(x, positions, g_pre_mix, w_in, ssm_lambda_re, ssm_lambda_im, ssm_log_dt, ssm_b_re, ssm_b_im, ssm_c_re, ssm_c_im, ssm_d, w_glu, b_glu, attn_sinks, g_ssm_out, g_attn_out, w_out, g_post_mix, g_pre_ffn, w_gate_up, w_down, g_post_ffn, loss_target, m_g_pre_mix, m_w_in, m_ssm_lambda_re, m_ssm_lambda_im, m_ssm_log_dt, m_ssm_b_re, m_ssm_b_im, m_ssm_c_re, m_ssm_c_im, m_ssm_d, m_w_glu, m_b_glu, m_attn_sinks, m_g_ssm_out, m_g_attn_out, m_w_out, m_g_post_mix, m_g_pre_ffn, m_w_gate_up, m_w_down, m_g_post_ffn, v_g_pre_mix, v_w_in, v_ssm_lambda_re, v_ssm_lambda_im, v_ssm_log_dt, v_ssm_b_re, v_ssm_b_im, v_ssm_c_re, v_ssm_c_im, v_ssm_d, v_w_glu, v_b_glu, v_attn_sinks, v_g_ssm_out, v_g_attn_out, v_w_out, v_g_post_mix, v_g_pre_ffn, v_w_gate_up, v_w_down, v_g_post_ffn):
    w = dict(g_pre_mix=g_pre_mix, w_in=w_in, ssm_lambda_re=ssm_lambda_re, ssm_lambda_im=ssm_lambda_im,
             ssm_log_dt=ssm_log_dt, ssm_b_re=ssm_b_re, ssm_b_im=ssm_b_im, ssm_c_re=ssm_c_re, ssm_c_im=ssm_c_im,
             ssm_d=ssm_d, w_glu=w_glu, b_glu=b_glu, attn_sinks=attn_sinks, g_ssm_out=g_ssm_out,
             g_attn_out=g_attn_out, w_out=w_out, g_post_mix=g_post_mix, g_pre_ffn=g_pre_ffn,
             w_gate_up=w_gate_up, w_down=w_down, g_post_ffn=g_post_ffn)
    m = dict(g_pre_mix=m_g_pre_mix, w_in=m_w_in, ssm_lambda_re=m_ssm_lambda_re, ssm_lambda_im=m_ssm_lambda_im,
             ssm_log_dt=m_ssm_log_dt, ssm_b_re=m_ssm_b_re, ssm_b_im=m_ssm_b_im, ssm_c_re=m_ssm_c_re,
             ssm_c_im=m_ssm_c_im, ssm_d=m_ssm_d, w_glu=m_w_glu, b_glu=m_b_glu, attn_sinks=m_attn_sinks,
             g_ssm_out=m_g_ssm_out, g_attn_out=m_g_attn_out, w_out=m_w_out, g_post_mix=m_g_post_mix,
             g_pre_ffn=m_g_pre_ffn, w_gate_up=m_w_gate_up, w_down=m_w_down, g_post_ffn=m_g_post_ffn)
    v = dict(g_pre_mix=v_g_pre_mix, w_in=v_w_in, ssm_lambda_re=v_ssm_lambda_re, ssm_lambda_im=v_ssm_lambda_im,
             ssm_log_dt=v_ssm_log_dt, ssm_b_re=v_ssm_b_re, ssm_b_im=v_ssm_b_im, ssm_c_re=v_ssm_c_re,
             ssm_c_im=v_ssm_c_im, ssm_d=v_ssm_d, w_glu=v_w_glu, b_glu=v_b_glu, attn_sinks=v_attn_sinks,
             g_ssm_out=v_g_ssm_out, g_attn_out=v_g_attn_out, w_out=v_w_out, g_post_mix=v_g_post_mix,
             g_pre_ffn=v_g_pre_ffn, w_gate_up=v_w_gate_up, w_down=v_w_down, g_post_ffn=v_g_post_ffn)

    transposed = ("w_in", "w_glu", "w_gate_up")
    native_transposed = ("w_in", "w_gate_up")
    shard = {n: (w[n][0].T if n in transposed else w[n][0]).astype(_BF16) for n in _BIG}
    gathered = {}
    for cid, names in enumerate((("w_in",), ("w_glu", "w_out"), ("w_gate_up", "w_down")), start=1):
        lands = _sequencer_gather([shard[n] for n in names], "gather_" + names[0], cid)
        gathered.update({n: a.reshape(-1, a.shape[2]) for n, a in zip(names, lands)})

    def fetch(names, after):
        del after
        return [gathered[n] for n in names]

    sent = []
    ids = iter(range(4, 16))
    two_step = {}

    def publish(named):
        big = [n for n in named if n in _BIG]
        if set(big) == {"w_gate_up", "w_down"}:
            blocks = [named[n].reshape(N_DEV, -1, named[n].shape[1]) for n in big]
            two_step.update(names=big, blocks=blocks,
                            received=_sequencer_pair_exchange(blocks, "grads_pair", next(ids)))
            return [named[n] for n in big]
        rows = [n for n in named if n in _ROW_WIDTH]
        plain = [n for n in named if n not in big + rows]
        sources = [named[n].reshape(N_DEV, -1, named[n].shape[1]) for n in big]
        slots = _row_slots(rows)
        if rows:
            sources.append(_stack_rows(named, slots))
        sources += [named[n] for n in plain]
        flags = [True] * len(big) + [False] * (len(sources) - len(big))
        cid = next(ids)
        if big:
            lands = _sequencer_exchange(sources, flags, "grads_%d" % cid, cid)
        else:
            lands = _sequencer_gather(sources, "grads_%d" % cid, cid)
        sent.append((big, slots, plain, lands))
        return [named[n] for n in big]

    def progress(after):
        core = lax.axis_index("c").astype(jnp.int32).reshape(1)
        partials = [_pair_sum(b, r, core, "pair_sum_" + n, [after])
                    for n, b, r in zip(two_step["names"], two_step["blocks"], two_step["received"])]
        sent.append((two_step["names"], {}, [], _sequencer_chip_exchange(partials, "grads_chips", next(ids))))
        return partials

    p = {n: w[n] for n in _SMALL}
    grad_x = _local_step(x[0], positions[0], loss_target[0], p, fetch, publish, progress)

    state = {n: [_to_2d(n, a) for a in (w[n], m[n], v[n])] for n in _SMALL}
    result = {}
    total_loss = None
    chain = []
    for big, slots, plain, lands in sent:
        lands = list(lands)
        after = list(chain)
        for name in big:
            part = lands.pop(0)
            if name in native_transposed:
                updated = _adamw(part, w[name][0].T, m[name][0].T, v[name][0].T, "adamw_" + name, after)
                result[name] = [a.T[None] for a in updated]
                chain.append(updated[3])
                continue
            updated = _adamw(part, w[name][0], m[name][0], v[name][0], "adamw_" + name, after,
                             transposed_parts=name in transposed)
            result[name] = [a[None] for a in updated]
            chain.append(updated[3])
        parts, items, sums, names = [], [], [], []
        if slots:
            parts.append(lands.pop(0))
            for name, (row, col, _) in slots.items():
                if name == "loss":
                    sums.append((0, slice(row, row + 1), slice(col, col + _LANES)))
                else:
                    items.append((0, slice(row, row + 1), slice(col, col + _SHAPE_2D[name][1]), *state[name]))
                    names.append(name)
        for name in plain:
            packed = _SSM_PACK if name == "ssm_pack" else {name: (0, _SHAPE_2D[name][0], 0, _SHAPE_2D[name][1])}
            for member, (first, rows_n, lane, cols_n) in packed.items():
                items.append((len(parts), slice(first, first + rows_n), slice(lane, lane + cols_n), *state[member]))
                names.append(member)
            parts.append(lands.pop(0))
        if items:
            updated, summed = _adamw_small(parts, items, sums, "adamw_small_" + names[0], after)
            chain.append(updated[0][3])
            result.update(dict(zip(names, updated)))
            if summed:
                total_loss = summed[0][0, 0]

    out = [total_loss, grad_x[None]]
    for kind in range(4):
        out += [_from_2d(n, result[n][kind], w[n].shape) for n in _WEIGHTS]
    return tuple(out)
```

```python
import math

import numpy as np
import jax
import jax.numpy as jnp
from jax import lax
from jax.experimental import pallas as pl
from jax.experimental.pallas import tpu as pltpu
from jax.experimental.pallas import tpu_sc as plsc

D_MODEL = 1024
SSM_WIDTH = 512
SSM_GROUP = 16
SSM_GROUPS = 32
SSM_STATE = 64
N_STATE = SSM_GROUPS * SSM_STATE
ATTN_WIDTH = 512
HEAD_DIM = 64
N_Q_HEADS = 8
N_KV_HEADS = 2
Q_PER_KV = 4
KV_WIDTH = 128
IN_WIDTH = 1280
BLOCK = 128
ROPE_DIM = 16
ROPE_THETA = 500000.0
D_FF = 2816
NORM_EPS = 1e-6
MASK_VALUE = -1e30
ADAM_LR = 0.001
ADAM_B1 = 0.9
ADAM_B2 = 0.999
ADAM_EPS = 1e-08
ADAM_WD = 0.01
ADAM_STEP = 10

N_DEV = 8
SCAN_CHUNKS = 8
SCAN_UNROLL = 8
FFN_CHUNK = 2816
TOKEN_TILE = 256
VMEM_LIMIT = 56 * 1024 * 1024

_F32 = jnp.float32
_BF16 = jnp.bfloat16
_MXU = jnp.bfloat16

_NN = ((1,), (0,))
_NT = ((1,), (1,))
_TN = ((0,), (0,))


def _dot(a, b, dims):
    return lax.dot_general(a.astype(_MXU), b.astype(_MXU), (dims, ((), ())),
                           preferred_element_type=_F32)


def _dot_exact(a, b, dims):
    return lax.dot_general(a.astype(_F32), b.astype(_F32), (dims, ((), ())),
                           precision=lax.Precision.HIGHEST, preferred_element_type=_F32)


def _iota(shape, dim):
    return lax.broadcasted_iota(jnp.int32, shape, dim)


def _rms_fwd(x, g):
    r = lax.rsqrt(jnp.mean(x * x, axis=-1, keepdims=True) + NORM_EPS)
    return x * r * g, r


def _rms_bwd(dy, x, g, r):
    a = dy * g
    xn = x * r
    dx = r * (a - xn * jnp.mean(a * xn, axis=-1, keepdims=True))
    dg = jnp.sum(dy * xn, axis=0, keepdims=True)
    return dx, dg


def _call(body, grid, in_specs, out_specs, out_shape, name, scratch=(), tokens=()):
    params = pltpu.CompilerParams(dimension_semantics=("arbitrary",) * len(grid),
                                  vmem_limit_bytes=VMEM_LIMIT)
    n_in, n_tok = len(in_specs), len(tokens)

    def run(*refs):
        return body(*refs[:n_in], *refs[n_in + n_tok:])

    call = pl.pallas_call(run, grid=grid,
                          in_specs=list(in_specs) + [pl.BlockSpec(memory_space=pl.ANY)] * n_tok,
                          out_specs=out_specs, out_shape=out_shape, scratch_shapes=list(scratch),
                          compiler_params=params, name=name)
    return lambda *args: call(*args, *tokens)


def _rows(tm, n):
    return pl.BlockSpec((tm, n), lambda i: (i, 0))


def _whole(shape):
    nd = len(shape)
    return pl.BlockSpec(shape, lambda i: (0,) * nd)


def _sds(shape, dtype):
    return jax.ShapeDtypeStruct(shape, dtype)


def _tile(L):
    return min(TOKEN_TILE, L)


def _chunk_tile(L):
    return L // SCAN_CHUNKS


def _chunk_block(L, n):
    return pl.BlockSpec((_chunk_tile(L), n), lambda i: (0, i))


def _chunk_shape(L, n):
    return (_chunk_tile(L), SCAN_CHUNKS * n)


def _accumulate(ref, val, first):
    @pl.when(first)
    def _():
        ref[...] = val

    @pl.when(jnp.logical_not(first))
    def _():
        ref[...] += val


def _rope_rows():
    half = ROPE_DIM // 2
    inv = (np.float32(ROPE_THETA) ** (-np.arange(half, dtype=np.float32) * np.float32(2.0) / np.float32(ROPE_DIM))).astype(np.float32)
    col = np.arange(KV_WIDTH) % HEAD_DIM
    freq = np.where(col < ROPE_DIM, inv[col % half], 0.0).astype(np.float32)
    sign = np.where(col < half, -1.0, np.where(col < ROPE_DIM, 1.0, 0.0)).astype(np.float32)
    return freq[None, :], sign[None, :]


def _rope_tables(pos_col):
    L = pos_col.shape[0]
    tm = _tile(L)
    freq, sign = _rope_rows()

    def body(pos_ref, freq_ref, sign_ref, cos_ref, sin_ref):
        ang = pos_ref[...].astype(_F32) * freq_ref[...]
        cos_ref[...] = jnp.cos(ang)
        sin_ref[...] = jnp.sin(ang) * sign_ref[...]

    return _call(body, (L // tm,),
                 [_rows(tm, 1), _whole((1, KV_WIDTH)), _whole((1, KV_WIDTH))],
                 [_rows(tm, KV_WIDTH), _rows(tm, KV_WIDTH)],
                 [_sds((L, KV_WIDTH), _F32)] * 2, "rope_tables")(pos_col, jnp.asarray(freq), jnp.asarray(sign))


def _widen(t, width):
    return t if width == KV_WIDTH else jnp.concatenate([t] * (width // KV_WIDTH), axis=1)


def _rope_partner(t):
    w = t.shape[1]
    in_head = _iota((1, w), 1) & (HEAD_DIM - 1)
    second = jnp.where(in_head < ROPE_DIM, pltpu.roll(t, ROPE_DIM // 2, 1), 0.0)
    return jnp.where(in_head < ROPE_DIM // 2, pltpu.roll(t, w - ROPE_DIM // 2, 1), second)


def _rope_apply(t, cos_t, sin_t):
    w = t.shape[1]
    return t * _widen(cos_t, w) + _rope_partner(t) * _widen(sin_t, w)


def _rope_transpose(dt, cos_t, sin_t):
    w = dt.shape[1]
    return dt * _widen(cos_t, w) + _rope_partner(dt * _widen(sin_t, w))


def _in_proj(x, g_pre_mix, w_in, cos_t, sin_t):
    L = x.shape[0]
    tm = _chunk_tile(L)

    def body(x_ref, g_ref, w_ref, cos_ref, sin_ref, hn_ref, u_ref, q_ref, k_ref, v_ref):
        hn, _ = _rms_fwd(x_ref[...], g_ref[...])
        hn = hn.astype(_BF16)
        hn_ref[...] = hn
        proj = _dot(hn, w_ref[...], _NT)
        u_ref[...] = proj[:, :SSM_WIDTH]
        q = proj[:, SSM_WIDTH:SSM_WIDTH + ATTN_WIDTH]
        k = proj[:, SSM_WIDTH + ATTN_WIDTH:SSM_WIDTH + ATTN_WIDTH + KV_WIDTH]
        cos_v, sin_v = cos_ref[...], sin_ref[...]
        q_ref[...] = _rope_apply(q, cos_v, sin_v).astype(_BF16)
        k_ref[...] = _rope_apply(k, cos_v, sin_v).astype(_BF16)
        v_ref[...] = proj[:, SSM_WIDTH + ATTN_WIDTH + KV_WIDTH:].astype(_BF16)

    return _call(body, (L // tm,),
                 [_rows(tm, D_MODEL), _whole((1, D_MODEL)), _whole((IN_WIDTH, D_MODEL)),
                  _rows(tm, KV_WIDTH), _rows(tm, KV_WIDTH)],
                 [_rows(tm, D_MODEL), _chunk_block(L, SSM_WIDTH), _rows(tm, ATTN_WIDTH),
                  _rows(tm, KV_WIDTH), _rows(tm, KV_WIDTH)],
                 [_sds((L, D_MODEL), _BF16), _sds(_chunk_shape(L, SSM_WIDTH), _F32), _sds((L, ATTN_WIDTH), _BF16),
                  _sds((L, KV_WIDTH), _BF16), _sds((L, KV_WIDTH), _BF16)],
                 "in_proj")(x, g_pre_mix, w_in, cos_t, sin_t)


def _s5_discretize(lam_re, lam_im, log_dt):
    lr = jnp.minimum(lam_re, -1e-4)
    li = lam_im
    dt = jnp.exp(log_dt)
    mag = jnp.exp(lr * dt)
    ar = mag * jnp.cos(li * dt)
    ai = mag * jnp.sin(li * dt)
    den = lr * lr + li * li
    fr = ((ar - 1.0) * lr + ai * li) / den
    fi = (ai * lr - (ar - 1.0) * li) / den
    return ar, ai, fr, fi


SUPER = 4
SB_STATE = N_STATE // SUPER
SB_WIDTH = SSM_WIDTH // SUPER


def _sb_state(k):
    return slice(SB_STATE * k, SB_STATE * (k + 1))


def _sb_width(k):
    return slice(SB_WIDTH * k, SB_WIDTH * (k + 1))


def _dt_column(log_dt_row):
    eye = _iota((SSM_GROUPS, SSM_GROUPS), 0) == _iota((SSM_GROUPS, SSM_GROUPS), 1)
    return jnp.sum(jnp.where(eye, log_dt_row, 0.0), axis=1, keepdims=True)


def _group_masks():
    e64 = ((_iota((SSM_STATE, N_STATE), 1) & (SSM_STATE - 1)) == _iota((SSM_STATE, N_STATE), 0)).astype(_F32)
    own = _iota((SSM_GROUPS, N_STATE), 0) == (_iota((SSM_GROUPS, N_STATE), 1) >> 6)
    return e64, own


def _rows_of_group():
    return ((_iota((SSM_WIDTH, SSM_GROUPS), 0) >> 4) == _iota((SSM_WIDTH, SSM_GROUPS), 1)).astype(_F32)


def _ssm_prep(lam_re, lam_im, log_dt, b_re, b_im, c_re, c_im):
    def body(lr_ref, li_ref, ld_ref, bre, bim, cre, cim, ar_ref, ai_ref, btr, bti, ctr, cti):
        ar, ai, fr, fi = _s5_discretize(lr_ref[...], li_ref[...], _dt_column(ld_ref[...]))
        e64, own = _group_masks()
        mask_c = (_iota((SSM_WIDTH, N_STATE), 0) >> 4) == (_iota((SSM_WIDTH, N_STATE), 1) >> 6)

        def to_row(t):
            return jnp.sum(jnp.where(own, _dot_exact(t, e64, _NN), 0.0), axis=0, keepdims=True)

        def fold(m):
            full = jnp.where(mask_c, _dot(m, e64, _NN), 0.0)
            return sum(full[_sb_width(k), :] for k in range(SUPER)).astype(_BF16)

        ar_ref[...] = to_row(ar)
        ai_ref[...] = to_row(ai)
        spread = _rows_of_group()
        fr_t = _dot_exact(spread, fr, _NN)
        fi_t = _dot_exact(spread, fi, _NN)
        btr[...] = fold(fr_t * bre[...] - fi_t * bim[...])
        bti[...] = fold(fr_t * bim[...] + fi_t * bre[...])
        ctr[...] = fold(cre[...])
        cti[...] = fold(cim[...])

    row = (1, N_STATE)
    ins = [lam_re, lam_im, log_dt, b_re, b_im, c_re, c_im]
    return _call(body, (1,), [_whole(a.shape) for a in ins],
                 [_whole(row), _whole(row)] + [_whole((SB_WIDTH, N_STATE))] * 4,
                 [_sds(row, _F32), _sds(row, _F32)] + [_sds((SB_WIDTH, N_STATE), _BF16)] * 4,
                 "ssm_prep")(*ins)


def _complex_power(ar, ai, n):
    pr, pi = jnp.ones_like(ar), jnp.zeros_like(ai)
    while n:
        if n & 1:
            pr, pi = pr * ar - pi * ai, pr * ai + pi * ar
        ar, ai = ar * ar - ai * ai, 2.0 * ar * ai
        n >>= 1
    return pr, pi


def _chunk_carries(er, ei, pr, pi, reverse):
    rows = _iota(er.shape, 0)
    sr = jnp.zeros_like(pr)
    si = jnp.zeros_like(pi)
    out_r = jnp.zeros_like(er)
    out_i = jnp.zeros_like(ei)
    order = range(SCAN_CHUNKS - 1, 0, -1) if reverse else range(SCAN_CHUNKS - 1)
    for c in order:
        e_r = er[c:c + 1, :]
        e_i = ei[c:c + 1, :]
        sr, si = pr * sr - pi * si + e_r, pr * si + pi * sr + e_i
        nxt = c - 1 if reverse else c + 1
        out_r = jnp.where(rows == nxt, sr, out_r)
        out_i = jnp.where(rows == nxt, si, out_i)
    return out_r, out_i


_GELU_K = math.sqrt(2.0 / math.pi)
_GELU_C = 0.044715


def _gelu(y):
    return 0.5 * y * (1.0 + jnp.tanh(_GELU_K * (y + _GELU_C * y * y * y)))


def _gelu_grad(y):
    t = jnp.tanh(_GELU_K * (y + _GELU_C * y * y * y))
    return 0.5 * (1.0 + t) + 0.5 * y * (1.0 - t * t) * _GELU_K * (1.0 + 3.0 * _GELU_C * y * y)


def _step_rows(t):
    return pl.ds(pl.multiple_of(t * SCAN_CHUNKS, SCAN_CHUNKS), SCAN_CHUNKS)


def _scan_in_place(br, bi, ar, ai, T):
    W = br.shape[1]
    ar8 = jnp.broadcast_to(ar, (SCAN_CHUNKS, W))
    ai8 = jnp.broadcast_to(ai, (SCAN_CHUNKS, W))

    def local(t, c):
        cr, ci = c
        rows = _step_rows(t)
        return ar8 * cr - ai8 * ci + br[rows, :], ar8 * ci + ai8 * cr + bi[rows, :]

    zero = jnp.zeros((SCAN_CHUNKS, W), _F32)
    er, ei = lax.fori_loop(0, T, local, (zero, zero), unroll=SCAN_UNROLL)
    pr, pi = _complex_power(ar, ai, T)
    carries = _chunk_carries(er, ei, pr, pi, reverse=False)

    def final(t, c):
        nr, ni = local(t, c)
        rows = _step_rows(t)
        br[rows, :] = nr
        bi[rows, :] = ni
        return nr, ni

    lax.fori_loop(0, T, final, carries, unroll=SCAN_UNROLL)


def _scan_reverse_in_place(dr, di, xr, xi, ar, ai, T):
    W = dr.shape[1]
    ar8 = jnp.broadcast_to(ar, (SCAN_CHUNKS, W))
    ai8 = jnp.broadcast_to(ai, (SCAN_CHUNKS, W))

    def local(t, c):
        cr, ci = c
        rows = _step_rows(t)
        return ar8 * cr + ai8 * ci + dr[rows, :], ar8 * ci - ai8 * cr + di[rows, :]

    zero = jnp.zeros((SCAN_CHUNKS, W), _F32)
    er, ei = lax.fori_loop(0, T, lambda k, c: local(T - 1 - k, c), (zero, zero), unroll=SCAN_UNROLL)
    pr, pi = _complex_power(ar, -ai, T)
    sr, si = _chunk_carries(er, ei, pr, pi, reverse=True)

    def grad_a(acc, nr, ni, xpr, xpi):
        return acc[0] + nr * xpr + ni * xpi, acc[1] + ni * xpr - nr * xpi

    def final(k, c):
        t = T - 1 - k
        nr, ni = local(t, c[:2])
        rows = _step_rows(t)
        dr[rows, :] = nr
        di[rows, :] = ni
        before = _step_rows(t - 1)
        gr, gi = grad_a(c[2:], nr, ni, xr[before, :], xi[before, :])
        return nr, ni, gr, gi

    cr, ci, gr, gi = lax.fori_loop(0, T - 1, final, (sr, si, zero, zero), unroll=SCAN_UNROLL)
    nr, ni = local(0, (cr, ci))
    dr[_step_rows(0), :] = nr
    di[_step_rows(0), :] = ni
    first = _iota((SCAN_CHUNKS, W), 0) == 0
    last = _step_rows(T - 1)
    xpr = jnp.where(first, 0.0, pltpu.roll(xr[last, :], 1, 0))
    xpi = jnp.where(first, 0.0, pltpu.roll(xi[last, :], 1, 0))
    gr, gi = grad_a((gr, gi), nr, ni, xpr, xpi)
    return jnp.sum(gr, axis=0, keepdims=True), jnp.sum(gi, axis=0, keepdims=True)


def _ssm_super_specs(L):
    width = pl.BlockSpec((L, SB_WIDTH), lambda k: (0, k))
    matrix = pl.BlockSpec((SB_WIDTH, SB_STATE), lambda k: (0, k))
    row = pl.BlockSpec((1, SB_STATE), lambda k: (0, k))
    return width, matrix, row


def _ssm_core_fwd(u, bt_re, bt_im, ct_re, ct_im, a_re, a_im):
    L = u.shape[0]
    T = L // SCAN_CHUNKS

    def body(u_ref, br_ref, bi_ref, cr_ref, ci_ref, ar_ref, ai_ref, y_ref, xr, xi):
        ub = u_ref[...].astype(_BF16)
        xr[...] = _dot(ub, br_ref[...], _NN)
        xi[...] = _dot(ub, bi_ref[...], _NN)
        _scan_in_place(xr, xi, ar_ref[...], ai_ref[...], T)
        y_ref[...] = _dot(xr[...], cr_ref[...], _NT) - _dot(xi[...], ci_ref[...], _NT)

    width, matrix, row = _ssm_super_specs(L)
    state = pl.BlockSpec((L, SB_STATE), lambda k: (0, k))
    return _call(body, (SUPER,), [width, matrix, matrix, matrix, matrix, row, row], [width, state, state],
                 [_sds((L, SSM_WIDTH), _F32)] + [_sds((L, N_STATE), _F32)] * 2,
                 "ssm_core_fwd")(u, bt_re, bt_im, ct_re, ct_im, a_re, a_im)


def _ssm_core_bwd(u, dy, dud, x_re, x_im, bt_re, bt_im, ct_re, ct_im, a_re, a_im, tokens=()):
    L = u.shape[0]
    T = L // SCAN_CHUNKS

    def body(u_ref, dy_ref, dud_ref, xr, xi, br_ref, bi_ref, cr_ref, ci_ref, ar_ref, ai_ref,
             du_ref, dcr_ref, dci_ref, dbr_ref, dbi_ref, dar_ref, dai_ref, lr, li):
        dyb = dy_ref[...]
        lr[...] = _dot(dyb, cr_ref[...], _NN)
        li[...] = -_dot(dyb, ci_ref[...], _NN)
        da_re, da_im = _scan_reverse_in_place(lr, li, xr, xi, ar_ref[...], ai_ref[...], T)
        dar_ref[...] = da_re
        dai_ref[...] = da_im
        du_ref[...] = _dot(lr[...], br_ref[...], _NT) + _dot(li[...], bi_ref[...], _NT) + dud_ref[...]
        ub = u_ref[...].astype(_BF16)
        dcr_ref[...] = _dot(dyb, xr[...], _TN)
        dci_ref[...] = _dot(dyb, xi[...], _TN)
        dbr_ref[...] = _dot(ub, lr[...], _TN)
        dbi_ref[...] = _dot(ub, li[...], _TN)

    width, matrix, row = _ssm_super_specs(L)
    state = pl.BlockSpec((L, SB_STATE), lambda k: (0, k))
    return _call(body, (SUPER,), [width, width, width, state, state, matrix, matrix, matrix, matrix, row, row],
                 [width] + [matrix] * 4 + [row] * 2,
                 [_sds((L, SSM_WIDTH), _F32)] + [_sds((SB_WIDTH, N_STATE), _F32)] * 4 + [_sds((1, N_STATE), _F32)] * 2,
                 "ssm_core_bwd", scratch=[pltpu.VMEM((L, SB_STATE), _F32)] * 2,
                 tokens=tokens)(u, dy, dud, x_re, x_im, bt_re, bt_im, ct_re, ct_im, a_re, a_im)


def _ssm_out(cx, u, d_row, w_glu, b_glu, g_ssm):
    L = u.shape[0]
    tm = _tile(L)

    def body(cx_ref, u_ref, d_ref, w_ref, b_ref, g_ref, y_ref, z_ref, n_ref, stage):
        y = cx_ref[...] + d_ref[...] * u_ref[...]
        y_ref[...] = y
        z = _dot(_gelu(y), w_ref[...], _NT) + b_ref[...]
        z_ref[...] = z
        out = z[:, :SSM_WIDTH] * jax.nn.sigmoid(z[:, SSM_WIDTH:])
        n, _ = _rms_fwd(out, g_ref[...])
        for k in range(SSM_WIDTH // _LANES):
            stage[k] = n[:, _LANES * k:_LANES * (k + 1)]
            for c in range(SCAN_CHUNKS):
                rows = stage[k, pl.ds(c, tm // SCAN_CHUNKS, stride=SCAN_CHUNKS), :]
                lane = SSM_WIDTH * c + _LANES * k
                n_ref[:, lane:lane + _LANES] = rows.astype(_BF16)

    return _call(body, (L // tm,),
                 [_rows(tm, SSM_WIDTH), _rows(tm, SSM_WIDTH), _whole((1, SSM_WIDTH)),
                  _whole((2 * SSM_WIDTH, SSM_WIDTH)), _whole((1, 2 * SSM_WIDTH)), _whole((1, SSM_WIDTH))],
                 [_rows(tm, SSM_WIDTH), _rows(tm, 2 * SSM_WIDTH), _rows(tm // SCAN_CHUNKS, SCAN_CHUNKS * SSM_WIDTH)],
                 [_sds((L, SSM_WIDTH), _F32), _sds((L, 2 * SSM_WIDTH), _F32), _sds(_chunk_shape(L, SSM_WIDTH), _BF16)],
                 "ssm_out", scratch=[pltpu.VMEM((SSM_WIDTH // _LANES, tm, _LANES), _F32)])(
        cx, u, d_row, w_glu, b_glu, g_ssm)


def _ssm_out_bwd(dn, y, z, u, d_row, w_glu, g_ssm, tokens=()):
    L = u.shape[0]
    tm = _tile(L)

    def body(dn_ref, y_ref, z_ref, u_ref, d_ref, w_ref, g_ref,
             gy_ref, dz_ref, dy_ref, dud_ref, dg_ref, db_ref, dd_ref, stage):
        first = pl.program_id(0) == 0
        for k in range(SSM_WIDTH // _LANES):
            for c in range(SCAN_CHUNKS):
                lane = SSM_WIDTH * c + _LANES * k
                stage[k, pl.ds(c, tm // SCAN_CHUNKS, stride=SCAN_CHUNKS), :] = dn_ref[:, lane:lane + _LANES]
        dn = jnp.concatenate([stage[k] for k in range(SSM_WIDTH // _LANES)], axis=1)
        z = z_ref[...]
        z1, z2 = z[:, :SSM_WIDTH], z[:, SSM_WIDTH:]
        sig = jax.nn.sigmoid(z2)
        out = z1 * sig
        g = g_ref[...]
        _, r = _rms_fwd(out, g)
        dout, dg = _rms_bwd(dn, out, g, r)
        _accumulate(dg_ref, dg, first)
        dz = jnp.concatenate([dout * sig, dout * z1 * sig * (1.0 - sig)], axis=1)
        _accumulate(db_ref, jnp.sum(dz, axis=0, keepdims=True), first)
        dzb = dz.astype(_BF16)
        dz_ref[...] = dzb
        y = y_ref[...]
        gy_ref[...] = _gelu(y).astype(_BF16)
        dy = _dot(dzb, w_ref[...], _NN) * _gelu_grad(y)
        u = u_ref[...]
        _accumulate(dd_ref, jnp.sum(dy * u, axis=0, keepdims=True), first)
        dud_ref[...] = d_ref[...] * dy
        dy_ref[...] = dy.astype(_BF16)

    row = _whole((1, SSM_WIDTH))
    return _call(body, (L // tm,),
                 [_rows(tm // SCAN_CHUNKS, SCAN_CHUNKS * SSM_WIDTH), _rows(tm, SSM_WIDTH), _rows(tm, 2 * SSM_WIDTH),
                  _rows(tm, SSM_WIDTH), row, _whole((2 * SSM_WIDTH, SSM_WIDTH)), row],
                 [_rows(tm, SSM_WIDTH), _rows(tm, 2 * SSM_WIDTH), _rows(tm, SSM_WIDTH), _rows(tm, SSM_WIDTH),
                  row, _whole((1, 2 * SSM_WIDTH)), row],
                 [_sds((L, SSM_WIDTH), _BF16), _sds((L, 2 * SSM_WIDTH), _BF16), _sds((L, SSM_WIDTH), _BF16),
                  _sds((L, SSM_WIDTH), _F32),
                  _sds((1, SSM_WIDTH), _F32), _sds((1, 2 * SSM_WIDTH), _F32), _sds((1, SSM_WIDTH), _F32)],
                 "ssm_out_bwd", scratch=[pltpu.VMEM((SSM_WIDTH // _LANES, tm, _LANES), _F32)], tokens=tokens)(
        dn, y, z, u, d_row, w_glu, g_ssm)


_SSM_PACK = {"ssm_b_re": (0, SSM_WIDTH, 0, SSM_STATE), "ssm_c_re": (0, SSM_WIDTH, 64, SSM_STATE),
             "ssm_b_im": (512, SSM_WIDTH, 0, SSM_STATE), "ssm_c_im": (512, SSM_WIDTH, 64, SSM_STATE),
             "ssm_lambda_re": (1024, SSM_GROUPS, 0, SSM_STATE), "ssm_lambda_im": (1024, SSM_GROUPS, 64, SSM_STATE),
             "ssm_d": (1056, SSM_GROUPS, 0, SSM_GROUP), "ssm_log_dt": (1088, 1, 0, SSM_GROUPS)}
_PACK_TILE = 16
_SSM_PACK_ROWS = 1088 + _PACK_TILE


def _ssm_param_bwd(da_re, da_im, dbt_re, dbt_im, dct_re, dct_im, lam_re, lam_im, log_dt, b_re, b_im, g_d):
    def body(dar, dai, dbr, dbi, dcr, dci, lr_ref, li_ref, ld_ref, bre_ref, bim_ref, gd_ref, pack_ref):
        lane_in = _iota((SSM_STATE, _LANES), 0)
        lane_out = _iota((SSM_STATE, _LANES), 1)
        low = (lane_out == lane_in).astype(_F32)
        high = (lane_out == lane_in + SSM_STATE).astype(_F32)

        def side_by_side(a, b):
            return _dot_exact(a, low, _NN) + _dot_exact(b, high, _NN)

        tail = _SSM_PACK["ssm_d"][0]
        pack_ref[tail:, :] = jnp.zeros((_SSM_PACK_ROWS - tail, _LANES), _BF16)
        pack_ref[tail:tail + SSM_GROUPS, 0:SSM_GROUP] = gd_ref[...].astype(_BF16)
        own_c = (_iota((SB_WIDTH, SB_STATE), 0) >> 4) == (_iota((SB_WIDTH, SB_STATE), 1) >> 6)

        def unfold(ref):
            blocks = []
            for k in range(SUPER):
                t = jnp.where(own_c, ref[:, _sb_state(k)], 0.0)
                t = sum(t[:, 128 * i:128 * (i + 1)] for i in range(SB_STATE // 128))
                blocks.append((t + pltpu.roll(t, SSM_STATE, 1))[:, :SSM_STATE])
            return jnp.concatenate(blocks, axis=0)

        dbb_re, dbb_im = unfold(dbr), unfold(dbi)
        b_re, b_im = bre_ref[...], bim_ref[...]
        dt_col = _dt_column(ld_ref[...])
        (_, _, fr, fi), vjp = jax.vjp(_s5_discretize, lr_ref[...], li_ref[...], dt_col)
        spread = _rows_of_group()
        fr_t = _dot_exact(spread, fr, _NN)
        fi_t = _dot_exact(spread, fi, _NN)
        pack_ref[0:SSM_WIDTH, :] = side_by_side(fr_t * dbb_re + fi_t * dbb_im, unfold(dcr)).astype(_BF16)
        pack_ref[SSM_WIDTH:2 * SSM_WIDTH, :] = side_by_side(fr_t * dbb_im - fi_t * dbb_re, -unfold(dci)).astype(_BF16)
        d_fr = _dot_exact(spread, dbb_re * b_re + dbb_im * b_im, _TN)
        d_fi = _dot_exact(spread, dbb_im * b_re - dbb_re * b_im, _TN)
        e64, own = _group_masks()

        def from_row(ref):
            return _dot_exact(jnp.where(own, ref[...], 0.0), e64, _NT)

        d_lr, d_li, d_dt = vjp((from_row(dar), from_row(dai), d_fr, d_fi))
        lam_rows = _SSM_PACK["ssm_lambda_re"][0]
        pack_ref[lam_rows:lam_rows + SSM_GROUPS, :] = side_by_side(d_lr, d_li).astype(_BF16)
        eye = (_iota((SSM_GROUPS, SSM_GROUPS), 0) == _iota((SSM_GROUPS, SSM_GROUPS), 1)).astype(_F32)
        dt_row = _SSM_PACK["ssm_log_dt"][0]
        pack_ref[dt_row:dt_row + _PACK_TILE, 0:SSM_GROUPS] = _dot_exact(
            jnp.broadcast_to(d_dt, (SSM_GROUPS, 128)), eye, _TN)[0:_PACK_TILE].astype(_BF16)

    ins = [da_re, da_im, dbt_re, dbt_im, dct_re, dct_im, lam_re, lam_im, log_dt, b_re, b_im, g_d]
    out = (_SSM_PACK_ROWS, _LANES)
    return _call(body, (1,), [_whole(a.shape) for a in ins], _whole(out), _sds(out, _BF16), "ssm_param_bwd")(*ins)


def _head_spread(j):
    r = _iota((KV_WIDTH, 256), 0)
    c = _iota((KV_WIDTH, 256), 1)
    return (r == HEAD_DIM * j + (c & (HEAD_DIM - 1))).astype(_BF16)


STACK = Q_PER_KV * BLOCK
_HEAD_HALVES = ((0, 1), (2, 3))


def _stack_heads(t, heads=tuple(range(Q_PER_KV))):
    lane_head = _iota((1, 256), 1) >> 6
    return jnp.concatenate([jnp.where(lane_head == g, t, jnp.zeros_like(t)) for g in heads], axis=0)


def _unstack_heads(t, heads=tuple(range(Q_PER_KV))):
    lane_head = _iota((1, 256), 1) >> 6
    return sum(jnp.where(lane_head == g, t[BLOCK * i:BLOCK * (i + 1)], 0.0) for i, g in enumerate(heads))


def _stacked_sinks(sink_ref, j):
    block = _iota((STACK, 1), 0) >> 7
    col = jnp.full((STACK, 1), sink_ref[Q_PER_KV * j], _F32)
    for g in range(1, Q_PER_KV):
        col = jnp.where(block == g, sink_ref[Q_PER_KV * j + g], col)
    return col


def _fold_heads(t, j):
    t = t[:, :KV_WIDTH] + t[:, KV_WIDTH:]
    t = t + pltpu.roll(t, HEAD_DIM, 1)
    return jnp.where((_iota((1, KV_WIDTH), 1) >> 6) == j, t, 0.0)


def _attn_scores(q_stacked, kt, blk, sink):
    s = _dot(q_stacked, kt, _NT) * (HEAD_DIM ** -0.5)
    qi = _iota((STACK, 2 * BLOCK), 0) & (BLOCK - 1)
    kj = _iota((STACK, 2 * BLOCK), 1)
    rel = qi + BLOCK - kj
    valid = (rel >= 0) & (rel < BLOCK) & (blk * BLOCK - BLOCK + kj >= 0)
    s = jnp.where(valid, s, MASK_VALUE)
    m = jnp.maximum(jnp.max(s, axis=-1, keepdims=True), sink)
    p = jnp.exp(s - m)
    e_sink = jnp.exp(sink - m)
    den = jnp.sum(p, axis=-1, keepdims=True) + e_sink
    return p / den, e_sink / den


def _sink_slot():
    return _iota((STACK, 2 * BLOCK), 1) == 0


def _prob_block():
    return pl.BlockSpec((None, N_KV_HEADS, STACK, 2 * BLOCK), lambda i: (i, 0, 0, 0))


def _attn_specs():
    prev = lambda i: (jnp.maximum(i - 1, 0), 0)
    cur = lambda i: (i, 0)
    kv = [pl.BlockSpec((BLOCK, KV_WIDTH), prev), pl.BlockSpec((BLOCK, KV_WIDTH), cur)]
    return [pl.BlockSpec((BLOCK, ATTN_WIDTH), cur)] + kv + kv


def _attn_fwd(q, k, v, sinks, g_attn):
    L = q.shape[0]

    def body(q_ref, kp_ref, kc_ref, vp_ref, vc_ref, sink_ref, g_ref, o_ref, n_ref, p_ref):
        blk = pl.program_id(0)
        kwin = jnp.concatenate([kp_ref[...], kc_ref[...]], axis=0)
        vwin = jnp.concatenate([vp_ref[...], vc_ref[...]], axis=0)
        halves = []
        for j in range(N_KV_HEADS):
            spread = _head_spread(j)
            kt = _dot(kwin, spread, _NN).astype(_BF16)
            vt = _dot(vwin, spread, _NN).astype(_BF16)
            qs = _stack_heads(q_ref[:, 256 * j:256 * (j + 1)])
            p, p_sink = _attn_scores(qs, kt, blk, _stacked_sinks(sink_ref, j))
            p_ref[j] = jnp.where(_sink_slot(), p_sink, p)
            halves.append(_unstack_heads(_dot(p, vt, _NN)))
        o = jnp.concatenate(halves, axis=1)
        o_ref[...] = o
        n, _ = _rms_fwd(o, g_ref[...])
        n_ref[...] = n.astype(_BF16)

    cur = lambda i: (i, 0)
    return _call(body, (L // BLOCK,),
                 _attn_specs() + [pl.BlockSpec(memory_space=pltpu.SMEM), _whole((1, ATTN_WIDTH))],
                 [pl.BlockSpec((BLOCK, ATTN_WIDTH), cur)] * 2 + [_prob_block()],
                 [_sds((L, ATTN_WIDTH), _F32), _sds((L, ATTN_WIDTH), _BF16),
                  _sds((L // BLOCK, N_KV_HEADS, STACK, 2 * BLOCK), _F32)],
                 "attn_fwd")(q, k, k, v, v, sinks, g_attn)


def _attn_bwd(q, k, v, o, dn, probs, g_attn):
    L = q.shape[0]

    def body(q_ref, kp_ref, kc_ref, vp_ref, vc_ref, o_ref, dn_ref, p_ref, g_ref,
             dq_ref, dk_ref, dv_ref, dsink_ref, dg_ref):
        blk = pl.program_id(0)
        first = blk == 0

        @pl.when(first)
        def _():
            dk_ref[...] = jnp.zeros_like(dk_ref)
            dv_ref[...] = jnp.zeros_like(dv_ref)
            dsink_ref[...] = jnp.zeros_like(dsink_ref)

        o = o_ref[...]
        g = g_ref[...]
        _, r = _rms_fwd(o, g)
        do, dg = _rms_bwd(dn_ref[...], o, g, r)
        _accumulate(dg_ref, dg, first)
        kwin = jnp.concatenate([kp_ref[...], kc_ref[...]], axis=0)
        vwin = jnp.concatenate([vp_ref[...], vc_ref[...]], axis=0)
        lane = _iota((1, 128), 1)
        dsink = jnp.zeros((1, 128), _F32)
        dkwin = jnp.zeros((2 * BLOCK, KV_WIDTH), _F32)
        dvwin = jnp.zeros((2 * BLOCK, KV_WIDTH), _F32)
        dq_halves = []
        for j in range(N_KV_HEADS):
            spread = _head_spread(j)
            kt = _dot(kwin, spread, _NN).astype(_BF16)
            vt = _dot(vwin, spread, _NN).astype(_BF16)
            q_j = q_ref[:, 256 * j:256 * (j + 1)]
            do_j = do[:, 256 * j:256 * (j + 1)]
            dq_j = dv_j = dk_j = None
            for heads in _HEAD_HALVES:
                rows = len(heads) * BLOCK
                qs = _stack_heads(q_j, heads)
                dos = _stack_heads(do_j, heads).astype(_BF16)
                saved = p_ref[j, BLOCK * heads[0]:BLOCK * heads[0] + rows, :]
                p_sink = saved[:, 0:1]
                p = jnp.where(_iota(saved.shape, 1) == 0, 0.0, saved)
                dp = _dot(dos, vt, _NT)
                delta = jnp.sum(p * dp, axis=-1, keepdims=True)
                ds = (p * (dp - delta) * (HEAD_DIM ** -0.5)).astype(_BF16)
                sink_term = p_sink * delta
                for i, g in enumerate(heads):
                    head_sum = jnp.sum(sink_term[BLOCK * i:BLOCK * (i + 1)], axis=0, keepdims=True)
                    dsink = dsink - jnp.where(lane == Q_PER_KV * j + g, head_sum, 0.0)
                dv_h = _dot(p, dos, _TN)
                dk_h = _dot(ds, qs, _TN)
                dq_h = _unstack_heads(_dot(ds, kt, _NN), heads)
                dv_j = dv_h if dv_j is None else dv_j + dv_h
                dk_j = dk_h if dk_j is None else dk_j + dk_h
                dq_j = dq_h if dq_j is None else dq_j + dq_h
            dvwin = dvwin + _fold_heads(dv_j, j)
            dkwin = dkwin + _fold_heads(dk_j, j)
            dq_halves.append(dq_j)
        dq_ref[...] = jnp.concatenate(dq_halves, axis=1)
        dsink_ref[...] += dsink
        prev = pl.ds(pl.multiple_of(jnp.maximum(blk - 1, 0) * BLOCK, BLOCK), BLOCK)
        cur = pl.ds(pl.multiple_of(blk * BLOCK, BLOCK), BLOCK)
        dk_ref[prev, :] += dkwin[:BLOCK]
        dk_ref[cur, :] += dkwin[BLOCK:]
        dv_ref[prev, :] += dvwin[:BLOCK]
        dv_ref[cur, :] += dvwin[BLOCK:]

    cur = lambda i: (i, 0)
    blk_q = pl.BlockSpec((BLOCK, ATTN_WIDTH), cur)
    return _call(body, (L // BLOCK,),
                 _attn_specs() + [blk_q, blk_q, _prob_block(), _whole((1, ATTN_WIDTH))],
                 [blk_q, _whole((L, KV_WIDTH)), _whole((L, KV_WIDTH)), _whole((1, 128)), _whole((1, ATTN_WIDTH))],
                 [_sds((L, ATTN_WIDTH), _F32), _sds((L, KV_WIDTH), _F32), _sds((L, KV_WIDTH), _F32),
                  _sds((1, 128), _F32), _sds((1, ATTN_WIDTH), _F32)],
                 "attn_bwd")(q, k, k, v, v, o, dn, probs, g_attn)


def _out_proj(n_ssm, n_attn, x, w_out, g_post_mix, g_pre_ffn):
    L = x.shape[0]
    tm = _chunk_tile(L)

    def body(ns_ref, na_ref, x_ref, w_ref, g1_ref, g2_ref, merged_ref, mo_ref, h1_ref, hn2_ref):
        merged = jnp.concatenate([ns_ref[...], na_ref[...]], axis=1)
        merged_ref[...] = merged
        mo = _dot(merged, w_ref[...], _NN)
        mo_ref[...] = mo
        n, _ = _rms_fwd(mo, g1_ref[...])
        h1 = x_ref[...] + n
        h1_ref[...] = h1
        hn2, _ = _rms_fwd(h1, g2_ref[...])
        hn2_ref[...] = hn2.astype(_BF16)

    row = _whole((1, D_MODEL))
    return _call(body, (L // tm,),
                 [_chunk_block(L, SSM_WIDTH), _rows(tm, ATTN_WIDTH), _rows(tm, D_MODEL), _whole((D_MODEL, D_MODEL)),
                  row, row],
                 [_rows(tm, D_MODEL)] * 4,
                 [_sds((L, D_MODEL), _BF16), _sds((L, D_MODEL), _F32), _sds((L, D_MODEL), _F32), _sds((L, D_MODEL), _BF16)],
                 "out_proj")(n_ssm, n_attn, x, w_out, g_post_mix, g_pre_ffn)


def _ffn(hn2, h1, target, w_gate_up, w_down, g_pre_ffn, g_post_ffn):
    L = h1.shape[0]
    tm = _tile(L)
    half = FFN_CHUNK

    def body(hn2_ref, h1_ref, tgt_ref, wgu_hbm, wd_hbm, g2_ref, g3_ref,
             act_ref, dgu_ref, dff_ref, dh1_ref, loss_ref, dg3_ref, dg2_ref,
             wgu, wd, gu, sem):
        first = pl.program_id(0) == 0

        @pl.when(first)
        def _():
            c1 = pltpu.make_async_copy(wgu_hbm, wgu, sem.at[0])
            c2 = pltpu.make_async_copy(wd_hbm, wd, sem.at[1])
            c1.start()
            c2.start()
            c1.wait()
            c2.wait()

        hn2 = hn2_ref[...]
        ff = jnp.zeros((tm, D_MODEL), _F32)
        for c in range(D_FF // half):
            gate = _dot(hn2, wgu[half * c:half * (c + 1), :], _NT)
            up = _dot(hn2, wgu[D_FF + half * c:D_FF + half * (c + 1), :], _NT)
            gu[:, half * c:half * (c + 1)] = gate
            gu[:, D_FF + half * c:D_FF + half * (c + 1)] = up
            act = gate * jax.nn.sigmoid(gate) * up
            act_ref[half * c:half * (c + 1), :] = act.T.astype(_BF16)
            ff = ff + _dot(act, wd[half * c:half * (c + 1), :], _NN)
        g3 = g3_ref[...]
        n, r = _rms_fwd(ff, g3)
        h1 = h1_ref[...]
        err = h1 + n - tgt_ref[...]
        loss = 0.5 * jnp.sum(jnp.mean(err * err, axis=-1, keepdims=True), axis=0, keepdims=True)
        _accumulate(loss_ref, jnp.broadcast_to(loss, (1, 128)), first)
        dh2 = err * (1.0 / D_MODEL)
        dff, dg3 = _rms_bwd(dh2, ff, g3, r)
        _accumulate(dg3_ref, dg3, first)
        dffb = dff.astype(_BF16)
        dff_ref[...] = dffb
        dhn2 = jnp.zeros((tm, D_MODEL), _F32)
        for c in range(D_FF // half):
            dact = _dot(dffb, wd[half * c:half * (c + 1), :], _NT)
            gate = gu[:, half * c:half * (c + 1)]
            up = gu[:, D_FF + half * c:D_FF + half * (c + 1)]
            sig = jax.nn.sigmoid(gate)
            silu = gate * sig
            dgate = dact * up * (sig + silu * (1.0 - sig))
            dup = dact * silu
            dgu_ref[half * c:half * (c + 1), :] = dgate.T.astype(_BF16)
            dgu_ref[D_FF + half * c:D_FF + half * (c + 1), :] = dup.T.astype(_BF16)
            dhn2 = dhn2 + _dot(dgate, wgu[half * c:half * (c + 1), :], _NN)
            dhn2 = dhn2 + _dot(dup, wgu[D_FF + half * c:D_FF + half * (c + 1), :], _NN)
        g2 = g2_ref[...]
        _, r2 = _rms_fwd(h1, g2)
        dh1, dg2 = _rms_bwd(dhn2, h1, g2, r2)
        _accumulate(dg2_ref, dg2, first)
        dh1_ref[...] = dh2 + dh1

    row = _whole((1, D_MODEL))
    anyspace = pl.BlockSpec(memory_space=pl.ANY)
    return _call(body, (L // tm,),
                 [_rows(tm, D_MODEL), _rows(tm, D_MODEL), _rows(tm, D_MODEL), anyspace, anyspace, row, row],
                 [pl.BlockSpec((D_FF, tm), lambda i: (0, i)), pl.BlockSpec((2 * D_FF, tm), lambda i: (0, i)),
                  _rows(tm, D_MODEL), _rows(tm, D_MODEL), _whole((1, 128)), row, row],
                 [_sds((D_FF, L), _BF16), _sds((2 * D_FF, L), _BF16), _sds((L, D_MODEL), _BF16),
                  _sds((L, D_MODEL), _F32), _sds((1, 128), _F32), _sds((1, D_MODEL), _F32), _sds((1, D_MODEL), _F32)],
                 "ffn",
                 scratch=[pltpu.VMEM((2 * D_FF, D_MODEL), _BF16), pltpu.VMEM((D_FF, D_MODEL), _BF16),
                          pltpu.VMEM((tm, 2 * D_FF), _F32), pltpu.SemaphoreType.DMA((2,))],
                 )(hn2, h1, target, w_gate_up, w_down, g_pre_ffn, g_post_ffn)


def _out_proj_bwd(dh1, mo, w_out, g_post_mix, tokens=()):
    L = dh1.shape[0]
    tm = _chunk_tile(L)

    def body(dh1_ref, mo_ref, w_ref, g_ref, dmo_ref, dns_ref, dna_ref, dg_ref):
        first = pl.program_id(0) == 0
        mo = mo_ref[...]
        g = g_ref[...]
        _, r = _rms_fwd(mo, g)
        dmo, dg = _rms_bwd(dh1_ref[...], mo, g, r)
        _accumulate(dg_ref, dg, first)
        dmob = dmo.astype(_BF16)
        dmo_ref[...] = dmob
        dmerged = _dot(dmob, w_ref[...], _NT)
        dns_ref[...] = dmerged[:, :SSM_WIDTH]
        dna_ref[...] = dmerged[:, SSM_WIDTH:]

    row = _whole((1, D_MODEL))
    return _call(body, (L // tm,),
                 [_rows(tm, D_MODEL), _rows(tm, D_MODEL), _whole((D_MODEL, D_MODEL)), row],
                 [_rows(tm, D_MODEL), _chunk_block(L, SSM_WIDTH), _rows(tm, ATTN_WIDTH), row],
                 [_sds((L, D_MODEL), _BF16), _sds(_chunk_shape(L, SSM_WIDTH), _F32), _sds((L, ATTN_WIDTH), _F32),
                  _sds((1, D_MODEL), _F32)],
                 "out_proj_bwd", tokens=tokens)(dh1, mo, w_out, g_post_mix)


def _in_proj_bwd(du, dq, dk, dv, cos_t, sin_t, x, dh1, g_pre_mix, w_in, tokens=()):
    L = x.shape[0]
    tm = _chunk_tile(L)

    def body(du_ref, dq_ref, dk_ref, dv_ref, cos_ref, sin_ref, x_ref, dh1_ref, g_ref, w_ref,
             dproj_ref, dx_ref, dg_ref):
        first = pl.program_id(0) == 0
        cos_v, sin_v = cos_ref[...], sin_ref[...]
        dproj = jnp.concatenate([du_ref[...], _rope_transpose(dq_ref[...], cos_v, sin_v),
                                 _rope_transpose(dk_ref[...], cos_v, sin_v), dv_ref[...]], axis=1).astype(_BF16)
        dproj_ref[...] = dproj
        dhn = _dot(dproj, w_ref[...], _NN)
        x = x_ref[...]
        g = g_ref[...]
        _, r = _rms_fwd(x, g)
        dx, dg = _rms_bwd(dhn, x, g, r)
        _accumulate(dg_ref, dg, first)
        dx_ref[...] = dh1_ref[...] + dx

    row = _whole((1, D_MODEL))
    return _call(body, (L // tm,),
                 [_chunk_block(L, SSM_WIDTH), _rows(tm, ATTN_WIDTH), _rows(tm, KV_WIDTH), _rows(tm, KV_WIDTH),
                  _rows(tm, KV_WIDTH), _rows(tm, KV_WIDTH), _rows(tm, D_MODEL), _rows(tm, D_MODEL), row,
                  _whole((IN_WIDTH, D_MODEL))],
                 [_rows(tm, IN_WIDTH), _rows(tm, D_MODEL), row],
                 [_sds((L, IN_WIDTH), _BF16), _sds((L, D_MODEL), _F32), _sds((1, D_MODEL), _F32)],
                 "in_proj_bwd", tokens=tokens)(du, dq, dk, dv, cos_t, sin_t, x, dh1, g_pre_mix, w_in)


def _matmul_nn(a, b, out_dtype, name):
    M, K = a.shape
    N = b.shape[1]
    tm = next(t for t in (704, 512, 256, 128) if M % t == 0)
    tn = N if N <= D_MODEL else next(t for t in (512, 256, 128) if N % t == 0)

    def body(a_ref, b_ref, o_ref):
        o_ref[...] = _dot(a_ref[...], b_ref[...], _NN).astype(out_dtype)

    params = pltpu.CompilerParams(dimension_semantics=("arbitrary", "arbitrary"), vmem_limit_bytes=VMEM_LIMIT)
    return pl.pallas_call(body, grid=(M // tm, N // tn),
                          in_specs=[pl.BlockSpec((tm, K), lambda i, j: (i, 0)),
                                    pl.BlockSpec((K, tn), lambda i, j: (0, j))],
                          out_specs=pl.BlockSpec((tm, tn), lambda i, j: (i, j)),
                          out_shape=_sds((M, N), out_dtype), compiler_params=params, name=name)(a, b)


def _matmul_tn(a, b, out_dtype, name, scale=1.0):
    K, M = a.shape
    N = b.shape[1]
    tm = next(t for t in (512, 256, 128) if M % t == 0)
    tn = N if N <= D_MODEL else next(t for t in (512, 256, 128) if N % t == 0)

    def body(a_ref, b_ref, o_ref):
        acc = _dot(a_ref[...], b_ref[...], _TN)
        o_ref[...] = (acc if scale == 1.0 else acc * scale).astype(out_dtype)

    params = pltpu.CompilerParams(dimension_semantics=("arbitrary", "arbitrary"), vmem_limit_bytes=VMEM_LIMIT)
    return pl.pallas_call(body, grid=(M // tm, N // tn),
                          in_specs=[pl.BlockSpec((K, tm), lambda i, j: (0, i)),
                                    pl.BlockSpec((K, tn), lambda i, j: (0, j))],
                          out_specs=pl.BlockSpec((tm, tn), lambda i, j: (i, j)),
                          out_shape=_sds((M, N), out_dtype), compiler_params=params, name=name)(a, b)


def _local_step(x, pos, target, p, fetch, publish, progress):
    L = x.shape[0]
    T = L // SCAN_CHUNKS
    cos_t, sin_t = _rope_tables(pos.reshape(L, 1))
    w_in, = fetch(("w_in",), None)
    hn, u, q, k, v = _in_proj(x, p["g_pre_mix"], w_in, cos_t, sin_t)

    ssm = {n: _to_2d(n, p[n]) for n in ("ssm_lambda_re", "ssm_lambda_im", "ssm_log_dt", "ssm_b_re", "ssm_b_im",
                                        "ssm_c_re", "ssm_c_im")}
    d_row = p["ssm_d"].reshape(1, SSM_WIDTH)
    a_re, a_im, bt_re, bt_im, ct_re, ct_im = _ssm_prep(
        ssm["ssm_lambda_re"], ssm["ssm_lambda_im"], ssm["ssm_log_dt"], ssm["ssm_b_re"], ssm["ssm_b_im"],
        ssm["ssm_c_re"], ssm["ssm_c_im"])

    u_c = u.reshape(L, SSM_WIDTH)
    cx, x_re, x_im = _ssm_core_fwd(u_c, bt_re, bt_im, ct_re, ct_im, a_re, a_im)
    w_glu, = fetch(("w_glu",), cx)
    y, z, n_ssm = _ssm_out(cx, u_c, d_row, w_glu, p["b_glu"], p["g_ssm_out"])

    sinks = p["attn_sinks"].reshape(N_Q_HEADS)
    o, n_attn, probs = _attn_fwd(q, k, v, sinks, p["g_attn_out"])
    w_out, = fetch(("w_out",), n_attn)
    merged, mo, h1, hn2 = _out_proj(n_ssm, n_attn, x, w_out, p["g_post_mix"], p["g_pre_ffn"])
    w_gate_up, w_down = fetch(("w_gate_up", "w_down"), hn2)
    act_t, dgu_t, dff, dh1, loss, dg_post_ffn, dg_pre_ffn = _ffn(
        hn2, h1, target, w_gate_up, w_down, p["g_pre_ffn"], p["g_post_ffn"])
    grads = {"g_post_ffn": dg_post_ffn, "g_pre_ffn": dg_pre_ffn}
    tokens = publish({"w_down": _matmul_nn(act_t, dff, _BF16, "grad_w_down"),
                      "w_gate_up": _matmul_nn(dgu_t, hn2, _BF16, "grad_w_gate_up")})

    dmo, dn_ssm, dn_attn, grads["g_post_mix"] = _out_proj_bwd(dh1, mo, w_out, p["g_post_mix"], tokens)
    grad_w_out = _matmul_tn(merged, dmo, _BF16, "grad_w_out")

    dq, dk, dv, dsink, grads["g_attn_out"] = _attn_bwd(q, k, v, o, dn_attn, probs, p["g_attn_out"])
    grads["attn_sinks"] = dsink

    gy, dz, dy, dud, grads["g_ssm_out"], grads["b_glu"], dd = _ssm_out_bwd(
        dn_ssm, y, z, u_c, d_row, w_glu, p["g_ssm_out"], progress(dmo))
    tokens = publish({"w_out": grad_w_out, "w_glu": _matmul_tn(dz, gy, _BF16, "grad_w_glu")})
    du_c, dct_re, dct_im, dbt_re, dbt_im, da_re, da_im = _ssm_core_bwd(
        u_c, dy, dud, x_re, x_im, bt_re, bt_im, ct_re, ct_im, a_re, a_im, tokens)
    ssm_pack = _ssm_param_bwd(
        da_re, da_im, dbt_re, dbt_im, dct_re, dct_im,
        ssm["ssm_lambda_re"], ssm["ssm_lambda_im"], ssm["ssm_log_dt"], ssm["ssm_b_re"], ssm["ssm_b_im"],
        dd.reshape(SSM_GROUPS, SSM_GROUP))
    grads.update(ssm_pack=ssm_pack, loss=loss)
    publish(grads)

    du = du_c.reshape(_chunk_shape(L, SSM_WIDTH))
    dproj, grad_x, g_pre_mix = _in_proj_bwd(du, dq, dk, dv, cos_t, sin_t, x, dh1, p["g_pre_mix"], w_in, [ssm_pack])
    publish({"g_pre_mix": g_pre_mix, "w_in": _matmul_tn(dproj, hn, _BF16, "grad_w_in")})
    return grad_x


_MESH = pl.DeviceIdType.MESH
_PEERS = N_DEV - 1


def _mesh_pos():
    return lax.axis_index("x"), lax.axis_index("y"), lax.axis_index("c")


def _dev_index(px, py, pc):
    return 4 * px + 2 * py + pc


def _peer(x, y, c, r):
    return (x ^ ((r >> 2) & 1), y ^ ((r >> 1) & 1), c ^ (r & 1))


def _sequencer_exchange(sources, blocked, name, collective_id):
    n = len(sources)
    flags = blocked

    def body(*refs):
        srcs, zones = refs[:n], refs[n:2 * n]
        send_sems, recv_sems, local_sems = refs[2 * n:]
        x, y, c = _mesh_pos()
        me = _dev_index(x, y, c)
        barrier = pltpu.get_barrier_semaphore()
        for r in range(1, N_DEV):
            pl.semaphore_signal(barrier, inc=1, device_id=_peer(x, y, c, r), device_id_type=_MESH)
        pl.semaphore_wait(barrier, _PEERS)
        local, sends, recvs = [], [], []
        for w in range(n):
            cp = pltpu.make_async_copy(srcs[w].at[me] if flags[w] else srcs[w], zones[w].at[me], local_sems.at[w])
            cp.start()
            local.append(cp)
            for r in range(1, N_DEV):
                peer = _peer(x, y, c, r)
                idx = _dev_index(*peer)
                k = _PEERS * w + r - 1
                src = srcs[w].at[idx] if flags[w] else srcs[w]
                send = pltpu.make_async_remote_copy(
                    src_ref=src, dst_ref=zones[w].at[me], send_sem=send_sems.at[k], recv_sem=recv_sems.at[k],
                    device_id=peer, device_id_type=_MESH)
                send.start()
                sends.append(send)
                recvs.append(pltpu.make_async_remote_copy(
                    src_ref=src, dst_ref=zones[w].at[idx], send_sem=send_sems.at[k], recv_sem=recv_sems.at[k],
                    device_id=peer, device_id_type=_MESH))
        for cp in recvs:
            cp.wait_recv()
        for cp in sends:
            cp.wait_send()
        for cp in local:
            cp.wait()

    return pl.kernel(
        body, name=name,
        out_type=[_sds((N_DEV,) + (s.shape[1:] if f else s.shape), s.dtype) for s, f in zip(sources, flags)],
        mesh=plsc.ScalarSubcoreMesh(axis_name="sequencer", num_cores=1),
        scratch_types=[pltpu.SemaphoreType.DMA((_PEERS * n,)), pltpu.SemaphoreType.DMA((_PEERS * n,)),
                       pltpu.SemaphoreType.DMA((n,))],
        compiler_params=pltpu.CompilerParams(collective_id=collective_id),
    )(*sources)


def _sequencer_gather(shards, name, collective_id):
    n = len(shards)
    fan = 4

    def body(*refs):
        srcs, zones = refs[:n], refs[n:2 * n]
        send_sems, recv_sems, local_sems = refs[2 * n:]
        x, y, c = _mesh_pos()
        me, sibling = (x, y, c), (x, y, 1 - c)
        chips = [(1 - x, y), (x, 1 - y), (1 - x, 1 - y)]
        barrier = pltpu.get_barrier_semaphore()
        for peer in [sibling] + [(*chip, c) for chip in chips]:
            pl.semaphore_signal(barrier, inc=1, device_id=peer, device_id_type=_MESH)
        pl.semaphore_wait(barrier, fan)

        def copy(w, k, block, to, src=None):
            slot = zones[w].at[_dev_index(*block)]
            return pltpu.make_async_remote_copy(
                src_ref=slot if src is None else src, dst_ref=slot,
                send_sem=send_sems.at[_PEERS * w + k], recv_sem=recv_sems.at[_PEERS * w + k],
                device_id=to, device_id_type=_MESH)

        mine, first, passed = [], [], []
        for w in range(n):
            cp = pltpu.make_async_copy(srcs[w], zones[w].at[_dev_index(*me)], local_sems.at[w])
            cp.start()
            mine.append(cp)
            sends = [copy(w, 0, me, sibling, src=srcs[w])]
            sends += [copy(w, 1 + j, me, (*chip, c), src=srcs[w]) for j, chip in enumerate(chips)]
            for cp in sends:
                cp.start()
            first += sends
        for w in range(n):
            for j, chip in enumerate(chips):
                copy(w, 1 + j, (*chip, c), me).wait_recv()
                cp = copy(w, fan + j, (*chip, c), sibling)
                cp.start()
                passed.append(cp)
        for w in range(n):
            copy(w, 0, sibling, me).wait_recv()
            for j, chip in enumerate(chips):
                copy(w, fan + j, (*chip, 1 - c), me).wait_recv()
        for cp in first + passed:
            cp.wait_send()
        for cp in mine:
            cp.wait()

    return pl.kernel(
        body, name=name, out_type=[_sds((N_DEV,) + s.shape, s.dtype) for s in shards],
        mesh=plsc.ScalarSubcoreMesh(axis_name="sequencer", num_cores=1),
        scratch_types=[pltpu.SemaphoreType.DMA((_PEERS * n,)), pltpu.SemaphoreType.DMA((_PEERS * n,)),
                       pltpu.SemaphoreType.DMA((n,))],
        compiler_params=pltpu.CompilerParams(collective_id=collective_id),
    )(*shards)


N_CHIPS = N_DEV // 2


def _sequencer_pair_exchange(sources, name, collective_id):
    n = len(sources)

    def body(*refs):
        srcs, zones = refs[:n], refs[n:2 * n]
        send_sems, recv_sems = refs[2 * n:]
        x, y, c = _mesh_pos()
        sibling = (x, y, 1 - c)
        barrier = pltpu.get_barrier_semaphore()
        pl.semaphore_signal(barrier, inc=1, device_id=sibling, device_id_type=_MESH)
        pl.semaphore_wait(barrier, 1)
        copies = []
        for w in range(n):
            for j in range(N_CHIPS):
                k = N_CHIPS * w + j
                cp = pltpu.make_async_remote_copy(
                    src_ref=srcs[w].at[2 * j + 1 - c], dst_ref=zones[w].at[j],
                    send_sem=send_sems.at[k], recv_sem=recv_sems.at[k], device_id=sibling, device_id_type=_MESH)
                cp.start()
                copies.append(cp)
        for cp in copies:
            cp.wait_recv()
        for cp in copies:
            cp.wait_send()

    return pl.kernel(
        body, name=name, out_type=[_sds((N_CHIPS,) + s.shape[1:], s.dtype) for s in sources],
        mesh=plsc.ScalarSubcoreMesh(axis_name="sequencer", num_cores=1),
        scratch_types=[pltpu.SemaphoreType.DMA((N_CHIPS * n,)), pltpu.SemaphoreType.DMA((N_CHIPS * n,))],
        compiler_params=pltpu.CompilerParams(collective_id=collective_id),
    )(*sources)


def _pair_sum(source, received, core, name, tokens=()):
    _, rows, cols = source.shape
    tr = _row_tile(rows)
    n_tok = len(tokens)

    def body(core_ref, s_ref, r_ref, *rest):
        o_ref = rest[n_tok]
        o_ref[...] = (s_ref[...].astype(_F32) + r_ref[...].astype(_F32)).astype(o_ref.dtype)

    quarter = pl.BlockSpec((N_CHIPS, tr, cols), lambda i, core_ref: (0, i, 0))
    mine = pl.BlockSpec((N_CHIPS, None, tr, cols), lambda i, core_ref: (0, core_ref[0], i, 0))
    spec = pltpu.PrefetchScalarGridSpec(
        num_scalar_prefetch=1, grid=(rows // tr,),
        in_specs=[mine, quarter] + [pl.BlockSpec(memory_space=pl.ANY)] * n_tok, out_specs=quarter)
    params = pltpu.CompilerParams(dimension_semantics=("arbitrary",), vmem_limit_bytes=VMEM_LIMIT)
    return pl.pallas_call(body, grid_spec=spec, out_shape=_sds((N_CHIPS, rows, cols), source.dtype),
                          compiler_params=params, name=name)(
        core, source.reshape(N_CHIPS, 2, rows, cols), received, *tokens)


def _sequencer_chip_exchange(partials, name, collective_id):
    n = len(partials)
    others = N_CHIPS - 1

    def body(*refs):
        srcs, zones = refs[:n], refs[n:2 * n]
        send_sems, recv_sems, local_sems = refs[2 * n:]
        x, y, c = _mesh_pos()
        mine = 2 * x + y
        peers = [(x ^ (r >> 1), y ^ (r & 1), c) for r in range(1, N_CHIPS)]
        barrier = pltpu.get_barrier_semaphore()
        for peer in peers:
            pl.semaphore_signal(barrier, inc=1, device_id=peer, device_id_type=_MESH)
        pl.semaphore_wait(barrier, others)
        local, sends, recvs = [], [], []
        for w in range(n):
            cp = pltpu.make_async_copy(srcs[w].at[mine], zones[w].at[mine], local_sems.at[w])
            cp.start()
            local.append(cp)
            for r, peer in enumerate(peers):
                theirs = 2 * peer[0] + peer[1]
                k = others * w + r
                send = pltpu.make_async_remote_copy(
                    src_ref=srcs[w].at[theirs], dst_ref=zones[w].at[mine],
                    send_sem=send_sems.at[k], recv_sem=recv_sems.at[k], device_id=peer, device_id_type=_MESH)
                send.start()
                sends.append(send)
                recvs.append(pltpu.make_async_remote_copy(
                    src_ref=srcs[w].at[theirs], dst_ref=zones[w].at[theirs],
                    send_sem=send_sems.at[k], recv_sem=recv_sems.at[k], device_id=peer, device_id_type=_MESH))
        for cp in recvs:
            cp.wait_recv()
        for cp in sends:
            cp.wait_send()
        for cp in local:
            cp.wait()

    return pl.kernel(
        body, name=name, out_type=[_sds(s.shape, s.dtype) for s in partials],
        mesh=plsc.ScalarSubcoreMesh(axis_name="sequencer", num_cores=1),
        scratch_types=[pltpu.SemaphoreType.DMA((others * n,)), pltpu.SemaphoreType.DMA((others * n,)),
                       pltpu.SemaphoreType.DMA((n,))],
        compiler_params=pltpu.CompilerParams(collective_id=collective_id),
    )(*partials)


def _row_tile(rows):
    return next(t for t in range(min(rows, 256), 0, -16) if rows % t == 0)


def _adam_update(g, w, m, v):
    new_m = ADAM_B1 * m + (1.0 - ADAM_B1) * g
    new_v = ADAM_B2 * v + (1.0 - ADAM_B2) * (g * g)
    m_hat = new_m / (1.0 - ADAM_B1 ** ADAM_STEP)
    v_hat = new_v / (1.0 - ADAM_B2 ** ADAM_STEP)
    return -ADAM_LR * (m_hat / (jnp.sqrt(v_hat) + ADAM_EPS) + ADAM_WD * w), new_m, new_v


def _adamw_small(parts, items, sums, name, tokens=()):
    n_p, n_i = len(parts), len(items)

    def body(*refs):
        p_refs, state, outs = refs[:n_p], refs[n_p:n_p + 3 * n_i], refs[n_p + 3 * n_i:]

        def total(part, rows, cols):
            shift = cols.start % _LANES
            window = slice(cols.start - shift, cols.start - shift + _LANES) if shift else cols
            n_rows = rows.stop - rows.start
            narrow = p_refs[part].dtype.itemsize < 4 and n_rows % _PACK_TILE
            tile = slice(rows.start, rows.start + _PACK_TILE) if narrow else rows
            g = p_refs[part][0, tile, window].astype(_F32)
            for s in range(1, N_DEV):
                g = g + p_refs[part][s, tile, window].astype(_F32)
            g = g[:n_rows] if narrow else g
            return pltpu.roll(g, _LANES - shift, 1)[:, :cols.stop - cols.start] if shift else g

        for i, (part, rows, cols, _, _, _) in enumerate(items):
            g = total(part, rows, cols)
            w_ref, m_ref, v_ref = state[3 * i:3 * i + 3]
            delta, new_m, new_v = _adam_update(g, w_ref[...], m_ref[...], v_ref[...])
            outs[4 * i][...] = g
            outs[4 * i + 1][...] = delta
            outs[4 * i + 2][...] = new_m
            outs[4 * i + 3][...] = new_v
        for j, (part, rows, cols) in enumerate(sums):
            outs[4 * n_i + j][...] = total(part, rows, cols)

    ins = list(parts) + [a for item in items for a in item[3:]]
    out_shapes = [item[3].shape for item in items for _ in range(4)]
    out_shapes += [(rows.stop - rows.start, cols.stop - cols.start) for _, rows, cols in sums]
    out = _call(body, (1,), [_whole(a.shape) for a in ins], [_whole(s) for s in out_shapes],
                [_sds(s, _F32) for s in out_shapes], name, tokens=tokens)(*ins)
    return [out[4 * i:4 * i + 4] for i in range(n_i)], out[4 * n_i:]


def _adamw(parts, w, m, v, name, tokens=(), transposed_parts=False):
    rows, cols = w.shape
    tr = _row_tile(rows)
    n_parts = parts.shape[0]

    def body(p_ref, w_ref, m_ref, v_ref, g_ref, d_ref, nm_ref, nv_ref):
        g = p_ref[0].astype(_F32)
        for s in range(1, n_parts):
            g = g + p_ref[s].astype(_F32)
        if transposed_parts:
            g = g.T
        new_m = ADAM_B1 * m_ref[...] + (1.0 - ADAM_B1) * g
        new_v = ADAM_B2 * v_ref[...] + (1.0 - ADAM_B2) * (g * g)
        m_hat = new_m / (1.0 - ADAM_B1 ** ADAM_STEP)
        v_hat = new_v / (1.0 - ADAM_B2 ** ADAM_STEP)
        g_ref[...] = g
        d_ref[...] = -ADAM_LR * (m_hat / (jnp.sqrt(v_hat) + ADAM_EPS) + ADAM_WD * w_ref[...])
        nm_ref[...] = new_m
        nv_ref[...] = new_v

    blk = _rows(tr, cols)
    part = (pl.BlockSpec((n_parts, cols, tr), lambda i: (0, 0, i)) if transposed_parts
            else pl.BlockSpec((n_parts, tr, cols), lambda i: (0, i, 0)))
    return _call(body, (rows // tr,), [part, blk, blk, blk],
                 [blk] * 4, [_sds((rows, cols), _F32)] * 4, name, tokens=tokens)(parts, w, m, v)


_SMALL = ("g_pre_mix", "ssm_lambda_re", "ssm_lambda_im", "ssm_log_dt", "ssm_b_re", "ssm_b_im",
          "ssm_c_re", "ssm_c_im", "ssm_d", "b_glu", "attn_sinks", "g_ssm_out", "g_attn_out",
          "g_post_mix", "g_pre_ffn", "g_post_ffn")
_BIG = ("w_in", "w_glu", "w_out", "w_gate_up", "w_down")
_WEIGHTS = ("g_pre_mix", "w_in", "ssm_lambda_re", "ssm_lambda_im", "ssm_log_dt", "ssm_b_re", "ssm_b_im",
            "ssm_c_re", "ssm_c_im", "ssm_d", "w_glu", "b_glu", "attn_sinks", "g_ssm_out", "g_attn_out",
            "w_out", "g_post_mix", "g_pre_ffn", "w_gate_up", "w_down", "g_post_ffn")
_LANES = 128


_SHAPE_2D = {
    "g_pre_mix": (1, D_MODEL), "ssm_lambda_re": (SSM_GROUPS, SSM_STATE), "ssm_lambda_im": (SSM_GROUPS, SSM_STATE),
    "ssm_log_dt": (1, SSM_GROUPS), "ssm_b_re": (SSM_WIDTH, SSM_STATE), "ssm_b_im": (SSM_WIDTH, SSM_STATE),
    "ssm_c_re": (SSM_WIDTH, SSM_STATE), "ssm_c_im": (SSM_WIDTH, SSM_STATE), "ssm_d": (SSM_GROUPS, SSM_GROUP),
    "b_glu": (1, 2 * SSM_WIDTH), "attn_sinks": (1, N_Q_HEADS), "g_ssm_out": (1, SSM_WIDTH),
    "g_attn_out": (1, ATTN_WIDTH), "g_post_mix": (1, D_MODEL), "g_pre_ffn": (1, D_MODEL), "g_post_ffn": (1, D_MODEL)}
_ROW_WIDTH = {"g_pre_mix": D_MODEL, "b_glu": 2 * SSM_WIDTH, "attn_sinks": _LANES, "g_ssm_out": SSM_WIDTH,
              "g_attn_out": ATTN_WIDTH, "g_post_mix": D_MODEL, "g_pre_ffn": D_MODEL, "g_post_ffn": D_MODEL,
              "loss": _LANES}
_PER_GROUP_TRANSPOSED = ("ssm_b_re", "ssm_b_im")


def _to_2d(name, a):
    if name in _PER_GROUP_TRANSPOSED:
        a = a.reshape(SSM_GROUPS, SSM_STATE, SSM_GROUP).transpose(0, 2, 1)
    return a.reshape(_SHAPE_2D[name])


def _from_2d(name, a, shape):
    if name in _PER_GROUP_TRANSPOSED:
        a = a.reshape(SSM_GROUPS, SSM_GROUP, SSM_STATE).transpose(0, 2, 1)
    return a.reshape(shape)


def _row_slots(names):
    slots, row, col = {}, 0, 0
    for n in names:
        width = _ROW_WIDTH[n]
        if col + width > D_MODEL:
            row, col = row + 1, 0
        slots[n] = (row, col, width)
        col += width
    return slots


def _stack_rows(named, slots):
    n_rows = -(-(max(r for r, _, _ in slots.values()) + 1) // 8) * 8
    lines = []
    for r in range(n_rows):
        pieces = [named[n] for n, (row, _, _) in slots.items() if row == r]
        used = sum(p.shape[1] for p in pieces)
        if used < D_MODEL:
            pieces.append(jnp.zeros((1, D_MODEL - used), _F32))
        lines.append(jnp.concatenate(pieces, axis=1) if len(pieces) > 1 else pieces[0])
    return jnp.concatenate(lines, axis=0)


def kernel(x, positions, g_pre_mix, w_in, ssm_lambda_re, ssm_lambda_im, ssm_log_dt, ssm_b_re, ssm_b_im, ssm_c_re, ssm_c_im, ssm_d, w_glu, b_glu, attn_sinks, g_ssm_out, g_attn_out, w_out, g_post_mix, g_pre_ffn, w_gate_up, w_down, g_post_ffn, loss_target, m_g_pre_mix, m_w_in, m_ssm_lambda_re, m_ssm_lambda_im, m_ssm_log_dt, m_ssm_b_re, m_ssm_b_im, m_ssm_c_re, m_ssm_c_im, m_ssm_d, m_w_glu, m_b_glu, m_attn_sinks, m_g_ssm_out, m_g_attn_out, m_w_out, m_g_post_mix, m_g_pre_ffn, m_w_gate_up, m_w_down, m_g_post_ffn, v_g_pre_mix, v_w_in, v_ssm_lambda_re, v_ssm_lambda_im, v_ssm_log_dt, v_ssm_b_re, v_ssm_b_im, v_ssm_c_re, v_ssm_c_im, v_ssm_d, v_w_glu, v_b_glu, v_attn_sinks, v_g_ssm_out, v_g_attn_out, v_w_out, v_g_post_mix, v_g_pre_ffn, v_w_gate_up, v_w_down, v_g_post_ffn):
    w = dict(g_pre_mix=g_pre_mix, w_in=w_in, ssm_lambda_re=ssm_lambda_re, ssm_lambda_im=ssm_lambda_im,
             ssm_log_dt=ssm_log_dt, ssm_b_re=ssm_b_re, ssm_b_im=ssm_b_im, ssm_c_re=ssm_c_re, ssm_c_im=ssm_c_im,
             ssm_d=ssm_d, w_glu=w_glu, b_glu=b_glu, attn_sinks=attn_sinks, g_ssm_out=g_ssm_out,
             g_attn_out=g_attn_out, w_out=w_out, g_post_mix=g_post_mix, g_pre_ffn=g_pre_ffn,
             w_gate_up=w_gate_up, w_down=w_down, g_post_ffn=g_post_ffn)
    m = dict(g_pre_mix=m_g_pre_mix, w_in=m_w_in, ssm_lambda_re=m_ssm_lambda_re, ssm_lambda_im=m_ssm_lambda_im,
             ssm_log_dt=m_ssm_log_dt, ssm_b_re=m_ssm_b_re, ssm_b_im=m_ssm_b_im, ssm_c_re=m_ssm_c_re,
             ssm_c_im=m_ssm_c_im, ssm_d=m_ssm_d, w_glu=m_w_glu, b_glu=m_b_glu, attn_sinks=m_attn_sinks,
             g_ssm_out=m_g_ssm_out, g_attn_out=m_g_attn_out, w_out=m_w_out, g_post_mix=m_g_post_mix,
             g_pre_ffn=m_g_pre_ffn, w_gate_up=m_w_gate_up, w_down=m_w_down, g_post_ffn=m_g_post_ffn)
    v = dict(g_pre_mix=v_g_pre_mix, w_in=v_w_in, ssm_lambda_re=v_ssm_lambda_re, ssm_lambda_im=v_ssm_lambda_im,
             ssm_log_dt=v_ssm_log_dt, ssm_b_re=v_ssm_b_re, ssm_b_im=v_ssm_b_im, ssm_c_re=v_ssm_c_re,
             ssm_c_im=v_ssm_c_im, ssm_d=v_ssm_d, w_glu=v_w_glu, b_glu=v_b_glu, attn_sinks=v_attn_sinks,
             g_ssm_out=v_g_ssm_out, g_attn_out=v_g_attn_out, w_out=v_w_out, g_post_mix=v_g_post_mix,
             g_pre_ffn=v_g_pre_ffn, w_gate_up=v_w_gate_up, w_down=v_w_down, g_post_ffn=v_g_post_ffn)

    transposed = ("w_in", "w_glu", "w_gate_up")
    native_transposed = ("w_in", "w_gate_up")
    shard = {n: (w[n][0].T if n in transposed else w[n][0]).astype(_BF16) for n in _BIG}
    gathered = {}
    for cid, names in enumerate((("w_in",), ("w_glu", "w_out"), ("w_gate_up", "w_down")), start=1):
        lands = _sequencer_gather([shard[n] for n in names], "gather_" + names[0], cid)
        gathered.update({n: a.reshape(-1, a.shape[2]) for n, a in zip(names, lands)})

    def fetch(names, after):
        del after
        return [gathered[n] for n in names]

    sent = []
    ids = iter(range(4, 16))
    two_step = {}

    def publish(named):
        big = [n for n in named if n in _BIG]
        if set(big) == {"w_gate_up", "w_down"}:
            blocks = [named[n].reshape(N_DEV, -1, named[n].shape[1]) for n in big]
            two_step.update(names=big, blocks=blocks,
                            received=_sequencer_pair_exchange(blocks, "grads_pair", next(ids)))
            return [named[n] for n in big]
        rows = [n for n in named if n in _ROW_WIDTH]
        plain = [n for n in named if n not in big + rows]
        sources = [named[n].reshape(N_DEV, -1, named[n].shape[1]) for n in big]
        slots = _row_slots(rows)
        if rows:
            sources.append(_stack_rows(named, slots))
        sources += [named[n] for n in plain]
        flags = [True] * len(big) + [False] * (len(sources) - len(big))
        cid = next(ids)
        if big:
            lands = _sequencer_exchange(sources, flags, "grads_%d" % cid, cid)
        else:
            lands = _sequencer_gather(sources, "grads_%d" % cid, cid)
        sent.append((big, slots, plain, lands))
        return [named[n] for n in big]

    def progress(after):
        core = lax.axis_index("c").astype(jnp.int32).reshape(1)
        partials = [_pair_sum(b, r, core, "pair_sum_" + n, [after])
                    for n, b, r in zip(two_step["names"], two_step["blocks"], two_step["received"])]
        sent.append((two_step["names"], {}, [], _sequencer_chip_exchange(partials, "grads_chips", next(ids))))
        return partials

    p = {n: w[n] for n in _SMALL}
    grad_x = _local_step(x[0], positions[0], loss_target[0], p, fetch, publish, progress)

    state = {n: [_to_2d(n, a) for a in (w[n], m[n], v[n])] for n in _SMALL}
    result = {}
    total_loss = None
    chain = []
    for big, slots, plain, lands in sent:
        lands = list(lands)
        after = list(chain)
        for name in big:
            part = lands.pop(0)
            if name in native_transposed:
                updated = _adamw(part, w[name][0].T, m[name][0].T, v[name][0].T, "adamw_" + name, after)
                result[name] = [a.T[None] for a in updated]
                chain.append(updated[3])
                continue
            updated = _adamw(part, w[name][0], m[name][0], v[name][0], "adamw_" + name, after,
                             transposed_parts=name in transposed)
            result[name] = [a[None] for a in updated]
            chain.append(updated[3])
        parts, items, sums, names = [], [], [], []
        if slots:
            parts.append(lands.pop(0))
            for name, (row, col, _) in slots.items():
                if name == "loss":
                    sums.append((0, slice(row, row + 1), slice(col, col + _LANES)))
                else:
                    items.append((0, slice(row, row + 1), slice(col, col + _SHAPE_2D[name][1]), *state[name]))
                    names.append(name)
        for name in plain:
            packed = _SSM_PACK if name == "ssm_pack" else {name: (0, _SHAPE_2D[name][0], 0, _SHAPE_2D[name][1])}
            for member, (first, rows_n, lane, cols_n) in packed.items():
                items.append((len(parts), slice(first, first + rows_n), slice(lane, lane + cols_n), *state[member]))
                names.append(member)
            parts.append(lands.pop(0))
        if items:
            updated, summed = _adamw_small(parts, items, sums, "adamw_small_" + names[0], after)
            chain.append(updated[0][3])
            result.update(dict(zip(names, updated)))
            if summed:
                total_loss = summed[0][0, 0]

    out = [total_loss, grad_x[None]]
    for kind in range(4):
        out += [_from_2d(n, result[n][kind], w[n].shape) for n in _WEIGHTS]
    return tuple(out)
```
